```python
import jax, jax.numpy as jnp
from jax import lax
import numpy as np

D_MODEL = 2048
BATCH = 8
SEQ = 2048
DEPTH = 1

HEAD_DIM = 64
ATTN_HEADS = D_MODEL // 128
KV_HEADS = ATTN_HEADS // 4
Q_DIM = ATTN_HEADS * HEAD_DIM
KV_DIM = KV_HEADS * HEAD_DIM
WINDOW = 128
ATTN_BLOCK = 128
ROPE_THETA = 10000.0
D_INNER = D_MODEL
SSM_HEAD_DIM = 64
SSM_HEADS = D_INNER // SSM_HEAD_DIM
SSM_GROUPS = 4
D_STATE = 128
CONV_WIDTH = 4
CHUNK = 128
CONV_DIM = D_INNER + 2 * SSM_GROUPS * D_STATE
FFN_HIDDEN = -(-(8 * D_MODEL) // (3 * 256)) * 256
PLE_DIM = 256
IN_DIM = Q_DIM + 2 * KV_DIM + D_INNER + CONV_DIM + SSM_HEADS + 2 * D_MODEL
NORM_EPS = 1e-6
SSM_NORM_EPS = 1e-5

kernel_name = "hybrid_swa_sink_ssd_gated_block"


def rmsnorm(x, g, eps=NORM_EPS):
    xf = x.astype(jnp.float32)
    y = xf * lax.rsqrt(jnp.mean(xf * xf, axis=-1, keepdims=True) + eps)
    return (y * g.astype(jnp.float32)).astype(x.dtype)


def apply_rope(t, positions):
    half = HEAD_DIM // 2
    inv_freq = ROPE_THETA ** (-jnp.arange(half, dtype=jnp.float32) * 2.0 / HEAD_DIM)
    ang = positions.astype(jnp.float32)[..., None] * inv_freq
    cos, sin = jnp.cos(ang)[:, :, None, :], jnp.sin(ang)[:, :, None, :]
    t1, t2 = t[..., :half], t[..., half:]
    return jnp.concatenate([t1 * cos - t2 * sin, t2 * cos + t1 * sin], axis=-1)


def sliding_window_sink_attention(q, k, v, sinks):
    b, s = q.shape[0], q.shape[1]
    nb = s // ATTN_BLOCK
    grp = ATTN_HEADS // KV_HEADS
    qb = q.reshape(b, nb, ATTN_BLOCK, KV_HEADS, grp, HEAD_DIM)

    def banded(t):
        tb = t.reshape(b, nb, ATTN_BLOCK, KV_HEADS, HEAD_DIM)
        prev = jnp.pad(tb, ((0, 0), (1, 0), (0, 0), (0, 0), (0, 0)))[:, :-1]
        return jnp.concatenate([prev, tb], axis=2)

    kw, vw = banded(k), banded(v)
    scores = jnp.einsum('bnqhgd,bnkhd->bnhgqk', qb, kw) * (HEAD_DIM ** -0.5)
    qi = jnp.arange(ATTN_BLOCK)[:, None] + ATTN_BLOCK
    kj = jnp.arange(2 * ATTN_BLOCK)[None, :]
    dist = qi - kj
    key_pos = (jnp.arange(nb) * ATTN_BLOCK)[:, None, None] - ATTN_BLOCK + kj[None]
    valid = (dist >= 0)[None] & (dist < WINDOW)[None] & (key_pos >= 0)
    scores = jnp.where(valid[None, :, None, None], scores, -jnp.inf)
    sink = sinks.astype(jnp.float32).reshape(KV_HEADS, grp)[None, None, :, :, None]
    m = jnp.maximum(scores.max(axis=-1), sink)
    e = jnp.exp(scores - m[..., None])
    probs = e / (e.sum(axis=-1) + jnp.exp(sink - m))[..., None]
    out = jnp.einsum('bnhgqk,bnkhd->bnqhgd', probs, vw)
    return out.reshape(b, s, Q_DIM)


def causal_depthwise_conv(x, w, bias):
    out = lax.conv_general_dilated(
        x, w[:, None, :], window_strides=(1,), padding=[(CONV_WIDTH - 1, 0)],
        dimension_numbers=('NWC', 'WIO', 'NWC'), feature_group_count=x.shape[-1])
    return out + bias


def ssd_chunked(xh, dt, a_neg, bm, cm):
    b, s = xh.shape[0], xh.shape[1]
    nc = s // CHUNK
    e_per = SSM_HEADS // SSM_GROUPS
    xd = (xh * dt[..., None]).reshape(b, nc, CHUNK, SSM_GROUPS, e_per, SSM_HEAD_DIM)
    a = jnp.transpose((dt * a_neg).reshape(b, nc, CHUNK, SSM_GROUPS, e_per), (0, 1, 3, 4, 2))
    a_cs = jnp.cumsum(a, axis=-1)
    bc = bm.reshape(b, nc, CHUNK, SSM_GROUPS, D_STATE)
    cc = cm.reshape(b, nc, CHUNK, SSM_GROUPS, D_STATE)
    tril = jnp.tril(jnp.ones((CHUNK, CHUNK), dtype=bool))
    diff = a_cs[..., :, None] - a_cs[..., None, :]
    decay = jnp.where(tril, jnp.exp(jnp.where(tril, diff, 0.0)), 0.0)
    cb = jnp.einsum('bclgn,bcsgn->bcgls', cc, bc)
    y_diag = jnp.einsum('bcgels,bcsgep->bclgep', cb[:, :, :, None] * decay, xd)
    decay_states = jnp.exp(a_cs[..., -1:] - a_cs)
    states = jnp.einsum('bclgn,bcgel,bclgep->bcgepn', bc, decay_states, xd)
    chunk_decay = jnp.exp(a_cs[..., -1])

    def step(carry, inp):
        st, dec = inp
        return carry * dec[..., None, None] + st, carry

    init = jnp.zeros((b, SSM_GROUPS, e_per, SSM_HEAD_DIM, D_STATE), jnp.float32)
    _, prev = lax.scan(step, init, (jnp.moveaxis(states, 1, 0), jnp.moveaxis(chunk_decay, 1, 0)))
    prev = jnp.moveaxis(prev, 0, 1)
    y_off = jnp.einsum('bclgn,bcgepn,bcgel->bclgep', cc, prev, jnp.exp(a_cs))
    return (y_diag + y_off).reshape(b, s, SSM_HEADS, SSM_HEAD_DIM)


def _dense(key, shape, fan_in):
    return jax.random.normal(key, shape, jnp.float32) * (fan_in ** -0.5)


def _fwd_setup_inputs(seed: int = 0) -> dict:
    key = jax.random.key(seed)
    ks = jax.random.split(key, 24)
    L = DEPTH
    ones_noise = lambda k, shape: 1.0 + 0.05 * jax.random.normal(k, shape, jnp.float32)
    dt0 = jnp.exp(jax.random.uniform(ks[5], (L, SSM_HEADS), jnp.float32, np.log(1e-3), np.log(1e-1)))
    return {
        "x": jax.random.normal(ks[0], (BATCH, SEQ, D_MODEL), jnp.float32),
        "p": jax.random.normal(ks[1], (DEPTH, BATCH, SEQ, PLE_DIM), jnp.float32),
        "positions": jnp.tile(jnp.arange(SEQ, dtype=jnp.int32)[None], (BATCH, 1)),
        "g_mix": ones_noise(ks[2], (L, D_MODEL)),
        "w_in": _dense(ks[3], (L, D_MODEL, IN_DIM), D_MODEL),
        "conv_w": _dense(ks[4], (L, CONV_WIDTH, CONV_DIM), CONV_WIDTH),
        "conv_b": 0.02 * jax.random.normal(ks[6], (L, CONV_DIM), jnp.float32),
        "dt_bias": dt0 + jnp.log(-jnp.expm1(-dt0)),
        "a_log": jnp.log(jax.random.uniform(ks[7], (L, SSM_HEADS), jnp.float32, 1.0, 16.0)),
        "d_skip": ones_noise(ks[8], (L, SSM_HEADS)),
        "g_ssd": ones_noise(ks[9], (L, D_INNER)),
        "sinks": 0.5 * jax.random.normal(ks[10], (L, ATTN_HEADS), jnp.float32),
        "w_attn_br": _dense(ks[11], (L, Q_DIM, D_MODEL), Q_DIM),
        "w_ssd_br": _dense(ks[12], (L, D_INNER, D_MODEL), D_INNER),
        "w_o": _dense(ks[13], (L, D_MODEL, D_MODEL), D_MODEL),
        "g_ffn": ones_noise(ks[14], (L, D_MODEL)),
        "w_gate": _dense(ks[15], (L, D_MODEL, FFN_HIDDEN), D_MODEL),
        "w_up": _dense(ks[16], (L, D_MODEL, FFN_HIDDEN), D_MODEL),
        "w_down": _dense(ks[17], (L, FFN_HIDDEN, D_MODEL), FFN_HIDDEN),
        "g_ple": ones_noise(ks[18], (L, D_MODEL)),
        "w_ple_gate": _dense(ks[19], (L, D_MODEL, D_MODEL), D_MODEL),
        "w_ple_proj": _dense(ks[20], (L, PLE_DIM, D_MODEL), PLE_DIM),
        "g_final": ones_noise(ks[21], (D_MODEL,)),
    }


def _fwd_reference(x, p, positions, g_mix, w_in, conv_w, conv_b, dt_bias, a_log, d_skip, g_ssd,
              sinks, w_attn_br, w_ssd_br, w_o, g_ffn, w_gate, w_up, w_down, g_ple,
              w_ple_gate, w_ple_proj, g_final):
    b, s = x.shape[0], x.shape[1]
    f32 = jnp.float32
    sizes = [Q_DIM, KV_DIM, KV_DIM, D_INNER, CONV_DIM, SSM_HEADS, D_MODEL, D_MODEL]
    offsets = [int(o) for o in np.cumsum(sizes)[:-1]]
    h = x
    for i in range(DEPTH):
        u = rmsnorm(h, g_mix[i])
        proj = u @ w_in[i]
        q, k, v, z, xbc, dt_raw, g_a, g_s = jnp.split(proj, offsets, axis=-1)

        q = apply_rope(q.astype(f32).reshape(b, s, ATTN_HEADS, HEAD_DIM), positions)
        k = apply_rope(k.astype(f32).reshape(b, s, KV_HEADS, HEAD_DIM), positions)
        v = v.astype(f32).reshape(b, s, KV_HEADS, HEAD_DIM)
        attn = sliding_window_sink_attention(q, k, v, sinks[i]).astype(x.dtype)
        out_a = attn @ w_attn_br[i]

        xbc = jax.nn.silu(causal_depthwise_conv(xbc, conv_w[i], conv_b[i])).astype(f32)
        xs, bm, cm = jnp.split(xbc, [D_INNER, D_INNER + SSM_GROUPS * D_STATE], axis=-1)
        xh = xs.reshape(b, s, SSM_HEADS, SSM_HEAD_DIM)
        dt = jax.nn.softplus(dt_raw.astype(f32) + dt_bias[i].astype(f32))
        a_neg = -jnp.exp(a_log[i].astype(f32))
        y = ssd_chunked(xh, dt, a_neg,
                        bm.reshape(b, s, SSM_GROUPS, D_STATE),
                        cm.reshape(b, s, SSM_GROUPS, D_STATE))
        y = (y + d_skip[i].astype(f32)[:, None] * xh).reshape(b, s, D_INNER)
        y = rmsnorm(y * jax.nn.silu(z.astype(f32)), g_ssd[i], SSM_NORM_EPS).astype(x.dtype)
        out_s = y @ w_ssd_br[i]

        merged = jax.nn.sigmoid(g_a) * out_a + jax.nn.sigmoid(g_s) * out_s
        h = h + merged @ w_o[i]

        f = rmsnorm(h, g_ffn[i])
        h = h + (jax.nn.silu(f @ w_gate[i]) * (f @ w_up[i])) @ w_down[i]

        gate = jax.nn.sigmoid(rmsnorm(h, g_ple[i]) @ w_ple_gate[i])
        h = h + gate * (p[i] @ w_ple_proj[i])
    return rmsnorm(h, g_final)


import jax as _jax
import jax.numpy as _jnp

TWIN_FORMAT = 'train_step'
FWD_PARAMS = ['x', 'p', 'positions', 'g_mix', 'w_in', 'conv_w', 'conv_b', 'dt_bias', 'a_log', 'd_skip', 'g_ssd', 'sinks', 'w_attn_br', 'w_ssd_br', 'w_o', 'g_ffn', 'w_gate', 'w_up', 'w_down', 'g_ple', 'w_ple_gate', 'w_ple_proj', 'g_final']
TWIN_WEIGHTS = ['g_mix', 'w_in', 'conv_w', 'conv_b', 'dt_bias', 'a_log', 'd_skip', 'g_ssd', 'sinks', 'w_attn_br', 'w_ssd_br', 'w_o', 'g_ffn', 'w_gate', 'w_up', 'w_down', 'g_ple', 'w_ple_gate', 'w_ple_proj', 'g_final']
TWIN_DIFF_INPUT = 'x'
TWIN_INPUTS = ['x', 'p', 'positions', 'g_mix', 'w_in', 'conv_w', 'conv_b', 'dt_bias', 'a_log', 'd_skip', 'g_ssd', 'sinks', 'w_attn_br', 'w_ssd_br', 'w_o', 'g_ffn', 'w_gate', 'w_up', 'w_down', 'g_ple', 'w_ple_gate', 'w_ple_proj', 'g_final', 'loss_target', 'm_g_mix', 'm_w_in', 'm_conv_w', 'm_conv_b', 'm_dt_bias', 'm_a_log', 'm_d_skip', 'm_g_ssd', 'm_sinks', 'm_w_attn_br', 'm_w_ssd_br', 'm_w_o', 'm_g_ffn', 'm_w_gate', 'm_w_up', 'm_w_down', 'm_g_ple', 'm_w_ple_gate', 'm_w_ple_proj', 'm_g_final', 'v_g_mix', 'v_w_in', 'v_conv_w', 'v_conv_b', 'v_dt_bias', 'v_a_log', 'v_d_skip', 'v_g_ssd', 'v_sinks', 'v_w_attn_br', 'v_w_ssd_br', 'v_w_o', 'v_g_ffn', 'v_w_gate', 'v_w_up', 'v_w_down', 'v_g_ple', 'v_w_ple_gate', 'v_w_ple_proj', 'v_g_final']
TWIN_OUTPUTS = ['loss', 'grad_x', 'grad_g_mix', 'grad_w_in', 'grad_conv_w', 'grad_conv_b', 'grad_dt_bias', 'grad_a_log', 'grad_d_skip', 'grad_g_ssd', 'grad_sinks', 'grad_w_attn_br', 'grad_w_ssd_br', 'grad_w_o', 'grad_g_ffn', 'grad_w_gate', 'grad_w_up', 'grad_w_down', 'grad_g_ple', 'grad_w_ple_gate', 'grad_w_ple_proj', 'grad_g_final', 'delta_g_mix', 'delta_w_in', 'delta_conv_w', 'delta_conv_b', 'delta_dt_bias', 'delta_a_log', 'delta_d_skip', 'delta_g_ssd', 'delta_sinks', 'delta_w_attn_br', 'delta_w_ssd_br', 'delta_w_o', 'delta_g_ffn', 'delta_w_gate', 'delta_w_up', 'delta_w_down', 'delta_g_ple', 'delta_w_ple_gate', 'delta_w_ple_proj', 'delta_g_final', 'new_m_g_mix', 'new_m_w_in', 'new_m_conv_w', 'new_m_conv_b', 'new_m_dt_bias', 'new_m_a_log', 'new_m_d_skip', 'new_m_g_ssd', 'new_m_sinks', 'new_m_w_attn_br', 'new_m_w_ssd_br', 'new_m_w_o', 'new_m_g_ffn', 'new_m_w_gate', 'new_m_w_up', 'new_m_w_down', 'new_m_g_ple', 'new_m_w_ple_gate', 'new_m_w_ple_proj', 'new_m_g_final', 'new_v_g_mix', 'new_v_w_in', 'new_v_conv_w', 'new_v_conv_b', 'new_v_dt_bias', 'new_v_a_log', 'new_v_d_skip', 'new_v_g_ssd', 'new_v_sinks', 'new_v_w_attn_br', 'new_v_w_ssd_br', 'new_v_w_o', 'new_v_g_ffn', 'new_v_w_gate', 'new_v_w_up', 'new_v_w_down', 'new_v_g_ple', 'new_v_w_ple_gate', 'new_v_w_ple_proj', 'new_v_g_final']
TWIN_LEAF_KINDS = {'loss': 'loss', 'grad_x': 'grad_x', 'grad_g_mix': 'grad_w', 'grad_w_in': 'grad_w', 'grad_conv_w': 'grad_w', 'grad_conv_b': 'grad_w', 'grad_dt_bias': 'grad_w', 'grad_a_log': 'grad_w', 'grad_d_skip': 'grad_w', 'grad_g_ssd': 'grad_w', 'grad_sinks': 'grad_w', 'grad_w_attn_br': 'grad_w', 'grad_w_ssd_br': 'grad_w', 'grad_w_o': 'grad_w', 'grad_g_ffn': 'grad_w', 'grad_w_gate': 'grad_w', 'grad_w_up': 'grad_w', 'grad_w_down': 'grad_w', 'grad_g_ple': 'grad_w', 'grad_w_ple_gate': 'grad_w', 'grad_w_ple_proj': 'grad_w', 'grad_g_final': 'grad_w', 'delta_g_mix': 'delta_w', 'delta_w_in': 'delta_w', 'delta_conv_w': 'delta_w', 'delta_conv_b': 'delta_w', 'delta_dt_bias': 'delta_w', 'delta_a_log': 'delta_w', 'delta_d_skip': 'delta_w', 'delta_g_ssd': 'delta_w', 'delta_sinks': 'delta_w', 'delta_w_attn_br': 'delta_w', 'delta_w_ssd_br': 'delta_w', 'delta_w_o': 'delta_w', 'delta_g_ffn': 'delta_w', 'delta_w_gate': 'delta_w', 'delta_w_up': 'delta_w', 'delta_w_down': 'delta_w', 'delta_g_ple': 'delta_w', 'delta_w_ple_gate': 'delta_w', 'delta_w_ple_proj': 'delta_w', 'delta_g_final': 'delta_w', 'new_m_g_mix': 'new_m', 'new_m_w_in': 'new_m', 'new_m_conv_w': 'new_m', 'new_m_conv_b': 'new_m', 'new_m_dt_bias': 'new_m', 'new_m_a_log': 'new_m', 'new_m_d_skip': 'new_m', 'new_m_g_ssd': 'new_m', 'new_m_sinks': 'new_m', 'new_m_w_attn_br': 'new_m', 'new_m_w_ssd_br': 'new_m', 'new_m_w_o': 'new_m', 'new_m_g_ffn': 'new_m', 'new_m_w_gate': 'new_m', 'new_m_w_up': 'new_m', 'new_m_w_down': 'new_m', 'new_m_g_ple': 'new_m', 'new_m_w_ple_gate': 'new_m', 'new_m_w_ple_proj': 'new_m', 'new_m_g_final': 'new_m', 'new_v_g_mix': 'new_v', 'new_v_w_in': 'new_v', 'new_v_conv_w': 'new_v', 'new_v_conv_b': 'new_v', 'new_v_dt_bias': 'new_v', 'new_v_a_log': 'new_v', 'new_v_d_skip': 'new_v', 'new_v_g_ssd': 'new_v', 'new_v_sinks': 'new_v', 'new_v_w_attn_br': 'new_v', 'new_v_w_ssd_br': 'new_v', 'new_v_w_o': 'new_v', 'new_v_g_ffn': 'new_v', 'new_v_w_gate': 'new_v', 'new_v_w_up': 'new_v', 'new_v_w_down': 'new_v', 'new_v_g_ple': 'new_v', 'new_v_w_ple_gate': 'new_v', 'new_v_w_ple_proj': 'new_v', 'new_v_g_final': 'new_v'}


def _forward(args):
    return _fwd_reference(*[args[k] for k in FWD_PARAMS])


def _output_shape():
    out = _jax.eval_shape(lambda: _forward(_fwd_setup_inputs(0)))
    return out.shape, out.dtype

N_MICROBATCH = 1
ADAM_LR = 0.001
ADAM_B1 = 0.9
ADAM_B2 = 0.999
ADAM_EPS = 1e-08
ADAM_WD = 0.01
ADAM_STEP = 10
PER_EXAMPLE_BATCH_AXIS = {'x': 0, 'p': 1, 'positions': 0, 'loss_target': 0}
SHARED_INPUTS = []
_WEIGHT_DTYPES = {'g_mix': _jnp.float32, 'w_in': _jnp.float32, 'conv_w': _jnp.float32, 'conv_b': _jnp.float32, 'dt_bias': _jnp.float32, 'a_log': _jnp.float32, 'd_skip': _jnp.float32, 'g_ssd': _jnp.float32, 'sinks': _jnp.float32, 'w_attn_br': _jnp.float32, 'w_ssd_br': _jnp.float32, 'w_o': _jnp.float32, 'g_ffn': _jnp.float32, 'w_gate': _jnp.float32, 'w_up': _jnp.float32, 'w_down': _jnp.float32, 'g_ple': _jnp.float32, 'w_ple_gate': _jnp.float32, 'w_ple_proj': _jnp.float32, 'g_final': _jnp.float32}
MOMENT_SCALE = {'g_mix': 5.147006e-02, 'w_in': 2.194914e-02, 'conv_w': 2.751988e-02, 'conv_b': 3.685245e-02, 'dt_bias': 1.001003e-01, 'a_log': 1.398901e-01, 'd_skip': 1.245472e-01, 'g_ssd': 3.117174e-02, 'sinks': 4.779146e-03, 'w_attn_br': 6.385589e-03, 'w_ssd_br': 3.146052e-02, 'w_o': 3.160683e-02, 'g_ffn': 3.936799e-02, 'w_gate': 1.703747e-02, 'w_up': 1.655751e-02, 'w_down': 2.745244e-02, 'g_ple': 9.955751e-03, 'w_ple_gate': 9.838667e-03, 'w_ple_proj': 2.497740e-02, 'g_final': 8.012516e+00}


def _to_microbatches(a, axis):
    t = _jnp.moveaxis(a, axis, 0)
    t = t.reshape((N_MICROBATCH, t.shape[0] // N_MICROBATCH) + t.shape[1:])
    return _jnp.moveaxis(t, 1, axis + 1)


def setup_inputs(seed: int = 0) -> dict:
    inp = _fwd_setup_inputs(seed)
    key = _jax.random.fold_in(_jax.random.key(seed), 7919)
    shape, _ = _output_shape()
    out = dict(inp)
    out["loss_target"] = _jax.random.normal(_jax.random.fold_in(key, 0), shape, _jnp.float32)
    for i, name in enumerate(TWIN_WEIGHTS):
        w = inp[name].astype(_jnp.float32)
        if MOMENT_SCALE is None:
            s = _jnp.sqrt(_jnp.mean(_jnp.square(w)) + 1e-30)
        else:
            s = MOMENT_SCALE[name]
        km, kv = _jax.random.split(_jax.random.fold_in(key, i + 1))
        out[name] = w
        out["m_" + name] = s * _jax.random.normal(km, w.shape, _jnp.float32)
        out["v_" + name] = (s * s) * _jax.random.uniform(kv, w.shape, _jnp.float32, 0.5, 1.5)
    if N_MICROBATCH > 1:
        for name, axis in PER_EXAMPLE_BATCH_AXIS.items():
            out[name] = _to_microbatches(out[name], axis)
    return {'x': out['x'], 'p': out['p'], 'positions': out['positions'], 'g_mix': out['g_mix'], 'w_in': out['w_in'], 'conv_w': out['conv_w'], 'conv_b': out['conv_b'], 'dt_bias': out['dt_bias'], 'a_log': out['a_log'], 'd_skip': out['d_skip'], 'g_ssd': out['g_ssd'], 'sinks': out['sinks'], 'w_attn_br': out['w_attn_br'], 'w_ssd_br': out['w_ssd_br'], 'w_o': out['w_o'], 'g_ffn': out['g_ffn'], 'w_gate': out['w_gate'], 'w_up': out['w_up'], 'w_down': out['w_down'], 'g_ple': out['g_ple'], 'w_ple_gate': out['w_ple_gate'], 'w_ple_proj': out['w_ple_proj'], 'g_final': out['g_final'], 'loss_target': out['loss_target'], 'm_g_mix': out['m_g_mix'], 'm_w_in': out['m_w_in'], 'm_conv_w': out['m_conv_w'], 'm_conv_b': out['m_conv_b'], 'm_dt_bias': out['m_dt_bias'], 'm_a_log': out['m_a_log'], 'm_d_skip': out['m_d_skip'], 'm_g_ssd': out['m_g_ssd'], 'm_sinks': out['m_sinks'], 'm_w_attn_br': out['m_w_attn_br'], 'm_w_ssd_br': out['m_w_ssd_br'], 'm_w_o': out['m_w_o'], 'm_g_ffn': out['m_g_ffn'], 'm_w_gate': out['m_w_gate'], 'm_w_up': out['m_w_up'], 'm_w_down': out['m_w_down'], 'm_g_ple': out['m_g_ple'], 'm_w_ple_gate': out['m_w_ple_gate'], 'm_w_ple_proj': out['m_w_ple_proj'], 'm_g_final': out['m_g_final'], 'v_g_mix': out['v_g_mix'], 'v_w_in': out['v_w_in'], 'v_conv_w': out['v_conv_w'], 'v_conv_b': out['v_conv_b'], 'v_dt_bias': out['v_dt_bias'], 'v_a_log': out['v_a_log'], 'v_d_skip': out['v_d_skip'], 'v_g_ssd': out['v_g_ssd'], 'v_sinks': out['v_sinks'], 'v_w_attn_br': out['v_w_attn_br'], 'v_w_ssd_br': out['v_w_ssd_br'], 'v_w_o': out['v_w_o'], 'v_g_ffn': out['v_g_ffn'], 'v_w_gate': out['v_w_gate'], 'v_w_up': out['v_w_up'], 'v_w_down': out['v_w_down'], 'v_g_ple': out['v_g_ple'], 'v_w_ple_gate': out['v_w_ple_gate'], 'v_w_ple_proj': out['v_w_ple_proj'], 'v_g_final': out['v_g_final']}


def _loss(weights, diff, rest, loss_target):
    with _jax.named_scope("forward"):
        args = {**rest, TWIN_DIFF_INPUT: diff, **{k: w.astype(_WEIGHT_DTYPES[k]) for k, w in weights.items()}}
        y = _forward(args)
    with _jax.named_scope("loss_head"):
        err = _jnp.square(y.astype(_jnp.float32) - loss_target)
        return 0.5 * _jnp.sum(_jnp.mean(err, axis=-1)) if err.ndim else 0.5 * err


def _adamw(w, g, m, v):
    m = ADAM_B1 * m + (1.0 - ADAM_B1) * g
    v = ADAM_B2 * v + (1.0 - ADAM_B2) * _jnp.square(g)
    m_hat = m / (1.0 - ADAM_B1 ** ADAM_STEP)
    v_hat = v / (1.0 - ADAM_B2 ** ADAM_STEP)
    delta = -ADAM_LR * (m_hat / (_jnp.sqrt(v_hat) + ADAM_EPS) + ADAM_WD * w)
    return delta, m, v


def reference(x, p, positions, g_mix, w_in, conv_w, conv_b, dt_bias, a_log, d_skip, g_ssd, sinks, w_attn_br, w_ssd_br, w_o, g_ffn, w_gate, w_up, w_down, g_ple, w_ple_gate, w_ple_proj, g_final, loss_target, m_g_mix, m_w_in, m_conv_w, m_conv_b, m_dt_bias, m_a_log, m_d_skip, m_g_ssd, m_sinks, m_w_attn_br, m_w_ssd_br, m_w_o, m_g_ffn, m_w_gate, m_w_up, m_w_down, m_g_ple, m_w_ple_gate, m_w_ple_proj, m_g_final, v_g_mix, v_w_in, v_conv_w, v_conv_b, v_dt_bias, v_a_log, v_d_skip, v_g_ssd, v_sinks, v_w_attn_br, v_w_ssd_br, v_w_o, v_g_ffn, v_w_gate, v_w_up, v_w_down, v_g_ple, v_w_ple_gate, v_w_ple_proj, v_g_final):
    given = dict(x=x, p=p, positions=positions, g_mix=g_mix, w_in=w_in, conv_w=conv_w, conv_b=conv_b, dt_bias=dt_bias, a_log=a_log, d_skip=d_skip, g_ssd=g_ssd, sinks=sinks, w_attn_br=w_attn_br, w_ssd_br=w_ssd_br, w_o=w_o, g_ffn=g_ffn, w_gate=w_gate, w_up=w_up, w_down=w_down, g_ple=g_ple, w_ple_gate=w_ple_gate, w_ple_proj=w_ple_proj, g_final=g_final, loss_target=loss_target, m_g_mix=m_g_mix, m_w_in=m_w_in, m_conv_w=m_conv_w, m_conv_b=m_conv_b, m_dt_bias=m_dt_bias, m_a_log=m_a_log, m_d_skip=m_d_skip, m_g_ssd=m_g_ssd, m_sinks=m_sinks, m_w_attn_br=m_w_attn_br, m_w_ssd_br=m_w_ssd_br, m_w_o=m_w_o, m_g_ffn=m_g_ffn, m_w_gate=m_w_gate, m_w_up=m_w_up, m_w_down=m_w_down, m_g_ple=m_g_ple, m_w_ple_gate=m_w_ple_gate, m_w_ple_proj=m_w_ple_proj, m_g_final=m_g_final, v_g_mix=v_g_mix, v_w_in=v_w_in, v_conv_w=v_conv_w, v_conv_b=v_conv_b, v_dt_bias=v_dt_bias, v_a_log=v_a_log, v_d_skip=v_d_skip, v_g_ssd=v_g_ssd, v_sinks=v_sinks, v_w_attn_br=v_w_attn_br, v_w_ssd_br=v_w_ssd_br, v_w_o=v_w_o, v_g_ffn=v_g_ffn, v_w_gate=v_w_gate, v_w_up=v_w_up, v_w_down=v_w_down, v_g_ple=v_g_ple, v_w_ple_gate=v_w_ple_gate, v_w_ple_proj=v_w_ple_proj, v_g_final=v_g_final)
    weights = {n: given[n] for n in TWIN_WEIGHTS}
    shared = {n: given[n] for n in SHARED_INPUTS}
    per_example = {n: given[n] for n in ['x', 'p', 'positions']}
    grad_fn = _jax.value_and_grad(_loss, argnums=(0, 1))

    def one_microbatch(ex, loss_target):
        ex = dict(ex)
        diff = ex.pop(TWIN_DIFF_INPUT)
        return grad_fn(weights, diff, {**shared, **ex}, loss_target)

    if N_MICROBATCH == 1:
        loss, (grad_w, grad_x) = one_microbatch(per_example, given["loss_target"])
    else:
        def body(carry, xs):
            loss_sum, grad_sum = carry
            l_k, (gw_k, gx_k) = one_microbatch(xs[0], xs[1])
            with _jax.named_scope("update"):
                return (loss_sum + l_k, _jax.tree.map(_jnp.add, grad_sum, gw_k)), gx_k

        init = (_jnp.zeros((), _jnp.float32), _jax.tree.map(_jnp.zeros_like, weights))
        (loss, grad_w), grad_x = _jax.lax.scan(body, init, (per_example, given["loss_target"]))
    with _jax.named_scope("update"):
        delta_w, new_m, new_v = {}, {}, {}
        for n in TWIN_WEIGHTS:
            delta_w[n], new_m[n], new_v[n] = _adamw(weights[n], grad_w[n], given["m_" + n], given["v_" + n])
    return (loss, grad_x, *[grad_w[n] for n in TWIN_WEIGHTS], *[delta_w[n] for n in TWIN_WEIGHTS],
            *[new_m[n] for n in TWIN_WEIGHTS], *[new_v[n] for n in TWIN_WEIGHTS])
```

```python
import functools

import jax
import jax.numpy as jnp
import numpy as np
from jax import lax
from jax.experimental import pallas as pl
from jax.experimental.pallas import tpu as pltpu

F32 = jnp.float32
BF16 = jnp.bfloat16
MESH = pl.DeviceIdType.MESH

D = 2048
HD = 64
NQH = 16
NKV = 4
QD = NQH * HD
KVD = NKV * HD
DI = 2048
NH = 32
NG = 4
NS = 128
CW = 4
L = 128
CONV = DI + 2 * NG * NS
FFN = 5632
PLE = 256
IN_DIM = QD + 2 * KVD + DI + CONV + NH + 2 * D
EPS = 1e-6
SSM_EPS = 1e-5
ROPE_THETA = 10000.0
LR, B1, B2, AEPS, WD, STEP = 0.001, 0.9, 0.999, 1e-08, 0.01, 10

O_Z, O_GA, O_GS, O_XBC, O_Q, O_K, O_V, O_DT = 0, 2048, 4096, 6144, 9216, 10240, 10496, 10752
DT_PAD = 512
NP = O_DT + DT_PAD
R_Q, R_K, R_V, R_Z, R_XBC, R_DT, R_GA, R_GS = 0, 1024, 1280, 1536, 3584, 6656, 6688, 8736

NCHIP = 4
VMEM_LIMIT = 52 * 1024 * 1024
NEG = -1e30


def _cp(sem=None):
    return pltpu.CompilerParams(dimension_semantics=sem, vmem_limit_bytes=VMEM_LIMIT)


def _dot(a, b):
    return lax.dot_general(a, b, (((1,), (0,)), ((), ())), preferred_element_type=F32)


def _dot_nt(a, b):
    return lax.dot_general(a, b, (((1,), (1,)), ((), ())), preferred_element_type=F32)


def _dot_tn(a, b):
    return lax.dot_general(a, b, (((0,), (0,)), ((), ())), preferred_element_type=F32)


def _sigmoid(x):
    return 1.0 / (1.0 + jnp.exp(-x))


def _matmul(a, b, *, tb=False, out_dtype=F32, add=None, tm, tn, tk, name):
    m, k = a.shape
    n = b.shape[0] if tb else b.shape[1]
    assert (b.shape[1] if tb else b.shape[0]) == k
    assert m % tm == 0 and n % tn == 0 and k % tk == 0, (name, a.shape, b.shape)
    nk = k // tk
    has_add = add is not None

    def body(*refs):
        a_ref, b_ref = refs[0], refs[1]
        add_ref = refs[2] if has_add else None
        o_ref = refs[3] if has_add else refs[2]
        av = a_ref[...].astype(BF16)
        bv = b_ref[...].astype(BF16)
        part = _dot_nt(av, bv) if tb else _dot(av, bv)

        def finish(r):
            if has_add:
                r = r + add_ref[...]
            o_ref[...] = r.astype(out_dtype)

        if nk == 1:
            finish(part)
        else:
            acc_ref = refs[-1]
            kk = pl.program_id(2)

            @pl.when(kk == 0)
            def _():
                acc_ref[...] = part

            @pl.when(kk > 0)
            def _():
                acc_ref[...] += part

            @pl.when(kk == nk - 1)
            def _():
                finish(acc_ref[...])

    in_specs = [pl.BlockSpec((tm, tk), lambda i, j, kk: (i, kk)),
                pl.BlockSpec((tn, tk), lambda i, j, kk: (j, kk)) if tb
                else pl.BlockSpec((tk, tn), lambda i, j, kk: (kk, j))]
    args = [a, b]
    if has_add:
        in_specs.append(pl.BlockSpec((tm, tn), lambda i, j, kk: (i, j)))
        args.append(add)
    return pl.pallas_call(
        body, name=name,
        out_shape=jax.ShapeDtypeStruct((m, n), out_dtype),
        grid=(m // tm, n // tn, nk),
        in_specs=in_specs,
        out_specs=pl.BlockSpec((tm, tn), lambda i, j, kk: (i, j)),
        scratch_shapes=[pltpu.VMEM((tm, tn), F32)] if nk > 1 else [],
        compiler_params=_cp(("parallel", "parallel", "arbitrary")),
    )(*args)


ROWS = 256


def _rmsnorm_fwd(x, g, *, name):
    t, d = x.shape

    def body(x_ref, g_ref, o_ref):
        xv = x_ref[...]
        r = lax.rsqrt(jnp.mean(xv * xv, axis=-1, keepdims=True) + EPS)
        o_ref[...] = (xv * r * g_ref[...]).astype(BF16)

    return pl.pallas_call(
        body, name=name, out_shape=jax.ShapeDtypeStruct((t, d), BF16), grid=(t // ROWS,),
        in_specs=[pl.BlockSpec((ROWS, d), lambda i: (i, 0)), pl.BlockSpec((1, d), lambda i: (0, 0))],
        out_specs=pl.BlockSpec((ROWS, d), lambda i: (i, 0)), compiler_params=_cp(("parallel",)),
    )(x, g)


def _rmsnorm_bwd(x, g, dy, dres, *, name):
    t, d = x.shape

    def body(x_ref, g_ref, dy_ref, dres_ref, dx_ref, dxb_ref, dg_ref):
        xv = x_ref[...]
        r = lax.rsqrt(jnp.mean(xv * xv, axis=-1, keepdims=True) + EPS)
        xh = xv * r
        dyv = dy_ref[...]
        dxh = dyv * g_ref[...]
        dx = r * (dxh - xh * jnp.mean(dxh * xh, axis=-1, keepdims=True))
        tot = dres_ref[...] + dx
        dx_ref[...] = tot
        dxb_ref[...] = tot.astype(BF16)

        @pl.when(pl.program_id(0) == 0)
        def _():
            dg_ref[...] = jnp.zeros_like(dg_ref)

        dg_ref[...] += jnp.broadcast_to(jnp.sum(dyv * xh, axis=0, keepdims=True), dg_ref.shape)

    row = pl.BlockSpec((ROWS, d), lambda i: (i, 0))
    return pl.pallas_call(
        body, name=name,
        out_shape=(jax.ShapeDtypeStruct((t, d), F32), jax.ShapeDtypeStruct((t, d), BF16),
                   jax.ShapeDtypeStruct((8, d), F32)),
        grid=(t // ROWS,),
        in_specs=[row, pl.BlockSpec((1, d), lambda i: (0, 0)), row, row],
        out_specs=(row, row, pl.BlockSpec((8, d), lambda i: (0, 0))),
        compiler_params=_cp(("arbitrary",)),
    )(x, g, dy, dres)


def _final(h2, pgl, pp, target, g_final, *, name):
    t, d = h2.shape

    def body(h2_ref, pgl_ref, pp_ref, tg_ref, g_ref, dh3_ref, dpgl_ref, dpp_ref, loss_ref, dg_ref):
        s = _sigmoid(pgl_ref[...])
        ppv = pp_ref[...]
        h3 = h2_ref[...] + s * ppv
        r = lax.rsqrt(jnp.mean(h3 * h3, axis=-1, keepdims=True) + EPS)
        xh = h3 * r
        gv = g_ref[...]
        err = xh * gv - tg_ref[...]
        dyv = err * (1.0 / d)
        dxh = dyv * gv
        dh3 = r * (dxh - xh * jnp.mean(dxh * xh, axis=-1, keepdims=True))
        dh3_ref[...] = dh3
        dpp_ref[...] = (dh3 * s).astype(BF16)
        dpgl_ref[...] = (dh3 * ppv * s * (1.0 - s)).astype(BF16)

        @pl.when(pl.program_id(0) == 0)
        def _():
            loss_ref[...] = jnp.zeros_like(loss_ref)
            dg_ref[...] = jnp.zeros_like(dg_ref)

        part = 0.5 * jnp.sum(jnp.mean(err * err, axis=-1, keepdims=True), axis=0, keepdims=True)
        loss_ref[...] += jnp.broadcast_to(part, loss_ref.shape)
        dg_ref[...] += jnp.broadcast_to(jnp.sum(dyv * xh, axis=0, keepdims=True), dg_ref.shape)

    row = pl.BlockSpec((ROWS, d), lambda i: (i, 0))
    return pl.pallas_call(
        body, name=name,
        out_shape=(jax.ShapeDtypeStruct((t, d), F32), jax.ShapeDtypeStruct((t, d), BF16),
                   jax.ShapeDtypeStruct((t, d), BF16), jax.ShapeDtypeStruct((8, 128), F32),
                   jax.ShapeDtypeStruct((8, d), F32)),
        grid=(t // ROWS,),
        in_specs=[row, row, row, row, pl.BlockSpec((1, d), lambda i: (0, 0))],
        out_specs=(row, row, row, pl.BlockSpec((8, 128), lambda i: (0, 0)), pl.BlockSpec((8, d), lambda i: (0, 0))),
        compiler_params=_cp(("arbitrary",)),
    )(h2, pgl, pp, target, g_final)


def _merge_fwd(proj, out_a, out_s, *, name):
    t = proj.shape[0]

    def body(ga_ref, gs_ref, a_ref, s_ref, o_ref):
        o_ref[...] = (_sigmoid(ga_ref[...]) * a_ref[...] + _sigmoid(gs_ref[...]) * s_ref[...]).astype(BF16)

    row = pl.BlockSpec((ROWS, D), lambda i: (i, 0))
    return pl.pallas_call(
        body, name=name, out_shape=jax.ShapeDtypeStruct((t, D), BF16), grid=(t // ROWS,),
        in_specs=[pl.BlockSpec((ROWS, D), lambda i: (i, O_GA // D)), pl.BlockSpec((ROWS, D), lambda i: (i, O_GS // D)),
                  row, row],
        out_specs=row, compiler_params=_cp(("parallel",)),
    )(proj, proj, out_a, out_s)


def _merge_bwd(proj, out_a, out_s, dmerged, *, name):
    t = proj.shape[0]

    def body(ga_ref, gs_ref, a_ref, s_ref, dm_ref, da_ref, ds_ref, dga_ref, dgs_ref):
        sa = _sigmoid(ga_ref[...])
        ss = _sigmoid(gs_ref[...])
        dm = dm_ref[...]
        da_ref[...] = (dm * sa).astype(BF16)
        ds_ref[...] = (dm * ss).astype(BF16)
        dga_ref[...] = (dm * a_ref[...] * sa * (1.0 - sa)).astype(BF16)
        dgs_ref[...] = (dm * s_ref[...] * ss * (1.0 - ss)).astype(BF16)

    row = pl.BlockSpec((ROWS, D), lambda i: (i, 0))
    o = jax.ShapeDtypeStruct((t, D), BF16)
    return pl.pallas_call(
        body, name=name, out_shape=(o, o, o, o), grid=(t // ROWS,),
        in_specs=[pl.BlockSpec((ROWS, D), lambda i: (i, O_GA // D)), pl.BlockSpec((ROWS, D), lambda i: (i, O_GS // D)),
                  row, row, row],
        out_specs=(row, row, row, row), compiler_params=_cp(("parallel",)),
    )(proj, proj, out_a, out_s, dmerged)


def _swiglu_fwd(f, w_gate, w_up, *, name, tn=256):
    t, d = f.shape
    n = w_gate.shape[1]

    def body(f_ref, wg_ref, wu_ref, g_ref, u_ref, a_ref):
        fv = f_ref[...]
        g = _dot(fv, wg_ref[...])
        u = _dot(fv, wu_ref[...])
        g_ref[...] = g
        u_ref[...] = u
        a_ref[...] = (g * _sigmoid(g) * u).astype(BF16)

    col = pl.BlockSpec((t, tn), lambda j: (0, j))
    wcol = pl.BlockSpec((d, tn), lambda j: (0, j))
    return pl.pallas_call(
        body, name=name,
        out_shape=(jax.ShapeDtypeStruct((t, n), F32), jax.ShapeDtypeStruct((t, n), F32),
                   jax.ShapeDtypeStruct((t, n), BF16)),
        grid=(n // tn,),
        in_specs=[pl.BlockSpec((t, d), lambda j: (0, 0)), wcol, wcol],
        out_specs=(col, col, col), compiler_params=_cp(("parallel",)),
    )(f, w_gate, w_up)


def _swiglu_bwd(gate, up, dact, *, name, tc=1408):
    t, n = gate.shape

    def body(g_ref, u_ref, da_ref, dg_ref, du_ref):
        g = g_ref[...]
        s = _sigmoid(g)
        da = da_ref[...]
        du_ref[...] = (da * g * s).astype(BF16)
        dg_ref[...] = (da * u_ref[...] * s * (1.0 + g * (1.0 - s))).astype(BF16)

    blk = pl.BlockSpec((ROWS, tc), lambda i, j: (i, j))
    o = jax.ShapeDtypeStruct((t, n), BF16)
    return pl.pallas_call(
        body, name=name, out_shape=(o, o), grid=(t // ROWS, n // tc),
        in_specs=[blk, blk, blk], out_specs=(blk, blk), compiler_params=_cp(("parallel", "parallel")),
    )(gate, up, dact)


def _gated_norm_fwd(y_pre, proj, g_ssd, *, name):
    t = y_pre.shape[0]

    def body(y_ref, z_ref, g_ref, o_ref):
        z = z_ref[...]
        v = y_ref[...] * z * _sigmoid(z)
        r = lax.rsqrt(jnp.mean(v * v, axis=-1, keepdims=True) + SSM_EPS)
        o_ref[...] = (v * r * g_ref[...]).astype(BF16)

    row = pl.BlockSpec((ROWS, DI), lambda i: (i, 0))
    return pl.pallas_call(
        body, name=name, out_shape=jax.ShapeDtypeStruct((t, DI), BF16), grid=(t // ROWS,),
        in_specs=[row, pl.BlockSpec((ROWS, DI), lambda i: (i, O_Z // DI)), pl.BlockSpec((1, DI), lambda i: (0, 0))],
        out_specs=row, compiler_params=_cp(("parallel",)),
    )(y_pre, proj, g_ssd)


def _gated_norm_bwd(y_pre, proj, g_ssd, dyn, *, name):
    t = y_pre.shape[0]

    def body(y_ref, z_ref, g_ref, dyn_ref, dy_ref, dz_ref, dg_ref):
        z = z_ref[...]
        s = _sigmoid(z)
        sz = z * s
        yv = y_ref[...]
        v = yv * sz
        r = lax.rsqrt(jnp.mean(v * v, axis=-1, keepdims=True) + SSM_EPS)
        vh = v * r
        dn = dyn_ref[...]
        dvh = dn * g_ref[...]
        dv = r * (dvh - vh * jnp.mean(dvh * vh, axis=-1, keepdims=True))
        dy_ref[...] = dv * sz
        dz_ref[...] = (dv * yv * s * (1.0 + z * (1.0 - s))).astype(BF16)

        @pl.when(pl.program_id(0) == 0)
        def _():
            dg_ref[...] = jnp.zeros_like(dg_ref)

        dg_ref[...] += jnp.broadcast_to(jnp.sum(dn * vh, axis=0, keepdims=True), dg_ref.shape)

    row = pl.BlockSpec((ROWS, DI), lambda i: (i, 0))
    return pl.pallas_call(
        body, name=name,
        out_shape=(jax.ShapeDtypeStruct((t, DI), F32), jax.ShapeDtypeStruct((t, DI), BF16),
                   jax.ShapeDtypeStruct((8, DI), F32)),
        grid=(t // ROWS,),
        in_specs=[row, pl.BlockSpec((ROWS, DI), lambda i: (i, O_Z // DI)), pl.BlockSpec((1, DI), lambda i: (0, 0)), row],
        out_specs=(row, row, pl.BlockSpec((8, DI), lambda i: (0, 0))),
        compiler_params=_cp(("arbitrary",)),
    )(y_pre, proj, g_ssd, dyn)


CONV_TC = 512


def _shift_down(x, s, row):
    if s == 0:
        return x
    return jnp.where(row >= s, pltpu.roll(x, s, 0), 0.0)


def _shift_up(x, s, row, t):
    if s == 0:
        return x
    return jnp.where(row < t - s, pltpu.roll(x, t - s, 0), 0.0)


def _conv_fwd(proj, conv_w, conv_b, *, name):
    t = proj.shape[0]

    def body(x_ref, w_ref, b_ref, o_ref):
        x = x_ref[...]
        row = lax.broadcasted_iota(jnp.int32, x.shape, 0)
        pre = jnp.broadcast_to(b_ref[...], x.shape)
        for k in range(CW):
            pre = pre + w_ref[k:k + 1, :] * _shift_down(x, CW - 1 - k, row)
        o_ref[...] = pre * _sigmoid(pre)

    return pl.pallas_call(
        body, name=name, out_shape=jax.ShapeDtypeStruct((t, CONV), F32), grid=(CONV // CONV_TC,),
        in_specs=[pl.BlockSpec((t, CONV_TC), lambda j: (0, O_XBC // CONV_TC + j)),
                  pl.BlockSpec((CW, CONV_TC), lambda j: (0, j)), pl.BlockSpec((1, CONV_TC), lambda j: (0, j))],
        out_specs=pl.BlockSpec((t, CONV_TC), lambda j: (0, j)), compiler_params=_cp(("parallel",)),
    )(proj, conv_w, conv_b)


def _conv_bwd(proj, conv_w, conv_b, dact, *, name):
    t = proj.shape[0]

    def body(x_ref, w_ref, b_ref, da_ref, dx_ref, dw_ref, db_ref):
        x = x_ref[...]
        row = lax.broadcasted_iota(jnp.int32, x.shape, 0)
        xs = [_shift_down(x, CW - 1 - k, row) for k in range(CW)]
        pre = jnp.broadcast_to(b_ref[...], x.shape)
        for k in range(CW):
            pre = pre + w_ref[k:k + 1, :] * xs[k]
        s = _sigmoid(pre)
        dpre = da_ref[...] * s * (1.0 + pre * (1.0 - s))
        dx = jnp.zeros_like(x)
        row8 = lax.broadcasted_iota(jnp.int32, dw_ref.shape, 0)
        dw = jnp.zeros(dw_ref.shape, F32)
        for k in range(CW):
            dx = dx + w_ref[k:k + 1, :] * _shift_up(dpre, CW - 1 - k, row, t)
            dw = dw + jnp.where(row8 == k, jnp.sum(dpre * xs[k], axis=0, keepdims=True), 0.0)
        dx_ref[...] = dx.astype(BF16)
        dw_ref[...] = dw
        db_ref[...] = jnp.broadcast_to(jnp.sum(dpre, axis=0, keepdims=True), db_ref.shape)

    col8 = pl.BlockSpec((8, CONV_TC), lambda j: (0, j))
    return pl.pallas_call(
        body, name=name,
        out_shape=(jax.ShapeDtypeStruct((t, CONV), BF16), jax.ShapeDtypeStruct((8, CONV), F32),
                   jax.ShapeDtypeStruct((8, CONV), F32)),
        grid=(CONV // CONV_TC,),
        in_specs=[pl.BlockSpec((t, CONV_TC), lambda j: (0, O_XBC // CONV_TC + j)),
                  pl.BlockSpec((CW, CONV_TC), lambda j: (0, j)), pl.BlockSpec((1, CONV_TC), lambda j: (0, j)),
                  pl.BlockSpec((t, CONV_TC), lambda j: (0, j))],
        out_specs=(pl.BlockSpec((t, CONV_TC), lambda j: (0, j)), col8, col8),
        compiler_params=_cp(("parallel",)),
    )(proj, conv_w, conv_b, dact)


def _rope_tables(positions, t):
    half = HD // 2
    inv_freq = ROPE_THETA ** (-jnp.arange(half, dtype=F32) * 2.0 / HD)
    ang = positions.reshape(t).astype(F32)[:, None] * inv_freq
    cos, sin = jnp.cos(ang), jnp.sin(ang)
    return jnp.concatenate([cos] * 4, axis=1), jnp.concatenate([-sin, sin] * 2, axis=1)


def _lane_consts():
    lane = lax.broadcasted_iota(jnp.int32, (L, 128), 1)
    return lane, (lane % HD) < (HD // 2), lane < HD


def _rope(tv, cos, sin, lo):
    return tv * cos + jnp.where(lo, pltpu.roll(tv, 128 - HD // 2, 1), pltpu.roll(tv, HD // 2, 1)) * sin


def _rope_t(dv, cos, sin, lo):
    ds = dv * sin
    return dv * cos + jnp.where(lo, pltpu.roll(ds, 128 - HD // 2, 1), pltpu.roll(ds, HD // 2, 1))


def _placed(chunk, g, half0):
    own = jnp.where(half0 if g % 2 == 0 else jnp.logical_not(half0), chunk, 0.0)
    other = pltpu.roll(own, HD, 1)
    return (own, other) if g % 2 == 0 else (other, own)


def _unplace(acc, hf, g, half0):
    v = jnp.where(half0 if hf == 0 else jnp.logical_not(half0), acc, 0.0)
    return v if hf == g % 2 else pltpu.roll(v, HD, 1)


def _attn_fwd(proj, cos, sin, sinks, *, name):
    t = proj.shape[0]
    nb = t // L
    scale = HD ** -0.5

    def body(sink_ref, q_ref, kc_ref, kp_ref, vc_ref, vp_ref, cc_ref, sc_ref, cp_ref, sp_ref, o_ref, lse_ref):
        i = pl.program_id(0)
        lane, lo, half0 = _lane_consts()
        cos_c, sin_c, cos_p, sin_p = cc_ref[...], sc_ref[...], cp_ref[...], sp_ref[...]
        row = lax.broadcasted_iota(jnp.int32, (L, L), 0)
        col = lax.broadcasted_iota(jnp.int32, (L, L), 1)
        m_cur = col <= row
        m_prev = jnp.logical_and(col > row, i > 0)
        kc = [_rope(kc_ref[:, 128 * m:128 * (m + 1)], cos_c, sin_c, lo) for m in range(2)]
        kp = [_rope(kp_ref[:, 128 * m:128 * (m + 1)], cos_p, sin_p, lo) for m in range(2)]
        lse_acc = jnp.zeros((L, 128), F32)
        outs = [jnp.zeros((L, 128), F32) for _ in range(QD // 128)]
        qs = [(_rope(q_ref[:, 128 * ch:128 * (ch + 1)], cos_c, sin_c, lo) * scale).astype(BF16) for ch in range(QD // 128)]
        for g in range(NKV):
            kcv = [v.astype(BF16) for v in _placed(kc[g // 2], g, half0)]
            kpv = [v.astype(BF16) for v in _placed(kp[g // 2], g, half0)]
            vcv = [v.astype(BF16) for v in _placed(vc_ref[:, 128 * (g // 2):128 * (g // 2 + 1)], g, half0)]
            vpv = [v.astype(BF16) for v in _placed(vp_ref[:, 128 * (g // 2):128 * (g // 2 + 1)], g, half0)]
            for r in range(NQH // NKV):
                h = g * (NQH // NKV) + r
                ch, hf = h // 2, h % 2
                s_c = jnp.where(m_cur, _dot_nt(qs[ch], kcv[hf]), NEG)
                s_p = jnp.where(m_prev, _dot_nt(qs[ch], kpv[hf]), NEG)
                sink = sink_ref[0, h]
                mx = jnp.maximum(jnp.maximum(jnp.max(s_c, axis=-1, keepdims=True), jnp.max(s_p, axis=-1, keepdims=True)), sink)
                e_c = jnp.exp(s_c - mx)
                e_p = jnp.exp(s_p - mx)
                den = jnp.sum(e_c, axis=-1, keepdims=True) + jnp.sum(e_p, axis=-1, keepdims=True) + jnp.exp(sink - mx)
                inv = 1.0 / den
                outs[ch] = outs[ch] + _dot((e_c * inv).astype(BF16), vcv[hf]) + _dot((e_p * inv).astype(BF16), vpv[hf])
                lse_acc = jnp.where(lane == h, mx + jnp.log(den), lse_acc)
        for ch in range(QD // 128):
            o_ref[:, 128 * ch:128 * (ch + 1)] = outs[ch].astype(BF16)
        lse_ref[...] = lse_acc

    prev = lambda i: jnp.maximum(i - 1, 0)
    tab_c = pl.BlockSpec((L, 128), lambda i: (i, 0))
    tab_p = pl.BlockSpec((L, 128), lambda i: (prev(i), 0))
    return pl.pallas_call(
        body, name=name,
        out_shape=(jax.ShapeDtypeStruct((t, QD), BF16), jax.ShapeDtypeStruct((t, 128), F32)),
        grid=(nb,),
        in_specs=[pl.BlockSpec(memory_space=pltpu.SMEM),
                  pl.BlockSpec((L, QD), lambda i: (i, O_Q // QD)),
                  pl.BlockSpec((L, KVD), lambda i: (i, O_K // KVD)), pl.BlockSpec((L, KVD), lambda i: (prev(i), O_K // KVD)),
                  pl.BlockSpec((L, KVD), lambda i: (i, O_V // KVD)), pl.BlockSpec((L, KVD), lambda i: (prev(i), O_V // KVD)),
                  tab_c, tab_c, tab_p, tab_p],
        out_specs=(pl.BlockSpec((L, QD), lambda i: (i, 0)), pl.BlockSpec((L, 128), lambda i: (i, 0))),
        compiler_params=_cp(("parallel",)),
    )(sinks, proj, proj, proj, proj, proj, cos, sin, cos, sin)


def _attn_bwd(proj, cos, sin, sinks, attn, lse, dattn, *, name):
    t = proj.shape[0]
    nb = t // L
    scale = HD ** -0.5

    def body(sink_ref, qi_ref, qn_ref, kc_ref, kp_ref, vc_ref, vp_ref, doi_ref, don_ref, oi_ref, on_ref,
             lsei_ref, lsen_ref, cc_ref, sc_ref, cp_ref, sp_ref, cn_ref, sn_ref, dq_ref, dk_ref, dv_ref, dsk_ref):
        i = pl.program_id(0)
        lane, lo, half0 = _lane_consts()
        half1 = jnp.logical_not(half0)
        cos_c, sin_c = cc_ref[...], sc_ref[...]
        row = lax.broadcasted_iota(jnp.int32, (L, L), 0)
        col = lax.broadcasted_iota(jnp.int32, (L, L), 1)
        m_cur = col <= row
        m_prev = jnp.logical_and(col > row, i > 0)
        m_next = jnp.logical_and(col > row, i < nb - 1)
        kc = [_rope(kc_ref[:, 128 * m:128 * (m + 1)], cos_c, sin_c, lo) for m in range(2)]
        kp = [_rope(kp_ref[:, 128 * m:128 * (m + 1)], cp_ref[...], sp_ref[...], lo) for m in range(2)]
        lse_i, lse_n = lsei_ref[...], lsen_ref[...]
        dk_acc = [jnp.zeros((L, 128), F32) for _ in range(2)]
        dv_acc = [jnp.zeros((L, 128), F32) for _ in range(2)]
        dsk_acc = jnp.zeros((1, 128), F32)
        lane1 = lax.broadcasted_iota(jnp.int32, (1, 128), 1)
        place = lambda chunk, g: [v.astype(BF16) for v in _placed(chunk, g, half0)]
        kcs = [place(kc[g // 2], g) for g in range(NKV)]
        kps = [place(kp[g // 2], g) for g in range(NKV)]
        vcs = [place(vc_ref[:, 128 * (g // 2):128 * (g // 2 + 1)], g) for g in range(NKV)]
        vps = [place(vp_ref[:, 128 * (g // 2):128 * (g // 2 + 1)], g) for g in range(NKV)]
        for ch in range(QD // 128):
            sl = slice(128 * ch, 128 * (ch + 1))
            q_i = (_rope(qi_ref[:, sl], cos_c, sin_c, lo) * scale).astype(BF16)
            q_n = (_rope(qn_ref[:, sl], cn_ref[...], sn_ref[...], lo) * scale).astype(BF16)
            do_i, do_n = doi_ref[:, sl], don_ref[:, sl]
            do_ib, do_nb = do_i.astype(BF16), do_n.astype(BF16)
            od_i = do_i * oi_ref[:, sl].astype(F32)
            od_n = do_n * on_ref[:, sl].astype(F32)
            dq_ch = jnp.zeros((L, 128), F32)
            for hf in range(2):
                h = 2 * ch + hf
                g = h // (NQH // NKV)
                hm = half0 if hf == 0 else half1
                kcv, kpv, vcv, vpv = kcs[g][hf], kps[g][hf], vcs[g][hf], vps[g][hf]
                dl_i = jnp.sum(jnp.where(hm, od_i, 0.0), axis=-1, keepdims=True)
                dl_n = jnp.sum(jnp.where(hm, od_n, 0.0), axis=-1, keepdims=True)
                ls_i = jnp.sum(jnp.where(lane == h, lse_i, 0.0), axis=-1, keepdims=True)
                ls_n = jnp.sum(jnp.where(lane == h, lse_n, 0.0), axis=-1, keepdims=True)
                p_c = jnp.where(m_cur, jnp.exp(_dot_nt(q_i, kcv) - ls_i), 0.0)
                p_p = jnp.where(m_prev, jnp.exp(_dot_nt(q_i, kpv) - ls_i), 0.0)
                ds_c = (p_c * (_dot_nt(do_ib, vcv) - dl_i)).astype(BF16)
                ds_p = (p_p * (_dot_nt(do_ib, vpv) - dl_i)).astype(BF16)
                dq_ch = dq_ch + jnp.where(hm, (_dot(ds_c, kcv) + _dot(ds_p, kpv)) * scale, 0.0)
                sink = sink_ref[0, h]
                dsk = -jnp.sum(jnp.exp(sink - ls_i) * dl_i, axis=0, keepdims=True)
                dsk_acc = dsk_acc + jnp.where(lane1 == h, dsk, 0.0)
                p_n = jnp.where(m_next, jnp.exp(_dot_nt(q_n, kcv) - ls_n), 0.0)
                ds_n = (p_n * (_dot_nt(do_nb, vcv) - dl_n)).astype(BF16)
                dv_h = _dot_tn(p_c.astype(BF16), do_ib) + _dot_tn(p_n.astype(BF16), do_nb)
                dk_h = _dot_tn(ds_c, q_i) + _dot_tn(ds_n, q_n)
                dv_acc[g // 2] = dv_acc[g // 2] + _unplace(dv_h, hf, g, half0)
                dk_acc[g // 2] = dk_acc[g // 2] + _unplace(dk_h, hf, g, half0)
            dq_ref[:, sl] = _rope_t(dq_ch, cos_c, sin_c, lo).astype(BF16)
        for m in range(2):
            dk_ref[:, 128 * m:128 * (m + 1)] = _rope_t(dk_acc[m], cos_c, sin_c, lo).astype(BF16)
            dv_ref[:, 128 * m:128 * (m + 1)] = dv_acc[m].astype(BF16)

        @pl.when(i == 0)
        def _():
            dsk_ref[...] = jnp.zeros_like(dsk_ref)

        dsk_ref[...] += jnp.broadcast_to(dsk_acc, dsk_ref.shape)

    prev = lambda i: jnp.maximum(i - 1, 0)
    nxt = lambda i: jnp.minimum(i + 1, nb - 1)
    cur_q = pl.BlockSpec((L, QD), lambda i: (i, 0))
    nxt_q = pl.BlockSpec((L, QD), lambda i: (nxt(i), 0))
    tab = lambda f: pl.BlockSpec((L, 128), lambda i: (f(i), 0))
    ident = lambda i: i
    kv_o = jax.ShapeDtypeStruct((t, KVD), BF16)
    return pl.pallas_call(
        body, name=name,
        out_shape=(jax.ShapeDtypeStruct((t, QD), BF16), kv_o, kv_o, jax.ShapeDtypeStruct((8, 128), F32)),
        grid=(nb,),
        in_specs=[pl.BlockSpec(memory_space=pltpu.SMEM),
                  pl.BlockSpec((L, QD), lambda i: (i, O_Q // QD)), pl.BlockSpec((L, QD), lambda i: (nxt(i), O_Q // QD)),
                  pl.BlockSpec((L, KVD), lambda i: (i, O_K // KVD)), pl.BlockSpec((L, KVD), lambda i: (prev(i), O_K // KVD)),
                  pl.BlockSpec((L, KVD), lambda i: (i, O_V // KVD)), pl.BlockSpec((L, KVD), lambda i: (prev(i), O_V // KVD)),
                  cur_q, nxt_q, cur_q, nxt_q, tab(ident), tab(nxt),
                  tab(ident), tab(ident), tab(prev), tab(prev), tab(nxt), tab(nxt)],
        out_specs=(cur_q, pl.BlockSpec((L, KVD), lambda i: (i, 0)), pl.BlockSpec((L, KVD), lambda i: (i, 0)),
                   pl.BlockSpec((8, 128), lambda i: (0, 0))),
        compiler_params=_cp(("arbitrary",)),
    )(sinks, proj, proj, proj, proj, proj, proj, dattn, dattn, attn, attn, lse, lse, cos, sin, cos, sin, cos, sin)


PAIRS = NH // NG // 2


def _softplus(x):
    return jnp.maximum(x, 0.0) + jnp.log(1.0 + jnp.exp(-jnp.abs(x)))


def _ssd_chunk(g, xps, dtr, bm, cm, sps, dtb, alog, dsk):
    lane = lax.broadcasted_iota(jnp.int32, (L, 128), 1)
    lane1 = lax.broadcasted_iota(jnp.int32, (1, 128), 1)
    row = lax.broadcasted_iota(jnp.int32, (L, L), 0)
    col = lax.broadcasted_iota(jnp.int32, (L, L), 1)
    rowc = lax.broadcasted_iota(jnp.int32, (128, 1), 0)
    tril = col <= row
    dt = _softplus(dtr + dtb)
    a = dt * (-jnp.exp(alog))
    a_cs = lax.dot_general(tril.astype(F32), a, (((1,), (0,)), ((), ())), precision=lax.Precision.HIGHEST,
                           preferred_element_type=F32)
    a_cst = a_cs.T
    a_last = jnp.sum(jnp.where(row == L - 1, a_cs, 0.0), axis=0, keepdims=True)
    cb = _dot_nt(cm.astype(BF16), bm.astype(BF16))
    ys, snew = [], []
    for q in range(PAIRS):
        xp, sp = xps[q], sps[q]
        y_pair = jnp.zeros((L, 128), F32)
        st_pair = jnp.zeros((128, NS), F32)
        keep = jnp.zeros((128, 1), F32)
        for hh in range(2):
            h = g * 2 * PAIRS + 2 * q + hh
            hm = (lane < HD) if hh == 0 else (lane >= HD)
            rm = (rowc < HD) if hh == 0 else (rowc >= HD)
            dt_h = jnp.sum(jnp.where(lane == h, dt, 0.0), axis=1, keepdims=True)
            acs_h = jnp.sum(jnp.where(lane == h, a_cs, 0.0), axis=1, keepdims=True)
            acst_h = jnp.sum(jnp.where(row == h, a_cst, 0.0), axis=0, keepdims=True)
            al_h = jnp.sum(jnp.where(lane1 == h, a_last, 0.0), axis=1, keepdims=True)
            dsk_h = jnp.sum(jnp.where(lane1 == h, dsk, 0.0), axis=1, keepdims=True)
            decay = jnp.where(tril, jnp.exp(jnp.where(tril, acs_h - acst_h, 0.0)), 0.0)
            xh = jnp.where(hm, xp, 0.0)
            xd = (xh * dt_h).astype(BF16)
            y = _dot((cb * decay).astype(BF16), xd)
            y = y + jnp.where(hm, _dot_nt((cm * jnp.exp(acs_h)).astype(BF16), sp.astype(BF16)), 0.0)
            y_pair = y_pair + y + dsk_h * xh
            st_pair = st_pair + _dot_tn(xd, (bm * jnp.exp(al_h - acs_h)).astype(BF16))
            keep = keep + jnp.where(rm, jnp.exp(al_h), 0.0)
        ys.append(y_pair)
        snew.append(sp * keep + st_pair)
    return ys, snew


def _ssd_specs(t):
    nc = t // L
    xs = lambda f: pl.BlockSpec((L, 128 * PAIRS), lambda c, g: (f(c), g))
    bspec = lambda f: pl.BlockSpec((L, NS), lambda c, g: (f(c), DI // NS + g))
    cspec = lambda f: pl.BlockSpec((L, NS), lambda c, g: (f(c), DI // NS + NG + g))
    dts = lambda f: pl.BlockSpec((L, 128), lambda c, g: (f(c), O_DT // 128))
    par = pl.BlockSpec((1, 128), lambda c, g: (0, 0))
    st = lambda f: pl.BlockSpec((1, 1, PAIRS, 128, NS), lambda c, g: (f(c), g, 0, 0, 0))
    return nc, xs, bspec, cspec, dts, par, st


def _ssd_fwd(xbc_act, proj, dtb, alog, dsk, *, name):
    t = proj.shape[0]
    nc, xs, bspec, cspec, dts, par, st = _ssd_specs(t)
    ident = lambda c: c

    def body(x_ref, b_ref, c_ref, dt_ref, dtb_ref, al_ref, dsk_ref, y_ref, sin_ref, s_ref):
        c, g = pl.program_id(0), pl.program_id(1)

        @pl.when(c == 0)
        def _():
            s_ref[g] = jnp.zeros((PAIRS, 128, NS), F32)

        sps = [s_ref[g, q] for q in range(PAIRS)]
        for q in range(PAIRS):
            sin_ref[0, 0, q] = sps[q]
        xps = [x_ref[:, 128 * q:128 * (q + 1)] for q in range(PAIRS)]
        ys, snew = _ssd_chunk(g, xps, dt_ref[...], b_ref[...], c_ref[...], sps, dtb_ref[...], al_ref[...], dsk_ref[...])
        for q in range(PAIRS):
            y_ref[:, 128 * q:128 * (q + 1)] = ys[q]
            s_ref[g, q] = snew[q]

    return pl.pallas_call(
        body, name=name,
        out_shape=(jax.ShapeDtypeStruct((t, DI), F32), jax.ShapeDtypeStruct((nc, NG, PAIRS, 128, NS), F32)),
        grid=(nc, NG),
        in_specs=[xs(ident), bspec(ident), cspec(ident), dts(ident), par, par, par],
        out_specs=(pl.BlockSpec((L, 128 * PAIRS), lambda c, g: (c, g)), st(ident)),
        scratch_shapes=[pltpu.VMEM((NG, PAIRS, 128, NS), F32)],
        compiler_params=_cp(("arbitrary", "arbitrary")),
    )(xbc_act, xbc_act, xbc_act, proj, dtb, alog, dsk)


def _ssd_bwd(xbc_act, proj, dtb, alog, dsk, states, dy, *, name):
    t = proj.shape[0]
    nc, xs, bspec, cspec, dts, par, st = _ssd_specs(t)
    rev = lambda c: nc - 1 - c

    def body(x_ref, b_ref, c_ref, dt_ref, dtb_ref, al_ref, dsk_ref, sin_ref, dy_ref,
             dx_ref, db_ref, dc_ref, ddt_ref, ddtb_ref, dal_ref, ddsk_ref, ds_ref):
        c, g = pl.program_id(0), pl.program_id(1)

        @pl.when(c == 0)
        def _():
            ds_ref[g] = jnp.zeros((PAIRS, 128, NS), F32)

        @pl.when(jnp.logical_and(c == 0, g == 0))
        def _():
            ddtb_ref[...] = jnp.zeros_like(ddtb_ref)
            dal_ref[...] = jnp.zeros_like(dal_ref)
            ddsk_ref[...] = jnp.zeros_like(ddsk_ref)

        @pl.when(g == 0)
        def _():
            ddt_ref[...] = jnp.zeros_like(ddt_ref)

        sps = [sin_ref[0, 0, q] for q in range(PAIRS)]
        xps = [x_ref[:, 128 * q:128 * (q + 1)] for q in range(PAIRS)]
        _, vjp = jax.vjp(functools.partial(_ssd_chunk, g), xps, dt_ref[...], b_ref[...], c_ref[...], sps,
                         dtb_ref[...], al_ref[...], dsk_ref[...])
        dys = [dy_ref[:, 128 * q:128 * (q + 1)] for q in range(PAIRS)]
        dss = [ds_ref[g, q] for q in range(PAIRS)]
        dxps, ddt, db, dc, dsps, ddtb, dal, ddsk = vjp((dys, dss))
        for q in range(PAIRS):
            dx_ref[:, 128 * q:128 * (q + 1)] = dxps[q]
            ds_ref[g, q] = dsps[q]
        db_ref[...] = db
        dc_ref[...] = dc
        ddt_ref[...] += ddt
        ddtb_ref[...] += jnp.broadcast_to(ddtb, ddtb_ref.shape)
        dal_ref[...] += jnp.broadcast_to(dal, dal_ref.shape)
        ddsk_ref[...] += jnp.broadcast_to(ddsk, ddsk_ref.shape)

    acc = pl.BlockSpec((8, 128), lambda c, g: (0, 0))
    o8 = jax.ShapeDtypeStruct((8, 128), F32)
    return pl.pallas_call(
        body, name=name,
        out_shape=(jax.ShapeDtypeStruct((t, DI), F32), jax.ShapeDtypeStruct((t, NG * NS), F32),
                   jax.ShapeDtypeStruct((t, NG * NS), F32), jax.ShapeDtypeStruct((t, 128), F32), o8, o8, o8),
        grid=(nc, NG),
        in_specs=[xs(rev), bspec(rev), cspec(rev), dts(rev), par, par, par, st(rev),
                  pl.BlockSpec((L, 128 * PAIRS), lambda c, g: (rev(c), g))],
        out_specs=(pl.BlockSpec((L, 128 * PAIRS), lambda c, g: (rev(c), g)),
                   pl.BlockSpec((L, NS), lambda c, g: (rev(c), g)), pl.BlockSpec((L, NS), lambda c, g: (rev(c), g)),
                   pl.BlockSpec((L, 128), lambda c, g: (rev(c), 0)), acc, acc, acc),
        scratch_shapes=[pltpu.VMEM((NG, PAIRS, 128, NS), F32)],
        compiler_params=_cp(("arbitrary", "arbitrary")),
    )(xbc_act, xbc_act, xbc_act, proj, dtb, alog, dsk, states, dy)


def _pad_lanes(v, n=128):
    return jnp.pad(v, ((0, 0), (0, n - v.shape[1])))


def _local_step(x, p, positions, target, small, big):
    t = x.shape[0]
    cos, sin = _rope_tables(positions, t)
    dtb, alog, dsk = _pad_lanes(small["dt_bias"]), _pad_lanes(small["a_log"]), _pad_lanes(small["d_skip"])
    mm = functools.partial(_matmul, tm=t)
    mh = functools.partial(_matmul, tm=t // 2)

    u = _rmsnorm_fwd(x, small["g_mix"], name="norm_mix")
    proj = mm(u, big["w_in"], tn=512, tk=D, name="mm_in")
    attn, lse = _attn_fwd(proj, cos, sin, small["sinks"], name="attn_fwd")
    out_a = mm(attn, big["w_attn_br"], tn=512, tk=QD, name="mm_attn_br")
    xbc_act = _conv_fwd(proj, small["conv_w"], small["conv_b"], name="conv_fwd")
    y_pre, states = _ssd_fwd(xbc_act, proj, dtb, alog, dsk, name="ssd_fwd")
    yn = _gated_norm_fwd(y_pre, proj, small["g_ssd"], name="gated_norm_fwd")
    out_s = mm(yn, big["w_ssd_br"], tn=512, tk=DI, name="mm_ssd_br")
    merged = _merge_fwd(proj, out_a, out_s, name="merge_fwd")
    h1 = mm(merged, big["w_o"], add=x, tn=512, tk=D, name="mm_o")
    f = _rmsnorm_fwd(h1, small["g_ffn"], name="norm_ffn")
    gate, up, act = _swiglu_fwd(f, big["w_gate"], big["w_up"], name="swiglu_fwd")
    h2 = mh(act, big["w_down"], add=h1, tn=512, tk=FFN // 2, name="mm_down")
    e = _rmsnorm_fwd(h2, small["g_ple"], name="norm_ple")
    pgl = mm(e, big["w_ple_gate"], tn=512, tk=D, name="mm_ple_gate")
    pb = p.astype(BF16)
    pp = mm(pb, big["w_ple_proj"], tn=512, tk=PLE, name="mm_ple_proj")
    dh3, dpgl, dpp, loss, dg_final = _final(h2, pgl, pp, target, small["g_final"].reshape(1, D), name="final")

    gb = {}
    gb["w_ple_proj"] = _matmul(pb.T, dpp, out_dtype=BF16, tm=PLE, tn=512, tk=t, name="mm_d_ple_proj")
    gb["w_ple_gate"] = _matmul(e.T, dpgl, out_dtype=BF16, tm=D, tn=512, tk=t, name="mm_d_ple_gate")
    de = mm(dpgl, big["w_ple_gate"], tb=True, tn=512, tk=D, name="mm_de")
    dh2, dh2b, dg_ple = _rmsnorm_bwd(h2, small["g_ple"], de, dh3, name="norm_ple_bwd")
    gb["w_down"] = _matmul(act.T, dh2b, out_dtype=BF16, tm=FFN // 2, tn=512, tk=t, name="mm_d_down")
    dact = mm(dh2b, big["w_down"], tb=True, tn=512, tk=D, name="mm_dact")
    dgate, dup = _swiglu_bwd(gate, up, dact, name="swiglu_bwd")
    ft = f.T
    gb["w_gate"] = _matmul(ft, dgate, out_dtype=BF16, tm=D, tn=512, tk=t, name="mm_d_gate")
    gb["w_up"] = _matmul(ft, dup, out_dtype=BF16, tm=D, tn=512, tk=t, name="mm_d_up")
    df = mh(dgate, big["w_gate"], tb=True, tn=512, tk=FFN // 2, name="mm_df_gate")
    df = mh(dup, big["w_up"], tb=True, add=df, tn=512, tk=FFN // 2, name="mm_df_up")
    dh1, dh1b, dg_ffn = _rmsnorm_bwd(h1, small["g_ffn"], df, dh2, name="norm_ffn_bwd")
    gb["w_o"] = _matmul(merged.T, dh1b, out_dtype=BF16, tm=D, tn=512, tk=t, name="mm_d_o")
    dmerged = mm(dh1b, big["w_o"], tb=True, tn=512, tk=D, name="mm_dmerged")
    dout_a, dout_s, dga, dgs = _merge_bwd(proj, out_a, out_s, dmerged, name="merge_bwd")
    gb["w_attn_br"] = _matmul(attn.T, dout_a, out_dtype=BF16, tm=QD, tn=512, tk=t, name="mm_d_attn_br")
    gb["w_ssd_br"] = _matmul(yn.T, dout_s, out_dtype=BF16, tm=DI, tn=512, tk=t, name="mm_d_ssd_br")
    dattn = mm(dout_a, big["w_attn_br"], tb=True, tn=512, tk=D, name="mm_dattn")
    dyn = mm(dout_s, big["w_ssd_br"], tb=True, tn=512, tk=D, name="mm_dyn")
    dq, dk, dv, dsinks = _attn_bwd(proj, cos, sin, small["sinks"], attn, lse, dattn, name="attn_bwd")
    dy_pre, dz, dg_ssd = _gated_norm_bwd(y_pre, proj, small["g_ssd"], dyn, name="gated_norm_bwd")
    dxs, db, dc, ddt, ddtb, dalog, ddsk = _ssd_bwd(xbc_act, proj, dtb, alog, dsk, states, dy_pre, name="ssd_bwd")
    dxbc_act = jnp.concatenate([dxs, db, dc], axis=1)
    dxbc, dconv_w, dconv_b = _conv_bwd(proj, small["conv_w"], small["conv_b"], dxbc_act, name="conv_bwd")
    ddt_b = jnp.pad(ddt.astype(BF16), ((0, 0), (0, DT_PAD - 128)))
    dproj = jnp.concatenate([dz, dga, dgs, dxbc, dq, dk, dv, ddt_b], axis=1)
    gb["w_in"] = _matmul(u.T, dproj, out_dtype=BF16, tm=D, tn=512, tk=t, name="mm_d_in")
    du = mh(dproj, big["w_in"], tb=True, tn=512, tk=NP // 4, name="mm_du")
    grad_x, _, dg_mix = _rmsnorm_bwd(x, small["g_mix"], du, dh1, name="norm_mix_bwd")

    gs = {
        "g_mix": dg_mix[:1], "conv_w": dconv_w[:CW], "conv_b": dconv_b[:1], "dt_bias": ddtb[:1, :NH],
        "a_log": dalog[:1, :NH], "d_skip": ddsk[:1, :NH], "g_ssd": dg_ssd[:1], "sinks": dsinks[:1, :NQH],
        "g_ffn": dg_ffn[:1], "g_ple": dg_ple[:1], "g_final": dg_final[0],
    }
    return loss, grad_x, gb, gs


def _to_kernel_cols(w):
    seg = lambda o, n: w[:, o:o + n]
    return jnp.concatenate([seg(R_Z, DI), seg(R_GA, D), seg(R_GS, D), seg(R_XBC, CONV), seg(R_Q, QD), seg(R_K, KVD),
                            seg(R_V, KVD), seg(R_DT, NH), jnp.zeros((w.shape[0], DT_PAD - NH), w.dtype)], axis=1)


def _from_kernel_cols(g):
    seg = lambda o, n: g[:, o:o + n]
    return jnp.concatenate([seg(O_Q, QD), seg(O_K, KVD), seg(O_V, KVD), seg(O_Z, DI), seg(O_XBC, CONV), seg(O_DT, NH),
                            seg(O_GA, D), seg(O_GS, D)], axis=1)


RELS = ((0, 1), (1, 0), (1, 1))
ANY = pl.BlockSpec(memory_space=pl.ANY)
PACKS = (
    ("rows3", "row", 3, 512, 2048, 512, 256),
    ("down", "row", 1, 1408, 2048, 704, 704),
    ("gu", "col", 2, 2048, 1408, 256, 256),
    ("in", "stk", 1, 2048, 2696, 256, 256),
    ("abr", "col", 1, 1024, 512, 256, 256),
    ("pproj", "col", 1, 256, 512, 128, 128),
)


def _pos():
    return lax.axis_index("x"), lax.axis_index("y"), lax.axis_index("c")


def _flip(v, a):
    return 1 - v if a else v


def _remote(src, dst, send, recv, dev):
    return pltpu.make_async_remote_copy(src_ref=src, dst_ref=dst, send_sem=send, recv_sem=recv, device_id=dev,
                                        device_id_type=MESH)


def _whole_shape(kind, g, r, c):
    return {"row": (g, NCHIP * r, c), "col": (g, r, NCHIP * c), "stk": (NCHIP, r, c)}[kind]


def _cols(j, c):
    return pl.ds(pl.multiple_of(j * c, 128), c)


def _whole_shard(kind, ref, j, r, c):
    if kind == "row":
        return ref.at[:, pl.ds(j * r, r), :]
    if kind == "col":
        return ref.at[:, :, _cols(j, c)]
    return ref.at[pl.ds(j, 1)]


def _whole_shard_rows(kind, ref, j, h, r, c):
    if kind == "row":
        return ref.at[:, pl.ds(j * r + h * (r // 2), r // 2), :]
    if kind == "col":
        return ref.at[:, pl.ds(h * (r // 2), r // 2), _cols(j, c)]
    return ref.at[pl.ds(j, 1), pl.ds(h * (r // 2), r // 2), :]


def _gather_weights(shards):
    n = len(PACKS)

    def body(*refs):
        srcs, dsts = refs[:n], refs[n:2 * n]
        send, recv, loc = refs[2 * n:]
        x, y, c = _pos()
        me, sib = 2 * x + y, (x, y, 1 - c)
        peers = [(_flip(x, a), _flip(y, b)) for a, b in RELS]
        own, first, passed = [], [], []
        for w, (_, kind, g, r, cc, _, _) in enumerate(PACKS):
            cp = pltpu.make_async_copy(srcs[w], _whole_shard(kind, dsts[w], me, r, cc), loc.at[w])
            cp.start()
            own.append(cp)
            mine = srcs[w].at[:, pl.ds(c * (r // 2), r // 2), :]
            for k, (px, py) in enumerate(peers):
                cp = _remote(mine, _whole_shard_rows(kind, dsts[w], me, c, r, cc), send.at[6 * w + k], recv.at[6 * w + k],
                             (px, py, c))
                cp.start()
                first.append(cp)
        for w, (_, kind, g, r, cc, _, _) in enumerate(PACKS):
            for k, (px, py) in enumerate(peers):
                land = _whole_shard_rows(kind, dsts[w], 2 * px + py, c, r, cc)
                _remote(land, land, send.at[6 * w + k], recv.at[6 * w + k], sib).wait_recv()
                cp = _remote(land, land, send.at[6 * w + 3 + k], recv.at[6 * w + 3 + k], sib)
                cp.start()
                passed.append(cp)
        for w, (_, kind, g, r, cc, _, _) in enumerate(PACKS):
            for k, (px, py) in enumerate(peers):
                land = _whole_shard_rows(kind, dsts[w], 2 * px + py, 1 - c, r, cc)
                _remote(land, land, send.at[6 * w + 3 + k], recv.at[6 * w + 3 + k], sib).wait_recv()
        for cp in first + passed:
            cp.wait_send()
        for cp in own:
            cp.wait()

    return pl.pallas_call(
        body, name="gather_weights",
        out_shape=[jax.ShapeDtypeStruct(_whole_shape(kind, g, r, c), BF16) for _, kind, g, r, c, _, _ in PACKS],
        in_specs=[ANY] * n, out_specs=[ANY] * n,
        scratch_shapes=[pltpu.SemaphoreType.DMA((6 * n,)), pltpu.SemaphoreType.DMA((6 * n,)), pltpu.SemaphoreType.DMA((n,))],
    )(*shards)


def _half_of_whole(kind, ref, h, r, c):
    if kind == "row":
        return ref.at[:, :, pl.ds(pl.multiple_of(h * (c // 2), 128), c // 2)]
    return ref.at[:, pl.ds(h * (r // 2), r // 2), :]


def _half_shape(kind, g, r, c):
    return {"row": (g, NCHIP * r, c // 2), "col": (g, r // 2, NCHIP * c), "stk": (NCHIP, r // 2, c)}[kind]


def _piece_shape(kind, g, r, c):
    return {"row": (g, r, c // 2), "col": (g, r // 2, c), "stk": (1, r // 2, c)}[kind]


def _swap_halves(wholes):
    n = len(PACKS)

    def body(*refs):
        srcs, dsts = refs[:n], refs[n:2 * n]
        send, recv = refs[2 * n:]
        x, y, c = _pos()
        cps = []
        for w, (_, kind, g, r, cc, _, _) in enumerate(PACKS):
            cp = _remote(_half_of_whole(kind, srcs[w], 1 - c, r, cc), dsts[w], send.at[w], recv.at[w], (x, y, 1 - c))
            cp.start()
            cps.append(cp)
        for cp in cps:
            cp.wait()

    return pl.pallas_call(
        body, name="swap_halves",
        out_shape=[jax.ShapeDtypeStruct(_half_shape(kind, g, r, c), BF16) for _, kind, g, r, c, _, _ in PACKS],
        in_specs=[ANY] * n, out_specs=[ANY] * n,
        scratch_shapes=[pltpu.SemaphoreType.DMA((n,)), pltpu.SemaphoreType.DMA((n,))],
    )(*wholes)


def _pair_sum(pack, core, whole, got):
    name, kind, g, r, c, tr, _ = pack
    hs = _half_shape(kind, g, r, c)
    nb = hs[1] // tr

    def body(core_ref, a_ref, b_ref, o_ref):
        o_ref[...] = (a_ref[...].astype(F32) + b_ref[...].astype(F32)).astype(BF16)

    blk = (1, tr, hs[2])
    if kind == "row":
        a_map = lambda gi, i, core_ref: (gi, i, core_ref[0])
    else:
        a_map = lambda gi, i, core_ref: (gi, core_ref[0] * nb + i, 0)
    same = lambda gi, i, core_ref: (gi, i, 0)
    return pl.pallas_call(
        body, name="pair_sum_" + name, out_shape=jax.ShapeDtypeStruct(hs, BF16),
        grid_spec=pltpu.PrefetchScalarGridSpec(
            num_scalar_prefetch=1, grid=(hs[0], nb),
            in_specs=[pl.BlockSpec(blk, a_map), pl.BlockSpec(blk, same)], out_specs=pl.BlockSpec(blk, same)),
        compiler_params=_cp(("parallel", "parallel")),
    )(core, whole, got)


def _send_pieces(halves):
    n = len(PACKS)

    def body(*refs):
        srcs, dsts = refs[:n], refs[n:2 * n]
        send, recv = refs[2 * n:]
        x, y, c = _pos()
        cps = []
        for w, (_, kind, g, r, cc, _, _) in enumerate(PACKS):
            for k, (a, b) in enumerate(RELS):
                px, py = _flip(x, a), _flip(y, b)
                j = 2 * px + py
                if kind == "row":
                    piece = srcs[w].at[:, pl.ds(j * r, r), :]
                elif kind == "col":
                    piece = srcs[w].at[:, :, _cols(j, cc)]
                else:
                    piece = srcs[w].at[pl.ds(j, 1)]
                cp = _remote(piece, dsts[w].at[k], send.at[3 * w + k], recv.at[3 * w + k], (px, py, c))
                cp.start()
                cps.append(cp)
        for cp in cps:
            cp.wait()

    return pl.pallas_call(
        body, name="send_pieces",
        out_shape=[jax.ShapeDtypeStruct((3,) + _piece_shape(kind, g, r, c), BF16) for _, kind, g, r, c, _, _ in PACKS],
        in_specs=[ANY] * n, out_specs=[ANY] * n,
        scratch_shapes=[pltpu.SemaphoreType.DMA((3 * n,)), pltpu.SemaphoreType.DMA((3 * n,))],
    )(*halves)


def _shard_sum(pack, where, half, got):
    name, kind, g, r, c, _, tr = pack
    ps = _piece_shape(kind, g, r, c)
    nb = ps[1] // tr

    def body(where_ref, a_ref, b_ref, o_ref):
        o_ref[...] = a_ref[...].astype(F32) + ((b_ref[0].astype(F32) + b_ref[1].astype(F32)) + b_ref[2].astype(F32))

    blk = (1, tr, ps[2])
    if kind == "row":
        a_map = lambda gi, i, wr: (gi, wr[0] * nb + i, 0)
        o_map = lambda gi, i, wr: (gi, i, wr[1])
    elif kind == "col":
        a_map = lambda gi, i, wr: (gi, i, wr[0])
        o_map = lambda gi, i, wr: (gi, wr[1] * nb + i, 0)
    else:
        a_map = lambda gi, i, wr: (wr[0], i, 0)
        o_map = lambda gi, i, wr: (gi, wr[1] * nb + i, 0)
    return pl.pallas_call(
        body, name="shard_sum_" + name, out_shape=jax.ShapeDtypeStruct((g, r, c), F32),
        grid_spec=pltpu.PrefetchScalarGridSpec(
            num_scalar_prefetch=1, grid=(ps[0], nb),
            in_specs=[pl.BlockSpec(blk, a_map), pl.BlockSpec((3,) + blk, lambda gi, i, wr: (0, gi, i, 0))],
            out_specs=pl.BlockSpec(blk, o_map)),
        compiler_params=_cp(("parallel", "parallel")),
    )(where, half, got)


def _swap_shard_halves(shards):
    n = len(PACKS)

    def body(*refs):
        dsts = refs[n:2 * n]
        send, recv = refs[2 * n:]
        x, y, c = _pos()

        def half(w, h):
            _, kind, g, r, cc, _, _ = PACKS[w]
            if kind == "row":
                return dsts[w].at[:, :, pl.ds(pl.multiple_of(h * (cc // 2), 128), cc // 2)]
            return dsts[w].at[:, pl.ds(h * (r // 2), r // 2), :]

        cps = []
        for w in range(n):
            cp = _remote(half(w, c), half(w, c), send.at[w], recv.at[w], (x, y, 1 - c))
            cp.start()
            cps.append(cp)
        for w, cp in enumerate(cps):
            cp.wait_send()
            _remote(half(w, 1 - c), half(w, 1 - c), send.at[w], recv.at[w], (x, y, 1 - c)).wait_recv()

    return pl.pallas_call(
        body, name="swap_shard_halves",
        out_shape=[jax.ShapeDtypeStruct((g, r, c), F32) for _, _, g, r, c, _, _ in PACKS],
        in_specs=[ANY] * n, out_specs=[ANY] * n, input_output_aliases={w: w for w in range(n)},
        scratch_shapes=[pltpu.SemaphoreType.DMA((n,)), pltpu.SemaphoreType.DMA((n,))],
    )(*shards)


NDEV = 8


def _allreduce_small(v, *, name):
    rows = v.shape[0]

    def body(v_ref, o_ref, slots, send, recv):
        x, y, c = _pos()
        me = 4 * x + 2 * y + c
        slots[me] = v_ref[...]
        cps = []
        for k in range(1, NDEV):
            peer = (_flip(x, k & 4), _flip(y, k & 2), _flip(c, k & 1))
            cp = _remote(v_ref, slots.at[me], send.at[k - 1], recv.at[k - 1], peer)
            cp.start()
            cps.append(cp)
        for cp in cps:
            cp.wait()
        acc = slots[0]
        for s in range(1, NDEV):
            acc = acc + slots[s]
        o_ref[...] = acc

    return pl.pallas_call(
        body, name=name, out_shape=jax.ShapeDtypeStruct((rows, 128), F32),
        in_specs=[pl.BlockSpec(memory_space=pltpu.VMEM)], out_specs=pl.BlockSpec(memory_space=pltpu.VMEM),
        scratch_shapes=[pltpu.VMEM((NDEV, rows, 128), F32), pltpu.SemaphoreType.DMA((NDEV - 1,)),
                        pltpu.SemaphoreType.DMA((NDEV - 1,))],
    )(v)


def _adamw(w, g, m, v, *, tr, name):
    r, c = w.shape
    tr = r if tr is None else tr
    c1 = 1.0 / (1.0 - B1 ** STEP)
    c2 = 1.0 / (1.0 - B2 ** STEP)

    def body(w_ref, g_ref, m_ref, v_ref, d_ref, mo_ref, vo_ref):
        gv = g_ref[...]
        mn = B1 * m_ref[...] + (1.0 - B1) * gv
        vn = B2 * v_ref[...] + (1.0 - B2) * (gv * gv)
        mo_ref[...] = mn
        vo_ref[...] = vn
        d_ref[...] = -LR * ((mn * c1) / (jnp.sqrt(vn * c2) + AEPS) + WD * w_ref[...])

    blk = pl.BlockSpec((tr, c), lambda i: (i, 0))
    o = jax.ShapeDtypeStruct((r, c), F32)
    return pl.pallas_call(
        body, name=name, out_shape=(o, o, o), grid=(r // tr,), in_specs=[blk] * 4, out_specs=(blk, blk, blk),
        compiler_params=_cp(("parallel",)),
    )(w, g, m, v)


WEIGHTS = ("g_mix", "w_in", "conv_w", "conv_b", "dt_bias", "a_log", "d_skip", "g_ssd", "sinks", "w_attn_br", "w_ssd_br",
           "w_o", "g_ffn", "w_gate", "w_up", "w_down", "g_ple", "w_ple_gate", "w_ple_proj", "g_final")
BIG = {
    "w_ssd_br": ("rows3", 0, 128), "w_o": ("rows3", 1, 128), "w_ple_gate": ("rows3", 2, 128), "w_down": ("down", 0, 128),
    "w_gate": ("gu", 0, 256), "w_up": ("gu", 1, 256), "w_in": ("in", 0, 128), "w_attn_br": ("abr", 0, 256),
    "w_ple_proj": ("pproj", 0, 256),
}
SMALL = tuple(n for n in WEIGHTS if n not in BIG)


def _pack_small(parts):
    rows = []
    for a in parts:
        a = a.reshape(-1)
        rows.append(jnp.pad(a, (0, -a.shape[0] % 128)).reshape(-1, 128))
    out = jnp.concatenate(rows, axis=0)
    return jnp.pad(out, ((0, -out.shape[0] % 8), (0, 0)))


def _unpack_small(packed, shapes):
    out, r = [], 0
    for s in shapes:
        n = int(np.prod(s))
        nr = -(-n // 128)
        out.append(packed[r:r + nr].reshape(-1)[:n].reshape(s))
        r += nr
    return out


def kernel(x, p, positions, g_mix, w_in, conv_w, conv_b, dt_bias, a_log, d_skip, g_ssd, sinks, w_attn_br, w_ssd_br, w_o, g_ffn, w_gate, w_up, w_down, g_ple, w_ple_gate, w_ple_proj, g_final, loss_target, m_g_mix, m_w_in, m_conv_w, m_conv_b, m_dt_bias, m_a_log, m_d_skip, m_g_ssd, m_sinks, m_w_attn_br, m_w_ssd_br, m_w_o, m_g_ffn, m_w_gate, m_w_up, m_w_down, m_g_ple, m_w_ple_gate, m_w_ple_proj, m_g_final, v_g_mix, v_w_in, v_conv_w, v_conv_b, v_dt_bias, v_a_log, v_d_skip, v_g_ssd, v_sinks, v_w_attn_br, v_w_ssd_br, v_w_o, v_g_ffn, v_w_gate, v_w_up, v_w_down, v_g_ple, v_w_ple_gate, v_w_ple_proj, v_g_final):
    w = dict(zip(WEIGHTS, (g_mix, w_in, conv_w, conv_b, dt_bias, a_log, d_skip, g_ssd, sinks, w_attn_br, w_ssd_br, w_o,
                           g_ffn, w_gate, w_up, w_down, g_ple, w_ple_gate, w_ple_proj, g_final)))
    m = dict(zip(WEIGHTS, (m_g_mix, m_w_in, m_conv_w, m_conv_b, m_dt_bias, m_a_log, m_d_skip, m_g_ssd, m_sinks, m_w_attn_br,
                           m_w_ssd_br, m_w_o, m_g_ffn, m_w_gate, m_w_up, m_w_down, m_g_ple, m_w_ple_gate, m_w_ple_proj,
                           m_g_final)))
    v = dict(zip(WEIGHTS, (v_g_mix, v_w_in, v_conv_w, v_conv_b, v_dt_bias, v_a_log, v_d_skip, v_g_ssd, v_sinks, v_w_attn_br,
                           v_w_ssd_br, v_w_o, v_g_ffn, v_w_gate, v_w_up, v_w_down, v_g_ple, v_w_ple_gate, v_w_ple_proj,
                           v_g_final)))
    xi, yi, ci = _pos()
    chip = 2 * xi + yi
    t = x.shape[1]
    cshard = CONV // NCHIP

    bf = lambda a: a.astype(BF16)
    shards = {"rows3": jnp.concatenate([bf(w["w_ssd_br"]), bf(w["w_o"]), bf(w["w_ple_gate"])], axis=0),
              "down": bf(w["w_down"]), "gu": jnp.concatenate([bf(w["w_gate"]), bf(w["w_up"])], axis=0),
              "in": bf(w["w_in"]), "abr": bf(w["w_attn_br"]), "pproj": bf(w["w_ple_proj"])}
    whole = dict(zip([pk[0] for pk in PACKS], _gather_weights([shards[pk[0]] for pk in PACKS])))
    big = {n: whole[pack][idx] for n, (pack, idx, _) in BIG.items() if n != "w_in"}
    big["w_in"] = _to_kernel_cols(whole["in"].transpose(1, 0, 2).reshape(D, IN_DIM))
    placed = lax.dynamic_update_slice(jnp.zeros((CW, CONV), F32), w["conv_w"][0], (0, chip * cshard))
    conv_whole = _allreduce_small(jnp.where(ci == 0, placed, 0.0).reshape(-1, 128), name="gather_conv_w").reshape(CW, CONV)

    small = {n: w[n] for n in ("g_mix", "conv_b", "dt_bias", "a_log", "d_skip", "g_ssd", "sinks", "g_ffn", "g_ple", "g_final")}
    small["conv_w"] = conv_whole
    loss8, grad_x, gb, gs = _local_step(x[0], p[0, 0], positions, loss_target[0], small, big)

    gwhole = {"rows3": jnp.stack([gb["w_ssd_br"], gb["w_o"], gb["w_ple_gate"]]), "down": gb["w_down"][None],
              "gu": jnp.stack([gb["w_gate"], gb["w_up"]]),
              "in": _from_kernel_cols(gb["w_in"]).reshape(D, NCHIP, IN_DIM // NCHIP).transpose(1, 0, 2),
              "abr": gb["w_attn_br"][None], "pproj": gb["w_ple_proj"][None]}
    wholes = [gwhole[pk[0]] for pk in PACKS]
    core = ci.reshape(1).astype(jnp.int32)
    where = jnp.stack([chip, ci]).astype(jnp.int32)
    got = _swap_halves(wholes)
    halves = [_pair_sum(pk, core, a, b) for pk, a, b in zip(PACKS, wholes, got)]
    got = _send_pieces(halves)
    part = [_shard_sum(pk, where, a, b) for pk, a, b in zip(PACKS, halves, got)]
    gshard = dict(zip([pk[0] for pk in PACKS], _swap_shard_halves(part)))

    order = ("g_mix", "conv_b", "dt_bias", "a_log", "d_skip", "g_ssd", "sinks", "g_ffn", "g_ple", "g_final", "conv_w")
    summed = _allreduce_small(_pack_small([loss8[0, :1]] + [gs[n] for n in order]), name="sum_small")
    parts = _unpack_small(summed, [(1,)] + [w[n].shape for n in order[:-1]] + [(CW, CONV)])
    loss = parts[0][0]
    grad = dict(zip(order, parts[1:]))
    grad["conv_w"] = lax.dynamic_slice(grad["conv_w"], (0, chip * cshard), (CW, cshard))[None]

    delta, new_m, new_v = {}, {}, {}
    for n, (pack, idx, tr) in BIG.items():
        grad[n] = gshard[pack][idx][None]
        d_, m_, v_ = _adamw(w[n][0], grad[n][0], m[n][0], v[n][0], tr=tr, name="adamw_" + n)
        delta[n], new_m[n], new_v[n] = d_[None], m_[None], v_[None]
    shapes = [w[n].shape for n in SMALL]
    d_, m_, v_ = _adamw(_pack_small([w[n] for n in SMALL]), _pack_small([grad[n] for n in SMALL]),
                        _pack_small([m[n] for n in SMALL]), _pack_small([v[n] for n in SMALL]), tr=None, name="adamw_small")
    for n, a, b, c_ in zip(SMALL, _unpack_small(d_, shapes), _unpack_small(m_, shapes), _unpack_small(v_, shapes)):
        delta[n], new_m[n], new_v[n] = a, b, c_

    return (loss, grad_x[None], *[grad[n] for n in WEIGHTS], *[delta[n] for n in WEIGHTS],
            *[new_m[n] for n in WEIGHTS], *[new_v[n] for n in WEIGHTS])
```

```python
import functools

import jax
import jax.numpy as jnp
import numpy as np
from jax import lax
from jax.experimental import pallas as pl
from jax.experimental.pallas import tpu as pltpu

F32 = jnp.float32
BF16 = jnp.bfloat16
MESH = pl.DeviceIdType.MESH

D = 2048
HD = 64
NQH = 16
NKV = 4
QD = NQH * HD
KVD = NKV * HD
DI = 2048
NH = 32
NG = 4
NS = 128
CW = 4
L = 128
CONV = DI + 2 * NG * NS
FFN = 5632
PLE = 256
IN_DIM = QD + 2 * KVD + DI + CONV + NH + 2 * D
EPS = 1e-6
SSM_EPS = 1e-5
ROPE_THETA = 10000.0
LR, B1, B2, AEPS, WD, STEP = 0.001, 0.9, 0.999, 1e-08, 0.01, 10

O_Z, O_GA, O_GS, O_XBC, O_Q, O_K, O_V, O_DT = 0, 2048, 4096, 6144, 9216, 10240, 10496, 10752
DT_PAD = 512
NP = O_DT + DT_PAD
R_Q, R_K, R_V, R_Z, R_XBC, R_DT, R_GA, R_GS = 0, 1024, 1280, 1536, 3584, 6656, 6688, 8736

NCHIP = 4
VMEM_LIMIT = 52 * 1024 * 1024
NEG = -1e30


def _cp(sem=None):
    return pltpu.CompilerParams(dimension_semantics=sem, vmem_limit_bytes=VMEM_LIMIT)


def _dot(a, b):
    return lax.dot_general(a, b, (((1,), (0,)), ((), ())), preferred_element_type=F32)


def _dot_nt(a, b):
    return lax.dot_general(a, b, (((1,), (1,)), ((), ())), preferred_element_type=F32)


def _dot_tn(a, b):
    return lax.dot_general(a, b, (((0,), (0,)), ((), ())), preferred_element_type=F32)


def _sigmoid(x):
    return 1.0 / (1.0 + jnp.exp(-x))


ANY = pl.BlockSpec(memory_space=pl.ANY)


class _Job:
    srcs, dsts, news, scratch = (), (), (), ()

    def start(self, srcs, dsts, news, sems):
        raise NotImplementedError

    def finish(self, srcs, dsts, news, sems):
        raise NotImplementedError

    def done(self, dsts, news):
        pass


def _call(body, *, jobs=(), name, out_shape, in_specs, out_specs, grid=(), scratch_shapes=(), compiler_params=None):
    jobs = [j for j in jobs if j is not None]
    if not jobs:
        return pl.pallas_call(body, name=name, out_shape=out_shape, in_specs=in_specs, out_specs=out_specs, grid=grid,
                              scratch_shapes=scratch_shapes, compiler_params=compiler_params)
    single = not isinstance(out_shape, (tuple, list))
    outs = [out_shape] if single else list(out_shape)
    ospecs = [out_specs] if single else list(out_specs)
    n_in, n_out, n_scr = len(in_specs), len(outs), len(scratch_shapes)
    srcs = [a for j in jobs for a in j.srcs]
    dsts = [a for j in jobs for a in j.dsts]
    news = [a for j in jobs for a in j.news]
    sems = [a for j in jobs for a in j.scratch]

    def wrapped(*refs):
        pos = n_in + len(srcs) + len(dsts)
        ins, jsrc = refs[:n_in], refs[n_in:n_in + len(srcs)]
        o_refs = refs[pos:pos + n_out]
        pos += n_out
        jdst, jnew = refs[pos:pos + len(dsts)], refs[pos + len(dsts):pos + len(dsts) + len(news)]
        pos += len(dsts) + len(news)
        scr, jsem = refs[pos:pos + n_scr], refs[pos + n_scr:]

        def run(which):
            a = b = c = d = 0
            for j in jobs:
                getattr(j, which)(jsrc[a:a + len(j.srcs)], jdst[b:b + len(j.dsts)], jnew[c:c + len(j.news)],
                                  jsem[d:d + len(j.scratch)])
                a, b, c, d = a + len(j.srcs), b + len(j.dsts), c + len(j.news), d + len(j.scratch)

        if not grid:
            run("start")
            body(*ins, *o_refs, *scr)
            run("finish")
            return
        first = functools.reduce(jnp.logical_and, [pl.program_id(a) == 0 for a in range(len(grid))])
        last = functools.reduce(jnp.logical_and, [pl.program_id(a) == grid[a] - 1 for a in range(len(grid))])
        pl.when(first)(lambda: run("start"))
        body(*ins, *o_refs, *scr)
        pl.when(last)(lambda: run("finish"))

    call = pl.pallas_call(
        wrapped, name=name,
        out_shape=outs + [jax.ShapeDtypeStruct(a.shape, a.dtype) for a in dsts] + news,
        in_specs=list(in_specs) + [ANY] * (len(srcs) + len(dsts)),
        out_specs=ospecs + [ANY] * (len(dsts) + len(news)),
        grid=grid, scratch_shapes=list(scratch_shapes) + sems,
        input_output_aliases={n_in + len(srcs) + i: n_out + i for i in range(len(dsts))},
        compiler_params=_cp(("arbitrary",) * len(grid) if grid else None))

    def run_call(*args):
        res = call(*args, *srcs, *dsts)
        b, c = n_out, n_out + len(dsts)
        for j in jobs:
            j.done(res[b:b + len(j.dsts)], res[c:c + len(j.news)])
            b, c = b + len(j.dsts), c + len(j.news)
        return res[0] if single else tuple(res[:n_out])

    return run_call


def _matmul(a, b, *, tb=False, out_dtype=F32, add=None, tm, tn, tk, name, jobs=()):
    m, k = a.shape
    n = b.shape[0] if tb else b.shape[1]
    assert (b.shape[1] if tb else b.shape[0]) == k
    assert m % tm == 0 and n % tn == 0 and k % tk == 0, (name, a.shape, b.shape)
    nk = k // tk
    has_add = add is not None

    def body(*refs):
        a_ref, b_ref = refs[0], refs[1]
        add_ref = refs[2] if has_add else None
        o_ref = refs[3] if has_add else refs[2]
        av = a_ref[...].astype(BF16)
        bv = b_ref[...].astype(BF16)
        part = _dot_nt(av, bv) if tb else _dot(av, bv)

        def finish(r):
            if has_add:
                r = r + add_ref[...]
            o_ref[...] = r.astype(out_dtype)

        if nk == 1:
            finish(part)
        else:
            acc_ref = refs[-1]
            kk = pl.program_id(2)

            @pl.when(kk == 0)
            def _():
                acc_ref[...] = part

            @pl.when(kk > 0)
            def _():
                acc_ref[...] += part

            @pl.when(kk == nk - 1)
            def _():
                finish(acc_ref[...])

    in_specs = [pl.BlockSpec((tm, tk), lambda i, j, kk: (i, kk)),
                pl.BlockSpec((tn, tk), lambda i, j, kk: (j, kk)) if tb
                else pl.BlockSpec((tk, tn), lambda i, j, kk: (kk, j))]
    args = [a, b]
    if has_add:
        in_specs.append(pl.BlockSpec((tm, tn), lambda i, j, kk: (i, j)))
        args.append(add)
    return _call(
        body, jobs=jobs, name=name,
        out_shape=jax.ShapeDtypeStruct((m, n), out_dtype),
        grid=(m // tm, n // tn, nk),
        in_specs=in_specs,
        out_specs=pl.BlockSpec((tm, tn), lambda i, j, kk: (i, j)),
        scratch_shapes=[pltpu.VMEM((tm, tn), F32)] if nk > 1 else [],
        compiler_params=_cp(("parallel", "parallel", "arbitrary")),
    )(*args)


ROWS = 256


def _rmsnorm_fwd(x, g, *, name):
    t, d = x.shape

    def body(x_ref, g_ref, o_ref):
        xv = x_ref[...]
        r = lax.rsqrt(jnp.mean(xv * xv, axis=-1, keepdims=True) + EPS)
        o_ref[...] = (xv * r * g_ref[...]).astype(BF16)

    return pl.pallas_call(
        body, name=name, out_shape=jax.ShapeDtypeStruct((t, d), BF16), grid=(t // ROWS,),
        in_specs=[pl.BlockSpec((ROWS, d), lambda i: (i, 0)), pl.BlockSpec((1, d), lambda i: (0, 0))],
        out_specs=pl.BlockSpec((ROWS, d), lambda i: (i, 0)), compiler_params=_cp(("parallel",)),
    )(x, g)


def _rmsnorm_bwd(x, g, dy, dres, *, name, jobs=()):
    t, d = x.shape

    def body(x_ref, g_ref, dy_ref, dres_ref, dx_ref, dxb_ref, dg_ref):
        xv = x_ref[...]
        r = lax.rsqrt(jnp.mean(xv * xv, axis=-1, keepdims=True) + EPS)
        xh = xv * r
        dyv = dy_ref[...]
        dxh = dyv * g_ref[...]
        dx = r * (dxh - xh * jnp.mean(dxh * xh, axis=-1, keepdims=True))
        tot = dres_ref[...] + dx
        dx_ref[...] = tot
        dxb_ref[...] = tot.astype(BF16)

        @pl.when(pl.program_id(0) == 0)
        def _():
            dg_ref[...] = jnp.zeros_like(dg_ref)

        dg_ref[...] += jnp.broadcast_to(jnp.sum(dyv * xh, axis=0, keepdims=True), dg_ref.shape)

    row = pl.BlockSpec((ROWS, d), lambda i: (i, 0))
    return _call(
        body, jobs=jobs, name=name,
        out_shape=(jax.ShapeDtypeStruct((t, d), F32), jax.ShapeDtypeStruct((t, d), BF16),
                   jax.ShapeDtypeStruct((8, d), F32)),
        grid=(t // ROWS,),
        in_specs=[row, pl.BlockSpec((1, d), lambda i: (0, 0)), row, row],
        out_specs=(row, row, pl.BlockSpec((8, d), lambda i: (0, 0))),
        compiler_params=_cp(("arbitrary",)),
    )(x, g, dy, dres)


def _final(h2, pgl, pp, target, g_final, *, name):
    t, d = h2.shape

    def body(h2_ref, pgl_ref, pp_ref, tg_ref, g_ref, dh3_ref, dpgl_ref, dpp_ref, loss_ref, dg_ref):
        s = _sigmoid(pgl_ref[...])
        ppv = pp_ref[...]
        h3 = h2_ref[...] + s * ppv
        r = lax.rsqrt(jnp.mean(h3 * h3, axis=-1, keepdims=True) + EPS)
        xh = h3 * r
        gv = g_ref[...]
        err = xh * gv - tg_ref[...]
        dyv = err * (1.0 / d)
        dxh = dyv * gv
        dh3 = r * (dxh - xh * jnp.mean(dxh * xh, axis=-1, keepdims=True))
        dh3_ref[...] = dh3
        dpp_ref[...] = (dh3 * s).astype(BF16)
        dpgl_ref[...] = (dh3 * ppv * s * (1.0 - s)).astype(BF16)

        @pl.when(pl.program_id(0) == 0)
        def _():
            loss_ref[...] = jnp.zeros_like(loss_ref)
            dg_ref[...] = jnp.zeros_like(dg_ref)

        part = 0.5 * jnp.sum(jnp.mean(err * err, axis=-1, keepdims=True), axis=0, keepdims=True)
        loss_ref[...] += jnp.broadcast_to(part, loss_ref.shape)
        dg_ref[...] += jnp.broadcast_to(jnp.sum(dyv * xh, axis=0, keepdims=True), dg_ref.shape)

    row = pl.BlockSpec((ROWS, d), lambda i: (i, 0))
    return pl.pallas_call(
        body, name=name,
        out_shape=(jax.ShapeDtypeStruct((t, d), F32), jax.ShapeDtypeStruct((t, d), BF16),
                   jax.ShapeDtypeStruct((t, d), BF16), jax.ShapeDtypeStruct((8, 128), F32),
                   jax.ShapeDtypeStruct((8, d), F32)),
        grid=(t // ROWS,),
        in_specs=[row, row, row, row, pl.BlockSpec((1, d), lambda i: (0, 0))],
        out_specs=(row, row, row, pl.BlockSpec((8, 128), lambda i: (0, 0)), pl.BlockSpec((8, d), lambda i: (0, 0))),
        compiler_params=_cp(("arbitrary",)),
    )(h2, pgl, pp, target, g_final)


def _merge_fwd(proj, out_a, out_s, *, name):
    t = proj.shape[0]

    def body(ga_ref, gs_ref, a_ref, s_ref, o_ref):
        o_ref[...] = (_sigmoid(ga_ref[...]) * a_ref[...] + _sigmoid(gs_ref[...]) * s_ref[...]).astype(BF16)

    row = pl.BlockSpec((ROWS, D), lambda i: (i, 0))
    return pl.pallas_call(
        body, name=name, out_shape=jax.ShapeDtypeStruct((t, D), BF16), grid=(t // ROWS,),
        in_specs=[pl.BlockSpec((ROWS, D), lambda i: (i, O_GA // D)), pl.BlockSpec((ROWS, D), lambda i: (i, O_GS // D)),
                  row, row],
        out_specs=row, compiler_params=_cp(("parallel",)),
    )(proj, proj, out_a, out_s)


def _merge_bwd(proj, out_a, out_s, dmerged, *, name):
    t = proj.shape[0]

    def body(ga_ref, gs_ref, a_ref, s_ref, dm_ref, da_ref, ds_ref, dga_ref, dgs_ref):
        sa = _sigmoid(ga_ref[...])
        ss = _sigmoid(gs_ref[...])
        dm = dm_ref[...]
        da_ref[...] = (dm * sa).astype(BF16)
        ds_ref[...] = (dm * ss).astype(BF16)
        dga_ref[...] = (dm * a_ref[...] * sa * (1.0 - sa)).astype(BF16)
        dgs_ref[...] = (dm * s_ref[...] * ss * (1.0 - ss)).astype(BF16)

    row = pl.BlockSpec((ROWS, D), lambda i: (i, 0))
    o = jax.ShapeDtypeStruct((t, D), BF16)
    return pl.pallas_call(
        body, name=name, out_shape=(o, o, o, o), grid=(t // ROWS,),
        in_specs=[pl.BlockSpec((ROWS, D), lambda i: (i, O_GA // D)), pl.BlockSpec((ROWS, D), lambda i: (i, O_GS // D)),
                  row, row, row],
        out_specs=(row, row, row, row), compiler_params=_cp(("parallel",)),
    )(proj, proj, out_a, out_s, dmerged)


def _swiglu_fwd(f, w_gate, w_up, *, name, tn=256, jobs=()):
    t, d = f.shape
    n = w_gate.shape[1]

    def body(f_ref, wg_ref, wu_ref, g_ref, u_ref, a_ref):
        fv = f_ref[...]
        g = _dot(fv, wg_ref[...])
        u = _dot(fv, wu_ref[...])
        g_ref[...] = g
        u_ref[...] = u
        a_ref[...] = (g * _sigmoid(g) * u).astype(BF16)

    col = pl.BlockSpec((t, tn), lambda j: (0, j))
    wcol = pl.BlockSpec((d, tn), lambda j: (0, j))
    return _call(
        body, jobs=jobs, name=name,
        out_shape=(jax.ShapeDtypeStruct((t, n), F32), jax.ShapeDtypeStruct((t, n), F32),
                   jax.ShapeDtypeStruct((t, n), BF16)),
        grid=(n // tn,),
        in_specs=[pl.BlockSpec((t, d), lambda j: (0, 0)), wcol, wcol],
        out_specs=(col, col, col), compiler_params=_cp(("parallel",)),
    )(f, w_gate, w_up)


def _swiglu_bwd(gate, up, dact, *, name, tc=1408, jobs=()):
    t, n = gate.shape

    def body(g_ref, u_ref, da_ref, dg_ref, du_ref):
        g = g_ref[...]
        s = _sigmoid(g)
        da = da_ref[...]
        du_ref[...] = (da * g * s).astype(BF16)
        dg_ref[...] = (da * u_ref[...] * s * (1.0 + g * (1.0 - s))).astype(BF16)

    blk = pl.BlockSpec((ROWS, tc), lambda i, j: (i, j))
    o = jax.ShapeDtypeStruct((t, n), BF16)
    return _call(
        body, jobs=jobs, name=name, out_shape=(o, o), grid=(t // ROWS, n // tc),
        in_specs=[blk, blk, blk], out_specs=(blk, blk), compiler_params=_cp(("parallel", "parallel")),
    )(gate, up, dact)


def _gated_norm_fwd(y_pre, proj, g_ssd, *, name):
    t = y_pre.shape[0]

    def body(y_ref, z_ref, g_ref, o_ref):
        z = z_ref[...]
        v = y_ref[...] * z * _sigmoid(z)
        r = lax.rsqrt(jnp.mean(v * v, axis=-1, keepdims=True) + SSM_EPS)
        o_ref[...] = (v * r * g_ref[...]).astype(BF16)

    row = pl.BlockSpec((ROWS, DI), lambda i: (i, 0))
    return pl.pallas_call(
        body, name=name, out_shape=jax.ShapeDtypeStruct((t, DI), BF16), grid=(t // ROWS,),
        in_specs=[row, pl.BlockSpec((ROWS, DI), lambda i: (i, O_Z // DI)), pl.BlockSpec((1, DI), lambda i: (0, 0))],
        out_specs=row, compiler_params=_cp(("parallel",)),
    )(y_pre, proj, g_ssd)


def _gated_norm_bwd(y_pre, proj, g_ssd, dyn, *, name, jobs=()):
    t = y_pre.shape[0]

    def body(y_ref, z_ref, g_ref, dyn_ref, dy_ref, dz_ref, dg_ref):
        z = z_ref[...]
        s = _sigmoid(z)
        sz = z * s
        yv = y_ref[...]
        v = yv * sz
        r = lax.rsqrt(jnp.mean(v * v, axis=-1, keepdims=True) + SSM_EPS)
        vh = v * r
        dn = dyn_ref[...]
        dvh = dn * g_ref[...]
        dv = r * (dvh - vh * jnp.mean(dvh * vh, axis=-1, keepdims=True))
        dy_ref[...] = dv * sz
        dz_ref[...] = (dv * yv * s * (1.0 + z * (1.0 - s))).astype(BF16)

        @pl.when(pl.program_id(0) == 0)
        def _():
            dg_ref[...] = jnp.zeros_like(dg_ref)

        dg_ref[...] += jnp.broadcast_to(jnp.sum(dn * vh, axis=0, keepdims=True), dg_ref.shape)

    row = pl.BlockSpec((ROWS, DI), lambda i: (i, 0))
    return _call(
        body, jobs=jobs, name=name,
        out_shape=(jax.ShapeDtypeStruct((t, DI), F32), jax.ShapeDtypeStruct((t, DI), BF16),
                   jax.ShapeDtypeStruct((8, DI), F32)),
        grid=(t // ROWS,),
        in_specs=[row, pl.BlockSpec((ROWS, DI), lambda i: (i, O_Z // DI)), pl.BlockSpec((1, DI), lambda i: (0, 0)), row],
        out_specs=(row, row, pl.BlockSpec((8, DI), lambda i: (0, 0))),
        compiler_params=_cp(("arbitrary",)),
    )(y_pre, proj, g_ssd, dyn)


CONV_TC = 512


def _shift_down(x, s, row):
    if s == 0:
        return x
    return jnp.where(row >= s, pltpu.roll(x, s, 0), 0.0)


def _shift_up(x, s, row, t):
    if s == 0:
        return x
    return jnp.where(row < t - s, pltpu.roll(x, t - s, 0), 0.0)


def _conv_fwd(proj, conv_w, conv_b, *, name):
    t = proj.shape[0]

    def body(x_ref, w_ref, b_ref, o_ref):
        x = x_ref[...]
        row = lax.broadcasted_iota(jnp.int32, x.shape, 0)
        pre = jnp.broadcast_to(b_ref[...], x.shape)
        for k in range(CW):
            pre = pre + w_ref[k:k + 1, :] * _shift_down(x, CW - 1 - k, row)
        o_ref[...] = pre * _sigmoid(pre)

    return pl.pallas_call(
        body, name=name, out_shape=jax.ShapeDtypeStruct((t, CONV), F32), grid=(CONV // CONV_TC,),
        in_specs=[pl.BlockSpec((t, CONV_TC), lambda j: (0, O_XBC // CONV_TC + j)),
                  pl.BlockSpec((CW, CONV_TC), lambda j: (0, j)), pl.BlockSpec((1, CONV_TC), lambda j: (0, j))],
        out_specs=pl.BlockSpec((t, CONV_TC), lambda j: (0, j)), compiler_params=_cp(("parallel",)),
    )(proj, conv_w, conv_b)


def _conv_bwd(proj, conv_w, conv_b, dact, *, name, jobs=()):
    t = proj.shape[0]

    def body(x_ref, w_ref, b_ref, da_ref, dx_ref, dw_ref, db_ref):
        x = x_ref[...]
        row = lax.broadcasted_iota(jnp.int32, x.shape, 0)
        xs = [_shift_down(x, CW - 1 - k, row) for k in range(CW)]
        pre = jnp.broadcast_to(b_ref[...], x.shape)
        for k in range(CW):
            pre = pre + w_ref[k:k + 1, :] * xs[k]
        s = _sigmoid(pre)
        dpre = da_ref[...] * s * (1.0 + pre * (1.0 - s))
        dx = jnp.zeros_like(x)
        row8 = lax.broadcasted_iota(jnp.int32, dw_ref.shape, 0)
        dw = jnp.zeros(dw_ref.shape, F32)
        for k in range(CW):
            dx = dx + w_ref[k:k + 1, :] * _shift_up(dpre, CW - 1 - k, row, t)
            dw = dw + jnp.where(row8 == k, jnp.sum(dpre * xs[k], axis=0, keepdims=True), 0.0)
        dx_ref[...] = dx.astype(BF16)
        dw_ref[...] = dw
        db_ref[...] = jnp.broadcast_to(jnp.sum(dpre, axis=0, keepdims=True), db_ref.shape)

    col8 = pl.BlockSpec((8, CONV_TC), lambda j: (0, j))
    return _call(
        body, jobs=jobs, name=name,
        out_shape=(jax.ShapeDtypeStruct((t, CONV), BF16), jax.ShapeDtypeStruct((8, CONV), F32),
                   jax.ShapeDtypeStruct((8, CONV), F32)),
        grid=(CONV // CONV_TC,),
        in_specs=[pl.BlockSpec((t, CONV_TC), lambda j: (0, O_XBC // CONV_TC + j)),
                  pl.BlockSpec((CW, CONV_TC), lambda j: (0, j)), pl.BlockSpec((1, CONV_TC), lambda j: (0, j)),
                  pl.BlockSpec((t, CONV_TC), lambda j: (0, j))],
        out_specs=(pl.BlockSpec((t, CONV_TC), lambda j: (0, j)), col8, col8),
        compiler_params=_cp(("parallel",)),
    )(proj, conv_w, conv_b, dact)


def _rope_tables(positions, t):
    half = HD // 2
    inv_freq = ROPE_THETA ** (-jnp.arange(half, dtype=F32) * 2.0 / HD)
    ang = positions.reshape(t).astype(F32)[:, None] * inv_freq
    cos, sin = jnp.cos(ang), jnp.sin(ang)
    return jnp.concatenate([cos] * 4, axis=1), jnp.concatenate([-sin, sin] * 2, axis=1)


def _lane_consts():
    lane = lax.broadcasted_iota(jnp.int32, (L, 128), 1)
    return lane, (lane % HD) < (HD // 2), lane < HD


def _rope(tv, cos, sin, lo):
    return tv * cos + jnp.where(lo, pltpu.roll(tv, 128 - HD // 2, 1), pltpu.roll(tv, HD // 2, 1)) * sin


def _rope_t(dv, cos, sin, lo):
    ds = dv * sin
    return dv * cos + jnp.where(lo, pltpu.roll(ds, 128 - HD // 2, 1), pltpu.roll(ds, HD // 2, 1))


def _placed(chunk, g, half0):
    own = jnp.where(half0 if g % 2 == 0 else jnp.logical_not(half0), chunk, 0.0)
    other = pltpu.roll(own, HD, 1)
    return (own, other) if g % 2 == 0 else (other, own)


def _unplace(acc, hf, g, half0):
    v = jnp.where(half0 if hf == 0 else jnp.logical_not(half0), acc, 0.0)
    return v if hf == g % 2 else pltpu.roll(v, HD, 1)


def _attn_fwd(proj, cos, sin, sinks, *, name, jobs=()):
    t = proj.shape[0]
    nb = t // L
    scale = HD ** -0.5

    def body(sink_ref, q_ref, kc_ref, kp_ref, vc_ref, vp_ref, cc_ref, sc_ref, cp_ref, sp_ref, o_ref, lse_ref):
        i = pl.program_id(0)
        lane, lo, half0 = _lane_consts()
        cos_c, sin_c, cos_p, sin_p = cc_ref[...], sc_ref[...], cp_ref[...], sp_ref[...]
        row = lax.broadcasted_iota(jnp.int32, (L, L), 0)
        col = lax.broadcasted_iota(jnp.int32, (L, L), 1)
        m_cur = col <= row
        m_prev = jnp.logical_and(col > row, i > 0)
        kc = [_rope(kc_ref[:, 128 * m:128 * (m + 1)], cos_c, sin_c, lo) for m in range(2)]
        kp = [_rope(kp_ref[:, 128 * m:128 * (m + 1)], cos_p, sin_p, lo) for m in range(2)]
        lse_acc = jnp.zeros((L, 128), F32)
        outs = [jnp.zeros((L, 128), F32) for _ in range(QD // 128)]
        qs = [(_rope(q_ref[:, 128 * ch:128 * (ch + 1)], cos_c, sin_c, lo) * scale).astype(BF16) for ch in range(QD // 128)]
        for g in range(NKV):
            kcv = [v.astype(BF16) for v in _placed(kc[g // 2], g, half0)]
            kpv = [v.astype(BF16) for v in _placed(kp[g // 2], g, half0)]
            vcv = [v.astype(BF16) for v in _placed(vc_ref[:, 128 * (g // 2):128 * (g // 2 + 1)], g, half0)]
            vpv = [v.astype(BF16) for v in _placed(vp_ref[:, 128 * (g // 2):128 * (g // 2 + 1)], g, half0)]
            for r in range(NQH // NKV):
                h = g * (NQH // NKV) + r
                ch, hf = h // 2, h % 2
                s_c = jnp.where(m_cur, _dot_nt(qs[ch], kcv[hf]), NEG)
                s_p = jnp.where(m_prev, _dot_nt(qs[ch], kpv[hf]), NEG)
                sink = sink_ref[0, h]
                mx = jnp.maximum(jnp.maximum(jnp.max(s_c, axis=-1, keepdims=True), jnp.max(s_p, axis=-1, keepdims=True)), sink)
                e_c = jnp.exp(s_c - mx)
                e_p = jnp.exp(s_p - mx)
                den = jnp.sum(e_c, axis=-1, keepdims=True) + jnp.sum(e_p, axis=-1, keepdims=True) + jnp.exp(sink - mx)
                inv = 1.0 / den
                outs[ch] = outs[ch] + _dot((e_c * inv).astype(BF16), vcv[hf]) + _dot((e_p * inv).astype(BF16), vpv[hf])
                lse_acc = jnp.where(lane == h, mx + jnp.log(den), lse_acc)
        for ch in range(QD // 128):
            o_ref[:, 128 * ch:128 * (ch + 1)] = outs[ch].astype(BF16)
        lse_ref[...] = lse_acc

    prev = lambda i: jnp.maximum(i - 1, 0)
    tab_c = pl.BlockSpec((L, 128), lambda i: (i, 0))
    tab_p = pl.BlockSpec((L, 128), lambda i: (prev(i), 0))
    return _call(
        body, jobs=jobs, name=name,
        out_shape=(jax.ShapeDtypeStruct((t, QD), BF16), jax.ShapeDtypeStruct((t, 128), F32)),
        grid=(nb,),
        in_specs=[pl.BlockSpec(memory_space=pltpu.SMEM),
                  pl.BlockSpec((L, QD), lambda i: (i, O_Q // QD)),
                  pl.BlockSpec((L, KVD), lambda i: (i, O_K // KVD)), pl.BlockSpec((L, KVD), lambda i: (prev(i), O_K // KVD)),
                  pl.BlockSpec((L, KVD), lambda i: (i, O_V // KVD)), pl.BlockSpec((L, KVD), lambda i: (prev(i), O_V // KVD)),
                  tab_c, tab_c, tab_p, tab_p],
        out_specs=(pl.BlockSpec((L, QD), lambda i: (i, 0)), pl.BlockSpec((L, 128), lambda i: (i, 0))),
        compiler_params=_cp(("parallel",)),
    )(sinks, proj, proj, proj, proj, proj, cos, sin, cos, sin)


def _attn_bwd(proj, cos, sin, sinks, attn, lse, dattn, *, name, jobs=()):
    t = proj.shape[0]
    nb = t // L
    scale = HD ** -0.5

    def body(sink_ref, qi_ref, qn_ref, kc_ref, kp_ref, vc_ref, vp_ref, doi_ref, don_ref, oi_ref, on_ref,
             lsei_ref, lsen_ref, cc_ref, sc_ref, cp_ref, sp_ref, cn_ref, sn_ref, dq_ref, dk_ref, dv_ref, dsk_ref):
        i = pl.program_id(0)
        lane, lo, half0 = _lane_consts()
        half1 = jnp.logical_not(half0)
        cos_c, sin_c = cc_ref[...], sc_ref[...]
        row = lax.broadcasted_iota(jnp.int32, (L, L), 0)
        col = lax.broadcasted_iota(jnp.int32, (L, L), 1)
        m_cur = col <= row
        m_prev = jnp.logical_and(col > row, i > 0)
        m_next = jnp.logical_and(col > row, i < nb - 1)
        kc = [_rope(kc_ref[:, 128 * m:128 * (m + 1)], cos_c, sin_c, lo) for m in range(2)]
        kp = [_rope(kp_ref[:, 128 * m:128 * (m + 1)], cp_ref[...], sp_ref[...], lo) for m in range(2)]
        lse_i, lse_n = lsei_ref[...], lsen_ref[...]
        dk_acc = [jnp.zeros((L, 128), F32) for _ in range(2)]
        dv_acc = [jnp.zeros((L, 128), F32) for _ in range(2)]
        dsk_acc = jnp.zeros((1, 128), F32)
        lane1 = lax.broadcasted_iota(jnp.int32, (1, 128), 1)
        place = lambda chunk, g: [v.astype(BF16) for v in _placed(chunk, g, half0)]
        kcs = [place(kc[g // 2], g) for g in range(NKV)]
        kps = [place(kp[g // 2], g) for g in range(NKV)]
        vcs = [place(vc_ref[:, 128 * (g // 2):128 * (g // 2 + 1)], g) for g in range(NKV)]
        vps = [place(vp_ref[:, 128 * (g // 2):128 * (g // 2 + 1)], g) for g in range(NKV)]
        for ch in range(QD // 128):
            sl = slice(128 * ch, 128 * (ch + 1))
            q_i = (_rope(qi_ref[:, sl], cos_c, sin_c, lo) * scale).astype(BF16)
            q_n = (_rope(qn_ref[:, sl], cn_ref[...], sn_ref[...], lo) * scale).astype(BF16)
            do_i, do_n = doi_ref[:, sl], don_ref[:, sl]
            do_ib, do_nb = do_i.astype(BF16), do_n.astype(BF16)
            od_i = do_i * oi_ref[:, sl].astype(F32)
            od_n = do_n * on_ref[:, sl].astype(F32)
            dq_ch = jnp.zeros((L, 128), F32)
            for hf in range(2):
                h = 2 * ch + hf
                g = h // (NQH // NKV)
                hm = half0 if hf == 0 else half1
                kcv, kpv, vcv, vpv = kcs[g][hf], kps[g][hf], vcs[g][hf], vps[g][hf]
                dl_i = jnp.sum(jnp.where(hm, od_i, 0.0), axis=-1, keepdims=True)
                dl_n = jnp.sum(jnp.where(hm, od_n, 0.0), axis=-1, keepdims=True)
                ls_i = jnp.sum(jnp.where(lane == h, lse_i, 0.0), axis=-1, keepdims=True)
                ls_n = jnp.sum(jnp.where(lane == h, lse_n, 0.0), axis=-1, keepdims=True)
                p_c = jnp.where(m_cur, jnp.exp(_dot_nt(q_i, kcv) - ls_i), 0.0)
                p_p = jnp.where(m_prev, jnp.exp(_dot_nt(q_i, kpv) - ls_i), 0.0)
                ds_c = (p_c * (_dot_nt(do_ib, vcv) - dl_i)).astype(BF16)
                ds_p = (p_p * (_dot_nt(do_ib, vpv) - dl_i)).astype(BF16)
                dq_ch = dq_ch + jnp.where(hm, (_dot(ds_c, kcv) + _dot(ds_p, kpv)) * scale, 0.0)
                sink = sink_ref[0, h]
                dsk = -jnp.sum(jnp.exp(sink - ls_i) * dl_i, axis=0, keepdims=True)
                dsk_acc = dsk_acc + jnp.where(lane1 == h, dsk, 0.0)
                p_n = jnp.where(m_next, jnp.exp(_dot_nt(q_n, kcv) - ls_n), 0.0)
                ds_n = (p_n * (_dot_nt(do_nb, vcv) - dl_n)).astype(BF16)
                dv_h = _dot_tn(p_c.astype(BF16), do_ib) + _dot_tn(p_n.astype(BF16), do_nb)
                dk_h = _dot_tn(ds_c, q_i) + _dot_tn(ds_n, q_n)
                dv_acc[g // 2] = dv_acc[g // 2] + _unplace(dv_h, hf, g, half0)
                dk_acc[g // 2] = dk_acc[g // 2] + _unplace(dk_h, hf, g, half0)
            dq_ref[:, sl] = _rope_t(dq_ch, cos_c, sin_c, lo).astype(BF16)
        for m in range(2):
            dk_ref[:, 128 * m:128 * (m + 1)] = _rope_t(dk_acc[m], cos_c, sin_c, lo).astype(BF16)
            dv_ref[:, 128 * m:128 * (m + 1)] = dv_acc[m].astype(BF16)

        @pl.when(i == 0)
        def _():
            dsk_ref[...] = jnp.zeros_like(dsk_ref)

        dsk_ref[...] += jnp.broadcast_to(dsk_acc, dsk_ref.shape)

    prev = lambda i: jnp.maximum(i - 1, 0)
    nxt = lambda i: jnp.minimum(i + 1, nb - 1)
    cur_q = pl.BlockSpec((L, QD), lambda i: (i, 0))
    nxt_q = pl.BlockSpec((L, QD), lambda i: (nxt(i), 0))
    tab = lambda f: pl.BlockSpec((L, 128), lambda i: (f(i), 0))
    ident = lambda i: i
    kv_o = jax.ShapeDtypeStruct((t, KVD), BF16)
    return _call(
        body, jobs=jobs, name=name,
        out_shape=(jax.ShapeDtypeStruct((t, QD), BF16), kv_o, kv_o, jax.ShapeDtypeStruct((8, 128), F32)),
        grid=(nb,),
        in_specs=[pl.BlockSpec(memory_space=pltpu.SMEM),
                  pl.BlockSpec((L, QD), lambda i: (i, O_Q // QD)), pl.BlockSpec((L, QD), lambda i: (nxt(i), O_Q // QD)),
                  pl.BlockSpec((L, KVD), lambda i: (i, O_K // KVD)), pl.BlockSpec((L, KVD), lambda i: (prev(i), O_K // KVD)),
                  pl.BlockSpec((L, KVD), lambda i: (i, O_V // KVD)), pl.BlockSpec((L, KVD), lambda i: (prev(i), O_V // KVD)),
                  cur_q, nxt_q, cur_q, nxt_q, tab(ident), tab(nxt),
                  tab(ident), tab(ident), tab(prev), tab(prev), tab(nxt), tab(nxt)],
        out_specs=(cur_q, pl.BlockSpec((L, KVD), lambda i: (i, 0)), pl.BlockSpec((L, KVD), lambda i: (i, 0)),
                   pl.BlockSpec((8, 128), lambda i: (0, 0))),
        compiler_params=_cp(("arbitrary",)),
    )(sinks, proj, proj, proj, proj, proj, proj, dattn, dattn, attn, attn, lse, lse, cos, sin, cos, sin, cos, sin)


PAIRS = NH // NG // 2


def _softplus(x):
    return jnp.maximum(x, 0.0) + jnp.log(1.0 + jnp.exp(-jnp.abs(x)))


def _ssd_chunk(g, xps, dtr, bm, cm, sps, dtb, alog, dsk):
    lane = lax.broadcasted_iota(jnp.int32, (L, 128), 1)
    lane1 = lax.broadcasted_iota(jnp.int32, (1, 128), 1)
    row = lax.broadcasted_iota(jnp.int32, (L, L), 0)
    col = lax.broadcasted_iota(jnp.int32, (L, L), 1)
    rowc = lax.broadcasted_iota(jnp.int32, (128, 1), 0)
    tril = col <= row
    dt = _softplus(dtr + dtb)
    a = dt * (-jnp.exp(alog))
    a_cs = lax.dot_general(tril.astype(F32), a, (((1,), (0,)), ((), ())), precision=lax.Precision.HIGHEST,
                           preferred_element_type=F32)
    a_cst = a_cs.T
    a_last = jnp.sum(jnp.where(row == L - 1, a_cs, 0.0), axis=0, keepdims=True)
    cb = _dot_nt(cm.astype(BF16), bm.astype(BF16))
    ys, snew = [], []
    for q in range(PAIRS):
        xp, sp = xps[q], sps[q]
        y_pair = jnp.zeros((L, 128), F32)
        st_pair = jnp.zeros((128, NS), F32)
        keep = jnp.zeros((128, 1), F32)
        for hh in range(2):
            h = g * 2 * PAIRS + 2 * q + hh
            hm = (lane < HD) if hh == 0 else (lane >= HD)
            rm = (rowc < HD) if hh == 0 else (rowc >= HD)
            dt_h = jnp.sum(jnp.where(lane == h, dt, 0.0), axis=1, keepdims=True)
            acs_h = jnp.sum(jnp.where(lane == h, a_cs, 0.0), axis=1, keepdims=True)
            acst_h = jnp.sum(jnp.where(row == h, a_cst, 0.0), axis=0, keepdims=True)
            al_h = jnp.sum(jnp.where(lane1 == h, a_last, 0.0), axis=1, keepdims=True)
            dsk_h = jnp.sum(jnp.where(lane1 == h, dsk, 0.0), axis=1, keepdims=True)
            decay = jnp.where(tril, jnp.exp(jnp.where(tril, acs_h - acst_h, 0.0)), 0.0)
            xh = jnp.where(hm, xp, 0.0)
            xd = (xh * dt_h).astype(BF16)
            y = _dot((cb * decay).astype(BF16), xd)
            y = y + jnp.where(hm, _dot_nt((cm * jnp.exp(acs_h)).astype(BF16), sp.astype(BF16)), 0.0)
            y_pair = y_pair + y + dsk_h * xh
            st_pair = st_pair + _dot_tn(xd, (bm * jnp.exp(al_h - acs_h)).astype(BF16))
            keep = keep + jnp.where(rm, jnp.exp(al_h), 0.0)
        ys.append(y_pair)
        snew.append(sp * keep + st_pair)
    return ys, snew


def _ssd_specs(t):
    nc = t // L
    xs = lambda f: pl.BlockSpec((L, 128 * PAIRS), lambda c, g: (f(c), g))
    bspec = lambda f: pl.BlockSpec((L, NS), lambda c, g: (f(c), DI // NS + g))
    cspec = lambda f: pl.BlockSpec((L, NS), lambda c, g: (f(c), DI // NS + NG + g))
    dts = lambda f: pl.BlockSpec((L, 128), lambda c, g: (f(c), O_DT // 128))
    par = pl.BlockSpec((1, 128), lambda c, g: (0, 0))
    st = lambda f: pl.BlockSpec((1, 1, PAIRS, 128, NS), lambda c, g: (f(c), g, 0, 0, 0))
    return nc, xs, bspec, cspec, dts, par, st


def _ssd_fwd(xbc_act, proj, dtb, alog, dsk, *, name, jobs=()):
    t = proj.shape[0]
    nc, xs, bspec, cspec, dts, par, st = _ssd_specs(t)
    ident = lambda c: c

    def body(x_ref, b_ref, c_ref, dt_ref, dtb_ref, al_ref, dsk_ref, y_ref, sin_ref, s_ref):
        c, g = pl.program_id(0), pl.program_id(1)

        @pl.when(c == 0)
        def _():
            s_ref[g] = jnp.zeros((PAIRS, 128, NS), F32)

        sps = [s_ref[g, q] for q in range(PAIRS)]
        for q in range(PAIRS):
            sin_ref[0, 0, q] = sps[q]
        xps = [x_ref[:, 128 * q:128 * (q + 1)] for q in range(PAIRS)]
        ys, snew = _ssd_chunk(g, xps, dt_ref[...], b_ref[...], c_ref[...], sps, dtb_ref[...], al_ref[...], dsk_ref[...])
        for q in range(PAIRS):
            y_ref[:, 128 * q:128 * (q + 1)] = ys[q]
            s_ref[g, q] = snew[q]

    return _call(
        body, jobs=jobs, name=name,
        out_shape=(jax.ShapeDtypeStruct((t, DI), F32), jax.ShapeDtypeStruct((nc, NG, PAIRS, 128, NS), F32)),
        grid=(nc, NG),
        in_specs=[xs(ident), bspec(ident), cspec(ident), dts(ident), par, par, par],
        out_specs=(pl.BlockSpec((L, 128 * PAIRS), lambda c, g: (c, g)), st(ident)),
        scratch_shapes=[pltpu.VMEM((NG, PAIRS, 128, NS), F32)],
        compiler_params=_cp(("arbitrary", "arbitrary")),
    )(xbc_act, xbc_act, xbc_act, proj, dtb, alog, dsk)


def _ssd_bwd(xbc_act, proj, dtb, alog, dsk, states, dy, *, name, jobs=()):
    t = proj.shape[0]
    nc, xs, bspec, cspec, dts, par, st = _ssd_specs(t)
    rev = lambda c: nc - 1 - c

    def body(x_ref, b_ref, c_ref, dt_ref, dtb_ref, al_ref, dsk_ref, sin_ref, dy_ref,
             dx_ref, db_ref, dc_ref, ddt_ref, ddtb_ref, dal_ref, ddsk_ref, ds_ref):
        c, g = pl.program_id(0), pl.program_id(1)

        @pl.when(c == 0)
        def _():
            ds_ref[g] = jnp.zeros((PAIRS, 128, NS), F32)

        @pl.when(jnp.logical_and(c == 0, g == 0))
        def _():
            ddtb_ref[...] = jnp.zeros_like(ddtb_ref)
            dal_ref[...] = jnp.zeros_like(dal_ref)
            ddsk_ref[...] = jnp.zeros_like(ddsk_ref)

        @pl.when(g == 0)
        def _():
            ddt_ref[...] = jnp.zeros_like(ddt_ref)

        sps = [sin_ref[0, 0, q] for q in range(PAIRS)]
        xps = [x_ref[:, 128 * q:128 * (q + 1)] for q in range(PAIRS)]
        _, vjp = jax.vjp(functools.partial(_ssd_chunk, g), xps, dt_ref[...], b_ref[...], c_ref[...], sps,
                         dtb_ref[...], al_ref[...], dsk_ref[...])
        dys = [dy_ref[:, 128 * q:128 * (q + 1)] for q in range(PAIRS)]
        dss = [ds_ref[g, q] for q in range(PAIRS)]
        dxps, ddt, db, dc, dsps, ddtb, dal, ddsk = vjp((dys, dss))
        for q in range(PAIRS):
            dx_ref[:, 128 * q:128 * (q + 1)] = dxps[q]
            ds_ref[g, q] = dsps[q]
        db_ref[...] = db
        dc_ref[...] = dc
        ddt_ref[...] += ddt
        ddtb_ref[...] += jnp.broadcast_to(ddtb, ddtb_ref.shape)
        dal_ref[...] += jnp.broadcast_to(dal, dal_ref.shape)
        ddsk_ref[...] += jnp.broadcast_to(ddsk, ddsk_ref.shape)

    acc = pl.BlockSpec((8, 128), lambda c, g: (0, 0))
    o8 = jax.ShapeDtypeStruct((8, 128), F32)
    return _call(
        body, jobs=jobs, name=name,
        out_shape=(jax.ShapeDtypeStruct((t, DI), F32), jax.ShapeDtypeStruct((t, NG * NS), F32),
                   jax.ShapeDtypeStruct((t, NG * NS), F32), jax.ShapeDtypeStruct((t, 128), F32), o8, o8, o8),
        grid=(nc, NG),
        in_specs=[xs(rev), bspec(rev), cspec(rev), dts(rev), par, par, par, st(rev),
                  pl.BlockSpec((L, 128 * PAIRS), lambda c, g: (rev(c), g))],
        out_specs=(pl.BlockSpec((L, 128 * PAIRS), lambda c, g: (rev(c), g)),
                   pl.BlockSpec((L, NS), lambda c, g: (rev(c), g)), pl.BlockSpec((L, NS), lambda c, g: (rev(c), g)),
                   pl.BlockSpec((L, 128), lambda c, g: (rev(c), 0)), acc, acc, acc),
        scratch_shapes=[pltpu.VMEM((NG, PAIRS, 128, NS), F32)],
        compiler_params=_cp(("arbitrary", "arbitrary")),
    )(xbc_act, xbc_act, xbc_act, proj, dtb, alog, dsk, states, dy)


def _pad_lanes(v, n=128):
    return jnp.pad(v, ((0, 0), (0, n - v.shape[1])))


class _LocalPlan:
    def __init__(self, big):
        self.big, self.grad = big, {}

    def w(self, n):
        return self.big[n]

    def g(self, n, a):
        self.grad[n] = a

    def jobs(self, tag):
        return ()


def _local_step(x, p, positions, target, small, plan):
    t = x.shape[0]
    cos, sin = _rope_tables(positions, t)
    dtb, alog, dsk = _pad_lanes(small["dt_bias"]), _pad_lanes(small["a_log"]), _pad_lanes(small["d_skip"])
    w, jobs = plan.w, plan.jobs

    def mm(a, b, *, name, tm=t, **kw):
        return _matmul(a, b, tm=tm, tn=512, name=name, jobs=jobs(name), **kw)

    def dw(wname, at, dy, *, name, tm):
        plan.g(wname, _matmul(at, dy, out_dtype=BF16, tm=tm, tn=512, tk=t, name=name, jobs=jobs(name)))

    u = _rmsnorm_fwd(x, small["g_mix"], name="norm_mix")
    proj = mm(u, w("w_in"), tk=D, name="mm_in")
    attn, lse = _attn_fwd(proj, cos, sin, small["sinks"], name="attn_fwd", jobs=jobs("attn_fwd"))
    out_a = mm(attn, w("w_attn_br"), tk=QD, name="mm_attn_br")
    xbc_act = _conv_fwd(proj, small["conv_w"], small["conv_b"], name="conv_fwd")
    y_pre, states = _ssd_fwd(xbc_act, proj, dtb, alog, dsk, name="ssd_fwd", jobs=jobs("ssd_fwd"))
    yn = _gated_norm_fwd(y_pre, proj, small["g_ssd"], name="gated_norm_fwd")
    out_s = mm(yn, w("w_ssd_br"), tk=DI, name="mm_ssd_br")
    merged = _merge_fwd(proj, out_a, out_s, name="merge_fwd")
    h1 = mm(merged, w("w_o"), add=x, tk=D, name="mm_o")
    f = _rmsnorm_fwd(h1, small["g_ffn"], name="norm_ffn")
    gate, up, act = _swiglu_fwd(f, w("w_gate"), w("w_up"), name="swiglu_fwd", jobs=jobs("swiglu_fwd"))
    h2 = mm(act, w("w_down"), add=h1, tm=t // 2, tk=FFN // 2, name="mm_down")
    e = _rmsnorm_fwd(h2, small["g_ple"], name="norm_ple")
    pgl = mm(e, w("w_ple_gate"), tk=D, name="mm_ple_gate")
    pb = p.astype(BF16)
    pp = mm(pb, w("w_ple_proj"), tk=PLE, name="mm_ple_proj")
    dh3, dpgl, dpp, loss, dg_final = _final(h2, pgl, pp, target, small["g_final"].reshape(1, D), name="final")

    dw("w_ple_proj", pb.T, dpp, tm=PLE, name="mm_d_ple_proj")
    dw("w_ple_gate", e.T, dpgl, tm=D, name="mm_d_ple_gate")
    de = mm(dpgl, w("w_ple_gate"), tb=True, tk=D, name="mm_de")
    dh2, dh2b, dg_ple = _rmsnorm_bwd(h2, small["g_ple"], de, dh3, name="norm_ple_bwd", jobs=jobs("norm_ple_bwd"))
    dw("w_down", act.T, dh2b, tm=FFN // 2, name="mm_d_down")
    dact = mm(dh2b, w("w_down"), tb=True, tk=D, name="mm_dact")
    dgate, dup = _swiglu_bwd(gate, up, dact, name="swiglu_bwd", jobs=jobs("swiglu_bwd"))
    ft = f.T
    dw("w_gate", ft, dgate, tm=D, name="mm_d_gate")
    dw("w_up", ft, dup, tm=D, name="mm_d_up")
    df = mm(dgate, w("w_gate"), tb=True, tm=t // 2, tk=FFN // 2, name="mm_df_gate")
    df = mm(dup, w("w_up"), tb=True, add=df, tm=t // 2, tk=FFN // 2, name="mm_df_up")
    dh1, dh1b, dg_ffn = _rmsnorm_bwd(h1, small["g_ffn"], df, dh2, name="norm_ffn_bwd", jobs=jobs("norm_ffn_bwd"))
    dw("w_o", merged.T, dh1b, tm=D, name="mm_d_o")
    dmerged = mm(dh1b, w("w_o"), tb=True, tk=D, name="mm_dmerged")
    dout_a, dout_s, dga, dgs = _merge_bwd(proj, out_a, out_s, dmerged, name="merge_bwd")
    dw("w_attn_br", attn.T, dout_a, tm=QD, name="mm_d_attn_br")
    dw("w_ssd_br", yn.T, dout_s, tm=DI, name="mm_d_ssd_br")
    dattn = mm(dout_a, w("w_attn_br"), tb=True, tk=D, name="mm_dattn")
    dyn = mm(dout_s, w("w_ssd_br"), tb=True, tk=D, name="mm_dyn")
    dq, dk, dv, dsinks = _attn_bwd(proj, cos, sin, small["sinks"], attn, lse, dattn, name="attn_bwd", jobs=jobs("attn_bwd"))
    dy_pre, dz, dg_ssd = _gated_norm_bwd(y_pre, proj, small["g_ssd"], dyn, name="gated_norm_bwd",
                                         jobs=jobs("gated_norm_bwd"))
    dxs, db, dc, ddt, ddtb, dalog, ddsk = _ssd_bwd(xbc_act, proj, dtb, alog, dsk, states, dy_pre, name="ssd_bwd",
                                                   jobs=jobs("ssd_bwd"))
    dxbc_act = jnp.concatenate([dxs, db, dc], axis=1)
    dxbc, dconv_w, dconv_b = _conv_bwd(proj, small["conv_w"], small["conv_b"], dxbc_act, name="conv_bwd",
                                       jobs=jobs("conv_bwd"))
    ddt_b = jnp.pad(ddt.astype(BF16), ((0, 0), (0, DT_PAD - 128)))
    dproj = jnp.concatenate([dz, dga, dgs, dxbc, dq, dk, dv, ddt_b], axis=1)
    dw("w_in", u.T, dproj, tm=D, name="mm_d_in")
    du = mm(dproj, w("w_in"), tb=True, tm=t // 2, tk=NP // 4, name="mm_du")
    grad_x, _, dg_mix = _rmsnorm_bwd(x, small["g_mix"], du, dh1, name="norm_mix_bwd", jobs=jobs("norm_mix_bwd"))

    gs = {
        "g_mix": dg_mix[:1], "conv_w": dconv_w[:CW], "conv_b": dconv_b[:1], "dt_bias": ddtb[:1, :NH],
        "a_log": dalog[:1, :NH], "d_skip": ddsk[:1, :NH], "g_ssd": dg_ssd[:1], "sinks": dsinks[:1, :NQH],
        "g_ffn": dg_ffn[:1], "g_ple": dg_ple[:1], "g_final": dg_final[0],
    }
    return loss, grad_x, gs


def _to_kernel_cols(w):
    seg = lambda o, n: w[:, o:o + n]
    return jnp.concatenate([seg(R_Z, DI), seg(R_GA, D), seg(R_GS, D), seg(R_XBC, CONV), seg(R_Q, QD), seg(R_K, KVD),
                            seg(R_V, KVD), seg(R_DT, NH), jnp.zeros((w.shape[0], DT_PAD - NH), w.dtype)], axis=1)


def _from_kernel_cols(g):
    seg = lambda o, n: g[:, o:o + n]
    return jnp.concatenate([seg(O_Q, QD), seg(O_K, KVD), seg(O_V, KVD), seg(O_Z, DI), seg(O_XBC, CONV), seg(O_DT, NH),
                            seg(O_GA, D), seg(O_GS, D)], axis=1)


RELS = ((0, 1), (1, 0), (1, 1))
MATS = {
    n: (n, kind, 1, r, c, tp, tf) for n, kind, r, c, tp, tf in (
        ("w_in", "stk", 2048, 2696, 256, 256),
        ("w_attn_br", "col", 1024, 512, 256, 256),
        ("w_ssd_br", "row", 512, 2048, 512, 256),
        ("w_o", "row", 512, 2048, 512, 256),
        ("w_gate", "col", 2048, 1408, 256, 256),
        ("w_up", "col", 2048, 1408, 256, 256),
        ("w_down", "row", 1408, 2048, 704, 704),
        ("w_ple_gate", "row", 512, 2048, 512, 256),
        ("w_ple_proj", "col", 256, 512, 128, 128),
    )}


def _pos():
    return lax.axis_index("x"), lax.axis_index("y"), lax.axis_index("c")


def _flip(v, a):
    return 1 - v if a else v


def _remote(src, dst, send, recv, dev):
    return pltpu.make_async_remote_copy(src_ref=src, dst_ref=dst, send_sem=send, recv_sem=recv, device_id=dev,
                                        device_id_type=MESH)


def _whole_shape(kind, g, r, c):
    return {"row": (g, NCHIP * r, c), "col": (g, r, NCHIP * c), "stk": (NCHIP, r, c)}[kind]


def _cols(j, c):
    return pl.ds(pl.multiple_of(j * c, 128), c)


def _whole_shard(kind, ref, j, r, c):
    if kind == "row":
        return ref.at[:, pl.ds(j * r, r), :]
    if kind == "col":
        return ref.at[:, :, _cols(j, c)]
    return ref.at[pl.ds(j, 1)]


def _whole_shard_rows(kind, ref, j, h, r, c):
    if kind == "row":
        return ref.at[:, pl.ds(j * r + h * (r // 2), r // 2), :]
    if kind == "col":
        return ref.at[:, pl.ds(h * (r // 2), r // 2), _cols(j, c)]
    return ref.at[pl.ds(j, 1), pl.ds(h * (r // 2), r // 2), :]


class _GatherJob(_Job):
    def __init__(self, names, shards, sink):
        self.mats = [MATS[n] for n in names]
        self.srcs = [shards[n] for n in names]
        self.news = [jax.ShapeDtypeStruct(_whole_shape(kind, g, r, c), BF16) for _, kind, g, r, c, _, _ in self.mats]
        n = len(names)
        self.scratch = [pltpu.SemaphoreType.DMA((6 * n,)), pltpu.SemaphoreType.DMA((6 * n,)), pltpu.SemaphoreType.DMA((n,))]
        self.names, self.sink = names, sink

    def _first(self, srcs, news, sems):
        send, recv, loc = sems
        x, y, c = _pos()
        own, first = [], []
        for w, (_, kind, g, r, cc, _, _) in enumerate(self.mats):
            own.append(pltpu.make_async_copy(srcs[w], _whole_shard(kind, news[w], 2 * x + y, r, cc), loc.at[w]))
            mine = srcs[w].at[:, pl.ds(c * (r // 2), r // 2), :]
            for k, (a, b) in enumerate(RELS):
                first.append(_remote(mine, _whole_shard_rows(kind, news[w], 2 * x + y, c, r, cc), send.at[6 * w + k],
                                     recv.at[6 * w + k], (_flip(x, a), _flip(y, b), c)))
        return own, first

    def start(self, srcs, dsts, news, sems):
        own, first = self._first(srcs, news, sems)
        for cp in own + first:
            cp.start()

    def finish(self, srcs, dsts, news, sems):
        send, recv, _ = sems
        x, y, c = _pos()
        sib = (x, y, 1 - c)
        own, first = self._first(srcs, news, sems)
        passed = []
        for w, (_, kind, g, r, cc, _, _) in enumerate(self.mats):
            for k, (a, b) in enumerate(RELS):
                land = _whole_shard_rows(kind, news[w], 2 * _flip(x, a) + _flip(y, b), c, r, cc)
                _remote(land, land, send.at[6 * w + k], recv.at[6 * w + k], sib).wait_recv()
                cp = _remote(land, land, send.at[6 * w + 3 + k], recv.at[6 * w + 3 + k], sib)
                cp.start()
                passed.append(cp)
        for w, (_, kind, g, r, cc, _, _) in enumerate(self.mats):
            for k, (a, b) in enumerate(RELS):
                land = _whole_shard_rows(kind, news[w], 2 * _flip(x, a) + _flip(y, b), 1 - c, r, cc)
                _remote(land, land, send.at[6 * w + 3 + k], recv.at[6 * w + 3 + k], sib).wait_recv()
        for cp in first + passed:
            cp.wait_send()
        for cp in own:
            cp.wait()

    def done(self, dsts, news):
        for n, a in zip(self.names, news):
            self.sink[n] = a


class _SwapJob(_Job):
    def __init__(self, build, ncopies, *, srcs=(), dsts=(), news=(), done=None):
        self.build, self.srcs, self.dsts, self.news, self._done = build, list(srcs), list(dsts), list(news), done
        self.scratch = [pltpu.SemaphoreType.DMA((ncopies,)), pltpu.SemaphoreType.DMA((ncopies,))]

    def start(self, srcs, dsts, news, sems):
        for cp in self.build(srcs, dsts, news, *sems):
            cp.start()

    def finish(self, srcs, dsts, news, sems):
        for cp in self.build(srcs, dsts, news, *sems):
            cp.wait()

    def done(self, dsts, news):
        if self._done is not None:
            self._done(dsts, news)


def _half_of_whole(kind, ref, h, r, c):
    if kind == "row":
        return ref.at[:, :, pl.ds(pl.multiple_of(h * (c // 2), 128), c // 2)]
    return ref.at[:, pl.ds(h * (r // 2), r // 2), :]


def _half_shape(kind, g, r, c):
    return {"row": (g, NCHIP * r, c // 2), "col": (g, r // 2, NCHIP * c), "stk": (NCHIP, r // 2, c)}[kind]


def _piece_shape(kind, g, r, c):
    return {"row": (g, r, c // 2), "col": (g, r // 2, c), "stk": (1, r // 2, c)}[kind]


def _piece_of_half(kind, ref, j, r, c):
    if kind == "row":
        return ref.at[:, pl.ds(j * r, r), :]
    if kind == "col":
        return ref.at[:, :, _cols(j, c)]
    return ref.at[pl.ds(j, 1)]


def _half_of_shard(kind, ref, h, r, c):
    if kind == "row":
        return ref.at[:, :, pl.ds(pl.multiple_of(h * (c // 2), 128), c // 2)]
    return ref.at[:, pl.ds(h * (r // 2), r // 2), :]


def _pair_sum(pack, core, whole, got):
    name, kind, g, r, c, tr, _ = pack
    hs = _half_shape(kind, g, r, c)
    nb = hs[1] // tr

    def body(core_ref, a_ref, b_ref, o_ref):
        o_ref[...] = (a_ref[...].astype(F32) + b_ref[...].astype(F32)).astype(BF16)

    blk = (1, tr, hs[2])
    if kind == "row":
        a_map = lambda gi, i, core_ref: (gi, i, core_ref[0])
    else:
        a_map = lambda gi, i, core_ref: (gi, core_ref[0] * nb + i, 0)
    same = lambda gi, i, core_ref: (gi, i, 0)
    return pl.pallas_call(
        body, name="pair_sum_" + name, out_shape=jax.ShapeDtypeStruct(hs, BF16),
        grid_spec=pltpu.PrefetchScalarGridSpec(
            num_scalar_prefetch=1, grid=(hs[0], nb),
            in_specs=[pl.BlockSpec(blk, a_map), pl.BlockSpec(blk, same)], out_specs=pl.BlockSpec(blk, same)),
        compiler_params=_cp(("parallel", "parallel")),
    )(core, whole, got)


def _shard_sum(pack, where, half, got):
    name, kind, g, r, c, _, tr = pack
    ps = _piece_shape(kind, g, r, c)
    nb = ps[1] // tr

    def body(where_ref, a_ref, b_ref, o_ref):
        o_ref[...] = a_ref[...].astype(F32) + ((b_ref[0].astype(F32) + b_ref[1].astype(F32)) + b_ref[2].astype(F32))

    blk = (1, tr, ps[2])
    if kind == "row":
        a_map = lambda gi, i, wr: (gi, wr[0] * nb + i, 0)
        o_map = lambda gi, i, wr: (gi, i, wr[1])
    elif kind == "col":
        a_map = lambda gi, i, wr: (gi, i, wr[0])
        o_map = lambda gi, i, wr: (gi, wr[1] * nb + i, 0)
    else:
        a_map = lambda gi, i, wr: (wr[0], i, 0)
        o_map = lambda gi, i, wr: (gi, wr[1] * nb + i, 0)
    return pl.pallas_call(
        body, name="shard_sum_" + name, out_shape=jax.ShapeDtypeStruct((g, r, c), F32),
        grid_spec=pltpu.PrefetchScalarGridSpec(
            num_scalar_prefetch=1, grid=(ps[0], nb),
            in_specs=[pl.BlockSpec(blk, a_map), pl.BlockSpec((3,) + blk, lambda gi, i, wr: (0, gi, i, 0))],
            out_specs=pl.BlockSpec(blk, o_map)),
        compiler_params=_cp(("parallel", "parallel")),
    )(where, half, got)


class _Plan:
    def __init__(self, shards, table):
        self.shards, self.table = shards, table
        self.whole, self.grad, self.got_a, self.half, self.got_b, self.sent_b, self.gshard = {}, {}, {}, {}, {}, {}, {}
        x, y, c = _pos()
        self.core = c.reshape(1).astype(jnp.int32)
        self.where = jnp.stack([2 * x + y, c]).astype(jnp.int32)
        self._w_in = None

    def w(self, n):
        if n != "w_in":
            return self.whole[n][0]
        if self._w_in is None:
            self._w_in = _to_kernel_cols(self.whole[n].transpose(1, 0, 2).reshape(D, IN_DIM))
        return self._w_in

    def g(self, n, a):
        if n == "w_in":
            a = _from_kernel_cols(a).reshape(D, NCHIP, IN_DIM // NCHIP).transpose(1, 0, 2)
        self.grad[n] = a if a.ndim == 3 else a[None]

    def jobs(self, tag):
        out = []
        for spec in self.table.get(tag, ()):
            out += getattr(self, "_" + spec[0])(*spec[1:])
        return out

    def run(self, name, jobs):
        if jobs:
            _call(lambda: None, jobs=jobs, name=name, out_shape=[], in_specs=[], out_specs=[])()

    def _gather(self, names):
        return [_GatherJob(names, self.shards, self.whole)]

    def _rs_a(self, names):
        mats = [MATS[n] for n in names]

        def build(srcs, dsts, news, send, recv):
            x, y, c = _pos()
            return [_remote(_half_of_whole(kind, srcs[i], 1 - c, r, cc), news[i], send.at[i], recv.at[i], (x, y, 1 - c))
                    for i, (_, kind, g, r, cc, _, _) in enumerate(mats)]

        def done(dsts, news):
            self.got_a.update(zip(names, news))

        return [_SwapJob(build, len(names), srcs=[self.grad[n] for n in names], done=done,
                         news=[jax.ShapeDtypeStruct(_half_shape(kind, g, r, c), BF16) for _, kind, g, r, c, _, _ in mats])]

    def _rs_b(self, names, ks=(0, 1, 2)):
        return [self._rs_b_one(n, ks) for n in names]

    def _rs_b_one(self, n, ks):
        _, kind, g, r, cc, _, _ = MATS[n]
        if n not in self.half:
            self.half[n] = _pair_sum(MATS[n], self.core, self.grad[n], self.got_a[n])

        def build(srcs, dsts, news, send, recv):
            x, y, c = _pos()
            land = (dsts or news)[0]
            cps = []
            for i, k in enumerate(ks):
                px, py = _flip(x, RELS[k][0]), _flip(y, RELS[k][1])
                cps.append(_remote(_piece_of_half(kind, srcs[0], 2 * px + py, r, cc), land.at[k], send.at[i], recv.at[i],
                                   (px, py, c)))
            return cps

        def done(dsts, news):
            self.got_b[n] = (dsts or news)[0]
            self.sent_b[n] = self.sent_b.get(n, ()) + tuple(ks)

        if n in self.got_b:
            return _SwapJob(build, len(ks), srcs=[self.half[n]], dsts=[self.got_b[n]], done=done)
        shape = jax.ShapeDtypeStruct((3,) + _piece_shape(kind, g, r, cc), BF16)
        return _SwapJob(build, len(ks), srcs=[self.half[n]], news=[shape], done=done)

    def _rs_c(self, names):
        mats = [MATS[n] for n in names]
        for n in names:
            assert sorted(self.sent_b[n]) == [0, 1, 2], (n, self.sent_b[n])
        parts = [_shard_sum(MATS[n], self.where, self.half[n], self.got_b[n]) for n in names]

        def build(srcs, dsts, news, send, recv):
            x, y, c = _pos()
            cps = []
            for i, (_, kind, g, r, cc, _, _) in enumerate(mats):
                mine = _half_of_shard(kind, dsts[i], c, r, cc)
                cps.append(_remote(mine, mine, send.at[i], recv.at[i], (x, y, 1 - c)))
            return cps

        def done(dsts, news):
            self.gshard.update(zip(names, dsts))

        return [_SwapJob(build, len(names), dsts=parts, done=done)]

    def finish(self, n):
        if n not in self.got_a:
            self.run("rs_a_" + n, self._rs_a((n,)))
        left = tuple(k for k in range(3) if k not in self.sent_b.get(n, ()))
        if left:
            self.run("rs_b_" + n, self._rs_b((n,), left))
        if n not in self.gshard:
            self.run("rs_c_" + n, self._rs_c((n,)))
        return self.gshard[n]


TABLE = {
    "gather_w_in": (("gather", ("w_in",)),),
    "mm_in": (("gather", ("w_attn_br", "w_ssd_br")),),
    "attn_fwd": (("gather", ("w_o",)),),
    "ssd_fwd": (("gather", ("w_gate",)),),
    "mm_ssd_br": (("gather", ("w_up",)),),
    "mm_o": (("gather", ("w_down",)),),
    "swiglu_fwd": (("gather", ("w_ple_gate", "w_ple_proj")),),
    "mm_de": (("rs_a", ("w_ple_proj", "w_ple_gate")),),
    "mm_d_down": (("rs_b", ("w_ple_proj", "w_ple_gate")),),
    "mm_dact": (("rs_a", ("w_down",)),),
    "swiglu_bwd": (("rs_c", ("w_ple_proj", "w_ple_gate")),),
    "mm_df_gate": (("rs_a", ("w_gate", "w_up")),),
    "mm_dmerged": (("rs_a", ("w_o",)),),
    "mm_dyn": (("rs_a", ("w_attn_br", "w_ssd_br")),),
    "attn_bwd": (("rs_b", ("w_down",)),),
    "gated_norm_bwd": (("rs_c", ("w_down",)),),
    "ssd_bwd": (("rs_b", ("w_gate", "w_up")),),
    "conv_bwd": (("rs_b", ("w_o",)),),
    "mm_d_in": (("rs_b", ("w_attn_br", "w_ssd_br")), ("rs_c", ("w_gate", "w_up"))),
    "mm_du": (("rs_a", ("w_in",)), ("rs_c", ("w_o",))),
    "norm_mix_bwd": (("rs_b", ("w_in",), (0,)), ("rs_c", ("w_attn_br", "w_ssd_br"))),
    "adamw_w_gate": (("rs_b", ("w_in",), (1,)),),
    "adamw_w_up": (("rs_b", ("w_in",), (2,)),),
    "adamw_w_down": (("rs_c", ("w_in",)),),
}


NDEV = 8


def _allreduce_small(v, *, name):
    rows = v.shape[0]

    def body(v_ref, o_ref, slots, send, recv):
        x, y, c = _pos()
        me = 4 * x + 2 * y + c
        slots[me] = v_ref[...]
        cps = []
        for k in range(1, NDEV):
            peer = (_flip(x, k & 4), _flip(y, k & 2), _flip(c, k & 1))
            cp = _remote(v_ref, slots.at[me], send.at[k - 1], recv.at[k - 1], peer)
            cp.start()
            cps.append(cp)
        for cp in cps:
            cp.wait()
        acc = slots[0]
        for s in range(1, NDEV):
            acc = acc + slots[s]
        o_ref[...] = acc

    return pl.pallas_call(
        body, name=name, out_shape=jax.ShapeDtypeStruct((rows, 128), F32),
        in_specs=[pl.BlockSpec(memory_space=pltpu.VMEM)], out_specs=pl.BlockSpec(memory_space=pltpu.VMEM),
        scratch_shapes=[pltpu.VMEM((NDEV, rows, 128), F32), pltpu.SemaphoreType.DMA((NDEV - 1,)),
                        pltpu.SemaphoreType.DMA((NDEV - 1,))],
    )(v)


def _adamw(w, g, m, v, *, name, tr=None, tc=None, jobs=()):
    r, c = w.shape
    tr = r if tr is None else tr
    c1 = 1.0 / (1.0 - B1 ** STEP)
    c2 = 1.0 / (1.0 - B2 ** STEP)

    def body(w_ref, g_ref, m_ref, v_ref, d_ref, mo_ref, vo_ref):
        gv = g_ref[...]
        mn = B1 * m_ref[...] + (1.0 - B1) * gv
        vn = B2 * v_ref[...] + (1.0 - B2) * (gv * gv)
        mo_ref[...] = mn
        vo_ref[...] = vn
        d_ref[...] = -LR * ((mn * c1) / (jnp.sqrt(vn * c2) + AEPS) + WD * w_ref[...])

    if tc is None:
        blk, grid = pl.BlockSpec((tr, c), lambda i: (i, 0)), (r // tr,)
    else:
        blk, grid = pl.BlockSpec((r, tc), lambda i: (0, i)), (c // tc,)
    o = jax.ShapeDtypeStruct((r, c), F32)
    return _call(
        body, jobs=jobs, name=name, out_shape=(o, o, o), grid=grid, in_specs=[blk] * 4, out_specs=(blk, blk, blk),
        compiler_params=_cp(("parallel",)),
    )(w, g, m, v)


WEIGHTS = ("g_mix", "w_in", "conv_w", "conv_b", "dt_bias", "a_log", "d_skip", "g_ssd", "sinks", "w_attn_br", "w_ssd_br",
           "w_o", "g_ffn", "w_gate", "w_up", "w_down", "g_ple", "w_ple_gate", "w_ple_proj", "g_final")
BIG = {
    "w_gate": 256, "w_up": 256, "w_down": 128, "w_ssd_br": 128, "w_o": 128, "w_ple_gate": 128, "w_attn_br": 256,
    "w_ple_proj": 256, "w_in": None,
}
SMALL = tuple(n for n in WEIGHTS if n not in BIG)


def _pack_small(parts):
    rows = []
    for a in parts:
        a = a.reshape(-1)
        rows.append(jnp.pad(a, (0, -a.shape[0] % 128)).reshape(-1, 128))
    out = jnp.concatenate(rows, axis=0)
    return jnp.pad(out, ((0, -out.shape[0] % 8), (0, 0)))


def _unpack_small(packed, shapes):
    out, r = [], 0
    for s in shapes:
        n = int(np.prod(s))
        nr = -(-n // 128)
        out.append(packed[r:r + nr].reshape(-1)[:n].reshape(s))
        r += nr
    return out


def kernel(x, p, positions, g_mix, w_in, conv_w, conv_b, dt_bias, a_log, d_skip, g_ssd, sinks, w_attn_br, w_ssd_br, w_o, g_ffn, w_gate, w_up, w_down, g_ple, w_ple_gate, w_ple_proj, g_final, loss_target, m_g_mix, m_w_in, m_conv_w, m_conv_b, m_dt_bias, m_a_log, m_d_skip, m_g_ssd, m_sinks, m_w_attn_br, m_w_ssd_br, m_w_o, m_g_ffn, m_w_gate, m_w_up, m_w_down, m_g_ple, m_w_ple_gate, m_w_ple_proj, m_g_final, v_g_mix, v_w_in, v_conv_w, v_conv_b, v_dt_bias, v_a_log, v_d_skip, v_g_ssd, v_sinks, v_w_attn_br, v_w_ssd_br, v_w_o, v_g_ffn, v_w_gate, v_w_up, v_w_down, v_g_ple, v_w_ple_gate, v_w_ple_proj, v_g_final):
    w = dict(zip(WEIGHTS, (g_mix, w_in, conv_w, conv_b, dt_bias, a_log, d_skip, g_ssd, sinks, w_attn_br, w_ssd_br, w_o,
                           g_ffn, w_gate, w_up, w_down, g_ple, w_ple_gate, w_ple_proj, g_final)))
    m = dict(zip(WEIGHTS, (m_g_mix, m_w_in, m_conv_w, m_conv_b, m_dt_bias, m_a_log, m_d_skip, m_g_ssd, m_sinks, m_w_attn_br,
                           m_w_ssd_br, m_w_o, m_g_ffn, m_w_gate, m_w_up, m_w_down, m_g_ple, m_w_ple_gate, m_w_ple_proj,
                           m_g_final)))
    v = dict(zip(WEIGHTS, (v_g_mix, v_w_in, v_conv_w, v_conv_b, v_dt_bias, v_a_log, v_d_skip, v_g_ssd, v_sinks, v_w_attn_br,
                           v_w_ssd_br, v_w_o, v_g_ffn, v_w_gate, v_w_up, v_w_down, v_g_ple, v_w_ple_gate, v_w_ple_proj,
                           v_g_final)))
    xi, yi, ci = _pos()
    chip = 2 * xi + yi
    t = x.shape[1]
    cshard = CONV // NCHIP

    plan = _Plan({n: w[n].astype(BF16) for n in MATS}, TABLE)
    plan.run("gather_w_in", plan.jobs("gather_w_in"))
    placed = lax.dynamic_update_slice(jnp.zeros((CW, CONV), F32), w["conv_w"][0], (0, chip * cshard))
    conv_whole = _allreduce_small(jnp.where(ci == 0, placed, 0.0).reshape(-1, 128), name="gather_conv_w").reshape(CW, CONV)

    small = {n: w[n] for n in ("g_mix", "conv_b", "dt_bias", "a_log", "d_skip", "g_ssd", "sinks", "g_ffn", "g_ple", "g_final")}
    small["conv_w"] = conv_whole
    loss8, grad_x, gs = _local_step(x[0], p[0, 0], positions, loss_target[0], small, plan)

    order = ("g_mix", "conv_b", "dt_bias", "a_log", "d_skip", "g_ssd", "sinks", "g_ffn", "g_ple", "g_final", "conv_w")
    summed = _allreduce_small(_pack_small([loss8[0, :1]] + [gs[n] for n in order]), name="sum_small")
    parts = _unpack_small(summed, [(1,)] + [w[n].shape for n in order[:-1]] + [(CW, CONV)])
    loss = parts[0][0]
    grad = dict(zip(order, parts[1:]))
    grad["conv_w"] = lax.dynamic_slice(grad["conv_w"], (0, chip * cshard), (CW, cshard))[None]

    delta, new_m, new_v = {}, {}, {}
    for n, tr in BIG.items():
        grad[n] = plan.finish(n)
        if n == "w_in":
            d_, m_, v_ = _adamw(w[n][0].T, grad[n][0].T, m[n][0].T, v[n][0].T, tc=128, name="adamw_" + n)
            d_, m_, v_ = d_.T, m_.T, v_.T
        else:
            d_, m_, v_ = _adamw(w[n][0], grad[n][0], m[n][0], v[n][0], tr=tr, name="adamw_" + n,
                                jobs=plan.jobs("adamw_" + n))
        delta[n], new_m[n], new_v[n] = d_[None], m_[None], v_[None]
    shapes = [w[n].shape for n in SMALL]
    d_, m_, v_ = _adamw(_pack_small([w[n] for n in SMALL]), _pack_small([grad[n] for n in SMALL]),
                        _pack_small([m[n] for n in SMALL]), _pack_small([v[n] for n in SMALL]), tr=None, name="adamw_small")
    for n, a, b, c_ in zip(SMALL, _unpack_small(d_, shapes), _unpack_small(m_, shapes), _unpack_small(v_, shapes)):
        delta[n], new_m[n], new_v[n] = a, b, c_

    return (loss, grad_x[None], *[grad[n] for n in WEIGHTS], *[delta[n] for n in WEIGHTS],
            *[new_m[n] for n in WEIGHTS], *[new_v[n] for n in WEIGHTS])
```

```python
import functools

import jax
import jax.numpy as jnp
import numpy as np
from jax import lax
from jax.experimental import pallas as pl
from jax.experimental.pallas import tpu as pltpu

F32 = jnp.float32
BF16 = jnp.bfloat16
MESH = pl.DeviceIdType.MESH

D = 2048
HD = 64
NQH = 16
NKV = 4
QD = NQH * HD
KVD = NKV * HD
DI = 2048
NH = 32
NG = 4
NS = 128
CW = 4
L = 128
CONV = DI + 2 * NG * NS
FFN = 5632
PLE = 256
IN_DIM = QD + 2 * KVD + DI + CONV + NH + 2 * D
EPS = 1e-6
SSM_EPS = 1e-5
ROPE_THETA = 10000.0
LR, B1, B2, AEPS, WD, STEP = 0.001, 0.9, 0.999, 1e-08, 0.01, 10

O_GA, O_GS, O_Z, O_XBC, O_Q, O_K, O_V, O_DT = 0, 2048, 4096, 6144, 9216, 10240, 10496, 10752
DT_PAD = 512
NP = O_DT + DT_PAD
R_Q, R_K, R_V, R_Z, R_XBC, R_DT, R_GA, R_GS = 0, 1024, 1280, 1536, 3584, 6656, 6688, 8736

NCHIP = 4
VMEM_LIMIT = 52 * 1024 * 1024
NEG = -1e30


def _cp(sem=None):
    return pltpu.CompilerParams(dimension_semantics=sem, vmem_limit_bytes=VMEM_LIMIT)


def _dot(a, b):
    return lax.dot_general(a, b, (((1,), (0,)), ((), ())), preferred_element_type=F32)


def _dot_nt(a, b):
    return lax.dot_general(a, b, (((1,), (1,)), ((), ())), preferred_element_type=F32)


def _dot_tn(a, b):
    return lax.dot_general(a, b, (((0,), (0,)), ((), ())), preferred_element_type=F32)


def _sigmoid(x):
    return 1.0 / (1.0 + jnp.exp(-x))


ANY = pl.BlockSpec(memory_space=pl.ANY)


class _Job:
    srcs, dsts, news, scratch = (), (), (), ()

    def start(self, srcs, dsts, news, sems):
        raise NotImplementedError

    def finish(self, srcs, dsts, news, sems):
        raise NotImplementedError

    def done(self, dsts, news):
        pass


def _call(body, *, jobs=(), name, out_shape, in_specs, out_specs, grid=(), scratch_shapes=(), compiler_params=None,
          aliases=None):
    jobs = [j for j in jobs if j is not None]
    aliases = dict(aliases or {})
    if not jobs:
        return pl.pallas_call(body, name=name, out_shape=out_shape, in_specs=in_specs, out_specs=out_specs, grid=grid,
                              scratch_shapes=scratch_shapes, compiler_params=compiler_params,
                              input_output_aliases=aliases)
    single = not isinstance(out_shape, (tuple, list))
    outs = [out_shape] if single else list(out_shape)
    ospecs = [out_specs] if single else list(out_specs)
    n_in, n_out, n_scr = len(in_specs), len(outs), len(scratch_shapes)
    srcs = [a for j in jobs for a in j.srcs]
    dsts = [a for j in jobs for a in j.dsts]
    news = [a for j in jobs for a in j.news]
    sems = [a for j in jobs for a in j.scratch]

    def wrapped(*refs):
        pos = n_in + len(srcs) + len(dsts)
        ins, jsrc = refs[:n_in], refs[n_in:n_in + len(srcs)]
        o_refs = refs[pos:pos + n_out]
        pos += n_out
        jdst, jnew = refs[pos:pos + len(dsts)], refs[pos + len(dsts):pos + len(dsts) + len(news)]
        pos += len(dsts) + len(news)
        scr, jsem = refs[pos:pos + n_scr], refs[pos + n_scr:]

        def run(which):
            a = b = c = d = 0
            for j in jobs:
                getattr(j, which)(jsrc[a:a + len(j.srcs)], jdst[b:b + len(j.dsts)], jnew[c:c + len(j.news)],
                                  jsem[d:d + len(j.scratch)])
                a, b, c, d = a + len(j.srcs), b + len(j.dsts), c + len(j.news), d + len(j.scratch)

        if not grid:
            run("start")
            body(*ins, *o_refs, *scr)
            run("finish")
            return
        first = functools.reduce(jnp.logical_and, [pl.program_id(a) == 0 for a in range(len(grid))])
        last = functools.reduce(jnp.logical_and, [pl.program_id(a) == grid[a] - 1 for a in range(len(grid))])
        pl.when(first)(lambda: run("start"))
        body(*ins, *o_refs, *scr)
        pl.when(last)(lambda: run("finish"))

    call = pl.pallas_call(
        wrapped, name=name,
        out_shape=outs + [jax.ShapeDtypeStruct(a.shape, a.dtype) for a in dsts] + news,
        in_specs=list(in_specs) + [ANY] * (len(srcs) + len(dsts)),
        out_specs=ospecs + [ANY] * (len(dsts) + len(news)),
        grid=grid, scratch_shapes=list(scratch_shapes) + sems,
        input_output_aliases={**aliases, **{n_in + len(srcs) + i: n_out + i for i in range(len(dsts))}},
        compiler_params=_cp(("arbitrary",) * len(grid) if grid else None))

    def run_call(*args):
        res = call(*args, *srcs, *dsts)
        b, c = n_out, n_out + len(dsts)
        for j in jobs:
            j.done(res[b:b + len(j.dsts)], res[c:c + len(j.news)])
            b, c = b + len(j.dsts), c + len(j.news)
        return res[0] if single else tuple(res[:n_out])

    return run_call


def _matmul(a, b, *, ta=False, tb=False, out_dtype=F32, add=None, tm, tn, tk, name, jobs=()):
    k, m = a.shape if ta else a.shape[::-1]
    n = b.shape[0] if tb else b.shape[1]
    assert (b.shape[1] if tb else b.shape[0]) == k and not (ta and tb)
    assert m % tm == 0 and n % tn == 0 and k % tk == 0, (name, a.shape, b.shape)
    nk = k // tk
    has_add = add is not None

    def body(*refs):
        a_ref, b_ref = refs[0], refs[1]
        add_ref = refs[2] if has_add else None
        o_ref = refs[3] if has_add else refs[2]
        av = a_ref[...].astype(BF16)
        bv = b_ref[...].astype(BF16)
        part = _dot_tn(av, bv) if ta else _dot_nt(av, bv) if tb else _dot(av, bv)

        def finish(r):
            if has_add:
                r = r + add_ref[...]
            o_ref[...] = r.astype(out_dtype)

        if nk == 1:
            finish(part)
        else:
            acc_ref = refs[-1]
            kk = pl.program_id(2)

            @pl.when(kk == 0)
            def _():
                acc_ref[...] = part

            @pl.when(kk > 0)
            def _():
                acc_ref[...] += part

            @pl.when(kk == nk - 1)
            def _():
                finish(acc_ref[...])

    in_specs = [pl.BlockSpec((tk, tm), lambda i, j, kk: (kk, i)) if ta else pl.BlockSpec((tm, tk), lambda i, j, kk: (i, kk)),
                pl.BlockSpec((tn, tk), lambda i, j, kk: (j, kk)) if tb
                else pl.BlockSpec((tk, tn), lambda i, j, kk: (kk, j))]
    args = [a, b]
    if has_add:
        in_specs.append(pl.BlockSpec((tm, tn), lambda i, j, kk: (i, j)))
        args.append(add)
    return _call(
        body, jobs=jobs, name=name,
        out_shape=jax.ShapeDtypeStruct((m, n), out_dtype),
        grid=(m // tm, n // tn, nk),
        in_specs=in_specs,
        out_specs=pl.BlockSpec((tm, tn), lambda i, j, kk: (i, j)),
        scratch_shapes=[pltpu.VMEM((tm, tn), F32)] if nk > 1 else [],
        compiler_params=_cp(("parallel", "parallel", "arbitrary")),
    )(*args)


ROWS = 256


def _rmsnorm_fwd(x, g, *, name):
    t, d = x.shape

    def body(x_ref, g_ref, o_ref):
        xv = x_ref[...]
        r = lax.rsqrt(jnp.mean(xv * xv, axis=-1, keepdims=True) + EPS)
        o_ref[...] = (xv * r * g_ref[...]).astype(BF16)

    return pl.pallas_call(
        body, name=name, out_shape=jax.ShapeDtypeStruct((t, d), BF16), grid=(t // ROWS,),
        in_specs=[pl.BlockSpec((ROWS, d), lambda i: (i, 0)), pl.BlockSpec((1, d), lambda i: (0, 0))],
        out_specs=pl.BlockSpec((ROWS, d), lambda i: (i, 0)), compiler_params=_cp(("parallel",)),
    )(x, g)


def _rmsnorm_bwd(x, g, dy, dres, *, name, jobs=()):
    t, d = x.shape

    def body(x_ref, g_ref, dy_ref, dres_ref, dx_ref, dxb_ref, dg_ref):
        xv = x_ref[...]
        r = lax.rsqrt(jnp.mean(xv * xv, axis=-1, keepdims=True) + EPS)
        xh = xv * r
        dyv = dy_ref[...]
        dxh = dyv * g_ref[...]
        dx = r * (dxh - xh * jnp.mean(dxh * xh, axis=-1, keepdims=True))
        tot = dres_ref[...] + dx
        dx_ref[...] = tot
        dxb_ref[...] = tot.astype(BF16)

        @pl.when(pl.program_id(0) == 0)
        def _():
            dg_ref[...] = jnp.zeros_like(dg_ref)

        dg_ref[...] += jnp.broadcast_to(jnp.sum(dyv * xh, axis=0, keepdims=True), dg_ref.shape)

    row = pl.BlockSpec((ROWS, d), lambda i: (i, 0))
    return _call(
        body, jobs=jobs, name=name,
        out_shape=(jax.ShapeDtypeStruct((t, d), F32), jax.ShapeDtypeStruct((t, d), BF16),
                   jax.ShapeDtypeStruct((8, d), F32)),
        grid=(t // ROWS,),
        in_specs=[row, pl.BlockSpec((1, d), lambda i: (0, 0)), row, row],
        out_specs=(row, row, pl.BlockSpec((8, d), lambda i: (0, 0))),
        compiler_params=_cp(("arbitrary",)),
    )(x, g, dy, dres)


def _final(h2, pgl, pp, target, g_final, *, name):
    t, d = h2.shape

    def body(h2_ref, pgl_ref, pp_ref, tg_ref, g_ref, dh3_ref, dpgl_ref, dpp_ref, loss_ref, dg_ref):
        s = _sigmoid(pgl_ref[...])
        ppv = pp_ref[...]
        h3 = h2_ref[...] + s * ppv
        r = lax.rsqrt(jnp.mean(h3 * h3, axis=-1, keepdims=True) + EPS)
        xh = h3 * r
        gv = g_ref[...]
        err = xh * gv - tg_ref[...]
        dyv = err * (1.0 / d)
        dxh = dyv * gv
        dh3 = r * (dxh - xh * jnp.mean(dxh * xh, axis=-1, keepdims=True))
        dh3_ref[...] = dh3
        dpp_ref[...] = (dh3 * s).astype(BF16)
        dpgl_ref[...] = (dh3 * ppv * s * (1.0 - s)).astype(BF16)

        @pl.when(pl.program_id(0) == 0)
        def _():
            loss_ref[...] = jnp.zeros_like(loss_ref)
            dg_ref[...] = jnp.zeros_like(dg_ref)

        part = 0.5 * jnp.sum(jnp.mean(err * err, axis=-1, keepdims=True), axis=0, keepdims=True)
        loss_ref[...] += jnp.broadcast_to(part, loss_ref.shape)
        dg_ref[...] += jnp.broadcast_to(jnp.sum(dyv * xh, axis=0, keepdims=True), dg_ref.shape)

    row = pl.BlockSpec((ROWS, d), lambda i: (i, 0))
    return pl.pallas_call(
        body, name=name,
        out_shape=(jax.ShapeDtypeStruct((t, d), F32), jax.ShapeDtypeStruct((t, d), BF16),
                   jax.ShapeDtypeStruct((t, d), BF16), jax.ShapeDtypeStruct((8, 128), F32),
                   jax.ShapeDtypeStruct((8, d), F32)),
        grid=(t // ROWS,),
        in_specs=[row, row, row, row, pl.BlockSpec((1, d), lambda i: (0, 0))],
        out_specs=(row, row, row, pl.BlockSpec((8, 128), lambda i: (0, 0)), pl.BlockSpec((8, d), lambda i: (0, 0))),
        compiler_params=_cp(("arbitrary",)),
    )(h2, pgl, pp, target, g_final)


def _merge_fwd(proj, out_a, out_s, *, name):
    t = proj.shape[0]

    def body(ga_ref, gs_ref, a_ref, s_ref, o_ref):
        o_ref[...] = (_sigmoid(ga_ref[...]) * a_ref[...] + _sigmoid(gs_ref[...]) * s_ref[...]).astype(BF16)

    row = pl.BlockSpec((ROWS, D), lambda i: (i, 0))
    return pl.pallas_call(
        body, name=name, out_shape=jax.ShapeDtypeStruct((t, D), BF16), grid=(t // ROWS,),
        in_specs=[pl.BlockSpec((ROWS, D), lambda i: (i, O_GA // D)), pl.BlockSpec((ROWS, D), lambda i: (i, O_GS // D)),
                  row, row],
        out_specs=row, compiler_params=_cp(("parallel",)),
    )(proj, proj, out_a, out_s)


def _merge_bwd(proj, out_a, out_s, dmerged, *, name):
    t = proj.shape[0]
    assert O_GA == 0 and O_GS == D

    def body(ga_ref, gs_ref, a_ref, s_ref, dm_ref, da_ref, ds_ref, dp_ref):
        sa = _sigmoid(ga_ref[...])
        ss = _sigmoid(gs_ref[...])
        dm = dm_ref[...]
        da_ref[...] = (dm * sa).astype(BF16)
        ds_ref[...] = (dm * ss).astype(BF16)
        dp_ref[:, :D] = (dm * a_ref[...] * sa * (1.0 - sa)).astype(BF16)
        dp_ref[:, D:] = (dm * s_ref[...] * ss * (1.0 - ss)).astype(BF16)

    row = pl.BlockSpec((ROWS, D), lambda i: (i, 0))
    o = jax.ShapeDtypeStruct((t, D), BF16)
    return pl.pallas_call(
        body, name=name, out_shape=(o, o, jax.ShapeDtypeStruct((t, NP), BF16)), grid=(t // ROWS,),
        in_specs=[pl.BlockSpec((ROWS, D), lambda i: (i, O_GA // D)), pl.BlockSpec((ROWS, D), lambda i: (i, O_GS // D)),
                  row, row, row],
        out_specs=(row, row, pl.BlockSpec((ROWS, 2 * D), lambda i: (i, 0))), compiler_params=_cp(("parallel",)),
    )(proj, proj, out_a, out_s, dmerged)


def _swiglu_fwd(f, w_gate, w_up, *, name, tn=256, jobs=()):
    t, d = f.shape
    n = w_gate.shape[1]

    def body(f_ref, wg_ref, wu_ref, g_ref, u_ref, a_ref):
        fv = f_ref[...]
        g = _dot(fv, wg_ref[...])
        u = _dot(fv, wu_ref[...])
        g_ref[...] = g
        u_ref[...] = u
        a_ref[...] = (g * _sigmoid(g) * u).astype(BF16)

    col = pl.BlockSpec((t, tn), lambda j: (0, j))
    wcol = pl.BlockSpec((d, tn), lambda j: (0, j))
    return _call(
        body, jobs=jobs, name=name,
        out_shape=(jax.ShapeDtypeStruct((t, n), F32), jax.ShapeDtypeStruct((t, n), F32),
                   jax.ShapeDtypeStruct((t, n), BF16)),
        grid=(n // tn,),
        in_specs=[pl.BlockSpec((t, d), lambda j: (0, 0)), wcol, wcol],
        out_specs=(col, col, col), compiler_params=_cp(("parallel",)),
    )(f, w_gate, w_up)


def _swiglu_bwd(gate, up, dact, *, name, tc=1408, jobs=()):
    t, n = gate.shape

    def body(g_ref, u_ref, da_ref, dg_ref, du_ref):
        g = g_ref[...]
        s = _sigmoid(g)
        da = da_ref[...]
        du_ref[...] = (da * g * s).astype(BF16)
        dg_ref[...] = (da * u_ref[...] * s * (1.0 + g * (1.0 - s))).astype(BF16)

    blk = pl.BlockSpec((ROWS, tc), lambda i, j: (i, j))
    o = jax.ShapeDtypeStruct((t, n), BF16)
    return _call(
        body, jobs=jobs, name=name, out_shape=(o, o), grid=(t // ROWS, n // tc),
        in_specs=[blk, blk, blk], out_specs=(blk, blk), compiler_params=_cp(("parallel", "parallel")),
    )(gate, up, dact)


def _gated_norm_fwd(y_pre, proj, g_ssd, *, name):
    t = y_pre.shape[0]

    def body(y_ref, z_ref, g_ref, o_ref):
        z = z_ref[...]
        v = y_ref[...] * z * _sigmoid(z)
        r = lax.rsqrt(jnp.mean(v * v, axis=-1, keepdims=True) + SSM_EPS)
        o_ref[...] = (v * r * g_ref[...]).astype(BF16)

    row = pl.BlockSpec((ROWS, DI), lambda i: (i, 0))
    return pl.pallas_call(
        body, name=name, out_shape=jax.ShapeDtypeStruct((t, DI), BF16), grid=(t // ROWS,),
        in_specs=[row, pl.BlockSpec((ROWS, DI), lambda i: (i, O_Z // DI)), pl.BlockSpec((1, DI), lambda i: (0, 0))],
        out_specs=row, compiler_params=_cp(("parallel",)),
    )(y_pre, proj, g_ssd)


def _gated_norm_bwd(y_pre, proj, g_ssd, dyn, dproj, *, name, jobs=()):
    t = y_pre.shape[0]

    def body(y_ref, z_ref, g_ref, dyn_ref, _, dy_ref, dz_ref, dg_ref):
        z = z_ref[...]
        s = _sigmoid(z)
        sz = z * s
        yv = y_ref[...]
        v = yv * sz
        r = lax.rsqrt(jnp.mean(v * v, axis=-1, keepdims=True) + SSM_EPS)
        vh = v * r
        dn = dyn_ref[...]
        dvh = dn * g_ref[...]
        dv = r * (dvh - vh * jnp.mean(dvh * vh, axis=-1, keepdims=True))
        dy_ref[...] = dv * sz
        dz_ref[...] = (dv * yv * s * (1.0 + z * (1.0 - s))).astype(BF16)

        @pl.when(pl.program_id(0) == 0)
        def _():
            dg_ref[...] = jnp.zeros_like(dg_ref)

        dg_ref[...] += jnp.broadcast_to(jnp.sum(dn * vh, axis=0, keepdims=True), dg_ref.shape)

    row = pl.BlockSpec((ROWS, DI), lambda i: (i, 0))
    return _call(
        body, jobs=jobs, name=name,
        out_shape=(jax.ShapeDtypeStruct((t, DI), F32), jax.ShapeDtypeStruct(dproj.shape, BF16),
                   jax.ShapeDtypeStruct((8, DI), F32)),
        grid=(t // ROWS,),
        in_specs=[row, pl.BlockSpec((ROWS, DI), lambda i: (i, O_Z // DI)), pl.BlockSpec((1, DI), lambda i: (0, 0)), row, ANY],
        out_specs=(row, pl.BlockSpec((ROWS, DI), lambda i: (i, O_Z // DI)), pl.BlockSpec((8, DI), lambda i: (0, 0))),
        compiler_params=_cp(("arbitrary",)), aliases={4: 1},
    )(y_pre, proj, g_ssd, dyn, dproj)


CONV_TC = 512


def _shift_down(x, s, row):
    if s == 0:
        return x
    return jnp.where(row >= s, pltpu.roll(x, s, 0), 0.0)


def _shift_up(x, s, row, t):
    if s == 0:
        return x
    return jnp.where(row < t - s, pltpu.roll(x, t - s, 0), 0.0)


def _conv_fwd(proj, conv_w, conv_b, *, name):
    t = proj.shape[0]

    def body(x_ref, w_ref, b_ref, o_ref):
        x = x_ref[...]
        row = lax.broadcasted_iota(jnp.int32, x.shape, 0)
        pre = jnp.broadcast_to(b_ref[...], x.shape)
        for k in range(CW):
            pre = pre + w_ref[k:k + 1, :] * _shift_down(x, CW - 1 - k, row)
        o_ref[...] = pre * _sigmoid(pre)

    return pl.pallas_call(
        body, name=name, out_shape=jax.ShapeDtypeStruct((t, CONV), F32), grid=(CONV // CONV_TC,),
        in_specs=[pl.BlockSpec((t, CONV_TC), lambda j: (0, O_XBC // CONV_TC + j)),
                  pl.BlockSpec((CW, CONV_TC), lambda j: (0, j)), pl.BlockSpec((1, CONV_TC), lambda j: (0, j))],
        out_specs=pl.BlockSpec((t, CONV_TC), lambda j: (0, j)), compiler_params=_cp(("parallel",)),
    )(proj, conv_w, conv_b)


def _conv_bwd(proj, conv_w, conv_b, dxs, db, dc, dproj, *, name, jobs=()):
    t = proj.shape[0]
    nx = DI // CONV_TC
    assert NG * NS == CONV_TC

    def body(x_ref, w_ref, b_ref, dxs_ref, db_ref, dc_ref, _, dx_ref, dw_ref, dbias_ref):
        j = pl.program_id(0)
        x = x_ref[...]
        row = lax.broadcasted_iota(jnp.int32, x.shape, 0)
        xs = [_shift_down(x, CW - 1 - k, row) for k in range(CW)]
        pre = jnp.broadcast_to(b_ref[...], x.shape)
        for k in range(CW):
            pre = pre + w_ref[k:k + 1, :] * xs[k]
        s = _sigmoid(pre)
        da = jnp.where(j < nx, dxs_ref[...], jnp.where(j == nx, db_ref[...], dc_ref[...]))
        dpre = da * s * (1.0 + pre * (1.0 - s))
        dx = jnp.zeros_like(x)
        row8 = lax.broadcasted_iota(jnp.int32, dw_ref.shape, 0)
        dw = jnp.zeros(dw_ref.shape, F32)
        for k in range(CW):
            dx = dx + w_ref[k:k + 1, :] * _shift_up(dpre, CW - 1 - k, row, t)
            dw = dw + jnp.where(row8 == k, jnp.sum(dpre * xs[k], axis=0, keepdims=True), 0.0)
        dx_ref[...] = dx.astype(BF16)
        dw_ref[...] = dw
        dbias_ref[...] = jnp.broadcast_to(jnp.sum(dpre, axis=0, keepdims=True), dbias_ref.shape)

    col8 = pl.BlockSpec((8, CONV_TC), lambda j: (0, j))
    xbc = pl.BlockSpec((t, CONV_TC), lambda j: (0, O_XBC // CONV_TC + j))
    whole = pl.BlockSpec((t, CONV_TC), lambda j: (0, 0))
    return _call(
        body, jobs=jobs, name=name,
        out_shape=(jax.ShapeDtypeStruct(dproj.shape, BF16), jax.ShapeDtypeStruct((8, CONV), F32),
                   jax.ShapeDtypeStruct((8, CONV), F32)),
        grid=(CONV // CONV_TC,),
        in_specs=[xbc, pl.BlockSpec((CW, CONV_TC), lambda j: (0, j)), pl.BlockSpec((1, CONV_TC), lambda j: (0, j)),
                  pl.BlockSpec((t, CONV_TC), lambda j: (0, jnp.minimum(j, nx - 1))), whole, whole, ANY],
        out_specs=(xbc, col8, col8),
        compiler_params=_cp(("arbitrary",)), aliases={6: 0},
    )(proj, conv_w, conv_b, dxs, db, dc, dproj)


def _rope_tables(positions, t):
    half = HD // 2
    inv_freq = ROPE_THETA ** (-jnp.arange(half, dtype=F32) * 2.0 / HD)
    ang = positions.reshape(t).astype(F32)[:, None] * inv_freq
    cos, sin = jnp.cos(ang), jnp.sin(ang)
    return jnp.concatenate([cos] * 4, axis=1), jnp.concatenate([-sin, sin] * 2, axis=1)


def _lane_consts():
    lane = lax.broadcasted_iota(jnp.int32, (L, 128), 1)
    return lane, (lane % HD) < (HD // 2), lane < HD


def _rope(tv, cos, sin, lo):
    return tv * cos + jnp.where(lo, pltpu.roll(tv, 128 - HD // 2, 1), pltpu.roll(tv, HD // 2, 1)) * sin


def _rope_t(dv, cos, sin, lo):
    ds = dv * sin
    return dv * cos + jnp.where(lo, pltpu.roll(ds, 128 - HD // 2, 1), pltpu.roll(ds, HD // 2, 1))


def _placed(chunk, g, half0):
    own = jnp.where(half0 if g % 2 == 0 else jnp.logical_not(half0), chunk, 0.0)
    other = pltpu.roll(own, HD, 1)
    return (own, other) if g % 2 == 0 else (other, own)


def _unplace(acc, hf, g, half0):
    v = jnp.where(half0 if hf == 0 else jnp.logical_not(half0), acc, 0.0)
    return v if hf == g % 2 else pltpu.roll(v, HD, 1)


def _attn_fwd(proj, cos, sin, sinks, *, name, jobs=()):
    t = proj.shape[0]
    nb = t // L
    scale = HD ** -0.5

    def body(sink_ref, q_ref, kc_ref, kp_ref, vc_ref, vp_ref, cc_ref, sc_ref, cp_ref, sp_ref, o_ref, lse_ref):
        i = pl.program_id(0)
        lane, lo, half0 = _lane_consts()
        cos_c, sin_c, cos_p, sin_p = cc_ref[...], sc_ref[...], cp_ref[...], sp_ref[...]
        row = lax.broadcasted_iota(jnp.int32, (L, L), 0)
        col = lax.broadcasted_iota(jnp.int32, (L, L), 1)
        m_cur = col <= row
        m_prev = jnp.logical_and(col > row, i > 0)
        kc = [_rope(kc_ref[:, 128 * m:128 * (m + 1)], cos_c, sin_c, lo) for m in range(2)]
        kp = [_rope(kp_ref[:, 128 * m:128 * (m + 1)], cos_p, sin_p, lo) for m in range(2)]
        lse_acc = jnp.zeros((L, 128), F32)
        outs = [jnp.zeros((L, 128), F32) for _ in range(QD // 128)]
        qs = [(_rope(q_ref[:, 128 * ch:128 * (ch + 1)], cos_c, sin_c, lo) * scale).astype(BF16) for ch in range(QD // 128)]
        for g in range(NKV):
            kcv = [v.astype(BF16) for v in _placed(kc[g // 2], g, half0)]
            kpv = [v.astype(BF16) for v in _placed(kp[g // 2], g, half0)]
            vcv = [v.astype(BF16) for v in _placed(vc_ref[:, 128 * (g // 2):128 * (g // 2 + 1)], g, half0)]
            vpv = [v.astype(BF16) for v in _placed(vp_ref[:, 128 * (g // 2):128 * (g // 2 + 1)], g, half0)]
            for r in range(NQH // NKV):
                h = g * (NQH // NKV) + r
                ch, hf = h // 2, h % 2
                s_c = jnp.where(m_cur, _dot_nt(qs[ch], kcv[hf]), NEG)
                s_p = jnp.where(m_prev, _dot_nt(qs[ch], kpv[hf]), NEG)
                sink = sink_ref[0, h]
                mx = jnp.maximum(jnp.maximum(jnp.max(s_c, axis=-1, keepdims=True), jnp.max(s_p, axis=-1, keepdims=True)), sink)
                e_c = jnp.exp(s_c - mx)
                e_p = jnp.exp(s_p - mx)
                den = jnp.sum(e_c, axis=-1, keepdims=True) + jnp.sum(e_p, axis=-1, keepdims=True) + jnp.exp(sink - mx)
                inv = 1.0 / den
                outs[ch] = outs[ch] + _dot((e_c * inv).astype(BF16), vcv[hf]) + _dot((e_p * inv).astype(BF16), vpv[hf])
                lse_acc = jnp.where(lane == h, mx + jnp.log(den), lse_acc)
        for ch in range(QD // 128):
            o_ref[:, 128 * ch:128 * (ch + 1)] = outs[ch].astype(BF16)
        lse_ref[...] = lse_acc

    prev = lambda i: jnp.maximum(i - 1, 0)
    tab_c = pl.BlockSpec((L, 128), lambda i: (i, 0))
    tab_p = pl.BlockSpec((L, 128), lambda i: (prev(i), 0))
    return _call(
        body, jobs=jobs, name=name,
        out_shape=(jax.ShapeDtypeStruct((t, QD), BF16), jax.ShapeDtypeStruct((t, 128), F32)),
        grid=(nb,),
        in_specs=[pl.BlockSpec(memory_space=pltpu.SMEM),
                  pl.BlockSpec((L, QD), lambda i: (i, O_Q // QD)),
                  pl.BlockSpec((L, KVD), lambda i: (i, O_K // KVD)), pl.BlockSpec((L, KVD), lambda i: (prev(i), O_K // KVD)),
                  pl.BlockSpec((L, KVD), lambda i: (i, O_V // KVD)), pl.BlockSpec((L, KVD), lambda i: (prev(i), O_V // KVD)),
                  tab_c, tab_c, tab_p, tab_p],
        out_specs=(pl.BlockSpec((L, QD), lambda i: (i, 0)), pl.BlockSpec((L, 128), lambda i: (i, 0))),
        compiler_params=_cp(("parallel",)),
    )(sinks, proj, proj, proj, proj, proj, cos, sin, cos, sin)


def _attn_bwd(proj, cos, sin, sinks, attn, lse, dattn, dproj, *, name, jobs=()):
    t = proj.shape[0]
    nb = t // L
    scale = HD ** -0.5

    def body(sink_ref, qi_ref, qn_ref, kc_ref, kp_ref, vc_ref, vp_ref, doi_ref, don_ref, oi_ref, on_ref,
             lsei_ref, lsen_ref, cc_ref, sc_ref, cp_ref, sp_ref, cn_ref, sn_ref, _, dqkv_ref, dsk_ref):
        i = pl.program_id(0)
        lane, lo, half0 = _lane_consts()
        half1 = jnp.logical_not(half0)
        cos_c, sin_c = cc_ref[...], sc_ref[...]
        row = lax.broadcasted_iota(jnp.int32, (L, L), 0)
        col = lax.broadcasted_iota(jnp.int32, (L, L), 1)
        m_cur = col <= row
        m_prev = jnp.logical_and(col > row, i > 0)
        m_next = jnp.logical_and(col > row, i < nb - 1)
        kc = [_rope(kc_ref[:, 128 * m:128 * (m + 1)], cos_c, sin_c, lo) for m in range(2)]
        kp = [_rope(kp_ref[:, 128 * m:128 * (m + 1)], cp_ref[...], sp_ref[...], lo) for m in range(2)]
        lse_i, lse_n = lsei_ref[...], lsen_ref[...]
        dk_acc = [jnp.zeros((L, 128), F32) for _ in range(2)]
        dv_acc = [jnp.zeros((L, 128), F32) for _ in range(2)]
        dsk_acc = jnp.zeros((1, 128), F32)
        lane1 = lax.broadcasted_iota(jnp.int32, (1, 128), 1)
        place = lambda chunk, g: [v.astype(BF16) for v in _placed(chunk, g, half0)]
        kcs = [place(kc[g // 2], g) for g in range(NKV)]
        kps = [place(kp[g // 2], g) for g in range(NKV)]
        vcs = [place(vc_ref[:, 128 * (g // 2):128 * (g // 2 + 1)], g) for g in range(NKV)]
        vps = [place(vp_ref[:, 128 * (g // 2):128 * (g // 2 + 1)], g) for g in range(NKV)]
        for ch in range(QD // 128):
            sl = slice(128 * ch, 128 * (ch + 1))
            q_i = (_rope(qi_ref[:, sl], cos_c, sin_c, lo) * scale).astype(BF16)
            q_n = (_rope(qn_ref[:, sl], cn_ref[...], sn_ref[...], lo) * scale).astype(BF16)
            do_i, do_n = doi_ref[:, sl], don_ref[:, sl]
            do_ib, do_nb = do_i.astype(BF16), do_n.astype(BF16)
            od_i = do_i * oi_ref[:, sl].astype(F32)
            od_n = do_n * on_ref[:, sl].astype(F32)
            dq_ch = jnp.zeros((L, 128), F32)
            for hf in range(2):
                h = 2 * ch + hf
                g = h // (NQH // NKV)
                hm = half0 if hf == 0 else half1
                kcv, kpv, vcv, vpv = kcs[g][hf], kps[g][hf], vcs[g][hf], vps[g][hf]
                dl_i = jnp.sum(jnp.where(hm, od_i, 0.0), axis=-1, keepdims=True)
                dl_n = jnp.sum(jnp.where(hm, od_n, 0.0), axis=-1, keepdims=True)
                ls_i = jnp.sum(jnp.where(lane == h, lse_i, 0.0), axis=-1, keepdims=True)
                ls_n = jnp.sum(jnp.where(lane == h, lse_n, 0.0), axis=-1, keepdims=True)
                p_c = jnp.where(m_cur, jnp.exp(_dot_nt(q_i, kcv) - ls_i), 0.0)
                p_p = jnp.where(m_prev, jnp.exp(_dot_nt(q_i, kpv) - ls_i), 0.0)
                ds_c = (p_c * (_dot_nt(do_ib, vcv) - dl_i)).astype(BF16)
                ds_p = (p_p * (_dot_nt(do_ib, vpv) - dl_i)).astype(BF16)
                dq_ch = dq_ch + jnp.where(hm, (_dot(ds_c, kcv) + _dot(ds_p, kpv)) * scale, 0.0)
                sink = sink_ref[0, h]
                dsk = -jnp.sum(jnp.exp(sink - ls_i) * dl_i, axis=0, keepdims=True)
                dsk_acc = dsk_acc + jnp.where(lane1 == h, dsk, 0.0)
                p_n = jnp.where(m_next, jnp.exp(_dot_nt(q_n, kcv) - ls_n), 0.0)
                ds_n = (p_n * (_dot_nt(do_nb, vcv) - dl_n)).astype(BF16)
                dv_h = _dot_tn(p_c.astype(BF16), do_ib) + _dot_tn(p_n.astype(BF16), do_nb)
                dk_h = _dot_tn(ds_c, q_i) + _dot_tn(ds_n, q_n)
                dv_acc[g // 2] = dv_acc[g // 2] + _unplace(dv_h, hf, g, half0)
                dk_acc[g // 2] = dk_acc[g // 2] + _unplace(dk_h, hf, g, half0)
            dqkv_ref[:, sl] = _rope_t(dq_ch, cos_c, sin_c, lo).astype(BF16)
        for m in range(2):
            dqkv_ref[:, QD + 128 * m:QD + 128 * (m + 1)] = _rope_t(dk_acc[m], cos_c, sin_c, lo).astype(BF16)
            dqkv_ref[:, QD + KVD + 128 * m:QD + KVD + 128 * (m + 1)] = dv_acc[m].astype(BF16)

        @pl.when(i == 0)
        def _():
            dsk_ref[...] = jnp.zeros_like(dsk_ref)

        dsk_ref[...] += jnp.broadcast_to(dsk_acc, dsk_ref.shape)

    prev = lambda i: jnp.maximum(i - 1, 0)
    nxt = lambda i: jnp.minimum(i + 1, nb - 1)
    cur_q = pl.BlockSpec((L, QD), lambda i: (i, 0))
    nxt_q = pl.BlockSpec((L, QD), lambda i: (nxt(i), 0))
    tab = lambda f: pl.BlockSpec((L, 128), lambda i: (f(i), 0))
    ident = lambda i: i
    qkv = QD + 2 * KVD
    assert O_K == O_Q + QD and O_V == O_K + KVD and O_Q % qkv == 0
    return _call(
        body, jobs=jobs, name=name,
        out_shape=(jax.ShapeDtypeStruct(dproj.shape, BF16), jax.ShapeDtypeStruct((8, 128), F32)),
        grid=(nb,),
        in_specs=[pl.BlockSpec(memory_space=pltpu.SMEM),
                  pl.BlockSpec((L, QD), lambda i: (i, O_Q // QD)), pl.BlockSpec((L, QD), lambda i: (nxt(i), O_Q // QD)),
                  pl.BlockSpec((L, KVD), lambda i: (i, O_K // KVD)), pl.BlockSpec((L, KVD), lambda i: (prev(i), O_K // KVD)),
                  pl.BlockSpec((L, KVD), lambda i: (i, O_V // KVD)), pl.BlockSpec((L, KVD), lambda i: (prev(i), O_V // KVD)),
                  cur_q, nxt_q, cur_q, nxt_q, tab(ident), tab(nxt),
                  tab(ident), tab(ident), tab(prev), tab(prev), tab(nxt), tab(nxt), ANY],
        out_specs=(pl.BlockSpec((L, qkv), lambda i: (i, O_Q // qkv)), pl.BlockSpec((8, 128), lambda i: (0, 0))),
        compiler_params=_cp(("arbitrary",)), aliases={19: 0},
    )(sinks, proj, proj, proj, proj, proj, proj, dattn, dattn, attn, attn, lse, lse, cos, sin, cos, sin, cos, sin, dproj)


PAIRS = NH // NG // 2


def _softplus(x):
    return jnp.maximum(x, 0.0) + jnp.log(1.0 + jnp.exp(-jnp.abs(x)))


def _ssd_chunk(g, xps, dtr, bm, cm, sps, dtb, alog, dsk):
    lane = lax.broadcasted_iota(jnp.int32, (L, 128), 1)
    lane1 = lax.broadcasted_iota(jnp.int32, (1, 128), 1)
    row = lax.broadcasted_iota(jnp.int32, (L, L), 0)
    col = lax.broadcasted_iota(jnp.int32, (L, L), 1)
    rowc = lax.broadcasted_iota(jnp.int32, (128, 1), 0)
    tril = col <= row
    dt = _softplus(dtr + dtb)
    a = dt * (-jnp.exp(alog))
    a_cs = lax.dot_general(tril.astype(F32), a, (((1,), (0,)), ((), ())), precision=lax.Precision.HIGHEST,
                           preferred_element_type=F32)
    a_cst = a_cs.T
    a_last = jnp.sum(jnp.where(row == L - 1, a_cs, 0.0), axis=0, keepdims=True)
    cb = _dot_nt(cm.astype(BF16), bm.astype(BF16))
    ys, snew = [], []
    for q in range(PAIRS):
        xp, sp = xps[q], sps[q]
        y_pair = jnp.zeros((L, 128), F32)
        st_pair = jnp.zeros((128, NS), F32)
        keep = jnp.zeros((128, 1), F32)
        for hh in range(2):
            h = g * 2 * PAIRS + 2 * q + hh
            hm = (lane < HD) if hh == 0 else (lane >= HD)
            rm = (rowc < HD) if hh == 0 else (rowc >= HD)
            dt_h = jnp.sum(jnp.where(lane == h, dt, 0.0), axis=1, keepdims=True)
            acs_h = jnp.sum(jnp.where(lane == h, a_cs, 0.0), axis=1, keepdims=True)
            acst_h = jnp.sum(jnp.where(row == h, a_cst, 0.0), axis=0, keepdims=True)
            al_h = jnp.sum(jnp.where(lane1 == h, a_last, 0.0), axis=1, keepdims=True)
            dsk_h = jnp.sum(jnp.where(lane1 == h, dsk, 0.0), axis=1, keepdims=True)
            decay = jnp.where(tril, jnp.exp(jnp.where(tril, acs_h - acst_h, 0.0)), 0.0)
            xh = jnp.where(hm, xp, 0.0)
            xd = (xh * dt_h).astype(BF16)
            y = _dot((cb * decay).astype(BF16), xd)
            y = y + jnp.where(hm, _dot_nt((cm * jnp.exp(acs_h)).astype(BF16), sp.astype(BF16)), 0.0)
            y_pair = y_pair + y + dsk_h * xh
            st_pair = st_pair + _dot_tn(xd, (bm * jnp.exp(al_h - acs_h)).astype(BF16))
            keep = keep + jnp.where(rm, jnp.exp(al_h), 0.0)
        ys.append(y_pair)
        snew.append(sp * keep + st_pair)
    return ys, snew


def _ssd_specs(t):
    nc = t // L
    xs = lambda f: pl.BlockSpec((L, 128 * PAIRS), lambda c, g: (f(c), g))
    bspec = lambda f: pl.BlockSpec((L, NS), lambda c, g: (f(c), DI // NS + g))
    cspec = lambda f: pl.BlockSpec((L, NS), lambda c, g: (f(c), DI // NS + NG + g))
    dts = lambda f: pl.BlockSpec((L, 128), lambda c, g: (f(c), O_DT // 128))
    par = pl.BlockSpec((1, 128), lambda c, g: (0, 0))
    st = lambda f: pl.BlockSpec((1, 1, PAIRS, 128, NS), lambda c, g: (f(c), g, 0, 0, 0))
    return nc, xs, bspec, cspec, dts, par, st


def _ssd_fwd(xbc_act, proj, dtb, alog, dsk, *, name, jobs=()):
    t = proj.shape[0]
    nc, xs, bspec, cspec, dts, par, st = _ssd_specs(t)
    ident = lambda c: c

    def body(x_ref, b_ref, c_ref, dt_ref, dtb_ref, al_ref, dsk_ref, y_ref, sin_ref, s_ref):
        c, g = pl.program_id(0), pl.program_id(1)

        @pl.when(c == 0)
        def _():
            s_ref[g] = jnp.zeros((PAIRS, 128, NS), F32)

        sps = [s_ref[g, q] for q in range(PAIRS)]
        for q in range(PAIRS):
            sin_ref[0, 0, q] = sps[q]
        xps = [x_ref[:, 128 * q:128 * (q + 1)] for q in range(PAIRS)]
        ys, snew = _ssd_chunk(g, xps, dt_ref[...], b_ref[...], c_ref[...], sps, dtb_ref[...], al_ref[...], dsk_ref[...])
        for q in range(PAIRS):
            y_ref[:, 128 * q:128 * (q + 1)] = ys[q]
            s_ref[g, q] = snew[q]

    return _call(
        body, jobs=jobs, name=name,
        out_shape=(jax.ShapeDtypeStruct((t, DI), F32), jax.ShapeDtypeStruct((nc, NG, PAIRS, 128, NS), F32)),
        grid=(nc, NG),
        in_specs=[xs(ident), bspec(ident), cspec(ident), dts(ident), par, par, par],
        out_specs=(pl.BlockSpec((L, 128 * PAIRS), lambda c, g: (c, g)), st(ident)),
        scratch_shapes=[pltpu.VMEM((NG, PAIRS, 128, NS), F32)],
        compiler_params=_cp(("arbitrary", "arbitrary")),
    )(xbc_act, xbc_act, xbc_act, proj, dtb, alog, dsk)


def _ssd_bwd(xbc_act, proj, dtb, alog, dsk, states, dy, dproj, *, name, jobs=()):
    t = proj.shape[0]
    nc, xs, bspec, cspec, dts, par, st = _ssd_specs(t)
    rev = lambda c: nc - 1 - c

    def body(x_ref, b_ref, c_ref, dt_ref, dtb_ref, al_ref, dsk_ref, sin_ref, dy_ref, _,
             dx_ref, db_ref, dc_ref, ddtp_ref, ddtb_ref, dal_ref, ddsk_ref, ds_ref, ddt_ref):
        c, g = pl.program_id(0), pl.program_id(1)

        @pl.when(c == 0)
        def _():
            ds_ref[g] = jnp.zeros((PAIRS, 128, NS), F32)

        @pl.when(jnp.logical_and(c == 0, g == 0))
        def _():
            ddtb_ref[...] = jnp.zeros_like(ddtb_ref)
            dal_ref[...] = jnp.zeros_like(dal_ref)
            ddsk_ref[...] = jnp.zeros_like(ddsk_ref)

        @pl.when(g == 0)
        def _():
            ddt_ref[...] = jnp.zeros_like(ddt_ref)

        sps = [sin_ref[0, 0, q] for q in range(PAIRS)]
        xps = [x_ref[:, 128 * q:128 * (q + 1)] for q in range(PAIRS)]
        _, vjp = jax.vjp(functools.partial(_ssd_chunk, g), xps, dt_ref[...], b_ref[...], c_ref[...], sps,
                         dtb_ref[...], al_ref[...], dsk_ref[...])
        dys = [dy_ref[:, 128 * q:128 * (q + 1)] for q in range(PAIRS)]
        dss = [ds_ref[g, q] for q in range(PAIRS)]
        dxps, ddt, db, dc, dsps, ddtb, dal, ddsk = vjp((dys, dss))
        for q in range(PAIRS):
            dx_ref[:, 128 * q:128 * (q + 1)] = dxps[q]
            ds_ref[g, q] = dsps[q]
        db_ref[...] = db
        dc_ref[...] = dc
        ddt_ref[...] += ddt
        ddtb_ref[...] += jnp.broadcast_to(ddtb, ddtb_ref.shape)
        dal_ref[...] += jnp.broadcast_to(dal, dal_ref.shape)
        ddsk_ref[...] += jnp.broadcast_to(ddsk, ddsk_ref.shape)

        @pl.when(g == NG - 1)
        def _():
            ddtp_ref[:, :128] = ddt_ref[...].astype(BF16)
            ddtp_ref[:, 128:] = jnp.zeros((L, DT_PAD - 128), BF16)

    acc = pl.BlockSpec((8, 128), lambda c, g: (0, 0))
    o8 = jax.ShapeDtypeStruct((8, 128), F32)
    return _call(
        body, jobs=jobs, name=name,
        out_shape=(jax.ShapeDtypeStruct((t, DI), F32), jax.ShapeDtypeStruct((t, NG * NS), F32),
                   jax.ShapeDtypeStruct((t, NG * NS), F32), jax.ShapeDtypeStruct(dproj.shape, BF16), o8, o8, o8),
        grid=(nc, NG),
        in_specs=[xs(rev), bspec(rev), cspec(rev), dts(rev), par, par, par, st(rev),
                  pl.BlockSpec((L, 128 * PAIRS), lambda c, g: (rev(c), g)), ANY],
        out_specs=(pl.BlockSpec((L, 128 * PAIRS), lambda c, g: (rev(c), g)),
                   pl.BlockSpec((L, NS), lambda c, g: (rev(c), g)), pl.BlockSpec((L, NS), lambda c, g: (rev(c), g)),
                   pl.BlockSpec((L, DT_PAD), lambda c, g: (rev(c), O_DT // DT_PAD)), acc, acc, acc),
        scratch_shapes=[pltpu.VMEM((NG, PAIRS, 128, NS), F32), pltpu.VMEM((L, 128), F32)],
        compiler_params=_cp(("arbitrary", "arbitrary")), aliases={9: 3},
    )(xbc_act, xbc_act, xbc_act, proj, dtb, alog, dsk, states, dy, dproj)


def _pad_lanes(v, n=128):
    return jnp.pad(v, ((0, 0), (0, n - v.shape[1])))


class _LocalPlan:
    core = 0

    def __init__(self, big):
        self.big, self.grad, self.halves = big, {}, {}

    def w(self, n):
        return self.big[n]

    def g(self, n, a):
        self.grad[n] = a

    def g_half(self, n, which, a):
        self.halves[which] = a
        if len(self.halves) == 2:
            self.grad[n] = jnp.concatenate([self.halves["keep"], self.halves["send"]], axis=0)

    def jobs(self, tag):
        return ()


def _local_step(x, p, positions, target, small, plan):
    t = x.shape[0]
    cos, sin = _rope_tables(positions, t)
    dtb, alog, dsk = _pad_lanes(small["dt_bias"]), _pad_lanes(small["a_log"]), _pad_lanes(small["d_skip"])
    w, jobs = plan.w, plan.jobs

    def mm(a, b, *, name, tm=t, **kw):
        return _matmul(a, b, tm=tm, tn=512, name=name, jobs=jobs(name), **kw)

    def dw(wname, a, dy, *, name, tm):
        plan.g(wname, _matmul(a, dy, ta=True, out_dtype=BF16, tm=tm, tn=512, tk=t, name=name, jobs=jobs(name)))

    u = _rmsnorm_fwd(x, small["g_mix"], name="norm_mix")
    proj = mm(u, w("w_in"), tk=D, name="mm_in")
    attn, lse = _attn_fwd(proj, cos, sin, small["sinks"], name="attn_fwd", jobs=jobs("attn_fwd"))
    out_a = mm(attn, w("w_attn_br"), tk=QD, name="mm_attn_br")
    xbc_act = _conv_fwd(proj, small["conv_w"], small["conv_b"], name="conv_fwd")
    y_pre, states = _ssd_fwd(xbc_act, proj, dtb, alog, dsk, name="ssd_fwd", jobs=jobs("ssd_fwd"))
    yn = _gated_norm_fwd(y_pre, proj, small["g_ssd"], name="gated_norm_fwd")
    out_s = mm(yn, w("w_ssd_br"), tk=DI, name="mm_ssd_br")
    merged = _merge_fwd(proj, out_a, out_s, name="merge_fwd")
    h1 = mm(merged, w("w_o"), add=x, tk=D, name="mm_o")
    f = _rmsnorm_fwd(h1, small["g_ffn"], name="norm_ffn")
    gate, up, act = _swiglu_fwd(f, w("w_gate"), w("w_up"), name="swiglu_fwd", jobs=jobs("swiglu_fwd"))
    h2 = mm(act, w("w_down"), add=h1, tm=t // 2, tk=FFN // 2, name="mm_down")
    e = _rmsnorm_fwd(h2, small["g_ple"], name="norm_ple")
    pgl = mm(e, w("w_ple_gate"), tk=D, name="mm_ple_gate")
    pb = p.astype(BF16)
    pp = mm(pb, w("w_ple_proj"), tk=PLE, name="mm_ple_proj")
    dh3, dpgl, dpp, loss, dg_final = _final(h2, pgl, pp, target, small["g_final"].reshape(1, D), name="final")

    dw("w_ple_proj", pb, dpp, tm=PLE, name="mm_d_ple_proj")
    dw("w_ple_gate", e, dpgl, tm=D, name="mm_d_ple_gate")
    de = mm(dpgl, w("w_ple_gate"), tb=True, tk=D, name="mm_de")
    dh2, dh2b, dg_ple = _rmsnorm_bwd(h2, small["g_ple"], de, dh3, name="norm_ple_bwd", jobs=jobs("norm_ple_bwd"))
    dw("w_down", act, dh2b, tm=FFN // 2, name="mm_d_down")
    dact = mm(dh2b, w("w_down"), tb=True, tk=D, name="mm_dact")
    dgate, dup = _swiglu_bwd(gate, up, dact, name="swiglu_bwd", jobs=jobs("swiglu_bwd"))
    dw("w_gate", f, dgate, tm=D, name="mm_d_gate")
    dw("w_up", f, dup, tm=D, name="mm_d_up")
    df = mm(dgate, w("w_gate"), tb=True, tm=t // 2, tk=FFN // 2, name="mm_df_gate")
    df = mm(dup, w("w_up"), tb=True, add=df, tm=t // 2, tk=FFN // 2, name="mm_df_up")
    dh1, dh1b, dg_ffn = _rmsnorm_bwd(h1, small["g_ffn"], df, dh2, name="norm_ffn_bwd", jobs=jobs("norm_ffn_bwd"))
    dw("w_o", merged, dh1b, tm=D, name="mm_d_o")
    dmerged = mm(dh1b, w("w_o"), tb=True, tk=D, name="mm_dmerged")
    dout_a, dout_s, dproj = _merge_bwd(proj, out_a, out_s, dmerged, name="merge_bwd")
    dw("w_attn_br", attn, dout_a, tm=QD, name="mm_d_attn_br")
    dw("w_ssd_br", yn, dout_s, tm=DI, name="mm_d_ssd_br")
    dattn = mm(dout_a, w("w_attn_br"), tb=True, tk=D, name="mm_dattn")
    dyn = mm(dout_s, w("w_ssd_br"), tb=True, tk=D, name="mm_dyn")
    dproj, dsinks = _attn_bwd(proj, cos, sin, small["sinks"], attn, lse, dattn, dproj, name="attn_bwd",
                              jobs=jobs("attn_bwd"))
    dy_pre, dproj, dg_ssd = _gated_norm_bwd(y_pre, proj, small["g_ssd"], dyn, dproj, name="gated_norm_bwd",
                                            jobs=jobs("gated_norm_bwd"))
    dxs, db, dc, dproj, ddtb, dalog, ddsk = _ssd_bwd(xbc_act, proj, dtb, alog, dsk, states, dy_pre, dproj, name="ssd_bwd",
                                                     jobs=jobs("ssd_bwd"))
    dproj, dconv_w, dconv_b = _conv_bwd(proj, small["conv_w"], small["conv_b"], dxs, db, dc, dproj, name="conv_bwd",
                                        jobs=jobs("conv_bwd"))
    for which, h in (("send", 1 - plan.core), ("keep", plan.core)):
        uh = lax.dynamic_slice_in_dim(u, h * (D // 2), D // 2, axis=1)
        name = "mm_d_in_" + which
        plan.g_half("w_in", which, _matmul(uh, dproj, ta=True, out_dtype=BF16, tm=D // 2, tn=512, tk=t, name=name,
                                           jobs=jobs(name)))
    du = mm(dproj, w("w_in"), tb=True, tm=t // 2, tk=NP // 4, name="mm_du")
    grad_x, _, dg_mix = _rmsnorm_bwd(x, small["g_mix"], du, dh1, name="norm_mix_bwd", jobs=jobs("norm_mix_bwd"))

    gs = {
        "g_mix": dg_mix[:1], "conv_w": dconv_w[:CW], "conv_b": dconv_b[:1], "dt_bias": ddtb[:1, :NH],
        "a_log": dalog[:1, :NH], "d_skip": ddsk[:1, :NH], "g_ssd": dg_ssd[:1], "sinks": dsinks[:1, :NQH],
        "g_ffn": dg_ffn[:1], "g_ple": dg_ple[:1], "g_final": dg_final[0],
    }
    return loss, grad_x, gs


def _to_kernel_cols(w):
    seg = lambda o, n: w[:, o:o + n]
    return jnp.concatenate([seg(R_GA, D), seg(R_GS, D), seg(R_Z, DI), seg(R_XBC, CONV), seg(R_Q, QD), seg(R_K, KVD),
                            seg(R_V, KVD), seg(R_DT, NH), jnp.zeros((w.shape[0], DT_PAD - NH), w.dtype)], axis=1)


def _from_kernel_cols(g):
    seg = lambda o, n: g[:, o:o + n]
    return jnp.concatenate([seg(O_Q, QD), seg(O_K, KVD), seg(O_V, KVD), seg(O_Z, DI), seg(O_XBC, CONV), seg(O_DT, NH),
                            seg(O_GA, D), seg(O_GS, D)], axis=1)


RELS = ((0, 1), (1, 0), (1, 1))
MATS = {
    n: (n, kind, 1, r, c, tp, tf) for n, kind, r, c, tp, tf in (
        ("w_in", "stk", 2048, 2696, 256, 256),
        ("w_attn_br", "col", 1024, 512, 256, 256),
        ("w_ssd_br", "row", 512, 2048, 512, 256),
        ("w_o", "row", 512, 2048, 512, 256),
        ("w_gate", "col", 2048, 1408, 256, 256),
        ("w_up", "col", 2048, 1408, 256, 256),
        ("w_down", "row", 1408, 2048, 704, 704),
        ("w_ple_gate", "row", 512, 2048, 512, 256),
        ("w_ple_proj", "col", 256, 512, 128, 128),
    )}


def _pos():
    return lax.axis_index("x"), lax.axis_index("y"), lax.axis_index("c")


def _flip(v, a):
    return 1 - v if a else v


def _remote(src, dst, send, recv, dev):
    return pltpu.make_async_remote_copy(src_ref=src, dst_ref=dst, send_sem=send, recv_sem=recv, device_id=dev,
                                        device_id_type=MESH)


def _whole_shape(kind, g, r, c):
    return {"row": (g, NCHIP * r, c), "col": (g, r, NCHIP * c), "stk": (NCHIP, r, c)}[kind]


def _cols(j, c):
    return pl.ds(pl.multiple_of(j * c, 128), c)


def _whole_shard(kind, ref, j, r, c):
    if kind == "row":
        return ref.at[:, pl.ds(j * r, r), :]
    if kind == "col":
        return ref.at[:, :, _cols(j, c)]
    return ref.at[pl.ds(j, 1)]


def _whole_shard_rows(kind, ref, j, h, r, c):
    if kind == "row":
        return ref.at[:, pl.ds(j * r + h * (r // 2), r // 2), :]
    if kind == "col":
        return ref.at[:, pl.ds(h * (r // 2), r // 2), _cols(j, c)]
    return ref.at[pl.ds(j, 1), pl.ds(h * (r // 2), r // 2), :]


class _GatherJob(_Job):
    def __init__(self, names, shards, sink):
        self.mats = [MATS[n] for n in names]
        self.srcs = [shards[n] for n in names]
        self.news = [jax.ShapeDtypeStruct(_whole_shape(kind, g, r, c), BF16) for _, kind, g, r, c, _, _ in self.mats]
        n = len(names)
        self.scratch = [pltpu.SemaphoreType.DMA((6 * n,)), pltpu.SemaphoreType.DMA((6 * n,)), pltpu.SemaphoreType.DMA((n,))]
        self.names, self.sink = names, sink

    def _first(self, srcs, news, sems):
        send, recv, loc = sems
        x, y, c = _pos()
        own, first = [], []
        for w, (_, kind, g, r, cc, _, _) in enumerate(self.mats):
            own.append(pltpu.make_async_copy(srcs[w], _whole_shard(kind, news[w], 2 * x + y, r, cc), loc.at[w]))
            mine = srcs[w].at[:, pl.ds(c * (r // 2), r // 2), :]
            for k, (a, b) in enumerate(RELS):
                first.append(_remote(mine, _whole_shard_rows(kind, news[w], 2 * x + y, c, r, cc), send.at[6 * w + k],
                                     recv.at[6 * w + k], (_flip(x, a), _flip(y, b), c)))
        return own, first

    def start(self, srcs, dsts, news, sems):
        own, first = self._first(srcs, news, sems)
        for cp in own + first:
            cp.start()

    def finish(self, srcs, dsts, news, sems):
        send, recv, _ = sems
        x, y, c = _pos()
        sib = (x, y, 1 - c)
        own, first = self._first(srcs, news, sems)
        passed = []
        for w, (_, kind, g, r, cc, _, _) in enumerate(self.mats):
            for k, (a, b) in enumerate(RELS):
                land = _whole_shard_rows(kind, news[w], 2 * _flip(x, a) + _flip(y, b), c, r, cc)
                _remote(land, land, send.at[6 * w + k], recv.at[6 * w + k], sib).wait_recv()
                cp = _remote(land, land, send.at[6 * w + 3 + k], recv.at[6 * w + 3 + k], sib)
                cp.start()
                passed.append(cp)
        for w, (_, kind, g, r, cc, _, _) in enumerate(self.mats):
            for k, (a, b) in enumerate(RELS):
                land = _whole_shard_rows(kind, news[w], 2 * _flip(x, a) + _flip(y, b), 1 - c, r, cc)
                _remote(land, land, send.at[6 * w + 3 + k], recv.at[6 * w + 3 + k], sib).wait_recv()
        for cp in first + passed:
            cp.wait_send()
        for cp in own:
            cp.wait()

    def done(self, dsts, news):
        for n, a in zip(self.names, news):
            self.sink[n] = a


class _SwapJob(_Job):
    def __init__(self, build, ncopies, *, srcs=(), dsts=(), news=(), done=None):
        self.build, self.srcs, self.dsts, self.news, self._done = build, list(srcs), list(dsts), list(news), done
        self.scratch = [pltpu.SemaphoreType.DMA((ncopies,)), pltpu.SemaphoreType.DMA((ncopies,))]

    def start(self, srcs, dsts, news, sems):
        for cp in self.build(srcs, dsts, news, *sems):
            cp.start()

    def finish(self, srcs, dsts, news, sems):
        for cp in self.build(srcs, dsts, news, *sems):
            cp.wait()

    def done(self, dsts, news):
        if self._done is not None:
            self._done(dsts, news)


def _half_of_whole(kind, ref, h, r, c):
    if kind == "row":
        return ref.at[:, :, pl.ds(pl.multiple_of(h * (c // 2), 128), c // 2)]
    return ref.at[:, pl.ds(h * (r // 2), r // 2), :]


def _half_shape(kind, g, r, c):
    return {"row": (g, NCHIP * r, c // 2), "col": (g, r // 2, NCHIP * c), "stk": (NCHIP, r // 2, c)}[kind]


def _piece_shape(kind, g, r, c):
    return {"row": (g, r, c // 2), "col": (g, r // 2, c), "stk": (1, r // 2, c)}[kind]


def _piece_of_half(kind, ref, j, r, c):
    if kind == "row":
        return ref.at[:, pl.ds(j * r, r), :]
    if kind == "col":
        return ref.at[:, :, _cols(j, c)]
    return ref.at[pl.ds(j, 1)]


def _half_of_shard(kind, ref, h, r, c):
    if kind == "row":
        return ref.at[:, :, pl.ds(pl.multiple_of(h * (c // 2), 128), c // 2)]
    return ref.at[:, pl.ds(h * (r // 2), r // 2), :]


def _pair_sum(pack, core, mine, got, whole=True):
    name, kind, g, r, c, tr, _ = pack
    hs = _half_shape(kind, g, r, c)
    nb = hs[1] // tr

    def body(core_ref, a_ref, b_ref, o_ref):
        o_ref[...] = (a_ref[...].astype(F32) + b_ref[...].astype(F32)).astype(BF16)

    blk = (1, tr, hs[2])
    same = lambda gi, i, core_ref: (gi, i, 0)
    if not whole:
        a_map = same
    elif kind == "row":
        a_map = lambda gi, i, core_ref: (gi, i, core_ref[0])
    else:
        a_map = lambda gi, i, core_ref: (gi, core_ref[0] * nb + i, 0)
    return pl.pallas_call(
        body, name="pair_sum_" + name, out_shape=jax.ShapeDtypeStruct(hs, BF16),
        grid_spec=pltpu.PrefetchScalarGridSpec(
            num_scalar_prefetch=1, grid=(hs[0], nb),
            in_specs=[pl.BlockSpec(blk, a_map), pl.BlockSpec(blk, same)], out_specs=pl.BlockSpec(blk, same)),
        compiler_params=_cp(("parallel", "parallel")),
    )(core, mine, got)


def _shard_sum(pack, where, half, got):
    name, kind, g, r, c, _, tr = pack
    ps = _piece_shape(kind, g, r, c)
    nb = ps[1] // tr

    def body(where_ref, a_ref, b_ref, o_ref):
        o_ref[...] = a_ref[...].astype(F32) + ((b_ref[0].astype(F32) + b_ref[1].astype(F32)) + b_ref[2].astype(F32))

    blk = (1, tr, ps[2])
    if kind == "row":
        a_map = lambda gi, i, wr: (gi, wr[0] * nb + i, 0)
        o_map = lambda gi, i, wr: (gi, i, wr[1])
    elif kind == "col":
        a_map = lambda gi, i, wr: (gi, i, wr[0])
        o_map = lambda gi, i, wr: (gi, wr[1] * nb + i, 0)
    else:
        a_map = lambda gi, i, wr: (wr[0], i, 0)
        o_map = lambda gi, i, wr: (gi, wr[1] * nb + i, 0)
    return pl.pallas_call(
        body, name="shard_sum_" + name, out_shape=jax.ShapeDtypeStruct((g, r, c), F32),
        grid_spec=pltpu.PrefetchScalarGridSpec(
            num_scalar_prefetch=1, grid=(ps[0], nb),
            in_specs=[pl.BlockSpec(blk, a_map), pl.BlockSpec((3,) + blk, lambda gi, i, wr: (0, gi, i, 0))],
            out_specs=pl.BlockSpec(blk, o_map)),
        compiler_params=_cp(("parallel", "parallel")),
    )(where, half, got)


class _Plan:
    def __init__(self, shards, table):
        self.shards, self.table = shards, table
        self.whole, self.grad, self.got_a, self.half, self.got_b, self.sent_b, self.gshard = {}, {}, {}, {}, {}, {}, {}
        x, y, c = _pos()
        self.core = c
        self.core1 = c.reshape(1).astype(jnp.int32)
        self.where = jnp.stack([2 * x + y, c]).astype(jnp.int32)
        self._w_in = None
        self.send, self.keep = {}, {}

    def w(self, n):
        if n != "w_in":
            return self.whole[n][0]
        if self._w_in is None:
            self._w_in = _to_kernel_cols(self.whole[n].transpose(1, 0, 2).reshape(D, IN_DIM))
        return self._w_in

    def g(self, n, a):
        self.grad[n] = a[None]

    def g_half(self, n, which, a):
        a = _from_kernel_cols(a).reshape(D // 2, NCHIP, IN_DIM // NCHIP).transpose(1, 0, 2)
        (self.send if which == "send" else self.keep)[n] = a

    def jobs(self, tag):
        out = []
        for spec in self.table.get(tag, ()):
            out += getattr(self, "_" + spec[0])(*spec[1:])
        return out

    def run(self, name, jobs):
        if jobs:
            _call(lambda: None, jobs=jobs, name=name, out_shape=[], in_specs=[], out_specs=[])()

    def _gather(self, names):
        return [_GatherJob(names, self.shards, self.whole)]

    def _rs_a(self, names):
        mats = [MATS[n] for n in names]

        def build(srcs, dsts, news, send, recv):
            x, y, c = _pos()
            return [_remote(srcs[i] if names[i] in self.send else _half_of_whole(kind, srcs[i], 1 - c, r, cc), news[i],
                            send.at[i], recv.at[i], (x, y, 1 - c))
                    for i, (_, kind, g, r, cc, _, _) in enumerate(mats)]

        def done(dsts, news):
            self.got_a.update(zip(names, news))

        return [_SwapJob(build, len(names), srcs=[self.send.get(n, self.grad.get(n)) for n in names], done=done,
                         news=[jax.ShapeDtypeStruct(_half_shape(kind, g, r, c), BF16) for _, kind, g, r, c, _, _ in mats])]

    def _rs_b(self, names, ks=(0, 1, 2)):
        return [self._rs_b_one(n, ks) for n in names]

    def _rs_b_one(self, n, ks):
        _, kind, g, r, cc, _, _ = MATS[n]
        if n not in self.half:
            if n in self.keep:
                self.half[n] = _pair_sum(MATS[n], self.core1, self.keep[n], self.got_a[n], whole=False)
            else:
                self.half[n] = _pair_sum(MATS[n], self.core1, self.grad[n], self.got_a[n])

        def build(srcs, dsts, news, send, recv):
            x, y, c = _pos()
            land = (dsts or news)[0]
            cps = []
            for i, k in enumerate(ks):
                px, py = _flip(x, RELS[k][0]), _flip(y, RELS[k][1])
                cps.append(_remote(_piece_of_half(kind, srcs[0], 2 * px + py, r, cc), land.at[k], send.at[i], recv.at[i],
                                   (px, py, c)))
            return cps

        def done(dsts, news):
            self.got_b[n] = (dsts or news)[0]
            self.sent_b[n] = self.sent_b.get(n, ()) + tuple(ks)

        if n in self.got_b:
            return _SwapJob(build, len(ks), srcs=[self.half[n]], dsts=[self.got_b[n]], done=done)
        shape = jax.ShapeDtypeStruct((3,) + _piece_shape(kind, g, r, cc), BF16)
        return _SwapJob(build, len(ks), srcs=[self.half[n]], news=[shape], done=done)

    def _rs_c(self, names):
        mats = [MATS[n] for n in names]
        for n in names:
            assert sorted(self.sent_b[n]) == [0, 1, 2], (n, self.sent_b[n])
        parts = [_shard_sum(MATS[n], self.where, self.half[n], self.got_b[n]) for n in names]

        def build(srcs, dsts, news, send, recv):
            x, y, c = _pos()
            cps = []
            for i, (_, kind, g, r, cc, _, _) in enumerate(mats):
                mine = _half_of_shard(kind, dsts[i], c, r, cc)
                cps.append(_remote(mine, mine, send.at[i], recv.at[i], (x, y, 1 - c)))
            return cps

        def done(dsts, news):
            self.gshard.update(zip(names, dsts))

        return [_SwapJob(build, len(names), dsts=parts, done=done)]

    def finish(self, n):
        if n not in self.got_a:
            self.run("rs_a_" + n, self._rs_a((n,)))
        left = tuple(k for k in range(3) if k not in self.sent_b.get(n, ()))
        if left:
            self.run("rs_b_" + n, self._rs_b((n,), left))
        if n not in self.gshard:
            self.run("rs_c_" + n, self._rs_c((n,)))
        return self.gshard[n]


TABLE = {
    "gather_w_in": (("gather", ("w_in",)),),
    "mm_in": (("gather", ("w_attn_br", "w_ssd_br")),),
    "attn_fwd": (("gather", ("w_o",)),),
    "ssd_fwd": (("gather", ("w_gate",)),),
    "mm_ssd_br": (("gather", ("w_up",)),),
    "mm_o": (("gather", ("w_down",)),),
    "swiglu_fwd": (("gather", ("w_ple_gate", "w_ple_proj")),),
    "mm_de": (("rs_a", ("w_ple_proj", "w_ple_gate")),),
    "mm_d_down": (("rs_b", ("w_ple_proj", "w_ple_gate")),),
    "mm_dact": (("rs_a", ("w_down",)),),
    "swiglu_bwd": (("rs_c", ("w_ple_proj", "w_ple_gate")),),
    "mm_df_gate": (("rs_a", ("w_gate", "w_up")),),
    "mm_dmerged": (("rs_a", ("w_o",)),),
    "mm_dyn": (("rs_a", ("w_attn_br", "w_ssd_br")),),
    "attn_bwd": (("rs_b", ("w_down",)),),
    "gated_norm_bwd": (("rs_c", ("w_down",)),),
    "ssd_bwd": (("rs_b", ("w_gate", "w_up")),),
    "conv_bwd": (("rs_b", ("w_o",)),),
    "mm_d_in_send": (("rs_b", ("w_attn_br", "w_ssd_br")), ("rs_c", ("w_gate", "w_up"))),
    "mm_d_in_keep": (("rs_a", ("w_in",)), ("rs_c", ("w_o",))),
    "mm_du": (("rs_b", ("w_in",)),),
    "norm_mix_bwd": (("rs_c", ("w_attn_br", "w_ssd_br")),),
}


NDEV = 8


def _allreduce_small(v, *, name):
    rows = v.shape[0]

    def body(v_ref, o_ref, slots, send, recv):
        x, y, c = _pos()
        me = 4 * x + 2 * y + c
        slots[me] = v_ref[...]
        cps = []
        for k in range(1, NDEV):
            peer = (_flip(x, k & 4), _flip(y, k & 2), _flip(c, k & 1))
            cp = _remote(v_ref, slots.at[me], send.at[k - 1], recv.at[k - 1], peer)
            cp.start()
            cps.append(cp)
        for cp in cps:
            cp.wait()
        acc = slots[0]
        for s in range(1, NDEV):
            acc = acc + slots[s]
        o_ref[...] = acc

    return pl.pallas_call(
        body, name=name, out_shape=jax.ShapeDtypeStruct((rows, 128), F32),
        in_specs=[pl.BlockSpec(memory_space=pltpu.VMEM)], out_specs=pl.BlockSpec(memory_space=pltpu.VMEM),
        scratch_shapes=[pltpu.VMEM((NDEV, rows, 128), F32), pltpu.SemaphoreType.DMA((NDEV - 1,)),
                        pltpu.SemaphoreType.DMA((NDEV - 1,))],
    )(v)


def _adamw(w, g, m, v, *, name, tr=None, tc=None, jobs=()):
    r, c = w.shape
    tr = r if tr is None else tr
    c1 = 1.0 / (1.0 - B1 ** STEP)
    c2 = 1.0 / (1.0 - B2 ** STEP)

    def body(w_ref, g_ref, m_ref, v_ref, d_ref, mo_ref, vo_ref):
        gv = g_ref[...]
        mn = B1 * m_ref[...] + (1.0 - B1) * gv
        vn = B2 * v_ref[...] + (1.0 - B2) * (gv * gv)
        mo_ref[...] = mn
        vo_ref[...] = vn
        d_ref[...] = -LR * ((mn * c1) / (jnp.sqrt(vn * c2) + AEPS) + WD * w_ref[...])

    if tc is None:
        blk, grid = pl.BlockSpec((tr, c), lambda i: (i, 0)), (r // tr,)
    else:
        blk, grid = pl.BlockSpec((r, tc), lambda i: (0, i)), (c // tc,)
    o = jax.ShapeDtypeStruct((r, c), F32)
    return _call(
        body, jobs=jobs, name=name, out_shape=(o, o, o), grid=grid, in_specs=[blk] * 4, out_specs=(blk, blk, blk),
        compiler_params=_cp(("parallel",)),
    )(w, g, m, v)


WEIGHTS = ("g_mix", "w_in", "conv_w", "conv_b", "dt_bias", "a_log", "d_skip", "g_ssd", "sinks", "w_attn_br", "w_ssd_br",
           "w_o", "g_ffn", "w_gate", "w_up", "w_down", "g_ple", "w_ple_gate", "w_ple_proj", "g_final")
BIG = {
    "w_gate": 256, "w_up": 256, "w_down": 128, "w_ssd_br": 128, "w_o": 128, "w_ple_gate": 128, "w_attn_br": 256,
    "w_ple_proj": 256, "w_in": None,
}
SMALL = tuple(n for n in WEIGHTS if n not in BIG)


def _pack_small(parts):
    rows = []
    for a in parts:
        a = a.reshape(-1)
        rows.append(jnp.pad(a, (0, -a.shape[0] % 128)).reshape(-1, 128))
    out = jnp.concatenate(rows, axis=0)
    return jnp.pad(out, ((0, -out.shape[0] % 8), (0, 0)))


def _unpack_small(packed, shapes):
    out, r = [], 0
    for s in shapes:
        n = int(np.prod(s))
        nr = -(-n // 128)
        out.append(packed[r:r + nr].reshape(-1)[:n].reshape(s))
        r += nr
    return out


def kernel(x, p, positions, g_mix, w_in, conv_w, conv_b, dt_bias, a_log, d_skip, g_ssd, sinks, w_attn_br, w_ssd_br, w_o, g_ffn, w_gate, w_up, w_down, g_ple, w_ple_gate, w_ple_proj, g_final, loss_target, m_g_mix, m_w_in, m_conv_w, m_conv_b, m_dt_bias, m_a_log, m_d_skip, m_g_ssd, m_sinks, m_w_attn_br, m_w_ssd_br, m_w_o, m_g_ffn, m_w_gate, m_w_up, m_w_down, m_g_ple, m_w_ple_gate, m_w_ple_proj, m_g_final, v_g_mix, v_w_in, v_conv_w, v_conv_b, v_dt_bias, v_a_log, v_d_skip, v_g_ssd, v_sinks, v_w_attn_br, v_w_ssd_br, v_w_o, v_g_ffn, v_w_gate, v_w_up, v_w_down, v_g_ple, v_w_ple_gate, v_w_ple_proj, v_g_final):
    w = dict(zip(WEIGHTS, (g_mix, w_in, conv_w, conv_b, dt_bias, a_log, d_skip, g_ssd, sinks, w_attn_br, w_ssd_br, w_o,
                           g_ffn, w_gate, w_up, w_down, g_ple, w_ple_gate, w_ple_proj, g_final)))
    m = dict(zip(WEIGHTS, (m_g_mix, m_w_in, m_conv_w, m_conv_b, m_dt_bias, m_a_log, m_d_skip, m_g_ssd, m_sinks, m_w_attn_br,
                           m_w_ssd_br, m_w_o, m_g_ffn, m_w_gate, m_w_up, m_w_down, m_g_ple, m_w_ple_gate, m_w_ple_proj,
                           m_g_final)))
    v = dict(zip(WEIGHTS, (v_g_mix, v_w_in, v_conv_w, v_conv_b, v_dt_bias, v_a_log, v_d_skip, v_g_ssd, v_sinks, v_w_attn_br,
                           v_w_ssd_br, v_w_o, v_g_ffn, v_w_gate, v_w_up, v_w_down, v_g_ple, v_w_ple_gate, v_w_ple_proj,
                           v_g_final)))
    xi, yi, ci = _pos()
    chip = 2 * xi + yi
    t = x.shape[1]
    cshard = CONV // NCHIP

    plan = _Plan({n: w[n].astype(BF16) for n in MATS}, TABLE)
    plan.run("gather_w_in", plan.jobs("gather_w_in"))
    placed = lax.dynamic_update_slice(jnp.zeros((CW, CONV), F32), w["conv_w"][0], (0, chip * cshard))
    conv_whole = _allreduce_small(jnp.where(ci == 0, placed, 0.0).reshape(-1, 128), name="gather_conv_w").reshape(CW, CONV)

    small = {n: w[n] for n in ("g_mix", "conv_b", "dt_bias", "a_log", "d_skip", "g_ssd", "sinks", "g_ffn", "g_ple", "g_final")}
    small["conv_w"] = conv_whole
    loss8, grad_x, gs = _local_step(x[0], p[0, 0], positions, loss_target[0], small, plan)

    order = ("g_mix", "conv_b", "dt_bias", "a_log", "d_skip", "g_ssd", "sinks", "g_ffn", "g_ple", "g_final", "conv_w")
    summed = _allreduce_small(_pack_small([loss8[0, :1]] + [gs[n] for n in order]), name="sum_small")
    parts = _unpack_small(summed, [(1,)] + [w[n].shape for n in order[:-1]] + [(CW, CONV)])
    loss = parts[0][0]
    grad = dict(zip(order, parts[1:]))
    grad["conv_w"] = lax.dynamic_slice(grad["conv_w"], (0, chip * cshard), (CW, cshard))[None]

    delta, new_m, new_v = {}, {}, {}
    for n, tr in BIG.items():
        grad[n] = plan.finish(n)
        if n == "w_in":
            d_, m_, v_ = _adamw(w[n][0].T, grad[n][0].T, m[n][0].T, v[n][0].T, tc=128, name="adamw_" + n)
            d_, m_, v_ = d_.T, m_.T, v_.T
        else:
            d_, m_, v_ = _adamw(w[n][0], grad[n][0], m[n][0], v[n][0], tr=tr, name="adamw_" + n)
        delta[n], new_m[n], new_v[n] = d_[None], m_[None], v_[None]
    shapes = [w[n].shape for n in SMALL]
    d_, m_, v_ = _adamw(_pack_small([w[n] for n in SMALL]), _pack_small([grad[n] for n in SMALL]),
                        _pack_small([m[n] for n in SMALL]), _pack_small([v[n] for n in SMALL]), tr=None, name="adamw_small")
    for n, a, b, c_ in zip(SMALL, _unpack_small(d_, shapes), _unpack_small(m_, shapes), _unpack_small(v_, shapes)):
        delta[n], new_m[n], new_v[n] = a, b, c_

    return (loss, grad_x[None], *[grad[n] for n in WEIGHTS], *[delta[n] for n in WEIGHTS],
            *[new_m[n] for n in WEIGHTS], *[new_v[n] for n in WEIGHTS])
```

```python
import functools

import jax
import jax.numpy as jnp
import numpy as np
from jax import lax
from jax.experimental import pallas as pl
from jax.experimental.pallas import tpu as pltpu

F32 = jnp.float32
BF16 = jnp.bfloat16
MESH = pl.DeviceIdType.MESH

D = 2048
HD = 64
NQH = 16
NKV = 4
QD = NQH * HD
KVD = NKV * HD
DI = 2048
NH = 32
NG = 4
NS = 128
CW = 4
L = 128
CONV = DI + 2 * NG * NS
FFN = 5632
PLE = 256
IN_DIM = QD + 2 * KVD + DI + CONV + NH + 2 * D
EPS = 1e-6
SSM_EPS = 1e-5
ROPE_THETA = 10000.0
LR, B1, B2, AEPS, WD, STEP = 0.001, 0.9, 0.999, 1e-08, 0.01, 10

O_GA, O_GS, O_Z, O_XBC, O_Q, O_K, O_V, O_DT = 0, 2048, 4096, 6144, 9216, 10240, 10496, 10752
DT_PAD = 512
NP = O_DT + DT_PAD
R_Q, R_K, R_V, R_Z, R_XBC, R_DT, R_GA, R_GS = 0, 1024, 1280, 1536, 3584, 6656, 6688, 8736

NCHIP = 4
VMEM_LIMIT = 52 * 1024 * 1024
NEG = -1e30


def _cp(sem=None):
    return pltpu.CompilerParams(dimension_semantics=sem, vmem_limit_bytes=VMEM_LIMIT)


def _dot(a, b):
    return lax.dot_general(a, b, (((1,), (0,)), ((), ())), preferred_element_type=F32)


def _dot_nt(a, b):
    return lax.dot_general(a, b, (((1,), (1,)), ((), ())), preferred_element_type=F32)


def _dot_tn(a, b):
    return lax.dot_general(a, b, (((0,), (0,)), ((), ())), preferred_element_type=F32)


def _sigmoid(x):
    return 1.0 / (1.0 + jnp.exp(-x))


ANY = pl.BlockSpec(memory_space=pl.ANY)


class _Job:
    srcs, dsts, news, scratch = (), (), (), ()
    has_mid = False

    def start(self, srcs, dsts, news, sems):
        raise NotImplementedError

    def mid(self, srcs, dsts, news, sems):
        pass

    def finish(self, srcs, dsts, news, sems):
        raise NotImplementedError

    def done(self, dsts, news):
        pass


def _call(body, *, jobs=(), name, out_shape, in_specs, out_specs, grid=(), scratch_shapes=(), compiler_params=None,
          aliases=None):
    jobs = [j for j in jobs if j is not None]
    aliases = dict(aliases or {})
    if not jobs:
        return pl.pallas_call(body, name=name, out_shape=out_shape, in_specs=in_specs, out_specs=out_specs, grid=grid,
                              scratch_shapes=scratch_shapes, compiler_params=compiler_params,
                              input_output_aliases=aliases)
    single = not isinstance(out_shape, (tuple, list))
    outs = [out_shape] if single else list(out_shape)
    ospecs = [out_specs] if single else list(out_specs)
    n_in, n_out, n_scr = len(in_specs), len(outs), len(scratch_shapes)
    srcs = [a for j in jobs for a in j.srcs]
    dsts = [a for j in jobs for a in j.dsts]
    news = [a for j in jobs for a in j.news]
    sems = [a for j in jobs for a in j.scratch]

    def wrapped(*refs):
        pos = n_in + len(srcs) + len(dsts)
        ins, jsrc = refs[:n_in], refs[n_in:n_in + len(srcs)]
        o_refs = refs[pos:pos + n_out]
        pos += n_out
        jdst, jnew = refs[pos:pos + len(dsts)], refs[pos + len(dsts):pos + len(dsts) + len(news)]
        pos += len(dsts) + len(news)
        scr, jsem = refs[pos:pos + n_scr], refs[pos + n_scr:]

        def run(which):
            a = b = c = d = 0
            for j in jobs:
                getattr(j, which)(jsrc[a:a + len(j.srcs)], jdst[b:b + len(j.dsts)], jnew[c:c + len(j.news)],
                                  jsem[d:d + len(j.scratch)])
                a, b, c, d = a + len(j.srcs), b + len(j.dsts), c + len(j.news), d + len(j.scratch)

        if not grid:
            run("start")
            run("mid")
            body(*ins, *o_refs, *scr)
            run("finish")
            return
        step = functools.reduce(lambda acc, a: acc * grid[a] + pl.program_id(a), range(len(grid)), 0)
        steps = int(np.prod(grid))
        pl.when(step == 0)(lambda: run("start"))
        if any(j.has_mid for j in jobs):
            pl.when(step == steps // 3)(lambda: run("mid"))
        body(*ins, *o_refs, *scr)
        pl.when(step == steps - 1)(lambda: run("finish"))

    call = pl.pallas_call(
        wrapped, name=name,
        out_shape=outs + [jax.ShapeDtypeStruct(a.shape, a.dtype) for a in dsts] + news,
        in_specs=list(in_specs) + [ANY] * (len(srcs) + len(dsts)),
        out_specs=ospecs + [ANY] * (len(dsts) + len(news)),
        grid=grid, scratch_shapes=list(scratch_shapes) + sems,
        input_output_aliases={**aliases, **{n_in + len(srcs) + i: n_out + i for i in range(len(dsts))}},
        compiler_params=_cp(("arbitrary",) * len(grid) if grid else None))

    def run_call(*args):
        res = call(*args, *srcs, *dsts)
        b, c = n_out, n_out + len(dsts)
        for j in jobs:
            j.done(res[b:b + len(j.dsts)], res[c:c + len(j.news)])
            b, c = b + len(j.dsts), c + len(j.news)
        return res[0] if single else tuple(res[:n_out])

    return run_call


def _matmul(a, b, *, ta=False, tb=False, out_dtype=F32, add=None, tm, tn, tk, name, jobs=()):
    k, m = a.shape if ta else a.shape[::-1]
    n = b.shape[0] if tb else b.shape[1]
    assert (b.shape[1] if tb else b.shape[0]) == k and not (ta and tb)
    assert m % tm == 0 and n % tn == 0 and k % tk == 0, (name, a.shape, b.shape)
    nk = k // tk
    has_add = add is not None

    def body(*refs):
        a_ref, b_ref = refs[0], refs[1]
        add_ref = refs[2] if has_add else None
        o_ref = refs[3] if has_add else refs[2]
        av = a_ref[...].astype(BF16)
        bv = b_ref[...].astype(BF16)
        part = _dot_tn(av, bv) if ta else _dot_nt(av, bv) if tb else _dot(av, bv)

        def finish(r):
            if has_add:
                r = r + add_ref[...]
            o_ref[...] = r.astype(out_dtype)

        if nk == 1:
            finish(part)
        else:
            acc_ref = refs[-1]
            kk = pl.program_id(2)

            @pl.when(kk == 0)
            def _():
                acc_ref[...] = part

            @pl.when(kk > 0)
            def _():
                acc_ref[...] += part

            @pl.when(kk == nk - 1)
            def _():
                finish(acc_ref[...])

    in_specs = [pl.BlockSpec((tk, tm), lambda i, j, kk: (kk, i)) if ta else pl.BlockSpec((tm, tk), lambda i, j, kk: (i, kk)),
                pl.BlockSpec((tn, tk), lambda i, j, kk: (j, kk)) if tb
                else pl.BlockSpec((tk, tn), lambda i, j, kk: (kk, j))]
    args = [a, b]
    if has_add:
        in_specs.append(pl.BlockSpec((tm, tn), lambda i, j, kk: (i, j)))
        args.append(add)
    return _call(
        body, jobs=jobs, name=name,
        out_shape=jax.ShapeDtypeStruct((m, n), out_dtype),
        grid=(m // tm, n // tn, nk),
        in_specs=in_specs,
        out_specs=pl.BlockSpec((tm, tn), lambda i, j, kk: (i, j)),
        scratch_shapes=[pltpu.VMEM((tm, tn), F32)] if nk > 1 else [],
        compiler_params=_cp(("parallel", "parallel", "arbitrary")),
    )(*args)


ROWS = 256


def _rmsnorm_fwd(x, g, *, name):
    t, d = x.shape

    def body(x_ref, g_ref, o_ref):
        xv = x_ref[...]
        r = lax.rsqrt(jnp.mean(xv * xv, axis=-1, keepdims=True) + EPS)
        o_ref[...] = (xv * r * g_ref[...]).astype(BF16)

    return pl.pallas_call(
        body, name=name, out_shape=jax.ShapeDtypeStruct((t, d), BF16), grid=(t // ROWS,),
        in_specs=[pl.BlockSpec((ROWS, d), lambda i: (i, 0)), pl.BlockSpec((1, d), lambda i: (0, 0))],
        out_specs=pl.BlockSpec((ROWS, d), lambda i: (i, 0)), compiler_params=_cp(("parallel",)),
    )(x, g)


def _rmsnorm_bwd(x, g, dy, dres, *, name, jobs=()):
    t, d = x.shape

    def body(x_ref, g_ref, dy_ref, dres_ref, dx_ref, dxb_ref, dg_ref):
        xv = x_ref[...]
        r = lax.rsqrt(jnp.mean(xv * xv, axis=-1, keepdims=True) + EPS)
        xh = xv * r
        dyv = dy_ref[...]
        dxh = dyv * g_ref[...]
        dx = r * (dxh - xh * jnp.mean(dxh * xh, axis=-1, keepdims=True))
        tot = dres_ref[...] + dx
        dx_ref[...] = tot
        dxb_ref[...] = tot.astype(BF16)

        @pl.when(pl.program_id(0) == 0)
        def _():
            dg_ref[...] = jnp.zeros_like(dg_ref)

        dg_ref[...] += jnp.broadcast_to(jnp.sum(dyv * xh, axis=0, keepdims=True), dg_ref.shape)

    row = pl.BlockSpec((ROWS, d), lambda i: (i, 0))
    return _call(
        body, jobs=jobs, name=name,
        out_shape=(jax.ShapeDtypeStruct((t, d), F32), jax.ShapeDtypeStruct((t, d), BF16),
                   jax.ShapeDtypeStruct((8, d), F32)),
        grid=(t // ROWS,),
        in_specs=[row, pl.BlockSpec((1, d), lambda i: (0, 0)), row, row],
        out_specs=(row, row, pl.BlockSpec((8, d), lambda i: (0, 0))),
        compiler_params=_cp(("arbitrary",)),
    )(x, g, dy, dres)


def _final(h2, pgl, pp, target, g_final, *, name):
    t, d = h2.shape

    def body(h2_ref, pgl_ref, pp_ref, tg_ref, g_ref, dh3_ref, dpgl_ref, dpp_ref, loss_ref, dg_ref):
        s = _sigmoid(pgl_ref[...])
        ppv = pp_ref[...]
        h3 = h2_ref[...] + s * ppv
        r = lax.rsqrt(jnp.mean(h3 * h3, axis=-1, keepdims=True) + EPS)
        xh = h3 * r
        gv = g_ref[...]
        err = xh * gv - tg_ref[...]
        dyv = err * (1.0 / d)
        dxh = dyv * gv
        dh3 = r * (dxh - xh * jnp.mean(dxh * xh, axis=-1, keepdims=True))
        dh3_ref[...] = dh3
        dpp_ref[...] = (dh3 * s).astype(BF16)
        dpgl_ref[...] = (dh3 * ppv * s * (1.0 - s)).astype(BF16)

        @pl.when(pl.program_id(0) == 0)
        def _():
            loss_ref[...] = jnp.zeros_like(loss_ref)
            dg_ref[...] = jnp.zeros_like(dg_ref)

        part = 0.5 * jnp.sum(jnp.mean(err * err, axis=-1, keepdims=True), axis=0, keepdims=True)
        loss_ref[...] += jnp.broadcast_to(part, loss_ref.shape)
        dg_ref[...] += jnp.broadcast_to(jnp.sum(dyv * xh, axis=0, keepdims=True), dg_ref.shape)

    row = pl.BlockSpec((ROWS, d), lambda i: (i, 0))
    return pl.pallas_call(
        body, name=name,
        out_shape=(jax.ShapeDtypeStruct((t, d), F32), jax.ShapeDtypeStruct((t, d), BF16),
                   jax.ShapeDtypeStruct((t, d), BF16), jax.ShapeDtypeStruct((8, 128), F32),
                   jax.ShapeDtypeStruct((8, d), F32)),
        grid=(t // ROWS,),
        in_specs=[row, row, row, row, pl.BlockSpec((1, d), lambda i: (0, 0))],
        out_specs=(row, row, row, pl.BlockSpec((8, 128), lambda i: (0, 0)), pl.BlockSpec((8, d), lambda i: (0, 0))),
        compiler_params=_cp(("arbitrary",)),
    )(h2, pgl, pp, target, g_final)


def _merge_fwd(proj, out_a, out_s, *, name):
    t = proj.shape[0]

    def body(ga_ref, gs_ref, a_ref, s_ref, o_ref):
        o_ref[...] = (_sigmoid(ga_ref[...]) * a_ref[...] + _sigmoid(gs_ref[...]) * s_ref[...]).astype(BF16)

    row = pl.BlockSpec((ROWS, D), lambda i: (i, 0))
    return pl.pallas_call(
        body, name=name, out_shape=jax.ShapeDtypeStruct((t, D), BF16), grid=(t // ROWS,),
        in_specs=[pl.BlockSpec((ROWS, D), lambda i: (i, O_GA // D)), pl.BlockSpec((ROWS, D), lambda i: (i, O_GS // D)),
                  row, row],
        out_specs=row, compiler_params=_cp(("parallel",)),
    )(proj, proj, out_a, out_s)


def _merge_bwd(proj, out_a, out_s, dmerged, *, name):
    t = proj.shape[0]
    assert O_GA == 0 and O_GS == D

    def body(ga_ref, gs_ref, a_ref, s_ref, dm_ref, da_ref, ds_ref, dp_ref):
        sa = _sigmoid(ga_ref[...])
        ss = _sigmoid(gs_ref[...])
        dm = dm_ref[...]
        da_ref[...] = (dm * sa).astype(BF16)
        ds_ref[...] = (dm * ss).astype(BF16)
        dp_ref[:, :D] = (dm * a_ref[...] * sa * (1.0 - sa)).astype(BF16)
        dp_ref[:, D:] = (dm * s_ref[...] * ss * (1.0 - ss)).astype(BF16)

    row = pl.BlockSpec((ROWS, D), lambda i: (i, 0))
    o = jax.ShapeDtypeStruct((t, D), BF16)
    return pl.pallas_call(
        body, name=name, out_shape=(o, o, jax.ShapeDtypeStruct((t, NP), BF16)), grid=(t // ROWS,),
        in_specs=[pl.BlockSpec((ROWS, D), lambda i: (i, O_GA // D)), pl.BlockSpec((ROWS, D), lambda i: (i, O_GS // D)),
                  row, row, row],
        out_specs=(row, row, pl.BlockSpec((ROWS, 2 * D), lambda i: (i, 0))), compiler_params=_cp(("parallel",)),
    )(proj, proj, out_a, out_s, dmerged)


def _swiglu_fwd(f, w_gate, w_up, *, name, tn=256, jobs=()):
    t, d = f.shape
    n = w_gate.shape[1]

    def body(f_ref, wg_ref, wu_ref, g_ref, u_ref, a_ref):
        fv = f_ref[...]
        g = _dot(fv, wg_ref[...])
        u = _dot(fv, wu_ref[...])
        g_ref[...] = g
        u_ref[...] = u
        a_ref[...] = (g * _sigmoid(g) * u).astype(BF16)

    col = pl.BlockSpec((t, tn), lambda j: (0, j))
    wcol = pl.BlockSpec((d, tn), lambda j: (0, j))
    return _call(
        body, jobs=jobs, name=name,
        out_shape=(jax.ShapeDtypeStruct((t, n), F32), jax.ShapeDtypeStruct((t, n), F32),
                   jax.ShapeDtypeStruct((t, n), BF16)),
        grid=(n // tn,),
        in_specs=[pl.BlockSpec((t, d), lambda j: (0, 0)), wcol, wcol],
        out_specs=(col, col, col), compiler_params=_cp(("parallel",)),
    )(f, w_gate, w_up)


def _swiglu_bwd(gate, up, dact, *, name, tc=1408, jobs=()):
    t, n = gate.shape

    def body(g_ref, u_ref, da_ref, dg_ref, du_ref):
        g = g_ref[...]
        s = _sigmoid(g)
        da = da_ref[...]
        du_ref[...] = (da * g * s).astype(BF16)
        dg_ref[...] = (da * u_ref[...] * s * (1.0 + g * (1.0 - s))).astype(BF16)

    blk = pl.BlockSpec((ROWS, tc), lambda i, j: (i, j))
    o = jax.ShapeDtypeStruct((t, n), BF16)
    return _call(
        body, jobs=jobs, name=name, out_shape=(o, o), grid=(t // ROWS, n // tc),
        in_specs=[blk, blk, blk], out_specs=(blk, blk), compiler_params=_cp(("parallel", "parallel")),
    )(gate, up, dact)


def _gated_norm_fwd(y_pre, proj, g_ssd, *, name):
    t = y_pre.shape[0]

    def body(y_ref, z_ref, g_ref, o_ref):
        z = z_ref[...]
        v = y_ref[...] * z * _sigmoid(z)
        r = lax.rsqrt(jnp.mean(v * v, axis=-1, keepdims=True) + SSM_EPS)
        o_ref[...] = (v * r * g_ref[...]).astype(BF16)

    row = pl.BlockSpec((ROWS, DI), lambda i: (i, 0))
    return pl.pallas_call(
        body, name=name, out_shape=jax.ShapeDtypeStruct((t, DI), BF16), grid=(t // ROWS,),
        in_specs=[row, pl.BlockSpec((ROWS, DI), lambda i: (i, O_Z // DI)), pl.BlockSpec((1, DI), lambda i: (0, 0))],
        out_specs=row, compiler_params=_cp(("parallel",)),
    )(y_pre, proj, g_ssd)


def _gated_norm_bwd(y_pre, proj, g_ssd, dyn, dproj, *, name, jobs=()):
    t = y_pre.shape[0]

    def body(y_ref, z_ref, g_ref, dyn_ref, _, dy_ref, dz_ref, dg_ref):
        z = z_ref[...]
        s = _sigmoid(z)
        sz = z * s
        yv = y_ref[...]
        v = yv * sz
        r = lax.rsqrt(jnp.mean(v * v, axis=-1, keepdims=True) + SSM_EPS)
        vh = v * r
        dn = dyn_ref[...]
        dvh = dn * g_ref[...]
        dv = r * (dvh - vh * jnp.mean(dvh * vh, axis=-1, keepdims=True))
        dy_ref[...] = dv * sz
        dz_ref[...] = (dv * yv * s * (1.0 + z * (1.0 - s))).astype(BF16)

        @pl.when(pl.program_id(0) == 0)
        def _():
            dg_ref[...] = jnp.zeros_like(dg_ref)

        dg_ref[...] += jnp.broadcast_to(jnp.sum(dn * vh, axis=0, keepdims=True), dg_ref.shape)

    row = pl.BlockSpec((ROWS, DI), lambda i: (i, 0))
    return _call(
        body, jobs=jobs, name=name,
        out_shape=(jax.ShapeDtypeStruct((t, DI), F32), jax.ShapeDtypeStruct(dproj.shape, BF16),
                   jax.ShapeDtypeStruct((8, DI), F32)),
        grid=(t // ROWS,),
        in_specs=[row, pl.BlockSpec((ROWS, DI), lambda i: (i, O_Z // DI)), pl.BlockSpec((1, DI), lambda i: (0, 0)), row, ANY],
        out_specs=(row, pl.BlockSpec((ROWS, DI), lambda i: (i, O_Z // DI)), pl.BlockSpec((8, DI), lambda i: (0, 0))),
        compiler_params=_cp(("arbitrary",)), aliases={4: 1},
    )(y_pre, proj, g_ssd, dyn, dproj)


CONV_TC = 512


def _shift_down(x, s, row):
    if s == 0:
        return x
    return jnp.where(row >= s, pltpu.roll(x, s, 0), 0.0)


def _shift_up(x, s, row, t):
    if s == 0:
        return x
    return jnp.where(row < t - s, pltpu.roll(x, t - s, 0), 0.0)


def _conv_fwd(proj, conv_w, conv_b, *, name):
    t = proj.shape[0]

    def body(x_ref, w_ref, b_ref, o_ref):
        x = x_ref[...]
        row = lax.broadcasted_iota(jnp.int32, x.shape, 0)
        pre = jnp.broadcast_to(b_ref[...], x.shape)
        for k in range(CW):
            pre = pre + w_ref[k:k + 1, :] * _shift_down(x, CW - 1 - k, row)
        o_ref[...] = pre * _sigmoid(pre)

    return pl.pallas_call(
        body, name=name, out_shape=jax.ShapeDtypeStruct((t, CONV), F32), grid=(CONV // CONV_TC,),
        in_specs=[pl.BlockSpec((t, CONV_TC), lambda j: (0, O_XBC // CONV_TC + j)),
                  pl.BlockSpec((CW, CONV_TC), lambda j: (0, j)), pl.BlockSpec((1, CONV_TC), lambda j: (0, j))],
        out_specs=pl.BlockSpec((t, CONV_TC), lambda j: (0, j)), compiler_params=_cp(("parallel",)),
    )(proj, conv_w, conv_b)


def _conv_bwd(proj, conv_w, conv_b, dxs, db, dc, dproj, *, name, jobs=()):
    t = proj.shape[0]
    nx = DI // CONV_TC
    assert NG * NS == CONV_TC

    def body(x_ref, w_ref, b_ref, dxs_ref, db_ref, dc_ref, _, dx_ref, dw_ref, dbias_ref):
        j = pl.program_id(0)
        x = x_ref[...]
        row = lax.broadcasted_iota(jnp.int32, x.shape, 0)
        xs = [_shift_down(x, CW - 1 - k, row) for k in range(CW)]
        pre = jnp.broadcast_to(b_ref[...], x.shape)
        for k in range(CW):
            pre = pre + w_ref[k:k + 1, :] * xs[k]
        s = _sigmoid(pre)
        da = jnp.where(j < nx, dxs_ref[...], jnp.where(j == nx, db_ref[...], dc_ref[...]))
        dpre = da * s * (1.0 + pre * (1.0 - s))
        dx = jnp.zeros_like(x)
        row8 = lax.broadcasted_iota(jnp.int32, dw_ref.shape, 0)
        dw = jnp.zeros(dw_ref.shape, F32)
        for k in range(CW):
            dx = dx + w_ref[k:k + 1, :] * _shift_up(dpre, CW - 1 - k, row, t)
            dw = dw + jnp.where(row8 == k, jnp.sum(dpre * xs[k], axis=0, keepdims=True), 0.0)
        dx_ref[...] = dx.astype(BF16)
        dw_ref[...] = dw
        dbias_ref[...] = jnp.broadcast_to(jnp.sum(dpre, axis=0, keepdims=True), dbias_ref.shape)

    col8 = pl.BlockSpec((8, CONV_TC), lambda j: (0, j))
    xbc = pl.BlockSpec((t, CONV_TC), lambda j: (0, O_XBC // CONV_TC + j))
    whole = pl.BlockSpec((t, CONV_TC), lambda j: (0, 0))
    return _call(
        body, jobs=jobs, name=name,
        out_shape=(jax.ShapeDtypeStruct(dproj.shape, BF16), jax.ShapeDtypeStruct((8, CONV), F32),
                   jax.ShapeDtypeStruct((8, CONV), F32)),
        grid=(CONV // CONV_TC,),
        in_specs=[xbc, pl.BlockSpec((CW, CONV_TC), lambda j: (0, j)), pl.BlockSpec((1, CONV_TC), lambda j: (0, j)),
                  pl.BlockSpec((t, CONV_TC), lambda j: (0, jnp.minimum(j, nx - 1))), whole, whole, ANY],
        out_specs=(xbc, col8, col8),
        compiler_params=_cp(("arbitrary",)), aliases={6: 0},
    )(proj, conv_w, conv_b, dxs, db, dc, dproj)


def _rope_tables(positions, t):
    half = HD // 2
    inv_freq = ROPE_THETA ** (-jnp.arange(half, dtype=F32) * 2.0 / HD)
    ang = positions.reshape(t).astype(F32)[:, None] * inv_freq
    cos, sin = jnp.cos(ang), jnp.sin(ang)
    return jnp.concatenate([cos] * 4, axis=1), jnp.concatenate([-sin, sin] * 2, axis=1)


def _lane_consts():
    lane = lax.broadcasted_iota(jnp.int32, (L, 128), 1)
    return lane, (lane % HD) < (HD // 2), lane < HD


def _rope(tv, cos, sin, lo):
    return tv * cos + jnp.where(lo, pltpu.roll(tv, 128 - HD // 2, 1), pltpu.roll(tv, HD // 2, 1)) * sin


def _rope_t(dv, cos, sin, lo):
    ds = dv * sin
    return dv * cos + jnp.where(lo, pltpu.roll(ds, 128 - HD // 2, 1), pltpu.roll(ds, HD // 2, 1))


def _placed(chunk, g, half0):
    own = jnp.where(half0 if g % 2 == 0 else jnp.logical_not(half0), chunk, 0.0)
    other = pltpu.roll(own, HD, 1)
    return (own, other) if g % 2 == 0 else (other, own)


def _unplace(acc, hf, g, half0):
    v = jnp.where(half0 if hf == 0 else jnp.logical_not(half0), acc, 0.0)
    return v if hf == g % 2 else pltpu.roll(v, HD, 1)


def _attn_fwd(proj, cos, sin, sinks, *, name, jobs=()):
    t = proj.shape[0]
    nb = t // L
    scale = HD ** -0.5

    def body(sink_ref, q_ref, kc_ref, kp_ref, vc_ref, vp_ref, cc_ref, sc_ref, cp_ref, sp_ref, o_ref, lse_ref):
        i = pl.program_id(0)
        lane, lo, half0 = _lane_consts()
        cos_c, sin_c, cos_p, sin_p = cc_ref[...], sc_ref[...], cp_ref[...], sp_ref[...]
        row = lax.broadcasted_iota(jnp.int32, (L, L), 0)
        col = lax.broadcasted_iota(jnp.int32, (L, L), 1)
        m_cur = col <= row
        m_prev = jnp.logical_and(col > row, i > 0)
        kc = [_rope(kc_ref[:, 128 * m:128 * (m + 1)], cos_c, sin_c, lo) for m in range(2)]
        kp = [_rope(kp_ref[:, 128 * m:128 * (m + 1)], cos_p, sin_p, lo) for m in range(2)]
        lse_acc = jnp.zeros((L, 128), F32)
        outs = [jnp.zeros((L, 128), F32) for _ in range(QD // 128)]
        qs = [(_rope(q_ref[:, 128 * ch:128 * (ch + 1)], cos_c, sin_c, lo) * scale).astype(BF16) for ch in range(QD // 128)]
        for g in range(NKV):
            kcv = [v.astype(BF16) for v in _placed(kc[g // 2], g, half0)]
            kpv = [v.astype(BF16) for v in _placed(kp[g // 2], g, half0)]
            vcv = [v.astype(BF16) for v in _placed(vc_ref[:, 128 * (g // 2):128 * (g // 2 + 1)], g, half0)]
            vpv = [v.astype(BF16) for v in _placed(vp_ref[:, 128 * (g // 2):128 * (g // 2 + 1)], g, half0)]
            for r in range(NQH // NKV):
                h = g * (NQH // NKV) + r
                ch, hf = h // 2, h % 2
                s_c = jnp.where(m_cur, _dot_nt(qs[ch], kcv[hf]), NEG)
                s_p = jnp.where(m_prev, _dot_nt(qs[ch], kpv[hf]), NEG)
                sink = sink_ref[0, h]
                mx = jnp.maximum(jnp.maximum(jnp.max(s_c, axis=-1, keepdims=True), jnp.max(s_p, axis=-1, keepdims=True)), sink)
                e_c = jnp.exp(s_c - mx)
                e_p = jnp.exp(s_p - mx)
                den = jnp.sum(e_c, axis=-1, keepdims=True) + jnp.sum(e_p, axis=-1, keepdims=True) + jnp.exp(sink - mx)
                inv = 1.0 / den
                outs[ch] = outs[ch] + _dot((e_c * inv).astype(BF16), vcv[hf]) + _dot((e_p * inv).astype(BF16), vpv[hf])
                lse_acc = jnp.where(lane == h, mx + jnp.log(den), lse_acc)
        for ch in range(QD // 128):
            o_ref[:, 128 * ch:128 * (ch + 1)] = outs[ch].astype(BF16)
        lse_ref[...] = lse_acc

    prev = lambda i: jnp.maximum(i - 1, 0)
    tab_c = pl.BlockSpec((L, 128), lambda i: (i, 0))
    tab_p = pl.BlockSpec((L, 128), lambda i: (prev(i), 0))
    return _call(
        body, jobs=jobs, name=name,
        out_shape=(jax.ShapeDtypeStruct((t, QD), BF16), jax.ShapeDtypeStruct((t, 128), F32)),
        grid=(nb,),
        in_specs=[pl.BlockSpec(memory_space=pltpu.SMEM),
                  pl.BlockSpec((L, QD), lambda i: (i, O_Q // QD)),
                  pl.BlockSpec((L, KVD), lambda i: (i, O_K // KVD)), pl.BlockSpec((L, KVD), lambda i: (prev(i), O_K // KVD)),
                  pl.BlockSpec((L, KVD), lambda i: (i, O_V // KVD)), pl.BlockSpec((L, KVD), lambda i: (prev(i), O_V // KVD)),
                  tab_c, tab_c, tab_p, tab_p],
        out_specs=(pl.BlockSpec((L, QD), lambda i: (i, 0)), pl.BlockSpec((L, 128), lambda i: (i, 0))),
        compiler_params=_cp(("parallel",)),
    )(sinks, proj, proj, proj, proj, proj, cos, sin, cos, sin)


def _attn_bwd(proj, cos, sin, sinks, attn, lse, dattn, dproj, *, name, jobs=()):
    t = proj.shape[0]
    nb = t // L
    scale = HD ** -0.5

    def body(sink_ref, qi_ref, qn_ref, kc_ref, kp_ref, vc_ref, vp_ref, doi_ref, don_ref, oi_ref, on_ref,
             lsei_ref, lsen_ref, cc_ref, sc_ref, cp_ref, sp_ref, cn_ref, sn_ref, _, dqkv_ref, dsk_ref):
        i = pl.program_id(0)
        lane, lo, half0 = _lane_consts()
        half1 = jnp.logical_not(half0)
        cos_c, sin_c = cc_ref[...], sc_ref[...]
        row = lax.broadcasted_iota(jnp.int32, (L, L), 0)
        col = lax.broadcasted_iota(jnp.int32, (L, L), 1)
        m_cur = col <= row
        m_prev = jnp.logical_and(col > row, i > 0)
        m_next = jnp.logical_and(col > row, i < nb - 1)
        kc = [_rope(kc_ref[:, 128 * m:128 * (m + 1)], cos_c, sin_c, lo) for m in range(2)]
        kp = [_rope(kp_ref[:, 128 * m:128 * (m + 1)], cp_ref[...], sp_ref[...], lo) for m in range(2)]
        lse_i, lse_n = lsei_ref[...], lsen_ref[...]
        dk_acc = [jnp.zeros((L, 128), F32) for _ in range(2)]
        dv_acc = [jnp.zeros((L, 128), F32) for _ in range(2)]
        dsk_acc = jnp.zeros((1, 128), F32)
        lane1 = lax.broadcasted_iota(jnp.int32, (1, 128), 1)
        place = lambda chunk, g: [v.astype(BF16) for v in _placed(chunk, g, half0)]
        kcs = [place(kc[g // 2], g) for g in range(NKV)]
        kps = [place(kp[g // 2], g) for g in range(NKV)]
        vcs = [place(vc_ref[:, 128 * (g // 2):128 * (g // 2 + 1)], g) for g in range(NKV)]
        vps = [place(vp_ref[:, 128 * (g // 2):128 * (g // 2 + 1)], g) for g in range(NKV)]
        for ch in range(QD // 128):
            sl = slice(128 * ch, 128 * (ch + 1))
            q_i = (_rope(qi_ref[:, sl], cos_c, sin_c, lo) * scale).astype(BF16)
            q_n = (_rope(qn_ref[:, sl], cn_ref[...], sn_ref[...], lo) * scale).astype(BF16)
            do_i, do_n = doi_ref[:, sl], don_ref[:, sl]
            do_ib, do_nb = do_i.astype(BF16), do_n.astype(BF16)
            od_i = do_i * oi_ref[:, sl].astype(F32)
            od_n = do_n * on_ref[:, sl].astype(F32)
            dq_ch = jnp.zeros((L, 128), F32)
            for hf in range(2):
                h = 2 * ch + hf
                g = h // (NQH // NKV)
                hm = half0 if hf == 0 else half1
                kcv, kpv, vcv, vpv = kcs[g][hf], kps[g][hf], vcs[g][hf], vps[g][hf]
                dl_i = jnp.sum(jnp.where(hm, od_i, 0.0), axis=-1, keepdims=True)
                dl_n = jnp.sum(jnp.where(hm, od_n, 0.0), axis=-1, keepdims=True)
                ls_i = jnp.sum(jnp.where(lane == h, lse_i, 0.0), axis=-1, keepdims=True)
                ls_n = jnp.sum(jnp.where(lane == h, lse_n, 0.0), axis=-1, keepdims=True)
                p_c = jnp.where(m_cur, jnp.exp(_dot_nt(q_i, kcv) - ls_i), 0.0)
                p_p = jnp.where(m_prev, jnp.exp(_dot_nt(q_i, kpv) - ls_i), 0.0)
                ds_c = (p_c * (_dot_nt(do_ib, vcv) - dl_i)).astype(BF16)
                ds_p = (p_p * (_dot_nt(do_ib, vpv) - dl_i)).astype(BF16)
                dq_ch = dq_ch + jnp.where(hm, (_dot(ds_c, kcv) + _dot(ds_p, kpv)) * scale, 0.0)
                sink = sink_ref[0, h]
                dsk = -jnp.sum(jnp.exp(sink - ls_i) * dl_i, axis=0, keepdims=True)
                dsk_acc = dsk_acc + jnp.where(lane1 == h, dsk, 0.0)
                p_n = jnp.where(m_next, jnp.exp(_dot_nt(q_n, kcv) - ls_n), 0.0)
                ds_n = (p_n * (_dot_nt(do_nb, vcv) - dl_n)).astype(BF16)
                dv_h = _dot_tn(p_c.astype(BF16), do_ib) + _dot_tn(p_n.astype(BF16), do_nb)
                dk_h = _dot_tn(ds_c, q_i) + _dot_tn(ds_n, q_n)
                dv_acc[g // 2] = dv_acc[g // 2] + _unplace(dv_h, hf, g, half0)
                dk_acc[g // 2] = dk_acc[g // 2] + _unplace(dk_h, hf, g, half0)
            dqkv_ref[:, sl] = _rope_t(dq_ch, cos_c, sin_c, lo).astype(BF16)
        for m in range(2):
            dqkv_ref[:, QD + 128 * m:QD + 128 * (m + 1)] = _rope_t(dk_acc[m], cos_c, sin_c, lo).astype(BF16)
            dqkv_ref[:, QD + KVD + 128 * m:QD + KVD + 128 * (m + 1)] = dv_acc[m].astype(BF16)

        @pl.when(i == 0)
        def _():
            dsk_ref[...] = jnp.zeros_like(dsk_ref)

        dsk_ref[...] += jnp.broadcast_to(dsk_acc, dsk_ref.shape)

    prev = lambda i: jnp.maximum(i - 1, 0)
    nxt = lambda i: jnp.minimum(i + 1, nb - 1)
    cur_q = pl.BlockSpec((L, QD), lambda i: (i, 0))
    nxt_q = pl.BlockSpec((L, QD), lambda i: (nxt(i), 0))
    tab = lambda f: pl.BlockSpec((L, 128), lambda i: (f(i), 0))
    ident = lambda i: i
    qkv = QD + 2 * KVD
    assert O_K == O_Q + QD and O_V == O_K + KVD and O_Q % qkv == 0
    return _call(
        body, jobs=jobs, name=name,
        out_shape=(jax.ShapeDtypeStruct(dproj.shape, BF16), jax.ShapeDtypeStruct((8, 128), F32)),
        grid=(nb,),
        in_specs=[pl.BlockSpec(memory_space=pltpu.SMEM),
                  pl.BlockSpec((L, QD), lambda i: (i, O_Q // QD)), pl.BlockSpec((L, QD), lambda i: (nxt(i), O_Q // QD)),
                  pl.BlockSpec((L, KVD), lambda i: (i, O_K // KVD)), pl.BlockSpec((L, KVD), lambda i: (prev(i), O_K // KVD)),
                  pl.BlockSpec((L, KVD), lambda i: (i, O_V // KVD)), pl.BlockSpec((L, KVD), lambda i: (prev(i), O_V // KVD)),
                  cur_q, nxt_q, cur_q, nxt_q, tab(ident), tab(nxt),
                  tab(ident), tab(ident), tab(prev), tab(prev), tab(nxt), tab(nxt), ANY],
        out_specs=(pl.BlockSpec((L, qkv), lambda i: (i, O_Q // qkv)), pl.BlockSpec((8, 128), lambda i: (0, 0))),
        compiler_params=_cp(("arbitrary",)), aliases={19: 0},
    )(sinks, proj, proj, proj, proj, proj, proj, dattn, dattn, attn, attn, lse, lse, cos, sin, cos, sin, cos, sin, dproj)


PAIRS = NH // NG // 2


def _softplus(x):
    return jnp.maximum(x, 0.0) + jnp.log(1.0 + jnp.exp(-jnp.abs(x)))


def _ssd_chunk(g, xps, dtr, bm, cm, sps, dtb, alog, dsk):
    lane = lax.broadcasted_iota(jnp.int32, (L, 128), 1)
    lane1 = lax.broadcasted_iota(jnp.int32, (1, 128), 1)
    row = lax.broadcasted_iota(jnp.int32, (L, L), 0)
    col = lax.broadcasted_iota(jnp.int32, (L, L), 1)
    rowc = lax.broadcasted_iota(jnp.int32, (128, 1), 0)
    tril = col <= row
    dt = _softplus(dtr + dtb)
    a = dt * (-jnp.exp(alog))
    a_cs = lax.dot_general(tril.astype(F32), a, (((1,), (0,)), ((), ())), precision=lax.Precision.HIGHEST,
                           preferred_element_type=F32)
    a_cst = a_cs.T
    a_last = jnp.sum(jnp.where(row == L - 1, a_cs, 0.0), axis=0, keepdims=True)
    cb = _dot_nt(cm.astype(BF16), bm.astype(BF16))
    ys, snew = [], []
    for q in range(PAIRS):
        xp, sp = xps[q], sps[q]
        y_pair = jnp.zeros((L, 128), F32)
        st_pair = jnp.zeros((128, NS), F32)
        keep = jnp.zeros((128, 1), F32)
        for hh in range(2):
            h = g * 2 * PAIRS + 2 * q + hh
            hm = (lane < HD) if hh == 0 else (lane >= HD)
            rm = (rowc < HD) if hh == 0 else (rowc >= HD)
            dt_h = jnp.sum(jnp.where(lane == h, dt, 0.0), axis=1, keepdims=True)
            acs_h = jnp.sum(jnp.where(lane == h, a_cs, 0.0), axis=1, keepdims=True)
            acst_h = jnp.sum(jnp.where(row == h, a_cst, 0.0), axis=0, keepdims=True)
            al_h = jnp.sum(jnp.where(lane1 == h, a_last, 0.0), axis=1, keepdims=True)
            dsk_h = jnp.sum(jnp.where(lane1 == h, dsk, 0.0), axis=1, keepdims=True)
            decay = jnp.where(tril, jnp.exp(jnp.where(tril, acs_h - acst_h, 0.0)), 0.0)
            xh = jnp.where(hm, xp, 0.0)
            xd = (xh * dt_h).astype(BF16)
            y = _dot((cb * decay).astype(BF16), xd)
            y = y + jnp.where(hm, _dot_nt((cm * jnp.exp(acs_h)).astype(BF16), sp.astype(BF16)), 0.0)
            y_pair = y_pair + y + dsk_h * xh
            st_pair = st_pair + _dot_tn(xd, (bm * jnp.exp(al_h - acs_h)).astype(BF16))
            keep = keep + jnp.where(rm, jnp.exp(al_h), 0.0)
        ys.append(y_pair)
        snew.append(sp * keep + st_pair)
    return ys, snew


def _ssd_specs(t):
    nc = t // L
    xs = lambda f: pl.BlockSpec((L, 128 * PAIRS), lambda c, g: (f(c), g))
    bspec = lambda f: pl.BlockSpec((L, NS), lambda c, g: (f(c), DI // NS + g))
    cspec = lambda f: pl.BlockSpec((L, NS), lambda c, g: (f(c), DI // NS + NG + g))
    dts = lambda f: pl.BlockSpec((L, 128), lambda c, g: (f(c), O_DT // 128))
    par = pl.BlockSpec((1, 128), lambda c, g: (0, 0))
    st = lambda f: pl.BlockSpec((1, 1, PAIRS, 128, NS), lambda c, g: (f(c), g, 0, 0, 0))
    return nc, xs, bspec, cspec, dts, par, st


def _ssd_fwd(xbc_act, proj, dtb, alog, dsk, *, name, jobs=()):
    t = proj.shape[0]
    nc, xs, bspec, cspec, dts, par, st = _ssd_specs(t)
    ident = lambda c: c

    def body(x_ref, b_ref, c_ref, dt_ref, dtb_ref, al_ref, dsk_ref, y_ref, sin_ref, s_ref):
        c, g = pl.program_id(0), pl.program_id(1)

        @pl.when(c == 0)
        def _():
            s_ref[g] = jnp.zeros((PAIRS, 128, NS), F32)

        sps = [s_ref[g, q] for q in range(PAIRS)]
        for q in range(PAIRS):
            sin_ref[0, 0, q] = sps[q]
        xps = [x_ref[:, 128 * q:128 * (q + 1)] for q in range(PAIRS)]
        ys, snew = _ssd_chunk(g, xps, dt_ref[...], b_ref[...], c_ref[...], sps, dtb_ref[...], al_ref[...], dsk_ref[...])
        for q in range(PAIRS):
            y_ref[:, 128 * q:128 * (q + 1)] = ys[q]
            s_ref[g, q] = snew[q]

    return _call(
        body, jobs=jobs, name=name,
        out_shape=(jax.ShapeDtypeStruct((t, DI), F32), jax.ShapeDtypeStruct((nc, NG, PAIRS, 128, NS), F32)),
        grid=(nc, NG),
        in_specs=[xs(ident), bspec(ident), cspec(ident), dts(ident), par, par, par],
        out_specs=(pl.BlockSpec((L, 128 * PAIRS), lambda c, g: (c, g)), st(ident)),
        scratch_shapes=[pltpu.VMEM((NG, PAIRS, 128, NS), F32)],
        compiler_params=_cp(("arbitrary", "arbitrary")),
    )(xbc_act, xbc_act, xbc_act, proj, dtb, alog, dsk)


def _ssd_bwd(xbc_act, proj, dtb, alog, dsk, states, dy, dproj, *, name, jobs=()):
    t = proj.shape[0]
    nc, xs, bspec, cspec, dts, par, st = _ssd_specs(t)
    rev = lambda c: nc - 1 - c

    def body(x_ref, b_ref, c_ref, dt_ref, dtb_ref, al_ref, dsk_ref, sin_ref, dy_ref, _,
             dx_ref, db_ref, dc_ref, ddtp_ref, ddtb_ref, dal_ref, ddsk_ref, ds_ref, ddt_ref):
        c, g = pl.program_id(0), pl.program_id(1)

        @pl.when(c == 0)
        def _():
            ds_ref[g] = jnp.zeros((PAIRS, 128, NS), F32)

        @pl.when(jnp.logical_and(c == 0, g == 0))
        def _():
            ddtb_ref[...] = jnp.zeros_like(ddtb_ref)
            dal_ref[...] = jnp.zeros_like(dal_ref)
            ddsk_ref[...] = jnp.zeros_like(ddsk_ref)

        @pl.when(g == 0)
        def _():
            ddt_ref[...] = jnp.zeros_like(ddt_ref)

        sps = [sin_ref[0, 0, q] for q in range(PAIRS)]
        xps = [x_ref[:, 128 * q:128 * (q + 1)] for q in range(PAIRS)]
        _, vjp = jax.vjp(functools.partial(_ssd_chunk, g), xps, dt_ref[...], b_ref[...], c_ref[...], sps,
                         dtb_ref[...], al_ref[...], dsk_ref[...])
        dys = [dy_ref[:, 128 * q:128 * (q + 1)] for q in range(PAIRS)]
        dss = [ds_ref[g, q] for q in range(PAIRS)]
        dxps, ddt, db, dc, dsps, ddtb, dal, ddsk = vjp((dys, dss))
        for q in range(PAIRS):
            dx_ref[:, 128 * q:128 * (q + 1)] = dxps[q]
            ds_ref[g, q] = dsps[q]
        db_ref[...] = db
        dc_ref[...] = dc
        ddt_ref[...] += ddt
        ddtb_ref[...] += jnp.broadcast_to(ddtb, ddtb_ref.shape)
        dal_ref[...] += jnp.broadcast_to(dal, dal_ref.shape)
        ddsk_ref[...] += jnp.broadcast_to(ddsk, ddsk_ref.shape)

        @pl.when(g == NG - 1)
        def _():
            ddtp_ref[:, :128] = ddt_ref[...].astype(BF16)
            ddtp_ref[:, 128:] = jnp.zeros((L, DT_PAD - 128), BF16)

    acc = pl.BlockSpec((8, 128), lambda c, g: (0, 0))
    o8 = jax.ShapeDtypeStruct((8, 128), F32)
    return _call(
        body, jobs=jobs, name=name,
        out_shape=(jax.ShapeDtypeStruct((t, DI), F32), jax.ShapeDtypeStruct((t, NG * NS), F32),
                   jax.ShapeDtypeStruct((t, NG * NS), F32), jax.ShapeDtypeStruct(dproj.shape, BF16), o8, o8, o8),
        grid=(nc, NG),
        in_specs=[xs(rev), bspec(rev), cspec(rev), dts(rev), par, par, par, st(rev),
                  pl.BlockSpec((L, 128 * PAIRS), lambda c, g: (rev(c), g)), ANY],
        out_specs=(pl.BlockSpec((L, 128 * PAIRS), lambda c, g: (rev(c), g)),
                   pl.BlockSpec((L, NS), lambda c, g: (rev(c), g)), pl.BlockSpec((L, NS), lambda c, g: (rev(c), g)),
                   pl.BlockSpec((L, DT_PAD), lambda c, g: (rev(c), O_DT // DT_PAD)), acc, acc, acc),
        scratch_shapes=[pltpu.VMEM((NG, PAIRS, 128, NS), F32), pltpu.VMEM((L, 128), F32)],
        compiler_params=_cp(("arbitrary", "arbitrary")), aliases={9: 3},
    )(xbc_act, xbc_act, xbc_act, proj, dtb, alog, dsk, states, dy, dproj)


def _pad_lanes(v, n=128):
    return jnp.pad(v, ((0, 0), (0, n - v.shape[1])))


class _LocalPlan:
    core = 0

    def __init__(self, big):
        self.big, self.grad, self.halves = big, {}, {}

    def w(self, n):
        return self.big[n]

    def g(self, n, a):
        self.grad[n] = a

    def g_half(self, n, which, a):
        self.halves[which] = a
        if len(self.halves) == 2:
            self.grad[n] = jnp.concatenate([self.halves["keep"], self.halves["send"]], axis=0)

    def jobs(self, tag):
        return ()


def _local_step(x, p, positions, target, small, plan):
    t = x.shape[0]
    cos, sin = _rope_tables(positions, t)
    dtb, alog, dsk = _pad_lanes(small["dt_bias"]), _pad_lanes(small["a_log"]), _pad_lanes(small["d_skip"])
    w, jobs = plan.w, plan.jobs

    def mm(a, b, *, name, tm=t, **kw):
        return _matmul(a, b, tm=tm, tn=512, name=name, jobs=jobs(name), **kw)

    def dw(wname, a, dy, *, name, tm):
        plan.g(wname, _matmul(a, dy, ta=True, out_dtype=BF16, tm=tm, tn=512, tk=t, name=name, jobs=jobs(name)))

    u = _rmsnorm_fwd(x, small["g_mix"], name="norm_mix")
    proj = mm(u, w("w_in"), tk=D, name="mm_in")
    attn, lse = _attn_fwd(proj, cos, sin, small["sinks"], name="attn_fwd", jobs=jobs("attn_fwd"))
    out_a = mm(attn, w("w_attn_br"), tk=QD, name="mm_attn_br")
    xbc_act = _conv_fwd(proj, small["conv_w"], small["conv_b"], name="conv_fwd")
    y_pre, states = _ssd_fwd(xbc_act, proj, dtb, alog, dsk, name="ssd_fwd", jobs=jobs("ssd_fwd"))
    yn = _gated_norm_fwd(y_pre, proj, small["g_ssd"], name="gated_norm_fwd")
    out_s = mm(yn, w("w_ssd_br"), tk=DI, name="mm_ssd_br")
    merged = _merge_fwd(proj, out_a, out_s, name="merge_fwd")
    h1 = mm(merged, w("w_o"), add=x, tk=D, name="mm_o")
    f = _rmsnorm_fwd(h1, small["g_ffn"], name="norm_ffn")
    gate, up, act = _swiglu_fwd(f, w("w_gate"), w("w_up"), name="swiglu_fwd", jobs=jobs("swiglu_fwd"))
    h2 = mm(act, w("w_down"), add=h1, tm=t // 2, tk=FFN // 2, name="mm_down")
    e = _rmsnorm_fwd(h2, small["g_ple"], name="norm_ple")
    pgl = mm(e, w("w_ple_gate"), tk=D, name="mm_ple_gate")
    pb = p.astype(BF16)
    pp = mm(pb, w("w_ple_proj"), tk=PLE, name="mm_ple_proj")
    dh3, dpgl, dpp, loss, dg_final = _final(h2, pgl, pp, target, small["g_final"].reshape(1, D), name="final")

    dw("w_ple_proj", pb, dpp, tm=PLE, name="mm_d_ple_proj")
    dw("w_ple_gate", e, dpgl, tm=D, name="mm_d_ple_gate")
    de = mm(dpgl, w("w_ple_gate"), tb=True, tk=D, name="mm_de")
    dh2, dh2b, dg_ple = _rmsnorm_bwd(h2, small["g_ple"], de, dh3, name="norm_ple_bwd", jobs=jobs("norm_ple_bwd"))
    dw("w_down", act, dh2b, tm=FFN // 2, name="mm_d_down")
    dact = mm(dh2b, w("w_down"), tb=True, tk=D, name="mm_dact")
    dgate, dup = _swiglu_bwd(gate, up, dact, name="swiglu_bwd", jobs=jobs("swiglu_bwd"))
    dw("w_gate", f, dgate, tm=D, name="mm_d_gate")
    dw("w_up", f, dup, tm=D, name="mm_d_up")
    df = mm(dgate, w("w_gate"), tb=True, tm=t // 2, tk=FFN // 2, name="mm_df_gate")
    df = mm(dup, w("w_up"), tb=True, add=df, tm=t // 2, tk=FFN // 2, name="mm_df_up")
    dh1, dh1b, dg_ffn = _rmsnorm_bwd(h1, small["g_ffn"], df, dh2, name="norm_ffn_bwd", jobs=jobs("norm_ffn_bwd"))
    dw("w_o", merged, dh1b, tm=D, name="mm_d_o")
    dmerged = mm(dh1b, w("w_o"), tb=True, tk=D, name="mm_dmerged")
    dout_a, dout_s, dproj = _merge_bwd(proj, out_a, out_s, dmerged, name="merge_bwd")
    dw("w_attn_br", attn, dout_a, tm=QD, name="mm_d_attn_br")
    dw("w_ssd_br", yn, dout_s, tm=DI, name="mm_d_ssd_br")
    dattn = mm(dout_a, w("w_attn_br"), tb=True, tk=D, name="mm_dattn")
    dyn = mm(dout_s, w("w_ssd_br"), tb=True, tk=D, name="mm_dyn")
    dproj, dsinks = _attn_bwd(proj, cos, sin, small["sinks"], attn, lse, dattn, dproj, name="attn_bwd",
                              jobs=jobs("attn_bwd"))
    dy_pre, dproj, dg_ssd = _gated_norm_bwd(y_pre, proj, small["g_ssd"], dyn, dproj, name="gated_norm_bwd",
                                            jobs=jobs("gated_norm_bwd"))
    dxs, db, dc, dproj, ddtb, dalog, ddsk = _ssd_bwd(xbc_act, proj, dtb, alog, dsk, states, dy_pre, dproj, name="ssd_bwd",
                                                     jobs=jobs("ssd_bwd"))
    dproj, dconv_w, dconv_b = _conv_bwd(proj, small["conv_w"], small["conv_b"], dxs, db, dc, dproj, name="conv_bwd",
                                        jobs=jobs("conv_bwd"))
    for which, h in (("send", 1 - plan.core), ("keep", plan.core)):
        uh = lax.dynamic_slice_in_dim(u, h * (D // 2), D // 2, axis=1)
        name = "mm_d_in_" + which
        plan.g_half("w_in", which, _matmul(uh, dproj, ta=True, out_dtype=BF16, tm=D // 2, tn=512, tk=t, name=name,
                                           jobs=jobs(name)))
    du = mm(dproj, w("w_in"), tb=True, tm=t // 2, tk=NP // 4, name="mm_du")
    grad_x, _, dg_mix = _rmsnorm_bwd(x, small["g_mix"], du, dh1, name="norm_mix_bwd", jobs=jobs("norm_mix_bwd"))

    gs = {
        "g_mix": dg_mix[:1], "conv_w": dconv_w[:CW], "conv_b": dconv_b[:1], "dt_bias": ddtb[:1, :NH],
        "a_log": dalog[:1, :NH], "d_skip": ddsk[:1, :NH], "g_ssd": dg_ssd[:1], "sinks": dsinks[:1, :NQH],
        "g_ffn": dg_ffn[:1], "g_ple": dg_ple[:1], "g_final": dg_final[0],
    }
    return loss, grad_x, gs


def _to_kernel_cols(w):
    seg = lambda o, n: w[:, o:o + n]
    return jnp.concatenate([seg(R_GA, D), seg(R_GS, D), seg(R_Z, DI), seg(R_XBC, CONV), seg(R_Q, QD), seg(R_K, KVD),
                            seg(R_V, KVD), seg(R_DT, NH), jnp.zeros((w.shape[0], DT_PAD - NH), w.dtype)], axis=1)


def _from_kernel_cols(g):
    seg = lambda o, n: g[:, o:o + n]
    return jnp.concatenate([seg(O_Q, QD), seg(O_K, KVD), seg(O_V, KVD), seg(O_Z, DI), seg(O_XBC, CONV), seg(O_DT, NH),
                            seg(O_GA, D), seg(O_GS, D)], axis=1)


RELS = ((0, 1), (1, 0), (1, 1))
MATS = {
    n: (n, kind, 1, r, c, tp, tf) for n, kind, r, c, tp, tf in (
        ("w_in", "stk", 2048, 2696, 256, 256),
        ("w_attn_br", "col", 1024, 512, 256, 256),
        ("w_ssd_br", "row", 512, 2048, 512, 256),
        ("w_o", "row", 512, 2048, 512, 256),
        ("w_gate", "col", 2048, 1408, 256, 256),
        ("w_up", "col", 2048, 1408, 256, 256),
        ("w_down", "row", 1408, 2048, 704, 704),
        ("w_ple_gate", "row", 512, 2048, 512, 256),
        ("w_ple_proj", "col", 256, 512, 128, 128),
    )}


def _pos():
    return lax.axis_index("x"), lax.axis_index("y"), lax.axis_index("c")


def _flip(v, a):
    return 1 - v if a else v


def _remote(src, dst, send, recv, dev):
    return pltpu.make_async_remote_copy(src_ref=src, dst_ref=dst, send_sem=send, recv_sem=recv, device_id=dev,
                                        device_id_type=MESH)


def _whole_shape(kind, g, r, c):
    return {"row": (g, NCHIP * r, c), "col": (g, r, NCHIP * c), "stk": (NCHIP, r, c)}[kind]


def _cols(j, c):
    return pl.ds(pl.multiple_of(j * c, 128), c)


def _whole_shard(kind, ref, j, r, c):
    if kind == "row":
        return ref.at[:, pl.ds(j * r, r), :]
    if kind == "col":
        return ref.at[:, :, _cols(j, c)]
    return ref.at[pl.ds(j, 1)]


def _whole_rows(kind, ref, j, row, n, r, c):
    if kind == "row":
        return ref.at[:, pl.ds(j * r + row, n), :]
    if kind == "col":
        return ref.at[:, pl.ds(row, n), _cols(j, c)]
    return ref.at[pl.ds(j, 1), pl.ds(row, n), :]


class _GatherJob(_Job):
    has_mid = True
    NCP = 9

    def __init__(self, names, shards, sink):
        self.mats = [MATS[n] for n in names]
        self.srcs = [shards[n] for n in names]
        self.news = [jax.ShapeDtypeStruct(_whole_shape(kind, g, r, c), BF16) for _, kind, g, r, c, _, _ in self.mats]
        n = len(names)
        self.scratch = [pltpu.SemaphoreType.DMA((self.NCP * n,)), pltpu.SemaphoreType.DMA((self.NCP * n,)),
                        pltpu.SemaphoreType.DMA((n,))]
        self.names, self.sink = names, sink

    def _copies(self, srcs, news, sems):
        send, recv, loc = sems
        x, y, c = _pos()
        me, jx, jy, jd = 2 * x + y, 2 * (1 - x) + y, 2 * x + (1 - y), 2 * (1 - x) + (1 - y)
        nbx, nby, sib = (1 - x, y, c), (x, 1 - y, c), (x, y, 1 - c)
        own, cps = [], []
        for w, (_, kind, g, r, cc, _, _) in enumerate(self.mats):
            hr, qr = r // 2, r // 4
            at = lambda j, h, q, n: _whole_rows(kind, news[w], j, h * hr + q * qr, n, r, cc)
            mine = lambda q: srcs[w].at[:, pl.ds(c * hr + q * qr, qr), :]
            cp = lambda k, s, d, dev: _remote(s, d, send.at[self.NCP * w + k], recv.at[self.NCP * w + k], dev)
            own.append(pltpu.make_async_copy(srcs[w], _whole_shard(kind, news[w], me, r, cc), loc.at[w]))
            cps.append([
                cp(0, mine(0), at(me, c, 0, qr), nbx), cp(1, mine(1), at(me, c, 1, qr), nbx),
                cp(2, mine(1), at(me, c, 1, qr), nby), cp(3, mine(0), at(me, c, 0, qr), nby),
                cp(4, at(jx, c, 0, qr), at(jx, c, 0, qr), nby), cp(5, at(jy, c, 1, qr), at(jy, c, 1, qr), nbx),
                cp(6, at(jx, c, 0, hr), at(jx, c, 0, hr), sib), cp(7, at(jy, c, 0, hr), at(jy, c, 0, hr), sib),
                cp(8, at(jd, c, 0, hr), at(jd, c, 0, hr), sib)])
        return own, cps

    def start(self, srcs, dsts, news, sems):
        own, cps = self._copies(srcs, news, sems)
        for w in range(len(self.mats)):
            own[w].start()
            for k in range(4):
                cps[w][k].start()

    def mid(self, srcs, dsts, news, sems):
        own, cps = self._copies(srcs, news, sems)
        for w in range(len(self.mats)):
            cps[w][0].wait_recv()
            cps[w][4].start()
            cps[w][2].wait_recv()
            cps[w][5].start()

    def finish(self, srcs, dsts, news, sems):
        own, cps = self._copies(srcs, news, sems)
        for w in range(len(self.mats)):
            cps[w][1].wait_recv()
            cps[w][6].start()
            cps[w][3].wait_recv()
            cps[w][7].start()
        for w in range(len(self.mats)):
            cps[w][4].wait_recv()
            cps[w][5].wait_recv()
            cps[w][8].start()
        for w in range(len(self.mats)):
            for k in (6, 7, 8):
                cps[w][k].wait_recv()
            for k in range(self.NCP):
                cps[w][k].wait_send()
            own[w].wait()

    def done(self, dsts, news):
        for n, a in zip(self.names, news):
            self.sink[n] = a


class _SwapJob(_Job):
    def __init__(self, build, ncopies, *, srcs=(), dsts=(), news=(), done=None):
        self.build, self.srcs, self.dsts, self.news, self._done = build, list(srcs), list(dsts), list(news), done
        self.scratch = [pltpu.SemaphoreType.DMA((ncopies,)), pltpu.SemaphoreType.DMA((ncopies,))]

    def start(self, srcs, dsts, news, sems):
        for cp in self.build(srcs, dsts, news, *sems):
            cp.start()

    def finish(self, srcs, dsts, news, sems):
        for cp in self.build(srcs, dsts, news, *sems):
            cp.wait()

    def done(self, dsts, news):
        if self._done is not None:
            self._done(dsts, news)


def _half_of_whole(kind, ref, h, r, c):
    if kind == "row":
        return ref.at[:, :, pl.ds(pl.multiple_of(h * (c // 2), 128), c // 2)]
    return ref.at[:, pl.ds(h * (r // 2), r // 2), :]


def _half_shape(kind, g, r, c):
    return {"row": (g, NCHIP * r, c // 2), "col": (g, r // 2, NCHIP * c), "stk": (NCHIP, r // 2, c)}[kind]


def _piece_shape(kind, g, r, c):
    return {"row": (g, r, c // 2), "col": (g, r // 2, c), "stk": (1, r // 2, c)}[kind]


def _piece_of_half(kind, ref, j, r, c):
    if kind == "row":
        return ref.at[:, pl.ds(j * r, r), :]
    if kind == "col":
        return ref.at[:, :, _cols(j, c)]
    return ref.at[pl.ds(j, 1)]


def _half_of_shard(kind, ref, h, r, c):
    if kind == "row":
        return ref.at[:, :, pl.ds(pl.multiple_of(h * (c // 2), 128), c // 2)]
    return ref.at[:, pl.ds(h * (r // 2), r // 2), :]


def _pair_sum(pack, core, mine, got, whole=True):
    name, kind, g, r, c, tr, _ = pack
    hs = _half_shape(kind, g, r, c)
    nb = hs[1] // tr

    def body(core_ref, a_ref, b_ref, o_ref):
        o_ref[...] = (a_ref[...].astype(F32) + b_ref[...].astype(F32)).astype(BF16)

    blk = (1, tr, hs[2])
    same = lambda gi, i, core_ref: (gi, i, 0)
    if not whole:
        a_map = same
    elif kind == "row":
        a_map = lambda gi, i, core_ref: (gi, i, core_ref[0])
    else:
        a_map = lambda gi, i, core_ref: (gi, core_ref[0] * nb + i, 0)
    return pl.pallas_call(
        body, name="pair_sum_" + name, out_shape=jax.ShapeDtypeStruct(hs, BF16),
        grid_spec=pltpu.PrefetchScalarGridSpec(
            num_scalar_prefetch=1, grid=(hs[0], nb),
            in_specs=[pl.BlockSpec(blk, a_map), pl.BlockSpec(blk, same)], out_specs=pl.BlockSpec(blk, same)),
        compiler_params=_cp(("parallel", "parallel")),
    )(core, mine, got)


def _shard_sum(pack, where, half, got):
    name, kind, g, r, c, _, tr = pack
    ps = _piece_shape(kind, g, r, c)
    nb = ps[1] // tr

    def body(where_ref, a_ref, b_ref, o_ref):
        o_ref[...] = a_ref[...].astype(F32) + ((b_ref[0].astype(F32) + b_ref[1].astype(F32)) + b_ref[2].astype(F32))

    blk = (1, tr, ps[2])
    if kind == "row":
        a_map = lambda gi, i, wr: (gi, wr[0] * nb + i, 0)
        o_map = lambda gi, i, wr: (gi, i, wr[1])
    elif kind == "col":
        a_map = lambda gi, i, wr: (gi, i, wr[0])
        o_map = lambda gi, i, wr: (gi, wr[1] * nb + i, 0)
    else:
        a_map = lambda gi, i, wr: (wr[0], i, 0)
        o_map = lambda gi, i, wr: (gi, wr[1] * nb + i, 0)
    return pl.pallas_call(
        body, name="shard_sum_" + name, out_shape=jax.ShapeDtypeStruct((g, r, c), F32),
        grid_spec=pltpu.PrefetchScalarGridSpec(
            num_scalar_prefetch=1, grid=(ps[0], nb),
            in_specs=[pl.BlockSpec(blk, a_map), pl.BlockSpec((3,) + blk, lambda gi, i, wr: (0, gi, i, 0))],
            out_specs=pl.BlockSpec(blk, o_map)),
        compiler_params=_cp(("parallel", "parallel")),
    )(where, half, got)


class _Plan:
    def __init__(self, shards, table):
        self.shards, self.table = shards, table
        self.whole, self.grad, self.got_a, self.half, self.got_b, self.sent_b, self.gshard = {}, {}, {}, {}, {}, {}, {}
        x, y, c = _pos()
        self.core = c
        self.core1 = c.reshape(1).astype(jnp.int32)
        self.where = jnp.stack([2 * x + y, c]).astype(jnp.int32)
        self._w_in = None
        self.send, self.keep = {}, {}

    def w(self, n):
        if n != "w_in":
            return self.whole[n][0]
        if self._w_in is None:
            self._w_in = _to_kernel_cols(self.whole[n].transpose(1, 0, 2).reshape(D, IN_DIM))
        return self._w_in

    def g(self, n, a):
        self.grad[n] = a[None]

    def g_half(self, n, which, a):
        a = _from_kernel_cols(a).reshape(D // 2, NCHIP, IN_DIM // NCHIP).transpose(1, 0, 2)
        (self.send if which == "send" else self.keep)[n] = a

    def jobs(self, tag):
        out = []
        for spec in self.table.get(tag, ()):
            out += getattr(self, "_" + spec[0])(*spec[1:])
        return out

    def run(self, name, jobs):
        if jobs:
            _call(lambda: None, jobs=jobs, name=name, out_shape=[], in_specs=[], out_specs=[])()

    def _gather(self, names):
        return [_GatherJob(names, self.shards, self.whole)]

    def _rs_a(self, names):
        mats = [MATS[n] for n in names]

        def build(srcs, dsts, news, send, recv):
            x, y, c = _pos()
            return [_remote(srcs[i] if names[i] in self.send else _half_of_whole(kind, srcs[i], 1 - c, r, cc), news[i],
                            send.at[i], recv.at[i], (x, y, 1 - c))
                    for i, (_, kind, g, r, cc, _, _) in enumerate(mats)]

        def done(dsts, news):
            self.got_a.update(zip(names, news))

        return [_SwapJob(build, len(names), srcs=[self.send.get(n, self.grad.get(n)) for n in names], done=done,
                         news=[jax.ShapeDtypeStruct(_half_shape(kind, g, r, c), BF16) for _, kind, g, r, c, _, _ in mats])]

    def _rs_b(self, names, ks=(0, 1, 2)):
        return [self._rs_b_one(n, ks) for n in names]

    def _rs_b_one(self, n, ks):
        _, kind, g, r, cc, _, _ = MATS[n]
        if n not in self.half:
            if n in self.keep:
                self.half[n] = _pair_sum(MATS[n], self.core1, self.keep[n], self.got_a[n], whole=False)
            else:
                self.half[n] = _pair_sum(MATS[n], self.core1, self.grad[n], self.got_a[n])

        def build(srcs, dsts, news, send, recv):
            x, y, c = _pos()
            land = (dsts or news)[0]
            cps = []
            for i, k in enumerate(ks):
                px, py = _flip(x, RELS[k][0]), _flip(y, RELS[k][1])
                cps.append(_remote(_piece_of_half(kind, srcs[0], 2 * px + py, r, cc), land.at[k], send.at[i], recv.at[i],
                                   (px, py, c)))
            return cps

        def done(dsts, news):
            self.got_b[n] = (dsts or news)[0]
            self.sent_b[n] = self.sent_b.get(n, ()) + tuple(ks)

        if n in self.got_b:
            return _SwapJob(build, len(ks), srcs=[self.half[n]], dsts=[self.got_b[n]], done=done)
        shape = jax.ShapeDtypeStruct((3,) + _piece_shape(kind, g, r, cc), BF16)
        return _SwapJob(build, len(ks), srcs=[self.half[n]], news=[shape], done=done)

    def _rs_c(self, names):
        mats = [MATS[n] for n in names]
        for n in names:
            assert sorted(self.sent_b[n]) == [0, 1, 2], (n, self.sent_b[n])
        parts = [_shard_sum(MATS[n], self.where, self.half[n], self.got_b[n]) for n in names]

        def build(srcs, dsts, news, send, recv):
            x, y, c = _pos()
            cps = []
            for i, (_, kind, g, r, cc, _, _) in enumerate(mats):
                mine = _half_of_shard(kind, dsts[i], c, r, cc)
                cps.append(_remote(mine, mine, send.at[i], recv.at[i], (x, y, 1 - c)))
            return cps

        def done(dsts, news):
            self.gshard.update(zip(names, dsts))

        return [_SwapJob(build, len(names), dsts=parts, done=done)]

    def finish(self, n):
        if n not in self.got_a:
            self.run("rs_a_" + n, self._rs_a((n,)))
        left = tuple(k for k in range(3) if k not in self.sent_b.get(n, ()))
        if left:
            self.run("rs_b_" + n, self._rs_b((n,), left))
        if n not in self.gshard:
            self.run("rs_c_" + n, self._rs_c((n,)))
        return self.gshard[n]


TABLE = {
    "gather_w_in": (("gather", ("w_in",)),),
    "mm_in": (("gather", ("w_attn_br", "w_ssd_br")),),
    "attn_fwd": (("gather", ("w_o",)),),
    "ssd_fwd": (("gather", ("w_gate",)),),
    "mm_ssd_br": (("gather", ("w_up",)),),
    "mm_o": (("gather", ("w_down",)),),
    "swiglu_fwd": (("gather", ("w_ple_gate", "w_ple_proj")),),
    "mm_de": (("rs_a", ("w_ple_proj", "w_ple_gate")),),
    "mm_d_down": (("rs_b", ("w_ple_proj", "w_ple_gate")),),
    "mm_dact": (("rs_a", ("w_down",)),),
    "swiglu_bwd": (("rs_c", ("w_ple_proj", "w_ple_gate")),),
    "mm_df_gate": (("rs_a", ("w_gate", "w_up")),),
    "mm_dmerged": (("rs_a", ("w_o",)),),
    "mm_dyn": (("rs_a", ("w_attn_br", "w_ssd_br")),),
    "attn_bwd": (("rs_b", ("w_down",)),),
    "gated_norm_bwd": (("rs_c", ("w_down",)),),
    "ssd_bwd": (("rs_b", ("w_gate", "w_up")),),
    "conv_bwd": (("rs_b", ("w_o",)),),
    "mm_d_in_send": (("rs_b", ("w_attn_br", "w_ssd_br")), ("rs_c", ("w_gate", "w_up"))),
    "mm_d_in_keep": (("rs_a", ("w_in",)), ("rs_c", ("w_o",))),
    "mm_du": (("rs_b", ("w_in",)),),
    "norm_mix_bwd": (("rs_c", ("w_attn_br", "w_ssd_br")),),
}


NDEV = 8


def _allreduce_small(v, *, name):
    rows = v.shape[0]

    def body(v_ref, o_ref, slots, send, recv):
        x, y, c = _pos()
        me = 4 * x + 2 * y + c
        slots[me] = v_ref[...]
        cps = []
        for k in range(1, NDEV):
            peer = (_flip(x, k & 4), _flip(y, k & 2), _flip(c, k & 1))
            cp = _remote(v_ref, slots.at[me], send.at[k - 1], recv.at[k - 1], peer)
            cp.start()
            cps.append(cp)
        for cp in cps:
            cp.wait()
        acc = slots[0]
        for s in range(1, NDEV):
            acc = acc + slots[s]
        o_ref[...] = acc

    return pl.pallas_call(
        body, name=name, out_shape=jax.ShapeDtypeStruct((rows, 128), F32),
        in_specs=[pl.BlockSpec(memory_space=pltpu.VMEM)], out_specs=pl.BlockSpec(memory_space=pltpu.VMEM),
        scratch_shapes=[pltpu.VMEM((NDEV, rows, 128), F32), pltpu.SemaphoreType.DMA((NDEV - 1,)),
                        pltpu.SemaphoreType.DMA((NDEV - 1,))],
    )(v)


def _adamw(w, g, m, v, *, name, tr=None, tc=None, jobs=()):
    r, c = w.shape
    tr = r if tr is None else tr
    c1 = 1.0 / (1.0 - B1 ** STEP)
    c2 = 1.0 / (1.0 - B2 ** STEP)

    def body(w_ref, g_ref, m_ref, v_ref, d_ref, mo_ref, vo_ref):
        gv = g_ref[...]
        mn = B1 * m_ref[...] + (1.0 - B1) * gv
        vn = B2 * v_ref[...] + (1.0 - B2) * (gv * gv)
        mo_ref[...] = mn
        vo_ref[...] = vn
        d_ref[...] = -LR * ((mn * c1) / (jnp.sqrt(vn * c2) + AEPS) + WD * w_ref[...])

    if tc is None:
        blk, grid = pl.BlockSpec((tr, c), lambda i: (i, 0)), (r // tr,)
    else:
        blk, grid = pl.BlockSpec((r, tc), lambda i: (0, i)), (c // tc,)
    o = jax.ShapeDtypeStruct((r, c), F32)
    return _call(
        body, jobs=jobs, name=name, out_shape=(o, o, o), grid=grid, in_specs=[blk] * 4, out_specs=(blk, blk, blk),
        compiler_params=_cp(("parallel",)),
    )(w, g, m, v)


WEIGHTS = ("g_mix", "w_in", "conv_w", "conv_b", "dt_bias", "a_log", "d_skip", "g_ssd", "sinks", "w_attn_br", "w_ssd_br",
           "w_o", "g_ffn", "w_gate", "w_up", "w_down", "g_ple", "w_ple_gate", "w_ple_proj", "g_final")
BIG = {
    "w_gate": 256, "w_up": 256, "w_down": 128, "w_ssd_br": 128, "w_o": 128, "w_ple_gate": 128, "w_attn_br": 256,
    "w_ple_proj": 256, "w_in": None,
}
SMALL = tuple(n for n in WEIGHTS if n not in BIG)


def _pack_small(parts):
    rows = []
    for a in parts:
        a = a.reshape(-1)
        rows.append(jnp.pad(a, (0, -a.shape[0] % 128)).reshape(-1, 128))
    out = jnp.concatenate(rows, axis=0)
    return jnp.pad(out, ((0, -out.shape[0] % 8), (0, 0)))


def _unpack_small(packed, shapes):
    out, r = [], 0
    for s in shapes:
        n = int(np.prod(s))
        nr = -(-n // 128)
        out.append(packed[r:r + nr].reshape(-1)[:n].reshape(s))
        r += nr
    return out


def kernel(x, p, positions, g_mix, w_in, conv_w, conv_b, dt_bias, a_log, d_skip, g_ssd, sinks, w_attn_br, w_ssd_br, w_o, g_ffn, w_gate, w_up, w_down, g_ple, w_ple_gate, w_ple_proj, g_final, loss_target, m_g_mix, m_w_in, m_conv_w, m_conv_b, m_dt_bias, m_a_log, m_d_skip, m_g_ssd, m_sinks, m_w_attn_br, m_w_ssd_br, m_w_o, m_g_ffn, m_w_gate, m_w_up, m_w_down, m_g_ple, m_w_ple_gate, m_w_ple_proj, m_g_final, v_g_mix, v_w_in, v_conv_w, v_conv_b, v_dt_bias, v_a_log, v_d_skip, v_g_ssd, v_sinks, v_w_attn_br, v_w_ssd_br, v_w_o, v_g_ffn, v_w_gate, v_w_up, v_w_down, v_g_ple, v_w_ple_gate, v_w_ple_proj, v_g_final):
    w = dict(zip(WEIGHTS, (g_mix, w_in, conv_w, conv_b, dt_bias, a_log, d_skip, g_ssd, sinks, w_attn_br, w_ssd_br, w_o,
                           g_ffn, w_gate, w_up, w_down, g_ple, w_ple_gate, w_ple_proj, g_final)))
    m = dict(zip(WEIGHTS, (m_g_mix, m_w_in, m_conv_w, m_conv_b, m_dt_bias, m_a_log, m_d_skip, m_g_ssd, m_sinks, m_w_attn_br,
                           m_w_ssd_br, m_w_o, m_g_ffn, m_w_gate, m_w_up, m_w_down, m_g_ple, m_w_ple_gate, m_w_ple_proj,
                           m_g_final)))
    v = dict(zip(WEIGHTS, (v_g_mix, v_w_in, v_conv_w, v_conv_b, v_dt_bias, v_a_log, v_d_skip, v_g_ssd, v_sinks, v_w_attn_br,
                           v_w_ssd_br, v_w_o, v_g_ffn, v_w_gate, v_w_up, v_w_down, v_g_ple, v_w_ple_gate, v_w_ple_proj,
                           v_g_final)))
    xi, yi, ci = _pos()
    chip = 2 * xi + yi
    t = x.shape[1]
    cshard = CONV // NCHIP

    plan = _Plan({n: w[n].astype(BF16) for n in MATS}, TABLE)
    plan.run("gather_w_in", plan.jobs("gather_w_in"))
    placed = lax.dynamic_update_slice(jnp.zeros((CW, CONV), F32), w["conv_w"][0], (0, chip * cshard))
    conv_whole = _allreduce_small(jnp.where(ci == 0, placed, 0.0).reshape(-1, 128), name="gather_conv_w").reshape(CW, CONV)

    small = {n: w[n] for n in ("g_mix", "conv_b", "dt_bias", "a_log", "d_skip", "g_ssd", "sinks", "g_ffn", "g_ple", "g_final")}
    small["conv_w"] = conv_whole
    loss8, grad_x, gs = _local_step(x[0], p[0, 0], positions, loss_target[0], small, plan)

    order = ("g_mix", "conv_b", "dt_bias", "a_log", "d_skip", "g_ssd", "sinks", "g_ffn", "g_ple", "g_final", "conv_w")
    summed = _allreduce_small(_pack_small([loss8[0, :1]] + [gs[n] for n in order]), name="sum_small")
    parts = _unpack_small(summed, [(1,)] + [w[n].shape for n in order[:-1]] + [(CW, CONV)])
    loss = parts[0][0]
    grad = dict(zip(order, parts[1:]))
    grad["conv_w"] = lax.dynamic_slice(grad["conv_w"], (0, chip * cshard), (CW, cshard))[None]

    delta, new_m, new_v = {}, {}, {}
    for n, tr in BIG.items():
        grad[n] = plan.finish(n)
        if n == "w_in":
            d_, m_, v_ = _adamw(w[n][0].T, grad[n][0].T, m[n][0].T, v[n][0].T, tc=128, name="adamw_" + n)
            d_, m_, v_ = d_.T, m_.T, v_.T
        else:
            d_, m_, v_ = _adamw(w[n][0], grad[n][0], m[n][0], v[n][0], tr=tr, name="adamw_" + n)
        delta[n], new_m[n], new_v[n] = d_[None], m_[None], v_[None]
    shapes = [w[n].shape for n in SMALL]
    d_, m_, v_ = _adamw(_pack_small([w[n] for n in SMALL]), _pack_small([grad[n] for n in SMALL]),
                        _pack_small([m[n] for n in SMALL]), _pack_small([v[n] for n in SMALL]), tr=None, name="adamw_small")
    for n, a, b, c_ in zip(SMALL, _unpack_small(d_, shapes), _unpack_small(m_, shapes), _unpack_small(v_, shapes)):
        delta[n], new_m[n], new_v[n] = a, b, c_

    return (loss, grad_x[None], *[grad[n] for n in WEIGHTS], *[delta[n] for n in WEIGHTS],
            *[new_m[n] for n in WEIGHTS], *[new_v[n] for n in WEIGHTS])
```

```python
import functools

import jax
import jax.numpy as jnp
import numpy as np
from jax import lax
from jax.experimental import pallas as pl
from jax.experimental.pallas import tpu as pltpu

F32 = jnp.float32
BF16 = jnp.bfloat16
MESH = pl.DeviceIdType.MESH

D = 2048
HD = 64
NQH = 16
NKV = 4
QD = NQH * HD
KVD = NKV * HD
DI = 2048
NH = 32
NG = 4
NS = 128
CW = 4
L = 128
CONV = DI + 2 * NG * NS
FFN = 5632
PLE = 256
IN_DIM = QD + 2 * KVD + DI + CONV + NH + 2 * D
EPS = 1e-6
SSM_EPS = 1e-5
ROPE_THETA = 10000.0
LR, B1, B2, AEPS, WD, STEP = 0.001, 0.9, 0.999, 1e-08, 0.01, 10

O_GA, O_GS, O_Z, O_XBC, O_Q, O_K, O_V, O_DT = 0, 2048, 4096, 6144, 9216, 10240, 10496, 10752
DT_PAD = 512
NP = O_DT + DT_PAD
R_Q, R_K, R_V, R_Z, R_XBC, R_DT, R_GA, R_GS = 0, 1024, 1280, 1536, 3584, 6656, 6688, 8736

NCHIP = 4
VMEM_LIMIT = 52 * 1024 * 1024
NEG = -1e30


def _cp(sem=None):
    return pltpu.CompilerParams(dimension_semantics=sem, vmem_limit_bytes=VMEM_LIMIT)


def _dot(a, b):
    return lax.dot_general(a, b, (((1,), (0,)), ((), ())), preferred_element_type=F32)


def _dot_nt(a, b):
    return lax.dot_general(a, b, (((1,), (1,)), ((), ())), preferred_element_type=F32)


def _dot_tn(a, b):
    return lax.dot_general(a, b, (((0,), (0,)), ((), ())), preferred_element_type=F32)


def _sigmoid(x):
    return 1.0 / (1.0 + jnp.exp(-x))


ANY = pl.BlockSpec(memory_space=pl.ANY)


class _Job:
    srcs, dsts, news, scratch = (), (), (), ()
    has_mid = False

    def start(self, srcs, dsts, news, sems):
        raise NotImplementedError

    def mid(self, srcs, dsts, news, sems):
        pass

    def finish(self, srcs, dsts, news, sems):
        raise NotImplementedError

    def done(self, dsts, news):
        pass


def _call(body, *, jobs=(), name, out_shape, in_specs, out_specs, grid=(), scratch_shapes=(), compiler_params=None,
          aliases=None):
    jobs = [j for j in jobs if j is not None]
    aliases = dict(aliases or {})
    if not jobs:
        return pl.pallas_call(body, name=name, out_shape=out_shape, in_specs=in_specs, out_specs=out_specs, grid=grid,
                              scratch_shapes=scratch_shapes, compiler_params=compiler_params,
                              input_output_aliases=aliases)
    single = not isinstance(out_shape, (tuple, list))
    outs = [out_shape] if single else list(out_shape)
    ospecs = [out_specs] if single else list(out_specs)
    n_in, n_out, n_scr = len(in_specs), len(outs), len(scratch_shapes)
    srcs = [a for j in jobs for a in j.srcs]
    dsts = [a for j in jobs for a in j.dsts]
    news = [a for j in jobs for a in j.news]
    sems = [a for j in jobs for a in j.scratch]

    def wrapped(*refs):
        pos = n_in + len(srcs) + len(dsts)
        ins, jsrc = refs[:n_in], refs[n_in:n_in + len(srcs)]
        o_refs = refs[pos:pos + n_out]
        pos += n_out
        jdst, jnew = refs[pos:pos + len(dsts)], refs[pos + len(dsts):pos + len(dsts) + len(news)]
        pos += len(dsts) + len(news)
        scr, jsem = refs[pos:pos + n_scr], refs[pos + n_scr:]

        def run(which):
            a = b = c = d = 0
            for j in jobs:
                getattr(j, which)(jsrc[a:a + len(j.srcs)], jdst[b:b + len(j.dsts)], jnew[c:c + len(j.news)],
                                  jsem[d:d + len(j.scratch)])
                a, b, c, d = a + len(j.srcs), b + len(j.dsts), c + len(j.news), d + len(j.scratch)

        if not grid:
            run("start")
            run("mid")
            body(*ins, *o_refs, *scr)
            run("finish")
            return
        step = functools.reduce(lambda acc, a: acc * grid[a] + pl.program_id(a), range(len(grid)), 0)
        steps = int(np.prod(grid))
        pl.when(step == 0)(lambda: run("start"))
        if any(j.has_mid for j in jobs):
            pl.when(step == steps // 3)(lambda: run("mid"))
        body(*ins, *o_refs, *scr)
        pl.when(step == steps - 1)(lambda: run("finish"))

    call = pl.pallas_call(
        wrapped, name=name,
        out_shape=outs + [jax.ShapeDtypeStruct(a.shape, a.dtype) for a in dsts] + news,
        in_specs=list(in_specs) + [ANY] * (len(srcs) + len(dsts)),
        out_specs=ospecs + [ANY] * (len(dsts) + len(news)),
        grid=grid, scratch_shapes=list(scratch_shapes) + sems,
        input_output_aliases={**aliases, **{n_in + len(srcs) + i: n_out + i for i in range(len(dsts))}},
        compiler_params=_cp(("arbitrary",) * len(grid) if grid else None))

    def run_call(*args):
        res = call(*args, *srcs, *dsts)
        b, c = n_out, n_out + len(dsts)
        for j in jobs:
            j.done(res[b:b + len(j.dsts)], res[c:c + len(j.news)])
            b, c = b + len(j.dsts), c + len(j.news)
        return res[0] if single else tuple(res[:n_out])

    return run_call


def _matmul(a, b, *, ta=False, tb=False, out_dtype=F32, add=None, tm, tn, tk, name, jobs=()):
    k, m = a.shape if ta else a.shape[::-1]
    n = b.shape[0] if tb else b.shape[1]
    assert (b.shape[1] if tb else b.shape[0]) == k and not (ta and tb)
    assert m % tm == 0 and n % tn == 0 and k % tk == 0, (name, a.shape, b.shape)
    nk = k // tk
    has_add = add is not None

    def body(*refs):
        a_ref, b_ref = refs[0], refs[1]
        add_ref = refs[2] if has_add else None
        o_ref = refs[3] if has_add else refs[2]
        av = a_ref[...].astype(BF16)
        bv = b_ref[...].astype(BF16)
        part = _dot_tn(av, bv) if ta else _dot_nt(av, bv) if tb else _dot(av, bv)

        def finish(r):
            if has_add:
                r = r + add_ref[...]
            o_ref[...] = r.astype(out_dtype)

        if nk == 1:
            finish(part)
        else:
            acc_ref = refs[-1]
            kk = pl.program_id(2)

            @pl.when(kk == 0)
            def _():
                acc_ref[...] = part

            @pl.when(kk > 0)
            def _():
                acc_ref[...] += part

            @pl.when(kk == nk - 1)
            def _():
                finish(acc_ref[...])

    in_specs = [pl.BlockSpec((tk, tm), lambda i, j, kk: (kk, i)) if ta else pl.BlockSpec((tm, tk), lambda i, j, kk: (i, kk)),
                pl.BlockSpec((tn, tk), lambda i, j, kk: (j, kk)) if tb
                else pl.BlockSpec((tk, tn), lambda i, j, kk: (kk, j))]
    args = [a, b]
    if has_add:
        in_specs.append(pl.BlockSpec((tm, tn), lambda i, j, kk: (i, j)))
        args.append(add)
    return _call(
        body, jobs=jobs, name=name,
        out_shape=jax.ShapeDtypeStruct((m, n), out_dtype),
        grid=(m // tm, n // tn, nk),
        in_specs=in_specs,
        out_specs=pl.BlockSpec((tm, tn), lambda i, j, kk: (i, j)),
        scratch_shapes=[pltpu.VMEM((tm, tn), F32)] if nk > 1 else [],
        compiler_params=_cp(("parallel", "parallel", "arbitrary")),
    )(*args)


ROWS = 256


def _rmsnorm_fwd(x, g, *, name):
    t, d = x.shape

    def body(x_ref, g_ref, o_ref):
        xv = x_ref[...]
        r = lax.rsqrt(jnp.mean(xv * xv, axis=-1, keepdims=True) + EPS)
        o_ref[...] = (xv * r * g_ref[...]).astype(BF16)

    return pl.pallas_call(
        body, name=name, out_shape=jax.ShapeDtypeStruct((t, d), BF16), grid=(t // ROWS,),
        in_specs=[pl.BlockSpec((ROWS, d), lambda i: (i, 0)), pl.BlockSpec((1, d), lambda i: (0, 0))],
        out_specs=pl.BlockSpec((ROWS, d), lambda i: (i, 0)), compiler_params=_cp(("parallel",)),
    )(x, g)


def _rmsnorm_bwd(x, g, dy, dres, *, name, jobs=()):
    t, d = x.shape

    def body(x_ref, g_ref, dy_ref, dres_ref, dx_ref, dxb_ref, dg_ref):
        xv = x_ref[...]
        r = lax.rsqrt(jnp.mean(xv * xv, axis=-1, keepdims=True) + EPS)
        xh = xv * r
        dyv = dy_ref[...]
        dxh = dyv * g_ref[...]
        dx = r * (dxh - xh * jnp.mean(dxh * xh, axis=-1, keepdims=True))
        tot = dres_ref[...] + dx
        dx_ref[...] = tot
        dxb_ref[...] = tot.astype(BF16)

        @pl.when(pl.program_id(0) == 0)
        def _():
            dg_ref[...] = jnp.zeros_like(dg_ref)

        dg_ref[...] += jnp.broadcast_to(jnp.sum(dyv * xh, axis=0, keepdims=True), dg_ref.shape)

    row = pl.BlockSpec((ROWS, d), lambda i: (i, 0))
    return _call(
        body, jobs=jobs, name=name,
        out_shape=(jax.ShapeDtypeStruct((t, d), F32), jax.ShapeDtypeStruct((t, d), BF16),
                   jax.ShapeDtypeStruct((8, d), F32)),
        grid=(t // ROWS,),
        in_specs=[row, pl.BlockSpec((1, d), lambda i: (0, 0)), row, row],
        out_specs=(row, row, pl.BlockSpec((8, d), lambda i: (0, 0))),
        compiler_params=_cp(("arbitrary",)),
    )(x, g, dy, dres)


def _final(h2, pgl, pp, target, g_final, *, name):
    t, d = h2.shape

    def body(h2_ref, pgl_ref, pp_ref, tg_ref, g_ref, dh3_ref, dpgl_ref, dpp_ref, loss_ref, dg_ref):
        s = _sigmoid(pgl_ref[...])
        ppv = pp_ref[...]
        h3 = h2_ref[...] + s * ppv
        r = lax.rsqrt(jnp.mean(h3 * h3, axis=-1, keepdims=True) + EPS)
        xh = h3 * r
        gv = g_ref[...]
        err = xh * gv - tg_ref[...]
        dyv = err * (1.0 / d)
        dxh = dyv * gv
        dh3 = r * (dxh - xh * jnp.mean(dxh * xh, axis=-1, keepdims=True))
        dh3_ref[...] = dh3
        dpp_ref[...] = (dh3 * s).astype(BF16)
        dpgl_ref[...] = (dh3 * ppv * s * (1.0 - s)).astype(BF16)

        @pl.when(pl.program_id(0) == 0)
        def _():
            loss_ref[...] = jnp.zeros_like(loss_ref)
            dg_ref[...] = jnp.zeros_like(dg_ref)

        part = 0.5 * jnp.sum(jnp.mean(err * err, axis=-1, keepdims=True), axis=0, keepdims=True)
        loss_ref[...] += jnp.broadcast_to(part, loss_ref.shape)
        dg_ref[...] += jnp.broadcast_to(jnp.sum(dyv * xh, axis=0, keepdims=True), dg_ref.shape)

    row = pl.BlockSpec((ROWS, d), lambda i: (i, 0))
    return pl.pallas_call(
        body, name=name,
        out_shape=(jax.ShapeDtypeStruct((t, d), F32), jax.ShapeDtypeStruct((t, d), BF16),
                   jax.ShapeDtypeStruct((t, d), BF16), jax.ShapeDtypeStruct((8, 128), F32),
                   jax.ShapeDtypeStruct((8, d), F32)),
        grid=(t // ROWS,),
        in_specs=[row, row, row, row, pl.BlockSpec((1, d), lambda i: (0, 0))],
        out_specs=(row, row, row, pl.BlockSpec((8, 128), lambda i: (0, 0)), pl.BlockSpec((8, d), lambda i: (0, 0))),
        compiler_params=_cp(("arbitrary",)),
    )(h2, pgl, pp, target, g_final)


def _merge_fwd(proj, out_a, out_s, *, name):
    t = proj.shape[0]

    def body(ga_ref, gs_ref, a_ref, s_ref, o_ref):
        o_ref[...] = (_sigmoid(ga_ref[...]) * a_ref[...] + _sigmoid(gs_ref[...]) * s_ref[...]).astype(BF16)

    row = pl.BlockSpec((ROWS, D), lambda i: (i, 0))
    return pl.pallas_call(
        body, name=name, out_shape=jax.ShapeDtypeStruct((t, D), BF16), grid=(t // ROWS,),
        in_specs=[pl.BlockSpec((ROWS, D), lambda i: (i, O_GA // D)), pl.BlockSpec((ROWS, D), lambda i: (i, O_GS // D)),
                  row, row],
        out_specs=row, compiler_params=_cp(("parallel",)),
    )(proj, proj, out_a, out_s)


def _merge_bwd(proj, out_a, out_s, dmerged, *, name):
    t = proj.shape[0]
    assert O_GA == 0 and O_GS == D

    def body(ga_ref, gs_ref, a_ref, s_ref, dm_ref, da_ref, ds_ref, dp_ref):
        sa = _sigmoid(ga_ref[...])
        ss = _sigmoid(gs_ref[...])
        dm = dm_ref[...]
        da_ref[...] = (dm * sa).astype(BF16)
        ds_ref[...] = (dm * ss).astype(BF16)
        dp_ref[:, :D] = (dm * a_ref[...] * sa * (1.0 - sa)).astype(BF16)
        dp_ref[:, D:] = (dm * s_ref[...] * ss * (1.0 - ss)).astype(BF16)

    row = pl.BlockSpec((ROWS, D), lambda i: (i, 0))
    o = jax.ShapeDtypeStruct((t, D), BF16)
    return pl.pallas_call(
        body, name=name, out_shape=(o, o, jax.ShapeDtypeStruct((t, NP), BF16)), grid=(t // ROWS,),
        in_specs=[pl.BlockSpec((ROWS, D), lambda i: (i, O_GA // D)), pl.BlockSpec((ROWS, D), lambda i: (i, O_GS // D)),
                  row, row, row],
        out_specs=(row, row, pl.BlockSpec((ROWS, 2 * D), lambda i: (i, 0))), compiler_params=_cp(("parallel",)),
    )(proj, proj, out_a, out_s, dmerged)


def _swiglu_fwd(f, w_gate, w_up, *, name, tn=256, jobs=()):
    t, d = f.shape
    n = w_gate.shape[1]

    def body(f_ref, wg_ref, wu_ref, g_ref, u_ref, a_ref):
        fv = f_ref[...]
        g = _dot(fv, wg_ref[...])
        u = _dot(fv, wu_ref[...])
        g_ref[...] = g
        u_ref[...] = u
        a_ref[...] = (g * _sigmoid(g) * u).astype(BF16)

    col = pl.BlockSpec((t, tn), lambda j: (0, j))
    wcol = pl.BlockSpec((d, tn), lambda j: (0, j))
    return _call(
        body, jobs=jobs, name=name,
        out_shape=(jax.ShapeDtypeStruct((t, n), F32), jax.ShapeDtypeStruct((t, n), F32),
                   jax.ShapeDtypeStruct((t, n), BF16)),
        grid=(n // tn,),
        in_specs=[pl.BlockSpec((t, d), lambda j: (0, 0)), wcol, wcol],
        out_specs=(col, col, col), compiler_params=_cp(("parallel",)),
    )(f, w_gate, w_up)


def _swiglu_bwd(gate, up, dact, *, name, tc=1408, jobs=()):
    t, n = gate.shape

    def body(g_ref, u_ref, da_ref, dg_ref, du_ref):
        g = g_ref[...]
        s = _sigmoid(g)
        da = da_ref[...]
        du_ref[...] = (da * g * s).astype(BF16)
        dg_ref[...] = (da * u_ref[...] * s * (1.0 + g * (1.0 - s))).astype(BF16)

    blk = pl.BlockSpec((ROWS, tc), lambda i, j: (i, j))
    o = jax.ShapeDtypeStruct((t, n), BF16)
    return _call(
        body, jobs=jobs, name=name, out_shape=(o, o), grid=(t // ROWS, n // tc),
        in_specs=[blk, blk, blk], out_specs=(blk, blk), compiler_params=_cp(("parallel", "parallel")),
    )(gate, up, dact)


def _gated_norm_fwd(y_pre, proj, g_ssd, *, name):
    t = y_pre.shape[0]

    def body(y_ref, z_ref, g_ref, o_ref):
        z = z_ref[...]
        v = y_ref[...] * z * _sigmoid(z)
        r = lax.rsqrt(jnp.mean(v * v, axis=-1, keepdims=True) + SSM_EPS)
        o_ref[...] = (v * r * g_ref[...]).astype(BF16)

    row = pl.BlockSpec((ROWS, DI), lambda i: (i, 0))
    return pl.pallas_call(
        body, name=name, out_shape=jax.ShapeDtypeStruct((t, DI), BF16), grid=(t // ROWS,),
        in_specs=[row, pl.BlockSpec((ROWS, DI), lambda i: (i, O_Z // DI)), pl.BlockSpec((1, DI), lambda i: (0, 0))],
        out_specs=row, compiler_params=_cp(("parallel",)),
    )(y_pre, proj, g_ssd)


def _gated_norm_bwd(y_pre, proj, g_ssd, dyn, dproj, *, name, jobs=()):
    t = y_pre.shape[0]

    def body(y_ref, z_ref, g_ref, dyn_ref, _, dy_ref, dz_ref, dg_ref):
        z = z_ref[...]
        s = _sigmoid(z)
        sz = z * s
        yv = y_ref[...]
        v = yv * sz
        r = lax.rsqrt(jnp.mean(v * v, axis=-1, keepdims=True) + SSM_EPS)
        vh = v * r
        dn = dyn_ref[...]
        dvh = dn * g_ref[...]
        dv = r * (dvh - vh * jnp.mean(dvh * vh, axis=-1, keepdims=True))
        dy_ref[...] = dv * sz
        dz_ref[...] = (dv * yv * s * (1.0 + z * (1.0 - s))).astype(BF16)

        @pl.when(pl.program_id(0) == 0)
        def _():
            dg_ref[...] = jnp.zeros_like(dg_ref)

        dg_ref[...] += jnp.broadcast_to(jnp.sum(dn * vh, axis=0, keepdims=True), dg_ref.shape)

    row = pl.BlockSpec((ROWS, DI), lambda i: (i, 0))
    return _call(
        body, jobs=jobs, name=name,
        out_shape=(jax.ShapeDtypeStruct((t, DI), F32), jax.ShapeDtypeStruct(dproj.shape, BF16),
                   jax.ShapeDtypeStruct((8, DI), F32)),
        grid=(t // ROWS,),
        in_specs=[row, pl.BlockSpec((ROWS, DI), lambda i: (i, O_Z // DI)), pl.BlockSpec((1, DI), lambda i: (0, 0)), row, ANY],
        out_specs=(row, pl.BlockSpec((ROWS, DI), lambda i: (i, O_Z // DI)), pl.BlockSpec((8, DI), lambda i: (0, 0))),
        compiler_params=_cp(("arbitrary",)), aliases={4: 1},
    )(y_pre, proj, g_ssd, dyn, dproj)


CONV_TC = 512


def _shift_down(x, s, row):
    if s == 0:
        return x
    return jnp.where(row >= s, pltpu.roll(x, s, 0), 0.0)


def _shift_up(x, s, row, t):
    if s == 0:
        return x
    return jnp.where(row < t - s, pltpu.roll(x, t - s, 0), 0.0)


def _conv_fwd(proj, conv_w, conv_b, *, name):
    t = proj.shape[0]

    def body(x_ref, w_ref, b_ref, o_ref):
        x = x_ref[...]
        row = lax.broadcasted_iota(jnp.int32, x.shape, 0)
        pre = jnp.broadcast_to(b_ref[...], x.shape)
        for k in range(CW):
            pre = pre + w_ref[k:k + 1, :] * _shift_down(x, CW - 1 - k, row)
        o_ref[...] = pre * _sigmoid(pre)

    return pl.pallas_call(
        body, name=name, out_shape=jax.ShapeDtypeStruct((t, CONV), F32), grid=(CONV // CONV_TC,),
        in_specs=[pl.BlockSpec((t, CONV_TC), lambda j: (0, O_XBC // CONV_TC + j)),
                  pl.BlockSpec((CW, CONV_TC), lambda j: (0, j)), pl.BlockSpec((1, CONV_TC), lambda j: (0, j))],
        out_specs=pl.BlockSpec((t, CONV_TC), lambda j: (0, j)), compiler_params=_cp(("parallel",)),
    )(proj, conv_w, conv_b)


def _conv_bwd(proj, conv_w, conv_b, dxs, db, dc, dproj, *, name, jobs=()):
    t = proj.shape[0]
    nx = DI // CONV_TC
    assert NG * NS == CONV_TC

    def body(x_ref, w_ref, b_ref, dxs_ref, db_ref, dc_ref, _, dx_ref, dw_ref, dbias_ref):
        j = pl.program_id(0)
        x = x_ref[...]
        row = lax.broadcasted_iota(jnp.int32, x.shape, 0)
        xs = [_shift_down(x, CW - 1 - k, row) for k in range(CW)]
        pre = jnp.broadcast_to(b_ref[...], x.shape)
        for k in range(CW):
            pre = pre + w_ref[k:k + 1, :] * xs[k]
        s = _sigmoid(pre)
        da = jnp.where(j < nx, dxs_ref[...], jnp.where(j == nx, db_ref[...], dc_ref[...]))
        dpre = da * s * (1.0 + pre * (1.0 - s))
        dx = jnp.zeros_like(x)
        row8 = lax.broadcasted_iota(jnp.int32, dw_ref.shape, 0)
        dw = jnp.zeros(dw_ref.shape, F32)
        for k in range(CW):
            dx = dx + w_ref[k:k + 1, :] * _shift_up(dpre, CW - 1 - k, row, t)
            dw = dw + jnp.where(row8 == k, jnp.sum(dpre * xs[k], axis=0, keepdims=True), 0.0)
        dx_ref[...] = dx.astype(BF16)
        dw_ref[...] = dw
        dbias_ref[...] = jnp.broadcast_to(jnp.sum(dpre, axis=0, keepdims=True), dbias_ref.shape)

    col8 = pl.BlockSpec((8, CONV_TC), lambda j: (0, j))
    xbc = pl.BlockSpec((t, CONV_TC), lambda j: (0, O_XBC // CONV_TC + j))
    whole = pl.BlockSpec((t, CONV_TC), lambda j: (0, 0))
    return _call(
        body, jobs=jobs, name=name,
        out_shape=(jax.ShapeDtypeStruct(dproj.shape, BF16), jax.ShapeDtypeStruct((8, CONV), F32),
                   jax.ShapeDtypeStruct((8, CONV), F32)),
        grid=(CONV // CONV_TC,),
        in_specs=[xbc, pl.BlockSpec((CW, CONV_TC), lambda j: (0, j)), pl.BlockSpec((1, CONV_TC), lambda j: (0, j)),
                  pl.BlockSpec((t, CONV_TC), lambda j: (0, jnp.minimum(j, nx - 1))), whole, whole, ANY],
        out_specs=(xbc, col8, col8),
        compiler_params=_cp(("arbitrary",)), aliases={6: 0},
    )(proj, conv_w, conv_b, dxs, db, dc, dproj)


def _rope_tables(positions, t):
    half = HD // 2
    inv_freq = ROPE_THETA ** (-jnp.arange(half, dtype=F32) * 2.0 / HD)
    ang = positions.reshape(t).astype(F32)[:, None] * inv_freq
    cos, sin = jnp.cos(ang), jnp.sin(ang)
    return jnp.concatenate([cos] * 4, axis=1), jnp.concatenate([-sin, sin] * 2, axis=1)


def _lane_consts():
    lane = lax.broadcasted_iota(jnp.int32, (L, 128), 1)
    return lane, (lane % HD) < (HD // 2), lane < HD


def _rope(tv, cos, sin, lo):
    return tv * cos + jnp.where(lo, pltpu.roll(tv, 128 - HD // 2, 1), pltpu.roll(tv, HD // 2, 1)) * sin


def _rope_t(dv, cos, sin, lo):
    ds = dv * sin
    return dv * cos + jnp.where(lo, pltpu.roll(ds, 128 - HD // 2, 1), pltpu.roll(ds, HD // 2, 1))


def _placed(chunk, g, half0):
    own = jnp.where(half0 if g % 2 == 0 else jnp.logical_not(half0), chunk, 0.0)
    other = pltpu.roll(own, HD, 1)
    return (own, other) if g % 2 == 0 else (other, own)


def _unplace(acc, hf, g, half0):
    v = jnp.where(half0 if hf == 0 else jnp.logical_not(half0), acc, 0.0)
    return v if hf == g % 2 else pltpu.roll(v, HD, 1)


def _attn_fwd(proj, cos, sin, sinks, *, name, jobs=()):
    t = proj.shape[0]
    nb = t // L
    scale = HD ** -0.5

    def body(sink_ref, q_ref, kc_ref, kp_ref, vc_ref, vp_ref, cc_ref, sc_ref, cp_ref, sp_ref, o_ref, lse_ref):
        i = pl.program_id(0)
        lane, lo, half0 = _lane_consts()
        cos_c, sin_c, cos_p, sin_p = cc_ref[...], sc_ref[...], cp_ref[...], sp_ref[...]
        row = lax.broadcasted_iota(jnp.int32, (L, L), 0)
        col = lax.broadcasted_iota(jnp.int32, (L, L), 1)
        m_cur = col <= row
        m_prev = jnp.logical_and(col > row, i > 0)
        kc = [_rope(kc_ref[:, 128 * m:128 * (m + 1)], cos_c, sin_c, lo) for m in range(2)]
        kp = [_rope(kp_ref[:, 128 * m:128 * (m + 1)], cos_p, sin_p, lo) for m in range(2)]
        lse_acc = jnp.zeros((L, 128), F32)
        outs = [jnp.zeros((L, 128), F32) for _ in range(QD // 128)]
        qs = [(_rope(q_ref[:, 128 * ch:128 * (ch + 1)], cos_c, sin_c, lo) * scale).astype(BF16) for ch in range(QD // 128)]
        for g in range(NKV):
            kcv = [v.astype(BF16) for v in _placed(kc[g // 2], g, half0)]
            kpv = [v.astype(BF16) for v in _placed(kp[g // 2], g, half0)]
            vcv = [v.astype(BF16) for v in _placed(vc_ref[:, 128 * (g // 2):128 * (g // 2 + 1)], g, half0)]
            vpv = [v.astype(BF16) for v in _placed(vp_ref[:, 128 * (g // 2):128 * (g // 2 + 1)], g, half0)]
            for r in range(NQH // NKV):
                h = g * (NQH // NKV) + r
                ch, hf = h // 2, h % 2
                s_c = jnp.where(m_cur, _dot_nt(qs[ch], kcv[hf]), NEG)
                s_p = jnp.where(m_prev, _dot_nt(qs[ch], kpv[hf]), NEG)
                sink = sink_ref[0, h]
                mx = jnp.maximum(jnp.maximum(jnp.max(s_c, axis=-1, keepdims=True), jnp.max(s_p, axis=-1, keepdims=True)), sink)
                e_c = jnp.exp(s_c - mx)
                e_p = jnp.exp(s_p - mx)
                den = jnp.sum(e_c, axis=-1, keepdims=True) + jnp.sum(e_p, axis=-1, keepdims=True) + jnp.exp(sink - mx)
                inv = 1.0 / den
                outs[ch] = outs[ch] + _dot((e_c * inv).astype(BF16), vcv[hf]) + _dot((e_p * inv).astype(BF16), vpv[hf])
                lse_acc = jnp.where(lane == h, mx + jnp.log(den), lse_acc)
        for ch in range(QD // 128):
            o_ref[:, 128 * ch:128 * (ch + 1)] = outs[ch].astype(BF16)
        lse_ref[...] = lse_acc

    prev = lambda i: jnp.maximum(i - 1, 0)
    tab_c = pl.BlockSpec((L, 128), lambda i: (i, 0))
    tab_p = pl.BlockSpec((L, 128), lambda i: (prev(i), 0))
    return _call(
        body, jobs=jobs, name=name,
        out_shape=(jax.ShapeDtypeStruct((t, QD), BF16), jax.ShapeDtypeStruct((t, 128), F32)),
        grid=(nb,),
        in_specs=[pl.BlockSpec(memory_space=pltpu.SMEM),
                  pl.BlockSpec((L, QD), lambda i: (i, O_Q // QD)),
                  pl.BlockSpec((L, KVD), lambda i: (i, O_K // KVD)), pl.BlockSpec((L, KVD), lambda i: (prev(i), O_K // KVD)),
                  pl.BlockSpec((L, KVD), lambda i: (i, O_V // KVD)), pl.BlockSpec((L, KVD), lambda i: (prev(i), O_V // KVD)),
                  tab_c, tab_c, tab_p, tab_p],
        out_specs=(pl.BlockSpec((L, QD), lambda i: (i, 0)), pl.BlockSpec((L, 128), lambda i: (i, 0))),
        compiler_params=_cp(("parallel",)),
    )(sinks, proj, proj, proj, proj, proj, cos, sin, cos, sin)


def _attn_bwd(proj, cos, sin, sinks, attn, lse, dattn, dproj, *, name, jobs=()):
    t = proj.shape[0]
    nb = t // L
    scale = HD ** -0.5

    def body(sink_ref, qi_ref, qn_ref, kc_ref, kp_ref, vc_ref, vp_ref, doi_ref, don_ref, oi_ref, on_ref,
             lsei_ref, lsen_ref, cc_ref, sc_ref, cp_ref, sp_ref, cn_ref, sn_ref, _, dqkv_ref, dsk_ref):
        i = pl.program_id(0)
        lane, lo, half0 = _lane_consts()
        half1 = jnp.logical_not(half0)
        cos_c, sin_c = cc_ref[...], sc_ref[...]
        row = lax.broadcasted_iota(jnp.int32, (L, L), 0)
        col = lax.broadcasted_iota(jnp.int32, (L, L), 1)
        m_cur = col <= row
        m_prev = jnp.logical_and(col > row, i > 0)
        m_next = jnp.logical_and(col > row, i < nb - 1)
        kc = [_rope(kc_ref[:, 128 * m:128 * (m + 1)], cos_c, sin_c, lo) for m in range(2)]
        kp = [_rope(kp_ref[:, 128 * m:128 * (m + 1)], cp_ref[...], sp_ref[...], lo) for m in range(2)]
        lse_i, lse_n = lsei_ref[...], lsen_ref[...]
        dk_acc = [jnp.zeros((L, 128), F32) for _ in range(2)]
        dv_acc = [jnp.zeros((L, 128), F32) for _ in range(2)]
        dsk_acc = jnp.zeros((1, 128), F32)
        lane1 = lax.broadcasted_iota(jnp.int32, (1, 128), 1)
        place = lambda chunk, g: [v.astype(BF16) for v in _placed(chunk, g, half0)]
        kcs = [place(kc[g // 2], g) for g in range(NKV)]
        kps = [place(kp[g // 2], g) for g in range(NKV)]
        vcs = [place(vc_ref[:, 128 * (g // 2):128 * (g // 2 + 1)], g) for g in range(NKV)]
        vps = [place(vp_ref[:, 128 * (g // 2):128 * (g // 2 + 1)], g) for g in range(NKV)]
        for ch in range(QD // 128):
            sl = slice(128 * ch, 128 * (ch + 1))
            q_i = (_rope(qi_ref[:, sl], cos_c, sin_c, lo) * scale).astype(BF16)
            q_n = (_rope(qn_ref[:, sl], cn_ref[...], sn_ref[...], lo) * scale).astype(BF16)
            do_i, do_n = doi_ref[:, sl], don_ref[:, sl]
            do_ib, do_nb = do_i.astype(BF16), do_n.astype(BF16)
            od_i = do_i * oi_ref[:, sl].astype(F32)
            od_n = do_n * on_ref[:, sl].astype(F32)
            dq_ch = jnp.zeros((L, 128), F32)
            for hf in range(2):
                h = 2 * ch + hf
                g = h // (NQH // NKV)
                hm = half0 if hf == 0 else half1
                kcv, kpv, vcv, vpv = kcs[g][hf], kps[g][hf], vcs[g][hf], vps[g][hf]
                dl_i = jnp.sum(jnp.where(hm, od_i, 0.0), axis=-1, keepdims=True)
                dl_n = jnp.sum(jnp.where(hm, od_n, 0.0), axis=-1, keepdims=True)
                ls_i = jnp.sum(jnp.where(lane == h, lse_i, 0.0), axis=-1, keepdims=True)
                ls_n = jnp.sum(jnp.where(lane == h, lse_n, 0.0), axis=-1, keepdims=True)
                p_c = jnp.where(m_cur, jnp.exp(_dot_nt(q_i, kcv) - ls_i), 0.0)
                p_p = jnp.where(m_prev, jnp.exp(_dot_nt(q_i, kpv) - ls_i), 0.0)
                ds_c = (p_c * (_dot_nt(do_ib, vcv) - dl_i)).astype(BF16)
                ds_p = (p_p * (_dot_nt(do_ib, vpv) - dl_i)).astype(BF16)
                dq_ch = dq_ch + jnp.where(hm, (_dot(ds_c, kcv) + _dot(ds_p, kpv)) * scale, 0.0)
                sink = sink_ref[0, h]
                dsk = -jnp.sum(jnp.exp(sink - ls_i) * dl_i, axis=0, keepdims=True)
                dsk_acc = dsk_acc + jnp.where(lane1 == h, dsk, 0.0)
                p_n = jnp.where(m_next, jnp.exp(_dot_nt(q_n, kcv) - ls_n), 0.0)
                ds_n = (p_n * (_dot_nt(do_nb, vcv) - dl_n)).astype(BF16)
                dv_h = _dot_tn(p_c.astype(BF16), do_ib) + _dot_tn(p_n.astype(BF16), do_nb)
                dk_h = _dot_tn(ds_c, q_i) + _dot_tn(ds_n, q_n)
                dv_acc[g // 2] = dv_acc[g // 2] + _unplace(dv_h, hf, g, half0)
                dk_acc[g // 2] = dk_acc[g // 2] + _unplace(dk_h, hf, g, half0)
            dqkv_ref[:, sl] = _rope_t(dq_ch, cos_c, sin_c, lo).astype(BF16)
        for m in range(2):
            dqkv_ref[:, QD + 128 * m:QD + 128 * (m + 1)] = _rope_t(dk_acc[m], cos_c, sin_c, lo).astype(BF16)
            dqkv_ref[:, QD + KVD + 128 * m:QD + KVD + 128 * (m + 1)] = dv_acc[m].astype(BF16)

        @pl.when(i == 0)
        def _():
            dsk_ref[...] = jnp.zeros_like(dsk_ref)

        dsk_ref[...] += jnp.broadcast_to(dsk_acc, dsk_ref.shape)

    prev = lambda i: jnp.maximum(i - 1, 0)
    nxt = lambda i: jnp.minimum(i + 1, nb - 1)
    cur_q = pl.BlockSpec((L, QD), lambda i: (i, 0))
    nxt_q = pl.BlockSpec((L, QD), lambda i: (nxt(i), 0))
    tab = lambda f: pl.BlockSpec((L, 128), lambda i: (f(i), 0))
    ident = lambda i: i
    qkv = QD + 2 * KVD
    assert O_K == O_Q + QD and O_V == O_K + KVD and O_Q % qkv == 0
    return _call(
        body, jobs=jobs, name=name,
        out_shape=(jax.ShapeDtypeStruct(dproj.shape, BF16), jax.ShapeDtypeStruct((8, 128), F32)),
        grid=(nb,),
        in_specs=[pl.BlockSpec(memory_space=pltpu.SMEM),
                  pl.BlockSpec((L, QD), lambda i: (i, O_Q // QD)), pl.BlockSpec((L, QD), lambda i: (nxt(i), O_Q // QD)),
                  pl.BlockSpec((L, KVD), lambda i: (i, O_K // KVD)), pl.BlockSpec((L, KVD), lambda i: (prev(i), O_K // KVD)),
                  pl.BlockSpec((L, KVD), lambda i: (i, O_V // KVD)), pl.BlockSpec((L, KVD), lambda i: (prev(i), O_V // KVD)),
                  cur_q, nxt_q, cur_q, nxt_q, tab(ident), tab(nxt),
                  tab(ident), tab(ident), tab(prev), tab(prev), tab(nxt), tab(nxt), ANY],
        out_specs=(pl.BlockSpec((L, qkv), lambda i: (i, O_Q // qkv)), pl.BlockSpec((8, 128), lambda i: (0, 0))),
        compiler_params=_cp(("arbitrary",)), aliases={19: 0},
    )(sinks, proj, proj, proj, proj, proj, proj, dattn, dattn, attn, attn, lse, lse, cos, sin, cos, sin, cos, sin, dproj)


PAIRS = NH // NG // 2


def _softplus(x):
    return jnp.maximum(x, 0.0) + jnp.log(1.0 + jnp.exp(-jnp.abs(x)))


def _ssd_chunk(g, xps, dtr, bm, cm, sps, dtb, alog, dsk):
    lane = lax.broadcasted_iota(jnp.int32, (L, 128), 1)
    lane1 = lax.broadcasted_iota(jnp.int32, (1, 128), 1)
    row = lax.broadcasted_iota(jnp.int32, (L, L), 0)
    col = lax.broadcasted_iota(jnp.int32, (L, L), 1)
    rowc = lax.broadcasted_iota(jnp.int32, (128, 1), 0)
    tril = col <= row
    dt = _softplus(dtr + dtb)
    a = dt * (-jnp.exp(alog))
    a_cs = lax.dot_general(tril.astype(F32), a, (((1,), (0,)), ((), ())), precision=lax.Precision.HIGHEST,
                           preferred_element_type=F32)
    a_cst = a_cs.T
    a_last = jnp.sum(jnp.where(row == L - 1, a_cs, 0.0), axis=0, keepdims=True)
    cb = _dot_nt(cm.astype(BF16), bm.astype(BF16))
    ys, snew = [], []
    for q in range(PAIRS):
        xp, sp = xps[q], sps[q]
        y_pair = jnp.zeros((L, 128), F32)
        st_pair = jnp.zeros((128, NS), F32)
        keep = jnp.zeros((128, 1), F32)
        for hh in range(2):
            h = g * 2 * PAIRS + 2 * q + hh
            hm = (lane < HD) if hh == 0 else (lane >= HD)
            rm = (rowc < HD) if hh == 0 else (rowc >= HD)
            dt_h = jnp.sum(jnp.where(lane == h, dt, 0.0), axis=1, keepdims=True)
            acs_h = jnp.sum(jnp.where(lane == h, a_cs, 0.0), axis=1, keepdims=True)
            acst_h = jnp.sum(jnp.where(row == h, a_cst, 0.0), axis=0, keepdims=True)
            al_h = jnp.sum(jnp.where(lane1 == h, a_last, 0.0), axis=1, keepdims=True)
            dsk_h = jnp.sum(jnp.where(lane1 == h, dsk, 0.0), axis=1, keepdims=True)
            decay = jnp.where(tril, jnp.exp(jnp.where(tril, acs_h - acst_h, 0.0)), 0.0)
            xh = jnp.where(hm, xp, 0.0)
            xd = (xh * dt_h).astype(BF16)
            y = _dot((cb * decay).astype(BF16), xd)
            y = y + jnp.where(hm, _dot_nt((cm * jnp.exp(acs_h)).astype(BF16), sp.astype(BF16)), 0.0)
            y_pair = y_pair + y + dsk_h * xh
            st_pair = st_pair + _dot_tn(xd, (bm * jnp.exp(al_h - acs_h)).astype(BF16))
            keep = keep + jnp.where(rm, jnp.exp(al_h), 0.0)
        ys.append(y_pair)
        snew.append(sp * keep + st_pair)
    return ys, snew


def _ssd_specs(t):
    nc = t // L
    xs = lambda f: pl.BlockSpec((L, 128 * PAIRS), lambda c, g: (f(c), g))
    bspec = lambda f: pl.BlockSpec((L, NS), lambda c, g: (f(c), DI // NS + g))
    cspec = lambda f: pl.BlockSpec((L, NS), lambda c, g: (f(c), DI // NS + NG + g))
    dts = lambda f: pl.BlockSpec((L, 128), lambda c, g: (f(c), O_DT // 128))
    par = pl.BlockSpec((1, 128), lambda c, g: (0, 0))
    st = lambda f: pl.BlockSpec((1, 1, PAIRS, 128, NS), lambda c, g: (f(c), g, 0, 0, 0))
    return nc, xs, bspec, cspec, dts, par, st


def _ssd_fwd(xbc_act, proj, dtb, alog, dsk, *, name, jobs=()):
    t = proj.shape[0]
    nc, xs, bspec, cspec, dts, par, st = _ssd_specs(t)
    ident = lambda c: c

    def body(x_ref, b_ref, c_ref, dt_ref, dtb_ref, al_ref, dsk_ref, y_ref, sin_ref, s_ref):
        c, g = pl.program_id(0), pl.program_id(1)

        @pl.when(c == 0)
        def _():
            s_ref[g] = jnp.zeros((PAIRS, 128, NS), F32)

        sps = [s_ref[g, q] for q in range(PAIRS)]
        for q in range(PAIRS):
            sin_ref[0, 0, q] = sps[q]
        xps = [x_ref[:, 128 * q:128 * (q + 1)] for q in range(PAIRS)]
        ys, snew = _ssd_chunk(g, xps, dt_ref[...], b_ref[...], c_ref[...], sps, dtb_ref[...], al_ref[...], dsk_ref[...])
        for q in range(PAIRS):
            y_ref[:, 128 * q:128 * (q + 1)] = ys[q]
            s_ref[g, q] = snew[q]

    return _call(
        body, jobs=jobs, name=name,
        out_shape=(jax.ShapeDtypeStruct((t, DI), F32), jax.ShapeDtypeStruct((nc, NG, PAIRS, 128, NS), F32)),
        grid=(nc, NG),
        in_specs=[xs(ident), bspec(ident), cspec(ident), dts(ident), par, par, par],
        out_specs=(pl.BlockSpec((L, 128 * PAIRS), lambda c, g: (c, g)), st(ident)),
        scratch_shapes=[pltpu.VMEM((NG, PAIRS, 128, NS), F32)],
        compiler_params=_cp(("arbitrary", "arbitrary")),
    )(xbc_act, xbc_act, xbc_act, proj, dtb, alog, dsk)


def _ssd_bwd(xbc_act, proj, dtb, alog, dsk, states, dy, dproj, *, name, jobs=()):
    t = proj.shape[0]
    nc, xs, bspec, cspec, dts, par, st = _ssd_specs(t)
    rev = lambda c: nc - 1 - c

    def body(x_ref, b_ref, c_ref, dt_ref, dtb_ref, al_ref, dsk_ref, sin_ref, dy_ref, _,
             dx_ref, db_ref, dc_ref, ddtp_ref, ddtb_ref, dal_ref, ddsk_ref, ds_ref, ddt_ref):
        c, g = pl.program_id(0), pl.program_id(1)

        @pl.when(c == 0)
        def _():
            ds_ref[g] = jnp.zeros((PAIRS, 128, NS), F32)

        @pl.when(jnp.logical_and(c == 0, g == 0))
        def _():
            ddtb_ref[...] = jnp.zeros_like(ddtb_ref)
            dal_ref[...] = jnp.zeros_like(dal_ref)
            ddsk_ref[...] = jnp.zeros_like(ddsk_ref)

        @pl.when(g == 0)
        def _():
            ddt_ref[...] = jnp.zeros_like(ddt_ref)

        sps = [sin_ref[0, 0, q] for q in range(PAIRS)]
        xps = [x_ref[:, 128 * q:128 * (q + 1)] for q in range(PAIRS)]
        _, vjp = jax.vjp(functools.partial(_ssd_chunk, g), xps, dt_ref[...], b_ref[...], c_ref[...], sps,
                         dtb_ref[...], al_ref[...], dsk_ref[...])
        dys = [dy_ref[:, 128 * q:128 * (q + 1)] for q in range(PAIRS)]
        dss = [ds_ref[g, q] for q in range(PAIRS)]
        dxps, ddt, db, dc, dsps, ddtb, dal, ddsk = vjp((dys, dss))
        for q in range(PAIRS):
            dx_ref[:, 128 * q:128 * (q + 1)] = dxps[q]
            ds_ref[g, q] = dsps[q]
        db_ref[...] = db
        dc_ref[...] = dc
        ddt_ref[...] += ddt
        ddtb_ref[...] += jnp.broadcast_to(ddtb, ddtb_ref.shape)
        dal_ref[...] += jnp.broadcast_to(dal, dal_ref.shape)
        ddsk_ref[...] += jnp.broadcast_to(ddsk, ddsk_ref.shape)

        @pl.when(g == NG - 1)
        def _():
            ddtp_ref[:, :128] = ddt_ref[...].astype(BF16)
            ddtp_ref[:, 128:] = jnp.zeros((L, DT_PAD - 128), BF16)

    acc = pl.BlockSpec((8, 128), lambda c, g: (0, 0))
    o8 = jax.ShapeDtypeStruct((8, 128), F32)
    return _call(
        body, jobs=jobs, name=name,
        out_shape=(jax.ShapeDtypeStruct((t, DI), F32), jax.ShapeDtypeStruct((t, NG * NS), F32),
                   jax.ShapeDtypeStruct((t, NG * NS), F32), jax.ShapeDtypeStruct(dproj.shape, BF16), o8, o8, o8),
        grid=(nc, NG),
        in_specs=[xs(rev), bspec(rev), cspec(rev), dts(rev), par, par, par, st(rev),
                  pl.BlockSpec((L, 128 * PAIRS), lambda c, g: (rev(c), g)), ANY],
        out_specs=(pl.BlockSpec((L, 128 * PAIRS), lambda c, g: (rev(c), g)),
                   pl.BlockSpec((L, NS), lambda c, g: (rev(c), g)), pl.BlockSpec((L, NS), lambda c, g: (rev(c), g)),
                   pl.BlockSpec((L, DT_PAD), lambda c, g: (rev(c), O_DT // DT_PAD)), acc, acc, acc),
        scratch_shapes=[pltpu.VMEM((NG, PAIRS, 128, NS), F32), pltpu.VMEM((L, 128), F32)],
        compiler_params=_cp(("arbitrary", "arbitrary")), aliases={9: 3},
    )(xbc_act, xbc_act, xbc_act, proj, dtb, alog, dsk, states, dy, dproj)


def _pad_lanes(v, n=128):
    return jnp.pad(v, ((0, 0), (0, n - v.shape[1])))


class _LocalPlan:
    core = 0

    def __init__(self, big):
        self.big, self.grad, self.halves = big, {}, {}

    def w(self, n):
        return self.big[n]

    def g(self, n, a):
        self.grad[n] = a

    def g_half(self, n, which, a):
        self.halves[which] = a
        if len(self.halves) == 2:
            self.grad[n] = jnp.concatenate([self.halves["keep"], self.halves["send"]], axis=0)

    def jobs(self, tag):
        return ()


def _local_step(x, p, positions, target, small, plan):
    t = x.shape[0]
    cos, sin = _rope_tables(positions, t)
    dtb, alog, dsk = _pad_lanes(small["dt_bias"]), _pad_lanes(small["a_log"]), _pad_lanes(small["d_skip"])
    w, jobs = plan.w, plan.jobs

    def mm(a, b, *, name, tm=t, **kw):
        return _matmul(a, b, tm=tm, tn=512, name=name, jobs=jobs(name), **kw)

    def dw(wname, a, dy, *, name, tm):
        plan.g(wname, _matmul(a, dy, ta=True, out_dtype=BF16, tm=tm, tn=512, tk=t, name=name, jobs=jobs(name)))

    u = _rmsnorm_fwd(x, small["g_mix"], name="norm_mix")
    proj = mm(u, w("w_in"), tk=D, name="mm_in")
    attn, lse = _attn_fwd(proj, cos, sin, small["sinks"], name="attn_fwd", jobs=jobs("attn_fwd"))
    out_a = mm(attn, w("w_attn_br"), tk=QD, name="mm_attn_br")
    xbc_act = _conv_fwd(proj, small["conv_w"], small["conv_b"], name="conv_fwd")
    y_pre, states = _ssd_fwd(xbc_act, proj, dtb, alog, dsk, name="ssd_fwd", jobs=jobs("ssd_fwd"))
    yn = _gated_norm_fwd(y_pre, proj, small["g_ssd"], name="gated_norm_fwd")
    out_s = mm(yn, w("w_ssd_br"), tk=DI, name="mm_ssd_br")
    merged = _merge_fwd(proj, out_a, out_s, name="merge_fwd")
    h1 = mm(merged, w("w_o"), add=x, tk=D, name="mm_o")
    f = _rmsnorm_fwd(h1, small["g_ffn"], name="norm_ffn")
    gate, up, act = _swiglu_fwd(f, w("w_gate"), w("w_up"), name="swiglu_fwd", jobs=jobs("swiglu_fwd"))
    h2 = mm(act, w("w_down"), add=h1, tm=t // 2, tk=FFN // 2, name="mm_down")
    e = _rmsnorm_fwd(h2, small["g_ple"], name="norm_ple")
    pgl = mm(e, w("w_ple_gate"), tk=D, name="mm_ple_gate")
    pb = p.astype(BF16)
    pp = mm(pb, w("w_ple_proj"), tk=PLE, name="mm_ple_proj")
    dh3, dpgl, dpp, loss, dg_final = _final(h2, pgl, pp, target, small["g_final"].reshape(1, D), name="final")

    dw("w_ple_proj", pb, dpp, tm=PLE, name="mm_d_ple_proj")
    dw("w_ple_gate", e, dpgl, tm=D, name="mm_d_ple_gate")
    de = mm(dpgl, w("w_ple_gate"), tb=True, tk=D, name="mm_de")
    dh2, dh2b, dg_ple = _rmsnorm_bwd(h2, small["g_ple"], de, dh3, name="norm_ple_bwd", jobs=jobs("norm_ple_bwd"))
    dw("w_down", act, dh2b, tm=FFN // 2, name="mm_d_down")
    dact = mm(dh2b, w("w_down"), tb=True, tk=D, name="mm_dact")
    dgate, dup = _swiglu_bwd(gate, up, dact, name="swiglu_bwd", jobs=jobs("swiglu_bwd"))
    dw("w_gate", f, dgate, tm=D, name="mm_d_gate")
    dw("w_up", f, dup, tm=D, name="mm_d_up")
    df = mm(dgate, w("w_gate"), tb=True, tm=t // 2, tk=FFN // 2, name="mm_df_gate")
    df = mm(dup, w("w_up"), tb=True, add=df, tm=t // 2, tk=FFN // 2, name="mm_df_up")
    dh1, dh1b, dg_ffn = _rmsnorm_bwd(h1, small["g_ffn"], df, dh2, name="norm_ffn_bwd", jobs=jobs("norm_ffn_bwd"))
    dw("w_o", merged, dh1b, tm=D, name="mm_d_o")
    dmerged = mm(dh1b, w("w_o"), tb=True, tk=D, name="mm_dmerged")
    dout_a, dout_s, dproj = _merge_bwd(proj, out_a, out_s, dmerged, name="merge_bwd")
    dw("w_attn_br", attn, dout_a, tm=QD, name="mm_d_attn_br")
    dw("w_ssd_br", yn, dout_s, tm=DI, name="mm_d_ssd_br")
    dattn = mm(dout_a, w("w_attn_br"), tb=True, tk=D, name="mm_dattn")
    dyn = mm(dout_s, w("w_ssd_br"), tb=True, tk=D, name="mm_dyn")
    dproj, dsinks = _attn_bwd(proj, cos, sin, small["sinks"], attn, lse, dattn, dproj, name="attn_bwd",
                              jobs=jobs("attn_bwd"))
    dy_pre, dproj, dg_ssd = _gated_norm_bwd(y_pre, proj, small["g_ssd"], dyn, dproj, name="gated_norm_bwd",
                                            jobs=jobs("gated_norm_bwd"))
    dxs, db, dc, dproj, ddtb, dalog, ddsk = _ssd_bwd(xbc_act, proj, dtb, alog, dsk, states, dy_pre, dproj, name="ssd_bwd",
                                                     jobs=jobs("ssd_bwd"))
    dproj, dconv_w, dconv_b = _conv_bwd(proj, small["conv_w"], small["conv_b"], dxs, db, dc, dproj, name="conv_bwd",
                                        jobs=jobs("conv_bwd"))
    for which, h in (("send", 1 - plan.core), ("keep", plan.core)):
        uh = lax.dynamic_slice_in_dim(u, h * (D // 2), D // 2, axis=1)
        name = "mm_d_in_" + which
        plan.g_half("w_in", which, _matmul(uh, dproj, ta=True, out_dtype=BF16, tm=D // 2, tn=512, tk=t, name=name,
                                           jobs=jobs(name)))
    du = mm(dproj, w("w_in"), tb=True, tm=t // 2, tk=NP // 4, name="mm_du")
    grad_x, _, dg_mix = _rmsnorm_bwd(x, small["g_mix"], du, dh1, name="norm_mix_bwd", jobs=jobs("norm_mix_bwd"))

    gs = {
        "g_mix": dg_mix[:1], "conv_w": dconv_w[:CW], "conv_b": dconv_b[:1], "dt_bias": ddtb[:1, :NH],
        "a_log": dalog[:1, :NH], "d_skip": ddsk[:1, :NH], "g_ssd": dg_ssd[:1], "sinks": dsinks[:1, :NQH],
        "g_ffn": dg_ffn[:1], "g_ple": dg_ple[:1], "g_final": dg_final[0],
    }
    return loss, grad_x, gs


def _to_kernel_cols(w):
    seg = lambda o, n: w[:, o:o + n]
    return jnp.concatenate([seg(R_GA, D), seg(R_GS, D), seg(R_Z, DI), seg(R_XBC, CONV), seg(R_Q, QD), seg(R_K, KVD),
                            seg(R_V, KVD), seg(R_DT, NH), jnp.zeros((w.shape[0], DT_PAD - NH), w.dtype)], axis=1)


def _from_kernel_cols(g):
    seg = lambda o, n: g[:, o:o + n]
    return jnp.concatenate([seg(O_Q, QD), seg(O_K, KVD), seg(O_V, KVD), seg(O_Z, DI), seg(O_XBC, CONV), seg(O_DT, NH),
                            seg(O_GA, D), seg(O_GS, D)], axis=1)


def _shard_pieces():
    segs = ((R_Q, QD, O_Q), (R_K, KVD, O_K), (R_V, KVD, O_V), (R_Z, DI, O_Z), (R_XBC, CONV, O_XBC), (R_DT, NH, O_DT),
            (R_GA, D, O_GA), (R_GS, D, O_GS))
    cs = IN_DIM // NCHIP
    out = []
    for j in range(NCHIP):
        for r0, n, k0 in segs:
            lo, hi = max(r0, j * cs), min(r0 + n, (j + 1) * cs)
            if lo < hi:
                out.append((j, lo - j * cs, hi - lo, k0 + lo - r0))
    return out


def _slabs_to_kernel_cols(slabs):
    pieces = sorted(_shard_pieces(), key=lambda p: p[3])
    cols, at = [], 0
    for j, a, n, k0 in pieces:
        if k0 > at:
            cols.append(jnp.zeros((slabs.shape[1], k0 - at), slabs.dtype))
        cols.append(slabs[j, :, a:a + n])
        at = k0 + n
    cols.append(jnp.zeros((slabs.shape[1], NP - at), slabs.dtype))
    return jnp.concatenate(cols, axis=1)


def _kernel_cols_to_slabs(g):
    pieces = _shard_pieces()
    return jnp.stack([jnp.concatenate([g[:, k0:k0 + n] for j, a, n, k0 in pieces if j == s], axis=1)
                      for s in range(NCHIP)])


RELS = ((0, 1), (1, 0), (1, 1))
MATS = {
    n: (n, kind, 1, r, c, tp, tf) for n, kind, r, c, tp, tf in (
        ("w_in", "stk", 2048, 2696, 256, 256),
        ("w_attn_br", "col", 1024, 512, 256, 256),
        ("w_ssd_br", "row", 512, 2048, 512, 256),
        ("w_o", "row", 512, 2048, 512, 256),
        ("w_gate", "col", 2048, 1408, 256, 256),
        ("w_up", "col", 2048, 1408, 256, 256),
        ("w_down", "row", 1408, 2048, 704, 704),
        ("w_ple_gate", "row", 512, 2048, 512, 256),
        ("w_ple_proj", "col", 256, 512, 128, 128),
    )}


def _pos():
    return lax.axis_index("x"), lax.axis_index("y"), lax.axis_index("c")


def _flip(v, a):
    return 1 - v if a else v


def _remote(src, dst, send, recv, dev):
    return pltpu.make_async_remote_copy(src_ref=src, dst_ref=dst, send_sem=send, recv_sem=recv, device_id=dev,
                                        device_id_type=MESH)


def _whole_shape(kind, g, r, c):
    return {"row": (g, NCHIP * r, c), "col": (g, r, NCHIP * c), "stk": (NCHIP, r, c)}[kind]


def _cols(j, c):
    return pl.ds(pl.multiple_of(j * c, 128), c)


def _whole_shard(kind, ref, j, r, c):
    if kind == "row":
        return ref.at[:, pl.ds(j * r, r), :]
    if kind == "col":
        return ref.at[:, :, _cols(j, c)]
    return ref.at[pl.ds(j, 1)]


def _whole_rows(kind, ref, j, row, n, r, c):
    if kind == "row":
        return ref.at[:, pl.ds(j * r + row, n), :]
    if kind == "col":
        return ref.at[:, pl.ds(row, n), _cols(j, c)]
    return ref.at[pl.ds(j, 1), pl.ds(row, n), :]


class _GatherJob(_Job):
    has_mid = True
    NCP = 10

    def __init__(self, names, shards, sink):
        self.mats = [MATS[n] for n in names]
        self.srcs = [shards[n] for n in names]
        self.news = [jax.ShapeDtypeStruct(_whole_shape(kind, g, r, c), BF16) for _, kind, g, r, c, _, _ in self.mats]
        n = len(names)
        self.scratch = [pltpu.SemaphoreType.DMA((self.NCP * n,)), pltpu.SemaphoreType.DMA((self.NCP * n,))]
        self.names, self.sink = names, sink

    def _copies(self, srcs, news, sems):
        send, recv = sems
        x, y, c = _pos()
        me, jx, jy, jd = 2 * x + y, 2 * (1 - x) + y, 2 * x + (1 - y), 2 * (1 - x) + (1 - y)
        nbx, nby, sib = (1 - x, y, c), (x, 1 - y, c), (x, y, 1 - c)
        cps = []
        for w, (_, kind, g, r, cc, _, _) in enumerate(self.mats):
            hr, qr = r // 2, r // 4
            at = lambda j, h, q, n: _whole_rows(kind, news[w], j, h * hr + q * qr, n, r, cc)
            mine = lambda q: srcs[w].at[:, pl.ds(c * hr + q * qr, qr), :]
            cp = lambda k, s, d, dev: _remote(s, d, send.at[self.NCP * w + k], recv.at[self.NCP * w + k], dev)
            cps.append([
                cp(0, mine(0), at(me, c, 0, qr), nbx), cp(1, mine(1), at(me, c, 1, qr), nbx),
                cp(2, mine(1), at(me, c, 1, qr), nby), cp(3, mine(0), at(me, c, 0, qr), nby),
                cp(4, at(jx, c, 0, qr), at(jx, c, 0, qr), nby), cp(5, at(jy, c, 1, qr), at(jy, c, 1, qr), nbx),
                cp(6, at(jx, c, 0, hr), at(jx, c, 0, hr), sib), cp(7, at(jy, c, 0, hr), at(jy, c, 0, hr), sib),
                cp(8, at(jd, c, 0, hr), at(jd, c, 0, hr), sib),
                cp(9, srcs[w], _whole_shard(kind, news[w], me, r, cc), sib)])
        return cps

    def start(self, srcs, dsts, news, sems):
        cps = self._copies(srcs, news, sems)
        for w in range(len(self.mats)):
            for k in (0, 1, 2, 3, 9):
                cps[w][k].start()

    def mid(self, srcs, dsts, news, sems):
        cps = self._copies(srcs, news, sems)
        for w in range(len(self.mats)):
            cps[w][0].wait_recv()
            cps[w][4].start()
            cps[w][2].wait_recv()
            cps[w][5].start()

    def finish(self, srcs, dsts, news, sems):
        cps = self._copies(srcs, news, sems)
        for w in range(len(self.mats)):
            cps[w][1].wait_recv()
            cps[w][6].start()
            cps[w][3].wait_recv()
            cps[w][7].start()
        for w in range(len(self.mats)):
            cps[w][4].wait_recv()
            cps[w][5].wait_recv()
            cps[w][8].start()
        for w in range(len(self.mats)):
            for k in (6, 7, 8, 9):
                cps[w][k].wait_recv()
            for k in range(self.NCP):
                cps[w][k].wait_send()

    def done(self, dsts, news):
        for n, a in zip(self.names, news):
            self.sink[n] = a


class _SwapJob(_Job):
    def __init__(self, build, ncopies, *, srcs=(), dsts=(), news=(), done=None):
        self.build, self.srcs, self.dsts, self.news, self._done = build, list(srcs), list(dsts), list(news), done
        self.scratch = [pltpu.SemaphoreType.DMA((ncopies,)), pltpu.SemaphoreType.DMA((ncopies,))]

    def start(self, srcs, dsts, news, sems):
        for cp in self.build(srcs, dsts, news, *sems):
            cp.start()

    def finish(self, srcs, dsts, news, sems):
        for cp in self.build(srcs, dsts, news, *sems):
            cp.wait()

    def done(self, dsts, news):
        if self._done is not None:
            self._done(dsts, news)


def _half_of_whole(kind, ref, h, r, c):
    if kind == "row":
        return ref.at[:, :, pl.ds(pl.multiple_of(h * (c // 2), 128), c // 2)]
    return ref.at[:, pl.ds(h * (r // 2), r // 2), :]


def _half_shape(kind, g, r, c):
    return {"row": (g, NCHIP * r, c // 2), "col": (g, r // 2, NCHIP * c), "stk": (NCHIP, r // 2, c)}[kind]


def _piece_shape(kind, g, r, c):
    return {"row": (g, r, c // 2), "col": (g, r // 2, c), "stk": (1, r // 2, c)}[kind]


def _piece_of_half(kind, ref, j, r, c):
    if kind == "row":
        return ref.at[:, pl.ds(j * r, r), :]
    if kind == "col":
        return ref.at[:, :, _cols(j, c)]
    return ref.at[pl.ds(j, 1)]


def _half_of_shard(kind, ref, h, r, c):
    if kind == "row":
        return ref.at[:, :, pl.ds(pl.multiple_of(h * (c // 2), 128), c // 2)]
    return ref.at[:, pl.ds(h * (r // 2), r // 2), :]


def _pair_sum(pack, core, mine, got, whole=True):
    name, kind, g, r, c, tr, _ = pack
    hs = _half_shape(kind, g, r, c)
    nb = hs[1] // tr

    def body(core_ref, a_ref, b_ref, o_ref):
        o_ref[...] = (a_ref[...].astype(F32) + b_ref[...].astype(F32)).astype(BF16)

    blk = (1, tr, hs[2])
    same = lambda gi, i, core_ref: (gi, i, 0)
    if not whole:
        a_map = same
    elif kind == "row":
        a_map = lambda gi, i, core_ref: (gi, i, core_ref[0])
    else:
        a_map = lambda gi, i, core_ref: (gi, core_ref[0] * nb + i, 0)
    return pl.pallas_call(
        body, name="pair_sum_" + name, out_shape=jax.ShapeDtypeStruct(hs, BF16),
        grid_spec=pltpu.PrefetchScalarGridSpec(
            num_scalar_prefetch=1, grid=(hs[0], nb),
            in_specs=[pl.BlockSpec(blk, a_map), pl.BlockSpec(blk, same)], out_specs=pl.BlockSpec(blk, same)),
        compiler_params=_cp(("parallel", "parallel")),
    )(core, mine, got)


def _shard_sum(pack, where, half, got):
    name, kind, g, r, c, _, tr = pack
    ps = _piece_shape(kind, g, r, c)
    nb = ps[1] // tr

    def body(where_ref, a_ref, b_ref, o_ref):
        o_ref[...] = a_ref[...].astype(F32) + ((b_ref[0].astype(F32) + b_ref[1].astype(F32)) + b_ref[2].astype(F32))

    blk = (1, tr, ps[2])
    if kind == "row":
        a_map = lambda gi, i, wr: (gi, wr[0] * nb + i, 0)
        o_map = lambda gi, i, wr: (gi, i, wr[1])
    elif kind == "col":
        a_map = lambda gi, i, wr: (gi, i, wr[0])
        o_map = lambda gi, i, wr: (gi, wr[1] * nb + i, 0)
    else:
        a_map = lambda gi, i, wr: (wr[0], i, 0)
        o_map = lambda gi, i, wr: (gi, wr[1] * nb + i, 0)
    return pl.pallas_call(
        body, name="shard_sum_" + name, out_shape=jax.ShapeDtypeStruct((g, r, c), F32),
        grid_spec=pltpu.PrefetchScalarGridSpec(
            num_scalar_prefetch=1, grid=(ps[0], nb),
            in_specs=[pl.BlockSpec(blk, a_map), pl.BlockSpec((3,) + blk, lambda gi, i, wr: (0, gi, i, 0))],
            out_specs=pl.BlockSpec(blk, o_map)),
        compiler_params=_cp(("parallel", "parallel")),
    )(where, half, got)


class _Plan:
    def __init__(self, shards, table):
        self.shards, self.table = shards, table
        self.whole, self.grad, self.got_a, self.half, self.got_b, self.sent_b, self.gshard = {}, {}, {}, {}, {}, {}, {}
        x, y, c = _pos()
        self.core = c
        self.core1 = c.reshape(1).astype(jnp.int32)
        self.where = jnp.stack([2 * x + y, c]).astype(jnp.int32)
        self._w_in = None
        self.send, self.keep = {}, {}

    def w(self, n):
        if n != "w_in":
            return self.whole[n][0]
        if self._w_in is None:
            self._w_in = _slabs_to_kernel_cols(self.whole[n])
        return self._w_in

    def g(self, n, a):
        self.grad[n] = a[None]

    def g_half(self, n, which, a):
        (self.send if which == "send" else self.keep)[n] = _kernel_cols_to_slabs(a)

    def jobs(self, tag):
        out = []
        for spec in self.table.get(tag, ()):
            out += getattr(self, "_" + spec[0])(*spec[1:])
        return out

    def run(self, name, jobs):
        if jobs:
            _call(lambda: None, jobs=jobs, name=name, out_shape=[], in_specs=[], out_specs=[])()

    def _gather(self, names):
        return [_GatherJob(names, self.shards, self.whole)]

    def _rs_a(self, names):
        mats = [MATS[n] for n in names]

        def build(srcs, dsts, news, send, recv):
            x, y, c = _pos()
            return [_remote(srcs[i] if names[i] in self.send else _half_of_whole(kind, srcs[i], 1 - c, r, cc), news[i],
                            send.at[i], recv.at[i], (x, y, 1 - c))
                    for i, (_, kind, g, r, cc, _, _) in enumerate(mats)]

        def done(dsts, news):
            self.got_a.update(zip(names, news))

        return [_SwapJob(build, len(names), srcs=[self.send.get(n, self.grad.get(n)) for n in names], done=done,
                         news=[jax.ShapeDtypeStruct(_half_shape(kind, g, r, c), BF16) for _, kind, g, r, c, _, _ in mats])]

    def _rs_b(self, names, ks=(0, 1, 2)):
        return [self._rs_b_one(n, ks) for n in names]

    def _rs_b_one(self, n, ks):
        _, kind, g, r, cc, _, _ = MATS[n]
        if n not in self.half:
            if n in self.keep:
                self.half[n] = _pair_sum(MATS[n], self.core1, self.keep[n], self.got_a[n], whole=False)
            else:
                self.half[n] = _pair_sum(MATS[n], self.core1, self.grad[n], self.got_a[n])

        def build(srcs, dsts, news, send, recv):
            x, y, c = _pos()
            land = (dsts or news)[0]
            cps = []
            for i, k in enumerate(ks):
                px, py = _flip(x, RELS[k][0]), _flip(y, RELS[k][1])
                cps.append(_remote(_piece_of_half(kind, srcs[0], 2 * px + py, r, cc), land.at[k], send.at[i], recv.at[i],
                                   (px, py, c)))
            return cps

        def done(dsts, news):
            self.got_b[n] = (dsts or news)[0]
            self.sent_b[n] = self.sent_b.get(n, ()) + tuple(ks)

        if n in self.got_b:
            return _SwapJob(build, len(ks), srcs=[self.half[n]], dsts=[self.got_b[n]], done=done)
        shape = jax.ShapeDtypeStruct((3,) + _piece_shape(kind, g, r, cc), BF16)
        return _SwapJob(build, len(ks), srcs=[self.half[n]], news=[shape], done=done)

    def _rs_c(self, names):
        mats = [MATS[n] for n in names]
        for n in names:
            assert sorted(self.sent_b[n]) == [0, 1, 2], (n, self.sent_b[n])
        parts = [_shard_sum(MATS[n], self.where, self.half[n], self.got_b[n]) for n in names]

        def build(srcs, dsts, news, send, recv):
            x, y, c = _pos()
            cps = []
            for i, (_, kind, g, r, cc, _, _) in enumerate(mats):
                mine = _half_of_shard(kind, dsts[i], c, r, cc)
                cps.append(_remote(mine, mine, send.at[i], recv.at[i], (x, y, 1 - c)))
            return cps

        def done(dsts, news):
            self.gshard.update(zip(names, dsts))

        return [_SwapJob(build, len(names), dsts=parts, done=done)]

    def finish(self, n):
        if n not in self.got_a:
            self.run("rs_a_" + n, self._rs_a((n,)))
        left = tuple(k for k in range(3) if k not in self.sent_b.get(n, ()))
        if left:
            self.run("rs_b_" + n, self._rs_b((n,), left))
        if n not in self.gshard:
            self.run("rs_c_" + n, self._rs_c((n,)))
        return self.gshard[n]


TABLE = {
    "gather_w_in": (("gather", ("w_in",)),),
    "mm_in": (("gather", ("w_attn_br", "w_ssd_br")),),
    "attn_fwd": (("gather", ("w_o",)),),
    "ssd_fwd": (("gather", ("w_gate",)),),
    "mm_ssd_br": (("gather", ("w_up",)),),
    "mm_o": (("gather", ("w_down",)),),
    "swiglu_fwd": (("gather", ("w_ple_gate", "w_ple_proj")),),
    "mm_de": (("rs_a", ("w_ple_proj", "w_ple_gate")),),
    "mm_d_down": (("rs_b", ("w_ple_proj", "w_ple_gate")),),
    "mm_dact": (("rs_a", ("w_down",)),),
    "swiglu_bwd": (("rs_c", ("w_ple_proj", "w_ple_gate")),),
    "mm_df_gate": (("rs_a", ("w_gate", "w_up")),),
    "mm_dmerged": (("rs_a", ("w_o",)),),
    "mm_dyn": (("rs_a", ("w_attn_br", "w_ssd_br")),),
    "attn_bwd": (("rs_b", ("w_down",)),),
    "gated_norm_bwd": (("rs_c", ("w_down",)),),
    "ssd_bwd": (("rs_b", ("w_gate", "w_up")),),
    "conv_bwd": (("rs_b", ("w_o",)),),
    "mm_d_in_send": (("rs_b", ("w_attn_br", "w_ssd_br")), ("rs_c", ("w_gate", "w_up"))),
    "mm_d_in_keep": (("rs_a", ("w_in",)), ("rs_c", ("w_o",))),
    "mm_du": (("rs_b", ("w_in",)),),
    "norm_mix_bwd": (("rs_c", ("w_attn_br", "w_ssd_br")),),
}


NDEV = 8


def _allreduce_small(v, *, name):
    rows = v.shape[0]

    def body(v_ref, o_ref, slots, send, recv):
        x, y, c = _pos()
        me = 4 * x + 2 * y + c
        slots[me] = v_ref[...]
        cps = []
        for k in range(1, NDEV):
            peer = (_flip(x, k & 4), _flip(y, k & 2), _flip(c, k & 1))
            cp = _remote(v_ref, slots.at[me], send.at[k - 1], recv.at[k - 1], peer)
            cp.start()
            cps.append(cp)
        for cp in cps:
            cp.wait()
        acc = slots[0]
        for s in range(1, NDEV):
            acc = acc + slots[s]
        o_ref[...] = acc

    return pl.pallas_call(
        body, name=name, out_shape=jax.ShapeDtypeStruct((rows, 128), F32),
        in_specs=[pl.BlockSpec(memory_space=pltpu.VMEM)], out_specs=pl.BlockSpec(memory_space=pltpu.VMEM),
        scratch_shapes=[pltpu.VMEM((NDEV, rows, 128), F32), pltpu.SemaphoreType.DMA((NDEV - 1,)),
                        pltpu.SemaphoreType.DMA((NDEV - 1,))],
    )(v)


def _adamw(w, g, m, v, *, name, tr=None, tc=None, jobs=()):
    r, c = w.shape
    tr = r if tr is None else tr
    c1 = 1.0 / (1.0 - B1 ** STEP)
    c2 = 1.0 / (1.0 - B2 ** STEP)

    def body(w_ref, g_ref, m_ref, v_ref, d_ref, mo_ref, vo_ref):
        gv = g_ref[...]
        mn = B1 * m_ref[...] + (1.0 - B1) * gv
        vn = B2 * v_ref[...] + (1.0 - B2) * (gv * gv)
        mo_ref[...] = mn
        vo_ref[...] = vn
        d_ref[...] = -LR * ((mn * c1) / (jnp.sqrt(vn * c2) + AEPS) + WD * w_ref[...])

    if tc is None:
        blk, grid = pl.BlockSpec((tr, c), lambda i: (i, 0)), (r // tr,)
    else:
        blk, grid = pl.BlockSpec((r, tc), lambda i: (0, i)), (c // tc,)
    o = jax.ShapeDtypeStruct((r, c), F32)
    return _call(
        body, jobs=jobs, name=name, out_shape=(o, o, o), grid=grid, in_specs=[blk] * 4, out_specs=(blk, blk, blk),
        compiler_params=_cp(("parallel",)),
    )(w, g, m, v)


WEIGHTS = ("g_mix", "w_in", "conv_w", "conv_b", "dt_bias", "a_log", "d_skip", "g_ssd", "sinks", "w_attn_br", "w_ssd_br",
           "w_o", "g_ffn", "w_gate", "w_up", "w_down", "g_ple", "w_ple_gate", "w_ple_proj", "g_final")
BIG = {
    "w_gate": 256, "w_up": 256, "w_down": 128, "w_ssd_br": 128, "w_o": 128, "w_ple_gate": 128, "w_attn_br": 256,
    "w_ple_proj": 256, "w_in": None,
}
SMALL = tuple(n for n in WEIGHTS if n not in BIG)


def _pack_small(parts):
    rows = []
    for a in parts:
        a = a.reshape(-1)
        rows.append(jnp.pad(a, (0, -a.shape[0] % 128)).reshape(-1, 128))
    out = jnp.concatenate(rows, axis=0)
    return jnp.pad(out, ((0, -out.shape[0] % 8), (0, 0)))


def _unpack_small(packed, shapes):
    out, r = [], 0
    for s in shapes:
        n = int(np.prod(s))
        nr = -(-n // 128)
        out.append(packed[r:r + nr].reshape(-1)[:n].reshape(s))
        r += nr
    return out


def kernel(x, p, positions, g_mix, w_in, conv_w, conv_b, dt_bias, a_log, d_skip, g_ssd, sinks, w_attn_br, w_ssd_br, w_o, g_ffn, w_gate, w_up, w_down, g_ple, w_ple_gate, w_ple_proj, g_final, loss_target, m_g_mix, m_w_in, m_conv_w, m_conv_b, m_dt_bias, m_a_log, m_d_skip, m_g_ssd, m_sinks, m_w_attn_br, m_w_ssd_br, m_w_o, m_g_ffn, m_w_gate, m_w_up, m_w_down, m_g_ple, m_w_ple_gate, m_w_ple_proj, m_g_final, v_g_mix, v_w_in, v_conv_w, v_conv_b, v_dt_bias, v_a_log, v_d_skip, v_g_ssd, v_sinks, v_w_attn_br, v_w_ssd_br, v_w_o, v_g_ffn, v_w_gate, v_w_up, v_w_down, v_g_ple, v_w_ple_gate, v_w_ple_proj, v_g_final):
    w = dict(zip(WEIGHTS, (g_mix, w_in, conv_w, conv_b, dt_bias, a_log, d_skip, g_ssd, sinks, w_attn_br, w_ssd_br, w_o,
                           g_ffn, w_gate, w_up, w_down, g_ple, w_ple_gate, w_ple_proj, g_final)))
    m = dict(zip(WEIGHTS, (m_g_mix, m_w_in, m_conv_w, m_conv_b, m_dt_bias, m_a_log, m_d_skip, m_g_ssd, m_sinks, m_w_attn_br,
                           m_w_ssd_br, m_w_o, m_g_ffn, m_w_gate, m_w_up, m_w_down, m_g_ple, m_w_ple_gate, m_w_ple_proj,
                           m_g_final)))
    v = dict(zip(WEIGHTS, (v_g_mix, v_w_in, v_conv_w, v_conv_b, v_dt_bias, v_a_log, v_d_skip, v_g_ssd, v_sinks, v_w_attn_br,
                           v_w_ssd_br, v_w_o, v_g_ffn, v_w_gate, v_w_up, v_w_down, v_g_ple, v_w_ple_gate, v_w_ple_proj,
                           v_g_final)))
    xi, yi, ci = _pos()
    chip = 2 * xi + yi
    t = x.shape[1]
    cshard = CONV // NCHIP

    plan = _Plan({n: w[n].astype(BF16) for n in MATS}, TABLE)
    plan.run("gather_w_in", plan.jobs("gather_w_in"))
    placed = lax.dynamic_update_slice(jnp.zeros((CW, CONV), F32), w["conv_w"][0], (0, chip * cshard))
    conv_whole = _allreduce_small(jnp.where(ci == 0, placed, 0.0).reshape(-1, 128), name="gather_conv_w").reshape(CW, CONV)

    small = {n: w[n] for n in ("g_mix", "conv_b", "dt_bias", "a_log", "d_skip", "g_ssd", "sinks", "g_ffn", "g_ple", "g_final")}
    small["conv_w"] = conv_whole
    loss8, grad_x, gs = _local_step(x[0], p[0, 0], positions, loss_target[0], small, plan)

    order = ("g_mix", "conv_b", "dt_bias", "a_log", "d_skip", "g_ssd", "sinks", "g_ffn", "g_ple", "g_final", "conv_w")
    summed = _allreduce_small(_pack_small([loss8[0, :1]] + [gs[n] for n in order]), name="sum_small")
    parts = _unpack_small(summed, [(1,)] + [w[n].shape for n in order[:-1]] + [(CW, CONV)])
    loss = parts[0][0]
    grad = dict(zip(order, parts[1:]))
    grad["conv_w"] = lax.dynamic_slice(grad["conv_w"], (0, chip * cshard), (CW, cshard))[None]

    delta, new_m, new_v = {}, {}, {}
    for n, tr in BIG.items():
        grad[n] = plan.finish(n)
        if n == "w_in":
            d_, m_, v_ = _adamw(w[n][0].T, grad[n][0].T, m[n][0].T, v[n][0].T, tc=128, name="adamw_" + n)
            d_, m_, v_ = d_.T, m_.T, v_.T
        else:
            d_, m_, v_ = _adamw(w[n][0], grad[n][0], m[n][0], v[n][0], tr=tr, name="adamw_" + n)
        delta[n], new_m[n], new_v[n] = d_[None], m_[None], v_[None]
    shapes = [w[n].shape for n in SMALL]
    d_, m_, v_ = _adamw(_pack_small([w[n] for n in SMALL]), _pack_small([grad[n] for n in SMALL]),
                        _pack_small([m[n] for n in SMALL]), _pack_small([v[n] for n in SMALL]), tr=None, name="adamw_small")
    for n, a, b, c_ in zip(SMALL, _unpack_small(d_, shapes), _unpack_small(m_, shapes), _unpack_small(v_, shapes)):
        delta[n], new_m[n], new_v[n] = a, b, c_

    return (loss, grad_x[None], *[grad[n] for n in WEIGHTS], *[delta[n] for n in WEIGHTS],
            *[new_m[n] for n in WEIGHTS], *[new_v[n] for n in WEIGHTS])
```

```python
import functools

import jax
import jax.numpy as jnp
import numpy as np
from jax import lax
from jax.experimental import pallas as pl
from jax.experimental.pallas import tpu as pltpu

F32 = jnp.float32
BF16 = jnp.bfloat16
MESH = pl.DeviceIdType.MESH

D = 2048
HD = 64
NQH = 16
NKV = 4
QD = NQH * HD
KVD = NKV * HD
DI = 2048
NH = 32
NG = 4
NS = 128
CW = 4
L = 128
CONV = DI + 2 * NG * NS
FFN = 5632
PLE = 256
IN_DIM = QD + 2 * KVD + DI + CONV + NH + 2 * D
EPS = 1e-6
SSM_EPS = 1e-5
ROPE_THETA = 10000.0
LR, B1, B2, AEPS, WD, STEP = 0.001, 0.9, 0.999, 1e-08, 0.01, 10

O_GA, O_GS, O_Z, O_XBC, O_Q, O_K, O_V, O_DT = 0, 2048, 4096, 6144, 9216, 10240, 10496, 10752
DT_PAD = 512
NP = O_DT + DT_PAD
R_Q, R_K, R_V, R_Z, R_XBC, R_DT, R_GA, R_GS = 0, 1024, 1280, 1536, 3584, 6656, 6688, 8736

NCHIP = 4
VMEM_LIMIT = 52 * 1024 * 1024
NEG = -1e30


def _cp(sem=None):
    return pltpu.CompilerParams(dimension_semantics=sem, vmem_limit_bytes=VMEM_LIMIT)


def _dot(a, b):
    return lax.dot_general(a, b, (((1,), (0,)), ((), ())), preferred_element_type=F32)


def _dot_nt(a, b):
    return lax.dot_general(a, b, (((1,), (1,)), ((), ())), preferred_element_type=F32)


def _dot_tn(a, b):
    return lax.dot_general(a, b, (((0,), (0,)), ((), ())), preferred_element_type=F32)


def _sigmoid(x):
    return 1.0 / (1.0 + jnp.exp(-x))


def _bf16_dot(dot, da, db):
    @jax.custom_vjp
    def f(a, b):
        return dot(a.astype(BF16), b.astype(BF16))

    def fwd(a, b):
        return f(a, b), (a.astype(BF16), b.astype(BF16))

    def bwd(res, g):
        a, b = res
        g = g.astype(BF16)
        return da(g, a, b), db(g, a, b)

    f.defvjp(fwd, bwd)
    return f


_bdot = _bf16_dot(_dot, lambda g, a, b: _dot_nt(g, b), lambda g, a, b: _dot_tn(a, g))
_bdot_nt = _bf16_dot(_dot_nt, lambda g, a, b: _dot(g, b), lambda g, a, b: _dot_tn(g, a))
_bdot_tn = _bf16_dot(_dot_tn, lambda g, a, b: _dot_nt(b, g), lambda g, a, b: _dot(a, g))


ANY = pl.BlockSpec(memory_space=pl.ANY)


class _Job:
    srcs, dsts, news, scratch = (), (), (), ()
    has_mid = False

    def start(self, srcs, dsts, news, sems):
        raise NotImplementedError

    def mid(self, srcs, dsts, news, sems):
        pass

    def finish(self, srcs, dsts, news, sems):
        raise NotImplementedError

    def done(self, dsts, news):
        pass


def _call(body, *, jobs=(), name, out_shape, in_specs, out_specs, grid=(), scratch_shapes=(), compiler_params=None,
          aliases=None):
    jobs = [j for j in jobs if j is not None]
    aliases = dict(aliases or {})
    if not jobs:
        return pl.pallas_call(body, name=name, out_shape=out_shape, in_specs=in_specs, out_specs=out_specs, grid=grid,
                              scratch_shapes=scratch_shapes, compiler_params=compiler_params,
                              input_output_aliases=aliases)
    single = not isinstance(out_shape, (tuple, list))
    outs = [out_shape] if single else list(out_shape)
    ospecs = [out_specs] if single else list(out_specs)
    n_in, n_out, n_scr = len(in_specs), len(outs), len(scratch_shapes)
    srcs = [a for j in jobs for a in j.srcs]
    dsts = [a for j in jobs for a in j.dsts]
    news = [a for j in jobs for a in j.news]
    sems = [a for j in jobs for a in j.scratch]

    def wrapped(*refs):
        pos = n_in + len(srcs) + len(dsts)
        ins, jsrc = refs[:n_in], refs[n_in:n_in + len(srcs)]
        o_refs = refs[pos:pos + n_out]
        pos += n_out
        jdst, jnew = refs[pos:pos + len(dsts)], refs[pos + len(dsts):pos + len(dsts) + len(news)]
        pos += len(dsts) + len(news)
        scr, jsem = refs[pos:pos + n_scr], refs[pos + n_scr:]

        def run(which):
            a = b = c = d = 0
            for j in jobs:
                getattr(j, which)(jsrc[a:a + len(j.srcs)], jdst[b:b + len(j.dsts)], jnew[c:c + len(j.news)],
                                  jsem[d:d + len(j.scratch)])
                a, b, c, d = a + len(j.srcs), b + len(j.dsts), c + len(j.news), d + len(j.scratch)

        if not grid:
            run("start")
            run("mid")
            body(*ins, *o_refs, *scr)
            run("finish")
            return
        step = functools.reduce(lambda acc, a: acc * grid[a] + pl.program_id(a), range(len(grid)), 0)
        steps = int(np.prod(grid))
        pl.when(step == 0)(lambda: run("start"))
        if any(j.has_mid for j in jobs):
            pl.when(step == steps // 3)(lambda: run("mid"))
        body(*ins, *o_refs, *scr)
        pl.when(step == steps - 1)(lambda: run("finish"))

    call = pl.pallas_call(
        wrapped, name=name,
        out_shape=outs + [jax.ShapeDtypeStruct(a.shape, a.dtype) for a in dsts] + news,
        in_specs=list(in_specs) + [ANY] * (len(srcs) + len(dsts)),
        out_specs=ospecs + [ANY] * (len(dsts) + len(news)),
        grid=grid, scratch_shapes=list(scratch_shapes) + sems,
        input_output_aliases={**aliases, **{n_in + len(srcs) + i: n_out + i for i in range(len(dsts))}},
        compiler_params=_cp(("arbitrary",) * len(grid) if grid else None))

    def run_call(*args):
        res = call(*args, *srcs, *dsts)
        b, c = n_out, n_out + len(dsts)
        for j in jobs:
            j.done(res[b:b + len(j.dsts)], res[c:c + len(j.news)])
            b, c = b + len(j.dsts), c + len(j.news)
        return res[0] if single else tuple(res[:n_out])

    return run_call


def _matmul(a, b, *, ta=False, tb=False, out_dtype=F32, add=None, tm, tn, tk, name, jobs=()):
    k, m = a.shape if ta else a.shape[::-1]
    n = b.shape[0] if tb else b.shape[1]
    assert (b.shape[1] if tb else b.shape[0]) == k and not (ta and tb)
    assert m % tm == 0 and n % tn == 0 and k % tk == 0, (name, a.shape, b.shape)
    nk = k // tk
    has_add = add is not None

    def body(*refs):
        a_ref, b_ref = refs[0], refs[1]
        add_ref = refs[2] if has_add else None
        o_ref = refs[3] if has_add else refs[2]
        av = a_ref[...].astype(BF16)
        bv = b_ref[...].astype(BF16)
        part = _dot_tn(av, bv) if ta else _dot_nt(av, bv) if tb else _dot(av, bv)

        def finish(r):
            if has_add:
                r = r + add_ref[...]
            o_ref[...] = r.astype(out_dtype)

        if nk == 1:
            finish(part)
        else:
            acc_ref = refs[-1]
            kk = pl.program_id(2)

            @pl.when(kk == 0)
            def _():
                acc_ref[...] = part

            @pl.when(kk > 0)
            def _():
                acc_ref[...] += part

            @pl.when(kk == nk - 1)
            def _():
                finish(acc_ref[...])

    in_specs = [pl.BlockSpec((tk, tm), lambda i, j, kk: (kk, i)) if ta else pl.BlockSpec((tm, tk), lambda i, j, kk: (i, kk)),
                pl.BlockSpec((tn, tk), lambda i, j, kk: (j, kk)) if tb
                else pl.BlockSpec((tk, tn), lambda i, j, kk: (kk, j))]
    args = [a, b]
    if has_add:
        in_specs.append(pl.BlockSpec((tm, tn), lambda i, j, kk: (i, j)))
        args.append(add)
    return _call(
        body, jobs=jobs, name=name,
        out_shape=jax.ShapeDtypeStruct((m, n), out_dtype),
        grid=(m // tm, n // tn, nk),
        in_specs=in_specs,
        out_specs=pl.BlockSpec((tm, tn), lambda i, j, kk: (i, j)),
        scratch_shapes=[pltpu.VMEM((tm, tn), F32)] if nk > 1 else [],
        compiler_params=_cp(("parallel", "parallel", "arbitrary")),
    )(*args)


ROWS = 256


def _rmsnorm_fwd(x, g, *, name):
    t, d = x.shape

    def body(x_ref, g_ref, o_ref):
        xv = x_ref[...]
        r = lax.rsqrt(jnp.mean(xv * xv, axis=-1, keepdims=True) + EPS)
        o_ref[...] = (xv * r * g_ref[...]).astype(BF16)

    return pl.pallas_call(
        body, name=name, out_shape=jax.ShapeDtypeStruct((t, d), BF16), grid=(t // ROWS,),
        in_specs=[pl.BlockSpec((ROWS, d), lambda i: (i, 0)), pl.BlockSpec((1, d), lambda i: (0, 0))],
        out_specs=pl.BlockSpec((ROWS, d), lambda i: (i, 0)), compiler_params=_cp(("parallel",)),
    )(x, g)


def _rmsnorm_bwd(x, g, dy, dres, *, name, jobs=()):
    t, d = x.shape

    def body(x_ref, g_ref, dy_ref, dres_ref, dx_ref, dxb_ref, dg_ref):
        xv = x_ref[...]
        r = lax.rsqrt(jnp.mean(xv * xv, axis=-1, keepdims=True) + EPS)
        xh = xv * r
        dyv = dy_ref[...]
        dxh = dyv * g_ref[...]
        dx = r * (dxh - xh * jnp.mean(dxh * xh, axis=-1, keepdims=True))
        tot = dres_ref[...] + dx
        dx_ref[...] = tot
        dxb_ref[...] = tot.astype(BF16)

        @pl.when(pl.program_id(0) == 0)
        def _():
            dg_ref[...] = jnp.zeros_like(dg_ref)

        dg_ref[...] += jnp.broadcast_to(jnp.sum(dyv * xh, axis=0, keepdims=True), dg_ref.shape)

    row = pl.BlockSpec((ROWS, d), lambda i: (i, 0))
    return _call(
        body, jobs=jobs, name=name,
        out_shape=(jax.ShapeDtypeStruct((t, d), F32), jax.ShapeDtypeStruct((t, d), BF16),
                   jax.ShapeDtypeStruct((8, d), F32)),
        grid=(t // ROWS,),
        in_specs=[row, pl.BlockSpec((1, d), lambda i: (0, 0)), row, row],
        out_specs=(row, row, pl.BlockSpec((8, d), lambda i: (0, 0))),
        compiler_params=_cp(("arbitrary",)),
    )(x, g, dy, dres)


def _final(h2, pgl, pp, target, g_final, *, name):
    t, d = h2.shape

    def body(h2_ref, pgl_ref, pp_ref, tg_ref, g_ref, dh3_ref, dpgl_ref, dpp_ref, loss_ref, dg_ref):
        s = _sigmoid(pgl_ref[...])
        ppv = pp_ref[...]
        h3 = h2_ref[...] + s * ppv
        r = lax.rsqrt(jnp.mean(h3 * h3, axis=-1, keepdims=True) + EPS)
        xh = h3 * r
        gv = g_ref[...]
        err = xh * gv - tg_ref[...]
        dyv = err * (1.0 / d)
        dxh = dyv * gv
        dh3 = r * (dxh - xh * jnp.mean(dxh * xh, axis=-1, keepdims=True))
        dh3_ref[...] = dh3
        dpp_ref[...] = (dh3 * s).astype(BF16)
        dpgl_ref[...] = (dh3 * ppv * s * (1.0 - s)).astype(BF16)

        @pl.when(pl.program_id(0) == 0)
        def _():
            loss_ref[...] = jnp.zeros_like(loss_ref)
            dg_ref[...] = jnp.zeros_like(dg_ref)

        part = 0.5 * jnp.sum(jnp.mean(err * err, axis=-1, keepdims=True), axis=0, keepdims=True)
        loss_ref[...] += jnp.broadcast_to(part, loss_ref.shape)
        dg_ref[...] += jnp.broadcast_to(jnp.sum(dyv * xh, axis=0, keepdims=True), dg_ref.shape)

    row = pl.BlockSpec((ROWS, d), lambda i: (i, 0))
    return pl.pallas_call(
        body, name=name,
        out_shape=(jax.ShapeDtypeStruct((t, d), F32), jax.ShapeDtypeStruct((t, d), BF16),
                   jax.ShapeDtypeStruct((t, d), BF16), jax.ShapeDtypeStruct((8, 128), F32),
                   jax.ShapeDtypeStruct((8, d), F32)),
        grid=(t // ROWS,),
        in_specs=[row, row, row, row, pl.BlockSpec((1, d), lambda i: (0, 0))],
        out_specs=(row, row, row, pl.BlockSpec((8, 128), lambda i: (0, 0)), pl.BlockSpec((8, d), lambda i: (0, 0))),
        compiler_params=_cp(("arbitrary",)),
    )(h2, pgl, pp, target, g_final)


def _merge_fwd(proj, out_a, out_s, *, name):
    t = proj.shape[0]

    def body(ga_ref, gs_ref, a_ref, s_ref, o_ref):
        o_ref[...] = (_sigmoid(ga_ref[...]) * a_ref[...] + _sigmoid(gs_ref[...]) * s_ref[...]).astype(BF16)

    row = pl.BlockSpec((ROWS, D), lambda i: (i, 0))
    return pl.pallas_call(
        body, name=name, out_shape=jax.ShapeDtypeStruct((t, D), BF16), grid=(t // ROWS,),
        in_specs=[pl.BlockSpec((ROWS, D), lambda i: (i, O_GA // D)), pl.BlockSpec((ROWS, D), lambda i: (i, O_GS // D)),
                  row, row],
        out_specs=row, compiler_params=_cp(("parallel",)),
    )(proj, proj, out_a, out_s)


def _merge_bwd(proj, out_a, out_s, dmerged, *, name):
    t = proj.shape[0]
    assert O_GA == 0 and O_GS == D

    def body(ga_ref, gs_ref, a_ref, s_ref, dm_ref, da_ref, ds_ref, dp_ref):
        sa = _sigmoid(ga_ref[...])
        ss = _sigmoid(gs_ref[...])
        dm = dm_ref[...]
        da_ref[...] = (dm * sa).astype(BF16)
        ds_ref[...] = (dm * ss).astype(BF16)
        dp_ref[:, :D] = (dm * a_ref[...] * sa * (1.0 - sa)).astype(BF16)
        dp_ref[:, D:] = (dm * s_ref[...] * ss * (1.0 - ss)).astype(BF16)

    row = pl.BlockSpec((ROWS, D), lambda i: (i, 0))
    o = jax.ShapeDtypeStruct((t, D), BF16)
    return pl.pallas_call(
        body, name=name, out_shape=(o, o, jax.ShapeDtypeStruct((t, NP), BF16)), grid=(t // ROWS,),
        in_specs=[pl.BlockSpec((ROWS, D), lambda i: (i, O_GA // D)), pl.BlockSpec((ROWS, D), lambda i: (i, O_GS // D)),
                  row, row, row],
        out_specs=(row, row, pl.BlockSpec((ROWS, 2 * D), lambda i: (i, 0))), compiler_params=_cp(("parallel",)),
    )(proj, proj, out_a, out_s, dmerged)


def _swiglu_fwd(f, w_gate, w_up, *, name, tn=256, jobs=()):
    t, d = f.shape
    n = w_gate.shape[1]

    def body(f_ref, wg_ref, wu_ref, g_ref, u_ref, a_ref):
        fv = f_ref[...]
        g = _dot(fv, wg_ref[...])
        u = _dot(fv, wu_ref[...])
        g_ref[...] = g
        u_ref[...] = u
        a_ref[...] = (g * _sigmoid(g) * u).astype(BF16)

    col = pl.BlockSpec((t, tn), lambda j: (0, j))
    wcol = pl.BlockSpec((d, tn), lambda j: (0, j))
    return _call(
        body, jobs=jobs, name=name,
        out_shape=(jax.ShapeDtypeStruct((t, n), F32), jax.ShapeDtypeStruct((t, n), F32),
                   jax.ShapeDtypeStruct((t, n), BF16)),
        grid=(n // tn,),
        in_specs=[pl.BlockSpec((t, d), lambda j: (0, 0)), wcol, wcol],
        out_specs=(col, col, col), compiler_params=_cp(("parallel",)),
    )(f, w_gate, w_up)


def _swiglu_bwd(gate, up, dact, *, name, tc=1408, jobs=()):
    t, n = gate.shape

    def body(g_ref, u_ref, da_ref, dg_ref, du_ref):
        g = g_ref[...]
        s = _sigmoid(g)
        da = da_ref[...]
        du_ref[...] = (da * g * s).astype(BF16)
        dg_ref[...] = (da * u_ref[...] * s * (1.0 + g * (1.0 - s))).astype(BF16)

    blk = pl.BlockSpec((ROWS, tc), lambda i, j: (i, j))
    o = jax.ShapeDtypeStruct((t, n), BF16)
    return _call(
        body, jobs=jobs, name=name, out_shape=(o, o), grid=(t // ROWS, n // tc),
        in_specs=[blk, blk, blk], out_specs=(blk, blk), compiler_params=_cp(("parallel", "parallel")),
    )(gate, up, dact)


def _gated_norm_fwd(y_pre, proj, g_ssd, *, name):
    t = y_pre.shape[0]

    def body(y_ref, z_ref, g_ref, o_ref):
        z = z_ref[...]
        v = y_ref[...] * z * _sigmoid(z)
        r = lax.rsqrt(jnp.mean(v * v, axis=-1, keepdims=True) + SSM_EPS)
        o_ref[...] = (v * r * g_ref[...]).astype(BF16)

    row = pl.BlockSpec((ROWS, DI), lambda i: (i, 0))
    return pl.pallas_call(
        body, name=name, out_shape=jax.ShapeDtypeStruct((t, DI), BF16), grid=(t // ROWS,),
        in_specs=[row, pl.BlockSpec((ROWS, DI), lambda i: (i, O_Z // DI)), pl.BlockSpec((1, DI), lambda i: (0, 0))],
        out_specs=row, compiler_params=_cp(("parallel",)),
    )(y_pre, proj, g_ssd)


def _gated_norm_bwd(y_pre, proj, g_ssd, dyn, dproj, *, name, jobs=()):
    t = y_pre.shape[0]

    def body(y_ref, z_ref, g_ref, dyn_ref, _, dy_ref, dz_ref, dg_ref):
        z = z_ref[...]
        s = _sigmoid(z)
        sz = z * s
        yv = y_ref[...]
        v = yv * sz
        r = lax.rsqrt(jnp.mean(v * v, axis=-1, keepdims=True) + SSM_EPS)
        vh = v * r
        dn = dyn_ref[...]
        dvh = dn * g_ref[...]
        dv = r * (dvh - vh * jnp.mean(dvh * vh, axis=-1, keepdims=True))
        dy_ref[...] = dv * sz
        dz_ref[...] = (dv * yv * s * (1.0 + z * (1.0 - s))).astype(BF16)

        @pl.when(pl.program_id(0) == 0)
        def _():
            dg_ref[...] = jnp.zeros_like(dg_ref)

        dg_ref[...] += jnp.broadcast_to(jnp.sum(dn * vh, axis=0, keepdims=True), dg_ref.shape)

    row = pl.BlockSpec((ROWS, DI), lambda i: (i, 0))
    return _call(
        body, jobs=jobs, name=name,
        out_shape=(jax.ShapeDtypeStruct((t, DI), F32), jax.ShapeDtypeStruct(dproj.shape, BF16),
                   jax.ShapeDtypeStruct((8, DI), F32)),
        grid=(t // ROWS,),
        in_specs=[row, pl.BlockSpec((ROWS, DI), lambda i: (i, O_Z // DI)), pl.BlockSpec((1, DI), lambda i: (0, 0)), row, ANY],
        out_specs=(row, pl.BlockSpec((ROWS, DI), lambda i: (i, O_Z // DI)), pl.BlockSpec((8, DI), lambda i: (0, 0))),
        compiler_params=_cp(("arbitrary",)), aliases={4: 1},
    )(y_pre, proj, g_ssd, dyn, dproj)


CONV_TC = 512


def _shift_down(x, s, row):
    if s == 0:
        return x
    return jnp.where(row >= s, pltpu.roll(x, s, 0), 0.0)


def _shift_up(x, s, row, t):
    if s == 0:
        return x
    return jnp.where(row < t - s, pltpu.roll(x, t - s, 0), 0.0)


def _conv_fwd(proj, conv_w, conv_b, *, name):
    t = proj.shape[0]

    def body(x_ref, w_ref, b_ref, o_ref):
        x = x_ref[...]
        row = lax.broadcasted_iota(jnp.int32, x.shape, 0)
        pre = jnp.broadcast_to(b_ref[...], x.shape)
        for k in range(CW):
            pre = pre + w_ref[k:k + 1, :] * _shift_down(x, CW - 1 - k, row)
        o_ref[...] = pre * _sigmoid(pre)

    return pl.pallas_call(
        body, name=name, out_shape=jax.ShapeDtypeStruct((t, CONV), F32), grid=(CONV // CONV_TC,),
        in_specs=[pl.BlockSpec((t, CONV_TC), lambda j: (0, O_XBC // CONV_TC + j)),
                  pl.BlockSpec((CW, CONV_TC), lambda j: (0, j)), pl.BlockSpec((1, CONV_TC), lambda j: (0, j))],
        out_specs=pl.BlockSpec((t, CONV_TC), lambda j: (0, j)), compiler_params=_cp(("parallel",)),
    )(proj, conv_w, conv_b)


def _conv_bwd(proj, conv_w, conv_b, dxs, db, dc, dproj, *, name, jobs=()):
    t = proj.shape[0]
    nx = DI // CONV_TC
    assert NG * NS == CONV_TC

    def body(x_ref, w_ref, b_ref, dxs_ref, db_ref, dc_ref, _, dx_ref, dw_ref, dbias_ref):
        j = pl.program_id(0)
        x = x_ref[...]
        row = lax.broadcasted_iota(jnp.int32, x.shape, 0)
        xs = [_shift_down(x, CW - 1 - k, row) for k in range(CW)]
        pre = jnp.broadcast_to(b_ref[...], x.shape)
        for k in range(CW):
            pre = pre + w_ref[k:k + 1, :] * xs[k]
        s = _sigmoid(pre)
        da = jnp.where(j < nx, dxs_ref[...], jnp.where(j == nx, db_ref[...], dc_ref[...]))
        dpre = da * s * (1.0 + pre * (1.0 - s))
        dx = jnp.zeros_like(x)
        row8 = lax.broadcasted_iota(jnp.int32, dw_ref.shape, 0)
        dw = jnp.zeros(dw_ref.shape, F32)
        for k in range(CW):
            dx = dx + w_ref[k:k + 1, :] * _shift_up(dpre, CW - 1 - k, row, t)
            dw = dw + jnp.where(row8 == k, jnp.sum(dpre * xs[k], axis=0, keepdims=True), 0.0)
        dx_ref[...] = dx.astype(BF16)
        dw_ref[...] = dw
        dbias_ref[...] = jnp.broadcast_to(jnp.sum(dpre, axis=0, keepdims=True), dbias_ref.shape)

    col8 = pl.BlockSpec((8, CONV_TC), lambda j: (0, j))
    xbc = pl.BlockSpec((t, CONV_TC), lambda j: (0, O_XBC // CONV_TC + j))
    whole = pl.BlockSpec((t, CONV_TC), lambda j: (0, 0))
    return _call(
        body, jobs=jobs, name=name,
        out_shape=(jax.ShapeDtypeStruct(dproj.shape, BF16), jax.ShapeDtypeStruct((8, CONV), F32),
                   jax.ShapeDtypeStruct((8, CONV), F32)),
        grid=(CONV // CONV_TC,),
        in_specs=[xbc, pl.BlockSpec((CW, CONV_TC), lambda j: (0, j)), pl.BlockSpec((1, CONV_TC), lambda j: (0, j)),
                  pl.BlockSpec((t, CONV_TC), lambda j: (0, jnp.minimum(j, nx - 1))), whole, whole, ANY],
        out_specs=(xbc, col8, col8),
        compiler_params=_cp(("arbitrary",)), aliases={6: 0},
    )(proj, conv_w, conv_b, dxs, db, dc, dproj)


def _rope_tables(positions, t):
    half = HD // 2
    inv_freq = ROPE_THETA ** (-jnp.arange(half, dtype=F32) * 2.0 / HD)
    ang = positions.reshape(t).astype(F32)[:, None] * inv_freq
    cos, sin = jnp.cos(ang), jnp.sin(ang)
    return jnp.concatenate([cos] * 4, axis=1), jnp.concatenate([-sin, sin] * 2, axis=1)


def _lane_consts():
    lane = lax.broadcasted_iota(jnp.int32, (L, 128), 1)
    return lane, (lane % HD) < (HD // 2), lane < HD


def _rope(tv, cos, sin, lo):
    return tv * cos + jnp.where(lo, pltpu.roll(tv, 128 - HD // 2, 1), pltpu.roll(tv, HD // 2, 1)) * sin


def _rope_t(dv, cos, sin, lo):
    ds = dv * sin
    return dv * cos + jnp.where(lo, pltpu.roll(ds, 128 - HD // 2, 1), pltpu.roll(ds, HD // 2, 1))


def _placed(chunk, g, half0):
    own = jnp.where(half0 if g % 2 == 0 else jnp.logical_not(half0), chunk, 0.0)
    other = pltpu.roll(own, HD, 1)
    return (own, other) if g % 2 == 0 else (other, own)


def _unplace(acc, hf, g, half0):
    v = jnp.where(half0 if hf == 0 else jnp.logical_not(half0), acc, 0.0)
    return v if hf == g % 2 else pltpu.roll(v, HD, 1)


def _attn_fwd(proj, cos, sin, sinks, *, name, jobs=()):
    t = proj.shape[0]
    nb = t // L
    scale = HD ** -0.5

    def body(sink_ref, q_ref, kc_ref, kp_ref, vc_ref, vp_ref, cc_ref, sc_ref, cp_ref, sp_ref, o_ref, lse_ref):
        i = pl.program_id(0)
        lane, lo, half0 = _lane_consts()
        cos_c, sin_c, cos_p, sin_p = cc_ref[...], sc_ref[...], cp_ref[...], sp_ref[...]
        row = lax.broadcasted_iota(jnp.int32, (L, L), 0)
        col = lax.broadcasted_iota(jnp.int32, (L, L), 1)
        m_cur = col <= row
        m_prev = jnp.logical_and(col > row, i > 0)
        kc = [_rope(kc_ref[:, 128 * m:128 * (m + 1)], cos_c, sin_c, lo) for m in range(2)]
        kp = [_rope(kp_ref[:, 128 * m:128 * (m + 1)], cos_p, sin_p, lo) for m in range(2)]
        lse_acc = jnp.zeros((L, 128), F32)
        outs = [jnp.zeros((L, 128), F32) for _ in range(QD // 128)]
        qs = [(_rope(q_ref[:, 128 * ch:128 * (ch + 1)], cos_c, sin_c, lo) * scale).astype(BF16) for ch in range(QD // 128)]
        for g in range(NKV):
            kcv = [v.astype(BF16) for v in _placed(kc[g // 2], g, half0)]
            kpv = [v.astype(BF16) for v in _placed(kp[g // 2], g, half0)]
            vcv = [v.astype(BF16) for v in _placed(vc_ref[:, 128 * (g // 2):128 * (g // 2 + 1)], g, half0)]
            vpv = [v.astype(BF16) for v in _placed(vp_ref[:, 128 * (g // 2):128 * (g // 2 + 1)], g, half0)]
            for r in range(NQH // NKV):
                h = g * (NQH // NKV) + r
                ch, hf = h // 2, h % 2
                s_c = jnp.where(m_cur, _dot_nt(qs[ch], kcv[hf]), NEG)
                s_p = jnp.where(m_prev, _dot_nt(qs[ch], kpv[hf]), NEG)
                sink = sink_ref[0, h]
                mx = jnp.maximum(jnp.maximum(jnp.max(s_c, axis=-1, keepdims=True), jnp.max(s_p, axis=-1, keepdims=True)), sink)
                e_c = jnp.exp(s_c - mx)
                e_p = jnp.exp(s_p - mx)
                den = jnp.sum(e_c, axis=-1, keepdims=True) + jnp.sum(e_p, axis=-1, keepdims=True) + jnp.exp(sink - mx)
                inv = 1.0 / den
                outs[ch] = outs[ch] + _dot((e_c * inv).astype(BF16), vcv[hf]) + _dot((e_p * inv).astype(BF16), vpv[hf])
                lse_acc = jnp.where(lane == h, mx + jnp.log(den), lse_acc)
        for ch in range(QD // 128):
            o_ref[:, 128 * ch:128 * (ch + 1)] = outs[ch].astype(BF16)
        lse_ref[...] = lse_acc

    prev = lambda i: jnp.maximum(i - 1, 0)
    tab_c = pl.BlockSpec((L, 128), lambda i: (i, 0))
    tab_p = pl.BlockSpec((L, 128), lambda i: (prev(i), 0))
    return _call(
        body, jobs=jobs, name=name,
        out_shape=(jax.ShapeDtypeStruct((t, QD), BF16), jax.ShapeDtypeStruct((t, 128), F32)),
        grid=(nb,),
        in_specs=[pl.BlockSpec(memory_space=pltpu.SMEM),
                  pl.BlockSpec((L, QD), lambda i: (i, O_Q // QD)),
                  pl.BlockSpec((L, KVD), lambda i: (i, O_K // KVD)), pl.BlockSpec((L, KVD), lambda i: (prev(i), O_K // KVD)),
                  pl.BlockSpec((L, KVD), lambda i: (i, O_V // KVD)), pl.BlockSpec((L, KVD), lambda i: (prev(i), O_V // KVD)),
                  tab_c, tab_c, tab_p, tab_p],
        out_specs=(pl.BlockSpec((L, QD), lambda i: (i, 0)), pl.BlockSpec((L, 128), lambda i: (i, 0))),
        compiler_params=_cp(("parallel",)),
    )(sinks, proj, proj, proj, proj, proj, cos, sin, cos, sin)


def _attn_bwd(proj, cos, sin, sinks, attn, lse, dattn, dproj, *, name, jobs=()):
    t = proj.shape[0]
    nb = t // L
    scale = HD ** -0.5

    def body(sink_ref, qi_ref, qn_ref, kc_ref, kp_ref, vc_ref, vp_ref, doi_ref, don_ref, oi_ref, on_ref,
             lsei_ref, lsen_ref, cc_ref, sc_ref, cp_ref, sp_ref, cn_ref, sn_ref, _, dqkv_ref, dsk_ref):
        i = pl.program_id(0)
        lane, lo, half0 = _lane_consts()
        half1 = jnp.logical_not(half0)
        cos_c, sin_c = cc_ref[...], sc_ref[...]
        row = lax.broadcasted_iota(jnp.int32, (L, L), 0)
        col = lax.broadcasted_iota(jnp.int32, (L, L), 1)
        m_cur = col <= row
        m_prev = jnp.logical_and(col > row, i > 0)
        m_next = jnp.logical_and(col > row, i < nb - 1)
        kc = [_rope(kc_ref[:, 128 * m:128 * (m + 1)], cos_c, sin_c, lo) for m in range(2)]
        kp = [_rope(kp_ref[:, 128 * m:128 * (m + 1)], cp_ref[...], sp_ref[...], lo) for m in range(2)]
        lse_i, lse_n = lsei_ref[...], lsen_ref[...]
        dk_acc = [jnp.zeros((L, 128), F32) for _ in range(2)]
        dv_acc = [jnp.zeros((L, 128), F32) for _ in range(2)]
        dsk_acc = jnp.zeros((1, 128), F32)
        lane1 = lax.broadcasted_iota(jnp.int32, (1, 128), 1)
        place = lambda chunk, g: [v.astype(BF16) for v in _placed(chunk, g, half0)]
        kcs = [place(kc[g // 2], g) for g in range(NKV)]
        kps = [place(kp[g // 2], g) for g in range(NKV)]
        vcs = [place(vc_ref[:, 128 * (g // 2):128 * (g // 2 + 1)], g) for g in range(NKV)]
        vps = [place(vp_ref[:, 128 * (g // 2):128 * (g // 2 + 1)], g) for g in range(NKV)]
        for ch in range(QD // 128):
            sl = slice(128 * ch, 128 * (ch + 1))
            q_i = (_rope(qi_ref[:, sl], cos_c, sin_c, lo) * scale).astype(BF16)
            q_n = (_rope(qn_ref[:, sl], cn_ref[...], sn_ref[...], lo) * scale).astype(BF16)
            do_i, do_n = doi_ref[:, sl], don_ref[:, sl]
            do_ib, do_nb = do_i.astype(BF16), do_n.astype(BF16)
            od_i = do_i * oi_ref[:, sl].astype(F32)
            od_n = do_n * on_ref[:, sl].astype(F32)
            dq_ch = jnp.zeros((L, 128), F32)
            for hf in range(2):
                h = 2 * ch + hf
                g = h // (NQH // NKV)
                hm = half0 if hf == 0 else half1
                kcv, kpv, vcv, vpv = kcs[g][hf], kps[g][hf], vcs[g][hf], vps[g][hf]
                dl_i = jnp.sum(jnp.where(hm, od_i, 0.0), axis=-1, keepdims=True)
                dl_n = jnp.sum(jnp.where(hm, od_n, 0.0), axis=-1, keepdims=True)
                ls_i = jnp.sum(jnp.where(lane == h, lse_i, 0.0), axis=-1, keepdims=True)
                ls_n = jnp.sum(jnp.where(lane == h, lse_n, 0.0), axis=-1, keepdims=True)
                p_c = jnp.where(m_cur, jnp.exp(_dot_nt(q_i, kcv) - ls_i), 0.0)
                p_p = jnp.where(m_prev, jnp.exp(_dot_nt(q_i, kpv) - ls_i), 0.0)
                ds_c = (p_c * (_dot_nt(do_ib, vcv) - dl_i)).astype(BF16)
                ds_p = (p_p * (_dot_nt(do_ib, vpv) - dl_i)).astype(BF16)
                dq_ch = dq_ch + jnp.where(hm, (_dot(ds_c, kcv) + _dot(ds_p, kpv)) * scale, 0.0)
                sink = sink_ref[0, h]
                dsk = -jnp.sum(jnp.exp(sink - ls_i) * dl_i, axis=0, keepdims=True)
                dsk_acc = dsk_acc + jnp.where(lane1 == h, dsk, 0.0)
                p_n = jnp.where(m_next, jnp.exp(_dot_nt(q_n, kcv) - ls_n), 0.0)
                ds_n = (p_n * (_dot_nt(do_nb, vcv) - dl_n)).astype(BF16)
                dv_h = _dot_tn(p_c.astype(BF16), do_ib) + _dot_tn(p_n.astype(BF16), do_nb)
                dk_h = _dot_tn(ds_c, q_i) + _dot_tn(ds_n, q_n)
                dv_acc[g // 2] = dv_acc[g // 2] + _unplace(dv_h, hf, g, half0)
                dk_acc[g // 2] = dk_acc[g // 2] + _unplace(dk_h, hf, g, half0)
            dqkv_ref[:, sl] = _rope_t(dq_ch, cos_c, sin_c, lo).astype(BF16)
        for m in range(2):
            dqkv_ref[:, QD + 128 * m:QD + 128 * (m + 1)] = _rope_t(dk_acc[m], cos_c, sin_c, lo).astype(BF16)
            dqkv_ref[:, QD + KVD + 128 * m:QD + KVD + 128 * (m + 1)] = dv_acc[m].astype(BF16)

        @pl.when(i == 0)
        def _():
            dsk_ref[...] = jnp.zeros_like(dsk_ref)

        dsk_ref[...] += jnp.broadcast_to(dsk_acc, dsk_ref.shape)

    prev = lambda i: jnp.maximum(i - 1, 0)
    nxt = lambda i: jnp.minimum(i + 1, nb - 1)
    cur_q = pl.BlockSpec((L, QD), lambda i: (i, 0))
    nxt_q = pl.BlockSpec((L, QD), lambda i: (nxt(i), 0))
    tab = lambda f: pl.BlockSpec((L, 128), lambda i: (f(i), 0))
    ident = lambda i: i
    qkv = QD + 2 * KVD
    assert O_K == O_Q + QD and O_V == O_K + KVD and O_Q % qkv == 0
    return _call(
        body, jobs=jobs, name=name,
        out_shape=(jax.ShapeDtypeStruct(dproj.shape, BF16), jax.ShapeDtypeStruct((8, 128), F32)),
        grid=(nb,),
        in_specs=[pl.BlockSpec(memory_space=pltpu.SMEM),
                  pl.BlockSpec((L, QD), lambda i: (i, O_Q // QD)), pl.BlockSpec((L, QD), lambda i: (nxt(i), O_Q // QD)),
                  pl.BlockSpec((L, KVD), lambda i: (i, O_K // KVD)), pl.BlockSpec((L, KVD), lambda i: (prev(i), O_K // KVD)),
                  pl.BlockSpec((L, KVD), lambda i: (i, O_V // KVD)), pl.BlockSpec((L, KVD), lambda i: (prev(i), O_V // KVD)),
                  cur_q, nxt_q, cur_q, nxt_q, tab(ident), tab(nxt),
                  tab(ident), tab(ident), tab(prev), tab(prev), tab(nxt), tab(nxt), ANY],
        out_specs=(pl.BlockSpec((L, qkv), lambda i: (i, O_Q // qkv)), pl.BlockSpec((8, 128), lambda i: (0, 0))),
        compiler_params=_cp(("arbitrary",)), aliases={19: 0},
    )(sinks, proj, proj, proj, proj, proj, proj, dattn, dattn, attn, attn, lse, lse, cos, sin, cos, sin, cos, sin, dproj)


PAIRS = NH // NG // 2


def _softplus(x):
    return jnp.maximum(x, 0.0) + jnp.log(1.0 + jnp.exp(-jnp.abs(x)))


def _ssd_chunk(g, xps, dtr, bm, cm, sps, dtb, alog, dsk):
    lane = lax.broadcasted_iota(jnp.int32, (L, 128), 1)
    lane1 = lax.broadcasted_iota(jnp.int32, (1, 128), 1)
    row = lax.broadcasted_iota(jnp.int32, (L, L), 0)
    col = lax.broadcasted_iota(jnp.int32, (L, L), 1)
    rowc = lax.broadcasted_iota(jnp.int32, (128, 1), 0)
    tril = col <= row
    dt = _softplus(dtr + dtb)
    a = dt * (-jnp.exp(alog))
    a_cs = lax.dot_general(tril.astype(F32), a, (((1,), (0,)), ((), ())), precision=lax.Precision.HIGHEST,
                           preferred_element_type=F32)
    a_cst = a_cs.T
    a_last = jnp.sum(jnp.where(row == L - 1, a_cs, 0.0), axis=0, keepdims=True)
    cb = _bdot_nt(cm, bm)
    ys, snew = [], []
    for q in range(PAIRS):
        xp, sp = xps[q], sps[q]
        y_pair = jnp.zeros((L, 128), F32)
        st_pair = jnp.zeros((128, NS), F32)
        keep = jnp.zeros((128, 1), F32)
        for hh in range(2):
            h = g * 2 * PAIRS + 2 * q + hh
            hm = (lane < HD) if hh == 0 else (lane >= HD)
            rm = (rowc < HD) if hh == 0 else (rowc >= HD)
            dt_h = jnp.sum(jnp.where(lane == h, dt, 0.0), axis=1, keepdims=True)
            acs_h = jnp.sum(jnp.where(lane == h, a_cs, 0.0), axis=1, keepdims=True)
            acst_h = jnp.sum(jnp.where(row == h, a_cst, 0.0), axis=0, keepdims=True)
            al_h = jnp.sum(jnp.where(lane1 == h, a_last, 0.0), axis=1, keepdims=True)
            dsk_h = jnp.sum(jnp.where(lane1 == h, dsk, 0.0), axis=1, keepdims=True)
            decay = jnp.where(tril, jnp.exp(jnp.where(tril, acs_h - acst_h, 0.0)), 0.0)
            xh = jnp.where(hm, xp, 0.0)
            xd = xh * dt_h
            y = _bdot(cb * decay, xd)
            y = y + jnp.where(hm, _bdot_nt(cm * jnp.exp(acs_h), sp), 0.0)
            y_pair = y_pair + y + dsk_h * xh
            st_pair = st_pair + _bdot_tn(xd, bm * jnp.exp(al_h - acs_h))
            keep = keep + jnp.where(rm, jnp.exp(al_h), 0.0)
        ys.append(y_pair)
        snew.append(sp * keep + st_pair)
    return ys, snew


def _ssd_specs(t):
    nc = t // L
    xs = lambda f: pl.BlockSpec((L, 128 * PAIRS), lambda c, g: (f(c), g))
    bspec = lambda f: pl.BlockSpec((L, NS), lambda c, g: (f(c), DI // NS + g))
    cspec = lambda f: pl.BlockSpec((L, NS), lambda c, g: (f(c), DI // NS + NG + g))
    dts = lambda f: pl.BlockSpec((L, 128), lambda c, g: (f(c), O_DT // 128))
    par = pl.BlockSpec((1, 128), lambda c, g: (0, 0))
    st = lambda f: pl.BlockSpec((1, 1, PAIRS, 128, NS), lambda c, g: (f(c), g, 0, 0, 0))
    return nc, xs, bspec, cspec, dts, par, st


def _ssd_fwd(xbc_act, proj, dtb, alog, dsk, *, name, jobs=()):
    t = proj.shape[0]
    nc, xs, bspec, cspec, dts, par, st = _ssd_specs(t)
    ident = lambda c: c

    def body(x_ref, b_ref, c_ref, dt_ref, dtb_ref, al_ref, dsk_ref, y_ref, sin_ref, s_ref):
        c, g = pl.program_id(0), pl.program_id(1)

        @pl.when(c == 0)
        def _():
            s_ref[g] = jnp.zeros((PAIRS, 128, NS), F32)

        sps = [s_ref[g, q] for q in range(PAIRS)]
        for q in range(PAIRS):
            sin_ref[0, 0, q] = sps[q]
        xps = [x_ref[:, 128 * q:128 * (q + 1)] for q in range(PAIRS)]
        ys, snew = _ssd_chunk(g, xps, dt_ref[...], b_ref[...], c_ref[...], sps, dtb_ref[...], al_ref[...], dsk_ref[...])
        for q in range(PAIRS):
            y_ref[:, 128 * q:128 * (q + 1)] = ys[q]
            s_ref[g, q] = snew[q]

    return _call(
        body, jobs=jobs, name=name,
        out_shape=(jax.ShapeDtypeStruct((t, DI), F32), jax.ShapeDtypeStruct((nc, NG, PAIRS, 128, NS), F32)),
        grid=(nc, NG),
        in_specs=[xs(ident), bspec(ident), cspec(ident), dts(ident), par, par, par],
        out_specs=(pl.BlockSpec((L, 128 * PAIRS), lambda c, g: (c, g)), st(ident)),
        scratch_shapes=[pltpu.VMEM((NG, PAIRS, 128, NS), F32)],
        compiler_params=_cp(("arbitrary", "arbitrary")),
    )(xbc_act, xbc_act, xbc_act, proj, dtb, alog, dsk)


def _ssd_bwd(xbc_act, proj, dtb, alog, dsk, states, dy, dproj, *, name, jobs=()):
    t = proj.shape[0]
    nc, xs, bspec, cspec, dts, par, st = _ssd_specs(t)
    rev = lambda c: nc - 1 - c

    def body(x_ref, b_ref, c_ref, dt_ref, dtb_ref, al_ref, dsk_ref, sin_ref, dy_ref, _,
             dx_ref, db_ref, dc_ref, ddtp_ref, ddtb_ref, dal_ref, ddsk_ref, ds_ref, ddt_ref):
        c, g = pl.program_id(0), pl.program_id(1)

        @pl.when(c == 0)
        def _():
            ds_ref[g] = jnp.zeros((PAIRS, 128, NS), F32)

        @pl.when(jnp.logical_and(c == 0, g == 0))
        def _():
            ddtb_ref[...] = jnp.zeros_like(ddtb_ref)
            dal_ref[...] = jnp.zeros_like(dal_ref)
            ddsk_ref[...] = jnp.zeros_like(ddsk_ref)

        @pl.when(g == 0)
        def _():
            ddt_ref[...] = jnp.zeros_like(ddt_ref)

        sps = [sin_ref[0, 0, q] for q in range(PAIRS)]
        xps = [x_ref[:, 128 * q:128 * (q + 1)] for q in range(PAIRS)]
        _, vjp = jax.vjp(functools.partial(_ssd_chunk, g), xps, dt_ref[...], b_ref[...], c_ref[...], sps,
                         dtb_ref[...], al_ref[...], dsk_ref[...])
        dys = [dy_ref[:, 128 * q:128 * (q + 1)] for q in range(PAIRS)]
        dss = [ds_ref[g, q] for q in range(PAIRS)]
        dxps, ddt, db, dc, dsps, ddtb, dal, ddsk = vjp((dys, dss))
        for q in range(PAIRS):
            dx_ref[:, 128 * q:128 * (q + 1)] = dxps[q]
            ds_ref[g, q] = dsps[q]
        db_ref[...] = db
        dc_ref[...] = dc
        ddt_ref[...] += ddt
        ddtb_ref[...] += jnp.broadcast_to(ddtb, ddtb_ref.shape)
        dal_ref[...] += jnp.broadcast_to(dal, dal_ref.shape)
        ddsk_ref[...] += jnp.broadcast_to(ddsk, ddsk_ref.shape)

        @pl.when(g == NG - 1)
        def _():
            ddtp_ref[:, :128] = ddt_ref[...].astype(BF16)
            ddtp_ref[:, 128:] = jnp.zeros((L, DT_PAD - 128), BF16)

    acc = pl.BlockSpec((8, 128), lambda c, g: (0, 0))
    o8 = jax.ShapeDtypeStruct((8, 128), F32)
    return _call(
        body, jobs=jobs, name=name,
        out_shape=(jax.ShapeDtypeStruct((t, DI), F32), jax.ShapeDtypeStruct((t, NG * NS), F32),
                   jax.ShapeDtypeStruct((t, NG * NS), F32), jax.ShapeDtypeStruct(dproj.shape, BF16), o8, o8, o8),
        grid=(nc, NG),
        in_specs=[xs(rev), bspec(rev), cspec(rev), dts(rev), par, par, par, st(rev),
                  pl.BlockSpec((L, 128 * PAIRS), lambda c, g: (rev(c), g)), ANY],
        out_specs=(pl.BlockSpec((L, 128 * PAIRS), lambda c, g: (rev(c), g)),
                   pl.BlockSpec((L, NS), lambda c, g: (rev(c), g)), pl.BlockSpec((L, NS), lambda c, g: (rev(c), g)),
                   pl.BlockSpec((L, DT_PAD), lambda c, g: (rev(c), O_DT // DT_PAD)), acc, acc, acc),
        scratch_shapes=[pltpu.VMEM((NG, PAIRS, 128, NS), F32), pltpu.VMEM((L, 128), F32)],
        compiler_params=_cp(("arbitrary", "arbitrary")), aliases={9: 3},
    )(xbc_act, xbc_act, xbc_act, proj, dtb, alog, dsk, states, dy, dproj)


def _pad_lanes(v, n=128):
    return jnp.pad(v, ((0, 0), (0, n - v.shape[1])))


class _LocalPlan:
    core = 0

    def __init__(self, big):
        self.big, self.grad, self.halves = big, {}, {}

    def w(self, n):
        return self.big[n]

    def g(self, n, a):
        self.grad[n] = a

    def g_half(self, n, which, a):
        self.halves[which] = a
        if len(self.halves) == 2:
            self.grad[n] = jnp.concatenate([self.halves["keep"], self.halves["send"]], axis=0)

    def jobs(self, tag):
        return ()


def _local_step(x, p, positions, target, small, plan):
    t = x.shape[0]
    cos, sin = _rope_tables(positions, t)
    dtb, alog, dsk = _pad_lanes(small["dt_bias"]), _pad_lanes(small["a_log"]), _pad_lanes(small["d_skip"])
    w, jobs = plan.w, plan.jobs

    def mm(a, b, *, name, tm=t, **kw):
        return _matmul(a, b, tm=tm, tn=512, name=name, jobs=jobs(name), **kw)

    def dw(wname, a, dy, *, name, tm):
        plan.g(wname, _matmul(a, dy, ta=True, out_dtype=BF16, tm=tm, tn=512, tk=t, name=name, jobs=jobs(name)))

    u = _rmsnorm_fwd(x, small["g_mix"], name="norm_mix")
    proj = mm(u, w("w_in"), tk=D, name="mm_in")
    attn, lse = _attn_fwd(proj, cos, sin, small["sinks"], name="attn_fwd", jobs=jobs("attn_fwd"))
    out_a = mm(attn, w("w_attn_br"), tk=QD, name="mm_attn_br")
    xbc_act = _conv_fwd(proj, small["conv_w"], small["conv_b"], name="conv_fwd")
    y_pre, states = _ssd_fwd(xbc_act, proj, dtb, alog, dsk, name="ssd_fwd", jobs=jobs("ssd_fwd"))
    yn = _gated_norm_fwd(y_pre, proj, small["g_ssd"], name="gated_norm_fwd")
    out_s = mm(yn, w("w_ssd_br"), tk=DI, name="mm_ssd_br")
    merged = _merge_fwd(proj, out_a, out_s, name="merge_fwd")
    h1 = mm(merged, w("w_o"), add=x, tk=D, name="mm_o")
    f = _rmsnorm_fwd(h1, small["g_ffn"], name="norm_ffn")
    gate, up, act = _swiglu_fwd(f, w("w_gate"), w("w_up"), name="swiglu_fwd", jobs=jobs("swiglu_fwd"))
    h2 = mm(act, w("w_down"), add=h1, tm=t // 2, tk=FFN // 2, name="mm_down")
    e = _rmsnorm_fwd(h2, small["g_ple"], name="norm_ple")
    pgl = mm(e, w("w_ple_gate"), tk=D, name="mm_ple_gate")
    pb = p.astype(BF16)
    pp = mm(pb, w("w_ple_proj"), tk=PLE, name="mm_ple_proj")
    dh3, dpgl, dpp, loss, dg_final = _final(h2, pgl, pp, target, small["g_final"].reshape(1, D), name="final")

    dw("w_ple_proj", pb, dpp, tm=PLE, name="mm_d_ple_proj")
    dw("w_ple_gate", e, dpgl, tm=D, name="mm_d_ple_gate")
    de = mm(dpgl, w("w_ple_gate"), tb=True, tk=D, name="mm_de")
    dh2, dh2b, dg_ple = _rmsnorm_bwd(h2, small["g_ple"], de, dh3, name="norm_ple_bwd", jobs=jobs("norm_ple_bwd"))
    dw("w_down", act, dh2b, tm=FFN // 2, name="mm_d_down")
    dact = mm(dh2b, w("w_down"), tb=True, tk=D, name="mm_dact")
    dgate, dup = _swiglu_bwd(gate, up, dact, name="swiglu_bwd", jobs=jobs("swiglu_bwd"))
    dw("w_gate", f, dgate, tm=D, name="mm_d_gate")
    dw("w_up", f, dup, tm=D, name="mm_d_up")
    df = mm(dgate, w("w_gate"), tb=True, tm=t // 2, tk=FFN // 2, name="mm_df_gate")
    df = mm(dup, w("w_up"), tb=True, add=df, tm=t // 2, tk=FFN // 2, name="mm_df_up")
    dh1, dh1b, dg_ffn = _rmsnorm_bwd(h1, small["g_ffn"], df, dh2, name="norm_ffn_bwd", jobs=jobs("norm_ffn_bwd"))
    dw("w_o", merged, dh1b, tm=D, name="mm_d_o")
    dmerged = mm(dh1b, w("w_o"), tb=True, tk=D, name="mm_dmerged")
    dout_a, dout_s, dproj = _merge_bwd(proj, out_a, out_s, dmerged, name="merge_bwd")
    dw("w_attn_br", attn, dout_a, tm=QD, name="mm_d_attn_br")
    dw("w_ssd_br", yn, dout_s, tm=DI, name="mm_d_ssd_br")
    dattn = mm(dout_a, w("w_attn_br"), tb=True, tk=D, name="mm_dattn")
    dyn = mm(dout_s, w("w_ssd_br"), tb=True, tk=D, name="mm_dyn")
    dproj, dsinks = _attn_bwd(proj, cos, sin, small["sinks"], attn, lse, dattn, dproj, name="attn_bwd",
                              jobs=jobs("attn_bwd"))
    dy_pre, dproj, dg_ssd = _gated_norm_bwd(y_pre, proj, small["g_ssd"], dyn, dproj, name="gated_norm_bwd",
                                            jobs=jobs("gated_norm_bwd"))
    dxs, db, dc, dproj, ddtb, dalog, ddsk = _ssd_bwd(xbc_act, proj, dtb, alog, dsk, states, dy_pre, dproj, name="ssd_bwd",
                                                     jobs=jobs("ssd_bwd"))
    dproj, dconv_w, dconv_b = _conv_bwd(proj, small["conv_w"], small["conv_b"], dxs, db, dc, dproj, name="conv_bwd",
                                        jobs=jobs("conv_bwd"))
    for which, h in (("send", 1 - plan.core), ("keep", plan.core)):
        uh = lax.dynamic_slice_in_dim(u, h * (D // 2), D // 2, axis=1)
        name = "mm_d_in_" + which
        plan.g_half("w_in", which, _matmul(uh, dproj, ta=True, out_dtype=BF16, tm=D // 2, tn=512, tk=t, name=name,
                                           jobs=jobs(name)))
    du = mm(dproj, w("w_in"), tb=True, tm=t // 2, tk=NP // 4, name="mm_du")
    grad_x, _, dg_mix = _rmsnorm_bwd(x, small["g_mix"], du, dh1, name="norm_mix_bwd", jobs=jobs("norm_mix_bwd"))

    gs = {
        "g_mix": dg_mix[:1], "conv_w": dconv_w[:CW], "conv_b": dconv_b[:1], "dt_bias": ddtb[:1, :NH],
        "a_log": dalog[:1, :NH], "d_skip": ddsk[:1, :NH], "g_ssd": dg_ssd[:1], "sinks": dsinks[:1, :NQH],
        "g_ffn": dg_ffn[:1], "g_ple": dg_ple[:1], "g_final": dg_final[0],
    }
    return loss, grad_x, gs


def _to_kernel_cols(w):
    seg = lambda o, n: w[:, o:o + n]
    return jnp.concatenate([seg(R_GA, D), seg(R_GS, D), seg(R_Z, DI), seg(R_XBC, CONV), seg(R_Q, QD), seg(R_K, KVD),
                            seg(R_V, KVD), seg(R_DT, NH), jnp.zeros((w.shape[0], DT_PAD - NH), w.dtype)], axis=1)


def _from_kernel_cols(g):
    seg = lambda o, n: g[:, o:o + n]
    return jnp.concatenate([seg(O_Q, QD), seg(O_K, KVD), seg(O_V, KVD), seg(O_Z, DI), seg(O_XBC, CONV), seg(O_DT, NH),
                            seg(O_GA, D), seg(O_GS, D)], axis=1)


def _shard_pieces():
    segs = ((R_Q, QD, O_Q), (R_K, KVD, O_K), (R_V, KVD, O_V), (R_Z, DI, O_Z), (R_XBC, CONV, O_XBC), (R_DT, NH, O_DT),
            (R_GA, D, O_GA), (R_GS, D, O_GS))
    cs = IN_DIM // NCHIP
    out = []
    for j in range(NCHIP):
        for r0, n, k0 in segs:
            lo, hi = max(r0, j * cs), min(r0 + n, (j + 1) * cs)
            if lo < hi:
                out.append((j, lo - j * cs, hi - lo, k0 + lo - r0))
    return out


def _slabs_to_kernel_cols(slabs):
    pieces = sorted(_shard_pieces(), key=lambda p: p[3])
    cols, at = [], 0
    for j, a, n, k0 in pieces:
        if k0 > at:
            cols.append(jnp.zeros((slabs.shape[1], k0 - at), slabs.dtype))
        cols.append(slabs[j, :, a:a + n])
        at = k0 + n
    cols.append(jnp.zeros((slabs.shape[1], NP - at), slabs.dtype))
    return jnp.concatenate(cols, axis=1)


def _kernel_cols_to_slabs(g):
    pieces = _shard_pieces()
    return jnp.stack([jnp.concatenate([g[:, k0:k0 + n] for j, a, n, k0 in pieces if j == s], axis=1)
                      for s in range(NCHIP)])


RELS = ((0, 1), (1, 0), (1, 1))
MATS = {
    n: (n, kind, 1, r, c, tp, tf) for n, kind, r, c, tp, tf in (
        ("w_in", "stk", 2048, 2696, 256, 256),
        ("w_attn_br", "col", 1024, 512, 256, 256),
        ("w_ssd_br", "row", 512, 2048, 512, 256),
        ("w_o", "row", 512, 2048, 512, 256),
        ("w_gate", "col", 2048, 1408, 256, 256),
        ("w_up", "col", 2048, 1408, 256, 256),
        ("w_down", "row", 1408, 2048, 704, 704),
        ("w_ple_gate", "row", 512, 2048, 512, 256),
        ("w_ple_proj", "col", 256, 512, 128, 128),
    )}


def _pos():
    return lax.axis_index("x"), lax.axis_index("y"), lax.axis_index("c")


def _flip(v, a):
    return 1 - v if a else v


def _remote(src, dst, send, recv, dev):
    return pltpu.make_async_remote_copy(src_ref=src, dst_ref=dst, send_sem=send, recv_sem=recv, device_id=dev,
                                        device_id_type=MESH)


def _whole_shape(kind, g, r, c):
    return {"row": (g, NCHIP * r, c), "col": (g, r, NCHIP * c), "stk": (NCHIP, r, c)}[kind]


def _cols(j, c):
    return pl.ds(pl.multiple_of(j * c, 128), c)


def _whole_shard(kind, ref, j, r, c):
    if kind == "row":
        return ref.at[:, pl.ds(j * r, r), :]
    if kind == "col":
        return ref.at[:, :, _cols(j, c)]
    return ref.at[pl.ds(j, 1)]


def _whole_rows(kind, ref, j, row, n, r, c):
    if kind == "row":
        return ref.at[:, pl.ds(j * r + row, n), :]
    if kind == "col":
        return ref.at[:, pl.ds(row, n), _cols(j, c)]
    return ref.at[pl.ds(j, 1), pl.ds(row, n), :]


class _GatherJob(_Job):
    has_mid = True
    NCP = 10

    def __init__(self, names, shards, sink):
        self.mats = [MATS[n] for n in names]
        self.srcs = [shards[n] for n in names]
        self.news = [jax.ShapeDtypeStruct(_whole_shape(kind, g, r, c), BF16) for _, kind, g, r, c, _, _ in self.mats]
        n = len(names)
        self.scratch = [pltpu.SemaphoreType.DMA((self.NCP * n,)), pltpu.SemaphoreType.DMA((self.NCP * n,))]
        self.names, self.sink = names, sink

    def _copies(self, srcs, news, sems):
        send, recv = sems
        x, y, c = _pos()
        me, jx, jy, jd = 2 * x + y, 2 * (1 - x) + y, 2 * x + (1 - y), 2 * (1 - x) + (1 - y)
        nbx, nby, sib = (1 - x, y, c), (x, 1 - y, c), (x, y, 1 - c)
        cps = []
        for w, (_, kind, g, r, cc, _, _) in enumerate(self.mats):
            hr, qr = r // 2, r // 4
            at = lambda j, h, q, n: _whole_rows(kind, news[w], j, h * hr + q * qr, n, r, cc)
            mine = lambda q: srcs[w].at[:, pl.ds(c * hr + q * qr, qr), :]
            cp = lambda k, s, d, dev: _remote(s, d, send.at[self.NCP * w + k], recv.at[self.NCP * w + k], dev)
            cps.append([
                cp(0, mine(0), at(me, c, 0, qr), nbx), cp(1, mine(1), at(me, c, 1, qr), nbx),
                cp(2, mine(1), at(me, c, 1, qr), nby), cp(3, mine(0), at(me, c, 0, qr), nby),
                cp(4, at(jx, c, 0, qr), at(jx, c, 0, qr), nby), cp(5, at(jy, c, 1, qr), at(jy, c, 1, qr), nbx),
                cp(6, at(jx, c, 0, hr), at(jx, c, 0, hr), sib), cp(7, at(jy, c, 0, hr), at(jy, c, 0, hr), sib),
                cp(8, at(jd, c, 0, hr), at(jd, c, 0, hr), sib),
                cp(9, srcs[w], _whole_shard(kind, news[w], me, r, cc), sib)])
        return cps

    def start(self, srcs, dsts, news, sems):
        cps = self._copies(srcs, news, sems)
        for w in range(len(self.mats)):
            for k in (0, 1, 2, 3, 9):
                cps[w][k].start()

    def mid(self, srcs, dsts, news, sems):
        cps = self._copies(srcs, news, sems)
        for w in range(len(self.mats)):
            cps[w][0].wait_recv()
            cps[w][4].start()
            cps[w][2].wait_recv()
            cps[w][5].start()

    def finish(self, srcs, dsts, news, sems):
        cps = self._copies(srcs, news, sems)
        for w in range(len(self.mats)):
            cps[w][1].wait_recv()
            cps[w][6].start()
            cps[w][3].wait_recv()
            cps[w][7].start()
        for w in range(len(self.mats)):
            cps[w][4].wait_recv()
            cps[w][5].wait_recv()
            cps[w][8].start()
        for w in range(len(self.mats)):
            for k in (6, 7, 8, 9):
                cps[w][k].wait_recv()
            for k in range(self.NCP):
                cps[w][k].wait_send()

    def done(self, dsts, news):
        for n, a in zip(self.names, news):
            self.sink[n] = a


class _SwapJob(_Job):
    def __init__(self, build, ncopies, *, srcs=(), dsts=(), news=(), done=None):
        self.build, self.srcs, self.dsts, self.news, self._done = build, list(srcs), list(dsts), list(news), done
        self.scratch = [pltpu.SemaphoreType.DMA((ncopies,)), pltpu.SemaphoreType.DMA((ncopies,))]

    def start(self, srcs, dsts, news, sems):
        for cp in self.build(srcs, dsts, news, *sems):
            cp.start()

    def finish(self, srcs, dsts, news, sems):
        for cp in self.build(srcs, dsts, news, *sems):
            cp.wait()

    def done(self, dsts, news):
        if self._done is not None:
            self._done(dsts, news)


def _half_of_whole(kind, ref, h, r, c):
    if kind == "row":
        return ref.at[:, :, pl.ds(pl.multiple_of(h * (c // 2), 128), c // 2)]
    return ref.at[:, pl.ds(h * (r // 2), r // 2), :]


def _half_shape(kind, g, r, c):
    return {"row": (g, NCHIP * r, c // 2), "col": (g, r // 2, NCHIP * c), "stk": (NCHIP, r // 2, c)}[kind]


def _piece_shape(kind, g, r, c):
    return {"row": (g, r, c // 2), "col": (g, r // 2, c), "stk": (1, r // 2, c)}[kind]


def _piece_of_half(kind, ref, j, r, c):
    if kind == "row":
        return ref.at[:, pl.ds(j * r, r), :]
    if kind == "col":
        return ref.at[:, :, _cols(j, c)]
    return ref.at[pl.ds(j, 1)]


def _half_of_shard(kind, ref, h, r, c):
    if kind == "row":
        return ref.at[:, :, pl.ds(pl.multiple_of(h * (c // 2), 128), c // 2)]
    return ref.at[:, pl.ds(h * (r // 2), r // 2), :]


def _pair_sum(pack, core, mine, got, whole=True):
    name, kind, g, r, c, tr, _ = pack
    hs = _half_shape(kind, g, r, c)
    nb = hs[1] // tr

    def body(core_ref, a_ref, b_ref, o_ref):
        o_ref[...] = (a_ref[...].astype(F32) + b_ref[...].astype(F32)).astype(BF16)

    blk = (1, tr, hs[2])
    same = lambda gi, i, core_ref: (gi, i, 0)
    if not whole:
        a_map = same
    elif kind == "row":
        a_map = lambda gi, i, core_ref: (gi, i, core_ref[0])
    else:
        a_map = lambda gi, i, core_ref: (gi, core_ref[0] * nb + i, 0)
    return pl.pallas_call(
        body, name="pair_sum_" + name, out_shape=jax.ShapeDtypeStruct(hs, BF16),
        grid_spec=pltpu.PrefetchScalarGridSpec(
            num_scalar_prefetch=1, grid=(hs[0], nb),
            in_specs=[pl.BlockSpec(blk, a_map), pl.BlockSpec(blk, same)], out_specs=pl.BlockSpec(blk, same)),
        compiler_params=_cp(("parallel", "parallel")),
    )(core, mine, got)


def _shard_sum(pack, where, half, got):
    name, kind, g, r, c, _, tr = pack
    ps = _piece_shape(kind, g, r, c)
    nb = ps[1] // tr

    def body(where_ref, a_ref, b_ref, o_ref):
        o_ref[...] = a_ref[...].astype(F32) + ((b_ref[0].astype(F32) + b_ref[1].astype(F32)) + b_ref[2].astype(F32))

    blk = (1, tr, ps[2])
    if kind == "row":
        a_map = lambda gi, i, wr: (gi, wr[0] * nb + i, 0)
        o_map = lambda gi, i, wr: (gi, i, wr[1])
    elif kind == "col":
        a_map = lambda gi, i, wr: (gi, i, wr[0])
        o_map = lambda gi, i, wr: (gi, wr[1] * nb + i, 0)
    else:
        a_map = lambda gi, i, wr: (wr[0], i, 0)
        o_map = lambda gi, i, wr: (gi, wr[1] * nb + i, 0)
    return pl.pallas_call(
        body, name="shard_sum_" + name, out_shape=jax.ShapeDtypeStruct((g, r, c), F32),
        grid_spec=pltpu.PrefetchScalarGridSpec(
            num_scalar_prefetch=1, grid=(ps[0], nb),
            in_specs=[pl.BlockSpec(blk, a_map), pl.BlockSpec((3,) + blk, lambda gi, i, wr: (0, gi, i, 0))],
            out_specs=pl.BlockSpec(blk, o_map)),
        compiler_params=_cp(("parallel", "parallel")),
    )(where, half, got)


class _Plan:
    def __init__(self, shards, table):
        self.shards, self.table = shards, table
        self.whole, self.grad, self.got_a, self.half, self.got_b, self.sent_b, self.gshard = {}, {}, {}, {}, {}, {}, {}
        x, y, c = _pos()
        self.core = c
        self.core1 = c.reshape(1).astype(jnp.int32)
        self.where = jnp.stack([2 * x + y, c]).astype(jnp.int32)
        self._w_in = None
        self.send, self.keep = {}, {}

    def w(self, n):
        if n != "w_in":
            return self.whole[n][0]
        if self._w_in is None:
            self._w_in = _slabs_to_kernel_cols(self.whole[n])
        return self._w_in

    def g(self, n, a):
        self.grad[n] = a[None]

    def g_half(self, n, which, a):
        (self.send if which == "send" else self.keep)[n] = _kernel_cols_to_slabs(a)

    def jobs(self, tag):
        out = []
        for spec in self.table.get(tag, ()):
            out += getattr(self, "_" + spec[0])(*spec[1:])
        return out

    def run(self, name, jobs):
        if jobs:
            _call(lambda: None, jobs=jobs, name=name, out_shape=[], in_specs=[], out_specs=[])()

    def _gather(self, names):
        return [_GatherJob(names, self.shards, self.whole)]

    def _rs_a(self, names):
        mats = [MATS[n] for n in names]

        def build(srcs, dsts, news, send, recv):
            x, y, c = _pos()
            return [_remote(srcs[i] if names[i] in self.send else _half_of_whole(kind, srcs[i], 1 - c, r, cc), news[i],
                            send.at[i], recv.at[i], (x, y, 1 - c))
                    for i, (_, kind, g, r, cc, _, _) in enumerate(mats)]

        def done(dsts, news):
            self.got_a.update(zip(names, news))

        return [_SwapJob(build, len(names), srcs=[self.send.get(n, self.grad.get(n)) for n in names], done=done,
                         news=[jax.ShapeDtypeStruct(_half_shape(kind, g, r, c), BF16) for _, kind, g, r, c, _, _ in mats])]

    def _rs_b(self, names, ks=(0, 1, 2)):
        return [self._rs_b_one(n, ks) for n in names]

    def _rs_b_one(self, n, ks):
        _, kind, g, r, cc, _, _ = MATS[n]
        if n not in self.half:
            if n in self.keep:
                self.half[n] = _pair_sum(MATS[n], self.core1, self.keep[n], self.got_a[n], whole=False)
            else:
                self.half[n] = _pair_sum(MATS[n], self.core1, self.grad[n], self.got_a[n])

        def build(srcs, dsts, news, send, recv):
            x, y, c = _pos()
            land = (dsts or news)[0]
            cps = []
            for i, k in enumerate(ks):
                px, py = _flip(x, RELS[k][0]), _flip(y, RELS[k][1])
                cps.append(_remote(_piece_of_half(kind, srcs[0], 2 * px + py, r, cc), land.at[k], send.at[i], recv.at[i],
                                   (px, py, c)))
            return cps

        def done(dsts, news):
            self.got_b[n] = (dsts or news)[0]
            self.sent_b[n] = self.sent_b.get(n, ()) + tuple(ks)

        if n in self.got_b:
            return _SwapJob(build, len(ks), srcs=[self.half[n]], dsts=[self.got_b[n]], done=done)
        shape = jax.ShapeDtypeStruct((3,) + _piece_shape(kind, g, r, cc), BF16)
        return _SwapJob(build, len(ks), srcs=[self.half[n]], news=[shape], done=done)

    def _rs_c(self, names):
        mats = [MATS[n] for n in names]
        for n in names:
            assert sorted(self.sent_b[n]) == [0, 1, 2], (n, self.sent_b[n])
        parts = [_shard_sum(MATS[n], self.where, self.half[n], self.got_b[n]) for n in names]

        def build(srcs, dsts, news, send, recv):
            x, y, c = _pos()
            cps = []
            for i, (_, kind, g, r, cc, _, _) in enumerate(mats):
                mine = _half_of_shard(kind, dsts[i], c, r, cc)
                cps.append(_remote(mine, mine, send.at[i], recv.at[i], (x, y, 1 - c)))
            return cps

        def done(dsts, news):
            self.gshard.update(zip(names, dsts))

        return [_SwapJob(build, len(names), dsts=parts, done=done)]

    def finish(self, n):
        if n not in self.got_a:
            self.run("rs_a_" + n, self._rs_a((n,)))
        left = tuple(k for k in range(3) if k not in self.sent_b.get(n, ()))
        if left:
            self.run("rs_b_" + n, self._rs_b((n,), left))
        if n not in self.gshard:
            self.run("rs_c_" + n, self._rs_c((n,)))
        return self.gshard[n]


TABLE = {
    "gather_w_in": (("gather", ("w_in",)),),
    "mm_in": (("gather", ("w_gate",)),),
    "attn_fwd": (("gather", ("w_attn_br", "w_ssd_br")),),
    "ssd_fwd": (("gather", ("w_up",)),),
    "mm_ssd_br": (("gather", ("w_o",)),),
    "swiglu_fwd": (("gather", ("w_down",)),),
    "mm_down": (("gather", ("w_ple_gate", "w_ple_proj")),),
    "mm_de": (("rs_a", ("w_ple_proj", "w_ple_gate")),),
    "mm_d_down": (("rs_b", ("w_ple_proj", "w_ple_gate")),),
    "mm_dact": (("rs_a", ("w_down",)),),
    "swiglu_bwd": (("rs_c", ("w_ple_proj", "w_ple_gate")),),
    "mm_df_gate": (("rs_a", ("w_gate", "w_up")),),
    "mm_dmerged": (("rs_a", ("w_o",)),),
    "mm_dyn": (("rs_a", ("w_attn_br", "w_ssd_br")),),
    "attn_bwd": (("rs_b", ("w_down",)),),
    "gated_norm_bwd": (("rs_c", ("w_down",)),),
    "ssd_bwd": (("rs_b", ("w_gate", "w_up")),),
    "conv_bwd": (("rs_b", ("w_o",)),),
    "mm_d_in_send": (("rs_b", ("w_attn_br", "w_ssd_br")), ("rs_c", ("w_gate", "w_up"))),
    "mm_d_in_keep": (("rs_a", ("w_in",)), ("rs_c", ("w_o",))),
    "mm_du": (("rs_b", ("w_in",)),),
    "norm_mix_bwd": (("rs_c", ("w_attn_br", "w_ssd_br")),),
}


NDEV = 8


def _allreduce_small(v, *, name):
    rows = v.shape[0]

    def body(v_ref, o_ref, slots, send, recv):
        x, y, c = _pos()
        me = 4 * x + 2 * y + c
        slots[me] = v_ref[...]
        cps = []
        for k in range(1, NDEV):
            peer = (_flip(x, k & 4), _flip(y, k & 2), _flip(c, k & 1))
            cp = _remote(v_ref, slots.at[me], send.at[k - 1], recv.at[k - 1], peer)
            cp.start()
            cps.append(cp)
        for cp in cps:
            cp.wait()
        acc = slots[0]
        for s in range(1, NDEV):
            acc = acc + slots[s]
        o_ref[...] = acc

    return pl.pallas_call(
        body, name=name, out_shape=jax.ShapeDtypeStruct((rows, 128), F32),
        in_specs=[pl.BlockSpec(memory_space=pltpu.VMEM)], out_specs=pl.BlockSpec(memory_space=pltpu.VMEM),
        scratch_shapes=[pltpu.VMEM((NDEV, rows, 128), F32), pltpu.SemaphoreType.DMA((NDEV - 1,)),
                        pltpu.SemaphoreType.DMA((NDEV - 1,))],
    )(v)


def _adamw(w, g, m, v, *, name, tr=None, tc=None, jobs=()):
    r, c = w.shape
    tr = r if tr is None else tr
    c1 = 1.0 / (1.0 - B1 ** STEP)
    c2 = 1.0 / (1.0 - B2 ** STEP)

    def body(w_ref, g_ref, m_ref, v_ref, d_ref, mo_ref, vo_ref):
        gv = g_ref[...]
        mn = B1 * m_ref[...] + (1.0 - B1) * gv
        vn = B2 * v_ref[...] + (1.0 - B2) * (gv * gv)
        mo_ref[...] = mn
        vo_ref[...] = vn
        d_ref[...] = -LR * ((mn * c1) / (jnp.sqrt(vn * c2) + AEPS) + WD * w_ref[...])

    if tc is None:
        blk, grid = pl.BlockSpec((tr, c), lambda i: (i, 0)), (r // tr,)
    else:
        blk, grid = pl.BlockSpec((r, tc), lambda i: (0, i)), (c // tc,)
    o = jax.ShapeDtypeStruct((r, c), F32)
    return _call(
        body, jobs=jobs, name=name, out_shape=(o, o, o), grid=grid, in_specs=[blk] * 4, out_specs=(blk, blk, blk),
        compiler_params=_cp(("parallel",)),
    )(w, g, m, v)


WEIGHTS = ("g_mix", "w_in", "conv_w", "conv_b", "dt_bias", "a_log", "d_skip", "g_ssd", "sinks", "w_attn_br", "w_ssd_br",
           "w_o", "g_ffn", "w_gate", "w_up", "w_down", "g_ple", "w_ple_gate", "w_ple_proj", "g_final")
BIG = {
    "w_gate": 256, "w_up": 256, "w_down": 128, "w_ssd_br": 128, "w_o": 128, "w_ple_gate": 128, "w_attn_br": 256,
    "w_ple_proj": 256, "w_in": None,
}
SMALL = tuple(n for n in WEIGHTS if n not in BIG)


def _pack_small(parts):
    rows = []
    for a in parts:
        a = a.reshape(-1)
        rows.append(jnp.pad(a, (0, -a.shape[0] % 128)).reshape(-1, 128))
    out = jnp.concatenate(rows, axis=0)
    return jnp.pad(out, ((0, -out.shape[0] % 8), (0, 0)))


def _unpack_small(packed, shapes):
    out, r = [], 0
    for s in shapes:
        n = int(np.prod(s))
        nr = -(-n // 128)
        out.append(packed[r:r + nr].reshape(-1)[:n].reshape(s))
        r += nr
    return out


def kernel(x, p, positions, g_mix, w_in, conv_w, conv_b, dt_bias, a_log, d_skip, g_ssd, sinks, w_attn_br, w_ssd_br, w_o, g_ffn, w_gate, w_up, w_down, g_ple, w_ple_gate, w_ple_proj, g_final, loss_target, m_g_mix, m_w_in, m_conv_w, m_conv_b, m_dt_bias, m_a_log, m_d_skip, m_g_ssd, m_sinks, m_w_attn_br, m_w_ssd_br, m_w_o, m_g_ffn, m_w_gate, m_w_up, m_w_down, m_g_ple, m_w_ple_gate, m_w_ple_proj, m_g_final, v_g_mix, v_w_in, v_conv_w, v_conv_b, v_dt_bias, v_a_log, v_d_skip, v_g_ssd, v_sinks, v_w_attn_br, v_w_ssd_br, v_w_o, v_g_ffn, v_w_gate, v_w_up, v_w_down, v_g_ple, v_w_ple_gate, v_w_ple_proj, v_g_final):
    w = dict(zip(WEIGHTS, (g_mix, w_in, conv_w, conv_b, dt_bias, a_log, d_skip, g_ssd, sinks, w_attn_br, w_ssd_br, w_o,
                           g_ffn, w_gate, w_up, w_down, g_ple, w_ple_gate, w_ple_proj, g_final)))
    m = dict(zip(WEIGHTS, (m_g_mix, m_w_in, m_conv_w, m_conv_b, m_dt_bias, m_a_log, m_d_skip, m_g_ssd, m_sinks, m_w_attn_br,
                           m_w_ssd_br, m_w_o, m_g_ffn, m_w_gate, m_w_up, m_w_down, m_g_ple, m_w_ple_gate, m_w_ple_proj,
                           m_g_final)))
    v = dict(zip(WEIGHTS, (v_g_mix, v_w_in, v_conv_w, v_conv_b, v_dt_bias, v_a_log, v_d_skip, v_g_ssd, v_sinks, v_w_attn_br,
                           v_w_ssd_br, v_w_o, v_g_ffn, v_w_gate, v_w_up, v_w_down, v_g_ple, v_w_ple_gate, v_w_ple_proj,
                           v_g_final)))
    xi, yi, ci = _pos()
    chip = 2 * xi + yi
    t = x.shape[1]
    cshard = CONV // NCHIP

    plan = _Plan({n: w[n].astype(BF16) for n in MATS}, TABLE)
    plan.run("gather_w_in", plan.jobs("gather_w_in"))
    placed = lax.dynamic_update_slice(jnp.zeros((CW, CONV), F32), w["conv_w"][0], (0, chip * cshard))
    conv_whole = _allreduce_small(jnp.where(ci == 0, placed, 0.0).reshape(-1, 128), name="gather_conv_w").reshape(CW, CONV)

    small = {n: w[n] for n in ("g_mix", "conv_b", "dt_bias", "a_log", "d_skip", "g_ssd", "sinks", "g_ffn", "g_ple", "g_final")}
    small["conv_w"] = conv_whole
    loss8, grad_x, gs = _local_step(x[0], p[0, 0], positions, loss_target[0], small, plan)

    order = ("g_mix", "conv_b", "dt_bias", "a_log", "d_skip", "g_ssd", "sinks", "g_ffn", "g_ple", "g_final", "conv_w")
    summed = _allreduce_small(_pack_small([loss8[0, :1]] + [gs[n] for n in order]), name="sum_small")
    parts = _unpack_small(summed, [(1,)] + [w[n].shape for n in order[:-1]] + [(CW, CONV)])
    loss = parts[0][0]
    grad = dict(zip(order, parts[1:]))
    grad["conv_w"] = lax.dynamic_slice(grad["conv_w"], (0, chip * cshard), (CW, cshard))[None]

    delta, new_m, new_v = {}, {}, {}
    for n, tr in BIG.items():
        grad[n] = plan.finish(n)
        if n == "w_in":
            d_, m_, v_ = _adamw(w[n][0].T, grad[n][0].T, m[n][0].T, v[n][0].T, tc=128, name="adamw_" + n)
            d_, m_, v_ = d_.T, m_.T, v_.T
        else:
            d_, m_, v_ = _adamw(w[n][0], grad[n][0], m[n][0], v[n][0], tr=tr, name="adamw_" + n)
        delta[n], new_m[n], new_v[n] = d_[None], m_[None], v_[None]
    shapes = [w[n].shape for n in SMALL]
    d_, m_, v_ = _adamw(_pack_small([w[n] for n in SMALL]), _pack_small([grad[n] for n in SMALL]),
                        _pack_small([m[n] for n in SMALL]), _pack_small([v[n] for n in SMALL]), tr=None, name="adamw_small")
    for n, a, b, c_ in zip(SMALL, _unpack_small(d_, shapes), _unpack_small(m_, shapes), _unpack_small(v_, shapes)):
        delta[n], new_m[n], new_v[n] = a, b, c_

    return (loss, grad_x[None], *[grad[n] for n in WEIGHTS], *[delta[n] for n in WEIGHTS],
            *[new_m[n] for n in WEIGHTS], *[new_v[n] for n in WEIGHTS])
```

```python
import functools

import jax
import jax.numpy as jnp
import numpy as np
from jax import lax
from jax.experimental import pallas as pl
from jax.experimental.pallas import tpu as pltpu

F32 = jnp.float32
BF16 = jnp.bfloat16
MESH = pl.DeviceIdType.MESH

D = 2048
HD = 64
NQH = 16
NKV = 4
QD = NQH * HD
KVD = NKV * HD
DI = 2048
NH = 32
NG = 4
NS = 128
CW = 4
L = 128
CONV = DI + 2 * NG * NS
FFN = 5632
PLE = 256
IN_DIM = QD + 2 * KVD + DI + CONV + NH + 2 * D
EPS = 1e-6
SSM_EPS = 1e-5
ROPE_THETA = 10000.0
LR, B1, B2, AEPS, WD, STEP = 0.001, 0.9, 0.999, 1e-08, 0.01, 10

O_GA, O_GS, O_Z, O_XBC, O_Q, O_K, O_V, O_DT = 0, 2048, 4096, 6144, 9216, 10240, 10496, 10752
DT_PAD = 512
NP = O_DT + DT_PAD
R_Q, R_K, R_V, R_Z, R_XBC, R_DT, R_GA, R_GS = 0, 1024, 1280, 1536, 3584, 6656, 6688, 8736

NCHIP = 4
VMEM_LIMIT = 52 * 1024 * 1024
NEG = -1e30


def _cp(sem=None):
    return pltpu.CompilerParams(dimension_semantics=sem, vmem_limit_bytes=VMEM_LIMIT)


def _dot(a, b):
    return lax.dot_general(a, b, (((1,), (0,)), ((), ())), preferred_element_type=F32)


def _dot_nt(a, b):
    return lax.dot_general(a, b, (((1,), (1,)), ((), ())), preferred_element_type=F32)


def _dot_tn(a, b):
    return lax.dot_general(a, b, (((0,), (0,)), ((), ())), preferred_element_type=F32)


def _sigmoid(x):
    return 1.0 / (1.0 + jnp.exp(-x))


def _bf16_dot(dot, da, db):
    @jax.custom_vjp
    def f(a, b):
        return dot(a.astype(BF16), b.astype(BF16))

    def fwd(a, b):
        return f(a, b), (a.astype(BF16), b.astype(BF16))

    def bwd(res, g):
        a, b = res
        g = g.astype(BF16)
        return da(g, a, b), db(g, a, b)

    f.defvjp(fwd, bwd)
    return f


_bdot = _bf16_dot(_dot, lambda g, a, b: _dot_nt(g, b), lambda g, a, b: _dot_tn(a, g))
_bdot_nt = _bf16_dot(_dot_nt, lambda g, a, b: _dot(g, b), lambda g, a, b: _dot_tn(g, a))
_bdot_tn = _bf16_dot(_dot_tn, lambda g, a, b: _dot_nt(b, g), lambda g, a, b: _dot(a, g))


ANY = pl.BlockSpec(memory_space=pl.ANY)


class _Job:
    srcs, dsts, news, scratch = (), (), (), ()
    has_mid = False

    def start(self, srcs, dsts, news, sems):
        raise NotImplementedError

    def mid(self, srcs, dsts, news, sems):
        pass

    def finish(self, srcs, dsts, news, sems):
        raise NotImplementedError

    def done(self, dsts, news):
        pass


def _call(body, *, jobs=(), name, out_shape, in_specs, out_specs, grid=(), scratch_shapes=(), compiler_params=None,
          aliases=None):
    jobs = [j for j in jobs if j is not None]
    aliases = dict(aliases or {})
    if not jobs:
        return pl.pallas_call(body, name=name, out_shape=out_shape, in_specs=in_specs, out_specs=out_specs, grid=grid,
                              scratch_shapes=scratch_shapes, compiler_params=compiler_params,
                              input_output_aliases=aliases)
    single = not isinstance(out_shape, (tuple, list))
    outs = [out_shape] if single else list(out_shape)
    ospecs = [out_specs] if single else list(out_specs)
    n_in, n_out, n_scr = len(in_specs), len(outs), len(scratch_shapes)
    srcs = [a for j in jobs for a in j.srcs]
    dsts = [a for j in jobs for a in j.dsts]
    news = [a for j in jobs for a in j.news]
    sems = [a for j in jobs for a in j.scratch]

    def wrapped(*refs):
        pos = n_in + len(srcs) + len(dsts)
        ins, jsrc = refs[:n_in], refs[n_in:n_in + len(srcs)]
        o_refs = refs[pos:pos + n_out]
        pos += n_out
        jdst, jnew = refs[pos:pos + len(dsts)], refs[pos + len(dsts):pos + len(dsts) + len(news)]
        pos += len(dsts) + len(news)
        scr, jsem = refs[pos:pos + n_scr], refs[pos + n_scr:]

        def run(which):
            a = b = c = d = 0
            for j in jobs:
                getattr(j, which)(jsrc[a:a + len(j.srcs)], jdst[b:b + len(j.dsts)], jnew[c:c + len(j.news)],
                                  jsem[d:d + len(j.scratch)])
                a, b, c, d = a + len(j.srcs), b + len(j.dsts), c + len(j.news), d + len(j.scratch)

        if not grid:
            run("start")
            run("mid")
            body(*ins, *o_refs, *scr)
            run("finish")
            return
        step = functools.reduce(lambda acc, a: acc * grid[a] + pl.program_id(a), range(len(grid)), 0)
        steps = int(np.prod(grid))
        pl.when(step == 0)(lambda: run("start"))
        if any(j.has_mid for j in jobs):
            pl.when(step == steps // 3)(lambda: run("mid"))
        body(*ins, *o_refs, *scr)
        pl.when(step == steps - 1)(lambda: run("finish"))

    call = pl.pallas_call(
        wrapped, name=name,
        out_shape=outs + [jax.ShapeDtypeStruct(a.shape, a.dtype) for a in dsts] + news,
        in_specs=list(in_specs) + [ANY] * (len(srcs) + len(dsts)),
        out_specs=ospecs + [ANY] * (len(dsts) + len(news)),
        grid=grid, scratch_shapes=list(scratch_shapes) + sems,
        input_output_aliases={**aliases, **{n_in + len(srcs) + i: n_out + i for i in range(len(dsts))}},
        compiler_params=_cp(("arbitrary",) * len(grid) if grid else None))

    def run_call(*args):
        res = call(*args, *srcs, *dsts)
        b, c = n_out, n_out + len(dsts)
        for j in jobs:
            j.done(res[b:b + len(j.dsts)], res[c:c + len(j.news)])
            b, c = b + len(j.dsts), c + len(j.news)
        return res[0] if single else tuple(res[:n_out])

    return run_call


def _matmul(a, b, *, ta=False, tb=False, out_dtype=F32, add=None, tm, tn, tk, name, jobs=()):
    k, m = a.shape if ta else a.shape[::-1]
    n = b.shape[0] if tb else b.shape[1]
    assert (b.shape[1] if tb else b.shape[0]) == k and not (ta and tb)
    assert m % tm == 0 and n % tn == 0 and k % tk == 0, (name, a.shape, b.shape)
    nk = k // tk
    has_add = add is not None

    def body(*refs):
        a_ref, b_ref = refs[0], refs[1]
        add_ref = refs[2] if has_add else None
        o_ref = refs[3] if has_add else refs[2]
        av = a_ref[...].astype(BF16)
        bv = b_ref[...].astype(BF16)
        part = _dot_tn(av, bv) if ta else _dot_nt(av, bv) if tb else _dot(av, bv)

        def finish(r):
            if has_add:
                r = r + add_ref[...]
            o_ref[...] = r.astype(out_dtype)

        if nk == 1:
            finish(part)
        elif out_dtype == F32:
            kk = pl.program_id(2)
            pl.when(kk == 0)(lambda: finish(part))

            @pl.when(kk > 0)
            def _():
                o_ref[...] += part
        else:
            acc_ref = refs[-1]
            kk = pl.program_id(2)

            @pl.when(kk == 0)
            def _():
                acc_ref[...] = part

            @pl.when(kk > 0)
            def _():
                acc_ref[...] += part

            @pl.when(kk == nk - 1)
            def _():
                finish(acc_ref[...])

    in_specs = [pl.BlockSpec((tk, tm), lambda i, j, kk: (kk, i)) if ta else pl.BlockSpec((tm, tk), lambda i, j, kk: (i, kk)),
                pl.BlockSpec((tn, tk), lambda i, j, kk: (j, kk)) if tb
                else pl.BlockSpec((tk, tn), lambda i, j, kk: (kk, j))]
    args = [a, b]
    if has_add:
        in_specs.append(pl.BlockSpec((tm, tn), lambda i, j, kk: (i, j)))
        args.append(add)
    return _call(
        body, jobs=jobs, name=name,
        out_shape=jax.ShapeDtypeStruct((m, n), out_dtype),
        grid=(m // tm, n // tn, nk),
        in_specs=in_specs,
        out_specs=pl.BlockSpec((tm, tn), lambda i, j, kk: (i, j)),
        scratch_shapes=[pltpu.VMEM((tm, tn), F32)] if nk > 1 and out_dtype != F32 else [],
        compiler_params=_cp(("parallel", "parallel", "arbitrary")),
    )(*args)


ROWS = 256


def _rmsnorm_fwd(x, g, *, name):
    t, d = x.shape

    def body(x_ref, g_ref, o_ref):
        xv = x_ref[...]
        r = lax.rsqrt(jnp.mean(xv * xv, axis=-1, keepdims=True) + EPS)
        o_ref[...] = (xv * r * g_ref[...]).astype(BF16)

    return pl.pallas_call(
        body, name=name, out_shape=jax.ShapeDtypeStruct((t, d), BF16), grid=(t // ROWS,),
        in_specs=[pl.BlockSpec((ROWS, d), lambda i: (i, 0)), pl.BlockSpec((1, d), lambda i: (0, 0))],
        out_specs=pl.BlockSpec((ROWS, d), lambda i: (i, 0)), compiler_params=_cp(("parallel",)),
    )(x, g)


def _rmsnorm_bwd(x, g, dy, dres, *, name, jobs=()):
    t, d = x.shape

    def body(x_ref, g_ref, dy_ref, dres_ref, dx_ref, dxb_ref, dg_ref):
        xv = x_ref[...]
        r = lax.rsqrt(jnp.mean(xv * xv, axis=-1, keepdims=True) + EPS)
        xh = xv * r
        dyv = dy_ref[...]
        dxh = dyv * g_ref[...]
        dx = r * (dxh - xh * jnp.mean(dxh * xh, axis=-1, keepdims=True))
        tot = dres_ref[...] + dx
        dx_ref[...] = tot
        dxb_ref[...] = tot.astype(BF16)

        @pl.when(pl.program_id(0) == 0)
        def _():
            dg_ref[...] = jnp.zeros_like(dg_ref)

        dg_ref[...] += jnp.broadcast_to(jnp.sum(dyv * xh, axis=0, keepdims=True), dg_ref.shape)

    row = pl.BlockSpec((ROWS, d), lambda i: (i, 0))
    return _call(
        body, jobs=jobs, name=name,
        out_shape=(jax.ShapeDtypeStruct((t, d), F32), jax.ShapeDtypeStruct((t, d), BF16),
                   jax.ShapeDtypeStruct((8, d), F32)),
        grid=(t // ROWS,),
        in_specs=[row, pl.BlockSpec((1, d), lambda i: (0, 0)), row, row],
        out_specs=(row, row, pl.BlockSpec((8, d), lambda i: (0, 0))),
        compiler_params=_cp(("arbitrary",)),
    )(x, g, dy, dres)


def _final(h2, pgl, pp, target, g_final, *, name):
    t, d = h2.shape

    def body(h2_ref, pgl_ref, pp_ref, tg_ref, g_ref, dh3_ref, dpgl_ref, dpp_ref, loss_ref, dg_ref):
        s = _sigmoid(pgl_ref[...])
        ppv = pp_ref[...]
        h3 = h2_ref[...] + s * ppv
        r = lax.rsqrt(jnp.mean(h3 * h3, axis=-1, keepdims=True) + EPS)
        xh = h3 * r
        gv = g_ref[...]
        err = xh * gv - tg_ref[...]
        dyv = err * (1.0 / d)
        dxh = dyv * gv
        dh3 = r * (dxh - xh * jnp.mean(dxh * xh, axis=-1, keepdims=True))
        dh3_ref[...] = dh3
        dpp_ref[...] = (dh3 * s).astype(BF16)
        dpgl_ref[...] = (dh3 * ppv * s * (1.0 - s)).astype(BF16)

        @pl.when(pl.program_id(0) == 0)
        def _():
            loss_ref[...] = jnp.zeros_like(loss_ref)
            dg_ref[...] = jnp.zeros_like(dg_ref)

        part = 0.5 * jnp.sum(jnp.mean(err * err, axis=-1, keepdims=True), axis=0, keepdims=True)
        loss_ref[...] += jnp.broadcast_to(part, loss_ref.shape)
        dg_ref[...] += jnp.broadcast_to(jnp.sum(dyv * xh, axis=0, keepdims=True), dg_ref.shape)

    row = pl.BlockSpec((ROWS, d), lambda i: (i, 0))
    return pl.pallas_call(
        body, name=name,
        out_shape=(jax.ShapeDtypeStruct((t, d), F32), jax.ShapeDtypeStruct((t, d), BF16),
                   jax.ShapeDtypeStruct((t, d), BF16), jax.ShapeDtypeStruct((8, 128), F32),
                   jax.ShapeDtypeStruct((8, d), F32)),
        grid=(t // ROWS,),
        in_specs=[row, row, row, row, pl.BlockSpec((1, d), lambda i: (0, 0))],
        out_specs=(row, row, row, pl.BlockSpec((8, 128), lambda i: (0, 0)), pl.BlockSpec((8, d), lambda i: (0, 0))),
        compiler_params=_cp(("arbitrary",)),
    )(h2, pgl, pp, target, g_final)


def _merge_fwd(proj, out_a, out_s, *, name):
    t = proj.shape[0]

    def body(ga_ref, gs_ref, a_ref, s_ref, o_ref):
        o_ref[...] = (_sigmoid(ga_ref[...]) * a_ref[...] + _sigmoid(gs_ref[...]) * s_ref[...]).astype(BF16)

    row = pl.BlockSpec((ROWS, D), lambda i: (i, 0))
    return pl.pallas_call(
        body, name=name, out_shape=jax.ShapeDtypeStruct((t, D), BF16), grid=(t // ROWS,),
        in_specs=[pl.BlockSpec((ROWS, D), lambda i: (i, O_GA // D)), pl.BlockSpec((ROWS, D), lambda i: (i, O_GS // D)),
                  row, row],
        out_specs=row, compiler_params=_cp(("parallel",)),
    )(proj, proj, out_a, out_s)


def _merge_bwd(proj, out_a, out_s, dmerged, *, name):
    t = proj.shape[0]
    assert O_GA == 0 and O_GS == D

    def body(ga_ref, gs_ref, a_ref, s_ref, dm_ref, da_ref, ds_ref, dp_ref):
        sa = _sigmoid(ga_ref[...])
        ss = _sigmoid(gs_ref[...])
        dm = dm_ref[...]
        da_ref[...] = (dm * sa).astype(BF16)
        ds_ref[...] = (dm * ss).astype(BF16)
        dp_ref[:, :D] = (dm * a_ref[...] * sa * (1.0 - sa)).astype(BF16)
        dp_ref[:, D:] = (dm * s_ref[...] * ss * (1.0 - ss)).astype(BF16)

    row = pl.BlockSpec((ROWS, D), lambda i: (i, 0))
    o = jax.ShapeDtypeStruct((t, D), BF16)
    return pl.pallas_call(
        body, name=name, out_shape=(o, o, jax.ShapeDtypeStruct((t, NP), BF16)), grid=(t // ROWS,),
        in_specs=[pl.BlockSpec((ROWS, D), lambda i: (i, O_GA // D)), pl.BlockSpec((ROWS, D), lambda i: (i, O_GS // D)),
                  row, row, row],
        out_specs=(row, row, pl.BlockSpec((ROWS, 2 * D), lambda i: (i, 0))), compiler_params=_cp(("parallel",)),
    )(proj, proj, out_a, out_s, dmerged)


def _swiglu_fwd(f, w_gate, w_up, *, name, tn=512, jobs=()):
    t, d = f.shape
    n = w_gate.shape[1]

    def body(f_ref, wg_ref, wu_ref, g_ref, u_ref, a_ref):
        fv = f_ref[...]
        g = _dot(fv, wg_ref[...])
        u = _dot(fv, wu_ref[...])
        g_ref[...] = g.astype(BF16)
        u_ref[...] = u.astype(BF16)
        a_ref[...] = (g * _sigmoid(g) * u).astype(BF16)

    col = pl.BlockSpec((t, tn), lambda j: (0, j))
    wcol = pl.BlockSpec((d, tn), lambda j: (0, j))
    return _call(
        body, jobs=jobs, name=name,
        out_shape=(jax.ShapeDtypeStruct((t, n), BF16), jax.ShapeDtypeStruct((t, n), BF16),
                   jax.ShapeDtypeStruct((t, n), BF16)),
        grid=(n // tn,),
        in_specs=[pl.BlockSpec((t, d), lambda j: (0, 0)), wcol, wcol],
        out_specs=(col, col, col), compiler_params=_cp(("parallel",)),
    )(f, w_gate, w_up)


def _swiglu_bwd(gate, up, dact, *, name, tc=1408, jobs=()):
    t, n = gate.shape

    def body(g_ref, u_ref, da_ref, dg_ref, du_ref):
        g = g_ref[...].astype(F32)
        s = _sigmoid(g)
        da = da_ref[...]
        du_ref[...] = (da * g * s).astype(BF16)
        dg_ref[...] = (da * u_ref[...].astype(F32) * s * (1.0 + g * (1.0 - s))).astype(BF16)

    blk = pl.BlockSpec((ROWS, tc), lambda i, j: (i, j))
    o = jax.ShapeDtypeStruct((t, n), BF16)
    return _call(
        body, jobs=jobs, name=name, out_shape=(o, o), grid=(t // ROWS, n // tc),
        in_specs=[blk, blk, blk], out_specs=(blk, blk), compiler_params=_cp(("parallel", "parallel")),
    )(gate, up, dact)


def _gated_norm_fwd(y_pre, proj, g_ssd, *, name):
    t = y_pre.shape[0]

    def body(y_ref, z_ref, g_ref, o_ref):
        z = z_ref[...]
        v = y_ref[...] * z * _sigmoid(z)
        r = lax.rsqrt(jnp.mean(v * v, axis=-1, keepdims=True) + SSM_EPS)
        o_ref[...] = (v * r * g_ref[...]).astype(BF16)

    row = pl.BlockSpec((ROWS, DI), lambda i: (i, 0))
    return pl.pallas_call(
        body, name=name, out_shape=jax.ShapeDtypeStruct((t, DI), BF16), grid=(t // ROWS,),
        in_specs=[row, pl.BlockSpec((ROWS, DI), lambda i: (i, O_Z // DI)), pl.BlockSpec((1, DI), lambda i: (0, 0))],
        out_specs=row, compiler_params=_cp(("parallel",)),
    )(y_pre, proj, g_ssd)


def _gated_norm_bwd(y_pre, proj, g_ssd, dyn, dproj, *, name, jobs=()):
    t = y_pre.shape[0]

    def body(y_ref, z_ref, g_ref, dyn_ref, _, dy_ref, dz_ref, dg_ref):
        z = z_ref[...]
        s = _sigmoid(z)
        sz = z * s
        yv = y_ref[...]
        v = yv * sz
        r = lax.rsqrt(jnp.mean(v * v, axis=-1, keepdims=True) + SSM_EPS)
        vh = v * r
        dn = dyn_ref[...]
        dvh = dn * g_ref[...]
        dv = r * (dvh - vh * jnp.mean(dvh * vh, axis=-1, keepdims=True))
        dy_ref[...] = dv * sz
        dz_ref[...] = (dv * yv * s * (1.0 + z * (1.0 - s))).astype(BF16)

        @pl.when(pl.program_id(0) == 0)
        def _():
            dg_ref[...] = jnp.zeros_like(dg_ref)

        dg_ref[...] += jnp.broadcast_to(jnp.sum(dn * vh, axis=0, keepdims=True), dg_ref.shape)

    row = pl.BlockSpec((ROWS, DI), lambda i: (i, 0))
    return _call(
        body, jobs=jobs, name=name,
        out_shape=(jax.ShapeDtypeStruct((t, DI), F32), jax.ShapeDtypeStruct(dproj.shape, BF16),
                   jax.ShapeDtypeStruct((8, DI), F32)),
        grid=(t // ROWS,),
        in_specs=[row, pl.BlockSpec((ROWS, DI), lambda i: (i, O_Z // DI)), pl.BlockSpec((1, DI), lambda i: (0, 0)), row, ANY],
        out_specs=(row, pl.BlockSpec((ROWS, DI), lambda i: (i, O_Z // DI)), pl.BlockSpec((8, DI), lambda i: (0, 0))),
        compiler_params=_cp(("arbitrary",)), aliases={4: 1},
    )(y_pre, proj, g_ssd, dyn, dproj)


CONV_TC = 512


def _shift_down(x, s, row):
    if s == 0:
        return x
    return jnp.where(row >= s, pltpu.roll(x, s, 0), 0.0)


def _shift_up(x, s, row, t):
    if s == 0:
        return x
    return jnp.where(row < t - s, pltpu.roll(x, t - s, 0), 0.0)


def _conv_fwd(proj, conv_w, conv_b, *, name):
    t = proj.shape[0]

    def body(x_ref, w_ref, b_ref, o_ref):
        x = x_ref[...]
        row = lax.broadcasted_iota(jnp.int32, x.shape, 0)
        pre = jnp.broadcast_to(b_ref[...], x.shape)
        for k in range(CW):
            pre = pre + w_ref[k:k + 1, :] * _shift_down(x, CW - 1 - k, row)
        o_ref[...] = pre * _sigmoid(pre)

    return pl.pallas_call(
        body, name=name, out_shape=jax.ShapeDtypeStruct((t, CONV), F32), grid=(CONV // CONV_TC,),
        in_specs=[pl.BlockSpec((t, CONV_TC), lambda j: (0, O_XBC // CONV_TC + j)),
                  pl.BlockSpec((CW, CONV_TC), lambda j: (0, j)), pl.BlockSpec((1, CONV_TC), lambda j: (0, j))],
        out_specs=pl.BlockSpec((t, CONV_TC), lambda j: (0, j)), compiler_params=_cp(("parallel",)),
    )(proj, conv_w, conv_b)


def _conv_bwd(proj, conv_w, conv_b, dxs, db, dc, dproj, *, name, jobs=()):
    t = proj.shape[0]
    nx = DI // CONV_TC
    assert NG * NS == CONV_TC

    def body(x_ref, w_ref, b_ref, dxs_ref, db_ref, dc_ref, _, dx_ref, dw_ref, dbias_ref):
        j = pl.program_id(0)
        x = x_ref[...]
        row = lax.broadcasted_iota(jnp.int32, x.shape, 0)
        xs = [_shift_down(x, CW - 1 - k, row) for k in range(CW)]
        pre = jnp.broadcast_to(b_ref[...], x.shape)
        for k in range(CW):
            pre = pre + w_ref[k:k + 1, :] * xs[k]
        s = _sigmoid(pre)
        da = jnp.where(j < nx, dxs_ref[...], jnp.where(j == nx, db_ref[...], dc_ref[...]))
        dpre = da * s * (1.0 + pre * (1.0 - s))
        dx = jnp.zeros_like(x)
        row8 = lax.broadcasted_iota(jnp.int32, dw_ref.shape, 0)
        dw = jnp.zeros(dw_ref.shape, F32)
        for k in range(CW):
            dx = dx + w_ref[k:k + 1, :] * _shift_up(dpre, CW - 1 - k, row, t)
            dw = dw + jnp.where(row8 == k, jnp.sum(dpre * xs[k], axis=0, keepdims=True), 0.0)
        dx_ref[...] = dx.astype(BF16)
        dw_ref[...] = dw
        dbias_ref[...] = jnp.broadcast_to(jnp.sum(dpre, axis=0, keepdims=True), dbias_ref.shape)

    col8 = pl.BlockSpec((8, CONV_TC), lambda j: (0, j))
    xbc = pl.BlockSpec((t, CONV_TC), lambda j: (0, O_XBC // CONV_TC + j))
    whole = pl.BlockSpec((t, CONV_TC), lambda j: (0, 0))
    return _call(
        body, jobs=jobs, name=name,
        out_shape=(jax.ShapeDtypeStruct(dproj.shape, BF16), jax.ShapeDtypeStruct((8, CONV), F32),
                   jax.ShapeDtypeStruct((8, CONV), F32)),
        grid=(CONV // CONV_TC,),
        in_specs=[xbc, pl.BlockSpec((CW, CONV_TC), lambda j: (0, j)), pl.BlockSpec((1, CONV_TC), lambda j: (0, j)),
                  pl.BlockSpec((t, CONV_TC), lambda j: (0, jnp.minimum(j, nx - 1))), whole, whole, ANY],
        out_specs=(xbc, col8, col8),
        compiler_params=_cp(("arbitrary",)), aliases={6: 0},
    )(proj, conv_w, conv_b, dxs, db, dc, dproj)


def _rope_tables(positions, t):
    half = HD // 2
    inv_freq = ROPE_THETA ** (-jnp.arange(half, dtype=F32) * 2.0 / HD)
    ang = positions.reshape(t).astype(F32)[:, None] * inv_freq
    cos, sin = jnp.cos(ang), jnp.sin(ang)
    return jnp.concatenate([cos] * 4, axis=1), jnp.concatenate([-sin, sin] * 2, axis=1)


def _lane_consts():
    lane = lax.broadcasted_iota(jnp.int32, (L, 128), 1)
    return lane, (lane % HD) < (HD // 2), lane < HD


def _rope(tv, cos, sin, lo):
    return tv * cos + jnp.where(lo, pltpu.roll(tv, 128 - HD // 2, 1), pltpu.roll(tv, HD // 2, 1)) * sin


def _rope_t(dv, cos, sin, lo):
    ds = dv * sin
    return dv * cos + jnp.where(lo, pltpu.roll(ds, 128 - HD // 2, 1), pltpu.roll(ds, HD // 2, 1))


def _placed(chunk, g, half0):
    own = jnp.where(half0 if g % 2 == 0 else jnp.logical_not(half0), chunk, 0.0)
    other = pltpu.roll(own, HD, 1)
    return (own, other) if g % 2 == 0 else (other, own)


def _unplace(acc, hf, g, half0):
    v = jnp.where(half0 if hf == 0 else jnp.logical_not(half0), acc, 0.0)
    return v if hf == g % 2 else pltpu.roll(v, HD, 1)


def _attn_fwd(proj, cos, sin, sinks, *, name, jobs=()):
    t = proj.shape[0]
    nb = t // L
    scale = HD ** -0.5

    def body(sink_ref, q_ref, kc_ref, kp_ref, vc_ref, vp_ref, cc_ref, sc_ref, cp_ref, sp_ref, o_ref, lse_ref):
        i = pl.program_id(0)
        lane, lo, half0 = _lane_consts()
        cos_c, sin_c, cos_p, sin_p = cc_ref[...], sc_ref[...], cp_ref[...], sp_ref[...]
        row = lax.broadcasted_iota(jnp.int32, (L, L), 0)
        col = lax.broadcasted_iota(jnp.int32, (L, L), 1)
        m_cur = col <= row
        m_prev = jnp.logical_and(col > row, i > 0)
        kc = [_rope(kc_ref[:, 128 * m:128 * (m + 1)], cos_c, sin_c, lo) for m in range(2)]
        kp = [_rope(kp_ref[:, 128 * m:128 * (m + 1)], cos_p, sin_p, lo) for m in range(2)]
        lse_acc = jnp.zeros((L, 128), F32)
        outs = [jnp.zeros((L, 128), F32) for _ in range(QD // 128)]
        qs = [(_rope(q_ref[:, 128 * ch:128 * (ch + 1)], cos_c, sin_c, lo) * scale).astype(BF16) for ch in range(QD // 128)]
        for g in range(NKV):
            kcv = [v.astype(BF16) for v in _placed(kc[g // 2], g, half0)]
            kpv = [v.astype(BF16) for v in _placed(kp[g // 2], g, half0)]
            vcv = [v.astype(BF16) for v in _placed(vc_ref[:, 128 * (g // 2):128 * (g // 2 + 1)], g, half0)]
            vpv = [v.astype(BF16) for v in _placed(vp_ref[:, 128 * (g // 2):128 * (g // 2 + 1)], g, half0)]
            for r in range(NQH // NKV):
                h = g * (NQH // NKV) + r
                ch, hf = h // 2, h % 2
                s_c = jnp.where(m_cur, _dot_nt(qs[ch], kcv[hf]), NEG)
                s_p = jnp.where(m_prev, _dot_nt(qs[ch], kpv[hf]), NEG)
                sink = sink_ref[0, h]
                mx = jnp.maximum(jnp.maximum(jnp.max(s_c, axis=-1, keepdims=True), jnp.max(s_p, axis=-1, keepdims=True)), sink)
                e_c = jnp.exp(s_c - mx)
                e_p = jnp.exp(s_p - mx)
                den = jnp.sum(e_c, axis=-1, keepdims=True) + jnp.sum(e_p, axis=-1, keepdims=True) + jnp.exp(sink - mx)
                inv = 1.0 / den
                outs[ch] = outs[ch] + _dot((e_c * inv).astype(BF16), vcv[hf]) + _dot((e_p * inv).astype(BF16), vpv[hf])
                lse_acc = jnp.where(lane == h, mx + jnp.log(den), lse_acc)
        for ch in range(QD // 128):
            o_ref[:, 128 * ch:128 * (ch + 1)] = outs[ch].astype(BF16)
        lse_ref[...] = lse_acc

    prev = lambda i: jnp.maximum(i - 1, 0)
    tab_c = pl.BlockSpec((L, 128), lambda i: (i, 0))
    tab_p = pl.BlockSpec((L, 128), lambda i: (prev(i), 0))
    return _call(
        body, jobs=jobs, name=name,
        out_shape=(jax.ShapeDtypeStruct((t, QD), BF16), jax.ShapeDtypeStruct((t, 128), F32)),
        grid=(nb,),
        in_specs=[pl.BlockSpec(memory_space=pltpu.SMEM),
                  pl.BlockSpec((L, QD), lambda i: (i, O_Q // QD)),
                  pl.BlockSpec((L, KVD), lambda i: (i, O_K // KVD)), pl.BlockSpec((L, KVD), lambda i: (prev(i), O_K // KVD)),
                  pl.BlockSpec((L, KVD), lambda i: (i, O_V // KVD)), pl.BlockSpec((L, KVD), lambda i: (prev(i), O_V // KVD)),
                  tab_c, tab_c, tab_p, tab_p],
        out_specs=(pl.BlockSpec((L, QD), lambda i: (i, 0)), pl.BlockSpec((L, 128), lambda i: (i, 0))),
        compiler_params=_cp(("parallel",)),
    )(sinks, proj, proj, proj, proj, proj, cos, sin, cos, sin)


def _attn_bwd(proj, cos, sin, sinks, attn, lse, dattn, dproj, *, name, jobs=()):
    t = proj.shape[0]
    nb = t // L
    scale = HD ** -0.5

    def body(sink_ref, qi_ref, qn_ref, kc_ref, kp_ref, vc_ref, vp_ref, doi_ref, don_ref, oi_ref, on_ref,
             lsei_ref, lsen_ref, cc_ref, sc_ref, cp_ref, sp_ref, cn_ref, sn_ref, _, dqkv_ref, dsk_ref):
        i = pl.program_id(0)
        lane, lo, half0 = _lane_consts()
        half1 = jnp.logical_not(half0)
        cos_c, sin_c = cc_ref[...], sc_ref[...]
        row = lax.broadcasted_iota(jnp.int32, (L, L), 0)
        col = lax.broadcasted_iota(jnp.int32, (L, L), 1)
        m_cur = col <= row
        m_prev = jnp.logical_and(col > row, i > 0)
        m_next = jnp.logical_and(col > row, i < nb - 1)
        kc = [_rope(kc_ref[:, 128 * m:128 * (m + 1)], cos_c, sin_c, lo) for m in range(2)]
        kp = [_rope(kp_ref[:, 128 * m:128 * (m + 1)], cp_ref[...], sp_ref[...], lo) for m in range(2)]
        lse_i, lse_n = lsei_ref[...], lsen_ref[...]
        dk_acc = [jnp.zeros((L, 128), F32) for _ in range(2)]
        dv_acc = [jnp.zeros((L, 128), F32) for _ in range(2)]
        dsk_acc = jnp.zeros((1, 128), F32)
        lane1 = lax.broadcasted_iota(jnp.int32, (1, 128), 1)
        place = lambda chunk, g: [v.astype(BF16) for v in _placed(chunk, g, half0)]
        kcs = [place(kc[g // 2], g) for g in range(NKV)]
        kps = [place(kp[g // 2], g) for g in range(NKV)]
        vcs = [place(vc_ref[:, 128 * (g // 2):128 * (g // 2 + 1)], g) for g in range(NKV)]
        vps = [place(vp_ref[:, 128 * (g // 2):128 * (g // 2 + 1)], g) for g in range(NKV)]
        for ch in range(QD // 128):
            sl = slice(128 * ch, 128 * (ch + 1))
            q_i = (_rope(qi_ref[:, sl], cos_c, sin_c, lo) * scale).astype(BF16)
            q_n = (_rope(qn_ref[:, sl], cn_ref[...], sn_ref[...], lo) * scale).astype(BF16)
            do_i, do_n = doi_ref[:, sl], don_ref[:, sl]
            do_ib, do_nb = do_i.astype(BF16), do_n.astype(BF16)
            od_i = do_i * oi_ref[:, sl].astype(F32)
            od_n = do_n * on_ref[:, sl].astype(F32)
            dq_ch = jnp.zeros((L, 128), F32)
            for hf in range(2):
                h = 2 * ch + hf
                g = h // (NQH // NKV)
                hm = half0 if hf == 0 else half1
                kcv, kpv, vcv, vpv = kcs[g][hf], kps[g][hf], vcs[g][hf], vps[g][hf]
                dl_i = jnp.sum(jnp.where(hm, od_i, 0.0), axis=-1, keepdims=True)
                dl_n = jnp.sum(jnp.where(hm, od_n, 0.0), axis=-1, keepdims=True)
                ls_i = jnp.sum(jnp.where(lane == h, lse_i, 0.0), axis=-1, keepdims=True)
                ls_n = jnp.sum(jnp.where(lane == h, lse_n, 0.0), axis=-1, keepdims=True)
                p_c = jnp.where(m_cur, jnp.exp(_dot_nt(q_i, kcv) - ls_i), 0.0)
                p_p = jnp.where(m_prev, jnp.exp(_dot_nt(q_i, kpv) - ls_i), 0.0)
                ds_c = (p_c * (_dot_nt(do_ib, vcv) - dl_i)).astype(BF16)
                ds_p = (p_p * (_dot_nt(do_ib, vpv) - dl_i)).astype(BF16)
                dq_ch = dq_ch + jnp.where(hm, (_dot(ds_c, kcv) + _dot(ds_p, kpv)) * scale, 0.0)
                sink = sink_ref[0, h]
                dsk = -jnp.sum(jnp.exp(sink - ls_i) * dl_i, axis=0, keepdims=True)
                dsk_acc = dsk_acc + jnp.where(lane1 == h, dsk, 0.0)
                p_n = jnp.where(m_next, jnp.exp(_dot_nt(q_n, kcv) - ls_n), 0.0)
                ds_n = (p_n * (_dot_nt(do_nb, vcv) - dl_n)).astype(BF16)
                dv_h = _dot_tn(p_c.astype(BF16), do_ib) + _dot_tn(p_n.astype(BF16), do_nb)
                dk_h = _dot_tn(ds_c, q_i) + _dot_tn(ds_n, q_n)
                dv_acc[g // 2] = dv_acc[g // 2] + _unplace(dv_h, hf, g, half0)
                dk_acc[g // 2] = dk_acc[g // 2] + _unplace(dk_h, hf, g, half0)
            dqkv_ref[:, sl] = _rope_t(dq_ch, cos_c, sin_c, lo).astype(BF16)
        for m in range(2):
            dqkv_ref[:, QD + 128 * m:QD + 128 * (m + 1)] = _rope_t(dk_acc[m], cos_c, sin_c, lo).astype(BF16)
            dqkv_ref[:, QD + KVD + 128 * m:QD + KVD + 128 * (m + 1)] = dv_acc[m].astype(BF16)

        @pl.when(i == 0)
        def _():
            dsk_ref[...] = jnp.zeros_like(dsk_ref)

        dsk_ref[...] += jnp.broadcast_to(dsk_acc, dsk_ref.shape)

    prev = lambda i: jnp.maximum(i - 1, 0)
    nxt = lambda i: jnp.minimum(i + 1, nb - 1)
    cur_q = pl.BlockSpec((L, QD), lambda i: (i, 0))
    nxt_q = pl.BlockSpec((L, QD), lambda i: (nxt(i), 0))
    tab = lambda f: pl.BlockSpec((L, 128), lambda i: (f(i), 0))
    ident = lambda i: i
    qkv = QD + 2 * KVD
    assert O_K == O_Q + QD and O_V == O_K + KVD and O_Q % qkv == 0
    return _call(
        body, jobs=jobs, name=name,
        out_shape=(jax.ShapeDtypeStruct(dproj.shape, BF16), jax.ShapeDtypeStruct((8, 128), F32)),
        grid=(nb,),
        in_specs=[pl.BlockSpec(memory_space=pltpu.SMEM),
                  pl.BlockSpec((L, QD), lambda i: (i, O_Q // QD)), pl.BlockSpec((L, QD), lambda i: (nxt(i), O_Q // QD)),
                  pl.BlockSpec((L, KVD), lambda i: (i, O_K // KVD)), pl.BlockSpec((L, KVD), lambda i: (prev(i), O_K // KVD)),
                  pl.BlockSpec((L, KVD), lambda i: (i, O_V // KVD)), pl.BlockSpec((L, KVD), lambda i: (prev(i), O_V // KVD)),
                  cur_q, nxt_q, cur_q, nxt_q, tab(ident), tab(nxt),
                  tab(ident), tab(ident), tab(prev), tab(prev), tab(nxt), tab(nxt), ANY],
        out_specs=(pl.BlockSpec((L, qkv), lambda i: (i, O_Q // qkv)), pl.BlockSpec((8, 128), lambda i: (0, 0))),
        compiler_params=_cp(("arbitrary",)), aliases={19: 0},
    )(sinks, proj, proj, proj, proj, proj, proj, dattn, dattn, attn, attn, lse, lse, cos, sin, cos, sin, cos, sin, dproj)


PAIRS = NH // NG // 2


def _softplus(x):
    return jnp.maximum(x, 0.0) + jnp.log(1.0 + jnp.exp(-jnp.abs(x)))


def _ssd_chunk(g, xps, dtr, bm, cm, sps, dtb, alog, dsk):
    lane = lax.broadcasted_iota(jnp.int32, (L, 128), 1)
    lane1 = lax.broadcasted_iota(jnp.int32, (1, 128), 1)
    row = lax.broadcasted_iota(jnp.int32, (L, L), 0)
    col = lax.broadcasted_iota(jnp.int32, (L, L), 1)
    rowc = lax.broadcasted_iota(jnp.int32, (128, 1), 0)
    tril = col <= row
    dt = _softplus(dtr + dtb)
    a = dt * (-jnp.exp(alog))
    a_cs = lax.dot_general(tril.astype(F32), a, (((1,), (0,)), ((), ())), precision=lax.Precision.HIGHEST,
                           preferred_element_type=F32)
    a_cst = a_cs.T
    a_last = jnp.sum(jnp.where(row == L - 1, a_cs, 0.0), axis=0, keepdims=True)
    cb = _bdot_nt(cm, bm)
    ys, snew = [], []
    for q in range(PAIRS):
        xp, sp = xps[q], sps[q]
        y_pair = jnp.zeros((L, 128), F32)
        st_pair = jnp.zeros((128, NS), F32)
        keep = jnp.zeros((128, 1), F32)
        for hh in range(2):
            h = g * 2 * PAIRS + 2 * q + hh
            hm = (lane < HD) if hh == 0 else (lane >= HD)
            rm = (rowc < HD) if hh == 0 else (rowc >= HD)
            dt_h = jnp.sum(jnp.where(lane == h, dt, 0.0), axis=1, keepdims=True)
            acs_h = jnp.sum(jnp.where(lane == h, a_cs, 0.0), axis=1, keepdims=True)
            acst_h = jnp.sum(jnp.where(row == h, a_cst, 0.0), axis=0, keepdims=True)
            al_h = jnp.sum(jnp.where(lane1 == h, a_last, 0.0), axis=1, keepdims=True)
            dsk_h = jnp.sum(jnp.where(lane1 == h, dsk, 0.0), axis=1, keepdims=True)
            decay = jnp.where(tril, jnp.exp(jnp.where(tril, acs_h - acst_h, 0.0)), 0.0)
            xh = jnp.where(hm, xp, 0.0)
            xd = xh * dt_h
            y = _bdot(cb * decay, xd)
            y = y + jnp.where(hm, _bdot_nt(cm * jnp.exp(acs_h), sp), 0.0)
            y_pair = y_pair + y + dsk_h * xh
            st_pair = st_pair + _bdot_tn(xd, bm * jnp.exp(al_h - acs_h))
            keep = keep + jnp.where(rm, jnp.exp(al_h), 0.0)
        ys.append(y_pair)
        snew.append(sp * keep + st_pair)
    return ys, snew


def _ssd_specs(t):
    nc = t // L
    xs = lambda f: pl.BlockSpec((L, 128 * PAIRS), lambda c, g: (f(c), g))
    bspec = lambda f: pl.BlockSpec((L, NS), lambda c, g: (f(c), DI // NS + g))
    cspec = lambda f: pl.BlockSpec((L, NS), lambda c, g: (f(c), DI // NS + NG + g))
    dts = lambda f: pl.BlockSpec((L, 128), lambda c, g: (f(c), O_DT // 128))
    par = pl.BlockSpec((1, 128), lambda c, g: (0, 0))
    st = lambda f: pl.BlockSpec((1, 1, PAIRS, 128, NS), lambda c, g: (f(c), g, 0, 0, 0))
    return nc, xs, bspec, cspec, dts, par, st


def _ssd_fwd(xbc_act, proj, dtb, alog, dsk, *, name, jobs=()):
    t = proj.shape[0]
    nc, xs, bspec, cspec, dts, par, st = _ssd_specs(t)
    ident = lambda c: c

    def body(x_ref, b_ref, c_ref, dt_ref, dtb_ref, al_ref, dsk_ref, y_ref, sin_ref, s_ref):
        c, g = pl.program_id(0), pl.program_id(1)

        @pl.when(c == 0)
        def _():
            s_ref[g] = jnp.zeros((PAIRS, 128, NS), F32)

        sps = [s_ref[g, q] for q in range(PAIRS)]
        for q in range(PAIRS):
            sin_ref[0, 0, q] = sps[q]
        xps = [x_ref[:, 128 * q:128 * (q + 1)] for q in range(PAIRS)]
        ys, snew = _ssd_chunk(g, xps, dt_ref[...], b_ref[...], c_ref[...], sps, dtb_ref[...], al_ref[...], dsk_ref[...])
        for q in range(PAIRS):
            y_ref[:, 128 * q:128 * (q + 1)] = ys[q]
            s_ref[g, q] = snew[q]

    return _call(
        body, jobs=jobs, name=name,
        out_shape=(jax.ShapeDtypeStruct((t, DI), F32), jax.ShapeDtypeStruct((nc, NG, PAIRS, 128, NS), F32)),
        grid=(nc, NG),
        in_specs=[xs(ident), bspec(ident), cspec(ident), dts(ident), par, par, par],
        out_specs=(pl.BlockSpec((L, 128 * PAIRS), lambda c, g: (c, g)), st(ident)),
        scratch_shapes=[pltpu.VMEM((NG, PAIRS, 128, NS), F32)],
        compiler_params=_cp(("arbitrary", "arbitrary")),
    )(xbc_act, xbc_act, xbc_act, proj, dtb, alog, dsk)


def _ssd_bwd(xbc_act, proj, dtb, alog, dsk, states, dy, dproj, *, name, jobs=()):
    t = proj.shape[0]
    nc, xs, bspec, cspec, dts, par, st = _ssd_specs(t)
    rev = lambda c: nc - 1 - c

    def body(x_ref, b_ref, c_ref, dt_ref, dtb_ref, al_ref, dsk_ref, sin_ref, dy_ref, _,
             dx_ref, db_ref, dc_ref, ddtp_ref, ddtb_ref, dal_ref, ddsk_ref, ds_ref, ddt_ref):
        c, g = pl.program_id(0), pl.program_id(1)

        @pl.when(c == 0)
        def _():
            ds_ref[g] = jnp.zeros((PAIRS, 128, NS), F32)

        @pl.when(jnp.logical_and(c == 0, g == 0))
        def _():
            ddtb_ref[...] = jnp.zeros_like(ddtb_ref)
            dal_ref[...] = jnp.zeros_like(dal_ref)
            ddsk_ref[...] = jnp.zeros_like(ddsk_ref)

        @pl.when(g == 0)
        def _():
            ddt_ref[...] = jnp.zeros_like(ddt_ref)

        sps = [sin_ref[0, 0, q] for q in range(PAIRS)]
        xps = [x_ref[:, 128 * q:128 * (q + 1)] for q in range(PAIRS)]
        _, vjp = jax.vjp(functools.partial(_ssd_chunk, g), xps, dt_ref[...], b_ref[...], c_ref[...], sps,
                         dtb_ref[...], al_ref[...], dsk_ref[...])
        dys = [dy_ref[:, 128 * q:128 * (q + 1)] for q in range(PAIRS)]
        dss = [ds_ref[g, q] for q in range(PAIRS)]
        dxps, ddt, db, dc, dsps, ddtb, dal, ddsk = vjp((dys, dss))
        for q in range(PAIRS):
            dx_ref[:, 128 * q:128 * (q + 1)] = dxps[q]
            ds_ref[g, q] = dsps[q]
        db_ref[...] = db
        dc_ref[...] = dc
        ddt_ref[...] += ddt
        ddtb_ref[...] += jnp.broadcast_to(ddtb, ddtb_ref.shape)
        dal_ref[...] += jnp.broadcast_to(dal, dal_ref.shape)
        ddsk_ref[...] += jnp.broadcast_to(ddsk, ddsk_ref.shape)

        @pl.when(g == NG - 1)
        def _():
            ddtp_ref[:, :128] = ddt_ref[...].astype(BF16)
            ddtp_ref[:, 128:] = jnp.zeros((L, DT_PAD - 128), BF16)

    acc = pl.BlockSpec((8, 128), lambda c, g: (0, 0))
    o8 = jax.ShapeDtypeStruct((8, 128), F32)
    return _call(
        body, jobs=jobs, name=name,
        out_shape=(jax.ShapeDtypeStruct((t, DI), F32), jax.ShapeDtypeStruct((t, NG * NS), F32),
                   jax.ShapeDtypeStruct((t, NG * NS), F32), jax.ShapeDtypeStruct(dproj.shape, BF16), o8, o8, o8),
        grid=(nc, NG),
        in_specs=[xs(rev), bspec(rev), cspec(rev), dts(rev), par, par, par, st(rev),
                  pl.BlockSpec((L, 128 * PAIRS), lambda c, g: (rev(c), g)), ANY],
        out_specs=(pl.BlockSpec((L, 128 * PAIRS), lambda c, g: (rev(c), g)),
                   pl.BlockSpec((L, NS), lambda c, g: (rev(c), g)), pl.BlockSpec((L, NS), lambda c, g: (rev(c), g)),
                   pl.BlockSpec((L, DT_PAD), lambda c, g: (rev(c), O_DT // DT_PAD)), acc, acc, acc),
        scratch_shapes=[pltpu.VMEM((NG, PAIRS, 128, NS), F32), pltpu.VMEM((L, 128), F32)],
        compiler_params=_cp(("arbitrary", "arbitrary")), aliases={9: 3},
    )(xbc_act, xbc_act, xbc_act, proj, dtb, alog, dsk, states, dy, dproj)


def _pad_lanes(v, n=128):
    return jnp.pad(v, ((0, 0), (0, n - v.shape[1])))


class _LocalPlan:
    core = 0

    def __init__(self, big):
        self.big, self.grad, self.halves = big, {}, {}

    def w(self, n):
        return self.big[n]

    def g(self, n, a):
        self.grad[n] = a

    def g_half(self, n, which, a):
        self.halves[which] = a
        if len(self.halves) == 2:
            self.grad[n] = jnp.concatenate([self.halves["keep"], self.halves["send"]], axis=0)

    def jobs(self, tag):
        return ()


def _local_step(x, p, positions, target, small, plan):
    t = x.shape[0]
    cos, sin = _rope_tables(positions, t)
    dtb, alog, dsk = _pad_lanes(small["dt_bias"]), _pad_lanes(small["a_log"]), _pad_lanes(small["d_skip"])
    w, jobs = plan.w, plan.jobs

    def mm(a, b, *, name, tn=512, **kw):
        return _matmul(a, b, tm=t, tn=tn, name=name, jobs=jobs(name), **kw)

    tkl = FFN // 4

    def dw(wname, a, dy, *, name, tm):
        plan.g(wname, _matmul(a, dy, ta=True, out_dtype=BF16, tm=tm, tn=512, tk=t, name=name, jobs=jobs(name)))

    u = _rmsnorm_fwd(x, small["g_mix"], name="norm_mix")
    proj = mm(u, w("w_in"), tk=D, name="mm_in")
    attn, lse = _attn_fwd(proj, cos, sin, small["sinks"], name="attn_fwd", jobs=jobs("attn_fwd"))
    out_a = mm(attn, w("w_attn_br"), tk=QD, name="mm_attn_br")
    xbc_act = _conv_fwd(proj, small["conv_w"], small["conv_b"], name="conv_fwd")
    y_pre, states = _ssd_fwd(xbc_act, proj, dtb, alog, dsk, name="ssd_fwd", jobs=jobs("ssd_fwd"))
    yn = _gated_norm_fwd(y_pre, proj, small["g_ssd"], name="gated_norm_fwd")
    out_s = mm(yn, w("w_ssd_br"), tk=DI, name="mm_ssd_br")
    merged = _merge_fwd(proj, out_a, out_s, name="merge_fwd")
    h1 = mm(merged, w("w_o"), add=x, tk=D, name="mm_o")
    f = _rmsnorm_fwd(h1, small["g_ffn"], name="norm_ffn")
    gate, up, act = _swiglu_fwd(f, w("w_gate"), w("w_up"), name="swiglu_fwd", jobs=jobs("swiglu_fwd"))
    h2 = mm(act, w("w_down"), add=h1, tk=tkl, name="mm_down")
    e = _rmsnorm_fwd(h2, small["g_ple"], name="norm_ple")
    pgl = mm(e, w("w_ple_gate"), tk=D, name="mm_ple_gate")
    pb = p.astype(BF16)
    pp = mm(pb, w("w_ple_proj"), tk=PLE, name="mm_ple_proj")
    dh3, dpgl, dpp, loss, dg_final = _final(h2, pgl, pp, target, small["g_final"].reshape(1, D), name="final")

    dw("w_ple_proj", pb, dpp, tm=PLE, name="mm_d_ple_proj")
    dw("w_ple_gate", e, dpgl, tm=D, name="mm_d_ple_gate")
    de = mm(dpgl, w("w_ple_gate"), tb=True, tk=D, name="mm_de")
    dh2, dh2b, dg_ple = _rmsnorm_bwd(h2, small["g_ple"], de, dh3, name="norm_ple_bwd", jobs=jobs("norm_ple_bwd"))
    dw("w_down", act, dh2b, tm=FFN // 2, name="mm_d_down")
    dact = mm(dh2b, w("w_down"), tb=True, tk=D, name="mm_dact")
    dgate, dup = _swiglu_bwd(gate, up, dact, name="swiglu_bwd", jobs=jobs("swiglu_bwd"))
    dw("w_gate", f, dgate, tm=D, name="mm_d_gate")
    dw("w_up", f, dup, tm=D, name="mm_d_up")
    df = mm(dgate, w("w_gate"), tb=True, tn=1024, tk=tkl, name="mm_df_gate")
    df = mm(dup, w("w_up"), tb=True, add=df, tk=tkl, name="mm_df_up")
    dh1, dh1b, dg_ffn = _rmsnorm_bwd(h1, small["g_ffn"], df, dh2, name="norm_ffn_bwd", jobs=jobs("norm_ffn_bwd"))
    dw("w_o", merged, dh1b, tm=D, name="mm_d_o")
    dmerged = mm(dh1b, w("w_o"), tb=True, tk=D, name="mm_dmerged")
    dout_a, dout_s, dproj = _merge_bwd(proj, out_a, out_s, dmerged, name="merge_bwd")
    dw("w_attn_br", attn, dout_a, tm=QD, name="mm_d_attn_br")
    dw("w_ssd_br", yn, dout_s, tm=DI, name="mm_d_ssd_br")
    dattn = mm(dout_a, w("w_attn_br"), tb=True, tk=D, name="mm_dattn")
    dyn = mm(dout_s, w("w_ssd_br"), tb=True, tk=D, name="mm_dyn")
    dproj, dsinks = _attn_bwd(proj, cos, sin, small["sinks"], attn, lse, dattn, dproj, name="attn_bwd",
                              jobs=jobs("attn_bwd"))
    dy_pre, dproj, dg_ssd = _gated_norm_bwd(y_pre, proj, small["g_ssd"], dyn, dproj, name="gated_norm_bwd",
                                            jobs=jobs("gated_norm_bwd"))
    dxs, db, dc, dproj, ddtb, dalog, ddsk = _ssd_bwd(xbc_act, proj, dtb, alog, dsk, states, dy_pre, dproj, name="ssd_bwd",
                                                     jobs=jobs("ssd_bwd"))
    dproj, dconv_w, dconv_b = _conv_bwd(proj, small["conv_w"], small["conv_b"], dxs, db, dc, dproj, name="conv_bwd",
                                        jobs=jobs("conv_bwd"))
    for which, h in (("send", 1 - plan.core), ("keep", plan.core)):
        uh = lax.dynamic_slice_in_dim(u, h * (D // 2), D // 2, axis=1)
        name = "mm_d_in_" + which
        plan.g_half("w_in", which, _matmul(uh, dproj, ta=True, out_dtype=BF16, tm=D // 2, tn=512, tk=t, name=name,
                                           jobs=jobs(name)))
    du = mm(dproj, w("w_in"), tb=True, tn=1024, tk=tkl, name="mm_du")
    grad_x, _, dg_mix = _rmsnorm_bwd(x, small["g_mix"], du, dh1, name="norm_mix_bwd", jobs=jobs("norm_mix_bwd"))

    gs = {
        "g_mix": dg_mix[:1], "conv_w": dconv_w[:CW], "conv_b": dconv_b[:1], "dt_bias": ddtb[:1, :NH],
        "a_log": dalog[:1, :NH], "d_skip": ddsk[:1, :NH], "g_ssd": dg_ssd[:1], "sinks": dsinks[:1, :NQH],
        "g_ffn": dg_ffn[:1], "g_ple": dg_ple[:1], "g_final": dg_final[0],
    }
    return loss, grad_x, gs


def _to_kernel_cols(w):
    seg = lambda o, n: w[:, o:o + n]
    return jnp.concatenate([seg(R_GA, D), seg(R_GS, D), seg(R_Z, DI), seg(R_XBC, CONV), seg(R_Q, QD), seg(R_K, KVD),
                            seg(R_V, KVD), seg(R_DT, NH), jnp.zeros((w.shape[0], DT_PAD - NH), w.dtype)], axis=1)


def _from_kernel_cols(g):
    seg = lambda o, n: g[:, o:o + n]
    return jnp.concatenate([seg(O_Q, QD), seg(O_K, KVD), seg(O_V, KVD), seg(O_Z, DI), seg(O_XBC, CONV), seg(O_DT, NH),
                            seg(O_GA, D), seg(O_GS, D)], axis=1)


def _shard_pieces():
    segs = ((R_Q, QD, O_Q), (R_K, KVD, O_K), (R_V, KVD, O_V), (R_Z, DI, O_Z), (R_XBC, CONV, O_XBC), (R_DT, NH, O_DT),
            (R_GA, D, O_GA), (R_GS, D, O_GS))
    cs = IN_DIM // NCHIP
    out = []
    for j in range(NCHIP):
        for r0, n, k0 in segs:
            lo, hi = max(r0, j * cs), min(r0 + n, (j + 1) * cs)
            if lo < hi:
                out.append((j, lo - j * cs, hi - lo, k0 + lo - r0))
    return out


def _slabs_to_kernel_cols(slabs):
    pieces = sorted(_shard_pieces(), key=lambda p: p[3])
    cols, at = [], 0
    for j, a, n, k0 in pieces:
        if k0 > at:
            cols.append(jnp.zeros((slabs.shape[1], k0 - at), slabs.dtype))
        cols.append(slabs[j, :, a:a + n])
        at = k0 + n
    cols.append(jnp.zeros((slabs.shape[1], NP - at), slabs.dtype))
    return jnp.concatenate(cols, axis=1)


def _kernel_cols_to_slabs(g):
    pieces = _shard_pieces()
    return jnp.stack([jnp.concatenate([g[:, k0:k0 + n] for j, a, n, k0 in pieces if j == s], axis=1)
                      for s in range(NCHIP)])


RELS = ((0, 1), (1, 0), (1, 1))
MATS = {
    n: (n, kind, 1, r, c, tp, tf) for n, kind, r, c, tp, tf in (
        ("w_in", "stk", 2048, 2696, 256, 256),
        ("w_attn_br", "col", 1024, 512, 256, 256),
        ("w_ssd_br", "row", 512, 2048, 512, 256),
        ("w_o", "row", 512, 2048, 512, 256),
        ("w_gate", "col", 2048, 1408, 256, 256),
        ("w_up", "col", 2048, 1408, 256, 256),
        ("w_down", "row", 1408, 2048, 704, 704),
        ("w_ple_gate", "row", 512, 2048, 512, 256),
        ("w_ple_proj", "col", 256, 512, 128, 128),
    )}


def _pos():
    return lax.axis_index("x"), lax.axis_index("y"), lax.axis_index("c")


def _flip(v, a):
    return 1 - v if a else v


def _remote(src, dst, send, recv, dev):
    return pltpu.make_async_remote_copy(src_ref=src, dst_ref=dst, send_sem=send, recv_sem=recv, device_id=dev,
                                        device_id_type=MESH)


def _whole_shape(kind, g, r, c):
    return {"row": (g, NCHIP * r, c), "col": (g, r, NCHIP * c), "stk": (NCHIP, r, c)}[kind]


def _cols(j, c):
    return pl.ds(pl.multiple_of(j * c, 128), c)


def _whole_shard(kind, ref, j, r, c):
    if kind == "row":
        return ref.at[:, pl.ds(j * r, r), :]
    if kind == "col":
        return ref.at[:, :, _cols(j, c)]
    return ref.at[pl.ds(j, 1)]


def _whole_rows(kind, ref, j, row, n, r, c):
    if kind == "row":
        return ref.at[:, pl.ds(j * r + row, n), :]
    if kind == "col":
        return ref.at[:, pl.ds(row, n), _cols(j, c)]
    return ref.at[pl.ds(j, 1), pl.ds(row, n), :]


class _GatherJob(_Job):
    has_mid = True
    NCP = 10

    def __init__(self, names, shards, sink):
        self.mats = [MATS[n] for n in names]
        self.srcs = [shards[n] for n in names]
        self.news = [jax.ShapeDtypeStruct(_whole_shape(kind, g, r, c), BF16) for _, kind, g, r, c, _, _ in self.mats]
        n = len(names)
        self.scratch = [pltpu.SemaphoreType.DMA((self.NCP * n,)), pltpu.SemaphoreType.DMA((self.NCP * n,))]
        self.names, self.sink = names, sink

    def _copies(self, srcs, news, sems):
        send, recv = sems
        x, y, c = _pos()
        me, jx, jy, jd = 2 * x + y, 2 * (1 - x) + y, 2 * x + (1 - y), 2 * (1 - x) + (1 - y)
        nbx, nby, sib = (1 - x, y, c), (x, 1 - y, c), (x, y, 1 - c)
        cps = []
        for w, (_, kind, g, r, cc, _, _) in enumerate(self.mats):
            hr, qr = r // 2, r // 4
            at = lambda j, h, q, n: _whole_rows(kind, news[w], j, h * hr + q * qr, n, r, cc)
            mine = lambda q: srcs[w].at[:, pl.ds(c * hr + q * qr, qr), :]
            cp = lambda k, s, d, dev: _remote(s, d, send.at[self.NCP * w + k], recv.at[self.NCP * w + k], dev)
            cps.append([
                cp(0, mine(0), at(me, c, 0, qr), nbx), cp(1, mine(1), at(me, c, 1, qr), nbx),
                cp(2, mine(1), at(me, c, 1, qr), nby), cp(3, mine(0), at(me, c, 0, qr), nby),
                cp(4, at(jx, c, 0, qr), at(jx, c, 0, qr), nby), cp(5, at(jy, c, 1, qr), at(jy, c, 1, qr), nbx),
                cp(6, at(jx, c, 0, hr), at(jx, c, 0, hr), sib), cp(7, at(jy, c, 0, hr), at(jy, c, 0, hr), sib),
                cp(8, at(jd, c, 0, hr), at(jd, c, 0, hr), sib),
                cp(9, srcs[w], _whole_shard(kind, news[w], me, r, cc), sib)])
        return cps

    def start(self, srcs, dsts, news, sems):
        cps = self._copies(srcs, news, sems)
        for w in range(len(self.mats)):
            for k in (0, 1, 2, 3, 9):
                cps[w][k].start()

    def mid(self, srcs, dsts, news, sems):
        cps = self._copies(srcs, news, sems)
        for w in range(len(self.mats)):
            cps[w][0].wait_recv()
            cps[w][4].start()
            cps[w][2].wait_recv()
            cps[w][5].start()

    def finish(self, srcs, dsts, news, sems):
        cps = self._copies(srcs, news, sems)
        for w in range(len(self.mats)):
            cps[w][1].wait_recv()
            cps[w][6].start()
            cps[w][3].wait_recv()
            cps[w][7].start()
        for w in range(len(self.mats)):
            cps[w][4].wait_recv()
            cps[w][5].wait_recv()
            cps[w][8].start()
        for w in range(len(self.mats)):
            for k in (6, 7, 8, 9):
                cps[w][k].wait_recv()
            for k in range(self.NCP):
                cps[w][k].wait_send()

    def done(self, dsts, news):
        for n, a in zip(self.names, news):
            self.sink[n] = a


class _SwapJob(_Job):
    def __init__(self, build, ncopies, *, srcs=(), dsts=(), news=(), done=None):
        self.build, self.srcs, self.dsts, self.news, self._done = build, list(srcs), list(dsts), list(news), done
        self.scratch = [pltpu.SemaphoreType.DMA((ncopies,)), pltpu.SemaphoreType.DMA((ncopies,))]

    def start(self, srcs, dsts, news, sems):
        for cp in self.build(srcs, dsts, news, *sems):
            cp.start()

    def finish(self, srcs, dsts, news, sems):
        for cp in self.build(srcs, dsts, news, *sems):
            cp.wait()

    def done(self, dsts, news):
        if self._done is not None:
            self._done(dsts, news)


def _half_of_whole(kind, ref, h, r, c):
    if kind == "row":
        return ref.at[:, :, pl.ds(pl.multiple_of(h * (c // 2), 128), c // 2)]
    return ref.at[:, pl.ds(h * (r // 2), r // 2), :]


def _half_shape(kind, g, r, c):
    return {"row": (g, NCHIP * r, c // 2), "col": (g, r // 2, NCHIP * c), "stk": (NCHIP, r // 2, c)}[kind]


def _piece_shape(kind, g, r, c):
    return {"row": (g, r, c // 2), "col": (g, r // 2, c), "stk": (1, r // 2, c)}[kind]


def _piece_of_half(kind, ref, j, r, c):
    if kind == "row":
        return ref.at[:, pl.ds(j * r, r), :]
    if kind == "col":
        return ref.at[:, :, _cols(j, c)]
    return ref.at[pl.ds(j, 1)]


def _half_of_shard(kind, ref, h, r, c):
    if kind == "row":
        return ref.at[:, :, pl.ds(pl.multiple_of(h * (c // 2), 128), c // 2)]
    return ref.at[:, pl.ds(h * (r // 2), r // 2), :]


def _pair_sum(pack, core, mine, got, whole=True):
    name, kind, g, r, c, tr, _ = pack
    hs = _half_shape(kind, g, r, c)
    nb = hs[1] // tr

    def body(core_ref, a_ref, b_ref, o_ref):
        o_ref[...] = (a_ref[...].astype(F32) + b_ref[...].astype(F32)).astype(BF16)

    blk = (1, tr, hs[2])
    same = lambda gi, i, core_ref: (gi, i, 0)
    if not whole:
        a_map = same
    elif kind == "row":
        a_map = lambda gi, i, core_ref: (gi, i, core_ref[0])
    else:
        a_map = lambda gi, i, core_ref: (gi, core_ref[0] * nb + i, 0)
    return pl.pallas_call(
        body, name="pair_sum_" + name, out_shape=jax.ShapeDtypeStruct(hs, BF16),
        grid_spec=pltpu.PrefetchScalarGridSpec(
            num_scalar_prefetch=1, grid=(hs[0], nb),
            in_specs=[pl.BlockSpec(blk, a_map), pl.BlockSpec(blk, same)], out_specs=pl.BlockSpec(blk, same)),
        compiler_params=_cp(("parallel", "parallel")),
    )(core, mine, got)


def _shard_sum(pack, where, half, got):
    name, kind, g, r, c, _, tr = pack
    ps = _piece_shape(kind, g, r, c)
    nb = ps[1] // tr

    def body(where_ref, a_ref, b_ref, o_ref):
        o_ref[...] = a_ref[...].astype(F32) + ((b_ref[0].astype(F32) + b_ref[1].astype(F32)) + b_ref[2].astype(F32))

    blk = (1, tr, ps[2])
    if kind == "row":
        a_map = lambda gi, i, wr: (gi, wr[0] * nb + i, 0)
        o_map = lambda gi, i, wr: (gi, i, wr[1])
    elif kind == "col":
        a_map = lambda gi, i, wr: (gi, i, wr[0])
        o_map = lambda gi, i, wr: (gi, wr[1] * nb + i, 0)
    else:
        a_map = lambda gi, i, wr: (wr[0], i, 0)
        o_map = lambda gi, i, wr: (gi, wr[1] * nb + i, 0)
    return pl.pallas_call(
        body, name="shard_sum_" + name, out_shape=jax.ShapeDtypeStruct((g, r, c), F32),
        grid_spec=pltpu.PrefetchScalarGridSpec(
            num_scalar_prefetch=1, grid=(ps[0], nb),
            in_specs=[pl.BlockSpec(blk, a_map), pl.BlockSpec((3,) + blk, lambda gi, i, wr: (0, gi, i, 0))],
            out_specs=pl.BlockSpec(blk, o_map)),
        compiler_params=_cp(("parallel", "parallel")),
    )(where, half, got)


class _Plan:
    def __init__(self, shards, table):
        self.shards, self.table = shards, table
        self.whole, self.grad, self.got_a, self.half, self.got_b, self.sent_b, self.gshard = {}, {}, {}, {}, {}, {}, {}
        x, y, c = _pos()
        self.core = c
        self.core1 = c.reshape(1).astype(jnp.int32)
        self.where = jnp.stack([2 * x + y, c]).astype(jnp.int32)
        self._w_in = None
        self.send, self.keep = {}, {}

    def w(self, n):
        if n != "w_in":
            return self.whole[n][0]
        if self._w_in is None:
            self._w_in = _slabs_to_kernel_cols(self.whole[n])
        return self._w_in

    def g(self, n, a):
        self.grad[n] = a[None]

    def g_half(self, n, which, a):
        (self.send if which == "send" else self.keep)[n] = _kernel_cols_to_slabs(a)

    def jobs(self, tag):
        out = []
        for spec in self.table.get(tag, ()):
            out += getattr(self, "_" + spec[0])(*spec[1:])
        return out

    def run(self, name, jobs):
        if jobs:
            _call(lambda: None, jobs=jobs, name=name, out_shape=[], in_specs=[], out_specs=[])()

    def _gather(self, names):
        return [_GatherJob(names, self.shards, self.whole)]

    def _rs_a(self, names):
        mats = [MATS[n] for n in names]

        def build(srcs, dsts, news, send, recv):
            x, y, c = _pos()
            return [_remote(srcs[i] if names[i] in self.send else _half_of_whole(kind, srcs[i], 1 - c, r, cc), news[i],
                            send.at[i], recv.at[i], (x, y, 1 - c))
                    for i, (_, kind, g, r, cc, _, _) in enumerate(mats)]

        def done(dsts, news):
            self.got_a.update(zip(names, news))

        return [_SwapJob(build, len(names), srcs=[self.send.get(n, self.grad.get(n)) for n in names], done=done,
                         news=[jax.ShapeDtypeStruct(_half_shape(kind, g, r, c), BF16) for _, kind, g, r, c, _, _ in mats])]

    def _rs_b(self, names, ks=(0, 1, 2)):
        return [self._rs_b_one(n, ks) for n in names]

    def _rs_b_one(self, n, ks):
        _, kind, g, r, cc, _, _ = MATS[n]
        if n not in self.half:
            if n in self.keep:
                self.half[n] = _pair_sum(MATS[n], self.core1, self.keep[n], self.got_a[n], whole=False)
            else:
                self.half[n] = _pair_sum(MATS[n], self.core1, self.grad[n], self.got_a[n])

        def build(srcs, dsts, news, send, recv):
            x, y, c = _pos()
            land = (dsts or news)[0]
            cps = []
            for i, k in enumerate(ks):
                px, py = _flip(x, RELS[k][0]), _flip(y, RELS[k][1])
                cps.append(_remote(_piece_of_half(kind, srcs[0], 2 * px + py, r, cc), land.at[k], send.at[i], recv.at[i],
                                   (px, py, c)))
            return cps

        def done(dsts, news):
            self.got_b[n] = (dsts or news)[0]
            self.sent_b[n] = self.sent_b.get(n, ()) + tuple(ks)

        if n in self.got_b:
            return _SwapJob(build, len(ks), srcs=[self.half[n]], dsts=[self.got_b[n]], done=done)
        shape = jax.ShapeDtypeStruct((3,) + _piece_shape(kind, g, r, cc), BF16)
        return _SwapJob(build, len(ks), srcs=[self.half[n]], news=[shape], done=done)

    def _rs_c(self, names):
        mats = [MATS[n] for n in names]
        for n in names:
            assert sorted(self.sent_b[n]) == [0, 1, 2], (n, self.sent_b[n])
        parts = [_shard_sum(MATS[n], self.where, self.half[n], self.got_b[n]) for n in names]

        def build(srcs, dsts, news, send, recv):
            x, y, c = _pos()
            cps = []
            for i, (_, kind, g, r, cc, _, _) in enumerate(mats):
                mine = _half_of_shard(kind, dsts[i], c, r, cc)
                cps.append(_remote(mine, mine, send.at[i], recv.at[i], (x, y, 1 - c)))
            return cps

        def done(dsts, news):
            self.gshard.update(zip(names, dsts))

        return [_SwapJob(build, len(names), dsts=parts, done=done)]

    def finish(self, n):
        if n not in self.got_a:
            self.run("rs_a_" + n, self._rs_a((n,)))
        left = tuple(k for k in range(3) if k not in self.sent_b.get(n, ()))
        if left:
            self.run("rs_b_" + n, self._rs_b((n,), left))
        if n not in self.gshard:
            self.run("rs_c_" + n, self._rs_c((n,)))
        return self.gshard[n]


TABLE = {
    "gather_w_in": (("gather", ("w_in",)),),
    "mm_in": (("gather", ("w_gate",)),),
    "attn_fwd": (("gather", ("w_attn_br", "w_ssd_br")),),
    "ssd_fwd": (("gather", ("w_up",)),),
    "mm_ssd_br": (("gather", ("w_o",)),),
    "swiglu_fwd": (("gather", ("w_down",)),),
    "mm_down": (("gather", ("w_ple_gate", "w_ple_proj")),),
    "mm_de": (("rs_a", ("w_ple_proj", "w_ple_gate")),),
    "mm_d_down": (("rs_b", ("w_ple_proj", "w_ple_gate")),),
    "mm_dact": (("rs_a", ("w_down",)),),
    "swiglu_bwd": (("rs_c", ("w_ple_proj", "w_ple_gate")),),
    "mm_df_gate": (("rs_a", ("w_gate", "w_up")),),
    "mm_dmerged": (("rs_a", ("w_o",)),),
    "mm_dyn": (("rs_a", ("w_attn_br", "w_ssd_br")),),
    "attn_bwd": (("rs_b", ("w_down",)),),
    "gated_norm_bwd": (("rs_c", ("w_down",)),),
    "ssd_bwd": (("rs_b", ("w_gate", "w_up")),),
    "conv_bwd": (("rs_b", ("w_o",)),),
    "mm_d_in_send": (("rs_b", ("w_attn_br", "w_ssd_br")), ("rs_c", ("w_gate", "w_up"))),
    "mm_d_in_keep": (("rs_a", ("w_in",)), ("rs_c", ("w_o",))),
    "mm_du": (("rs_b", ("w_in",)),),
    "norm_mix_bwd": (("rs_c", ("w_attn_br", "w_ssd_br")),),
}


NDEV = 8


def _allreduce_small(v, *, name):
    rows = v.shape[0]

    def body(v_ref, o_ref, slots, send, recv):
        x, y, c = _pos()
        me = 4 * x + 2 * y + c
        slots[me] = v_ref[...]
        cps = []
        for k in range(1, NDEV):
            peer = (_flip(x, k & 4), _flip(y, k & 2), _flip(c, k & 1))
            cp = _remote(v_ref, slots.at[me], send.at[k - 1], recv.at[k - 1], peer)
            cp.start()
            cps.append(cp)
        for cp in cps:
            cp.wait()
        acc = slots[0]
        for s in range(1, NDEV):
            acc = acc + slots[s]
        o_ref[...] = acc

    return pl.pallas_call(
        body, name=name, out_shape=jax.ShapeDtypeStruct((rows, 128), F32),
        in_specs=[pl.BlockSpec(memory_space=pltpu.VMEM)], out_specs=pl.BlockSpec(memory_space=pltpu.VMEM),
        scratch_shapes=[pltpu.VMEM((NDEV, rows, 128), F32), pltpu.SemaphoreType.DMA((NDEV - 1,)),
                        pltpu.SemaphoreType.DMA((NDEV - 1,))],
    )(v)


def _adamw(w, g, m, v, *, name, tr=None, tc=None, jobs=()):
    r, c = w.shape
    tr = r if tr is None else tr
    c1 = 1.0 / (1.0 - B1 ** STEP)
    c2 = 1.0 / (1.0 - B2 ** STEP)

    def body(w_ref, g_ref, m_ref, v_ref, d_ref, mo_ref, vo_ref):
        gv = g_ref[...]
        mn = B1 * m_ref[...] + (1.0 - B1) * gv
        vn = B2 * v_ref[...] + (1.0 - B2) * (gv * gv)
        mo_ref[...] = mn
        vo_ref[...] = vn
        d_ref[...] = -LR * ((mn * c1) / (jnp.sqrt(vn * c2) + AEPS) + WD * w_ref[...])

    if tc is None:
        blk, grid = pl.BlockSpec((tr, c), lambda i: (i, 0)), (r // tr,)
    else:
        blk, grid = pl.BlockSpec((r, tc), lambda i: (0, i)), (c // tc,)
    o = jax.ShapeDtypeStruct((r, c), F32)
    return _call(
        body, jobs=jobs, name=name, out_shape=(o, o, o), grid=grid, in_specs=[blk] * 4, out_specs=(blk, blk, blk),
        compiler_params=_cp(("parallel",)),
    )(w, g, m, v)


WEIGHTS = ("g_mix", "w_in", "conv_w", "conv_b", "dt_bias", "a_log", "d_skip", "g_ssd", "sinks", "w_attn_br", "w_ssd_br",
           "w_o", "g_ffn", "w_gate", "w_up", "w_down", "g_ple", "w_ple_gate", "w_ple_proj", "g_final")
BIG = {
    "w_gate": 256, "w_up": 256, "w_down": 128, "w_ssd_br": 128, "w_o": 128, "w_ple_gate": 128, "w_attn_br": 256,
    "w_ple_proj": 256, "w_in": None,
}
SMALL = tuple(n for n in WEIGHTS if n not in BIG)


def _pack_small(parts):
    rows = []
    for a in parts:
        a = a.reshape(-1)
        rows.append(jnp.pad(a, (0, -a.shape[0] % 128)).reshape(-1, 128))
    out = jnp.concatenate(rows, axis=0)
    return jnp.pad(out, ((0, -out.shape[0] % 8), (0, 0)))


def _unpack_small(packed, shapes):
    out, r = [], 0
    for s in shapes:
        n = int(np.prod(s))
        nr = -(-n // 128)
        out.append(packed[r:r + nr].reshape(-1)[:n].reshape(s))
        r += nr
    return out


def kernel(x, p, positions, g_mix, w_in, conv_w, conv_b, dt_bias, a_log, d_skip, g_ssd, sinks, w_attn_br, w_ssd_br, w_o, g_ffn, w_gate, w_up, w_down, g_ple, w_ple_gate, w_ple_proj, g_final, loss_target, m_g_mix, m_w_in, m_conv_w, m_conv_b, m_dt_bias, m_a_log, m_d_skip, m_g_ssd, m_sinks, m_w_attn_br, m_w_ssd_br, m_w_o, m_g_ffn, m_w_gate, m_w_up, m_w_down, m_g_ple, m_w_ple_gate, m_w_ple_proj, m_g_final, v_g_mix, v_w_in, v_conv_w, v_conv_b, v_dt_bias, v_a_log, v_d_skip, v_g_ssd, v_sinks, v_w_attn_br, v_w_ssd_br, v_w_o, v_g_ffn, v_w_gate, v_w_up, v_w_down, v_g_ple, v_w_ple_gate, v_w_ple_proj, v_g_final):
    w = dict(zip(WEIGHTS, (g_mix, w_in, conv_w, conv_b, dt_bias, a_log, d_skip, g_ssd, sinks, w_attn_br, w_ssd_br, w_o,
                           g_ffn, w_gate, w_up, w_down, g_ple, w_ple_gate, w_ple_proj, g_final)))
    m = dict(zip(WEIGHTS, (m_g_mix, m_w_in, m_conv_w, m_conv_b, m_dt_bias, m_a_log, m_d_skip, m_g_ssd, m_sinks, m_w_attn_br,
                           m_w_ssd_br, m_w_o, m_g_ffn, m_w_gate, m_w_up, m_w_down, m_g_ple, m_w_ple_gate, m_w_ple_proj,
                           m_g_final)))
    v = dict(zip(WEIGHTS, (v_g_mix, v_w_in, v_conv_w, v_conv_b, v_dt_bias, v_a_log, v_d_skip, v_g_ssd, v_sinks, v_w_attn_br,
                           v_w_ssd_br, v_w_o, v_g_ffn, v_w_gate, v_w_up, v_w_down, v_g_ple, v_w_ple_gate, v_w_ple_proj,
                           v_g_final)))
    xi, yi, ci = _pos()
    chip = 2 * xi + yi
    t = x.shape[1]
    cshard = CONV // NCHIP

    plan = _Plan({n: w[n].astype(BF16) for n in MATS}, TABLE)
    plan.run("gather_w_in", plan.jobs("gather_w_in"))
    placed = lax.dynamic_update_slice(jnp.zeros((CW, CONV), F32), w["conv_w"][0], (0, chip * cshard))
    conv_whole = _allreduce_small(jnp.where(ci == 0, placed, 0.0).reshape(-1, 128), name="gather_conv_w").reshape(CW, CONV)

    small = {n: w[n] for n in ("g_mix", "conv_b", "dt_bias", "a_log", "d_skip", "g_ssd", "sinks", "g_ffn", "g_ple", "g_final")}
    small["conv_w"] = conv_whole
    loss8, grad_x, gs = _local_step(x[0], p[0, 0], positions, loss_target[0], small, plan)

    order = ("g_mix", "conv_b", "dt_bias", "a_log", "d_skip", "g_ssd", "sinks", "g_ffn", "g_ple", "g_final", "conv_w")
    summed = _allreduce_small(_pack_small([loss8[0, :1]] + [gs[n] for n in order]), name="sum_small")
    parts = _unpack_small(summed, [(1,)] + [w[n].shape for n in order[:-1]] + [(CW, CONV)])
    loss = parts[0][0]
    grad = dict(zip(order, parts[1:]))
    grad["conv_w"] = lax.dynamic_slice(grad["conv_w"], (0, chip * cshard), (CW, cshard))[None]

    delta, new_m, new_v = {}, {}, {}
    for n, tr in BIG.items():
        grad[n] = plan.finish(n)
        if n == "w_in":
            d_, m_, v_ = _adamw(w[n][0].T, grad[n][0].T, m[n][0].T, v[n][0].T, tc=128, name="adamw_" + n)
            d_, m_, v_ = d_.T, m_.T, v_.T
        else:
            d_, m_, v_ = _adamw(w[n][0], grad[n][0], m[n][0], v[n][0], tr=tr, name="adamw_" + n)
        delta[n], new_m[n], new_v[n] = d_[None], m_[None], v_[None]
    shapes = [w[n].shape for n in SMALL]
    d_, m_, v_ = _adamw(_pack_small([w[n] for n in SMALL]), _pack_small([grad[n] for n in SMALL]),
                        _pack_small([m[n] for n in SMALL]), _pack_small([v[n] for n in SMALL]), tr=None, name="adamw_small")
    for n, a, b, c_ in zip(SMALL, _unpack_small(d_, shapes), _unpack_small(m_, shapes), _unpack_small(v_, shapes)):
        delta[n], new_m[n], new_v[n] = a, b, c_

    return (loss, grad_x[None], *[grad[n] for n in WEIGHTS], *[delta[n] for n in WEIGHTS],
            *[new_m[n] for n in WEIGHTS], *[new_v[n] for n in WEIGHTS])
```

```python
import functools

import jax
import jax.numpy as jnp
import numpy as np
from jax import lax
from jax.experimental import pallas as pl
from jax.experimental.pallas import tpu as pltpu

F32 = jnp.float32
BF16 = jnp.bfloat16
MESH = pl.DeviceIdType.MESH

D = 2048
HD = 64
NQH = 16
NKV = 4
QD = NQH * HD
KVD = NKV * HD
DI = 2048
NH = 32
NG = 4
NS = 128
CW = 4
L = 128
CONV = DI + 2 * NG * NS
FFN = 5632
PLE = 256
IN_DIM = QD + 2 * KVD + DI + CONV + NH + 2 * D
EPS = 1e-6
SSM_EPS = 1e-5
ROPE_THETA = 10000.0
LR, B1, B2, AEPS, WD, STEP = 0.001, 0.9, 0.999, 1e-08, 0.01, 10

O_GA, O_GS, O_Z, O_XBC, O_Q, O_K, O_V, O_DT = 0, 2048, 4096, 6144, 9216, 10240, 10496, 10752
DT_PAD = 512
NP = O_DT + DT_PAD
R_Q, R_K, R_V, R_Z, R_XBC, R_DT, R_GA, R_GS = 0, 1024, 1280, 1536, 3584, 6656, 6688, 8736

NCHIP = 4
VMEM_LIMIT = 52 * 1024 * 1024
NEG = -1e30


def _cp(sem=None):
    return pltpu.CompilerParams(dimension_semantics=sem, vmem_limit_bytes=VMEM_LIMIT)


def _dot(a, b):
    return lax.dot_general(a, b, (((1,), (0,)), ((), ())), preferred_element_type=F32)


def _dot_nt(a, b):
    return lax.dot_general(a, b, (((1,), (1,)), ((), ())), preferred_element_type=F32)


def _dot_tn(a, b):
    return lax.dot_general(a, b, (((0,), (0,)), ((), ())), preferred_element_type=F32)


def _sigmoid(x):
    return 1.0 / (1.0 + jnp.exp(-x))


def _bf16_dot(dot, da, db):
    @jax.custom_vjp
    def f(a, b):
        return dot(a.astype(BF16), b.astype(BF16))

    def fwd(a, b):
        return f(a, b), (a.astype(BF16), b.astype(BF16))

    def bwd(res, g):
        a, b = res
        g = g.astype(BF16)
        return da(g, a, b), db(g, a, b)

    f.defvjp(fwd, bwd)
    return f


_bdot = _bf16_dot(_dot, lambda g, a, b: _dot_nt(g, b), lambda g, a, b: _dot_tn(a, g))
_bdot_nt = _bf16_dot(_dot_nt, lambda g, a, b: _dot(g, b), lambda g, a, b: _dot_tn(g, a))
_bdot_tn = _bf16_dot(_dot_tn, lambda g, a, b: _dot_nt(b, g), lambda g, a, b: _dot(a, g))


ANY = pl.BlockSpec(memory_space=pl.ANY)


class _Job:
    srcs, dsts, news, scratch = (), (), (), ()
    has_mid = False

    def start(self, srcs, dsts, news, sems):
        raise NotImplementedError

    def mid(self, srcs, dsts, news, sems):
        pass

    def finish(self, srcs, dsts, news, sems):
        raise NotImplementedError

    def done(self, dsts, news):
        pass


def _call(body, *, jobs=(), name, out_shape, in_specs, out_specs, grid=(), scratch_shapes=(), compiler_params=None,
          aliases=None):
    jobs = [j for j in jobs if j is not None]
    aliases = dict(aliases or {})
    if not jobs:
        return pl.pallas_call(body, name=name, out_shape=out_shape, in_specs=in_specs, out_specs=out_specs, grid=grid,
                              scratch_shapes=scratch_shapes, compiler_params=compiler_params,
                              input_output_aliases=aliases)
    single = not isinstance(out_shape, (tuple, list))
    outs = [out_shape] if single else list(out_shape)
    ospecs = [out_specs] if single else list(out_specs)
    n_in, n_out, n_scr = len(in_specs), len(outs), len(scratch_shapes)
    srcs = [a for j in jobs for a in j.srcs]
    dsts = [a for j in jobs for a in j.dsts]
    news = [a for j in jobs for a in j.news]
    sems = [a for j in jobs for a in j.scratch]

    def wrapped(*refs):
        pos = n_in + len(srcs) + len(dsts)
        ins, jsrc = refs[:n_in], refs[n_in:n_in + len(srcs)]
        o_refs = refs[pos:pos + n_out]
        pos += n_out
        jdst, jnew = refs[pos:pos + len(dsts)], refs[pos + len(dsts):pos + len(dsts) + len(news)]
        pos += len(dsts) + len(news)
        scr, jsem = refs[pos:pos + n_scr], refs[pos + n_scr:]

        def run(which):
            a = b = c = d = 0
            for j in jobs:
                getattr(j, which)(jsrc[a:a + len(j.srcs)], jdst[b:b + len(j.dsts)], jnew[c:c + len(j.news)],
                                  jsem[d:d + len(j.scratch)])
                a, b, c, d = a + len(j.srcs), b + len(j.dsts), c + len(j.news), d + len(j.scratch)

        if not grid:
            run("start")
            run("mid")
            body(*ins, *o_refs, *scr)
            run("finish")
            return
        step = functools.reduce(lambda acc, a: acc * grid[a] + pl.program_id(a), range(len(grid)), 0)
        steps = int(np.prod(grid))
        pl.when(step == 0)(lambda: run("start"))
        if any(j.has_mid for j in jobs):
            pl.when(step == steps // 3)(lambda: run("mid"))
        body(*ins, *o_refs, *scr)
        pl.when(step == steps - 1)(lambda: run("finish"))

    call = pl.pallas_call(
        wrapped, name=name,
        out_shape=outs + [jax.ShapeDtypeStruct(a.shape, a.dtype) for a in dsts] + news,
        in_specs=list(in_specs) + [ANY] * (len(srcs) + len(dsts)),
        out_specs=ospecs + [ANY] * (len(dsts) + len(news)),
        grid=grid, scratch_shapes=list(scratch_shapes) + sems,
        input_output_aliases={**aliases, **{n_in + len(srcs) + i: n_out + i for i in range(len(dsts))}},
        compiler_params=_cp(("arbitrary",) * len(grid) if grid else None))

    def run_call(*args):
        res = call(*args, *srcs, *dsts)
        b, c = n_out, n_out + len(dsts)
        for j in jobs:
            j.done(res[b:b + len(j.dsts)], res[c:c + len(j.news)])
            b, c = b + len(j.dsts), c + len(j.news)
        return res[0] if single else tuple(res[:n_out])

    return run_call


def _matmul(a, b, *, ta=False, tb=False, out_dtype=F32, add=None, tm, tn, tk, name, jobs=()):
    k, m = a.shape if ta else a.shape[::-1]
    n = b.shape[0] if tb else b.shape[1]
    assert (b.shape[1] if tb else b.shape[0]) == k and not (ta and tb)
    assert m % tm == 0 and n % tn == 0 and k % tk == 0, (name, a.shape, b.shape)
    nk = k // tk
    has_add = add is not None

    def body(*refs):
        a_ref, b_ref = refs[0], refs[1]
        add_ref = refs[2] if has_add else None
        o_ref = refs[3] if has_add else refs[2]
        av = a_ref[...].astype(BF16)
        bv = b_ref[...].astype(BF16)
        part = _dot_tn(av, bv) if ta else _dot_nt(av, bv) if tb else _dot(av, bv)

        def finish(r):
            if has_add:
                r = r + add_ref[...]
            o_ref[...] = r.astype(out_dtype)

        if nk == 1:
            finish(part)
        elif out_dtype == F32:
            kk = pl.program_id(2)
            pl.when(kk == 0)(lambda: finish(part))

            @pl.when(kk > 0)
            def _():
                o_ref[...] += part
        else:
            acc_ref = refs[-1]
            kk = pl.program_id(2)

            @pl.when(kk == 0)
            def _():
                acc_ref[...] = part

            @pl.when(kk > 0)
            def _():
                acc_ref[...] += part

            @pl.when(kk == nk - 1)
            def _():
                finish(acc_ref[...])

    in_specs = [pl.BlockSpec((tk, tm), lambda i, j, kk: (kk, i)) if ta else pl.BlockSpec((tm, tk), lambda i, j, kk: (i, kk)),
                pl.BlockSpec((tn, tk), lambda i, j, kk: (j, kk)) if tb
                else pl.BlockSpec((tk, tn), lambda i, j, kk: (kk, j))]
    args = [a, b]
    if has_add:
        in_specs.append(pl.BlockSpec((tm, tn), lambda i, j, kk: (i, j)))
        args.append(add)
    return _call(
        body, jobs=jobs, name=name,
        out_shape=jax.ShapeDtypeStruct((m, n), out_dtype),
        grid=(m // tm, n // tn, nk),
        in_specs=in_specs,
        out_specs=pl.BlockSpec((tm, tn), lambda i, j, kk: (i, j)),
        scratch_shapes=[pltpu.VMEM((tm, tn), F32)] if nk > 1 and out_dtype != F32 else [],
        compiler_params=_cp(("parallel", "parallel", "arbitrary")),
    )(*args)


ROWS = 256


def _rmsnorm_fwd(x, g, *, name):
    t, d = x.shape

    def body(x_ref, g_ref, o_ref):
        xv = x_ref[...]
        r = lax.rsqrt(jnp.mean(xv * xv, axis=-1, keepdims=True) + EPS)
        o_ref[...] = (xv * r * g_ref[...]).astype(BF16)

    return pl.pallas_call(
        body, name=name, out_shape=jax.ShapeDtypeStruct((t, d), BF16), grid=(t // ROWS,),
        in_specs=[pl.BlockSpec((ROWS, d), lambda i: (i, 0)), pl.BlockSpec((1, d), lambda i: (0, 0))],
        out_specs=pl.BlockSpec((ROWS, d), lambda i: (i, 0)), compiler_params=_cp(("parallel",)),
    )(x, g)


def _rmsnorm_bwd(x, g, dy, dres, *, name, jobs=()):
    t, d = x.shape

    def body(x_ref, g_ref, dy_ref, dres_ref, dx_ref, dxb_ref, dg_ref):
        xv = x_ref[...]
        r = lax.rsqrt(jnp.mean(xv * xv, axis=-1, keepdims=True) + EPS)
        xh = xv * r
        dyv = dy_ref[...]
        dxh = dyv * g_ref[...]
        dx = r * (dxh - xh * jnp.mean(dxh * xh, axis=-1, keepdims=True))
        tot = dres_ref[...] + dx
        dx_ref[...] = tot
        dxb_ref[...] = tot.astype(BF16)

        @pl.when(pl.program_id(0) == 0)
        def _():
            dg_ref[...] = jnp.zeros_like(dg_ref)

        dg_ref[...] += jnp.broadcast_to(jnp.sum(dyv * xh, axis=0, keepdims=True), dg_ref.shape)

    row = pl.BlockSpec((ROWS, d), lambda i: (i, 0))
    return _call(
        body, jobs=jobs, name=name,
        out_shape=(jax.ShapeDtypeStruct((t, d), F32), jax.ShapeDtypeStruct((t, d), BF16),
                   jax.ShapeDtypeStruct((8, d), F32)),
        grid=(t // ROWS,),
        in_specs=[row, pl.BlockSpec((1, d), lambda i: (0, 0)), row, row],
        out_specs=(row, row, pl.BlockSpec((8, d), lambda i: (0, 0))),
        compiler_params=_cp(("arbitrary",)),
    )(x, g, dy, dres)


def _final(h2, pgl, pp, target, g_final, *, name):
    t, d = h2.shape

    def body(h2_ref, pgl_ref, pp_ref, tg_ref, g_ref, dh3_ref, dpgl_ref, dpp_ref, loss_ref, dg_ref):
        s = _sigmoid(pgl_ref[...])
        ppv = pp_ref[...]
        h3 = h2_ref[...] + s * ppv
        r = lax.rsqrt(jnp.mean(h3 * h3, axis=-1, keepdims=True) + EPS)
        xh = h3 * r
        gv = g_ref[...]
        err = xh * gv - tg_ref[...]
        dyv = err * (1.0 / d)
        dxh = dyv * gv
        dh3 = r * (dxh - xh * jnp.mean(dxh * xh, axis=-1, keepdims=True))
        dh3_ref[...] = dh3
        dpp_ref[...] = (dh3 * s).astype(BF16)
        dpgl_ref[...] = (dh3 * ppv * s * (1.0 - s)).astype(BF16)

        @pl.when(pl.program_id(0) == 0)
        def _():
            loss_ref[...] = jnp.zeros_like(loss_ref)
            dg_ref[...] = jnp.zeros_like(dg_ref)

        part = 0.5 * jnp.sum(jnp.mean(err * err, axis=-1, keepdims=True), axis=0, keepdims=True)
        loss_ref[...] += jnp.broadcast_to(part, loss_ref.shape)
        dg_ref[...] += jnp.broadcast_to(jnp.sum(dyv * xh, axis=0, keepdims=True), dg_ref.shape)

    row = pl.BlockSpec((ROWS, d), lambda i: (i, 0))
    return pl.pallas_call(
        body, name=name,
        out_shape=(jax.ShapeDtypeStruct((t, d), F32), jax.ShapeDtypeStruct((t, d), BF16),
                   jax.ShapeDtypeStruct((t, d), BF16), jax.ShapeDtypeStruct((8, 128), F32),
                   jax.ShapeDtypeStruct((8, d), F32)),
        grid=(t // ROWS,),
        in_specs=[row, row, row, row, pl.BlockSpec((1, d), lambda i: (0, 0))],
        out_specs=(row, row, row, pl.BlockSpec((8, 128), lambda i: (0, 0)), pl.BlockSpec((8, d), lambda i: (0, 0))),
        compiler_params=_cp(("arbitrary",)),
    )(h2, pgl, pp, target, g_final)


def _merge_fwd(proj, out_a, out_s, *, name):
    t = proj.shape[0]

    def body(ga_ref, gs_ref, a_ref, s_ref, o_ref):
        o_ref[...] = (_sigmoid(ga_ref[...]) * a_ref[...] + _sigmoid(gs_ref[...]) * s_ref[...]).astype(BF16)

    row = pl.BlockSpec((ROWS, D), lambda i: (i, 0))
    return pl.pallas_call(
        body, name=name, out_shape=jax.ShapeDtypeStruct((t, D), BF16), grid=(t // ROWS,),
        in_specs=[pl.BlockSpec((ROWS, D), lambda i: (i, O_GA // D)), pl.BlockSpec((ROWS, D), lambda i: (i, O_GS // D)),
                  row, row],
        out_specs=row, compiler_params=_cp(("parallel",)),
    )(proj, proj, out_a, out_s)


def _merge_bwd(proj, out_a, out_s, dmerged, *, name):
    t = proj.shape[0]
    assert O_GA == 0 and O_GS == D

    def body(ga_ref, gs_ref, a_ref, s_ref, dm_ref, da_ref, ds_ref, dp_ref):
        sa = _sigmoid(ga_ref[...])
        ss = _sigmoid(gs_ref[...])
        dm = dm_ref[...]
        da_ref[...] = (dm * sa).astype(BF16)
        ds_ref[...] = (dm * ss).astype(BF16)
        dp_ref[:, :D] = (dm * a_ref[...] * sa * (1.0 - sa)).astype(BF16)
        dp_ref[:, D:] = (dm * s_ref[...] * ss * (1.0 - ss)).astype(BF16)

    row = pl.BlockSpec((ROWS, D), lambda i: (i, 0))
    o = jax.ShapeDtypeStruct((t, D), BF16)
    return pl.pallas_call(
        body, name=name, out_shape=(o, o, jax.ShapeDtypeStruct((t, NP), BF16)), grid=(t // ROWS,),
        in_specs=[pl.BlockSpec((ROWS, D), lambda i: (i, O_GA // D)), pl.BlockSpec((ROWS, D), lambda i: (i, O_GS // D)),
                  row, row, row],
        out_specs=(row, row, pl.BlockSpec((ROWS, 2 * D), lambda i: (i, 0))), compiler_params=_cp(("parallel",)),
    )(proj, proj, out_a, out_s, dmerged)


def _swiglu_fwd(f, w_gate, w_up, *, name, tn=512, jobs=()):
    t, d = f.shape
    n = w_gate.shape[1]

    def body(f_ref, wg_ref, wu_ref, g_ref, u_ref, a_ref):
        fv = f_ref[...]
        g = _dot(fv, wg_ref[...])
        u = _dot(fv, wu_ref[...])
        g_ref[...] = g.astype(BF16)
        u_ref[...] = u.astype(BF16)
        a_ref[...] = (g * _sigmoid(g) * u).astype(BF16)

    col = pl.BlockSpec((t, tn), lambda j: (0, j))
    wcol = pl.BlockSpec((d, tn), lambda j: (0, j))
    return _call(
        body, jobs=jobs, name=name,
        out_shape=(jax.ShapeDtypeStruct((t, n), BF16), jax.ShapeDtypeStruct((t, n), BF16),
                   jax.ShapeDtypeStruct((t, n), BF16)),
        grid=(n // tn,),
        in_specs=[pl.BlockSpec((t, d), lambda j: (0, 0)), wcol, wcol],
        out_specs=(col, col, col), compiler_params=_cp(("parallel",)),
    )(f, w_gate, w_up)


def _swiglu_bwd(gate, up, dact, *, name, tc=1408, jobs=()):
    t, n = gate.shape

    def body(g_ref, u_ref, da_ref, dg_ref, du_ref):
        g = g_ref[...].astype(F32)
        s = _sigmoid(g)
        da = da_ref[...]
        du_ref[...] = (da * g * s).astype(BF16)
        dg_ref[...] = (da * u_ref[...].astype(F32) * s * (1.0 + g * (1.0 - s))).astype(BF16)

    blk = pl.BlockSpec((ROWS, tc), lambda i, j: (i, j))
    o = jax.ShapeDtypeStruct((t, n), BF16)
    return _call(
        body, jobs=jobs, name=name, out_shape=(o, o), grid=(t // ROWS, n // tc),
        in_specs=[blk, blk, blk], out_specs=(blk, blk), compiler_params=_cp(("parallel", "parallel")),
    )(gate, up, dact)


def _gated_norm_fwd(y_pre, proj, g_ssd, *, name):
    t = y_pre.shape[0]

    def body(y_ref, z_ref, g_ref, o_ref):
        z = z_ref[...]
        v = y_ref[...] * z * _sigmoid(z)
        r = lax.rsqrt(jnp.mean(v * v, axis=-1, keepdims=True) + SSM_EPS)
        o_ref[...] = (v * r * g_ref[...]).astype(BF16)

    row = pl.BlockSpec((ROWS, DI), lambda i: (i, 0))
    return pl.pallas_call(
        body, name=name, out_shape=jax.ShapeDtypeStruct((t, DI), BF16), grid=(t // ROWS,),
        in_specs=[row, pl.BlockSpec((ROWS, DI), lambda i: (i, O_Z // DI)), pl.BlockSpec((1, DI), lambda i: (0, 0))],
        out_specs=row, compiler_params=_cp(("parallel",)),
    )(y_pre, proj, g_ssd)


def _gated_norm_bwd(y_pre, proj, g_ssd, dyn, dproj, *, name, jobs=()):
    t = y_pre.shape[0]

    def body(y_ref, z_ref, g_ref, dyn_ref, _, dy_ref, dz_ref, dg_ref):
        z = z_ref[...]
        s = _sigmoid(z)
        sz = z * s
        yv = y_ref[...]
        v = yv * sz
        r = lax.rsqrt(jnp.mean(v * v, axis=-1, keepdims=True) + SSM_EPS)
        vh = v * r
        dn = dyn_ref[...]
        dvh = dn * g_ref[...]
        dv = r * (dvh - vh * jnp.mean(dvh * vh, axis=-1, keepdims=True))
        dy_ref[...] = dv * sz
        dz_ref[...] = (dv * yv * s * (1.0 + z * (1.0 - s))).astype(BF16)

        @pl.when(pl.program_id(0) == 0)
        def _():
            dg_ref[...] = jnp.zeros_like(dg_ref)

        dg_ref[...] += jnp.broadcast_to(jnp.sum(dn * vh, axis=0, keepdims=True), dg_ref.shape)

    row = pl.BlockSpec((ROWS, DI), lambda i: (i, 0))
    return _call(
        body, jobs=jobs, name=name,
        out_shape=(jax.ShapeDtypeStruct((t, DI), F32), jax.ShapeDtypeStruct(dproj.shape, BF16),
                   jax.ShapeDtypeStruct((8, DI), F32)),
        grid=(t // ROWS,),
        in_specs=[row, pl.BlockSpec((ROWS, DI), lambda i: (i, O_Z // DI)), pl.BlockSpec((1, DI), lambda i: (0, 0)), row, ANY],
        out_specs=(row, pl.BlockSpec((ROWS, DI), lambda i: (i, O_Z // DI)), pl.BlockSpec((8, DI), lambda i: (0, 0))),
        compiler_params=_cp(("arbitrary",)), aliases={4: 1},
    )(y_pre, proj, g_ssd, dyn, dproj)


CONV_TC = 512


def _shift_down(x, s, row):
    if s == 0:
        return x
    return jnp.where(row >= s, pltpu.roll(x, s, 0), 0.0)


def _shift_up(x, s, row, t):
    if s == 0:
        return x
    return jnp.where(row < t - s, pltpu.roll(x, t - s, 0), 0.0)


def _conv_fwd(proj, conv_w, conv_b, *, name):
    t = proj.shape[0]

    def body(x_ref, w_ref, b_ref, o_ref):
        x = x_ref[...]
        row = lax.broadcasted_iota(jnp.int32, x.shape, 0)
        pre = jnp.broadcast_to(b_ref[...], x.shape)
        for k in range(CW):
            pre = pre + w_ref[k:k + 1, :] * _shift_down(x, CW - 1 - k, row)
        o_ref[...] = pre * _sigmoid(pre)

    return pl.pallas_call(
        body, name=name, out_shape=jax.ShapeDtypeStruct((t, CONV), F32), grid=(CONV // CONV_TC,),
        in_specs=[pl.BlockSpec((t, CONV_TC), lambda j: (0, O_XBC // CONV_TC + j)),
                  pl.BlockSpec((CW, CONV_TC), lambda j: (0, j)), pl.BlockSpec((1, CONV_TC), lambda j: (0, j))],
        out_specs=pl.BlockSpec((t, CONV_TC), lambda j: (0, j)), compiler_params=_cp(("parallel",)),
    )(proj, conv_w, conv_b)


def _conv_bwd(proj, conv_w, conv_b, dxs, db, dc, dproj, *, name, jobs=()):
    t = proj.shape[0]
    nx = DI // CONV_TC
    assert NG * NS == CONV_TC

    def body(x_ref, w_ref, b_ref, dxs_ref, db_ref, dc_ref, _, dx_ref, dw_ref, dbias_ref):
        j = pl.program_id(0)
        x = x_ref[...]
        row = lax.broadcasted_iota(jnp.int32, x.shape, 0)
        xs = [_shift_down(x, CW - 1 - k, row) for k in range(CW)]
        pre = jnp.broadcast_to(b_ref[...], x.shape)
        for k in range(CW):
            pre = pre + w_ref[k:k + 1, :] * xs[k]
        s = _sigmoid(pre)
        da = jnp.where(j < nx, dxs_ref[...], jnp.where(j == nx, db_ref[...], dc_ref[...]))
        dpre = da * s * (1.0 + pre * (1.0 - s))
        dx = jnp.zeros_like(x)
        row8 = lax.broadcasted_iota(jnp.int32, dw_ref.shape, 0)
        dw = jnp.zeros(dw_ref.shape, F32)
        for k in range(CW):
            dx = dx + w_ref[k:k + 1, :] * _shift_up(dpre, CW - 1 - k, row, t)
            dw = dw + jnp.where(row8 == k, jnp.sum(dpre * xs[k], axis=0, keepdims=True), 0.0)
        dx_ref[...] = dx.astype(BF16)
        dw_ref[...] = dw
        dbias_ref[...] = jnp.broadcast_to(jnp.sum(dpre, axis=0, keepdims=True), dbias_ref.shape)

    col8 = pl.BlockSpec((8, CONV_TC), lambda j: (0, j))
    xbc = pl.BlockSpec((t, CONV_TC), lambda j: (0, O_XBC // CONV_TC + j))
    whole = pl.BlockSpec((t, CONV_TC), lambda j: (0, 0))
    return _call(
        body, jobs=jobs, name=name,
        out_shape=(jax.ShapeDtypeStruct(dproj.shape, BF16), jax.ShapeDtypeStruct((8, CONV), F32),
                   jax.ShapeDtypeStruct((8, CONV), F32)),
        grid=(CONV // CONV_TC,),
        in_specs=[xbc, pl.BlockSpec((CW, CONV_TC), lambda j: (0, j)), pl.BlockSpec((1, CONV_TC), lambda j: (0, j)),
                  pl.BlockSpec((t, CONV_TC), lambda j: (0, jnp.minimum(j, nx - 1))), whole, whole, ANY],
        out_specs=(xbc, col8, col8),
        compiler_params=_cp(("arbitrary",)), aliases={6: 0},
    )(proj, conv_w, conv_b, dxs, db, dc, dproj)


def _rope_tables(positions, t):
    half = HD // 2
    inv_freq = ROPE_THETA ** (-jnp.arange(half, dtype=F32) * 2.0 / HD)
    ang = positions.reshape(t).astype(F32)[:, None] * inv_freq
    cos, sin = jnp.cos(ang), jnp.sin(ang)
    return jnp.concatenate([cos] * 4, axis=1), jnp.concatenate([-sin, sin] * 2, axis=1)


def _lane_consts():
    lane = lax.broadcasted_iota(jnp.int32, (L, 128), 1)
    return lane, (lane % HD) < (HD // 2), lane < HD


def _rope(tv, cos, sin, lo):
    return tv * cos + jnp.where(lo, pltpu.roll(tv, 128 - HD // 2, 1), pltpu.roll(tv, HD // 2, 1)) * sin


def _rope_t(dv, cos, sin, lo):
    ds = dv * sin
    return dv * cos + jnp.where(lo, pltpu.roll(ds, 128 - HD // 2, 1), pltpu.roll(ds, HD // 2, 1))


def _placed(chunk, g, half0):
    own = jnp.where(half0 if g % 2 == 0 else jnp.logical_not(half0), chunk, 0.0)
    other = pltpu.roll(own, HD, 1)
    return (own, other) if g % 2 == 0 else (other, own)


def _unplace(acc, hf, g, half0):
    v = jnp.where(half0 if hf == 0 else jnp.logical_not(half0), acc, 0.0)
    return v if hf == g % 2 else pltpu.roll(v, HD, 1)


def _attn_fwd(proj, cos, sin, sinks, *, name, jobs=()):
    t = proj.shape[0]
    nb = t // L
    scale = HD ** -0.5

    def body(sink_ref, q_ref, kc_ref, kp_ref, vc_ref, vp_ref, cc_ref, sc_ref, cp_ref, sp_ref, o_ref, lse_ref):
        i = pl.program_id(0)
        lane, lo, half0 = _lane_consts()
        cos_c, sin_c, cos_p, sin_p = cc_ref[...], sc_ref[...], cp_ref[...], sp_ref[...]
        row = lax.broadcasted_iota(jnp.int32, (L, L), 0)
        col = lax.broadcasted_iota(jnp.int32, (L, L), 1)
        m_cur = col <= row
        m_prev = jnp.logical_and(col > row, i > 0)
        kc = [_rope(kc_ref[:, 128 * m:128 * (m + 1)], cos_c, sin_c, lo) for m in range(2)]
        kp = [_rope(kp_ref[:, 128 * m:128 * (m + 1)], cos_p, sin_p, lo) for m in range(2)]
        lse_acc = jnp.zeros((L, 128), F32)
        outs = [jnp.zeros((L, 128), F32) for _ in range(QD // 128)]
        qs = [(_rope(q_ref[:, 128 * ch:128 * (ch + 1)], cos_c, sin_c, lo) * scale).astype(BF16) for ch in range(QD // 128)]
        for g in range(NKV):
            kcv = [v.astype(BF16) for v in _placed(kc[g // 2], g, half0)]
            kpv = [v.astype(BF16) for v in _placed(kp[g // 2], g, half0)]
            vcv = [v.astype(BF16) for v in _placed(vc_ref[:, 128 * (g // 2):128 * (g // 2 + 1)], g, half0)]
            vpv = [v.astype(BF16) for v in _placed(vp_ref[:, 128 * (g // 2):128 * (g // 2 + 1)], g, half0)]
            for r in range(NQH // NKV):
                h = g * (NQH // NKV) + r
                ch, hf = h // 2, h % 2
                s_c = jnp.where(m_cur, _dot_nt(qs[ch], kcv[hf]), NEG)
                s_p = jnp.where(m_prev, _dot_nt(qs[ch], kpv[hf]), NEG)
                sink = sink_ref[0, h]
                mx = jnp.maximum(jnp.maximum(jnp.max(s_c, axis=-1, keepdims=True), jnp.max(s_p, axis=-1, keepdims=True)), sink)
                e_c = jnp.exp(s_c - mx)
                e_p = jnp.exp(s_p - mx)
                den = jnp.sum(e_c, axis=-1, keepdims=True) + jnp.sum(e_p, axis=-1, keepdims=True) + jnp.exp(sink - mx)
                inv = 1.0 / den
                outs[ch] = outs[ch] + _dot((e_c * inv).astype(BF16), vcv[hf]) + _dot((e_p * inv).astype(BF16), vpv[hf])
                lse_acc = jnp.where(lane == h, mx + jnp.log(den), lse_acc)
        for ch in range(QD // 128):
            o_ref[:, 128 * ch:128 * (ch + 1)] = outs[ch].astype(BF16)
        lse_ref[...] = lse_acc

    prev = lambda i: jnp.maximum(i - 1, 0)
    tab_c = pl.BlockSpec((L, 128), lambda i: (i, 0))
    tab_p = pl.BlockSpec((L, 128), lambda i: (prev(i), 0))
    return _call(
        body, jobs=jobs, name=name,
        out_shape=(jax.ShapeDtypeStruct((t, QD), BF16), jax.ShapeDtypeStruct((t, 128), F32)),
        grid=(nb,),
        in_specs=[pl.BlockSpec(memory_space=pltpu.SMEM),
                  pl.BlockSpec((L, QD), lambda i: (i, O_Q // QD)),
                  pl.BlockSpec((L, KVD), lambda i: (i, O_K // KVD)), pl.BlockSpec((L, KVD), lambda i: (prev(i), O_K // KVD)),
                  pl.BlockSpec((L, KVD), lambda i: (i, O_V // KVD)), pl.BlockSpec((L, KVD), lambda i: (prev(i), O_V // KVD)),
                  tab_c, tab_c, tab_p, tab_p],
        out_specs=(pl.BlockSpec((L, QD), lambda i: (i, 0)), pl.BlockSpec((L, 128), lambda i: (i, 0))),
        compiler_params=_cp(("parallel",)),
    )(sinks, proj, proj, proj, proj, proj, cos, sin, cos, sin)


def _attn_bwd(proj, cos, sin, sinks, attn, lse, dattn, dproj, *, name, jobs=()):
    t = proj.shape[0]
    nb = t // L
    scale = HD ** -0.5

    def body(sink_ref, qi_ref, qn_ref, kc_ref, kp_ref, vc_ref, vp_ref, doi_ref, don_ref, oi_ref, on_ref,
             lsei_ref, lsen_ref, cc_ref, sc_ref, cp_ref, sp_ref, cn_ref, sn_ref, _, dqkv_ref, dsk_ref):
        i = pl.program_id(0)
        lane, lo, half0 = _lane_consts()
        half1 = jnp.logical_not(half0)
        cos_c, sin_c = cc_ref[...], sc_ref[...]
        row = lax.broadcasted_iota(jnp.int32, (L, L), 0)
        col = lax.broadcasted_iota(jnp.int32, (L, L), 1)
        m_cur = col <= row
        m_prev = jnp.logical_and(col > row, i > 0)
        m_next = jnp.logical_and(col > row, i < nb - 1)
        kc = [_rope(kc_ref[:, 128 * m:128 * (m + 1)], cos_c, sin_c, lo) for m in range(2)]
        kp = [_rope(kp_ref[:, 128 * m:128 * (m + 1)], cp_ref[...], sp_ref[...], lo) for m in range(2)]
        lse_i, lse_n = lsei_ref[...], lsen_ref[...]
        dk_acc = [jnp.zeros((L, 128), F32) for _ in range(2)]
        dv_acc = [jnp.zeros((L, 128), F32) for _ in range(2)]
        dsk_acc = jnp.zeros((1, 128), F32)
        lane1 = lax.broadcasted_iota(jnp.int32, (1, 128), 1)
        place = lambda chunk, g: [v.astype(BF16) for v in _placed(chunk, g, half0)]
        kcs = [place(kc[g // 2], g) for g in range(NKV)]
        kps = [place(kp[g // 2], g) for g in range(NKV)]
        vcs = [place(vc_ref[:, 128 * (g // 2):128 * (g // 2 + 1)], g) for g in range(NKV)]
        vps = [place(vp_ref[:, 128 * (g // 2):128 * (g // 2 + 1)], g) for g in range(NKV)]
        for ch in range(QD // 128):
            sl = slice(128 * ch, 128 * (ch + 1))
            q_i = (_rope(qi_ref[:, sl], cos_c, sin_c, lo) * scale).astype(BF16)
            q_n = (_rope(qn_ref[:, sl], cn_ref[...], sn_ref[...], lo) * scale).astype(BF16)
            do_i, do_n = doi_ref[:, sl], don_ref[:, sl]
            do_ib, do_nb = do_i.astype(BF16), do_n.astype(BF16)
            od_i = do_i * oi_ref[:, sl].astype(F32)
            od_n = do_n * on_ref[:, sl].astype(F32)
            dq_ch = jnp.zeros((L, 128), F32)
            for hf in range(2):
                h = 2 * ch + hf
                g = h // (NQH // NKV)
                hm = half0 if hf == 0 else half1
                kcv, kpv, vcv, vpv = kcs[g][hf], kps[g][hf], vcs[g][hf], vps[g][hf]
                dl_i = jnp.sum(jnp.where(hm, od_i, 0.0), axis=-1, keepdims=True)
                dl_n = jnp.sum(jnp.where(hm, od_n, 0.0), axis=-1, keepdims=True)
                ls_i = jnp.sum(jnp.where(lane == h, lse_i, 0.0), axis=-1, keepdims=True)
                ls_n = jnp.sum(jnp.where(lane == h, lse_n, 0.0), axis=-1, keepdims=True)
                p_c = jnp.where(m_cur, jnp.exp(_dot_nt(q_i, kcv) - ls_i), 0.0)
                p_p = jnp.where(m_prev, jnp.exp(_dot_nt(q_i, kpv) - ls_i), 0.0)
                ds_c = (p_c * (_dot_nt(do_ib, vcv) - dl_i)).astype(BF16)
                ds_p = (p_p * (_dot_nt(do_ib, vpv) - dl_i)).astype(BF16)
                dq_ch = dq_ch + jnp.where(hm, (_dot(ds_c, kcv) + _dot(ds_p, kpv)) * scale, 0.0)
                sink = sink_ref[0, h]
                dsk = -jnp.sum(jnp.exp(sink - ls_i) * dl_i, axis=0, keepdims=True)
                dsk_acc = dsk_acc + jnp.where(lane1 == h, dsk, 0.0)
                p_n = jnp.where(m_next, jnp.exp(_dot_nt(q_n, kcv) - ls_n), 0.0)
                ds_n = (p_n * (_dot_nt(do_nb, vcv) - dl_n)).astype(BF16)
                dv_h = _dot_tn(p_c.astype(BF16), do_ib) + _dot_tn(p_n.astype(BF16), do_nb)
                dk_h = _dot_tn(ds_c, q_i) + _dot_tn(ds_n, q_n)
                dv_acc[g // 2] = dv_acc[g // 2] + _unplace(dv_h, hf, g, half0)
                dk_acc[g // 2] = dk_acc[g // 2] + _unplace(dk_h, hf, g, half0)
            dqkv_ref[:, sl] = _rope_t(dq_ch, cos_c, sin_c, lo).astype(BF16)
        for m in range(2):
            dqkv_ref[:, QD + 128 * m:QD + 128 * (m + 1)] = _rope_t(dk_acc[m], cos_c, sin_c, lo).astype(BF16)
            dqkv_ref[:, QD + KVD + 128 * m:QD + KVD + 128 * (m + 1)] = dv_acc[m].astype(BF16)

        @pl.when(i == 0)
        def _():
            dsk_ref[...] = jnp.zeros_like(dsk_ref)

        dsk_ref[...] += jnp.broadcast_to(dsk_acc, dsk_ref.shape)

    prev = lambda i: jnp.maximum(i - 1, 0)
    nxt = lambda i: jnp.minimum(i + 1, nb - 1)
    cur_q = pl.BlockSpec((L, QD), lambda i: (i, 0))
    nxt_q = pl.BlockSpec((L, QD), lambda i: (nxt(i), 0))
    tab = lambda f: pl.BlockSpec((L, 128), lambda i: (f(i), 0))
    ident = lambda i: i
    qkv = QD + 2 * KVD
    assert O_K == O_Q + QD and O_V == O_K + KVD and O_Q % qkv == 0
    return _call(
        body, jobs=jobs, name=name,
        out_shape=(jax.ShapeDtypeStruct(dproj.shape, BF16), jax.ShapeDtypeStruct((8, 128), F32)),
        grid=(nb,),
        in_specs=[pl.BlockSpec(memory_space=pltpu.SMEM),
                  pl.BlockSpec((L, QD), lambda i: (i, O_Q // QD)), pl.BlockSpec((L, QD), lambda i: (nxt(i), O_Q // QD)),
                  pl.BlockSpec((L, KVD), lambda i: (i, O_K // KVD)), pl.BlockSpec((L, KVD), lambda i: (prev(i), O_K // KVD)),
                  pl.BlockSpec((L, KVD), lambda i: (i, O_V // KVD)), pl.BlockSpec((L, KVD), lambda i: (prev(i), O_V // KVD)),
                  cur_q, nxt_q, cur_q, nxt_q, tab(ident), tab(nxt),
                  tab(ident), tab(ident), tab(prev), tab(prev), tab(nxt), tab(nxt), ANY],
        out_specs=(pl.BlockSpec((L, qkv), lambda i: (i, O_Q // qkv)), pl.BlockSpec((8, 128), lambda i: (0, 0))),
        compiler_params=_cp(("arbitrary",)), aliases={19: 0},
    )(sinks, proj, proj, proj, proj, proj, proj, dattn, dattn, attn, attn, lse, lse, cos, sin, cos, sin, cos, sin, dproj)


PAIRS = NH // NG // 2


def _softplus(x):
    return jnp.maximum(x, 0.0) + jnp.log(1.0 + jnp.exp(-jnp.abs(x)))


def _ssd_chunk(g, xps, dtr, bm, cm, sps, dtb, alog, dsk):
    lane = lax.broadcasted_iota(jnp.int32, (L, 128), 1)
    lane1 = lax.broadcasted_iota(jnp.int32, (1, 128), 1)
    row = lax.broadcasted_iota(jnp.int32, (L, L), 0)
    col = lax.broadcasted_iota(jnp.int32, (L, L), 1)
    rowc = lax.broadcasted_iota(jnp.int32, (128, 1), 0)
    tril = col <= row
    dt = _softplus(dtr + dtb)
    a = dt * (-jnp.exp(alog))
    a_cs = lax.dot_general(tril.astype(F32), a, (((1,), (0,)), ((), ())), precision=lax.Precision.HIGHEST,
                           preferred_element_type=F32)
    a_cst = a_cs.T
    a_last = jnp.sum(jnp.where(row == L - 1, a_cs, 0.0), axis=0, keepdims=True)
    cb = _bdot_nt(cm, bm)
    ys, snew = [], []
    for q in range(PAIRS):
        xp, sp = xps[q], sps[q]
        y_pair = jnp.zeros((L, 128), F32)
        st_pair = jnp.zeros((128, NS), F32)
        keep = jnp.zeros((128, 1), F32)
        for hh in range(2):
            h = g * 2 * PAIRS + 2 * q + hh
            hm = (lane < HD) if hh == 0 else (lane >= HD)
            rm = (rowc < HD) if hh == 0 else (rowc >= HD)
            dt_h = jnp.sum(jnp.where(lane == h, dt, 0.0), axis=1, keepdims=True)
            acs_h = jnp.sum(jnp.where(lane == h, a_cs, 0.0), axis=1, keepdims=True)
            acst_h = jnp.sum(jnp.where(row == h, a_cst, 0.0), axis=0, keepdims=True)
            al_h = jnp.sum(jnp.where(lane1 == h, a_last, 0.0), axis=1, keepdims=True)
            dsk_h = jnp.sum(jnp.where(lane1 == h, dsk, 0.0), axis=1, keepdims=True)
            decay = jnp.where(tril, jnp.exp(jnp.where(tril, acs_h - acst_h, 0.0)), 0.0)
            xh = jnp.where(hm, xp, 0.0)
            xd = xh * dt_h
            y = _bdot(cb * decay, xd)
            y = y + jnp.where(hm, _bdot_nt(cm * jnp.exp(acs_h), sp), 0.0)
            y_pair = y_pair + y + dsk_h * xh
            st_pair = st_pair + _bdot_tn(xd, bm * jnp.exp(al_h - acs_h))
            keep = keep + jnp.where(rm, jnp.exp(al_h), 0.0)
        ys.append(y_pair)
        snew.append(sp * keep + st_pair)
    return ys, snew


def _ssd_specs(t):
    nc = t // L
    xs = lambda f: pl.BlockSpec((L, 128 * PAIRS), lambda c, g: (f(c), g))
    bspec = lambda f: pl.BlockSpec((L, NS), lambda c, g: (f(c), DI // NS + g))
    cspec = lambda f: pl.BlockSpec((L, NS), lambda c, g: (f(c), DI // NS + NG + g))
    dts = lambda f: pl.BlockSpec((L, 128), lambda c, g: (f(c), O_DT // 128))
    par = pl.BlockSpec((1, 128), lambda c, g: (0, 0))
    st = lambda f: pl.BlockSpec((1, 1, PAIRS, 128, NS), lambda c, g: (f(c), g, 0, 0, 0))
    return nc, xs, bspec, cspec, dts, par, st


def _ssd_fwd(xbc_act, proj, dtb, alog, dsk, *, name, jobs=()):
    t = proj.shape[0]
    nc, xs, bspec, cspec, dts, par, st = _ssd_specs(t)
    ident = lambda c: c

    def body(x_ref, b_ref, c_ref, dt_ref, dtb_ref, al_ref, dsk_ref, y_ref, sin_ref, s_ref):
        c, g = pl.program_id(0), pl.program_id(1)

        @pl.when(c == 0)
        def _():
            s_ref[g] = jnp.zeros((PAIRS, 128, NS), F32)

        sps = [s_ref[g, q] for q in range(PAIRS)]
        for q in range(PAIRS):
            sin_ref[0, 0, q] = sps[q]
        xps = [x_ref[:, 128 * q:128 * (q + 1)] for q in range(PAIRS)]
        ys, snew = _ssd_chunk(g, xps, dt_ref[...], b_ref[...], c_ref[...], sps, dtb_ref[...], al_ref[...], dsk_ref[...])
        for q in range(PAIRS):
            y_ref[:, 128 * q:128 * (q + 1)] = ys[q]
            s_ref[g, q] = snew[q]

    return _call(
        body, jobs=jobs, name=name,
        out_shape=(jax.ShapeDtypeStruct((t, DI), F32), jax.ShapeDtypeStruct((nc, NG, PAIRS, 128, NS), F32)),
        grid=(nc, NG),
        in_specs=[xs(ident), bspec(ident), cspec(ident), dts(ident), par, par, par],
        out_specs=(pl.BlockSpec((L, 128 * PAIRS), lambda c, g: (c, g)), st(ident)),
        scratch_shapes=[pltpu.VMEM((NG, PAIRS, 128, NS), F32)],
        compiler_params=_cp(("arbitrary", "arbitrary")),
    )(xbc_act, xbc_act, xbc_act, proj, dtb, alog, dsk)


def _ssd_bwd(xbc_act, proj, dtb, alog, dsk, states, dy, dproj, *, name, jobs=()):
    t = proj.shape[0]
    nc, xs, bspec, cspec, dts, par, st = _ssd_specs(t)
    rev = lambda c: nc - 1 - c

    def body(x_ref, b_ref, c_ref, dt_ref, dtb_ref, al_ref, dsk_ref, sin_ref, dy_ref, _,
             dx_ref, db_ref, dc_ref, ddtp_ref, ddtb_ref, dal_ref, ddsk_ref, ds_ref, ddt_ref):
        c, g = pl.program_id(0), pl.program_id(1)

        @pl.when(c == 0)
        def _():
            ds_ref[g] = jnp.zeros((PAIRS, 128, NS), F32)

        @pl.when(jnp.logical_and(c == 0, g == 0))
        def _():
            ddtb_ref[...] = jnp.zeros_like(ddtb_ref)
            dal_ref[...] = jnp.zeros_like(dal_ref)
            ddsk_ref[...] = jnp.zeros_like(ddsk_ref)

        @pl.when(g == 0)
        def _():
            ddt_ref[...] = jnp.zeros_like(ddt_ref)

        sps = [sin_ref[0, 0, q] for q in range(PAIRS)]
        xps = [x_ref[:, 128 * q:128 * (q + 1)] for q in range(PAIRS)]
        _, vjp = jax.vjp(functools.partial(_ssd_chunk, g), xps, dt_ref[...], b_ref[...], c_ref[...], sps,
                         dtb_ref[...], al_ref[...], dsk_ref[...])
        dys = [dy_ref[:, 128 * q:128 * (q + 1)] for q in range(PAIRS)]
        dss = [ds_ref[g, q] for q in range(PAIRS)]
        dxps, ddt, db, dc, dsps, ddtb, dal, ddsk = vjp((dys, dss))
        for q in range(PAIRS):
            dx_ref[:, 128 * q:128 * (q + 1)] = dxps[q]
            ds_ref[g, q] = dsps[q]
        db_ref[...] = db
        dc_ref[...] = dc
        ddt_ref[...] += ddt
        ddtb_ref[...] += jnp.broadcast_to(ddtb, ddtb_ref.shape)
        dal_ref[...] += jnp.broadcast_to(dal, dal_ref.shape)
        ddsk_ref[...] += jnp.broadcast_to(ddsk, ddsk_ref.shape)

        @pl.when(g == NG - 1)
        def _():
            ddtp_ref[:, :128] = ddt_ref[...].astype(BF16)
            ddtp_ref[:, 128:] = jnp.zeros((L, DT_PAD - 128), BF16)

    acc = pl.BlockSpec((8, 128), lambda c, g: (0, 0))
    o8 = jax.ShapeDtypeStruct((8, 128), F32)
    return _call(
        body, jobs=jobs, name=name,
        out_shape=(jax.ShapeDtypeStruct((t, DI), F32), jax.ShapeDtypeStruct((t, NG * NS), F32),
                   jax.ShapeDtypeStruct((t, NG * NS), F32), jax.ShapeDtypeStruct(dproj.shape, BF16), o8, o8, o8),
        grid=(nc, NG),
        in_specs=[xs(rev), bspec(rev), cspec(rev), dts(rev), par, par, par, st(rev),
                  pl.BlockSpec((L, 128 * PAIRS), lambda c, g: (rev(c), g)), ANY],
        out_specs=(pl.BlockSpec((L, 128 * PAIRS), lambda c, g: (rev(c), g)),
                   pl.BlockSpec((L, NS), lambda c, g: (rev(c), g)), pl.BlockSpec((L, NS), lambda c, g: (rev(c), g)),
                   pl.BlockSpec((L, DT_PAD), lambda c, g: (rev(c), O_DT // DT_PAD)), acc, acc, acc),
        scratch_shapes=[pltpu.VMEM((NG, PAIRS, 128, NS), F32), pltpu.VMEM((L, 128), F32)],
        compiler_params=_cp(("arbitrary", "arbitrary")), aliases={9: 3},
    )(xbc_act, xbc_act, xbc_act, proj, dtb, alog, dsk, states, dy, dproj)


def _pad_lanes(v, n=128):
    return jnp.pad(v, ((0, 0), (0, n - v.shape[1])))


class _LocalPlan:
    core = 0

    def __init__(self, big):
        self.big, self.grad, self.halves = big, {}, {}

    def w(self, n):
        return self.big[n]

    def g(self, n, a):
        self.grad[n] = a

    def g_half(self, n, which, a):
        self.halves[which] = a
        if len(self.halves) == 2:
            self.grad[n] = jnp.concatenate([self.halves["keep"], self.halves["send"]], axis=0)

    def jobs(self, tag):
        return ()


def _local_step(x, p, positions, target, small, plan):
    t = x.shape[0]
    cos, sin = _rope_tables(positions, t)
    dtb, alog, dsk = _pad_lanes(small["dt_bias"]), _pad_lanes(small["a_log"]), _pad_lanes(small["d_skip"])
    w, jobs = plan.w, plan.jobs

    def mm(a, b, *, name, tn=512, **kw):
        return _matmul(a, b, tm=t, tn=tn, name=name, jobs=jobs(name), **kw)

    tkl = FFN // 4

    def dw(wname, a, dy, *, name, tm):
        plan.g(wname, _matmul(a, dy, ta=True, out_dtype=BF16, tm=tm, tn=512, tk=t, name=name, jobs=jobs(name)))

    u = _rmsnorm_fwd(x, small["g_mix"], name="norm_mix")
    proj = mm(u, w("w_in"), tk=D, name="mm_in")
    attn, lse = _attn_fwd(proj, cos, sin, small["sinks"], name="attn_fwd", jobs=jobs("attn_fwd"))
    out_a = mm(attn, w("w_attn_br"), tk=QD, name="mm_attn_br")
    xbc_act = _conv_fwd(proj, small["conv_w"], small["conv_b"], name="conv_fwd")
    y_pre, states = _ssd_fwd(xbc_act, proj, dtb, alog, dsk, name="ssd_fwd", jobs=jobs("ssd_fwd"))
    yn = _gated_norm_fwd(y_pre, proj, small["g_ssd"], name="gated_norm_fwd")
    out_s = mm(yn, w("w_ssd_br"), tk=DI, name="mm_ssd_br")
    merged = _merge_fwd(proj, out_a, out_s, name="merge_fwd")
    h1 = mm(merged, w("w_o"), add=x, tk=D, name="mm_o")
    f = _rmsnorm_fwd(h1, small["g_ffn"], name="norm_ffn")
    gate, up, act = _swiglu_fwd(f, w("w_gate"), w("w_up"), name="swiglu_fwd", jobs=jobs("swiglu_fwd"))
    h2 = mm(act, w("w_down"), add=h1, tk=tkl, name="mm_down")
    e = _rmsnorm_fwd(h2, small["g_ple"], name="norm_ple")
    pgl = mm(e, w("w_ple_gate"), tk=D, name="mm_ple_gate")
    pb = p.astype(BF16)
    pp = mm(pb, w("w_ple_proj"), tk=PLE, name="mm_ple_proj")
    dh3, dpgl, dpp, loss, dg_final = _final(h2, pgl, pp, target, small["g_final"].reshape(1, D), name="final")

    dw("w_ple_proj", pb, dpp, tm=PLE, name="mm_d_ple_proj")
    dw("w_ple_gate", e, dpgl, tm=D, name="mm_d_ple_gate")
    de = mm(dpgl, w("w_ple_gate"), tb=True, tk=D, name="mm_de")
    dh2, dh2b, dg_ple = _rmsnorm_bwd(h2, small["g_ple"], de, dh3, name="norm_ple_bwd", jobs=jobs("norm_ple_bwd"))
    dw("w_down", act, dh2b, tm=FFN // 2, name="mm_d_down")
    dact = mm(dh2b, w("w_down"), tb=True, tk=D, name="mm_dact")
    dgate, dup = _swiglu_bwd(gate, up, dact, name="swiglu_bwd", jobs=jobs("swiglu_bwd"))
    dw("w_gate", f, dgate, tm=D, name="mm_d_gate")
    dw("w_up", f, dup, tm=D, name="mm_d_up")
    df = mm(dgate, w("w_gate"), tb=True, tn=1024, tk=tkl, name="mm_df_gate")
    df = mm(dup, w("w_up"), tb=True, add=df, tk=tkl, name="mm_df_up")
    dh1, dh1b, dg_ffn = _rmsnorm_bwd(h1, small["g_ffn"], df, dh2, name="norm_ffn_bwd", jobs=jobs("norm_ffn_bwd"))
    dw("w_o", merged, dh1b, tm=D, name="mm_d_o")
    dmerged = mm(dh1b, w("w_o"), tb=True, tk=D, name="mm_dmerged")
    dout_a, dout_s, dproj = _merge_bwd(proj, out_a, out_s, dmerged, name="merge_bwd")
    dw("w_attn_br", attn, dout_a, tm=QD, name="mm_d_attn_br")
    dw("w_ssd_br", yn, dout_s, tm=DI, name="mm_d_ssd_br")
    dattn = mm(dout_a, w("w_attn_br"), tb=True, tk=D, name="mm_dattn")
    dyn = mm(dout_s, w("w_ssd_br"), tb=True, tk=D, name="mm_dyn")
    dproj, dsinks = _attn_bwd(proj, cos, sin, small["sinks"], attn, lse, dattn, dproj, name="attn_bwd",
                              jobs=jobs("attn_bwd"))
    dy_pre, dproj, dg_ssd = _gated_norm_bwd(y_pre, proj, small["g_ssd"], dyn, dproj, name="gated_norm_bwd",
                                            jobs=jobs("gated_norm_bwd"))
    dxs, db, dc, dproj, ddtb, dalog, ddsk = _ssd_bwd(xbc_act, proj, dtb, alog, dsk, states, dy_pre, dproj, name="ssd_bwd",
                                                     jobs=jobs("ssd_bwd"))
    dproj, dconv_w, dconv_b = _conv_bwd(proj, small["conv_w"], small["conv_b"], dxs, db, dc, dproj, name="conv_bwd",
                                        jobs=jobs("conv_bwd"))
    for which, h in (("send", 1 - plan.core), ("keep", plan.core)):
        uh = lax.dynamic_slice_in_dim(u, h * (D // 2), D // 2, axis=1)
        name = "mm_d_in_" + which
        plan.g_half("w_in", which, _matmul(uh, dproj, ta=True, out_dtype=BF16, tm=D // 2, tn=512, tk=t, name=name,
                                           jobs=jobs(name)))
    du = mm(dproj, w("w_in"), tb=True, tn=1024, tk=tkl, name="mm_du")
    grad_x, _, dg_mix = _rmsnorm_bwd(x, small["g_mix"], du, dh1, name="norm_mix_bwd", jobs=jobs("norm_mix_bwd"))

    gs = {
        "g_mix": dg_mix[:1], "conv_w": dconv_w[:CW], "conv_b": dconv_b[:1], "dt_bias": ddtb[:1, :NH],
        "a_log": dalog[:1, :NH], "d_skip": ddsk[:1, :NH], "g_ssd": dg_ssd[:1], "sinks": dsinks[:1, :NQH],
        "g_ffn": dg_ffn[:1], "g_ple": dg_ple[:1], "g_final": dg_final[0],
    }
    return loss, grad_x, gs


def _to_kernel_cols(w):
    seg = lambda o, n: w[:, o:o + n]
    return jnp.concatenate([seg(R_GA, D), seg(R_GS, D), seg(R_Z, DI), seg(R_XBC, CONV), seg(R_Q, QD), seg(R_K, KVD),
                            seg(R_V, KVD), seg(R_DT, NH), jnp.zeros((w.shape[0], DT_PAD - NH), w.dtype)], axis=1)


def _from_kernel_cols(g):
    seg = lambda o, n: g[:, o:o + n]
    return jnp.concatenate([seg(O_Q, QD), seg(O_K, KVD), seg(O_V, KVD), seg(O_Z, DI), seg(O_XBC, CONV), seg(O_DT, NH),
                            seg(O_GA, D), seg(O_GS, D)], axis=1)


def _shard_pieces():
    segs = ((R_Q, QD, O_Q), (R_K, KVD, O_K), (R_V, KVD, O_V), (R_Z, DI, O_Z), (R_XBC, CONV, O_XBC), (R_DT, NH, O_DT),
            (R_GA, D, O_GA), (R_GS, D, O_GS))
    cs = IN_DIM // NCHIP
    out = []
    for j in range(NCHIP):
        for r0, n, k0 in segs:
            lo, hi = max(r0, j * cs), min(r0 + n, (j + 1) * cs)
            if lo < hi:
                out.append((j, lo - j * cs, hi - lo, k0 + lo - r0))
    return out


SLAB = IN_DIM // NCHIP
SLAB_PAD = -(-SLAB // 128) * 128
REMAP_ROWS = 256


def _lane_remap(src, dst_slabs, dst_cols, moves, *, name, jobs=()):
    s_n, rows, s_cols = src.shape
    assert s_cols % 128 == 0 and dst_cols % 128 == 0 and rows % REMAP_ROWS == 0
    half = REMAP_ROWS // 2

    def body(s_ref, d_ref):
        lane = lax.broadcasted_iota(jnp.int32, (half, 128), 1)
        tiles = {}

        def tile(j, m):
            if (j, m) not in tiles:
                tiles[j, m] = pltpu.bitcast(s_ref[j, :, 128 * m:128 * (m + 1)], jnp.uint32)
            return tiles[j, m]

        def window(j, base):
            m0, s = base // 128, base % 128
            left = tile(j, m0) if 0 <= m0 < s_cols // 128 else None
            if s == 0:
                return left
            right = tile(j, m0 + 1) if 0 <= m0 + 1 < s_cols // 128 else None
            left = None if left is None else pltpu.roll(left, 128 - s, 1)
            right = None if right is None else pltpu.roll(right, 128 - s, 1)
            if left is None or right is None:
                return right if left is None else left
            return jnp.where(lane < 128 - s, left, right)

        for ds in range(dst_slabs):
            for t in range(dst_cols // 128):
                o = 128 * t
                acc = jnp.zeros((half, 128), jnp.uint32)
                for sj, sc, n, dj, dc in moves:
                    lo, hi = max(o, dc) - o, min(o + 128, dc + n) - o
                    if dj != ds or lo >= hi:
                        continue
                    win = window(sj, o - dc + sc)
                    acc = win if (lo, hi) == (0, 128) else jnp.where(jnp.logical_and(lane >= lo, lane < hi), win, acc)
                d_ref[ds, :, o:o + 128] = pltpu.bitcast(acc, BF16)

    return _call(
        body, jobs=jobs, name=name, out_shape=jax.ShapeDtypeStruct((dst_slabs, rows, dst_cols), BF16),
        grid=(rows // REMAP_ROWS,),
        in_specs=[pl.BlockSpec((s_n, REMAP_ROWS, s_cols), lambda i: (0, i, 0))],
        out_specs=pl.BlockSpec((dst_slabs, REMAP_ROWS, dst_cols), lambda i: (0, i, 0)),
        compiler_params=_cp(("parallel",)),
    )(src)


def _slabs_to_kernel_cols(slabs, *, name, jobs=()):
    moves = [(j, a, n, 0, k0) for j, a, n, k0 in _shard_pieces()]
    return _lane_remap(slabs, 1, NP, moves, name=name, jobs=jobs)[0]


def _kernel_cols_to_slabs(g, *, name, jobs=()):
    moves = [(0, k0, n, j, a) for j, a, n, k0 in _shard_pieces()]
    return _lane_remap(g[None], NCHIP, SLAB_PAD, moves, name=name, jobs=jobs)


RELS = ((0, 1), (1, 0), (1, 1))
MATS = {
    n: (n, kind, 1, r, c, tp, tf) for n, kind, r, c, tp, tf in (
        ("w_in", "stk", 2048, SLAB_PAD, 256, 256),
        ("w_attn_br", "col", 1024, 512, 256, 256),
        ("w_ssd_br", "row", 512, 2048, 512, 256),
        ("w_o", "row", 512, 2048, 512, 256),
        ("w_gate", "col", 2048, 1408, 256, 256),
        ("w_up", "col", 2048, 1408, 256, 256),
        ("w_down", "row", 1408, 2048, 704, 704),
        ("w_ple_gate", "row", 512, 2048, 512, 256),
        ("w_ple_proj", "col", 256, 512, 128, 128),
    )}


def _pos():
    return lax.axis_index("x"), lax.axis_index("y"), lax.axis_index("c")


def _flip(v, a):
    return 1 - v if a else v


def _remote(src, dst, send, recv, dev):
    return pltpu.make_async_remote_copy(src_ref=src, dst_ref=dst, send_sem=send, recv_sem=recv, device_id=dev,
                                        device_id_type=MESH)


def _whole_shape(kind, g, r, c):
    return {"row": (g, NCHIP * r, c), "col": (g, r, NCHIP * c), "stk": (NCHIP, r, c)}[kind]


def _cols(j, c):
    return pl.ds(pl.multiple_of(j * c, 128), c)


def _whole_shard(kind, ref, j, r, c):
    if kind == "row":
        return ref.at[:, pl.ds(j * r, r), :]
    if kind == "col":
        return ref.at[:, :, _cols(j, c)]
    return ref.at[pl.ds(j, 1)]


def _whole_rows(kind, ref, j, row, n, r, c):
    if kind == "row":
        return ref.at[:, pl.ds(j * r + row, n), :]
    if kind == "col":
        return ref.at[:, pl.ds(row, n), _cols(j, c)]
    return ref.at[pl.ds(j, 1), pl.ds(row, n), :]


class _GatherJob(_Job):
    has_mid = True
    NCP = 10

    def __init__(self, names, shards, sink):
        self.mats = [MATS[n] for n in names]
        self.srcs = [shards[n] for n in names]
        self.news = [jax.ShapeDtypeStruct(_whole_shape(kind, g, r, c), BF16) for _, kind, g, r, c, _, _ in self.mats]
        n = len(names)
        self.scratch = [pltpu.SemaphoreType.DMA((self.NCP * n,)), pltpu.SemaphoreType.DMA((self.NCP * n,))]
        self.names, self.sink = names, sink

    def _copies(self, srcs, news, sems):
        send, recv = sems
        x, y, c = _pos()
        me, jx, jy, jd = 2 * x + y, 2 * (1 - x) + y, 2 * x + (1 - y), 2 * (1 - x) + (1 - y)
        nbx, nby, sib = (1 - x, y, c), (x, 1 - y, c), (x, y, 1 - c)
        cps = []
        for w, (_, kind, g, r, cc, _, _) in enumerate(self.mats):
            hr, qr = r // 2, r // 4
            at = lambda j, h, q, n: _whole_rows(kind, news[w], j, h * hr + q * qr, n, r, cc)
            mine = lambda q: srcs[w].at[:, pl.ds(c * hr + q * qr, qr), :]
            cp = lambda k, s, d, dev: _remote(s, d, send.at[self.NCP * w + k], recv.at[self.NCP * w + k], dev)
            cps.append([
                cp(0, mine(0), at(me, c, 0, qr), nbx), cp(1, mine(1), at(me, c, 1, qr), nbx),
                cp(2, mine(1), at(me, c, 1, qr), nby), cp(3, mine(0), at(me, c, 0, qr), nby),
                cp(4, at(jx, c, 0, qr), at(jx, c, 0, qr), nby), cp(5, at(jy, c, 1, qr), at(jy, c, 1, qr), nbx),
                cp(6, at(jx, c, 0, hr), at(jx, c, 0, hr), sib), cp(7, at(jy, c, 0, hr), at(jy, c, 0, hr), sib),
                cp(8, at(jd, c, 0, hr), at(jd, c, 0, hr), sib),
                cp(9, srcs[w], _whole_shard(kind, news[w], me, r, cc), sib)])
        return cps

    def start(self, srcs, dsts, news, sems):
        cps = self._copies(srcs, news, sems)
        for w in range(len(self.mats)):
            for k in (0, 1, 2, 3, 9):
                cps[w][k].start()

    def mid(self, srcs, dsts, news, sems):
        cps = self._copies(srcs, news, sems)
        for w in range(len(self.mats)):
            cps[w][0].wait_recv()
            cps[w][4].start()
            cps[w][2].wait_recv()
            cps[w][5].start()

    def finish(self, srcs, dsts, news, sems):
        cps = self._copies(srcs, news, sems)
        for w in range(len(self.mats)):
            cps[w][1].wait_recv()
            cps[w][6].start()
            cps[w][3].wait_recv()
            cps[w][7].start()
        for w in range(len(self.mats)):
            cps[w][4].wait_recv()
            cps[w][5].wait_recv()
            cps[w][8].start()
        for w in range(len(self.mats)):
            for k in (6, 7, 8, 9):
                cps[w][k].wait_recv()
            for k in range(self.NCP):
                cps[w][k].wait_send()

    def done(self, dsts, news):
        for n, a in zip(self.names, news):
            self.sink[n] = a


class _SwapJob(_Job):
    def __init__(self, build, ncopies, *, srcs=(), dsts=(), news=(), done=None):
        self.build, self.srcs, self.dsts, self.news, self._done = build, list(srcs), list(dsts), list(news), done
        self.scratch = [pltpu.SemaphoreType.DMA((ncopies,)), pltpu.SemaphoreType.DMA((ncopies,))]

    def start(self, srcs, dsts, news, sems):
        for cp in self.build(srcs, dsts, news, *sems):
            cp.start()

    def finish(self, srcs, dsts, news, sems):
        for cp in self.build(srcs, dsts, news, *sems):
            cp.wait()

    def done(self, dsts, news):
        if self._done is not None:
            self._done(dsts, news)


def _half_of_whole(kind, ref, h, r, c):
    if kind == "row":
        return ref.at[:, :, pl.ds(pl.multiple_of(h * (c // 2), 128), c // 2)]
    return ref.at[:, pl.ds(h * (r // 2), r // 2), :]


def _half_shape(kind, g, r, c):
    return {"row": (g, NCHIP * r, c // 2), "col": (g, r // 2, NCHIP * c), "stk": (NCHIP, r // 2, c)}[kind]


def _piece_shape(kind, g, r, c):
    return {"row": (g, r, c // 2), "col": (g, r // 2, c), "stk": (1, r // 2, c)}[kind]


def _piece_of_half(kind, ref, j, r, c):
    if kind == "row":
        return ref.at[:, pl.ds(j * r, r), :]
    if kind == "col":
        return ref.at[:, :, _cols(j, c)]
    return ref.at[pl.ds(j, 1)]


def _half_of_shard(kind, ref, h, r, c):
    if kind == "row":
        return ref.at[:, :, pl.ds(pl.multiple_of(h * (c // 2), 128), c // 2)]
    return ref.at[:, pl.ds(h * (r // 2), r // 2), :]


def _pair_sum(pack, core, mine, got, whole=True):
    name, kind, g, r, c, tr, _ = pack
    hs = _half_shape(kind, g, r, c)
    nb = hs[1] // tr

    def body(core_ref, a_ref, b_ref, o_ref):
        o_ref[...] = (a_ref[...].astype(F32) + b_ref[...].astype(F32)).astype(BF16)

    blk = (1, tr, hs[2])
    same = lambda gi, i, core_ref: (gi, i, 0)
    if not whole:
        a_map = same
    elif kind == "row":
        a_map = lambda gi, i, core_ref: (gi, i, core_ref[0])
    else:
        a_map = lambda gi, i, core_ref: (gi, core_ref[0] * nb + i, 0)
    return pl.pallas_call(
        body, name="pair_sum_" + name, out_shape=jax.ShapeDtypeStruct(hs, BF16),
        grid_spec=pltpu.PrefetchScalarGridSpec(
            num_scalar_prefetch=1, grid=(hs[0], nb),
            in_specs=[pl.BlockSpec(blk, a_map), pl.BlockSpec(blk, same)], out_specs=pl.BlockSpec(blk, same)),
        compiler_params=_cp(("parallel", "parallel")),
    )(core, mine, got)


def _shard_sum(pack, where, half, got):
    name, kind, g, r, c, _, tr = pack
    ps = _piece_shape(kind, g, r, c)
    nb = ps[1] // tr

    def body(where_ref, a_ref, b_ref, o_ref):
        o_ref[...] = a_ref[...].astype(F32) + ((b_ref[0].astype(F32) + b_ref[1].astype(F32)) + b_ref[2].astype(F32))

    blk = (1, tr, ps[2])
    if kind == "row":
        a_map = lambda gi, i, wr: (gi, wr[0] * nb + i, 0)
        o_map = lambda gi, i, wr: (gi, i, wr[1])
    elif kind == "col":
        a_map = lambda gi, i, wr: (gi, i, wr[0])
        o_map = lambda gi, i, wr: (gi, wr[1] * nb + i, 0)
    else:
        a_map = lambda gi, i, wr: (wr[0], i, 0)
        o_map = lambda gi, i, wr: (gi, wr[1] * nb + i, 0)
    return pl.pallas_call(
        body, name="shard_sum_" + name, out_shape=jax.ShapeDtypeStruct((g, r, c), F32),
        grid_spec=pltpu.PrefetchScalarGridSpec(
            num_scalar_prefetch=1, grid=(ps[0], nb),
            in_specs=[pl.BlockSpec(blk, a_map), pl.BlockSpec((3,) + blk, lambda gi, i, wr: (0, gi, i, 0))],
            out_specs=pl.BlockSpec(blk, o_map)),
        compiler_params=_cp(("parallel", "parallel")),
    )(where, half, got)


class _Plan:
    def __init__(self, shards, table):
        self.shards, self.table = shards, table
        self.whole, self.grad, self.got_a, self.half, self.got_b, self.sent_b, self.gshard = {}, {}, {}, {}, {}, {}, {}
        x, y, c = _pos()
        self.core = c
        self.core1 = c.reshape(1).astype(jnp.int32)
        self.where = jnp.stack([2 * x + y, c]).astype(jnp.int32)
        self._w_in = None
        self.send, self.keep = {}, {}

    def w(self, n):
        if n != "w_in":
            return self.whole[n][0]
        if self._w_in is None:
            self._w_in = _slabs_to_kernel_cols(self.whole[n], name="relayout_w_in", jobs=self.jobs("relayout_w_in"))
        return self._w_in

    def g(self, n, a):
        self.grad[n] = a[None]

    def g_half(self, n, which, a):
        (self.send if which == "send" else self.keep)[n] = _kernel_cols_to_slabs(a, name="relayout_d_in_" + which)

    def jobs(self, tag):
        out = []
        for spec in self.table.get(tag, ()):
            out += getattr(self, "_" + spec[0])(*spec[1:])
        return out

    def run(self, name, jobs):
        if jobs:
            _call(lambda: None, jobs=jobs, name=name, out_shape=[], in_specs=[], out_specs=[])()

    def _gather(self, names):
        return [_GatherJob(names, self.shards, self.whole)]

    def _rs_a(self, names):
        mats = [MATS[n] for n in names]

        def build(srcs, dsts, news, send, recv):
            x, y, c = _pos()
            return [_remote(srcs[i] if names[i] in self.send else _half_of_whole(kind, srcs[i], 1 - c, r, cc), news[i],
                            send.at[i], recv.at[i], (x, y, 1 - c))
                    for i, (_, kind, g, r, cc, _, _) in enumerate(mats)]

        def done(dsts, news):
            self.got_a.update(zip(names, news))

        return [_SwapJob(build, len(names), srcs=[self.send.get(n, self.grad.get(n)) for n in names], done=done,
                         news=[jax.ShapeDtypeStruct(_half_shape(kind, g, r, c), BF16) for _, kind, g, r, c, _, _ in mats])]

    def _rs_b(self, names, ks=(0, 1, 2)):
        return [self._rs_b_one(n, ks) for n in names]

    def _rs_b_one(self, n, ks):
        _, kind, g, r, cc, _, _ = MATS[n]
        if n not in self.half:
            if n in self.keep:
                self.half[n] = _pair_sum(MATS[n], self.core1, self.keep[n], self.got_a[n], whole=False)
            else:
                self.half[n] = _pair_sum(MATS[n], self.core1, self.grad[n], self.got_a[n])

        def build(srcs, dsts, news, send, recv):
            x, y, c = _pos()
            land = (dsts or news)[0]
            cps = []
            for i, k in enumerate(ks):
                px, py = _flip(x, RELS[k][0]), _flip(y, RELS[k][1])
                cps.append(_remote(_piece_of_half(kind, srcs[0], 2 * px + py, r, cc), land.at[k], send.at[i], recv.at[i],
                                   (px, py, c)))
            return cps

        def done(dsts, news):
            self.got_b[n] = (dsts or news)[0]
            self.sent_b[n] = self.sent_b.get(n, ()) + tuple(ks)

        if n in self.got_b:
            return _SwapJob(build, len(ks), srcs=[self.half[n]], dsts=[self.got_b[n]], done=done)
        shape = jax.ShapeDtypeStruct((3,) + _piece_shape(kind, g, r, cc), BF16)
        return _SwapJob(build, len(ks), srcs=[self.half[n]], news=[shape], done=done)

    def _rs_c(self, names):
        mats = [MATS[n] for n in names]
        for n in names:
            assert sorted(self.sent_b[n]) == [0, 1, 2], (n, self.sent_b[n])
        parts = [_shard_sum(MATS[n], self.where, self.half[n], self.got_b[n]) for n in names]

        def build(srcs, dsts, news, send, recv):
            x, y, c = _pos()
            cps = []
            for i, (_, kind, g, r, cc, _, _) in enumerate(mats):
                mine = _half_of_shard(kind, dsts[i], c, r, cc)
                cps.append(_remote(mine, mine, send.at[i], recv.at[i], (x, y, 1 - c)))
            return cps

        def done(dsts, news):
            self.gshard.update(zip(names, dsts))

        return [_SwapJob(build, len(names), dsts=parts, done=done)]

    def finish(self, n):
        if n not in self.got_a:
            self.run("rs_a_" + n, self._rs_a((n,)))
        left = tuple(k for k in range(3) if k not in self.sent_b.get(n, ()))
        if left:
            self.run("rs_b_" + n, self._rs_b((n,), left))
        if n not in self.gshard:
            self.run("rs_c_" + n, self._rs_c((n,)))
        return self.gshard[n]


TABLE = {
    "gather_w_in": (("gather", ("w_in",)),),
    "relayout_w_in": (("gather", ("w_gate",)),),
    "mm_in": (("gather", ("w_up",)),),
    "attn_fwd": (("gather", ("w_attn_br", "w_ssd_br")),),
    "ssd_fwd": (("gather", ("w_o",)),),
    "swiglu_fwd": (("gather", ("w_down",)),),
    "mm_down": (("gather", ("w_ple_gate", "w_ple_proj")),),
    "mm_de": (("rs_a", ("w_ple_proj", "w_ple_gate")),),
    "mm_d_down": (("rs_b", ("w_ple_proj", "w_ple_gate")),),
    "mm_dact": (("rs_a", ("w_down",)),),
    "swiglu_bwd": (("rs_c", ("w_ple_proj", "w_ple_gate")),),
    "mm_df_gate": (("rs_a", ("w_gate", "w_up")),),
    "mm_dmerged": (("rs_a", ("w_o",)),),
    "mm_dyn": (("rs_a", ("w_attn_br", "w_ssd_br")),),
    "attn_bwd": (("rs_b", ("w_down",)),),
    "gated_norm_bwd": (("rs_c", ("w_down",)),),
    "ssd_bwd": (("rs_b", ("w_gate", "w_up")),),
    "conv_bwd": (("rs_b", ("w_o",)),),
    "mm_d_in_send": (("rs_b", ("w_attn_br", "w_ssd_br")), ("rs_c", ("w_gate", "w_up"))),
    "mm_d_in_keep": (("rs_a", ("w_in",)), ("rs_c", ("w_o",))),
    "mm_du": (("rs_b", ("w_in",)),),
    "norm_mix_bwd": (("rs_c", ("w_attn_br", "w_ssd_br")),),
}


NDEV = 8


def _allreduce_small(v, *, name):
    rows = v.shape[0]

    def body(v_ref, o_ref, slots, send, recv):
        x, y, c = _pos()
        me = 4 * x + 2 * y + c
        slots[me] = v_ref[...]
        cps = []
        for k in range(1, NDEV):
            peer = (_flip(x, k & 4), _flip(y, k & 2), _flip(c, k & 1))
            cp = _remote(v_ref, slots.at[me], send.at[k - 1], recv.at[k - 1], peer)
            cp.start()
            cps.append(cp)
        for cp in cps:
            cp.wait()
        acc = slots[0]
        for s in range(1, NDEV):
            acc = acc + slots[s]
        o_ref[...] = acc

    return pl.pallas_call(
        body, name=name, out_shape=jax.ShapeDtypeStruct((rows, 128), F32),
        in_specs=[pl.BlockSpec(memory_space=pltpu.VMEM)], out_specs=pl.BlockSpec(memory_space=pltpu.VMEM),
        scratch_shapes=[pltpu.VMEM((NDEV, rows, 128), F32), pltpu.SemaphoreType.DMA((NDEV - 1,)),
                        pltpu.SemaphoreType.DMA((NDEV - 1,))],
    )(v)


def _adamw(w, g, m, v, *, name, tr=None, tc=None, jobs=()):
    r, c = w.shape
    tr = r if tr is None else tr
    c1 = 1.0 / (1.0 - B1 ** STEP)
    c2 = 1.0 / (1.0 - B2 ** STEP)

    def body(w_ref, g_ref, m_ref, v_ref, d_ref, mo_ref, vo_ref):
        gv = g_ref[...]
        mn = B1 * m_ref[...] + (1.0 - B1) * gv
        vn = B2 * v_ref[...] + (1.0 - B2) * (gv * gv)
        mo_ref[...] = mn
        vo_ref[...] = vn
        d_ref[...] = -LR * ((mn * c1) / (jnp.sqrt(vn * c2) + AEPS) + WD * w_ref[...])

    if tc is None:
        blk, grid = pl.BlockSpec((tr, c), lambda i: (i, 0)), (r // tr,)
    else:
        blk, grid = pl.BlockSpec((r, tc), lambda i: (0, i)), (c // tc,)
    o = jax.ShapeDtypeStruct((r, c), F32)
    return _call(
        body, jobs=jobs, name=name, out_shape=(o, o, o), grid=grid, in_specs=[blk] * 4, out_specs=(blk, blk, blk),
        compiler_params=_cp(("parallel",)),
    )(w, g, m, v)


WEIGHTS = ("g_mix", "w_in", "conv_w", "conv_b", "dt_bias", "a_log", "d_skip", "g_ssd", "sinks", "w_attn_br", "w_ssd_br",
           "w_o", "g_ffn", "w_gate", "w_up", "w_down", "g_ple", "w_ple_gate", "w_ple_proj", "g_final")
BIG = {
    "w_gate": 256, "w_up": 256, "w_down": 128, "w_ssd_br": 128, "w_o": 128, "w_ple_gate": 128, "w_attn_br": 256,
    "w_ple_proj": 256, "w_in": None,
}
SMALL = tuple(n for n in WEIGHTS if n not in BIG)


def _pack_small(parts):
    rows = []
    for a in parts:
        a = a.reshape(-1)
        rows.append(jnp.pad(a, (0, -a.shape[0] % 128)).reshape(-1, 128))
    out = jnp.concatenate(rows, axis=0)
    return jnp.pad(out, ((0, -out.shape[0] % 8), (0, 0)))


def _unpack_small(packed, shapes):
    out, r = [], 0
    for s in shapes:
        n = int(np.prod(s))
        nr = -(-n // 128)
        out.append(packed[r:r + nr].reshape(-1)[:n].reshape(s))
        r += nr
    return out


def kernel(x, p, positions, g_mix, w_in, conv_w, conv_b, dt_bias, a_log, d_skip, g_ssd, sinks, w_attn_br, w_ssd_br, w_o, g_ffn, w_gate, w_up, w_down, g_ple, w_ple_gate, w_ple_proj, g_final, loss_target, m_g_mix, m_w_in, m_conv_w, m_conv_b, m_dt_bias, m_a_log, m_d_skip, m_g_ssd, m_sinks, m_w_attn_br, m_w_ssd_br, m_w_o, m_g_ffn, m_w_gate, m_w_up, m_w_down, m_g_ple, m_w_ple_gate, m_w_ple_proj, m_g_final, v_g_mix, v_w_in, v_conv_w, v_conv_b, v_dt_bias, v_a_log, v_d_skip, v_g_ssd, v_sinks, v_w_attn_br, v_w_ssd_br, v_w_o, v_g_ffn, v_w_gate, v_w_up, v_w_down, v_g_ple, v_w_ple_gate, v_w_ple_proj, v_g_final):
    w = dict(zip(WEIGHTS, (g_mix, w_in, conv_w, conv_b, dt_bias, a_log, d_skip, g_ssd, sinks, w_attn_br, w_ssd_br, w_o,
                           g_ffn, w_gate, w_up, w_down, g_ple, w_ple_gate, w_ple_proj, g_final)))
    m = dict(zip(WEIGHTS, (m_g_mix, m_w_in, m_conv_w, m_conv_b, m_dt_bias, m_a_log, m_d_skip, m_g_ssd, m_sinks, m_w_attn_br,
                           m_w_ssd_br, m_w_o, m_g_ffn, m_w_gate, m_w_up, m_w_down, m_g_ple, m_w_ple_gate, m_w_ple_proj,
                           m_g_final)))
    v = dict(zip(WEIGHTS, (v_g_mix, v_w_in, v_conv_w, v_conv_b, v_dt_bias, v_a_log, v_d_skip, v_g_ssd, v_sinks, v_w_attn_br,
                           v_w_ssd_br, v_w_o, v_g_ffn, v_w_gate, v_w_up, v_w_down, v_g_ple, v_w_ple_gate, v_w_ple_proj,
                           v_g_final)))
    xi, yi, ci = _pos()
    chip = 2 * xi + yi
    t = x.shape[1]
    cshard = CONV // NCHIP

    shards = {n: w[n].astype(BF16) for n in MATS}
    shards["w_in"] = jnp.pad(shards["w_in"], ((0, 0), (0, 0), (0, SLAB_PAD - SLAB)))
    plan = _Plan(shards, TABLE)
    plan.run("gather_w_in", plan.jobs("gather_w_in"))
    placed = lax.dynamic_update_slice(jnp.zeros((CW, CONV), F32), w["conv_w"][0], (0, chip * cshard))
    conv_whole = _allreduce_small(jnp.where(ci == 0, placed, 0.0).reshape(-1, 128), name="gather_conv_w").reshape(CW, CONV)

    small = {n: w[n] for n in ("g_mix", "conv_b", "dt_bias", "a_log", "d_skip", "g_ssd", "sinks", "g_ffn", "g_ple", "g_final")}
    small["conv_w"] = conv_whole
    loss8, grad_x, gs = _local_step(x[0], p[0, 0], positions, loss_target[0], small, plan)

    order = ("g_mix", "conv_b", "dt_bias", "a_log", "d_skip", "g_ssd", "sinks", "g_ffn", "g_ple", "g_final", "conv_w")
    summed = _allreduce_small(_pack_small([loss8[0, :1]] + [gs[n] for n in order]), name="sum_small")
    parts = _unpack_small(summed, [(1,)] + [w[n].shape for n in order[:-1]] + [(CW, CONV)])
    loss = parts[0][0]
    grad = dict(zip(order, parts[1:]))
    grad["conv_w"] = lax.dynamic_slice(grad["conv_w"], (0, chip * cshard), (CW, cshard))[None]

    delta, new_m, new_v = {}, {}, {}
    for n, tr in BIG.items():
        grad[n] = plan.finish(n)[:, :, :w[n].shape[2]]
        if n == "w_in":
            d_, m_, v_ = _adamw(w[n][0].T, grad[n][0].T, m[n][0].T, v[n][0].T, tc=128, name="adamw_" + n)
            d_, m_, v_ = d_.T, m_.T, v_.T
        else:
            d_, m_, v_ = _adamw(w[n][0], grad[n][0], m[n][0], v[n][0], tr=tr, name="adamw_" + n)
        delta[n], new_m[n], new_v[n] = d_[None], m_[None], v_[None]
    shapes = [w[n].shape for n in SMALL]
    d_, m_, v_ = _adamw(_pack_small([w[n] for n in SMALL]), _pack_small([grad[n] for n in SMALL]),
                        _pack_small([m[n] for n in SMALL]), _pack_small([v[n] for n in SMALL]), tr=None, name="adamw_small")
    for n, a, b, c_ in zip(SMALL, _unpack_small(d_, shapes), _unpack_small(m_, shapes), _unpack_small(v_, shapes)):
        delta[n], new_m[n], new_v[n] = a, b, c_

    return (loss, grad_x[None], *[grad[n] for n in WEIGHTS], *[delta[n] for n in WEIGHTS],
            *[new_m[n] for n in WEIGHTS], *[new_v[n] for n in WEIGHTS])
```

```python
import functools

import jax
import jax.numpy as jnp
import numpy as np
from jax import lax
from jax.experimental import pallas as pl
from jax.experimental.pallas import tpu as pltpu

F32 = jnp.float32
BF16 = jnp.bfloat16
MESH = pl.DeviceIdType.MESH

D = 2048
HD = 64
NQH = 16
NKV = 4
QD = NQH * HD
KVD = NKV * HD
DI = 2048
NH = 32
NG = 4
NS = 128
CW = 4
L = 128
CONV = DI + 2 * NG * NS
FFN = 5632
PLE = 256
IN_DIM = QD + 2 * KVD + DI + CONV + NH + 2 * D
EPS = 1e-6
SSM_EPS = 1e-5
ROPE_THETA = 10000.0
LR, B1, B2, AEPS, WD, STEP = 0.001, 0.9, 0.999, 1e-08, 0.01, 10

O_GA, O_GS, O_Z, O_XBC, O_Q, O_K, O_V, O_DT = 0, 2048, 4096, 6144, 9216, 10240, 10496, 10752
DT_PAD = 512
NP = O_DT + DT_PAD
R_Q, R_K, R_V, R_Z, R_XBC, R_DT, R_GA, R_GS = 0, 1024, 1280, 1536, 3584, 6656, 6688, 8736

NCHIP = 4
VMEM_LIMIT = 52 * 1024 * 1024
NEG = -1e30


def _cp(sem=None):
    return pltpu.CompilerParams(dimension_semantics=sem, vmem_limit_bytes=VMEM_LIMIT)


def _dot(a, b):
    return lax.dot_general(a, b, (((1,), (0,)), ((), ())), preferred_element_type=F32)


def _dot_nt(a, b):
    return lax.dot_general(a, b, (((1,), (1,)), ((), ())), preferred_element_type=F32)


def _dot_tn(a, b):
    return lax.dot_general(a, b, (((0,), (0,)), ((), ())), preferred_element_type=F32)


def _sigmoid(x):
    return 1.0 / (1.0 + jnp.exp(-x))


def _bf16_dot(dot, da, db):
    @jax.custom_vjp
    def f(a, b):
        return dot(a.astype(BF16), b.astype(BF16))

    def fwd(a, b):
        return f(a, b), (a.astype(BF16), b.astype(BF16))

    def bwd(res, g):
        a, b = res
        g = g.astype(BF16)
        return da(g, a, b), db(g, a, b)

    f.defvjp(fwd, bwd)
    return f


_bdot = _bf16_dot(_dot, lambda g, a, b: _dot_nt(g, b), lambda g, a, b: _dot_tn(a, g))
_bdot_nt = _bf16_dot(_dot_nt, lambda g, a, b: _dot(g, b), lambda g, a, b: _dot_tn(g, a))
_bdot_tn = _bf16_dot(_dot_tn, lambda g, a, b: _dot_nt(b, g), lambda g, a, b: _dot(a, g))


ANY = pl.BlockSpec(memory_space=pl.ANY)


class _Job:
    srcs, dsts, news, scratch = (), (), (), ()
    has_mid = False

    def start(self, srcs, dsts, news, sems):
        raise NotImplementedError

    def mid(self, srcs, dsts, news, sems):
        pass

    def finish(self, srcs, dsts, news, sems):
        raise NotImplementedError

    def done(self, dsts, news):
        pass


def _call(body, *, jobs=(), name, out_shape, in_specs, out_specs, grid=(), scratch_shapes=(), compiler_params=None,
          aliases=None):
    jobs = [j for j in jobs if j is not None]
    aliases = dict(aliases or {})
    if not jobs:
        return pl.pallas_call(body, name=name, out_shape=out_shape, in_specs=in_specs, out_specs=out_specs, grid=grid,
                              scratch_shapes=scratch_shapes, compiler_params=compiler_params,
                              input_output_aliases=aliases)
    single = not isinstance(out_shape, (tuple, list))
    outs = [out_shape] if single else list(out_shape)
    ospecs = [out_specs] if single else list(out_specs)
    n_in, n_out, n_scr = len(in_specs), len(outs), len(scratch_shapes)
    srcs = [a for j in jobs for a in j.srcs]
    dsts = [a for j in jobs for a in j.dsts]
    news = [a for j in jobs for a in j.news]
    sems = [a for j in jobs for a in j.scratch]

    def wrapped(*refs):
        pos = n_in + len(srcs) + len(dsts)
        ins, jsrc = refs[:n_in], refs[n_in:n_in + len(srcs)]
        o_refs = refs[pos:pos + n_out]
        pos += n_out
        jdst, jnew = refs[pos:pos + len(dsts)], refs[pos + len(dsts):pos + len(dsts) + len(news)]
        pos += len(dsts) + len(news)
        scr, jsem = refs[pos:pos + n_scr], refs[pos + n_scr:]

        def run(which):
            a = b = c = d = 0
            for j in jobs:
                getattr(j, which)(jsrc[a:a + len(j.srcs)], jdst[b:b + len(j.dsts)], jnew[c:c + len(j.news)],
                                  jsem[d:d + len(j.scratch)])
                a, b, c, d = a + len(j.srcs), b + len(j.dsts), c + len(j.news), d + len(j.scratch)

        if not grid:
            run("start")
            run("mid")
            body(*ins, *o_refs, *scr)
            run("finish")
            return
        step = functools.reduce(lambda acc, a: acc * grid[a] + pl.program_id(a), range(len(grid)), 0)
        steps = int(np.prod(grid))
        pl.when(step == 0)(lambda: run("start"))
        if any(j.has_mid for j in jobs):
            pl.when(step == steps // 3)(lambda: run("mid"))
        body(*ins, *o_refs, *scr)
        pl.when(step == steps - 1)(lambda: run("finish"))

    call = pl.pallas_call(
        wrapped, name=name,
        out_shape=outs + [jax.ShapeDtypeStruct(a.shape, a.dtype) for a in dsts] + news,
        in_specs=list(in_specs) + [ANY] * (len(srcs) + len(dsts)),
        out_specs=ospecs + [ANY] * (len(dsts) + len(news)),
        grid=grid, scratch_shapes=list(scratch_shapes) + sems,
        input_output_aliases={**aliases, **{n_in + len(srcs) + i: n_out + i for i in range(len(dsts))}},
        compiler_params=_cp(("arbitrary",) * len(grid) if grid else None))

    def run_call(*args):
        res = call(*args, *srcs, *dsts)
        b, c = n_out, n_out + len(dsts)
        for j in jobs:
            j.done(res[b:b + len(j.dsts)], res[c:c + len(j.news)])
            b, c = b + len(j.dsts), c + len(j.news)
        return res[0] if single else tuple(res[:n_out])

    return run_call


def _matmul(a, b, *, ta=False, tb=False, out_dtype=F32, add=None, tm, tn, tk, name, jobs=()):
    k, m = a.shape if ta else a.shape[::-1]
    n = b.shape[0] if tb else b.shape[1]
    assert (b.shape[1] if tb else b.shape[0]) == k and not (ta and tb)
    assert m % tm == 0 and n % tn == 0 and k % tk == 0, (name, a.shape, b.shape)
    nk = k // tk
    has_add = add is not None

    def body(*refs):
        a_ref, b_ref = refs[0], refs[1]
        add_ref = refs[2] if has_add else None
        o_ref = refs[3] if has_add else refs[2]
        av = a_ref[...].astype(BF16)
        bv = b_ref[...].astype(BF16)
        part = _dot_tn(av, bv) if ta else _dot_nt(av, bv) if tb else _dot(av, bv)

        def finish(r):
            if has_add:
                r = r + add_ref[...]
            o_ref[...] = r.astype(out_dtype)

        if nk == 1:
            finish(part)
        elif out_dtype == F32:
            kk = pl.program_id(2)
            pl.when(kk == 0)(lambda: finish(part))

            @pl.when(kk > 0)
            def _():
                o_ref[...] += part
        else:
            acc_ref = refs[-1]
            kk = pl.program_id(2)

            @pl.when(kk == 0)
            def _():
                acc_ref[...] = part

            @pl.when(kk > 0)
            def _():
                acc_ref[...] += part

            @pl.when(kk == nk - 1)
            def _():
                finish(acc_ref[...])

    in_specs = [pl.BlockSpec((tk, tm), lambda i, j, kk: (kk, i)) if ta else pl.BlockSpec((tm, tk), lambda i, j, kk: (i, kk)),
                pl.BlockSpec((tn, tk), lambda i, j, kk: (j, kk)) if tb
                else pl.BlockSpec((tk, tn), lambda i, j, kk: (kk, j))]
    args = [a, b]
    if has_add:
        in_specs.append(pl.BlockSpec((tm, tn), lambda i, j, kk: (i, j)))
        args.append(add)
    return _call(
        body, jobs=jobs, name=name,
        out_shape=jax.ShapeDtypeStruct((m, n), out_dtype),
        grid=(m // tm, n // tn, nk),
        in_specs=in_specs,
        out_specs=pl.BlockSpec((tm, tn), lambda i, j, kk: (i, j)),
        scratch_shapes=[pltpu.VMEM((tm, tn), F32)] if nk > 1 and out_dtype != F32 else [],
        compiler_params=_cp(("parallel", "parallel", "arbitrary")),
    )(*args)


ROWS = 256


def _rmsnorm_fwd(x, g, *, name):
    t, d = x.shape

    def body(x_ref, g_ref, o_ref):
        xv = x_ref[...]
        r = lax.rsqrt(jnp.mean(xv * xv, axis=-1, keepdims=True) + EPS)
        o_ref[...] = (xv * r * g_ref[...]).astype(BF16)

    return pl.pallas_call(
        body, name=name, out_shape=jax.ShapeDtypeStruct((t, d), BF16), grid=(t // ROWS,),
        in_specs=[pl.BlockSpec((ROWS, d), lambda i: (i, 0)), pl.BlockSpec((1, d), lambda i: (0, 0))],
        out_specs=pl.BlockSpec((ROWS, d), lambda i: (i, 0)), compiler_params=_cp(("parallel",)),
    )(x, g)


def _rmsnorm_bwd(x, g, dy, dres, *, name, jobs=()):
    t, d = x.shape

    def body(x_ref, g_ref, dy_ref, dres_ref, dx_ref, dxb_ref, dg_ref):
        xv = x_ref[...]
        r = lax.rsqrt(jnp.mean(xv * xv, axis=-1, keepdims=True) + EPS)
        xh = xv * r
        dyv = dy_ref[...]
        dxh = dyv * g_ref[...]
        dx = r * (dxh - xh * jnp.mean(dxh * xh, axis=-1, keepdims=True))
        tot = dres_ref[...] + dx
        dx_ref[...] = tot
        dxb_ref[...] = tot.astype(BF16)

        @pl.when(pl.program_id(0) == 0)
        def _():
            dg_ref[...] = jnp.zeros_like(dg_ref)

        dg_ref[...] += jnp.broadcast_to(jnp.sum(dyv * xh, axis=0, keepdims=True), dg_ref.shape)

    row = pl.BlockSpec((ROWS, d), lambda i: (i, 0))
    return _call(
        body, jobs=jobs, name=name,
        out_shape=(jax.ShapeDtypeStruct((t, d), F32), jax.ShapeDtypeStruct((t, d), BF16),
                   jax.ShapeDtypeStruct((8, d), F32)),
        grid=(t // ROWS,),
        in_specs=[row, pl.BlockSpec((1, d), lambda i: (0, 0)), row, row],
        out_specs=(row, row, pl.BlockSpec((8, d), lambda i: (0, 0))),
        compiler_params=_cp(("arbitrary",)),
    )(x, g, dy, dres)


def _final(h2, pgl, pp, target, g_final, *, name):
    t, d = h2.shape

    def body(h2_ref, pgl_ref, pp_ref, tg_ref, g_ref, dh3_ref, dpgl_ref, dpp_ref, loss_ref, dg_ref):
        s = _sigmoid(pgl_ref[...])
        ppv = pp_ref[...]
        h3 = h2_ref[...] + s * ppv
        r = lax.rsqrt(jnp.mean(h3 * h3, axis=-1, keepdims=True) + EPS)
        xh = h3 * r
        gv = g_ref[...]
        err = xh * gv - tg_ref[...]
        dyv = err * (1.0 / d)
        dxh = dyv * gv
        dh3 = r * (dxh - xh * jnp.mean(dxh * xh, axis=-1, keepdims=True))
        dh3_ref[...] = dh3
        dpp_ref[...] = (dh3 * s).astype(BF16)
        dpgl_ref[...] = (dh3 * ppv * s * (1.0 - s)).astype(BF16)

        @pl.when(pl.program_id(0) == 0)
        def _():
            loss_ref[...] = jnp.zeros_like(loss_ref)
            dg_ref[...] = jnp.zeros_like(dg_ref)

        part = 0.5 * jnp.sum(jnp.mean(err * err, axis=-1, keepdims=True), axis=0, keepdims=True)
        loss_ref[...] += jnp.broadcast_to(part, loss_ref.shape)
        dg_ref[...] += jnp.broadcast_to(jnp.sum(dyv * xh, axis=0, keepdims=True), dg_ref.shape)

    row = pl.BlockSpec((ROWS, d), lambda i: (i, 0))
    return pl.pallas_call(
        body, name=name,
        out_shape=(jax.ShapeDtypeStruct((t, d), F32), jax.ShapeDtypeStruct((t, d), BF16),
                   jax.ShapeDtypeStruct((t, d), BF16), jax.ShapeDtypeStruct((8, 128), F32),
                   jax.ShapeDtypeStruct((8, d), F32)),
        grid=(t // ROWS,),
        in_specs=[row, row, row, row, pl.BlockSpec((1, d), lambda i: (0, 0))],
        out_specs=(row, row, row, pl.BlockSpec((8, 128), lambda i: (0, 0)), pl.BlockSpec((8, d), lambda i: (0, 0))),
        compiler_params=_cp(("arbitrary",)),
    )(h2, pgl, pp, target, g_final)


def _merge_fwd(proj, out_a, out_s, *, name):
    t = proj.shape[0]

    def body(ga_ref, gs_ref, a_ref, s_ref, o_ref):
        o_ref[...] = (_sigmoid(ga_ref[...]) * a_ref[...] + _sigmoid(gs_ref[...]) * s_ref[...]).astype(BF16)

    row = pl.BlockSpec((ROWS, D), lambda i: (i, 0))
    return pl.pallas_call(
        body, name=name, out_shape=jax.ShapeDtypeStruct((t, D), BF16), grid=(t // ROWS,),
        in_specs=[pl.BlockSpec((ROWS, D), lambda i: (i, O_GA // D)), pl.BlockSpec((ROWS, D), lambda i: (i, O_GS // D)),
                  row, row],
        out_specs=row, compiler_params=_cp(("parallel",)),
    )(proj, proj, out_a, out_s)


def _merge_bwd(proj, out_a, out_s, dmerged, *, name):
    t = proj.shape[0]
    assert O_GA == 0 and O_GS == D

    def body(ga_ref, gs_ref, a_ref, s_ref, dm_ref, da_ref, ds_ref, dp_ref):
        sa = _sigmoid(ga_ref[...])
        ss = _sigmoid(gs_ref[...])
        dm = dm_ref[...]
        da_ref[...] = (dm * sa).astype(BF16)
        ds_ref[...] = (dm * ss).astype(BF16)
        dp_ref[:, :D] = (dm * a_ref[...] * sa * (1.0 - sa)).astype(BF16)
        dp_ref[:, D:] = (dm * s_ref[...] * ss * (1.0 - ss)).astype(BF16)

    row = pl.BlockSpec((ROWS, D), lambda i: (i, 0))
    o = jax.ShapeDtypeStruct((t, D), BF16)
    return pl.pallas_call(
        body, name=name, out_shape=(o, o, jax.ShapeDtypeStruct((t, NP), BF16)), grid=(t // ROWS,),
        in_specs=[pl.BlockSpec((ROWS, D), lambda i: (i, O_GA // D)), pl.BlockSpec((ROWS, D), lambda i: (i, O_GS // D)),
                  row, row, row],
        out_specs=(row, row, pl.BlockSpec((ROWS, 2 * D), lambda i: (i, 0))), compiler_params=_cp(("parallel",)),
    )(proj, proj, out_a, out_s, dmerged)


def _swiglu_fwd(f, w_gate, w_up, *, name, tn=512, jobs=()):
    t, d = f.shape
    n = w_gate.shape[1]

    def body(f_ref, wg_ref, wu_ref, g_ref, u_ref, a_ref):
        fv = f_ref[...]
        g = _dot(fv, wg_ref[...])
        u = _dot(fv, wu_ref[...])
        g_ref[...] = g.astype(BF16)
        u_ref[...] = u.astype(BF16)
        a_ref[...] = (g * _sigmoid(g) * u).astype(BF16)

    col = pl.BlockSpec((t, tn), lambda j: (0, j))
    wcol = pl.BlockSpec((d, tn), lambda j: (0, j))
    return _call(
        body, jobs=jobs, name=name,
        out_shape=(jax.ShapeDtypeStruct((t, n), BF16), jax.ShapeDtypeStruct((t, n), BF16),
                   jax.ShapeDtypeStruct((t, n), BF16)),
        grid=(n // tn,),
        in_specs=[pl.BlockSpec((t, d), lambda j: (0, 0)), wcol, wcol],
        out_specs=(col, col, col), compiler_params=_cp(("parallel",)),
    )(f, w_gate, w_up)


def _swiglu_bwd(gate, up, dact, *, name, tc=1408, jobs=()):
    t, n = gate.shape

    def body(g_ref, u_ref, da_ref, dg_ref, du_ref):
        g = g_ref[...].astype(F32)
        s = _sigmoid(g)
        da = da_ref[...]
        du_ref[...] = (da * g * s).astype(BF16)
        dg_ref[...] = (da * u_ref[...].astype(F32) * s * (1.0 + g * (1.0 - s))).astype(BF16)

    blk = pl.BlockSpec((ROWS, tc), lambda i, j: (i, j))
    o = jax.ShapeDtypeStruct((t, n), BF16)
    return _call(
        body, jobs=jobs, name=name, out_shape=(o, o), grid=(t // ROWS, n // tc),
        in_specs=[blk, blk, blk], out_specs=(blk, blk), compiler_params=_cp(("parallel", "parallel")),
    )(gate, up, dact)


def _gated_norm_fwd(y_pre, proj, g_ssd, *, name):
    t = y_pre.shape[0]

    def body(y_ref, z_ref, g_ref, o_ref):
        z = z_ref[...]
        v = y_ref[...] * z * _sigmoid(z)
        r = lax.rsqrt(jnp.mean(v * v, axis=-1, keepdims=True) + SSM_EPS)
        o_ref[...] = (v * r * g_ref[...]).astype(BF16)

    row = pl.BlockSpec((ROWS, DI), lambda i: (i, 0))
    return pl.pallas_call(
        body, name=name, out_shape=jax.ShapeDtypeStruct((t, DI), BF16), grid=(t // ROWS,),
        in_specs=[row, pl.BlockSpec((ROWS, DI), lambda i: (i, O_Z // DI)), pl.BlockSpec((1, DI), lambda i: (0, 0))],
        out_specs=row, compiler_params=_cp(("parallel",)),
    )(y_pre, proj, g_ssd)


def _gated_norm_bwd(y_pre, proj, g_ssd, dyn, dproj, *, name, jobs=()):
    t = y_pre.shape[0]

    def body(y_ref, z_ref, g_ref, dyn_ref, _, dy_ref, dz_ref, dg_ref):
        z = z_ref[...]
        s = _sigmoid(z)
        sz = z * s
        yv = y_ref[...]
        v = yv * sz
        r = lax.rsqrt(jnp.mean(v * v, axis=-1, keepdims=True) + SSM_EPS)
        vh = v * r
        dn = dyn_ref[...]
        dvh = dn * g_ref[...]
        dv = r * (dvh - vh * jnp.mean(dvh * vh, axis=-1, keepdims=True))
        dy_ref[...] = dv * sz
        dz_ref[...] = (dv * yv * s * (1.0 + z * (1.0 - s))).astype(BF16)

        @pl.when(pl.program_id(0) == 0)
        def _():
            dg_ref[...] = jnp.zeros_like(dg_ref)

        dg_ref[...] += jnp.broadcast_to(jnp.sum(dn * vh, axis=0, keepdims=True), dg_ref.shape)

    row = pl.BlockSpec((ROWS, DI), lambda i: (i, 0))
    return _call(
        body, jobs=jobs, name=name,
        out_shape=(jax.ShapeDtypeStruct((t, DI), F32), jax.ShapeDtypeStruct(dproj.shape, BF16),
                   jax.ShapeDtypeStruct((8, DI), F32)),
        grid=(t // ROWS,),
        in_specs=[row, pl.BlockSpec((ROWS, DI), lambda i: (i, O_Z // DI)), pl.BlockSpec((1, DI), lambda i: (0, 0)), row, ANY],
        out_specs=(row, pl.BlockSpec((ROWS, DI), lambda i: (i, O_Z // DI)), pl.BlockSpec((8, DI), lambda i: (0, 0))),
        compiler_params=_cp(("arbitrary",)), aliases={4: 1},
    )(y_pre, proj, g_ssd, dyn, dproj)


CONV_TC = 512


def _shift_down(x, s, row):
    if s == 0:
        return x
    return jnp.where(row >= s, pltpu.roll(x, s, 0), 0.0)


def _shift_up(x, s, row, t):
    if s == 0:
        return x
    return jnp.where(row < t - s, pltpu.roll(x, t - s, 0), 0.0)


def _conv_fwd(proj, conv_w, conv_b, *, name):
    t = proj.shape[0]

    def body(x_ref, w_ref, b_ref, o_ref):
        x = x_ref[...]
        row = lax.broadcasted_iota(jnp.int32, x.shape, 0)
        pre = jnp.broadcast_to(b_ref[...], x.shape)
        for k in range(CW):
            pre = pre + w_ref[k:k + 1, :] * _shift_down(x, CW - 1 - k, row)
        o_ref[...] = pre * _sigmoid(pre)

    return pl.pallas_call(
        body, name=name, out_shape=jax.ShapeDtypeStruct((t, CONV), F32), grid=(CONV // CONV_TC,),
        in_specs=[pl.BlockSpec((t, CONV_TC), lambda j: (0, O_XBC // CONV_TC + j)),
                  pl.BlockSpec((CW, CONV_TC), lambda j: (0, j)), pl.BlockSpec((1, CONV_TC), lambda j: (0, j))],
        out_specs=pl.BlockSpec((t, CONV_TC), lambda j: (0, j)), compiler_params=_cp(("parallel",)),
    )(proj, conv_w, conv_b)


def _conv_bwd(proj, conv_w, conv_b, dxs, db, dc, dproj, *, name, jobs=()):
    t = proj.shape[0]
    nx = DI // CONV_TC
    assert NG * NS == CONV_TC

    def body(x_ref, w_ref, b_ref, dxs_ref, db_ref, dc_ref, _, dx_ref, dw_ref, dbias_ref):
        j = pl.program_id(0)
        x = x_ref[...]
        row = lax.broadcasted_iota(jnp.int32, x.shape, 0)
        xs = [_shift_down(x, CW - 1 - k, row) for k in range(CW)]
        pre = jnp.broadcast_to(b_ref[...], x.shape)
        for k in range(CW):
            pre = pre + w_ref[k:k + 1, :] * xs[k]
        s = _sigmoid(pre)
        da = jnp.where(j < nx, dxs_ref[...], jnp.where(j == nx, db_ref[...], dc_ref[...]))
        dpre = da * s * (1.0 + pre * (1.0 - s))
        dx = jnp.zeros_like(x)
        row8 = lax.broadcasted_iota(jnp.int32, dw_ref.shape, 0)
        dw = jnp.zeros(dw_ref.shape, F32)
        for k in range(CW):
            dx = dx + w_ref[k:k + 1, :] * _shift_up(dpre, CW - 1 - k, row, t)
            dw = dw + jnp.where(row8 == k, jnp.sum(dpre * xs[k], axis=0, keepdims=True), 0.0)
        dx_ref[...] = dx.astype(BF16)
        dw_ref[...] = dw
        dbias_ref[...] = jnp.broadcast_to(jnp.sum(dpre, axis=0, keepdims=True), dbias_ref.shape)

    col8 = pl.BlockSpec((8, CONV_TC), lambda j: (0, j))
    xbc = pl.BlockSpec((t, CONV_TC), lambda j: (0, O_XBC // CONV_TC + j))
    whole = pl.BlockSpec((t, CONV_TC), lambda j: (0, 0))
    return _call(
        body, jobs=jobs, name=name,
        out_shape=(jax.ShapeDtypeStruct(dproj.shape, BF16), jax.ShapeDtypeStruct((8, CONV), F32),
                   jax.ShapeDtypeStruct((8, CONV), F32)),
        grid=(CONV // CONV_TC,),
        in_specs=[xbc, pl.BlockSpec((CW, CONV_TC), lambda j: (0, j)), pl.BlockSpec((1, CONV_TC), lambda j: (0, j)),
                  pl.BlockSpec((t, CONV_TC), lambda j: (0, jnp.minimum(j, nx - 1))), whole, whole, ANY],
        out_specs=(xbc, col8, col8),
        compiler_params=_cp(("arbitrary",)), aliases={6: 0},
    )(proj, conv_w, conv_b, dxs, db, dc, dproj)


def _rope_tables(positions, t):
    half = HD // 2
    inv_freq = ROPE_THETA ** (-jnp.arange(half, dtype=F32) * 2.0 / HD)
    ang = positions.reshape(t).astype(F32)[:, None] * inv_freq
    cos, sin = jnp.cos(ang), jnp.sin(ang)
    return jnp.concatenate([cos] * 4, axis=1), jnp.concatenate([-sin, sin] * 2, axis=1)


def _lane_consts():
    lane = lax.broadcasted_iota(jnp.int32, (L, 128), 1)
    return lane, (lane % HD) < (HD // 2), lane < HD


def _rope(tv, cos, sin, lo):
    return tv * cos + jnp.where(lo, pltpu.roll(tv, 128 - HD // 2, 1), pltpu.roll(tv, HD // 2, 1)) * sin


def _rope_t(dv, cos, sin, lo):
    ds = dv * sin
    return dv * cos + jnp.where(lo, pltpu.roll(ds, 128 - HD // 2, 1), pltpu.roll(ds, HD // 2, 1))


def _placed(chunk, g, half0):
    own = jnp.where(half0 if g % 2 == 0 else jnp.logical_not(half0), chunk, 0.0)
    other = pltpu.roll(own, HD, 1)
    return (own, other) if g % 2 == 0 else (other, own)


def _unplace(acc, hf, g, half0):
    v = jnp.where(half0 if hf == 0 else jnp.logical_not(half0), acc, 0.0)
    return v if hf == g % 2 else pltpu.roll(v, HD, 1)


def _attn_fwd(proj, cos, sin, sinks, *, name, jobs=()):
    t = proj.shape[0]
    nb = t // L
    scale = HD ** -0.5

    def body(sink_ref, q_ref, kc_ref, kp_ref, vc_ref, vp_ref, cc_ref, sc_ref, cp_ref, sp_ref, o_ref, lse_ref):
        i = pl.program_id(0)
        lane, lo, half0 = _lane_consts()
        cos_c, sin_c, cos_p, sin_p = cc_ref[...], sc_ref[...], cp_ref[...], sp_ref[...]
        row = lax.broadcasted_iota(jnp.int32, (L, L), 0)
        col = lax.broadcasted_iota(jnp.int32, (L, L), 1)
        m_cur = col <= row
        m_prev = jnp.logical_and(col > row, i > 0)
        kc = [_rope(kc_ref[:, 128 * m:128 * (m + 1)], cos_c, sin_c, lo) for m in range(2)]
        kp = [_rope(kp_ref[:, 128 * m:128 * (m + 1)], cos_p, sin_p, lo) for m in range(2)]
        lse_acc = jnp.zeros((L, 128), F32)
        outs = [jnp.zeros((L, 128), F32) for _ in range(QD // 128)]
        qs = [(_rope(q_ref[:, 128 * ch:128 * (ch + 1)], cos_c, sin_c, lo) * scale).astype(BF16) for ch in range(QD // 128)]
        for g in range(NKV):
            kcv = [v.astype(BF16) for v in _placed(kc[g // 2], g, half0)]
            kpv = [v.astype(BF16) for v in _placed(kp[g // 2], g, half0)]
            vcv = [v.astype(BF16) for v in _placed(vc_ref[:, 128 * (g // 2):128 * (g // 2 + 1)], g, half0)]
            vpv = [v.astype(BF16) for v in _placed(vp_ref[:, 128 * (g // 2):128 * (g // 2 + 1)], g, half0)]
            for r in range(NQH // NKV):
                h = g * (NQH // NKV) + r
                ch, hf = h // 2, h % 2
                s_c = jnp.where(m_cur, _dot_nt(qs[ch], kcv[hf]), NEG)
                s_p = jnp.where(m_prev, _dot_nt(qs[ch], kpv[hf]), NEG)
                sink = sink_ref[0, h]
                mx = jnp.maximum(jnp.maximum(jnp.max(s_c, axis=-1, keepdims=True), jnp.max(s_p, axis=-1, keepdims=True)), sink)
                e_c = jnp.exp(s_c - mx)
                e_p = jnp.exp(s_p - mx)
                den = jnp.sum(e_c, axis=-1, keepdims=True) + jnp.sum(e_p, axis=-1, keepdims=True) + jnp.exp(sink - mx)
                inv = 1.0 / den
                outs[ch] = outs[ch] + _dot((e_c * inv).astype(BF16), vcv[hf]) + _dot((e_p * inv).astype(BF16), vpv[hf])
                lse_acc = jnp.where(lane == h, mx + jnp.log(den), lse_acc)
        for ch in range(QD // 128):
            o_ref[:, 128 * ch:128 * (ch + 1)] = outs[ch].astype(BF16)
        lse_ref[...] = lse_acc

    prev = lambda i: jnp.maximum(i - 1, 0)
    tab_c = pl.BlockSpec((L, 128), lambda i: (i, 0))
    tab_p = pl.BlockSpec((L, 128), lambda i: (prev(i), 0))
    return _call(
        body, jobs=jobs, name=name,
        out_shape=(jax.ShapeDtypeStruct((t, QD), BF16), jax.ShapeDtypeStruct((t, 128), F32)),
        grid=(nb,),
        in_specs=[pl.BlockSpec(memory_space=pltpu.SMEM),
                  pl.BlockSpec((L, QD), lambda i: (i, O_Q // QD)),
                  pl.BlockSpec((L, KVD), lambda i: (i, O_K // KVD)), pl.BlockSpec((L, KVD), lambda i: (prev(i), O_K // KVD)),
                  pl.BlockSpec((L, KVD), lambda i: (i, O_V // KVD)), pl.BlockSpec((L, KVD), lambda i: (prev(i), O_V // KVD)),
                  tab_c, tab_c, tab_p, tab_p],
        out_specs=(pl.BlockSpec((L, QD), lambda i: (i, 0)), pl.BlockSpec((L, 128), lambda i: (i, 0))),
        compiler_params=_cp(("parallel",)),
    )(sinks, proj, proj, proj, proj, proj, cos, sin, cos, sin)


def _attn_bwd(proj, cos, sin, sinks, attn, lse, dattn, dproj, *, name, jobs=()):
    t = proj.shape[0]
    nb = t // L
    scale = HD ** -0.5

    def body(sink_ref, qi_ref, qn_ref, kc_ref, kp_ref, vc_ref, vp_ref, doi_ref, don_ref, oi_ref, on_ref,
             lsei_ref, lsen_ref, cc_ref, sc_ref, cp_ref, sp_ref, cn_ref, sn_ref, _, dqkv_ref, dsk_ref):
        i = pl.program_id(0)
        lane, lo, half0 = _lane_consts()
        half1 = jnp.logical_not(half0)
        cos_c, sin_c = cc_ref[...], sc_ref[...]
        row = lax.broadcasted_iota(jnp.int32, (L, L), 0)
        col = lax.broadcasted_iota(jnp.int32, (L, L), 1)
        m_cur = col <= row
        m_prev = jnp.logical_and(col > row, i > 0)
        m_next = jnp.logical_and(col > row, i < nb - 1)
        kc = [_rope(kc_ref[:, 128 * m:128 * (m + 1)], cos_c, sin_c, lo) for m in range(2)]
        kp = [_rope(kp_ref[:, 128 * m:128 * (m + 1)], cp_ref[...], sp_ref[...], lo) for m in range(2)]
        lse_i, lse_n = lsei_ref[...], lsen_ref[...]
        dk_acc = [jnp.zeros((L, 128), F32) for _ in range(2)]
        dv_acc = [jnp.zeros((L, 128), F32) for _ in range(2)]
        dsk_acc = jnp.zeros((1, 128), F32)
        lane1 = lax.broadcasted_iota(jnp.int32, (1, 128), 1)
        place = lambda chunk, g: [v.astype(BF16) for v in _placed(chunk, g, half0)]
        kcs = [place(kc[g // 2], g) for g in range(NKV)]
        kps = [place(kp[g // 2], g) for g in range(NKV)]
        vcs = [place(vc_ref[:, 128 * (g // 2):128 * (g // 2 + 1)], g) for g in range(NKV)]
        vps = [place(vp_ref[:, 128 * (g // 2):128 * (g // 2 + 1)], g) for g in range(NKV)]
        for ch in range(QD // 128):
            sl = slice(128 * ch, 128 * (ch + 1))
            q_i = (_rope(qi_ref[:, sl], cos_c, sin_c, lo) * scale).astype(BF16)
            q_n = (_rope(qn_ref[:, sl], cn_ref[...], sn_ref[...], lo) * scale).astype(BF16)
            do_i, do_n = doi_ref[:, sl], don_ref[:, sl]
            do_ib, do_nb = do_i.astype(BF16), do_n.astype(BF16)
            od_i = do_i * oi_ref[:, sl].astype(F32)
            od_n = do_n * on_ref[:, sl].astype(F32)
            dq_ch = jnp.zeros((L, 128), F32)
            for hf in range(2):
                h = 2 * ch + hf
                g = h // (NQH // NKV)
                hm = half0 if hf == 0 else half1
                kcv, kpv, vcv, vpv = kcs[g][hf], kps[g][hf], vcs[g][hf], vps[g][hf]
                dl_i = jnp.sum(jnp.where(hm, od_i, 0.0), axis=-1, keepdims=True)
                dl_n = jnp.sum(jnp.where(hm, od_n, 0.0), axis=-1, keepdims=True)
                ls_i = jnp.sum(jnp.where(lane == h, lse_i, 0.0), axis=-1, keepdims=True)
                ls_n = jnp.sum(jnp.where(lane == h, lse_n, 0.0), axis=-1, keepdims=True)
                p_c = jnp.where(m_cur, jnp.exp(_dot_nt(q_i, kcv) - ls_i), 0.0)
                p_p = jnp.where(m_prev, jnp.exp(_dot_nt(q_i, kpv) - ls_i), 0.0)
                ds_c = (p_c * (_dot_nt(do_ib, vcv) - dl_i)).astype(BF16)
                ds_p = (p_p * (_dot_nt(do_ib, vpv) - dl_i)).astype(BF16)
                dq_ch = dq_ch + jnp.where(hm, (_dot(ds_c, kcv) + _dot(ds_p, kpv)) * scale, 0.0)
                sink = sink_ref[0, h]
                dsk = -jnp.sum(jnp.exp(sink - ls_i) * dl_i, axis=0, keepdims=True)
                dsk_acc = dsk_acc + jnp.where(lane1 == h, dsk, 0.0)
                p_n = jnp.where(m_next, jnp.exp(_dot_nt(q_n, kcv) - ls_n), 0.0)
                ds_n = (p_n * (_dot_nt(do_nb, vcv) - dl_n)).astype(BF16)
                dv_h = _dot_tn(p_c.astype(BF16), do_ib) + _dot_tn(p_n.astype(BF16), do_nb)
                dk_h = _dot_tn(ds_c, q_i) + _dot_tn(ds_n, q_n)
                dv_acc[g // 2] = dv_acc[g // 2] + _unplace(dv_h, hf, g, half0)
                dk_acc[g // 2] = dk_acc[g // 2] + _unplace(dk_h, hf, g, half0)
            dqkv_ref[:, sl] = _rope_t(dq_ch, cos_c, sin_c, lo).astype(BF16)
        for m in range(2):
            dqkv_ref[:, QD + 128 * m:QD + 128 * (m + 1)] = _rope_t(dk_acc[m], cos_c, sin_c, lo).astype(BF16)
            dqkv_ref[:, QD + KVD + 128 * m:QD + KVD + 128 * (m + 1)] = dv_acc[m].astype(BF16)

        @pl.when(i == 0)
        def _():
            dsk_ref[...] = jnp.zeros_like(dsk_ref)

        dsk_ref[...] += jnp.broadcast_to(dsk_acc, dsk_ref.shape)

    prev = lambda i: jnp.maximum(i - 1, 0)
    nxt = lambda i: jnp.minimum(i + 1, nb - 1)
    cur_q = pl.BlockSpec((L, QD), lambda i: (i, 0))
    nxt_q = pl.BlockSpec((L, QD), lambda i: (nxt(i), 0))
    tab = lambda f: pl.BlockSpec((L, 128), lambda i: (f(i), 0))
    ident = lambda i: i
    qkv = QD + 2 * KVD
    assert O_K == O_Q + QD and O_V == O_K + KVD and O_Q % qkv == 0
    return _call(
        body, jobs=jobs, name=name,
        out_shape=(jax.ShapeDtypeStruct(dproj.shape, BF16), jax.ShapeDtypeStruct((8, 128), F32)),
        grid=(nb,),
        in_specs=[pl.BlockSpec(memory_space=pltpu.SMEM),
                  pl.BlockSpec((L, QD), lambda i: (i, O_Q // QD)), pl.BlockSpec((L, QD), lambda i: (nxt(i), O_Q // QD)),
                  pl.BlockSpec((L, KVD), lambda i: (i, O_K // KVD)), pl.BlockSpec((L, KVD), lambda i: (prev(i), O_K // KVD)),
                  pl.BlockSpec((L, KVD), lambda i: (i, O_V // KVD)), pl.BlockSpec((L, KVD), lambda i: (prev(i), O_V // KVD)),
                  cur_q, nxt_q, cur_q, nxt_q, tab(ident), tab(nxt),
                  tab(ident), tab(ident), tab(prev), tab(prev), tab(nxt), tab(nxt), ANY],
        out_specs=(pl.BlockSpec((L, qkv), lambda i: (i, O_Q // qkv)), pl.BlockSpec((8, 128), lambda i: (0, 0))),
        compiler_params=_cp(("arbitrary",)), aliases={19: 0},
    )(sinks, proj, proj, proj, proj, proj, proj, dattn, dattn, attn, attn, lse, lse, cos, sin, cos, sin, cos, sin, dproj)


PAIRS = NH // NG // 2


def _softplus(x):
    return jnp.maximum(x, 0.0) + jnp.log(1.0 + jnp.exp(-jnp.abs(x)))


def _ssd_chunk(g, xps, dtr, bm, cm, sps, dtb, alog, dsk):
    lane = lax.broadcasted_iota(jnp.int32, (L, 128), 1)
    lane1 = lax.broadcasted_iota(jnp.int32, (1, 128), 1)
    row = lax.broadcasted_iota(jnp.int32, (L, L), 0)
    col = lax.broadcasted_iota(jnp.int32, (L, L), 1)
    rowc = lax.broadcasted_iota(jnp.int32, (128, 1), 0)
    tril = col <= row
    dt = _softplus(dtr + dtb)
    a = dt * (-jnp.exp(alog))
    a_cs = lax.dot_general(tril.astype(F32), a, (((1,), (0,)), ((), ())), precision=lax.Precision.HIGHEST,
                           preferred_element_type=F32)
    a_cst = a_cs.T
    a_last = jnp.sum(jnp.where(row == L - 1, a_cs, 0.0), axis=0, keepdims=True)
    cb = _bdot_nt(cm, bm)
    ys, snew = [], []
    for q in range(PAIRS):
        xp, sp = xps[q], sps[q]
        y_pair = jnp.zeros((L, 128), F32)
        st_pair = jnp.zeros((128, NS), F32)
        keep = jnp.zeros((128, 1), F32)
        for hh in range(2):
            h = g * 2 * PAIRS + 2 * q + hh
            hm = (lane < HD) if hh == 0 else (lane >= HD)
            rm = (rowc < HD) if hh == 0 else (rowc >= HD)
            dt_h = jnp.sum(jnp.where(lane == h, dt, 0.0), axis=1, keepdims=True)
            acs_h = jnp.sum(jnp.where(lane == h, a_cs, 0.0), axis=1, keepdims=True)
            acst_h = jnp.sum(jnp.where(row == h, a_cst, 0.0), axis=0, keepdims=True)
            al_h = jnp.sum(jnp.where(lane1 == h, a_last, 0.0), axis=1, keepdims=True)
            dsk_h = jnp.sum(jnp.where(lane1 == h, dsk, 0.0), axis=1, keepdims=True)
            decay = jnp.where(tril, jnp.exp(jnp.where(tril, acs_h - acst_h, 0.0)), 0.0)
            xh = jnp.where(hm, xp, 0.0)
            xd = xh * dt_h
            y = _bdot(cb * decay, xd)
            y = y + jnp.where(hm, _bdot_nt(cm * jnp.exp(acs_h), sp), 0.0)
            y_pair = y_pair + y + dsk_h * xh
            st_pair = st_pair + _bdot_tn(xd, bm * jnp.exp(al_h - acs_h))
            keep = keep + jnp.where(rm, jnp.exp(al_h), 0.0)
        ys.append(y_pair)
        snew.append(sp * keep + st_pair)
    return ys, snew


def _ssd_specs(t):
    nc = t // L
    xs = lambda f: pl.BlockSpec((L, 128 * PAIRS), lambda c, g: (f(c), g))
    bspec = lambda f: pl.BlockSpec((L, NS), lambda c, g: (f(c), DI // NS + g))
    cspec = lambda f: pl.BlockSpec((L, NS), lambda c, g: (f(c), DI // NS + NG + g))
    dts = lambda f: pl.BlockSpec((L, 128), lambda c, g: (f(c), O_DT // 128))
    par = pl.BlockSpec((1, 128), lambda c, g: (0, 0))
    st = lambda f: pl.BlockSpec((1, 1, PAIRS, 128, NS), lambda c, g: (f(c), g, 0, 0, 0))
    return nc, xs, bspec, cspec, dts, par, st


def _ssd_fwd(xbc_act, proj, dtb, alog, dsk, *, name, jobs=()):
    t = proj.shape[0]
    nc, xs, bspec, cspec, dts, par, st = _ssd_specs(t)
    ident = lambda c: c

    def body(x_ref, b_ref, c_ref, dt_ref, dtb_ref, al_ref, dsk_ref, y_ref, sin_ref, s_ref):
        c, g = pl.program_id(0), pl.program_id(1)

        @pl.when(c == 0)
        def _():
            s_ref[g] = jnp.zeros((PAIRS, 128, NS), F32)

        sps = [s_ref[g, q] for q in range(PAIRS)]
        for q in range(PAIRS):
            sin_ref[0, 0, q] = sps[q]
        xps = [x_ref[:, 128 * q:128 * (q + 1)] for q in range(PAIRS)]
        ys, snew = _ssd_chunk(g, xps, dt_ref[...], b_ref[...], c_ref[...], sps, dtb_ref[...], al_ref[...], dsk_ref[...])
        for q in range(PAIRS):
            y_ref[:, 128 * q:128 * (q + 1)] = ys[q]
            s_ref[g, q] = snew[q]

    return _call(
        body, jobs=jobs, name=name,
        out_shape=(jax.ShapeDtypeStruct((t, DI), F32), jax.ShapeDtypeStruct((nc, NG, PAIRS, 128, NS), F32)),
        grid=(nc, NG),
        in_specs=[xs(ident), bspec(ident), cspec(ident), dts(ident), par, par, par],
        out_specs=(pl.BlockSpec((L, 128 * PAIRS), lambda c, g: (c, g)), st(ident)),
        scratch_shapes=[pltpu.VMEM((NG, PAIRS, 128, NS), F32)],
        compiler_params=_cp(("arbitrary", "arbitrary")),
    )(xbc_act, xbc_act, xbc_act, proj, dtb, alog, dsk)


def _ssd_bwd(xbc_act, proj, dtb, alog, dsk, states, dy, dproj, *, name, jobs=()):
    t = proj.shape[0]
    nc, xs, bspec, cspec, dts, par, st = _ssd_specs(t)
    rev = lambda c: nc - 1 - c

    def body(x_ref, b_ref, c_ref, dt_ref, dtb_ref, al_ref, dsk_ref, sin_ref, dy_ref, _,
             dx_ref, db_ref, dc_ref, ddtp_ref, ddtb_ref, dal_ref, ddsk_ref, ds_ref, ddt_ref):
        c, g = pl.program_id(0), pl.program_id(1)

        @pl.when(c == 0)
        def _():
            ds_ref[g] = jnp.zeros((PAIRS, 128, NS), F32)

        @pl.when(jnp.logical_and(c == 0, g == 0))
        def _():
            ddtb_ref[...] = jnp.zeros_like(ddtb_ref)
            dal_ref[...] = jnp.zeros_like(dal_ref)
            ddsk_ref[...] = jnp.zeros_like(ddsk_ref)

        @pl.when(g == 0)
        def _():
            ddt_ref[...] = jnp.zeros_like(ddt_ref)

        sps = [sin_ref[0, 0, q] for q in range(PAIRS)]
        xps = [x_ref[:, 128 * q:128 * (q + 1)] for q in range(PAIRS)]
        _, vjp = jax.vjp(functools.partial(_ssd_chunk, g), xps, dt_ref[...], b_ref[...], c_ref[...], sps,
                         dtb_ref[...], al_ref[...], dsk_ref[...])
        dys = [dy_ref[:, 128 * q:128 * (q + 1)] for q in range(PAIRS)]
        dss = [ds_ref[g, q] for q in range(PAIRS)]
        dxps, ddt, db, dc, dsps, ddtb, dal, ddsk = vjp((dys, dss))
        for q in range(PAIRS):
            dx_ref[:, 128 * q:128 * (q + 1)] = dxps[q]
            ds_ref[g, q] = dsps[q]
        db_ref[...] = db
        dc_ref[...] = dc
        ddt_ref[...] += ddt
        ddtb_ref[...] += jnp.broadcast_to(ddtb, ddtb_ref.shape)
        dal_ref[...] += jnp.broadcast_to(dal, dal_ref.shape)
        ddsk_ref[...] += jnp.broadcast_to(ddsk, ddsk_ref.shape)

        @pl.when(g == NG - 1)
        def _():
            ddtp_ref[:, :128] = ddt_ref[...].astype(BF16)
            ddtp_ref[:, 128:] = jnp.zeros((L, DT_PAD - 128), BF16)

    acc = pl.BlockSpec((8, 128), lambda c, g: (0, 0))
    o8 = jax.ShapeDtypeStruct((8, 128), F32)
    return _call(
        body, jobs=jobs, name=name,
        out_shape=(jax.ShapeDtypeStruct((t, DI), F32), jax.ShapeDtypeStruct((t, NG * NS), F32),
                   jax.ShapeDtypeStruct((t, NG * NS), F32), jax.ShapeDtypeStruct(dproj.shape, BF16), o8, o8, o8),
        grid=(nc, NG),
        in_specs=[xs(rev), bspec(rev), cspec(rev), dts(rev), par, par, par, st(rev),
                  pl.BlockSpec((L, 128 * PAIRS), lambda c, g: (rev(c), g)), ANY],
        out_specs=(pl.BlockSpec((L, 128 * PAIRS), lambda c, g: (rev(c), g)),
                   pl.BlockSpec((L, NS), lambda c, g: (rev(c), g)), pl.BlockSpec((L, NS), lambda c, g: (rev(c), g)),
                   pl.BlockSpec((L, DT_PAD), lambda c, g: (rev(c), O_DT // DT_PAD)), acc, acc, acc),
        scratch_shapes=[pltpu.VMEM((NG, PAIRS, 128, NS), F32), pltpu.VMEM((L, 128), F32)],
        compiler_params=_cp(("arbitrary", "arbitrary")), aliases={9: 3},
    )(xbc_act, xbc_act, xbc_act, proj, dtb, alog, dsk, states, dy, dproj)


def _pad_lanes(v, n=128):
    return jnp.pad(v, ((0, 0), (0, n - v.shape[1])))


class _LocalPlan:
    core = 0

    def __init__(self, big):
        self.big, self.grad, self.halves = big, {}, {}

    def w(self, n):
        return self.big[n]

    def g(self, n, a):
        self.grad[n] = a

    def g_half(self, n, which, a):
        self.halves[which] = a
        if len(self.halves) == 2:
            self.grad[n] = jnp.concatenate([self.halves["keep"], self.halves["send"]], axis=0)

    def jobs(self, tag):
        return ()


def _local_step(x, p, positions, target, small, plan):
    t = x.shape[0]
    cos, sin = _rope_tables(positions, t)
    dtb, alog, dsk = _pad_lanes(small["dt_bias"]), _pad_lanes(small["a_log"]), _pad_lanes(small["d_skip"])
    w, jobs = plan.w, plan.jobs

    def mm(a, b, *, name, tn=512, **kw):
        return _matmul(a, b, tm=t, tn=tn, name=name, jobs=jobs(name), **kw)

    tkl = FFN // 4

    def dw(wname, a, dy, *, name, tm):
        plan.g(wname, _matmul(a, dy, ta=True, out_dtype=BF16, tm=tm, tn=512, tk=t, name=name, jobs=jobs(name)))

    u = _rmsnorm_fwd(x, small["g_mix"], name="norm_mix")
    proj = mm(u, w("w_in"), tk=D, name="mm_in")
    attn, lse = _attn_fwd(proj, cos, sin, small["sinks"], name="attn_fwd", jobs=jobs("attn_fwd"))
    out_a = mm(attn, w("w_attn_br"), tk=QD, name="mm_attn_br")
    xbc_act = _conv_fwd(proj, small["conv_w"], small["conv_b"], name="conv_fwd")
    y_pre, states = _ssd_fwd(xbc_act, proj, dtb, alog, dsk, name="ssd_fwd", jobs=jobs("ssd_fwd"))
    yn = _gated_norm_fwd(y_pre, proj, small["g_ssd"], name="gated_norm_fwd")
    out_s = mm(yn, w("w_ssd_br"), tk=DI, name="mm_ssd_br")
    merged = _merge_fwd(proj, out_a, out_s, name="merge_fwd")
    h1 = mm(merged, w("w_o"), add=x, tk=D, name="mm_o")
    f = _rmsnorm_fwd(h1, small["g_ffn"], name="norm_ffn")
    gate, up, act = _swiglu_fwd(f, w("w_gate"), w("w_up"), name="swiglu_fwd", jobs=jobs("swiglu_fwd"))
    h2 = mm(act, w("w_down"), add=h1, tk=tkl, name="mm_down")
    e = _rmsnorm_fwd(h2, small["g_ple"], name="norm_ple")
    pgl = mm(e, w("w_ple_gate"), tk=D, name="mm_ple_gate")
    pb = p.astype(BF16)
    pp = mm(pb, w("w_ple_proj"), tk=PLE, name="mm_ple_proj")
    dh3, dpgl, dpp, loss, dg_final = _final(h2, pgl, pp, target, small["g_final"].reshape(1, D), name="final")

    dw("w_ple_proj", pb, dpp, tm=PLE, name="mm_d_ple_proj")
    dw("w_ple_gate", e, dpgl, tm=D, name="mm_d_ple_gate")
    de = mm(dpgl, w("w_ple_gate"), tb=True, tk=D, name="mm_de")
    dh2, dh2b, dg_ple = _rmsnorm_bwd(h2, small["g_ple"], de, dh3, name="norm_ple_bwd", jobs=jobs("norm_ple_bwd"))
    dw("w_down", act, dh2b, tm=FFN // 2, name="mm_d_down")
    dact = mm(dh2b, w("w_down"), tb=True, tk=D, name="mm_dact")
    dgate, dup = _swiglu_bwd(gate, up, dact, name="swiglu_bwd", jobs=jobs("swiglu_bwd"))
    dw("w_gate", f, dgate, tm=D, name="mm_d_gate")
    dw("w_up", f, dup, tm=D, name="mm_d_up")
    df = mm(dgate, w("w_gate"), tb=True, tn=1024, tk=tkl, name="mm_df_gate")
    df = mm(dup, w("w_up"), tb=True, add=df, tk=tkl, name="mm_df_up")
    dh1, dh1b, dg_ffn = _rmsnorm_bwd(h1, small["g_ffn"], df, dh2, name="norm_ffn_bwd", jobs=jobs("norm_ffn_bwd"))
    dw("w_o", merged, dh1b, tm=D, name="mm_d_o")
    dmerged = mm(dh1b, w("w_o"), tb=True, tk=D, name="mm_dmerged")
    dout_a, dout_s, dproj = _merge_bwd(proj, out_a, out_s, dmerged, name="merge_bwd")
    dw("w_attn_br", attn, dout_a, tm=QD, name="mm_d_attn_br")
    dw("w_ssd_br", yn, dout_s, tm=DI, name="mm_d_ssd_br")
    dattn = mm(dout_a, w("w_attn_br"), tb=True, tk=D, name="mm_dattn")
    dyn = mm(dout_s, w("w_ssd_br"), tb=True, tk=D, name="mm_dyn")
    dproj, dsinks = _attn_bwd(proj, cos, sin, small["sinks"], attn, lse, dattn, dproj, name="attn_bwd",
                              jobs=jobs("attn_bwd"))
    dy_pre, dproj, dg_ssd = _gated_norm_bwd(y_pre, proj, small["g_ssd"], dyn, dproj, name="gated_norm_bwd",
                                            jobs=jobs("gated_norm_bwd"))
    dxs, db, dc, dproj, ddtb, dalog, ddsk = _ssd_bwd(xbc_act, proj, dtb, alog, dsk, states, dy_pre, dproj, name="ssd_bwd",
                                                     jobs=jobs("ssd_bwd"))
    dproj, dconv_w, dconv_b = _conv_bwd(proj, small["conv_w"], small["conv_b"], dxs, db, dc, dproj, name="conv_bwd",
                                        jobs=jobs("conv_bwd"))
    for which, h in (("send", 1 - plan.core), ("keep", plan.core)):
        uh = lax.dynamic_slice_in_dim(u, h * (D // 2), D // 2, axis=1)
        name = "mm_d_in_" + which
        plan.g_half("w_in", which, _matmul(uh, dproj, ta=True, out_dtype=BF16, tm=D // 2, tn=512, tk=t, name=name,
                                           jobs=jobs(name)))
    du = mm(dproj, w("w_in"), tb=True, tn=1024, tk=tkl, name="mm_du")
    grad_x, _, dg_mix = _rmsnorm_bwd(x, small["g_mix"], du, dh1, name="norm_mix_bwd", jobs=jobs("norm_mix_bwd"))

    gs = {
        "g_mix": dg_mix[:1], "conv_w": dconv_w[:CW], "conv_b": dconv_b[:1], "dt_bias": ddtb[:1, :NH],
        "a_log": dalog[:1, :NH], "d_skip": ddsk[:1, :NH], "g_ssd": dg_ssd[:1], "sinks": dsinks[:1, :NQH],
        "g_ffn": dg_ffn[:1], "g_ple": dg_ple[:1], "g_final": dg_final[0],
    }
    return loss, grad_x, gs


def _to_kernel_cols(w):
    seg = lambda o, n: w[:, o:o + n]
    return jnp.concatenate([seg(R_GA, D), seg(R_GS, D), seg(R_Z, DI), seg(R_XBC, CONV), seg(R_Q, QD), seg(R_K, KVD),
                            seg(R_V, KVD), seg(R_DT, NH), jnp.zeros((w.shape[0], DT_PAD - NH), w.dtype)], axis=1)


def _from_kernel_cols(g):
    seg = lambda o, n: g[:, o:o + n]
    return jnp.concatenate([seg(O_Q, QD), seg(O_K, KVD), seg(O_V, KVD), seg(O_Z, DI), seg(O_XBC, CONV), seg(O_DT, NH),
                            seg(O_GA, D), seg(O_GS, D)], axis=1)


def _shard_pieces():
    segs = ((R_Q, QD, O_Q), (R_K, KVD, O_K), (R_V, KVD, O_V), (R_Z, DI, O_Z), (R_XBC, CONV, O_XBC), (R_DT, NH, O_DT),
            (R_GA, D, O_GA), (R_GS, D, O_GS))
    cs = IN_DIM // NCHIP
    out = []
    for j in range(NCHIP):
        for r0, n, k0 in segs:
            lo, hi = max(r0, j * cs), min(r0 + n, (j + 1) * cs)
            if lo < hi:
                out.append((j, lo - j * cs, hi - lo, k0 + lo - r0))
    return out


SLAB = IN_DIM // NCHIP
SLAB_PAD = -(-SLAB // 128) * 128
REMAP_ROWS = 256


def _lane_remap(src, dst_slabs, dst_cols, moves, *, name, jobs=()):
    s_n, rows, s_cols = src.shape
    assert s_cols % 128 == 0 and dst_cols % 128 == 0 and rows % REMAP_ROWS == 0
    half = REMAP_ROWS // 2

    def body(s_ref, d_ref):
        lane = lax.broadcasted_iota(jnp.int32, (half, 128), 1)
        tiles = {}

        def tile(j, m):
            if (j, m) not in tiles:
                tiles[j, m] = pltpu.bitcast(s_ref[j, :, 128 * m:128 * (m + 1)], jnp.uint32)
            return tiles[j, m]

        def window(j, base):
            m0, s = base // 128, base % 128
            left = tile(j, m0) if 0 <= m0 < s_cols // 128 else None
            if s == 0:
                return left
            right = tile(j, m0 + 1) if 0 <= m0 + 1 < s_cols // 128 else None
            left = None if left is None else pltpu.roll(left, 128 - s, 1)
            right = None if right is None else pltpu.roll(right, 128 - s, 1)
            if left is None or right is None:
                return right if left is None else left
            return jnp.where(lane < 128 - s, left, right)

        for ds in range(dst_slabs):
            for t in range(dst_cols // 128):
                o = 128 * t
                acc = jnp.zeros((half, 128), jnp.uint32)
                for sj, sc, n, dj, dc in moves:
                    lo, hi = max(o, dc) - o, min(o + 128, dc + n) - o
                    if dj != ds or lo >= hi:
                        continue
                    win = window(sj, o - dc + sc)
                    acc = win if (lo, hi) == (0, 128) else jnp.where(jnp.logical_and(lane >= lo, lane < hi), win, acc)
                d_ref[ds, :, o:o + 128] = pltpu.bitcast(acc, BF16)

    return _call(
        body, jobs=jobs, name=name, out_shape=jax.ShapeDtypeStruct((dst_slabs, rows, dst_cols), BF16),
        grid=(rows // REMAP_ROWS,),
        in_specs=[pl.BlockSpec((s_n, REMAP_ROWS, s_cols), lambda i: (0, i, 0))],
        out_specs=pl.BlockSpec((dst_slabs, REMAP_ROWS, dst_cols), lambda i: (0, i, 0)),
        compiler_params=_cp(("parallel",)),
    )(src)


def _slabs_to_kernel_cols(slabs, *, name, jobs=()):
    moves = [(j, a, n, 0, k0) for j, a, n, k0 in _shard_pieces()]
    return _lane_remap(slabs, 1, NP, moves, name=name, jobs=jobs)[0]


def _kernel_cols_to_slabs(g, *, name, jobs=()):
    moves = [(0, k0, n, j, a) for j, a, n, k0 in _shard_pieces()]
    return _lane_remap(g[None], NCHIP, SLAB_PAD, moves, name=name, jobs=jobs)


RELS = ((0, 1), (1, 0), (1, 1))
MATS = {
    n: (n, kind, 1, r, c, tp, tf) for n, kind, r, c, tp, tf in (
        ("w_in", "stk", 2048, SLAB_PAD, 256, 256),
        ("w_attn_br", "col", 1024, 512, 256, 256),
        ("w_ssd_br", "row", 512, 2048, 512, 256),
        ("w_o", "row", 512, 2048, 512, 256),
        ("w_gate", "col", 2048, 1408, 256, 256),
        ("w_up", "col", 2048, 1408, 256, 256),
        ("w_down", "row", 1408, 2048, 704, 704),
        ("w_ple_gate", "row", 512, 2048, 512, 256),
        ("w_ple_proj", "col", 256, 512, 128, 128),
    )}


def _pos():
    return lax.axis_index("x"), lax.axis_index("y"), lax.axis_index("c")


def _flip(v, a):
    return 1 - v if a else v


def _remote(src, dst, send, recv, dev):
    return pltpu.make_async_remote_copy(src_ref=src, dst_ref=dst, send_sem=send, recv_sem=recv, device_id=dev,
                                        device_id_type=MESH)


def _whole_shape(kind, g, r, c):
    return {"row": (g, NCHIP * r, c), "col": (g, r, NCHIP * c), "stk": (NCHIP, r, c)}[kind]


def _cols(j, c):
    return pl.ds(pl.multiple_of(j * c, 128), c)


def _whole_shard(kind, ref, j, r, c):
    if kind == "row":
        return ref.at[:, pl.ds(j * r, r), :]
    if kind == "col":
        return ref.at[:, :, _cols(j, c)]
    return ref.at[pl.ds(j, 1)]


def _whole_rows(kind, ref, j, row, n, r, c):
    if kind == "row":
        return ref.at[:, pl.ds(j * r + row, n), :]
    if kind == "col":
        return ref.at[:, pl.ds(row, n), _cols(j, c)]
    return ref.at[pl.ds(j, 1), pl.ds(row, n), :]


class _GatherJob(_Job):
    has_mid = True
    NCP = 10

    def __init__(self, names, shards, sink):
        self.mats = [MATS[n] for n in names]
        self.srcs = [shards[n] for n in names]
        self.news = [jax.ShapeDtypeStruct(_whole_shape(kind, g, r, c), BF16) for _, kind, g, r, c, _, _ in self.mats]
        n = len(names)
        self.scratch = [pltpu.SemaphoreType.DMA((self.NCP * n,)), pltpu.SemaphoreType.DMA((self.NCP * n,))]
        self.names, self.sink = names, sink

    def _copies(self, srcs, news, sems):
        send, recv = sems
        x, y, c = _pos()
        me, jx, jy, jd = 2 * x + y, 2 * (1 - x) + y, 2 * x + (1 - y), 2 * (1 - x) + (1 - y)
        nbx, nby, sib = (1 - x, y, c), (x, 1 - y, c), (x, y, 1 - c)
        cps = []
        for w, (_, kind, g, r, cc, _, _) in enumerate(self.mats):
            hr, qr = r // 2, r // 4
            at = lambda j, h, q, n: _whole_rows(kind, news[w], j, h * hr + q * qr, n, r, cc)
            mine = lambda q: srcs[w].at[:, pl.ds(c * hr + q * qr, qr), :]
            cp = lambda k, s, d, dev: _remote(s, d, send.at[self.NCP * w + k], recv.at[self.NCP * w + k], dev)
            cps.append([
                cp(0, mine(0), at(me, c, 0, qr), nbx), cp(1, mine(1), at(me, c, 1, qr), nbx),
                cp(2, mine(1), at(me, c, 1, qr), nby), cp(3, mine(0), at(me, c, 0, qr), nby),
                cp(4, at(jx, c, 0, qr), at(jx, c, 0, qr), nby), cp(5, at(jy, c, 1, qr), at(jy, c, 1, qr), nbx),
                cp(6, at(jx, c, 0, hr), at(jx, c, 0, hr), sib), cp(7, at(jy, c, 0, hr), at(jy, c, 0, hr), sib),
                cp(8, at(jd, c, 0, hr), at(jd, c, 0, hr), sib),
                cp(9, srcs[w], _whole_shard(kind, news[w], me, r, cc), sib)])
        return cps

    def start(self, srcs, dsts, news, sems):
        cps = self._copies(srcs, news, sems)
        for w in range(len(self.mats)):
            for k in (0, 1, 2, 3, 9):
                cps[w][k].start()

    def mid(self, srcs, dsts, news, sems):
        cps = self._copies(srcs, news, sems)
        for w in range(len(self.mats)):
            cps[w][0].wait_recv()
            cps[w][4].start()
            cps[w][2].wait_recv()
            cps[w][5].start()

    def finish(self, srcs, dsts, news, sems):
        cps = self._copies(srcs, news, sems)
        for w in range(len(self.mats)):
            cps[w][1].wait_recv()
            cps[w][6].start()
            cps[w][3].wait_recv()
            cps[w][7].start()
        for w in range(len(self.mats)):
            cps[w][4].wait_recv()
            cps[w][5].wait_recv()
            cps[w][8].start()
        for w in range(len(self.mats)):
            for k in (6, 7, 8, 9):
                cps[w][k].wait_recv()
            for k in range(self.NCP):
                cps[w][k].wait_send()

    def done(self, dsts, news):
        for n, a in zip(self.names, news):
            self.sink[n] = a


class _SwapJob(_Job):
    def __init__(self, build, ncopies, *, srcs=(), dsts=(), news=(), done=None):
        self.build, self.srcs, self.dsts, self.news, self._done = build, list(srcs), list(dsts), list(news), done
        self.scratch = [pltpu.SemaphoreType.DMA((ncopies,)), pltpu.SemaphoreType.DMA((ncopies,))]

    def start(self, srcs, dsts, news, sems):
        for cp in self.build(srcs, dsts, news, *sems):
            cp.start()

    def finish(self, srcs, dsts, news, sems):
        for cp in self.build(srcs, dsts, news, *sems):
            cp.wait()

    def done(self, dsts, news):
        if self._done is not None:
            self._done(dsts, news)


def _half_of_whole(kind, ref, h, r, c):
    if kind == "row":
        return ref.at[:, :, pl.ds(pl.multiple_of(h * (c // 2), 128), c // 2)]
    return ref.at[:, pl.ds(h * (r // 2), r // 2), :]


def _half_shape(kind, g, r, c):
    return {"row": (g, NCHIP * r, c // 2), "col": (g, r // 2, NCHIP * c), "stk": (NCHIP, r // 2, c)}[kind]


def _sub_shape(kind, r, c):
    return {"row": (1, r // 2, c // 2), "col": (1, r // 4, c), "stk": (1, r // 4, c)}[kind]


def _sub_of_half(kind, ref, j, p, r, c):
    sr = _sub_shape(kind, r, c)[1]
    if kind == "row":
        return ref.at[:, pl.ds(j * r + p * sr, sr), :]
    if kind == "col":
        return ref.at[:, pl.ds(p * sr, sr), _cols(j, c)]
    return ref.at[pl.ds(j, 1), pl.ds(p * sr, sr), :]


def _sub_tile(sr):
    return 256 if sr % 256 == 0 else sr


def _half_of_shard(kind, ref, h, r, c):
    if kind == "row":
        return ref.at[:, :, pl.ds(pl.multiple_of(h * (c // 2), 128), c // 2)]
    return ref.at[:, pl.ds(h * (r // 2), r // 2), :]


def _pair_sum(pack, core, mine, got, whole=True):
    name, kind, g, r, c, tr, _ = pack
    hs = _half_shape(kind, g, r, c)
    nb = hs[1] // tr

    def body(core_ref, a_ref, b_ref, o_ref):
        o_ref[...] = (a_ref[...].astype(F32) + b_ref[...].astype(F32)).astype(BF16)

    blk = (1, tr, hs[2])
    same = lambda gi, i, core_ref: (gi, i, 0)
    if not whole:
        a_map = same
    elif kind == "row":
        a_map = lambda gi, i, core_ref: (gi, i, core_ref[0])
    else:
        a_map = lambda gi, i, core_ref: (gi, core_ref[0] * nb + i, 0)
    return pl.pallas_call(
        body, name="pair_sum_" + name, out_shape=jax.ShapeDtypeStruct(hs, BF16),
        grid_spec=pltpu.PrefetchScalarGridSpec(
            num_scalar_prefetch=1, grid=(hs[0], nb),
            in_specs=[pl.BlockSpec(blk, a_map), pl.BlockSpec(blk, same)], out_specs=pl.BlockSpec(blk, same)),
        compiler_params=_cp(("parallel", "parallel")),
    )(core, mine, got)


def _sub_sum(pack, idx, half, got, first_slot, out_dtype, *, name):
    _, kind, g, r, c, _, _ = pack
    _, sr, sc = _sub_shape(kind, r, c)
    tr = _sub_tile(sr)
    nb = sr // tr

    def body(idx_ref, a_ref, b_ref, o_ref):
        o_ref[0, 0] = (a_ref[0].astype(F32) + b_ref[0, 0].astype(F32)).astype(out_dtype)

    if kind == "row":
        a_map = lambda q, i, ix: (0, ix[2 * q] * (r // tr) + ix[2 * q + 1] * nb + i, 0)
    elif kind == "col":
        a_map = lambda q, i, ix: (0, ix[2 * q + 1] * nb + i, ix[2 * q])
    else:
        a_map = lambda q, i, ix: (ix[2 * q], ix[2 * q + 1] * nb + i, 0)
    return pl.pallas_call(
        body, name=name, out_shape=jax.ShapeDtypeStruct((2, 1, sr, sc), out_dtype),
        grid_spec=pltpu.PrefetchScalarGridSpec(
            num_scalar_prefetch=1, grid=(2, nb),
            in_specs=[pl.BlockSpec((1, tr, sc), a_map),
                      pl.BlockSpec((1, 1, tr, sc), lambda q, i, ix: (first_slot + 2 * q, 0, i, 0))],
            out_specs=pl.BlockSpec((1, 1, tr, sc), lambda q, i, ix: (q, 0, i, 0))),
        compiler_params=_cp(("parallel", "parallel")),
    )(idx, half, got)


def _shard_sum(pack, core, keep, got):
    name, kind, g, r, c, _, _ = pack
    _, sr, sc = _sub_shape(kind, r, c)
    tr = _sub_tile(sr)
    nb = sr // tr

    def body(core_ref, a_ref, b_ref, o_ref):
        o_ref[0] = a_ref[0, 0] + b_ref[0, 0].astype(F32)

    blk = pl.BlockSpec((1, 1, tr, sc), lambda p, i, cr: (p, 0, i, 0))
    if kind == "row":
        o_map = lambda p, i, cr: (0, p * nb + i, cr[0])
    else:
        o_map = lambda p, i, cr: (0, cr[0] * 2 * nb + p * nb + i, 0)
    return pl.pallas_call(
        body, name="shard_sum_" + name, out_shape=jax.ShapeDtypeStruct((g, r, c), F32),
        grid_spec=pltpu.PrefetchScalarGridSpec(
            num_scalar_prefetch=1, grid=(2, nb), in_specs=[blk, blk], out_specs=pl.BlockSpec((1, tr, sc), o_map)),
        compiler_params=_cp(("parallel", "parallel")),
    )(core, keep, got)


class _Plan:
    def __init__(self, shards, table):
        self.shards, self.table = shards, table
        self.whole, self.grad, self.got_a, self.half, self.gshard = {}, {}, {}, {}, {}
        self.got_b1, self.kept, self.pass_on, self.got_b2 = {}, {}, {}, {}
        x, y, c = _pos()
        me, jx, jy = 2 * x + y, 2 * (1 - x) + y, 2 * x + (1 - y)
        self.core = c
        self.core1 = c.reshape(1).astype(jnp.int32)
        self.idx_keep = jnp.stack([me, 0 * me, me, 0 * me + 1]).astype(jnp.int32)
        self.idx_pass = jnp.stack([jy, 0 * me, jx, 0 * me + 1]).astype(jnp.int32)
        self._w_in = None
        self.send, self.keep = {}, {}

    def w(self, n):
        if n != "w_in":
            return self.whole[n][0]
        if self._w_in is None:
            self._w_in = _slabs_to_kernel_cols(self.whole[n], name="relayout_w_in", jobs=self.jobs("relayout_w_in"))
        return self._w_in

    def g(self, n, a):
        self.grad[n] = a[None]

    def g_half(self, n, which, a):
        (self.send if which == "send" else self.keep)[n] = _kernel_cols_to_slabs(a, name="relayout_d_in_" + which)

    def jobs(self, tag):
        out = []
        for spec in self.table.get(tag, ()):
            out += getattr(self, "_" + spec[0])(*spec[1:])
        return out

    def run(self, name, jobs):
        if jobs:
            _call(lambda: None, jobs=jobs, name=name, out_shape=[], in_specs=[], out_specs=[])()

    def _gather(self, names):
        return [_GatherJob(names, self.shards, self.whole)]

    def _rs_a(self, names):
        mats = [MATS[n] for n in names]

        def build(srcs, dsts, news, send, recv):
            x, y, c = _pos()
            return [_remote(srcs[i] if names[i] in self.send else _half_of_whole(kind, srcs[i], 1 - c, r, cc), news[i],
                            send.at[i], recv.at[i], (x, y, 1 - c))
                    for i, (_, kind, g, r, cc, _, _) in enumerate(mats)]

        def done(dsts, news):
            self.got_a.update(zip(names, news))

        return [_SwapJob(build, len(names), srcs=[self.send.get(n, self.grad.get(n)) for n in names], done=done,
                         news=[jax.ShapeDtypeStruct(_half_shape(kind, g, r, c), BF16) for _, kind, g, r, c, _, _ in mats])]

    def _rs_b1(self, names):
        mats = [MATS[n] for n in names]
        for n in names:
            if n in self.keep:
                self.half[n] = _pair_sum(MATS[n], self.core1, self.keep[n], self.got_a[n], whole=False)
            else:
                self.half[n] = _pair_sum(MATS[n], self.core1, self.grad[n], self.got_a[n])

        def build(srcs, dsts, news, send, recv):
            x, y, c = _pos()
            jx, jy, jd = 2 * (1 - x) + y, 2 * x + (1 - y), 2 * (1 - x) + (1 - y)
            nbx, nby = (1 - x, y, c), (x, 1 - y, c)
            cps = []
            for i, (_, kind, g, r, cc, _, _) in enumerate(mats):
                sub = lambda j, p: _sub_of_half(kind, srcs[i], j, p, r, cc)
                for k, (j, p, dev) in enumerate(((jx, 0, nbx), (jd, 0, nbx), (jy, 1, nby), (jd, 1, nby))):
                    cps.append(_remote(sub(j, p), news[i].at[k], send.at[4 * i + k], recv.at[4 * i + k], dev))
            return cps

        def done(dsts, news):
            self.got_b1.update(zip(names, news))

        return [_SwapJob(build, 4 * len(names), srcs=[self.half[n] for n in names], done=done,
                         news=[jax.ShapeDtypeStruct((4,) + _sub_shape(kind, r, c), BF16) for _, kind, g, r, c, _, _ in mats])]

    def _rs_b2(self, names):
        mats = [MATS[n] for n in names]
        for n in names:
            self.kept[n] = _sub_sum(MATS[n], self.idx_keep, self.half[n], self.got_b1[n], 0, F32, name="sum_keep_" + n)
            self.pass_on[n] = _sub_sum(MATS[n], self.idx_pass, self.half[n], self.got_b1[n], 1, BF16, name="sum_pass_" + n)

        def build(srcs, dsts, news, send, recv):
            x, y, c = _pos()
            cps = []
            for i in range(len(mats)):
                cps.append(_remote(srcs[i].at[0], news[i].at[0], send.at[2 * i], recv.at[2 * i], (x, 1 - y, c)))
                cps.append(_remote(srcs[i].at[1], news[i].at[1], send.at[2 * i + 1], recv.at[2 * i + 1], (1 - x, y, c)))
            return cps

        def done(dsts, news):
            self.got_b2.update(zip(names, news))

        return [_SwapJob(build, 2 * len(names), srcs=[self.pass_on[n] for n in names], done=done,
                         news=[jax.ShapeDtypeStruct((2,) + _sub_shape(kind, r, c), BF16) for _, kind, g, r, c, _, _ in mats])]

    def _rs_c(self, names):
        mats = [MATS[n] for n in names]
        parts = [_shard_sum(MATS[n], self.core1, self.kept[n], self.got_b2[n]) for n in names]

        def build(srcs, dsts, news, send, recv):
            x, y, c = _pos()
            cps = []
            for i, (_, kind, g, r, cc, _, _) in enumerate(mats):
                mine = _half_of_shard(kind, dsts[i], c, r, cc)
                cps.append(_remote(mine, mine, send.at[i], recv.at[i], (x, y, 1 - c)))
            return cps

        def done(dsts, news):
            self.gshard.update(zip(names, dsts))

        return [_SwapJob(build, len(names), dsts=parts, done=done)]

    def finish(self, n):
        if n not in self.got_a:
            self.run("rs_a_" + n, self._rs_a((n,)))
        if n not in self.got_b1:
            self.run("rs_b1_" + n, self._rs_b1((n,)))
        if n not in self.got_b2:
            self.run("rs_b2_" + n, self._rs_b2((n,)))
        if n not in self.gshard:
            self.run("rs_c_" + n, self._rs_c((n,)))
        return self.gshard[n]


TABLE = {
    "gather_w_in": (("gather", ("w_in",)),),
    "relayout_w_in": (("gather", ("w_gate",)),),
    "mm_in": (("gather", ("w_up",)),),
    "attn_fwd": (("gather", ("w_attn_br", "w_ssd_br")),),
    "ssd_fwd": (("gather", ("w_o",)),),
    "swiglu_fwd": (("gather", ("w_down",)),),
    "mm_down": (("gather", ("w_ple_gate", "w_ple_proj")),),
    "mm_de": (("rs_a", ("w_ple_proj", "w_ple_gate")),),
    "mm_d_down": (("rs_b1", ("w_ple_proj", "w_ple_gate")),),
    "mm_dact": (("rs_a", ("w_down",)), ("rs_b2", ("w_ple_proj", "w_ple_gate"))),
    "swiglu_bwd": (("rs_c", ("w_ple_proj", "w_ple_gate")),),
    "mm_d_gate": (("rs_b1", ("w_down",)),),
    "mm_d_up": (("rs_b2", ("w_down",)),),
    "mm_df_gate": (("rs_a", ("w_gate", "w_up")), ("rs_c", ("w_down",))),
    "mm_df_up": (("rs_b1", ("w_gate",)),),
    "norm_ffn_bwd": (("rs_b2", ("w_gate",)),),
    "mm_dmerged": (("rs_a", ("w_o",)),),
    "mm_dyn": (("rs_a", ("w_attn_br", "w_ssd_br")),),
    "attn_bwd": (("rs_b1", ("w_up", "w_o", "w_attn_br", "w_ssd_br")), ("rs_c", ("w_gate",))),
    "gated_norm_bwd": (("rs_b2", ("w_up",)),),
    "ssd_bwd": (("rs_b2", ("w_o", "w_attn_br", "w_ssd_br")),),
    "conv_bwd": (("rs_c", ("w_up", "w_o", "w_attn_br", "w_ssd_br")),),
    "mm_d_in_keep": (("rs_a", ("w_in",)),),
    "mm_du": (("rs_b1", ("w_in",)),),
    "norm_mix_bwd": (("rs_b2", ("w_in",)),),
}


NDEV = 8


def _allreduce_small(v, *, name):
    rows = v.shape[0]

    def body(v_ref, o_ref, slots, send, recv):
        x, y, c = _pos()
        me = 4 * x + 2 * y + c
        slots[me] = v_ref[...]
        cps = []
        for k in range(1, NDEV):
            peer = (_flip(x, k & 4), _flip(y, k & 2), _flip(c, k & 1))
            cp = _remote(v_ref, slots.at[me], send.at[k - 1], recv.at[k - 1], peer)
            cp.start()
            cps.append(cp)
        for cp in cps:
            cp.wait()
        acc = slots[0]
        for s in range(1, NDEV):
            acc = acc + slots[s]
        o_ref[...] = acc

    return pl.pallas_call(
        body, name=name, out_shape=jax.ShapeDtypeStruct((rows, 128), F32),
        in_specs=[pl.BlockSpec(memory_space=pltpu.VMEM)], out_specs=pl.BlockSpec(memory_space=pltpu.VMEM),
        scratch_shapes=[pltpu.VMEM((NDEV, rows, 128), F32), pltpu.SemaphoreType.DMA((NDEV - 1,)),
                        pltpu.SemaphoreType.DMA((NDEV - 1,))],
    )(v)


def _adamw(w, g, m, v, *, name, tr=None, tc=None, jobs=()):
    r, c = w.shape
    tr = r if tr is None else tr
    c1 = 1.0 / (1.0 - B1 ** STEP)
    c2 = 1.0 / (1.0 - B2 ** STEP)

    def body(w_ref, g_ref, m_ref, v_ref, d_ref, mo_ref, vo_ref):
        gv = g_ref[...]
        mn = B1 * m_ref[...] + (1.0 - B1) * gv
        vn = B2 * v_ref[...] + (1.0 - B2) * (gv * gv)
        mo_ref[...] = mn
        vo_ref[...] = vn
        d_ref[...] = -LR * ((mn * c1) / (jnp.sqrt(vn * c2) + AEPS) + WD * w_ref[...])

    if tc is None:
        blk, grid = pl.BlockSpec((tr, c), lambda i: (i, 0)), (r // tr,)
    else:
        blk, grid = pl.BlockSpec((r, tc), lambda i: (0, i)), (c // tc,)
    o = jax.ShapeDtypeStruct((r, c), F32)
    return _call(
        body, jobs=jobs, name=name, out_shape=(o, o, o), grid=grid, in_specs=[blk] * 4, out_specs=(blk, blk, blk),
        compiler_params=_cp(("parallel",)),
    )(w, g, m, v)


WEIGHTS = ("g_mix", "w_in", "conv_w", "conv_b", "dt_bias", "a_log", "d_skip", "g_ssd", "sinks", "w_attn_br", "w_ssd_br",
           "w_o", "g_ffn", "w_gate", "w_up", "w_down", "g_ple", "w_ple_gate", "w_ple_proj", "g_final")
BIG = {
    "w_gate": 256, "w_up": 256, "w_down": 128, "w_ssd_br": 128, "w_o": 128, "w_ple_gate": 128, "w_attn_br": 256,
    "w_ple_proj": 256, "w_in": None,
}
SMALL = tuple(n for n in WEIGHTS if n not in BIG)


def _pack_small(parts):
    rows = []
    for a in parts:
        a = a.reshape(-1)
        rows.append(jnp.pad(a, (0, -a.shape[0] % 128)).reshape(-1, 128))
    out = jnp.concatenate(rows, axis=0)
    return jnp.pad(out, ((0, -out.shape[0] % 8), (0, 0)))


def _unpack_small(packed, shapes):
    out, r = [], 0
    for s in shapes:
        n = int(np.prod(s))
        nr = -(-n // 128)
        out.append(packed[r:r + nr].reshape(-1)[:n].reshape(s))
        r += nr
    return out


def kernel(x, p, positions, g_mix, w_in, conv_w, conv_b, dt_bias, a_log, d_skip, g_ssd, sinks, w_attn_br, w_ssd_br, w_o, g_ffn, w_gate, w_up, w_down, g_ple, w_ple_gate, w_ple_proj, g_final, loss_target, m_g_mix, m_w_in, m_conv_w, m_conv_b, m_dt_bias, m_a_log, m_d_skip, m_g_ssd, m_sinks, m_w_attn_br, m_w_ssd_br, m_w_o, m_g_ffn, m_w_gate, m_w_up, m_w_down, m_g_ple, m_w_ple_gate, m_w_ple_proj, m_g_final, v_g_mix, v_w_in, v_conv_w, v_conv_b, v_dt_bias, v_a_log, v_d_skip, v_g_ssd, v_sinks, v_w_attn_br, v_w_ssd_br, v_w_o, v_g_ffn, v_w_gate, v_w_up, v_w_down, v_g_ple, v_w_ple_gate, v_w_ple_proj, v_g_final):
    w = dict(zip(WEIGHTS, (g_mix, w_in, conv_w, conv_b, dt_bias, a_log, d_skip, g_ssd, sinks, w_attn_br, w_ssd_br, w_o,
                           g_ffn, w_gate, w_up, w_down, g_ple, w_ple_gate, w_ple_proj, g_final)))
    m = dict(zip(WEIGHTS, (m_g_mix, m_w_in, m_conv_w, m_conv_b, m_dt_bias, m_a_log, m_d_skip, m_g_ssd, m_sinks, m_w_attn_br,
                           m_w_ssd_br, m_w_o, m_g_ffn, m_w_gate, m_w_up, m_w_down, m_g_ple, m_w_ple_gate, m_w_ple_proj,
                           m_g_final)))
    v = dict(zip(WEIGHTS, (v_g_mix, v_w_in, v_conv_w, v_conv_b, v_dt_bias, v_a_log, v_d_skip, v_g_ssd, v_sinks, v_w_attn_br,
                           v_w_ssd_br, v_w_o, v_g_ffn, v_w_gate, v_w_up, v_w_down, v_g_ple, v_w_ple_gate, v_w_ple_proj,
                           v_g_final)))
    xi, yi, ci = _pos()
    chip = 2 * xi + yi
    t = x.shape[1]
    cshard = CONV // NCHIP

    shards = {n: w[n].astype(BF16) for n in MATS}
    shards["w_in"] = jnp.pad(shards["w_in"], ((0, 0), (0, 0), (0, SLAB_PAD - SLAB)))
    plan = _Plan(shards, TABLE)
    plan.run("gather_w_in", plan.jobs("gather_w_in"))
    placed = lax.dynamic_update_slice(jnp.zeros((CW, CONV), F32), w["conv_w"][0], (0, chip * cshard))
    conv_whole = _allreduce_small(jnp.where(ci == 0, placed, 0.0).reshape(-1, 128), name="gather_conv_w").reshape(CW, CONV)

    small = {n: w[n] for n in ("g_mix", "conv_b", "dt_bias", "a_log", "d_skip", "g_ssd", "sinks", "g_ffn", "g_ple", "g_final")}
    small["conv_w"] = conv_whole
    loss8, grad_x, gs = _local_step(x[0], p[0, 0], positions, loss_target[0], small, plan)

    order = ("g_mix", "conv_b", "dt_bias", "a_log", "d_skip", "g_ssd", "sinks", "g_ffn", "g_ple", "g_final", "conv_w")
    summed = _allreduce_small(_pack_small([loss8[0, :1]] + [gs[n] for n in order]), name="sum_small")
    parts = _unpack_small(summed, [(1,)] + [w[n].shape for n in order[:-1]] + [(CW, CONV)])
    loss = parts[0][0]
    grad = dict(zip(order, parts[1:]))
    grad["conv_w"] = lax.dynamic_slice(grad["conv_w"], (0, chip * cshard), (CW, cshard))[None]

    delta, new_m, new_v = {}, {}, {}
    for n, tr in BIG.items():
        grad[n] = plan.finish(n)[:, :, :w[n].shape[2]]
        if n == "w_in":
            d_, m_, v_ = _adamw(w[n][0].T, grad[n][0].T, m[n][0].T, v[n][0].T, tc=128, name="adamw_" + n)
            d_, m_, v_ = d_.T, m_.T, v_.T
        else:
            d_, m_, v_ = _adamw(w[n][0], grad[n][0], m[n][0], v[n][0], tr=tr, name="adamw_" + n)
        delta[n], new_m[n], new_v[n] = d_[None], m_[None], v_[None]
    shapes = [w[n].shape for n in SMALL]
    d_, m_, v_ = _adamw(_pack_small([w[n] for n in SMALL]), _pack_small([grad[n] for n in SMALL]),
                        _pack_small([m[n] for n in SMALL]), _pack_small([v[n] for n in SMALL]), tr=None, name="adamw_small")
    for n, a, b, c_ in zip(SMALL, _unpack_small(d_, shapes), _unpack_small(m_, shapes), _unpack_small(v_, shapes)):
        delta[n], new_m[n], new_v[n] = a, b, c_

    return (loss, grad_x[None], *[grad[n] for n in WEIGHTS], *[delta[n] for n in WEIGHTS],
            *[new_m[n] for n in WEIGHTS], *[new_v[n] for n in WEIGHTS])
```

```python
import functools

import jax
import jax.numpy as jnp
import numpy as np
from jax import lax
from jax.experimental import pallas as pl
from jax.experimental.pallas import tpu as pltpu

F32 = jnp.float32
BF16 = jnp.bfloat16
MESH = pl.DeviceIdType.MESH

D = 2048
HD = 64
NQH = 16
NKV = 4
QD = NQH * HD
KVD = NKV * HD
DI = 2048
NH = 32
NG = 4
NS = 128
CW = 4
L = 128
CONV = DI + 2 * NG * NS
FFN = 5632
PLE = 256
IN_DIM = QD + 2 * KVD + DI + CONV + NH + 2 * D
EPS = 1e-6
SSM_EPS = 1e-5
ROPE_THETA = 10000.0
LR, B1, B2, AEPS, WD, STEP = 0.001, 0.9, 0.999, 1e-08, 0.01, 10

O_GA, O_GS, O_Z, O_XBC, O_Q, O_K, O_V, O_DT = 0, 2048, 4096, 6144, 9216, 10240, 10496, 10752
DT_PAD = 512
NP = O_DT + DT_PAD
R_Q, R_K, R_V, R_Z, R_XBC, R_DT, R_GA, R_GS = 0, 1024, 1280, 1536, 3584, 6656, 6688, 8736

NCHIP = 4
VMEM_LIMIT = 52 * 1024 * 1024
NEG = -1e30


def _cp(sem=None):
    return pltpu.CompilerParams(dimension_semantics=sem, vmem_limit_bytes=VMEM_LIMIT)


def _dot(a, b):
    return lax.dot_general(a, b, (((1,), (0,)), ((), ())), preferred_element_type=F32)


def _dot_nt(a, b):
    return lax.dot_general(a, b, (((1,), (1,)), ((), ())), preferred_element_type=F32)


def _dot_tn(a, b):
    return lax.dot_general(a, b, (((0,), (0,)), ((), ())), preferred_element_type=F32)


def _sigmoid(x):
    return 1.0 / (1.0 + jnp.exp(-x))


def _bf16_dot(dot, da, db):
    @jax.custom_vjp
    def f(a, b):
        return dot(a.astype(BF16), b.astype(BF16))

    def fwd(a, b):
        return f(a, b), (a.astype(BF16), b.astype(BF16))

    def bwd(res, g):
        a, b = res
        g = g.astype(BF16)
        return da(g, a, b), db(g, a, b)

    f.defvjp(fwd, bwd)
    return f


_bdot = _bf16_dot(_dot, lambda g, a, b: _dot_nt(g, b), lambda g, a, b: _dot_tn(a, g))
_bdot_nt = _bf16_dot(_dot_nt, lambda g, a, b: _dot(g, b), lambda g, a, b: _dot_tn(g, a))
_bdot_tn = _bf16_dot(_dot_tn, lambda g, a, b: _dot_nt(b, g), lambda g, a, b: _dot(a, g))


ANY = pl.BlockSpec(memory_space=pl.ANY)


class _Job:
    srcs, dsts, news, scratch = (), (), (), ()
    has_mid = False

    def start(self, srcs, dsts, news, sems):
        raise NotImplementedError

    def mid(self, srcs, dsts, news, sems):
        pass

    def finish(self, srcs, dsts, news, sems):
        raise NotImplementedError

    def done(self, dsts, news):
        pass


def _call(body, *, jobs=(), name, out_shape, in_specs, out_specs, grid=(), scratch_shapes=(), compiler_params=None,
          aliases=None):
    jobs = [j for j in jobs if j is not None]
    aliases = dict(aliases or {})
    if not jobs:
        return pl.pallas_call(body, name=name, out_shape=out_shape, in_specs=in_specs, out_specs=out_specs, grid=grid,
                              scratch_shapes=scratch_shapes, compiler_params=compiler_params,
                              input_output_aliases=aliases)
    single = not isinstance(out_shape, (tuple, list))
    outs = [out_shape] if single else list(out_shape)
    ospecs = [out_specs] if single else list(out_specs)
    n_in, n_out, n_scr = len(in_specs), len(outs), len(scratch_shapes)
    srcs = [a for j in jobs for a in j.srcs]
    dsts = [a for j in jobs for a in j.dsts]
    news = [a for j in jobs for a in j.news]
    sems = [a for j in jobs for a in j.scratch]

    def wrapped(*refs):
        pos = n_in + len(srcs) + len(dsts)
        ins, jsrc = refs[:n_in], refs[n_in:n_in + len(srcs)]
        o_refs = refs[pos:pos + n_out]
        pos += n_out
        jdst, jnew = refs[pos:pos + len(dsts)], refs[pos + len(dsts):pos + len(dsts) + len(news)]
        pos += len(dsts) + len(news)
        scr, jsem = refs[pos:pos + n_scr], refs[pos + n_scr:]

        def run(which):
            a = b = c = d = 0
            for j in jobs:
                getattr(j, which)(jsrc[a:a + len(j.srcs)], jdst[b:b + len(j.dsts)], jnew[c:c + len(j.news)],
                                  jsem[d:d + len(j.scratch)])
                a, b, c, d = a + len(j.srcs), b + len(j.dsts), c + len(j.news), d + len(j.scratch)

        if not grid:
            run("start")
            run("mid")
            body(*ins, *o_refs, *scr)
            run("finish")
            return
        step = functools.reduce(lambda acc, a: acc * grid[a] + pl.program_id(a), range(len(grid)), 0)
        steps = int(np.prod(grid))
        pl.when(step == 0)(lambda: run("start"))
        if any(j.has_mid for j in jobs):
            pl.when(step == steps // 3)(lambda: run("mid"))
        body(*ins, *o_refs, *scr)
        pl.when(step == steps - 1)(lambda: run("finish"))

    call = pl.pallas_call(
        wrapped, name=name,
        out_shape=outs + [jax.ShapeDtypeStruct(a.shape, a.dtype) for a in dsts] + news,
        in_specs=list(in_specs) + [ANY] * (len(srcs) + len(dsts)),
        out_specs=ospecs + [ANY] * (len(dsts) + len(news)),
        grid=grid, scratch_shapes=list(scratch_shapes) + sems,
        input_output_aliases={**aliases, **{n_in + len(srcs) + i: n_out + i for i in range(len(dsts))}},
        compiler_params=_cp(("arbitrary",) * len(grid) if grid else None))

    def run_call(*args):
        res = call(*args, *srcs, *dsts)
        b, c = n_out, n_out + len(dsts)
        for j in jobs:
            j.done(res[b:b + len(j.dsts)], res[c:c + len(j.news)])
            b, c = b + len(j.dsts), c + len(j.news)
        return res[0] if single else tuple(res[:n_out])

    return run_call


def _matmul(a, b, *, ta=False, tb=False, out_dtype=F32, add=None, tm, tn, tk, name, jobs=()):
    k, m = a.shape if ta else a.shape[::-1]
    n = b.shape[0] if tb else b.shape[1]
    assert (b.shape[1] if tb else b.shape[0]) == k and not (ta and tb)
    assert m % tm == 0 and n % tn == 0 and k % tk == 0, (name, a.shape, b.shape)
    nk = k // tk
    has_add = add is not None

    def body(*refs):
        a_ref, b_ref = refs[0], refs[1]
        add_ref = refs[2] if has_add else None
        o_ref = refs[3] if has_add else refs[2]
        av = a_ref[...].astype(BF16)
        bv = b_ref[...].astype(BF16)
        part = _dot_tn(av, bv) if ta else _dot_nt(av, bv) if tb else _dot(av, bv)

        def finish(r):
            if has_add:
                r = r + add_ref[...]
            o_ref[...] = r.astype(out_dtype)

        if nk == 1:
            finish(part)
        elif out_dtype == F32:
            kk = pl.program_id(2)
            pl.when(kk == 0)(lambda: finish(part))

            @pl.when(kk > 0)
            def _():
                o_ref[...] += part
        else:
            acc_ref = refs[-1]
            kk = pl.program_id(2)

            @pl.when(kk == 0)
            def _():
                acc_ref[...] = part

            @pl.when(kk > 0)
            def _():
                acc_ref[...] += part

            @pl.when(kk == nk - 1)
            def _():
                finish(acc_ref[...])

    in_specs = [pl.BlockSpec((tk, tm), lambda i, j, kk: (kk, i)) if ta else pl.BlockSpec((tm, tk), lambda i, j, kk: (i, kk)),
                pl.BlockSpec((tn, tk), lambda i, j, kk: (j, kk)) if tb
                else pl.BlockSpec((tk, tn), lambda i, j, kk: (kk, j))]
    args = [a, b]
    if has_add:
        in_specs.append(pl.BlockSpec((tm, tn), lambda i, j, kk: (i, j)))
        args.append(add)
    return _call(
        body, jobs=jobs, name=name,
        out_shape=jax.ShapeDtypeStruct((m, n), out_dtype),
        grid=(m // tm, n // tn, nk),
        in_specs=in_specs,
        out_specs=pl.BlockSpec((tm, tn), lambda i, j, kk: (i, j)),
        scratch_shapes=[pltpu.VMEM((tm, tn), F32)] if nk > 1 and out_dtype != F32 else [],
        compiler_params=_cp(("parallel", "parallel", "arbitrary")),
    )(*args)


ROWS = 256


def _rmsnorm_fwd(x, g, *, name):
    t, d = x.shape

    def body(x_ref, g_ref, o_ref):
        xv = x_ref[...]
        r = lax.rsqrt(jnp.mean(xv * xv, axis=-1, keepdims=True) + EPS)
        o_ref[...] = (xv * r * g_ref[...]).astype(BF16)

    return pl.pallas_call(
        body, name=name, out_shape=jax.ShapeDtypeStruct((t, d), BF16), grid=(t // ROWS,),
        in_specs=[pl.BlockSpec((ROWS, d), lambda i: (i, 0)), pl.BlockSpec((1, d), lambda i: (0, 0))],
        out_specs=pl.BlockSpec((ROWS, d), lambda i: (i, 0)), compiler_params=_cp(("parallel",)),
    )(x, g)


def _rmsnorm_bwd(x, g, dy, dres, *, name, jobs=()):
    t, d = x.shape

    def body(x_ref, g_ref, dy_ref, dres_ref, dx_ref, dxb_ref, dg_ref):
        xv = x_ref[...]
        r = lax.rsqrt(jnp.mean(xv * xv, axis=-1, keepdims=True) + EPS)
        xh = xv * r
        dyv = dy_ref[...]
        dxh = dyv * g_ref[...]
        dx = r * (dxh - xh * jnp.mean(dxh * xh, axis=-1, keepdims=True))
        tot = dres_ref[...] + dx
        dx_ref[...] = tot
        dxb_ref[...] = tot.astype(BF16)

        @pl.when(pl.program_id(0) == 0)
        def _():
            dg_ref[...] = jnp.zeros_like(dg_ref)

        dg_ref[...] += jnp.broadcast_to(jnp.sum(dyv * xh, axis=0, keepdims=True), dg_ref.shape)

    row = pl.BlockSpec((ROWS, d), lambda i: (i, 0))
    return _call(
        body, jobs=jobs, name=name,
        out_shape=(jax.ShapeDtypeStruct((t, d), F32), jax.ShapeDtypeStruct((t, d), BF16),
                   jax.ShapeDtypeStruct((8, d), F32)),
        grid=(t // ROWS,),
        in_specs=[row, pl.BlockSpec((1, d), lambda i: (0, 0)), row, row],
        out_specs=(row, row, pl.BlockSpec((8, d), lambda i: (0, 0))),
        compiler_params=_cp(("arbitrary",)),
    )(x, g, dy, dres)


def _final(h2, pgl, pp, target, g_final, *, name):
    t, d = h2.shape

    def body(h2_ref, pgl_ref, pp_ref, tg_ref, g_ref, dh3_ref, dpgl_ref, dpp_ref, loss_ref, dg_ref):
        s = _sigmoid(pgl_ref[...])
        ppv = pp_ref[...]
        h3 = h2_ref[...] + s * ppv
        r = lax.rsqrt(jnp.mean(h3 * h3, axis=-1, keepdims=True) + EPS)
        xh = h3 * r
        gv = g_ref[...]
        err = xh * gv - tg_ref[...]
        dyv = err * (1.0 / d)
        dxh = dyv * gv
        dh3 = r * (dxh - xh * jnp.mean(dxh * xh, axis=-1, keepdims=True))
        dh3_ref[...] = dh3
        dpp_ref[...] = (dh3 * s).astype(BF16)
        dpgl_ref[...] = (dh3 * ppv * s * (1.0 - s)).astype(BF16)

        @pl.when(pl.program_id(0) == 0)
        def _():
            loss_ref[...] = jnp.zeros_like(loss_ref)
            dg_ref[...] = jnp.zeros_like(dg_ref)

        part = 0.5 * jnp.sum(jnp.mean(err * err, axis=-1, keepdims=True), axis=0, keepdims=True)
        loss_ref[...] += jnp.broadcast_to(part, loss_ref.shape)
        dg_ref[...] += jnp.broadcast_to(jnp.sum(dyv * xh, axis=0, keepdims=True), dg_ref.shape)

    row = pl.BlockSpec((ROWS, d), lambda i: (i, 0))
    return pl.pallas_call(
        body, name=name,
        out_shape=(jax.ShapeDtypeStruct((t, d), F32), jax.ShapeDtypeStruct((t, d), BF16),
                   jax.ShapeDtypeStruct((t, d), BF16), jax.ShapeDtypeStruct((8, 128), F32),
                   jax.ShapeDtypeStruct((8, d), F32)),
        grid=(t // ROWS,),
        in_specs=[row, row, row, row, pl.BlockSpec((1, d), lambda i: (0, 0))],
        out_specs=(row, row, row, pl.BlockSpec((8, 128), lambda i: (0, 0)), pl.BlockSpec((8, d), lambda i: (0, 0))),
        compiler_params=_cp(("arbitrary",)),
    )(h2, pgl, pp, target, g_final)


def _merge_fwd(proj, out_a, out_s, *, name):
    t = proj.shape[0]

    def body(ga_ref, gs_ref, a_ref, s_ref, o_ref):
        o_ref[...] = (_sigmoid(ga_ref[...]) * a_ref[...] + _sigmoid(gs_ref[...]) * s_ref[...]).astype(BF16)

    row = pl.BlockSpec((ROWS, D), lambda i: (i, 0))
    return pl.pallas_call(
        body, name=name, out_shape=jax.ShapeDtypeStruct((t, D), BF16), grid=(t // ROWS,),
        in_specs=[pl.BlockSpec((ROWS, D), lambda i: (i, O_GA // D)), pl.BlockSpec((ROWS, D), lambda i: (i, O_GS // D)),
                  row, row],
        out_specs=row, compiler_params=_cp(("parallel",)),
    )(proj, proj, out_a, out_s)


def _merge_bwd(proj, out_a, out_s, dmerged, *, name):
    t = proj.shape[0]
    assert O_GA == 0 and O_GS == D

    def body(ga_ref, gs_ref, a_ref, s_ref, dm_ref, da_ref, ds_ref, dp_ref):
        sa = _sigmoid(ga_ref[...])
        ss = _sigmoid(gs_ref[...])
        dm = dm_ref[...]
        da_ref[...] = (dm * sa).astype(BF16)
        ds_ref[...] = (dm * ss).astype(BF16)
        dp_ref[:, :D] = (dm * a_ref[...] * sa * (1.0 - sa)).astype(BF16)
        dp_ref[:, D:] = (dm * s_ref[...] * ss * (1.0 - ss)).astype(BF16)

    row = pl.BlockSpec((ROWS, D), lambda i: (i, 0))
    o = jax.ShapeDtypeStruct((t, D), BF16)
    return pl.pallas_call(
        body, name=name, out_shape=(o, o, jax.ShapeDtypeStruct((t, NP), BF16)), grid=(t // ROWS,),
        in_specs=[pl.BlockSpec((ROWS, D), lambda i: (i, O_GA // D)), pl.BlockSpec((ROWS, D), lambda i: (i, O_GS // D)),
                  row, row, row],
        out_specs=(row, row, pl.BlockSpec((ROWS, 2 * D), lambda i: (i, 0))), compiler_params=_cp(("parallel",)),
    )(proj, proj, out_a, out_s, dmerged)


def _swiglu_fwd(f, w_gate, w_up, *, name, tn=512, jobs=()):
    t, d = f.shape
    n = w_gate.shape[1]

    def body(f_ref, wg_ref, wu_ref, g_ref, u_ref, a_ref):
        fv = f_ref[...]
        g = _dot(fv, wg_ref[...])
        u = _dot(fv, wu_ref[...])
        g_ref[...] = g.astype(BF16)
        u_ref[...] = u.astype(BF16)
        a_ref[...] = (g * _sigmoid(g) * u).astype(BF16)

    col = pl.BlockSpec((t, tn), lambda j: (0, j))
    wcol = pl.BlockSpec((d, tn), lambda j: (0, j))
    return _call(
        body, jobs=jobs, name=name,
        out_shape=(jax.ShapeDtypeStruct((t, n), BF16), jax.ShapeDtypeStruct((t, n), BF16),
                   jax.ShapeDtypeStruct((t, n), BF16)),
        grid=(n // tn,),
        in_specs=[pl.BlockSpec((t, d), lambda j: (0, 0)), wcol, wcol],
        out_specs=(col, col, col), compiler_params=_cp(("parallel",)),
    )(f, w_gate, w_up)


def _swiglu_bwd(dh, w_down, gate, up, *, name, tn=512, jobs=()):
    t, d = dh.shape
    n = w_down.shape[0]

    def body(dh_ref, w_ref, g_ref, u_ref, dg_ref, du_ref):
        da = _dot_nt(dh_ref[...], w_ref[...])
        g = g_ref[...].astype(F32)
        s = _sigmoid(g)
        du_ref[...] = (da * g * s).astype(BF16)
        dg_ref[...] = (da * u_ref[...].astype(F32) * s * (1.0 + g * (1.0 - s))).astype(BF16)

    col = pl.BlockSpec((t, tn), lambda j: (0, j))
    o = jax.ShapeDtypeStruct((t, n), BF16)
    return _call(
        body, jobs=jobs, name=name, out_shape=(o, o), grid=(n // tn,),
        in_specs=[pl.BlockSpec((t, d), lambda j: (0, 0)), pl.BlockSpec((tn, d), lambda j: (j, 0)), col, col],
        out_specs=(col, col), compiler_params=_cp(("parallel",)),
    )(dh, w_down, gate, up)


def _gated_norm_fwd(y_pre, proj, g_ssd, *, name):
    t = y_pre.shape[0]

    def body(y_ref, z_ref, g_ref, o_ref):
        z = z_ref[...]
        v = y_ref[...] * z * _sigmoid(z)
        r = lax.rsqrt(jnp.mean(v * v, axis=-1, keepdims=True) + SSM_EPS)
        o_ref[...] = (v * r * g_ref[...]).astype(BF16)

    row = pl.BlockSpec((ROWS, DI), lambda i: (i, 0))
    return pl.pallas_call(
        body, name=name, out_shape=jax.ShapeDtypeStruct((t, DI), BF16), grid=(t // ROWS,),
        in_specs=[row, pl.BlockSpec((ROWS, DI), lambda i: (i, O_Z // DI)), pl.BlockSpec((1, DI), lambda i: (0, 0))],
        out_specs=row, compiler_params=_cp(("parallel",)),
    )(y_pre, proj, g_ssd)


def _gated_norm_bwd(y_pre, proj, g_ssd, dyn, dproj, *, name, jobs=()):
    t = y_pre.shape[0]

    def body(y_ref, z_ref, g_ref, dyn_ref, _, dy_ref, dz_ref, dg_ref):
        z = z_ref[...]
        s = _sigmoid(z)
        sz = z * s
        yv = y_ref[...]
        v = yv * sz
        r = lax.rsqrt(jnp.mean(v * v, axis=-1, keepdims=True) + SSM_EPS)
        vh = v * r
        dn = dyn_ref[...]
        dvh = dn * g_ref[...]
        dv = r * (dvh - vh * jnp.mean(dvh * vh, axis=-1, keepdims=True))
        dy_ref[...] = dv * sz
        dz_ref[...] = (dv * yv * s * (1.0 + z * (1.0 - s))).astype(BF16)

        @pl.when(pl.program_id(0) == 0)
        def _():
            dg_ref[...] = jnp.zeros_like(dg_ref)

        dg_ref[...] += jnp.broadcast_to(jnp.sum(dn * vh, axis=0, keepdims=True), dg_ref.shape)

    row = pl.BlockSpec((ROWS, DI), lambda i: (i, 0))
    return _call(
        body, jobs=jobs, name=name,
        out_shape=(jax.ShapeDtypeStruct((t, DI), F32), jax.ShapeDtypeStruct(dproj.shape, BF16),
                   jax.ShapeDtypeStruct((8, DI), F32)),
        grid=(t // ROWS,),
        in_specs=[row, pl.BlockSpec((ROWS, DI), lambda i: (i, O_Z // DI)), pl.BlockSpec((1, DI), lambda i: (0, 0)), row, ANY],
        out_specs=(row, pl.BlockSpec((ROWS, DI), lambda i: (i, O_Z // DI)), pl.BlockSpec((8, DI), lambda i: (0, 0))),
        compiler_params=_cp(("arbitrary",)), aliases={4: 1},
    )(y_pre, proj, g_ssd, dyn, dproj)


CONV_TC = 512


def _shift_down(x, s, row):
    if s == 0:
        return x
    return jnp.where(row >= s, pltpu.roll(x, s, 0), 0.0)


def _shift_up(x, s, row, t):
    if s == 0:
        return x
    return jnp.where(row < t - s, pltpu.roll(x, t - s, 0), 0.0)


def _conv_fwd(proj, conv_w, conv_b, *, name):
    t = proj.shape[0]

    def body(x_ref, w_ref, b_ref, o_ref):
        x = x_ref[...]
        row = lax.broadcasted_iota(jnp.int32, x.shape, 0)
        pre = jnp.broadcast_to(b_ref[...], x.shape)
        for k in range(CW):
            pre = pre + w_ref[k:k + 1, :] * _shift_down(x, CW - 1 - k, row)
        o_ref[...] = pre * _sigmoid(pre)

    return pl.pallas_call(
        body, name=name, out_shape=jax.ShapeDtypeStruct((t, CONV), F32), grid=(CONV // CONV_TC,),
        in_specs=[pl.BlockSpec((t, CONV_TC), lambda j: (0, O_XBC // CONV_TC + j)),
                  pl.BlockSpec((CW, CONV_TC), lambda j: (0, j)), pl.BlockSpec((1, CONV_TC), lambda j: (0, j))],
        out_specs=pl.BlockSpec((t, CONV_TC), lambda j: (0, j)), compiler_params=_cp(("parallel",)),
    )(proj, conv_w, conv_b)


def _conv_bwd(proj, conv_w, conv_b, dxs, db, dc, dproj, *, name, jobs=()):
    t = proj.shape[0]
    nx = DI // CONV_TC
    assert NG * NS == CONV_TC

    def body(x_ref, w_ref, b_ref, dxs_ref, db_ref, dc_ref, _, dx_ref, dw_ref, dbias_ref):
        j = pl.program_id(0)
        x = x_ref[...]
        row = lax.broadcasted_iota(jnp.int32, x.shape, 0)
        xs = [_shift_down(x, CW - 1 - k, row) for k in range(CW)]
        pre = jnp.broadcast_to(b_ref[...], x.shape)
        for k in range(CW):
            pre = pre + w_ref[k:k + 1, :] * xs[k]
        s = _sigmoid(pre)
        da = jnp.where(j < nx, dxs_ref[...], jnp.where(j == nx, db_ref[...], dc_ref[...]))
        dpre = da * s * (1.0 + pre * (1.0 - s))
        dx = jnp.zeros_like(x)
        row8 = lax.broadcasted_iota(jnp.int32, dw_ref.shape, 0)
        dw = jnp.zeros(dw_ref.shape, F32)
        for k in range(CW):
            dx = dx + w_ref[k:k + 1, :] * _shift_up(dpre, CW - 1 - k, row, t)
            dw = dw + jnp.where(row8 == k, jnp.sum(dpre * xs[k], axis=0, keepdims=True), 0.0)
        dx_ref[...] = dx.astype(BF16)
        dw_ref[...] = dw
        dbias_ref[...] = jnp.broadcast_to(jnp.sum(dpre, axis=0, keepdims=True), dbias_ref.shape)

    col8 = pl.BlockSpec((8, CONV_TC), lambda j: (0, j))
    xbc = pl.BlockSpec((t, CONV_TC), lambda j: (0, O_XBC // CONV_TC + j))
    whole = pl.BlockSpec((t, CONV_TC), lambda j: (0, 0))
    return _call(
        body, jobs=jobs, name=name,
        out_shape=(jax.ShapeDtypeStruct(dproj.shape, BF16), jax.ShapeDtypeStruct((8, CONV), F32),
                   jax.ShapeDtypeStruct((8, CONV), F32)),
        grid=(CONV // CONV_TC,),
        in_specs=[xbc, pl.BlockSpec((CW, CONV_TC), lambda j: (0, j)), pl.BlockSpec((1, CONV_TC), lambda j: (0, j)),
                  pl.BlockSpec((t, CONV_TC), lambda j: (0, jnp.minimum(j, nx - 1))), whole, whole, ANY],
        out_specs=(xbc, col8, col8),
        compiler_params=_cp(("arbitrary",)), aliases={6: 0},
    )(proj, conv_w, conv_b, dxs, db, dc, dproj)


def _rope_tables(positions, t):
    half = HD // 2
    inv_freq = ROPE_THETA ** (-jnp.arange(half, dtype=F32) * 2.0 / HD)
    ang = positions.reshape(t).astype(F32)[:, None] * inv_freq
    cos, sin = jnp.cos(ang), jnp.sin(ang)
    return jnp.concatenate([cos] * 4, axis=1), jnp.concatenate([-sin, sin] * 2, axis=1)


def _lane_consts():
    lane = lax.broadcasted_iota(jnp.int32, (L, 128), 1)
    return lane, (lane % HD) < (HD // 2), lane < HD


def _rope(tv, cos, sin, lo):
    return tv * cos + jnp.where(lo, pltpu.roll(tv, 128 - HD // 2, 1), pltpu.roll(tv, HD // 2, 1)) * sin


def _rope_t(dv, cos, sin, lo):
    ds = dv * sin
    return dv * cos + jnp.where(lo, pltpu.roll(ds, 128 - HD // 2, 1), pltpu.roll(ds, HD // 2, 1))


def _placed(chunk, g, half0):
    own = jnp.where(half0 if g % 2 == 0 else jnp.logical_not(half0), chunk, 0.0)
    other = pltpu.roll(own, HD, 1)
    return (own, other) if g % 2 == 0 else (other, own)


def _unplace(acc, hf, g, half0):
    v = jnp.where(half0 if hf == 0 else jnp.logical_not(half0), acc, 0.0)
    return v if hf == g % 2 else pltpu.roll(v, HD, 1)


def _attn_fwd(proj, cos, sin, sinks, *, name, jobs=()):
    t = proj.shape[0]
    nb = t // L
    scale = HD ** -0.5

    def body(sink_ref, q_ref, kc_ref, kp_ref, vc_ref, vp_ref, cc_ref, sc_ref, cp_ref, sp_ref, o_ref, lse_ref):
        i = pl.program_id(0)
        lane, lo, half0 = _lane_consts()
        cos_c, sin_c, cos_p, sin_p = cc_ref[...], sc_ref[...], cp_ref[...], sp_ref[...]
        row = lax.broadcasted_iota(jnp.int32, (L, 2 * L), 0)
        col = lax.broadcasted_iota(jnp.int32, (L, 2 * L), 1)
        valid = jnp.logical_or(jnp.logical_and(jnp.logical_and(col < L, col > row), i > 0),
                               jnp.logical_and(col >= L, col - L <= row))
        kc = [_rope(kc_ref[:, 128 * m:128 * (m + 1)], cos_c, sin_c, lo) for m in range(2)]
        kp = [_rope(kp_ref[:, 128 * m:128 * (m + 1)], cos_p, sin_p, lo) for m in range(2)]
        lse_acc = jnp.zeros((L, 128), F32)
        outs = [jnp.zeros((L, 128), F32) for _ in range(QD // 128)]
        qs = [(_rope(q_ref[:, 128 * ch:128 * (ch + 1)], cos_c, sin_c, lo) * scale).astype(BF16) for ch in range(QD // 128)]
        both = lambda prev, cur, g: [jnp.concatenate([a, b], axis=0).astype(BF16)
                                     for a, b in zip(_placed(prev, g, half0), _placed(cur, g, half0))]
        for g in range(NKV):
            sl = slice(128 * (g // 2), 128 * (g // 2 + 1))
            kv = both(kp[g // 2], kc[g // 2], g)
            vv = both(vp_ref[:, sl], vc_ref[:, sl], g)
            for r in range(NQH // NKV):
                h = g * (NQH // NKV) + r
                ch, hf = h // 2, h % 2
                s = jnp.where(valid, _dot_nt(qs[ch], kv[hf]), NEG)
                sink = sink_ref[0, h]
                mx = jnp.maximum(jnp.max(s, axis=-1, keepdims=True), sink)
                e = jnp.exp(s - mx)
                den = jnp.sum(e, axis=-1, keepdims=True) + jnp.exp(sink - mx)
                outs[ch] = outs[ch] + _dot((e * (1.0 / den)).astype(BF16), vv[hf])
                lse_acc = jnp.where(lane == h, mx + jnp.log(den), lse_acc)
        for ch in range(QD // 128):
            o_ref[:, 128 * ch:128 * (ch + 1)] = outs[ch].astype(BF16)
        lse_ref[...] = lse_acc

    prev = lambda i: jnp.maximum(i - 1, 0)
    tab_c = pl.BlockSpec((L, 128), lambda i: (i, 0))
    tab_p = pl.BlockSpec((L, 128), lambda i: (prev(i), 0))
    return _call(
        body, jobs=jobs, name=name,
        out_shape=(jax.ShapeDtypeStruct((t, QD), BF16), jax.ShapeDtypeStruct((t, 128), F32)),
        grid=(nb,),
        in_specs=[pl.BlockSpec(memory_space=pltpu.SMEM),
                  pl.BlockSpec((L, QD), lambda i: (i, O_Q // QD)),
                  pl.BlockSpec((L, KVD), lambda i: (i, O_K // KVD)), pl.BlockSpec((L, KVD), lambda i: (prev(i), O_K // KVD)),
                  pl.BlockSpec((L, KVD), lambda i: (i, O_V // KVD)), pl.BlockSpec((L, KVD), lambda i: (prev(i), O_V // KVD)),
                  tab_c, tab_c, tab_p, tab_p],
        out_specs=(pl.BlockSpec((L, QD), lambda i: (i, 0)), pl.BlockSpec((L, 128), lambda i: (i, 0))),
        compiler_params=_cp(("parallel",)),
    )(sinks, proj, proj, proj, proj, proj, cos, sin, cos, sin)


def _attn_bwd(proj, cos, sin, sinks, attn, lse, dattn, dproj, *, name, jobs=()):
    t = proj.shape[0]
    nb = t // L
    scale = HD ** -0.5

    def body(sink_ref, qi_ref, qn_ref, kc_ref, kp_ref, vc_ref, vp_ref, doi_ref, don_ref, oi_ref, on_ref,
             lsei_ref, lsen_ref, cc_ref, sc_ref, cp_ref, sp_ref, cn_ref, sn_ref, _, dqkv_ref, dsk_ref):
        i = pl.program_id(0)
        lane, lo, half0 = _lane_consts()
        half1 = jnp.logical_not(half0)
        cos_c, sin_c = cc_ref[...], sc_ref[...]
        row = lax.broadcasted_iota(jnp.int32, (L, 2 * L), 0)
        col = lax.broadcasted_iota(jnp.int32, (L, 2 * L), 1)
        valid = jnp.logical_or(jnp.logical_and(jnp.logical_and(col < L, col > row), i > 0),
                               jnp.logical_and(col >= L, col - L <= row))
        m_next = jnp.logical_and(col[:, :L] > row[:, :L], i < nb - 1)
        kc = [_rope(kc_ref[:, 128 * m:128 * (m + 1)], cos_c, sin_c, lo) for m in range(2)]
        kp = [_rope(kp_ref[:, 128 * m:128 * (m + 1)], cp_ref[...], sp_ref[...], lo) for m in range(2)]
        lse_i, lse_n = lsei_ref[...], lsen_ref[...]
        dk_acc = [jnp.zeros((L, 128), F32) for _ in range(2)]
        dv_acc = [jnp.zeros((L, 128), F32) for _ in range(2)]
        dsk_acc = jnp.zeros((1, 128), F32)
        lane1 = lax.broadcasted_iota(jnp.int32, (1, 128), 1)
        both = lambda prev, cur, g: [jnp.concatenate([a, b], axis=0).astype(BF16)
                                     for a, b in zip(_placed(prev, g, half0), _placed(cur, g, half0))]
        kvs = [both(kp[g // 2], kc[g // 2], g) for g in range(NKV)]
        vvs = [both(vp_ref[:, 128 * (g // 2):128 * (g // 2 + 1)], vc_ref[:, 128 * (g // 2):128 * (g // 2 + 1)], g)
               for g in range(NKV)]
        for ch in range(QD // 128):
            sl = slice(128 * ch, 128 * (ch + 1))
            q_i = (_rope(qi_ref[:, sl], cos_c, sin_c, lo) * scale).astype(BF16)
            q_n = (_rope(qn_ref[:, sl], cn_ref[...], sn_ref[...], lo) * scale).astype(BF16)
            q_in = jnp.concatenate([q_i, q_n], axis=0)
            do_i, do_n = doi_ref[:, sl], don_ref[:, sl]
            do_ib, do_nb = do_i.astype(BF16), do_n.astype(BF16)
            do_in = jnp.concatenate([do_ib, do_nb], axis=0)
            od_i = do_i * oi_ref[:, sl].astype(F32)
            od_n = do_n * on_ref[:, sl].astype(F32)
            dq_ch = jnp.zeros((L, 128), F32)
            for hf in range(2):
                h = 2 * ch + hf
                g = h // (NQH // NKV)
                hm = half0 if hf == 0 else half1
                kv, vv = kvs[g][hf], vvs[g][hf]
                kcv, vcv = kv[L:], vv[L:]
                dl_i = jnp.sum(jnp.where(hm, od_i, 0.0), axis=-1, keepdims=True)
                dl_n = jnp.sum(jnp.where(hm, od_n, 0.0), axis=-1, keepdims=True)
                ls_i = jnp.sum(jnp.where(lane == h, lse_i, 0.0), axis=-1, keepdims=True)
                ls_n = jnp.sum(jnp.where(lane == h, lse_n, 0.0), axis=-1, keepdims=True)
                p = jnp.where(valid, jnp.exp(_dot_nt(q_i, kv) - ls_i), 0.0)
                ds = (p * (_dot_nt(do_ib, vv) - dl_i)).astype(BF16)
                dq_ch = dq_ch + jnp.where(hm, _dot(ds, kv) * scale, 0.0)
                sink = sink_ref[0, h]
                dsk = -jnp.sum(jnp.exp(sink - ls_i) * dl_i, axis=0, keepdims=True)
                dsk_acc = dsk_acc + jnp.where(lane1 == h, dsk, 0.0)
                p_n = jnp.where(m_next, jnp.exp(_dot_nt(q_n, kcv) - ls_n), 0.0)
                ds_n = (p_n * (_dot_nt(do_nb, vcv) - dl_n)).astype(BF16)
                dv_h = _dot_tn(jnp.concatenate([p[:, L:].astype(BF16), p_n.astype(BF16)], axis=0), do_in)
                dk_h = _dot_tn(jnp.concatenate([ds[:, L:], ds_n], axis=0), q_in)
                dv_acc[g // 2] = dv_acc[g // 2] + _unplace(dv_h, hf, g, half0)
                dk_acc[g // 2] = dk_acc[g // 2] + _unplace(dk_h, hf, g, half0)
            dqkv_ref[:, sl] = _rope_t(dq_ch, cos_c, sin_c, lo).astype(BF16)
        for m in range(2):
            dqkv_ref[:, QD + 128 * m:QD + 128 * (m + 1)] = _rope_t(dk_acc[m], cos_c, sin_c, lo).astype(BF16)
            dqkv_ref[:, QD + KVD + 128 * m:QD + KVD + 128 * (m + 1)] = dv_acc[m].astype(BF16)

        @pl.when(i == 0)
        def _():
            dsk_ref[...] = jnp.zeros_like(dsk_ref)

        dsk_ref[...] += jnp.broadcast_to(dsk_acc, dsk_ref.shape)

    prev = lambda i: jnp.maximum(i - 1, 0)
    nxt = lambda i: jnp.minimum(i + 1, nb - 1)
    cur_q = pl.BlockSpec((L, QD), lambda i: (i, 0))
    nxt_q = pl.BlockSpec((L, QD), lambda i: (nxt(i), 0))
    tab = lambda f: pl.BlockSpec((L, 128), lambda i: (f(i), 0))
    ident = lambda i: i
    qkv = QD + 2 * KVD
    assert O_K == O_Q + QD and O_V == O_K + KVD and O_Q % qkv == 0
    return _call(
        body, jobs=jobs, name=name,
        out_shape=(jax.ShapeDtypeStruct(dproj.shape, BF16), jax.ShapeDtypeStruct((8, 128), F32)),
        grid=(nb,),
        in_specs=[pl.BlockSpec(memory_space=pltpu.SMEM),
                  pl.BlockSpec((L, QD), lambda i: (i, O_Q // QD)), pl.BlockSpec((L, QD), lambda i: (nxt(i), O_Q // QD)),
                  pl.BlockSpec((L, KVD), lambda i: (i, O_K // KVD)), pl.BlockSpec((L, KVD), lambda i: (prev(i), O_K // KVD)),
                  pl.BlockSpec((L, KVD), lambda i: (i, O_V // KVD)), pl.BlockSpec((L, KVD), lambda i: (prev(i), O_V // KVD)),
                  cur_q, nxt_q, cur_q, nxt_q, tab(ident), tab(nxt),
                  tab(ident), tab(ident), tab(prev), tab(prev), tab(nxt), tab(nxt), ANY],
        out_specs=(pl.BlockSpec((L, qkv), lambda i: (i, O_Q // qkv)), pl.BlockSpec((8, 128), lambda i: (0, 0))),
        compiler_params=_cp(("arbitrary",)), aliases={19: 0},
    )(sinks, proj, proj, proj, proj, proj, proj, dattn, dattn, attn, attn, lse, lse, cos, sin, cos, sin, cos, sin, dproj)


PAIRS = NH // NG // 2


def _softplus(x):
    return jnp.maximum(x, 0.0) + jnp.log(1.0 + jnp.exp(-jnp.abs(x)))


def _ssd_chunk(g, xps, dtr, bm, cm, sps, dtb, alog, dsk):
    lane = lax.broadcasted_iota(jnp.int32, (L, 128), 1)
    lane1 = lax.broadcasted_iota(jnp.int32, (1, 128), 1)
    row = lax.broadcasted_iota(jnp.int32, (L, L), 0)
    col = lax.broadcasted_iota(jnp.int32, (L, L), 1)
    rowc = lax.broadcasted_iota(jnp.int32, (128, 1), 0)
    tril = col <= row
    dt = _softplus(dtr + dtb)
    a = dt * (-jnp.exp(alog))
    a_cs = lax.dot_general(tril.astype(F32), a, (((1,), (0,)), ((), ())), precision=lax.Precision.HIGHEST,
                           preferred_element_type=F32)
    a_cst = a_cs.T
    a_last = jnp.sum(jnp.where(row == L - 1, a_cs, 0.0), axis=0, keepdims=True)
    cb = _bdot_nt(cm, bm)
    ys, snew = [], []
    for q in range(PAIRS):
        xp, sp = xps[q], sps[q]
        y_pair = jnp.zeros((L, 128), F32)
        st_pair = jnp.zeros((128, NS), F32)
        keep = jnp.zeros((128, 1), F32)
        for hh in range(2):
            h = g * 2 * PAIRS + 2 * q + hh
            hm = (lane < HD) if hh == 0 else (lane >= HD)
            rm = (rowc < HD) if hh == 0 else (rowc >= HD)
            dt_h = jnp.sum(jnp.where(lane == h, dt, 0.0), axis=1, keepdims=True)
            acs_h = jnp.sum(jnp.where(lane == h, a_cs, 0.0), axis=1, keepdims=True)
            acst_h = jnp.sum(jnp.where(row == h, a_cst, 0.0), axis=0, keepdims=True)
            al_h = jnp.sum(jnp.where(lane1 == h, a_last, 0.0), axis=1, keepdims=True)
            dsk_h = jnp.sum(jnp.where(lane1 == h, dsk, 0.0), axis=1, keepdims=True)
            decay = jnp.where(tril, jnp.exp(jnp.where(tril, acs_h - acst_h, 0.0)), 0.0)
            xh = jnp.where(hm, xp, 0.0)
            xd = xh * dt_h
            y = _bdot(cb * decay, xd)
            y = y + jnp.where(hm, _bdot_nt(cm * jnp.exp(acs_h), sp), 0.0)
            y_pair = y_pair + y + dsk_h * xh
            st_pair = st_pair + _bdot_tn(xd, bm * jnp.exp(al_h - acs_h))
            keep = keep + jnp.where(rm, jnp.exp(al_h), 0.0)
        ys.append(y_pair)
        snew.append(sp * keep + st_pair)
    return ys, snew


def _ssd_specs(t):
    nc = t // L
    xs = lambda f: pl.BlockSpec((L, 128 * PAIRS), lambda c, g: (f(c), g))
    bspec = lambda f: pl.BlockSpec((L, NS), lambda c, g: (f(c), DI // NS + g))
    cspec = lambda f: pl.BlockSpec((L, NS), lambda c, g: (f(c), DI // NS + NG + g))
    dts = lambda f: pl.BlockSpec((L, 128), lambda c, g: (f(c), O_DT // 128))
    par = pl.BlockSpec((1, 128), lambda c, g: (0, 0))
    st = lambda f: pl.BlockSpec((1, 1, PAIRS, 128, NS), lambda c, g: (f(c), g, 0, 0, 0))
    return nc, xs, bspec, cspec, dts, par, st


def _ssd_fwd(xbc_act, proj, dtb, alog, dsk, *, name, jobs=()):
    t = proj.shape[0]
    nc, xs, bspec, cspec, dts, par, st = _ssd_specs(t)
    ident = lambda c: c

    def body(x_ref, b_ref, c_ref, dt_ref, dtb_ref, al_ref, dsk_ref, y_ref, sin_ref, s_ref):
        c, g = pl.program_id(0), pl.program_id(1)

        @pl.when(c == 0)
        def _():
            s_ref[g] = jnp.zeros((PAIRS, 128, NS), F32)

        sps = [s_ref[g, q] for q in range(PAIRS)]
        for q in range(PAIRS):
            sin_ref[0, 0, q] = sps[q]
        xps = [x_ref[:, 128 * q:128 * (q + 1)] for q in range(PAIRS)]
        ys, snew = _ssd_chunk(g, xps, dt_ref[...], b_ref[...], c_ref[...], sps, dtb_ref[...], al_ref[...], dsk_ref[...])
        for q in range(PAIRS):
            y_ref[:, 128 * q:128 * (q + 1)] = ys[q]
            s_ref[g, q] = snew[q]

    return _call(
        body, jobs=jobs, name=name,
        out_shape=(jax.ShapeDtypeStruct((t, DI), F32), jax.ShapeDtypeStruct((nc, NG, PAIRS, 128, NS), F32)),
        grid=(nc, NG),
        in_specs=[xs(ident), bspec(ident), cspec(ident), dts(ident), par, par, par],
        out_specs=(pl.BlockSpec((L, 128 * PAIRS), lambda c, g: (c, g)), st(ident)),
        scratch_shapes=[pltpu.VMEM((NG, PAIRS, 128, NS), F32)],
        compiler_params=_cp(("arbitrary", "arbitrary")),
    )(xbc_act, xbc_act, xbc_act, proj, dtb, alog, dsk)


def _ssd_bwd(xbc_act, proj, dtb, alog, dsk, states, dy, dproj, *, name, jobs=()):
    t = proj.shape[0]
    nc, xs, bspec, cspec, dts, par, st = _ssd_specs(t)
    rev = lambda c: nc - 1 - c

    def body(x_ref, b_ref, c_ref, dt_ref, dtb_ref, al_ref, dsk_ref, sin_ref, dy_ref, _,
             dx_ref, db_ref, dc_ref, ddtp_ref, ddtb_ref, dal_ref, ddsk_ref, ds_ref, ddt_ref):
        c, g = pl.program_id(0), pl.program_id(1)

        @pl.when(c == 0)
        def _():
            ds_ref[g] = jnp.zeros((PAIRS, 128, NS), F32)

        @pl.when(jnp.logical_and(c == 0, g == 0))
        def _():
            ddtb_ref[...] = jnp.zeros_like(ddtb_ref)
            dal_ref[...] = jnp.zeros_like(dal_ref)
            ddsk_ref[...] = jnp.zeros_like(ddsk_ref)

        @pl.when(g == 0)
        def _():
            ddt_ref[...] = jnp.zeros_like(ddt_ref)

        sps = [sin_ref[0, 0, q] for q in range(PAIRS)]
        xps = [x_ref[:, 128 * q:128 * (q + 1)] for q in range(PAIRS)]
        _, vjp = jax.vjp(functools.partial(_ssd_chunk, g), xps, dt_ref[...], b_ref[...], c_ref[...], sps,
                         dtb_ref[...], al_ref[...], dsk_ref[...])
        dys = [dy_ref[:, 128 * q:128 * (q + 1)] for q in range(PAIRS)]
        dss = [ds_ref[g, q] for q in range(PAIRS)]
        dxps, ddt, db, dc, dsps, ddtb, dal, ddsk = vjp((dys, dss))
        for q in range(PAIRS):
            dx_ref[:, 128 * q:128 * (q + 1)] = dxps[q]
            ds_ref[g, q] = dsps[q]
        db_ref[...] = db
        dc_ref[...] = dc
        ddt_ref[...] += ddt
        ddtb_ref[...] += jnp.broadcast_to(ddtb, ddtb_ref.shape)
        dal_ref[...] += jnp.broadcast_to(dal, dal_ref.shape)
        ddsk_ref[...] += jnp.broadcast_to(ddsk, ddsk_ref.shape)

        @pl.when(g == NG - 1)
        def _():
            ddtp_ref[:, :128] = ddt_ref[...].astype(BF16)
            ddtp_ref[:, 128:] = jnp.zeros((L, DT_PAD - 128), BF16)

    acc = pl.BlockSpec((8, 128), lambda c, g: (0, 0))
    o8 = jax.ShapeDtypeStruct((8, 128), F32)
    return _call(
        body, jobs=jobs, name=name,
        out_shape=(jax.ShapeDtypeStruct((t, DI), F32), jax.ShapeDtypeStruct((t, NG * NS), F32),
                   jax.ShapeDtypeStruct((t, NG * NS), F32), jax.ShapeDtypeStruct(dproj.shape, BF16), o8, o8, o8),
        grid=(nc, NG),
        in_specs=[xs(rev), bspec(rev), cspec(rev), dts(rev), par, par, par, st(rev),
                  pl.BlockSpec((L, 128 * PAIRS), lambda c, g: (rev(c), g)), ANY],
        out_specs=(pl.BlockSpec((L, 128 * PAIRS), lambda c, g: (rev(c), g)),
                   pl.BlockSpec((L, NS), lambda c, g: (rev(c), g)), pl.BlockSpec((L, NS), lambda c, g: (rev(c), g)),
                   pl.BlockSpec((L, DT_PAD), lambda c, g: (rev(c), O_DT // DT_PAD)), acc, acc, acc),
        scratch_shapes=[pltpu.VMEM((NG, PAIRS, 128, NS), F32), pltpu.VMEM((L, 128), F32)],
        compiler_params=_cp(("arbitrary", "arbitrary")), aliases={9: 3},
    )(xbc_act, xbc_act, xbc_act, proj, dtb, alog, dsk, states, dy, dproj)


def _pad_lanes(v, n=128):
    return jnp.pad(v, ((0, 0), (0, n - v.shape[1])))


class _LocalPlan:
    core = 0

    def __init__(self, big):
        self.big, self.grad, self.halves = big, {}, {}

    def w(self, n):
        return self.big[n]

    def g(self, n, a):
        self.grad[n] = a

    def g_half(self, n, which, a):
        self.halves[which] = a
        if len(self.halves) == 2:
            self.grad[n] = jnp.concatenate([self.halves["keep"], self.halves["send"]], axis=0)

    def jobs(self, tag):
        return ()


def _local_step(x, p, positions, target, small, plan):
    t = x.shape[0]
    cos, sin = _rope_tables(positions, t)
    dtb, alog, dsk = _pad_lanes(small["dt_bias"]), _pad_lanes(small["a_log"]), _pad_lanes(small["d_skip"])
    w, jobs = plan.w, plan.jobs

    def mm(a, b, *, name, tn=512, **kw):
        return _matmul(a, b, tm=t, tn=tn, name=name, jobs=jobs(name), **kw)

    tkl = FFN // 4

    def dw(wname, a, dy, *, name, tm):
        plan.g(wname, _matmul(a, dy, ta=True, out_dtype=BF16, tm=tm, tn=512, tk=t, name=name, jobs=jobs(name)))

    u = _rmsnorm_fwd(x, small["g_mix"], name="norm_mix")
    proj = mm(u, w("w_in"), tk=D, name="mm_in")
    attn, lse = _attn_fwd(proj, cos, sin, small["sinks"], name="attn_fwd", jobs=jobs("attn_fwd"))
    out_a = mm(attn, w("w_attn_br"), tk=QD, name="mm_attn_br")
    xbc_act = _conv_fwd(proj, small["conv_w"], small["conv_b"], name="conv_fwd")
    y_pre, states = _ssd_fwd(xbc_act, proj, dtb, alog, dsk, name="ssd_fwd", jobs=jobs("ssd_fwd"))
    yn = _gated_norm_fwd(y_pre, proj, small["g_ssd"], name="gated_norm_fwd")
    out_s = mm(yn, w("w_ssd_br"), tk=DI, name="mm_ssd_br")
    merged = _merge_fwd(proj, out_a, out_s, name="merge_fwd")
    h1 = mm(merged, w("w_o"), add=x, tk=D, name="mm_o")
    f = _rmsnorm_fwd(h1, small["g_ffn"], name="norm_ffn")
    gate, up, act = _swiglu_fwd(f, w("w_gate"), w("w_up"), name="swiglu_fwd", jobs=jobs("swiglu_fwd"))
    h2 = mm(act, w("w_down"), add=h1, tk=tkl, name="mm_down")
    e = _rmsnorm_fwd(h2, small["g_ple"], name="norm_ple")
    pgl = mm(e, w("w_ple_gate"), tk=D, name="mm_ple_gate")
    pb = p.astype(BF16)
    pp = mm(pb, w("w_ple_proj"), tk=PLE, name="mm_ple_proj")
    dh3, dpgl, dpp, loss, dg_final = _final(h2, pgl, pp, target, small["g_final"].reshape(1, D), name="final")

    dw("w_ple_proj", pb, dpp, tm=PLE, name="mm_d_ple_proj")
    dw("w_ple_gate", e, dpgl, tm=D, name="mm_d_ple_gate")
    de = mm(dpgl, w("w_ple_gate"), tb=True, tk=D, name="mm_de")
    dh2, dh2b, dg_ple = _rmsnorm_bwd(h2, small["g_ple"], de, dh3, name="norm_ple_bwd", jobs=jobs("norm_ple_bwd"))
    dw("w_down", act, dh2b, tm=FFN // 2, name="mm_d_down")
    dgate, dup = _swiglu_bwd(dh2b, w("w_down"), gate, up, name="swiglu_bwd", jobs=jobs("swiglu_bwd"))
    dw("w_gate", f, dgate, tm=D, name="mm_d_gate")
    dw("w_up", f, dup, tm=D, name="mm_d_up")
    df = mm(dgate, w("w_gate"), tb=True, tn=1024, tk=tkl, name="mm_df_gate")
    df = mm(dup, w("w_up"), tb=True, add=df, tk=tkl, name="mm_df_up")
    dh1, dh1b, dg_ffn = _rmsnorm_bwd(h1, small["g_ffn"], df, dh2, name="norm_ffn_bwd", jobs=jobs("norm_ffn_bwd"))
    dw("w_o", merged, dh1b, tm=D, name="mm_d_o")
    dmerged = mm(dh1b, w("w_o"), tb=True, tk=D, name="mm_dmerged")
    dout_a, dout_s, dproj = _merge_bwd(proj, out_a, out_s, dmerged, name="merge_bwd")
    dw("w_attn_br", attn, dout_a, tm=QD, name="mm_d_attn_br")
    dw("w_ssd_br", yn, dout_s, tm=DI, name="mm_d_ssd_br")
    dattn = mm(dout_a, w("w_attn_br"), tb=True, tk=D, name="mm_dattn")
    dyn = mm(dout_s, w("w_ssd_br"), tb=True, tk=D, name="mm_dyn")
    dproj, dsinks = _attn_bwd(proj, cos, sin, small["sinks"], attn, lse, dattn, dproj, name="attn_bwd",
                              jobs=jobs("attn_bwd"))
    dy_pre, dproj, dg_ssd = _gated_norm_bwd(y_pre, proj, small["g_ssd"], dyn, dproj, name="gated_norm_bwd",
                                            jobs=jobs("gated_norm_bwd"))
    dxs, db, dc, dproj, ddtb, dalog, ddsk = _ssd_bwd(xbc_act, proj, dtb, alog, dsk, states, dy_pre, dproj, name="ssd_bwd",
                                                     jobs=jobs("ssd_bwd"))
    dproj, dconv_w, dconv_b = _conv_bwd(proj, small["conv_w"], small["conv_b"], dxs, db, dc, dproj, name="conv_bwd",
                                        jobs=jobs("conv_bwd"))
    for which, h in (("send", 1 - plan.core), ("keep", plan.core)):
        uh = lax.dynamic_slice_in_dim(u, h * (D // 2), D // 2, axis=1)
        name = "mm_d_in_" + which
        plan.g_half("w_in", which, _matmul(uh, dproj, ta=True, out_dtype=BF16, tm=D // 2, tn=512, tk=t, name=name,
                                           jobs=jobs(name)))
    du = mm(dproj, w("w_in"), tb=True, tn=1024, tk=tkl, name="mm_du")
    grad_x, _, dg_mix = _rmsnorm_bwd(x, small["g_mix"], du, dh1, name="norm_mix_bwd", jobs=jobs("norm_mix_bwd"))

    gs = {
        "g_mix": dg_mix[:1], "conv_w": dconv_w[:CW], "conv_b": dconv_b[:1], "dt_bias": ddtb[:1, :NH],
        "a_log": dalog[:1, :NH], "d_skip": ddsk[:1, :NH], "g_ssd": dg_ssd[:1], "sinks": dsinks[:1, :NQH],
        "g_ffn": dg_ffn[:1], "g_ple": dg_ple[:1], "g_final": dg_final[0],
    }
    return loss, grad_x, gs


def _to_kernel_cols(w):
    seg = lambda o, n: w[:, o:o + n]
    return jnp.concatenate([seg(R_GA, D), seg(R_GS, D), seg(R_Z, DI), seg(R_XBC, CONV), seg(R_Q, QD), seg(R_K, KVD),
                            seg(R_V, KVD), seg(R_DT, NH), jnp.zeros((w.shape[0], DT_PAD - NH), w.dtype)], axis=1)


def _from_kernel_cols(g):
    seg = lambda o, n: g[:, o:o + n]
    return jnp.concatenate([seg(O_Q, QD), seg(O_K, KVD), seg(O_V, KVD), seg(O_Z, DI), seg(O_XBC, CONV), seg(O_DT, NH),
                            seg(O_GA, D), seg(O_GS, D)], axis=1)


def _shard_pieces():
    segs = ((R_Q, QD, O_Q), (R_K, KVD, O_K), (R_V, KVD, O_V), (R_Z, DI, O_Z), (R_XBC, CONV, O_XBC), (R_DT, NH, O_DT),
            (R_GA, D, O_GA), (R_GS, D, O_GS))
    cs = IN_DIM // NCHIP
    out = []
    for j in range(NCHIP):
        for r0, n, k0 in segs:
            lo, hi = max(r0, j * cs), min(r0 + n, (j + 1) * cs)
            if lo < hi:
                out.append((j, lo - j * cs, hi - lo, k0 + lo - r0))
    return out


SLAB = IN_DIM // NCHIP
SLAB_PAD = -(-SLAB // 128) * 128
REMAP_ROWS = 256


def _lane_remap(src, dst_slabs, dst_cols, moves, *, name, jobs=()):
    s_n, rows, s_cols = src.shape
    assert s_cols % 128 == 0 and dst_cols % 128 == 0 and rows % REMAP_ROWS == 0
    half = REMAP_ROWS // 2

    def body(s_ref, d_ref):
        lane = lax.broadcasted_iota(jnp.int32, (half, 128), 1)
        tiles = {}

        def tile(j, m):
            if (j, m) not in tiles:
                tiles[j, m] = pltpu.bitcast(s_ref[j, :, 128 * m:128 * (m + 1)], jnp.uint32)
            return tiles[j, m]

        def window(j, base):
            m0, s = base // 128, base % 128
            left = tile(j, m0) if 0 <= m0 < s_cols // 128 else None
            if s == 0:
                return left
            right = tile(j, m0 + 1) if 0 <= m0 + 1 < s_cols // 128 else None
            left = None if left is None else pltpu.roll(left, 128 - s, 1)
            right = None if right is None else pltpu.roll(right, 128 - s, 1)
            if left is None or right is None:
                return right if left is None else left
            return jnp.where(lane < 128 - s, left, right)

        for ds in range(dst_slabs):
            for t in range(dst_cols // 128):
                o = 128 * t
                acc = jnp.zeros((half, 128), jnp.uint32)
                for sj, sc, n, dj, dc in moves:
                    lo, hi = max(o, dc) - o, min(o + 128, dc + n) - o
                    if dj != ds or lo >= hi:
                        continue
                    win = window(sj, o - dc + sc)
                    acc = win if (lo, hi) == (0, 128) else jnp.where(jnp.logical_and(lane >= lo, lane < hi), win, acc)
                d_ref[ds, :, o:o + 128] = pltpu.bitcast(acc, BF16)

    return _call(
        body, jobs=jobs, name=name, out_shape=jax.ShapeDtypeStruct((dst_slabs, rows, dst_cols), BF16),
        grid=(rows // REMAP_ROWS,),
        in_specs=[pl.BlockSpec((s_n, REMAP_ROWS, s_cols), lambda i: (0, i, 0))],
        out_specs=pl.BlockSpec((dst_slabs, REMAP_ROWS, dst_cols), lambda i: (0, i, 0)),
        compiler_params=_cp(("parallel",)),
    )(src)


def _slabs_to_kernel_cols(slabs, *, name, jobs=()):
    moves = [(j, a, n, 0, k0) for j, a, n, k0 in _shard_pieces()]
    return _lane_remap(slabs, 1, NP, moves, name=name, jobs=jobs)[0]


def _kernel_cols_to_slabs(g, *, name, jobs=()):
    moves = [(0, k0, n, j, a) for j, a, n, k0 in _shard_pieces()]
    return _lane_remap(g[None], NCHIP, SLAB_PAD, moves, name=name, jobs=jobs)


RELS = ((0, 1), (1, 0), (1, 1))
MATS = {
    n: (n, kind, 1, r, c, tp, tf) for n, kind, r, c, tp, tf in (
        ("w_in", "stk", 2048, SLAB_PAD, 256, 256),
        ("w_attn_br", "col", 1024, 512, 256, 256),
        ("w_ssd_br", "row", 512, 2048, 512, 256),
        ("w_o", "row", 512, 2048, 512, 256),
        ("w_gate", "col", 2048, 1408, 256, 256),
        ("w_up", "col", 2048, 1408, 256, 256),
        ("w_down", "row", 1408, 2048, 704, 704),
        ("w_ple_gate", "row", 512, 2048, 512, 256),
        ("w_ple_proj", "col", 256, 512, 128, 128),
    )}


def _pos():
    return lax.axis_index("x"), lax.axis_index("y"), lax.axis_index("c")


def _flip(v, a):
    return 1 - v if a else v


def _remote(src, dst, send, recv, dev):
    return pltpu.make_async_remote_copy(src_ref=src, dst_ref=dst, send_sem=send, recv_sem=recv, device_id=dev,
                                        device_id_type=MESH)


def _whole_shape(kind, g, r, c):
    return {"row": (g, NCHIP * r, c), "col": (g, r, NCHIP * c), "stk": (NCHIP, r, c)}[kind]


def _cols(j, c):
    return pl.ds(pl.multiple_of(j * c, 128), c)


def _whole_shard(kind, ref, j, r, c):
    if kind == "row":
        return ref.at[:, pl.ds(j * r, r), :]
    if kind == "col":
        return ref.at[:, :, _cols(j, c)]
    return ref.at[pl.ds(j, 1)]


def _whole_rows(kind, ref, j, row, n, r, c):
    if kind == "row":
        return ref.at[:, pl.ds(j * r + row, n), :]
    if kind == "col":
        return ref.at[:, pl.ds(row, n), _cols(j, c)]
    return ref.at[pl.ds(j, 1), pl.ds(row, n), :]


class _GatherJob(_Job):
    has_mid = True
    NCP = 10

    def __init__(self, names, shards, sink):
        self.mats = [MATS[n] for n in names]
        self.srcs = [shards[n] for n in names]
        self.news = [jax.ShapeDtypeStruct(_whole_shape(kind, g, r, c), BF16) for _, kind, g, r, c, _, _ in self.mats]
        n = len(names)
        self.scratch = [pltpu.SemaphoreType.DMA((self.NCP * n,)), pltpu.SemaphoreType.DMA((self.NCP * n,))]
        self.names, self.sink = names, sink

    def _copies(self, srcs, news, sems):
        send, recv = sems
        x, y, c = _pos()
        me, jx, jy, jd = 2 * x + y, 2 * (1 - x) + y, 2 * x + (1 - y), 2 * (1 - x) + (1 - y)
        nbx, nby, sib = (1 - x, y, c), (x, 1 - y, c), (x, y, 1 - c)
        cps = []
        for w, (_, kind, g, r, cc, _, _) in enumerate(self.mats):
            hr, qr = r // 2, r // 4
            at = lambda j, h, q, n: _whole_rows(kind, news[w], j, h * hr + q * qr, n, r, cc)
            mine = lambda q: srcs[w].at[:, pl.ds(c * hr + q * qr, qr), :]
            cp = lambda k, s, d, dev: _remote(s, d, send.at[self.NCP * w + k], recv.at[self.NCP * w + k], dev)
            cps.append([
                cp(0, mine(0), at(me, c, 0, qr), nbx), cp(1, mine(1), at(me, c, 1, qr), nbx),
                cp(2, mine(1), at(me, c, 1, qr), nby), cp(3, mine(0), at(me, c, 0, qr), nby),
                cp(4, at(jx, c, 0, qr), at(jx, c, 0, qr), nby), cp(5, at(jy, c, 1, qr), at(jy, c, 1, qr), nbx),
                cp(6, at(jx, c, 0, hr), at(jx, c, 0, hr), sib), cp(7, at(jy, c, 0, hr), at(jy, c, 0, hr), sib),
                cp(8, at(jd, c, 0, hr), at(jd, c, 0, hr), sib),
                cp(9, srcs[w], _whole_shard(kind, news[w], me, r, cc), sib)])
        return cps

    def start(self, srcs, dsts, news, sems):
        cps = self._copies(srcs, news, sems)
        for w in range(len(self.mats)):
            for k in (0, 1, 2, 3, 9):
                cps[w][k].start()

    def mid(self, srcs, dsts, news, sems):
        cps = self._copies(srcs, news, sems)
        for w in range(len(self.mats)):
            cps[w][0].wait_recv()
            cps[w][4].start()
            cps[w][2].wait_recv()
            cps[w][5].start()

    def finish(self, srcs, dsts, news, sems):
        cps = self._copies(srcs, news, sems)
        for w in range(len(self.mats)):
            cps[w][1].wait_recv()
            cps[w][6].start()
            cps[w][3].wait_recv()
            cps[w][7].start()
        for w in range(len(self.mats)):
            cps[w][4].wait_recv()
            cps[w][5].wait_recv()
            cps[w][8].start()
        for w in range(len(self.mats)):
            for k in (6, 7, 8, 9):
                cps[w][k].wait_recv()
            for k in range(self.NCP):
                cps[w][k].wait_send()

    def done(self, dsts, news):
        for n, a in zip(self.names, news):
            self.sink[n] = a


class _SwapJob(_Job):
    def __init__(self, build, ncopies, *, srcs=(), dsts=(), news=(), done=None):
        self.build, self.srcs, self.dsts, self.news, self._done = build, list(srcs), list(dsts), list(news), done
        self.scratch = [pltpu.SemaphoreType.DMA((ncopies,)), pltpu.SemaphoreType.DMA((ncopies,))]

    def start(self, srcs, dsts, news, sems):
        for cp in self.build(srcs, dsts, news, *sems):
            cp.start()

    def finish(self, srcs, dsts, news, sems):
        for cp in self.build(srcs, dsts, news, *sems):
            cp.wait()

    def done(self, dsts, news):
        if self._done is not None:
            self._done(dsts, news)


def _half_of_whole(kind, ref, h, r, c):
    if kind == "row":
        return ref.at[:, :, pl.ds(pl.multiple_of(h * (c // 2), 128), c // 2)]
    return ref.at[:, pl.ds(h * (r // 2), r // 2), :]


def _half_shape(kind, g, r, c):
    return {"row": (g, NCHIP * r, c // 2), "col": (g, r // 2, NCHIP * c), "stk": (NCHIP, r // 2, c)}[kind]


def _sub_shape(kind, r, c):
    return {"row": (1, r // 2, c // 2), "col": (1, r // 4, c), "stk": (1, r // 4, c)}[kind]


def _sub_of_half(kind, ref, j, p, r, c):
    sr = _sub_shape(kind, r, c)[1]
    if kind == "row":
        return ref.at[:, pl.ds(j * r + p * sr, sr), :]
    if kind == "col":
        return ref.at[:, pl.ds(p * sr, sr), _cols(j, c)]
    return ref.at[pl.ds(j, 1), pl.ds(p * sr, sr), :]


def _sub_tile(sr):
    return 256 if sr % 256 == 0 else sr


def _half_of_shard(kind, ref, h, r, c):
    if kind == "row":
        return ref.at[:, :, pl.ds(pl.multiple_of(h * (c // 2), 128), c // 2)]
    return ref.at[:, pl.ds(h * (r // 2), r // 2), :]


def _pair_sum(pack, core, mine, got, whole=True):
    name, kind, g, r, c, tr, _ = pack
    hs = _half_shape(kind, g, r, c)
    nb = hs[1] // tr

    def body(core_ref, a_ref, b_ref, o_ref):
        o_ref[...] = (a_ref[...].astype(F32) + b_ref[...].astype(F32)).astype(BF16)

    blk = (1, tr, hs[2])
    same = lambda gi, i, core_ref: (gi, i, 0)
    if not whole:
        a_map = same
    elif kind == "row":
        a_map = lambda gi, i, core_ref: (gi, i, core_ref[0])
    else:
        a_map = lambda gi, i, core_ref: (gi, core_ref[0] * nb + i, 0)
    return pl.pallas_call(
        body, name="pair_sum_" + name, out_shape=jax.ShapeDtypeStruct(hs, BF16),
        grid_spec=pltpu.PrefetchScalarGridSpec(
            num_scalar_prefetch=1, grid=(hs[0], nb),
            in_specs=[pl.BlockSpec(blk, a_map), pl.BlockSpec(blk, same)], out_specs=pl.BlockSpec(blk, same)),
        compiler_params=_cp(("parallel", "parallel")),
    )(core, mine, got)


def _sub_sum(pack, idx, half, got, first_slot, out_dtype, *, name):
    _, kind, g, r, c, _, _ = pack
    _, sr, sc = _sub_shape(kind, r, c)
    tr = _sub_tile(sr)
    nb = sr // tr

    def body(idx_ref, a_ref, b_ref, o_ref):
        o_ref[0, 0] = (a_ref[0].astype(F32) + b_ref[0, 0].astype(F32)).astype(out_dtype)

    if kind == "row":
        a_map = lambda q, i, ix: (0, ix[2 * q] * (r // tr) + ix[2 * q + 1] * nb + i, 0)
    elif kind == "col":
        a_map = lambda q, i, ix: (0, ix[2 * q + 1] * nb + i, ix[2 * q])
    else:
        a_map = lambda q, i, ix: (ix[2 * q], ix[2 * q + 1] * nb + i, 0)
    return pl.pallas_call(
        body, name=name, out_shape=jax.ShapeDtypeStruct((2, 1, sr, sc), out_dtype),
        grid_spec=pltpu.PrefetchScalarGridSpec(
            num_scalar_prefetch=1, grid=(2, nb),
            in_specs=[pl.BlockSpec((1, tr, sc), a_map),
                      pl.BlockSpec((1, 1, tr, sc), lambda q, i, ix: (first_slot + 2 * q, 0, i, 0))],
            out_specs=pl.BlockSpec((1, 1, tr, sc), lambda q, i, ix: (q, 0, i, 0))),
        compiler_params=_cp(("parallel", "parallel")),
    )(idx, half, got)


def _shard_sum(pack, core, keep, got):
    name, kind, g, r, c, _, _ = pack
    _, sr, sc = _sub_shape(kind, r, c)
    tr = _sub_tile(sr)
    nb = sr // tr

    def body(core_ref, a_ref, b_ref, o_ref):
        o_ref[0] = a_ref[0, 0] + b_ref[0, 0].astype(F32)

    blk = pl.BlockSpec((1, 1, tr, sc), lambda p, i, cr: (p, 0, i, 0))
    if kind == "row":
        o_map = lambda p, i, cr: (0, p * nb + i, cr[0])
    else:
        o_map = lambda p, i, cr: (0, cr[0] * 2 * nb + p * nb + i, 0)
    return pl.pallas_call(
        body, name="shard_sum_" + name, out_shape=jax.ShapeDtypeStruct((g, r, c), F32),
        grid_spec=pltpu.PrefetchScalarGridSpec(
            num_scalar_prefetch=1, grid=(2, nb), in_specs=[blk, blk], out_specs=pl.BlockSpec((1, tr, sc), o_map)),
        compiler_params=_cp(("parallel", "parallel")),
    )(core, keep, got)


class _Plan:
    def __init__(self, shards, table):
        self.shards, self.table = shards, table
        self.whole, self.grad, self.got_a, self.half, self.gshard = {}, {}, {}, {}, {}
        self.got_b1, self.kept, self.pass_on, self.got_b2 = {}, {}, {}, {}
        x, y, c = _pos()
        me, jx, jy = 2 * x + y, 2 * (1 - x) + y, 2 * x + (1 - y)
        self.core = c
        self.core1 = c.reshape(1).astype(jnp.int32)
        self.idx_keep = jnp.stack([me, 0 * me, me, 0 * me + 1]).astype(jnp.int32)
        self.idx_pass = jnp.stack([jy, 0 * me, jx, 0 * me + 1]).astype(jnp.int32)
        self._w_in = None
        self.send, self.keep = {}, {}

    def w(self, n):
        if n != "w_in":
            return self.whole[n][0]
        if self._w_in is None:
            self._w_in = _slabs_to_kernel_cols(self.whole[n], name="relayout_w_in", jobs=self.jobs("relayout_w_in"))
        return self._w_in

    def g(self, n, a):
        self.grad[n] = a[None]

    def g_half(self, n, which, a):
        (self.send if which == "send" else self.keep)[n] = _kernel_cols_to_slabs(a, name="relayout_d_in_" + which)

    def jobs(self, tag):
        out = []
        for spec in self.table.get(tag, ()):
            out += getattr(self, "_" + spec[0])(*spec[1:])
        return out

    def run(self, name, jobs):
        if jobs:
            _call(lambda: None, jobs=jobs, name=name, out_shape=[], in_specs=[], out_specs=[])()

    def _gather(self, names):
        return [_GatherJob(names, self.shards, self.whole)]

    def _rs_a(self, names):
        mats = [MATS[n] for n in names]

        def build(srcs, dsts, news, send, recv):
            x, y, c = _pos()
            return [_remote(srcs[i] if names[i] in self.send else _half_of_whole(kind, srcs[i], 1 - c, r, cc), news[i],
                            send.at[i], recv.at[i], (x, y, 1 - c))
                    for i, (_, kind, g, r, cc, _, _) in enumerate(mats)]

        def done(dsts, news):
            self.got_a.update(zip(names, news))

        return [_SwapJob(build, len(names), srcs=[self.send.get(n, self.grad.get(n)) for n in names], done=done,
                         news=[jax.ShapeDtypeStruct(_half_shape(kind, g, r, c), BF16) for _, kind, g, r, c, _, _ in mats])]

    def _rs_b1(self, names):
        mats = [MATS[n] for n in names]
        for n in names:
            if n in self.keep:
                self.half[n] = _pair_sum(MATS[n], self.core1, self.keep[n], self.got_a[n], whole=False)
            else:
                self.half[n] = _pair_sum(MATS[n], self.core1, self.grad[n], self.got_a[n])

        def build(srcs, dsts, news, send, recv):
            x, y, c = _pos()
            jx, jy, jd = 2 * (1 - x) + y, 2 * x + (1 - y), 2 * (1 - x) + (1 - y)
            nbx, nby = (1 - x, y, c), (x, 1 - y, c)
            cps = []
            for i, (_, kind, g, r, cc, _, _) in enumerate(mats):
                sub = lambda j, p: _sub_of_half(kind, srcs[i], j, p, r, cc)
                for k, (j, p, dev) in enumerate(((jx, 0, nbx), (jd, 0, nbx), (jy, 1, nby), (jd, 1, nby))):
                    cps.append(_remote(sub(j, p), news[i].at[k], send.at[4 * i + k], recv.at[4 * i + k], dev))
            return cps

        def done(dsts, news):
            self.got_b1.update(zip(names, news))

        return [_SwapJob(build, 4 * len(names), srcs=[self.half[n] for n in names], done=done,
                         news=[jax.ShapeDtypeStruct((4,) + _sub_shape(kind, r, c), BF16) for _, kind, g, r, c, _, _ in mats])]

    def _rs_b2(self, names):
        mats = [MATS[n] for n in names]
        for n in names:
            self.kept[n] = _sub_sum(MATS[n], self.idx_keep, self.half[n], self.got_b1[n], 0, F32, name="sum_keep_" + n)
            self.pass_on[n] = _sub_sum(MATS[n], self.idx_pass, self.half[n], self.got_b1[n], 1, BF16, name="sum_pass_" + n)

        def build(srcs, dsts, news, send, recv):
            x, y, c = _pos()
            cps = []
            for i in range(len(mats)):
                cps.append(_remote(srcs[i].at[0], news[i].at[0], send.at[2 * i], recv.at[2 * i], (x, 1 - y, c)))
                cps.append(_remote(srcs[i].at[1], news[i].at[1], send.at[2 * i + 1], recv.at[2 * i + 1], (1 - x, y, c)))
            return cps

        def done(dsts, news):
            self.got_b2.update(zip(names, news))

        return [_SwapJob(build, 2 * len(names), srcs=[self.pass_on[n] for n in names], done=done,
                         news=[jax.ShapeDtypeStruct((2,) + _sub_shape(kind, r, c), BF16) for _, kind, g, r, c, _, _ in mats])]

    def _rs_c(self, names):
        mats = [MATS[n] for n in names]
        parts = [_shard_sum(MATS[n], self.core1, self.kept[n], self.got_b2[n]) for n in names]

        def build(srcs, dsts, news, send, recv):
            x, y, c = _pos()
            cps = []
            for i, (_, kind, g, r, cc, _, _) in enumerate(mats):
                mine = _half_of_shard(kind, dsts[i], c, r, cc)
                cps.append(_remote(mine, mine, send.at[i], recv.at[i], (x, y, 1 - c)))
            return cps

        def done(dsts, news):
            self.gshard.update(zip(names, dsts))

        return [_SwapJob(build, len(names), dsts=parts, done=done)]

    def finish(self, n):
        if n not in self.got_a:
            self.run("rs_a_" + n, self._rs_a((n,)))
        if n not in self.got_b1:
            self.run("rs_b1_" + n, self._rs_b1((n,)))
        if n not in self.got_b2:
            self.run("rs_b2_" + n, self._rs_b2((n,)))
        if n not in self.gshard:
            self.run("rs_c_" + n, self._rs_c((n,)))
        return self.gshard[n]


TABLE = {
    "gather_w_in": (("gather", ("w_in",)),),
    "relayout_w_in": (("gather", ("w_gate",)),),
    "mm_in": (("gather", ("w_up",)),),
    "attn_fwd": (("gather", ("w_attn_br", "w_ssd_br")),),
    "ssd_fwd": (("gather", ("w_o",)),),
    "swiglu_fwd": (("gather", ("w_down",)),),
    "mm_down": (("gather", ("w_ple_gate", "w_ple_proj")),),
    "mm_de": (("rs_a", ("w_ple_proj", "w_ple_gate")),),
    "mm_d_down": (("rs_b1", ("w_ple_proj", "w_ple_gate")),),
    "swiglu_bwd": (("rs_a", ("w_down",)), ("rs_b2", ("w_ple_proj", "w_ple_gate"))),
    "mm_d_gate": (("rs_b1", ("w_down",)),),
    "mm_d_up": (("rs_b2", ("w_down",)), ("rs_c", ("w_ple_proj", "w_ple_gate"))),
    "mm_df_gate": (("rs_a", ("w_gate", "w_up")), ("rs_c", ("w_down",))),
    "mm_df_up": (("rs_b1", ("w_gate",)),),
    "norm_ffn_bwd": (("rs_b2", ("w_gate",)),),
    "mm_dmerged": (("rs_a", ("w_o",)),),
    "mm_dyn": (("rs_a", ("w_attn_br", "w_ssd_br")),),
    "attn_bwd": (("rs_b1", ("w_up", "w_o", "w_attn_br", "w_ssd_br")), ("rs_c", ("w_gate",))),
    "gated_norm_bwd": (("rs_b2", ("w_up",)),),
    "ssd_bwd": (("rs_b2", ("w_o", "w_attn_br", "w_ssd_br")),),
    "conv_bwd": (("rs_c", ("w_up", "w_o", "w_attn_br", "w_ssd_br")),),
    "mm_d_in_keep": (("rs_a", ("w_in",)),),
    "mm_du": (("rs_b1", ("w_in",)),),
    "norm_mix_bwd": (("rs_b2", ("w_in",)),),
}


NDEV = 8


def _allreduce_small(v, *, name):
    rows = v.shape[0]

    def body(v_ref, o_ref, slots, send, recv):
        x, y, c = _pos()
        me = 4 * x + 2 * y + c
        slots[me] = v_ref[...]
        cps = []
        for k in range(1, NDEV):
            peer = (_flip(x, k & 4), _flip(y, k & 2), _flip(c, k & 1))
            cp = _remote(v_ref, slots.at[me], send.at[k - 1], recv.at[k - 1], peer)
            cp.start()
            cps.append(cp)
        for cp in cps:
            cp.wait()
        acc = slots[0]
        for s in range(1, NDEV):
            acc = acc + slots[s]
        o_ref[...] = acc

    return pl.pallas_call(
        body, name=name, out_shape=jax.ShapeDtypeStruct((rows, 128), F32),
        in_specs=[pl.BlockSpec(memory_space=pltpu.VMEM)], out_specs=pl.BlockSpec(memory_space=pltpu.VMEM),
        scratch_shapes=[pltpu.VMEM((NDEV, rows, 128), F32), pltpu.SemaphoreType.DMA((NDEV - 1,)),
                        pltpu.SemaphoreType.DMA((NDEV - 1,))],
    )(v)


def _adamw(w, g, m, v, *, name, tr=None, tc=None, jobs=()):
    r, c = w.shape
    tr = r if tr is None else tr
    c1 = 1.0 / (1.0 - B1 ** STEP)
    c2 = 1.0 / (1.0 - B2 ** STEP)

    def body(w_ref, g_ref, m_ref, v_ref, d_ref, mo_ref, vo_ref):
        gv = g_ref[...]
        mn = B1 * m_ref[...] + (1.0 - B1) * gv
        vn = B2 * v_ref[...] + (1.0 - B2) * (gv * gv)
        mo_ref[...] = mn
        vo_ref[...] = vn
        d_ref[...] = -LR * ((mn * c1) / (jnp.sqrt(vn * c2) + AEPS) + WD * w_ref[...])

    if tc is None:
        blk, grid = pl.BlockSpec((tr, c), lambda i: (i, 0)), (r // tr,)
    else:
        blk, grid = pl.BlockSpec((r, tc), lambda i: (0, i)), (c // tc,)
    o = jax.ShapeDtypeStruct((r, c), F32)
    return _call(
        body, jobs=jobs, name=name, out_shape=(o, o, o), grid=grid, in_specs=[blk] * 4, out_specs=(blk, blk, blk),
        compiler_params=_cp(("parallel",)),
    )(w, g, m, v)


WEIGHTS = ("g_mix", "w_in", "conv_w", "conv_b", "dt_bias", "a_log", "d_skip", "g_ssd", "sinks", "w_attn_br", "w_ssd_br",
           "w_o", "g_ffn", "w_gate", "w_up", "w_down", "g_ple", "w_ple_gate", "w_ple_proj", "g_final")
BIG = {
    "w_gate": 256, "w_up": 256, "w_down": 128, "w_ssd_br": 128, "w_o": 128, "w_ple_gate": 128, "w_attn_br": 256,
    "w_ple_proj": 256, "w_in": None,
}
SMALL = tuple(n for n in WEIGHTS if n not in BIG)


def _pack_small(parts):
    rows = []
    for a in parts:
        a = a.reshape(-1)
        rows.append(jnp.pad(a, (0, -a.shape[0] % 128)).reshape(-1, 128))
    out = jnp.concatenate(rows, axis=0)
    return jnp.pad(out, ((0, -out.shape[0] % 8), (0, 0)))


def _unpack_small(packed, shapes):
    out, r = [], 0
    for s in shapes:
        n = int(np.prod(s))
        nr = -(-n // 128)
        out.append(packed[r:r + nr].reshape(-1)[:n].reshape(s))
        r += nr
    return out


def kernel(x, p, positions, g_mix, w_in, conv_w, conv_b, dt_bias, a_log, d_skip, g_ssd, sinks, w_attn_br, w_ssd_br, w_o, g_ffn, w_gate, w_up, w_down, g_ple, w_ple_gate, w_ple_proj, g_final, loss_target, m_g_mix, m_w_in, m_conv_w, m_conv_b, m_dt_bias, m_a_log, m_d_skip, m_g_ssd, m_sinks, m_w_attn_br, m_w_ssd_br, m_w_o, m_g_ffn, m_w_gate, m_w_up, m_w_down, m_g_ple, m_w_ple_gate, m_w_ple_proj, m_g_final, v_g_mix, v_w_in, v_conv_w, v_conv_b, v_dt_bias, v_a_log, v_d_skip, v_g_ssd, v_sinks, v_w_attn_br, v_w_ssd_br, v_w_o, v_g_ffn, v_w_gate, v_w_up, v_w_down, v_g_ple, v_w_ple_gate, v_w_ple_proj, v_g_final):
    w = dict(zip(WEIGHTS, (g_mix, w_in, conv_w, conv_b, dt_bias, a_log, d_skip, g_ssd, sinks, w_attn_br, w_ssd_br, w_o,
                           g_ffn, w_gate, w_up, w_down, g_ple, w_ple_gate, w_ple_proj, g_final)))
    m = dict(zip(WEIGHTS, (m_g_mix, m_w_in, m_conv_w, m_conv_b, m_dt_bias, m_a_log, m_d_skip, m_g_ssd, m_sinks, m_w_attn_br,
                           m_w_ssd_br, m_w_o, m_g_ffn, m_w_gate, m_w_up, m_w_down, m_g_ple, m_w_ple_gate, m_w_ple_proj,
                           m_g_final)))
    v = dict(zip(WEIGHTS, (v_g_mix, v_w_in, v_conv_w, v_conv_b, v_dt_bias, v_a_log, v_d_skip, v_g_ssd, v_sinks, v_w_attn_br,
                           v_w_ssd_br, v_w_o, v_g_ffn, v_w_gate, v_w_up, v_w_down, v_g_ple, v_w_ple_gate, v_w_ple_proj,
                           v_g_final)))
    xi, yi, ci = _pos()
    chip = 2 * xi + yi
    t = x.shape[1]
    cshard = CONV // NCHIP

    shards = {n: w[n].astype(BF16) for n in MATS}
    shards["w_in"] = jnp.pad(shards["w_in"], ((0, 0), (0, 0), (0, SLAB_PAD - SLAB)))
    plan = _Plan(shards, TABLE)
    plan.run("gather_w_in", plan.jobs("gather_w_in"))
    placed = lax.dynamic_update_slice(jnp.zeros((CW, CONV), F32), w["conv_w"][0], (0, chip * cshard))
    conv_whole = _allreduce_small(jnp.where(ci == 0, placed, 0.0).reshape(-1, 128), name="gather_conv_w").reshape(CW, CONV)

    small = {n: w[n] for n in ("g_mix", "conv_b", "dt_bias", "a_log", "d_skip", "g_ssd", "sinks", "g_ffn", "g_ple", "g_final")}
    small["conv_w"] = conv_whole
    loss8, grad_x, gs = _local_step(x[0], p[0, 0], positions, loss_target[0], small, plan)

    order = ("g_mix", "conv_b", "dt_bias", "a_log", "d_skip", "g_ssd", "sinks", "g_ffn", "g_ple", "g_final", "conv_w")
    summed = _allreduce_small(_pack_small([loss8[0, :1]] + [gs[n] for n in order]), name="sum_small")
    parts = _unpack_small(summed, [(1,)] + [w[n].shape for n in order[:-1]] + [(CW, CONV)])
    loss = parts[0][0]
    grad = dict(zip(order, parts[1:]))
    grad["conv_w"] = lax.dynamic_slice(grad["conv_w"], (0, chip * cshard), (CW, cshard))[None]

    delta, new_m, new_v = {}, {}, {}
    for n, tr in BIG.items():
        grad[n] = plan.finish(n)[:, :, :w[n].shape[2]]
        if n == "w_in":
            d_, m_, v_ = _adamw(w[n][0].T, grad[n][0].T, m[n][0].T, v[n][0].T, tc=128, name="adamw_" + n)
            d_, m_, v_ = d_.T, m_.T, v_.T
        else:
            d_, m_, v_ = _adamw(w[n][0], grad[n][0], m[n][0], v[n][0], tr=tr, name="adamw_" + n)
        delta[n], new_m[n], new_v[n] = d_[None], m_[None], v_[None]
    shapes = [w[n].shape for n in SMALL]
    d_, m_, v_ = _adamw(_pack_small([w[n] for n in SMALL]), _pack_small([grad[n] for n in SMALL]),
                        _pack_small([m[n] for n in SMALL]), _pack_small([v[n] for n in SMALL]), tr=None, name="adamw_small")
    for n, a, b, c_ in zip(SMALL, _unpack_small(d_, shapes), _unpack_small(m_, shapes), _unpack_small(v_, shapes)):
        delta[n], new_m[n], new_v[n] = a, b, c_

    return (loss, grad_x[None], *[grad[n] for n in WEIGHTS], *[delta[n] for n in WEIGHTS],
            *[new_m[n] for n in WEIGHTS], *[new_v[n] for n in WEIGHTS])
```

```python
import functools

import jax
import jax.numpy as jnp
import numpy as np
from jax import lax
from jax.experimental import pallas as pl
from jax.experimental.pallas import tpu as pltpu

F32 = jnp.float32
BF16 = jnp.bfloat16
MESH = pl.DeviceIdType.MESH

D = 2048
HD = 64
NQH = 16
NKV = 4
QD = NQH * HD
KVD = NKV * HD
DI = 2048
NH = 32
NG = 4
NS = 128
CW = 4
L = 128
CONV = DI + 2 * NG * NS
FFN = 5632
PLE = 256
IN_DIM = QD + 2 * KVD + DI + CONV + NH + 2 * D
EPS = 1e-6
SSM_EPS = 1e-5
ROPE_THETA = 10000.0
LR, B1, B2, AEPS, WD, STEP = 0.001, 0.9, 0.999, 1e-08, 0.01, 10

O_GA, O_GS, O_Z, O_XBC, O_Q, O_K, O_V, O_DT = 0, 2048, 4096, 6144, 9216, 10240, 10496, 10752
DT_PAD = 512
NP = O_DT + DT_PAD
R_Q, R_K, R_V, R_Z, R_XBC, R_DT, R_GA, R_GS = 0, 1024, 1280, 1536, 3584, 6656, 6688, 8736

NCHIP = 4
VMEM_LIMIT = 52 * 1024 * 1024
NEG = -1e30


def _cp(sem=None):
    return pltpu.CompilerParams(dimension_semantics=sem, vmem_limit_bytes=VMEM_LIMIT)


def _dot(a, b):
    return lax.dot_general(a, b, (((1,), (0,)), ((), ())), preferred_element_type=F32)


def _dot_nt(a, b):
    return lax.dot_general(a, b, (((1,), (1,)), ((), ())), preferred_element_type=F32)


def _dot_tn(a, b):
    return lax.dot_general(a, b, (((0,), (0,)), ((), ())), preferred_element_type=F32)


def _sigmoid(x):
    return 1.0 / (1.0 + jnp.exp(-x))


def _bf16_dot(dot, da, db):
    @jax.custom_vjp
    def f(a, b):
        return dot(a.astype(BF16), b.astype(BF16))

    def fwd(a, b):
        return f(a, b), (a.astype(BF16), b.astype(BF16))

    def bwd(res, g):
        a, b = res
        g = g.astype(BF16)
        return da(g, a, b), db(g, a, b)

    f.defvjp(fwd, bwd)
    return f


_bdot = _bf16_dot(_dot, lambda g, a, b: _dot_nt(g, b), lambda g, a, b: _dot_tn(a, g))
_bdot_nt = _bf16_dot(_dot_nt, lambda g, a, b: _dot(g, b), lambda g, a, b: _dot_tn(g, a))
_bdot_tn = _bf16_dot(_dot_tn, lambda g, a, b: _dot_nt(b, g), lambda g, a, b: _dot(a, g))


ANY = pl.BlockSpec(memory_space=pl.ANY)


class _Job:
    srcs, dsts, news, scratch = (), (), (), ()
    has_mid = False

    def start(self, srcs, dsts, news, sems):
        raise NotImplementedError

    def mid(self, srcs, dsts, news, sems):
        pass

    def late(self, srcs, dsts, news, sems):
        pass

    def finish(self, srcs, dsts, news, sems):
        raise NotImplementedError

    def done(self, dsts, news):
        pass


def _call(body, *, jobs=(), name, out_shape, in_specs, out_specs, grid=(), scratch_shapes=(), compiler_params=None,
          aliases=None):
    jobs = [j for j in jobs if j is not None]
    aliases = dict(aliases or {})
    if not jobs:
        return pl.pallas_call(body, name=name, out_shape=out_shape, in_specs=in_specs, out_specs=out_specs, grid=grid,
                              scratch_shapes=scratch_shapes, compiler_params=compiler_params,
                              input_output_aliases=aliases)
    single = not isinstance(out_shape, (tuple, list))
    outs = [out_shape] if single else list(out_shape)
    ospecs = [out_specs] if single else list(out_specs)
    n_in, n_out, n_scr = len(in_specs), len(outs), len(scratch_shapes)
    srcs = [a for j in jobs for a in j.srcs]
    dsts = [a for j in jobs for a in j.dsts]
    news = [a for j in jobs for a in j.news]
    sems = [a for j in jobs for a in j.scratch]

    def wrapped(*refs):
        pos = n_in + len(srcs) + len(dsts)
        ins, jsrc = refs[:n_in], refs[n_in:n_in + len(srcs)]
        o_refs = refs[pos:pos + n_out]
        pos += n_out
        jdst, jnew = refs[pos:pos + len(dsts)], refs[pos + len(dsts):pos + len(dsts) + len(news)]
        pos += len(dsts) + len(news)
        scr, jsem = refs[pos:pos + n_scr], refs[pos + n_scr:]

        def run(which):
            a = b = c = d = 0
            for j in jobs:
                getattr(j, which)(jsrc[a:a + len(j.srcs)], jdst[b:b + len(j.dsts)], jnew[c:c + len(j.news)],
                                  jsem[d:d + len(j.scratch)])
                a, b, c, d = a + len(j.srcs), b + len(j.dsts), c + len(j.news), d + len(j.scratch)

        if not grid:
            run("start")
            run("mid")
            run("late")
            body(*ins, *o_refs, *scr)
            run("finish")
            return
        step = functools.reduce(lambda acc, a: acc * grid[a] + pl.program_id(a), range(len(grid)), 0)
        steps = int(np.prod(grid))
        pl.when(step == 0)(lambda: run("start"))
        if any(j.has_mid for j in jobs):
            pl.when(step == steps // 3)(lambda: run("mid"))
            pl.when(step == (2 * steps) // 3)(lambda: run("late"))
        body(*ins, *o_refs, *scr)
        pl.when(step == steps - 1)(lambda: run("finish"))

    call = pl.pallas_call(
        wrapped, name=name,
        out_shape=outs + [jax.ShapeDtypeStruct(a.shape, a.dtype) for a in dsts] + news,
        in_specs=list(in_specs) + [ANY] * (len(srcs) + len(dsts)),
        out_specs=ospecs + [ANY] * (len(dsts) + len(news)),
        grid=grid, scratch_shapes=list(scratch_shapes) + sems,
        input_output_aliases={**aliases, **{n_in + len(srcs) + i: n_out + i for i in range(len(dsts))}},
        compiler_params=_cp(("arbitrary",) * len(grid) if grid else None))

    def run_call(*args):
        res = call(*args, *srcs, *dsts)
        b, c = n_out, n_out + len(dsts)
        for j in jobs:
            j.done(res[b:b + len(j.dsts)], res[c:c + len(j.news)])
            b, c = b + len(j.dsts), c + len(j.news)
        return res[0] if single else tuple(res[:n_out])

    return run_call


def _matmul(a, b, *, ta=False, tb=False, out_dtype=F32, add=None, tm, tn, tk, name, jobs=()):
    k, m = a.shape if ta else a.shape[::-1]
    n = b.shape[0] if tb else b.shape[1]
    assert (b.shape[1] if tb else b.shape[0]) == k and not (ta and tb)
    assert m % tm == 0 and n % tn == 0 and k % tk == 0, (name, a.shape, b.shape)
    nk = k // tk
    has_add = add is not None

    def body(*refs):
        a_ref, b_ref = refs[0], refs[1]
        add_ref = refs[2] if has_add else None
        o_ref = refs[3] if has_add else refs[2]
        av = a_ref[...].astype(BF16)
        bv = b_ref[...].astype(BF16)
        part = _dot_tn(av, bv) if ta else _dot_nt(av, bv) if tb else _dot(av, bv)

        def finish(r):
            if has_add:
                r = r + add_ref[...]
            o_ref[...] = r.astype(out_dtype)

        if nk == 1:
            finish(part)
        elif out_dtype == F32:
            kk = pl.program_id(2)
            pl.when(kk == 0)(lambda: finish(part))

            @pl.when(kk > 0)
            def _():
                o_ref[...] += part
        else:
            acc_ref = refs[-1]
            kk = pl.program_id(2)

            @pl.when(kk == 0)
            def _():
                acc_ref[...] = part

            @pl.when(kk > 0)
            def _():
                acc_ref[...] += part

            @pl.when(kk == nk - 1)
            def _():
                finish(acc_ref[...])

    in_specs = [pl.BlockSpec((tk, tm), lambda i, j, kk: (kk, i)) if ta else pl.BlockSpec((tm, tk), lambda i, j, kk: (i, kk)),
                pl.BlockSpec((tn, tk), lambda i, j, kk: (j, kk)) if tb
                else pl.BlockSpec((tk, tn), lambda i, j, kk: (kk, j))]
    args = [a, b]
    if has_add:
        in_specs.append(pl.BlockSpec((tm, tn), lambda i, j, kk: (i, j)))
        args.append(add)
    return _call(
        body, jobs=jobs, name=name,
        out_shape=jax.ShapeDtypeStruct((m, n), out_dtype),
        grid=(m // tm, n // tn, nk),
        in_specs=in_specs,
        out_specs=pl.BlockSpec((tm, tn), lambda i, j, kk: (i, j)),
        scratch_shapes=[pltpu.VMEM((tm, tn), F32)] if nk > 1 and out_dtype != F32 else [],
        compiler_params=_cp(("parallel", "parallel", "arbitrary")),
    )(*args)


ROWS = 256


def _rmsnorm_fwd(x, g, *, name):
    t, d = x.shape

    def body(x_ref, g_ref, o_ref):
        xv = x_ref[...]
        r = lax.rsqrt(jnp.mean(xv * xv, axis=-1, keepdims=True) + EPS)
        o_ref[...] = (xv * r * g_ref[...]).astype(BF16)

    return pl.pallas_call(
        body, name=name, out_shape=jax.ShapeDtypeStruct((t, d), BF16), grid=(t // ROWS,),
        in_specs=[pl.BlockSpec((ROWS, d), lambda i: (i, 0)), pl.BlockSpec((1, d), lambda i: (0, 0))],
        out_specs=pl.BlockSpec((ROWS, d), lambda i: (i, 0)), compiler_params=_cp(("parallel",)),
    )(x, g)


def _rmsnorm_bwd(x, g, dy, dres, *, name, jobs=()):
    t, d = x.shape

    def body(x_ref, g_ref, dy_ref, dres_ref, dx_ref, dxb_ref, dg_ref):
        xv = x_ref[...]
        r = lax.rsqrt(jnp.mean(xv * xv, axis=-1, keepdims=True) + EPS)
        xh = xv * r
        dyv = dy_ref[...]
        dxh = dyv * g_ref[...]
        dx = r * (dxh - xh * jnp.mean(dxh * xh, axis=-1, keepdims=True))
        tot = dres_ref[...] + dx
        dx_ref[...] = tot
        dxb_ref[...] = tot.astype(BF16)

        @pl.when(pl.program_id(0) == 0)
        def _():
            dg_ref[...] = jnp.zeros_like(dg_ref)

        dg_ref[...] += jnp.broadcast_to(jnp.sum(dyv * xh, axis=0, keepdims=True), dg_ref.shape)

    row = pl.BlockSpec((ROWS, d), lambda i: (i, 0))
    return _call(
        body, jobs=jobs, name=name,
        out_shape=(jax.ShapeDtypeStruct((t, d), F32), jax.ShapeDtypeStruct((t, d), BF16),
                   jax.ShapeDtypeStruct((8, d), F32)),
        grid=(t // ROWS,),
        in_specs=[row, pl.BlockSpec((1, d), lambda i: (0, 0)), row, row],
        out_specs=(row, row, pl.BlockSpec((8, d), lambda i: (0, 0))),
        compiler_params=_cp(("arbitrary",)),
    )(x, g, dy, dres)


def _final(h2, pgl, pp, target, g_final, *, name):
    t, d = h2.shape

    def body(h2_ref, pgl_ref, pp_ref, tg_ref, g_ref, dh3_ref, dpgl_ref, dpp_ref, loss_ref, dg_ref):
        s = _sigmoid(pgl_ref[...])
        ppv = pp_ref[...]
        h3 = h2_ref[...] + s * ppv
        r = lax.rsqrt(jnp.mean(h3 * h3, axis=-1, keepdims=True) + EPS)
        xh = h3 * r
        gv = g_ref[...]
        err = xh * gv - tg_ref[...]
        dyv = err * (1.0 / d)
        dxh = dyv * gv
        dh3 = r * (dxh - xh * jnp.mean(dxh * xh, axis=-1, keepdims=True))
        dh3_ref[...] = dh3
        dpp_ref[...] = (dh3 * s).astype(BF16)
        dpgl_ref[...] = (dh3 * ppv * s * (1.0 - s)).astype(BF16)

        @pl.when(pl.program_id(0) == 0)
        def _():
            loss_ref[...] = jnp.zeros_like(loss_ref)
            dg_ref[...] = jnp.zeros_like(dg_ref)

        part = 0.5 * jnp.sum(jnp.mean(err * err, axis=-1, keepdims=True), axis=0, keepdims=True)
        loss_ref[...] += jnp.broadcast_to(part, loss_ref.shape)
        dg_ref[...] += jnp.broadcast_to(jnp.sum(dyv * xh, axis=0, keepdims=True), dg_ref.shape)

    row = pl.BlockSpec((ROWS, d), lambda i: (i, 0))
    return pl.pallas_call(
        body, name=name,
        out_shape=(jax.ShapeDtypeStruct((t, d), F32), jax.ShapeDtypeStruct((t, d), BF16),
                   jax.ShapeDtypeStruct((t, d), BF16), jax.ShapeDtypeStruct((8, 128), F32),
                   jax.ShapeDtypeStruct((8, d), F32)),
        grid=(t // ROWS,),
        in_specs=[row, row, row, row, pl.BlockSpec((1, d), lambda i: (0, 0))],
        out_specs=(row, row, row, pl.BlockSpec((8, 128), lambda i: (0, 0)), pl.BlockSpec((8, d), lambda i: (0, 0))),
        compiler_params=_cp(("arbitrary",)),
    )(h2, pgl, pp, target, g_final)


def _merge_fwd(proj, out_a, out_s, *, name):
    t = proj.shape[0]

    def body(ga_ref, gs_ref, a_ref, s_ref, o_ref):
        o_ref[...] = (_sigmoid(ga_ref[...]) * a_ref[...] + _sigmoid(gs_ref[...]) * s_ref[...]).astype(BF16)

    row = pl.BlockSpec((ROWS, D), lambda i: (i, 0))
    return pl.pallas_call(
        body, name=name, out_shape=jax.ShapeDtypeStruct((t, D), BF16), grid=(t // ROWS,),
        in_specs=[pl.BlockSpec((ROWS, D), lambda i: (i, O_GA // D)), pl.BlockSpec((ROWS, D), lambda i: (i, O_GS // D)),
                  row, row],
        out_specs=row, compiler_params=_cp(("parallel",)),
    )(proj, proj, out_a, out_s)


def _merge_bwd(proj, out_a, out_s, dmerged, *, name):
    t = proj.shape[0]
    assert O_GA == 0 and O_GS == D

    def body(ga_ref, gs_ref, a_ref, s_ref, dm_ref, da_ref, ds_ref, dp_ref):
        sa = _sigmoid(ga_ref[...])
        ss = _sigmoid(gs_ref[...])
        dm = dm_ref[...]
        da_ref[...] = (dm * sa).astype(BF16)
        ds_ref[...] = (dm * ss).astype(BF16)
        dp_ref[:, :D] = (dm * a_ref[...] * sa * (1.0 - sa)).astype(BF16)
        dp_ref[:, D:] = (dm * s_ref[...] * ss * (1.0 - ss)).astype(BF16)

    row = pl.BlockSpec((ROWS, D), lambda i: (i, 0))
    o = jax.ShapeDtypeStruct((t, D), BF16)
    return pl.pallas_call(
        body, name=name, out_shape=(o, o, jax.ShapeDtypeStruct((t, NP), BF16)), grid=(t // ROWS,),
        in_specs=[pl.BlockSpec((ROWS, D), lambda i: (i, O_GA // D)), pl.BlockSpec((ROWS, D), lambda i: (i, O_GS // D)),
                  row, row, row],
        out_specs=(row, row, pl.BlockSpec((ROWS, 2 * D), lambda i: (i, 0))), compiler_params=_cp(("parallel",)),
    )(proj, proj, out_a, out_s, dmerged)


def _swiglu_fwd(f, w_gate, w_up, *, name, tn=512, jobs=()):
    t, d = f.shape
    n = w_gate.shape[1]

    def body(f_ref, wg_ref, wu_ref, g_ref, u_ref, a_ref):
        fv = f_ref[...]
        g = _dot(fv, wg_ref[...])
        u = _dot(fv, wu_ref[...])
        g_ref[...] = g.astype(BF16)
        u_ref[...] = u.astype(BF16)
        a_ref[...] = (g * _sigmoid(g) * u).astype(BF16)

    col = pl.BlockSpec((t, tn), lambda j: (0, j))
    wcol = pl.BlockSpec((d, tn), lambda j: (0, j))
    return _call(
        body, jobs=jobs, name=name,
        out_shape=(jax.ShapeDtypeStruct((t, n), BF16), jax.ShapeDtypeStruct((t, n), BF16),
                   jax.ShapeDtypeStruct((t, n), BF16)),
        grid=(n // tn,),
        in_specs=[pl.BlockSpec((t, d), lambda j: (0, 0)), wcol, wcol],
        out_specs=(col, col, col), compiler_params=_cp(("parallel",)),
    )(f, w_gate, w_up)


def _swiglu_bwd(dh, w_down, gate, up, *, name, tn=512, jobs=()):
    t, d = dh.shape
    n = w_down.shape[0]

    def body(dh_ref, w_ref, g_ref, u_ref, dg_ref, du_ref):
        da = _dot_nt(dh_ref[...], w_ref[...])
        g = g_ref[...].astype(F32)
        s = _sigmoid(g)
        du_ref[...] = (da * g * s).astype(BF16)
        dg_ref[...] = (da * u_ref[...].astype(F32) * s * (1.0 + g * (1.0 - s))).astype(BF16)

    col = pl.BlockSpec((t, tn), lambda j: (0, j))
    o = jax.ShapeDtypeStruct((t, n), BF16)
    return _call(
        body, jobs=jobs, name=name, out_shape=(o, o), grid=(n // tn,),
        in_specs=[pl.BlockSpec((t, d), lambda j: (0, 0)), pl.BlockSpec((tn, d), lambda j: (j, 0)), col, col],
        out_specs=(col, col), compiler_params=_cp(("parallel",)),
    )(dh, w_down, gate, up)


def _gated_norm_fwd(y_pre, proj, g_ssd, *, name):
    t = y_pre.shape[0]

    def body(y_ref, z_ref, g_ref, o_ref):
        z = z_ref[...]
        v = y_ref[...] * z * _sigmoid(z)
        r = lax.rsqrt(jnp.mean(v * v, axis=-1, keepdims=True) + SSM_EPS)
        o_ref[...] = (v * r * g_ref[...]).astype(BF16)

    row = pl.BlockSpec((ROWS, DI), lambda i: (i, 0))
    return pl.pallas_call(
        body, name=name, out_shape=jax.ShapeDtypeStruct((t, DI), BF16), grid=(t // ROWS,),
        in_specs=[row, pl.BlockSpec((ROWS, DI), lambda i: (i, O_Z // DI)), pl.BlockSpec((1, DI), lambda i: (0, 0))],
        out_specs=row, compiler_params=_cp(("parallel",)),
    )(y_pre, proj, g_ssd)


def _gated_norm_bwd(y_pre, proj, g_ssd, dyn, dproj, *, name, jobs=()):
    t = y_pre.shape[0]

    def body(y_ref, z_ref, g_ref, dyn_ref, _, dy_ref, dz_ref, dg_ref):
        z = z_ref[...]
        s = _sigmoid(z)
        sz = z * s
        yv = y_ref[...]
        v = yv * sz
        r = lax.rsqrt(jnp.mean(v * v, axis=-1, keepdims=True) + SSM_EPS)
        vh = v * r
        dn = dyn_ref[...]
        dvh = dn * g_ref[...]
        dv = r * (dvh - vh * jnp.mean(dvh * vh, axis=-1, keepdims=True))
        dy_ref[...] = dv * sz
        dz_ref[...] = (dv * yv * s * (1.0 + z * (1.0 - s))).astype(BF16)

        @pl.when(pl.program_id(0) == 0)
        def _():
            dg_ref[...] = jnp.zeros_like(dg_ref)

        dg_ref[...] += jnp.broadcast_to(jnp.sum(dn * vh, axis=0, keepdims=True), dg_ref.shape)

    row = pl.BlockSpec((ROWS, DI), lambda i: (i, 0))
    return _call(
        body, jobs=jobs, name=name,
        out_shape=(jax.ShapeDtypeStruct((t, DI), F32), jax.ShapeDtypeStruct(dproj.shape, BF16),
                   jax.ShapeDtypeStruct((8, DI), F32)),
        grid=(t // ROWS,),
        in_specs=[row, pl.BlockSpec((ROWS, DI), lambda i: (i, O_Z // DI)), pl.BlockSpec((1, DI), lambda i: (0, 0)), row, ANY],
        out_specs=(row, pl.BlockSpec((ROWS, DI), lambda i: (i, O_Z // DI)), pl.BlockSpec((8, DI), lambda i: (0, 0))),
        compiler_params=_cp(("arbitrary",)), aliases={4: 1},
    )(y_pre, proj, g_ssd, dyn, dproj)


CONV_TC = 512


def _shift_down(x, s, row):
    if s == 0:
        return x
    return jnp.where(row >= s, pltpu.roll(x, s, 0), 0.0)


def _shift_up(x, s, row, t):
    if s == 0:
        return x
    return jnp.where(row < t - s, pltpu.roll(x, t - s, 0), 0.0)


def _conv_fwd(proj, conv_w, conv_b, *, name):
    t = proj.shape[0]

    def body(x_ref, w_ref, b_ref, o_ref):
        x = x_ref[...]
        row = lax.broadcasted_iota(jnp.int32, x.shape, 0)
        pre = jnp.broadcast_to(b_ref[...], x.shape)
        for k in range(CW):
            pre = pre + w_ref[k:k + 1, :] * _shift_down(x, CW - 1 - k, row)
        o_ref[...] = pre * _sigmoid(pre)

    return pl.pallas_call(
        body, name=name, out_shape=jax.ShapeDtypeStruct((t, CONV), F32), grid=(CONV // CONV_TC,),
        in_specs=[pl.BlockSpec((t, CONV_TC), lambda j: (0, O_XBC // CONV_TC + j)),
                  pl.BlockSpec((CW, CONV_TC), lambda j: (0, j)), pl.BlockSpec((1, CONV_TC), lambda j: (0, j))],
        out_specs=pl.BlockSpec((t, CONV_TC), lambda j: (0, j)), compiler_params=_cp(("parallel",)),
    )(proj, conv_w, conv_b)


def _conv_bwd(proj, conv_w, conv_b, dxs, db, dc, dproj, *, name, jobs=()):
    t = proj.shape[0]
    nx = DI // CONV_TC
    assert NG * NS == CONV_TC

    def body(x_ref, w_ref, b_ref, dxs_ref, db_ref, dc_ref, _, dx_ref, dw_ref, dbias_ref):
        j = pl.program_id(0)
        x = x_ref[...]
        row = lax.broadcasted_iota(jnp.int32, x.shape, 0)
        xs = [_shift_down(x, CW - 1 - k, row) for k in range(CW)]
        pre = jnp.broadcast_to(b_ref[...], x.shape)
        for k in range(CW):
            pre = pre + w_ref[k:k + 1, :] * xs[k]
        s = _sigmoid(pre)
        da = jnp.where(j < nx, dxs_ref[...], jnp.where(j == nx, db_ref[...], dc_ref[...]))
        dpre = da * s * (1.0 + pre * (1.0 - s))
        dx = jnp.zeros_like(x)
        row8 = lax.broadcasted_iota(jnp.int32, dw_ref.shape, 0)
        dw = jnp.zeros(dw_ref.shape, F32)
        for k in range(CW):
            dx = dx + w_ref[k:k + 1, :] * _shift_up(dpre, CW - 1 - k, row, t)
            dw = dw + jnp.where(row8 == k, jnp.sum(dpre * xs[k], axis=0, keepdims=True), 0.0)
        dx_ref[...] = dx.astype(BF16)
        dw_ref[...] = dw
        dbias_ref[...] = jnp.broadcast_to(jnp.sum(dpre, axis=0, keepdims=True), dbias_ref.shape)

    col8 = pl.BlockSpec((8, CONV_TC), lambda j: (0, j))
    xbc = pl.BlockSpec((t, CONV_TC), lambda j: (0, O_XBC // CONV_TC + j))
    whole = pl.BlockSpec((t, CONV_TC), lambda j: (0, 0))
    return _call(
        body, jobs=jobs, name=name,
        out_shape=(jax.ShapeDtypeStruct(dproj.shape, BF16), jax.ShapeDtypeStruct((8, CONV), F32),
                   jax.ShapeDtypeStruct((8, CONV), F32)),
        grid=(CONV // CONV_TC,),
        in_specs=[xbc, pl.BlockSpec((CW, CONV_TC), lambda j: (0, j)), pl.BlockSpec((1, CONV_TC), lambda j: (0, j)),
                  pl.BlockSpec((t, CONV_TC), lambda j: (0, jnp.minimum(j, nx - 1))), whole, whole, ANY],
        out_specs=(xbc, col8, col8),
        compiler_params=_cp(("arbitrary",)), aliases={6: 0},
    )(proj, conv_w, conv_b, dxs, db, dc, dproj)


def _rope_tables(positions, t):
    half = HD // 2
    inv_freq = ROPE_THETA ** (-jnp.arange(half, dtype=F32) * 2.0 / HD)
    ang = positions.reshape(t).astype(F32)[:, None] * inv_freq
    cos, sin = jnp.cos(ang), jnp.sin(ang)
    return jnp.concatenate([cos] * 4, axis=1), jnp.concatenate([-sin, sin] * 2, axis=1)


def _lane_consts():
    lane = lax.broadcasted_iota(jnp.int32, (L, 128), 1)
    return lane, (lane % HD) < (HD // 2), lane < HD


def _rope(tv, cos, sin, lo):
    return tv * cos + jnp.where(lo, pltpu.roll(tv, 128 - HD // 2, 1), pltpu.roll(tv, HD // 2, 1)) * sin


def _rope_t(dv, cos, sin, lo):
    ds = dv * sin
    return dv * cos + jnp.where(lo, pltpu.roll(ds, 128 - HD // 2, 1), pltpu.roll(ds, HD // 2, 1))


def _placed(chunk, g, half0):
    own = jnp.where(half0 if g % 2 == 0 else jnp.logical_not(half0), chunk, 0.0)
    other = pltpu.roll(own, HD, 1)
    return (own, other) if g % 2 == 0 else (other, own)


def _unplace(acc, hf, g, half0):
    v = jnp.where(half0 if hf == 0 else jnp.logical_not(half0), acc, 0.0)
    return v if hf == g % 2 else pltpu.roll(v, HD, 1)


def _attn_fwd(proj, cos, sin, sinks, *, name, jobs=()):
    t = proj.shape[0]
    nb = t // L
    scale = HD ** -0.5

    def body(sink_ref, q_ref, kc_ref, kp_ref, vc_ref, vp_ref, cc_ref, sc_ref, cp_ref, sp_ref, o_ref, lse_ref):
        i = pl.program_id(0)
        lane, lo, half0 = _lane_consts()
        cos_c, sin_c, cos_p, sin_p = cc_ref[...], sc_ref[...], cp_ref[...], sp_ref[...]
        row = lax.broadcasted_iota(jnp.int32, (L, 2 * L), 0)
        col = lax.broadcasted_iota(jnp.int32, (L, 2 * L), 1)
        valid = jnp.logical_or(jnp.logical_and(jnp.logical_and(col < L, col > row), i > 0),
                               jnp.logical_and(col >= L, col - L <= row))
        kc = [_rope(kc_ref[:, 128 * m:128 * (m + 1)], cos_c, sin_c, lo) for m in range(2)]
        kp = [_rope(kp_ref[:, 128 * m:128 * (m + 1)], cos_p, sin_p, lo) for m in range(2)]
        lse_acc = jnp.zeros((L, 128), F32)
        outs = [jnp.zeros((L, 128), F32) for _ in range(QD // 128)]
        qs = [(_rope(q_ref[:, 128 * ch:128 * (ch + 1)], cos_c, sin_c, lo) * scale).astype(BF16) for ch in range(QD // 128)]
        both = lambda prev, cur, g: [jnp.concatenate([a, b], axis=0).astype(BF16)
                                     for a, b in zip(_placed(prev, g, half0), _placed(cur, g, half0))]
        for g in range(NKV):
            sl = slice(128 * (g // 2), 128 * (g // 2 + 1))
            kv = both(kp[g // 2], kc[g // 2], g)
            vv = both(vp_ref[:, sl], vc_ref[:, sl], g)
            for r in range(NQH // NKV):
                h = g * (NQH // NKV) + r
                ch, hf = h // 2, h % 2
                s = jnp.where(valid, _dot_nt(qs[ch], kv[hf]), NEG)
                sink = sink_ref[0, h]
                mx = jnp.maximum(jnp.max(s, axis=-1, keepdims=True), sink)
                e = jnp.exp(s - mx)
                den = jnp.sum(e, axis=-1, keepdims=True) + jnp.exp(sink - mx)
                outs[ch] = outs[ch] + _dot((e * (1.0 / den)).astype(BF16), vv[hf])
                lse_acc = jnp.where(lane == h, mx + jnp.log(den), lse_acc)
        for ch in range(QD // 128):
            o_ref[:, 128 * ch:128 * (ch + 1)] = outs[ch].astype(BF16)
        lse_ref[...] = lse_acc

    prev = lambda i: jnp.maximum(i - 1, 0)
    tab_c = pl.BlockSpec((L, 128), lambda i: (i, 0))
    tab_p = pl.BlockSpec((L, 128), lambda i: (prev(i), 0))
    return _call(
        body, jobs=jobs, name=name,
        out_shape=(jax.ShapeDtypeStruct((t, QD), BF16), jax.ShapeDtypeStruct((t, 128), F32)),
        grid=(nb,),
        in_specs=[pl.BlockSpec(memory_space=pltpu.SMEM),
                  pl.BlockSpec((L, QD), lambda i: (i, O_Q // QD)),
                  pl.BlockSpec((L, KVD), lambda i: (i, O_K // KVD)), pl.BlockSpec((L, KVD), lambda i: (prev(i), O_K // KVD)),
                  pl.BlockSpec((L, KVD), lambda i: (i, O_V // KVD)), pl.BlockSpec((L, KVD), lambda i: (prev(i), O_V // KVD)),
                  tab_c, tab_c, tab_p, tab_p],
        out_specs=(pl.BlockSpec((L, QD), lambda i: (i, 0)), pl.BlockSpec((L, 128), lambda i: (i, 0))),
        compiler_params=_cp(("parallel",)),
    )(sinks, proj, proj, proj, proj, proj, cos, sin, cos, sin)


def _attn_bwd(proj, cos, sin, sinks, attn, lse, dattn, dproj, *, name, jobs=()):
    t = proj.shape[0]
    nb = t // L
    scale = HD ** -0.5

    def body(sink_ref, qi_ref, qn_ref, kc_ref, kp_ref, vc_ref, vp_ref, doi_ref, don_ref, oi_ref, on_ref,
             lsei_ref, lsen_ref, cc_ref, sc_ref, cp_ref, sp_ref, cn_ref, sn_ref, _, dqkv_ref, dsk_ref):
        i = pl.program_id(0)
        lane, lo, half0 = _lane_consts()
        half1 = jnp.logical_not(half0)
        cos_c, sin_c = cc_ref[...], sc_ref[...]
        row = lax.broadcasted_iota(jnp.int32, (L, 2 * L), 0)
        col = lax.broadcasted_iota(jnp.int32, (L, 2 * L), 1)
        valid = jnp.logical_or(jnp.logical_and(jnp.logical_and(col < L, col > row), i > 0),
                               jnp.logical_and(col >= L, col - L <= row))
        m_next = jnp.logical_and(col[:, :L] > row[:, :L], i < nb - 1)
        kc = [_rope(kc_ref[:, 128 * m:128 * (m + 1)], cos_c, sin_c, lo) for m in range(2)]
        kp = [_rope(kp_ref[:, 128 * m:128 * (m + 1)], cp_ref[...], sp_ref[...], lo) for m in range(2)]
        lse_i, lse_n = lsei_ref[...], lsen_ref[...]
        dk_acc = [jnp.zeros((L, 128), F32) for _ in range(2)]
        dv_acc = [jnp.zeros((L, 128), F32) for _ in range(2)]
        dsk_acc = jnp.zeros((1, 128), F32)
        lane1 = lax.broadcasted_iota(jnp.int32, (1, 128), 1)
        both = lambda prev, cur, g: [jnp.concatenate([a, b], axis=0).astype(BF16)
                                     for a, b in zip(_placed(prev, g, half0), _placed(cur, g, half0))]
        kvs = [both(kp[g // 2], kc[g // 2], g) for g in range(NKV)]
        vvs = [both(vp_ref[:, 128 * (g // 2):128 * (g // 2 + 1)], vc_ref[:, 128 * (g // 2):128 * (g // 2 + 1)], g)
               for g in range(NKV)]
        for ch in range(QD // 128):
            sl = slice(128 * ch, 128 * (ch + 1))
            q_i = (_rope(qi_ref[:, sl], cos_c, sin_c, lo) * scale).astype(BF16)
            q_n = (_rope(qn_ref[:, sl], cn_ref[...], sn_ref[...], lo) * scale).astype(BF16)
            q_in = jnp.concatenate([q_i, q_n], axis=0)
            do_i, do_n = doi_ref[:, sl], don_ref[:, sl]
            do_ib, do_nb = do_i.astype(BF16), do_n.astype(BF16)
            do_in = jnp.concatenate([do_ib, do_nb], axis=0)
            od_i = do_i * oi_ref[:, sl].astype(F32)
            od_n = do_n * on_ref[:, sl].astype(F32)
            dq_ch = jnp.zeros((L, 128), F32)
            for hf in range(2):
                h = 2 * ch + hf
                g = h // (NQH // NKV)
                hm = half0 if hf == 0 else half1
                kv, vv = kvs[g][hf], vvs[g][hf]
                kcv, vcv = kv[L:], vv[L:]
                dl_i = jnp.sum(jnp.where(hm, od_i, 0.0), axis=-1, keepdims=True)
                dl_n = jnp.sum(jnp.where(hm, od_n, 0.0), axis=-1, keepdims=True)
                ls_i = jnp.sum(jnp.where(lane == h, lse_i, 0.0), axis=-1, keepdims=True)
                ls_n = jnp.sum(jnp.where(lane == h, lse_n, 0.0), axis=-1, keepdims=True)
                p = jnp.where(valid, jnp.exp(_dot_nt(q_i, kv) - ls_i), 0.0)
                ds = (p * (_dot_nt(do_ib, vv) - dl_i)).astype(BF16)
                dq_ch = dq_ch + jnp.where(hm, _dot(ds, kv) * scale, 0.0)
                sink = sink_ref[0, h]
                dsk = -jnp.sum(jnp.exp(sink - ls_i) * dl_i, axis=0, keepdims=True)
                dsk_acc = dsk_acc + jnp.where(lane1 == h, dsk, 0.0)
                p_n = jnp.where(m_next, jnp.exp(_dot_nt(q_n, kcv) - ls_n), 0.0)
                ds_n = (p_n * (_dot_nt(do_nb, vcv) - dl_n)).astype(BF16)
                dv_h = _dot_tn(jnp.concatenate([p[:, L:].astype(BF16), p_n.astype(BF16)], axis=0), do_in)
                dk_h = _dot_tn(jnp.concatenate([ds[:, L:], ds_n], axis=0), q_in)
                dv_acc[g // 2] = dv_acc[g // 2] + _unplace(dv_h, hf, g, half0)
                dk_acc[g // 2] = dk_acc[g // 2] + _unplace(dk_h, hf, g, half0)
            dqkv_ref[:, sl] = _rope_t(dq_ch, cos_c, sin_c, lo).astype(BF16)
        for m in range(2):
            dqkv_ref[:, QD + 128 * m:QD + 128 * (m + 1)] = _rope_t(dk_acc[m], cos_c, sin_c, lo).astype(BF16)
            dqkv_ref[:, QD + KVD + 128 * m:QD + KVD + 128 * (m + 1)] = dv_acc[m].astype(BF16)

        @pl.when(i == 0)
        def _():
            dsk_ref[...] = jnp.zeros_like(dsk_ref)

        dsk_ref[...] += jnp.broadcast_to(dsk_acc, dsk_ref.shape)

    prev = lambda i: jnp.maximum(i - 1, 0)
    nxt = lambda i: jnp.minimum(i + 1, nb - 1)
    cur_q = pl.BlockSpec((L, QD), lambda i: (i, 0))
    nxt_q = pl.BlockSpec((L, QD), lambda i: (nxt(i), 0))
    tab = lambda f: pl.BlockSpec((L, 128), lambda i: (f(i), 0))
    ident = lambda i: i
    qkv = QD + 2 * KVD
    assert O_K == O_Q + QD and O_V == O_K + KVD and O_Q % qkv == 0
    return _call(
        body, jobs=jobs, name=name,
        out_shape=(jax.ShapeDtypeStruct(dproj.shape, BF16), jax.ShapeDtypeStruct((8, 128), F32)),
        grid=(nb,),
        in_specs=[pl.BlockSpec(memory_space=pltpu.SMEM),
                  pl.BlockSpec((L, QD), lambda i: (i, O_Q // QD)), pl.BlockSpec((L, QD), lambda i: (nxt(i), O_Q // QD)),
                  pl.BlockSpec((L, KVD), lambda i: (i, O_K // KVD)), pl.BlockSpec((L, KVD), lambda i: (prev(i), O_K // KVD)),
                  pl.BlockSpec((L, KVD), lambda i: (i, O_V // KVD)), pl.BlockSpec((L, KVD), lambda i: (prev(i), O_V // KVD)),
                  cur_q, nxt_q, cur_q, nxt_q, tab(ident), tab(nxt),
                  tab(ident), tab(ident), tab(prev), tab(prev), tab(nxt), tab(nxt), ANY],
        out_specs=(pl.BlockSpec((L, qkv), lambda i: (i, O_Q // qkv)), pl.BlockSpec((8, 128), lambda i: (0, 0))),
        compiler_params=_cp(("arbitrary",)), aliases={19: 0},
    )(sinks, proj, proj, proj, proj, proj, proj, dattn, dattn, attn, attn, lse, lse, cos, sin, cos, sin, cos, sin, dproj)


PAIRS = NH // NG // 2


def _softplus(x):
    return jnp.maximum(x, 0.0) + jnp.log(1.0 + jnp.exp(-jnp.abs(x)))


def _ssd_chunk(g, xps, dtr, bm, cm, sps, dtb, alog, dsk):
    lane = lax.broadcasted_iota(jnp.int32, (L, 128), 1)
    lane1 = lax.broadcasted_iota(jnp.int32, (1, 128), 1)
    row = lax.broadcasted_iota(jnp.int32, (L, L), 0)
    col = lax.broadcasted_iota(jnp.int32, (L, L), 1)
    rowc = lax.broadcasted_iota(jnp.int32, (128, 1), 0)
    tril = col <= row
    dt = _softplus(dtr + dtb)
    a = dt * (-jnp.exp(alog))
    a_cs = lax.dot_general(tril.astype(F32), a, (((1,), (0,)), ((), ())), precision=lax.Precision.HIGHEST,
                           preferred_element_type=F32)
    a_cst = a_cs.T
    a_last = jnp.sum(jnp.where(row == L - 1, a_cs, 0.0), axis=0, keepdims=True)
    cb = _bdot_nt(cm, bm)
    ys, snew = [], []
    for q in range(PAIRS):
        xp, sp = xps[q], sps[q]
        y_pair = jnp.zeros((L, 128), F32)
        st_pair = jnp.zeros((128, NS), F32)
        keep = jnp.zeros((128, 1), F32)
        for hh in range(2):
            h = g * 2 * PAIRS + 2 * q + hh
            hm = (lane < HD) if hh == 0 else (lane >= HD)
            rm = (rowc < HD) if hh == 0 else (rowc >= HD)
            dt_h = jnp.sum(jnp.where(lane == h, dt, 0.0), axis=1, keepdims=True)
            acs_h = jnp.sum(jnp.where(lane == h, a_cs, 0.0), axis=1, keepdims=True)
            acst_h = jnp.sum(jnp.where(row == h, a_cst, 0.0), axis=0, keepdims=True)
            al_h = jnp.sum(jnp.where(lane1 == h, a_last, 0.0), axis=1, keepdims=True)
            dsk_h = jnp.sum(jnp.where(lane1 == h, dsk, 0.0), axis=1, keepdims=True)
            decay = jnp.where(tril, jnp.exp(jnp.where(tril, acs_h - acst_h, 0.0)), 0.0)
            xh = jnp.where(hm, xp, 0.0)
            xd = xh * dt_h
            y = _bdot(cb * decay, xd)
            y = y + jnp.where(hm, _bdot_nt(cm * jnp.exp(acs_h), sp), 0.0)
            y_pair = y_pair + y + dsk_h * xh
            st_pair = st_pair + _bdot_tn(xd, bm * jnp.exp(al_h - acs_h))
            keep = keep + jnp.where(rm, jnp.exp(al_h), 0.0)
        ys.append(y_pair)
        snew.append(sp * keep + st_pair)
    return ys, snew


def _ssd_specs(t):
    nc = t // L
    xs = lambda f: pl.BlockSpec((L, 128 * PAIRS), lambda c, g: (f(c), g))
    bspec = lambda f: pl.BlockSpec((L, NS), lambda c, g: (f(c), DI // NS + g))
    cspec = lambda f: pl.BlockSpec((L, NS), lambda c, g: (f(c), DI // NS + NG + g))
    dts = lambda f: pl.BlockSpec((L, 128), lambda c, g: (f(c), O_DT // 128))
    par = pl.BlockSpec((1, 128), lambda c, g: (0, 0))
    st = lambda f: pl.BlockSpec((1, 1, PAIRS, 128, NS), lambda c, g: (f(c), g, 0, 0, 0))
    return nc, xs, bspec, cspec, dts, par, st


def _ssd_fwd(xbc_act, proj, dtb, alog, dsk, *, name, jobs=()):
    t = proj.shape[0]
    nc, xs, bspec, cspec, dts, par, st = _ssd_specs(t)
    ident = lambda c: c

    def body(x_ref, b_ref, c_ref, dt_ref, dtb_ref, al_ref, dsk_ref, y_ref, sin_ref, s_ref):
        c, g = pl.program_id(0), pl.program_id(1)

        @pl.when(c == 0)
        def _():
            s_ref[g] = jnp.zeros((PAIRS, 128, NS), F32)

        sps = [s_ref[g, q] for q in range(PAIRS)]
        for q in range(PAIRS):
            sin_ref[0, 0, q] = sps[q]
        xps = [x_ref[:, 128 * q:128 * (q + 1)] for q in range(PAIRS)]
        ys, snew = _ssd_chunk(g, xps, dt_ref[...], b_ref[...], c_ref[...], sps, dtb_ref[...], al_ref[...], dsk_ref[...])
        for q in range(PAIRS):
            y_ref[:, 128 * q:128 * (q + 1)] = ys[q]
            s_ref[g, q] = snew[q]

    return _call(
        body, jobs=jobs, name=name,
        out_shape=(jax.ShapeDtypeStruct((t, DI), F32), jax.ShapeDtypeStruct((nc, NG, PAIRS, 128, NS), F32)),
        grid=(nc, NG),
        in_specs=[xs(ident), bspec(ident), cspec(ident), dts(ident), par, par, par],
        out_specs=(pl.BlockSpec((L, 128 * PAIRS), lambda c, g: (c, g)), st(ident)),
        scratch_shapes=[pltpu.VMEM((NG, PAIRS, 128, NS), F32)],
        compiler_params=_cp(("arbitrary", "arbitrary")),
    )(xbc_act, xbc_act, xbc_act, proj, dtb, alog, dsk)


def _ssd_bwd(xbc_act, proj, dtb, alog, dsk, states, dy, dproj, *, name, jobs=()):
    t = proj.shape[0]
    nc, xs, bspec, cspec, dts, par, st = _ssd_specs(t)
    rev = lambda c: nc - 1 - c

    def body(x_ref, b_ref, c_ref, dt_ref, dtb_ref, al_ref, dsk_ref, sin_ref, dy_ref, _,
             dx_ref, db_ref, dc_ref, ddtp_ref, ddtb_ref, dal_ref, ddsk_ref, ds_ref, ddt_ref):
        c, g = pl.program_id(0), pl.program_id(1)

        @pl.when(c == 0)
        def _():
            ds_ref[g] = jnp.zeros((PAIRS, 128, NS), F32)

        @pl.when(jnp.logical_and(c == 0, g == 0))
        def _():
            ddtb_ref[...] = jnp.zeros_like(ddtb_ref)
            dal_ref[...] = jnp.zeros_like(dal_ref)
            ddsk_ref[...] = jnp.zeros_like(ddsk_ref)

        @pl.when(g == 0)
        def _():
            ddt_ref[...] = jnp.zeros_like(ddt_ref)

        sps = [sin_ref[0, 0, q] for q in range(PAIRS)]
        xps = [x_ref[:, 128 * q:128 * (q + 1)] for q in range(PAIRS)]
        _, vjp = jax.vjp(functools.partial(_ssd_chunk, g), xps, dt_ref[...], b_ref[...], c_ref[...], sps,
                         dtb_ref[...], al_ref[...], dsk_ref[...])
        dys = [dy_ref[:, 128 * q:128 * (q + 1)] for q in range(PAIRS)]
        dss = [ds_ref[g, q] for q in range(PAIRS)]
        dxps, ddt, db, dc, dsps, ddtb, dal, ddsk = vjp((dys, dss))
        for q in range(PAIRS):
            dx_ref[:, 128 * q:128 * (q + 1)] = dxps[q]
            ds_ref[g, q] = dsps[q]
        db_ref[...] = db
        dc_ref[...] = dc
        ddt_ref[...] += ddt
        ddtb_ref[...] += jnp.broadcast_to(ddtb, ddtb_ref.shape)
        dal_ref[...] += jnp.broadcast_to(dal, dal_ref.shape)
        ddsk_ref[...] += jnp.broadcast_to(ddsk, ddsk_ref.shape)

        @pl.when(g == NG - 1)
        def _():
            ddtp_ref[:, :128] = ddt_ref[...].astype(BF16)
            ddtp_ref[:, 128:] = jnp.zeros((L, DT_PAD - 128), BF16)

    acc = pl.BlockSpec((8, 128), lambda c, g: (0, 0))
    o8 = jax.ShapeDtypeStruct((8, 128), F32)
    return _call(
        body, jobs=jobs, name=name,
        out_shape=(jax.ShapeDtypeStruct((t, DI), F32), jax.ShapeDtypeStruct((t, NG * NS), F32),
                   jax.ShapeDtypeStruct((t, NG * NS), F32), jax.ShapeDtypeStruct(dproj.shape, BF16), o8, o8, o8),
        grid=(nc, NG),
        in_specs=[xs(rev), bspec(rev), cspec(rev), dts(rev), par, par, par, st(rev),
                  pl.BlockSpec((L, 128 * PAIRS), lambda c, g: (rev(c), g)), ANY],
        out_specs=(pl.BlockSpec((L, 128 * PAIRS), lambda c, g: (rev(c), g)),
                   pl.BlockSpec((L, NS), lambda c, g: (rev(c), g)), pl.BlockSpec((L, NS), lambda c, g: (rev(c), g)),
                   pl.BlockSpec((L, DT_PAD), lambda c, g: (rev(c), O_DT // DT_PAD)), acc, acc, acc),
        scratch_shapes=[pltpu.VMEM((NG, PAIRS, 128, NS), F32), pltpu.VMEM((L, 128), F32)],
        compiler_params=_cp(("arbitrary", "arbitrary")), aliases={9: 3},
    )(xbc_act, xbc_act, xbc_act, proj, dtb, alog, dsk, states, dy, dproj)


def _pad_lanes(v, n=128):
    return jnp.pad(v, ((0, 0), (0, n - v.shape[1])))


class _LocalPlan:
    core = 0

    def __init__(self, big):
        self.big, self.grad, self.halves = big, {}, {}

    def w(self, n):
        return self.big[n]

    def g(self, n, a):
        self.grad[n] = a

    def g_half(self, n, which, a):
        self.halves[which] = a
        if len(self.halves) == 2:
            self.grad[n] = jnp.concatenate([self.halves["keep"], self.halves["send"]], axis=0)

    def jobs(self, tag):
        return ()


def _local_step(x, p, positions, target, small, plan):
    t = x.shape[0]
    cos, sin = _rope_tables(positions, t)
    dtb, alog, dsk = _pad_lanes(small["dt_bias"]), _pad_lanes(small["a_log"]), _pad_lanes(small["d_skip"])
    w, jobs = plan.w, plan.jobs

    def mm(a, b, *, name, tn=512, **kw):
        return _matmul(a, b, tm=t, tn=tn, name=name, jobs=jobs(name), **kw)

    tkl = FFN // 4

    def dw(wname, a, dy, *, name, tm):
        plan.g(wname, _matmul(a, dy, ta=True, out_dtype=BF16, tm=tm, tn=512, tk=t, name=name, jobs=jobs(name)))

    u = _rmsnorm_fwd(x, small["g_mix"], name="norm_mix")
    proj = mm(u, w("w_in"), tk=D, name="mm_in")
    attn, lse = _attn_fwd(proj, cos, sin, small["sinks"], name="attn_fwd", jobs=jobs("attn_fwd"))
    out_a = mm(attn, w("w_attn_br"), tk=QD, name="mm_attn_br")
    xbc_act = _conv_fwd(proj, small["conv_w"], small["conv_b"], name="conv_fwd")
    y_pre, states = _ssd_fwd(xbc_act, proj, dtb, alog, dsk, name="ssd_fwd", jobs=jobs("ssd_fwd"))
    yn = _gated_norm_fwd(y_pre, proj, small["g_ssd"], name="gated_norm_fwd")
    out_s = mm(yn, w("w_ssd_br"), tk=DI, name="mm_ssd_br")
    merged = _merge_fwd(proj, out_a, out_s, name="merge_fwd")
    h1 = mm(merged, w("w_o"), add=x, tk=D, name="mm_o")
    f = _rmsnorm_fwd(h1, small["g_ffn"], name="norm_ffn")
    gate, up, act = _swiglu_fwd(f, w("w_gate"), w("w_up"), name="swiglu_fwd", jobs=jobs("swiglu_fwd"))
    h2 = mm(act, w("w_down"), add=h1, tk=tkl, name="mm_down")
    e = _rmsnorm_fwd(h2, small["g_ple"], name="norm_ple")
    pgl = mm(e, w("w_ple_gate"), tk=D, name="mm_ple_gate")
    pb = p.astype(BF16)
    pp = mm(pb, w("w_ple_proj"), tk=PLE, name="mm_ple_proj")
    dh3, dpgl, dpp, loss, dg_final = _final(h2, pgl, pp, target, small["g_final"].reshape(1, D), name="final")

    dw("w_ple_proj", pb, dpp, tm=PLE, name="mm_d_ple_proj")
    dw("w_ple_gate", e, dpgl, tm=D, name="mm_d_ple_gate")
    de = mm(dpgl, w("w_ple_gate"), tb=True, tk=D, name="mm_de")
    dh2, dh2b, dg_ple = _rmsnorm_bwd(h2, small["g_ple"], de, dh3, name="norm_ple_bwd", jobs=jobs("norm_ple_bwd"))
    dw("w_down", act, dh2b, tm=FFN // 2, name="mm_d_down")
    dgate, dup = _swiglu_bwd(dh2b, w("w_down"), gate, up, name="swiglu_bwd", jobs=jobs("swiglu_bwd"))
    dw("w_gate", f, dgate, tm=D, name="mm_d_gate")
    dw("w_up", f, dup, tm=D, name="mm_d_up")
    df = mm(dgate, w("w_gate"), tb=True, tn=1024, tk=tkl, name="mm_df_gate")
    df = mm(dup, w("w_up"), tb=True, add=df, tk=tkl, name="mm_df_up")
    dh1, dh1b, dg_ffn = _rmsnorm_bwd(h1, small["g_ffn"], df, dh2, name="norm_ffn_bwd", jobs=jobs("norm_ffn_bwd"))
    dw("w_o", merged, dh1b, tm=D, name="mm_d_o")
    dmerged = mm(dh1b, w("w_o"), tb=True, tk=D, name="mm_dmerged")
    dout_a, dout_s, dproj = _merge_bwd(proj, out_a, out_s, dmerged, name="merge_bwd")
    dw("w_attn_br", attn, dout_a, tm=QD, name="mm_d_attn_br")
    dw("w_ssd_br", yn, dout_s, tm=DI, name="mm_d_ssd_br")
    dattn = mm(dout_a, w("w_attn_br"), tb=True, tk=D, name="mm_dattn")
    dyn = mm(dout_s, w("w_ssd_br"), tb=True, tk=D, name="mm_dyn")
    dproj, dsinks = _attn_bwd(proj, cos, sin, small["sinks"], attn, lse, dattn, dproj, name="attn_bwd",
                              jobs=jobs("attn_bwd"))
    dy_pre, dproj, dg_ssd = _gated_norm_bwd(y_pre, proj, small["g_ssd"], dyn, dproj, name="gated_norm_bwd",
                                            jobs=jobs("gated_norm_bwd"))
    dxs, db, dc, dproj, ddtb, dalog, ddsk = _ssd_bwd(xbc_act, proj, dtb, alog, dsk, states, dy_pre, dproj, name="ssd_bwd",
                                                     jobs=jobs("ssd_bwd"))
    dproj, dconv_w, dconv_b = _conv_bwd(proj, small["conv_w"], small["conv_b"], dxs, db, dc, dproj, name="conv_bwd",
                                        jobs=jobs("conv_bwd"))
    for which, h in (("send", 1 - plan.core), ("keep", plan.core)):
        uh = lax.dynamic_slice_in_dim(u, h * (D // 2), D // 2, axis=1)
        name = "mm_d_in_" + which
        plan.g_half("w_in", which, _matmul(uh, dproj, ta=True, out_dtype=BF16, tm=D // 2, tn=512, tk=t, name=name,
                                           jobs=jobs(name)))
    du = mm(dproj, w("w_in"), tb=True, tn=1024, tk=tkl, name="mm_du")
    grad_x, _, dg_mix = _rmsnorm_bwd(x, small["g_mix"], du, dh1, name="norm_mix_bwd", jobs=jobs("norm_mix_bwd"))

    gs = {
        "g_mix": dg_mix[:1], "conv_w": dconv_w[:CW], "conv_b": dconv_b[:1], "dt_bias": ddtb[:1, :NH],
        "a_log": dalog[:1, :NH], "d_skip": ddsk[:1, :NH], "g_ssd": dg_ssd[:1], "sinks": dsinks[:1, :NQH],
        "g_ffn": dg_ffn[:1], "g_ple": dg_ple[:1], "g_final": dg_final[0],
    }
    return loss, grad_x, gs


def _to_kernel_cols(w):
    seg = lambda o, n: w[:, o:o + n]
    return jnp.concatenate([seg(R_GA, D), seg(R_GS, D), seg(R_Z, DI), seg(R_XBC, CONV), seg(R_Q, QD), seg(R_K, KVD),
                            seg(R_V, KVD), seg(R_DT, NH), jnp.zeros((w.shape[0], DT_PAD - NH), w.dtype)], axis=1)


def _from_kernel_cols(g):
    seg = lambda o, n: g[:, o:o + n]
    return jnp.concatenate([seg(O_Q, QD), seg(O_K, KVD), seg(O_V, KVD), seg(O_Z, DI), seg(O_XBC, CONV), seg(O_DT, NH),
                            seg(O_GA, D), seg(O_GS, D)], axis=1)


def _shard_pieces():
    segs = ((R_Q, QD, O_Q), (R_K, KVD, O_K), (R_V, KVD, O_V), (R_Z, DI, O_Z), (R_XBC, CONV, O_XBC), (R_DT, NH, O_DT),
            (R_GA, D, O_GA), (R_GS, D, O_GS))
    cs = IN_DIM // NCHIP
    out = []
    for j in range(NCHIP):
        for r0, n, k0 in segs:
            lo, hi = max(r0, j * cs), min(r0 + n, (j + 1) * cs)
            if lo < hi:
                out.append((j, lo - j * cs, hi - lo, k0 + lo - r0))
    return out


SLAB = IN_DIM // NCHIP
SLAB_PAD = -(-SLAB // 128) * 128
REMAP_ROWS = 256


def _lane_remap(src, dst_slabs, dst_cols, moves, *, name, jobs=()):
    s_n, rows, s_cols = src.shape
    assert s_cols % 128 == 0 and dst_cols % 128 == 0 and rows % REMAP_ROWS == 0
    half = REMAP_ROWS // 2

    def body(s_ref, d_ref):
        lane = lax.broadcasted_iota(jnp.int32, (half, 128), 1)
        tiles = {}

        def tile(j, m):
            if (j, m) not in tiles:
                tiles[j, m] = pltpu.bitcast(s_ref[j, :, 128 * m:128 * (m + 1)], jnp.uint32)
            return tiles[j, m]

        def window(j, base):
            m0, s = base // 128, base % 128
            left = tile(j, m0) if 0 <= m0 < s_cols // 128 else None
            if s == 0:
                return left
            right = tile(j, m0 + 1) if 0 <= m0 + 1 < s_cols // 128 else None
            left = None if left is None else pltpu.roll(left, 128 - s, 1)
            right = None if right is None else pltpu.roll(right, 128 - s, 1)
            if left is None or right is None:
                return right if left is None else left
            return jnp.where(lane < 128 - s, left, right)

        for ds in range(dst_slabs):
            for t in range(dst_cols // 128):
                o = 128 * t
                acc = jnp.zeros((half, 128), jnp.uint32)
                for sj, sc, n, dj, dc in moves:
                    lo, hi = max(o, dc) - o, min(o + 128, dc + n) - o
                    if dj != ds or lo >= hi:
                        continue
                    win = window(sj, o - dc + sc)
                    acc = win if (lo, hi) == (0, 128) else jnp.where(jnp.logical_and(lane >= lo, lane < hi), win, acc)
                d_ref[ds, :, o:o + 128] = pltpu.bitcast(acc, BF16)

    return _call(
        body, jobs=jobs, name=name, out_shape=jax.ShapeDtypeStruct((dst_slabs, rows, dst_cols), BF16),
        grid=(rows // REMAP_ROWS,),
        in_specs=[pl.BlockSpec((s_n, REMAP_ROWS, s_cols), lambda i: (0, i, 0))],
        out_specs=pl.BlockSpec((dst_slabs, REMAP_ROWS, dst_cols), lambda i: (0, i, 0)),
        compiler_params=_cp(("parallel",)),
    )(src)


def _slabs_to_kernel_cols(slabs, *, name, jobs=()):
    moves = [(j, a, n, 0, k0) for j, a, n, k0 in _shard_pieces()]
    return _lane_remap(slabs, 1, NP, moves, name=name, jobs=jobs)[0]


def _kernel_cols_to_slabs(g, *, name, jobs=()):
    moves = [(0, k0, n, j, a) for j, a, n, k0 in _shard_pieces()]
    return _lane_remap(g[None], NCHIP, SLAB_PAD, moves, name=name, jobs=jobs)


RELS = ((0, 1), (1, 0), (1, 1))
MATS = {
    n: (n, kind, 1, r, c, tp, tf) for n, kind, r, c, tp, tf in (
        ("w_in", "stk", 2048, SLAB_PAD, 256, 256),
        ("w_attn_br", "col", 1024, 512, 256, 256),
        ("w_ssd_br", "row", 512, 2048, 512, 256),
        ("w_o", "row", 512, 2048, 512, 256),
        ("w_gate", "col", 2048, 1408, 256, 256),
        ("w_up", "col", 2048, 1408, 256, 256),
        ("w_down", "row", 1408, 2048, 704, 704),
        ("w_ple_gate", "row", 512, 2048, 512, 256),
        ("w_ple_proj", "col", 256, 512, 128, 128),
    )}


def _pos():
    return lax.axis_index("x"), lax.axis_index("y"), lax.axis_index("c")


def _flip(v, a):
    return 1 - v if a else v


def _remote(src, dst, send, recv, dev):
    return pltpu.make_async_remote_copy(src_ref=src, dst_ref=dst, send_sem=send, recv_sem=recv, device_id=dev,
                                        device_id_type=MESH)


def _whole_shape(kind, g, r, c):
    return {"row": (g, NCHIP * r, c), "col": (g, r, NCHIP * c), "stk": (NCHIP, r, c)}[kind]


def _cols(j, c):
    return pl.ds(pl.multiple_of(j * c, 128), c)


def _whole_shard(kind, ref, j, r, c):
    if kind == "row":
        return ref.at[:, pl.ds(j * r, r), :]
    if kind == "col":
        return ref.at[:, :, _cols(j, c)]
    return ref.at[pl.ds(j, 1)]


def _whole_rows(kind, ref, j, row, n, r, c):
    if kind == "row":
        return ref.at[:, pl.ds(j * r + row, n), :]
    if kind == "col":
        return ref.at[:, pl.ds(row, n), _cols(j, c)]
    return ref.at[pl.ds(j, 1), pl.ds(row, n), :]


class _GatherJob(_Job):
    has_mid = True
    NCP = 13

    def __init__(self, names, shards, sink):
        self.mats = [MATS[n] for n in names]
        self.srcs = [shards[n] for n in names]
        self.news = [jax.ShapeDtypeStruct(_whole_shape(kind, g, r, c), BF16) for _, kind, g, r, c, _, _ in self.mats]
        n = len(names)
        self.scratch = [pltpu.SemaphoreType.DMA((self.NCP * n,)), pltpu.SemaphoreType.DMA((self.NCP * n,))]
        self.names, self.sink = names, sink

    def _copies(self, srcs, news, sems):
        send, recv = sems
        x, y, c = _pos()
        me, jx, jy, jd = 2 * x + y, 2 * (1 - x) + y, 2 * x + (1 - y), 2 * (1 - x) + (1 - y)
        nbx, nby, sib = (1 - x, y, c), (x, 1 - y, c), (x, y, 1 - c)
        cps = []
        for w, (_, kind, g, r, cc, _, _) in enumerate(self.mats):
            hr, qr = r // 2, r // 4
            at = lambda j, h, q, n: _whole_rows(kind, news[w], j, h * hr + q * qr, n, r, cc)
            mine = lambda q: srcs[w].at[:, pl.ds(c * hr + q * qr, qr), :]
            cp = lambda k, s, d, dev: _remote(s, d, send.at[self.NCP * w + k], recv.at[self.NCP * w + k], dev)
            cps.append([
                cp(0, mine(0), at(me, c, 0, qr), nbx), cp(1, mine(1), at(me, c, 1, qr), nbx),
                cp(2, mine(1), at(me, c, 1, qr), nby), cp(3, mine(0), at(me, c, 0, qr), nby),
                cp(4, at(jx, c, 0, qr), at(jx, c, 0, qr), nby), cp(5, at(jy, c, 1, qr), at(jy, c, 1, qr), nbx),
                cp(6, at(jx, c, 0, qr), at(jx, c, 0, qr), sib), cp(7, at(jx, c, 1, qr), at(jx, c, 1, qr), sib),
                cp(8, at(jy, c, 1, qr), at(jy, c, 1, qr), sib), cp(9, at(jy, c, 0, qr), at(jy, c, 0, qr), sib),
                cp(10, at(jd, c, 0, qr), at(jd, c, 0, qr), sib), cp(11, at(jd, c, 1, qr), at(jd, c, 1, qr), sib),
                cp(12, srcs[w], _whole_shard(kind, news[w], me, r, cc), sib)])
        return cps

    def _pass_on(self, srcs, news, sems, pairs):
        cps = self._copies(srcs, news, sems)
        for w in range(len(self.mats)):
            for arrived, onward in pairs:
                cps[w][arrived].wait_recv()
                for k in onward:
                    cps[w][k].start()

    def start(self, srcs, dsts, news, sems):
        cps = self._copies(srcs, news, sems)
        for w in range(len(self.mats)):
            for k in (0, 1, 2, 3, 12):
                cps[w][k].start()

    def mid(self, srcs, dsts, news, sems):
        self._pass_on(srcs, news, sems, ((0, (4, 6)), (2, (5, 8))))

    def late(self, srcs, dsts, news, sems):
        self._pass_on(srcs, news, sems, ((1, (7,)), (3, (9,)), (4, (10,)), (5, (11,))))

    def finish(self, srcs, dsts, news, sems):
        cps = self._copies(srcs, news, sems)
        for w in range(len(self.mats)):
            for k in (6, 7, 8, 9, 10, 11, 12):
                cps[w][k].wait_recv()
            for k in range(self.NCP):
                cps[w][k].wait_send()

    def done(self, dsts, news):
        for n, a in zip(self.names, news):
            self.sink[n] = a


class _SwapJob(_Job):
    def __init__(self, build, ncopies, *, srcs=(), dsts=(), news=(), done=None):
        self.build, self.srcs, self.dsts, self.news, self._done = build, list(srcs), list(dsts), list(news), done
        self.scratch = [pltpu.SemaphoreType.DMA((ncopies,)), pltpu.SemaphoreType.DMA((ncopies,))]

    def start(self, srcs, dsts, news, sems):
        for cp in self.build(srcs, dsts, news, *sems):
            cp.start()

    def finish(self, srcs, dsts, news, sems):
        for cp in self.build(srcs, dsts, news, *sems):
            cp.wait()

    def done(self, dsts, news):
        if self._done is not None:
            self._done(dsts, news)


def _half_of_whole(kind, ref, h, r, c):
    if kind == "row":
        return ref.at[:, :, pl.ds(pl.multiple_of(h * (c // 2), 128), c // 2)]
    return ref.at[:, pl.ds(h * (r // 2), r // 2), :]


def _half_shape(kind, g, r, c):
    return {"row": (g, NCHIP * r, c // 2), "col": (g, r // 2, NCHIP * c), "stk": (NCHIP, r // 2, c)}[kind]


def _sub_shape(kind, r, c):
    return {"row": (1, r // 2, c // 2), "col": (1, r // 4, c), "stk": (1, r // 4, c)}[kind]


def _sub_of_half(kind, ref, j, p, r, c):
    sr = _sub_shape(kind, r, c)[1]
    if kind == "row":
        return ref.at[:, pl.ds(j * r + p * sr, sr), :]
    if kind == "col":
        return ref.at[:, pl.ds(p * sr, sr), _cols(j, c)]
    return ref.at[pl.ds(j, 1), pl.ds(p * sr, sr), :]


def _sub_tile(sr):
    return 256 if sr % 256 == 0 else sr


def _half_of_shard(kind, ref, h, r, c):
    if kind == "row":
        return ref.at[:, :, pl.ds(pl.multiple_of(h * (c // 2), 128), c // 2)]
    return ref.at[:, pl.ds(h * (r // 2), r // 2), :]


def _pair_sum(pack, core, mine, got, whole=True):
    name, kind, g, r, c, tr, _ = pack
    hs = _half_shape(kind, g, r, c)
    nb = hs[1] // tr

    def body(core_ref, a_ref, b_ref, o_ref):
        o_ref[...] = (a_ref[...].astype(F32) + b_ref[...].astype(F32)).astype(BF16)

    blk = (1, tr, hs[2])
    same = lambda gi, i, core_ref: (gi, i, 0)
    if not whole:
        a_map = same
    elif kind == "row":
        a_map = lambda gi, i, core_ref: (gi, i, core_ref[0])
    else:
        a_map = lambda gi, i, core_ref: (gi, core_ref[0] * nb + i, 0)
    return pl.pallas_call(
        body, name="pair_sum_" + name, out_shape=jax.ShapeDtypeStruct(hs, BF16),
        grid_spec=pltpu.PrefetchScalarGridSpec(
            num_scalar_prefetch=1, grid=(hs[0], nb),
            in_specs=[pl.BlockSpec(blk, a_map), pl.BlockSpec(blk, same)], out_specs=pl.BlockSpec(blk, same)),
        compiler_params=_cp(("parallel", "parallel")),
    )(core, mine, got)


def _sub_sum(pack, idx, half, got, first_slot, out_dtype, *, name):
    _, kind, g, r, c, _, _ = pack
    _, sr, sc = _sub_shape(kind, r, c)
    tr = _sub_tile(sr)
    nb = sr // tr

    def body(idx_ref, a_ref, b_ref, o_ref):
        o_ref[0, 0] = (a_ref[0].astype(F32) + b_ref[0, 0].astype(F32)).astype(out_dtype)

    if kind == "row":
        a_map = lambda q, i, ix: (0, ix[2 * q] * (r // tr) + ix[2 * q + 1] * nb + i, 0)
    elif kind == "col":
        a_map = lambda q, i, ix: (0, ix[2 * q + 1] * nb + i, ix[2 * q])
    else:
        a_map = lambda q, i, ix: (ix[2 * q], ix[2 * q + 1] * nb + i, 0)
    return pl.pallas_call(
        body, name=name, out_shape=jax.ShapeDtypeStruct((2, 1, sr, sc), out_dtype),
        grid_spec=pltpu.PrefetchScalarGridSpec(
            num_scalar_prefetch=1, grid=(2, nb),
            in_specs=[pl.BlockSpec((1, tr, sc), a_map),
                      pl.BlockSpec((1, 1, tr, sc), lambda q, i, ix: (first_slot + 2 * q, 0, i, 0))],
            out_specs=pl.BlockSpec((1, 1, tr, sc), lambda q, i, ix: (q, 0, i, 0))),
        compiler_params=_cp(("parallel", "parallel")),
    )(idx, half, got)


def _shard_sum(pack, core, keep, got):
    name, kind, g, r, c, _, _ = pack
    _, sr, sc = _sub_shape(kind, r, c)
    tr = _sub_tile(sr)
    nb = sr // tr

    def body(core_ref, a_ref, b_ref, o_ref):
        o_ref[0] = a_ref[0, 0] + b_ref[0, 0].astype(F32)

    blk = pl.BlockSpec((1, 1, tr, sc), lambda p, i, cr: (p, 0, i, 0))
    if kind == "row":
        o_map = lambda p, i, cr: (0, p * nb + i, cr[0])
    else:
        o_map = lambda p, i, cr: (0, cr[0] * 2 * nb + p * nb + i, 0)
    return pl.pallas_call(
        body, name="shard_sum_" + name, out_shape=jax.ShapeDtypeStruct((g, r, c), F32),
        grid_spec=pltpu.PrefetchScalarGridSpec(
            num_scalar_prefetch=1, grid=(2, nb), in_specs=[blk, blk], out_specs=pl.BlockSpec((1, tr, sc), o_map)),
        compiler_params=_cp(("parallel", "parallel")),
    )(core, keep, got)


class _Plan:
    def __init__(self, shards, table):
        self.shards, self.table = shards, table
        self.whole, self.grad, self.got_a, self.half, self.gshard = {}, {}, {}, {}, {}
        self.got_b1, self.kept, self.pass_on, self.got_b2 = {}, {}, {}, {}
        x, y, c = _pos()
        me, jx, jy = 2 * x + y, 2 * (1 - x) + y, 2 * x + (1 - y)
        self.core = c
        self.core1 = c.reshape(1).astype(jnp.int32)
        self.idx_keep = jnp.stack([me, 0 * me, me, 0 * me + 1]).astype(jnp.int32)
        self.idx_pass = jnp.stack([jy, 0 * me, jx, 0 * me + 1]).astype(jnp.int32)
        self._w_in = None
        self.send, self.keep = {}, {}

    def w(self, n):
        if n != "w_in":
            return self.whole[n][0]
        if self._w_in is None:
            self._w_in = _slabs_to_kernel_cols(self.whole[n], name="relayout_w_in", jobs=self.jobs("relayout_w_in"))
        return self._w_in

    def g(self, n, a):
        self.grad[n] = a[None]

    def g_half(self, n, which, a):
        (self.send if which == "send" else self.keep)[n] = _kernel_cols_to_slabs(a, name="relayout_d_in_" + which)

    def jobs(self, tag):
        out = []
        for spec in self.table.get(tag, ()):
            out += getattr(self, "_" + spec[0])(*spec[1:])
        return out

    def run(self, name, jobs):
        if jobs:
            _call(lambda: None, jobs=jobs, name=name, out_shape=[], in_specs=[], out_specs=[])()

    def _gather(self, names):
        return [_GatherJob(names, self.shards, self.whole)]

    def _rs_a(self, names):
        mats = [MATS[n] for n in names]

        def build(srcs, dsts, news, send, recv):
            x, y, c = _pos()
            return [_remote(srcs[i] if names[i] in self.send else _half_of_whole(kind, srcs[i], 1 - c, r, cc), news[i],
                            send.at[i], recv.at[i], (x, y, 1 - c))
                    for i, (_, kind, g, r, cc, _, _) in enumerate(mats)]

        def done(dsts, news):
            self.got_a.update(zip(names, news))

        return [_SwapJob(build, len(names), srcs=[self.send.get(n, self.grad.get(n)) for n in names], done=done,
                         news=[jax.ShapeDtypeStruct(_half_shape(kind, g, r, c), BF16) for _, kind, g, r, c, _, _ in mats])]

    def _rs_b1(self, names):
        mats = [MATS[n] for n in names]
        for n in names:
            if n in self.keep:
                self.half[n] = _pair_sum(MATS[n], self.core1, self.keep[n], self.got_a[n], whole=False)
            else:
                self.half[n] = _pair_sum(MATS[n], self.core1, self.grad[n], self.got_a[n])

        def build(srcs, dsts, news, send, recv):
            x, y, c = _pos()
            jx, jy, jd = 2 * (1 - x) + y, 2 * x + (1 - y), 2 * (1 - x) + (1 - y)
            nbx, nby = (1 - x, y, c), (x, 1 - y, c)
            cps = []
            for i, (_, kind, g, r, cc, _, _) in enumerate(mats):
                sub = lambda j, p: _sub_of_half(kind, srcs[i], j, p, r, cc)
                for k, (j, p, dev) in enumerate(((jx, 0, nbx), (jd, 0, nbx), (jy, 1, nby), (jd, 1, nby))):
                    cps.append(_remote(sub(j, p), news[i].at[k], send.at[4 * i + k], recv.at[4 * i + k], dev))
            return cps

        def done(dsts, news):
            self.got_b1.update(zip(names, news))

        return [_SwapJob(build, 4 * len(names), srcs=[self.half[n] for n in names], done=done,
                         news=[jax.ShapeDtypeStruct((4,) + _sub_shape(kind, r, c), BF16) for _, kind, g, r, c, _, _ in mats])]

    def _rs_b2(self, names):
        mats = [MATS[n] for n in names]
        for n in names:
            self.kept[n] = _sub_sum(MATS[n], self.idx_keep, self.half[n], self.got_b1[n], 0, F32, name="sum_keep_" + n)
            self.pass_on[n] = _sub_sum(MATS[n], self.idx_pass, self.half[n], self.got_b1[n], 1, BF16, name="sum_pass_" + n)

        def build(srcs, dsts, news, send, recv):
            x, y, c = _pos()
            cps = []
            for i in range(len(mats)):
                cps.append(_remote(srcs[i].at[0], news[i].at[0], send.at[2 * i], recv.at[2 * i], (x, 1 - y, c)))
                cps.append(_remote(srcs[i].at[1], news[i].at[1], send.at[2 * i + 1], recv.at[2 * i + 1], (1 - x, y, c)))
            return cps

        def done(dsts, news):
            self.got_b2.update(zip(names, news))

        return [_SwapJob(build, 2 * len(names), srcs=[self.pass_on[n] for n in names], done=done,
                         news=[jax.ShapeDtypeStruct((2,) + _sub_shape(kind, r, c), BF16) for _, kind, g, r, c, _, _ in mats])]

    def _rs_c(self, names):
        mats = [MATS[n] for n in names]
        parts = [_shard_sum(MATS[n], self.core1, self.kept[n], self.got_b2[n]) for n in names]

        def build(srcs, dsts, news, send, recv):
            x, y, c = _pos()
            cps = []
            for i, (_, kind, g, r, cc, _, _) in enumerate(mats):
                mine = _half_of_shard(kind, dsts[i], c, r, cc)
                cps.append(_remote(mine, mine, send.at[i], recv.at[i], (x, y, 1 - c)))
            return cps

        def done(dsts, news):
            self.gshard.update(zip(names, dsts))

        return [_SwapJob(build, len(names), dsts=parts, done=done)]

    def finish(self, n):
        if n not in self.got_a:
            self.run("rs_a_" + n, self._rs_a((n,)))
        if n not in self.got_b1:
            self.run("rs_b1_" + n, self._rs_b1((n,)))
        if n not in self.got_b2:
            self.run("rs_b2_" + n, self._rs_b2((n,)))
        if n not in self.gshard:
            self.run("rs_c_" + n, self._rs_c((n,)))
        return self.gshard[n]


TABLE = {
    "gather_w_in": (("gather", ("w_in",)),),
    "relayout_w_in": (("gather", ("w_gate",)),),
    "mm_in": (("gather", ("w_up",)),),
    "attn_fwd": (("gather", ("w_attn_br", "w_ssd_br")),),
    "ssd_fwd": (("gather", ("w_o",)),),
    "swiglu_fwd": (("gather", ("w_down",)),),
    "mm_down": (("gather", ("w_ple_gate", "w_ple_proj")),),
    "mm_de": (("rs_a", ("w_ple_proj", "w_ple_gate")),),
    "mm_d_down": (("rs_b1", ("w_ple_proj", "w_ple_gate")),),
    "swiglu_bwd": (("rs_a", ("w_down",)), ("rs_b2", ("w_ple_proj", "w_ple_gate"))),
    "mm_d_gate": (("rs_b1", ("w_down",)),),
    "mm_d_up": (("rs_b2", ("w_down",)), ("rs_c", ("w_ple_proj", "w_ple_gate")), ("rs_a", ("w_gate",))),
    "mm_df_gate": (("rs_b1", ("w_gate",)), ("rs_a", ("w_up",)), ("rs_c", ("w_down",))),
    "mm_df_up": (("rs_b2", ("w_gate",)),),
    "norm_ffn_bwd": (("rs_c", ("w_gate",)),),
    "mm_dmerged": (("rs_a", ("w_o",)),),
    "mm_dyn": (("rs_a", ("w_attn_br", "w_ssd_br")),),
    "attn_bwd": (("rs_b1", ("w_up",)),),
    "gated_norm_bwd": (("rs_b2", ("w_up",)),),
    "ssd_bwd": (("rs_b1", ("w_o", "w_attn_br", "w_ssd_br")), ("rs_c", ("w_up",))),
    "conv_bwd": (("rs_b2", ("w_o", "w_attn_br", "w_ssd_br")),),
    "mm_d_in_send": (("rs_c", ("w_o", "w_attn_br", "w_ssd_br")),),
    "mm_d_in_keep": (("rs_a", ("w_in",)),),
    "mm_du": (("rs_b1", ("w_in",)),),
    "norm_mix_bwd": (("rs_b2", ("w_in",)),),
}


NDEV = 8


def _allreduce_small(v, *, name):
    rows = v.shape[0]

    def body(v_ref, o_ref, slots, send, recv):
        x, y, c = _pos()
        me = 4 * x + 2 * y + c
        slots[me] = v_ref[...]
        cps = []
        for k in range(1, NDEV):
            peer = (_flip(x, k & 4), _flip(y, k & 2), _flip(c, k & 1))
            cp = _remote(v_ref, slots.at[me], send.at[k - 1], recv.at[k - 1], peer)
            cp.start()
            cps.append(cp)
        for cp in cps:
            cp.wait()
        acc = slots[0]
        for s in range(1, NDEV):
            acc = acc + slots[s]
        o_ref[...] = acc

    return pl.pallas_call(
        body, name=name, out_shape=jax.ShapeDtypeStruct((rows, 128), F32),
        in_specs=[pl.BlockSpec(memory_space=pltpu.VMEM)], out_specs=pl.BlockSpec(memory_space=pltpu.VMEM),
        scratch_shapes=[pltpu.VMEM((NDEV, rows, 128), F32), pltpu.SemaphoreType.DMA((NDEV - 1,)),
                        pltpu.SemaphoreType.DMA((NDEV - 1,))],
    )(v)


def _adamw(w, g, m, v, *, name, tr=None, tc=None, jobs=()):
    r, c = w.shape
    tr = r if tr is None else tr
    c1 = 1.0 / (1.0 - B1 ** STEP)
    c2 = 1.0 / (1.0 - B2 ** STEP)

    def body(w_ref, g_ref, m_ref, v_ref, d_ref, mo_ref, vo_ref):
        gv = g_ref[...]
        mn = B1 * m_ref[...] + (1.0 - B1) * gv
        vn = B2 * v_ref[...] + (1.0 - B2) * (gv * gv)
        mo_ref[...] = mn
        vo_ref[...] = vn
        d_ref[...] = -LR * ((mn * c1) / (jnp.sqrt(vn * c2) + AEPS) + WD * w_ref[...])

    if tc is None:
        blk, grid = pl.BlockSpec((tr, c), lambda i: (i, 0)), (r // tr,)
    else:
        blk, grid = pl.BlockSpec((r, tc), lambda i: (0, i)), (c // tc,)
    o = jax.ShapeDtypeStruct((r, c), F32)
    return _call(
        body, jobs=jobs, name=name, out_shape=(o, o, o), grid=grid, in_specs=[blk] * 4, out_specs=(blk, blk, blk),
        compiler_params=_cp(("parallel",)),
    )(w, g, m, v)


WEIGHTS = ("g_mix", "w_in", "conv_w", "conv_b", "dt_bias", "a_log", "d_skip", "g_ssd", "sinks", "w_attn_br", "w_ssd_br",
           "w_o", "g_ffn", "w_gate", "w_up", "w_down", "g_ple", "w_ple_gate", "w_ple_proj", "g_final")
BIG = {
    "w_gate": 256, "w_up": 256, "w_down": 128, "w_ssd_br": 128, "w_o": 128, "w_ple_gate": 128, "w_attn_br": 256,
    "w_ple_proj": 256, "w_in": None,
}
SMALL = tuple(n for n in WEIGHTS if n not in BIG)


def _pack_small(parts):
    rows = []
    for a in parts:
        a = a.reshape(-1)
        rows.append(jnp.pad(a, (0, -a.shape[0] % 128)).reshape(-1, 128))
    out = jnp.concatenate(rows, axis=0)
    return jnp.pad(out, ((0, -out.shape[0] % 8), (0, 0)))


def _unpack_small(packed, shapes):
    out, r = [], 0
    for s in shapes:
        n = int(np.prod(s))
        nr = -(-n // 128)
        out.append(packed[r:r + nr].reshape(-1)[:n].reshape(s))
        r += nr
    return out


def kernel(x, p, positions, g_mix, w_in, conv_w, conv_b, dt_bias, a_log, d_skip, g_ssd, sinks, w_attn_br, w_ssd_br, w_o, g_ffn, w_gate, w_up, w_down, g_ple, w_ple_gate, w_ple_proj, g_final, loss_target, m_g_mix, m_w_in, m_conv_w, m_conv_b, m_dt_bias, m_a_log, m_d_skip, m_g_ssd, m_sinks, m_w_attn_br, m_w_ssd_br, m_w_o, m_g_ffn, m_w_gate, m_w_up, m_w_down, m_g_ple, m_w_ple_gate, m_w_ple_proj, m_g_final, v_g_mix, v_w_in, v_conv_w, v_conv_b, v_dt_bias, v_a_log, v_d_skip, v_g_ssd, v_sinks, v_w_attn_br, v_w_ssd_br, v_w_o, v_g_ffn, v_w_gate, v_w_up, v_w_down, v_g_ple, v_w_ple_gate, v_w_ple_proj, v_g_final):
    w = dict(zip(WEIGHTS, (g_mix, w_in, conv_w, conv_b, dt_bias, a_log, d_skip, g_ssd, sinks, w_attn_br, w_ssd_br, w_o,
                           g_ffn, w_gate, w_up, w_down, g_ple, w_ple_gate, w_ple_proj, g_final)))
    m = dict(zip(WEIGHTS, (m_g_mix, m_w_in, m_conv_w, m_conv_b, m_dt_bias, m_a_log, m_d_skip, m_g_ssd, m_sinks, m_w_attn_br,
                           m_w_ssd_br, m_w_o, m_g_ffn, m_w_gate, m_w_up, m_w_down, m_g_ple, m_w_ple_gate, m_w_ple_proj,
                           m_g_final)))
    v = dict(zip(WEIGHTS, (v_g_mix, v_w_in, v_conv_w, v_conv_b, v_dt_bias, v_a_log, v_d_skip, v_g_ssd, v_sinks, v_w_attn_br,
                           v_w_ssd_br, v_w_o, v_g_ffn, v_w_gate, v_w_up, v_w_down, v_g_ple, v_w_ple_gate, v_w_ple_proj,
                           v_g_final)))
    xi, yi, ci = _pos()
    chip = 2 * xi + yi
    t = x.shape[1]
    cshard = CONV // NCHIP

    shards = {n: w[n].astype(BF16) for n in MATS}
    shards["w_in"] = jnp.pad(shards["w_in"], ((0, 0), (0, 0), (0, SLAB_PAD - SLAB)))
    plan = _Plan(shards, TABLE)
    plan.run("gather_w_in", plan.jobs("gather_w_in"))
    placed = lax.dynamic_update_slice(jnp.zeros((CW, CONV), F32), w["conv_w"][0], (0, chip * cshard))
    conv_whole = _allreduce_small(jnp.where(ci == 0, placed, 0.0).reshape(-1, 128), name="gather_conv_w").reshape(CW, CONV)

    small = {n: w[n] for n in ("g_mix", "conv_b", "dt_bias", "a_log", "d_skip", "g_ssd", "sinks", "g_ffn", "g_ple", "g_final")}
    small["conv_w"] = conv_whole
    loss8, grad_x, gs = _local_step(x[0], p[0, 0], positions, loss_target[0], small, plan)

    order = ("g_mix", "conv_b", "dt_bias", "a_log", "d_skip", "g_ssd", "sinks", "g_ffn", "g_ple", "g_final", "conv_w")
    summed = _allreduce_small(_pack_small([loss8[0, :1]] + [gs[n] for n in order]), name="sum_small")
    parts = _unpack_small(summed, [(1,)] + [w[n].shape for n in order[:-1]] + [(CW, CONV)])
    loss = parts[0][0]
    grad = dict(zip(order, parts[1:]))
    grad["conv_w"] = lax.dynamic_slice(grad["conv_w"], (0, chip * cshard), (CW, cshard))[None]

    delta, new_m, new_v = {}, {}, {}
    for n, tr in BIG.items():
        grad[n] = plan.finish(n)[:, :, :w[n].shape[2]]
        if n == "w_in":
            d_, m_, v_ = _adamw(w[n][0].T, grad[n][0].T, m[n][0].T, v[n][0].T, tc=128, name="adamw_" + n)
            d_, m_, v_ = d_.T, m_.T, v_.T
        else:
            d_, m_, v_ = _adamw(w[n][0], grad[n][0], m[n][0], v[n][0], tr=tr, name="adamw_" + n)
        delta[n], new_m[n], new_v[n] = d_[None], m_[None], v_[None]
    shapes = [w[n].shape for n in SMALL]
    d_, m_, v_ = _adamw(_pack_small([w[n] for n in SMALL]), _pack_small([grad[n] for n in SMALL]),
                        _pack_small([m[n] for n in SMALL]), _pack_small([v[n] for n in SMALL]), tr=None, name="adamw_small")
    for n, a, b, c_ in zip(SMALL, _unpack_small(d_, shapes), _unpack_small(m_, shapes), _unpack_small(v_, shapes)):
        delta[n], new_m[n], new_v[n] = a, b, c_

    return (loss, grad_x[None], *[grad[n] for n in WEIGHTS], *[delta[n] for n in WEIGHTS],
            *[new_m[n] for n in WEIGHTS], *[new_v[n] for n in WEIGHTS])
```

```python
import functools

import jax
import jax.numpy as jnp
import numpy as np
from jax import lax
from jax.experimental import pallas as pl
from jax.experimental.pallas import tpu as pltpu

F32 = jnp.float32
BF16 = jnp.bfloat16
MESH = pl.DeviceIdType.MESH

D = 2048
HD = 64
NQH = 16
NKV = 4
QD = NQH * HD
KVD = NKV * HD
DI = 2048
NH = 32
NG = 4
NS = 128
CW = 4
L = 128
CONV = DI + 2 * NG * NS
FFN = 5632
PLE = 256
IN_DIM = QD + 2 * KVD + DI + CONV + NH + 2 * D
EPS = 1e-6
SSM_EPS = 1e-5
ROPE_THETA = 10000.0
LR, B1, B2, AEPS, WD, STEP = 0.001, 0.9, 0.999, 1e-08, 0.01, 10

O_GA, O_GS, O_Z, O_XBC, O_Q, O_K, O_V, O_DT = 0, 2048, 4096, 6144, 9216, 10240, 10496, 10752
DT_PAD = 512
NP = O_DT + DT_PAD
R_Q, R_K, R_V, R_Z, R_XBC, R_DT, R_GA, R_GS = 0, 1024, 1280, 1536, 3584, 6656, 6688, 8736

NCHIP = 4
VMEM_LIMIT = 52 * 1024 * 1024
NEG = -1e30


def _cp(sem=None):
    return pltpu.CompilerParams(dimension_semantics=sem, vmem_limit_bytes=VMEM_LIMIT)


def _dot(a, b):
    return lax.dot_general(a, b, (((1,), (0,)), ((), ())), preferred_element_type=F32)


def _dot_nt(a, b):
    return lax.dot_general(a, b, (((1,), (1,)), ((), ())), preferred_element_type=F32)


def _dot_tn(a, b):
    return lax.dot_general(a, b, (((0,), (0,)), ((), ())), preferred_element_type=F32)


def _sigmoid(x):
    return 1.0 / (1.0 + jnp.exp(-x))


def _bf16_dot(dot, da, db):
    @jax.custom_vjp
    def f(a, b):
        return dot(a.astype(BF16), b.astype(BF16))

    def fwd(a, b):
        return f(a, b), (a.astype(BF16), b.astype(BF16))

    def bwd(res, g):
        a, b = res
        g = g.astype(BF16)
        return da(g, a, b), db(g, a, b)

    f.defvjp(fwd, bwd)
    return f


_bdot = _bf16_dot(_dot, lambda g, a, b: _dot_nt(g, b), lambda g, a, b: _dot_tn(a, g))
_bdot_nt = _bf16_dot(_dot_nt, lambda g, a, b: _dot(g, b), lambda g, a, b: _dot_tn(g, a))
_bdot_tn = _bf16_dot(_dot_tn, lambda g, a, b: _dot_nt(b, g), lambda g, a, b: _dot(a, g))


ANY = pl.BlockSpec(memory_space=pl.ANY)


class _Job:
    srcs, dsts, news, scratch = (), (), (), ()
    has_mid = False

    def start(self, srcs, dsts, news, sems):
        raise NotImplementedError

    def mid(self, srcs, dsts, news, sems):
        pass

    def late(self, srcs, dsts, news, sems):
        pass

    def finish(self, srcs, dsts, news, sems):
        raise NotImplementedError

    def done(self, dsts, news):
        pass


def _call(body, *, jobs=(), name, out_shape, in_specs, out_specs, grid=(), scratch_shapes=(), compiler_params=None,
          aliases=None):
    jobs = [j for j in jobs if j is not None]
    aliases = dict(aliases or {})
    if not jobs:
        return pl.pallas_call(body, name=name, out_shape=out_shape, in_specs=in_specs, out_specs=out_specs, grid=grid,
                              scratch_shapes=scratch_shapes, compiler_params=compiler_params,
                              input_output_aliases=aliases)
    single = not isinstance(out_shape, (tuple, list))
    outs = [out_shape] if single else list(out_shape)
    ospecs = [out_specs] if single else list(out_specs)
    n_in, n_out, n_scr = len(in_specs), len(outs), len(scratch_shapes)
    srcs = [a for j in jobs for a in j.srcs]
    dsts = [a for j in jobs for a in j.dsts]
    news = [a for j in jobs for a in j.news]
    sems = [a for j in jobs for a in j.scratch]

    def wrapped(*refs):
        pos = n_in + len(srcs) + len(dsts)
        ins, jsrc = refs[:n_in], refs[n_in:n_in + len(srcs)]
        o_refs = refs[pos:pos + n_out]
        pos += n_out
        jdst, jnew = refs[pos:pos + len(dsts)], refs[pos + len(dsts):pos + len(dsts) + len(news)]
        pos += len(dsts) + len(news)
        scr, jsem = refs[pos:pos + n_scr], refs[pos + n_scr:]

        def run(which):
            a = b = c = d = 0
            for j in jobs:
                getattr(j, which)(jsrc[a:a + len(j.srcs)], jdst[b:b + len(j.dsts)], jnew[c:c + len(j.news)],
                                  jsem[d:d + len(j.scratch)])
                a, b, c, d = a + len(j.srcs), b + len(j.dsts), c + len(j.news), d + len(j.scratch)

        if not grid:
            run("start")
            run("mid")
            run("late")
            body(*ins, *o_refs, *scr)
            run("finish")
            return
        step = functools.reduce(lambda acc, a: acc * grid[a] + pl.program_id(a), range(len(grid)), 0)
        steps = int(np.prod(grid))
        pl.when(step == 0)(lambda: run("start"))
        if any(j.has_mid for j in jobs):
            pl.when(step == steps // 3)(lambda: run("mid"))
            pl.when(step == (2 * steps) // 3)(lambda: run("late"))
        body(*ins, *o_refs, *scr)
        pl.when(step == steps - 1)(lambda: run("finish"))

    call = pl.pallas_call(
        wrapped, name=name,
        out_shape=outs + [jax.ShapeDtypeStruct(a.shape, a.dtype) for a in dsts] + news,
        in_specs=list(in_specs) + [ANY] * (len(srcs) + len(dsts)),
        out_specs=ospecs + [ANY] * (len(dsts) + len(news)),
        grid=grid, scratch_shapes=list(scratch_shapes) + sems,
        input_output_aliases={**aliases, **{n_in + len(srcs) + i: n_out + i for i in range(len(dsts))}},
        compiler_params=_cp(("arbitrary",) * len(grid) if grid else None))

    def run_call(*args):
        res = call(*args, *srcs, *dsts)
        b, c = n_out, n_out + len(dsts)
        for j in jobs:
            j.done(res[b:b + len(j.dsts)], res[c:c + len(j.news)])
            b, c = b + len(j.dsts), c + len(j.news)
        return res[0] if single else tuple(res[:n_out])

    return run_call


def _matmul(a, b, *, ta=False, tb=False, out_dtype=F32, add=None, tm, tn, tk, name, jobs=()):
    k, m = a.shape if ta else a.shape[::-1]
    n = b.shape[0] if tb else b.shape[1]
    assert (b.shape[1] if tb else b.shape[0]) == k and not (ta and tb)
    assert m % tm == 0 and n % tn == 0 and k % tk == 0, (name, a.shape, b.shape)
    nk = k // tk
    has_add = add is not None

    def body(*refs):
        a_ref, b_ref = refs[0], refs[1]
        add_ref = refs[2] if has_add else None
        o_ref = refs[3] if has_add else refs[2]
        av = a_ref[...].astype(BF16)
        bv = b_ref[...].astype(BF16)
        part = _dot_tn(av, bv) if ta else _dot_nt(av, bv) if tb else _dot(av, bv)

        def finish(r):
            if has_add:
                r = r + add_ref[...]
            o_ref[...] = r.astype(out_dtype)

        if nk == 1:
            finish(part)
        elif out_dtype == F32:
            kk = pl.program_id(2)
            pl.when(kk == 0)(lambda: finish(part))

            @pl.when(kk > 0)
            def _():
                o_ref[...] += part
        else:
            acc_ref = refs[-1]
            kk = pl.program_id(2)

            @pl.when(kk == 0)
            def _():
                acc_ref[...] = part

            @pl.when(kk > 0)
            def _():
                acc_ref[...] += part

            @pl.when(kk == nk - 1)
            def _():
                finish(acc_ref[...])

    in_specs = [pl.BlockSpec((tk, tm), lambda i, j, kk: (kk, i)) if ta else pl.BlockSpec((tm, tk), lambda i, j, kk: (i, kk)),
                pl.BlockSpec((tn, tk), lambda i, j, kk: (j, kk)) if tb
                else pl.BlockSpec((tk, tn), lambda i, j, kk: (kk, j))]
    args = [a, b]
    if has_add:
        in_specs.append(pl.BlockSpec((tm, tn), lambda i, j, kk: (i, j)))
        args.append(add)
    return _call(
        body, jobs=jobs, name=name,
        out_shape=jax.ShapeDtypeStruct((m, n), out_dtype),
        grid=(m // tm, n // tn, nk),
        in_specs=in_specs,
        out_specs=pl.BlockSpec((tm, tn), lambda i, j, kk: (i, j)),
        scratch_shapes=[pltpu.VMEM((tm, tn), F32)] if nk > 1 and out_dtype != F32 else [],
        compiler_params=_cp(("parallel", "parallel", "arbitrary")),
    )(*args)


ROWS = 256


def _rmsnorm_fwd(x, g, *, name):
    t, d = x.shape

    def body(x_ref, g_ref, o_ref):
        xv = x_ref[...]
        r = lax.rsqrt(jnp.mean(xv * xv, axis=-1, keepdims=True) + EPS)
        o_ref[...] = (xv * r * g_ref[...]).astype(BF16)

    return pl.pallas_call(
        body, name=name, out_shape=jax.ShapeDtypeStruct((t, d), BF16), grid=(t // ROWS,),
        in_specs=[pl.BlockSpec((ROWS, d), lambda i: (i, 0)), pl.BlockSpec((1, d), lambda i: (0, 0))],
        out_specs=pl.BlockSpec((ROWS, d), lambda i: (i, 0)), compiler_params=_cp(("parallel",)),
    )(x, g)


def _rmsnorm_bwd(x, g, dy, dres, *, name, jobs=()):
    t, d = x.shape

    def body(x_ref, g_ref, dy_ref, dres_ref, dx_ref, dxb_ref, dg_ref):
        xv = x_ref[...]
        r = lax.rsqrt(jnp.mean(xv * xv, axis=-1, keepdims=True) + EPS)
        xh = xv * r
        dyv = dy_ref[...]
        dxh = dyv * g_ref[...]
        dx = r * (dxh - xh * jnp.mean(dxh * xh, axis=-1, keepdims=True))
        tot = dres_ref[...] + dx
        dx_ref[...] = tot
        dxb_ref[...] = tot.astype(BF16)

        @pl.when(pl.program_id(0) == 0)
        def _():
            dg_ref[...] = jnp.zeros_like(dg_ref)

        dg_ref[...] += jnp.broadcast_to(jnp.sum(dyv * xh, axis=0, keepdims=True), dg_ref.shape)

    row = pl.BlockSpec((ROWS, d), lambda i: (i, 0))
    return _call(
        body, jobs=jobs, name=name,
        out_shape=(jax.ShapeDtypeStruct((t, d), F32), jax.ShapeDtypeStruct((t, d), BF16),
                   jax.ShapeDtypeStruct((8, d), F32)),
        grid=(t // ROWS,),
        in_specs=[row, pl.BlockSpec((1, d), lambda i: (0, 0)), row, row],
        out_specs=(row, row, pl.BlockSpec((8, d), lambda i: (0, 0))),
        compiler_params=_cp(("arbitrary",)),
    )(x, g, dy, dres)


def _final(h2, pgl, pp, target, g_final, *, name):
    t, d = h2.shape

    def body(h2_ref, pgl_ref, pp_ref, tg_ref, g_ref, dh3_ref, dpgl_ref, dpp_ref, loss_ref, dg_ref):
        s = _sigmoid(pgl_ref[...])
        ppv = pp_ref[...]
        h3 = h2_ref[...] + s * ppv
        r = lax.rsqrt(jnp.mean(h3 * h3, axis=-1, keepdims=True) + EPS)
        xh = h3 * r
        gv = g_ref[...]
        err = xh * gv - tg_ref[...]
        dyv = err * (1.0 / d)
        dxh = dyv * gv
        dh3 = r * (dxh - xh * jnp.mean(dxh * xh, axis=-1, keepdims=True))
        dh3_ref[...] = dh3
        dpp_ref[...] = (dh3 * s).astype(BF16)
        dpgl_ref[...] = (dh3 * ppv * s * (1.0 - s)).astype(BF16)

        @pl.when(pl.program_id(0) == 0)
        def _():
            loss_ref[...] = jnp.zeros_like(loss_ref)
            dg_ref[...] = jnp.zeros_like(dg_ref)

        part = 0.5 * jnp.sum(jnp.mean(err * err, axis=-1, keepdims=True), axis=0, keepdims=True)
        loss_ref[...] += jnp.broadcast_to(part, loss_ref.shape)
        dg_ref[...] += jnp.broadcast_to(jnp.sum(dyv * xh, axis=0, keepdims=True), dg_ref.shape)

    row = pl.BlockSpec((ROWS, d), lambda i: (i, 0))
    return pl.pallas_call(
        body, name=name,
        out_shape=(jax.ShapeDtypeStruct((t, d), F32), jax.ShapeDtypeStruct((t, d), BF16),
                   jax.ShapeDtypeStruct((t, d), BF16), jax.ShapeDtypeStruct((8, 128), F32),
                   jax.ShapeDtypeStruct((8, d), F32)),
        grid=(t // ROWS,),
        in_specs=[row, row, row, row, pl.BlockSpec((1, d), lambda i: (0, 0))],
        out_specs=(row, row, row, pl.BlockSpec((8, 128), lambda i: (0, 0)), pl.BlockSpec((8, d), lambda i: (0, 0))),
        compiler_params=_cp(("arbitrary",)),
    )(h2, pgl, pp, target, g_final)


def _merge_fwd(proj, out_a, out_s, *, name):
    t = proj.shape[0]

    def body(ga_ref, gs_ref, a_ref, s_ref, o_ref):
        o_ref[...] = (_sigmoid(ga_ref[...]) * a_ref[...] + _sigmoid(gs_ref[...]) * s_ref[...]).astype(BF16)

    row = pl.BlockSpec((ROWS, D), lambda i: (i, 0))
    return pl.pallas_call(
        body, name=name, out_shape=jax.ShapeDtypeStruct((t, D), BF16), grid=(t // ROWS,),
        in_specs=[pl.BlockSpec((ROWS, D), lambda i: (i, O_GA // D)), pl.BlockSpec((ROWS, D), lambda i: (i, O_GS // D)),
                  row, row],
        out_specs=row, compiler_params=_cp(("parallel",)),
    )(proj, proj, out_a, out_s)


def _merge_bwd(proj, out_a, out_s, dmerged, *, name):
    t = proj.shape[0]
    assert O_GA == 0 and O_GS == D

    def body(ga_ref, gs_ref, a_ref, s_ref, dm_ref, da_ref, ds_ref, dp_ref):
        sa = _sigmoid(ga_ref[...])
        ss = _sigmoid(gs_ref[...])
        dm = dm_ref[...]
        da_ref[...] = (dm * sa).astype(BF16)
        ds_ref[...] = (dm * ss).astype(BF16)
        dp_ref[:, :D] = (dm * a_ref[...] * sa * (1.0 - sa)).astype(BF16)
        dp_ref[:, D:] = (dm * s_ref[...] * ss * (1.0 - ss)).astype(BF16)

    row = pl.BlockSpec((ROWS, D), lambda i: (i, 0))
    o = jax.ShapeDtypeStruct((t, D), BF16)
    return pl.pallas_call(
        body, name=name, out_shape=(o, o, jax.ShapeDtypeStruct((t, NP), BF16)), grid=(t // ROWS,),
        in_specs=[pl.BlockSpec((ROWS, D), lambda i: (i, O_GA // D)), pl.BlockSpec((ROWS, D), lambda i: (i, O_GS // D)),
                  row, row, row],
        out_specs=(row, row, pl.BlockSpec((ROWS, 2 * D), lambda i: (i, 0))), compiler_params=_cp(("parallel",)),
    )(proj, proj, out_a, out_s, dmerged)


def _swiglu_fwd(f, w_gate, w_up, *, name, tn=512, jobs=()):
    t, d = f.shape
    n = w_gate.shape[1]

    def body(f_ref, wg_ref, wu_ref, g_ref, u_ref, a_ref):
        fv = f_ref[...]
        g = _dot(fv, wg_ref[...])
        u = _dot(fv, wu_ref[...])
        g_ref[...] = g.astype(BF16)
        u_ref[...] = u.astype(BF16)
        a_ref[...] = (g * _sigmoid(g) * u).astype(BF16)

    col = pl.BlockSpec((t, tn), lambda j: (0, j))
    wcol = pl.BlockSpec((d, tn), lambda j: (0, j))
    return _call(
        body, jobs=jobs, name=name,
        out_shape=(jax.ShapeDtypeStruct((t, n), BF16), jax.ShapeDtypeStruct((t, n), BF16),
                   jax.ShapeDtypeStruct((t, n), BF16)),
        grid=(n // tn,),
        in_specs=[pl.BlockSpec((t, d), lambda j: (0, 0)), wcol, wcol],
        out_specs=(col, col, col), compiler_params=_cp(("parallel",)),
    )(f, w_gate, w_up)


def _swiglu_bwd(dh, w_down, gate, up, *, name, tn=512, jobs=()):
    t, d = dh.shape
    n = w_down.shape[0]

    def body(dh_ref, w_ref, g_ref, u_ref, dg_ref, du_ref):
        da = _dot_nt(dh_ref[...], w_ref[...])
        g = g_ref[...].astype(F32)
        s = _sigmoid(g)
        du_ref[...] = (da * g * s).astype(BF16)
        dg_ref[...] = (da * u_ref[...].astype(F32) * s * (1.0 + g * (1.0 - s))).astype(BF16)

    col = pl.BlockSpec((t, tn), lambda j: (0, j))
    o = jax.ShapeDtypeStruct((t, n), BF16)
    return _call(
        body, jobs=jobs, name=name, out_shape=(o, o), grid=(n // tn,),
        in_specs=[pl.BlockSpec((t, d), lambda j: (0, 0)), pl.BlockSpec((tn, d), lambda j: (j, 0)), col, col],
        out_specs=(col, col), compiler_params=_cp(("parallel",)),
    )(dh, w_down, gate, up)


def _gated_norm_fwd(y_pre, proj, g_ssd, *, name):
    t = y_pre.shape[0]

    def body(y_ref, z_ref, g_ref, o_ref):
        z = z_ref[...]
        v = y_ref[...] * z * _sigmoid(z)
        r = lax.rsqrt(jnp.mean(v * v, axis=-1, keepdims=True) + SSM_EPS)
        o_ref[...] = (v * r * g_ref[...]).astype(BF16)

    row = pl.BlockSpec((ROWS, DI), lambda i: (i, 0))
    return pl.pallas_call(
        body, name=name, out_shape=jax.ShapeDtypeStruct((t, DI), BF16), grid=(t // ROWS,),
        in_specs=[row, pl.BlockSpec((ROWS, DI), lambda i: (i, O_Z // DI)), pl.BlockSpec((1, DI), lambda i: (0, 0))],
        out_specs=row, compiler_params=_cp(("parallel",)),
    )(y_pre, proj, g_ssd)


def _gated_norm_bwd(y_pre, proj, g_ssd, dyn, dproj, *, name, jobs=()):
    t = y_pre.shape[0]

    def body(y_ref, z_ref, g_ref, dyn_ref, _, dy_ref, dz_ref, dg_ref):
        z = z_ref[...]
        s = _sigmoid(z)
        sz = z * s
        yv = y_ref[...]
        v = yv * sz
        r = lax.rsqrt(jnp.mean(v * v, axis=-1, keepdims=True) + SSM_EPS)
        vh = v * r
        dn = dyn_ref[...]
        dvh = dn * g_ref[...]
        dv = r * (dvh - vh * jnp.mean(dvh * vh, axis=-1, keepdims=True))
        dy_ref[...] = dv * sz
        dz_ref[...] = (dv * yv * s * (1.0 + z * (1.0 - s))).astype(BF16)

        @pl.when(pl.program_id(0) == 0)
        def _():
            dg_ref[...] = jnp.zeros_like(dg_ref)

        dg_ref[...] += jnp.broadcast_to(jnp.sum(dn * vh, axis=0, keepdims=True), dg_ref.shape)

    row = pl.BlockSpec((ROWS, DI), lambda i: (i, 0))
    return _call(
        body, jobs=jobs, name=name,
        out_shape=(jax.ShapeDtypeStruct((t, DI), F32), jax.ShapeDtypeStruct(dproj.shape, BF16),
                   jax.ShapeDtypeStruct((8, DI), F32)),
        grid=(t // ROWS,),
        in_specs=[row, pl.BlockSpec((ROWS, DI), lambda i: (i, O_Z // DI)), pl.BlockSpec((1, DI), lambda i: (0, 0)), row, ANY],
        out_specs=(row, pl.BlockSpec((ROWS, DI), lambda i: (i, O_Z // DI)), pl.BlockSpec((8, DI), lambda i: (0, 0))),
        compiler_params=_cp(("arbitrary",)), aliases={4: 1},
    )(y_pre, proj, g_ssd, dyn, dproj)


CONV_TC = 512


def _shift_down(x, s, row):
    if s == 0:
        return x
    return jnp.where(row >= s, pltpu.roll(x, s, 0), 0.0)


def _shift_up(x, s, row, t):
    if s == 0:
        return x
    return jnp.where(row < t - s, pltpu.roll(x, t - s, 0), 0.0)


def _conv_fwd(proj, conv_w, conv_b, *, name):
    t = proj.shape[0]

    def body(x_ref, w_ref, b_ref, o_ref):
        x = x_ref[...]
        row = lax.broadcasted_iota(jnp.int32, x.shape, 0)
        pre = jnp.broadcast_to(b_ref[...], x.shape)
        for k in range(CW):
            pre = pre + w_ref[k:k + 1, :] * _shift_down(x, CW - 1 - k, row)
        o_ref[...] = pre * _sigmoid(pre)

    return pl.pallas_call(
        body, name=name, out_shape=jax.ShapeDtypeStruct((t, CONV), F32), grid=(CONV // CONV_TC,),
        in_specs=[pl.BlockSpec((t, CONV_TC), lambda j: (0, O_XBC // CONV_TC + j)),
                  pl.BlockSpec((CW, CONV_TC), lambda j: (0, j)), pl.BlockSpec((1, CONV_TC), lambda j: (0, j))],
        out_specs=pl.BlockSpec((t, CONV_TC), lambda j: (0, j)), compiler_params=_cp(("parallel",)),
    )(proj, conv_w, conv_b)


def _conv_bwd(proj, conv_w, conv_b, dxs, db, dc, dproj, *, name, jobs=()):
    t = proj.shape[0]
    nx = DI // CONV_TC
    assert NG * NS == CONV_TC

    def body(x_ref, w_ref, b_ref, dxs_ref, db_ref, dc_ref, _, dx_ref, dw_ref, dbias_ref):
        j = pl.program_id(0)
        x = x_ref[...]
        row = lax.broadcasted_iota(jnp.int32, x.shape, 0)
        xs = [_shift_down(x, CW - 1 - k, row) for k in range(CW)]
        pre = jnp.broadcast_to(b_ref[...], x.shape)
        for k in range(CW):
            pre = pre + w_ref[k:k + 1, :] * xs[k]
        s = _sigmoid(pre)
        da = jnp.where(j < nx, dxs_ref[...], jnp.where(j == nx, db_ref[...], dc_ref[...]))
        dpre = da * s * (1.0 + pre * (1.0 - s))
        dx = jnp.zeros_like(x)
        row8 = lax.broadcasted_iota(jnp.int32, dw_ref.shape, 0)
        dw = jnp.zeros(dw_ref.shape, F32)
        for k in range(CW):
            dx = dx + w_ref[k:k + 1, :] * _shift_up(dpre, CW - 1 - k, row, t)
            dw = dw + jnp.where(row8 == k, jnp.sum(dpre * xs[k], axis=0, keepdims=True), 0.0)
        dx_ref[...] = dx.astype(BF16)
        dw_ref[...] = dw
        dbias_ref[...] = jnp.broadcast_to(jnp.sum(dpre, axis=0, keepdims=True), dbias_ref.shape)

    col8 = pl.BlockSpec((8, CONV_TC), lambda j: (0, j))
    xbc = pl.BlockSpec((t, CONV_TC), lambda j: (0, O_XBC // CONV_TC + j))
    whole = pl.BlockSpec((t, CONV_TC), lambda j: (0, 0))
    return _call(
        body, jobs=jobs, name=name,
        out_shape=(jax.ShapeDtypeStruct(dproj.shape, BF16), jax.ShapeDtypeStruct((8, CONV), F32),
                   jax.ShapeDtypeStruct((8, CONV), F32)),
        grid=(CONV // CONV_TC,),
        in_specs=[xbc, pl.BlockSpec((CW, CONV_TC), lambda j: (0, j)), pl.BlockSpec((1, CONV_TC), lambda j: (0, j)),
                  pl.BlockSpec((t, CONV_TC), lambda j: (0, jnp.minimum(j, nx - 1))), whole, whole, ANY],
        out_specs=(xbc, col8, col8),
        compiler_params=_cp(("arbitrary",)), aliases={6: 0},
    )(proj, conv_w, conv_b, dxs, db, dc, dproj)


def _rope_tables(positions, t):
    half = HD // 2
    inv_freq = ROPE_THETA ** (-jnp.arange(half, dtype=F32) * 2.0 / HD)
    ang = positions.reshape(t).astype(F32)[:, None] * inv_freq
    cos, sin = jnp.cos(ang), jnp.sin(ang)
    return jnp.concatenate([cos] * 4, axis=1), jnp.concatenate([-sin, sin] * 2, axis=1)


def _lane_consts():
    lane = lax.broadcasted_iota(jnp.int32, (L, 128), 1)
    return lane, (lane % HD) < (HD // 2), lane < HD


def _rope(tv, cos, sin, lo):
    return tv * cos + jnp.where(lo, pltpu.roll(tv, 128 - HD // 2, 1), pltpu.roll(tv, HD // 2, 1)) * sin


def _rope_t(dv, cos, sin, lo):
    ds = dv * sin
    return dv * cos + jnp.where(lo, pltpu.roll(ds, 128 - HD // 2, 1), pltpu.roll(ds, HD // 2, 1))


def _placed(chunk, g, half0):
    own = jnp.where(half0 if g % 2 == 0 else jnp.logical_not(half0), chunk, 0.0)
    other = pltpu.roll(own, HD, 1)
    return (own, other) if g % 2 == 0 else (other, own)


def _unplace(acc, hf, g, half0):
    v = jnp.where(half0 if hf == 0 else jnp.logical_not(half0), acc, 0.0)
    return v if hf == g % 2 else pltpu.roll(v, HD, 1)


def _attn_fwd(proj, cos, sin, sinks, *, name, jobs=()):
    t = proj.shape[0]
    nb = t // L
    scale = HD ** -0.5

    def body(sink_ref, q_ref, kc_ref, kp_ref, vc_ref, vp_ref, cc_ref, sc_ref, cp_ref, sp_ref, o_ref, lse_ref):
        i = pl.program_id(0)
        lane, lo, half0 = _lane_consts()
        cos_c, sin_c, cos_p, sin_p = cc_ref[...], sc_ref[...], cp_ref[...], sp_ref[...]
        row = lax.broadcasted_iota(jnp.int32, (L, 2 * L), 0)
        col = lax.broadcasted_iota(jnp.int32, (L, 2 * L), 1)
        valid = jnp.logical_or(jnp.logical_and(jnp.logical_and(col < L, col > row), i > 0),
                               jnp.logical_and(col >= L, col - L <= row))
        kc = [_rope(kc_ref[:, 128 * m:128 * (m + 1)], cos_c, sin_c, lo) for m in range(2)]
        kp = [_rope(kp_ref[:, 128 * m:128 * (m + 1)], cos_p, sin_p, lo) for m in range(2)]
        lse_acc = jnp.zeros((L, 128), F32)
        outs = [jnp.zeros((L, 128), F32) for _ in range(QD // 128)]
        qs = [(_rope(q_ref[:, 128 * ch:128 * (ch + 1)], cos_c, sin_c, lo) * scale).astype(BF16) for ch in range(QD // 128)]
        both = lambda prev, cur, g: [jnp.concatenate([a, b], axis=0).astype(BF16)
                                     for a, b in zip(_placed(prev, g, half0), _placed(cur, g, half0))]
        for g in range(NKV):
            sl = slice(128 * (g // 2), 128 * (g // 2 + 1))
            kv = both(kp[g // 2], kc[g // 2], g)
            vv = both(vp_ref[:, sl], vc_ref[:, sl], g)
            for r in range(NQH // NKV):
                h = g * (NQH // NKV) + r
                ch, hf = h // 2, h % 2
                s = jnp.where(valid, _dot_nt(qs[ch], kv[hf]), NEG)
                sink = sink_ref[0, h]
                mx = jnp.maximum(jnp.max(s, axis=-1, keepdims=True), sink)
                e = jnp.exp(s - mx)
                den = jnp.sum(e, axis=-1, keepdims=True) + jnp.exp(sink - mx)
                outs[ch] = outs[ch] + _dot((e * (1.0 / den)).astype(BF16), vv[hf])
                lse_acc = jnp.where(lane == h, mx + jnp.log(den), lse_acc)
        for ch in range(QD // 128):
            o_ref[:, 128 * ch:128 * (ch + 1)] = outs[ch].astype(BF16)
        lse_ref[...] = lse_acc

    prev = lambda i: jnp.maximum(i - 1, 0)
    tab_c = pl.BlockSpec((L, 128), lambda i: (i, 0))
    tab_p = pl.BlockSpec((L, 128), lambda i: (prev(i), 0))
    return _call(
        body, jobs=jobs, name=name,
        out_shape=(jax.ShapeDtypeStruct((t, QD), BF16), jax.ShapeDtypeStruct((t, 128), F32)),
        grid=(nb,),
        in_specs=[pl.BlockSpec(memory_space=pltpu.SMEM),
                  pl.BlockSpec((L, QD), lambda i: (i, O_Q // QD)),
                  pl.BlockSpec((L, KVD), lambda i: (i, O_K // KVD)), pl.BlockSpec((L, KVD), lambda i: (prev(i), O_K // KVD)),
                  pl.BlockSpec((L, KVD), lambda i: (i, O_V // KVD)), pl.BlockSpec((L, KVD), lambda i: (prev(i), O_V // KVD)),
                  tab_c, tab_c, tab_p, tab_p],
        out_specs=(pl.BlockSpec((L, QD), lambda i: (i, 0)), pl.BlockSpec((L, 128), lambda i: (i, 0))),
        compiler_params=_cp(("parallel",)),
    )(sinks, proj, proj, proj, proj, proj, cos, sin, cos, sin)


def _attn_bwd(proj, cos, sin, sinks, attn, lse, dattn, dproj, *, name, jobs=()):
    t = proj.shape[0]
    nb = t // L
    scale = HD ** -0.5

    def body(sink_ref, qi_ref, qn_ref, kc_ref, kp_ref, vc_ref, vp_ref, doi_ref, don_ref, oi_ref, on_ref,
             lsei_ref, lsen_ref, cc_ref, sc_ref, cp_ref, sp_ref, cn_ref, sn_ref, _, dqkv_ref, dsk_ref):
        i = pl.program_id(0)
        lane, lo, half0 = _lane_consts()
        half1 = jnp.logical_not(half0)
        cos_c, sin_c = cc_ref[...], sc_ref[...]
        row = lax.broadcasted_iota(jnp.int32, (L, 2 * L), 0)
        col = lax.broadcasted_iota(jnp.int32, (L, 2 * L), 1)
        valid = jnp.logical_or(jnp.logical_and(jnp.logical_and(col < L, col > row), i > 0),
                               jnp.logical_and(col >= L, col - L <= row))
        m_next = jnp.logical_and(col[:, :L] > row[:, :L], i < nb - 1)
        kc = [_rope(kc_ref[:, 128 * m:128 * (m + 1)], cos_c, sin_c, lo) for m in range(2)]
        kp = [_rope(kp_ref[:, 128 * m:128 * (m + 1)], cp_ref[...], sp_ref[...], lo) for m in range(2)]
        lse_i, lse_n = lsei_ref[...], lsen_ref[...]
        dk_acc = [jnp.zeros((L, 128), F32) for _ in range(2)]
        dv_acc = [jnp.zeros((L, 128), F32) for _ in range(2)]
        dsk_acc = jnp.zeros((1, 128), F32)
        lane1 = lax.broadcasted_iota(jnp.int32, (1, 128), 1)
        both = lambda prev, cur, g: [jnp.concatenate([a, b], axis=0).astype(BF16)
                                     for a, b in zip(_placed(prev, g, half0), _placed(cur, g, half0))]
        kvs = [both(kp[g // 2], kc[g // 2], g) for g in range(NKV)]
        vvs = [both(vp_ref[:, 128 * (g // 2):128 * (g // 2 + 1)], vc_ref[:, 128 * (g // 2):128 * (g // 2 + 1)], g)
               for g in range(NKV)]
        for ch in range(QD // 128):
            sl = slice(128 * ch, 128 * (ch + 1))
            q_i = (_rope(qi_ref[:, sl], cos_c, sin_c, lo) * scale).astype(BF16)
            q_n = (_rope(qn_ref[:, sl], cn_ref[...], sn_ref[...], lo) * scale).astype(BF16)
            q_in = jnp.concatenate([q_i, q_n], axis=0)
            do_i, do_n = doi_ref[:, sl], don_ref[:, sl]
            do_ib, do_nb = do_i.astype(BF16), do_n.astype(BF16)
            do_in = jnp.concatenate([do_ib, do_nb], axis=0)
            od_i = do_i * oi_ref[:, sl].astype(F32)
            od_n = do_n * on_ref[:, sl].astype(F32)
            dq_ch = jnp.zeros((L, 128), F32)
            for hf in range(2):
                h = 2 * ch + hf
                g = h // (NQH // NKV)
                hm = half0 if hf == 0 else half1
                kv, vv = kvs[g][hf], vvs[g][hf]
                kcv, vcv = kv[L:], vv[L:]
                dl_i = jnp.sum(jnp.where(hm, od_i, 0.0), axis=-1, keepdims=True)
                dl_n = jnp.sum(jnp.where(hm, od_n, 0.0), axis=-1, keepdims=True)
                ls_i = jnp.sum(jnp.where(lane == h, lse_i, 0.0), axis=-1, keepdims=True)
                ls_n = jnp.sum(jnp.where(lane == h, lse_n, 0.0), axis=-1, keepdims=True)
                p = jnp.where(valid, jnp.exp(_dot_nt(q_i, kv) - ls_i), 0.0)
                ds = (p * (_dot_nt(do_ib, vv) - dl_i)).astype(BF16)
                dq_ch = dq_ch + jnp.where(hm, _dot(ds, kv) * scale, 0.0)
                sink = sink_ref[0, h]
                dsk = -jnp.sum(jnp.exp(sink - ls_i) * dl_i, axis=0, keepdims=True)
                dsk_acc = dsk_acc + jnp.where(lane1 == h, dsk, 0.0)
                p_n = jnp.where(m_next, jnp.exp(_dot_nt(q_n, kcv) - ls_n), 0.0)
                ds_n = (p_n * (_dot_nt(do_nb, vcv) - dl_n)).astype(BF16)
                dv_h = _dot_tn(jnp.concatenate([p[:, L:].astype(BF16), p_n.astype(BF16)], axis=0), do_in)
                dk_h = _dot_tn(jnp.concatenate([ds[:, L:], ds_n], axis=0), q_in)
                dv_acc[g // 2] = dv_acc[g // 2] + _unplace(dv_h, hf, g, half0)
                dk_acc[g // 2] = dk_acc[g // 2] + _unplace(dk_h, hf, g, half0)
            dqkv_ref[:, sl] = _rope_t(dq_ch, cos_c, sin_c, lo).astype(BF16)
        for m in range(2):
            dqkv_ref[:, QD + 128 * m:QD + 128 * (m + 1)] = _rope_t(dk_acc[m], cos_c, sin_c, lo).astype(BF16)
            dqkv_ref[:, QD + KVD + 128 * m:QD + KVD + 128 * (m + 1)] = dv_acc[m].astype(BF16)

        @pl.when(i == 0)
        def _():
            dsk_ref[...] = jnp.zeros_like(dsk_ref)

        dsk_ref[...] += jnp.broadcast_to(dsk_acc, dsk_ref.shape)

    prev = lambda i: jnp.maximum(i - 1, 0)
    nxt = lambda i: jnp.minimum(i + 1, nb - 1)
    cur_q = pl.BlockSpec((L, QD), lambda i: (i, 0))
    nxt_q = pl.BlockSpec((L, QD), lambda i: (nxt(i), 0))
    tab = lambda f: pl.BlockSpec((L, 128), lambda i: (f(i), 0))
    ident = lambda i: i
    qkv = QD + 2 * KVD
    assert O_K == O_Q + QD and O_V == O_K + KVD and O_Q % qkv == 0
    return _call(
        body, jobs=jobs, name=name,
        out_shape=(jax.ShapeDtypeStruct(dproj.shape, BF16), jax.ShapeDtypeStruct((8, 128), F32)),
        grid=(nb,),
        in_specs=[pl.BlockSpec(memory_space=pltpu.SMEM),
                  pl.BlockSpec((L, QD), lambda i: (i, O_Q // QD)), pl.BlockSpec((L, QD), lambda i: (nxt(i), O_Q // QD)),
                  pl.BlockSpec((L, KVD), lambda i: (i, O_K // KVD)), pl.BlockSpec((L, KVD), lambda i: (prev(i), O_K // KVD)),
                  pl.BlockSpec((L, KVD), lambda i: (i, O_V // KVD)), pl.BlockSpec((L, KVD), lambda i: (prev(i), O_V // KVD)),
                  cur_q, nxt_q, cur_q, nxt_q, tab(ident), tab(nxt),
                  tab(ident), tab(ident), tab(prev), tab(prev), tab(nxt), tab(nxt), ANY],
        out_specs=(pl.BlockSpec((L, qkv), lambda i: (i, O_Q // qkv)), pl.BlockSpec((8, 128), lambda i: (0, 0))),
        compiler_params=_cp(("arbitrary",)), aliases={19: 0},
    )(sinks, proj, proj, proj, proj, proj, proj, dattn, dattn, attn, attn, lse, lse, cos, sin, cos, sin, cos, sin, dproj)


PAIRS = NH // NG // 2


def _softplus(x):
    return jnp.maximum(x, 0.0) + jnp.log(1.0 + jnp.exp(-jnp.abs(x)))


def _ssd_chunk(g, xps, dtr, bm, cm, sps, dtb, alog, dsk):
    lane = lax.broadcasted_iota(jnp.int32, (L, 128), 1)
    lane1 = lax.broadcasted_iota(jnp.int32, (1, 128), 1)
    row = lax.broadcasted_iota(jnp.int32, (L, L), 0)
    col = lax.broadcasted_iota(jnp.int32, (L, L), 1)
    rowc = lax.broadcasted_iota(jnp.int32, (128, 1), 0)
    tril = col <= row
    dt = _softplus(dtr + dtb)
    a = dt * (-jnp.exp(alog))
    a_cs = lax.dot_general(tril.astype(F32), a, (((1,), (0,)), ((), ())), precision=lax.Precision.HIGHEST,
                           preferred_element_type=F32)
    a_cst = a_cs.T
    a_last = jnp.sum(jnp.where(row == L - 1, a_cs, 0.0), axis=0, keepdims=True)
    cb = _bdot_nt(cm, bm)
    ys, snew = [], []
    for q in range(PAIRS):
        xp, sp = xps[q], sps[q]
        y_pair = jnp.zeros((L, 128), F32)
        st_pair = jnp.zeros((128, NS), F32)
        keep = jnp.zeros((128, 1), F32)
        for hh in range(2):
            h = g * 2 * PAIRS + 2 * q + hh
            hm = (lane < HD) if hh == 0 else (lane >= HD)
            rm = (rowc < HD) if hh == 0 else (rowc >= HD)
            dt_h = jnp.sum(jnp.where(lane == h, dt, 0.0), axis=1, keepdims=True)
            acs_h = jnp.sum(jnp.where(lane == h, a_cs, 0.0), axis=1, keepdims=True)
            acst_h = jnp.sum(jnp.where(row == h, a_cst, 0.0), axis=0, keepdims=True)
            al_h = jnp.sum(jnp.where(lane1 == h, a_last, 0.0), axis=1, keepdims=True)
            dsk_h = jnp.sum(jnp.where(lane1 == h, dsk, 0.0), axis=1, keepdims=True)
            decay = jnp.where(tril, jnp.exp(jnp.where(tril, acs_h - acst_h, 0.0)), 0.0)
            xh = jnp.where(hm, xp, 0.0)
            xd = xh * dt_h
            y = _bdot(cb * decay, xd)
            y = y + jnp.where(hm, _bdot_nt(cm * jnp.exp(acs_h), sp), 0.0)
            y_pair = y_pair + y + dsk_h * xh
            st_pair = st_pair + _bdot_tn(xd, bm * jnp.exp(al_h - acs_h))
            keep = keep + jnp.where(rm, jnp.exp(al_h), 0.0)
        ys.append(y_pair)
        snew.append(sp * keep + st_pair)
    return ys, snew


def _ssd_specs(t):
    nc = t // L
    xs = lambda f: pl.BlockSpec((L, 128 * PAIRS), lambda c, g: (f(c), g))
    bspec = lambda f: pl.BlockSpec((L, NS), lambda c, g: (f(c), DI // NS + g))
    cspec = lambda f: pl.BlockSpec((L, NS), lambda c, g: (f(c), DI // NS + NG + g))
    dts = lambda f: pl.BlockSpec((L, 128), lambda c, g: (f(c), O_DT // 128))
    par = pl.BlockSpec((1, 128), lambda c, g: (0, 0))
    st = lambda f: pl.BlockSpec((1, 1, PAIRS, 128, NS), lambda c, g: (f(c), g, 0, 0, 0))
    return nc, xs, bspec, cspec, dts, par, st


def _ssd_fwd(xbc_act, proj, dtb, alog, dsk, *, name, jobs=()):
    t = proj.shape[0]
    nc, xs, bspec, cspec, dts, par, st = _ssd_specs(t)
    ident = lambda c: c

    def body(x_ref, b_ref, c_ref, dt_ref, dtb_ref, al_ref, dsk_ref, y_ref, sin_ref, s_ref):
        c, g = pl.program_id(0), pl.program_id(1)

        @pl.when(c == 0)
        def _():
            s_ref[g] = jnp.zeros((PAIRS, 128, NS), F32)

        sps = [s_ref[g, q] for q in range(PAIRS)]
        for q in range(PAIRS):
            sin_ref[0, 0, q] = sps[q]
        xps = [x_ref[:, 128 * q:128 * (q + 1)] for q in range(PAIRS)]
        ys, snew = _ssd_chunk(g, xps, dt_ref[...], b_ref[...], c_ref[...], sps, dtb_ref[...], al_ref[...], dsk_ref[...])
        for q in range(PAIRS):
            y_ref[:, 128 * q:128 * (q + 1)] = ys[q]
            s_ref[g, q] = snew[q]

    return _call(
        body, jobs=jobs, name=name,
        out_shape=(jax.ShapeDtypeStruct((t, DI), F32), jax.ShapeDtypeStruct((nc, NG, PAIRS, 128, NS), F32)),
        grid=(nc, NG),
        in_specs=[xs(ident), bspec(ident), cspec(ident), dts(ident), par, par, par],
        out_specs=(pl.BlockSpec((L, 128 * PAIRS), lambda c, g: (c, g)), st(ident)),
        scratch_shapes=[pltpu.VMEM((NG, PAIRS, 128, NS), F32)],
        compiler_params=_cp(("arbitrary", "arbitrary")),
    )(xbc_act, xbc_act, xbc_act, proj, dtb, alog, dsk)


def _ssd_bwd(xbc_act, proj, dtb, alog, dsk, states, dy, dproj, *, name, jobs=()):
    t = proj.shape[0]
    nc, xs, bspec, cspec, dts, par, st = _ssd_specs(t)
    rev = lambda c: nc - 1 - c

    def body(x_ref, b_ref, c_ref, dt_ref, dtb_ref, al_ref, dsk_ref, sin_ref, dy_ref, _,
             dx_ref, db_ref, dc_ref, ddtp_ref, ddtb_ref, dal_ref, ddsk_ref, ds_ref, ddt_ref):
        c, g = pl.program_id(0), pl.program_id(1)

        @pl.when(c == 0)
        def _():
            ds_ref[g] = jnp.zeros((PAIRS, 128, NS), F32)

        @pl.when(jnp.logical_and(c == 0, g == 0))
        def _():
            ddtb_ref[...] = jnp.zeros_like(ddtb_ref)
            dal_ref[...] = jnp.zeros_like(dal_ref)
            ddsk_ref[...] = jnp.zeros_like(ddsk_ref)

        @pl.when(g == 0)
        def _():
            ddt_ref[...] = jnp.zeros_like(ddt_ref)

        sps = [sin_ref[0, 0, q] for q in range(PAIRS)]
        xps = [x_ref[:, 128 * q:128 * (q + 1)] for q in range(PAIRS)]
        _, vjp = jax.vjp(functools.partial(_ssd_chunk, g), xps, dt_ref[...], b_ref[...], c_ref[...], sps,
                         dtb_ref[...], al_ref[...], dsk_ref[...])
        dys = [dy_ref[:, 128 * q:128 * (q + 1)] for q in range(PAIRS)]
        dss = [ds_ref[g, q] for q in range(PAIRS)]
        dxps, ddt, db, dc, dsps, ddtb, dal, ddsk = vjp((dys, dss))
        for q in range(PAIRS):
            dx_ref[:, 128 * q:128 * (q + 1)] = dxps[q]
            ds_ref[g, q] = dsps[q]
        db_ref[...] = db
        dc_ref[...] = dc
        ddt_ref[...] += ddt
        ddtb_ref[...] += jnp.broadcast_to(ddtb, ddtb_ref.shape)
        dal_ref[...] += jnp.broadcast_to(dal, dal_ref.shape)
        ddsk_ref[...] += jnp.broadcast_to(ddsk, ddsk_ref.shape)

        @pl.when(g == NG - 1)
        def _():
            ddtp_ref[:, :128] = ddt_ref[...].astype(BF16)
            ddtp_ref[:, 128:] = jnp.zeros((L, DT_PAD - 128), BF16)

    acc = pl.BlockSpec((8, 128), lambda c, g: (0, 0))
    o8 = jax.ShapeDtypeStruct((8, 128), F32)
    return _call(
        body, jobs=jobs, name=name,
        out_shape=(jax.ShapeDtypeStruct((t, DI), F32), jax.ShapeDtypeStruct((t, NG * NS), F32),
                   jax.ShapeDtypeStruct((t, NG * NS), F32), jax.ShapeDtypeStruct(dproj.shape, BF16), o8, o8, o8),
        grid=(nc, NG),
        in_specs=[xs(rev), bspec(rev), cspec(rev), dts(rev), par, par, par, st(rev),
                  pl.BlockSpec((L, 128 * PAIRS), lambda c, g: (rev(c), g)), ANY],
        out_specs=(pl.BlockSpec((L, 128 * PAIRS), lambda c, g: (rev(c), g)),
                   pl.BlockSpec((L, NS), lambda c, g: (rev(c), g)), pl.BlockSpec((L, NS), lambda c, g: (rev(c), g)),
                   pl.BlockSpec((L, DT_PAD), lambda c, g: (rev(c), O_DT // DT_PAD)), acc, acc, acc),
        scratch_shapes=[pltpu.VMEM((NG, PAIRS, 128, NS), F32), pltpu.VMEM((L, 128), F32)],
        compiler_params=_cp(("arbitrary", "arbitrary")), aliases={9: 3},
    )(xbc_act, xbc_act, xbc_act, proj, dtb, alog, dsk, states, dy, dproj)


def _pad_lanes(v, n=128):
    return jnp.pad(v, ((0, 0), (0, n - v.shape[1])))


class _LocalPlan:
    core = 0

    def __init__(self, big):
        self.big, self.grad, self.halves = big, {}, {}

    def w(self, n):
        return self.big[n]

    def g(self, n, a):
        self.grad[n] = a

    def g_half(self, n, which, a):
        self.halves[which] = a
        if len(self.halves) == 2:
            self.grad[n] = jnp.concatenate([self.halves["keep"], self.halves["send"]], axis=0)

    def jobs(self, tag):
        return ()


def _local_step(x, p, positions, target, small, plan):
    t = x.shape[0]
    cos, sin = _rope_tables(positions, t)
    dtb, alog, dsk = _pad_lanes(small["dt_bias"]), _pad_lanes(small["a_log"]), _pad_lanes(small["d_skip"])
    w, jobs = plan.w, plan.jobs

    def mm(a, b, *, name, tn=512, **kw):
        return _matmul(a, b, tm=t, tn=tn, name=name, jobs=jobs(name), **kw)

    tkl = FFN // 4

    def dw(wname, a, dy, *, name, tm):
        plan.g(wname, _matmul(a, dy, ta=True, out_dtype=BF16, tm=tm, tn=512, tk=t, name=name, jobs=jobs(name)))

    u = _rmsnorm_fwd(x, small["g_mix"], name="norm_mix")
    proj = mm(u, w("w_in"), tn=1024, tk=D, name="mm_in")
    attn, lse = _attn_fwd(proj, cos, sin, small["sinks"], name="attn_fwd", jobs=jobs("attn_fwd"))
    out_a = mm(attn, w("w_attn_br"), tk=QD, name="mm_attn_br")
    xbc_act = _conv_fwd(proj, small["conv_w"], small["conv_b"], name="conv_fwd")
    y_pre, states = _ssd_fwd(xbc_act, proj, dtb, alog, dsk, name="ssd_fwd", jobs=jobs("ssd_fwd"))
    yn = _gated_norm_fwd(y_pre, proj, small["g_ssd"], name="gated_norm_fwd")
    out_s = mm(yn, w("w_ssd_br"), tk=DI, name="mm_ssd_br")
    merged = _merge_fwd(proj, out_a, out_s, name="merge_fwd")
    h1 = mm(merged, w("w_o"), add=x, tk=D, name="mm_o")
    f = _rmsnorm_fwd(h1, small["g_ffn"], name="norm_ffn")
    gate, up, act = _swiglu_fwd(f, w("w_gate"), w("w_up"), name="swiglu_fwd", jobs=jobs("swiglu_fwd"))
    h2 = mm(act, w("w_down"), add=h1, tk=tkl, name="mm_down")
    e = _rmsnorm_fwd(h2, small["g_ple"], name="norm_ple")
    pgl = mm(e, w("w_ple_gate"), tk=D, name="mm_ple_gate")
    pb = p.astype(BF16)
    pp = mm(pb, w("w_ple_proj"), tk=PLE, name="mm_ple_proj")
    dh3, dpgl, dpp, loss, dg_final = _final(h2, pgl, pp, target, small["g_final"].reshape(1, D), name="final")

    dw("w_ple_proj", pb, dpp, tm=PLE, name="mm_d_ple_proj")
    dw("w_ple_gate", e, dpgl, tm=D, name="mm_d_ple_gate")
    de = mm(dpgl, w("w_ple_gate"), tb=True, tk=D, name="mm_de")
    dh2, dh2b, dg_ple = _rmsnorm_bwd(h2, small["g_ple"], de, dh3, name="norm_ple_bwd", jobs=jobs("norm_ple_bwd"))
    dw("w_down", act, dh2b, tm=FFN // 2, name="mm_d_down")
    dgate, dup = _swiglu_bwd(dh2b, w("w_down"), gate, up, name="swiglu_bwd", jobs=jobs("swiglu_bwd"))
    dw("w_gate", f, dgate, tm=D, name="mm_d_gate")
    dw("w_up", f, dup, tm=D, name="mm_d_up")
    df = mm(dgate, w("w_gate"), tb=True, tn=1024, tk=tkl, name="mm_df_gate")
    df = mm(dup, w("w_up"), tb=True, add=df, tk=tkl, name="mm_df_up")
    dh1, dh1b, dg_ffn = _rmsnorm_bwd(h1, small["g_ffn"], df, dh2, name="norm_ffn_bwd", jobs=jobs("norm_ffn_bwd"))
    dw("w_o", merged, dh1b, tm=D, name="mm_d_o")
    dmerged = mm(dh1b, w("w_o"), tb=True, tk=D, name="mm_dmerged")
    dout_a, dout_s, dproj = _merge_bwd(proj, out_a, out_s, dmerged, name="merge_bwd")
    dw("w_attn_br", attn, dout_a, tm=QD, name="mm_d_attn_br")
    dw("w_ssd_br", yn, dout_s, tm=DI, name="mm_d_ssd_br")
    dattn = mm(dout_a, w("w_attn_br"), tb=True, tk=D, name="mm_dattn")
    dyn = mm(dout_s, w("w_ssd_br"), tb=True, tk=D, name="mm_dyn")
    dproj, dsinks = _attn_bwd(proj, cos, sin, small["sinks"], attn, lse, dattn, dproj, name="attn_bwd",
                              jobs=jobs("attn_bwd"))
    dy_pre, dproj, dg_ssd = _gated_norm_bwd(y_pre, proj, small["g_ssd"], dyn, dproj, name="gated_norm_bwd",
                                            jobs=jobs("gated_norm_bwd"))
    dxs, db, dc, dproj, ddtb, dalog, ddsk = _ssd_bwd(xbc_act, proj, dtb, alog, dsk, states, dy_pre, dproj, name="ssd_bwd",
                                                     jobs=jobs("ssd_bwd"))
    dproj, dconv_w, dconv_b = _conv_bwd(proj, small["conv_w"], small["conv_b"], dxs, db, dc, dproj, name="conv_bwd",
                                        jobs=jobs("conv_bwd"))
    for which, h in (("send", 1 - plan.core), ("keep", plan.core)):
        uh = lax.dynamic_slice_in_dim(u, h * (D // 2), D // 2, axis=1)
        name = "mm_d_in_" + which
        plan.g_half("w_in", which, _matmul(uh, dproj, ta=True, out_dtype=BF16, tm=D // 2, tn=1024, tk=t, name=name,
                                           jobs=jobs(name)))
    du = mm(dproj, w("w_in"), tb=True, tn=1024, tk=tkl, name="mm_du")
    grad_x, _, dg_mix = _rmsnorm_bwd(x, small["g_mix"], du, dh1, name="norm_mix_bwd", jobs=jobs("norm_mix_bwd"))

    gs = {
        "g_mix": dg_mix[:1], "conv_w": dconv_w[:CW], "conv_b": dconv_b[:1], "dt_bias": ddtb[:1, :NH],
        "a_log": dalog[:1, :NH], "d_skip": ddsk[:1, :NH], "g_ssd": dg_ssd[:1], "sinks": dsinks[:1, :NQH],
        "g_ffn": dg_ffn[:1], "g_ple": dg_ple[:1], "g_final": dg_final[0],
    }
    return loss, grad_x, gs


def _to_kernel_cols(w):
    seg = lambda o, n: w[:, o:o + n]
    return jnp.concatenate([seg(R_GA, D), seg(R_GS, D), seg(R_Z, DI), seg(R_XBC, CONV), seg(R_Q, QD), seg(R_K, KVD),
                            seg(R_V, KVD), seg(R_DT, NH), jnp.zeros((w.shape[0], DT_PAD - NH), w.dtype)], axis=1)


def _from_kernel_cols(g):
    seg = lambda o, n: g[:, o:o + n]
    return jnp.concatenate([seg(O_Q, QD), seg(O_K, KVD), seg(O_V, KVD), seg(O_Z, DI), seg(O_XBC, CONV), seg(O_DT, NH),
                            seg(O_GA, D), seg(O_GS, D)], axis=1)


def _shard_pieces():
    segs = ((R_Q, QD, O_Q), (R_K, KVD, O_K), (R_V, KVD, O_V), (R_Z, DI, O_Z), (R_XBC, CONV, O_XBC), (R_DT, NH, O_DT),
            (R_GA, D, O_GA), (R_GS, D, O_GS))
    cs = IN_DIM // NCHIP
    out = []
    for j in range(NCHIP):
        for r0, n, k0 in segs:
            lo, hi = max(r0, j * cs), min(r0 + n, (j + 1) * cs)
            if lo < hi:
                out.append((j, lo - j * cs, hi - lo, k0 + lo - r0))
    return out


SLAB = IN_DIM // NCHIP
SLAB_PAD = -(-SLAB // 128) * 128
REMAP_ROWS = 256


def _lane_remap(src, dst_slabs, dst_cols, moves, *, name, jobs=()):
    s_n, rows, s_cols = src.shape
    assert s_cols % 128 == 0 and dst_cols % 128 == 0 and rows % REMAP_ROWS == 0
    half = REMAP_ROWS // 2

    def body(s_ref, d_ref):
        lane = lax.broadcasted_iota(jnp.int32, (half, 128), 1)
        tiles = {}

        def tile(j, m):
            if (j, m) not in tiles:
                tiles[j, m] = pltpu.bitcast(s_ref[j, :, 128 * m:128 * (m + 1)], jnp.uint32)
            return tiles[j, m]

        def window(j, base):
            m0, s = base // 128, base % 128
            left = tile(j, m0) if 0 <= m0 < s_cols // 128 else None
            if s == 0:
                return left
            right = tile(j, m0 + 1) if 0 <= m0 + 1 < s_cols // 128 else None
            left = None if left is None else pltpu.roll(left, 128 - s, 1)
            right = None if right is None else pltpu.roll(right, 128 - s, 1)
            if left is None or right is None:
                return right if left is None else left
            return jnp.where(lane < 128 - s, left, right)

        for ds in range(dst_slabs):
            for t in range(dst_cols // 128):
                o = 128 * t
                acc = jnp.zeros((half, 128), jnp.uint32)
                for sj, sc, n, dj, dc in moves:
                    lo, hi = max(o, dc) - o, min(o + 128, dc + n) - o
                    if dj != ds or lo >= hi:
                        continue
                    win = window(sj, o - dc + sc)
                    acc = win if (lo, hi) == (0, 128) else jnp.where(jnp.logical_and(lane >= lo, lane < hi), win, acc)
                d_ref[ds, :, o:o + 128] = pltpu.bitcast(acc, BF16)

    return _call(
        body, jobs=jobs, name=name, out_shape=jax.ShapeDtypeStruct((dst_slabs, rows, dst_cols), BF16),
        grid=(rows // REMAP_ROWS,),
        in_specs=[pl.BlockSpec((s_n, REMAP_ROWS, s_cols), lambda i: (0, i, 0))],
        out_specs=pl.BlockSpec((dst_slabs, REMAP_ROWS, dst_cols), lambda i: (0, i, 0)),
        compiler_params=_cp(("parallel",)),
    )(src)


def _slabs_to_kernel_cols(slabs, *, name, jobs=()):
    moves = [(j, a, n, 0, k0) for j, a, n, k0 in _shard_pieces()]
    return _lane_remap(slabs, 1, NP, moves, name=name, jobs=jobs)[0]


def _kernel_cols_to_slabs(g, *, name, jobs=()):
    moves = [(0, k0, n, j, a) for j, a, n, k0 in _shard_pieces()]
    return _lane_remap(g[None], NCHIP, SLAB_PAD, moves, name=name, jobs=jobs)


RELS = ((0, 1), (1, 0), (1, 1))
MATS = {
    n: (n, kind, 1, r, c, tp, tf) for n, kind, r, c, tp, tf in (
        ("w_in", "stk", 2048, SLAB_PAD, 256, 256),
        ("w_attn_br", "col", 1024, 512, 256, 256),
        ("w_ssd_br", "row", 512, 2048, 512, 256),
        ("w_o", "row", 512, 2048, 512, 256),
        ("w_gate", "col", 2048, 1408, 256, 256),
        ("w_up", "col", 2048, 1408, 256, 256),
        ("w_down", "row", 1408, 2048, 704, 704),
        ("w_ple_gate", "row", 512, 2048, 512, 256),
        ("w_ple_proj", "col", 256, 512, 128, 128),
    )}


def _pos():
    return lax.axis_index("x"), lax.axis_index("y"), lax.axis_index("c")


def _flip(v, a):
    return 1 - v if a else v


def _remote(src, dst, send, recv, dev):
    return pltpu.make_async_remote_copy(src_ref=src, dst_ref=dst, send_sem=send, recv_sem=recv, device_id=dev,
                                        device_id_type=MESH)


def _whole_shape(kind, g, r, c):
    return {"row": (g, NCHIP * r, c), "col": (g, r, NCHIP * c), "stk": (NCHIP, r, c)}[kind]


def _cols(j, c):
    return pl.ds(pl.multiple_of(j * c, 128), c)


def _whole_shard(kind, ref, j, r, c):
    if kind == "row":
        return ref.at[:, pl.ds(j * r, r), :]
    if kind == "col":
        return ref.at[:, :, _cols(j, c)]
    return ref.at[pl.ds(j, 1)]


def _whole_rows(kind, ref, j, row, n, r, c):
    if kind == "row":
        return ref.at[:, pl.ds(j * r + row, n), :]
    if kind == "col":
        return ref.at[:, pl.ds(row, n), _cols(j, c)]
    return ref.at[pl.ds(j, 1), pl.ds(row, n), :]


class _GatherJob(_Job):
    has_mid = True
    NCP = 13

    def __init__(self, names, shards, sink):
        self.mats = [MATS[n] for n in names]
        self.srcs = [shards[n] for n in names]
        self.news = [jax.ShapeDtypeStruct(_whole_shape(kind, g, r, c), BF16) for _, kind, g, r, c, _, _ in self.mats]
        n = len(names)
        self.scratch = [pltpu.SemaphoreType.DMA((self.NCP * n,)), pltpu.SemaphoreType.DMA((self.NCP * n,))]
        self.names, self.sink = names, sink

    def _copies(self, srcs, news, sems):
        send, recv = sems
        x, y, c = _pos()
        me, jx, jy, jd = 2 * x + y, 2 * (1 - x) + y, 2 * x + (1 - y), 2 * (1 - x) + (1 - y)
        nbx, nby, sib = (1 - x, y, c), (x, 1 - y, c), (x, y, 1 - c)
        cps = []
        for w, (_, kind, g, r, cc, _, _) in enumerate(self.mats):
            hr, qr = r // 2, r // 4
            at = lambda j, h, q, n: _whole_rows(kind, news[w], j, h * hr + q * qr, n, r, cc)
            mine = lambda q: srcs[w].at[:, pl.ds(c * hr + q * qr, qr), :]
            cp = lambda k, s, d, dev: _remote(s, d, send.at[self.NCP * w + k], recv.at[self.NCP * w + k], dev)
            cps.append([
                cp(0, mine(0), at(me, c, 0, qr), nbx), cp(1, mine(1), at(me, c, 1, qr), nbx),
                cp(2, mine(1), at(me, c, 1, qr), nby), cp(3, mine(0), at(me, c, 0, qr), nby),
                cp(4, at(jx, c, 0, qr), at(jx, c, 0, qr), nby), cp(5, at(jy, c, 1, qr), at(jy, c, 1, qr), nbx),
                cp(6, at(jx, c, 0, qr), at(jx, c, 0, qr), sib), cp(7, at(jx, c, 1, qr), at(jx, c, 1, qr), sib),
                cp(8, at(jy, c, 1, qr), at(jy, c, 1, qr), sib), cp(9, at(jy, c, 0, qr), at(jy, c, 0, qr), sib),
                cp(10, at(jd, c, 0, qr), at(jd, c, 0, qr), sib), cp(11, at(jd, c, 1, qr), at(jd, c, 1, qr), sib),
                cp(12, srcs[w], _whole_shard(kind, news[w], me, r, cc), sib)])
        return cps

    def _pass_on(self, srcs, news, sems, pairs):
        cps = self._copies(srcs, news, sems)
        for w in range(len(self.mats)):
            for arrived, onward in pairs:
                cps[w][arrived].wait_recv()
                for k in onward:
                    cps[w][k].start()

    def start(self, srcs, dsts, news, sems):
        cps = self._copies(srcs, news, sems)
        for w in range(len(self.mats)):
            for k in (0, 1, 2, 3, 12):
                cps[w][k].start()

    def mid(self, srcs, dsts, news, sems):
        self._pass_on(srcs, news, sems, ((0, (4, 6)), (2, (5, 8))))

    def late(self, srcs, dsts, news, sems):
        self._pass_on(srcs, news, sems, ((1, (7,)), (3, (9,))))

    def finish(self, srcs, dsts, news, sems):
        self._pass_on(srcs, news, sems, ((4, (10,)), (5, (11,))))
        cps = self._copies(srcs, news, sems)
        for w in range(len(self.mats)):
            for k in (6, 7, 8, 9, 10, 11, 12):
                cps[w][k].wait_recv()
            for k in range(self.NCP):
                cps[w][k].wait_send()

    def done(self, dsts, news):
        for n, a in zip(self.names, news):
            self.sink[n] = a


class _SwapJob(_Job):
    def __init__(self, build, ncopies, *, srcs=(), dsts=(), news=(), done=None):
        self.build, self.srcs, self.dsts, self.news, self._done = build, list(srcs), list(dsts), list(news), done
        self.scratch = [pltpu.SemaphoreType.DMA((ncopies,)), pltpu.SemaphoreType.DMA((ncopies,))]

    def start(self, srcs, dsts, news, sems):
        for cp in self.build(srcs, dsts, news, *sems):
            cp.start()

    def finish(self, srcs, dsts, news, sems):
        for cp in self.build(srcs, dsts, news, *sems):
            cp.wait()

    def done(self, dsts, news):
        if self._done is not None:
            self._done(dsts, news)


def _half_of_whole(kind, ref, h, r, c):
    if kind == "row":
        return ref.at[:, :, pl.ds(pl.multiple_of(h * (c // 2), 128), c // 2)]
    return ref.at[:, pl.ds(h * (r // 2), r // 2), :]


def _half_shape(kind, g, r, c):
    return {"row": (g, NCHIP * r, c // 2), "col": (g, r // 2, NCHIP * c), "stk": (NCHIP, r // 2, c)}[kind]


def _sub_shape(kind, r, c):
    return {"row": (1, r // 2, c // 2), "col": (1, r // 4, c), "stk": (1, r // 4, c)}[kind]


def _sub_of_half(kind, ref, j, p, r, c):
    sr = _sub_shape(kind, r, c)[1]
    if kind == "row":
        return ref.at[:, pl.ds(j * r + p * sr, sr), :]
    if kind == "col":
        return ref.at[:, pl.ds(p * sr, sr), _cols(j, c)]
    return ref.at[pl.ds(j, 1), pl.ds(p * sr, sr), :]


def _sub_tile(sr):
    return 256 if sr % 256 == 0 else sr


def _half_of_shard(kind, ref, h, r, c):
    if kind == "row":
        return ref.at[:, :, pl.ds(pl.multiple_of(h * (c // 2), 128), c // 2)]
    return ref.at[:, pl.ds(h * (r // 2), r // 2), :]


def _pair_sum(pack, core, mine, got, whole=True):
    name, kind, g, r, c, tr, _ = pack
    hs = _half_shape(kind, g, r, c)
    nb = hs[1] // tr

    def body(core_ref, a_ref, b_ref, o_ref):
        o_ref[...] = (a_ref[...].astype(F32) + b_ref[...].astype(F32)).astype(BF16)

    blk = (1, tr, hs[2])
    same = lambda gi, i, core_ref: (gi, i, 0)
    if not whole:
        a_map = same
    elif kind == "row":
        a_map = lambda gi, i, core_ref: (gi, i, core_ref[0])
    else:
        a_map = lambda gi, i, core_ref: (gi, core_ref[0] * nb + i, 0)
    return pl.pallas_call(
        body, name="pair_sum_" + name, out_shape=jax.ShapeDtypeStruct(hs, BF16),
        grid_spec=pltpu.PrefetchScalarGridSpec(
            num_scalar_prefetch=1, grid=(hs[0], nb),
            in_specs=[pl.BlockSpec(blk, a_map), pl.BlockSpec(blk, same)], out_specs=pl.BlockSpec(blk, same)),
        compiler_params=_cp(("parallel", "parallel")),
    )(core, mine, got)


def _sub_sum(pack, idx, half, got, first_slot, out_dtype, *, name):
    _, kind, g, r, c, _, _ = pack
    _, sr, sc = _sub_shape(kind, r, c)
    tr = _sub_tile(sr)
    nb = sr // tr

    def body(idx_ref, a_ref, b_ref, o_ref):
        o_ref[0, 0] = (a_ref[0].astype(F32) + b_ref[0, 0].astype(F32)).astype(out_dtype)

    if kind == "row":
        a_map = lambda q, i, ix: (0, ix[2 * q] * (r // tr) + ix[2 * q + 1] * nb + i, 0)
    elif kind == "col":
        a_map = lambda q, i, ix: (0, ix[2 * q + 1] * nb + i, ix[2 * q])
    else:
        a_map = lambda q, i, ix: (ix[2 * q], ix[2 * q + 1] * nb + i, 0)
    return pl.pallas_call(
        body, name=name, out_shape=jax.ShapeDtypeStruct((2, 1, sr, sc), out_dtype),
        grid_spec=pltpu.PrefetchScalarGridSpec(
            num_scalar_prefetch=1, grid=(2, nb),
            in_specs=[pl.BlockSpec((1, tr, sc), a_map),
                      pl.BlockSpec((1, 1, tr, sc), lambda q, i, ix: (first_slot + 2 * q, 0, i, 0))],
            out_specs=pl.BlockSpec((1, 1, tr, sc), lambda q, i, ix: (q, 0, i, 0))),
        compiler_params=_cp(("parallel", "parallel")),
    )(idx, half, got)


def _shard_sum(pack, core, keep, got):
    name, kind, g, r, c, _, _ = pack
    _, sr, sc = _sub_shape(kind, r, c)
    tr = _sub_tile(sr)
    nb = sr // tr

    def body(core_ref, a_ref, b_ref, o_ref):
        o_ref[0] = a_ref[0, 0] + b_ref[0, 0].astype(F32)

    blk = pl.BlockSpec((1, 1, tr, sc), lambda p, i, cr: (p, 0, i, 0))
    if kind == "row":
        o_map = lambda p, i, cr: (0, p * nb + i, cr[0])
    else:
        o_map = lambda p, i, cr: (0, cr[0] * 2 * nb + p * nb + i, 0)
    return pl.pallas_call(
        body, name="shard_sum_" + name, out_shape=jax.ShapeDtypeStruct((g, r, c), F32),
        grid_spec=pltpu.PrefetchScalarGridSpec(
            num_scalar_prefetch=1, grid=(2, nb), in_specs=[blk, blk], out_specs=pl.BlockSpec((1, tr, sc), o_map)),
        compiler_params=_cp(("parallel", "parallel")),
    )(core, keep, got)


class _Plan:
    def __init__(self, shards, table):
        self.shards, self.table = shards, table
        self.whole, self.grad, self.got_a, self.half, self.gshard = {}, {}, {}, {}, {}
        self.got_b1, self.kept, self.pass_on, self.got_b2 = {}, {}, {}, {}
        x, y, c = _pos()
        me, jx, jy = 2 * x + y, 2 * (1 - x) + y, 2 * x + (1 - y)
        self.core = c
        self.core1 = c.reshape(1).astype(jnp.int32)
        self.idx_keep = jnp.stack([me, 0 * me, me, 0 * me + 1]).astype(jnp.int32)
        self.idx_pass = jnp.stack([jy, 0 * me, jx, 0 * me + 1]).astype(jnp.int32)
        self._w_in = None
        self.send, self.keep = {}, {}

    def w(self, n):
        if n != "w_in":
            return self.whole[n][0]
        if self._w_in is None:
            self._w_in = _slabs_to_kernel_cols(self.whole[n], name="relayout_w_in", jobs=self.jobs("relayout_w_in"))
        return self._w_in

    def g(self, n, a):
        self.grad[n] = a[None]

    def g_half(self, n, which, a):
        (self.send if which == "send" else self.keep)[n] = _kernel_cols_to_slabs(a, name="relayout_d_in_" + which)

    def jobs(self, tag):
        out = []
        for spec in self.table.get(tag, ()):
            out += getattr(self, "_" + spec[0])(*spec[1:])
        return out

    def run(self, name, jobs):
        if jobs:
            _call(lambda: None, jobs=jobs, name=name, out_shape=[], in_specs=[], out_specs=[])()

    def _gather(self, names):
        return [_GatherJob(names, self.shards, self.whole)]

    def _rs_a(self, names):
        mats = [MATS[n] for n in names]

        def build(srcs, dsts, news, send, recv):
            x, y, c = _pos()
            return [_remote(srcs[i] if names[i] in self.send else _half_of_whole(kind, srcs[i], 1 - c, r, cc), news[i],
                            send.at[i], recv.at[i], (x, y, 1 - c))
                    for i, (_, kind, g, r, cc, _, _) in enumerate(mats)]

        def done(dsts, news):
            self.got_a.update(zip(names, news))

        return [_SwapJob(build, len(names), srcs=[self.send.get(n, self.grad.get(n)) for n in names], done=done,
                         news=[jax.ShapeDtypeStruct(_half_shape(kind, g, r, c), BF16) for _, kind, g, r, c, _, _ in mats])]

    def _rs_b1(self, names):
        mats = [MATS[n] for n in names]
        for n in names:
            if n in self.keep:
                self.half[n] = _pair_sum(MATS[n], self.core1, self.keep[n], self.got_a[n], whole=False)
            else:
                self.half[n] = _pair_sum(MATS[n], self.core1, self.grad[n], self.got_a[n])

        def build(srcs, dsts, news, send, recv):
            x, y, c = _pos()
            jx, jy, jd = 2 * (1 - x) + y, 2 * x + (1 - y), 2 * (1 - x) + (1 - y)
            nbx, nby = (1 - x, y, c), (x, 1 - y, c)
            cps = []
            for i, (_, kind, g, r, cc, _, _) in enumerate(mats):
                sub = lambda j, p: _sub_of_half(kind, srcs[i], j, p, r, cc)
                for k, (j, p, dev) in enumerate(((jx, 0, nbx), (jd, 0, nbx), (jy, 1, nby), (jd, 1, nby))):
                    cps.append(_remote(sub(j, p), news[i].at[k], send.at[4 * i + k], recv.at[4 * i + k], dev))
            return cps

        def done(dsts, news):
            self.got_b1.update(zip(names, news))

        return [_SwapJob(build, 4 * len(names), srcs=[self.half[n] for n in names], done=done,
                         news=[jax.ShapeDtypeStruct((4,) + _sub_shape(kind, r, c), BF16) for _, kind, g, r, c, _, _ in mats])]

    def _rs_b2(self, names):
        mats = [MATS[n] for n in names]
        for n in names:
            self.kept[n] = _sub_sum(MATS[n], self.idx_keep, self.half[n], self.got_b1[n], 0, F32, name="sum_keep_" + n)
            self.pass_on[n] = _sub_sum(MATS[n], self.idx_pass, self.half[n], self.got_b1[n], 1, BF16, name="sum_pass_" + n)

        def build(srcs, dsts, news, send, recv):
            x, y, c = _pos()
            cps = []
            for i in range(len(mats)):
                cps.append(_remote(srcs[i].at[0], news[i].at[0], send.at[2 * i], recv.at[2 * i], (x, 1 - y, c)))
                cps.append(_remote(srcs[i].at[1], news[i].at[1], send.at[2 * i + 1], recv.at[2 * i + 1], (1 - x, y, c)))
            return cps

        def done(dsts, news):
            self.got_b2.update(zip(names, news))

        return [_SwapJob(build, 2 * len(names), srcs=[self.pass_on[n] for n in names], done=done,
                         news=[jax.ShapeDtypeStruct((2,) + _sub_shape(kind, r, c), BF16) for _, kind, g, r, c, _, _ in mats])]

    def _rs_c(self, names):
        mats = [MATS[n] for n in names]
        parts = [_shard_sum(MATS[n], self.core1, self.kept[n], self.got_b2[n]) for n in names]

        def build(srcs, dsts, news, send, recv):
            x, y, c = _pos()
            cps = []
            for i, (_, kind, g, r, cc, _, _) in enumerate(mats):
                mine = _half_of_shard(kind, dsts[i], c, r, cc)
                cps.append(_remote(mine, mine, send.at[i], recv.at[i], (x, y, 1 - c)))
            return cps

        def done(dsts, news):
            self.gshard.update(zip(names, dsts))

        return [_SwapJob(build, len(names), dsts=parts, done=done)]

    def finish(self, n):
        if n not in self.got_a:
            self.run("rs_a_" + n, self._rs_a((n,)))
        if n not in self.got_b1:
            self.run("rs_b1_" + n, self._rs_b1((n,)))
        if n not in self.got_b2:
            self.run("rs_b2_" + n, self._rs_b2((n,)))
        if n not in self.gshard:
            self.run("rs_c_" + n, self._rs_c((n,)))
        return self.gshard[n]


TABLE = {
    "gather_w_in": (("gather", ("w_in",)),),
    "relayout_w_in": (("gather", ("w_gate",)),),
    "mm_in": (("gather", ("w_up",)),),
    "attn_fwd": (("gather", ("w_attn_br", "w_ssd_br")),),
    "ssd_fwd": (("gather", ("w_o",)),),
    "swiglu_fwd": (("gather", ("w_down",)),),
    "mm_down": (("gather", ("w_ple_gate", "w_ple_proj")),),
    "mm_de": (("rs_a", ("w_ple_proj", "w_ple_gate")),),
    "mm_d_down": (("rs_b1", ("w_ple_proj", "w_ple_gate")),),
    "swiglu_bwd": (("rs_a", ("w_down",)), ("rs_b2", ("w_ple_proj", "w_ple_gate"))),
    "mm_d_gate": (("rs_b1", ("w_down",)),),
    "mm_d_up": (("rs_b2", ("w_down",)), ("rs_c", ("w_ple_proj", "w_ple_gate")), ("rs_a", ("w_gate",))),
    "mm_df_gate": (("rs_b1", ("w_gate",)), ("rs_a", ("w_up",)), ("rs_c", ("w_down",))),
    "mm_df_up": (("rs_b2", ("w_gate",)),),
    "norm_ffn_bwd": (("rs_c", ("w_gate",)),),
    "mm_dmerged": (("rs_a", ("w_o",)),),
    "mm_dyn": (("rs_a", ("w_attn_br", "w_ssd_br")),),
    "attn_bwd": (("rs_b1", ("w_up",)),),
    "gated_norm_bwd": (("rs_b2", ("w_up",)),),
    "ssd_bwd": (("rs_b1", ("w_o", "w_attn_br", "w_ssd_br")), ("rs_c", ("w_up",))),
    "conv_bwd": (("rs_b2", ("w_o", "w_attn_br", "w_ssd_br")),),
    "mm_d_in_send": (("rs_c", ("w_o", "w_attn_br", "w_ssd_br")),),
    "mm_d_in_keep": (("rs_a", ("w_in",)),),
    "mm_du": (("rs_b1", ("w_in",)),),
    "norm_mix_bwd": (("rs_b2", ("w_in",)),),
}


NDEV = 8


def _allreduce_small(v, *, name):
    rows = v.shape[0]

    def body(v_ref, o_ref, slots, send, recv):
        x, y, c = _pos()
        me = 4 * x + 2 * y + c
        slots[me] = v_ref[...]
        cps = []
        for k in range(1, NDEV):
            peer = (_flip(x, k & 4), _flip(y, k & 2), _flip(c, k & 1))
            cp = _remote(v_ref, slots.at[me], send.at[k - 1], recv.at[k - 1], peer)
            cp.start()
            cps.append(cp)
        for cp in cps:
            cp.wait()
        acc = slots[0]
        for s in range(1, NDEV):
            acc = acc + slots[s]
        o_ref[...] = acc

    return pl.pallas_call(
        body, name=name, out_shape=jax.ShapeDtypeStruct((rows, 128), F32),
        in_specs=[pl.BlockSpec(memory_space=pltpu.VMEM)], out_specs=pl.BlockSpec(memory_space=pltpu.VMEM),
        scratch_shapes=[pltpu.VMEM((NDEV, rows, 128), F32), pltpu.SemaphoreType.DMA((NDEV - 1,)),
                        pltpu.SemaphoreType.DMA((NDEV - 1,))],
    )(v)


def _adamw(w, g, m, v, *, name, tr=None, tc=None, jobs=()):
    r, c = w.shape
    tr = r if tr is None else tr
    c1 = 1.0 / (1.0 - B1 ** STEP)
    c2 = 1.0 / (1.0 - B2 ** STEP)

    def body(w_ref, g_ref, m_ref, v_ref, d_ref, mo_ref, vo_ref):
        gv = g_ref[...]
        mn = B1 * m_ref[...] + (1.0 - B1) * gv
        vn = B2 * v_ref[...] + (1.0 - B2) * (gv * gv)
        mo_ref[...] = mn
        vo_ref[...] = vn
        d_ref[...] = -LR * ((mn * c1) / (jnp.sqrt(vn * c2) + AEPS) + WD * w_ref[...])

    if tc is None:
        blk, grid = pl.BlockSpec((tr, c), lambda i: (i, 0)), (r // tr,)
    else:
        blk, grid = pl.BlockSpec((r, tc), lambda i: (0, i)), (c // tc,)
    o = jax.ShapeDtypeStruct((r, c), F32)
    return _call(
        body, jobs=jobs, name=name, out_shape=(o, o, o), grid=grid, in_specs=[blk] * 4, out_specs=(blk, blk, blk),
        compiler_params=_cp(("parallel",)),
    )(w, g, m, v)


WEIGHTS = ("g_mix", "w_in", "conv_w", "conv_b", "dt_bias", "a_log", "d_skip", "g_ssd", "sinks", "w_attn_br", "w_ssd_br",
           "w_o", "g_ffn", "w_gate", "w_up", "w_down", "g_ple", "w_ple_gate", "w_ple_proj", "g_final")
BIG = {
    "w_gate": 256, "w_up": 256, "w_down": 128, "w_ssd_br": 128, "w_o": 128, "w_ple_gate": 128, "w_attn_br": 256,
    "w_ple_proj": 256, "w_in": None,
}
SMALL = tuple(n for n in WEIGHTS if n not in BIG)


def _pack_small(parts):
    rows = []
    for a in parts:
        a = a.reshape(-1)
        rows.append(jnp.pad(a, (0, -a.shape[0] % 128)).reshape(-1, 128))
    out = jnp.concatenate(rows, axis=0)
    return jnp.pad(out, ((0, -out.shape[0] % 8), (0, 0)))


def _unpack_small(packed, shapes):
    out, r = [], 0
    for s in shapes:
        n = int(np.prod(s))
        nr = -(-n // 128)
        out.append(packed[r:r + nr].reshape(-1)[:n].reshape(s))
        r += nr
    return out


def kernel(x, p, positions, g_mix, w_in, conv_w, conv_b, dt_bias, a_log, d_skip, g_ssd, sinks, w_attn_br, w_ssd_br, w_o, g_ffn, w_gate, w_up, w_down, g_ple, w_ple_gate, w_ple_proj, g_final, loss_target, m_g_mix, m_w_in, m_conv_w, m_conv_b, m_dt_bias, m_a_log, m_d_skip, m_g_ssd, m_sinks, m_w_attn_br, m_w_ssd_br, m_w_o, m_g_ffn, m_w_gate, m_w_up, m_w_down, m_g_ple, m_w_ple_gate, m_w_ple_proj, m_g_final, v_g_mix, v_w_in, v_conv_w, v_conv_b, v_dt_bias, v_a_log, v_d_skip, v_g_ssd, v_sinks, v_w_attn_br, v_w_ssd_br, v_w_o, v_g_ffn, v_w_gate, v_w_up, v_w_down, v_g_ple, v_w_ple_gate, v_w_ple_proj, v_g_final):
    w = dict(zip(WEIGHTS, (g_mix, w_in, conv_w, conv_b, dt_bias, a_log, d_skip, g_ssd, sinks, w_attn_br, w_ssd_br, w_o,
                           g_ffn, w_gate, w_up, w_down, g_ple, w_ple_gate, w_ple_proj, g_final)))
    m = dict(zip(WEIGHTS, (m_g_mix, m_w_in, m_conv_w, m_conv_b, m_dt_bias, m_a_log, m_d_skip, m_g_ssd, m_sinks, m_w_attn_br,
                           m_w_ssd_br, m_w_o, m_g_ffn, m_w_gate, m_w_up, m_w_down, m_g_ple, m_w_ple_gate, m_w_ple_proj,
                           m_g_final)))
    v = dict(zip(WEIGHTS, (v_g_mix, v_w_in, v_conv_w, v_conv_b, v_dt_bias, v_a_log, v_d_skip, v_g_ssd, v_sinks, v_w_attn_br,
                           v_w_ssd_br, v_w_o, v_g_ffn, v_w_gate, v_w_up, v_w_down, v_g_ple, v_w_ple_gate, v_w_ple_proj,
                           v_g_final)))
    xi, yi, ci = _pos()
    chip = 2 * xi + yi
    t = x.shape[1]
    cshard = CONV // NCHIP

    shards = {n: w[n].astype(BF16) for n in MATS}
    shards["w_in"] = jnp.pad(shards["w_in"], ((0, 0), (0, 0), (0, SLAB_PAD - SLAB)))
    plan = _Plan(shards, TABLE)
    plan.run("gather_w_in", plan.jobs("gather_w_in"))
    placed = lax.dynamic_update_slice(jnp.zeros((CW, CONV), F32), w["conv_w"][0], (0, chip * cshard))
    conv_whole = _allreduce_small(jnp.where(ci == 0, placed, 0.0).reshape(-1, 128), name="gather_conv_w").reshape(CW, CONV)

    small = {n: w[n] for n in ("g_mix", "conv_b", "dt_bias", "a_log", "d_skip", "g_ssd", "sinks", "g_ffn", "g_ple", "g_final")}
    small["conv_w"] = conv_whole
    loss8, grad_x, gs = _local_step(x[0], p[0, 0], positions, loss_target[0], small, plan)

    order = ("g_mix", "conv_b", "dt_bias", "a_log", "d_skip", "g_ssd", "sinks", "g_ffn", "g_ple", "g_final", "conv_w")
    summed = _allreduce_small(_pack_small([loss8[0, :1]] + [gs[n] for n in order]), name="sum_small")
    parts = _unpack_small(summed, [(1,)] + [w[n].shape for n in order[:-1]] + [(CW, CONV)])
    loss = parts[0][0]
    grad = dict(zip(order, parts[1:]))
    grad["conv_w"] = lax.dynamic_slice(grad["conv_w"], (0, chip * cshard), (CW, cshard))[None]

    delta, new_m, new_v = {}, {}, {}
    for n, tr in BIG.items():
        grad[n] = plan.finish(n)[:, :, :w[n].shape[2]]
        if n == "w_in":
            d_, m_, v_ = _adamw(w[n][0].T, grad[n][0].T, m[n][0].T, v[n][0].T, tc=128, name="adamw_" + n)
            d_, m_, v_ = d_.T, m_.T, v_.T
        else:
            d_, m_, v_ = _adamw(w[n][0], grad[n][0], m[n][0], v[n][0], tr=tr, name="adamw_" + n)
        delta[n], new_m[n], new_v[n] = d_[None], m_[None], v_[None]
    shapes = [w[n].shape for n in SMALL]
    d_, m_, v_ = _adamw(_pack_small([w[n] for n in SMALL]), _pack_small([grad[n] for n in SMALL]),
                        _pack_small([m[n] for n in SMALL]), _pack_small([v[n] for n in SMALL]), tr=None, name="adamw_small")
    for n, a, b, c_ in zip(SMALL, _unpack_small(d_, shapes), _unpack_small(m_, shapes), _unpack_small(v_, shapes)):
        delta[n], new_m[n], new_v[n] = a, b, c_

    return (loss, grad_x[None], *[grad[n] for n in WEIGHTS], *[delta[n] for n in WEIGHTS],
            *[new_m[n] for n in WEIGHTS], *[new_v[n] for n in WEIGHTS])
```

```python
import functools

import jax
import jax.numpy as jnp
import numpy as np
from jax import lax
from jax.experimental import pallas as pl
from jax.experimental.pallas import tpu as pltpu

F32 = jnp.float32
BF16 = jnp.bfloat16
MESH = pl.DeviceIdType.MESH

D = 2048
HD = 64
NQH = 16
NKV = 4
QD = NQH * HD
KVD = NKV * HD
DI = 2048
NH = 32
NG = 4
NS = 128
CW = 4
L = 128
CONV = DI + 2 * NG * NS
FFN = 5632
PLE = 256
IN_DIM = QD + 2 * KVD + DI + CONV + NH + 2 * D
EPS = 1e-6
SSM_EPS = 1e-5
ROPE_THETA = 10000.0
LR, B1, B2, AEPS, WD, STEP = 0.001, 0.9, 0.999, 1e-08, 0.01, 10

O_GA, O_GS, O_Z, O_XBC, O_Q, O_K, O_V, O_DT = 0, 2048, 4096, 6144, 9216, 10240, 10496, 10752
DT_PAD = 512
NP = O_DT + DT_PAD
R_Q, R_K, R_V, R_Z, R_XBC, R_DT, R_GA, R_GS = 0, 1024, 1280, 1536, 3584, 6656, 6688, 8736

NCHIP = 4
VMEM_LIMIT = 52 * 1024 * 1024
NEG = -1e30


def _cp(sem=None):
    return pltpu.CompilerParams(dimension_semantics=sem, vmem_limit_bytes=VMEM_LIMIT)


def _dot(a, b):
    return lax.dot_general(a, b, (((1,), (0,)), ((), ())), preferred_element_type=F32)


def _dot_nt(a, b):
    return lax.dot_general(a, b, (((1,), (1,)), ((), ())), preferred_element_type=F32)


def _dot_tn(a, b):
    return lax.dot_general(a, b, (((0,), (0,)), ((), ())), preferred_element_type=F32)


def _sigmoid(x):
    return 1.0 / (1.0 + jnp.exp(-x))


def _bf16_dot(dot, da, db):
    @jax.custom_vjp
    def f(a, b):
        return dot(a.astype(BF16), b.astype(BF16))

    def fwd(a, b):
        return f(a, b), (a.astype(BF16), b.astype(BF16))

    def bwd(res, g):
        a, b = res
        g = g.astype(BF16)
        return da(g, a, b), db(g, a, b)

    f.defvjp(fwd, bwd)
    return f


_bdot = _bf16_dot(_dot, lambda g, a, b: _dot_nt(g, b), lambda g, a, b: _dot_tn(a, g))
_bdot_nt = _bf16_dot(_dot_nt, lambda g, a, b: _dot(g, b), lambda g, a, b: _dot_tn(g, a))
_bdot_tn = _bf16_dot(_dot_tn, lambda g, a, b: _dot_nt(b, g), lambda g, a, b: _dot(a, g))


ANY = pl.BlockSpec(memory_space=pl.ANY)


class _Job:
    srcs, dsts, news, scratch = (), (), (), ()
    has_mid = False

    def start(self, srcs, dsts, news, sems):
        raise NotImplementedError

    def mid(self, srcs, dsts, news, sems):
        pass

    def late(self, srcs, dsts, news, sems):
        pass

    def finish(self, srcs, dsts, news, sems):
        raise NotImplementedError

    def done(self, dsts, news):
        pass


def _call(body, *, jobs=(), name, out_shape, in_specs, out_specs, grid=(), scratch_shapes=(), compiler_params=None,
          aliases=None):
    jobs = [j for j in jobs if j is not None]
    aliases = dict(aliases or {})
    if not jobs:
        return pl.pallas_call(body, name=name, out_shape=out_shape, in_specs=in_specs, out_specs=out_specs, grid=grid,
                              scratch_shapes=scratch_shapes, compiler_params=compiler_params,
                              input_output_aliases=aliases)
    single = not isinstance(out_shape, (tuple, list))
    outs = [out_shape] if single else list(out_shape)
    ospecs = [out_specs] if single else list(out_specs)
    n_in, n_out, n_scr = len(in_specs), len(outs), len(scratch_shapes)
    srcs = [a for j in jobs for a in j.srcs]
    dsts = [a for j in jobs for a in j.dsts]
    news = [a for j in jobs for a in j.news]
    sems = [a for j in jobs for a in j.scratch]

    def wrapped(*refs):
        pos = n_in + len(srcs) + len(dsts)
        ins, jsrc = refs[:n_in], refs[n_in:n_in + len(srcs)]
        o_refs = refs[pos:pos + n_out]
        pos += n_out
        jdst, jnew = refs[pos:pos + len(dsts)], refs[pos + len(dsts):pos + len(dsts) + len(news)]
        pos += len(dsts) + len(news)
        scr, jsem = refs[pos:pos + n_scr], refs[pos + n_scr:]

        def run(which):
            a = b = c = d = 0
            for j in jobs:
                getattr(j, which)(jsrc[a:a + len(j.srcs)], jdst[b:b + len(j.dsts)], jnew[c:c + len(j.news)],
                                  jsem[d:d + len(j.scratch)])
                a, b, c, d = a + len(j.srcs), b + len(j.dsts), c + len(j.news), d + len(j.scratch)

        if not grid:
            run("start")
            run("mid")
            run("late")
            body(*ins, *o_refs, *scr)
            run("finish")
            return
        step = functools.reduce(lambda acc, a: acc * grid[a] + pl.program_id(a), range(len(grid)), 0)
        steps = int(np.prod(grid))
        pl.when(step == 0)(lambda: run("start"))
        if any(j.has_mid for j in jobs):
            pl.when(step == steps // 3)(lambda: run("mid"))
            pl.when(step == (2 * steps) // 3)(lambda: run("late"))
        body(*ins, *o_refs, *scr)
        pl.when(step == steps - 1)(lambda: run("finish"))

    call = pl.pallas_call(
        wrapped, name=name,
        out_shape=outs + [jax.ShapeDtypeStruct(a.shape, a.dtype) for a in dsts] + news,
        in_specs=list(in_specs) + [ANY] * (len(srcs) + len(dsts)),
        out_specs=ospecs + [ANY] * (len(dsts) + len(news)),
        grid=grid, scratch_shapes=list(scratch_shapes) + sems,
        input_output_aliases={**aliases, **{n_in + len(srcs) + i: n_out + i for i in range(len(dsts))}},
        compiler_params=_cp(("arbitrary",) * len(grid) if grid else None))

    def run_call(*args):
        res = call(*args, *srcs, *dsts)
        b, c = n_out, n_out + len(dsts)
        for j in jobs:
            j.done(res[b:b + len(j.dsts)], res[c:c + len(j.news)])
            b, c = b + len(j.dsts), c + len(j.news)
        return res[0] if single else tuple(res[:n_out])

    return run_call


def _matmul(a, b, *, ta=False, tb=False, out_dtype=F32, add=None, tm, tn, tk, name, jobs=()):
    k, m = a.shape if ta else a.shape[::-1]
    n = b.shape[0] if tb else b.shape[1]
    assert (b.shape[1] if tb else b.shape[0]) == k and not (ta and tb)
    assert m % tm == 0 and n % tn == 0 and k % tk == 0, (name, a.shape, b.shape)
    nk = k // tk
    has_add = add is not None

    def body(*refs):
        a_ref, b_ref = refs[0], refs[1]
        add_ref = refs[2] if has_add else None
        o_ref = refs[3] if has_add else refs[2]
        av = a_ref[...].astype(BF16)
        bv = b_ref[...].astype(BF16)
        part = _dot_tn(av, bv) if ta else _dot_nt(av, bv) if tb else _dot(av, bv)

        def finish(r):
            if has_add:
                r = r + add_ref[...]
            o_ref[...] = r.astype(out_dtype)

        if nk == 1:
            finish(part)
        elif out_dtype == F32:
            kk = pl.program_id(2)
            pl.when(kk == 0)(lambda: finish(part))

            @pl.when(kk > 0)
            def _():
                o_ref[...] += part
        else:
            acc_ref = refs[-1]
            kk = pl.program_id(2)

            @pl.when(kk == 0)
            def _():
                acc_ref[...] = part

            @pl.when(kk > 0)
            def _():
                acc_ref[...] += part

            @pl.when(kk == nk - 1)
            def _():
                finish(acc_ref[...])

    in_specs = [pl.BlockSpec((tk, tm), lambda i, j, kk: (kk, i)) if ta else pl.BlockSpec((tm, tk), lambda i, j, kk: (i, kk)),
                pl.BlockSpec((tn, tk), lambda i, j, kk: (j, kk)) if tb
                else pl.BlockSpec((tk, tn), lambda i, j, kk: (kk, j))]
    args = [a, b]
    if has_add:
        in_specs.append(pl.BlockSpec((tm, tn), lambda i, j, kk: (i, j)))
        args.append(add)
    return _call(
        body, jobs=jobs, name=name,
        out_shape=jax.ShapeDtypeStruct((m, n), out_dtype),
        grid=(m // tm, n // tn, nk),
        in_specs=in_specs,
        out_specs=pl.BlockSpec((tm, tn), lambda i, j, kk: (i, j)),
        scratch_shapes=[pltpu.VMEM((tm, tn), F32)] if nk > 1 and out_dtype != F32 else [],
        compiler_params=_cp(("parallel", "parallel", "arbitrary")),
    )(*args)


ROWS = 256


def _rmsnorm_fwd(x, g, *, name):
    t, d = x.shape

    def body(x_ref, g_ref, o_ref):
        xv = x_ref[...]
        r = lax.rsqrt(jnp.mean(xv * xv, axis=-1, keepdims=True) + EPS)
        o_ref[...] = (xv * r * g_ref[...]).astype(BF16)

    return pl.pallas_call(
        body, name=name, out_shape=jax.ShapeDtypeStruct((t, d), BF16), grid=(t // ROWS,),
        in_specs=[pl.BlockSpec((ROWS, d), lambda i: (i, 0)), pl.BlockSpec((1, d), lambda i: (0, 0))],
        out_specs=pl.BlockSpec((ROWS, d), lambda i: (i, 0)), compiler_params=_cp(("parallel",)),
    )(x, g)


def _rmsnorm_bwd(x, g, dy, dres, *, name, jobs=()):
    t, d = x.shape

    def body(x_ref, g_ref, dy_ref, dres_ref, dx_ref, dxb_ref, dg_ref):
        xv = x_ref[...]
        r = lax.rsqrt(jnp.mean(xv * xv, axis=-1, keepdims=True) + EPS)
        xh = xv * r
        dyv = dy_ref[...]
        dxh = dyv * g_ref[...]
        dx = r * (dxh - xh * jnp.mean(dxh * xh, axis=-1, keepdims=True))
        tot = dres_ref[...] + dx
        dx_ref[...] = tot
        dxb_ref[...] = tot.astype(BF16)

        @pl.when(pl.program_id(0) == 0)
        def _():
            dg_ref[...] = jnp.zeros_like(dg_ref)

        dg_ref[...] += jnp.broadcast_to(jnp.sum(dyv * xh, axis=0, keepdims=True), dg_ref.shape)

    row = pl.BlockSpec((ROWS, d), lambda i: (i, 0))
    return _call(
        body, jobs=jobs, name=name,
        out_shape=(jax.ShapeDtypeStruct((t, d), F32), jax.ShapeDtypeStruct((t, d), BF16),
                   jax.ShapeDtypeStruct((8, d), F32)),
        grid=(t // ROWS,),
        in_specs=[row, pl.BlockSpec((1, d), lambda i: (0, 0)), row, row],
        out_specs=(row, row, pl.BlockSpec((8, d), lambda i: (0, 0))),
        compiler_params=_cp(("arbitrary",)),
    )(x, g, dy, dres)


def _final(h2, pgl, pp, target, g_final, *, name):
    t, d = h2.shape

    def body(h2_ref, pgl_ref, pp_ref, tg_ref, g_ref, dh3_ref, dpgl_ref, dpp_ref, loss_ref, dg_ref):
        s = _sigmoid(pgl_ref[...])
        ppv = pp_ref[...]
        h3 = h2_ref[...] + s * ppv
        r = lax.rsqrt(jnp.mean(h3 * h3, axis=-1, keepdims=True) + EPS)
        xh = h3 * r
        gv = g_ref[...]
        err = xh * gv - tg_ref[...]
        dyv = err * (1.0 / d)
        dxh = dyv * gv
        dh3 = r * (dxh - xh * jnp.mean(dxh * xh, axis=-1, keepdims=True))
        dh3_ref[...] = dh3
        dpp_ref[...] = (dh3 * s).astype(BF16)
        dpgl_ref[...] = (dh3 * ppv * s * (1.0 - s)).astype(BF16)

        @pl.when(pl.program_id(0) == 0)
        def _():
            loss_ref[...] = jnp.zeros_like(loss_ref)
            dg_ref[...] = jnp.zeros_like(dg_ref)

        part = 0.5 * jnp.sum(jnp.mean(err * err, axis=-1, keepdims=True), axis=0, keepdims=True)
        loss_ref[...] += jnp.broadcast_to(part, loss_ref.shape)
        dg_ref[...] += jnp.broadcast_to(jnp.sum(dyv * xh, axis=0, keepdims=True), dg_ref.shape)

    row = pl.BlockSpec((ROWS, d), lambda i: (i, 0))
    return pl.pallas_call(
        body, name=name,
        out_shape=(jax.ShapeDtypeStruct((t, d), F32), jax.ShapeDtypeStruct((t, d), BF16),
                   jax.ShapeDtypeStruct((t, d), BF16), jax.ShapeDtypeStruct((8, 128), F32),
                   jax.ShapeDtypeStruct((8, d), F32)),
        grid=(t // ROWS,),
        in_specs=[row, row, row, row, pl.BlockSpec((1, d), lambda i: (0, 0))],
        out_specs=(row, row, row, pl.BlockSpec((8, 128), lambda i: (0, 0)), pl.BlockSpec((8, d), lambda i: (0, 0))),
        compiler_params=_cp(("arbitrary",)),
    )(h2, pgl, pp, target, g_final)


def _merge_fwd(proj, out_a, out_s, *, name):
    t = proj.shape[0]

    def body(ga_ref, gs_ref, a_ref, s_ref, o_ref):
        o_ref[...] = (_sigmoid(ga_ref[...]) * a_ref[...] + _sigmoid(gs_ref[...]) * s_ref[...]).astype(BF16)

    row = pl.BlockSpec((ROWS, D), lambda i: (i, 0))
    return pl.pallas_call(
        body, name=name, out_shape=jax.ShapeDtypeStruct((t, D), BF16), grid=(t // ROWS,),
        in_specs=[pl.BlockSpec((ROWS, D), lambda i: (i, O_GA // D)), pl.BlockSpec((ROWS, D), lambda i: (i, O_GS // D)),
                  row, row],
        out_specs=row, compiler_params=_cp(("parallel",)),
    )(proj, proj, out_a, out_s)


def _merge_bwd(proj, out_a, out_s, dmerged, *, name):
    t = proj.shape[0]
    assert O_GA == 0 and O_GS == D

    def body(ga_ref, gs_ref, a_ref, s_ref, dm_ref, da_ref, ds_ref, dp_ref):
        sa = _sigmoid(ga_ref[...])
        ss = _sigmoid(gs_ref[...])
        dm = dm_ref[...]
        da_ref[...] = (dm * sa).astype(BF16)
        ds_ref[...] = (dm * ss).astype(BF16)
        dp_ref[:, :D] = (dm * a_ref[...] * sa * (1.0 - sa)).astype(BF16)
        dp_ref[:, D:] = (dm * s_ref[...] * ss * (1.0 - ss)).astype(BF16)

    row = pl.BlockSpec((ROWS, D), lambda i: (i, 0))
    o = jax.ShapeDtypeStruct((t, D), BF16)
    return pl.pallas_call(
        body, name=name, out_shape=(o, o, jax.ShapeDtypeStruct((t, NP), BF16)), grid=(t // ROWS,),
        in_specs=[pl.BlockSpec((ROWS, D), lambda i: (i, O_GA // D)), pl.BlockSpec((ROWS, D), lambda i: (i, O_GS // D)),
                  row, row, row],
        out_specs=(row, row, pl.BlockSpec((ROWS, 2 * D), lambda i: (i, 0))), compiler_params=_cp(("parallel",)),
    )(proj, proj, out_a, out_s, dmerged)


def _swiglu_fwd(f, w_gate, w_up, *, name, tn=512, jobs=()):
    t, d = f.shape
    n = w_gate.shape[1]

    def body(f_ref, wg_ref, wu_ref, g_ref, u_ref, a_ref):
        fv = f_ref[...]
        g = _dot(fv, wg_ref[...])
        u = _dot(fv, wu_ref[...])
        g_ref[...] = g.astype(BF16)
        u_ref[...] = u.astype(BF16)
        a_ref[...] = (g * _sigmoid(g) * u).astype(BF16)

    col = pl.BlockSpec((t, tn), lambda j: (0, j))
    wcol = pl.BlockSpec((d, tn), lambda j: (0, j))
    return _call(
        body, jobs=jobs, name=name,
        out_shape=(jax.ShapeDtypeStruct((t, n), BF16), jax.ShapeDtypeStruct((t, n), BF16),
                   jax.ShapeDtypeStruct((t, n), BF16)),
        grid=(n // tn,),
        in_specs=[pl.BlockSpec((t, d), lambda j: (0, 0)), wcol, wcol],
        out_specs=(col, col, col), compiler_params=_cp(("parallel",)),
    )(f, w_gate, w_up)


def _swiglu_bwd(dh, w_down, gate, up, *, name, tn=512, jobs=()):
    t, d = dh.shape
    n = w_down.shape[0]

    def body(dh_ref, w_ref, g_ref, u_ref, dg_ref, du_ref):
        da = _dot_nt(dh_ref[...], w_ref[...])
        g = g_ref[...].astype(F32)
        s = _sigmoid(g)
        du_ref[...] = (da * g * s).astype(BF16)
        dg_ref[...] = (da * u_ref[...].astype(F32) * s * (1.0 + g * (1.0 - s))).astype(BF16)

    col = pl.BlockSpec((t, tn), lambda j: (0, j))
    o = jax.ShapeDtypeStruct((t, n), BF16)
    return _call(
        body, jobs=jobs, name=name, out_shape=(o, o), grid=(n // tn,),
        in_specs=[pl.BlockSpec((t, d), lambda j: (0, 0)), pl.BlockSpec((tn, d), lambda j: (j, 0)), col, col],
        out_specs=(col, col), compiler_params=_cp(("parallel",)),
    )(dh, w_down, gate, up)


def _gated_norm_fwd(y_pre, proj, g_ssd, *, name):
    t = y_pre.shape[0]

    def body(y_ref, z_ref, g_ref, o_ref):
        z = z_ref[...]
        v = y_ref[...] * z * _sigmoid(z)
        r = lax.rsqrt(jnp.mean(v * v, axis=-1, keepdims=True) + SSM_EPS)
        o_ref[...] = (v * r * g_ref[...]).astype(BF16)

    row = pl.BlockSpec((ROWS, DI), lambda i: (i, 0))
    return pl.pallas_call(
        body, name=name, out_shape=jax.ShapeDtypeStruct((t, DI), BF16), grid=(t // ROWS,),
        in_specs=[row, pl.BlockSpec((ROWS, DI), lambda i: (i, O_Z // DI)), pl.BlockSpec((1, DI), lambda i: (0, 0))],
        out_specs=row, compiler_params=_cp(("parallel",)),
    )(y_pre, proj, g_ssd)


def _gated_norm_bwd(y_pre, proj, g_ssd, dyn, dproj, *, name, jobs=()):
    t = y_pre.shape[0]

    def body(y_ref, z_ref, g_ref, dyn_ref, _, dy_ref, dz_ref, dg_ref):
        z = z_ref[...]
        s = _sigmoid(z)
        sz = z * s
        yv = y_ref[...]
        v = yv * sz
        r = lax.rsqrt(jnp.mean(v * v, axis=-1, keepdims=True) + SSM_EPS)
        vh = v * r
        dn = dyn_ref[...]
        dvh = dn * g_ref[...]
        dv = r * (dvh - vh * jnp.mean(dvh * vh, axis=-1, keepdims=True))
        dy_ref[...] = dv * sz
        dz_ref[...] = (dv * yv * s * (1.0 + z * (1.0 - s))).astype(BF16)

        @pl.when(pl.program_id(0) == 0)
        def _():
            dg_ref[...] = jnp.zeros_like(dg_ref)

        dg_ref[...] += jnp.broadcast_to(jnp.sum(dn * vh, axis=0, keepdims=True), dg_ref.shape)

    row = pl.BlockSpec((ROWS, DI), lambda i: (i, 0))
    return _call(
        body, jobs=jobs, name=name,
        out_shape=(jax.ShapeDtypeStruct((t, DI), F32), jax.ShapeDtypeStruct(dproj.shape, BF16),
                   jax.ShapeDtypeStruct((8, DI), F32)),
        grid=(t // ROWS,),
        in_specs=[row, pl.BlockSpec((ROWS, DI), lambda i: (i, O_Z // DI)), pl.BlockSpec((1, DI), lambda i: (0, 0)), row, ANY],
        out_specs=(row, pl.BlockSpec((ROWS, DI), lambda i: (i, O_Z // DI)), pl.BlockSpec((8, DI), lambda i: (0, 0))),
        compiler_params=_cp(("arbitrary",)), aliases={4: 1},
    )(y_pre, proj, g_ssd, dyn, dproj)


CONV_TC = 512


def _shift_down(x, s, row):
    if s == 0:
        return x
    return jnp.where(row >= s, pltpu.roll(x, s, 0), 0.0)


def _shift_up(x, s, row, t):
    if s == 0:
        return x
    return jnp.where(row < t - s, pltpu.roll(x, t - s, 0), 0.0)


def _conv_fwd(proj, conv_w, conv_b, *, name):
    t = proj.shape[0]

    def body(x_ref, w_ref, b_ref, o_ref):
        x = x_ref[...]
        row = lax.broadcasted_iota(jnp.int32, x.shape, 0)
        pre = jnp.broadcast_to(b_ref[...], x.shape)
        for k in range(CW):
            pre = pre + w_ref[k:k + 1, :] * _shift_down(x, CW - 1 - k, row)
        o_ref[...] = pre * _sigmoid(pre)

    return pl.pallas_call(
        body, name=name, out_shape=jax.ShapeDtypeStruct((t, CONV), F32), grid=(CONV // CONV_TC,),
        in_specs=[pl.BlockSpec((t, CONV_TC), lambda j: (0, O_XBC // CONV_TC + j)),
                  pl.BlockSpec((CW, CONV_TC), lambda j: (0, j)), pl.BlockSpec((1, CONV_TC), lambda j: (0, j))],
        out_specs=pl.BlockSpec((t, CONV_TC), lambda j: (0, j)), compiler_params=_cp(("parallel",)),
    )(proj, conv_w, conv_b)


def _conv_bwd(proj, conv_w, conv_b, dxs, db, dc, dproj, *, name, jobs=()):
    t = proj.shape[0]
    nx = DI // CONV_TC
    assert NG * NS == CONV_TC

    def body(x_ref, w_ref, b_ref, dxs_ref, db_ref, dc_ref, _, dx_ref, dw_ref, dbias_ref):
        j = pl.program_id(0)
        x = x_ref[...]
        row = lax.broadcasted_iota(jnp.int32, x.shape, 0)
        xs = [_shift_down(x, CW - 1 - k, row) for k in range(CW)]
        pre = jnp.broadcast_to(b_ref[...], x.shape)
        for k in range(CW):
            pre = pre + w_ref[k:k + 1, :] * xs[k]
        s = _sigmoid(pre)
        da = jnp.where(j < nx, dxs_ref[...], jnp.where(j == nx, db_ref[...], dc_ref[...]))
        dpre = da * s * (1.0 + pre * (1.0 - s))
        dx = jnp.zeros_like(x)
        row8 = lax.broadcasted_iota(jnp.int32, dw_ref.shape, 0)
        dw = jnp.zeros(dw_ref.shape, F32)
        for k in range(CW):
            dx = dx + w_ref[k:k + 1, :] * _shift_up(dpre, CW - 1 - k, row, t)
            dw = dw + jnp.where(row8 == k, jnp.sum(dpre * xs[k], axis=0, keepdims=True), 0.0)
        dx_ref[...] = dx.astype(BF16)
        dw_ref[...] = dw
        dbias_ref[...] = jnp.broadcast_to(jnp.sum(dpre, axis=0, keepdims=True), dbias_ref.shape)

    col8 = pl.BlockSpec((8, CONV_TC), lambda j: (0, j))
    xbc = pl.BlockSpec((t, CONV_TC), lambda j: (0, O_XBC // CONV_TC + j))
    whole = pl.BlockSpec((t, CONV_TC), lambda j: (0, 0))
    return _call(
        body, jobs=jobs, name=name,
        out_shape=(jax.ShapeDtypeStruct(dproj.shape, BF16), jax.ShapeDtypeStruct((8, CONV), F32),
                   jax.ShapeDtypeStruct((8, CONV), F32)),
        grid=(CONV // CONV_TC,),
        in_specs=[xbc, pl.BlockSpec((CW, CONV_TC), lambda j: (0, j)), pl.BlockSpec((1, CONV_TC), lambda j: (0, j)),
                  pl.BlockSpec((t, CONV_TC), lambda j: (0, jnp.minimum(j, nx - 1))), whole, whole, ANY],
        out_specs=(xbc, col8, col8),
        compiler_params=_cp(("arbitrary",)), aliases={6: 0},
    )(proj, conv_w, conv_b, dxs, db, dc, dproj)


def _rope_tables(positions, t):
    half = HD // 2
    inv_freq = ROPE_THETA ** (-jnp.arange(half, dtype=F32) * 2.0 / HD)
    ang = positions.reshape(t).astype(F32)[:, None] * inv_freq
    cos, sin = jnp.cos(ang), jnp.sin(ang)
    return jnp.concatenate([cos] * 4, axis=1), jnp.concatenate([-sin, sin] * 2, axis=1)


def _lane_consts():
    lane = lax.broadcasted_iota(jnp.int32, (L, 128), 1)
    return lane, (lane % HD) < (HD // 2), lane < HD


def _rope(tv, cos, sin, lo):
    return tv * cos + jnp.where(lo, pltpu.roll(tv, 128 - HD // 2, 1), pltpu.roll(tv, HD // 2, 1)) * sin


def _rope_t(dv, cos, sin, lo):
    ds = dv * sin
    return dv * cos + jnp.where(lo, pltpu.roll(ds, 128 - HD // 2, 1), pltpu.roll(ds, HD // 2, 1))


def _placed(chunk, g, half0):
    own = jnp.where(half0 if g % 2 == 0 else jnp.logical_not(half0), chunk, 0.0)
    other = pltpu.roll(own, HD, 1)
    return (own, other) if g % 2 == 0 else (other, own)


def _unplace(acc, hf, g, half0):
    v = jnp.where(half0 if hf == 0 else jnp.logical_not(half0), acc, 0.0)
    return v if hf == g % 2 else pltpu.roll(v, HD, 1)


def _attn_fwd(proj, cos, sin, sinks, *, name, jobs=()):
    t = proj.shape[0]
    nb = t // L
    scale = HD ** -0.5

    def body(sink_ref, q_ref, kc_ref, kp_ref, vc_ref, vp_ref, cc_ref, sc_ref, cp_ref, sp_ref, o_ref, lse_ref):
        i = pl.program_id(0)
        lane, lo, half0 = _lane_consts()
        cos_c, sin_c, cos_p, sin_p = cc_ref[...], sc_ref[...], cp_ref[...], sp_ref[...]
        row = lax.broadcasted_iota(jnp.int32, (L, 2 * L), 0)
        col = lax.broadcasted_iota(jnp.int32, (L, 2 * L), 1)
        valid = jnp.logical_or(jnp.logical_and(jnp.logical_and(col < L, col > row), i > 0),
                               jnp.logical_and(col >= L, col - L <= row))
        kc = [_rope(kc_ref[:, 128 * m:128 * (m + 1)], cos_c, sin_c, lo) for m in range(2)]
        kp = [_rope(kp_ref[:, 128 * m:128 * (m + 1)], cos_p, sin_p, lo) for m in range(2)]
        lse_acc = jnp.zeros((L, 128), F32)
        outs = [jnp.zeros((L, 128), F32) for _ in range(QD // 128)]
        qs = [(_rope(q_ref[:, 128 * ch:128 * (ch + 1)], cos_c, sin_c, lo) * scale).astype(BF16) for ch in range(QD // 128)]
        both = lambda prev, cur, g: [jnp.concatenate([a, b], axis=0).astype(BF16)
                                     for a, b in zip(_placed(prev, g, half0), _placed(cur, g, half0))]
        for g in range(NKV):
            sl = slice(128 * (g // 2), 128 * (g // 2 + 1))
            kv = both(kp[g // 2], kc[g // 2], g)
            vv = both(vp_ref[:, sl], vc_ref[:, sl], g)
            for r in range(NQH // NKV):
                h = g * (NQH // NKV) + r
                ch, hf = h // 2, h % 2
                s = jnp.where(valid, _dot_nt(qs[ch], kv[hf]), NEG)
                sink = sink_ref[0, h]
                mx = jnp.maximum(jnp.max(s, axis=-1, keepdims=True), sink)
                e = jnp.exp(s - mx)
                den = jnp.sum(e, axis=-1, keepdims=True) + jnp.exp(sink - mx)
                outs[ch] = outs[ch] + _dot((e * (1.0 / den)).astype(BF16), vv[hf])
                lse_acc = jnp.where(lane == h, mx + jnp.log(den), lse_acc)
        for ch in range(QD // 128):
            o_ref[:, 128 * ch:128 * (ch + 1)] = outs[ch].astype(BF16)
        lse_ref[...] = lse_acc

    prev = lambda i: jnp.maximum(i - 1, 0)
    tab_c = pl.BlockSpec((L, 128), lambda i: (i, 0))
    tab_p = pl.BlockSpec((L, 128), lambda i: (prev(i), 0))
    return _call(
        body, jobs=jobs, name=name,
        out_shape=(jax.ShapeDtypeStruct((t, QD), BF16), jax.ShapeDtypeStruct((t, 128), F32)),
        grid=(nb,),
        in_specs=[pl.BlockSpec(memory_space=pltpu.SMEM),
                  pl.BlockSpec((L, QD), lambda i: (i, O_Q // QD)),
                  pl.BlockSpec((L, KVD), lambda i: (i, O_K // KVD)), pl.BlockSpec((L, KVD), lambda i: (prev(i), O_K // KVD)),
                  pl.BlockSpec((L, KVD), lambda i: (i, O_V // KVD)), pl.BlockSpec((L, KVD), lambda i: (prev(i), O_V // KVD)),
                  tab_c, tab_c, tab_p, tab_p],
        out_specs=(pl.BlockSpec((L, QD), lambda i: (i, 0)), pl.BlockSpec((L, 128), lambda i: (i, 0))),
        compiler_params=_cp(("parallel",)),
    )(sinks, proj, proj, proj, proj, proj, cos, sin, cos, sin)


def _attn_bwd(proj, cos, sin, sinks, attn, lse, dattn, dproj, *, name, jobs=()):
    t = proj.shape[0]
    nb = t // L
    scale = HD ** -0.5

    def body(sink_ref, qi_ref, qn_ref, kc_ref, kp_ref, vc_ref, vp_ref, doi_ref, don_ref, oi_ref, on_ref,
             lsei_ref, lsen_ref, cc_ref, sc_ref, cp_ref, sp_ref, cn_ref, sn_ref, _, dqkv_ref, dsk_ref):
        i = pl.program_id(0)
        lane, lo, half0 = _lane_consts()
        half1 = jnp.logical_not(half0)
        cos_c, sin_c = cc_ref[...], sc_ref[...]
        row = lax.broadcasted_iota(jnp.int32, (L, 2 * L), 0)
        col = lax.broadcasted_iota(jnp.int32, (L, 2 * L), 1)
        valid = jnp.logical_or(jnp.logical_and(jnp.logical_and(col < L, col > row), i > 0),
                               jnp.logical_and(col >= L, col - L <= row))
        m_next = jnp.logical_and(col[:, :L] > row[:, :L], i < nb - 1)
        kc = [_rope(kc_ref[:, 128 * m:128 * (m + 1)], cos_c, sin_c, lo) for m in range(2)]
        kp = [_rope(kp_ref[:, 128 * m:128 * (m + 1)], cp_ref[...], sp_ref[...], lo) for m in range(2)]
        lse_i, lse_n = lsei_ref[...], lsen_ref[...]
        dk_acc = [jnp.zeros((L, 128), F32) for _ in range(2)]
        dv_acc = [jnp.zeros((L, 128), F32) for _ in range(2)]
        dsk_acc = jnp.zeros((1, 128), F32)
        lane1 = lax.broadcasted_iota(jnp.int32, (1, 128), 1)
        both = lambda prev, cur, g: [jnp.concatenate([a, b], axis=0).astype(BF16)
                                     for a, b in zip(_placed(prev, g, half0), _placed(cur, g, half0))]
        kvs = [both(kp[g // 2], kc[g // 2], g) for g in range(NKV)]
        vvs = [both(vp_ref[:, 128 * (g // 2):128 * (g // 2 + 1)], vc_ref[:, 128 * (g // 2):128 * (g // 2 + 1)], g)
               for g in range(NKV)]
        for ch in range(QD // 128):
            sl = slice(128 * ch, 128 * (ch + 1))
            q_i = (_rope(qi_ref[:, sl], cos_c, sin_c, lo) * scale).astype(BF16)
            q_n = (_rope(qn_ref[:, sl], cn_ref[...], sn_ref[...], lo) * scale).astype(BF16)
            q_in = jnp.concatenate([q_i, q_n], axis=0)
            do_i, do_n = doi_ref[:, sl], don_ref[:, sl]
            do_ib, do_nb = do_i.astype(BF16), do_n.astype(BF16)
            do_in = jnp.concatenate([do_ib, do_nb], axis=0)
            od_i = do_i * oi_ref[:, sl].astype(F32)
            od_n = do_n * on_ref[:, sl].astype(F32)
            dq_ch = jnp.zeros((L, 128), F32)
            for hf in range(2):
                h = 2 * ch + hf
                g = h // (NQH // NKV)
                hm = half0 if hf == 0 else half1
                kv, vv = kvs[g][hf], vvs[g][hf]
                kcv, vcv = kv[L:], vv[L:]
                dl_i = jnp.sum(jnp.where(hm, od_i, 0.0), axis=-1, keepdims=True)
                dl_n = jnp.sum(jnp.where(hm, od_n, 0.0), axis=-1, keepdims=True)
                ls_i = jnp.sum(jnp.where(lane == h, lse_i, 0.0), axis=-1, keepdims=True)
                ls_n = jnp.sum(jnp.where(lane == h, lse_n, 0.0), axis=-1, keepdims=True)
                p = jnp.where(valid, jnp.exp(_dot_nt(q_i, kv) - ls_i), 0.0)
                ds = (p * (_dot_nt(do_ib, vv) - dl_i)).astype(BF16)
                dq_ch = dq_ch + jnp.where(hm, _dot(ds, kv) * scale, 0.0)
                sink = sink_ref[0, h]
                dsk = -jnp.sum(jnp.exp(sink - ls_i) * dl_i, axis=0, keepdims=True)
                dsk_acc = dsk_acc + jnp.where(lane1 == h, dsk, 0.0)
                p_n = jnp.where(m_next, jnp.exp(_dot_nt(q_n, kcv) - ls_n), 0.0)
                ds_n = (p_n * (_dot_nt(do_nb, vcv) - dl_n)).astype(BF16)
                dv_h = _dot_tn(jnp.concatenate([p[:, L:].astype(BF16), p_n.astype(BF16)], axis=0), do_in)
                dk_h = _dot_tn(jnp.concatenate([ds[:, L:], ds_n], axis=0), q_in)
                dv_acc[g // 2] = dv_acc[g // 2] + _unplace(dv_h, hf, g, half0)
                dk_acc[g // 2] = dk_acc[g // 2] + _unplace(dk_h, hf, g, half0)
            dqkv_ref[:, sl] = _rope_t(dq_ch, cos_c, sin_c, lo).astype(BF16)
        for m in range(2):
            dqkv_ref[:, QD + 128 * m:QD + 128 * (m + 1)] = _rope_t(dk_acc[m], cos_c, sin_c, lo).astype(BF16)
            dqkv_ref[:, QD + KVD + 128 * m:QD + KVD + 128 * (m + 1)] = dv_acc[m].astype(BF16)

        @pl.when(i == 0)
        def _():
            dsk_ref[...] = jnp.zeros_like(dsk_ref)

        dsk_ref[...] += jnp.broadcast_to(dsk_acc, dsk_ref.shape)

    prev = lambda i: jnp.maximum(i - 1, 0)
    nxt = lambda i: jnp.minimum(i + 1, nb - 1)
    cur_q = pl.BlockSpec((L, QD), lambda i: (i, 0))
    nxt_q = pl.BlockSpec((L, QD), lambda i: (nxt(i), 0))
    tab = lambda f: pl.BlockSpec((L, 128), lambda i: (f(i), 0))
    ident = lambda i: i
    qkv = QD + 2 * KVD
    assert O_K == O_Q + QD and O_V == O_K + KVD and O_Q % qkv == 0
    return _call(
        body, jobs=jobs, name=name,
        out_shape=(jax.ShapeDtypeStruct(dproj.shape, BF16), jax.ShapeDtypeStruct((8, 128), F32)),
        grid=(nb,),
        in_specs=[pl.BlockSpec(memory_space=pltpu.SMEM),
                  pl.BlockSpec((L, QD), lambda i: (i, O_Q // QD)), pl.BlockSpec((L, QD), lambda i: (nxt(i), O_Q // QD)),
                  pl.BlockSpec((L, KVD), lambda i: (i, O_K // KVD)), pl.BlockSpec((L, KVD), lambda i: (prev(i), O_K // KVD)),
                  pl.BlockSpec((L, KVD), lambda i: (i, O_V // KVD)), pl.BlockSpec((L, KVD), lambda i: (prev(i), O_V // KVD)),
                  cur_q, nxt_q, cur_q, nxt_q, tab(ident), tab(nxt),
                  tab(ident), tab(ident), tab(prev), tab(prev), tab(nxt), tab(nxt), ANY],
        out_specs=(pl.BlockSpec((L, qkv), lambda i: (i, O_Q // qkv)), pl.BlockSpec((8, 128), lambda i: (0, 0))),
        compiler_params=_cp(("arbitrary",)), aliases={19: 0},
    )(sinks, proj, proj, proj, proj, proj, proj, dattn, dattn, attn, attn, lse, lse, cos, sin, cos, sin, cos, sin, dproj)


PAIRS = NH // NG // 2


def _softplus(x):
    return jnp.maximum(x, 0.0) + jnp.log(1.0 + jnp.exp(-jnp.abs(x)))


def _ssd_chunk(g, xps, dtr, bm, cm, sps, dtb, alog, dsk):
    lane = lax.broadcasted_iota(jnp.int32, (L, 128), 1)
    lane1 = lax.broadcasted_iota(jnp.int32, (1, 128), 1)
    row = lax.broadcasted_iota(jnp.int32, (L, L), 0)
    col = lax.broadcasted_iota(jnp.int32, (L, L), 1)
    rowc = lax.broadcasted_iota(jnp.int32, (128, 1), 0)
    tril = col <= row
    dt = _softplus(dtr + dtb)
    a = dt * (-jnp.exp(alog))
    a_cs = lax.dot_general(tril.astype(F32), a, (((1,), (0,)), ((), ())), precision=lax.Precision.HIGHEST,
                           preferred_element_type=F32)
    a_cst = a_cs.T
    a_last = jnp.sum(jnp.where(row == L - 1, a_cs, 0.0), axis=0, keepdims=True)
    cb = _bdot_nt(cm, bm)
    ys, snew = [], []
    for q in range(PAIRS):
        xp, sp = xps[q], sps[q]
        y_pair = jnp.zeros((L, 128), F32)
        st_pair = jnp.zeros((128, NS), F32)
        keep = jnp.zeros((128, 1), F32)
        for hh in range(2):
            h = g * 2 * PAIRS + 2 * q + hh
            hm = (lane < HD) if hh == 0 else (lane >= HD)
            rm = (rowc < HD) if hh == 0 else (rowc >= HD)
            dt_h = jnp.sum(jnp.where(lane == h, dt, 0.0), axis=1, keepdims=True)
            acs_h = jnp.sum(jnp.where(lane == h, a_cs, 0.0), axis=1, keepdims=True)
            acst_h = jnp.sum(jnp.where(row == h, a_cst, 0.0), axis=0, keepdims=True)
            al_h = jnp.sum(jnp.where(lane1 == h, a_last, 0.0), axis=1, keepdims=True)
            dsk_h = jnp.sum(jnp.where(lane1 == h, dsk, 0.0), axis=1, keepdims=True)
            decay = jnp.where(tril, jnp.exp(jnp.where(tril, acs_h - acst_h, 0.0)), 0.0)
            xh = jnp.where(hm, xp, 0.0)
            xd = xh * dt_h
            y = _bdot(cb * decay, xd)
            y = y + jnp.where(hm, _bdot_nt(cm * jnp.exp(acs_h), sp), 0.0)
            y_pair = y_pair + y + dsk_h * xh
            st_pair = st_pair + _bdot_tn(xd, bm * jnp.exp(al_h - acs_h))
            keep = keep + jnp.where(rm, jnp.exp(al_h), 0.0)
        ys.append(y_pair)
        snew.append(sp * keep + st_pair)
    return ys, snew


def _ssd_specs(t):
    nc = t // L
    xs = lambda f: pl.BlockSpec((L, 128 * PAIRS), lambda c, g: (f(c), g))
    bspec = lambda f: pl.BlockSpec((L, NS), lambda c, g: (f(c), DI // NS + g))
    cspec = lambda f: pl.BlockSpec((L, NS), lambda c, g: (f(c), DI // NS + NG + g))
    dts = lambda f: pl.BlockSpec((L, 128), lambda c, g: (f(c), O_DT // 128))
    par = pl.BlockSpec((1, 128), lambda c, g: (0, 0))
    st = lambda f: pl.BlockSpec((1, 1, PAIRS, 128, NS), lambda c, g: (f(c), g, 0, 0, 0))
    return nc, xs, bspec, cspec, dts, par, st


def _ssd_fwd(xbc_act, proj, dtb, alog, dsk, *, name, jobs=()):
    t = proj.shape[0]
    nc, xs, bspec, cspec, dts, par, st = _ssd_specs(t)
    ident = lambda c: c

    def body(x_ref, b_ref, c_ref, dt_ref, dtb_ref, al_ref, dsk_ref, y_ref, sin_ref, s_ref):
        c, g = pl.program_id(0), pl.program_id(1)

        @pl.when(c == 0)
        def _():
            s_ref[g] = jnp.zeros((PAIRS, 128, NS), F32)

        sps = [s_ref[g, q] for q in range(PAIRS)]
        for q in range(PAIRS):
            sin_ref[0, 0, q] = sps[q]
        xps = [x_ref[:, 128 * q:128 * (q + 1)] for q in range(PAIRS)]
        ys, snew = _ssd_chunk(g, xps, dt_ref[...], b_ref[...], c_ref[...], sps, dtb_ref[...], al_ref[...], dsk_ref[...])
        for q in range(PAIRS):
            y_ref[:, 128 * q:128 * (q + 1)] = ys[q]
            s_ref[g, q] = snew[q]

    return _call(
        body, jobs=jobs, name=name,
        out_shape=(jax.ShapeDtypeStruct((t, DI), F32), jax.ShapeDtypeStruct((nc, NG, PAIRS, 128, NS), F32)),
        grid=(nc, NG),
        in_specs=[xs(ident), bspec(ident), cspec(ident), dts(ident), par, par, par],
        out_specs=(pl.BlockSpec((L, 128 * PAIRS), lambda c, g: (c, g)), st(ident)),
        scratch_shapes=[pltpu.VMEM((NG, PAIRS, 128, NS), F32)],
        compiler_params=_cp(("arbitrary", "arbitrary")),
    )(xbc_act, xbc_act, xbc_act, proj, dtb, alog, dsk)


def _ssd_bwd(xbc_act, proj, dtb, alog, dsk, states, dy, dproj, *, name, jobs=()):
    t = proj.shape[0]
    nc, xs, bspec, cspec, dts, par, st = _ssd_specs(t)
    rev = lambda c: nc - 1 - c

    def body(x_ref, b_ref, c_ref, dt_ref, dtb_ref, al_ref, dsk_ref, sin_ref, dy_ref, _,
             dx_ref, db_ref, dc_ref, ddtp_ref, ddtb_ref, dal_ref, ddsk_ref, ds_ref, ddt_ref):
        c, g = pl.program_id(0), pl.program_id(1)

        @pl.when(c == 0)
        def _():
            ds_ref[g] = jnp.zeros((PAIRS, 128, NS), F32)

        @pl.when(jnp.logical_and(c == 0, g == 0))
        def _():
            ddtb_ref[...] = jnp.zeros_like(ddtb_ref)
            dal_ref[...] = jnp.zeros_like(dal_ref)
            ddsk_ref[...] = jnp.zeros_like(ddsk_ref)

        @pl.when(g == 0)
        def _():
            ddt_ref[...] = jnp.zeros_like(ddt_ref)

        sps = [sin_ref[0, 0, q] for q in range(PAIRS)]
        xps = [x_ref[:, 128 * q:128 * (q + 1)] for q in range(PAIRS)]
        _, vjp = jax.vjp(functools.partial(_ssd_chunk, g), xps, dt_ref[...], b_ref[...], c_ref[...], sps,
                         dtb_ref[...], al_ref[...], dsk_ref[...])
        dys = [dy_ref[:, 128 * q:128 * (q + 1)] for q in range(PAIRS)]
        dss = [ds_ref[g, q] for q in range(PAIRS)]
        dxps, ddt, db, dc, dsps, ddtb, dal, ddsk = vjp((dys, dss))
        for q in range(PAIRS):
            dx_ref[:, 128 * q:128 * (q + 1)] = dxps[q]
            ds_ref[g, q] = dsps[q]
        db_ref[...] = db
        dc_ref[...] = dc
        ddt_ref[...] += ddt
        ddtb_ref[...] += jnp.broadcast_to(ddtb, ddtb_ref.shape)
        dal_ref[...] += jnp.broadcast_to(dal, dal_ref.shape)
        ddsk_ref[...] += jnp.broadcast_to(ddsk, ddsk_ref.shape)

        @pl.when(g == NG - 1)
        def _():
            ddtp_ref[:, :128] = ddt_ref[...].astype(BF16)
            ddtp_ref[:, 128:] = jnp.zeros((L, DT_PAD - 128), BF16)

    acc = pl.BlockSpec((8, 128), lambda c, g: (0, 0))
    o8 = jax.ShapeDtypeStruct((8, 128), F32)
    return _call(
        body, jobs=jobs, name=name,
        out_shape=(jax.ShapeDtypeStruct((t, DI), F32), jax.ShapeDtypeStruct((t, NG * NS), F32),
                   jax.ShapeDtypeStruct((t, NG * NS), F32), jax.ShapeDtypeStruct(dproj.shape, BF16), o8, o8, o8),
        grid=(nc, NG),
        in_specs=[xs(rev), bspec(rev), cspec(rev), dts(rev), par, par, par, st(rev),
                  pl.BlockSpec((L, 128 * PAIRS), lambda c, g: (rev(c), g)), ANY],
        out_specs=(pl.BlockSpec((L, 128 * PAIRS), lambda c, g: (rev(c), g)),
                   pl.BlockSpec((L, NS), lambda c, g: (rev(c), g)), pl.BlockSpec((L, NS), lambda c, g: (rev(c), g)),
                   pl.BlockSpec((L, DT_PAD), lambda c, g: (rev(c), O_DT // DT_PAD)), acc, acc, acc),
        scratch_shapes=[pltpu.VMEM((NG, PAIRS, 128, NS), F32), pltpu.VMEM((L, 128), F32)],
        compiler_params=_cp(("arbitrary", "arbitrary")), aliases={9: 3},
    )(xbc_act, xbc_act, xbc_act, proj, dtb, alog, dsk, states, dy, dproj)


def _pad_lanes(v, n=128):
    return jnp.pad(v, ((0, 0), (0, n - v.shape[1])))


class _LocalPlan:
    core = 0

    def __init__(self, big):
        self.big, self.grad, self.halves = big, {}, {}

    def w(self, n):
        return self.big[n]

    def g(self, n, a):
        self.grad[n] = a

    def g_half(self, n, which, a):
        self.halves[which] = a
        if len(self.halves) == 2:
            self.grad[n] = jnp.concatenate([self.halves["keep"], self.halves["send"]], axis=0)

    def jobs(self, tag):
        return ()


def _local_step(x, p, positions, target, small, plan):
    t = x.shape[0]
    cos, sin = _rope_tables(positions, t)
    dtb, alog, dsk = _pad_lanes(small["dt_bias"]), _pad_lanes(small["a_log"]), _pad_lanes(small["d_skip"])
    w, jobs = plan.w, plan.jobs

    def mm(a, b, *, name, tn=512, **kw):
        return _matmul(a, b, tm=t, tn=tn, name=name, jobs=jobs(name), **kw)

    tkl = FFN // 4

    def dw(wname, a, dy, *, name, tm):
        plan.g(wname, _matmul(a, dy, ta=True, out_dtype=BF16, tm=tm, tn=512, tk=t, name=name, jobs=jobs(name)))

    u = _rmsnorm_fwd(x, small["g_mix"], name="norm_mix")
    proj = mm(u, w("w_in"), tn=1024, tk=D, name="mm_in")
    attn, lse = _attn_fwd(proj, cos, sin, small["sinks"], name="attn_fwd", jobs=jobs("attn_fwd"))
    out_a = mm(attn, w("w_attn_br"), tk=QD, name="mm_attn_br")
    xbc_act = _conv_fwd(proj, small["conv_w"], small["conv_b"], name="conv_fwd")
    y_pre, states = _ssd_fwd(xbc_act, proj, dtb, alog, dsk, name="ssd_fwd", jobs=jobs("ssd_fwd"))
    yn = _gated_norm_fwd(y_pre, proj, small["g_ssd"], name="gated_norm_fwd")
    out_s = mm(yn, w("w_ssd_br"), tk=DI, name="mm_ssd_br")
    merged = _merge_fwd(proj, out_a, out_s, name="merge_fwd")
    h1 = mm(merged, w("w_o"), add=x, tk=D, name="mm_o")
    f = _rmsnorm_fwd(h1, small["g_ffn"], name="norm_ffn")
    gate, up, act = _swiglu_fwd(f, w("w_gate"), w("w_up"), name="swiglu_fwd", jobs=jobs("swiglu_fwd"))
    h2 = mm(act, w("w_down"), add=h1, tk=tkl, name="mm_down")
    e = _rmsnorm_fwd(h2, small["g_ple"], name="norm_ple")
    pgl = mm(e, w("w_ple_gate"), tk=D, name="mm_ple_gate")
    pb = p.astype(BF16)
    pp = mm(pb, w("w_ple_proj"), tk=PLE, name="mm_ple_proj")
    dh3, dpgl, dpp, loss, dg_final = _final(h2, pgl, pp, target, small["g_final"].reshape(1, D), name="final")

    dw("w_ple_proj", pb, dpp, tm=PLE, name="mm_d_ple_proj")
    dw("w_ple_gate", e, dpgl, tm=D, name="mm_d_ple_gate")
    de = mm(dpgl, w("w_ple_gate"), tb=True, tk=D, name="mm_de")
    dh2, dh2b, dg_ple = _rmsnorm_bwd(h2, small["g_ple"], de, dh3, name="norm_ple_bwd", jobs=jobs("norm_ple_bwd"))
    dw("w_down", act, dh2b, tm=FFN // 2, name="mm_d_down")
    dgate, dup = _swiglu_bwd(dh2b, w("w_down"), gate, up, name="swiglu_bwd", jobs=jobs("swiglu_bwd"))
    dw("w_gate", f, dgate, tm=D, name="mm_d_gate")
    dw("w_up", f, dup, tm=D, name="mm_d_up")
    df = mm(dgate, w("w_gate"), tb=True, tn=1024, tk=tkl, name="mm_df_gate")
    df = mm(dup, w("w_up"), tb=True, add=df, tk=tkl, name="mm_df_up")
    dh1, dh1b, dg_ffn = _rmsnorm_bwd(h1, small["g_ffn"], df, dh2, name="norm_ffn_bwd", jobs=jobs("norm_ffn_bwd"))
    dw("w_o", merged, dh1b, tm=D, name="mm_d_o")
    dmerged = mm(dh1b, w("w_o"), tb=True, tk=D, name="mm_dmerged")
    dout_a, dout_s, dproj = _merge_bwd(proj, out_a, out_s, dmerged, name="merge_bwd")
    dw("w_attn_br", attn, dout_a, tm=QD, name="mm_d_attn_br")
    dw("w_ssd_br", yn, dout_s, tm=DI, name="mm_d_ssd_br")
    dattn = mm(dout_a, w("w_attn_br"), tb=True, tk=D, name="mm_dattn")
    dyn = mm(dout_s, w("w_ssd_br"), tb=True, tk=D, name="mm_dyn")
    dproj, dsinks = _attn_bwd(proj, cos, sin, small["sinks"], attn, lse, dattn, dproj, name="attn_bwd",
                              jobs=jobs("attn_bwd"))
    dy_pre, dproj, dg_ssd = _gated_norm_bwd(y_pre, proj, small["g_ssd"], dyn, dproj, name="gated_norm_bwd",
                                            jobs=jobs("gated_norm_bwd"))
    dxs, db, dc, dproj, ddtb, dalog, ddsk = _ssd_bwd(xbc_act, proj, dtb, alog, dsk, states, dy_pre, dproj, name="ssd_bwd",
                                                     jobs=jobs("ssd_bwd"))
    dproj, dconv_w, dconv_b = _conv_bwd(proj, small["conv_w"], small["conv_b"], dxs, db, dc, dproj, name="conv_bwd",
                                        jobs=jobs("conv_bwd"))
    for which, h in (("send", 1 - plan.core), ("keep", plan.core)):
        uh = lax.dynamic_slice_in_dim(u, h * (D // 2), D // 2, axis=1)
        name = "mm_d_in_" + which
        plan.g_half("w_in", which, _matmul(uh, dproj, ta=True, out_dtype=BF16, tm=D // 2, tn=1024, tk=t, name=name,
                                           jobs=jobs(name)))
    du = mm(dproj, w("w_in"), tb=True, tn=1024, tk=tkl, name="mm_du")
    grad_x, _, dg_mix = _rmsnorm_bwd(x, small["g_mix"], du, dh1, name="norm_mix_bwd", jobs=jobs("norm_mix_bwd"))

    gs = {
        "g_mix": dg_mix[:1], "conv_w": dconv_w[:CW], "conv_b": dconv_b[:1], "dt_bias": ddtb[:1, :NH],
        "a_log": dalog[:1, :NH], "d_skip": ddsk[:1, :NH], "g_ssd": dg_ssd[:1], "sinks": dsinks[:1, :NQH],
        "g_ffn": dg_ffn[:1], "g_ple": dg_ple[:1], "g_final": dg_final[0],
    }
    return loss, grad_x, gs


def _to_kernel_cols(w):
    seg = lambda o, n: w[:, o:o + n]
    return jnp.concatenate([seg(R_GA, D), seg(R_GS, D), seg(R_Z, DI), seg(R_XBC, CONV), seg(R_Q, QD), seg(R_K, KVD),
                            seg(R_V, KVD), seg(R_DT, NH), jnp.zeros((w.shape[0], DT_PAD - NH), w.dtype)], axis=1)


def _from_kernel_cols(g):
    seg = lambda o, n: g[:, o:o + n]
    return jnp.concatenate([seg(O_Q, QD), seg(O_K, KVD), seg(O_V, KVD), seg(O_Z, DI), seg(O_XBC, CONV), seg(O_DT, NH),
                            seg(O_GA, D), seg(O_GS, D)], axis=1)


def _shard_pieces():
    segs = ((R_Q, QD, O_Q), (R_K, KVD, O_K), (R_V, KVD, O_V), (R_Z, DI, O_Z), (R_XBC, CONV, O_XBC), (R_DT, NH, O_DT),
            (R_GA, D, O_GA), (R_GS, D, O_GS))
    cs = IN_DIM // NCHIP
    out = []
    for j in range(NCHIP):
        for r0, n, k0 in segs:
            lo, hi = max(r0, j * cs), min(r0 + n, (j + 1) * cs)
            if lo < hi:
                out.append((j, lo - j * cs, hi - lo, k0 + lo - r0))
    return out


SLAB = IN_DIM // NCHIP
SLAB_PAD = -(-SLAB // 128) * 128
REMAP_ROWS = 256


def _lane_remap(src, dst_slabs, dst_cols, moves, *, name, add=None, jobs=()):
    s_n, rows, s_cols = src.shape
    assert s_cols % 128 == 0 and dst_cols % 128 == 0 and rows % REMAP_ROWS == 0
    half = REMAP_ROWS // 2

    def body(s_ref, *refs):
        d_ref = refs[-1]
        lane = lax.broadcasted_iota(jnp.int32, (half, 128), 1)
        tiles = {}

        def tile(j, m):
            if (j, m) not in tiles:
                tiles[j, m] = pltpu.bitcast(s_ref[j, :, 128 * m:128 * (m + 1)], jnp.uint32)
            return tiles[j, m]

        def window(j, base):
            m0, s = base // 128, base % 128
            left = tile(j, m0) if 0 <= m0 < s_cols // 128 else None
            if s == 0:
                return left
            right = tile(j, m0 + 1) if 0 <= m0 + 1 < s_cols // 128 else None
            left = None if left is None else pltpu.roll(left, 128 - s, 1)
            right = None if right is None else pltpu.roll(right, 128 - s, 1)
            if left is None or right is None:
                return right if left is None else left
            return jnp.where(lane < 128 - s, left, right)

        for ds in range(dst_slabs):
            for t in range(dst_cols // 128):
                o = 128 * t
                acc = jnp.zeros((half, 128), jnp.uint32)
                for sj, sc, n, dj, dc in moves:
                    lo, hi = max(o, dc) - o, min(o + 128, dc + n) - o
                    if dj != ds or lo >= hi:
                        continue
                    win = window(sj, o - dc + sc)
                    acc = win if (lo, hi) == (0, 128) else jnp.where(jnp.logical_and(lane >= lo, lane < hi), win, acc)
                out = pltpu.bitcast(acc, BF16)
                if add is not None:
                    out = (out.astype(F32) + refs[0][ds, :, o:o + 128].astype(F32)).astype(BF16)
                d_ref[ds, :, o:o + 128] = out

    dst_blk = pl.BlockSpec((dst_slabs, REMAP_ROWS, dst_cols), lambda i: (0, i, 0))
    return _call(
        body, jobs=jobs, name=name, out_shape=jax.ShapeDtypeStruct((dst_slabs, rows, dst_cols), BF16),
        grid=(rows // REMAP_ROWS,),
        in_specs=[pl.BlockSpec((s_n, REMAP_ROWS, s_cols), lambda i: (0, i, 0))] + ([dst_blk] if add is not None else []),
        out_specs=dst_blk, compiler_params=_cp(("parallel",)),
    )(*((src,) if add is None else (src, add)))


def _slabs_to_kernel_cols(slabs, *, name, jobs=()):
    moves = [(j, a, n, 0, k0) for j, a, n, k0 in _shard_pieces()]
    return _lane_remap(slabs, 1, NP, moves, name=name, jobs=jobs)[0]


def _kernel_cols_to_slabs(g, *, name, add=None, jobs=()):
    moves = [(0, k0, n, j, a) for j, a, n, k0 in _shard_pieces()]
    return _lane_remap(g[None], NCHIP, SLAB_PAD, moves, name=name, add=add, jobs=jobs)


RELS = ((0, 1), (1, 0), (1, 1))
MATS = {
    n: (n, kind, 1, r, c, tp, tf) for n, kind, r, c, tp, tf in (
        ("w_in", "stk", 2048, SLAB_PAD, 256, 256),
        ("w_attn_br", "col", 1024, 512, 256, 256),
        ("w_ssd_br", "row", 512, 2048, 512, 256),
        ("w_o", "row", 512, 2048, 512, 256),
        ("w_gate", "col", 2048, 1408, 256, 256),
        ("w_up", "col", 2048, 1408, 256, 256),
        ("w_down", "row", 1408, 2048, 704, 704),
        ("w_ple_gate", "row", 512, 2048, 512, 256),
        ("w_ple_proj", "col", 256, 512, 128, 128),
    )}


def _pos():
    return lax.axis_index("x"), lax.axis_index("y"), lax.axis_index("c")


def _flip(v, a):
    return 1 - v if a else v


def _remote(src, dst, send, recv, dev):
    return pltpu.make_async_remote_copy(src_ref=src, dst_ref=dst, send_sem=send, recv_sem=recv, device_id=dev,
                                        device_id_type=MESH)


def _whole_shape(kind, g, r, c):
    return {"row": (g, NCHIP * r, c), "col": (g, r, NCHIP * c), "stk": (NCHIP, r, c)}[kind]


def _cols(j, c):
    return pl.ds(pl.multiple_of(j * c, 128), c)


def _whole_shard(kind, ref, j, r, c):
    if kind == "row":
        return ref.at[:, pl.ds(j * r, r), :]
    if kind == "col":
        return ref.at[:, :, _cols(j, c)]
    return ref.at[pl.ds(j, 1)]


def _whole_rows(kind, ref, j, row, n, r, c):
    if kind == "row":
        return ref.at[:, pl.ds(j * r + row, n), :]
    if kind == "col":
        return ref.at[:, pl.ds(row, n), _cols(j, c)]
    return ref.at[pl.ds(j, 1), pl.ds(row, n), :]


class _GatherJob(_Job):
    has_mid = True
    NCP = 13

    def __init__(self, names, shards, sink):
        self.mats = [MATS[n] for n in names]
        self.srcs = [shards[n] for n in names]
        self.news = [jax.ShapeDtypeStruct(_whole_shape(kind, g, r, c), BF16) for _, kind, g, r, c, _, _ in self.mats]
        n = len(names)
        self.scratch = [pltpu.SemaphoreType.DMA((self.NCP * n,)), pltpu.SemaphoreType.DMA((self.NCP * n,))]
        self.names, self.sink = names, sink

    def _copies(self, srcs, news, sems):
        send, recv = sems
        x, y, c = _pos()
        me, jx, jy, jd = 2 * x + y, 2 * (1 - x) + y, 2 * x + (1 - y), 2 * (1 - x) + (1 - y)
        nbx, nby, sib = (1 - x, y, c), (x, 1 - y, c), (x, y, 1 - c)
        cps = []
        for w, (_, kind, g, r, cc, _, _) in enumerate(self.mats):
            hr, qr = r // 2, r // 4
            at = lambda j, h, q, n: _whole_rows(kind, news[w], j, h * hr + q * qr, n, r, cc)
            mine = lambda q: srcs[w].at[:, pl.ds(c * hr + q * qr, qr), :]
            cp = lambda k, s, d, dev: _remote(s, d, send.at[self.NCP * w + k], recv.at[self.NCP * w + k], dev)
            cps.append([
                cp(0, mine(0), at(me, c, 0, qr), nbx), cp(1, mine(1), at(me, c, 1, qr), nbx),
                cp(2, mine(1), at(me, c, 1, qr), nby), cp(3, mine(0), at(me, c, 0, qr), nby),
                cp(4, at(jx, c, 0, qr), at(jx, c, 0, qr), nby), cp(5, at(jy, c, 1, qr), at(jy, c, 1, qr), nbx),
                cp(6, at(jx, c, 0, qr), at(jx, c, 0, qr), sib), cp(7, at(jx, c, 1, qr), at(jx, c, 1, qr), sib),
                cp(8, at(jy, c, 1, qr), at(jy, c, 1, qr), sib), cp(9, at(jy, c, 0, qr), at(jy, c, 0, qr), sib),
                cp(10, at(jd, c, 0, qr), at(jd, c, 0, qr), sib), cp(11, at(jd, c, 1, qr), at(jd, c, 1, qr), sib),
                cp(12, srcs[w], _whole_shard(kind, news[w], me, r, cc), sib)])
        return cps

    def _pass_on(self, srcs, news, sems, pairs):
        cps = self._copies(srcs, news, sems)
        for w in range(len(self.mats)):
            for arrived, onward in pairs:
                cps[w][arrived].wait_recv()
                for k in onward:
                    cps[w][k].start()

    def start(self, srcs, dsts, news, sems):
        cps = self._copies(srcs, news, sems)
        for w in range(len(self.mats)):
            for k in (0, 1, 2, 3, 12):
                cps[w][k].start()

    def mid(self, srcs, dsts, news, sems):
        self._pass_on(srcs, news, sems, ((0, (4, 6)), (2, (5, 8))))

    def late(self, srcs, dsts, news, sems):
        self._pass_on(srcs, news, sems, ((1, (7,)), (3, (9,))))

    def finish(self, srcs, dsts, news, sems):
        self._pass_on(srcs, news, sems, ((4, (10,)), (5, (11,))))
        cps = self._copies(srcs, news, sems)
        for w in range(len(self.mats)):
            for k in (6, 7, 8, 9, 10, 11, 12):
                cps[w][k].wait_recv()
            for k in range(self.NCP):
                cps[w][k].wait_send()

    def done(self, dsts, news):
        for n, a in zip(self.names, news):
            self.sink[n] = a


class _SwapJob(_Job):
    def __init__(self, build, ncopies, *, srcs=(), dsts=(), news=(), done=None):
        self.build, self.srcs, self.dsts, self.news, self._done = build, list(srcs), list(dsts), list(news), done
        self.scratch = [pltpu.SemaphoreType.DMA((ncopies,)), pltpu.SemaphoreType.DMA((ncopies,))]

    def start(self, srcs, dsts, news, sems):
        for cp in self.build(srcs, dsts, news, *sems):
            cp.start()

    def finish(self, srcs, dsts, news, sems):
        for cp in self.build(srcs, dsts, news, *sems):
            cp.wait()

    def done(self, dsts, news):
        if self._done is not None:
            self._done(dsts, news)


def _half_of_whole(kind, ref, h, r, c):
    if kind == "row":
        return ref.at[:, :, pl.ds(pl.multiple_of(h * (c // 2), 128), c // 2)]
    return ref.at[:, pl.ds(h * (r // 2), r // 2), :]


def _half_shape(kind, g, r, c):
    return {"row": (g, NCHIP * r, c // 2), "col": (g, r // 2, NCHIP * c), "stk": (NCHIP, r // 2, c)}[kind]


def _sub_shape(kind, r, c):
    return {"row": (1, r // 2, c // 2), "col": (1, r // 4, c), "stk": (1, r // 4, c)}[kind]


def _sub_of_half(kind, ref, j, p, r, c):
    sr = _sub_shape(kind, r, c)[1]
    if kind == "row":
        return ref.at[:, pl.ds(j * r + p * sr, sr), :]
    if kind == "col":
        return ref.at[:, pl.ds(p * sr, sr), _cols(j, c)]
    return ref.at[pl.ds(j, 1), pl.ds(p * sr, sr), :]


def _sub_tile(sr):
    return 256 if sr % 256 == 0 else sr


def _half_of_shard(kind, ref, h, r, c):
    if kind == "row":
        return ref.at[:, :, pl.ds(pl.multiple_of(h * (c // 2), 128), c // 2)]
    return ref.at[:, pl.ds(h * (r // 2), r // 2), :]


def _pair_sum(pack, core, mine, got, whole=True):
    name, kind, g, r, c, tr, _ = pack
    hs = _half_shape(kind, g, r, c)
    nb = hs[1] // tr

    def body(core_ref, a_ref, b_ref, o_ref):
        o_ref[...] = (a_ref[...].astype(F32) + b_ref[...].astype(F32)).astype(BF16)

    blk = (1, tr, hs[2])
    same = lambda gi, i, core_ref: (gi, i, 0)
    if not whole:
        a_map = same
    elif kind == "row":
        a_map = lambda gi, i, core_ref: (gi, i, core_ref[0])
    else:
        a_map = lambda gi, i, core_ref: (gi, core_ref[0] * nb + i, 0)
    return pl.pallas_call(
        body, name="pair_sum_" + name, out_shape=jax.ShapeDtypeStruct(hs, BF16),
        grid_spec=pltpu.PrefetchScalarGridSpec(
            num_scalar_prefetch=1, grid=(hs[0], nb),
            in_specs=[pl.BlockSpec(blk, a_map), pl.BlockSpec(blk, same)], out_specs=pl.BlockSpec(blk, same)),
        compiler_params=_cp(("parallel", "parallel")),
    )(core, mine, got)


def _sub_sums(pack, idx, half, got, *, name):
    _, kind, g, r, c, _, _ = pack
    _, sr, sc = _sub_shape(kind, r, c)
    tr = _sub_tile(sr)
    nb = sr // tr

    def body(idx_ref, a_ref, ga_ref, b_ref, gb_ref, k_ref, p_ref):
        k_ref[0, 0] = a_ref[0].astype(F32) + ga_ref[0, 0].astype(F32)
        p_ref[0, 0] = (b_ref[0].astype(F32) + gb_ref[0, 0].astype(F32)).astype(BF16)

    def sub_map(o):
        if kind == "row":
            return lambda q, i, ix: (0, ix[4 * q + o] * (r // tr) + ix[4 * q + o + 1] * nb + i, 0)
        if kind == "col":
            return lambda q, i, ix: (0, ix[4 * q + o + 1] * nb + i, ix[4 * q + o])
        return lambda q, i, ix: (ix[4 * q + o], ix[4 * q + o + 1] * nb + i, 0)

    sub = lambda o: pl.BlockSpec((1, tr, sc), sub_map(o))
    got_blk = lambda o: pl.BlockSpec((1, 1, tr, sc), lambda q, i, ix: (2 * q + o, 0, i, 0))
    out_blk = pl.BlockSpec((1, 1, tr, sc), lambda q, i, ix: (q, 0, i, 0))
    return pl.pallas_call(
        body, name=name,
        out_shape=(jax.ShapeDtypeStruct((2, 1, sr, sc), F32), jax.ShapeDtypeStruct((2, 1, sr, sc), BF16)),
        grid_spec=pltpu.PrefetchScalarGridSpec(
            num_scalar_prefetch=1, grid=(2, nb), in_specs=[sub(0), got_blk(0), sub(2), got_blk(1)],
            out_specs=(out_blk, out_blk)),
        compiler_params=_cp(("parallel", "parallel")),
    )(idx, half, got, half, got)


def _shard_sum(pack, core, keep, got):
    name, kind, g, r, c, _, _ = pack
    _, sr, sc = _sub_shape(kind, r, c)
    tr = _sub_tile(sr)
    nb = sr // tr

    def body(core_ref, a_ref, b_ref, o_ref):
        o_ref[0] = a_ref[0, 0] + b_ref[0, 0].astype(F32)

    blk = pl.BlockSpec((1, 1, tr, sc), lambda p, i, cr: (p, 0, i, 0))
    if kind == "row":
        o_map = lambda p, i, cr: (0, p * nb + i, cr[0])
    else:
        o_map = lambda p, i, cr: (0, cr[0] * 2 * nb + p * nb + i, 0)
    return pl.pallas_call(
        body, name="shard_sum_" + name, out_shape=jax.ShapeDtypeStruct((g, r, c), F32),
        grid_spec=pltpu.PrefetchScalarGridSpec(
            num_scalar_prefetch=1, grid=(2, nb), in_specs=[blk, blk], out_specs=pl.BlockSpec((1, tr, sc), o_map)),
        compiler_params=_cp(("parallel", "parallel")),
    )(core, keep, got)


class _Plan:
    def __init__(self, shards, table):
        self.shards, self.table = shards, table
        self.whole, self.grad, self.got_a, self.half, self.gshard = {}, {}, {}, {}, {}
        self.got_b1, self.kept, self.pass_on, self.got_b2 = {}, {}, {}, {}
        x, y, c = _pos()
        me, jx, jy = 2 * x + y, 2 * (1 - x) + y, 2 * x + (1 - y)
        self.core = c
        self.core1 = c.reshape(1).astype(jnp.int32)
        zero = 0 * me
        self.idx_sums = jnp.stack([me, zero, jy, zero, me, zero + 1, jx, zero + 1]).astype(jnp.int32)
        self._w_in = None
        self.send, self.keep = {}, {}

    def w(self, n):
        if n != "w_in":
            return self.whole[n][0]
        if self._w_in is None:
            self._w_in = _slabs_to_kernel_cols(self.whole[n], name="relayout_w_in", jobs=self.jobs("relayout_w_in"))
        return self._w_in

    def g(self, n, a):
        self.grad[n] = a[None]

    def g_half(self, n, which, a):
        if which == "keep" and n in self.got_a:
            self.half[n] = _kernel_cols_to_slabs(a, name="relayout_d_in_keep", add=self.got_a[n])
        else:
            (self.send if which == "send" else self.keep)[n] = _kernel_cols_to_slabs(a, name="relayout_d_in_" + which)

    def jobs(self, tag):
        out = []
        for spec in self.table.get(tag, ()):
            out += getattr(self, "_" + spec[0])(*spec[1:])
        return out

    def run(self, name, jobs):
        if jobs:
            _call(lambda: None, jobs=jobs, name=name, out_shape=[], in_specs=[], out_specs=[])()

    def _gather(self, names):
        return [_GatherJob(names, self.shards, self.whole)]

    def _rs_a(self, names):
        mats = [MATS[n] for n in names]

        def build(srcs, dsts, news, send, recv):
            x, y, c = _pos()
            return [_remote(srcs[i] if names[i] in self.send else _half_of_whole(kind, srcs[i], 1 - c, r, cc), news[i],
                            send.at[i], recv.at[i], (x, y, 1 - c))
                    for i, (_, kind, g, r, cc, _, _) in enumerate(mats)]

        def done(dsts, news):
            self.got_a.update(zip(names, news))

        return [_SwapJob(build, len(names), srcs=[self.send.get(n, self.grad.get(n)) for n in names], done=done,
                         news=[jax.ShapeDtypeStruct(_half_shape(kind, g, r, c), BF16) for _, kind, g, r, c, _, _ in mats])]

    def _rs_b1(self, names):
        mats = [MATS[n] for n in names]
        for n in names:
            if n in self.half:
                continue
            if n in self.keep:
                self.half[n] = _pair_sum(MATS[n], self.core1, self.keep[n], self.got_a[n], whole=False)
            else:
                self.half[n] = _pair_sum(MATS[n], self.core1, self.grad[n], self.got_a[n])

        def build(srcs, dsts, news, send, recv):
            x, y, c = _pos()
            jx, jy, jd = 2 * (1 - x) + y, 2 * x + (1 - y), 2 * (1 - x) + (1 - y)
            nbx, nby = (1 - x, y, c), (x, 1 - y, c)
            cps = []
            for i, (_, kind, g, r, cc, _, _) in enumerate(mats):
                sub = lambda j, p: _sub_of_half(kind, srcs[i], j, p, r, cc)
                for k, (j, p, dev) in enumerate(((jx, 0, nbx), (jd, 0, nbx), (jy, 1, nby), (jd, 1, nby))):
                    cps.append(_remote(sub(j, p), news[i].at[k], send.at[4 * i + k], recv.at[4 * i + k], dev))
            return cps

        def done(dsts, news):
            self.got_b1.update(zip(names, news))

        return [_SwapJob(build, 4 * len(names), srcs=[self.half[n] for n in names], done=done,
                         news=[jax.ShapeDtypeStruct((4,) + _sub_shape(kind, r, c), BF16) for _, kind, g, r, c, _, _ in mats])]

    def _rs_b2(self, names):
        mats = [MATS[n] for n in names]
        for n in names:
            self.kept[n], self.pass_on[n] = _sub_sums(MATS[n], self.idx_sums, self.half[n], self.got_b1[n], name="sums_" + n)

        def build(srcs, dsts, news, send, recv):
            x, y, c = _pos()
            cps = []
            for i in range(len(mats)):
                cps.append(_remote(srcs[i].at[0], news[i].at[0], send.at[2 * i], recv.at[2 * i], (x, 1 - y, c)))
                cps.append(_remote(srcs[i].at[1], news[i].at[1], send.at[2 * i + 1], recv.at[2 * i + 1], (1 - x, y, c)))
            return cps

        def done(dsts, news):
            self.got_b2.update(zip(names, news))

        return [_SwapJob(build, 2 * len(names), srcs=[self.pass_on[n] for n in names], done=done,
                         news=[jax.ShapeDtypeStruct((2,) + _sub_shape(kind, r, c), BF16) for _, kind, g, r, c, _, _ in mats])]

    def _rs_c(self, names):
        mats = [MATS[n] for n in names]
        parts = [_shard_sum(MATS[n], self.core1, self.kept[n], self.got_b2[n]) for n in names]

        def build(srcs, dsts, news, send, recv):
            x, y, c = _pos()
            cps = []
            for i, (_, kind, g, r, cc, _, _) in enumerate(mats):
                mine = _half_of_shard(kind, dsts[i], c, r, cc)
                cps.append(_remote(mine, mine, send.at[i], recv.at[i], (x, y, 1 - c)))
            return cps

        def done(dsts, news):
            self.gshard.update(zip(names, dsts))

        return [_SwapJob(build, len(names), dsts=parts, done=done)]

    def finish(self, n):
        if n not in self.got_a:
            self.run("rs_a_" + n, self._rs_a((n,)))
        if n not in self.got_b1:
            self.run("rs_b1_" + n, self._rs_b1((n,)))
        if n not in self.got_b2:
            self.run("rs_b2_" + n, self._rs_b2((n,)))
        if n not in self.gshard:
            self.run("rs_c_" + n, self._rs_c((n,)))
        return self.gshard[n]


TABLE = {
    "gather_w_in": (("gather", ("w_in",)),),
    "relayout_w_in": (("gather", ("w_gate",)),),
    "mm_in": (("gather", ("w_up",)),),
    "attn_fwd": (("gather", ("w_attn_br", "w_ssd_br")),),
    "ssd_fwd": (("gather", ("w_o",)),),
    "swiglu_fwd": (("gather", ("w_down",)),),
    "mm_down": (("gather", ("w_ple_gate", "w_ple_proj")),),
    "mm_de": (("rs_a", ("w_ple_proj", "w_ple_gate")),),
    "mm_d_down": (("rs_b1", ("w_ple_proj", "w_ple_gate")),),
    "swiglu_bwd": (("rs_a", ("w_down",)), ("rs_b2", ("w_ple_proj", "w_ple_gate"))),
    "mm_d_gate": (("rs_b1", ("w_down",)),),
    "mm_d_up": (("rs_b2", ("w_down",)), ("rs_c", ("w_ple_proj", "w_ple_gate")), ("rs_a", ("w_gate",))),
    "mm_df_gate": (("rs_b1", ("w_gate",)), ("rs_a", ("w_up",)), ("rs_c", ("w_down",))),
    "mm_df_up": (("rs_b2", ("w_gate",)),),
    "norm_ffn_bwd": (("rs_c", ("w_gate",)),),
    "mm_dmerged": (("rs_a", ("w_o",)),),
    "mm_dyn": (("rs_a", ("w_attn_br", "w_ssd_br")),),
    "attn_bwd": (("rs_b1", ("w_up",)),),
    "gated_norm_bwd": (("rs_b2", ("w_up",)),),
    "ssd_bwd": (("rs_b1", ("w_o", "w_attn_br", "w_ssd_br")), ("rs_c", ("w_up",))),
    "conv_bwd": (("rs_b2", ("w_o", "w_attn_br", "w_ssd_br")),),
    "mm_d_in_send": (("rs_c", ("w_o", "w_attn_br", "w_ssd_br")),),
    "mm_d_in_keep": (("rs_a", ("w_in",)),),
    "mm_du": (("rs_b1", ("w_in",)),),
    "norm_mix_bwd": (("rs_b2", ("w_in",)),),
}


NDEV = 8


def _allreduce_small(v, *, name):
    rows = v.shape[0]

    def body(v_ref, o_ref, slots, send, recv):
        x, y, c = _pos()
        me = 4 * x + 2 * y + c
        slots[me] = v_ref[...]
        cps = []
        for k in range(1, NDEV):
            peer = (_flip(x, k & 4), _flip(y, k & 2), _flip(c, k & 1))
            cp = _remote(v_ref, slots.at[me], send.at[k - 1], recv.at[k - 1], peer)
            cp.start()
            cps.append(cp)
        for cp in cps:
            cp.wait()
        acc = slots[0]
        for s in range(1, NDEV):
            acc = acc + slots[s]
        o_ref[...] = acc

    return pl.pallas_call(
        body, name=name, out_shape=jax.ShapeDtypeStruct((rows, 128), F32),
        in_specs=[pl.BlockSpec(memory_space=pltpu.VMEM)], out_specs=pl.BlockSpec(memory_space=pltpu.VMEM),
        scratch_shapes=[pltpu.VMEM((NDEV, rows, 128), F32), pltpu.SemaphoreType.DMA((NDEV - 1,)),
                        pltpu.SemaphoreType.DMA((NDEV - 1,))],
    )(v)


def _adamw(w, g, m, v, *, name, tr=None, tc=None, jobs=()):
    r, c = w.shape
    tr = r if tr is None else tr
    c1 = 1.0 / (1.0 - B1 ** STEP)
    c2 = 1.0 / (1.0 - B2 ** STEP)

    def body(w_ref, g_ref, m_ref, v_ref, d_ref, mo_ref, vo_ref):
        gv = g_ref[...]
        mn = B1 * m_ref[...] + (1.0 - B1) * gv
        vn = B2 * v_ref[...] + (1.0 - B2) * (gv * gv)
        mo_ref[...] = mn
        vo_ref[...] = vn
        d_ref[...] = -LR * ((mn * c1) / (jnp.sqrt(vn * c2) + AEPS) + WD * w_ref[...])

    if tc is None:
        blk, grid = pl.BlockSpec((tr, c), lambda i: (i, 0)), (r // tr,)
    else:
        blk, grid = pl.BlockSpec((r, tc), lambda i: (0, i)), (c // tc,)
    o = jax.ShapeDtypeStruct((r, c), F32)
    return _call(
        body, jobs=jobs, name=name, out_shape=(o, o, o), grid=grid, in_specs=[blk] * 4, out_specs=(blk, blk, blk),
        compiler_params=_cp(("parallel",)),
    )(w, g, m, v)


WEIGHTS = ("g_mix", "w_in", "conv_w", "conv_b", "dt_bias", "a_log", "d_skip", "g_ssd", "sinks", "w_attn_br", "w_ssd_br",
           "w_o", "g_ffn", "w_gate", "w_up", "w_down", "g_ple", "w_ple_gate", "w_ple_proj", "g_final")
BIG = {
    "w_gate": 256, "w_up": 256, "w_down": 128, "w_ssd_br": 128, "w_o": 128, "w_ple_gate": 128, "w_attn_br": 256,
    "w_ple_proj": 256, "w_in": None,
}
SMALL = tuple(n for n in WEIGHTS if n not in BIG)


def _pack_small(parts):
    rows = []
    for a in parts:
        a = a.reshape(-1)
        rows.append(jnp.pad(a, (0, -a.shape[0] % 128)).reshape(-1, 128))
    out = jnp.concatenate(rows, axis=0)
    return jnp.pad(out, ((0, -out.shape[0] % 8), (0, 0)))


def _unpack_small(packed, shapes):
    out, r = [], 0
    for s in shapes:
        n = int(np.prod(s))
        nr = -(-n // 128)
        out.append(packed[r:r + nr].reshape(-1)[:n].reshape(s))
        r += nr
    return out


def kernel(x, p, positions, g_mix, w_in, conv_w, conv_b, dt_bias, a_log, d_skip, g_ssd, sinks, w_attn_br, w_ssd_br, w_o, g_ffn, w_gate, w_up, w_down, g_ple, w_ple_gate, w_ple_proj, g_final, loss_target, m_g_mix, m_w_in, m_conv_w, m_conv_b, m_dt_bias, m_a_log, m_d_skip, m_g_ssd, m_sinks, m_w_attn_br, m_w_ssd_br, m_w_o, m_g_ffn, m_w_gate, m_w_up, m_w_down, m_g_ple, m_w_ple_gate, m_w_ple_proj, m_g_final, v_g_mix, v_w_in, v_conv_w, v_conv_b, v_dt_bias, v_a_log, v_d_skip, v_g_ssd, v_sinks, v_w_attn_br, v_w_ssd_br, v_w_o, v_g_ffn, v_w_gate, v_w_up, v_w_down, v_g_ple, v_w_ple_gate, v_w_ple_proj, v_g_final):
    w = dict(zip(WEIGHTS, (g_mix, w_in, conv_w, conv_b, dt_bias, a_log, d_skip, g_ssd, sinks, w_attn_br, w_ssd_br, w_o,
                           g_ffn, w_gate, w_up, w_down, g_ple, w_ple_gate, w_ple_proj, g_final)))
    m = dict(zip(WEIGHTS, (m_g_mix, m_w_in, m_conv_w, m_conv_b, m_dt_bias, m_a_log, m_d_skip, m_g_ssd, m_sinks, m_w_attn_br,
                           m_w_ssd_br, m_w_o, m_g_ffn, m_w_gate, m_w_up, m_w_down, m_g_ple, m_w_ple_gate, m_w_ple_proj,
                           m_g_final)))
    v = dict(zip(WEIGHTS, (v_g_mix, v_w_in, v_conv_w, v_conv_b, v_dt_bias, v_a_log, v_d_skip, v_g_ssd, v_sinks, v_w_attn_br,
                           v_w_ssd_br, v_w_o, v_g_ffn, v_w_gate, v_w_up, v_w_down, v_g_ple, v_w_ple_gate, v_w_ple_proj,
                           v_g_final)))
    xi, yi, ci = _pos()
    chip = 2 * xi + yi
    t = x.shape[1]
    cshard = CONV // NCHIP

    shards = {n: w[n].astype(BF16) for n in MATS}
    shards["w_in"] = jnp.pad(shards["w_in"], ((0, 0), (0, 0), (0, SLAB_PAD - SLAB)))
    plan = _Plan(shards, TABLE)
    plan.run("gather_w_in", plan.jobs("gather_w_in"))
    placed = lax.dynamic_update_slice(jnp.zeros((CW, CONV), F32), w["conv_w"][0], (0, chip * cshard))
    conv_whole = _allreduce_small(jnp.where(ci == 0, placed, 0.0).reshape(-1, 128), name="gather_conv_w").reshape(CW, CONV)

    small = {n: w[n] for n in ("g_mix", "conv_b", "dt_bias", "a_log", "d_skip", "g_ssd", "sinks", "g_ffn", "g_ple", "g_final")}
    small["conv_w"] = conv_whole
    loss8, grad_x, gs = _local_step(x[0], p[0, 0], positions, loss_target[0], small, plan)

    order = ("g_mix", "conv_b", "dt_bias", "a_log", "d_skip", "g_ssd", "sinks", "g_ffn", "g_ple", "g_final", "conv_w")
    summed = _allreduce_small(_pack_small([loss8[0, :1]] + [gs[n] for n in order]), name="sum_small")
    parts = _unpack_small(summed, [(1,)] + [w[n].shape for n in order[:-1]] + [(CW, CONV)])
    loss = parts[0][0]
    grad = dict(zip(order, parts[1:]))
    grad["conv_w"] = lax.dynamic_slice(grad["conv_w"], (0, chip * cshard), (CW, cshard))[None]

    delta, new_m, new_v = {}, {}, {}
    for n, tr in BIG.items():
        grad[n] = plan.finish(n)[:, :, :w[n].shape[2]]
        if n == "w_in":
            d_, m_, v_ = _adamw(w[n][0].T, grad[n][0].T, m[n][0].T, v[n][0].T, tc=128, name="adamw_" + n)
            d_, m_, v_ = d_.T, m_.T, v_.T
        else:
            d_, m_, v_ = _adamw(w[n][0], grad[n][0], m[n][0], v[n][0], tr=tr, name="adamw_" + n)
        delta[n], new_m[n], new_v[n] = d_[None], m_[None], v_[None]
    shapes = [w[n].shape for n in SMALL]
    d_, m_, v_ = _adamw(_pack_small([w[n] for n in SMALL]), _pack_small([grad[n] for n in SMALL]),
                        _pack_small([m[n] for n in SMALL]), _pack_small([v[n] for n in SMALL]), tr=None, name="adamw_small")
    for n, a, b, c_ in zip(SMALL, _unpack_small(d_, shapes), _unpack_small(m_, shapes), _unpack_small(v_, shapes)):
        delta[n], new_m[n], new_v[n] = a, b, c_

    return (loss, grad_x[None], *[grad[n] for n in WEIGHTS], *[delta[n] for n in WEIGHTS],
            *[new_m[n] for n in WEIGHTS], *[new_v[n] for n in WEIGHTS])
```

```python
import functools

import jax
import jax.numpy as jnp
import numpy as np
from jax import lax
from jax.experimental import pallas as pl
from jax.experimental.pallas import tpu as pltpu

F32 = jnp.float32
BF16 = jnp.bfloat16
MESH = pl.DeviceIdType.MESH

D = 2048
HD = 64
NQH = 16
NKV = 4
QD = NQH * HD
KVD = NKV * HD
DI = 2048
NH = 32
NG = 4
NS = 128
CW = 4
L = 128
CONV = DI + 2 * NG * NS
FFN = 5632
PLE = 256
IN_DIM = QD + 2 * KVD + DI + CONV + NH + 2 * D
EPS = 1e-6
SSM_EPS = 1e-5
ROPE_THETA = 10000.0
LR, B1, B2, AEPS, WD, STEP = 0.001, 0.9, 0.999, 1e-08, 0.01, 10

O_GA, O_GS, O_Z, O_XBC, O_Q, O_K, O_V, O_DT = 0, 2048, 4096, 6144, 9216, 10240, 10496, 10752
DT_PAD = 512
NP = O_DT + DT_PAD
R_Q, R_K, R_V, R_Z, R_XBC, R_DT, R_GA, R_GS = 0, 1024, 1280, 1536, 3584, 6656, 6688, 8736

NCHIP = 4
VMEM_LIMIT = 52 * 1024 * 1024
NEG = -1e30


def _cp(sem=None):
    return pltpu.CompilerParams(dimension_semantics=sem, vmem_limit_bytes=VMEM_LIMIT)


def _dot(a, b):
    return lax.dot_general(a, b, (((1,), (0,)), ((), ())), preferred_element_type=F32)


def _dot_nt(a, b):
    return lax.dot_general(a, b, (((1,), (1,)), ((), ())), preferred_element_type=F32)


def _dot_tn(a, b):
    return lax.dot_general(a, b, (((0,), (0,)), ((), ())), preferred_element_type=F32)


def _sigmoid(x):
    return 1.0 / (1.0 + jnp.exp(-x))


def _bf16_dot(dot, da, db):
    @jax.custom_vjp
    def f(a, b):
        return dot(a.astype(BF16), b.astype(BF16))

    def fwd(a, b):
        return f(a, b), (a.astype(BF16), b.astype(BF16))

    def bwd(res, g):
        a, b = res
        g = g.astype(BF16)
        return da(g, a, b), db(g, a, b)

    f.defvjp(fwd, bwd)
    return f


_bdot = _bf16_dot(_dot, lambda g, a, b: _dot_nt(g, b), lambda g, a, b: _dot_tn(a, g))
_bdot_nt = _bf16_dot(_dot_nt, lambda g, a, b: _dot(g, b), lambda g, a, b: _dot_tn(g, a))
_bdot_tn = _bf16_dot(_dot_tn, lambda g, a, b: _dot_nt(b, g), lambda g, a, b: _dot(a, g))


ANY = pl.BlockSpec(memory_space=pl.ANY)


class _Job:
    srcs, dsts, news, scratch = (), (), (), ()
    has_mid = False

    def start(self, srcs, dsts, news, sems):
        raise NotImplementedError

    def mid(self, srcs, dsts, news, sems):
        pass

    def late(self, srcs, dsts, news, sems):
        pass

    def finish(self, srcs, dsts, news, sems):
        raise NotImplementedError

    def done(self, dsts, news):
        pass


def _call(body, *, jobs=(), name, out_shape, in_specs, out_specs, grid=(), scratch_shapes=(), compiler_params=None,
          aliases=None):
    jobs = [j for j in jobs if j is not None]
    aliases = dict(aliases or {})
    if not jobs:
        return pl.pallas_call(body, name=name, out_shape=out_shape, in_specs=in_specs, out_specs=out_specs, grid=grid,
                              scratch_shapes=scratch_shapes, compiler_params=compiler_params,
                              input_output_aliases=aliases)
    single = not isinstance(out_shape, (tuple, list))
    outs = [out_shape] if single else list(out_shape)
    ospecs = [out_specs] if single else list(out_specs)
    n_in, n_out, n_scr = len(in_specs), len(outs), len(scratch_shapes)
    srcs = [a for j in jobs for a in j.srcs]
    dsts = [a for j in jobs for a in j.dsts]
    news = [a for j in jobs for a in j.news]
    sems = [a for j in jobs for a in j.scratch]

    def wrapped(*refs):
        pos = n_in + len(srcs) + len(dsts)
        ins, jsrc = refs[:n_in], refs[n_in:n_in + len(srcs)]
        o_refs = refs[pos:pos + n_out]
        pos += n_out
        jdst, jnew = refs[pos:pos + len(dsts)], refs[pos + len(dsts):pos + len(dsts) + len(news)]
        pos += len(dsts) + len(news)
        scr, jsem = refs[pos:pos + n_scr], refs[pos + n_scr:]

        def run(which):
            a = b = c = d = 0
            for j in jobs:
                getattr(j, which)(jsrc[a:a + len(j.srcs)], jdst[b:b + len(j.dsts)], jnew[c:c + len(j.news)],
                                  jsem[d:d + len(j.scratch)])
                a, b, c, d = a + len(j.srcs), b + len(j.dsts), c + len(j.news), d + len(j.scratch)

        if not grid:
            run("start")
            run("mid")
            run("late")
            body(*ins, *o_refs, *scr)
            run("finish")
            return
        step = functools.reduce(lambda acc, a: acc * grid[a] + pl.program_id(a), range(len(grid)), 0)
        steps = int(np.prod(grid))
        pl.when(step == 0)(lambda: run("start"))
        if any(j.has_mid for j in jobs):
            pl.when(step == steps // 3)(lambda: run("mid"))
            pl.when(step == (2 * steps) // 3)(lambda: run("late"))
        body(*ins, *o_refs, *scr)
        pl.when(step == steps - 1)(lambda: run("finish"))

    call = pl.pallas_call(
        wrapped, name=name,
        out_shape=outs + [jax.ShapeDtypeStruct(a.shape, a.dtype) for a in dsts] + news,
        in_specs=list(in_specs) + [ANY] * (len(srcs) + len(dsts)),
        out_specs=ospecs + [ANY] * (len(dsts) + len(news)),
        grid=grid, scratch_shapes=list(scratch_shapes) + sems,
        input_output_aliases={**aliases, **{n_in + len(srcs) + i: n_out + i for i in range(len(dsts))}},
        compiler_params=_cp(("arbitrary",) * len(grid) if grid else None))

    def run_call(*args):
        res = call(*args, *srcs, *dsts)
        b, c = n_out, n_out + len(dsts)
        for j in jobs:
            j.done(res[b:b + len(j.dsts)], res[c:c + len(j.news)])
            b, c = b + len(j.dsts), c + len(j.news)
        return res[0] if single else tuple(res[:n_out])

    return run_call


def _matmul(a, b, *, ta=False, tb=False, out_dtype=F32, add=None, tm, tn, tk, name, jobs=()):
    k, m = a.shape if ta else a.shape[::-1]
    n = b.shape[0] if tb else b.shape[1]
    assert (b.shape[1] if tb else b.shape[0]) == k and not (ta and tb)
    assert m % tm == 0 and n % tn == 0 and k % tk == 0, (name, a.shape, b.shape)
    nk = k // tk
    has_add = add is not None

    def body(*refs):
        a_ref, b_ref = refs[0], refs[1]
        add_ref = refs[2] if has_add else None
        o_ref = refs[3] if has_add else refs[2]
        av = a_ref[...].astype(BF16)
        bv = b_ref[...].astype(BF16)
        part = _dot_tn(av, bv) if ta else _dot_nt(av, bv) if tb else _dot(av, bv)

        def finish(r):
            if has_add:
                r = r + add_ref[...]
            o_ref[...] = r.astype(out_dtype)

        if nk == 1:
            finish(part)
        elif out_dtype == F32:
            kk = pl.program_id(2)
            pl.when(kk == 0)(lambda: finish(part))

            @pl.when(kk > 0)
            def _():
                o_ref[...] += part
        else:
            acc_ref = refs[-1]
            kk = pl.program_id(2)

            @pl.when(kk == 0)
            def _():
                acc_ref[...] = part

            @pl.when(kk > 0)
            def _():
                acc_ref[...] += part

            @pl.when(kk == nk - 1)
            def _():
                finish(acc_ref[...])

    in_specs = [pl.BlockSpec((tk, tm), lambda i, j, kk: (kk, i)) if ta else pl.BlockSpec((tm, tk), lambda i, j, kk: (i, kk)),
                pl.BlockSpec((tn, tk), lambda i, j, kk: (j, kk)) if tb
                else pl.BlockSpec((tk, tn), lambda i, j, kk: (kk, j))]
    args = [a, b]
    if has_add:
        in_specs.append(pl.BlockSpec((tm, tn), lambda i, j, kk: (i, j)))
        args.append(add)
    return _call(
        body, jobs=jobs, name=name,
        out_shape=jax.ShapeDtypeStruct((m, n), out_dtype),
        grid=(m // tm, n // tn, nk),
        in_specs=in_specs,
        out_specs=pl.BlockSpec((tm, tn), lambda i, j, kk: (i, j)),
        scratch_shapes=[pltpu.VMEM((tm, tn), F32)] if nk > 1 and out_dtype != F32 else [],
        compiler_params=_cp(("parallel", "parallel", "arbitrary")),
    )(*args)


ROWS = 256


def _rmsnorm_fwd(x, g, *, name):
    t, d = x.shape

    def body(x_ref, g_ref, o_ref):
        xv = x_ref[...]
        r = lax.rsqrt(jnp.mean(xv * xv, axis=-1, keepdims=True) + EPS)
        o_ref[...] = (xv * r * g_ref[...]).astype(BF16)

    return pl.pallas_call(
        body, name=name, out_shape=jax.ShapeDtypeStruct((t, d), BF16), grid=(t // ROWS,),
        in_specs=[pl.BlockSpec((ROWS, d), lambda i: (i, 0)), pl.BlockSpec((1, d), lambda i: (0, 0))],
        out_specs=pl.BlockSpec((ROWS, d), lambda i: (i, 0)), compiler_params=_cp(("parallel",)),
    )(x, g)


def _rmsnorm_bwd(x, g, dy, dres, *, name, jobs=()):
    t, d = x.shape

    def body(x_ref, g_ref, dy_ref, dres_ref, dx_ref, dxb_ref, dg_ref):
        xv = x_ref[...]
        r = lax.rsqrt(jnp.mean(xv * xv, axis=-1, keepdims=True) + EPS)
        xh = xv * r
        dyv = dy_ref[...]
        dxh = dyv * g_ref[...]
        dx = r * (dxh - xh * jnp.mean(dxh * xh, axis=-1, keepdims=True))
        tot = dres_ref[...] + dx
        dx_ref[...] = tot
        dxb_ref[...] = tot.astype(BF16)

        @pl.when(pl.program_id(0) == 0)
        def _():
            dg_ref[...] = jnp.zeros_like(dg_ref)

        dg_ref[...] += jnp.broadcast_to(jnp.sum(dyv * xh, axis=0, keepdims=True), dg_ref.shape)

    row = pl.BlockSpec((ROWS, d), lambda i: (i, 0))
    return _call(
        body, jobs=jobs, name=name,
        out_shape=(jax.ShapeDtypeStruct((t, d), F32), jax.ShapeDtypeStruct((t, d), BF16),
                   jax.ShapeDtypeStruct((8, d), F32)),
        grid=(t // ROWS,),
        in_specs=[row, pl.BlockSpec((1, d), lambda i: (0, 0)), row, row],
        out_specs=(row, row, pl.BlockSpec((8, d), lambda i: (0, 0))),
        compiler_params=_cp(("arbitrary",)),
    )(x, g, dy, dres)


def _final(h2, pgl, pp, target, g_final, *, name):
    t, d = h2.shape

    def body(h2_ref, pgl_ref, pp_ref, tg_ref, g_ref, dh3_ref, dpgl_ref, dpp_ref, loss_ref, dg_ref):
        s = _sigmoid(pgl_ref[...])
        ppv = pp_ref[...]
        h3 = h2_ref[...] + s * ppv
        r = lax.rsqrt(jnp.mean(h3 * h3, axis=-1, keepdims=True) + EPS)
        xh = h3 * r
        gv = g_ref[...]
        err = xh * gv - tg_ref[...]
        dyv = err * (1.0 / d)
        dxh = dyv * gv
        dh3 = r * (dxh - xh * jnp.mean(dxh * xh, axis=-1, keepdims=True))
        dh3_ref[...] = dh3
        dpp_ref[...] = (dh3 * s).astype(BF16)
        dpgl_ref[...] = (dh3 * ppv * s * (1.0 - s)).astype(BF16)

        @pl.when(pl.program_id(0) == 0)
        def _():
            loss_ref[...] = jnp.zeros_like(loss_ref)
            dg_ref[...] = jnp.zeros_like(dg_ref)

        part = 0.5 * jnp.sum(jnp.mean(err * err, axis=-1, keepdims=True), axis=0, keepdims=True)
        loss_ref[...] += jnp.broadcast_to(part, loss_ref.shape)
        dg_ref[...] += jnp.broadcast_to(jnp.sum(dyv * xh, axis=0, keepdims=True), dg_ref.shape)

    row = pl.BlockSpec((ROWS, d), lambda i: (i, 0))
    return pl.pallas_call(
        body, name=name,
        out_shape=(jax.ShapeDtypeStruct((t, d), F32), jax.ShapeDtypeStruct((t, d), BF16),
                   jax.ShapeDtypeStruct((t, d), BF16), jax.ShapeDtypeStruct((8, 128), F32),
                   jax.ShapeDtypeStruct((8, d), F32)),
        grid=(t // ROWS,),
        in_specs=[row, row, row, row, pl.BlockSpec((1, d), lambda i: (0, 0))],
        out_specs=(row, row, row, pl.BlockSpec((8, 128), lambda i: (0, 0)), pl.BlockSpec((8, d), lambda i: (0, 0))),
        compiler_params=_cp(("arbitrary",)),
    )(h2, pgl, pp, target, g_final)


def _merge_fwd(proj, out_a, out_s, *, name):
    t = proj.shape[0]

    def body(ga_ref, gs_ref, a_ref, s_ref, o_ref):
        o_ref[...] = (_sigmoid(ga_ref[...]) * a_ref[...] + _sigmoid(gs_ref[...]) * s_ref[...]).astype(BF16)

    row = pl.BlockSpec((ROWS, D), lambda i: (i, 0))
    return pl.pallas_call(
        body, name=name, out_shape=jax.ShapeDtypeStruct((t, D), BF16), grid=(t // ROWS,),
        in_specs=[pl.BlockSpec((ROWS, D), lambda i: (i, O_GA // D)), pl.BlockSpec((ROWS, D), lambda i: (i, O_GS // D)),
                  row, row],
        out_specs=row, compiler_params=_cp(("parallel",)),
    )(proj, proj, out_a, out_s)


def _merge_bwd(proj, out_a, out_s, dmerged, *, name):
    t = proj.shape[0]
    assert O_GA == 0 and O_GS == D

    def body(ga_ref, gs_ref, a_ref, s_ref, dm_ref, da_ref, ds_ref, dp_ref):
        sa = _sigmoid(ga_ref[...])
        ss = _sigmoid(gs_ref[...])
        dm = dm_ref[...]
        da_ref[...] = (dm * sa).astype(BF16)
        ds_ref[...] = (dm * ss).astype(BF16)
        dp_ref[:, :D] = (dm * a_ref[...] * sa * (1.0 - sa)).astype(BF16)
        dp_ref[:, D:] = (dm * s_ref[...] * ss * (1.0 - ss)).astype(BF16)

    row = pl.BlockSpec((ROWS, D), lambda i: (i, 0))
    o = jax.ShapeDtypeStruct((t, D), BF16)
    return pl.pallas_call(
        body, name=name, out_shape=(o, o, jax.ShapeDtypeStruct((t, NP), BF16)), grid=(t // ROWS,),
        in_specs=[pl.BlockSpec((ROWS, D), lambda i: (i, O_GA // D)), pl.BlockSpec((ROWS, D), lambda i: (i, O_GS // D)),
                  row, row, row],
        out_specs=(row, row, pl.BlockSpec((ROWS, 2 * D), lambda i: (i, 0))), compiler_params=_cp(("parallel",)),
    )(proj, proj, out_a, out_s, dmerged)


def _swiglu_fwd(f, w_gate, w_up, *, name, tn=512, jobs=()):
    t, d = f.shape
    n = w_gate.shape[1]

    def body(f_ref, wg_ref, wu_ref, g_ref, u_ref, a_ref):
        fv = f_ref[...]
        g = _dot(fv, wg_ref[...])
        u = _dot(fv, wu_ref[...])
        g_ref[...] = g.astype(BF16)
        u_ref[...] = u.astype(BF16)
        a_ref[...] = (g * _sigmoid(g) * u).astype(BF16)

    col = pl.BlockSpec((t, tn), lambda j: (0, j))
    wcol = pl.BlockSpec((d, tn), lambda j: (0, j))
    return _call(
        body, jobs=jobs, name=name,
        out_shape=(jax.ShapeDtypeStruct((t, n), BF16), jax.ShapeDtypeStruct((t, n), BF16),
                   jax.ShapeDtypeStruct((t, n), BF16)),
        grid=(n // tn,),
        in_specs=[pl.BlockSpec((t, d), lambda j: (0, 0)), wcol, wcol],
        out_specs=(col, col, col), compiler_params=_cp(("parallel",)),
    )(f, w_gate, w_up)


def _swiglu_bwd(dh, w_down, gate, up, *, name, tn=512, jobs=()):
    t, d = dh.shape
    n = w_down.shape[0]

    def body(dh_ref, w_ref, g_ref, u_ref, dg_ref, du_ref):
        da = _dot_nt(dh_ref[...], w_ref[...])
        g = g_ref[...].astype(F32)
        s = _sigmoid(g)
        du_ref[...] = (da * g * s).astype(BF16)
        dg_ref[...] = (da * u_ref[...].astype(F32) * s * (1.0 + g * (1.0 - s))).astype(BF16)

    col = pl.BlockSpec((t, tn), lambda j: (0, j))
    o = jax.ShapeDtypeStruct((t, n), BF16)
    return _call(
        body, jobs=jobs, name=name, out_shape=(o, o), grid=(n // tn,),
        in_specs=[pl.BlockSpec((t, d), lambda j: (0, 0)), pl.BlockSpec((tn, d), lambda j: (j, 0)), col, col],
        out_specs=(col, col), compiler_params=_cp(("parallel",)),
    )(dh, w_down, gate, up)


def _gated_norm_fwd(y_pre, proj, g_ssd, *, name):
    t = y_pre.shape[0]

    def body(y_ref, z_ref, g_ref, o_ref):
        z = z_ref[...]
        v = y_ref[...] * z * _sigmoid(z)
        r = lax.rsqrt(jnp.mean(v * v, axis=-1, keepdims=True) + SSM_EPS)
        o_ref[...] = (v * r * g_ref[...]).astype(BF16)

    row = pl.BlockSpec((ROWS, DI), lambda i: (i, 0))
    return pl.pallas_call(
        body, name=name, out_shape=jax.ShapeDtypeStruct((t, DI), BF16), grid=(t // ROWS,),
        in_specs=[row, pl.BlockSpec((ROWS, DI), lambda i: (i, O_Z // DI)), pl.BlockSpec((1, DI), lambda i: (0, 0))],
        out_specs=row, compiler_params=_cp(("parallel",)),
    )(y_pre, proj, g_ssd)


def _gated_norm_bwd(y_pre, proj, g_ssd, dyn, dproj, *, name, jobs=()):
    t = y_pre.shape[0]

    def body(y_ref, z_ref, g_ref, dyn_ref, _, dy_ref, dz_ref, dg_ref):
        z = z_ref[...]
        s = _sigmoid(z)
        sz = z * s
        yv = y_ref[...]
        v = yv * sz
        r = lax.rsqrt(jnp.mean(v * v, axis=-1, keepdims=True) + SSM_EPS)
        vh = v * r
        dn = dyn_ref[...]
        dvh = dn * g_ref[...]
        dv = r * (dvh - vh * jnp.mean(dvh * vh, axis=-1, keepdims=True))
        dy_ref[...] = dv * sz
        dz_ref[...] = (dv * yv * s * (1.0 + z * (1.0 - s))).astype(BF16)

        @pl.when(pl.program_id(0) == 0)
        def _():
            dg_ref[...] = jnp.zeros_like(dg_ref)

        dg_ref[...] += jnp.broadcast_to(jnp.sum(dn * vh, axis=0, keepdims=True), dg_ref.shape)

    row = pl.BlockSpec((ROWS, DI), lambda i: (i, 0))
    return _call(
        body, jobs=jobs, name=name,
        out_shape=(jax.ShapeDtypeStruct((t, DI), F32), jax.ShapeDtypeStruct(dproj.shape, BF16),
                   jax.ShapeDtypeStruct((8, DI), F32)),
        grid=(t // ROWS,),
        in_specs=[row, pl.BlockSpec((ROWS, DI), lambda i: (i, O_Z // DI)), pl.BlockSpec((1, DI), lambda i: (0, 0)), row, ANY],
        out_specs=(row, pl.BlockSpec((ROWS, DI), lambda i: (i, O_Z // DI)), pl.BlockSpec((8, DI), lambda i: (0, 0))),
        compiler_params=_cp(("arbitrary",)), aliases={4: 1},
    )(y_pre, proj, g_ssd, dyn, dproj)


CONV_TC = 512


def _shift_down(x, s, row):
    if s == 0:
        return x
    return jnp.where(row >= s, pltpu.roll(x, s, 0), 0.0)


def _shift_up(x, s, row, t):
    if s == 0:
        return x
    return jnp.where(row < t - s, pltpu.roll(x, t - s, 0), 0.0)


def _conv_fwd(proj, conv_w, conv_b, *, name):
    t = proj.shape[0]

    def body(x_ref, w_ref, b_ref, o_ref):
        x = x_ref[...]
        row = lax.broadcasted_iota(jnp.int32, x.shape, 0)
        pre = jnp.broadcast_to(b_ref[...], x.shape)
        for k in range(CW):
            pre = pre + w_ref[k:k + 1, :] * _shift_down(x, CW - 1 - k, row)
        o_ref[...] = pre * _sigmoid(pre)

    return pl.pallas_call(
        body, name=name, out_shape=jax.ShapeDtypeStruct((t, CONV), F32), grid=(CONV // CONV_TC,),
        in_specs=[pl.BlockSpec((t, CONV_TC), lambda j: (0, O_XBC // CONV_TC + j)),
                  pl.BlockSpec((CW, CONV_TC), lambda j: (0, j)), pl.BlockSpec((1, CONV_TC), lambda j: (0, j))],
        out_specs=pl.BlockSpec((t, CONV_TC), lambda j: (0, j)), compiler_params=_cp(("parallel",)),
    )(proj, conv_w, conv_b)


def _conv_bwd(proj, conv_w, conv_b, dxs, db, dc, dproj, *, name, jobs=()):
    t = proj.shape[0]
    nx = DI // CONV_TC
    assert NG * NS == CONV_TC

    def body(x_ref, w_ref, b_ref, dxs_ref, db_ref, dc_ref, _, dx_ref, dw_ref, dbias_ref):
        j = pl.program_id(0)
        x = x_ref[...]
        row = lax.broadcasted_iota(jnp.int32, x.shape, 0)
        xs = [_shift_down(x, CW - 1 - k, row) for k in range(CW)]
        pre = jnp.broadcast_to(b_ref[...], x.shape)
        for k in range(CW):
            pre = pre + w_ref[k:k + 1, :] * xs[k]
        s = _sigmoid(pre)
        da = jnp.where(j < nx, dxs_ref[...], jnp.where(j == nx, db_ref[...], dc_ref[...]))
        dpre = da * s * (1.0 + pre * (1.0 - s))
        dx = jnp.zeros_like(x)
        row8 = lax.broadcasted_iota(jnp.int32, dw_ref.shape, 0)
        dw = jnp.zeros(dw_ref.shape, F32)
        for k in range(CW):
            dx = dx + w_ref[k:k + 1, :] * _shift_up(dpre, CW - 1 - k, row, t)
            dw = dw + jnp.where(row8 == k, jnp.sum(dpre * xs[k], axis=0, keepdims=True), 0.0)
        dx_ref[...] = dx.astype(BF16)
        dw_ref[...] = dw
        dbias_ref[...] = jnp.broadcast_to(jnp.sum(dpre, axis=0, keepdims=True), dbias_ref.shape)

    col8 = pl.BlockSpec((8, CONV_TC), lambda j: (0, j))
    xbc = pl.BlockSpec((t, CONV_TC), lambda j: (0, O_XBC // CONV_TC + j))
    whole = pl.BlockSpec((t, CONV_TC), lambda j: (0, 0))
    return _call(
        body, jobs=jobs, name=name,
        out_shape=(jax.ShapeDtypeStruct(dproj.shape, BF16), jax.ShapeDtypeStruct((8, CONV), F32),
                   jax.ShapeDtypeStruct((8, CONV), F32)),
        grid=(CONV // CONV_TC,),
        in_specs=[xbc, pl.BlockSpec((CW, CONV_TC), lambda j: (0, j)), pl.BlockSpec((1, CONV_TC), lambda j: (0, j)),
                  pl.BlockSpec((t, CONV_TC), lambda j: (0, jnp.minimum(j, nx - 1))), whole, whole, ANY],
        out_specs=(xbc, col8, col8),
        compiler_params=_cp(("arbitrary",)), aliases={6: 0},
    )(proj, conv_w, conv_b, dxs, db, dc, dproj)


def _rope_tables(positions, t):
    half = HD // 2
    inv_freq = ROPE_THETA ** (-jnp.arange(half, dtype=F32) * 2.0 / HD)
    ang = positions.reshape(t).astype(F32)[:, None] * inv_freq
    cos, sin = jnp.cos(ang), jnp.sin(ang)
    return jnp.concatenate([cos] * 4, axis=1), jnp.concatenate([-sin, sin] * 2, axis=1)


def _lane_consts():
    lane = lax.broadcasted_iota(jnp.int32, (L, 128), 1)
    return lane, (lane % HD) < (HD // 2), lane < HD


def _rope(tv, cos, sin, lo):
    return tv * cos + jnp.where(lo, pltpu.roll(tv, 128 - HD // 2, 1), pltpu.roll(tv, HD // 2, 1)) * sin


def _rope_t(dv, cos, sin, lo):
    ds = dv * sin
    return dv * cos + jnp.where(lo, pltpu.roll(ds, 128 - HD // 2, 1), pltpu.roll(ds, HD // 2, 1))


def _placed(chunk, g, half0):
    own = jnp.where(half0 if g % 2 == 0 else jnp.logical_not(half0), chunk, 0.0)
    other = pltpu.roll(own, HD, 1)
    return (own, other) if g % 2 == 0 else (other, own)


def _unplace(acc, hf, g, half0):
    v = jnp.where(half0 if hf == 0 else jnp.logical_not(half0), acc, 0.0)
    return v if hf == g % 2 else pltpu.roll(v, HD, 1)


def _attn_fwd(proj, cos, sin, sinks, *, name, jobs=()):
    t = proj.shape[0]
    nb = t // L
    scale = HD ** -0.5

    def body(sink_ref, q_ref, kc_ref, kp_ref, vc_ref, vp_ref, cc_ref, sc_ref, cp_ref, sp_ref, o_ref, lse_ref):
        i = pl.program_id(0)
        lane, lo, half0 = _lane_consts()
        cos_c, sin_c, cos_p, sin_p = cc_ref[...], sc_ref[...], cp_ref[...], sp_ref[...]
        row = lax.broadcasted_iota(jnp.int32, (L, 2 * L), 0)
        col = lax.broadcasted_iota(jnp.int32, (L, 2 * L), 1)
        valid = jnp.logical_or(jnp.logical_and(jnp.logical_and(col < L, col > row), i > 0),
                               jnp.logical_and(col >= L, col - L <= row))
        kc = [_rope(kc_ref[:, 128 * m:128 * (m + 1)], cos_c, sin_c, lo) for m in range(2)]
        kp = [_rope(kp_ref[:, 128 * m:128 * (m + 1)], cos_p, sin_p, lo) for m in range(2)]
        lse_acc = jnp.zeros((L, 128), F32)
        outs = [jnp.zeros((L, 128), F32) for _ in range(QD // 128)]
        qs = [(_rope(q_ref[:, 128 * ch:128 * (ch + 1)], cos_c, sin_c, lo) * scale).astype(BF16) for ch in range(QD // 128)]
        both = lambda prev, cur, g: [jnp.concatenate([a, b], axis=0).astype(BF16)
                                     for a, b in zip(_placed(prev, g, half0), _placed(cur, g, half0))]
        for g in range(NKV):
            sl = slice(128 * (g // 2), 128 * (g // 2 + 1))
            kv = both(kp[g // 2], kc[g // 2], g)
            vv = both(vp_ref[:, sl], vc_ref[:, sl], g)
            for r in range(NQH // NKV):
                h = g * (NQH // NKV) + r
                ch, hf = h // 2, h % 2
                s = jnp.where(valid, _dot_nt(qs[ch], kv[hf]), NEG)
                sink = sink_ref[0, h]
                mx = jnp.maximum(jnp.max(s, axis=-1, keepdims=True), sink)
                e = jnp.exp(s - mx)
                den = jnp.sum(e, axis=-1, keepdims=True) + jnp.exp(sink - mx)
                outs[ch] = outs[ch] + _dot((e * (1.0 / den)).astype(BF16), vv[hf])
                lse_acc = jnp.where(lane == h, mx + jnp.log(den), lse_acc)
        for ch in range(QD // 128):
            o_ref[:, 128 * ch:128 * (ch + 1)] = outs[ch].astype(BF16)
        lse_ref[...] = lse_acc

    prev = lambda i: jnp.maximum(i - 1, 0)
    tab_c = pl.BlockSpec((L, 128), lambda i: (i, 0))
    tab_p = pl.BlockSpec((L, 128), lambda i: (prev(i), 0))
    return _call(
        body, jobs=jobs, name=name,
        out_shape=(jax.ShapeDtypeStruct((t, QD), BF16), jax.ShapeDtypeStruct((t, 128), F32)),
        grid=(nb,),
        in_specs=[pl.BlockSpec(memory_space=pltpu.SMEM),
                  pl.BlockSpec((L, QD), lambda i: (i, O_Q // QD)),
                  pl.BlockSpec((L, KVD), lambda i: (i, O_K // KVD)), pl.BlockSpec((L, KVD), lambda i: (prev(i), O_K // KVD)),
                  pl.BlockSpec((L, KVD), lambda i: (i, O_V // KVD)), pl.BlockSpec((L, KVD), lambda i: (prev(i), O_V // KVD)),
                  tab_c, tab_c, tab_p, tab_p],
        out_specs=(pl.BlockSpec((L, QD), lambda i: (i, 0)), pl.BlockSpec((L, 128), lambda i: (i, 0))),
        compiler_params=_cp(("parallel",)),
    )(sinks, proj, proj, proj, proj, proj, cos, sin, cos, sin)


def _attn_bwd(proj, cos, sin, sinks, attn, lse, dattn, dproj, *, name, jobs=()):
    t = proj.shape[0]
    nb = t // L
    scale = HD ** -0.5

    def body(sink_ref, qi_ref, qn_ref, kc_ref, kp_ref, vc_ref, vp_ref, doi_ref, don_ref, oi_ref, on_ref,
             lsei_ref, lsen_ref, cc_ref, sc_ref, cp_ref, sp_ref, cn_ref, sn_ref, _, dqkv_ref, dsk_ref):
        i = pl.program_id(0)
        lane, lo, half0 = _lane_consts()
        half1 = jnp.logical_not(half0)
        cos_c, sin_c = cc_ref[...], sc_ref[...]
        row = lax.broadcasted_iota(jnp.int32, (L, 2 * L), 0)
        col = lax.broadcasted_iota(jnp.int32, (L, 2 * L), 1)
        valid = jnp.logical_or(jnp.logical_and(jnp.logical_and(col < L, col > row), i > 0),
                               jnp.logical_and(col >= L, col - L <= row))
        m_next = jnp.logical_and(col[:, :L] > row[:, :L], i < nb - 1)
        kc = [_rope(kc_ref[:, 128 * m:128 * (m + 1)], cos_c, sin_c, lo) for m in range(2)]
        kp = [_rope(kp_ref[:, 128 * m:128 * (m + 1)], cp_ref[...], sp_ref[...], lo) for m in range(2)]
        lse_i, lse_n = lsei_ref[...], lsen_ref[...]
        dk_acc = [jnp.zeros((L, 128), F32) for _ in range(2)]
        dv_acc = [jnp.zeros((L, 128), F32) for _ in range(2)]
        dsk_acc = jnp.zeros((1, 128), F32)
        lane1 = lax.broadcasted_iota(jnp.int32, (1, 128), 1)
        both = lambda prev, cur, g: [jnp.concatenate([a, b], axis=0).astype(BF16)
                                     for a, b in zip(_placed(prev, g, half0), _placed(cur, g, half0))]
        kvs = [both(kp[g // 2], kc[g // 2], g) for g in range(NKV)]
        vvs = [both(vp_ref[:, 128 * (g // 2):128 * (g // 2 + 1)], vc_ref[:, 128 * (g // 2):128 * (g // 2 + 1)], g)
               for g in range(NKV)]
        for ch in range(QD // 128):
            sl = slice(128 * ch, 128 * (ch + 1))
            q_i = (_rope(qi_ref[:, sl], cos_c, sin_c, lo) * scale).astype(BF16)
            q_n = (_rope(qn_ref[:, sl], cn_ref[...], sn_ref[...], lo) * scale).astype(BF16)
            q_in = jnp.concatenate([q_i, q_n], axis=0)
            do_i, do_n = doi_ref[:, sl], don_ref[:, sl]
            do_ib, do_nb = do_i.astype(BF16), do_n.astype(BF16)
            do_in = jnp.concatenate([do_ib, do_nb], axis=0)
            od_i = do_i * oi_ref[:, sl].astype(F32)
            od_n = do_n * on_ref[:, sl].astype(F32)
            dq_ch = jnp.zeros((L, 128), F32)
            for hf in range(2):
                h = 2 * ch + hf
                g = h // (NQH // NKV)
                hm = half0 if hf == 0 else half1
                kv, vv = kvs[g][hf], vvs[g][hf]
                kcv, vcv = kv[L:], vv[L:]
                dl_i = jnp.sum(jnp.where(hm, od_i, 0.0), axis=-1, keepdims=True)
                dl_n = jnp.sum(jnp.where(hm, od_n, 0.0), axis=-1, keepdims=True)
                ls_i = jnp.sum(jnp.where(lane == h, lse_i, 0.0), axis=-1, keepdims=True)
                ls_n = jnp.sum(jnp.where(lane == h, lse_n, 0.0), axis=-1, keepdims=True)
                p = jnp.where(valid, jnp.exp(_dot_nt(q_i, kv) - ls_i), 0.0)
                ds = (p * (_dot_nt(do_ib, vv) - dl_i)).astype(BF16)
                dq_ch = dq_ch + jnp.where(hm, _dot(ds, kv) * scale, 0.0)
                sink = sink_ref[0, h]
                dsk = -jnp.sum(jnp.exp(sink - ls_i) * dl_i, axis=0, keepdims=True)
                dsk_acc = dsk_acc + jnp.where(lane1 == h, dsk, 0.0)
                p_n = jnp.where(m_next, jnp.exp(_dot_nt(q_n, kcv) - ls_n), 0.0)
                ds_n = (p_n * (_dot_nt(do_nb, vcv) - dl_n)).astype(BF16)
                dv_h = _dot_tn(jnp.concatenate([p[:, L:].astype(BF16), p_n.astype(BF16)], axis=0), do_in)
                dk_h = _dot_tn(jnp.concatenate([ds[:, L:], ds_n], axis=0), q_in)
                dv_acc[g // 2] = dv_acc[g // 2] + _unplace(dv_h, hf, g, half0)
                dk_acc[g // 2] = dk_acc[g // 2] + _unplace(dk_h, hf, g, half0)
            dqkv_ref[:, sl] = _rope_t(dq_ch, cos_c, sin_c, lo).astype(BF16)
        for m in range(2):
            dqkv_ref[:, QD + 128 * m:QD + 128 * (m + 1)] = _rope_t(dk_acc[m], cos_c, sin_c, lo).astype(BF16)
            dqkv_ref[:, QD + KVD + 128 * m:QD + KVD + 128 * (m + 1)] = dv_acc[m].astype(BF16)

        @pl.when(i == 0)
        def _():
            dsk_ref[...] = jnp.zeros_like(dsk_ref)

        dsk_ref[...] += jnp.broadcast_to(dsk_acc, dsk_ref.shape)

    prev = lambda i: jnp.maximum(i - 1, 0)
    nxt = lambda i: jnp.minimum(i + 1, nb - 1)
    cur_q = pl.BlockSpec((L, QD), lambda i: (i, 0))
    nxt_q = pl.BlockSpec((L, QD), lambda i: (nxt(i), 0))
    tab = lambda f: pl.BlockSpec((L, 128), lambda i: (f(i), 0))
    ident = lambda i: i
    qkv = QD + 2 * KVD
    assert O_K == O_Q + QD and O_V == O_K + KVD and O_Q % qkv == 0
    return _call(
        body, jobs=jobs, name=name,
        out_shape=(jax.ShapeDtypeStruct(dproj.shape, BF16), jax.ShapeDtypeStruct((8, 128), F32)),
        grid=(nb,),
        in_specs=[pl.BlockSpec(memory_space=pltpu.SMEM),
                  pl.BlockSpec((L, QD), lambda i: (i, O_Q // QD)), pl.BlockSpec((L, QD), lambda i: (nxt(i), O_Q // QD)),
                  pl.BlockSpec((L, KVD), lambda i: (i, O_K // KVD)), pl.BlockSpec((L, KVD), lambda i: (prev(i), O_K // KVD)),
                  pl.BlockSpec((L, KVD), lambda i: (i, O_V // KVD)), pl.BlockSpec((L, KVD), lambda i: (prev(i), O_V // KVD)),
                  cur_q, nxt_q, cur_q, nxt_q, tab(ident), tab(nxt),
                  tab(ident), tab(ident), tab(prev), tab(prev), tab(nxt), tab(nxt), ANY],
        out_specs=(pl.BlockSpec((L, qkv), lambda i: (i, O_Q // qkv)), pl.BlockSpec((8, 128), lambda i: (0, 0))),
        compiler_params=_cp(("arbitrary",)), aliases={19: 0},
    )(sinks, proj, proj, proj, proj, proj, proj, dattn, dattn, attn, attn, lse, lse, cos, sin, cos, sin, cos, sin, dproj)


PAIRS = NH // NG // 2


def _softplus(x):
    return jnp.maximum(x, 0.0) + jnp.log(1.0 + jnp.exp(-jnp.abs(x)))


def _ssd_chunk(g, xps, dtr, bm, cm, sps, dtb, alog, dsk):
    lane = lax.broadcasted_iota(jnp.int32, (L, 128), 1)
    lane1 = lax.broadcasted_iota(jnp.int32, (1, 128), 1)
    row = lax.broadcasted_iota(jnp.int32, (L, L), 0)
    col = lax.broadcasted_iota(jnp.int32, (L, L), 1)
    rowc = lax.broadcasted_iota(jnp.int32, (128, 1), 0)
    tril = col <= row
    dt = _softplus(dtr + dtb)
    a = dt * (-jnp.exp(alog))
    a_cs = lax.dot_general(tril.astype(F32), a, (((1,), (0,)), ((), ())), precision=lax.Precision.HIGHEST,
                           preferred_element_type=F32)
    a_cst = a_cs.T
    a_last = jnp.sum(jnp.where(row == L - 1, a_cs, 0.0), axis=0, keepdims=True)
    cb = _bdot_nt(cm, bm)
    ys, snew = [], []
    for q in range(PAIRS):
        xp, sp = xps[q], sps[q]
        skip = jnp.zeros((L, 128), F32)
        keep = jnp.zeros((128, 1), F32)
        ms, xds, cds, sms, bds = [], [], [], [], []
        for hh in range(2):
            h = g * 2 * PAIRS + 2 * q + hh
            hm = (lane < HD) if hh == 0 else (lane >= HD)
            rm = (rowc < HD) if hh == 0 else (rowc >= HD)
            dt_h = jnp.sum(jnp.where(lane == h, dt, 0.0), axis=1, keepdims=True)
            acs_h = jnp.sum(jnp.where(lane == h, a_cs, 0.0), axis=1, keepdims=True)
            acst_h = jnp.sum(jnp.where(row == h, a_cst, 0.0), axis=0, keepdims=True)
            al_h = jnp.sum(jnp.where(lane1 == h, a_last, 0.0), axis=1, keepdims=True)
            dsk_h = jnp.sum(jnp.where(lane1 == h, dsk, 0.0), axis=1, keepdims=True)
            decay = jnp.where(tril, jnp.exp(jnp.where(tril, acs_h - acst_h, 0.0)), 0.0)
            xh = jnp.where(hm, xp, 0.0)
            ms.append(cb * decay)
            xds.append(xh * dt_h)
            cds.append(cm * jnp.exp(acs_h))
            sms.append(jnp.where(rm, sp, 0.0))
            bds.append(bm * jnp.exp(al_h - acs_h))
            skip = skip + dsk_h * xh
            keep = keep + jnp.where(rm, jnp.exp(al_h), 0.0)
        xd2 = jnp.concatenate(xds, axis=0)
        y_pair = (_bdot(jnp.concatenate(ms, axis=1), xd2)
                  + _bdot_nt(jnp.concatenate(cds, axis=1), jnp.concatenate(sms, axis=1)) + skip)
        ys.append(y_pair)
        snew.append(sp * keep + _bdot_tn(xd2, jnp.concatenate(bds, axis=0)))
    return ys, snew


def _ssd_specs(t):
    nc = t // L
    xs = lambda f: pl.BlockSpec((L, 128 * PAIRS), lambda c, g: (f(c), g))
    bspec = lambda f: pl.BlockSpec((L, NS), lambda c, g: (f(c), DI // NS + g))
    cspec = lambda f: pl.BlockSpec((L, NS), lambda c, g: (f(c), DI // NS + NG + g))
    dts = lambda f: pl.BlockSpec((L, 128), lambda c, g: (f(c), O_DT // 128))
    par = pl.BlockSpec((1, 128), lambda c, g: (0, 0))
    st = lambda f: pl.BlockSpec((1, 1, PAIRS, 128, NS), lambda c, g: (f(c), g, 0, 0, 0))
    return nc, xs, bspec, cspec, dts, par, st


def _ssd_fwd(xbc_act, proj, dtb, alog, dsk, *, name, jobs=()):
    t = proj.shape[0]
    nc, xs, bspec, cspec, dts, par, st = _ssd_specs(t)
    ident = lambda c: c

    def body(x_ref, b_ref, c_ref, dt_ref, dtb_ref, al_ref, dsk_ref, y_ref, sin_ref, s_ref):
        c, g = pl.program_id(0), pl.program_id(1)

        @pl.when(c == 0)
        def _():
            s_ref[g] = jnp.zeros((PAIRS, 128, NS), F32)

        sps = [s_ref[g, q] for q in range(PAIRS)]
        for q in range(PAIRS):
            sin_ref[0, 0, q] = sps[q]
        xps = [x_ref[:, 128 * q:128 * (q + 1)] for q in range(PAIRS)]
        ys, snew = _ssd_chunk(g, xps, dt_ref[...], b_ref[...], c_ref[...], sps, dtb_ref[...], al_ref[...], dsk_ref[...])
        for q in range(PAIRS):
            y_ref[:, 128 * q:128 * (q + 1)] = ys[q]
            s_ref[g, q] = snew[q]

    return _call(
        body, jobs=jobs, name=name,
        out_shape=(jax.ShapeDtypeStruct((t, DI), F32), jax.ShapeDtypeStruct((nc, NG, PAIRS, 128, NS), F32)),
        grid=(nc, NG),
        in_specs=[xs(ident), bspec(ident), cspec(ident), dts(ident), par, par, par],
        out_specs=(pl.BlockSpec((L, 128 * PAIRS), lambda c, g: (c, g)), st(ident)),
        scratch_shapes=[pltpu.VMEM((NG, PAIRS, 128, NS), F32)],
        compiler_params=_cp(("arbitrary", "arbitrary")),
    )(xbc_act, xbc_act, xbc_act, proj, dtb, alog, dsk)


def _ssd_bwd(xbc_act, proj, dtb, alog, dsk, states, dy, dproj, *, name, jobs=()):
    t = proj.shape[0]
    nc, xs, bspec, cspec, dts, par, st = _ssd_specs(t)
    rev = lambda c: nc - 1 - c

    def body(x_ref, b_ref, c_ref, dt_ref, dtb_ref, al_ref, dsk_ref, sin_ref, dy_ref, _,
             dx_ref, db_ref, dc_ref, ddtp_ref, ddtb_ref, dal_ref, ddsk_ref, ds_ref, ddt_ref):
        c, g = pl.program_id(0), pl.program_id(1)

        @pl.when(c == 0)
        def _():
            ds_ref[g] = jnp.zeros((PAIRS, 128, NS), F32)

        @pl.when(jnp.logical_and(c == 0, g == 0))
        def _():
            ddtb_ref[...] = jnp.zeros_like(ddtb_ref)
            dal_ref[...] = jnp.zeros_like(dal_ref)
            ddsk_ref[...] = jnp.zeros_like(ddsk_ref)

        @pl.when(g == 0)
        def _():
            ddt_ref[...] = jnp.zeros_like(ddt_ref)

        sps = [sin_ref[0, 0, q] for q in range(PAIRS)]
        xps = [x_ref[:, 128 * q:128 * (q + 1)] for q in range(PAIRS)]
        _, vjp = jax.vjp(functools.partial(_ssd_chunk, g), xps, dt_ref[...], b_ref[...], c_ref[...], sps,
                         dtb_ref[...], al_ref[...], dsk_ref[...])
        dys = [dy_ref[:, 128 * q:128 * (q + 1)] for q in range(PAIRS)]
        dss = [ds_ref[g, q] for q in range(PAIRS)]
        dxps, ddt, db, dc, dsps, ddtb, dal, ddsk = vjp((dys, dss))
        for q in range(PAIRS):
            dx_ref[:, 128 * q:128 * (q + 1)] = dxps[q]
            ds_ref[g, q] = dsps[q]
        db_ref[...] = db
        dc_ref[...] = dc
        ddt_ref[...] += ddt
        ddtb_ref[...] += jnp.broadcast_to(ddtb, ddtb_ref.shape)
        dal_ref[...] += jnp.broadcast_to(dal, dal_ref.shape)
        ddsk_ref[...] += jnp.broadcast_to(ddsk, ddsk_ref.shape)

        @pl.when(g == NG - 1)
        def _():
            ddtp_ref[:, :128] = ddt_ref[...].astype(BF16)
            ddtp_ref[:, 128:] = jnp.zeros((L, DT_PAD - 128), BF16)

    acc = pl.BlockSpec((8, 128), lambda c, g: (0, 0))
    o8 = jax.ShapeDtypeStruct((8, 128), F32)
    return _call(
        body, jobs=jobs, name=name,
        out_shape=(jax.ShapeDtypeStruct((t, DI), F32), jax.ShapeDtypeStruct((t, NG * NS), F32),
                   jax.ShapeDtypeStruct((t, NG * NS), F32), jax.ShapeDtypeStruct(dproj.shape, BF16), o8, o8, o8),
        grid=(nc, NG),
        in_specs=[xs(rev), bspec(rev), cspec(rev), dts(rev), par, par, par, st(rev),
                  pl.BlockSpec((L, 128 * PAIRS), lambda c, g: (rev(c), g)), ANY],
        out_specs=(pl.BlockSpec((L, 128 * PAIRS), lambda c, g: (rev(c), g)),
                   pl.BlockSpec((L, NS), lambda c, g: (rev(c), g)), pl.BlockSpec((L, NS), lambda c, g: (rev(c), g)),
                   pl.BlockSpec((L, DT_PAD), lambda c, g: (rev(c), O_DT // DT_PAD)), acc, acc, acc),
        scratch_shapes=[pltpu.VMEM((NG, PAIRS, 128, NS), F32), pltpu.VMEM((L, 128), F32)],
        compiler_params=_cp(("arbitrary", "arbitrary")), aliases={9: 3},
    )(xbc_act, xbc_act, xbc_act, proj, dtb, alog, dsk, states, dy, dproj)


def _pad_lanes(v, n=128):
    return jnp.pad(v, ((0, 0), (0, n - v.shape[1])))


class _LocalPlan:
    core = 0

    def __init__(self, big):
        self.big, self.grad, self.halves = big, {}, {}

    def w(self, n):
        return self.big[n]

    def g(self, n, a):
        self.grad[n] = a

    def g_half(self, n, which, a):
        self.halves[which] = a
        if len(self.halves) == 2:
            self.grad[n] = jnp.concatenate([self.halves["keep"], self.halves["send"]], axis=0)

    def jobs(self, tag):
        return ()


def _local_step(x, p, positions, target, small, plan):
    t = x.shape[0]
    cos, sin = _rope_tables(positions, t)
    dtb, alog, dsk = _pad_lanes(small["dt_bias"]), _pad_lanes(small["a_log"]), _pad_lanes(small["d_skip"])
    w, jobs = plan.w, plan.jobs

    def mm(a, b, *, name, tn=None, **kw):
        n = b.shape[0] if kw.get("tb") else b.shape[1]
        if tn is None:
            tn = 1024 if n % 1024 == 0 and kw.get("add") is None else 512
        return _matmul(a, b, tm=t, tn=tn, name=name, jobs=jobs(name), **kw)

    tkl = FFN // 4

    def dw(wname, a, dy, *, name, tm):
        tn = 1024 if dy.shape[1] % 1024 == 0 and tm <= D else 512
        plan.g(wname, _matmul(a, dy, ta=True, out_dtype=BF16, tm=tm, tn=tn, tk=t, name=name, jobs=jobs(name)))

    u = _rmsnorm_fwd(x, small["g_mix"], name="norm_mix")
    proj = mm(u, w("w_in"), tn=1024, tk=D, name="mm_in")
    attn, lse = _attn_fwd(proj, cos, sin, small["sinks"], name="attn_fwd", jobs=jobs("attn_fwd"))
    out_a = mm(attn, w("w_attn_br"), tk=QD, name="mm_attn_br")
    xbc_act = _conv_fwd(proj, small["conv_w"], small["conv_b"], name="conv_fwd")
    y_pre, states = _ssd_fwd(xbc_act, proj, dtb, alog, dsk, name="ssd_fwd", jobs=jobs("ssd_fwd"))
    yn = _gated_norm_fwd(y_pre, proj, small["g_ssd"], name="gated_norm_fwd")
    out_s = mm(yn, w("w_ssd_br"), tk=DI, name="mm_ssd_br")
    merged = _merge_fwd(proj, out_a, out_s, name="merge_fwd")
    h1 = mm(merged, w("w_o"), add=x, tk=D, name="mm_o")
    f = _rmsnorm_fwd(h1, small["g_ffn"], name="norm_ffn")
    gate, up, act = _swiglu_fwd(f, w("w_gate"), w("w_up"), name="swiglu_fwd", jobs=jobs("swiglu_fwd"))
    h2 = mm(act, w("w_down"), add=h1, tk=tkl, name="mm_down")
    e = _rmsnorm_fwd(h2, small["g_ple"], name="norm_ple")
    pgl = mm(e, w("w_ple_gate"), tk=D, name="mm_ple_gate")
    pb = p.astype(BF16)
    pp = mm(pb, w("w_ple_proj"), tk=PLE, name="mm_ple_proj")
    dh3, dpgl, dpp, loss, dg_final = _final(h2, pgl, pp, target, small["g_final"].reshape(1, D), name="final")

    dw("w_ple_proj", pb, dpp, tm=PLE, name="mm_d_ple_proj")
    dw("w_ple_gate", e, dpgl, tm=D, name="mm_d_ple_gate")
    de = mm(dpgl, w("w_ple_gate"), tb=True, tk=D, name="mm_de")
    dh2, dh2b, dg_ple = _rmsnorm_bwd(h2, small["g_ple"], de, dh3, name="norm_ple_bwd", jobs=jobs("norm_ple_bwd"))
    dw("w_down", act, dh2b, tm=FFN // 2, name="mm_d_down")
    dgate, dup = _swiglu_bwd(dh2b, w("w_down"), gate, up, name="swiglu_bwd", jobs=jobs("swiglu_bwd"))
    dw("w_gate", f, dgate, tm=D, name="mm_d_gate")
    dw("w_up", f, dup, tm=D, name="mm_d_up")
    df = mm(dgate, w("w_gate"), tb=True, tn=1024, tk=tkl, name="mm_df_gate")
    df = mm(dup, w("w_up"), tb=True, add=df, tk=tkl, name="mm_df_up")
    dh1, dh1b, dg_ffn = _rmsnorm_bwd(h1, small["g_ffn"], df, dh2, name="norm_ffn_bwd", jobs=jobs("norm_ffn_bwd"))
    dw("w_o", merged, dh1b, tm=D, name="mm_d_o")
    dmerged = mm(dh1b, w("w_o"), tb=True, tk=D, name="mm_dmerged")
    dout_a, dout_s, dproj = _merge_bwd(proj, out_a, out_s, dmerged, name="merge_bwd")
    dw("w_attn_br", attn, dout_a, tm=QD, name="mm_d_attn_br")
    dw("w_ssd_br", yn, dout_s, tm=DI, name="mm_d_ssd_br")
    dattn = mm(dout_a, w("w_attn_br"), tb=True, tk=D, name="mm_dattn")
    dyn = mm(dout_s, w("w_ssd_br"), tb=True, tk=D, name="mm_dyn")
    dproj, dsinks = _attn_bwd(proj, cos, sin, small["sinks"], attn, lse, dattn, dproj, name="attn_bwd",
                              jobs=jobs("attn_bwd"))
    dy_pre, dproj, dg_ssd = _gated_norm_bwd(y_pre, proj, small["g_ssd"], dyn, dproj, name="gated_norm_bwd",
                                            jobs=jobs("gated_norm_bwd"))
    dxs, db, dc, dproj, ddtb, dalog, ddsk = _ssd_bwd(xbc_act, proj, dtb, alog, dsk, states, dy_pre, dproj, name="ssd_bwd",
                                                     jobs=jobs("ssd_bwd"))
    dproj, dconv_w, dconv_b = _conv_bwd(proj, small["conv_w"], small["conv_b"], dxs, db, dc, dproj, name="conv_bwd",
                                        jobs=jobs("conv_bwd"))
    for which, h in (("send", 1 - plan.core), ("keep", plan.core)):
        uh = lax.dynamic_slice_in_dim(u, h * (D // 2), D // 2, axis=1)
        name = "mm_d_in_" + which
        plan.g_half("w_in", which, _matmul(uh, dproj, ta=True, out_dtype=BF16, tm=D // 2, tn=1024, tk=t, name=name,
                                           jobs=jobs(name)))
    du = mm(dproj, w("w_in"), tb=True, tn=1024, tk=tkl, name="mm_du")
    grad_x, _, dg_mix = _rmsnorm_bwd(x, small["g_mix"], du, dh1, name="norm_mix_bwd", jobs=jobs("norm_mix_bwd"))

    gs = {
        "g_mix": dg_mix[:1], "conv_w": dconv_w[:CW], "conv_b": dconv_b[:1], "dt_bias": ddtb[:1, :NH],
        "a_log": dalog[:1, :NH], "d_skip": ddsk[:1, :NH], "g_ssd": dg_ssd[:1], "sinks": dsinks[:1, :NQH],
        "g_ffn": dg_ffn[:1], "g_ple": dg_ple[:1], "g_final": dg_final[0],
    }
    return loss, grad_x, gs


def _to_kernel_cols(w):
    seg = lambda o, n: w[:, o:o + n]
    return jnp.concatenate([seg(R_GA, D), seg(R_GS, D), seg(R_Z, DI), seg(R_XBC, CONV), seg(R_Q, QD), seg(R_K, KVD),
                            seg(R_V, KVD), seg(R_DT, NH), jnp.zeros((w.shape[0], DT_PAD - NH), w.dtype)], axis=1)


def _from_kernel_cols(g):
    seg = lambda o, n: g[:, o:o + n]
    return jnp.concatenate([seg(O_Q, QD), seg(O_K, KVD), seg(O_V, KVD), seg(O_Z, DI), seg(O_XBC, CONV), seg(O_DT, NH),
                            seg(O_GA, D), seg(O_GS, D)], axis=1)


def _shard_pieces():
    segs = ((R_Q, QD, O_Q), (R_K, KVD, O_K), (R_V, KVD, O_V), (R_Z, DI, O_Z), (R_XBC, CONV, O_XBC), (R_DT, NH, O_DT),
            (R_GA, D, O_GA), (R_GS, D, O_GS))
    cs = IN_DIM // NCHIP
    out = []
    for j in range(NCHIP):
        for r0, n, k0 in segs:
            lo, hi = max(r0, j * cs), min(r0 + n, (j + 1) * cs)
            if lo < hi:
                out.append((j, lo - j * cs, hi - lo, k0 + lo - r0))
    return out


SLAB = IN_DIM // NCHIP
SLAB_PAD = -(-SLAB // 128) * 128
REMAP_ROWS = 256


def _lane_remap(src, dst_slabs, dst_cols, moves, *, name, add=None, jobs=()):
    s_n, rows, s_cols = src.shape
    assert s_cols % 128 == 0 and dst_cols % 128 == 0 and rows % REMAP_ROWS == 0
    half = REMAP_ROWS // 2

    def body(s_ref, *refs):
        d_ref = refs[-1]
        lane = lax.broadcasted_iota(jnp.int32, (half, 128), 1)
        tiles = {}

        def tile(j, m):
            if (j, m) not in tiles:
                tiles[j, m] = pltpu.bitcast(s_ref[j, :, 128 * m:128 * (m + 1)], jnp.uint32)
            return tiles[j, m]

        def window(j, base):
            m0, s = base // 128, base % 128
            left = tile(j, m0) if 0 <= m0 < s_cols // 128 else None
            if s == 0:
                return left
            right = tile(j, m0 + 1) if 0 <= m0 + 1 < s_cols // 128 else None
            left = None if left is None else pltpu.roll(left, 128 - s, 1)
            right = None if right is None else pltpu.roll(right, 128 - s, 1)
            if left is None or right is None:
                return right if left is None else left
            return jnp.where(lane < 128 - s, left, right)

        for ds in range(dst_slabs):
            for t in range(dst_cols // 128):
                o = 128 * t
                acc = jnp.zeros((half, 128), jnp.uint32)
                for sj, sc, n, dj, dc in moves:
                    lo, hi = max(o, dc) - o, min(o + 128, dc + n) - o
                    if dj != ds or lo >= hi:
                        continue
                    win = window(sj, o - dc + sc)
                    acc = win if (lo, hi) == (0, 128) else jnp.where(jnp.logical_and(lane >= lo, lane < hi), win, acc)
                out = pltpu.bitcast(acc, BF16)
                if add is not None:
                    out = (out.astype(F32) + refs[0][ds, :, o:o + 128].astype(F32)).astype(BF16)
                d_ref[ds, :, o:o + 128] = out

    dst_blk = pl.BlockSpec((dst_slabs, REMAP_ROWS, dst_cols), lambda i: (0, i, 0))
    return _call(
        body, jobs=jobs, name=name, out_shape=jax.ShapeDtypeStruct((dst_slabs, rows, dst_cols), BF16),
        grid=(rows // REMAP_ROWS,),
        in_specs=[pl.BlockSpec((s_n, REMAP_ROWS, s_cols), lambda i: (0, i, 0))] + ([dst_blk] if add is not None else []),
        out_specs=dst_blk, compiler_params=_cp(("parallel",)),
    )(*((src,) if add is None else (src, add)))


def _slabs_to_kernel_cols(slabs, *, name, jobs=()):
    moves = [(j, a, n, 0, k0) for j, a, n, k0 in _shard_pieces()]
    return _lane_remap(slabs, 1, NP, moves, name=name, jobs=jobs)[0]


def _kernel_cols_to_slabs(g, *, name, add=None, jobs=()):
    moves = [(0, k0, n, j, a) for j, a, n, k0 in _shard_pieces()]
    return _lane_remap(g[None], NCHIP, SLAB_PAD, moves, name=name, add=add, jobs=jobs)


RELS = ((0, 1), (1, 0), (1, 1))
MATS = {
    n: (n, kind, 1, r, c, tp, tf) for n, kind, r, c, tp, tf in (
        ("w_in", "stk", 2048, SLAB_PAD, 256, 256),
        ("w_attn_br", "col", 1024, 512, 256, 256),
        ("w_ssd_br", "row", 512, 2048, 512, 256),
        ("w_o", "row", 512, 2048, 512, 256),
        ("w_gate", "col", 2048, 1408, 256, 256),
        ("w_up", "col", 2048, 1408, 256, 256),
        ("w_down", "row", 1408, 2048, 704, 704),
        ("w_ple_gate", "row", 512, 2048, 512, 256),
        ("w_ple_proj", "col", 256, 512, 128, 128),
    )}


def _pos():
    return lax.axis_index("x"), lax.axis_index("y"), lax.axis_index("c")


def _flip(v, a):
    return 1 - v if a else v


def _remote(src, dst, send, recv, dev):
    return pltpu.make_async_remote_copy(src_ref=src, dst_ref=dst, send_sem=send, recv_sem=recv, device_id=dev,
                                        device_id_type=MESH)


def _whole_shape(kind, g, r, c):
    return {"row": (g, NCHIP * r, c), "col": (g, r, NCHIP * c), "stk": (NCHIP, r, c)}[kind]


def _cols(j, c):
    return pl.ds(pl.multiple_of(j * c, 128), c)


def _whole_shard(kind, ref, j, r, c):
    if kind == "row":
        return ref.at[:, pl.ds(j * r, r), :]
    if kind == "col":
        return ref.at[:, :, _cols(j, c)]
    return ref.at[pl.ds(j, 1)]


def _whole_rows(kind, ref, j, row, n, r, c):
    if kind == "row":
        return ref.at[:, pl.ds(j * r + row, n), :]
    if kind == "col":
        return ref.at[:, pl.ds(row, n), _cols(j, c)]
    return ref.at[pl.ds(j, 1), pl.ds(row, n), :]


class _GatherJob(_Job):
    has_mid = True
    NCP = 13

    def __init__(self, names, shards, sink):
        self.mats = [MATS[n] for n in names]
        self.srcs = [shards[n] for n in names]
        self.news = [jax.ShapeDtypeStruct(_whole_shape(kind, g, r, c), BF16) for _, kind, g, r, c, _, _ in self.mats]
        n = len(names)
        self.scratch = [pltpu.SemaphoreType.DMA((self.NCP * n,)), pltpu.SemaphoreType.DMA((self.NCP * n,))]
        self.names, self.sink = names, sink

    def _copies(self, srcs, news, sems):
        send, recv = sems
        x, y, c = _pos()
        me, jx, jy, jd = 2 * x + y, 2 * (1 - x) + y, 2 * x + (1 - y), 2 * (1 - x) + (1 - y)
        nbx, nby, sib = (1 - x, y, c), (x, 1 - y, c), (x, y, 1 - c)
        cps = []
        for w, (_, kind, g, r, cc, _, _) in enumerate(self.mats):
            hr, qr = r // 2, r // 4
            at = lambda j, h, q, n: _whole_rows(kind, news[w], j, h * hr + q * qr, n, r, cc)
            mine = lambda q: srcs[w].at[:, pl.ds(c * hr + q * qr, qr), :]
            cp = lambda k, s, d, dev: _remote(s, d, send.at[self.NCP * w + k], recv.at[self.NCP * w + k], dev)
            cps.append([
                cp(0, mine(0), at(me, c, 0, qr), nbx), cp(1, mine(1), at(me, c, 1, qr), nbx),
                cp(2, mine(1), at(me, c, 1, qr), nby), cp(3, mine(0), at(me, c, 0, qr), nby),
                cp(4, at(jx, c, 0, qr), at(jx, c, 0, qr), nby), cp(5, at(jy, c, 1, qr), at(jy, c, 1, qr), nbx),
                cp(6, at(jx, c, 0, qr), at(jx, c, 0, qr), sib), cp(7, at(jx, c, 1, qr), at(jx, c, 1, qr), sib),
                cp(8, at(jy, c, 1, qr), at(jy, c, 1, qr), sib), cp(9, at(jy, c, 0, qr), at(jy, c, 0, qr), sib),
                cp(10, at(jd, c, 0, qr), at(jd, c, 0, qr), sib), cp(11, at(jd, c, 1, qr), at(jd, c, 1, qr), sib),
                cp(12, srcs[w], _whole_shard(kind, news[w], me, r, cc), sib)])
        return cps

    def _pass_on(self, srcs, news, sems, pairs):
        cps = self._copies(srcs, news, sems)
        for w in range(len(self.mats)):
            for arrived, onward in pairs:
                cps[w][arrived].wait_recv()
                for k in onward:
                    cps[w][k].start()

    def start(self, srcs, dsts, news, sems):
        cps = self._copies(srcs, news, sems)
        for w in range(len(self.mats)):
            for k in (0, 1, 2, 3, 12):
                cps[w][k].start()

    def mid(self, srcs, dsts, news, sems):
        self._pass_on(srcs, news, sems, ((0, (4, 6)), (2, (5, 8))))

    def late(self, srcs, dsts, news, sems):
        self._pass_on(srcs, news, sems, ((1, (7,)), (3, (9,))))

    def finish(self, srcs, dsts, news, sems):
        self._pass_on(srcs, news, sems, ((4, (10,)), (5, (11,))))
        cps = self._copies(srcs, news, sems)
        for w in range(len(self.mats)):
            for k in (6, 7, 8, 9, 10, 11, 12):
                cps[w][k].wait_recv()
            for k in range(self.NCP):
                cps[w][k].wait_send()

    def done(self, dsts, news):
        for n, a in zip(self.names, news):
            self.sink[n] = a


class _SwapJob(_Job):
    def __init__(self, build, ncopies, *, srcs=(), dsts=(), news=(), done=None):
        self.build, self.srcs, self.dsts, self.news, self._done = build, list(srcs), list(dsts), list(news), done
        self.scratch = [pltpu.SemaphoreType.DMA((ncopies,)), pltpu.SemaphoreType.DMA((ncopies,))]

    def start(self, srcs, dsts, news, sems):
        for cp in self.build(srcs, dsts, news, *sems):
            cp.start()

    def finish(self, srcs, dsts, news, sems):
        for cp in self.build(srcs, dsts, news, *sems):
            cp.wait()

    def done(self, dsts, news):
        if self._done is not None:
            self._done(dsts, news)


def _half_of_whole(kind, ref, h, r, c):
    if kind == "row":
        return ref.at[:, :, pl.ds(pl.multiple_of(h * (c // 2), 128), c // 2)]
    return ref.at[:, pl.ds(h * (r // 2), r // 2), :]


def _half_shape(kind, g, r, c):
    return {"row": (g, NCHIP * r, c // 2), "col": (g, r // 2, NCHIP * c), "stk": (NCHIP, r // 2, c)}[kind]


def _sub_shape(kind, r, c):
    return {"row": (1, r // 2, c // 2), "col": (1, r // 4, c), "stk": (1, r // 4, c)}[kind]


def _sub_of_half(kind, ref, j, p, r, c):
    sr = _sub_shape(kind, r, c)[1]
    if kind == "row":
        return ref.at[:, pl.ds(j * r + p * sr, sr), :]
    if kind == "col":
        return ref.at[:, pl.ds(p * sr, sr), _cols(j, c)]
    return ref.at[pl.ds(j, 1), pl.ds(p * sr, sr), :]


def _sub_tile(sr):
    return 256 if sr % 256 == 0 else sr


def _half_of_shard(kind, ref, h, r, c):
    if kind == "row":
        return ref.at[:, :, pl.ds(pl.multiple_of(h * (c // 2), 128), c // 2)]
    return ref.at[:, pl.ds(h * (r // 2), r // 2), :]


def _pair_sum(pack, core, mine, got, whole=True):
    name, kind, g, r, c, tr, _ = pack
    hs = _half_shape(kind, g, r, c)
    nb = hs[1] // tr

    def body(core_ref, a_ref, b_ref, o_ref):
        o_ref[...] = (a_ref[...].astype(F32) + b_ref[...].astype(F32)).astype(BF16)

    blk = (1, tr, hs[2])
    same = lambda gi, i, core_ref: (gi, i, 0)
    if not whole:
        a_map = same
    elif kind == "row":
        a_map = lambda gi, i, core_ref: (gi, i, core_ref[0])
    else:
        a_map = lambda gi, i, core_ref: (gi, core_ref[0] * nb + i, 0)
    return pl.pallas_call(
        body, name="pair_sum_" + name, out_shape=jax.ShapeDtypeStruct(hs, BF16),
        grid_spec=pltpu.PrefetchScalarGridSpec(
            num_scalar_prefetch=1, grid=(hs[0], nb),
            in_specs=[pl.BlockSpec(blk, a_map), pl.BlockSpec(blk, same)], out_specs=pl.BlockSpec(blk, same)),
        compiler_params=_cp(("parallel", "parallel")),
    )(core, mine, got)


def _sub_sums(pack, idx, half, got, *, name):
    _, kind, g, r, c, _, _ = pack
    _, sr, sc = _sub_shape(kind, r, c)
    tr = _sub_tile(sr)
    nb = sr // tr

    def body(idx_ref, a_ref, ga_ref, b_ref, gb_ref, k_ref, p_ref):
        k_ref[0, 0] = a_ref[0].astype(F32) + ga_ref[0, 0].astype(F32)
        p_ref[0, 0] = (b_ref[0].astype(F32) + gb_ref[0, 0].astype(F32)).astype(BF16)

    def sub_map(o):
        if kind == "row":
            return lambda q, i, ix: (0, ix[4 * q + o] * (r // tr) + ix[4 * q + o + 1] * nb + i, 0)
        if kind == "col":
            return lambda q, i, ix: (0, ix[4 * q + o + 1] * nb + i, ix[4 * q + o])
        return lambda q, i, ix: (ix[4 * q + o], ix[4 * q + o + 1] * nb + i, 0)

    sub = lambda o: pl.BlockSpec((1, tr, sc), sub_map(o))
    got_blk = lambda o: pl.BlockSpec((1, 1, tr, sc), lambda q, i, ix: (2 * q + o, 0, i, 0))
    out_blk = pl.BlockSpec((1, 1, tr, sc), lambda q, i, ix: (q, 0, i, 0))
    return pl.pallas_call(
        body, name=name,
        out_shape=(jax.ShapeDtypeStruct((2, 1, sr, sc), F32), jax.ShapeDtypeStruct((2, 1, sr, sc), BF16)),
        grid_spec=pltpu.PrefetchScalarGridSpec(
            num_scalar_prefetch=1, grid=(2, nb), in_specs=[sub(0), got_blk(0), sub(2), got_blk(1)],
            out_specs=(out_blk, out_blk)),
        compiler_params=_cp(("parallel", "parallel")),
    )(idx, half, got, half, got)


def _shard_sum(pack, core, keep, got):
    name, kind, g, r, c, _, _ = pack
    _, sr, sc = _sub_shape(kind, r, c)
    tr = _sub_tile(sr)
    nb = sr // tr

    def body(core_ref, a_ref, b_ref, o_ref):
        o_ref[0] = a_ref[0, 0] + b_ref[0, 0].astype(F32)

    blk = pl.BlockSpec((1, 1, tr, sc), lambda p, i, cr: (p, 0, i, 0))
    if kind == "row":
        o_map = lambda p, i, cr: (0, p * nb + i, cr[0])
    else:
        o_map = lambda p, i, cr: (0, cr[0] * 2 * nb + p * nb + i, 0)
    return pl.pallas_call(
        body, name="shard_sum_" + name, out_shape=jax.ShapeDtypeStruct((g, r, c), F32),
        grid_spec=pltpu.PrefetchScalarGridSpec(
            num_scalar_prefetch=1, grid=(2, nb), in_specs=[blk, blk], out_specs=pl.BlockSpec((1, tr, sc), o_map)),
        compiler_params=_cp(("parallel", "parallel")),
    )(core, keep, got)


class _Plan:
    def __init__(self, shards, table):
        self.shards, self.table = shards, table
        self.whole, self.grad, self.got_a, self.half, self.gshard = {}, {}, {}, {}, {}
        self.got_b1, self.kept, self.pass_on, self.got_b2 = {}, {}, {}, {}
        x, y, c = _pos()
        me, jx, jy = 2 * x + y, 2 * (1 - x) + y, 2 * x + (1 - y)
        self.core = c
        self.core1 = c.reshape(1).astype(jnp.int32)
        zero = 0 * me
        self.idx_sums = jnp.stack([me, zero, jy, zero, me, zero + 1, jx, zero + 1]).astype(jnp.int32)
        self._w_in = None
        self.send, self.keep = {}, {}

    def w(self, n):
        if n != "w_in":
            return self.whole[n][0]
        if self._w_in is None:
            self._w_in = _slabs_to_kernel_cols(self.whole[n], name="relayout_w_in", jobs=self.jobs("relayout_w_in"))
        return self._w_in

    def g(self, n, a):
        self.grad[n] = a[None]

    def g_half(self, n, which, a):
        if which == "keep" and n in self.got_a:
            self.half[n] = _kernel_cols_to_slabs(a, name="relayout_d_in_keep", add=self.got_a[n])
        else:
            (self.send if which == "send" else self.keep)[n] = _kernel_cols_to_slabs(a, name="relayout_d_in_" + which)

    def jobs(self, tag):
        out = []
        for spec in self.table.get(tag, ()):
            out += getattr(self, "_" + spec[0])(*spec[1:])
        return out

    def run(self, name, jobs):
        if jobs:
            _call(lambda: None, jobs=jobs, name=name, out_shape=[], in_specs=[], out_specs=[])()

    def _gather(self, names):
        return [_GatherJob(names, self.shards, self.whole)]

    def _rs_a(self, names):
        mats = [MATS[n] for n in names]

        def build(srcs, dsts, news, send, recv):
            x, y, c = _pos()
            return [_remote(srcs[i] if names[i] in self.send else _half_of_whole(kind, srcs[i], 1 - c, r, cc), news[i],
                            send.at[i], recv.at[i], (x, y, 1 - c))
                    for i, (_, kind, g, r, cc, _, _) in enumerate(mats)]

        def done(dsts, news):
            self.got_a.update(zip(names, news))

        return [_SwapJob(build, len(names), srcs=[self.send.get(n, self.grad.get(n)) for n in names], done=done,
                         news=[jax.ShapeDtypeStruct(_half_shape(kind, g, r, c), BF16) for _, kind, g, r, c, _, _ in mats])]

    def _rs_b1(self, names):
        mats = [MATS[n] for n in names]
        for n in names:
            if n in self.half:
                continue
            if n in self.keep:
                self.half[n] = _pair_sum(MATS[n], self.core1, self.keep[n], self.got_a[n], whole=False)
            else:
                self.half[n] = _pair_sum(MATS[n], self.core1, self.grad[n], self.got_a[n])

        def build(srcs, dsts, news, send, recv):
            x, y, c = _pos()
            jx, jy, jd = 2 * (1 - x) + y, 2 * x + (1 - y), 2 * (1 - x) + (1 - y)
            nbx, nby = (1 - x, y, c), (x, 1 - y, c)
            cps = []
            for i, (_, kind, g, r, cc, _, _) in enumerate(mats):
                sub = lambda j, p: _sub_of_half(kind, srcs[i], j, p, r, cc)
                for k, (j, p, dev) in enumerate(((jx, 0, nbx), (jd, 0, nbx), (jy, 1, nby), (jd, 1, nby))):
                    cps.append(_remote(sub(j, p), news[i].at[k], send.at[4 * i + k], recv.at[4 * i + k], dev))
            return cps

        def done(dsts, news):
            self.got_b1.update(zip(names, news))

        return [_SwapJob(build, 4 * len(names), srcs=[self.half[n] for n in names], done=done,
                         news=[jax.ShapeDtypeStruct((4,) + _sub_shape(kind, r, c), BF16) for _, kind, g, r, c, _, _ in mats])]

    def _rs_b2(self, names):
        mats = [MATS[n] for n in names]
        for n in names:
            self.kept[n], self.pass_on[n] = _sub_sums(MATS[n], self.idx_sums, self.half[n], self.got_b1[n], name="sums_" + n)

        def build(srcs, dsts, news, send, recv):
            x, y, c = _pos()
            cps = []
            for i in range(len(mats)):
                cps.append(_remote(srcs[i].at[0], news[i].at[0], send.at[2 * i], recv.at[2 * i], (x, 1 - y, c)))
                cps.append(_remote(srcs[i].at[1], news[i].at[1], send.at[2 * i + 1], recv.at[2 * i + 1], (1 - x, y, c)))
            return cps

        def done(dsts, news):
            self.got_b2.update(zip(names, news))

        return [_SwapJob(build, 2 * len(names), srcs=[self.pass_on[n] for n in names], done=done,
                         news=[jax.ShapeDtypeStruct((2,) + _sub_shape(kind, r, c), BF16) for _, kind, g, r, c, _, _ in mats])]

    def _rs_c(self, names):
        mats = [MATS[n] for n in names]
        parts = [_shard_sum(MATS[n], self.core1, self.kept[n], self.got_b2[n]) for n in names]

        def build(srcs, dsts, news, send, recv):
            x, y, c = _pos()
            cps = []
            for i, (_, kind, g, r, cc, _, _) in enumerate(mats):
                mine = _half_of_shard(kind, dsts[i], c, r, cc)
                cps.append(_remote(mine, mine, send.at[i], recv.at[i], (x, y, 1 - c)))
            return cps

        def done(dsts, news):
            self.gshard.update(zip(names, dsts))

        return [_SwapJob(build, len(names), dsts=parts, done=done)]

    def finish(self, n):
        if n not in self.got_a:
            self.run("rs_a_" + n, self._rs_a((n,)))
        if n not in self.got_b1:
            self.run("rs_b1_" + n, self._rs_b1((n,)))
        if n not in self.got_b2:
            self.run("rs_b2_" + n, self._rs_b2((n,)))
        if n not in self.gshard:
            self.run("rs_c_" + n, self._rs_c((n,)))
        return self.gshard[n]


TABLE = {
    "gather_w_in": (("gather", ("w_in",)),),
    "relayout_w_in": (("gather", ("w_gate",)),),
    "mm_in": (("gather", ("w_up",)),),
    "attn_fwd": (("gather", ("w_attn_br", "w_ssd_br")),),
    "ssd_fwd": (("gather", ("w_o",)),),
    "swiglu_fwd": (("gather", ("w_down",)),),
    "mm_down": (("gather", ("w_ple_gate", "w_ple_proj")),),
    "mm_de": (("rs_a", ("w_ple_proj", "w_ple_gate")),),
    "mm_d_down": (("rs_b1", ("w_ple_proj", "w_ple_gate")),),
    "swiglu_bwd": (("rs_a", ("w_down",)), ("rs_b2", ("w_ple_proj", "w_ple_gate"))),
    "mm_d_gate": (("rs_b1", ("w_down",)),),
    "mm_d_up": (("rs_b2", ("w_down",)), ("rs_c", ("w_ple_proj", "w_ple_gate")), ("rs_a", ("w_gate",))),
    "mm_df_gate": (("rs_b1", ("w_gate",)), ("rs_a", ("w_up",)), ("rs_c", ("w_down",))),
    "mm_df_up": (("rs_b2", ("w_gate",)),),
    "norm_ffn_bwd": (("rs_c", ("w_gate",)),),
    "mm_dmerged": (("rs_a", ("w_o",)),),
    "mm_dyn": (("rs_a", ("w_attn_br", "w_ssd_br")),),
    "attn_bwd": (("rs_b1", ("w_up",)),),
    "gated_norm_bwd": (("rs_b2", ("w_up",)),),
    "ssd_bwd": (("rs_b1", ("w_o", "w_attn_br", "w_ssd_br")), ("rs_c", ("w_up",))),
    "conv_bwd": (("rs_b2", ("w_o", "w_attn_br", "w_ssd_br")),),
    "mm_d_in_send": (("rs_c", ("w_o", "w_attn_br", "w_ssd_br")),),
    "mm_d_in_keep": (("rs_a", ("w_in",)),),
    "mm_du": (("rs_b1", ("w_in",)),),
    "norm_mix_bwd": (("rs_b2", ("w_in",)),),
}


NDEV = 8


def _allreduce_small(v, *, name):
    rows = v.shape[0]

    def body(v_ref, o_ref, slots, send, recv):
        x, y, c = _pos()
        me = 4 * x + 2 * y + c
        slots[me] = v_ref[...]
        cps = []
        for k in range(1, NDEV):
            peer = (_flip(x, k & 4), _flip(y, k & 2), _flip(c, k & 1))
            cp = _remote(v_ref, slots.at[me], send.at[k - 1], recv.at[k - 1], peer)
            cp.start()
            cps.append(cp)
        for cp in cps:
            cp.wait()
        acc = slots[0]
        for s in range(1, NDEV):
            acc = acc + slots[s]
        o_ref[...] = acc

    return pl.pallas_call(
        body, name=name, out_shape=jax.ShapeDtypeStruct((rows, 128), F32),
        in_specs=[pl.BlockSpec(memory_space=pltpu.VMEM)], out_specs=pl.BlockSpec(memory_space=pltpu.VMEM),
        scratch_shapes=[pltpu.VMEM((NDEV, rows, 128), F32), pltpu.SemaphoreType.DMA((NDEV - 1,)),
                        pltpu.SemaphoreType.DMA((NDEV - 1,))],
    )(v)


def _adamw(w, g, m, v, *, name, tr=None, tc=None, jobs=()):
    r, c = w.shape
    tr = r if tr is None else tr
    c1 = 1.0 / (1.0 - B1 ** STEP)
    c2 = 1.0 / (1.0 - B2 ** STEP)

    def body(w_ref, g_ref, m_ref, v_ref, d_ref, mo_ref, vo_ref):
        gv = g_ref[...]
        mn = B1 * m_ref[...] + (1.0 - B1) * gv
        vn = B2 * v_ref[...] + (1.0 - B2) * (gv * gv)
        mo_ref[...] = mn
        vo_ref[...] = vn
        d_ref[...] = -LR * ((mn * c1) / (jnp.sqrt(vn * c2) + AEPS) + WD * w_ref[...])

    if tc is None:
        blk, grid = pl.BlockSpec((tr, c), lambda i: (i, 0)), (r // tr,)
    else:
        blk, grid = pl.BlockSpec((r, tc), lambda i: (0, i)), (c // tc,)
    o = jax.ShapeDtypeStruct((r, c), F32)
    return _call(
        body, jobs=jobs, name=name, out_shape=(o, o, o), grid=grid, in_specs=[blk] * 4, out_specs=(blk, blk, blk),
        compiler_params=_cp(("parallel",)),
    )(w, g, m, v)


WEIGHTS = ("g_mix", "w_in", "conv_w", "conv_b", "dt_bias", "a_log", "d_skip", "g_ssd", "sinks", "w_attn_br", "w_ssd_br",
           "w_o", "g_ffn", "w_gate", "w_up", "w_down", "g_ple", "w_ple_gate", "w_ple_proj", "g_final")
BIG = {
    "w_gate": 256, "w_up": 256, "w_down": 128, "w_ssd_br": 128, "w_o": 128, "w_ple_gate": 128, "w_attn_br": 256,
    "w_ple_proj": 256, "w_in": None,
}
SMALL = tuple(n for n in WEIGHTS if n not in BIG)


def _pack_small(parts):
    rows = []
    for a in parts:
        a = a.reshape(-1)
        rows.append(jnp.pad(a, (0, -a.shape[0] % 128)).reshape(-1, 128))
    out = jnp.concatenate(rows, axis=0)
    return jnp.pad(out, ((0, -out.shape[0] % 8), (0, 0)))


def _unpack_small(packed, shapes):
    out, r = [], 0
    for s in shapes:
        n = int(np.prod(s))
        nr = -(-n // 128)
        out.append(packed[r:r + nr].reshape(-1)[:n].reshape(s))
        r += nr
    return out


def kernel(x, p, positions, g_mix, w_in, conv_w, conv_b, dt_bias, a_log, d_skip, g_ssd, sinks, w_attn_br, w_ssd_br, w_o, g_ffn, w_gate, w_up, w_down, g_ple, w_ple_gate, w_ple_proj, g_final, loss_target, m_g_mix, m_w_in, m_conv_w, m_conv_b, m_dt_bias, m_a_log, m_d_skip, m_g_ssd, m_sinks, m_w_attn_br, m_w_ssd_br, m_w_o, m_g_ffn, m_w_gate, m_w_up, m_w_down, m_g_ple, m_w_ple_gate, m_w_ple_proj, m_g_final, v_g_mix, v_w_in, v_conv_w, v_conv_b, v_dt_bias, v_a_log, v_d_skip, v_g_ssd, v_sinks, v_w_attn_br, v_w_ssd_br, v_w_o, v_g_ffn, v_w_gate, v_w_up, v_w_down, v_g_ple, v_w_ple_gate, v_w_ple_proj, v_g_final):
    w = dict(zip(WEIGHTS, (g_mix, w_in, conv_w, conv_b, dt_bias, a_log, d_skip, g_ssd, sinks, w_attn_br, w_ssd_br, w_o,
                           g_ffn, w_gate, w_up, w_down, g_ple, w_ple_gate, w_ple_proj, g_final)))
    m = dict(zip(WEIGHTS, (m_g_mix, m_w_in, m_conv_w, m_conv_b, m_dt_bias, m_a_log, m_d_skip, m_g_ssd, m_sinks, m_w_attn_br,
                           m_w_ssd_br, m_w_o, m_g_ffn, m_w_gate, m_w_up, m_w_down, m_g_ple, m_w_ple_gate, m_w_ple_proj,
                           m_g_final)))
    v = dict(zip(WEIGHTS, (v_g_mix, v_w_in, v_conv_w, v_conv_b, v_dt_bias, v_a_log, v_d_skip, v_g_ssd, v_sinks, v_w_attn_br,
                           v_w_ssd_br, v_w_o, v_g_ffn, v_w_gate, v_w_up, v_w_down, v_g_ple, v_w_ple_gate, v_w_ple_proj,
                           v_g_final)))
    xi, yi, ci = _pos()
    chip = 2 * xi + yi
    t = x.shape[1]
    cshard = CONV // NCHIP

    shards = {n: w[n].astype(BF16) for n in MATS}
    shards["w_in"] = jnp.pad(shards["w_in"], ((0, 0), (0, 0), (0, SLAB_PAD - SLAB)))
    plan = _Plan(shards, TABLE)
    plan.run("gather_w_in", plan.jobs("gather_w_in"))
    placed = lax.dynamic_update_slice(jnp.zeros((CW, CONV), F32), w["conv_w"][0], (0, chip * cshard))
    conv_whole = _allreduce_small(jnp.where(ci == 0, placed, 0.0).reshape(-1, 128), name="gather_conv_w").reshape(CW, CONV)

    small = {n: w[n] for n in ("g_mix", "conv_b", "dt_bias", "a_log", "d_skip", "g_ssd", "sinks", "g_ffn", "g_ple", "g_final")}
    small["conv_w"] = conv_whole
    loss8, grad_x, gs = _local_step(x[0], p[0, 0], positions, loss_target[0], small, plan)

    order = ("g_mix", "conv_b", "dt_bias", "a_log", "d_skip", "g_ssd", "sinks", "g_ffn", "g_ple", "g_final", "conv_w")
    summed = _allreduce_small(_pack_small([loss8[0, :1]] + [gs[n] for n in order]), name="sum_small")
    parts = _unpack_small(summed, [(1,)] + [w[n].shape for n in order[:-1]] + [(CW, CONV)])
    loss = parts[0][0]
    grad = dict(zip(order, parts[1:]))
    grad["conv_w"] = lax.dynamic_slice(grad["conv_w"], (0, chip * cshard), (CW, cshard))[None]

    delta, new_m, new_v = {}, {}, {}
    for n, tr in BIG.items():
        grad[n] = plan.finish(n)[:, :, :w[n].shape[2]]
        if n == "w_in":
            d_, m_, v_ = _adamw(w[n][0].T, grad[n][0].T, m[n][0].T, v[n][0].T, tc=128, name="adamw_" + n)
            d_, m_, v_ = d_.T, m_.T, v_.T
        else:
            d_, m_, v_ = _adamw(w[n][0], grad[n][0], m[n][0], v[n][0], tr=tr, name="adamw_" + n)
        delta[n], new_m[n], new_v[n] = d_[None], m_[None], v_[None]
    shapes = [w[n].shape for n in SMALL]
    d_, m_, v_ = _adamw(_pack_small([w[n] for n in SMALL]), _pack_small([grad[n] for n in SMALL]),
                        _pack_small([m[n] for n in SMALL]), _pack_small([v[n] for n in SMALL]), tr=None, name="adamw_small")
    for n, a, b, c_ in zip(SMALL, _unpack_small(d_, shapes), _unpack_small(m_, shapes), _unpack_small(v_, shapes)):
        delta[n], new_m[n], new_v[n] = a, b, c_

    return (loss, grad_x[None], *[grad[n] for n in WEIGHTS], *[delta[n] for n in WEIGHTS],
            *[new_m[n] for n in WEIGHTS], *[new_v[n] for n in WEIGHTS])
```

```python
import functools

import jax
import jax.numpy as jnp
import numpy as np
from jax import lax
from jax.experimental import pallas as pl
from jax.experimental.pallas import tpu as pltpu

F32 = jnp.float32
BF16 = jnp.bfloat16
MESH = pl.DeviceIdType.MESH

D = 2048
HD = 64
NQH = 16
NKV = 4
QD = NQH * HD
KVD = NKV * HD
DI = 2048
NH = 32
NG = 4
NS = 128
CW = 4
L = 128
CONV = DI + 2 * NG * NS
FFN = 5632
PLE = 256
IN_DIM = QD + 2 * KVD + DI + CONV + NH + 2 * D
EPS = 1e-6
SSM_EPS = 1e-5
ROPE_THETA = 10000.0
LR, B1, B2, AEPS, WD, STEP = 0.001, 0.9, 0.999, 1e-08, 0.01, 10

O_GA, O_GS, O_Z, O_XBC, O_Q, O_K, O_V, O_DT = 0, 2048, 4096, 6144, 9216, 10240, 10496, 10752
DT_PAD = 512
NP = O_DT + DT_PAD
R_Q, R_K, R_V, R_Z, R_XBC, R_DT, R_GA, R_GS = 0, 1024, 1280, 1536, 3584, 6656, 6688, 8736

NCHIP = 4
VMEM_LIMIT = 52 * 1024 * 1024
NEG = -1e30


def _cp(sem=None):
    return pltpu.CompilerParams(dimension_semantics=sem, vmem_limit_bytes=VMEM_LIMIT)


def _dot(a, b):
    return lax.dot_general(a, b, (((1,), (0,)), ((), ())), preferred_element_type=F32)


def _dot_nt(a, b):
    return lax.dot_general(a, b, (((1,), (1,)), ((), ())), preferred_element_type=F32)


def _dot_tn(a, b):
    return lax.dot_general(a, b, (((0,), (0,)), ((), ())), preferred_element_type=F32)


def _sigmoid(x):
    return 1.0 / (1.0 + jnp.exp(-x))


def _bf16_dot(dot, da, db):
    @jax.custom_vjp
    def f(a, b):
        return dot(a.astype(BF16), b.astype(BF16))

    def fwd(a, b):
        return f(a, b), (a.astype(BF16), b.astype(BF16))

    def bwd(res, g):
        a, b = res
        g = g.astype(BF16)
        return da(g, a, b), db(g, a, b)

    f.defvjp(fwd, bwd)
    return f


_bdot = _bf16_dot(_dot, lambda g, a, b: _dot_nt(g, b), lambda g, a, b: _dot_tn(a, g))
_bdot_nt = _bf16_dot(_dot_nt, lambda g, a, b: _dot(g, b), lambda g, a, b: _dot_tn(g, a))
_bdot_tn = _bf16_dot(_dot_tn, lambda g, a, b: _dot_nt(b, g), lambda g, a, b: _dot(a, g))


ANY = pl.BlockSpec(memory_space=pl.ANY)


class _Job:
    srcs, dsts, news, scratch = (), (), (), ()
    has_mid = False

    def start(self, srcs, dsts, news, sems):
        raise NotImplementedError

    def mid(self, srcs, dsts, news, sems):
        pass

    def late(self, srcs, dsts, news, sems):
        pass

    def finish(self, srcs, dsts, news, sems):
        raise NotImplementedError

    def done(self, dsts, news):
        pass


def _call(body, *, jobs=(), name, out_shape, in_specs, out_specs, grid=(), scratch_shapes=(), compiler_params=None,
          aliases=None):
    jobs = [j for j in jobs if j is not None]
    aliases = dict(aliases or {})
    if not jobs:
        return pl.pallas_call(body, name=name, out_shape=out_shape, in_specs=in_specs, out_specs=out_specs, grid=grid,
                              scratch_shapes=scratch_shapes, compiler_params=compiler_params,
                              input_output_aliases=aliases)
    single = not isinstance(out_shape, (tuple, list))
    outs = [out_shape] if single else list(out_shape)
    ospecs = [out_specs] if single else list(out_specs)
    n_in, n_out, n_scr = len(in_specs), len(outs), len(scratch_shapes)
    srcs = [a for j in jobs for a in j.srcs]
    dsts = [a for j in jobs for a in j.dsts]
    news = [a for j in jobs for a in j.news]
    sems = [a for j in jobs for a in j.scratch]

    def wrapped(*refs):
        pos = n_in + len(srcs) + len(dsts)
        ins, jsrc = refs[:n_in], refs[n_in:n_in + len(srcs)]
        o_refs = refs[pos:pos + n_out]
        pos += n_out
        jdst, jnew = refs[pos:pos + len(dsts)], refs[pos + len(dsts):pos + len(dsts) + len(news)]
        pos += len(dsts) + len(news)
        scr, jsem = refs[pos:pos + n_scr], refs[pos + n_scr:]

        def run(which):
            a = b = c = d = 0
            for j in jobs:
                getattr(j, which)(jsrc[a:a + len(j.srcs)], jdst[b:b + len(j.dsts)], jnew[c:c + len(j.news)],
                                  jsem[d:d + len(j.scratch)])
                a, b, c, d = a + len(j.srcs), b + len(j.dsts), c + len(j.news), d + len(j.scratch)

        if not grid:
            run("start")
            run("mid")
            run("late")
            body(*ins, *o_refs, *scr)
            run("finish")
            return
        step = functools.reduce(lambda acc, a: acc * grid[a] + pl.program_id(a), range(len(grid)), 0)
        steps = int(np.prod(grid))
        pl.when(step == 0)(lambda: run("start"))
        if any(j.has_mid for j in jobs):
            pl.when(step == steps // 3)(lambda: run("mid"))
            pl.when(step == (2 * steps) // 3)(lambda: run("late"))
        body(*ins, *o_refs, *scr)
        pl.when(step == steps - 1)(lambda: run("finish"))

    call = pl.pallas_call(
        wrapped, name=name,
        out_shape=outs + [jax.ShapeDtypeStruct(a.shape, a.dtype) for a in dsts] + news,
        in_specs=list(in_specs) + [ANY] * (len(srcs) + len(dsts)),
        out_specs=ospecs + [ANY] * (len(dsts) + len(news)),
        grid=grid, scratch_shapes=list(scratch_shapes) + sems,
        input_output_aliases={**aliases, **{n_in + len(srcs) + i: n_out + i for i in range(len(dsts))}},
        compiler_params=_cp(("arbitrary",) * len(grid) if grid else None))

    def run_call(*args):
        res = call(*args, *srcs, *dsts)
        b, c = n_out, n_out + len(dsts)
        for j in jobs:
            j.done(res[b:b + len(j.dsts)], res[c:c + len(j.news)])
            b, c = b + len(j.dsts), c + len(j.news)
        return res[0] if single else tuple(res[:n_out])

    return run_call


def _matmul(a, b, *, ta=False, tb=False, out_dtype=F32, add=None, tm, tn, tk, name, jobs=()):
    k, m = a.shape if ta else a.shape[::-1]
    n = b.shape[0] if tb else b.shape[1]
    assert (b.shape[1] if tb else b.shape[0]) == k and not (ta and tb)
    assert m % tm == 0 and n % tn == 0 and k % tk == 0, (name, a.shape, b.shape)
    nk = k // tk
    has_add = add is not None

    def body(*refs):
        a_ref, b_ref = refs[0], refs[1]
        add_ref = refs[2] if has_add else None
        o_ref = refs[3] if has_add else refs[2]
        av = a_ref[...].astype(BF16)
        bv = b_ref[...].astype(BF16)
        part = _dot_tn(av, bv) if ta else _dot_nt(av, bv) if tb else _dot(av, bv)

        def finish(r):
            if has_add:
                r = r + add_ref[...]
            o_ref[...] = r.astype(out_dtype)

        if nk == 1:
            finish(part)
        elif out_dtype == F32:
            kk = pl.program_id(2)
            pl.when(kk == 0)(lambda: finish(part))

            @pl.when(kk > 0)
            def _():
                o_ref[...] += part
        else:
            acc_ref = refs[-1]
            kk = pl.program_id(2)

            @pl.when(kk == 0)
            def _():
                acc_ref[...] = part

            @pl.when(kk > 0)
            def _():
                acc_ref[...] += part

            @pl.when(kk == nk - 1)
            def _():
                finish(acc_ref[...])

    in_specs = [pl.BlockSpec((tk, tm), lambda i, j, kk: (kk, i)) if ta else pl.BlockSpec((tm, tk), lambda i, j, kk: (i, kk)),
                pl.BlockSpec((tn, tk), lambda i, j, kk: (j, kk)) if tb
                else pl.BlockSpec((tk, tn), lambda i, j, kk: (kk, j))]
    args = [a, b]
    if has_add:
        in_specs.append(pl.BlockSpec((tm, tn), lambda i, j, kk: (i, j)))
        args.append(add)
    return _call(
        body, jobs=jobs, name=name,
        out_shape=jax.ShapeDtypeStruct((m, n), out_dtype),
        grid=(m // tm, n // tn, nk),
        in_specs=in_specs,
        out_specs=pl.BlockSpec((tm, tn), lambda i, j, kk: (i, j)),
        scratch_shapes=[pltpu.VMEM((tm, tn), F32)] if nk > 1 and out_dtype != F32 else [],
        compiler_params=_cp(("parallel", "parallel", "arbitrary")),
    )(*args)


ROWS = 256


def _rmsnorm_fwd(x, g, *, name):
    t, d = x.shape

    def body(x_ref, g_ref, o_ref):
        xv = x_ref[...]
        r = lax.rsqrt(jnp.mean(xv * xv, axis=-1, keepdims=True) + EPS)
        o_ref[...] = (xv * r * g_ref[...]).astype(BF16)

    return pl.pallas_call(
        body, name=name, out_shape=jax.ShapeDtypeStruct((t, d), BF16), grid=(t // ROWS,),
        in_specs=[pl.BlockSpec((ROWS, d), lambda i: (i, 0)), pl.BlockSpec((1, d), lambda i: (0, 0))],
        out_specs=pl.BlockSpec((ROWS, d), lambda i: (i, 0)), compiler_params=_cp(("parallel",)),
    )(x, g)


def _rmsnorm_bwd(x, g, dy, dres, *, name, jobs=()):
    t, d = x.shape

    def body(x_ref, g_ref, dy_ref, dres_ref, dx_ref, dxb_ref, dg_ref):
        xv = x_ref[...]
        r = lax.rsqrt(jnp.mean(xv * xv, axis=-1, keepdims=True) + EPS)
        xh = xv * r
        dyv = dy_ref[...]
        dxh = dyv * g_ref[...]
        dx = r * (dxh - xh * jnp.mean(dxh * xh, axis=-1, keepdims=True))
        tot = dres_ref[...] + dx
        dx_ref[...] = tot
        dxb_ref[...] = tot.astype(BF16)

        @pl.when(pl.program_id(0) == 0)
        def _():
            dg_ref[...] = jnp.zeros_like(dg_ref)

        dg_ref[...] += jnp.broadcast_to(jnp.sum(dyv * xh, axis=0, keepdims=True), dg_ref.shape)

    row = pl.BlockSpec((ROWS, d), lambda i: (i, 0))
    return _call(
        body, jobs=jobs, name=name,
        out_shape=(jax.ShapeDtypeStruct((t, d), F32), jax.ShapeDtypeStruct((t, d), BF16),
                   jax.ShapeDtypeStruct((8, d), F32)),
        grid=(t // ROWS,),
        in_specs=[row, pl.BlockSpec((1, d), lambda i: (0, 0)), row, row],
        out_specs=(row, row, pl.BlockSpec((8, d), lambda i: (0, 0))),
        compiler_params=_cp(("arbitrary",)),
    )(x, g, dy, dres)


def _final(h2, pgl, pp, target, g_final, *, name):
    t, d = h2.shape

    def body(h2_ref, pgl_ref, pp_ref, tg_ref, g_ref, dh3_ref, dpgl_ref, dpp_ref, loss_ref, dg_ref):
        s = _sigmoid(pgl_ref[...])
        ppv = pp_ref[...]
        h3 = h2_ref[...] + s * ppv
        r = lax.rsqrt(jnp.mean(h3 * h3, axis=-1, keepdims=True) + EPS)
        xh = h3 * r
        gv = g_ref[...]
        err = xh * gv - tg_ref[...]
        dyv = err * (1.0 / d)
        dxh = dyv * gv
        dh3 = r * (dxh - xh * jnp.mean(dxh * xh, axis=-1, keepdims=True))
        dh3_ref[...] = dh3
        dpp_ref[...] = (dh3 * s).astype(BF16)
        dpgl_ref[...] = (dh3 * ppv * s * (1.0 - s)).astype(BF16)

        @pl.when(pl.program_id(0) == 0)
        def _():
            loss_ref[...] = jnp.zeros_like(loss_ref)
            dg_ref[...] = jnp.zeros_like(dg_ref)

        part = 0.5 * jnp.sum(jnp.mean(err * err, axis=-1, keepdims=True), axis=0, keepdims=True)
        loss_ref[...] += jnp.broadcast_to(part, loss_ref.shape)
        dg_ref[...] += jnp.broadcast_to(jnp.sum(dyv * xh, axis=0, keepdims=True), dg_ref.shape)

    row = pl.BlockSpec((ROWS, d), lambda i: (i, 0))
    return pl.pallas_call(
        body, name=name,
        out_shape=(jax.ShapeDtypeStruct((t, d), F32), jax.ShapeDtypeStruct((t, d), BF16),
                   jax.ShapeDtypeStruct((t, d), BF16), jax.ShapeDtypeStruct((8, 128), F32),
                   jax.ShapeDtypeStruct((8, d), F32)),
        grid=(t // ROWS,),
        in_specs=[row, row, row, row, pl.BlockSpec((1, d), lambda i: (0, 0))],
        out_specs=(row, row, row, pl.BlockSpec((8, 128), lambda i: (0, 0)), pl.BlockSpec((8, d), lambda i: (0, 0))),
        compiler_params=_cp(("arbitrary",)),
    )(h2, pgl, pp, target, g_final)


def _merge_fwd(proj, out_a, out_s, *, name):
    t = proj.shape[0]

    def body(ga_ref, gs_ref, a_ref, s_ref, o_ref):
        o_ref[...] = (_sigmoid(ga_ref[...]) * a_ref[...] + _sigmoid(gs_ref[...]) * s_ref[...]).astype(BF16)

    row = pl.BlockSpec((ROWS, D), lambda i: (i, 0))
    return pl.pallas_call(
        body, name=name, out_shape=jax.ShapeDtypeStruct((t, D), BF16), grid=(t // ROWS,),
        in_specs=[pl.BlockSpec((ROWS, D), lambda i: (i, O_GA // D)), pl.BlockSpec((ROWS, D), lambda i: (i, O_GS // D)),
                  row, row],
        out_specs=row, compiler_params=_cp(("parallel",)),
    )(proj, proj, out_a, out_s)


def _merge_bwd(proj, out_a, out_s, dmerged, *, name):
    t = proj.shape[0]
    assert O_GA == 0 and O_GS == D

    def body(ga_ref, gs_ref, a_ref, s_ref, dm_ref, da_ref, ds_ref, dp_ref):
        sa = _sigmoid(ga_ref[...])
        ss = _sigmoid(gs_ref[...])
        dm = dm_ref[...]
        da_ref[...] = (dm * sa).astype(BF16)
        ds_ref[...] = (dm * ss).astype(BF16)
        dp_ref[:, :D] = (dm * a_ref[...] * sa * (1.0 - sa)).astype(BF16)
        dp_ref[:, D:] = (dm * s_ref[...] * ss * (1.0 - ss)).astype(BF16)

    row = pl.BlockSpec((ROWS, D), lambda i: (i, 0))
    o = jax.ShapeDtypeStruct((t, D), BF16)
    return pl.pallas_call(
        body, name=name, out_shape=(o, o, jax.ShapeDtypeStruct((t, NP), BF16)), grid=(t // ROWS,),
        in_specs=[pl.BlockSpec((ROWS, D), lambda i: (i, O_GA // D)), pl.BlockSpec((ROWS, D), lambda i: (i, O_GS // D)),
                  row, row, row],
        out_specs=(row, row, pl.BlockSpec((ROWS, 2 * D), lambda i: (i, 0))), compiler_params=_cp(("parallel",)),
    )(proj, proj, out_a, out_s, dmerged)


def _swiglu_fwd(f, w_gate, w_up, *, name, tn=512, jobs=()):
    t, d = f.shape
    n = w_gate.shape[1]

    def body(f_ref, wg_ref, wu_ref, g_ref, u_ref, a_ref):
        fv = f_ref[...]
        g = _dot(fv, wg_ref[...])
        u = _dot(fv, wu_ref[...])
        g_ref[...] = g.astype(BF16)
        u_ref[...] = u.astype(BF16)
        a_ref[...] = (g * _sigmoid(g) * u).astype(BF16)

    col = pl.BlockSpec((t, tn), lambda j: (0, j))
    wcol = pl.BlockSpec((d, tn), lambda j: (0, j))
    return _call(
        body, jobs=jobs, name=name,
        out_shape=(jax.ShapeDtypeStruct((t, n), BF16), jax.ShapeDtypeStruct((t, n), BF16),
                   jax.ShapeDtypeStruct((t, n), BF16)),
        grid=(n // tn,),
        in_specs=[pl.BlockSpec((t, d), lambda j: (0, 0)), wcol, wcol],
        out_specs=(col, col, col), compiler_params=_cp(("parallel",)),
    )(f, w_gate, w_up)


def _swiglu_bwd(dh, w_down, gate, up, *, name, tn=512, jobs=()):
    t, d = dh.shape
    n = w_down.shape[0]

    def body(dh_ref, w_ref, g_ref, u_ref, dg_ref, du_ref):
        da = _dot_nt(dh_ref[...], w_ref[...])
        g = g_ref[...].astype(F32)
        s = _sigmoid(g)
        du_ref[...] = (da * g * s).astype(BF16)
        dg_ref[...] = (da * u_ref[...].astype(F32) * s * (1.0 + g * (1.0 - s))).astype(BF16)

    col = pl.BlockSpec((t, tn), lambda j: (0, j))
    o = jax.ShapeDtypeStruct((t, n), BF16)
    return _call(
        body, jobs=jobs, name=name, out_shape=(o, o), grid=(n // tn,),
        in_specs=[pl.BlockSpec((t, d), lambda j: (0, 0)), pl.BlockSpec((tn, d), lambda j: (j, 0)), col, col],
        out_specs=(col, col), compiler_params=_cp(("parallel",)),
    )(dh, w_down, gate, up)


def _gated_norm_fwd(y_pre, proj, g_ssd, *, name):
    t = y_pre.shape[0]

    def body(y_ref, z_ref, g_ref, o_ref):
        z = z_ref[...]
        v = y_ref[...] * z * _sigmoid(z)
        r = lax.rsqrt(jnp.mean(v * v, axis=-1, keepdims=True) + SSM_EPS)
        o_ref[...] = (v * r * g_ref[...]).astype(BF16)

    row = pl.BlockSpec((ROWS, DI), lambda i: (i, 0))
    return pl.pallas_call(
        body, name=name, out_shape=jax.ShapeDtypeStruct((t, DI), BF16), grid=(t // ROWS,),
        in_specs=[row, pl.BlockSpec((ROWS, DI), lambda i: (i, O_Z // DI)), pl.BlockSpec((1, DI), lambda i: (0, 0))],
        out_specs=row, compiler_params=_cp(("parallel",)),
    )(y_pre, proj, g_ssd)


def _gated_norm_bwd(y_pre, proj, g_ssd, dyn, dproj, *, name, jobs=()):
    t = y_pre.shape[0]

    def body(y_ref, z_ref, g_ref, dyn_ref, _, dy_ref, dz_ref, dg_ref):
        z = z_ref[...]
        s = _sigmoid(z)
        sz = z * s
        yv = y_ref[...]
        v = yv * sz
        r = lax.rsqrt(jnp.mean(v * v, axis=-1, keepdims=True) + SSM_EPS)
        vh = v * r
        dn = dyn_ref[...]
        dvh = dn * g_ref[...]
        dv = r * (dvh - vh * jnp.mean(dvh * vh, axis=-1, keepdims=True))
        dy_ref[...] = dv * sz
        dz_ref[...] = (dv * yv * s * (1.0 + z * (1.0 - s))).astype(BF16)

        @pl.when(pl.program_id(0) == 0)
        def _():
            dg_ref[...] = jnp.zeros_like(dg_ref)

        dg_ref[...] += jnp.broadcast_to(jnp.sum(dn * vh, axis=0, keepdims=True), dg_ref.shape)

    row = pl.BlockSpec((ROWS, DI), lambda i: (i, 0))
    return _call(
        body, jobs=jobs, name=name,
        out_shape=(jax.ShapeDtypeStruct((t, DI), F32), jax.ShapeDtypeStruct(dproj.shape, BF16),
                   jax.ShapeDtypeStruct((8, DI), F32)),
        grid=(t // ROWS,),
        in_specs=[row, pl.BlockSpec((ROWS, DI), lambda i: (i, O_Z // DI)), pl.BlockSpec((1, DI), lambda i: (0, 0)), row, ANY],
        out_specs=(row, pl.BlockSpec((ROWS, DI), lambda i: (i, O_Z // DI)), pl.BlockSpec((8, DI), lambda i: (0, 0))),
        compiler_params=_cp(("arbitrary",)), aliases={4: 1},
    )(y_pre, proj, g_ssd, dyn, dproj)


CONV_TC = 512


def _shift_down(x, s, row):
    if s == 0:
        return x
    return jnp.where(row >= s, pltpu.roll(x, s, 0), 0.0)


def _shift_up(x, s, row, t):
    if s == 0:
        return x
    return jnp.where(row < t - s, pltpu.roll(x, t - s, 0), 0.0)


def _conv_fwd(proj, conv_w, conv_b, *, name):
    t = proj.shape[0]

    def body(x_ref, w_ref, b_ref, o_ref):
        x = x_ref[...]
        row = lax.broadcasted_iota(jnp.int32, x.shape, 0)
        pre = jnp.broadcast_to(b_ref[...], x.shape)
        for k in range(CW):
            pre = pre + w_ref[k:k + 1, :] * _shift_down(x, CW - 1 - k, row)
        o_ref[...] = pre * _sigmoid(pre)

    return pl.pallas_call(
        body, name=name, out_shape=jax.ShapeDtypeStruct((t, CONV), F32), grid=(CONV // CONV_TC,),
        in_specs=[pl.BlockSpec((t, CONV_TC), lambda j: (0, O_XBC // CONV_TC + j)),
                  pl.BlockSpec((CW, CONV_TC), lambda j: (0, j)), pl.BlockSpec((1, CONV_TC), lambda j: (0, j))],
        out_specs=pl.BlockSpec((t, CONV_TC), lambda j: (0, j)), compiler_params=_cp(("parallel",)),
    )(proj, conv_w, conv_b)


def _conv_bwd(proj, conv_w, conv_b, dxs, db, dc, dproj, *, name, jobs=()):
    t = proj.shape[0]
    nx = DI // CONV_TC
    assert NG * NS == CONV_TC

    def body(x_ref, w_ref, b_ref, dxs_ref, db_ref, dc_ref, _, dx_ref, dw_ref, dbias_ref):
        j = pl.program_id(0)
        x = x_ref[...]
        row = lax.broadcasted_iota(jnp.int32, x.shape, 0)
        xs = [_shift_down(x, CW - 1 - k, row) for k in range(CW)]
        pre = jnp.broadcast_to(b_ref[...], x.shape)
        for k in range(CW):
            pre = pre + w_ref[k:k + 1, :] * xs[k]
        s = _sigmoid(pre)
        da = jnp.where(j < nx, dxs_ref[...], jnp.where(j == nx, db_ref[...], dc_ref[...]))
        dpre = da * s * (1.0 + pre * (1.0 - s))
        dx = jnp.zeros_like(x)
        row8 = lax.broadcasted_iota(jnp.int32, dw_ref.shape, 0)
        dw = jnp.zeros(dw_ref.shape, F32)
        for k in range(CW):
            dx = dx + w_ref[k:k + 1, :] * _shift_up(dpre, CW - 1 - k, row, t)
            dw = dw + jnp.where(row8 == k, jnp.sum(dpre * xs[k], axis=0, keepdims=True), 0.0)
        dx_ref[...] = dx.astype(BF16)
        dw_ref[...] = dw
        dbias_ref[...] = jnp.broadcast_to(jnp.sum(dpre, axis=0, keepdims=True), dbias_ref.shape)

    col8 = pl.BlockSpec((8, CONV_TC), lambda j: (0, j))
    xbc = pl.BlockSpec((t, CONV_TC), lambda j: (0, O_XBC // CONV_TC + j))
    whole = pl.BlockSpec((t, CONV_TC), lambda j: (0, 0))
    return _call(
        body, jobs=jobs, name=name,
        out_shape=(jax.ShapeDtypeStruct(dproj.shape, BF16), jax.ShapeDtypeStruct((8, CONV), F32),
                   jax.ShapeDtypeStruct((8, CONV), F32)),
        grid=(CONV // CONV_TC,),
        in_specs=[xbc, pl.BlockSpec((CW, CONV_TC), lambda j: (0, j)), pl.BlockSpec((1, CONV_TC), lambda j: (0, j)),
                  pl.BlockSpec((t, CONV_TC), lambda j: (0, jnp.minimum(j, nx - 1))), whole, whole, ANY],
        out_specs=(xbc, col8, col8),
        compiler_params=_cp(("arbitrary",)), aliases={6: 0},
    )(proj, conv_w, conv_b, dxs, db, dc, dproj)


def _rope_tables(positions, t):
    half = HD // 2
    inv_freq = ROPE_THETA ** (-jnp.arange(half, dtype=F32) * 2.0 / HD)
    ang = positions.reshape(t).astype(F32)[:, None] * inv_freq
    cos, sin = jnp.cos(ang), jnp.sin(ang)
    return jnp.concatenate([cos] * 4, axis=1), jnp.concatenate([-sin, sin] * 2, axis=1)


def _lane_consts():
    lane = lax.broadcasted_iota(jnp.int32, (L, 128), 1)
    return lane, (lane % HD) < (HD // 2), lane < HD


def _rope(tv, cos, sin, lo):
    return tv * cos + jnp.where(lo, pltpu.roll(tv, 128 - HD // 2, 1), pltpu.roll(tv, HD // 2, 1)) * sin


def _rope_t(dv, cos, sin, lo):
    ds = dv * sin
    return dv * cos + jnp.where(lo, pltpu.roll(ds, 128 - HD // 2, 1), pltpu.roll(ds, HD // 2, 1))


def _placed(chunk, g, half0):
    own = jnp.where(half0 if g % 2 == 0 else jnp.logical_not(half0), chunk, 0.0)
    other = pltpu.roll(own, HD, 1)
    return (own, other) if g % 2 == 0 else (other, own)


def _unplace(acc, hf, g, half0):
    v = jnp.where(half0 if hf == 0 else jnp.logical_not(half0), acc, 0.0)
    return v if hf == g % 2 else pltpu.roll(v, HD, 1)


def _attn_fwd(proj, cos, sin, sinks, *, name, jobs=()):
    t = proj.shape[0]
    nb = t // L
    scale = HD ** -0.5

    def body(sink_ref, q_ref, kc_ref, kp_ref, vc_ref, vp_ref, cc_ref, sc_ref, cp_ref, sp_ref, o_ref, lse_ref):
        i = pl.program_id(0)
        lane, lo, half0 = _lane_consts()
        cos_c, sin_c, cos_p, sin_p = cc_ref[...], sc_ref[...], cp_ref[...], sp_ref[...]
        row = lax.broadcasted_iota(jnp.int32, (L, 2 * L), 0)
        col = lax.broadcasted_iota(jnp.int32, (L, 2 * L), 1)
        valid = jnp.logical_or(jnp.logical_and(jnp.logical_and(col < L, col > row), i > 0),
                               jnp.logical_and(col >= L, col - L <= row))
        kc = [_rope(kc_ref[:, 128 * m:128 * (m + 1)], cos_c, sin_c, lo) for m in range(2)]
        kp = [_rope(kp_ref[:, 128 * m:128 * (m + 1)], cos_p, sin_p, lo) for m in range(2)]
        lse_acc = jnp.zeros((L, 128), F32)
        outs = [jnp.zeros((L, 128), F32) for _ in range(QD // 128)]
        qs = [(_rope(q_ref[:, 128 * ch:128 * (ch + 1)], cos_c, sin_c, lo) * scale).astype(BF16) for ch in range(QD // 128)]
        both = lambda prev, cur, g: [jnp.concatenate([a, b], axis=0).astype(BF16)
                                     for a, b in zip(_placed(prev, g, half0), _placed(cur, g, half0))]
        for g in range(NKV):
            sl = slice(128 * (g // 2), 128 * (g // 2 + 1))
            kv = both(kp[g // 2], kc[g // 2], g)
            vv = both(vp_ref[:, sl], vc_ref[:, sl], g)
            for r in range(NQH // NKV):
                h = g * (NQH // NKV) + r
                ch, hf = h // 2, h % 2
                s = jnp.where(valid, _dot_nt(qs[ch], kv[hf]), NEG)
                sink = sink_ref[0, h]
                mx = jnp.maximum(jnp.max(s, axis=-1, keepdims=True), sink)
                e = jnp.exp(s - mx)
                den = jnp.sum(e, axis=-1, keepdims=True) + jnp.exp(sink - mx)
                outs[ch] = outs[ch] + _dot((e * (1.0 / den)).astype(BF16), vv[hf])
                lse_acc = jnp.where(lane == h, mx + jnp.log(den), lse_acc)
        for ch in range(QD // 128):
            o_ref[:, 128 * ch:128 * (ch + 1)] = outs[ch].astype(BF16)
        lse_ref[...] = lse_acc

    prev = lambda i: jnp.maximum(i - 1, 0)
    tab_c = pl.BlockSpec((L, 128), lambda i: (i, 0))
    tab_p = pl.BlockSpec((L, 128), lambda i: (prev(i), 0))
    return _call(
        body, jobs=jobs, name=name,
        out_shape=(jax.ShapeDtypeStruct((t, QD), BF16), jax.ShapeDtypeStruct((t, 128), F32)),
        grid=(nb,),
        in_specs=[pl.BlockSpec(memory_space=pltpu.SMEM),
                  pl.BlockSpec((L, QD), lambda i: (i, O_Q // QD)),
                  pl.BlockSpec((L, KVD), lambda i: (i, O_K // KVD)), pl.BlockSpec((L, KVD), lambda i: (prev(i), O_K // KVD)),
                  pl.BlockSpec((L, KVD), lambda i: (i, O_V // KVD)), pl.BlockSpec((L, KVD), lambda i: (prev(i), O_V // KVD)),
                  tab_c, tab_c, tab_p, tab_p],
        out_specs=(pl.BlockSpec((L, QD), lambda i: (i, 0)), pl.BlockSpec((L, 128), lambda i: (i, 0))),
        compiler_params=_cp(("parallel",)),
    )(sinks, proj, proj, proj, proj, proj, cos, sin, cos, sin)


def _attn_bwd(proj, cos, sin, sinks, attn, lse, dattn, dproj, *, name, jobs=()):
    t = proj.shape[0]
    nb = t // L
    scale = HD ** -0.5

    def body(sink_ref, qi_ref, qn_ref, kc_ref, kp_ref, vc_ref, vp_ref, doi_ref, don_ref, oi_ref, on_ref,
             lsei_ref, lsen_ref, cc_ref, sc_ref, cp_ref, sp_ref, cn_ref, sn_ref, _, dqkv_ref, dsk_ref):
        i = pl.program_id(0)
        lane, lo, half0 = _lane_consts()
        half1 = jnp.logical_not(half0)
        cos_c, sin_c = cc_ref[...], sc_ref[...]
        row = lax.broadcasted_iota(jnp.int32, (L, 2 * L), 0)
        col = lax.broadcasted_iota(jnp.int32, (L, 2 * L), 1)
        valid = jnp.logical_or(jnp.logical_and(jnp.logical_and(col < L, col > row), i > 0),
                               jnp.logical_and(col >= L, col - L <= row))
        m_next = jnp.logical_and(col[:, :L] > row[:, :L], i < nb - 1)
        kc = [_rope(kc_ref[:, 128 * m:128 * (m + 1)], cos_c, sin_c, lo) for m in range(2)]
        kp = [_rope(kp_ref[:, 128 * m:128 * (m + 1)], cp_ref[...], sp_ref[...], lo) for m in range(2)]
        lse_i, lse_n = lsei_ref[...], lsen_ref[...]
        dk_acc = [jnp.zeros((L, 128), F32) for _ in range(2)]
        dv_acc = [jnp.zeros((L, 128), F32) for _ in range(2)]
        dsk_acc = jnp.zeros((1, 128), F32)
        lane1 = lax.broadcasted_iota(jnp.int32, (1, 128), 1)
        both = lambda prev, cur, g: [jnp.concatenate([a, b], axis=0).astype(BF16)
                                     for a, b in zip(_placed(prev, g, half0), _placed(cur, g, half0))]
        kvs = [both(kp[g // 2], kc[g // 2], g) for g in range(NKV)]
        vvs = [both(vp_ref[:, 128 * (g // 2):128 * (g // 2 + 1)], vc_ref[:, 128 * (g // 2):128 * (g // 2 + 1)], g)
               for g in range(NKV)]
        for ch in range(QD // 128):
            sl = slice(128 * ch, 128 * (ch + 1))
            q_i = (_rope(qi_ref[:, sl], cos_c, sin_c, lo) * scale).astype(BF16)
            q_n = (_rope(qn_ref[:, sl], cn_ref[...], sn_ref[...], lo) * scale).astype(BF16)
            q_in = jnp.concatenate([q_i, q_n], axis=0)
            do_i, do_n = doi_ref[:, sl], don_ref[:, sl]
            do_ib, do_nb = do_i.astype(BF16), do_n.astype(BF16)
            do_in = jnp.concatenate([do_ib, do_nb], axis=0)
            od_i = do_i * oi_ref[:, sl].astype(F32)
            od_n = do_n * on_ref[:, sl].astype(F32)
            dq_ch = jnp.zeros((L, 128), F32)
            for hf in range(2):
                h = 2 * ch + hf
                g = h // (NQH // NKV)
                hm = half0 if hf == 0 else half1
                kv, vv = kvs[g][hf], vvs[g][hf]
                kcv, vcv = kv[L:], vv[L:]
                dl_i = jnp.sum(jnp.where(hm, od_i, 0.0), axis=-1, keepdims=True)
                dl_n = jnp.sum(jnp.where(hm, od_n, 0.0), axis=-1, keepdims=True)
                ls_i = jnp.sum(jnp.where(lane == h, lse_i, 0.0), axis=-1, keepdims=True)
                ls_n = jnp.sum(jnp.where(lane == h, lse_n, 0.0), axis=-1, keepdims=True)
                p = jnp.where(valid, jnp.exp(_dot_nt(q_i, kv) - ls_i), 0.0)
                ds = (p * (_dot_nt(do_ib, vv) - dl_i)).astype(BF16)
                dq_ch = dq_ch + jnp.where(hm, _dot(ds, kv) * scale, 0.0)
                sink = sink_ref[0, h]
                dsk = -jnp.sum(jnp.exp(sink - ls_i) * dl_i, axis=0, keepdims=True)
                dsk_acc = dsk_acc + jnp.where(lane1 == h, dsk, 0.0)
                p_n = jnp.where(m_next, jnp.exp(_dot_nt(q_n, kcv) - ls_n), 0.0)
                ds_n = (p_n * (_dot_nt(do_nb, vcv) - dl_n)).astype(BF16)
                dv_h = _dot_tn(jnp.concatenate([p[:, L:].astype(BF16), p_n.astype(BF16)], axis=0), do_in)
                dk_h = _dot_tn(jnp.concatenate([ds[:, L:], ds_n], axis=0), q_in)
                dv_acc[g // 2] = dv_acc[g // 2] + _unplace(dv_h, hf, g, half0)
                dk_acc[g // 2] = dk_acc[g // 2] + _unplace(dk_h, hf, g, half0)
            dqkv_ref[:, sl] = _rope_t(dq_ch, cos_c, sin_c, lo).astype(BF16)
        for m in range(2):
            dqkv_ref[:, QD + 128 * m:QD + 128 * (m + 1)] = _rope_t(dk_acc[m], cos_c, sin_c, lo).astype(BF16)
            dqkv_ref[:, QD + KVD + 128 * m:QD + KVD + 128 * (m + 1)] = dv_acc[m].astype(BF16)

        @pl.when(i == 0)
        def _():
            dsk_ref[...] = jnp.zeros_like(dsk_ref)

        dsk_ref[...] += jnp.broadcast_to(dsk_acc, dsk_ref.shape)

    prev = lambda i: jnp.maximum(i - 1, 0)
    nxt = lambda i: jnp.minimum(i + 1, nb - 1)
    cur_q = pl.BlockSpec((L, QD), lambda i: (i, 0))
    nxt_q = pl.BlockSpec((L, QD), lambda i: (nxt(i), 0))
    tab = lambda f: pl.BlockSpec((L, 128), lambda i: (f(i), 0))
    ident = lambda i: i
    qkv = QD + 2 * KVD
    assert O_K == O_Q + QD and O_V == O_K + KVD and O_Q % qkv == 0
    return _call(
        body, jobs=jobs, name=name,
        out_shape=(jax.ShapeDtypeStruct(dproj.shape, BF16), jax.ShapeDtypeStruct((8, 128), F32)),
        grid=(nb,),
        in_specs=[pl.BlockSpec(memory_space=pltpu.SMEM),
                  pl.BlockSpec((L, QD), lambda i: (i, O_Q // QD)), pl.BlockSpec((L, QD), lambda i: (nxt(i), O_Q // QD)),
                  pl.BlockSpec((L, KVD), lambda i: (i, O_K // KVD)), pl.BlockSpec((L, KVD), lambda i: (prev(i), O_K // KVD)),
                  pl.BlockSpec((L, KVD), lambda i: (i, O_V // KVD)), pl.BlockSpec((L, KVD), lambda i: (prev(i), O_V // KVD)),
                  cur_q, nxt_q, cur_q, nxt_q, tab(ident), tab(nxt),
                  tab(ident), tab(ident), tab(prev), tab(prev), tab(nxt), tab(nxt), ANY],
        out_specs=(pl.BlockSpec((L, qkv), lambda i: (i, O_Q // qkv)), pl.BlockSpec((8, 128), lambda i: (0, 0))),
        compiler_params=_cp(("arbitrary",)), aliases={19: 0},
    )(sinks, proj, proj, proj, proj, proj, proj, dattn, dattn, attn, attn, lse, lse, cos, sin, cos, sin, cos, sin, dproj)


PAIRS = NH // NG // 2


def _softplus(x):
    return jnp.maximum(x, 0.0) + jnp.log(1.0 + jnp.exp(-jnp.abs(x)))


def _ssd_chunk(g, xps, dtr, bm, cm, sps, dtb, alog, dsk):
    lane = lax.broadcasted_iota(jnp.int32, (L, 128), 1)
    lane1 = lax.broadcasted_iota(jnp.int32, (1, 128), 1)
    row = lax.broadcasted_iota(jnp.int32, (L, L), 0)
    col = lax.broadcasted_iota(jnp.int32, (L, L), 1)
    rowc = lax.broadcasted_iota(jnp.int32, (128, 1), 0)
    tril = col <= row
    dt = _softplus(dtr + dtb)
    a = dt * (-jnp.exp(alog))
    a_cs = lax.dot_general(tril.astype(F32), a, (((1,), (0,)), ((), ())), precision=lax.Precision.HIGHEST,
                           preferred_element_type=F32)
    a_cst = a_cs.T
    a_last = jnp.sum(jnp.where(row == L - 1, a_cs, 0.0), axis=0, keepdims=True)
    cb = _bdot_nt(cm, bm)
    ys, snew = [], []
    for q in range(PAIRS):
        xp, sp = xps[q], sps[q]
        skip = jnp.zeros((L, 128), F32)
        keep = jnp.zeros((128, 1), F32)
        ms, xds, cds, sms, bds = [], [], [], [], []
        for hh in range(2):
            h = g * 2 * PAIRS + 2 * q + hh
            hm = (lane < HD) if hh == 0 else (lane >= HD)
            rm = (rowc < HD) if hh == 0 else (rowc >= HD)
            dt_h = jnp.sum(jnp.where(lane == h, dt, 0.0), axis=1, keepdims=True)
            acs_h = jnp.sum(jnp.where(lane == h, a_cs, 0.0), axis=1, keepdims=True)
            acst_h = jnp.sum(jnp.where(row == h, a_cst, 0.0), axis=0, keepdims=True)
            al_h = jnp.sum(jnp.where(lane1 == h, a_last, 0.0), axis=1, keepdims=True)
            dsk_h = jnp.sum(jnp.where(lane1 == h, dsk, 0.0), axis=1, keepdims=True)
            decay = jnp.where(tril, jnp.exp(jnp.where(tril, acs_h - acst_h, 0.0)), 0.0)
            xh = jnp.where(hm, xp, 0.0)
            ms.append(cb * decay)
            xds.append(xh * dt_h)
            cds.append(cm * jnp.exp(acs_h))
            sms.append(jnp.where(rm, sp, 0.0))
            bds.append(bm * jnp.exp(al_h - acs_h))
            skip = skip + dsk_h * xh
            keep = keep + jnp.where(rm, jnp.exp(al_h), 0.0)
        xd2 = jnp.concatenate(xds, axis=0)
        y_pair = (_bdot(jnp.concatenate(ms, axis=1), xd2)
                  + _bdot_nt(jnp.concatenate(cds, axis=1), jnp.concatenate(sms, axis=1)) + skip)
        ys.append(y_pair)
        snew.append(sp * keep + _bdot_tn(xd2, jnp.concatenate(bds, axis=0)))
    return ys, snew


def _ssd_specs(t):
    nc = t // L
    xs = lambda f: pl.BlockSpec((L, 128 * PAIRS), lambda c, g: (f(c), g))
    bspec = lambda f: pl.BlockSpec((L, NS), lambda c, g: (f(c), DI // NS + g))
    cspec = lambda f: pl.BlockSpec((L, NS), lambda c, g: (f(c), DI // NS + NG + g))
    dts = lambda f: pl.BlockSpec((L, 128), lambda c, g: (f(c), O_DT // 128))
    par = pl.BlockSpec((1, 128), lambda c, g: (0, 0))
    st = lambda f: pl.BlockSpec((1, 1, PAIRS, 128, NS), lambda c, g: (f(c), g, 0, 0, 0))
    return nc, xs, bspec, cspec, dts, par, st


def _ssd_fwd(xbc_act, proj, dtb, alog, dsk, *, name, jobs=()):
    t = proj.shape[0]
    nc, xs, bspec, cspec, dts, par, st = _ssd_specs(t)
    ident = lambda c: c

    def body(x_ref, b_ref, c_ref, dt_ref, dtb_ref, al_ref, dsk_ref, y_ref, sin_ref, s_ref):
        c, g = pl.program_id(0), pl.program_id(1)

        @pl.when(c == 0)
        def _():
            s_ref[g] = jnp.zeros((PAIRS, 128, NS), F32)

        sps = [s_ref[g, q] for q in range(PAIRS)]
        for q in range(PAIRS):
            sin_ref[0, 0, q] = sps[q]
        xps = [x_ref[:, 128 * q:128 * (q + 1)] for q in range(PAIRS)]
        ys, snew = _ssd_chunk(g, xps, dt_ref[...], b_ref[...], c_ref[...], sps, dtb_ref[...], al_ref[...], dsk_ref[...])
        for q in range(PAIRS):
            y_ref[:, 128 * q:128 * (q + 1)] = ys[q]
            s_ref[g, q] = snew[q]

    return _call(
        body, jobs=jobs, name=name,
        out_shape=(jax.ShapeDtypeStruct((t, DI), F32), jax.ShapeDtypeStruct((nc, NG, PAIRS, 128, NS), F32)),
        grid=(nc, NG),
        in_specs=[xs(ident), bspec(ident), cspec(ident), dts(ident), par, par, par],
        out_specs=(pl.BlockSpec((L, 128 * PAIRS), lambda c, g: (c, g)), st(ident)),
        scratch_shapes=[pltpu.VMEM((NG, PAIRS, 128, NS), F32)],
        compiler_params=_cp(("arbitrary", "arbitrary")),
    )(xbc_act, xbc_act, xbc_act, proj, dtb, alog, dsk)


def _ssd_bwd(xbc_act, proj, dtb, alog, dsk, states, dy, dproj, *, name, jobs=()):
    t = proj.shape[0]
    nc, xs, bspec, cspec, dts, par, st = _ssd_specs(t)
    rev = lambda c: nc - 1 - c

    def body(x_ref, b_ref, c_ref, dt_ref, dtb_ref, al_ref, dsk_ref, sin_ref, dy_ref, _,
             dx_ref, db_ref, dc_ref, ddtp_ref, ddtb_ref, dal_ref, ddsk_ref, ds_ref, ddt_ref):
        c, g = pl.program_id(0), pl.program_id(1)

        @pl.when(c == 0)
        def _():
            ds_ref[g] = jnp.zeros((PAIRS, 128, NS), F32)

        @pl.when(jnp.logical_and(c == 0, g == 0))
        def _():
            ddtb_ref[...] = jnp.zeros_like(ddtb_ref)
            dal_ref[...] = jnp.zeros_like(dal_ref)
            ddsk_ref[...] = jnp.zeros_like(ddsk_ref)

        @pl.when(g == 0)
        def _():
            ddt_ref[...] = jnp.zeros_like(ddt_ref)

        sps = [sin_ref[0, 0, q] for q in range(PAIRS)]
        xps = [x_ref[:, 128 * q:128 * (q + 1)] for q in range(PAIRS)]
        _, vjp = jax.vjp(functools.partial(_ssd_chunk, g), xps, dt_ref[...], b_ref[...], c_ref[...], sps,
                         dtb_ref[...], al_ref[...], dsk_ref[...])
        dys = [dy_ref[:, 128 * q:128 * (q + 1)] for q in range(PAIRS)]
        dss = [ds_ref[g, q] for q in range(PAIRS)]
        dxps, ddt, db, dc, dsps, ddtb, dal, ddsk = vjp((dys, dss))
        for q in range(PAIRS):
            dx_ref[:, 128 * q:128 * (q + 1)] = dxps[q]
            ds_ref[g, q] = dsps[q]
        db_ref[...] = db
        dc_ref[...] = dc
        ddt_ref[...] += ddt
        ddtb_ref[...] += jnp.broadcast_to(ddtb, ddtb_ref.shape)
        dal_ref[...] += jnp.broadcast_to(dal, dal_ref.shape)
        ddsk_ref[...] += jnp.broadcast_to(ddsk, ddsk_ref.shape)

        @pl.when(g == NG - 1)
        def _():
            ddtp_ref[:, :128] = ddt_ref[...].astype(BF16)
            ddtp_ref[:, 128:] = jnp.zeros((L, DT_PAD - 128), BF16)

    acc = pl.BlockSpec((8, 128), lambda c, g: (0, 0))
    o8 = jax.ShapeDtypeStruct((8, 128), F32)
    return _call(
        body, jobs=jobs, name=name,
        out_shape=(jax.ShapeDtypeStruct((t, DI), F32), jax.ShapeDtypeStruct((t, NG * NS), F32),
                   jax.ShapeDtypeStruct((t, NG * NS), F32), jax.ShapeDtypeStruct(dproj.shape, BF16), o8, o8, o8),
        grid=(nc, NG),
        in_specs=[xs(rev), bspec(rev), cspec(rev), dts(rev), par, par, par, st(rev),
                  pl.BlockSpec((L, 128 * PAIRS), lambda c, g: (rev(c), g)), ANY],
        out_specs=(pl.BlockSpec((L, 128 * PAIRS), lambda c, g: (rev(c), g)),
                   pl.BlockSpec((L, NS), lambda c, g: (rev(c), g)), pl.BlockSpec((L, NS), lambda c, g: (rev(c), g)),
                   pl.BlockSpec((L, DT_PAD), lambda c, g: (rev(c), O_DT // DT_PAD)), acc, acc, acc),
        scratch_shapes=[pltpu.VMEM((NG, PAIRS, 128, NS), F32), pltpu.VMEM((L, 128), F32)],
        compiler_params=_cp(("arbitrary", "arbitrary")), aliases={9: 3},
    )(xbc_act, xbc_act, xbc_act, proj, dtb, alog, dsk, states, dy, dproj)


def _pad_lanes(v, n=128):
    return jnp.pad(v, ((0, 0), (0, n - v.shape[1])))


class _LocalPlan:
    core = 0

    def __init__(self, big):
        self.big, self.grad, self.halves = big, {}, {}

    def w(self, n):
        return self.big[n]

    def g(self, n, a):
        self.grad[n] = a

    def g_half(self, n, which, a):
        self.halves[which] = a
        if len(self.halves) == 2:
            self.grad[n] = jnp.concatenate([self.halves["keep"], self.halves["send"]], axis=0)

    def jobs(self, tag):
        return ()


def _local_step(x, p, positions, target, small, plan):
    t = x.shape[0]
    cos, sin = _rope_tables(positions, t)
    dtb, alog, dsk = _pad_lanes(small["dt_bias"]), _pad_lanes(small["a_log"]), _pad_lanes(small["d_skip"])
    w, jobs = plan.w, plan.jobs

    def mm(a, b, *, name, tm=t, tn=512, **kw):
        return _matmul(a, b, tm=tm, tn=tn, name=name, jobs=jobs(name), **kw)

    tkl = FFN // 4

    def dw(wname, a, dy, *, name, tm):
        plan.g(wname, _matmul(a, dy, ta=True, out_dtype=BF16, tm=tm, tn=512, tk=t, name=name, jobs=jobs(name)))

    u = _rmsnorm_fwd(x, small["g_mix"], name="norm_mix")
    proj = mm(u, w("w_in"), tn=1024, tk=D, name="mm_in")
    attn, lse = _attn_fwd(proj, cos, sin, small["sinks"], name="attn_fwd", jobs=jobs("attn_fwd"))
    out_a = mm(attn, w("w_attn_br"), tk=QD, name="mm_attn_br")
    xbc_act = _conv_fwd(proj, small["conv_w"], small["conv_b"], name="conv_fwd")
    y_pre, states = _ssd_fwd(xbc_act, proj, dtb, alog, dsk, name="ssd_fwd", jobs=jobs("ssd_fwd"))
    yn = _gated_norm_fwd(y_pre, proj, small["g_ssd"], name="gated_norm_fwd")
    out_s = mm(yn, w("w_ssd_br"), tk=DI, name="mm_ssd_br")
    merged = _merge_fwd(proj, out_a, out_s, name="merge_fwd")
    h1 = mm(merged, w("w_o"), add=x, tk=D, name="mm_o")
    f = _rmsnorm_fwd(h1, small["g_ffn"], name="norm_ffn")
    gate, up, act = _swiglu_fwd(f, w("w_gate"), w("w_up"), name="swiglu_fwd", jobs=jobs("swiglu_fwd"))
    h2 = mm(act, w("w_down"), add=h1, tm=t // 2, tk=FFN // 2, name="mm_down")
    e = _rmsnorm_fwd(h2, small["g_ple"], name="norm_ple")
    pgl = mm(e, w("w_ple_gate"), tk=D, name="mm_ple_gate")
    pb = p.astype(BF16)
    pp = mm(pb, w("w_ple_proj"), tk=PLE, name="mm_ple_proj")
    dh3, dpgl, dpp, loss, dg_final = _final(h2, pgl, pp, target, small["g_final"].reshape(1, D), name="final")

    dw("w_ple_proj", pb, dpp, tm=PLE, name="mm_d_ple_proj")
    dw("w_ple_gate", e, dpgl, tm=D, name="mm_d_ple_gate")
    de = mm(dpgl, w("w_ple_gate"), tb=True, tk=D, name="mm_de")
    dh2, dh2b, dg_ple = _rmsnorm_bwd(h2, small["g_ple"], de, dh3, name="norm_ple_bwd", jobs=jobs("norm_ple_bwd"))
    dw("w_down", act, dh2b, tm=FFN // 2, name="mm_d_down")
    dgate, dup = _swiglu_bwd(dh2b, w("w_down"), gate, up, name="swiglu_bwd", jobs=jobs("swiglu_bwd"))
    dw("w_gate", f, dgate, tm=D, name="mm_d_gate")
    dw("w_up", f, dup, tm=D, name="mm_d_up")
    df = mm(dgate, w("w_gate"), tb=True, tn=1024, tk=tkl, name="mm_df_gate")
    df = mm(dup, w("w_up"), tb=True, add=df, tm=t // 2, tk=FFN // 2, name="mm_df_up")
    dh1, dh1b, dg_ffn = _rmsnorm_bwd(h1, small["g_ffn"], df, dh2, name="norm_ffn_bwd", jobs=jobs("norm_ffn_bwd"))
    dw("w_o", merged, dh1b, tm=D, name="mm_d_o")
    dmerged = mm(dh1b, w("w_o"), tb=True, tk=D, name="mm_dmerged")
    dout_a, dout_s, dproj = _merge_bwd(proj, out_a, out_s, dmerged, name="merge_bwd")
    dw("w_attn_br", attn, dout_a, tm=QD, name="mm_d_attn_br")
    dw("w_ssd_br", yn, dout_s, tm=DI, name="mm_d_ssd_br")
    dattn = mm(dout_a, w("w_attn_br"), tb=True, tk=D, name="mm_dattn")
    dyn = mm(dout_s, w("w_ssd_br"), tb=True, tk=D, name="mm_dyn")
    dproj, dsinks = _attn_bwd(proj, cos, sin, small["sinks"], attn, lse, dattn, dproj, name="attn_bwd",
                              jobs=jobs("attn_bwd"))
    dy_pre, dproj, dg_ssd = _gated_norm_bwd(y_pre, proj, small["g_ssd"], dyn, dproj, name="gated_norm_bwd",
                                            jobs=jobs("gated_norm_bwd"))
    dxs, db, dc, dproj, ddtb, dalog, ddsk = _ssd_bwd(xbc_act, proj, dtb, alog, dsk, states, dy_pre, dproj, name="ssd_bwd",
                                                     jobs=jobs("ssd_bwd"))
    dproj, dconv_w, dconv_b = _conv_bwd(proj, small["conv_w"], small["conv_b"], dxs, db, dc, dproj, name="conv_bwd",
                                        jobs=jobs("conv_bwd"))
    for which, h in (("send", 1 - plan.core), ("keep", plan.core)):
        uh = lax.dynamic_slice_in_dim(u, h * (D // 2), D // 2, axis=1)
        name = "mm_d_in_" + which
        plan.g_half("w_in", which, _matmul(uh, dproj, ta=True, out_dtype=BF16, tm=D // 2, tn=1024, tk=t, name=name,
                                           jobs=jobs(name)))
    du = mm(dproj, w("w_in"), tb=True, tn=1024, tk=tkl, name="mm_du")
    grad_x, _, dg_mix = _rmsnorm_bwd(x, small["g_mix"], du, dh1, name="norm_mix_bwd", jobs=jobs("norm_mix_bwd"))

    gs = {
        "g_mix": dg_mix[:1], "conv_w": dconv_w[:CW], "conv_b": dconv_b[:1], "dt_bias": ddtb[:1, :NH],
        "a_log": dalog[:1, :NH], "d_skip": ddsk[:1, :NH], "g_ssd": dg_ssd[:1], "sinks": dsinks[:1, :NQH],
        "g_ffn": dg_ffn[:1], "g_ple": dg_ple[:1], "g_final": dg_final[0],
    }
    return loss, grad_x, gs


def _to_kernel_cols(w):
    seg = lambda o, n: w[:, o:o + n]
    return jnp.concatenate([seg(R_GA, D), seg(R_GS, D), seg(R_Z, DI), seg(R_XBC, CONV), seg(R_Q, QD), seg(R_K, KVD),
                            seg(R_V, KVD), seg(R_DT, NH), jnp.zeros((w.shape[0], DT_PAD - NH), w.dtype)], axis=1)


def _from_kernel_cols(g):
    seg = lambda o, n: g[:, o:o + n]
    return jnp.concatenate([seg(O_Q, QD), seg(O_K, KVD), seg(O_V, KVD), seg(O_Z, DI), seg(O_XBC, CONV), seg(O_DT, NH),
                            seg(O_GA, D), seg(O_GS, D)], axis=1)


def _shard_pieces():
    segs = ((R_Q, QD, O_Q), (R_K, KVD, O_K), (R_V, KVD, O_V), (R_Z, DI, O_Z), (R_XBC, CONV, O_XBC), (R_DT, NH, O_DT),
            (R_GA, D, O_GA), (R_GS, D, O_GS))
    cs = IN_DIM // NCHIP
    out = []
    for j in range(NCHIP):
        for r0, n, k0 in segs:
            lo, hi = max(r0, j * cs), min(r0 + n, (j + 1) * cs)
            if lo < hi:
                out.append((j, lo - j * cs, hi - lo, k0 + lo - r0))
    return out


SLAB = IN_DIM // NCHIP
SLAB_PAD = -(-SLAB // 128) * 128
REMAP_ROWS = 256


def _lane_remap(src, dst_slabs, dst_cols, moves, *, name, add=None, jobs=()):
    s_n, rows, s_cols = src.shape
    assert s_cols % 128 == 0 and dst_cols % 128 == 0 and rows % REMAP_ROWS == 0
    half = REMAP_ROWS // 2

    def body(s_ref, *refs):
        d_ref = refs[-1]
        lane = lax.broadcasted_iota(jnp.int32, (half, 128), 1)
        tiles = {}

        def tile(j, m):
            if (j, m) not in tiles:
                tiles[j, m] = pltpu.bitcast(s_ref[j, :, 128 * m:128 * (m + 1)], jnp.uint32)
            return tiles[j, m]

        def window(j, base):
            m0, s = base // 128, base % 128
            left = tile(j, m0) if 0 <= m0 < s_cols // 128 else None
            if s == 0:
                return left
            right = tile(j, m0 + 1) if 0 <= m0 + 1 < s_cols // 128 else None
            left = None if left is None else pltpu.roll(left, 128 - s, 1)
            right = None if right is None else pltpu.roll(right, 128 - s, 1)
            if left is None or right is None:
                return right if left is None else left
            return jnp.where(lane < 128 - s, left, right)

        for ds in range(dst_slabs):
            for t in range(dst_cols // 128):
                o = 128 * t
                acc = jnp.zeros((half, 128), jnp.uint32)
                for sj, sc, n, dj, dc in moves:
                    lo, hi = max(o, dc) - o, min(o + 128, dc + n) - o
                    if dj != ds or lo >= hi:
                        continue
                    win = window(sj, o - dc + sc)
                    acc = win if (lo, hi) == (0, 128) else jnp.where(jnp.logical_and(lane >= lo, lane < hi), win, acc)
                out = pltpu.bitcast(acc, BF16)
                if add is not None:
                    out = (out.astype(F32) + refs[0][ds, :, o:o + 128].astype(F32)).astype(BF16)
                d_ref[ds, :, o:o + 128] = out

    dst_blk = pl.BlockSpec((dst_slabs, REMAP_ROWS, dst_cols), lambda i: (0, i, 0))
    return _call(
        body, jobs=jobs, name=name, out_shape=jax.ShapeDtypeStruct((dst_slabs, rows, dst_cols), BF16),
        grid=(rows // REMAP_ROWS,),
        in_specs=[pl.BlockSpec((s_n, REMAP_ROWS, s_cols), lambda i: (0, i, 0))] + ([dst_blk] if add is not None else []),
        out_specs=dst_blk, compiler_params=_cp(("parallel",)),
    )(*((src,) if add is None else (src, add)))


def _slabs_to_kernel_cols(slabs, *, name, jobs=()):
    moves = [(j, a, n, 0, k0) for j, a, n, k0 in _shard_pieces()]
    return _lane_remap(slabs, 1, NP, moves, name=name, jobs=jobs)[0]


def _kernel_cols_to_slabs(g, *, name, add=None, jobs=()):
    moves = [(0, k0, n, j, a) for j, a, n, k0 in _shard_pieces()]
    return _lane_remap(g[None], NCHIP, SLAB_PAD, moves, name=name, add=add, jobs=jobs)


RELS = ((0, 1), (1, 0), (1, 1))
MATS = {
    n: (n, kind, 1, r, c, tp, tf) for n, kind, r, c, tp, tf in (
        ("w_in", "stk", 2048, SLAB_PAD, 256, 256),
        ("w_attn_br", "col", 1024, 512, 256, 256),
        ("w_ssd_br", "row", 512, 2048, 512, 256),
        ("w_o", "row", 512, 2048, 512, 256),
        ("w_gate", "col", 2048, 1408, 256, 256),
        ("w_up", "col", 2048, 1408, 256, 256),
        ("w_down", "row", 1408, 2048, 704, 704),
        ("w_ple_gate", "row", 512, 2048, 512, 256),
        ("w_ple_proj", "col", 256, 512, 128, 128),
    )}


def _pos():
    return lax.axis_index("x"), lax.axis_index("y"), lax.axis_index("c")


def _flip(v, a):
    return 1 - v if a else v


def _remote(src, dst, send, recv, dev):
    return pltpu.make_async_remote_copy(src_ref=src, dst_ref=dst, send_sem=send, recv_sem=recv, device_id=dev,
                                        device_id_type=MESH)


def _whole_shape(kind, g, r, c):
    return {"row": (g, NCHIP * r, c), "col": (g, r, NCHIP * c), "stk": (NCHIP, r, c)}[kind]


def _cols(j, c):
    return pl.ds(pl.multiple_of(j * c, 128), c)


def _whole_shard(kind, ref, j, r, c):
    if kind == "row":
        return ref.at[:, pl.ds(j * r, r), :]
    if kind == "col":
        return ref.at[:, :, _cols(j, c)]
    return ref.at[pl.ds(j, 1)]


def _whole_rows(kind, ref, j, row, n, r, c):
    if kind == "row":
        return ref.at[:, pl.ds(j * r + row, n), :]
    if kind == "col":
        return ref.at[:, pl.ds(row, n), _cols(j, c)]
    return ref.at[pl.ds(j, 1), pl.ds(row, n), :]


class _GatherJob(_Job):
    has_mid = True
    NCP = 13

    def __init__(self, names, shards, sink):
        self.mats = [MATS[n] for n in names]
        self.srcs = [shards[n] for n in names]
        self.news = [jax.ShapeDtypeStruct(_whole_shape(kind, g, r, c), BF16) for _, kind, g, r, c, _, _ in self.mats]
        n = len(names)
        self.scratch = [pltpu.SemaphoreType.DMA((self.NCP * n,)), pltpu.SemaphoreType.DMA((self.NCP * n,))]
        self.names, self.sink = names, sink

    def _copies(self, srcs, news, sems):
        send, recv = sems
        x, y, c = _pos()
        me, jx, jy, jd = 2 * x + y, 2 * (1 - x) + y, 2 * x + (1 - y), 2 * (1 - x) + (1 - y)
        nbx, nby, sib = (1 - x, y, c), (x, 1 - y, c), (x, y, 1 - c)
        cps = []
        for w, (_, kind, g, r, cc, _, _) in enumerate(self.mats):
            hr, qr = r // 2, r // 4
            at = lambda j, h, q, n: _whole_rows(kind, news[w], j, h * hr + q * qr, n, r, cc)
            mine = lambda q: srcs[w].at[:, pl.ds(c * hr + q * qr, qr), :]
            cp = lambda k, s, d, dev: _remote(s, d, send.at[self.NCP * w + k], recv.at[self.NCP * w + k], dev)
            cps.append([
                cp(0, mine(0), at(me, c, 0, qr), nbx), cp(1, mine(1), at(me, c, 1, qr), nbx),
                cp(2, mine(1), at(me, c, 1, qr), nby), cp(3, mine(0), at(me, c, 0, qr), nby),
                cp(4, at(jx, c, 0, qr), at(jx, c, 0, qr), nby), cp(5, at(jy, c, 1, qr), at(jy, c, 1, qr), nbx),
                cp(6, at(jx, c, 0, qr), at(jx, c, 0, qr), sib), cp(7, at(jx, c, 1, qr), at(jx, c, 1, qr), sib),
                cp(8, at(jy, c, 1, qr), at(jy, c, 1, qr), sib), cp(9, at(jy, c, 0, qr), at(jy, c, 0, qr), sib),
                cp(10, at(jd, c, 0, qr), at(jd, c, 0, qr), sib), cp(11, at(jd, c, 1, qr), at(jd, c, 1, qr), sib),
                cp(12, srcs[w], _whole_shard(kind, news[w], me, r, cc), sib)])
        return cps

    def _pass_on(self, srcs, news, sems, pairs):
        cps = self._copies(srcs, news, sems)
        for w in range(len(self.mats)):
            for arrived, onward in pairs:
                cps[w][arrived].wait_recv()
                for k in onward:
                    cps[w][k].start()

    def start(self, srcs, dsts, news, sems):
        cps = self._copies(srcs, news, sems)
        for w in range(len(self.mats)):
            for k in (0, 1, 2, 3, 12):
                cps[w][k].start()

    def mid(self, srcs, dsts, news, sems):
        self._pass_on(srcs, news, sems, ((0, (4, 6)), (2, (5, 8))))

    def late(self, srcs, dsts, news, sems):
        self._pass_on(srcs, news, sems, ((1, (7,)), (3, (9,))))

    def finish(self, srcs, dsts, news, sems):
        self._pass_on(srcs, news, sems, ((4, (10,)), (5, (11,))))
        cps = self._copies(srcs, news, sems)
        for w in range(len(self.mats)):
            for k in (6, 7, 8, 9, 10, 11, 12):
                cps[w][k].wait_recv()
            for k in range(self.NCP):
                cps[w][k].wait_send()

    def done(self, dsts, news):
        for n, a in zip(self.names, news):
            self.sink[n] = a


class _SwapJob(_Job):
    def __init__(self, build, ncopies, *, srcs=(), dsts=(), news=(), done=None):
        self.build, self.srcs, self.dsts, self.news, self._done = build, list(srcs), list(dsts), list(news), done
        self.scratch = [pltpu.SemaphoreType.DMA((ncopies,)), pltpu.SemaphoreType.DMA((ncopies,))]

    def start(self, srcs, dsts, news, sems):
        for cp in self.build(srcs, dsts, news, *sems):
            cp.start()

    def finish(self, srcs, dsts, news, sems):
        for cp in self.build(srcs, dsts, news, *sems):
            cp.wait()

    def done(self, dsts, news):
        if self._done is not None:
            self._done(dsts, news)


def _half_of_whole(kind, ref, h, r, c):
    if kind == "row":
        return ref.at[:, :, pl.ds(pl.multiple_of(h * (c // 2), 128), c // 2)]
    return ref.at[:, pl.ds(h * (r // 2), r // 2), :]


def _half_shape(kind, g, r, c):
    return {"row": (g, NCHIP * r, c // 2), "col": (g, r // 2, NCHIP * c), "stk": (NCHIP, r // 2, c)}[kind]


def _sub_shape(kind, r, c):
    return {"row": (1, r // 2, c // 2), "col": (1, r // 4, c), "stk": (1, r // 4, c)}[kind]


def _sub_of_half(kind, ref, j, p, r, c):
    sr = _sub_shape(kind, r, c)[1]
    if kind == "row":
        return ref.at[:, pl.ds(j * r + p * sr, sr), :]
    if kind == "col":
        return ref.at[:, pl.ds(p * sr, sr), _cols(j, c)]
    return ref.at[pl.ds(j, 1), pl.ds(p * sr, sr), :]


def _sub_tile(sr):
    return 256 if sr % 256 == 0 else sr


def _half_of_shard(kind, ref, h, r, c):
    if kind == "row":
        return ref.at[:, :, pl.ds(pl.multiple_of(h * (c // 2), 128), c // 2)]
    return ref.at[:, pl.ds(h * (r // 2), r // 2), :]


def _pair_sum(pack, core, mine, got, whole=True):
    name, kind, g, r, c, tr, _ = pack
    hs = _half_shape(kind, g, r, c)
    nb = hs[1] // tr

    def body(core_ref, a_ref, b_ref, o_ref):
        o_ref[...] = (a_ref[...].astype(F32) + b_ref[...].astype(F32)).astype(BF16)

    blk = (1, tr, hs[2])
    same = lambda gi, i, core_ref: (gi, i, 0)
    if not whole:
        a_map = same
    elif kind == "row":
        a_map = lambda gi, i, core_ref: (gi, i, core_ref[0])
    else:
        a_map = lambda gi, i, core_ref: (gi, core_ref[0] * nb + i, 0)
    return pl.pallas_call(
        body, name="pair_sum_" + name, out_shape=jax.ShapeDtypeStruct(hs, BF16),
        grid_spec=pltpu.PrefetchScalarGridSpec(
            num_scalar_prefetch=1, grid=(hs[0], nb),
            in_specs=[pl.BlockSpec(blk, a_map), pl.BlockSpec(blk, same)], out_specs=pl.BlockSpec(blk, same)),
        compiler_params=_cp(("parallel", "parallel")),
    )(core, mine, got)


def _sub_sums(pack, idx, half, got, *, name):
    _, kind, g, r, c, _, _ = pack
    _, sr, sc = _sub_shape(kind, r, c)
    tr = _sub_tile(sr)
    nb = sr // tr

    def body(idx_ref, a_ref, ga_ref, b_ref, gb_ref, k_ref, p_ref):
        k_ref[0, 0] = a_ref[0].astype(F32) + ga_ref[0, 0].astype(F32)
        p_ref[0, 0] = (b_ref[0].astype(F32) + gb_ref[0, 0].astype(F32)).astype(BF16)

    def sub_map(o):
        if kind == "row":
            return lambda q, i, ix: (0, ix[4 * q + o] * (r // tr) + ix[4 * q + o + 1] * nb + i, 0)
        if kind == "col":
            return lambda q, i, ix: (0, ix[4 * q + o + 1] * nb + i, ix[4 * q + o])
        return lambda q, i, ix: (ix[4 * q + o], ix[4 * q + o + 1] * nb + i, 0)

    sub = lambda o: pl.BlockSpec((1, tr, sc), sub_map(o))
    got_blk = lambda o: pl.BlockSpec((1, 1, tr, sc), lambda q, i, ix: (2 * q + o, 0, i, 0))
    out_blk = pl.BlockSpec((1, 1, tr, sc), lambda q, i, ix: (q, 0, i, 0))
    return pl.pallas_call(
        body, name=name,
        out_shape=(jax.ShapeDtypeStruct((2, 1, sr, sc), F32), jax.ShapeDtypeStruct((2, 1, sr, sc), BF16)),
        grid_spec=pltpu.PrefetchScalarGridSpec(
            num_scalar_prefetch=1, grid=(2, nb), in_specs=[sub(0), got_blk(0), sub(2), got_blk(1)],
            out_specs=(out_blk, out_blk)),
        compiler_params=_cp(("parallel", "parallel")),
    )(idx, half, got, half, got)


def _shard_sum(pack, core, keep, got):
    name, kind, g, r, c, _, _ = pack
    _, sr, sc = _sub_shape(kind, r, c)
    tr = _sub_tile(sr)
    nb = sr // tr

    def body(core_ref, a_ref, b_ref, o_ref):
        o_ref[0] = a_ref[0, 0] + b_ref[0, 0].astype(F32)

    blk = pl.BlockSpec((1, 1, tr, sc), lambda p, i, cr: (p, 0, i, 0))
    if kind == "row":
        o_map = lambda p, i, cr: (0, p * nb + i, cr[0])
    else:
        o_map = lambda p, i, cr: (0, cr[0] * 2 * nb + p * nb + i, 0)
    return pl.pallas_call(
        body, name="shard_sum_" + name, out_shape=jax.ShapeDtypeStruct((g, r, c), F32),
        grid_spec=pltpu.PrefetchScalarGridSpec(
            num_scalar_prefetch=1, grid=(2, nb), in_specs=[blk, blk], out_specs=pl.BlockSpec((1, tr, sc), o_map)),
        compiler_params=_cp(("parallel", "parallel")),
    )(core, keep, got)


class _Plan:
    def __init__(self, shards, table):
        self.shards, self.table = shards, table
        self.whole, self.grad, self.got_a, self.half, self.gshard = {}, {}, {}, {}, {}
        self.got_b1, self.kept, self.pass_on, self.got_b2 = {}, {}, {}, {}
        x, y, c = _pos()
        me, jx, jy = 2 * x + y, 2 * (1 - x) + y, 2 * x + (1 - y)
        self.core = c
        self.core1 = c.reshape(1).astype(jnp.int32)
        zero = 0 * me
        self.idx_sums = jnp.stack([me, zero, jy, zero, me, zero + 1, jx, zero + 1]).astype(jnp.int32)
        self._w_in = None
        self.send, self.keep = {}, {}

    def w(self, n):
        if n != "w_in":
            return self.whole[n][0]
        if self._w_in is None:
            self._w_in = _slabs_to_kernel_cols(self.whole[n], name="relayout_w_in", jobs=self.jobs("relayout_w_in"))
        return self._w_in

    def g(self, n, a):
        self.grad[n] = a[None]

    def g_half(self, n, which, a):
        if which == "keep" and n in self.got_a:
            self.half[n] = _kernel_cols_to_slabs(a, name="relayout_d_in_keep", add=self.got_a[n])
        else:
            (self.send if which == "send" else self.keep)[n] = _kernel_cols_to_slabs(a, name="relayout_d_in_" + which)

    def jobs(self, tag):
        out = []
        for spec in self.table.get(tag, ()):
            out += getattr(self, "_" + spec[0])(*spec[1:])
        return out

    def run(self, name, jobs):
        if jobs:
            _call(lambda: None, jobs=jobs, name=name, out_shape=[], in_specs=[], out_specs=[])()

    def _gather(self, names):
        return [_GatherJob(names, self.shards, self.whole)]

    def _rs_a(self, names):
        mats = [MATS[n] for n in names]

        def build(srcs, dsts, news, send, recv):
            x, y, c = _pos()
            return [_remote(srcs[i] if names[i] in self.send else _half_of_whole(kind, srcs[i], 1 - c, r, cc), news[i],
                            send.at[i], recv.at[i], (x, y, 1 - c))
                    for i, (_, kind, g, r, cc, _, _) in enumerate(mats)]

        def done(dsts, news):
            self.got_a.update(zip(names, news))

        return [_SwapJob(build, len(names), srcs=[self.send.get(n, self.grad.get(n)) for n in names], done=done,
                         news=[jax.ShapeDtypeStruct(_half_shape(kind, g, r, c), BF16) for _, kind, g, r, c, _, _ in mats])]

    def _rs_b1(self, names):
        mats = [MATS[n] for n in names]
        for n in names:
            if n in self.half:
                continue
            if n in self.keep:
                self.half[n] = _pair_sum(MATS[n], self.core1, self.keep[n], self.got_a[n], whole=False)
            else:
                self.half[n] = _pair_sum(MATS[n], self.core1, self.grad[n], self.got_a[n])

        def build(srcs, dsts, news, send, recv):
            x, y, c = _pos()
            jx, jy, jd = 2 * (1 - x) + y, 2 * x + (1 - y), 2 * (1 - x) + (1 - y)
            nbx, nby = (1 - x, y, c), (x, 1 - y, c)
            cps = []
            for i, (_, kind, g, r, cc, _, _) in enumerate(mats):
                sub = lambda j, p: _sub_of_half(kind, srcs[i], j, p, r, cc)
                for k, (j, p, dev) in enumerate(((jx, 0, nbx), (jd, 0, nbx), (jy, 1, nby), (jd, 1, nby))):
                    cps.append(_remote(sub(j, p), news[i].at[k], send.at[4 * i + k], recv.at[4 * i + k], dev))
            return cps

        def done(dsts, news):
            self.got_b1.update(zip(names, news))

        return [_SwapJob(build, 4 * len(names), srcs=[self.half[n] for n in names], done=done,
                         news=[jax.ShapeDtypeStruct((4,) + _sub_shape(kind, r, c), BF16) for _, kind, g, r, c, _, _ in mats])]

    def _rs_b2(self, names):
        mats = [MATS[n] for n in names]
        for n in names:
            self.kept[n], self.pass_on[n] = _sub_sums(MATS[n], self.idx_sums, self.half[n], self.got_b1[n], name="sums_" + n)

        def build(srcs, dsts, news, send, recv):
            x, y, c = _pos()
            cps = []
            for i in range(len(mats)):
                cps.append(_remote(srcs[i].at[0], news[i].at[0], send.at[2 * i], recv.at[2 * i], (x, 1 - y, c)))
                cps.append(_remote(srcs[i].at[1], news[i].at[1], send.at[2 * i + 1], recv.at[2 * i + 1], (1 - x, y, c)))
            return cps

        def done(dsts, news):
            self.got_b2.update(zip(names, news))

        return [_SwapJob(build, 2 * len(names), srcs=[self.pass_on[n] for n in names], done=done,
                         news=[jax.ShapeDtypeStruct((2,) + _sub_shape(kind, r, c), BF16) for _, kind, g, r, c, _, _ in mats])]

    def _rs_c(self, names):
        mats = [MATS[n] for n in names]
        parts = [_shard_sum(MATS[n], self.core1, self.kept[n], self.got_b2[n]) for n in names]

        def build(srcs, dsts, news, send, recv):
            x, y, c = _pos()
            cps = []
            for i, (_, kind, g, r, cc, _, _) in enumerate(mats):
                mine = _half_of_shard(kind, dsts[i], c, r, cc)
                cps.append(_remote(mine, mine, send.at[i], recv.at[i], (x, y, 1 - c)))
            return cps

        def done(dsts, news):
            self.gshard.update(zip(names, dsts))

        return [_SwapJob(build, len(names), dsts=parts, done=done)]

    def finish(self, n):
        if n not in self.got_a:
            self.run("rs_a_" + n, self._rs_a((n,)))
        if n not in self.got_b1:
            self.run("rs_b1_" + n, self._rs_b1((n,)))
        if n not in self.got_b2:
            self.run("rs_b2_" + n, self._rs_b2((n,)))
        if n not in self.gshard:
            self.run("rs_c_" + n, self._rs_c((n,)))
        return self.gshard[n]


TABLE = {
    "gather_w_in": (("gather", ("w_in",)),),
    "relayout_w_in": (("gather", ("w_gate",)),),
    "mm_in": (("gather", ("w_up",)),),
    "attn_fwd": (("gather", ("w_attn_br", "w_ssd_br")),),
    "ssd_fwd": (("gather", ("w_o",)),),
    "swiglu_fwd": (("gather", ("w_down",)),),
    "mm_down": (("gather", ("w_ple_gate", "w_ple_proj")),),
    "mm_de": (("rs_a", ("w_ple_proj", "w_ple_gate")),),
    "mm_d_down": (("rs_b1", ("w_ple_proj", "w_ple_gate")),),
    "swiglu_bwd": (("rs_a", ("w_down",)), ("rs_b2", ("w_ple_proj", "w_ple_gate"))),
    "mm_d_gate": (("rs_b1", ("w_down",)),),
    "mm_d_up": (("rs_b2", ("w_down",)), ("rs_c", ("w_ple_proj", "w_ple_gate")), ("rs_a", ("w_gate",))),
    "mm_df_gate": (("rs_b1", ("w_gate",)), ("rs_a", ("w_up",)), ("rs_c", ("w_down",))),
    "mm_df_up": (("rs_b2", ("w_gate",)),),
    "norm_ffn_bwd": (("rs_c", ("w_gate",)),),
    "mm_dmerged": (("rs_a", ("w_o",)),),
    "mm_dyn": (("rs_a", ("w_attn_br", "w_ssd_br")),),
    "attn_bwd": (("rs_b1", ("w_up",)),),
    "gated_norm_bwd": (("rs_b2", ("w_up",)),),
    "ssd_bwd": (("rs_b1", ("w_o", "w_attn_br", "w_ssd_br")), ("rs_c", ("w_up",))),
    "conv_bwd": (("rs_b2", ("w_o", "w_attn_br", "w_ssd_br")),),
    "mm_d_in_send": (("rs_c", ("w_o", "w_attn_br", "w_ssd_br")),),
    "mm_d_in_keep": (("rs_a", ("w_in",)),),
    "mm_du": (("rs_b1", ("w_in",)),),
    "norm_mix_bwd": (("rs_b2", ("w_in",)),),
}


NDEV = 8


def _allreduce_small(v, *, name):
    rows = v.shape[0]

    def body(v_ref, o_ref, slots, send, recv):
        x, y, c = _pos()
        me = 4 * x + 2 * y + c
        slots[me] = v_ref[...]
        cps = []
        for k in range(1, NDEV):
            peer = (_flip(x, k & 4), _flip(y, k & 2), _flip(c, k & 1))
            cp = _remote(v_ref, slots.at[me], send.at[k - 1], recv.at[k - 1], peer)
            cp.start()
            cps.append(cp)
        for cp in cps:
            cp.wait()
        acc = slots[0]
        for s in range(1, NDEV):
            acc = acc + slots[s]
        o_ref[...] = acc

    return pl.pallas_call(
        body, name=name, out_shape=jax.ShapeDtypeStruct((rows, 128), F32),
        in_specs=[pl.BlockSpec(memory_space=pltpu.VMEM)], out_specs=pl.BlockSpec(memory_space=pltpu.VMEM),
        scratch_shapes=[pltpu.VMEM((NDEV, rows, 128), F32), pltpu.SemaphoreType.DMA((NDEV - 1,)),
                        pltpu.SemaphoreType.DMA((NDEV - 1,))],
    )(v)


def _adamw(w, g, m, v, *, name, tr=None, tc=None, jobs=()):
    r, c = w.shape
    tr = r if tr is None else tr
    c1 = 1.0 / (1.0 - B1 ** STEP)
    c2 = 1.0 / (1.0 - B2 ** STEP)

    def body(w_ref, g_ref, m_ref, v_ref, d_ref, mo_ref, vo_ref):
        gv = g_ref[...]
        mn = B1 * m_ref[...] + (1.0 - B1) * gv
        vn = B2 * v_ref[...] + (1.0 - B2) * (gv * gv)
        mo_ref[...] = mn
        vo_ref[...] = vn
        d_ref[...] = -LR * ((mn * c1) / (jnp.sqrt(vn * c2) + AEPS) + WD * w_ref[...])

    if tc is None:
        blk, grid = pl.BlockSpec((tr, c), lambda i: (i, 0)), (r // tr,)
    else:
        blk, grid = pl.BlockSpec((r, tc), lambda i: (0, i)), (c // tc,)
    o = jax.ShapeDtypeStruct((r, c), F32)
    return _call(
        body, jobs=jobs, name=name, out_shape=(o, o, o), grid=grid, in_specs=[blk] * 4, out_specs=(blk, blk, blk),
        compiler_params=_cp(("parallel",)),
    )(w, g, m, v)


WEIGHTS = ("g_mix", "w_in", "conv_w", "conv_b", "dt_bias", "a_log", "d_skip", "g_ssd", "sinks", "w_attn_br", "w_ssd_br",
           "w_o", "g_ffn", "w_gate", "w_up", "w_down", "g_ple", "w_ple_gate", "w_ple_proj", "g_final")
BIG = {
    "w_gate": 256, "w_up": 256, "w_down": 128, "w_ssd_br": 128, "w_o": 128, "w_ple_gate": 128, "w_attn_br": 256,
    "w_ple_proj": 256, "w_in": None,
}
SMALL = tuple(n for n in WEIGHTS if n not in BIG)


def _pack_small(parts):
    rows = []
    for a in parts:
        a = a.reshape(-1)
        rows.append(jnp.pad(a, (0, -a.shape[0] % 128)).reshape(-1, 128))
    out = jnp.concatenate(rows, axis=0)
    return jnp.pad(out, ((0, -out.shape[0] % 8), (0, 0)))


def _unpack_small(packed, shapes):
    out, r = [], 0
    for s in shapes:
        n = int(np.prod(s))
        nr = -(-n // 128)
        out.append(packed[r:r + nr].reshape(-1)[:n].reshape(s))
        r += nr
    return out


def kernel(x, p, positions, g_mix, w_in, conv_w, conv_b, dt_bias, a_log, d_skip, g_ssd, sinks, w_attn_br, w_ssd_br, w_o, g_ffn, w_gate, w_up, w_down, g_ple, w_ple_gate, w_ple_proj, g_final, loss_target, m_g_mix, m_w_in, m_conv_w, m_conv_b, m_dt_bias, m_a_log, m_d_skip, m_g_ssd, m_sinks, m_w_attn_br, m_w_ssd_br, m_w_o, m_g_ffn, m_w_gate, m_w_up, m_w_down, m_g_ple, m_w_ple_gate, m_w_ple_proj, m_g_final, v_g_mix, v_w_in, v_conv_w, v_conv_b, v_dt_bias, v_a_log, v_d_skip, v_g_ssd, v_sinks, v_w_attn_br, v_w_ssd_br, v_w_o, v_g_ffn, v_w_gate, v_w_up, v_w_down, v_g_ple, v_w_ple_gate, v_w_ple_proj, v_g_final):
    w = dict(zip(WEIGHTS, (g_mix, w_in, conv_w, conv_b, dt_bias, a_log, d_skip, g_ssd, sinks, w_attn_br, w_ssd_br, w_o,
                           g_ffn, w_gate, w_up, w_down, g_ple, w_ple_gate, w_ple_proj, g_final)))
    m = dict(zip(WEIGHTS, (m_g_mix, m_w_in, m_conv_w, m_conv_b, m_dt_bias, m_a_log, m_d_skip, m_g_ssd, m_sinks, m_w_attn_br,
                           m_w_ssd_br, m_w_o, m_g_ffn, m_w_gate, m_w_up, m_w_down, m_g_ple, m_w_ple_gate, m_w_ple_proj,
                           m_g_final)))
    v = dict(zip(WEIGHTS, (v_g_mix, v_w_in, v_conv_w, v_conv_b, v_dt_bias, v_a_log, v_d_skip, v_g_ssd, v_sinks, v_w_attn_br,
                           v_w_ssd_br, v_w_o, v_g_ffn, v_w_gate, v_w_up, v_w_down, v_g_ple, v_w_ple_gate, v_w_ple_proj,
                           v_g_final)))
    xi, yi, ci = _pos()
    chip = 2 * xi + yi
    t = x.shape[1]
    cshard = CONV // NCHIP

    shards = {n: w[n].astype(BF16) for n in MATS}
    shards["w_in"] = jnp.pad(shards["w_in"], ((0, 0), (0, 0), (0, SLAB_PAD - SLAB)))
    plan = _Plan(shards, TABLE)
    plan.run("gather_w_in", plan.jobs("gather_w_in"))
    placed = lax.dynamic_update_slice(jnp.zeros((CW, CONV), F32), w["conv_w"][0], (0, chip * cshard))
    conv_whole = _allreduce_small(jnp.where(ci == 0, placed, 0.0).reshape(-1, 128), name="gather_conv_w").reshape(CW, CONV)

    small = {n: w[n] for n in ("g_mix", "conv_b", "dt_bias", "a_log", "d_skip", "g_ssd", "sinks", "g_ffn", "g_ple", "g_final")}
    small["conv_w"] = conv_whole
    loss8, grad_x, gs = _local_step(x[0], p[0, 0], positions, loss_target[0], small, plan)

    order = ("g_mix", "conv_b", "dt_bias", "a_log", "d_skip", "g_ssd", "sinks", "g_ffn", "g_ple", "g_final", "conv_w")
    summed = _allreduce_small(_pack_small([loss8[0, :1]] + [gs[n] for n in order]), name="sum_small")
    parts = _unpack_small(summed, [(1,)] + [w[n].shape for n in order[:-1]] + [(CW, CONV)])
    loss = parts[0][0]
    grad = dict(zip(order, parts[1:]))
    grad["conv_w"] = lax.dynamic_slice(grad["conv_w"], (0, chip * cshard), (CW, cshard))[None]

    delta, new_m, new_v = {}, {}, {}
    for n, tr in BIG.items():
        grad[n] = plan.finish(n)[:, :, :w[n].shape[2]]
        if n == "w_in":
            d_, m_, v_ = _adamw(w[n][0].T, grad[n][0].T, m[n][0].T, v[n][0].T, tc=128, name="adamw_" + n)
            d_, m_, v_ = d_.T, m_.T, v_.T
        else:
            d_, m_, v_ = _adamw(w[n][0], grad[n][0], m[n][0], v[n][0], tr=tr, name="adamw_" + n)
        delta[n], new_m[n], new_v[n] = d_[None], m_[None], v_[None]
    shapes = [w[n].shape for n in SMALL]
    d_, m_, v_ = _adamw(_pack_small([w[n] for n in SMALL]), _pack_small([grad[n] for n in SMALL]),
                        _pack_small([m[n] for n in SMALL]), _pack_small([v[n] for n in SMALL]), tr=None, name="adamw_small")
    for n, a, b, c_ in zip(SMALL, _unpack_small(d_, shapes), _unpack_small(m_, shapes), _unpack_small(v_, shapes)):
        delta[n], new_m[n], new_v[n] = a, b, c_

    return (loss, grad_x[None], *[grad[n] for n in WEIGHTS], *[delta[n] for n in WEIGHTS],
            *[new_m[n] for n in WEIGHTS], *[new_v[n] for n in WEIGHTS])
```

```python
import functools

import jax
import jax.numpy as jnp
import numpy as np
from jax import lax
from jax.experimental import pallas as pl
from jax.experimental.pallas import tpu as pltpu

F32 = jnp.float32
BF16 = jnp.bfloat16
MESH = pl.DeviceIdType.MESH

D = 2048
HD = 64
NQH = 16
NKV = 4
QD = NQH * HD
KVD = NKV * HD
DI = 2048
NH = 32
NG = 4
NS = 128
CW = 4
L = 128
CONV = DI + 2 * NG * NS
FFN = 5632
PLE = 256
IN_DIM = QD + 2 * KVD + DI + CONV + NH + 2 * D
EPS = 1e-6
SSM_EPS = 1e-5
ROPE_THETA = 10000.0
LR, B1, B2, AEPS, WD, STEP = 0.001, 0.9, 0.999, 1e-08, 0.01, 10

O_GA, O_GS, O_Z, O_XBC, O_Q, O_K, O_V, O_DT = 0, 2048, 4096, 6144, 9216, 10240, 10496, 10752
DT_PAD = 512
NP = O_DT + DT_PAD
R_Q, R_K, R_V, R_Z, R_XBC, R_DT, R_GA, R_GS = 0, 1024, 1280, 1536, 3584, 6656, 6688, 8736

NCHIP = 4
VMEM_LIMIT = 52 * 1024 * 1024
NEG = -1e30


def _cp(sem=None):
    return pltpu.CompilerParams(dimension_semantics=sem, vmem_limit_bytes=VMEM_LIMIT)


def _dot(a, b):
    return lax.dot_general(a, b, (((1,), (0,)), ((), ())), preferred_element_type=F32)


def _dot_nt(a, b):
    return lax.dot_general(a, b, (((1,), (1,)), ((), ())), preferred_element_type=F32)


def _dot_tn(a, b):
    return lax.dot_general(a, b, (((0,), (0,)), ((), ())), preferred_element_type=F32)


def _sigmoid(x):
    return 1.0 / (1.0 + jnp.exp(-x))


def _bf16_dot(dot, da, db):
    @jax.custom_vjp
    def f(a, b):
        return dot(a.astype(BF16), b.astype(BF16))

    def fwd(a, b):
        return f(a, b), (a.astype(BF16), b.astype(BF16))

    def bwd(res, g):
        a, b = res
        g = g.astype(BF16)
        return da(g, a, b), db(g, a, b)

    f.defvjp(fwd, bwd)
    return f


_bdot = _bf16_dot(_dot, lambda g, a, b: _dot_nt(g, b), lambda g, a, b: _dot_tn(a, g))
_bdot_nt = _bf16_dot(_dot_nt, lambda g, a, b: _dot(g, b), lambda g, a, b: _dot_tn(g, a))
_bdot_tn = _bf16_dot(_dot_tn, lambda g, a, b: _dot_nt(b, g), lambda g, a, b: _dot(a, g))


ANY = pl.BlockSpec(memory_space=pl.ANY)


class _Job:
    srcs, dsts, news, scratch = (), (), (), ()
    has_mid = False

    def start(self, srcs, dsts, news, sems):
        raise NotImplementedError

    def mid(self, srcs, dsts, news, sems):
        pass

    def late(self, srcs, dsts, news, sems):
        pass

    def finish(self, srcs, dsts, news, sems):
        raise NotImplementedError

    def done(self, dsts, news):
        pass


def _call(body, *, jobs=(), name, out_shape, in_specs, out_specs, grid=(), scratch_shapes=(), compiler_params=None,
          aliases=None):
    jobs = [j for j in jobs if j is not None]
    aliases = dict(aliases or {})
    if not jobs:
        return pl.pallas_call(body, name=name, out_shape=out_shape, in_specs=in_specs, out_specs=out_specs, grid=grid,
                              scratch_shapes=scratch_shapes, compiler_params=compiler_params,
                              input_output_aliases=aliases)
    single = not isinstance(out_shape, (tuple, list))
    outs = [out_shape] if single else list(out_shape)
    ospecs = [out_specs] if single else list(out_specs)
    n_in, n_out, n_scr = len(in_specs), len(outs), len(scratch_shapes)
    srcs = [a for j in jobs for a in j.srcs]
    dsts = [a for j in jobs for a in j.dsts]
    news = [a for j in jobs for a in j.news]
    sems = [a for j in jobs for a in j.scratch]

    def wrapped(*refs):
        pos = n_in + len(srcs) + len(dsts)
        ins, jsrc = refs[:n_in], refs[n_in:n_in + len(srcs)]
        o_refs = refs[pos:pos + n_out]
        pos += n_out
        jdst, jnew = refs[pos:pos + len(dsts)], refs[pos + len(dsts):pos + len(dsts) + len(news)]
        pos += len(dsts) + len(news)
        scr, jsem = refs[pos:pos + n_scr], refs[pos + n_scr:]

        def run(which):
            a = b = c = d = 0
            for j in jobs:
                getattr(j, which)(jsrc[a:a + len(j.srcs)], jdst[b:b + len(j.dsts)], jnew[c:c + len(j.news)],
                                  jsem[d:d + len(j.scratch)])
                a, b, c, d = a + len(j.srcs), b + len(j.dsts), c + len(j.news), d + len(j.scratch)

        if not grid:
            run("start")
            run("mid")
            run("late")
            body(*ins, *o_refs, *scr)
            run("finish")
            return
        step = functools.reduce(lambda acc, a: acc * grid[a] + pl.program_id(a), range(len(grid)), 0)
        steps = int(np.prod(grid))
        pl.when(step == 0)(lambda: run("start"))
        if any(j.has_mid for j in jobs):
            pl.when(step == steps // 3)(lambda: run("mid"))
            pl.when(step == (2 * steps) // 3)(lambda: run("late"))
        body(*ins, *o_refs, *scr)
        pl.when(step == steps - 1)(lambda: run("finish"))

    call = pl.pallas_call(
        wrapped, name=name,
        out_shape=outs + [jax.ShapeDtypeStruct(a.shape, a.dtype) for a in dsts] + news,
        in_specs=list(in_specs) + [ANY] * (len(srcs) + len(dsts)),
        out_specs=ospecs + [ANY] * (len(dsts) + len(news)),
        grid=grid, scratch_shapes=list(scratch_shapes) + sems,
        input_output_aliases={**aliases, **{n_in + len(srcs) + i: n_out + i for i in range(len(dsts))}},
        compiler_params=_cp(("arbitrary",) * len(grid) if grid else None))

    def run_call(*args):
        res = call(*args, *srcs, *dsts)
        b, c = n_out, n_out + len(dsts)
        for j in jobs:
            j.done(res[b:b + len(j.dsts)], res[c:c + len(j.news)])
            b, c = b + len(j.dsts), c + len(j.news)
        return res[0] if single else tuple(res[:n_out])

    return run_call


def _matmul(a, b, *, ta=False, tb=False, out_dtype=F32, add=None, tm, tn, tk, name, jobs=()):
    k, m = a.shape if ta else a.shape[::-1]
    n = b.shape[0] if tb else b.shape[1]
    assert (b.shape[1] if tb else b.shape[0]) == k and not (ta and tb)
    assert m % tm == 0 and n % tn == 0 and k % tk == 0, (name, a.shape, b.shape)
    nk = k // tk
    has_add = add is not None

    def body(*refs):
        a_ref, b_ref = refs[0], refs[1]
        add_ref = refs[2] if has_add else None
        o_ref = refs[3] if has_add else refs[2]
        av = a_ref[...].astype(BF16)
        bv = b_ref[...].astype(BF16)
        part = _dot_tn(av, bv) if ta else _dot_nt(av, bv) if tb else _dot(av, bv)

        def finish(r):
            if has_add:
                r = r + add_ref[...]
            o_ref[...] = r.astype(out_dtype)

        if nk == 1:
            finish(part)
        elif out_dtype == F32:
            kk = pl.program_id(2)
            pl.when(kk == 0)(lambda: finish(part))

            @pl.when(kk > 0)
            def _():
                o_ref[...] += part
        else:
            acc_ref = refs[-1]
            kk = pl.program_id(2)

            @pl.when(kk == 0)
            def _():
                acc_ref[...] = part

            @pl.when(kk > 0)
            def _():
                acc_ref[...] += part

            @pl.when(kk == nk - 1)
            def _():
                finish(acc_ref[...])

    in_specs = [pl.BlockSpec((tk, tm), lambda i, j, kk: (kk, i)) if ta else pl.BlockSpec((tm, tk), lambda i, j, kk: (i, kk)),
                pl.BlockSpec((tn, tk), lambda i, j, kk: (j, kk)) if tb
                else pl.BlockSpec((tk, tn), lambda i, j, kk: (kk, j))]
    args = [a, b]
    if has_add:
        in_specs.append(pl.BlockSpec((tm, tn), lambda i, j, kk: (i, j)))
        args.append(add)
    return _call(
        body, jobs=jobs, name=name,
        out_shape=jax.ShapeDtypeStruct((m, n), out_dtype),
        grid=(m // tm, n // tn, nk),
        in_specs=in_specs,
        out_specs=pl.BlockSpec((tm, tn), lambda i, j, kk: (i, j)),
        scratch_shapes=[pltpu.VMEM((tm, tn), F32)] if nk > 1 and out_dtype != F32 else [],
        compiler_params=_cp(("parallel", "parallel", "arbitrary")),
    )(*args)


ROWS = 256


def _rmsnorm_fwd(x, g, *, name):
    t, d = x.shape

    def body(x_ref, g_ref, o_ref):
        xv = x_ref[...]
        r = lax.rsqrt(jnp.mean(xv * xv, axis=-1, keepdims=True) + EPS)
        o_ref[...] = (xv * r * g_ref[...]).astype(BF16)

    return pl.pallas_call(
        body, name=name, out_shape=jax.ShapeDtypeStruct((t, d), BF16), grid=(t // ROWS,),
        in_specs=[pl.BlockSpec((ROWS, d), lambda i: (i, 0)), pl.BlockSpec((1, d), lambda i: (0, 0))],
        out_specs=pl.BlockSpec((ROWS, d), lambda i: (i, 0)), compiler_params=_cp(("parallel",)),
    )(x, g)


def _rmsnorm_bwd(x, g, dy, dres, *, name, jobs=()):
    t, d = x.shape

    def body(x_ref, g_ref, dy_ref, dres_ref, dx_ref, dxb_ref, dg_ref):
        xv = x_ref[...]
        r = lax.rsqrt(jnp.mean(xv * xv, axis=-1, keepdims=True) + EPS)
        xh = xv * r
        dyv = dy_ref[...]
        dxh = dyv * g_ref[...]
        dx = r * (dxh - xh * jnp.mean(dxh * xh, axis=-1, keepdims=True))
        tot = dres_ref[...] + dx
        dx_ref[...] = tot
        dxb_ref[...] = tot.astype(BF16)

        @pl.when(pl.program_id(0) == 0)
        def _():
            dg_ref[...] = jnp.zeros_like(dg_ref)

        dg_ref[...] += jnp.broadcast_to(jnp.sum(dyv * xh, axis=0, keepdims=True), dg_ref.shape)

    row = pl.BlockSpec((ROWS, d), lambda i: (i, 0))
    return _call(
        body, jobs=jobs, name=name,
        out_shape=(jax.ShapeDtypeStruct((t, d), F32), jax.ShapeDtypeStruct((t, d), BF16),
                   jax.ShapeDtypeStruct((8, d), F32)),
        grid=(t // ROWS,),
        in_specs=[row, pl.BlockSpec((1, d), lambda i: (0, 0)), row, row],
        out_specs=(row, row, pl.BlockSpec((8, d), lambda i: (0, 0))),
        compiler_params=_cp(("arbitrary",)),
    )(x, g, dy, dres)


def _final(h2, pgl, pp, target, g_final, *, name):
    t, d = h2.shape

    def body(h2_ref, pgl_ref, pp_ref, tg_ref, g_ref, dh3_ref, dpgl_ref, dpp_ref, loss_ref, dg_ref):
        s = _sigmoid(pgl_ref[...])
        ppv = pp_ref[...]
        h3 = h2_ref[...] + s * ppv
        r = lax.rsqrt(jnp.mean(h3 * h3, axis=-1, keepdims=True) + EPS)
        xh = h3 * r
        gv = g_ref[...]
        err = xh * gv - tg_ref[...]
        dyv = err * (1.0 / d)
        dxh = dyv * gv
        dh3 = r * (dxh - xh * jnp.mean(dxh * xh, axis=-1, keepdims=True))
        dh3_ref[...] = dh3
        dpp_ref[...] = (dh3 * s).astype(BF16)
        dpgl_ref[...] = (dh3 * ppv * s * (1.0 - s)).astype(BF16)

        @pl.when(pl.program_id(0) == 0)
        def _():
            loss_ref[...] = jnp.zeros_like(loss_ref)
            dg_ref[...] = jnp.zeros_like(dg_ref)

        part = 0.5 * jnp.sum(jnp.mean(err * err, axis=-1, keepdims=True), axis=0, keepdims=True)
        loss_ref[...] += jnp.broadcast_to(part, loss_ref.shape)
        dg_ref[...] += jnp.broadcast_to(jnp.sum(dyv * xh, axis=0, keepdims=True), dg_ref.shape)

    row = pl.BlockSpec((ROWS, d), lambda i: (i, 0))
    return pl.pallas_call(
        body, name=name,
        out_shape=(jax.ShapeDtypeStruct((t, d), F32), jax.ShapeDtypeStruct((t, d), BF16),
                   jax.ShapeDtypeStruct((t, d), BF16), jax.ShapeDtypeStruct((8, 128), F32),
                   jax.ShapeDtypeStruct((8, d), F32)),
        grid=(t // ROWS,),
        in_specs=[row, row, row, row, pl.BlockSpec((1, d), lambda i: (0, 0))],
        out_specs=(row, row, row, pl.BlockSpec((8, 128), lambda i: (0, 0)), pl.BlockSpec((8, d), lambda i: (0, 0))),
        compiler_params=_cp(("arbitrary",)),
    )(h2, pgl, pp, target, g_final)


def _merge_fwd(proj, out_a, out_s, *, name):
    t = proj.shape[0]

    def body(ga_ref, gs_ref, a_ref, s_ref, o_ref):
        o_ref[...] = (_sigmoid(ga_ref[...]) * a_ref[...] + _sigmoid(gs_ref[...]) * s_ref[...]).astype(BF16)

    row = pl.BlockSpec((ROWS, D), lambda i: (i, 0))
    return pl.pallas_call(
        body, name=name, out_shape=jax.ShapeDtypeStruct((t, D), BF16), grid=(t // ROWS,),
        in_specs=[pl.BlockSpec((ROWS, D), lambda i: (i, O_GA // D)), pl.BlockSpec((ROWS, D), lambda i: (i, O_GS // D)),
                  row, row],
        out_specs=row, compiler_params=_cp(("parallel",)),
    )(proj, proj, out_a, out_s)


def _merge_bwd(proj, out_a, out_s, dmerged, *, name):
    t = proj.shape[0]
    assert O_GA == 0 and O_GS == D

    def body(ga_ref, gs_ref, a_ref, s_ref, dm_ref, da_ref, ds_ref, dp_ref):
        sa = _sigmoid(ga_ref[...])
        ss = _sigmoid(gs_ref[...])
        dm = dm_ref[...]
        da_ref[...] = (dm * sa).astype(BF16)
        ds_ref[...] = (dm * ss).astype(BF16)
        dp_ref[:, :D] = (dm * a_ref[...] * sa * (1.0 - sa)).astype(BF16)
        dp_ref[:, D:] = (dm * s_ref[...] * ss * (1.0 - ss)).astype(BF16)

    row = pl.BlockSpec((ROWS, D), lambda i: (i, 0))
    o = jax.ShapeDtypeStruct((t, D), BF16)
    return pl.pallas_call(
        body, name=name, out_shape=(o, o, jax.ShapeDtypeStruct((t, NP), BF16)), grid=(t // ROWS,),
        in_specs=[pl.BlockSpec((ROWS, D), lambda i: (i, O_GA // D)), pl.BlockSpec((ROWS, D), lambda i: (i, O_GS // D)),
                  row, row, row],
        out_specs=(row, row, pl.BlockSpec((ROWS, 2 * D), lambda i: (i, 0))), compiler_params=_cp(("parallel",)),
    )(proj, proj, out_a, out_s, dmerged)


def _swiglu_fwd(f, w_gate, w_up, *, name, tn=512, jobs=()):
    t, d = f.shape
    n = w_gate.shape[1]

    def body(f_ref, wg_ref, wu_ref, g_ref, u_ref, a_ref):
        fv = f_ref[...]
        g = _dot(fv, wg_ref[...])
        u = _dot(fv, wu_ref[...])
        g_ref[...] = g.astype(BF16)
        u_ref[...] = u.astype(BF16)
        a_ref[...] = (g * _sigmoid(g) * u).astype(BF16)

    col = pl.BlockSpec((t, tn), lambda j: (0, j))
    wcol = pl.BlockSpec((d, tn), lambda j: (0, j))
    return _call(
        body, jobs=jobs, name=name,
        out_shape=(jax.ShapeDtypeStruct((t, n), BF16), jax.ShapeDtypeStruct((t, n), BF16),
                   jax.ShapeDtypeStruct((t, n), BF16)),
        grid=(n // tn,),
        in_specs=[pl.BlockSpec((t, d), lambda j: (0, 0)), wcol, wcol],
        out_specs=(col, col, col), compiler_params=_cp(("parallel",)),
    )(f, w_gate, w_up)


def _swiglu_bwd(dh, w_down, gate, up, *, name, tn=512, jobs=()):
    t, d = dh.shape
    n = w_down.shape[0]

    def body(dh_ref, w_ref, g_ref, u_ref, dg_ref, du_ref):
        da = _dot_nt(dh_ref[...], w_ref[...])
        g = g_ref[...].astype(F32)
        s = _sigmoid(g)
        du_ref[...] = (da * g * s).astype(BF16)
        dg_ref[...] = (da * u_ref[...].astype(F32) * s * (1.0 + g * (1.0 - s))).astype(BF16)

    col = pl.BlockSpec((t, tn), lambda j: (0, j))
    o = jax.ShapeDtypeStruct((t, n), BF16)
    return _call(
        body, jobs=jobs, name=name, out_shape=(o, o), grid=(n // tn,),
        in_specs=[pl.BlockSpec((t, d), lambda j: (0, 0)), pl.BlockSpec((tn, d), lambda j: (j, 0)), col, col],
        out_specs=(col, col), compiler_params=_cp(("parallel",)),
    )(dh, w_down, gate, up)


def _gated_norm_fwd(y_pre, proj, g_ssd, *, name):
    t = y_pre.shape[0]

    def body(y_ref, z_ref, g_ref, o_ref):
        z = z_ref[...]
        v = y_ref[...] * z * _sigmoid(z)
        r = lax.rsqrt(jnp.mean(v * v, axis=-1, keepdims=True) + SSM_EPS)
        o_ref[...] = (v * r * g_ref[...]).astype(BF16)

    row = pl.BlockSpec((ROWS, DI), lambda i: (i, 0))
    return pl.pallas_call(
        body, name=name, out_shape=jax.ShapeDtypeStruct((t, DI), BF16), grid=(t // ROWS,),
        in_specs=[row, pl.BlockSpec((ROWS, DI), lambda i: (i, O_Z // DI)), pl.BlockSpec((1, DI), lambda i: (0, 0))],
        out_specs=row, compiler_params=_cp(("parallel",)),
    )(y_pre, proj, g_ssd)


def _gated_norm_bwd(y_pre, proj, g_ssd, dyn, dproj, *, name, jobs=()):
    t = y_pre.shape[0]

    def body(y_ref, z_ref, g_ref, dyn_ref, _, dy_ref, dz_ref, dg_ref):
        z = z_ref[...]
        s = _sigmoid(z)
        sz = z * s
        yv = y_ref[...]
        v = yv * sz
        r = lax.rsqrt(jnp.mean(v * v, axis=-1, keepdims=True) + SSM_EPS)
        vh = v * r
        dn = dyn_ref[...]
        dvh = dn * g_ref[...]
        dv = r * (dvh - vh * jnp.mean(dvh * vh, axis=-1, keepdims=True))
        dy_ref[...] = dv * sz
        dz_ref[...] = (dv * yv * s * (1.0 + z * (1.0 - s))).astype(BF16)

        @pl.when(pl.program_id(0) == 0)
        def _():
            dg_ref[...] = jnp.zeros_like(dg_ref)

        dg_ref[...] += jnp.broadcast_to(jnp.sum(dn * vh, axis=0, keepdims=True), dg_ref.shape)

    row = pl.BlockSpec((ROWS, DI), lambda i: (i, 0))
    return _call(
        body, jobs=jobs, name=name,
        out_shape=(jax.ShapeDtypeStruct((t, DI), F32), jax.ShapeDtypeStruct(dproj.shape, BF16),
                   jax.ShapeDtypeStruct((8, DI), F32)),
        grid=(t // ROWS,),
        in_specs=[row, pl.BlockSpec((ROWS, DI), lambda i: (i, O_Z // DI)), pl.BlockSpec((1, DI), lambda i: (0, 0)), row, ANY],
        out_specs=(row, pl.BlockSpec((ROWS, DI), lambda i: (i, O_Z // DI)), pl.BlockSpec((8, DI), lambda i: (0, 0))),
        compiler_params=_cp(("arbitrary",)), aliases={4: 1},
    )(y_pre, proj, g_ssd, dyn, dproj)


CONV_TC = 512


def _shift_down(x, s, row):
    if s == 0:
        return x
    return jnp.where(row >= s, pltpu.roll(x, s, 0), 0.0)


def _shift_up(x, s, row, t):
    if s == 0:
        return x
    return jnp.where(row < t - s, pltpu.roll(x, t - s, 0), 0.0)


def _conv_fwd(proj, conv_w, conv_b, *, name):
    t = proj.shape[0]

    def body(x_ref, w_ref, b_ref, o_ref):
        x = x_ref[...]
        row = lax.broadcasted_iota(jnp.int32, x.shape, 0)
        pre = jnp.broadcast_to(b_ref[...], x.shape)
        for k in range(CW):
            pre = pre + w_ref[k:k + 1, :] * _shift_down(x, CW - 1 - k, row)
        o_ref[...] = pre * _sigmoid(pre)

    return pl.pallas_call(
        body, name=name, out_shape=jax.ShapeDtypeStruct((t, CONV), F32), grid=(CONV // CONV_TC,),
        in_specs=[pl.BlockSpec((t, CONV_TC), lambda j: (0, O_XBC // CONV_TC + j)),
                  pl.BlockSpec((CW, CONV_TC), lambda j: (0, j)), pl.BlockSpec((1, CONV_TC), lambda j: (0, j))],
        out_specs=pl.BlockSpec((t, CONV_TC), lambda j: (0, j)), compiler_params=_cp(("parallel",)),
    )(proj, conv_w, conv_b)


def _conv_bwd(proj, conv_w, conv_b, dxs, db, dc, dproj, *, name, jobs=()):
    t = proj.shape[0]
    nx = DI // CONV_TC
    assert NG * NS == CONV_TC

    def body(x_ref, w_ref, b_ref, dxs_ref, db_ref, dc_ref, _, dx_ref, dw_ref, dbias_ref):
        j = pl.program_id(0)
        x = x_ref[...]
        row = lax.broadcasted_iota(jnp.int32, x.shape, 0)
        xs = [_shift_down(x, CW - 1 - k, row) for k in range(CW)]
        pre = jnp.broadcast_to(b_ref[...], x.shape)
        for k in range(CW):
            pre = pre + w_ref[k:k + 1, :] * xs[k]
        s = _sigmoid(pre)
        da = jnp.where(j < nx, dxs_ref[...], jnp.where(j == nx, db_ref[...], dc_ref[...]))
        dpre = da * s * (1.0 + pre * (1.0 - s))
        dx = jnp.zeros_like(x)
        row8 = lax.broadcasted_iota(jnp.int32, dw_ref.shape, 0)
        dw = jnp.zeros(dw_ref.shape, F32)
        for k in range(CW):
            dx = dx + w_ref[k:k + 1, :] * _shift_up(dpre, CW - 1 - k, row, t)
            dw = dw + jnp.where(row8 == k, jnp.sum(dpre * xs[k], axis=0, keepdims=True), 0.0)
        dx_ref[...] = dx.astype(BF16)
        dw_ref[...] = dw
        dbias_ref[...] = jnp.broadcast_to(jnp.sum(dpre, axis=0, keepdims=True), dbias_ref.shape)

    col8 = pl.BlockSpec((8, CONV_TC), lambda j: (0, j))
    xbc = pl.BlockSpec((t, CONV_TC), lambda j: (0, O_XBC // CONV_TC + j))
    whole = pl.BlockSpec((t, CONV_TC), lambda j: (0, 0))
    return _call(
        body, jobs=jobs, name=name,
        out_shape=(jax.ShapeDtypeStruct(dproj.shape, BF16), jax.ShapeDtypeStruct((8, CONV), F32),
                   jax.ShapeDtypeStruct((8, CONV), F32)),
        grid=(CONV // CONV_TC,),
        in_specs=[xbc, pl.BlockSpec((CW, CONV_TC), lambda j: (0, j)), pl.BlockSpec((1, CONV_TC), lambda j: (0, j)),
                  pl.BlockSpec((t, CONV_TC), lambda j: (0, jnp.minimum(j, nx - 1))), whole, whole, ANY],
        out_specs=(xbc, col8, col8),
        compiler_params=_cp(("arbitrary",)), aliases={6: 0},
    )(proj, conv_w, conv_b, dxs, db, dc, dproj)


def _rope_tables(positions, t):
    half = HD // 2
    inv_freq = ROPE_THETA ** (-jnp.arange(half, dtype=F32) * 2.0 / HD)
    ang = positions.reshape(t).astype(F32)[:, None] * inv_freq
    cos, sin = jnp.cos(ang), jnp.sin(ang)
    return jnp.concatenate([cos] * 4, axis=1), jnp.concatenate([-sin, sin] * 2, axis=1)


def _lane_consts():
    lane = lax.broadcasted_iota(jnp.int32, (L, 128), 1)
    return lane, (lane % HD) < (HD // 2), lane < HD


def _rope(tv, cos, sin, lo):
    return tv * cos + jnp.where(lo, pltpu.roll(tv, 128 - HD // 2, 1), pltpu.roll(tv, HD // 2, 1)) * sin


def _rope_t(dv, cos, sin, lo):
    ds = dv * sin
    return dv * cos + jnp.where(lo, pltpu.roll(ds, 128 - HD // 2, 1), pltpu.roll(ds, HD // 2, 1))


def _placed(chunk, g, half0):
    own = jnp.where(half0 if g % 2 == 0 else jnp.logical_not(half0), chunk, 0.0)
    other = pltpu.roll(own, HD, 1)
    return (own, other) if g % 2 == 0 else (other, own)


def _unplace(acc, hf, g, half0):
    v = jnp.where(half0 if hf == 0 else jnp.logical_not(half0), acc, 0.0)
    return v if hf == g % 2 else pltpu.roll(v, HD, 1)


def _attn_fwd(proj, cos, sin, sinks, *, name, jobs=()):
    t = proj.shape[0]
    nb = t // L
    scale = HD ** -0.5

    def body(sink_ref, q_ref, kc_ref, kp_ref, vc_ref, vp_ref, cc_ref, sc_ref, cp_ref, sp_ref, o_ref, lse_ref):
        i = pl.program_id(0)
        lane, lo, half0 = _lane_consts()
        cos_c, sin_c, cos_p, sin_p = cc_ref[...], sc_ref[...], cp_ref[...], sp_ref[...]
        row = lax.broadcasted_iota(jnp.int32, (L, 2 * L), 0)
        col = lax.broadcasted_iota(jnp.int32, (L, 2 * L), 1)
        valid = jnp.logical_or(jnp.logical_and(jnp.logical_and(col < L, col > row), i > 0),
                               jnp.logical_and(col >= L, col - L <= row))
        kc = [_rope(kc_ref[:, 128 * m:128 * (m + 1)], cos_c, sin_c, lo) for m in range(2)]
        kp = [_rope(kp_ref[:, 128 * m:128 * (m + 1)], cos_p, sin_p, lo) for m in range(2)]
        lse_acc = jnp.zeros((L, 128), F32)
        outs = [jnp.zeros((L, 128), F32) for _ in range(QD // 128)]
        qs = [(_rope(q_ref[:, 128 * ch:128 * (ch + 1)], cos_c, sin_c, lo) * scale).astype(BF16) for ch in range(QD // 128)]
        both = lambda prev, cur, g: [jnp.concatenate([a, b], axis=0).astype(BF16)
                                     for a, b in zip(_placed(prev, g, half0), _placed(cur, g, half0))]
        for g in range(NKV):
            sl = slice(128 * (g // 2), 128 * (g // 2 + 1))
            kv = both(kp[g // 2], kc[g // 2], g)
            vv = both(vp_ref[:, sl], vc_ref[:, sl], g)
            for r in range(NQH // NKV):
                h = g * (NQH // NKV) + r
                ch, hf = h // 2, h % 2
                s = jnp.where(valid, _dot_nt(qs[ch], kv[hf]), NEG)
                sink = sink_ref[0, h]
                mx = jnp.maximum(jnp.max(s, axis=-1, keepdims=True), sink)
                e = jnp.exp(s - mx)
                den = jnp.sum(e, axis=-1, keepdims=True) + jnp.exp(sink - mx)
                outs[ch] = outs[ch] + _dot((e * (1.0 / den)).astype(BF16), vv[hf])
                lse_acc = jnp.where(lane == h, mx + jnp.log(den), lse_acc)
        for ch in range(QD // 128):
            o_ref[:, 128 * ch:128 * (ch + 1)] = outs[ch].astype(BF16)
        lse_ref[...] = lse_acc

    prev = lambda i: jnp.maximum(i - 1, 0)
    tab_c = pl.BlockSpec((L, 128), lambda i: (i, 0))
    tab_p = pl.BlockSpec((L, 128), lambda i: (prev(i), 0))
    return _call(
        body, jobs=jobs, name=name,
        out_shape=(jax.ShapeDtypeStruct((t, QD), BF16), jax.ShapeDtypeStruct((t, 128), F32)),
        grid=(nb,),
        in_specs=[pl.BlockSpec(memory_space=pltpu.SMEM),
                  pl.BlockSpec((L, QD), lambda i: (i, O_Q // QD)),
                  pl.BlockSpec((L, KVD), lambda i: (i, O_K // KVD)), pl.BlockSpec((L, KVD), lambda i: (prev(i), O_K // KVD)),
                  pl.BlockSpec((L, KVD), lambda i: (i, O_V // KVD)), pl.BlockSpec((L, KVD), lambda i: (prev(i), O_V // KVD)),
                  tab_c, tab_c, tab_p, tab_p],
        out_specs=(pl.BlockSpec((L, QD), lambda i: (i, 0)), pl.BlockSpec((L, 128), lambda i: (i, 0))),
        compiler_params=_cp(("parallel",)),
    )(sinks, proj, proj, proj, proj, proj, cos, sin, cos, sin)


def _attn_bwd(proj, cos, sin, sinks, attn, lse, dattn, dproj, *, name, jobs=()):
    t = proj.shape[0]
    nb = t // L
    scale = HD ** -0.5

    def body(sink_ref, qi_ref, qn_ref, kc_ref, kp_ref, vc_ref, vp_ref, doi_ref, don_ref, oi_ref, on_ref,
             lsei_ref, lsen_ref, cc_ref, sc_ref, cp_ref, sp_ref, cn_ref, sn_ref, _, dqkv_ref, dsk_ref):
        i = pl.program_id(0)
        lane, lo, half0 = _lane_consts()
        half1 = jnp.logical_not(half0)
        cos_c, sin_c = cc_ref[...], sc_ref[...]
        row = lax.broadcasted_iota(jnp.int32, (L, 2 * L), 0)
        col = lax.broadcasted_iota(jnp.int32, (L, 2 * L), 1)
        valid = jnp.logical_or(jnp.logical_and(jnp.logical_and(col < L, col > row), i > 0),
                               jnp.logical_and(col >= L, col - L <= row))
        m_next = jnp.logical_and(col[:, :L] > row[:, :L], i < nb - 1)
        kc = [_rope(kc_ref[:, 128 * m:128 * (m + 1)], cos_c, sin_c, lo) for m in range(2)]
        kp = [_rope(kp_ref[:, 128 * m:128 * (m + 1)], cp_ref[...], sp_ref[...], lo) for m in range(2)]
        lse_i, lse_n = lsei_ref[...], lsen_ref[...]
        dk_acc = [jnp.zeros((L, 128), F32) for _ in range(2)]
        dv_acc = [jnp.zeros((L, 128), F32) for _ in range(2)]
        dsk_acc = jnp.zeros((1, 128), F32)
        lane1 = lax.broadcasted_iota(jnp.int32, (1, 128), 1)
        both = lambda prev, cur, g: [jnp.concatenate([a, b], axis=0).astype(BF16)
                                     for a, b in zip(_placed(prev, g, half0), _placed(cur, g, half0))]
        kvs = [both(kp[g // 2], kc[g // 2], g) for g in range(NKV)]
        vvs = [both(vp_ref[:, 128 * (g // 2):128 * (g // 2 + 1)], vc_ref[:, 128 * (g // 2):128 * (g // 2 + 1)], g)
               for g in range(NKV)]
        for ch in range(QD // 128):
            sl = slice(128 * ch, 128 * (ch + 1))
            q_i = (_rope(qi_ref[:, sl], cos_c, sin_c, lo) * scale).astype(BF16)
            q_n = (_rope(qn_ref[:, sl], cn_ref[...], sn_ref[...], lo) * scale).astype(BF16)
            q_in = jnp.concatenate([q_i, q_n], axis=0)
            do_i, do_n = doi_ref[:, sl], don_ref[:, sl]
            do_ib, do_nb = do_i.astype(BF16), do_n.astype(BF16)
            do_in = jnp.concatenate([do_ib, do_nb], axis=0)
            od_i = do_i * oi_ref[:, sl].astype(F32)
            od_n = do_n * on_ref[:, sl].astype(F32)
            dq_ch = jnp.zeros((L, 128), F32)
            for hf in range(2):
                h = 2 * ch + hf
                g = h // (NQH // NKV)
                hm = half0 if hf == 0 else half1
                kv, vv = kvs[g][hf], vvs[g][hf]
                kcv, vcv = kv[L:], vv[L:]
                dl_i = jnp.sum(jnp.where(hm, od_i, 0.0), axis=-1, keepdims=True)
                dl_n = jnp.sum(jnp.where(hm, od_n, 0.0), axis=-1, keepdims=True)
                ls_i = jnp.sum(jnp.where(lane == h, lse_i, 0.0), axis=-1, keepdims=True)
                ls_n = jnp.sum(jnp.where(lane == h, lse_n, 0.0), axis=-1, keepdims=True)
                p = jnp.where(valid, jnp.exp(_dot_nt(q_i, kv) - ls_i), 0.0)
                ds = (p * (_dot_nt(do_ib, vv) - dl_i)).astype(BF16)
                dq_ch = dq_ch + jnp.where(hm, _dot(ds, kv) * scale, 0.0)
                sink = sink_ref[0, h]
                dsk = -jnp.sum(jnp.exp(sink - ls_i) * dl_i, axis=0, keepdims=True)
                dsk_acc = dsk_acc + jnp.where(lane1 == h, dsk, 0.0)
                p_n = jnp.where(m_next, jnp.exp(_dot_nt(q_n, kcv) - ls_n), 0.0)
                ds_n = (p_n * (_dot_nt(do_nb, vcv) - dl_n)).astype(BF16)
                dv_h = _dot_tn(jnp.concatenate([p[:, L:].astype(BF16), p_n.astype(BF16)], axis=0), do_in)
                dk_h = _dot_tn(jnp.concatenate([ds[:, L:], ds_n], axis=0), q_in)
                dv_acc[g // 2] = dv_acc[g // 2] + _unplace(dv_h, hf, g, half0)
                dk_acc[g // 2] = dk_acc[g // 2] + _unplace(dk_h, hf, g, half0)
            dqkv_ref[:, sl] = _rope_t(dq_ch, cos_c, sin_c, lo).astype(BF16)
        for m in range(2):
            dqkv_ref[:, QD + 128 * m:QD + 128 * (m + 1)] = _rope_t(dk_acc[m], cos_c, sin_c, lo).astype(BF16)
            dqkv_ref[:, QD + KVD + 128 * m:QD + KVD + 128 * (m + 1)] = dv_acc[m].astype(BF16)

        @pl.when(i == 0)
        def _():
            dsk_ref[...] = jnp.zeros_like(dsk_ref)

        dsk_ref[...] += jnp.broadcast_to(dsk_acc, dsk_ref.shape)

    prev = lambda i: jnp.maximum(i - 1, 0)
    nxt = lambda i: jnp.minimum(i + 1, nb - 1)
    cur_q = pl.BlockSpec((L, QD), lambda i: (i, 0))
    nxt_q = pl.BlockSpec((L, QD), lambda i: (nxt(i), 0))
    tab = lambda f: pl.BlockSpec((L, 128), lambda i: (f(i), 0))
    ident = lambda i: i
    qkv = QD + 2 * KVD
    assert O_K == O_Q + QD and O_V == O_K + KVD and O_Q % qkv == 0
    return _call(
        body, jobs=jobs, name=name,
        out_shape=(jax.ShapeDtypeStruct(dproj.shape, BF16), jax.ShapeDtypeStruct((8, 128), F32)),
        grid=(nb,),
        in_specs=[pl.BlockSpec(memory_space=pltpu.SMEM),
                  pl.BlockSpec((L, QD), lambda i: (i, O_Q // QD)), pl.BlockSpec((L, QD), lambda i: (nxt(i), O_Q // QD)),
                  pl.BlockSpec((L, KVD), lambda i: (i, O_K // KVD)), pl.BlockSpec((L, KVD), lambda i: (prev(i), O_K // KVD)),
                  pl.BlockSpec((L, KVD), lambda i: (i, O_V // KVD)), pl.BlockSpec((L, KVD), lambda i: (prev(i), O_V // KVD)),
                  cur_q, nxt_q, cur_q, nxt_q, tab(ident), tab(nxt),
                  tab(ident), tab(ident), tab(prev), tab(prev), tab(nxt), tab(nxt), ANY],
        out_specs=(pl.BlockSpec((L, qkv), lambda i: (i, O_Q // qkv)), pl.BlockSpec((8, 128), lambda i: (0, 0))),
        compiler_params=_cp(("arbitrary",)), aliases={19: 0},
    )(sinks, proj, proj, proj, proj, proj, proj, dattn, dattn, attn, attn, lse, lse, cos, sin, cos, sin, cos, sin, dproj)


PAIRS = NH // NG // 2


def _softplus(x):
    return jnp.maximum(x, 0.0) + jnp.log(1.0 + jnp.exp(-jnp.abs(x)))


def _ssd_chunk(g, xps, dtr, bm, cm, sps, dtb, alog, dsk):
    lane = lax.broadcasted_iota(jnp.int32, (L, 128), 1)
    lane1 = lax.broadcasted_iota(jnp.int32, (1, 128), 1)
    row = lax.broadcasted_iota(jnp.int32, (L, L), 0)
    col = lax.broadcasted_iota(jnp.int32, (L, L), 1)
    rowc = lax.broadcasted_iota(jnp.int32, (128, 1), 0)
    tril = col <= row
    dt = _softplus(dtr + dtb)
    a = dt * (-jnp.exp(alog))
    a_cs = lax.dot_general(tril.astype(F32), a, (((1,), (0,)), ((), ())), precision=lax.Precision.HIGHEST,
                           preferred_element_type=F32)
    a_cst = a_cs.T
    a_last = jnp.sum(jnp.where(row == L - 1, a_cs, 0.0), axis=0, keepdims=True)
    cb = _bdot_nt(cm, bm)
    ys, snew = [], []
    for q in range(PAIRS):
        xp, sp = xps[q], sps[q]
        skip = jnp.zeros((L, 128), F32)
        keep = jnp.zeros((128, 1), F32)
        ms, xds, cds, sms, bds = [], [], [], [], []
        for hh in range(2):
            h = g * 2 * PAIRS + 2 * q + hh
            hm = (lane < HD) if hh == 0 else (lane >= HD)
            rm = (rowc < HD) if hh == 0 else (rowc >= HD)
            dt_h = jnp.sum(jnp.where(lane == h, dt, 0.0), axis=1, keepdims=True)
            acs_h = jnp.sum(jnp.where(lane == h, a_cs, 0.0), axis=1, keepdims=True)
            acst_h = jnp.sum(jnp.where(row == h, a_cst, 0.0), axis=0, keepdims=True)
            al_h = jnp.sum(jnp.where(lane1 == h, a_last, 0.0), axis=1, keepdims=True)
            dsk_h = jnp.sum(jnp.where(lane1 == h, dsk, 0.0), axis=1, keepdims=True)
            decay = jnp.where(tril, jnp.exp(jnp.where(tril, acs_h - acst_h, 0.0)), 0.0)
            xh = jnp.where(hm, xp, 0.0)
            ms.append(cb * decay)
            xds.append(xh * dt_h)
            cds.append(cm * jnp.exp(acs_h))
            sms.append(jnp.where(rm, sp, 0.0))
            bds.append(bm * jnp.exp(al_h - acs_h))
            skip = skip + dsk_h * xh
            keep = keep + jnp.where(rm, jnp.exp(al_h), 0.0)
        xd2 = jnp.concatenate(xds, axis=0)
        y_pair = (_bdot(jnp.concatenate(ms, axis=1), xd2)
                  + _bdot_nt(jnp.concatenate(cds, axis=1), jnp.concatenate(sms, axis=1)) + skip)
        ys.append(y_pair)
        snew.append(sp * keep + _bdot_tn(xd2, jnp.concatenate(bds, axis=0)))
    return ys, snew


def _ssd_specs(t):
    nc = t // L
    xs = lambda f: pl.BlockSpec((L, 128 * PAIRS), lambda c, g: (f(c), g))
    bspec = lambda f: pl.BlockSpec((L, NS), lambda c, g: (f(c), DI // NS + g))
    cspec = lambda f: pl.BlockSpec((L, NS), lambda c, g: (f(c), DI // NS + NG + g))
    dts = lambda f: pl.BlockSpec((L, 128), lambda c, g: (f(c), O_DT // 128))
    par = pl.BlockSpec((1, 128), lambda c, g: (0, 0))
    st = lambda f: pl.BlockSpec((1, 1, PAIRS, 128, NS), lambda c, g: (f(c), g, 0, 0, 0))
    return nc, xs, bspec, cspec, dts, par, st


def _ssd_fwd(xbc_act, proj, dtb, alog, dsk, *, name, jobs=()):
    t = proj.shape[0]
    nc, xs, bspec, cspec, dts, par, st = _ssd_specs(t)
    ident = lambda c: c

    def body(x_ref, b_ref, c_ref, dt_ref, dtb_ref, al_ref, dsk_ref, y_ref, sin_ref, s_ref):
        c, g = pl.program_id(0), pl.program_id(1)

        @pl.when(c == 0)
        def _():
            s_ref[g] = jnp.zeros((PAIRS, 128, NS), F32)

        sps = [s_ref[g, q] for q in range(PAIRS)]
        for q in range(PAIRS):
            sin_ref[0, 0, q] = sps[q]
        xps = [x_ref[:, 128 * q:128 * (q + 1)] for q in range(PAIRS)]
        ys, snew = _ssd_chunk(g, xps, dt_ref[...], b_ref[...], c_ref[...], sps, dtb_ref[...], al_ref[...], dsk_ref[...])
        for q in range(PAIRS):
            y_ref[:, 128 * q:128 * (q + 1)] = ys[q]
            s_ref[g, q] = snew[q]

    return _call(
        body, jobs=jobs, name=name,
        out_shape=(jax.ShapeDtypeStruct((t, DI), F32), jax.ShapeDtypeStruct((nc, NG, PAIRS, 128, NS), F32)),
        grid=(nc, NG),
        in_specs=[xs(ident), bspec(ident), cspec(ident), dts(ident), par, par, par],
        out_specs=(pl.BlockSpec((L, 128 * PAIRS), lambda c, g: (c, g)), st(ident)),
        scratch_shapes=[pltpu.VMEM((NG, PAIRS, 128, NS), F32)],
        compiler_params=_cp(("arbitrary", "arbitrary")),
    )(xbc_act, xbc_act, xbc_act, proj, dtb, alog, dsk)


def _ssd_bwd(xbc_act, proj, dtb, alog, dsk, states, dy, dproj, *, name, jobs=()):
    t = proj.shape[0]
    nc, xs, bspec, cspec, dts, par, st = _ssd_specs(t)
    rev = lambda c: nc - 1 - c

    def body(x_ref, b_ref, c_ref, dt_ref, dtb_ref, al_ref, dsk_ref, sin_ref, dy_ref, _,
             dx_ref, db_ref, dc_ref, ddtp_ref, ddtb_ref, dal_ref, ddsk_ref, ds_ref, ddt_ref):
        c, g = pl.program_id(0), pl.program_id(1)

        @pl.when(c == 0)
        def _():
            ds_ref[g] = jnp.zeros((PAIRS, 128, NS), F32)

        @pl.when(jnp.logical_and(c == 0, g == 0))
        def _():
            ddtb_ref[...] = jnp.zeros_like(ddtb_ref)
            dal_ref[...] = jnp.zeros_like(dal_ref)
            ddsk_ref[...] = jnp.zeros_like(ddsk_ref)

        @pl.when(g == 0)
        def _():
            ddt_ref[...] = jnp.zeros_like(ddt_ref)

        sps = [sin_ref[0, 0, q] for q in range(PAIRS)]
        xps = [x_ref[:, 128 * q:128 * (q + 1)] for q in range(PAIRS)]
        _, vjp = jax.vjp(functools.partial(_ssd_chunk, g), xps, dt_ref[...], b_ref[...], c_ref[...], sps,
                         dtb_ref[...], al_ref[...], dsk_ref[...])
        dys = [dy_ref[:, 128 * q:128 * (q + 1)] for q in range(PAIRS)]
        dss = [ds_ref[g, q] for q in range(PAIRS)]
        dxps, ddt, db, dc, dsps, ddtb, dal, ddsk = vjp((dys, dss))
        for q in range(PAIRS):
            dx_ref[:, 128 * q:128 * (q + 1)] = dxps[q]
            ds_ref[g, q] = dsps[q]
        db_ref[...] = db
        dc_ref[...] = dc
        ddt_ref[...] += ddt
        ddtb_ref[...] += jnp.broadcast_to(ddtb, ddtb_ref.shape)
        dal_ref[...] += jnp.broadcast_to(dal, dal_ref.shape)
        ddsk_ref[...] += jnp.broadcast_to(ddsk, ddsk_ref.shape)

        @pl.when(g == NG - 1)
        def _():
            ddtp_ref[:, :128] = ddt_ref[...].astype(BF16)
            ddtp_ref[:, 128:] = jnp.zeros((L, DT_PAD - 128), BF16)

    acc = pl.BlockSpec((8, 128), lambda c, g: (0, 0))
    o8 = jax.ShapeDtypeStruct((8, 128), F32)
    return _call(
        body, jobs=jobs, name=name,
        out_shape=(jax.ShapeDtypeStruct((t, DI), F32), jax.ShapeDtypeStruct((t, NG * NS), F32),
                   jax.ShapeDtypeStruct((t, NG * NS), F32), jax.ShapeDtypeStruct(dproj.shape, BF16), o8, o8, o8),
        grid=(nc, NG),
        in_specs=[xs(rev), bspec(rev), cspec(rev), dts(rev), par, par, par, st(rev),
                  pl.BlockSpec((L, 128 * PAIRS), lambda c, g: (rev(c), g)), ANY],
        out_specs=(pl.BlockSpec((L, 128 * PAIRS), lambda c, g: (rev(c), g)),
                   pl.BlockSpec((L, NS), lambda c, g: (rev(c), g)), pl.BlockSpec((L, NS), lambda c, g: (rev(c), g)),
                   pl.BlockSpec((L, DT_PAD), lambda c, g: (rev(c), O_DT // DT_PAD)), acc, acc, acc),
        scratch_shapes=[pltpu.VMEM((NG, PAIRS, 128, NS), F32), pltpu.VMEM((L, 128), F32)],
        compiler_params=_cp(("arbitrary", "arbitrary")), aliases={9: 3},
    )(xbc_act, xbc_act, xbc_act, proj, dtb, alog, dsk, states, dy, dproj)


def _pad_lanes(v, n=128):
    return jnp.pad(v, ((0, 0), (0, n - v.shape[1])))


def _local_step(x, p, positions, target, small, plan):
    t = x.shape[0]
    cos, sin = _rope_tables(positions, t)
    dtb, alog, dsk = _pad_lanes(small["dt_bias"]), _pad_lanes(small["a_log"]), _pad_lanes(small["d_skip"])
    w, jobs = plan.w, plan.jobs

    def mm(a, b, *, name, tm=t, tn=512, **kw):
        return _matmul(a, b, tm=tm, tn=tn, name=name, jobs=jobs(name), **kw)

    tkl = FFN // 4

    def dw(wname, a, dy, *, name, tm):
        plan.g(wname, _matmul(a, dy, ta=True, out_dtype=BF16, tm=tm, tn=512, tk=t, name=name, jobs=jobs(name)))

    u = _rmsnorm_fwd(x, small["g_mix"], name="norm_mix")
    proj = mm(u, w("w_in"), tn=1024, tk=D, name="mm_in")
    attn, lse = _attn_fwd(proj, cos, sin, small["sinks"], name="attn_fwd", jobs=jobs("attn_fwd"))
    out_a = mm(attn, w("w_attn_br"), tk=QD, name="mm_attn_br")
    xbc_act = _conv_fwd(proj, small["conv_w"], small["conv_b"], name="conv_fwd")
    y_pre, states = _ssd_fwd(xbc_act, proj, dtb, alog, dsk, name="ssd_fwd", jobs=jobs("ssd_fwd"))
    yn = _gated_norm_fwd(y_pre, proj, small["g_ssd"], name="gated_norm_fwd")
    out_s = mm(yn, w("w_ssd_br"), tk=DI, name="mm_ssd_br")
    merged = _merge_fwd(proj, out_a, out_s, name="merge_fwd")
    h1 = mm(merged, w("w_o"), add=x, tk=D, name="mm_o")
    f = _rmsnorm_fwd(h1, small["g_ffn"], name="norm_ffn")
    gate, up, act = _swiglu_fwd(f, w("w_gate"), w("w_up"), name="swiglu_fwd", jobs=jobs("swiglu_fwd"))
    h2 = mm(act, w("w_down"), add=h1, tm=t // 2, tk=FFN // 2, name="mm_down")
    e = _rmsnorm_fwd(h2, small["g_ple"], name="norm_ple")
    pgl = mm(e, w("w_ple_gate"), tk=D, name="mm_ple_gate")
    pb = p.astype(BF16)
    pp = mm(pb, w("w_ple_proj"), tk=PLE, name="mm_ple_proj")
    dh3, dpgl, dpp, loss, dg_final = _final(h2, pgl, pp, target, small["g_final"].reshape(1, D), name="final")

    dw("w_ple_proj", pb, dpp, tm=PLE, name="mm_d_ple_proj")
    dw("w_ple_gate", e, dpgl, tm=D, name="mm_d_ple_gate")
    de = mm(dpgl, w("w_ple_gate"), tb=True, tk=D, name="mm_de")
    dh2, dh2b, dg_ple = _rmsnorm_bwd(h2, small["g_ple"], de, dh3, name="norm_ple_bwd", jobs=jobs("norm_ple_bwd"))
    dw("w_down", act, dh2b, tm=FFN // 2, name="mm_d_down")
    dgate, dup = _swiglu_bwd(dh2b, w("w_down"), gate, up, name="swiglu_bwd", jobs=jobs("swiglu_bwd"))
    dw("w_gate", f, dgate, tm=D, name="mm_d_gate")
    dw("w_up", f, dup, tm=D, name="mm_d_up")
    df = mm(dgate, w("w_gate"), tb=True, tn=1024, tk=tkl, name="mm_df_gate")
    df = mm(dup, w("w_up"), tb=True, add=df, tm=t // 2, tk=FFN // 2, name="mm_df_up")
    dh1, dh1b, dg_ffn = _rmsnorm_bwd(h1, small["g_ffn"], df, dh2, name="norm_ffn_bwd", jobs=jobs("norm_ffn_bwd"))
    dw("w_o", merged, dh1b, tm=D, name="mm_d_o")
    dmerged = mm(dh1b, w("w_o"), tb=True, tk=D, name="mm_dmerged")
    dout_a, dout_s, dproj = _merge_bwd(proj, out_a, out_s, dmerged, name="merge_bwd")
    dw("w_attn_br", attn, dout_a, tm=QD, name="mm_d_attn_br")
    dw("w_ssd_br", yn, dout_s, tm=DI, name="mm_d_ssd_br")
    dattn = mm(dout_a, w("w_attn_br"), tb=True, tk=D, name="mm_dattn")
    dyn = mm(dout_s, w("w_ssd_br"), tb=True, tk=D, name="mm_dyn")
    dproj, dsinks = _attn_bwd(proj, cos, sin, small["sinks"], attn, lse, dattn, dproj, name="attn_bwd",
                              jobs=jobs("attn_bwd"))
    dy_pre, dproj, dg_ssd = _gated_norm_bwd(y_pre, proj, small["g_ssd"], dyn, dproj, name="gated_norm_bwd",
                                            jobs=jobs("gated_norm_bwd"))
    dxs, db, dc, dproj, ddtb, dalog, ddsk = _ssd_bwd(xbc_act, proj, dtb, alog, dsk, states, dy_pre, dproj, name="ssd_bwd",
                                                     jobs=jobs("ssd_bwd"))
    dproj, dconv_w, dconv_b = _conv_bwd(proj, small["conv_w"], small["conv_b"], dxs, db, dc, dproj, name="conv_bwd",
                                        jobs=jobs("conv_bwd"))
    for which, h in (("send", 1 - plan.core), ("keep", plan.core)):
        uh = lax.dynamic_slice_in_dim(u, h * (D // 2), D // 2, axis=1)
        name = "mm_d_in_" + which
        plan.g_half("w_in", which, _matmul(uh, dproj, ta=True, out_dtype=BF16, tm=D // 2, tn=1024, tk=t, name=name,
                                           jobs=jobs(name)))
    du = mm(dproj, w("w_in"), tb=True, tn=1024, tk=tkl, name="mm_du")
    grad_x, _, dg_mix = _rmsnorm_bwd(x, small["g_mix"], du, dh1, name="norm_mix_bwd", jobs=jobs("norm_mix_bwd"))

    gs = {
        "g_mix": dg_mix[:1], "conv_w": dconv_w[:CW], "conv_b": dconv_b[:1], "dt_bias": ddtb[:1, :NH],
        "a_log": dalog[:1, :NH], "d_skip": ddsk[:1, :NH], "g_ssd": dg_ssd[:1], "sinks": dsinks[:1, :NQH],
        "g_ffn": dg_ffn[:1], "g_ple": dg_ple[:1], "g_final": dg_final[0],
    }
    return loss, grad_x, gs


def _shard_pieces():
    segs = ((R_Q, QD, O_Q), (R_K, KVD, O_K), (R_V, KVD, O_V), (R_Z, DI, O_Z), (R_XBC, CONV, O_XBC), (R_DT, NH, O_DT),
            (R_GA, D, O_GA), (R_GS, D, O_GS))
    cs = IN_DIM // NCHIP
    out = []
    for j in range(NCHIP):
        for r0, n, k0 in segs:
            lo, hi = max(r0, j * cs), min(r0 + n, (j + 1) * cs)
            if lo < hi:
                out.append((j, lo - j * cs, hi - lo, k0 + lo - r0))
    return out


SLAB = IN_DIM // NCHIP
SLAB_PAD = -(-SLAB // 128) * 128
REMAP_ROWS = 256


def _lane_remap(src, dst_slabs, dst_cols, moves, *, name, add=None, jobs=()):
    s_n, rows, s_cols = src.shape
    assert s_cols % 128 == 0 and dst_cols % 128 == 0 and rows % REMAP_ROWS == 0
    half = REMAP_ROWS // 2

    def body(s_ref, *refs):
        d_ref = refs[-1]
        lane = lax.broadcasted_iota(jnp.int32, (half, 128), 1)
        tiles = {}

        def tile(j, m):
            if (j, m) not in tiles:
                tiles[j, m] = pltpu.bitcast(s_ref[j, :, 128 * m:128 * (m + 1)], jnp.uint32)
            return tiles[j, m]

        def window(j, base):
            m0, s = base // 128, base % 128
            left = tile(j, m0) if 0 <= m0 < s_cols // 128 else None
            if s == 0:
                return left
            right = tile(j, m0 + 1) if 0 <= m0 + 1 < s_cols // 128 else None
            left = None if left is None else pltpu.roll(left, 128 - s, 1)
            right = None if right is None else pltpu.roll(right, 128 - s, 1)
            if left is None or right is None:
                return right if left is None else left
            return jnp.where(lane < 128 - s, left, right)

        for ds in range(dst_slabs):
            for t in range(dst_cols // 128):
                o = 128 * t
                acc = jnp.zeros((half, 128), jnp.uint32)
                for sj, sc, n, dj, dc in moves:
                    lo, hi = max(o, dc) - o, min(o + 128, dc + n) - o
                    if dj != ds or lo >= hi:
                        continue
                    win = window(sj, o - dc + sc)
                    acc = win if (lo, hi) == (0, 128) else jnp.where(jnp.logical_and(lane >= lo, lane < hi), win, acc)
                out = pltpu.bitcast(acc, BF16)
                if add is not None:
                    out = (out.astype(F32) + refs[0][ds, :, o:o + 128].astype(F32)).astype(BF16)
                d_ref[ds, :, o:o + 128] = out

    dst_blk = pl.BlockSpec((dst_slabs, REMAP_ROWS, dst_cols), lambda i: (0, i, 0))
    return _call(
        body, jobs=jobs, name=name, out_shape=jax.ShapeDtypeStruct((dst_slabs, rows, dst_cols), BF16),
        grid=(rows // REMAP_ROWS,),
        in_specs=[pl.BlockSpec((s_n, REMAP_ROWS, s_cols), lambda i: (0, i, 0))] + ([dst_blk] if add is not None else []),
        out_specs=dst_blk, compiler_params=_cp(("parallel",)),
    )(*((src,) if add is None else (src, add)))


def _slabs_to_kernel_cols(slabs, *, name, jobs=()):
    moves = [(j, a, n, 0, k0) for j, a, n, k0 in _shard_pieces()]
    return _lane_remap(slabs, 1, NP, moves, name=name, jobs=jobs)[0]


def _kernel_cols_to_slabs(g, *, name, add=None, jobs=()):
    moves = [(0, k0, n, j, a) for j, a, n, k0 in _shard_pieces()]
    return _lane_remap(g[None], NCHIP, SLAB_PAD, moves, name=name, add=add, jobs=jobs)


MATS = {
    n: (n, kind, 1, r, c, tp, tf) for n, kind, r, c, tp, tf in (
        ("w_in", "stk", 2048, SLAB_PAD, 256, 256),
        ("w_attn_br", "col", 1024, 512, 256, 256),
        ("w_ssd_br", "row", 512, 2048, 512, 256),
        ("w_o", "row", 512, 2048, 512, 256),
        ("w_gate", "col", 2048, 1408, 256, 256),
        ("w_up", "col", 2048, 1408, 256, 256),
        ("w_down", "row", 1408, 2048, 704, 704),
        ("w_ple_gate", "row", 512, 2048, 512, 256),
        ("w_ple_proj", "col", 256, 512, 128, 128),
    )}


def _pos():
    return lax.axis_index("x"), lax.axis_index("y"), lax.axis_index("c")


def _flip(v, a):
    return 1 - v if a else v


def _remote(src, dst, send, recv, dev):
    return pltpu.make_async_remote_copy(src_ref=src, dst_ref=dst, send_sem=send, recv_sem=recv, device_id=dev,
                                        device_id_type=MESH)


def _whole_shape(kind, g, r, c):
    return {"row": (g, NCHIP * r, c), "col": (g, r, NCHIP * c), "stk": (NCHIP, r, c)}[kind]


def _cols(j, c):
    return pl.ds(pl.multiple_of(j * c, 128), c)


def _whole_shard(kind, ref, j, r, c):
    if kind == "row":
        return ref.at[:, pl.ds(j * r, r), :]
    if kind == "col":
        return ref.at[:, :, _cols(j, c)]
    return ref.at[pl.ds(j, 1)]


def _whole_rows(kind, ref, j, row, n, r, c):
    if kind == "row":
        return ref.at[:, pl.ds(j * r + row, n), :]
    if kind == "col":
        return ref.at[:, pl.ds(row, n), _cols(j, c)]
    return ref.at[pl.ds(j, 1), pl.ds(row, n), :]


class _GatherJob(_Job):
    has_mid = True
    NCP = 13

    def __init__(self, names, shards, sink):
        self.mats = [MATS[n] for n in names]
        self.srcs = [shards[n] for n in names]
        self.news = [jax.ShapeDtypeStruct(_whole_shape(kind, g, r, c), BF16) for _, kind, g, r, c, _, _ in self.mats]
        n = len(names)
        self.scratch = [pltpu.SemaphoreType.DMA((self.NCP * n,)), pltpu.SemaphoreType.DMA((self.NCP * n,))]
        self.names, self.sink = names, sink

    def _copies(self, srcs, news, sems):
        send, recv = sems
        x, y, c = _pos()
        me, jx, jy, jd = 2 * x + y, 2 * (1 - x) + y, 2 * x + (1 - y), 2 * (1 - x) + (1 - y)
        nbx, nby, sib = (1 - x, y, c), (x, 1 - y, c), (x, y, 1 - c)
        cps = []
        for w, (_, kind, g, r, cc, _, _) in enumerate(self.mats):
            hr, qr = r // 2, r // 4
            at = lambda j, h, q, n: _whole_rows(kind, news[w], j, h * hr + q * qr, n, r, cc)
            mine = lambda q: srcs[w].at[:, pl.ds(c * hr + q * qr, qr), :]
            cp = lambda k, s, d, dev: _remote(s, d, send.at[self.NCP * w + k], recv.at[self.NCP * w + k], dev)
            cps.append([
                cp(0, mine(0), at(me, c, 0, qr), nbx), cp(1, mine(1), at(me, c, 1, qr), nbx),
                cp(2, mine(1), at(me, c, 1, qr), nby), cp(3, mine(0), at(me, c, 0, qr), nby),
                cp(4, at(jx, c, 0, qr), at(jx, c, 0, qr), nby), cp(5, at(jy, c, 1, qr), at(jy, c, 1, qr), nbx),
                cp(6, at(jx, c, 0, qr), at(jx, c, 0, qr), sib), cp(7, at(jx, c, 1, qr), at(jx, c, 1, qr), sib),
                cp(8, at(jy, c, 1, qr), at(jy, c, 1, qr), sib), cp(9, at(jy, c, 0, qr), at(jy, c, 0, qr), sib),
                cp(10, at(jd, c, 0, qr), at(jd, c, 0, qr), sib), cp(11, at(jd, c, 1, qr), at(jd, c, 1, qr), sib),
                cp(12, srcs[w], _whole_shard(kind, news[w], me, r, cc), sib)])
        return cps

    def _pass_on(self, srcs, news, sems, pairs):
        cps = self._copies(srcs, news, sems)
        for w in range(len(self.mats)):
            for arrived, onward in pairs:
                cps[w][arrived].wait_recv()
                for k in onward:
                    cps[w][k].start()

    def start(self, srcs, dsts, news, sems):
        cps = self._copies(srcs, news, sems)
        for w in range(len(self.mats)):
            for k in (0, 1, 2, 3, 12):
                cps[w][k].start()

    def mid(self, srcs, dsts, news, sems):
        self._pass_on(srcs, news, sems, ((0, (4, 6)), (2, (5, 8))))

    def late(self, srcs, dsts, news, sems):
        self._pass_on(srcs, news, sems, ((1, (7,)), (3, (9,))))

    def finish(self, srcs, dsts, news, sems):
        self._pass_on(srcs, news, sems, ((4, (10,)), (5, (11,))))
        cps = self._copies(srcs, news, sems)
        for w in range(len(self.mats)):
            for k in (6, 7, 8, 9, 10, 11, 12):
                cps[w][k].wait_recv()
            for k in range(self.NCP):
                cps[w][k].wait_send()

    def done(self, dsts, news):
        for n, a in zip(self.names, news):
            self.sink[n] = a


class _SwapJob(_Job):
    def __init__(self, build, ncopies, *, srcs=(), dsts=(), news=(), done=None):
        self.build, self.srcs, self.dsts, self.news, self._done = build, list(srcs), list(dsts), list(news), done
        self.scratch = [pltpu.SemaphoreType.DMA((ncopies,)), pltpu.SemaphoreType.DMA((ncopies,))]

    def start(self, srcs, dsts, news, sems):
        for cp in self.build(srcs, dsts, news, *sems):
            cp.start()

    def finish(self, srcs, dsts, news, sems):
        for cp in self.build(srcs, dsts, news, *sems):
            cp.wait()

    def done(self, dsts, news):
        if self._done is not None:
            self._done(dsts, news)


def _half_of_whole(kind, ref, h, r, c):
    if kind == "row":
        return ref.at[:, :, pl.ds(pl.multiple_of(h * (c // 2), 128), c // 2)]
    return ref.at[:, pl.ds(h * (r // 2), r // 2), :]


def _half_shape(kind, g, r, c):
    return {"row": (g, NCHIP * r, c // 2), "col": (g, r // 2, NCHIP * c), "stk": (NCHIP, r // 2, c)}[kind]


def _sub_shape(kind, r, c):
    return {"row": (1, r // 2, c // 2), "col": (1, r // 4, c), "stk": (1, r // 4, c)}[kind]


def _sub_of_half(kind, ref, j, p, r, c):
    sr = _sub_shape(kind, r, c)[1]
    if kind == "row":
        return ref.at[:, pl.ds(j * r + p * sr, sr), :]
    if kind == "col":
        return ref.at[:, pl.ds(p * sr, sr), _cols(j, c)]
    return ref.at[pl.ds(j, 1), pl.ds(p * sr, sr), :]


def _sub_tile(sr):
    return 256 if sr % 256 == 0 else sr


def _half_of_shard(kind, ref, h, r, c):
    if kind == "row":
        return ref.at[:, :, pl.ds(pl.multiple_of(h * (c // 2), 128), c // 2)]
    return ref.at[:, pl.ds(h * (r // 2), r // 2), :]


def _pair_sum(pack, core, mine, got, whole=True):
    name, kind, g, r, c, tr, _ = pack
    hs = _half_shape(kind, g, r, c)
    nb = hs[1] // tr

    def body(core_ref, a_ref, b_ref, o_ref):
        o_ref[...] = (a_ref[...].astype(F32) + b_ref[...].astype(F32)).astype(BF16)

    blk = (1, tr, hs[2])
    same = lambda gi, i, core_ref: (gi, i, 0)
    if not whole:
        a_map = same
    elif kind == "row":
        a_map = lambda gi, i, core_ref: (gi, i, core_ref[0])
    else:
        a_map = lambda gi, i, core_ref: (gi, core_ref[0] * nb + i, 0)
    return pl.pallas_call(
        body, name="pair_sum_" + name, out_shape=jax.ShapeDtypeStruct(hs, BF16),
        grid_spec=pltpu.PrefetchScalarGridSpec(
            num_scalar_prefetch=1, grid=(hs[0], nb),
            in_specs=[pl.BlockSpec(blk, a_map), pl.BlockSpec(blk, same)], out_specs=pl.BlockSpec(blk, same)),
        compiler_params=_cp(("parallel", "parallel")),
    )(core, mine, got)


def _sub_sums(pack, idx, half, got, *, name):
    _, kind, g, r, c, _, _ = pack
    _, sr, sc = _sub_shape(kind, r, c)
    tr = _sub_tile(sr)
    nb = sr // tr

    def body(idx_ref, a_ref, ga_ref, b_ref, gb_ref, k_ref, p_ref):
        k_ref[0, 0] = a_ref[0].astype(F32) + ga_ref[0, 0].astype(F32)
        p_ref[0, 0] = (b_ref[0].astype(F32) + gb_ref[0, 0].astype(F32)).astype(BF16)

    def sub_map(o):
        if kind == "row":
            return lambda q, i, ix: (0, ix[4 * q + o] * (r // tr) + ix[4 * q + o + 1] * nb + i, 0)
        if kind == "col":
            return lambda q, i, ix: (0, ix[4 * q + o + 1] * nb + i, ix[4 * q + o])
        return lambda q, i, ix: (ix[4 * q + o], ix[4 * q + o + 1] * nb + i, 0)

    sub = lambda o: pl.BlockSpec((1, tr, sc), sub_map(o))
    got_blk = lambda o: pl.BlockSpec((1, 1, tr, sc), lambda q, i, ix: (2 * q + o, 0, i, 0))
    out_blk = pl.BlockSpec((1, 1, tr, sc), lambda q, i, ix: (q, 0, i, 0))
    return pl.pallas_call(
        body, name=name,
        out_shape=(jax.ShapeDtypeStruct((2, 1, sr, sc), F32), jax.ShapeDtypeStruct((2, 1, sr, sc), BF16)),
        grid_spec=pltpu.PrefetchScalarGridSpec(
            num_scalar_prefetch=1, grid=(2, nb), in_specs=[sub(0), got_blk(0), sub(2), got_blk(1)],
            out_specs=(out_blk, out_blk)),
        compiler_params=_cp(("parallel", "parallel")),
    )(idx, half, got, half, got)


def _shard_sum(pack, core, keep, got):
    name, kind, g, r, c, _, _ = pack
    _, sr, sc = _sub_shape(kind, r, c)
    tr = _sub_tile(sr)
    nb = sr // tr

    def body(core_ref, a_ref, b_ref, o_ref):
        o_ref[0] = a_ref[0, 0] + b_ref[0, 0].astype(F32)

    blk = pl.BlockSpec((1, 1, tr, sc), lambda p, i, cr: (p, 0, i, 0))
    if kind == "row":
        o_map = lambda p, i, cr: (0, p * nb + i, cr[0])
    else:
        o_map = lambda p, i, cr: (0, cr[0] * 2 * nb + p * nb + i, 0)
    return pl.pallas_call(
        body, name="shard_sum_" + name, out_shape=jax.ShapeDtypeStruct((g, r, c), F32),
        grid_spec=pltpu.PrefetchScalarGridSpec(
            num_scalar_prefetch=1, grid=(2, nb), in_specs=[blk, blk], out_specs=pl.BlockSpec((1, tr, sc), o_map)),
        compiler_params=_cp(("parallel", "parallel")),
    )(core, keep, got)


class _Plan:
    def __init__(self, shards, table):
        self.shards, self.table = shards, table
        self.whole, self.grad, self.got_a, self.half, self.gshard = {}, {}, {}, {}, {}
        self.got_b1, self.kept, self.pass_on, self.got_b2 = {}, {}, {}, {}
        x, y, c = _pos()
        me, jx, jy = 2 * x + y, 2 * (1 - x) + y, 2 * x + (1 - y)
        self.core = c
        self.core1 = c.reshape(1).astype(jnp.int32)
        zero = 0 * me
        self.idx_sums = jnp.stack([me, zero, jy, zero, me, zero + 1, jx, zero + 1]).astype(jnp.int32)
        self._w_in = None
        self.send, self.keep = {}, {}

    def w(self, n):
        if n != "w_in":
            return self.whole[n][0]
        if self._w_in is None:
            self._w_in = _slabs_to_kernel_cols(self.whole[n], name="relayout_w_in", jobs=self.jobs("relayout_w_in"))
        return self._w_in

    def g(self, n, a):
        self.grad[n] = a[None]

    def g_half(self, n, which, a):
        if which == "keep" and n in self.got_a:
            self.half[n] = _kernel_cols_to_slabs(a, name="relayout_d_in_keep", add=self.got_a[n])
        else:
            (self.send if which == "send" else self.keep)[n] = _kernel_cols_to_slabs(a, name="relayout_d_in_" + which)

    def jobs(self, tag):
        out = []
        for spec in self.table.get(tag, ()):
            out += getattr(self, "_" + spec[0])(*spec[1:])
        return out

    def run(self, name, jobs):
        if jobs:
            _call(lambda: None, jobs=jobs, name=name, out_shape=[], in_specs=[], out_specs=[])()

    def _gather(self, names):
        return [_GatherJob(names, self.shards, self.whole)]

    def _rs_a(self, names):
        mats = [MATS[n] for n in names]

        def build(srcs, dsts, news, send, recv):
            x, y, c = _pos()
            return [_remote(srcs[i] if names[i] in self.send else _half_of_whole(kind, srcs[i], 1 - c, r, cc), news[i],
                            send.at[i], recv.at[i], (x, y, 1 - c))
                    for i, (_, kind, g, r, cc, _, _) in enumerate(mats)]

        def done(dsts, news):
            self.got_a.update(zip(names, news))

        return [_SwapJob(build, len(names), srcs=[self.send.get(n, self.grad.get(n)) for n in names], done=done,
                         news=[jax.ShapeDtypeStruct(_half_shape(kind, g, r, c), BF16) for _, kind, g, r, c, _, _ in mats])]

    def _rs_b1(self, names):
        mats = [MATS[n] for n in names]
        for n in names:
            if n in self.half:
                continue
            if n in self.keep:
                self.half[n] = _pair_sum(MATS[n], self.core1, self.keep[n], self.got_a[n], whole=False)
            else:
                self.half[n] = _pair_sum(MATS[n], self.core1, self.grad[n], self.got_a[n])

        def build(srcs, dsts, news, send, recv):
            x, y, c = _pos()
            jx, jy, jd = 2 * (1 - x) + y, 2 * x + (1 - y), 2 * (1 - x) + (1 - y)
            nbx, nby = (1 - x, y, c), (x, 1 - y, c)
            cps = []
            for i, (_, kind, g, r, cc, _, _) in enumerate(mats):
                sub = lambda j, p: _sub_of_half(kind, srcs[i], j, p, r, cc)
                for k, (j, p, dev) in enumerate(((jx, 0, nbx), (jd, 0, nbx), (jy, 1, nby), (jd, 1, nby))):
                    cps.append(_remote(sub(j, p), news[i].at[k], send.at[4 * i + k], recv.at[4 * i + k], dev))
            return cps

        def done(dsts, news):
            self.got_b1.update(zip(names, news))

        return [_SwapJob(build, 4 * len(names), srcs=[self.half[n] for n in names], done=done,
                         news=[jax.ShapeDtypeStruct((4,) + _sub_shape(kind, r, c), BF16) for _, kind, g, r, c, _, _ in mats])]

    def _rs_b2(self, names):
        mats = [MATS[n] for n in names]
        for n in names:
            self.kept[n], self.pass_on[n] = _sub_sums(MATS[n], self.idx_sums, self.half[n], self.got_b1[n], name="sums_" + n)

        def build(srcs, dsts, news, send, recv):
            x, y, c = _pos()
            cps = []
            for i in range(len(mats)):
                cps.append(_remote(srcs[i].at[0], news[i].at[0], send.at[2 * i], recv.at[2 * i], (x, 1 - y, c)))
                cps.append(_remote(srcs[i].at[1], news[i].at[1], send.at[2 * i + 1], recv.at[2 * i + 1], (1 - x, y, c)))
            return cps

        def done(dsts, news):
            self.got_b2.update(zip(names, news))

        return [_SwapJob(build, 2 * len(names), srcs=[self.pass_on[n] for n in names], done=done,
                         news=[jax.ShapeDtypeStruct((2,) + _sub_shape(kind, r, c), BF16) for _, kind, g, r, c, _, _ in mats])]

    def _rs_c(self, names):
        mats = [MATS[n] for n in names]
        parts = [_shard_sum(MATS[n], self.core1, self.kept[n], self.got_b2[n]) for n in names]

        def build(srcs, dsts, news, send, recv):
            x, y, c = _pos()
            cps = []
            for i, (_, kind, g, r, cc, _, _) in enumerate(mats):
                mine = _half_of_shard(kind, dsts[i], c, r, cc)
                cps.append(_remote(mine, mine, send.at[i], recv.at[i], (x, y, 1 - c)))
            return cps

        def done(dsts, news):
            self.gshard.update(zip(names, dsts))

        return [_SwapJob(build, len(names), dsts=parts, done=done)]

    def finish(self, n):
        if n not in self.got_a:
            self.run("rs_a_" + n, self._rs_a((n,)))
        if n not in self.got_b1:
            self.run("rs_b1_" + n, self._rs_b1((n,)))
        if n not in self.got_b2:
            self.run("rs_b2_" + n, self._rs_b2((n,)))
        if n not in self.gshard:
            self.run("rs_c_" + n, self._rs_c((n,)))
        return self.gshard[n]


TABLE = {
    "gather_conv_w": (("gather", ("w_in",)),),
    "relayout_w_in": (("gather", ("w_gate",)),),
    "mm_in": (("gather", ("w_up",)),),
    "attn_fwd": (("gather", ("w_attn_br", "w_ssd_br")),),
    "ssd_fwd": (("gather", ("w_o",)),),
    "swiglu_fwd": (("gather", ("w_down",)),),
    "mm_down": (("gather", ("w_ple_gate", "w_ple_proj")),),
    "mm_de": (("rs_a", ("w_ple_proj", "w_ple_gate")),),
    "mm_d_down": (("rs_b1", ("w_ple_proj", "w_ple_gate")),),
    "swiglu_bwd": (("rs_a", ("w_down",)), ("rs_b2", ("w_ple_proj", "w_ple_gate"))),
    "mm_d_gate": (("rs_b1", ("w_down",)),),
    "mm_d_up": (("rs_b2", ("w_down",)), ("rs_c", ("w_ple_proj", "w_ple_gate")), ("rs_a", ("w_gate",))),
    "mm_df_gate": (("rs_b1", ("w_gate",)), ("rs_a", ("w_up",)), ("rs_c", ("w_down",))),
    "mm_df_up": (("rs_b2", ("w_gate",)),),
    "norm_ffn_bwd": (("rs_c", ("w_gate",)),),
    "mm_dmerged": (("rs_a", ("w_o",)),),
    "mm_dyn": (("rs_a", ("w_attn_br", "w_ssd_br")),),
    "attn_bwd": (("rs_b1", ("w_up",)),),
    "gated_norm_bwd": (("rs_b2", ("w_up",)),),
    "ssd_bwd": (("rs_b1", ("w_o", "w_attn_br", "w_ssd_br")), ("rs_c", ("w_up",))),
    "conv_bwd": (("rs_b2", ("w_o", "w_attn_br", "w_ssd_br")),),
    "mm_d_in_send": (("rs_c", ("w_o", "w_attn_br", "w_ssd_br")),),
    "mm_d_in_keep": (("rs_a", ("w_in",)),),
    "mm_du": (("rs_b1", ("w_in",)),),
    "norm_mix_bwd": (("rs_b2", ("w_in",)),),
    "sum_small": (("rs_c", ("w_in",)),),
}


NDEV = 8


def _allreduce_small(v, *, name, jobs=()):
    rows = v.shape[0]

    def body(v_ref, o_ref, slots, send, recv):
        x, y, c = _pos()
        me = 4 * x + 2 * y + c
        slots[me] = v_ref[...]
        cps = []
        for k in range(1, NDEV):
            peer = (_flip(x, k & 4), _flip(y, k & 2), _flip(c, k & 1))
            cp = _remote(v_ref, slots.at[me], send.at[k - 1], recv.at[k - 1], peer)
            cp.start()
            cps.append(cp)
        for cp in cps:
            cp.wait()
        acc = slots[0]
        for s in range(1, NDEV):
            acc = acc + slots[s]
        o_ref[...] = acc

    return _call(
        body, jobs=jobs, name=name, out_shape=jax.ShapeDtypeStruct((rows, 128), F32),
        in_specs=[pl.BlockSpec(memory_space=pltpu.VMEM)], out_specs=pl.BlockSpec(memory_space=pltpu.VMEM),
        scratch_shapes=[pltpu.VMEM((NDEV, rows, 128), F32), pltpu.SemaphoreType.DMA((NDEV - 1,)),
                        pltpu.SemaphoreType.DMA((NDEV - 1,))],
    )(v)


def _adamw(w, g, m, v, *, name, tr=None, tc=None, jobs=()):
    r, c = w.shape
    tr = r if tr is None else tr
    c1 = 1.0 / (1.0 - B1 ** STEP)
    c2 = 1.0 / (1.0 - B2 ** STEP)

    def body(w_ref, g_ref, m_ref, v_ref, d_ref, mo_ref, vo_ref):
        gv = g_ref[...]
        mn = B1 * m_ref[...] + (1.0 - B1) * gv
        vn = B2 * v_ref[...] + (1.0 - B2) * (gv * gv)
        mo_ref[...] = mn
        vo_ref[...] = vn
        d_ref[...] = -LR * ((mn * c1) / (jnp.sqrt(vn * c2) + AEPS) + WD * w_ref[...])

    if tc is None:
        blk, grid = pl.BlockSpec((tr, c), lambda i: (i, 0)), (r // tr,)
    else:
        blk, grid = pl.BlockSpec((r, tc), lambda i: (0, i)), (c // tc,)
    o = jax.ShapeDtypeStruct((r, c), F32)
    return _call(
        body, jobs=jobs, name=name, out_shape=(o, o, o), grid=grid, in_specs=[blk] * 4, out_specs=(blk, blk, blk),
        compiler_params=_cp(("parallel",)),
    )(w, g, m, v)


WEIGHTS = ("g_mix", "w_in", "conv_w", "conv_b", "dt_bias", "a_log", "d_skip", "g_ssd", "sinks", "w_attn_br", "w_ssd_br",
           "w_o", "g_ffn", "w_gate", "w_up", "w_down", "g_ple", "w_ple_gate", "w_ple_proj", "g_final")
BIG = {
    "w_gate": 256, "w_up": 256, "w_down": 128, "w_ssd_br": 128, "w_o": 128, "w_ple_gate": 128, "w_attn_br": 256,
    "w_ple_proj": 256, "w_in": None,
}
SMALL = tuple(n for n in WEIGHTS if n not in BIG)


def _pack_small(parts):
    rows = []
    for a in parts:
        a = a.reshape(-1)
        rows.append(jnp.pad(a, (0, -a.shape[0] % 128)).reshape(-1, 128))
    out = jnp.concatenate(rows, axis=0)
    return jnp.pad(out, ((0, -out.shape[0] % 8), (0, 0)))


def _unpack_small(packed, shapes):
    out, r = [], 0
    for s in shapes:
        n = int(np.prod(s))
        nr = -(-n // 128)
        out.append(packed[r:r + nr].reshape(-1)[:n].reshape(s))
        r += nr
    return out


def kernel(x, p, positions, g_mix, w_in, conv_w, conv_b, dt_bias, a_log, d_skip, g_ssd, sinks, w_attn_br, w_ssd_br, w_o, g_ffn, w_gate, w_up, w_down, g_ple, w_ple_gate, w_ple_proj, g_final, loss_target, m_g_mix, m_w_in, m_conv_w, m_conv_b, m_dt_bias, m_a_log, m_d_skip, m_g_ssd, m_sinks, m_w_attn_br, m_w_ssd_br, m_w_o, m_g_ffn, m_w_gate, m_w_up, m_w_down, m_g_ple, m_w_ple_gate, m_w_ple_proj, m_g_final, v_g_mix, v_w_in, v_conv_w, v_conv_b, v_dt_bias, v_a_log, v_d_skip, v_g_ssd, v_sinks, v_w_attn_br, v_w_ssd_br, v_w_o, v_g_ffn, v_w_gate, v_w_up, v_w_down, v_g_ple, v_w_ple_gate, v_w_ple_proj, v_g_final):
    w = dict(zip(WEIGHTS, (g_mix, w_in, conv_w, conv_b, dt_bias, a_log, d_skip, g_ssd, sinks, w_attn_br, w_ssd_br, w_o,
                           g_ffn, w_gate, w_up, w_down, g_ple, w_ple_gate, w_ple_proj, g_final)))
    m = dict(zip(WEIGHTS, (m_g_mix, m_w_in, m_conv_w, m_conv_b, m_dt_bias, m_a_log, m_d_skip, m_g_ssd, m_sinks, m_w_attn_br,
                           m_w_ssd_br, m_w_o, m_g_ffn, m_w_gate, m_w_up, m_w_down, m_g_ple, m_w_ple_gate, m_w_ple_proj,
                           m_g_final)))
    v = dict(zip(WEIGHTS, (v_g_mix, v_w_in, v_conv_w, v_conv_b, v_dt_bias, v_a_log, v_d_skip, v_g_ssd, v_sinks, v_w_attn_br,
                           v_w_ssd_br, v_w_o, v_g_ffn, v_w_gate, v_w_up, v_w_down, v_g_ple, v_w_ple_gate, v_w_ple_proj,
                           v_g_final)))
    xi, yi, ci = _pos()
    chip = 2 * xi + yi
    t = x.shape[1]
    cshard = CONV // NCHIP

    shards = {n: w[n].astype(BF16) for n in MATS}
    shards["w_in"] = jnp.pad(shards["w_in"], ((0, 0), (0, 0), (0, SLAB_PAD - SLAB)))
    plan = _Plan(shards, TABLE)
    placed = lax.dynamic_update_slice(jnp.zeros((CW, CONV), F32), w["conv_w"][0], (0, chip * cshard))
    conv_whole = _allreduce_small(jnp.where(ci == 0, placed, 0.0).reshape(-1, 128), name="gather_conv_w",
                                  jobs=plan.jobs("gather_conv_w")).reshape(CW, CONV)

    small = {n: w[n] for n in ("g_mix", "conv_b", "dt_bias", "a_log", "d_skip", "g_ssd", "sinks", "g_ffn", "g_ple", "g_final")}
    small["conv_w"] = conv_whole
    loss8, grad_x, gs = _local_step(x[0], p[0, 0], positions, loss_target[0], small, plan)

    order = ("g_mix", "conv_b", "dt_bias", "a_log", "d_skip", "g_ssd", "sinks", "g_ffn", "g_ple", "g_final", "conv_w")
    summed = _allreduce_small(_pack_small([loss8[0, :1]] + [gs[n] for n in order]), name="sum_small",
                              jobs=plan.jobs("sum_small"))
    parts = _unpack_small(summed, [(1,)] + [w[n].shape for n in order[:-1]] + [(CW, CONV)])
    loss = parts[0][0]
    grad = dict(zip(order, parts[1:]))
    grad["conv_w"] = lax.dynamic_slice(grad["conv_w"], (0, chip * cshard), (CW, cshard))[None]

    delta, new_m, new_v = {}, {}, {}
    for n, tr in BIG.items():
        grad[n] = plan.finish(n)[:, :, :w[n].shape[2]]
        if n == "w_in":
            d_, m_, v_ = _adamw(w[n][0].T, grad[n][0].T, m[n][0].T, v[n][0].T, tc=128, name="adamw_" + n)
            d_, m_, v_ = d_.T, m_.T, v_.T
        else:
            d_, m_, v_ = _adamw(w[n][0], grad[n][0], m[n][0], v[n][0], tr=tr, name="adamw_" + n)
        delta[n], new_m[n], new_v[n] = d_[None], m_[None], v_[None]
    shapes = [w[n].shape for n in SMALL]
    d_, m_, v_ = _adamw(_pack_small([w[n] for n in SMALL]), _pack_small([grad[n] for n in SMALL]),
                        _pack_small([m[n] for n in SMALL]), _pack_small([v[n] for n in SMALL]), tr=None, name="adamw_small")
    for n, a, b, c_ in zip(SMALL, _unpack_small(d_, shapes), _unpack_small(m_, shapes), _unpack_small(v_, shapes)):
        delta[n], new_m[n], new_v[n] = a, b, c_

    return (loss, grad_x[None], *[grad[n] for n in WEIGHTS], *[delta[n] for n in WEIGHTS],
            *[new_m[n] for n in WEIGHTS], *[new_v[n] for n in WEIGHTS])
```

```python
import functools

import jax
import jax.numpy as jnp
import numpy as np
from jax import lax
from jax.experimental import pallas as pl
from jax.experimental.pallas import tpu as pltpu

F32 = jnp.float32
BF16 = jnp.bfloat16
MESH = pl.DeviceIdType.MESH

D = 2048
HD = 64
NQH = 16
NKV = 4
QD = NQH * HD
KVD = NKV * HD
DI = 2048
NH = 32
NG = 4
NS = 128
CW = 4
L = 128
CONV = DI + 2 * NG * NS
FFN = 5632
PLE = 256
IN_DIM = QD + 2 * KVD + DI + CONV + NH + 2 * D
EPS = 1e-6
SSM_EPS = 1e-5
ROPE_THETA = 10000.0
LR, B1, B2, AEPS, WD, STEP = 0.001, 0.9, 0.999, 1e-08, 0.01, 10

O_GA, O_GS, O_Z, O_XBC, O_Q, O_K, O_V, O_DT = 0, 2048, 4096, 6144, 9216, 10240, 10496, 10752
DT_PAD = 512
NP = O_DT + DT_PAD
R_Q, R_K, R_V, R_Z, R_XBC, R_DT, R_GA, R_GS = 0, 1024, 1280, 1536, 3584, 6656, 6688, 8736

NCHIP = 4
VMEM_LIMIT = 52 * 1024 * 1024
NEG = -1e30


def _cp(sem=None):
    return pltpu.CompilerParams(dimension_semantics=sem, vmem_limit_bytes=VMEM_LIMIT)


def _dot(a, b):
    return lax.dot_general(a, b, (((1,), (0,)), ((), ())), preferred_element_type=F32)


def _dot_nt(a, b):
    return lax.dot_general(a, b, (((1,), (1,)), ((), ())), preferred_element_type=F32)


def _dot_tn(a, b):
    return lax.dot_general(a, b, (((0,), (0,)), ((), ())), preferred_element_type=F32)


def _sigmoid(x):
    return 1.0 / (1.0 + jnp.exp(-x))


def _bf16_dot(dot, da, db):
    @jax.custom_vjp
    def f(a, b):
        return dot(a.astype(BF16), b.astype(BF16))

    def fwd(a, b):
        return f(a, b), (a.astype(BF16), b.astype(BF16))

    def bwd(res, g):
        a, b = res
        g = g.astype(BF16)
        return da(g, a, b), db(g, a, b)

    f.defvjp(fwd, bwd)
    return f


_bdot = _bf16_dot(_dot, lambda g, a, b: _dot_nt(g, b), lambda g, a, b: _dot_tn(a, g))
_bdot_nt = _bf16_dot(_dot_nt, lambda g, a, b: _dot(g, b), lambda g, a, b: _dot_tn(g, a))
_bdot_tn = _bf16_dot(_dot_tn, lambda g, a, b: _dot_nt(b, g), lambda g, a, b: _dot(a, g))


ANY = pl.BlockSpec(memory_space=pl.ANY)


class _Job:
    srcs, dsts, news, scratch = (), (), (), ()
    has_mid = False

    def start(self, srcs, dsts, news, sems):
        raise NotImplementedError

    def mid(self, srcs, dsts, news, sems):
        pass

    def late(self, srcs, dsts, news, sems):
        pass

    def finish(self, srcs, dsts, news, sems):
        raise NotImplementedError

    def done(self, dsts, news):
        pass


def _call(body, *, jobs=(), name, out_shape, in_specs, out_specs, grid=(), scratch_shapes=(), compiler_params=None,
          aliases=None):
    jobs = [j for j in jobs if j is not None]
    aliases = dict(aliases or {})
    if not jobs:
        return pl.pallas_call(body, name=name, out_shape=out_shape, in_specs=in_specs, out_specs=out_specs, grid=grid,
                              scratch_shapes=scratch_shapes, compiler_params=compiler_params,
                              input_output_aliases=aliases)
    single = not isinstance(out_shape, (tuple, list))
    outs = [out_shape] if single else list(out_shape)
    ospecs = [out_specs] if single else list(out_specs)
    n_in, n_out, n_scr = len(in_specs), len(outs), len(scratch_shapes)
    srcs = [a for j in jobs for a in j.srcs]
    dsts = [a for j in jobs for a in j.dsts]
    news = [a for j in jobs for a in j.news]
    sems = [a for j in jobs for a in j.scratch]

    def wrapped(*refs):
        pos = n_in + len(srcs) + len(dsts)
        ins, jsrc = refs[:n_in], refs[n_in:n_in + len(srcs)]
        o_refs = refs[pos:pos + n_out]
        pos += n_out
        jdst, jnew = refs[pos:pos + len(dsts)], refs[pos + len(dsts):pos + len(dsts) + len(news)]
        pos += len(dsts) + len(news)
        scr, jsem = refs[pos:pos + n_scr], refs[pos + n_scr:]

        def run(which):
            a = b = c = d = 0
            for j in jobs:
                getattr(j, which)(jsrc[a:a + len(j.srcs)], jdst[b:b + len(j.dsts)], jnew[c:c + len(j.news)],
                                  jsem[d:d + len(j.scratch)])
                a, b, c, d = a + len(j.srcs), b + len(j.dsts), c + len(j.news), d + len(j.scratch)

        if not grid:
            run("start")
            run("mid")
            run("late")
            body(*ins, *o_refs, *scr)
            run("finish")
            return
        step = functools.reduce(lambda acc, a: acc * grid[a] + pl.program_id(a), range(len(grid)), 0)
        steps = int(np.prod(grid))
        pl.when(step == 0)(lambda: run("start"))
        if any(j.has_mid for j in jobs):
            pl.when(step == steps // 3)(lambda: run("mid"))
            pl.when(step == (2 * steps) // 3)(lambda: run("late"))
        body(*ins, *o_refs, *scr)
        pl.when(step == steps - 1)(lambda: run("finish"))

    call = pl.pallas_call(
        wrapped, name=name,
        out_shape=outs + [jax.ShapeDtypeStruct(a.shape, a.dtype) for a in dsts] + news,
        in_specs=list(in_specs) + [ANY] * (len(srcs) + len(dsts)),
        out_specs=ospecs + [ANY] * (len(dsts) + len(news)),
        grid=grid, scratch_shapes=list(scratch_shapes) + sems,
        input_output_aliases={**aliases, **{n_in + len(srcs) + i: n_out + i for i in range(len(dsts))}},
        compiler_params=_cp(("arbitrary",) * len(grid) if grid else None))

    def run_call(*args):
        res = call(*args, *srcs, *dsts)
        b, c = n_out, n_out + len(dsts)
        for j in jobs:
            j.done(res[b:b + len(j.dsts)], res[c:c + len(j.news)])
            b, c = b + len(j.dsts), c + len(j.news)
        return res[0] if single else tuple(res[:n_out])

    return run_call


def _matmul(a, b, *, ta=False, tb=False, out_dtype=F32, add=None, tm, tn, tk, name, jobs=()):
    k, m = a.shape if ta else a.shape[::-1]
    n = b.shape[0] if tb else b.shape[1]
    assert (b.shape[1] if tb else b.shape[0]) == k and not (ta and tb)
    assert m % tm == 0 and n % tn == 0 and k % tk == 0, (name, a.shape, b.shape)
    nk = k // tk
    has_add = add is not None

    def body(*refs):
        a_ref, b_ref = refs[0], refs[1]
        add_ref = refs[2] if has_add else None
        o_ref = refs[3] if has_add else refs[2]
        av = a_ref[...].astype(BF16)
        bv = b_ref[...].astype(BF16)
        part = _dot_tn(av, bv) if ta else _dot_nt(av, bv) if tb else _dot(av, bv)

        def finish(r):
            if has_add:
                r = r + add_ref[...]
            o_ref[...] = r.astype(out_dtype)

        if nk == 1:
            finish(part)
        elif out_dtype == F32:
            kk = pl.program_id(2)
            pl.when(kk == 0)(lambda: finish(part))

            @pl.when(kk > 0)
            def _():
                o_ref[...] += part
        else:
            acc_ref = refs[-1]
            kk = pl.program_id(2)

            @pl.when(kk == 0)
            def _():
                acc_ref[...] = part

            @pl.when(kk > 0)
            def _():
                acc_ref[...] += part

            @pl.when(kk == nk - 1)
            def _():
                finish(acc_ref[...])

    in_specs = [pl.BlockSpec((tk, tm), lambda i, j, kk: (kk, i)) if ta else pl.BlockSpec((tm, tk), lambda i, j, kk: (i, kk)),
                pl.BlockSpec((tn, tk), lambda i, j, kk: (j, kk)) if tb
                else pl.BlockSpec((tk, tn), lambda i, j, kk: (kk, j))]
    args = [a, b]
    if has_add:
        in_specs.append(pl.BlockSpec((tm, tn), lambda i, j, kk: (i, j)))
        args.append(add)
    return _call(
        body, jobs=jobs, name=name,
        out_shape=jax.ShapeDtypeStruct((m, n), out_dtype),
        grid=(m // tm, n // tn, nk),
        in_specs=in_specs,
        out_specs=pl.BlockSpec((tm, tn), lambda i, j, kk: (i, j)),
        scratch_shapes=[pltpu.VMEM((tm, tn), F32)] if nk > 1 and out_dtype != F32 else [],
        compiler_params=_cp(("parallel", "parallel", "arbitrary")),
    )(*args)


ROWS = 256


def _rmsnorm_fwd(x, g, *, name):
    t, d = x.shape

    def body(x_ref, g_ref, o_ref):
        xv = x_ref[...]
        r = lax.rsqrt(jnp.mean(xv * xv, axis=-1, keepdims=True) + EPS)
        o_ref[...] = (xv * r * g_ref[...]).astype(BF16)

    return pl.pallas_call(
        body, name=name, out_shape=jax.ShapeDtypeStruct((t, d), BF16), grid=(t // ROWS,),
        in_specs=[pl.BlockSpec((ROWS, d), lambda i: (i, 0)), pl.BlockSpec((1, d), lambda i: (0, 0))],
        out_specs=pl.BlockSpec((ROWS, d), lambda i: (i, 0)), compiler_params=_cp(("parallel",)),
    )(x, g)


def _rmsnorm_bwd(x, g, dy, dres, *, name, jobs=()):
    t, d = x.shape

    def body(x_ref, g_ref, dy_ref, dres_ref, dx_ref, dxb_ref, dg_ref):
        xv = x_ref[...]
        r = lax.rsqrt(jnp.mean(xv * xv, axis=-1, keepdims=True) + EPS)
        xh = xv * r
        dyv = dy_ref[...]
        dxh = dyv * g_ref[...]
        dx = r * (dxh - xh * jnp.mean(dxh * xh, axis=-1, keepdims=True))
        tot = dres_ref[...] + dx
        dx_ref[...] = tot
        dxb_ref[...] = tot.astype(BF16)

        @pl.when(pl.program_id(0) == 0)
        def _():
            dg_ref[...] = jnp.zeros_like(dg_ref)

        dg_ref[...] += jnp.broadcast_to(jnp.sum(dyv * xh, axis=0, keepdims=True), dg_ref.shape)

    row = pl.BlockSpec((ROWS, d), lambda i: (i, 0))
    return _call(
        body, jobs=jobs, name=name,
        out_shape=(jax.ShapeDtypeStruct((t, d), F32), jax.ShapeDtypeStruct((t, d), BF16),
                   jax.ShapeDtypeStruct((8, d), F32)),
        grid=(t // ROWS,),
        in_specs=[row, pl.BlockSpec((1, d), lambda i: (0, 0)), row, row],
        out_specs=(row, row, pl.BlockSpec((8, d), lambda i: (0, 0))),
        compiler_params=_cp(("arbitrary",)),
    )(x, g, dy, dres)


def _final(h2, pgl, pp, target, g_final, *, name):
    t, d = h2.shape

    def body(h2_ref, pgl_ref, pp_ref, tg_ref, g_ref, dh3_ref, dpgl_ref, dpp_ref, loss_ref, dg_ref):
        s = _sigmoid(pgl_ref[...])
        ppv = pp_ref[...]
        h3 = h2_ref[...] + s * ppv
        r = lax.rsqrt(jnp.mean(h3 * h3, axis=-1, keepdims=True) + EPS)
        xh = h3 * r
        gv = g_ref[...]
        err = xh * gv - tg_ref[...]
        dyv = err * (1.0 / d)
        dxh = dyv * gv
        dh3 = r * (dxh - xh * jnp.mean(dxh * xh, axis=-1, keepdims=True))
        dh3_ref[...] = dh3
        dpp_ref[...] = (dh3 * s).astype(BF16)
        dpgl_ref[...] = (dh3 * ppv * s * (1.0 - s)).astype(BF16)

        @pl.when(pl.program_id(0) == 0)
        def _():
            loss_ref[...] = jnp.zeros_like(loss_ref)
            dg_ref[...] = jnp.zeros_like(dg_ref)

        part = 0.5 * jnp.sum(jnp.mean(err * err, axis=-1, keepdims=True), axis=0, keepdims=True)
        loss_ref[...] += jnp.broadcast_to(part, loss_ref.shape)
        dg_ref[...] += jnp.broadcast_to(jnp.sum(dyv * xh, axis=0, keepdims=True), dg_ref.shape)

    row = pl.BlockSpec((ROWS, d), lambda i: (i, 0))
    return pl.pallas_call(
        body, name=name,
        out_shape=(jax.ShapeDtypeStruct((t, d), F32), jax.ShapeDtypeStruct((t, d), BF16),
                   jax.ShapeDtypeStruct((t, d), BF16), jax.ShapeDtypeStruct((8, 128), F32),
                   jax.ShapeDtypeStruct((8, d), F32)),
        grid=(t // ROWS,),
        in_specs=[row, row, row, row, pl.BlockSpec((1, d), lambda i: (0, 0))],
        out_specs=(row, row, row, pl.BlockSpec((8, 128), lambda i: (0, 0)), pl.BlockSpec((8, d), lambda i: (0, 0))),
        compiler_params=_cp(("arbitrary",)),
    )(h2, pgl, pp, target, g_final)


def _merge_fwd(proj, out_a, out_s, *, name):
    t = proj.shape[0]

    def body(ga_ref, gs_ref, a_ref, s_ref, o_ref):
        o_ref[...] = (_sigmoid(ga_ref[...]) * a_ref[...] + _sigmoid(gs_ref[...]) * s_ref[...]).astype(BF16)

    row = pl.BlockSpec((ROWS, D), lambda i: (i, 0))
    return pl.pallas_call(
        body, name=name, out_shape=jax.ShapeDtypeStruct((t, D), BF16), grid=(t // ROWS,),
        in_specs=[pl.BlockSpec((ROWS, D), lambda i: (i, O_GA // D)), pl.BlockSpec((ROWS, D), lambda i: (i, O_GS // D)),
                  row, row],
        out_specs=row, compiler_params=_cp(("parallel",)),
    )(proj, proj, out_a, out_s)


def _merge_bwd(proj, out_a, out_s, dmerged, *, name):
    t = proj.shape[0]
    assert O_GA == 0 and O_GS == D

    def body(ga_ref, gs_ref, a_ref, s_ref, dm_ref, da_ref, ds_ref, dp_ref):
        sa = _sigmoid(ga_ref[...])
        ss = _sigmoid(gs_ref[...])
        dm = dm_ref[...]
        da_ref[...] = (dm * sa).astype(BF16)
        ds_ref[...] = (dm * ss).astype(BF16)
        dp_ref[:, :D] = (dm * a_ref[...] * sa * (1.0 - sa)).astype(BF16)
        dp_ref[:, D:] = (dm * s_ref[...] * ss * (1.0 - ss)).astype(BF16)

    row = pl.BlockSpec((ROWS, D), lambda i: (i, 0))
    o = jax.ShapeDtypeStruct((t, D), BF16)
    return pl.pallas_call(
        body, name=name, out_shape=(o, o, jax.ShapeDtypeStruct((t, NP), BF16)), grid=(t // ROWS,),
        in_specs=[pl.BlockSpec((ROWS, D), lambda i: (i, O_GA // D)), pl.BlockSpec((ROWS, D), lambda i: (i, O_GS // D)),
                  row, row, row],
        out_specs=(row, row, pl.BlockSpec((ROWS, 2 * D), lambda i: (i, 0))), compiler_params=_cp(("parallel",)),
    )(proj, proj, out_a, out_s, dmerged)


def _swiglu_fwd(f, w_gate, w_up, *, name, tn=512, jobs=()):
    t, d = f.shape
    n = w_gate.shape[1]

    def body(f_ref, wg_ref, wu_ref, g_ref, u_ref, a_ref):
        fv = f_ref[...]
        g = _dot(fv, wg_ref[...])
        u = _dot(fv, wu_ref[...])
        g_ref[...] = g.astype(BF16)
        u_ref[...] = u.astype(BF16)
        a_ref[...] = (g * _sigmoid(g) * u).astype(BF16)

    col = pl.BlockSpec((t, tn), lambda j: (0, j))
    wcol = pl.BlockSpec((d, tn), lambda j: (0, j))
    return _call(
        body, jobs=jobs, name=name,
        out_shape=(jax.ShapeDtypeStruct((t, n), BF16), jax.ShapeDtypeStruct((t, n), BF16),
                   jax.ShapeDtypeStruct((t, n), BF16)),
        grid=(n // tn,),
        in_specs=[pl.BlockSpec((t, d), lambda j: (0, 0)), wcol, wcol],
        out_specs=(col, col, col), compiler_params=_cp(("parallel",)),
    )(f, w_gate, w_up)


def _swiglu_bwd(dh, w_down, gate, up, *, name, tn=512, jobs=()):
    t, d = dh.shape
    n = w_down.shape[0]

    def body(dh_ref, w_ref, g_ref, u_ref, dg_ref, du_ref):
        da = _dot_nt(dh_ref[...], w_ref[...])
        g = g_ref[...].astype(F32)
        s = _sigmoid(g)
        du_ref[...] = (da * g * s).astype(BF16)
        dg_ref[...] = (da * u_ref[...].astype(F32) * s * (1.0 + g * (1.0 - s))).astype(BF16)

    col = pl.BlockSpec((t, tn), lambda j: (0, j))
    o = jax.ShapeDtypeStruct((t, n), BF16)
    return _call(
        body, jobs=jobs, name=name, out_shape=(o, o), grid=(n // tn,),
        in_specs=[pl.BlockSpec((t, d), lambda j: (0, 0)), pl.BlockSpec((tn, d), lambda j: (j, 0)), col, col],
        out_specs=(col, col), compiler_params=_cp(("parallel",)),
    )(dh, w_down, gate, up)


def _gated_norm_fwd(y_pre, proj, g_ssd, *, name):
    t = y_pre.shape[0]

    def body(y_ref, z_ref, g_ref, o_ref):
        z = z_ref[...]
        v = y_ref[...] * z * _sigmoid(z)
        r = lax.rsqrt(jnp.mean(v * v, axis=-1, keepdims=True) + SSM_EPS)
        o_ref[...] = (v * r * g_ref[...]).astype(BF16)

    row = pl.BlockSpec((ROWS, DI), lambda i: (i, 0))
    return pl.pallas_call(
        body, name=name, out_shape=jax.ShapeDtypeStruct((t, DI), BF16), grid=(t // ROWS,),
        in_specs=[row, pl.BlockSpec((ROWS, DI), lambda i: (i, O_Z // DI)), pl.BlockSpec((1, DI), lambda i: (0, 0))],
        out_specs=row, compiler_params=_cp(("parallel",)),
    )(y_pre, proj, g_ssd)


def _gated_norm_bwd(y_pre, proj, g_ssd, dyn, dproj, *, name, jobs=()):
    t = y_pre.shape[0]

    def body(y_ref, z_ref, g_ref, dyn_ref, _, dy_ref, dz_ref, dg_ref):
        z = z_ref[...]
        s = _sigmoid(z)
        sz = z * s
        yv = y_ref[...]
        v = yv * sz
        r = lax.rsqrt(jnp.mean(v * v, axis=-1, keepdims=True) + SSM_EPS)
        vh = v * r
        dn = dyn_ref[...]
        dvh = dn * g_ref[...]
        dv = r * (dvh - vh * jnp.mean(dvh * vh, axis=-1, keepdims=True))
        dy_ref[...] = dv * sz
        dz_ref[...] = (dv * yv * s * (1.0 + z * (1.0 - s))).astype(BF16)

        @pl.when(pl.program_id(0) == 0)
        def _():
            dg_ref[...] = jnp.zeros_like(dg_ref)

        dg_ref[...] += jnp.broadcast_to(jnp.sum(dn * vh, axis=0, keepdims=True), dg_ref.shape)

    row = pl.BlockSpec((ROWS, DI), lambda i: (i, 0))
    return _call(
        body, jobs=jobs, name=name,
        out_shape=(jax.ShapeDtypeStruct((t, DI), F32), jax.ShapeDtypeStruct(dproj.shape, BF16),
                   jax.ShapeDtypeStruct((8, DI), F32)),
        grid=(t // ROWS,),
        in_specs=[row, pl.BlockSpec((ROWS, DI), lambda i: (i, O_Z // DI)), pl.BlockSpec((1, DI), lambda i: (0, 0)), row, ANY],
        out_specs=(row, pl.BlockSpec((ROWS, DI), lambda i: (i, O_Z // DI)), pl.BlockSpec((8, DI), lambda i: (0, 0))),
        compiler_params=_cp(("arbitrary",)), aliases={4: 1},
    )(y_pre, proj, g_ssd, dyn, dproj)


CONV_TC = 512


def _shift_down(x, s, row):
    if s == 0:
        return x
    return jnp.where(row >= s, pltpu.roll(x, s, 0), 0.0)


def _shift_up(x, s, row, t):
    if s == 0:
        return x
    return jnp.where(row < t - s, pltpu.roll(x, t - s, 0), 0.0)


def _conv_fwd(proj, conv_w, conv_b, *, name):
    t = proj.shape[0]

    def body(x_ref, w_ref, b_ref, o_ref):
        x = x_ref[...]
        row = lax.broadcasted_iota(jnp.int32, x.shape, 0)
        pre = jnp.broadcast_to(b_ref[...], x.shape)
        for k in range(CW):
            pre = pre + w_ref[k:k + 1, :] * _shift_down(x, CW - 1 - k, row)
        o_ref[...] = pre * _sigmoid(pre)

    return pl.pallas_call(
        body, name=name, out_shape=jax.ShapeDtypeStruct((t, CONV), F32), grid=(CONV // CONV_TC,),
        in_specs=[pl.BlockSpec((t, CONV_TC), lambda j: (0, O_XBC // CONV_TC + j)),
                  pl.BlockSpec((CW, CONV_TC), lambda j: (0, j)), pl.BlockSpec((1, CONV_TC), lambda j: (0, j))],
        out_specs=pl.BlockSpec((t, CONV_TC), lambda j: (0, j)), compiler_params=_cp(("parallel",)),
    )(proj, conv_w, conv_b)


def _conv_bwd(proj, conv_w, conv_b, dxs, db, dc, dproj, *, name, jobs=()):
    t = proj.shape[0]
    nx = DI // CONV_TC
    assert NG * NS == CONV_TC

    def body(x_ref, w_ref, b_ref, dxs_ref, db_ref, dc_ref, _, dx_ref, dw_ref, dbias_ref):
        j = pl.program_id(0)
        x = x_ref[...]
        row = lax.broadcasted_iota(jnp.int32, x.shape, 0)
        xs = [_shift_down(x, CW - 1 - k, row) for k in range(CW)]
        pre = jnp.broadcast_to(b_ref[...], x.shape)
        for k in range(CW):
            pre = pre + w_ref[k:k + 1, :] * xs[k]
        s = _sigmoid(pre)
        da = jnp.where(j < nx, dxs_ref[...], jnp.where(j == nx, db_ref[...], dc_ref[...]))
        dpre = da * s * (1.0 + pre * (1.0 - s))
        dx = jnp.zeros_like(x)
        row8 = lax.broadcasted_iota(jnp.int32, dw_ref.shape, 0)
        dw = jnp.zeros(dw_ref.shape, F32)
        for k in range(CW):
            dx = dx + w_ref[k:k + 1, :] * _shift_up(dpre, CW - 1 - k, row, t)
            dw = dw + jnp.where(row8 == k, jnp.sum(dpre * xs[k], axis=0, keepdims=True), 0.0)
        dx_ref[...] = dx.astype(BF16)
        dw_ref[...] = dw
        dbias_ref[...] = jnp.broadcast_to(jnp.sum(dpre, axis=0, keepdims=True), dbias_ref.shape)

    col8 = pl.BlockSpec((8, CONV_TC), lambda j: (0, j))
    xbc = pl.BlockSpec((t, CONV_TC), lambda j: (0, O_XBC // CONV_TC + j))
    whole = pl.BlockSpec((t, CONV_TC), lambda j: (0, 0))
    return _call(
        body, jobs=jobs, name=name,
        out_shape=(jax.ShapeDtypeStruct(dproj.shape, BF16), jax.ShapeDtypeStruct((8, CONV), F32),
                   jax.ShapeDtypeStruct((8, CONV), F32)),
        grid=(CONV // CONV_TC,),
        in_specs=[xbc, pl.BlockSpec((CW, CONV_TC), lambda j: (0, j)), pl.BlockSpec((1, CONV_TC), lambda j: (0, j)),
                  pl.BlockSpec((t, CONV_TC), lambda j: (0, jnp.minimum(j, nx - 1))), whole, whole, ANY],
        out_specs=(xbc, col8, col8),
        compiler_params=_cp(("arbitrary",)), aliases={6: 0},
    )(proj, conv_w, conv_b, dxs, db, dc, dproj)


def _rope_tables(positions, t):
    half = HD // 2
    inv_freq = ROPE_THETA ** (-jnp.arange(half, dtype=F32) * 2.0 / HD)
    ang = positions.reshape(t).astype(F32)[:, None] * inv_freq
    cos, sin = jnp.cos(ang), jnp.sin(ang)
    return jnp.concatenate([cos] * 4, axis=1), jnp.concatenate([-sin, sin] * 2, axis=1)


def _lane_consts():
    lane = lax.broadcasted_iota(jnp.int32, (L, 128), 1)
    return lane, (lane % HD) < (HD // 2), lane < HD


def _rope(tv, cos, sin, lo):
    return tv * cos + jnp.where(lo, pltpu.roll(tv, 128 - HD // 2, 1), pltpu.roll(tv, HD // 2, 1)) * sin


def _rope_t(dv, cos, sin, lo):
    ds = dv * sin
    return dv * cos + jnp.where(lo, pltpu.roll(ds, 128 - HD // 2, 1), pltpu.roll(ds, HD // 2, 1))


def _placed(chunk, g, half0):
    own = jnp.where(half0 if g % 2 == 0 else jnp.logical_not(half0), chunk, 0.0)
    other = pltpu.roll(own, HD, 1)
    return (own, other) if g % 2 == 0 else (other, own)


def _unplace(acc, hf, g, half0):
    v = jnp.where(half0 if hf == 0 else jnp.logical_not(half0), acc, 0.0)
    return v if hf == g % 2 else pltpu.roll(v, HD, 1)


def _attn_fwd(proj, cos, sin, sinks, *, name, jobs=()):
    t = proj.shape[0]
    nb = t // L
    scale = HD ** -0.5

    def body(sink_ref, q_ref, kc_ref, kp_ref, vc_ref, vp_ref, cc_ref, sc_ref, cp_ref, sp_ref, o_ref, lse_ref):
        i = pl.program_id(0)
        lane, lo, half0 = _lane_consts()
        cos_c, sin_c, cos_p, sin_p = cc_ref[...], sc_ref[...], cp_ref[...], sp_ref[...]
        row = lax.broadcasted_iota(jnp.int32, (L, 2 * L), 0)
        col = lax.broadcasted_iota(jnp.int32, (L, 2 * L), 1)
        valid = jnp.logical_or(jnp.logical_and(jnp.logical_and(col < L, col > row), i > 0),
                               jnp.logical_and(col >= L, col - L <= row))
        kc = [_rope(kc_ref[:, 128 * m:128 * (m + 1)], cos_c, sin_c, lo) for m in range(2)]
        kp = [_rope(kp_ref[:, 128 * m:128 * (m + 1)], cos_p, sin_p, lo) for m in range(2)]
        lse_acc = jnp.zeros((L, 128), F32)
        outs = [jnp.zeros((L, 128), F32) for _ in range(QD // 128)]
        qs = [(_rope(q_ref[:, 128 * ch:128 * (ch + 1)], cos_c, sin_c, lo) * scale).astype(BF16) for ch in range(QD // 128)]
        both = lambda prev, cur, g: [jnp.concatenate([a, b], axis=0).astype(BF16)
                                     for a, b in zip(_placed(prev, g, half0), _placed(cur, g, half0))]
        for g in range(NKV):
            sl = slice(128 * (g // 2), 128 * (g // 2 + 1))
            kv = both(kp[g // 2], kc[g // 2], g)
            vv = both(vp_ref[:, sl], vc_ref[:, sl], g)
            for r in range(NQH // NKV):
                h = g * (NQH // NKV) + r
                ch, hf = h // 2, h % 2
                s = jnp.where(valid, _dot_nt(qs[ch], kv[hf]), NEG)
                sink = sink_ref[0, h]
                mx = jnp.maximum(jnp.max(s, axis=-1, keepdims=True), sink)
                e = jnp.exp(s - mx)
                den = jnp.sum(e, axis=-1, keepdims=True) + jnp.exp(sink - mx)
                outs[ch] = outs[ch] + _dot((e * (1.0 / den)).astype(BF16), vv[hf])
                lse_acc = jnp.where(lane == h, mx + jnp.log(den), lse_acc)
        for ch in range(QD // 128):
            o_ref[:, 128 * ch:128 * (ch + 1)] = outs[ch].astype(BF16)
        lse_ref[...] = lse_acc

    prev = lambda i: jnp.maximum(i - 1, 0)
    tab_c = pl.BlockSpec((L, 128), lambda i: (i, 0))
    tab_p = pl.BlockSpec((L, 128), lambda i: (prev(i), 0))
    return _call(
        body, jobs=jobs, name=name,
        out_shape=(jax.ShapeDtypeStruct((t, QD), BF16), jax.ShapeDtypeStruct((t, 128), F32)),
        grid=(nb,),
        in_specs=[pl.BlockSpec(memory_space=pltpu.SMEM),
                  pl.BlockSpec((L, QD), lambda i: (i, O_Q // QD)),
                  pl.BlockSpec((L, KVD), lambda i: (i, O_K // KVD)), pl.BlockSpec((L, KVD), lambda i: (prev(i), O_K // KVD)),
                  pl.BlockSpec((L, KVD), lambda i: (i, O_V // KVD)), pl.BlockSpec((L, KVD), lambda i: (prev(i), O_V // KVD)),
                  tab_c, tab_c, tab_p, tab_p],
        out_specs=(pl.BlockSpec((L, QD), lambda i: (i, 0)), pl.BlockSpec((L, 128), lambda i: (i, 0))),
        compiler_params=_cp(("parallel",)),
    )(sinks, proj, proj, proj, proj, proj, cos, sin, cos, sin)


def _attn_bwd(proj, cos, sin, sinks, attn, lse, dattn, dproj, *, name, jobs=()):
    t = proj.shape[0]
    nb = t // L
    scale = HD ** -0.5

    def body(sink_ref, qi_ref, qn_ref, kc_ref, kp_ref, vc_ref, vp_ref, doi_ref, don_ref, oi_ref, on_ref,
             lsei_ref, lsen_ref, cc_ref, sc_ref, cp_ref, sp_ref, cn_ref, sn_ref, _, dqkv_ref, dsk_ref):
        i = pl.program_id(0)
        lane, lo, half0 = _lane_consts()
        half1 = jnp.logical_not(half0)
        cos_c, sin_c = cc_ref[...], sc_ref[...]
        row = lax.broadcasted_iota(jnp.int32, (L, 2 * L), 0)
        col = lax.broadcasted_iota(jnp.int32, (L, 2 * L), 1)
        valid = jnp.logical_or(jnp.logical_and(jnp.logical_and(col < L, col > row), i > 0),
                               jnp.logical_and(col >= L, col - L <= row))
        m_next = jnp.logical_and(col[:, :L] > row[:, :L], i < nb - 1)
        kc = [_rope(kc_ref[:, 128 * m:128 * (m + 1)], cos_c, sin_c, lo) for m in range(2)]
        kp = [_rope(kp_ref[:, 128 * m:128 * (m + 1)], cp_ref[...], sp_ref[...], lo) for m in range(2)]
        lse_i, lse_n = lsei_ref[...], lsen_ref[...]
        dk_acc = [jnp.zeros((L, 128), F32) for _ in range(2)]
        dv_acc = [jnp.zeros((L, 128), F32) for _ in range(2)]
        dsk_acc = jnp.zeros((1, 128), F32)
        lane1 = lax.broadcasted_iota(jnp.int32, (1, 128), 1)
        both = lambda prev, cur, g: [jnp.concatenate([a, b], axis=0).astype(BF16)
                                     for a, b in zip(_placed(prev, g, half0), _placed(cur, g, half0))]
        kvs = [both(kp[g // 2], kc[g // 2], g) for g in range(NKV)]
        vvs = [both(vp_ref[:, 128 * (g // 2):128 * (g // 2 + 1)], vc_ref[:, 128 * (g // 2):128 * (g // 2 + 1)], g)
               for g in range(NKV)]
        for ch in range(QD // 128):
            sl = slice(128 * ch, 128 * (ch + 1))
            q_i = (_rope(qi_ref[:, sl], cos_c, sin_c, lo) * scale).astype(BF16)
            q_n = (_rope(qn_ref[:, sl], cn_ref[...], sn_ref[...], lo) * scale).astype(BF16)
            q_in = jnp.concatenate([q_i, q_n], axis=0)
            do_i, do_n = doi_ref[:, sl], don_ref[:, sl]
            do_ib, do_nb = do_i.astype(BF16), do_n.astype(BF16)
            do_in = jnp.concatenate([do_ib, do_nb], axis=0)
            od_i = do_i * oi_ref[:, sl].astype(F32)
            od_n = do_n * on_ref[:, sl].astype(F32)
            dq_ch = jnp.zeros((L, 128), F32)
            for hf in range(2):
                h = 2 * ch + hf
                g = h // (NQH // NKV)
                hm = half0 if hf == 0 else half1
                kv, vv = kvs[g][hf], vvs[g][hf]
                kcv, vcv = kv[L:], vv[L:]
                dl_i = jnp.sum(jnp.where(hm, od_i, 0.0), axis=-1, keepdims=True)
                dl_n = jnp.sum(jnp.where(hm, od_n, 0.0), axis=-1, keepdims=True)
                ls_i = jnp.sum(jnp.where(lane == h, lse_i, 0.0), axis=-1, keepdims=True)
                ls_n = jnp.sum(jnp.where(lane == h, lse_n, 0.0), axis=-1, keepdims=True)
                p = jnp.where(valid, jnp.exp(_dot_nt(q_i, kv) - ls_i), 0.0)
                ds = (p * (_dot_nt(do_ib, vv) - dl_i)).astype(BF16)
                dq_ch = dq_ch + jnp.where(hm, _dot(ds, kv) * scale, 0.0)
                sink = sink_ref[0, h]
                dsk = -jnp.sum(jnp.exp(sink - ls_i) * dl_i, axis=0, keepdims=True)
                dsk_acc = dsk_acc + jnp.where(lane1 == h, dsk, 0.0)
                p_n = jnp.where(m_next, jnp.exp(_dot_nt(q_n, kcv) - ls_n), 0.0)
                ds_n = (p_n * (_dot_nt(do_nb, vcv) - dl_n)).astype(BF16)
                dv_h = _dot_tn(jnp.concatenate([p[:, L:].astype(BF16), p_n.astype(BF16)], axis=0), do_in)
                dk_h = _dot_tn(jnp.concatenate([ds[:, L:], ds_n], axis=0), q_in)
                dv_acc[g // 2] = dv_acc[g // 2] + _unplace(dv_h, hf, g, half0)
                dk_acc[g // 2] = dk_acc[g // 2] + _unplace(dk_h, hf, g, half0)
            dqkv_ref[:, sl] = _rope_t(dq_ch, cos_c, sin_c, lo).astype(BF16)
        for m in range(2):
            dqkv_ref[:, QD + 128 * m:QD + 128 * (m + 1)] = _rope_t(dk_acc[m], cos_c, sin_c, lo).astype(BF16)
            dqkv_ref[:, QD + KVD + 128 * m:QD + KVD + 128 * (m + 1)] = dv_acc[m].astype(BF16)

        @pl.when(i == 0)
        def _():
            dsk_ref[...] = jnp.zeros_like(dsk_ref)

        dsk_ref[...] += jnp.broadcast_to(dsk_acc, dsk_ref.shape)

    prev = lambda i: jnp.maximum(i - 1, 0)
    nxt = lambda i: jnp.minimum(i + 1, nb - 1)
    cur_q = pl.BlockSpec((L, QD), lambda i: (i, 0))
    nxt_q = pl.BlockSpec((L, QD), lambda i: (nxt(i), 0))
    tab = lambda f: pl.BlockSpec((L, 128), lambda i: (f(i), 0))
    ident = lambda i: i
    qkv = QD + 2 * KVD
    assert O_K == O_Q + QD and O_V == O_K + KVD and O_Q % qkv == 0
    return _call(
        body, jobs=jobs, name=name,
        out_shape=(jax.ShapeDtypeStruct(dproj.shape, BF16), jax.ShapeDtypeStruct((8, 128), F32)),
        grid=(nb,),
        in_specs=[pl.BlockSpec(memory_space=pltpu.SMEM),
                  pl.BlockSpec((L, QD), lambda i: (i, O_Q // QD)), pl.BlockSpec((L, QD), lambda i: (nxt(i), O_Q // QD)),
                  pl.BlockSpec((L, KVD), lambda i: (i, O_K // KVD)), pl.BlockSpec((L, KVD), lambda i: (prev(i), O_K // KVD)),
                  pl.BlockSpec((L, KVD), lambda i: (i, O_V // KVD)), pl.BlockSpec((L, KVD), lambda i: (prev(i), O_V // KVD)),
                  cur_q, nxt_q, cur_q, nxt_q, tab(ident), tab(nxt),
                  tab(ident), tab(ident), tab(prev), tab(prev), tab(nxt), tab(nxt), ANY],
        out_specs=(pl.BlockSpec((L, qkv), lambda i: (i, O_Q // qkv)), pl.BlockSpec((8, 128), lambda i: (0, 0))),
        compiler_params=_cp(("arbitrary",)), aliases={19: 0},
    )(sinks, proj, proj, proj, proj, proj, proj, dattn, dattn, attn, attn, lse, lse, cos, sin, cos, sin, cos, sin, dproj)


PAIRS = NH // NG // 2


def _softplus(x):
    return jnp.maximum(x, 0.0) + jnp.log(1.0 + jnp.exp(-jnp.abs(x)))


def _ssd_chunk(g, xps, dtr, bm, cm, sps, dtb, alog, dsk):
    lane = lax.broadcasted_iota(jnp.int32, (L, 128), 1)
    lane1 = lax.broadcasted_iota(jnp.int32, (1, 128), 1)
    row = lax.broadcasted_iota(jnp.int32, (L, L), 0)
    col = lax.broadcasted_iota(jnp.int32, (L, L), 1)
    rowc = lax.broadcasted_iota(jnp.int32, (128, 1), 0)
    tril = col <= row
    dt = _softplus(dtr + dtb)
    a = dt * (-jnp.exp(alog))
    a_cs = lax.dot_general(tril.astype(F32), a, (((1,), (0,)), ((), ())), precision=lax.Precision.HIGHEST,
                           preferred_element_type=F32)
    a_cst = a_cs.T
    a_last = jnp.sum(jnp.where(row == L - 1, a_cs, 0.0), axis=0, keepdims=True)
    cb = _bdot_nt(cm, bm)
    ys, snew = [], []
    for q in range(PAIRS):
        xp, sp = xps[q], sps[q]
        skip = jnp.zeros((L, 128), F32)
        keep = jnp.zeros((128, 1), F32)
        ms, xds, cds, sms, bds = [], [], [], [], []
        for hh in range(2):
            h = g * 2 * PAIRS + 2 * q + hh
            hm = (lane < HD) if hh == 0 else (lane >= HD)
            rm = (rowc < HD) if hh == 0 else (rowc >= HD)
            dt_h = jnp.sum(jnp.where(lane == h, dt, 0.0), axis=1, keepdims=True)
            acs_h = jnp.sum(jnp.where(lane == h, a_cs, 0.0), axis=1, keepdims=True)
            acst_h = jnp.sum(jnp.where(row == h, a_cst, 0.0), axis=0, keepdims=True)
            al_h = jnp.sum(jnp.where(lane1 == h, a_last, 0.0), axis=1, keepdims=True)
            dsk_h = jnp.sum(jnp.where(lane1 == h, dsk, 0.0), axis=1, keepdims=True)
            decay = jnp.where(tril, jnp.exp(jnp.where(tril, acs_h - acst_h, 0.0)), 0.0)
            xh = jnp.where(hm, xp, 0.0)
            ms.append(cb * decay)
            xds.append(xh * dt_h)
            cds.append(cm * jnp.exp(acs_h))
            sms.append(jnp.where(rm, sp, 0.0))
            bds.append(bm * jnp.exp(al_h - acs_h))
            skip = skip + dsk_h * xh
            keep = keep + jnp.where(rm, jnp.exp(al_h), 0.0)
        xd2 = jnp.concatenate(xds, axis=0)
        y_pair = (_bdot(jnp.concatenate(ms, axis=1), xd2)
                  + _bdot_nt(jnp.concatenate(cds, axis=1), jnp.concatenate(sms, axis=1)) + skip)
        ys.append(y_pair)
        snew.append(sp * keep + _bdot_tn(xd2, jnp.concatenate(bds, axis=0)))
    return ys, snew


def _ssd_specs(t):
    nc = t // L
    xs = lambda f: pl.BlockSpec((L, 128 * PAIRS), lambda c, g: (f(c), g))
    bspec = lambda f: pl.BlockSpec((L, NS), lambda c, g: (f(c), DI // NS + g))
    cspec = lambda f: pl.BlockSpec((L, NS), lambda c, g: (f(c), DI // NS + NG + g))
    dts = lambda f: pl.BlockSpec((L, 128), lambda c, g: (f(c), O_DT // 128))
    par = pl.BlockSpec((1, 128), lambda c, g: (0, 0))
    st = lambda f: pl.BlockSpec((1, 1, PAIRS, 128, NS), lambda c, g: (f(c), g, 0, 0, 0))
    return nc, xs, bspec, cspec, dts, par, st


def _ssd_fwd(xbc_act, proj, dtb, alog, dsk, *, name, jobs=()):
    t = proj.shape[0]
    nc, xs, bspec, cspec, dts, par, st = _ssd_specs(t)
    ident = lambda c: c

    def body(x_ref, b_ref, c_ref, dt_ref, dtb_ref, al_ref, dsk_ref, y_ref, sin_ref, s_ref):
        c, g = pl.program_id(0), pl.program_id(1)

        @pl.when(c == 0)
        def _():
            s_ref[g] = jnp.zeros((PAIRS, 128, NS), F32)

        sps = [s_ref[g, q] for q in range(PAIRS)]
        for q in range(PAIRS):
            sin_ref[0, 0, q] = sps[q]
        xps = [x_ref[:, 128 * q:128 * (q + 1)] for q in range(PAIRS)]
        ys, snew = _ssd_chunk(g, xps, dt_ref[...], b_ref[...], c_ref[...], sps, dtb_ref[...], al_ref[...], dsk_ref[...])
        for q in range(PAIRS):
            y_ref[:, 128 * q:128 * (q + 1)] = ys[q]
            s_ref[g, q] = snew[q]

    return _call(
        body, jobs=jobs, name=name,
        out_shape=(jax.ShapeDtypeStruct((t, DI), F32), jax.ShapeDtypeStruct((nc, NG, PAIRS, 128, NS), F32)),
        grid=(nc, NG),
        in_specs=[xs(ident), bspec(ident), cspec(ident), dts(ident), par, par, par],
        out_specs=(pl.BlockSpec((L, 128 * PAIRS), lambda c, g: (c, g)), st(ident)),
        scratch_shapes=[pltpu.VMEM((NG, PAIRS, 128, NS), F32)],
        compiler_params=_cp(("arbitrary", "arbitrary")),
    )(xbc_act, xbc_act, xbc_act, proj, dtb, alog, dsk)


def _ssd_bwd(xbc_act, proj, dtb, alog, dsk, states, dy, dproj, *, name, jobs=()):
    t = proj.shape[0]
    nc, xs, bspec, cspec, dts, par, st = _ssd_specs(t)
    rev = lambda c: nc - 1 - c

    def body(x_ref, b_ref, c_ref, dt_ref, dtb_ref, al_ref, dsk_ref, sin_ref, dy_ref, _,
             dx_ref, db_ref, dc_ref, ddtp_ref, ddtb_ref, dal_ref, ddsk_ref, ds_ref, ddt_ref):
        c, g = pl.program_id(0), pl.program_id(1)

        @pl.when(c == 0)
        def _():
            ds_ref[g] = jnp.zeros((PAIRS, 128, NS), F32)

        @pl.when(jnp.logical_and(c == 0, g == 0))
        def _():
            ddtb_ref[...] = jnp.zeros_like(ddtb_ref)
            dal_ref[...] = jnp.zeros_like(dal_ref)
            ddsk_ref[...] = jnp.zeros_like(ddsk_ref)

        @pl.when(g == 0)
        def _():
            ddt_ref[...] = jnp.zeros_like(ddt_ref)

        sps = [sin_ref[0, 0, q] for q in range(PAIRS)]
        xps = [x_ref[:, 128 * q:128 * (q + 1)] for q in range(PAIRS)]
        _, vjp = jax.vjp(functools.partial(_ssd_chunk, g), xps, dt_ref[...], b_ref[...], c_ref[...], sps,
                         dtb_ref[...], al_ref[...], dsk_ref[...])
        dys = [dy_ref[:, 128 * q:128 * (q + 1)] for q in range(PAIRS)]
        dss = [ds_ref[g, q] for q in range(PAIRS)]
        dxps, ddt, db, dc, dsps, ddtb, dal, ddsk = vjp((dys, dss))
        for q in range(PAIRS):
            dx_ref[:, 128 * q:128 * (q + 1)] = dxps[q]
            ds_ref[g, q] = dsps[q]
        db_ref[...] = db
        dc_ref[...] = dc
        ddt_ref[...] += ddt
        ddtb_ref[...] += jnp.broadcast_to(ddtb, ddtb_ref.shape)
        dal_ref[...] += jnp.broadcast_to(dal, dal_ref.shape)
        ddsk_ref[...] += jnp.broadcast_to(ddsk, ddsk_ref.shape)

        @pl.when(g == NG - 1)
        def _():
            ddtp_ref[:, :128] = ddt_ref[...].astype(BF16)
            ddtp_ref[:, 128:] = jnp.zeros((L, DT_PAD - 128), BF16)

    acc = pl.BlockSpec((8, 128), lambda c, g: (0, 0))
    o8 = jax.ShapeDtypeStruct((8, 128), F32)
    return _call(
        body, jobs=jobs, name=name,
        out_shape=(jax.ShapeDtypeStruct((t, DI), F32), jax.ShapeDtypeStruct((t, NG * NS), F32),
                   jax.ShapeDtypeStruct((t, NG * NS), F32), jax.ShapeDtypeStruct(dproj.shape, BF16), o8, o8, o8),
        grid=(nc, NG),
        in_specs=[xs(rev), bspec(rev), cspec(rev), dts(rev), par, par, par, st(rev),
                  pl.BlockSpec((L, 128 * PAIRS), lambda c, g: (rev(c), g)), ANY],
        out_specs=(pl.BlockSpec((L, 128 * PAIRS), lambda c, g: (rev(c), g)),
                   pl.BlockSpec((L, NS), lambda c, g: (rev(c), g)), pl.BlockSpec((L, NS), lambda c, g: (rev(c), g)),
                   pl.BlockSpec((L, DT_PAD), lambda c, g: (rev(c), O_DT // DT_PAD)), acc, acc, acc),
        scratch_shapes=[pltpu.VMEM((NG, PAIRS, 128, NS), F32), pltpu.VMEM((L, 128), F32)],
        compiler_params=_cp(("arbitrary", "arbitrary")), aliases={9: 3},
    )(xbc_act, xbc_act, xbc_act, proj, dtb, alog, dsk, states, dy, dproj)


def _pad_lanes(v, n=128):
    return jnp.pad(v, ((0, 0), (0, n - v.shape[1])))


def _local_step(x, p, positions, target, small, plan):
    t = x.shape[0]
    cos, sin = _rope_tables(positions, t)
    dtb, alog, dsk = _pad_lanes(small["dt_bias"]), _pad_lanes(small["a_log"]), _pad_lanes(small["d_skip"])
    w, jobs = plan.w, plan.jobs

    def mm(a, b, *, name, tm=t, tn=512, **kw):
        return _matmul(a, b, tm=tm, tn=tn, name=name, jobs=jobs(name), **kw)

    tkl = FFN // 4

    def dw(wname, a, dy, *, name, tm):
        plan.g(wname, _matmul(a, dy, ta=True, out_dtype=BF16, tm=tm, tn=512, tk=t, name=name, jobs=jobs(name)))

    u = _rmsnorm_fwd(x, small["g_mix"], name="norm_mix")
    proj = mm(u, w("w_in"), tn=1024, tk=D, name="mm_in")
    attn, lse = _attn_fwd(proj, cos, sin, small["sinks"], name="attn_fwd", jobs=jobs("attn_fwd"))
    out_a = mm(attn, w("w_attn_br"), tk=QD, name="mm_attn_br")
    xbc_act = _conv_fwd(proj, small["conv_w"], small["conv_b"], name="conv_fwd")
    y_pre, states = _ssd_fwd(xbc_act, proj, dtb, alog, dsk, name="ssd_fwd", jobs=jobs("ssd_fwd"))
    yn = _gated_norm_fwd(y_pre, proj, small["g_ssd"], name="gated_norm_fwd")
    out_s = mm(yn, w("w_ssd_br"), tk=DI, name="mm_ssd_br")
    merged = _merge_fwd(proj, out_a, out_s, name="merge_fwd")
    h1 = mm(merged, w("w_o"), add=x, tk=D, name="mm_o")
    f = _rmsnorm_fwd(h1, small["g_ffn"], name="norm_ffn")
    gate, up, act = _swiglu_fwd(f, w("w_gate"), w("w_up"), name="swiglu_fwd", jobs=jobs("swiglu_fwd"))
    h2 = mm(act, w("w_down"), add=h1, tm=t // 2, tk=FFN // 2, name="mm_down")
    e = _rmsnorm_fwd(h2, small["g_ple"], name="norm_ple")
    pgl = mm(e, w("w_ple_gate"), tk=D, name="mm_ple_gate")
    pb = p.astype(BF16)
    pp = mm(pb, w("w_ple_proj"), tk=PLE, name="mm_ple_proj")
    dh3, dpgl, dpp, loss, dg_final = _final(h2, pgl, pp, target, small["g_final"].reshape(1, D), name="final")

    dw("w_ple_proj", pb, dpp, tm=PLE, name="mm_d_ple_proj")
    dw("w_ple_gate", e, dpgl, tm=D, name="mm_d_ple_gate")
    de = mm(dpgl, w("w_ple_gate"), tb=True, tk=D, name="mm_de")
    dh2, dh2b, dg_ple = _rmsnorm_bwd(h2, small["g_ple"], de, dh3, name="norm_ple_bwd", jobs=jobs("norm_ple_bwd"))
    dw("w_down", act, dh2b, tm=FFN // 2, name="mm_d_down")
    dgate, dup = _swiglu_bwd(dh2b, w("w_down"), gate, up, name="swiglu_bwd", jobs=jobs("swiglu_bwd"))
    dw("w_gate", f, dgate, tm=D, name="mm_d_gate")
    dw("w_up", f, dup, tm=D, name="mm_d_up")
    df = mm(dgate, w("w_gate"), tb=True, tn=1024, tk=tkl, name="mm_df_gate")
    df = mm(dup, w("w_up"), tb=True, add=df, tm=t // 2, tk=FFN // 2, name="mm_df_up")
    dh1, dh1b, dg_ffn = _rmsnorm_bwd(h1, small["g_ffn"], df, dh2, name="norm_ffn_bwd", jobs=jobs("norm_ffn_bwd"))
    dw("w_o", merged, dh1b, tm=D, name="mm_d_o")
    dmerged = mm(dh1b, w("w_o"), tb=True, tk=D, name="mm_dmerged")
    dout_a, dout_s, dproj = _merge_bwd(proj, out_a, out_s, dmerged, name="merge_bwd")
    dw("w_attn_br", attn, dout_a, tm=QD, name="mm_d_attn_br")
    dw("w_ssd_br", yn, dout_s, tm=DI, name="mm_d_ssd_br")
    dattn = mm(dout_a, w("w_attn_br"), tb=True, tk=D, name="mm_dattn")
    dyn = mm(dout_s, w("w_ssd_br"), tb=True, tk=D, name="mm_dyn")
    dproj, dsinks = _attn_bwd(proj, cos, sin, small["sinks"], attn, lse, dattn, dproj, name="attn_bwd",
                              jobs=jobs("attn_bwd"))
    dy_pre, dproj, dg_ssd = _gated_norm_bwd(y_pre, proj, small["g_ssd"], dyn, dproj, name="gated_norm_bwd",
                                            jobs=jobs("gated_norm_bwd"))
    dxs, db, dc, dproj, ddtb, dalog, ddsk = _ssd_bwd(xbc_act, proj, dtb, alog, dsk, states, dy_pre, dproj, name="ssd_bwd",
                                                     jobs=jobs("ssd_bwd"))
    dproj, dconv_w, dconv_b = _conv_bwd(proj, small["conv_w"], small["conv_b"], dxs, db, dc, dproj, name="conv_bwd",
                                        jobs=jobs("conv_bwd"))
    for which, h in (("send", 1 - plan.core), ("keep", plan.core)):
        uh = lax.dynamic_slice_in_dim(u, h * (D // 2), D // 2, axis=1)
        name = "mm_d_in_" + which
        plan.g_half("w_in", which, _matmul(uh, dproj, ta=True, out_dtype=BF16, tm=D // 2, tn=1024, tk=t, name=name,
                                           jobs=jobs(name)))
    du = mm(dproj, w("w_in"), tb=True, tn=1024, tk=tkl, name="mm_du")
    grad_x, _, dg_mix = _rmsnorm_bwd(x, small["g_mix"], du, dh1, name="norm_mix_bwd", jobs=jobs("norm_mix_bwd"))

    gs = {
        "g_mix": dg_mix[:1], "conv_w": dconv_w[:CW], "conv_b": dconv_b[:1], "dt_bias": ddtb[:1, :NH],
        "a_log": dalog[:1, :NH], "d_skip": ddsk[:1, :NH], "g_ssd": dg_ssd[:1], "sinks": dsinks[:1, :NQH],
        "g_ffn": dg_ffn[:1], "g_ple": dg_ple[:1], "g_final": dg_final[0],
    }
    return loss, grad_x, gs


def _shard_pieces():
    segs = ((R_Q, QD, O_Q), (R_K, KVD, O_K), (R_V, KVD, O_V), (R_Z, DI, O_Z), (R_XBC, CONV, O_XBC), (R_DT, NH, O_DT),
            (R_GA, D, O_GA), (R_GS, D, O_GS))
    cs = IN_DIM // NCHIP
    out = []
    for j in range(NCHIP):
        for r0, n, k0 in segs:
            lo, hi = max(r0, j * cs), min(r0 + n, (j + 1) * cs)
            if lo < hi:
                out.append((j, lo - j * cs, hi - lo, k0 + lo - r0))
    return out


SLAB = IN_DIM // NCHIP
SLAB_PAD = -(-SLAB // 128) * 128
REMAP_ROWS = 256


def _lane_remap(src, dst_slabs, dst_cols, moves, *, name, add=None, jobs=()):
    s_n, rows, s_cols = src.shape
    assert s_cols % 128 == 0 and dst_cols % 128 == 0 and rows % REMAP_ROWS == 0
    half = REMAP_ROWS // 2

    def body(s_ref, *refs):
        d_ref = refs[-1]
        lane = lax.broadcasted_iota(jnp.int32, (half, 128), 1)
        tiles = {}

        def tile(j, m):
            if (j, m) not in tiles:
                tiles[j, m] = pltpu.bitcast(s_ref[j, :, 128 * m:128 * (m + 1)], jnp.uint32)
            return tiles[j, m]

        def window(j, base):
            m0, s = base // 128, base % 128
            left = tile(j, m0) if 0 <= m0 < s_cols // 128 else None
            if s == 0:
                return left
            right = tile(j, m0 + 1) if 0 <= m0 + 1 < s_cols // 128 else None
            left = None if left is None else pltpu.roll(left, 128 - s, 1)
            right = None if right is None else pltpu.roll(right, 128 - s, 1)
            if left is None or right is None:
                return right if left is None else left
            return jnp.where(lane < 128 - s, left, right)

        for ds in range(dst_slabs):
            for t in range(dst_cols // 128):
                o = 128 * t
                acc = jnp.zeros((half, 128), jnp.uint32)
                for sj, sc, n, dj, dc in moves:
                    lo, hi = max(o, dc) - o, min(o + 128, dc + n) - o
                    if dj != ds or lo >= hi:
                        continue
                    win = window(sj, o - dc + sc)
                    acc = win if (lo, hi) == (0, 128) else jnp.where(jnp.logical_and(lane >= lo, lane < hi), win, acc)
                out = pltpu.bitcast(acc, BF16)
                if add is not None:
                    out = (out.astype(F32) + refs[0][ds, :, o:o + 128].astype(F32)).astype(BF16)
                d_ref[ds, :, o:o + 128] = out

    dst_blk = pl.BlockSpec((dst_slabs, REMAP_ROWS, dst_cols), lambda i: (0, i, 0))
    return _call(
        body, jobs=jobs, name=name, out_shape=jax.ShapeDtypeStruct((dst_slabs, rows, dst_cols), BF16),
        grid=(rows // REMAP_ROWS,),
        in_specs=[pl.BlockSpec((s_n, REMAP_ROWS, s_cols), lambda i: (0, i, 0))] + ([dst_blk] if add is not None else []),
        out_specs=dst_blk, compiler_params=_cp(("parallel",)),
    )(*((src,) if add is None else (src, add)))


def _slabs_to_kernel_cols(slabs, *, name, jobs=()):
    moves = [(j, a, n, 0, k0) for j, a, n, k0 in _shard_pieces()]
    return _lane_remap(slabs, 1, NP, moves, name=name, jobs=jobs)[0]


def _kernel_cols_to_slabs(g, *, name, add=None, jobs=()):
    moves = [(0, k0, n, j, a) for j, a, n, k0 in _shard_pieces()]
    return _lane_remap(g[None], NCHIP, SLAB_PAD, moves, name=name, add=add, jobs=jobs)


MATS = {
    n: (n, kind, 1, r, c, tp, tf) for n, kind, r, c, tp, tf in (
        ("w_in", "stk", 2048, SLAB_PAD, 256, 256),
        ("w_attn_br", "col", 1024, 512, 256, 256),
        ("w_ssd_br", "row", 512, 2048, 512, 256),
        ("w_o", "row", 512, 2048, 512, 256),
        ("w_gate", "col", 2048, 1408, 256, 256),
        ("w_up", "col", 2048, 1408, 256, 256),
        ("w_down", "row", 1408, 2048, 704, 704),
        ("w_ple_gate", "row", 512, 2048, 512, 256),
        ("w_ple_proj", "col", 256, 512, 128, 128),
    )}


def _pos():
    return lax.axis_index("x"), lax.axis_index("y"), lax.axis_index("c")


def _flip(v, a):
    return 1 - v if a else v


def _remote(src, dst, send, recv, dev):
    return pltpu.make_async_remote_copy(src_ref=src, dst_ref=dst, send_sem=send, recv_sem=recv, device_id=dev,
                                        device_id_type=MESH)


def _whole_shape(kind, g, r, c):
    return {"row": (g, NCHIP * r, c), "col": (g, r, NCHIP * c), "stk": (NCHIP, r, c)}[kind]


def _cols(j, c):
    return pl.ds(pl.multiple_of(j * c, 128), c)


def _whole_shard(kind, ref, j, r, c):
    if kind == "row":
        return ref.at[:, pl.ds(j * r, r), :]
    if kind == "col":
        return ref.at[:, :, _cols(j, c)]
    return ref.at[pl.ds(j, 1)]


def _whole_rows(kind, ref, j, row, n, r, c):
    if kind == "row":
        return ref.at[:, pl.ds(j * r + row, n), :]
    if kind == "col":
        return ref.at[:, pl.ds(row, n), _cols(j, c)]
    return ref.at[pl.ds(j, 1), pl.ds(row, n), :]


class _GatherJob(_Job):
    has_mid = True
    NCP = 13

    def __init__(self, names, shards, sink):
        self.mats = [MATS[n] for n in names]
        self.srcs = [shards[n] for n in names]
        self.news = [jax.ShapeDtypeStruct(_whole_shape(kind, g, r, c), BF16) for _, kind, g, r, c, _, _ in self.mats]
        n = len(names)
        self.scratch = [pltpu.SemaphoreType.DMA((self.NCP * n,)), pltpu.SemaphoreType.DMA((self.NCP * n,))]
        self.names, self.sink = names, sink

    def _copies(self, srcs, news, sems):
        send, recv = sems
        x, y, c = _pos()
        me, jx, jy, jd = 2 * x + y, 2 * (1 - x) + y, 2 * x + (1 - y), 2 * (1 - x) + (1 - y)
        nbx, nby, sib = (1 - x, y, c), (x, 1 - y, c), (x, y, 1 - c)
        cps = []
        for w, (_, kind, g, r, cc, _, _) in enumerate(self.mats):
            hr, qr = r // 2, r // 4
            at = lambda j, h, q, n: _whole_rows(kind, news[w], j, h * hr + q * qr, n, r, cc)
            mine = lambda q: srcs[w].at[:, pl.ds(c * hr + q * qr, qr), :]
            cp = lambda k, s, d, dev: _remote(s, d, send.at[self.NCP * w + k], recv.at[self.NCP * w + k], dev)
            cps.append([
                cp(0, mine(0), at(me, c, 0, qr), nbx), cp(1, mine(1), at(me, c, 1, qr), nbx),
                cp(2, mine(1), at(me, c, 1, qr), nby), cp(3, mine(0), at(me, c, 0, qr), nby),
                cp(4, at(jx, c, 0, qr), at(jx, c, 0, qr), nby), cp(5, at(jy, c, 1, qr), at(jy, c, 1, qr), nbx),
                cp(6, at(jx, c, 0, qr), at(jx, c, 0, qr), sib), cp(7, at(jx, c, 1, qr), at(jx, c, 1, qr), sib),
                cp(8, at(jy, c, 1, qr), at(jy, c, 1, qr), sib), cp(9, at(jy, c, 0, qr), at(jy, c, 0, qr), sib),
                cp(10, at(jd, c, 0, qr), at(jd, c, 0, qr), sib), cp(11, at(jd, c, 1, qr), at(jd, c, 1, qr), sib),
                cp(12, srcs[w], _whole_shard(kind, news[w], me, r, cc), sib)])
        return cps

    def _pass_on(self, srcs, news, sems, pairs):
        cps = self._copies(srcs, news, sems)
        for w in range(len(self.mats)):
            for arrived, onward in pairs:
                cps[w][arrived].wait_recv()
                for k in onward:
                    cps[w][k].start()

    def start(self, srcs, dsts, news, sems):
        cps = self._copies(srcs, news, sems)
        for k in (0, 2, 1, 3, 12):
            for w in range(len(self.mats)):
                cps[w][k].start()

    def mid(self, srcs, dsts, news, sems):
        self._pass_on(srcs, news, sems, ((0, (4, 6)), (2, (5, 8))))

    def late(self, srcs, dsts, news, sems):
        self._pass_on(srcs, news, sems, ((1, (7,)), (3, (9,))))

    def finish(self, srcs, dsts, news, sems):
        self._pass_on(srcs, news, sems, ((4, (10,)), (5, (11,))))
        cps = self._copies(srcs, news, sems)
        for w in range(len(self.mats)):
            for k in (6, 7, 8, 9, 10, 11, 12):
                cps[w][k].wait_recv()
            for k in range(self.NCP):
                cps[w][k].wait_send()

    def done(self, dsts, news):
        for n, a in zip(self.names, news):
            self.sink[n] = a


class _SwapJob(_Job):
    def __init__(self, build, ncopies, *, srcs=(), dsts=(), news=(), done=None):
        self.build, self.srcs, self.dsts, self.news, self._done = build, list(srcs), list(dsts), list(news), done
        self.scratch = [pltpu.SemaphoreType.DMA((ncopies,)), pltpu.SemaphoreType.DMA((ncopies,))]

    def start(self, srcs, dsts, news, sems):
        for cp in self.build(srcs, dsts, news, *sems):
            cp.start()

    def finish(self, srcs, dsts, news, sems):
        for cp in self.build(srcs, dsts, news, *sems):
            cp.wait()

    def done(self, dsts, news):
        if self._done is not None:
            self._done(dsts, news)


def _half_of_whole(kind, ref, h, r, c):
    if kind == "row":
        return ref.at[:, :, pl.ds(pl.multiple_of(h * (c // 2), 128), c // 2)]
    return ref.at[:, pl.ds(h * (r // 2), r // 2), :]


def _half_shape(kind, g, r, c):
    return {"row": (g, NCHIP * r, c // 2), "col": (g, r // 2, NCHIP * c), "stk": (NCHIP, r // 2, c)}[kind]


def _sub_shape(kind, r, c):
    return {"row": (1, r // 2, c // 2), "col": (1, r // 4, c), "stk": (1, r // 4, c)}[kind]


def _sub_of_half(kind, ref, j, p, r, c):
    sr = _sub_shape(kind, r, c)[1]
    if kind == "row":
        return ref.at[:, pl.ds(j * r + p * sr, sr), :]
    if kind == "col":
        return ref.at[:, pl.ds(p * sr, sr), _cols(j, c)]
    return ref.at[pl.ds(j, 1), pl.ds(p * sr, sr), :]


def _sub_tile(sr):
    return 256 if sr % 256 == 0 else sr


def _half_of_shard(kind, ref, h, r, c):
    if kind == "row":
        return ref.at[:, :, pl.ds(pl.multiple_of(h * (c // 2), 128), c // 2)]
    return ref.at[:, pl.ds(h * (r // 2), r // 2), :]


def _pair_sum(pack, core, mine, got, whole=True):
    name, kind, g, r, c, tr, _ = pack
    hs = _half_shape(kind, g, r, c)
    nb = hs[1] // tr

    def body(core_ref, a_ref, b_ref, o_ref):
        o_ref[...] = (a_ref[...].astype(F32) + b_ref[...].astype(F32)).astype(BF16)

    blk = (1, tr, hs[2])
    same = lambda gi, i, core_ref: (gi, i, 0)
    if not whole:
        a_map = same
    elif kind == "row":
        a_map = lambda gi, i, core_ref: (gi, i, core_ref[0])
    else:
        a_map = lambda gi, i, core_ref: (gi, core_ref[0] * nb + i, 0)
    return pl.pallas_call(
        body, name="pair_sum_" + name, out_shape=jax.ShapeDtypeStruct(hs, BF16),
        grid_spec=pltpu.PrefetchScalarGridSpec(
            num_scalar_prefetch=1, grid=(hs[0], nb),
            in_specs=[pl.BlockSpec(blk, a_map), pl.BlockSpec(blk, same)], out_specs=pl.BlockSpec(blk, same)),
        compiler_params=_cp(("parallel", "parallel")),
    )(core, mine, got)


def _sub_sums(pack, idx, half, got, *, name):
    _, kind, g, r, c, _, _ = pack
    _, sr, sc = _sub_shape(kind, r, c)
    tr = _sub_tile(sr)
    nb = sr // tr

    def body(idx_ref, a_ref, ga_ref, b_ref, gb_ref, k_ref, p_ref):
        k_ref[0, 0] = a_ref[0].astype(F32) + ga_ref[0, 0].astype(F32)
        p_ref[0, 0] = (b_ref[0].astype(F32) + gb_ref[0, 0].astype(F32)).astype(BF16)

    def sub_map(o):
        if kind == "row":
            return lambda q, i, ix: (0, ix[4 * q + o] * (r // tr) + ix[4 * q + o + 1] * nb + i, 0)
        if kind == "col":
            return lambda q, i, ix: (0, ix[4 * q + o + 1] * nb + i, ix[4 * q + o])
        return lambda q, i, ix: (ix[4 * q + o], ix[4 * q + o + 1] * nb + i, 0)

    sub = lambda o: pl.BlockSpec((1, tr, sc), sub_map(o))
    got_blk = lambda o: pl.BlockSpec((1, 1, tr, sc), lambda q, i, ix: (2 * q + o, 0, i, 0))
    out_blk = pl.BlockSpec((1, 1, tr, sc), lambda q, i, ix: (q, 0, i, 0))
    return pl.pallas_call(
        body, name=name,
        out_shape=(jax.ShapeDtypeStruct((2, 1, sr, sc), F32), jax.ShapeDtypeStruct((2, 1, sr, sc), BF16)),
        grid_spec=pltpu.PrefetchScalarGridSpec(
            num_scalar_prefetch=1, grid=(2, nb), in_specs=[sub(0), got_blk(0), sub(2), got_blk(1)],
            out_specs=(out_blk, out_blk)),
        compiler_params=_cp(("parallel", "parallel")),
    )(idx, half, got, half, got)


def _shard_sum(pack, core, keep, got):
    name, kind, g, r, c, _, _ = pack
    _, sr, sc = _sub_shape(kind, r, c)
    tr = _sub_tile(sr)
    nb = sr // tr

    def body(core_ref, a_ref, b_ref, o_ref):
        o_ref[0] = a_ref[0, 0] + b_ref[0, 0].astype(F32)

    blk = pl.BlockSpec((1, 1, tr, sc), lambda p, i, cr: (p, 0, i, 0))
    if kind == "row":
        o_map = lambda p, i, cr: (0, p * nb + i, cr[0])
    else:
        o_map = lambda p, i, cr: (0, cr[0] * 2 * nb + p * nb + i, 0)
    return pl.pallas_call(
        body, name="shard_sum_" + name, out_shape=jax.ShapeDtypeStruct((g, r, c), F32),
        grid_spec=pltpu.PrefetchScalarGridSpec(
            num_scalar_prefetch=1, grid=(2, nb), in_specs=[blk, blk], out_specs=pl.BlockSpec((1, tr, sc), o_map)),
        compiler_params=_cp(("parallel", "parallel")),
    )(core, keep, got)


class _Plan:
    def __init__(self, shards, table):
        self.shards, self.table = shards, table
        self.whole, self.grad, self.got_a, self.half, self.gshard = {}, {}, {}, {}, {}
        self.got_b1, self.kept, self.pass_on, self.got_b2 = {}, {}, {}, {}
        x, y, c = _pos()
        me, jx, jy = 2 * x + y, 2 * (1 - x) + y, 2 * x + (1 - y)
        self.core = c
        self.core1 = c.reshape(1).astype(jnp.int32)
        zero = 0 * me
        self.idx_sums = jnp.stack([me, zero, jy, zero, me, zero + 1, jx, zero + 1]).astype(jnp.int32)
        self._w_in = None
        self.send, self.keep = {}, {}

    def w(self, n):
        if n != "w_in":
            return self.whole[n][0]
        if self._w_in is None:
            self._w_in = _slabs_to_kernel_cols(self.whole[n], name="relayout_w_in", jobs=self.jobs("relayout_w_in"))
        return self._w_in

    def g(self, n, a):
        self.grad[n] = a[None]

    def g_half(self, n, which, a):
        if which == "keep" and n in self.got_a:
            self.half[n] = _kernel_cols_to_slabs(a, name="relayout_d_in_keep", add=self.got_a[n])
        else:
            (self.send if which == "send" else self.keep)[n] = _kernel_cols_to_slabs(a, name="relayout_d_in_" + which)

    def jobs(self, tag):
        out = []
        for spec in self.table.get(tag, ()):
            out += getattr(self, "_" + spec[0])(*spec[1:])
        return out

    def run(self, name, jobs):
        if jobs:
            _call(lambda: None, jobs=jobs, name=name, out_shape=[], in_specs=[], out_specs=[])()

    def _gather(self, names):
        return [_GatherJob(names, self.shards, self.whole)]

    def _rs_a(self, names):
        mats = [MATS[n] for n in names]

        def build(srcs, dsts, news, send, recv):
            x, y, c = _pos()
            return [_remote(srcs[i] if names[i] in self.send else _half_of_whole(kind, srcs[i], 1 - c, r, cc), news[i],
                            send.at[i], recv.at[i], (x, y, 1 - c))
                    for i, (_, kind, g, r, cc, _, _) in enumerate(mats)]

        def done(dsts, news):
            self.got_a.update(zip(names, news))

        return [_SwapJob(build, len(names), srcs=[self.send.get(n, self.grad.get(n)) for n in names], done=done,
                         news=[jax.ShapeDtypeStruct(_half_shape(kind, g, r, c), BF16) for _, kind, g, r, c, _, _ in mats])]

    def _rs_b1(self, names):
        mats = [MATS[n] for n in names]
        for n in names:
            if n in self.half:
                continue
            if n in self.keep:
                self.half[n] = _pair_sum(MATS[n], self.core1, self.keep[n], self.got_a[n], whole=False)
            else:
                self.half[n] = _pair_sum(MATS[n], self.core1, self.grad[n], self.got_a[n])

        def build(srcs, dsts, news, send, recv):
            x, y, c = _pos()
            jx, jy, jd = 2 * (1 - x) + y, 2 * x + (1 - y), 2 * (1 - x) + (1 - y)
            nbx, nby = (1 - x, y, c), (x, 1 - y, c)
            cps = []
            for i, (_, kind, g, r, cc, _, _) in enumerate(mats):
                sub = lambda j, p: _sub_of_half(kind, srcs[i], j, p, r, cc)
                for k, (j, p, dev) in enumerate(((jx, 0, nbx), (jd, 0, nbx), (jy, 1, nby), (jd, 1, nby))):
                    cps.append(_remote(sub(j, p), news[i].at[k], send.at[4 * i + k], recv.at[4 * i + k], dev))
            return cps

        def done(dsts, news):
            self.got_b1.update(zip(names, news))

        return [_SwapJob(build, 4 * len(names), srcs=[self.half[n] for n in names], done=done,
                         news=[jax.ShapeDtypeStruct((4,) + _sub_shape(kind, r, c), BF16) for _, kind, g, r, c, _, _ in mats])]

    def _rs_b2(self, names):
        mats = [MATS[n] for n in names]
        for n in names:
            self.kept[n], self.pass_on[n] = _sub_sums(MATS[n], self.idx_sums, self.half[n], self.got_b1[n], name="sums_" + n)

        def build(srcs, dsts, news, send, recv):
            x, y, c = _pos()
            cps = []
            for i in range(len(mats)):
                cps.append(_remote(srcs[i].at[0], news[i].at[0], send.at[2 * i], recv.at[2 * i], (x, 1 - y, c)))
                cps.append(_remote(srcs[i].at[1], news[i].at[1], send.at[2 * i + 1], recv.at[2 * i + 1], (1 - x, y, c)))
            return cps

        def done(dsts, news):
            self.got_b2.update(zip(names, news))

        return [_SwapJob(build, 2 * len(names), srcs=[self.pass_on[n] for n in names], done=done,
                         news=[jax.ShapeDtypeStruct((2,) + _sub_shape(kind, r, c), BF16) for _, kind, g, r, c, _, _ in mats])]

    def _rs_c(self, names):
        mats = [MATS[n] for n in names]
        parts = [_shard_sum(MATS[n], self.core1, self.kept[n], self.got_b2[n]) for n in names]

        def build(srcs, dsts, news, send, recv):
            x, y, c = _pos()
            cps = []
            for i, (_, kind, g, r, cc, _, _) in enumerate(mats):
                mine = _half_of_shard(kind, dsts[i], c, r, cc)
                cps.append(_remote(mine, mine, send.at[i], recv.at[i], (x, y, 1 - c)))
            return cps

        def done(dsts, news):
            self.gshard.update(zip(names, dsts))

        return [_SwapJob(build, len(names), dsts=parts, done=done)]

    def finish(self, n):
        if n not in self.got_a:
            self.run("rs_a_" + n, self._rs_a((n,)))
        if n not in self.got_b1:
            self.run("rs_b1_" + n, self._rs_b1((n,)))
        if n not in self.got_b2:
            self.run("rs_b2_" + n, self._rs_b2((n,)))
        if n not in self.gshard:
            self.run("rs_c_" + n, self._rs_c((n,)))
        return self.gshard[n]


TABLE = {
    "gather_w_in": (("gather", ("w_in",)),),
    "relayout_w_in": (("gather", ("w_gate",)),),
    "mm_in": (("gather", ("w_up",)),),
    "attn_fwd": (("gather", ("w_attn_br", "w_ssd_br")),),
    "ssd_fwd": (("gather", ("w_o",)),),
    "swiglu_fwd": (("gather", ("w_down",)),),
    "mm_down": (("gather", ("w_ple_gate", "w_ple_proj")),),
    "mm_de": (("rs_a", ("w_ple_proj", "w_ple_gate")),),
    "mm_d_down": (("rs_b1", ("w_ple_proj", "w_ple_gate")),),
    "swiglu_bwd": (("rs_a", ("w_down",)), ("rs_b2", ("w_ple_proj", "w_ple_gate"))),
    "mm_d_gate": (("rs_b1", ("w_down",)),),
    "mm_d_up": (("rs_b2", ("w_down",)), ("rs_c", ("w_ple_proj", "w_ple_gate")), ("rs_a", ("w_gate",))),
    "mm_df_gate": (("rs_b1", ("w_gate",)), ("rs_a", ("w_up",)), ("rs_c", ("w_down",))),
    "mm_df_up": (("rs_b2", ("w_gate",)),),
    "norm_ffn_bwd": (("rs_c", ("w_gate",)),),
    "mm_dmerged": (("rs_a", ("w_o",)),),
    "mm_dyn": (("rs_a", ("w_attn_br", "w_ssd_br")),),
    "attn_bwd": (("rs_b1", ("w_up",)),),
    "gated_norm_bwd": (("rs_b2", ("w_up",)),),
    "ssd_bwd": (("rs_b1", ("w_o", "w_attn_br", "w_ssd_br")), ("rs_c", ("w_up",))),
    "conv_bwd": (("rs_b2", ("w_o", "w_attn_br", "w_ssd_br")),),
    "mm_d_in_send": (("rs_c", ("w_o", "w_attn_br", "w_ssd_br")),),
    "mm_d_in_keep": (("rs_a", ("w_in",)),),
    "mm_du": (("rs_b1", ("w_in",)),),
    "norm_mix_bwd": (("rs_b2", ("w_in",)),),
}


NDEV = 8


def _allreduce_small(v, *, name):
    rows = v.shape[0]

    def body(v_ref, o_ref, slots, send, recv):
        x, y, c = _pos()
        me = 4 * x + 2 * y + c
        slots[me] = v_ref[...]
        cps = []
        for k in range(1, NDEV):
            peer = (_flip(x, k & 4), _flip(y, k & 2), _flip(c, k & 1))
            cp = _remote(v_ref, slots.at[me], send.at[k - 1], recv.at[k - 1], peer)
            cp.start()
            cps.append(cp)
        for cp in cps:
            cp.wait()
        acc = slots[0]
        for s in range(1, NDEV):
            acc = acc + slots[s]
        o_ref[...] = acc

    return pl.pallas_call(
        body, name=name, out_shape=jax.ShapeDtypeStruct((rows, 128), F32),
        in_specs=[pl.BlockSpec(memory_space=pltpu.VMEM)], out_specs=pl.BlockSpec(memory_space=pltpu.VMEM),
        scratch_shapes=[pltpu.VMEM((NDEV, rows, 128), F32), pltpu.SemaphoreType.DMA((NDEV - 1,)),
                        pltpu.SemaphoreType.DMA((NDEV - 1,))],
    )(v)


def _adamw(w, g, m, v, *, name, tr=None, tc=None, jobs=()):
    r, c = w.shape
    tr = r if tr is None else tr
    c1 = 1.0 / (1.0 - B1 ** STEP)
    c2 = 1.0 / (1.0 - B2 ** STEP)

    def body(w_ref, g_ref, m_ref, v_ref, d_ref, mo_ref, vo_ref):
        gv = g_ref[...]
        mn = B1 * m_ref[...] + (1.0 - B1) * gv
        vn = B2 * v_ref[...] + (1.0 - B2) * (gv * gv)
        mo_ref[...] = mn
        vo_ref[...] = vn
        d_ref[...] = -LR * ((mn * c1) / (jnp.sqrt(vn * c2) + AEPS) + WD * w_ref[...])

    if tc is None:
        blk, grid = pl.BlockSpec((tr, c), lambda i: (i, 0)), (r // tr,)
    else:
        blk, grid = pl.BlockSpec((r, tc), lambda i: (0, i)), (c // tc,)
    o = jax.ShapeDtypeStruct((r, c), F32)
    return _call(
        body, jobs=jobs, name=name, out_shape=(o, o, o), grid=grid, in_specs=[blk] * 4, out_specs=(blk, blk, blk),
        compiler_params=_cp(("parallel",)),
    )(w, g, m, v)


WEIGHTS = ("g_mix", "w_in", "conv_w", "conv_b", "dt_bias", "a_log", "d_skip", "g_ssd", "sinks", "w_attn_br", "w_ssd_br",
           "w_o", "g_ffn", "w_gate", "w_up", "w_down", "g_ple", "w_ple_gate", "w_ple_proj", "g_final")
BIG = {
    "w_gate": 256, "w_up": 256, "w_down": 128, "w_ssd_br": 128, "w_o": 128, "w_ple_gate": 128, "w_attn_br": 256,
    "w_ple_proj": 256, "w_in": None,
}
SMALL = tuple(n for n in WEIGHTS if n not in BIG)


def _pack_small(parts):
    rows = []
    for a in parts:
        a = a.reshape(-1)
        rows.append(jnp.pad(a, (0, -a.shape[0] % 128)).reshape(-1, 128))
    out = jnp.concatenate(rows, axis=0)
    return jnp.pad(out, ((0, -out.shape[0] % 8), (0, 0)))


def _unpack_small(packed, shapes):
    out, r = [], 0
    for s in shapes:
        n = int(np.prod(s))
        nr = -(-n // 128)
        out.append(packed[r:r + nr].reshape(-1)[:n].reshape(s))
        r += nr
    return out


def kernel(x, p, positions, g_mix, w_in, conv_w, conv_b, dt_bias, a_log, d_skip, g_ssd, sinks, w_attn_br, w_ssd_br, w_o, g_ffn, w_gate, w_up, w_down, g_ple, w_ple_gate, w_ple_proj, g_final, loss_target, m_g_mix, m_w_in, m_conv_w, m_conv_b, m_dt_bias, m_a_log, m_d_skip, m_g_ssd, m_sinks, m_w_attn_br, m_w_ssd_br, m_w_o, m_g_ffn, m_w_gate, m_w_up, m_w_down, m_g_ple, m_w_ple_gate, m_w_ple_proj, m_g_final, v_g_mix, v_w_in, v_conv_w, v_conv_b, v_dt_bias, v_a_log, v_d_skip, v_g_ssd, v_sinks, v_w_attn_br, v_w_ssd_br, v_w_o, v_g_ffn, v_w_gate, v_w_up, v_w_down, v_g_ple, v_w_ple_gate, v_w_ple_proj, v_g_final):
    w = dict(zip(WEIGHTS, (g_mix, w_in, conv_w, conv_b, dt_bias, a_log, d_skip, g_ssd, sinks, w_attn_br, w_ssd_br, w_o,
                           g_ffn, w_gate, w_up, w_down, g_ple, w_ple_gate, w_ple_proj, g_final)))
    m = dict(zip(WEIGHTS, (m_g_mix, m_w_in, m_conv_w, m_conv_b, m_dt_bias, m_a_log, m_d_skip, m_g_ssd, m_sinks, m_w_attn_br,
                           m_w_ssd_br, m_w_o, m_g_ffn, m_w_gate, m_w_up, m_w_down, m_g_ple, m_w_ple_gate, m_w_ple_proj,
                           m_g_final)))
    v = dict(zip(WEIGHTS, (v_g_mix, v_w_in, v_conv_w, v_conv_b, v_dt_bias, v_a_log, v_d_skip, v_g_ssd, v_sinks, v_w_attn_br,
                           v_w_ssd_br, v_w_o, v_g_ffn, v_w_gate, v_w_up, v_w_down, v_g_ple, v_w_ple_gate, v_w_ple_proj,
                           v_g_final)))
    xi, yi, ci = _pos()
    chip = 2 * xi + yi
    t = x.shape[1]
    cshard = CONV // NCHIP

    shards = {n: w[n].astype(BF16) for n in MATS}
    shards["w_in"] = jnp.pad(shards["w_in"], ((0, 0), (0, 0), (0, SLAB_PAD - SLAB)))
    plan = _Plan(shards, TABLE)
    plan.run("gather_w_in", plan.jobs("gather_w_in"))
    placed = lax.dynamic_update_slice(jnp.zeros((CW, CONV), F32), w["conv_w"][0], (0, chip * cshard))
    conv_whole = _allreduce_small(jnp.where(ci == 0, placed, 0.0).reshape(-1, 128), name="gather_conv_w").reshape(CW, CONV)

    small = {n: w[n] for n in ("g_mix", "conv_b", "dt_bias", "a_log", "d_skip", "g_ssd", "sinks", "g_ffn", "g_ple", "g_final")}
    small["conv_w"] = conv_whole
    loss8, grad_x, gs = _local_step(x[0], p[0, 0], positions, loss_target[0], small, plan)

    order = ("g_mix", "conv_b", "dt_bias", "a_log", "d_skip", "g_ssd", "sinks", "g_ffn", "g_ple", "g_final", "conv_w")
    summed = _allreduce_small(_pack_small([loss8[0, :1]] + [gs[n] for n in order]), name="sum_small")
    parts = _unpack_small(summed, [(1,)] + [w[n].shape for n in order[:-1]] + [(CW, CONV)])
    loss = parts[0][0]
    grad = dict(zip(order, parts[1:]))
    grad["conv_w"] = lax.dynamic_slice(grad["conv_w"], (0, chip * cshard), (CW, cshard))[None]

    delta, new_m, new_v = {}, {}, {}
    for n, tr in BIG.items():
        grad[n] = plan.finish(n)[:, :, :w[n].shape[2]]
        if n == "w_in":
            d_, m_, v_ = _adamw(w[n][0].T, grad[n][0].T, m[n][0].T, v[n][0].T, tc=128, name="adamw_" + n)
            d_, m_, v_ = d_.T, m_.T, v_.T
        else:
            d_, m_, v_ = _adamw(w[n][0], grad[n][0], m[n][0], v[n][0], tr=tr, name="adamw_" + n)
        delta[n], new_m[n], new_v[n] = d_[None], m_[None], v_[None]
    shapes = [w[n].shape for n in SMALL]
    d_, m_, v_ = _adamw(_pack_small([w[n] for n in SMALL]), _pack_small([grad[n] for n in SMALL]),
                        _pack_small([m[n] for n in SMALL]), _pack_small([v[n] for n in SMALL]), tr=None, name="adamw_small")
    for n, a, b, c_ in zip(SMALL, _unpack_small(d_, shapes), _unpack_small(m_, shapes), _unpack_small(v_, shapes)):
        delta[n], new_m[n], new_v[n] = a, b, c_

    return (loss, grad_x[None], *[grad[n] for n in WEIGHTS], *[delta[n] for n in WEIGHTS],
            *[new_m[n] for n in WEIGHTS], *[new_v[n] for n in WEIGHTS])
```

```python
import functools

import jax
import jax.numpy as jnp
import numpy as np
from jax import lax
from jax.experimental import pallas as pl
from jax.experimental.pallas import tpu as pltpu

F32 = jnp.float32
BF16 = jnp.bfloat16
MESH = pl.DeviceIdType.MESH

D = 2048
HD = 64
NQH = 16
NKV = 4
QD = NQH * HD
KVD = NKV * HD
DI = 2048
NH = 32
NG = 4
NS = 128
CW = 4
L = 128
CONV = DI + 2 * NG * NS
FFN = 5632
PLE = 256
IN_DIM = QD + 2 * KVD + DI + CONV + NH + 2 * D
EPS = 1e-6
SSM_EPS = 1e-5
ROPE_THETA = 10000.0
LR, B1, B2, AEPS, WD, STEP = 0.001, 0.9, 0.999, 1e-08, 0.01, 10

O_GA, O_GS, O_Z, O_XBC, O_Q, O_K, O_V, O_DT = 0, 2048, 4096, 6144, 9216, 10240, 10496, 10752
DT_PAD = 512
NP = O_DT + DT_PAD
R_Q, R_K, R_V, R_Z, R_XBC, R_DT, R_GA, R_GS = 0, 1024, 1280, 1536, 3584, 6656, 6688, 8736

NCHIP = 4
VMEM_LIMIT = 52 * 1024 * 1024
NEG = -1e30


def _cp(sem=None):
    return pltpu.CompilerParams(dimension_semantics=sem, vmem_limit_bytes=VMEM_LIMIT)


def _dot(a, b):
    return lax.dot_general(a, b, (((1,), (0,)), ((), ())), preferred_element_type=F32)


def _dot_nt(a, b):
    return lax.dot_general(a, b, (((1,), (1,)), ((), ())), preferred_element_type=F32)


def _dot_tn(a, b):
    return lax.dot_general(a, b, (((0,), (0,)), ((), ())), preferred_element_type=F32)


def _sigmoid(x):
    return 1.0 / (1.0 + jnp.exp(-x))


def _bf16_dot(dot, da, db):
    @jax.custom_vjp
    def f(a, b):
        return dot(a.astype(BF16), b.astype(BF16))

    def fwd(a, b):
        return f(a, b), (a.astype(BF16), b.astype(BF16))

    def bwd(res, g):
        a, b = res
        g = g.astype(BF16)
        return da(g, a, b), db(g, a, b)

    f.defvjp(fwd, bwd)
    return f


_bdot = _bf16_dot(_dot, lambda g, a, b: _dot_nt(g, b), lambda g, a, b: _dot_tn(a, g))
_bdot_nt = _bf16_dot(_dot_nt, lambda g, a, b: _dot(g, b), lambda g, a, b: _dot_tn(g, a))
_bdot_tn = _bf16_dot(_dot_tn, lambda g, a, b: _dot_nt(b, g), lambda g, a, b: _dot(a, g))


ANY = pl.BlockSpec(memory_space=pl.ANY)


class _Job:
    srcs, dsts, news, scratch = (), (), (), ()
    has_mid = False

    def start(self, srcs, dsts, news, sems):
        raise NotImplementedError

    def mid(self, srcs, dsts, news, sems):
        pass

    def late(self, srcs, dsts, news, sems):
        pass

    def finish(self, srcs, dsts, news, sems):
        raise NotImplementedError

    def done(self, dsts, news):
        pass


def _call(body, *, jobs=(), name, out_shape, in_specs, out_specs, grid=(), scratch_shapes=(), compiler_params=None,
          aliases=None):
    jobs = [j for j in jobs if j is not None]
    aliases = dict(aliases or {})
    if not jobs:
        return pl.pallas_call(body, name=name, out_shape=out_shape, in_specs=in_specs, out_specs=out_specs, grid=grid,
                              scratch_shapes=scratch_shapes, compiler_params=compiler_params,
                              input_output_aliases=aliases)
    single = not isinstance(out_shape, (tuple, list))
    outs = [out_shape] if single else list(out_shape)
    ospecs = [out_specs] if single else list(out_specs)
    n_in, n_out, n_scr = len(in_specs), len(outs), len(scratch_shapes)
    srcs = [a for j in jobs for a in j.srcs]
    dsts = [a for j in jobs for a in j.dsts]
    news = [a for j in jobs for a in j.news]
    sems = [a for j in jobs for a in j.scratch]

    def wrapped(*refs):
        pos = n_in + len(srcs) + len(dsts)
        ins, jsrc = refs[:n_in], refs[n_in:n_in + len(srcs)]
        o_refs = refs[pos:pos + n_out]
        pos += n_out
        jdst, jnew = refs[pos:pos + len(dsts)], refs[pos + len(dsts):pos + len(dsts) + len(news)]
        pos += len(dsts) + len(news)
        scr, jsem = refs[pos:pos + n_scr], refs[pos + n_scr:]

        def run(which):
            a = b = c = d = 0
            for j in jobs:
                getattr(j, which)(jsrc[a:a + len(j.srcs)], jdst[b:b + len(j.dsts)], jnew[c:c + len(j.news)],
                                  jsem[d:d + len(j.scratch)])
                a, b, c, d = a + len(j.srcs), b + len(j.dsts), c + len(j.news), d + len(j.scratch)

        if not grid:
            run("start")
            run("mid")
            run("late")
            body(*ins, *o_refs, *scr)
            run("finish")
            return
        step = functools.reduce(lambda acc, a: acc * grid[a] + pl.program_id(a), range(len(grid)), 0)
        steps = int(np.prod(grid))
        pl.when(step == 0)(lambda: run("start"))
        if any(j.has_mid for j in jobs):
            pl.when(step == steps // 3)(lambda: run("mid"))
            pl.when(step == (2 * steps) // 3)(lambda: run("late"))
        body(*ins, *o_refs, *scr)
        pl.when(step == steps - 1)(lambda: run("finish"))

    call = pl.pallas_call(
        wrapped, name=name,
        out_shape=outs + [jax.ShapeDtypeStruct(a.shape, a.dtype) for a in dsts] + news,
        in_specs=list(in_specs) + [ANY] * (len(srcs) + len(dsts)),
        out_specs=ospecs + [ANY] * (len(dsts) + len(news)),
        grid=grid, scratch_shapes=list(scratch_shapes) + sems,
        input_output_aliases={**aliases, **{n_in + len(srcs) + i: n_out + i for i in range(len(dsts))}},
        compiler_params=_cp(("arbitrary",) * len(grid) if grid else None))

    def run_call(*args):
        res = call(*args, *srcs, *dsts)
        b, c = n_out, n_out + len(dsts)
        for j in jobs:
            j.done(res[b:b + len(j.dsts)], res[c:c + len(j.news)])
            b, c = b + len(j.dsts), c + len(j.news)
        return res[0] if single else tuple(res[:n_out])

    return run_call


def _matmul(a, b, *, ta=False, tb=False, out_dtype=F32, add=None, tm, tn, tk, name, jobs=()):
    k, m = a.shape if ta else a.shape[::-1]
    n = b.shape[0] if tb else b.shape[1]
    assert (b.shape[1] if tb else b.shape[0]) == k and not (ta and tb)
    assert m % tm == 0 and n % tn == 0 and k % tk == 0, (name, a.shape, b.shape)
    nk = k // tk
    has_add = add is not None

    def body(*refs):
        a_ref, b_ref = refs[0], refs[1]
        add_ref = refs[2] if has_add else None
        o_ref = refs[3] if has_add else refs[2]
        av = a_ref[...].astype(BF16)
        bv = b_ref[...].astype(BF16)
        part = _dot_tn(av, bv) if ta else _dot_nt(av, bv) if tb else _dot(av, bv)

        def finish(r):
            if has_add:
                r = r + add_ref[...]
            o_ref[...] = r.astype(out_dtype)

        if nk == 1:
            finish(part)
        elif out_dtype == F32:
            kk = pl.program_id(2)
            pl.when(kk == 0)(lambda: finish(part))

            @pl.when(kk > 0)
            def _():
                o_ref[...] += part
        else:
            acc_ref = refs[-1]
            kk = pl.program_id(2)

            @pl.when(kk == 0)
            def _():
                acc_ref[...] = part

            @pl.when(kk > 0)
            def _():
                acc_ref[...] += part

            @pl.when(kk == nk - 1)
            def _():
                finish(acc_ref[...])

    in_specs = [pl.BlockSpec((tk, tm), lambda i, j, kk: (kk, i)) if ta else pl.BlockSpec((tm, tk), lambda i, j, kk: (i, kk)),
                pl.BlockSpec((tn, tk), lambda i, j, kk: (j, kk)) if tb
                else pl.BlockSpec((tk, tn), lambda i, j, kk: (kk, j))]
    args = [a, b]
    if has_add:
        in_specs.append(pl.BlockSpec((tm, tn), lambda i, j, kk: (i, j)))
        args.append(add)
    return _call(
        body, jobs=jobs, name=name,
        out_shape=jax.ShapeDtypeStruct((m, n), out_dtype),
        grid=(m // tm, n // tn, nk),
        in_specs=in_specs,
        out_specs=pl.BlockSpec((tm, tn), lambda i, j, kk: (i, j)),
        scratch_shapes=[pltpu.VMEM((tm, tn), F32)] if nk > 1 and out_dtype != F32 else [],
        compiler_params=_cp(("parallel", "parallel", "arbitrary")),
    )(*args)


ROWS = 256


def _rmsnorm_fwd(x, g, *, name):
    t, d = x.shape

    def body(x_ref, g_ref, o_ref):
        xv = x_ref[...]
        r = lax.rsqrt(jnp.mean(xv * xv, axis=-1, keepdims=True) + EPS)
        o_ref[...] = (xv * r * g_ref[...]).astype(BF16)

    return pl.pallas_call(
        body, name=name, out_shape=jax.ShapeDtypeStruct((t, d), BF16), grid=(t // ROWS,),
        in_specs=[pl.BlockSpec((ROWS, d), lambda i: (i, 0)), pl.BlockSpec((1, d), lambda i: (0, 0))],
        out_specs=pl.BlockSpec((ROWS, d), lambda i: (i, 0)), compiler_params=_cp(("parallel",)),
    )(x, g)


def _rmsnorm_bwd(x, g, dy, dres, *, name, jobs=()):
    t, d = x.shape

    def body(x_ref, g_ref, dy_ref, dres_ref, dx_ref, dxb_ref, dg_ref):
        xv = x_ref[...]
        r = lax.rsqrt(jnp.mean(xv * xv, axis=-1, keepdims=True) + EPS)
        xh = xv * r
        dyv = dy_ref[...]
        dxh = dyv * g_ref[...]
        dx = r * (dxh - xh * jnp.mean(dxh * xh, axis=-1, keepdims=True))
        tot = dres_ref[...] + dx
        dx_ref[...] = tot
        dxb_ref[...] = tot.astype(BF16)

        @pl.when(pl.program_id(0) == 0)
        def _():
            dg_ref[...] = jnp.zeros_like(dg_ref)

        dg_ref[...] += jnp.broadcast_to(jnp.sum(dyv * xh, axis=0, keepdims=True), dg_ref.shape)

    row = pl.BlockSpec((ROWS, d), lambda i: (i, 0))
    return _call(
        body, jobs=jobs, name=name,
        out_shape=(jax.ShapeDtypeStruct((t, d), F32), jax.ShapeDtypeStruct((t, d), BF16),
                   jax.ShapeDtypeStruct((8, d), F32)),
        grid=(t // ROWS,),
        in_specs=[row, pl.BlockSpec((1, d), lambda i: (0, 0)), row, row],
        out_specs=(row, row, pl.BlockSpec((8, d), lambda i: (0, 0))),
        compiler_params=_cp(("arbitrary",)),
    )(x, g, dy, dres)


def _final(h2, pgl, pp, target, g_final, *, name):
    t, d = h2.shape

    def body(h2_ref, pgl_ref, pp_ref, tg_ref, g_ref, dh3_ref, dpgl_ref, dpp_ref, loss_ref, dg_ref):
        s = _sigmoid(pgl_ref[...])
        ppv = pp_ref[...]
        h3 = h2_ref[...] + s * ppv
        r = lax.rsqrt(jnp.mean(h3 * h3, axis=-1, keepdims=True) + EPS)
        xh = h3 * r
        gv = g_ref[...]
        err = xh * gv - tg_ref[...]
        dyv = err * (1.0 / d)
        dxh = dyv * gv
        dh3 = r * (dxh - xh * jnp.mean(dxh * xh, axis=-1, keepdims=True))
        dh3_ref[...] = dh3
        dpp_ref[...] = (dh3 * s).astype(BF16)
        dpgl_ref[...] = (dh3 * ppv * s * (1.0 - s)).astype(BF16)

        @pl.when(pl.program_id(0) == 0)
        def _():
            loss_ref[...] = jnp.zeros_like(loss_ref)
            dg_ref[...] = jnp.zeros_like(dg_ref)

        part = 0.5 * jnp.sum(jnp.mean(err * err, axis=-1, keepdims=True), axis=0, keepdims=True)
        loss_ref[...] += jnp.broadcast_to(part, loss_ref.shape)
        dg_ref[...] += jnp.broadcast_to(jnp.sum(dyv * xh, axis=0, keepdims=True), dg_ref.shape)

    row = pl.BlockSpec((ROWS, d), lambda i: (i, 0))
    return pl.pallas_call(
        body, name=name,
        out_shape=(jax.ShapeDtypeStruct((t, d), F32), jax.ShapeDtypeStruct((t, d), BF16),
                   jax.ShapeDtypeStruct((t, d), BF16), jax.ShapeDtypeStruct((8, 128), F32),
                   jax.ShapeDtypeStruct((8, d), F32)),
        grid=(t // ROWS,),
        in_specs=[row, row, row, row, pl.BlockSpec((1, d), lambda i: (0, 0))],
        out_specs=(row, row, row, pl.BlockSpec((8, 128), lambda i: (0, 0)), pl.BlockSpec((8, d), lambda i: (0, 0))),
        compiler_params=_cp(("arbitrary",)),
    )(h2, pgl, pp, target, g_final)


def _merge_fwd(proj, out_a, out_s, *, name):
    t = proj.shape[0]

    def body(ga_ref, gs_ref, a_ref, s_ref, o_ref):
        o_ref[...] = (_sigmoid(ga_ref[...]) * a_ref[...] + _sigmoid(gs_ref[...]) * s_ref[...]).astype(BF16)

    row = pl.BlockSpec((ROWS, D), lambda i: (i, 0))
    return pl.pallas_call(
        body, name=name, out_shape=jax.ShapeDtypeStruct((t, D), BF16), grid=(t // ROWS,),
        in_specs=[pl.BlockSpec((ROWS, D), lambda i: (i, O_GA // D)), pl.BlockSpec((ROWS, D), lambda i: (i, O_GS // D)),
                  row, row],
        out_specs=row, compiler_params=_cp(("parallel",)),
    )(proj, proj, out_a, out_s)


def _merge_bwd(proj, out_a, out_s, dmerged, *, name):
    t = proj.shape[0]
    assert O_GA == 0 and O_GS == D

    def body(ga_ref, gs_ref, a_ref, s_ref, dm_ref, da_ref, ds_ref, dp_ref):
        sa = _sigmoid(ga_ref[...])
        ss = _sigmoid(gs_ref[...])
        dm = dm_ref[...]
        da_ref[...] = (dm * sa).astype(BF16)
        ds_ref[...] = (dm * ss).astype(BF16)
        dp_ref[:, :D] = (dm * a_ref[...] * sa * (1.0 - sa)).astype(BF16)
        dp_ref[:, D:] = (dm * s_ref[...] * ss * (1.0 - ss)).astype(BF16)

    row = pl.BlockSpec((ROWS, D), lambda i: (i, 0))
    o = jax.ShapeDtypeStruct((t, D), BF16)
    return pl.pallas_call(
        body, name=name, out_shape=(o, o, jax.ShapeDtypeStruct((t, NP), BF16)), grid=(t // ROWS,),
        in_specs=[pl.BlockSpec((ROWS, D), lambda i: (i, O_GA // D)), pl.BlockSpec((ROWS, D), lambda i: (i, O_GS // D)),
                  row, row, row],
        out_specs=(row, row, pl.BlockSpec((ROWS, 2 * D), lambda i: (i, 0))), compiler_params=_cp(("parallel",)),
    )(proj, proj, out_a, out_s, dmerged)


def _swiglu_fwd(f, w_gate, w_up, *, name, tn=512, jobs=()):
    t, d = f.shape
    n = w_gate.shape[1]

    def body(f_ref, wg_ref, wu_ref, g_ref, u_ref, a_ref):
        fv = f_ref[...]
        g = _dot(fv, wg_ref[...])
        u = _dot(fv, wu_ref[...])
        g_ref[...] = g.astype(BF16)
        u_ref[...] = u.astype(BF16)
        a_ref[...] = (g * _sigmoid(g) * u).astype(BF16)

    col = pl.BlockSpec((t, tn), lambda j: (0, j))
    wcol = pl.BlockSpec((d, tn), lambda j: (0, j))
    return _call(
        body, jobs=jobs, name=name,
        out_shape=(jax.ShapeDtypeStruct((t, n), BF16), jax.ShapeDtypeStruct((t, n), BF16),
                   jax.ShapeDtypeStruct((t, n), BF16)),
        grid=(n // tn,),
        in_specs=[pl.BlockSpec((t, d), lambda j: (0, 0)), wcol, wcol],
        out_specs=(col, col, col), compiler_params=_cp(("parallel",)),
    )(f, w_gate, w_up)


def _swiglu_bwd(dh, w_down, gate, up, *, name, tn=512, jobs=()):
    t, d = dh.shape
    n = w_down.shape[0]

    def body(dh_ref, w_ref, g_ref, u_ref, dg_ref, du_ref):
        da = _dot_nt(dh_ref[...], w_ref[...])
        g = g_ref[...].astype(F32)
        s = _sigmoid(g)
        du_ref[...] = (da * g * s).astype(BF16)
        dg_ref[...] = (da * u_ref[...].astype(F32) * s * (1.0 + g * (1.0 - s))).astype(BF16)

    col = pl.BlockSpec((t, tn), lambda j: (0, j))
    o = jax.ShapeDtypeStruct((t, n), BF16)
    return _call(
        body, jobs=jobs, name=name, out_shape=(o, o), grid=(n // tn,),
        in_specs=[pl.BlockSpec((t, d), lambda j: (0, 0)), pl.BlockSpec((tn, d), lambda j: (j, 0)), col, col],
        out_specs=(col, col), compiler_params=_cp(("parallel",)),
    )(dh, w_down, gate, up)


def _gated_norm_fwd(y_pre, proj, g_ssd, *, name):
    t = y_pre.shape[0]

    def body(y_ref, z_ref, g_ref, o_ref):
        z = z_ref[...]
        v = y_ref[...] * z * _sigmoid(z)
        r = lax.rsqrt(jnp.mean(v * v, axis=-1, keepdims=True) + SSM_EPS)
        o_ref[...] = (v * r * g_ref[...]).astype(BF16)

    row = pl.BlockSpec((ROWS, DI), lambda i: (i, 0))
    return pl.pallas_call(
        body, name=name, out_shape=jax.ShapeDtypeStruct((t, DI), BF16), grid=(t // ROWS,),
        in_specs=[row, pl.BlockSpec((ROWS, DI), lambda i: (i, O_Z // DI)), pl.BlockSpec((1, DI), lambda i: (0, 0))],
        out_specs=row, compiler_params=_cp(("parallel",)),
    )(y_pre, proj, g_ssd)


def _gated_norm_bwd(y_pre, proj, g_ssd, dyn, dproj, *, name, jobs=()):
    t = y_pre.shape[0]

    def body(y_ref, z_ref, g_ref, dyn_ref, _, dy_ref, dz_ref, dg_ref):
        z = z_ref[...]
        s = _sigmoid(z)
        sz = z * s
        yv = y_ref[...]
        v = yv * sz
        r = lax.rsqrt(jnp.mean(v * v, axis=-1, keepdims=True) + SSM_EPS)
        vh = v * r
        dn = dyn_ref[...]
        dvh = dn * g_ref[...]
        dv = r * (dvh - vh * jnp.mean(dvh * vh, axis=-1, keepdims=True))
        dy_ref[...] = dv * sz
        dz_ref[...] = (dv * yv * s * (1.0 + z * (1.0 - s))).astype(BF16)

        @pl.when(pl.program_id(0) == 0)
        def _():
            dg_ref[...] = jnp.zeros_like(dg_ref)

        dg_ref[...] += jnp.broadcast_to(jnp.sum(dn * vh, axis=0, keepdims=True), dg_ref.shape)

    row = pl.BlockSpec((ROWS, DI), lambda i: (i, 0))
    return _call(
        body, jobs=jobs, name=name,
        out_shape=(jax.ShapeDtypeStruct((t, DI), F32), jax.ShapeDtypeStruct(dproj.shape, BF16),
                   jax.ShapeDtypeStruct((8, DI), F32)),
        grid=(t // ROWS,),
        in_specs=[row, pl.BlockSpec((ROWS, DI), lambda i: (i, O_Z // DI)), pl.BlockSpec((1, DI), lambda i: (0, 0)), row, ANY],
        out_specs=(row, pl.BlockSpec((ROWS, DI), lambda i: (i, O_Z // DI)), pl.BlockSpec((8, DI), lambda i: (0, 0))),
        compiler_params=_cp(("arbitrary",)), aliases={4: 1},
    )(y_pre, proj, g_ssd, dyn, dproj)


CONV_TC = 512


def _shift_down(x, s, row):
    if s == 0:
        return x
    return jnp.where(row >= s, pltpu.roll(x, s, 0), 0.0)


def _shift_up(x, s, row, t):
    if s == 0:
        return x
    return jnp.where(row < t - s, pltpu.roll(x, t - s, 0), 0.0)


def _conv_fwd(proj, conv_w, conv_b, *, name):
    t = proj.shape[0]

    def body(x_ref, w_ref, b_ref, o_ref):
        x = x_ref[...]
        row = lax.broadcasted_iota(jnp.int32, x.shape, 0)
        pre = jnp.broadcast_to(b_ref[...], x.shape)
        for k in range(CW):
            pre = pre + w_ref[k:k + 1, :] * _shift_down(x, CW - 1 - k, row)
        o_ref[...] = pre * _sigmoid(pre)

    return pl.pallas_call(
        body, name=name, out_shape=jax.ShapeDtypeStruct((t, CONV), F32), grid=(CONV // CONV_TC,),
        in_specs=[pl.BlockSpec((t, CONV_TC), lambda j: (0, O_XBC // CONV_TC + j)),
                  pl.BlockSpec((CW, CONV_TC), lambda j: (0, j)), pl.BlockSpec((1, CONV_TC), lambda j: (0, j))],
        out_specs=pl.BlockSpec((t, CONV_TC), lambda j: (0, j)), compiler_params=_cp(("parallel",)),
    )(proj, conv_w, conv_b)


def _conv_bwd(proj, conv_w, conv_b, dxs, db, dc, dproj, *, name, jobs=()):
    t = proj.shape[0]
    nx = DI // CONV_TC
    assert NG * NS == CONV_TC

    def body(x_ref, w_ref, b_ref, dxs_ref, db_ref, dc_ref, _, dx_ref, dw_ref, dbias_ref):
        j = pl.program_id(0)
        x = x_ref[...]
        row = lax.broadcasted_iota(jnp.int32, x.shape, 0)
        xs = [_shift_down(x, CW - 1 - k, row) for k in range(CW)]
        pre = jnp.broadcast_to(b_ref[...], x.shape)
        for k in range(CW):
            pre = pre + w_ref[k:k + 1, :] * xs[k]
        s = _sigmoid(pre)
        da = jnp.where(j < nx, dxs_ref[...], jnp.where(j == nx, db_ref[...], dc_ref[...]))
        dpre = da * s * (1.0 + pre * (1.0 - s))
        dx = jnp.zeros_like(x)
        row8 = lax.broadcasted_iota(jnp.int32, dw_ref.shape, 0)
        dw = jnp.zeros(dw_ref.shape, F32)
        for k in range(CW):
            dx = dx + w_ref[k:k + 1, :] * _shift_up(dpre, CW - 1 - k, row, t)
            dw = dw + jnp.where(row8 == k, jnp.sum(dpre * xs[k], axis=0, keepdims=True), 0.0)
        dx_ref[...] = dx.astype(BF16)
        dw_ref[...] = dw
        dbias_ref[...] = jnp.broadcast_to(jnp.sum(dpre, axis=0, keepdims=True), dbias_ref.shape)

    col8 = pl.BlockSpec((8, CONV_TC), lambda j: (0, j))
    xbc = pl.BlockSpec((t, CONV_TC), lambda j: (0, O_XBC // CONV_TC + j))
    whole = pl.BlockSpec((t, CONV_TC), lambda j: (0, 0))
    return _call(
        body, jobs=jobs, name=name,
        out_shape=(jax.ShapeDtypeStruct(dproj.shape, BF16), jax.ShapeDtypeStruct((8, CONV), F32),
                   jax.ShapeDtypeStruct((8, CONV), F32)),
        grid=(CONV // CONV_TC,),
        in_specs=[xbc, pl.BlockSpec((CW, CONV_TC), lambda j: (0, j)), pl.BlockSpec((1, CONV_TC), lambda j: (0, j)),
                  pl.BlockSpec((t, CONV_TC), lambda j: (0, jnp.minimum(j, nx - 1))), whole, whole, ANY],
        out_specs=(xbc, col8, col8),
        compiler_params=_cp(("arbitrary",)), aliases={6: 0},
    )(proj, conv_w, conv_b, dxs, db, dc, dproj)


def _rope_tables(positions, t):
    half = HD // 2
    inv_freq = ROPE_THETA ** (-jnp.arange(half, dtype=F32) * 2.0 / HD)
    ang = positions.reshape(t).astype(F32)[:, None] * inv_freq
    cos, sin = jnp.cos(ang), jnp.sin(ang)
    return jnp.concatenate([cos] * 4, axis=1), jnp.concatenate([-sin, sin] * 2, axis=1)


def _lane_consts():
    lane = lax.broadcasted_iota(jnp.int32, (L, 128), 1)
    return lane, (lane % HD) < (HD // 2), lane < HD


def _rope(tv, cos, sin, lo):
    return tv * cos + jnp.where(lo, pltpu.roll(tv, 128 - HD // 2, 1), pltpu.roll(tv, HD // 2, 1)) * sin


def _rope_t(dv, cos, sin, lo):
    ds = dv * sin
    return dv * cos + jnp.where(lo, pltpu.roll(ds, 128 - HD // 2, 1), pltpu.roll(ds, HD // 2, 1))


def _placed(chunk, g, half0):
    own = jnp.where(half0 if g % 2 == 0 else jnp.logical_not(half0), chunk, 0.0)
    other = pltpu.roll(own, HD, 1)
    return (own, other) if g % 2 == 0 else (other, own)


def _unplace(acc, hf, g, half0):
    v = jnp.where(half0 if hf == 0 else jnp.logical_not(half0), acc, 0.0)
    return v if hf == g % 2 else pltpu.roll(v, HD, 1)


def _attn_fwd(proj, cos, sin, sinks, *, name, jobs=()):
    t = proj.shape[0]
    nb = t // L
    scale = HD ** -0.5

    def body(sink_ref, q_ref, kc_ref, kp_ref, vc_ref, vp_ref, cc_ref, sc_ref, cp_ref, sp_ref, o_ref, lse_ref):
        i = pl.program_id(0)
        lane, lo, half0 = _lane_consts()
        cos_c, sin_c, cos_p, sin_p = cc_ref[...], sc_ref[...], cp_ref[...], sp_ref[...]
        row = lax.broadcasted_iota(jnp.int32, (L, 2 * L), 0)
        col = lax.broadcasted_iota(jnp.int32, (L, 2 * L), 1)
        valid = jnp.logical_or(jnp.logical_and(jnp.logical_and(col < L, col > row), i > 0),
                               jnp.logical_and(col >= L, col - L <= row))
        kc = [_rope(kc_ref[:, 128 * m:128 * (m + 1)], cos_c, sin_c, lo) for m in range(2)]
        kp = [_rope(kp_ref[:, 128 * m:128 * (m + 1)], cos_p, sin_p, lo) for m in range(2)]
        lse_acc = jnp.zeros((L, 128), F32)
        outs = [jnp.zeros((L, 128), F32) for _ in range(QD // 128)]
        qs = [(_rope(q_ref[:, 128 * ch:128 * (ch + 1)], cos_c, sin_c, lo) * scale).astype(BF16) for ch in range(QD // 128)]
        both = lambda prev, cur, g: [jnp.concatenate([a, b], axis=0).astype(BF16)
                                     for a, b in zip(_placed(prev, g, half0), _placed(cur, g, half0))]
        for g in range(NKV):
            sl = slice(128 * (g // 2), 128 * (g // 2 + 1))
            kv = both(kp[g // 2], kc[g // 2], g)
            vv = both(vp_ref[:, sl], vc_ref[:, sl], g)
            for r in range(NQH // NKV):
                h = g * (NQH // NKV) + r
                ch, hf = h // 2, h % 2
                s = jnp.where(valid, _dot_nt(qs[ch], kv[hf]), NEG)
                sink = sink_ref[0, h]
                mx = jnp.maximum(jnp.max(s, axis=-1, keepdims=True), sink)
                e = jnp.exp(s - mx)
                den = jnp.sum(e, axis=-1, keepdims=True) + jnp.exp(sink - mx)
                outs[ch] = outs[ch] + _dot((e * (1.0 / den)).astype(BF16), vv[hf])
                lse_acc = jnp.where(lane == h, mx + jnp.log(den), lse_acc)
        for ch in range(QD // 128):
            o_ref[:, 128 * ch:128 * (ch + 1)] = outs[ch].astype(BF16)
        lse_ref[...] = lse_acc

    prev = lambda i: jnp.maximum(i - 1, 0)
    tab_c = pl.BlockSpec((L, 128), lambda i: (i, 0))
    tab_p = pl.BlockSpec((L, 128), lambda i: (prev(i), 0))
    return _call(
        body, jobs=jobs, name=name,
        out_shape=(jax.ShapeDtypeStruct((t, QD), BF16), jax.ShapeDtypeStruct((t, 128), F32)),
        grid=(nb,),
        in_specs=[pl.BlockSpec(memory_space=pltpu.SMEM),
                  pl.BlockSpec((L, QD), lambda i: (i, O_Q // QD)),
                  pl.BlockSpec((L, KVD), lambda i: (i, O_K // KVD)), pl.BlockSpec((L, KVD), lambda i: (prev(i), O_K // KVD)),
                  pl.BlockSpec((L, KVD), lambda i: (i, O_V // KVD)), pl.BlockSpec((L, KVD), lambda i: (prev(i), O_V // KVD)),
                  tab_c, tab_c, tab_p, tab_p],
        out_specs=(pl.BlockSpec((L, QD), lambda i: (i, 0)), pl.BlockSpec((L, 128), lambda i: (i, 0))),
        compiler_params=_cp(("parallel",)),
    )(sinks, proj, proj, proj, proj, proj, cos, sin, cos, sin)


def _attn_bwd(proj, cos, sin, sinks, attn, lse, dattn, dproj, *, name, jobs=()):
    t = proj.shape[0]
    nb = t // L
    scale = HD ** -0.5

    def body(sink_ref, qi_ref, qn_ref, kc_ref, kp_ref, vc_ref, vp_ref, doi_ref, don_ref, oi_ref, on_ref,
             lsei_ref, lsen_ref, cc_ref, sc_ref, cp_ref, sp_ref, cn_ref, sn_ref, _, dqkv_ref, dsk_ref):
        i = pl.program_id(0)
        lane, lo, half0 = _lane_consts()
        half1 = jnp.logical_not(half0)
        cos_c, sin_c = cc_ref[...], sc_ref[...]
        row = lax.broadcasted_iota(jnp.int32, (L, 2 * L), 0)
        col = lax.broadcasted_iota(jnp.int32, (L, 2 * L), 1)
        valid = jnp.logical_or(jnp.logical_and(jnp.logical_and(col < L, col > row), i > 0),
                               jnp.logical_and(col >= L, col - L <= row))
        m_next = jnp.logical_and(col[:, :L] > row[:, :L], i < nb - 1)
        kc = [_rope(kc_ref[:, 128 * m:128 * (m + 1)], cos_c, sin_c, lo) for m in range(2)]
        kp = [_rope(kp_ref[:, 128 * m:128 * (m + 1)], cp_ref[...], sp_ref[...], lo) for m in range(2)]
        lse_i, lse_n = lsei_ref[...], lsen_ref[...]
        dk_acc = [jnp.zeros((L, 128), F32) for _ in range(2)]
        dv_acc = [jnp.zeros((L, 128), F32) for _ in range(2)]
        dsk_acc = jnp.zeros((1, 128), F32)
        lane1 = lax.broadcasted_iota(jnp.int32, (1, 128), 1)
        both = lambda prev, cur, g: [jnp.concatenate([a, b], axis=0).astype(BF16)
                                     for a, b in zip(_placed(prev, g, half0), _placed(cur, g, half0))]
        kvs = [both(kp[g // 2], kc[g // 2], g) for g in range(NKV)]
        vvs = [both(vp_ref[:, 128 * (g // 2):128 * (g // 2 + 1)], vc_ref[:, 128 * (g // 2):128 * (g // 2 + 1)], g)
               for g in range(NKV)]
        for ch in range(QD // 128):
            sl = slice(128 * ch, 128 * (ch + 1))
            q_i = (_rope(qi_ref[:, sl], cos_c, sin_c, lo) * scale).astype(BF16)
            q_n = (_rope(qn_ref[:, sl], cn_ref[...], sn_ref[...], lo) * scale).astype(BF16)
            q_in = jnp.concatenate([q_i, q_n], axis=0)
            do_i, do_n = doi_ref[:, sl], don_ref[:, sl]
            do_ib, do_nb = do_i.astype(BF16), do_n.astype(BF16)
            do_in = jnp.concatenate([do_ib, do_nb], axis=0)
            od_i = do_i * oi_ref[:, sl].astype(F32)
            od_n = do_n * on_ref[:, sl].astype(F32)
            dq_ch = jnp.zeros((L, 128), F32)
            for hf in range(2):
                h = 2 * ch + hf
                g = h // (NQH // NKV)
                hm = half0 if hf == 0 else half1
                kv, vv = kvs[g][hf], vvs[g][hf]
                kcv, vcv = kv[L:], vv[L:]
                dl_i = jnp.sum(jnp.where(hm, od_i, 0.0), axis=-1, keepdims=True)
                dl_n = jnp.sum(jnp.where(hm, od_n, 0.0), axis=-1, keepdims=True)
                ls_i = jnp.sum(jnp.where(lane == h, lse_i, 0.0), axis=-1, keepdims=True)
                ls_n = jnp.sum(jnp.where(lane == h, lse_n, 0.0), axis=-1, keepdims=True)
                p = jnp.where(valid, jnp.exp(_dot_nt(q_i, kv) - ls_i), 0.0)
                ds = (p * (_dot_nt(do_ib, vv) - dl_i)).astype(BF16)
                dq_ch = dq_ch + jnp.where(hm, _dot(ds, kv) * scale, 0.0)
                sink = sink_ref[0, h]
                dsk = -jnp.sum(jnp.exp(sink - ls_i) * dl_i, axis=0, keepdims=True)
                dsk_acc = dsk_acc + jnp.where(lane1 == h, dsk, 0.0)
                p_n = jnp.where(m_next, jnp.exp(_dot_nt(q_n, kcv) - ls_n), 0.0)
                ds_n = (p_n * (_dot_nt(do_nb, vcv) - dl_n)).astype(BF16)
                dv_h = _dot_tn(jnp.concatenate([p[:, L:].astype(BF16), p_n.astype(BF16)], axis=0), do_in)
                dk_h = _dot_tn(jnp.concatenate([ds[:, L:], ds_n], axis=0), q_in)
                dv_acc[g // 2] = dv_acc[g // 2] + _unplace(dv_h, hf, g, half0)
                dk_acc[g // 2] = dk_acc[g // 2] + _unplace(dk_h, hf, g, half0)
            dqkv_ref[:, sl] = _rope_t(dq_ch, cos_c, sin_c, lo).astype(BF16)
        for m in range(2):
            dqkv_ref[:, QD + 128 * m:QD + 128 * (m + 1)] = _rope_t(dk_acc[m], cos_c, sin_c, lo).astype(BF16)
            dqkv_ref[:, QD + KVD + 128 * m:QD + KVD + 128 * (m + 1)] = dv_acc[m].astype(BF16)

        @pl.when(i == 0)
        def _():
            dsk_ref[...] = jnp.zeros_like(dsk_ref)

        dsk_ref[...] += jnp.broadcast_to(dsk_acc, dsk_ref.shape)

    prev = lambda i: jnp.maximum(i - 1, 0)
    nxt = lambda i: jnp.minimum(i + 1, nb - 1)
    cur_q = pl.BlockSpec((L, QD), lambda i: (i, 0))
    nxt_q = pl.BlockSpec((L, QD), lambda i: (nxt(i), 0))
    tab = lambda f: pl.BlockSpec((L, 128), lambda i: (f(i), 0))
    ident = lambda i: i
    qkv = QD + 2 * KVD
    assert O_K == O_Q + QD and O_V == O_K + KVD and O_Q % qkv == 0
    return _call(
        body, jobs=jobs, name=name,
        out_shape=(jax.ShapeDtypeStruct(dproj.shape, BF16), jax.ShapeDtypeStruct((8, 128), F32)),
        grid=(nb,),
        in_specs=[pl.BlockSpec(memory_space=pltpu.SMEM),
                  pl.BlockSpec((L, QD), lambda i: (i, O_Q // QD)), pl.BlockSpec((L, QD), lambda i: (nxt(i), O_Q // QD)),
                  pl.BlockSpec((L, KVD), lambda i: (i, O_K // KVD)), pl.BlockSpec((L, KVD), lambda i: (prev(i), O_K // KVD)),
                  pl.BlockSpec((L, KVD), lambda i: (i, O_V // KVD)), pl.BlockSpec((L, KVD), lambda i: (prev(i), O_V // KVD)),
                  cur_q, nxt_q, cur_q, nxt_q, tab(ident), tab(nxt),
                  tab(ident), tab(ident), tab(prev), tab(prev), tab(nxt), tab(nxt), ANY],
        out_specs=(pl.BlockSpec((L, qkv), lambda i: (i, O_Q // qkv)), pl.BlockSpec((8, 128), lambda i: (0, 0))),
        compiler_params=_cp(("arbitrary",)), aliases={19: 0},
    )(sinks, proj, proj, proj, proj, proj, proj, dattn, dattn, attn, attn, lse, lse, cos, sin, cos, sin, cos, sin, dproj)


PAIRS = NH // NG // 2


def _softplus(x):
    return jnp.maximum(x, 0.0) + jnp.log(1.0 + jnp.exp(-jnp.abs(x)))


def _ssd_chunk(g, xps, dtr, bm, cm, sps, dtb, alog, dsk):
    lane = lax.broadcasted_iota(jnp.int32, (L, 128), 1)
    lane1 = lax.broadcasted_iota(jnp.int32, (1, 128), 1)
    row = lax.broadcasted_iota(jnp.int32, (L, L), 0)
    col = lax.broadcasted_iota(jnp.int32, (L, L), 1)
    rowc = lax.broadcasted_iota(jnp.int32, (128, 1), 0)
    tril = col <= row
    dt = _softplus(dtr + dtb)
    a = dt * (-jnp.exp(alog))
    a_cs = lax.dot_general(tril.astype(F32), a, (((1,), (0,)), ((), ())), precision=lax.Precision.HIGHEST,
                           preferred_element_type=F32)
    a_cst = a_cs.T
    a_last = jnp.sum(jnp.where(row == L - 1, a_cs, 0.0), axis=0, keepdims=True)
    cb = _bdot_nt(cm, bm)
    ys, snew = [], []
    for q in range(PAIRS):
        xp, sp = xps[q], sps[q]
        skip = jnp.zeros((L, 128), F32)
        keep = jnp.zeros((128, 1), F32)
        ms, xds, cds, sms, bds = [], [], [], [], []
        for hh in range(2):
            h = g * 2 * PAIRS + 2 * q + hh
            hm = (lane < HD) if hh == 0 else (lane >= HD)
            rm = (rowc < HD) if hh == 0 else (rowc >= HD)
            dt_h = jnp.sum(jnp.where(lane == h, dt, 0.0), axis=1, keepdims=True)
            acs_h = jnp.sum(jnp.where(lane == h, a_cs, 0.0), axis=1, keepdims=True)
            acst_h = jnp.sum(jnp.where(row == h, a_cst, 0.0), axis=0, keepdims=True)
            al_h = jnp.sum(jnp.where(lane1 == h, a_last, 0.0), axis=1, keepdims=True)
            dsk_h = jnp.sum(jnp.where(lane1 == h, dsk, 0.0), axis=1, keepdims=True)
            decay = jnp.where(tril, jnp.exp(jnp.where(tril, acs_h - acst_h, 0.0)), 0.0)
            xh = jnp.where(hm, xp, 0.0)
            ms.append(cb * decay)
            xds.append(xh * dt_h)
            cds.append(cm * jnp.exp(acs_h))
            sms.append(jnp.where(rm, sp, 0.0))
            bds.append(bm * jnp.exp(al_h - acs_h))
            skip = skip + dsk_h * xh
            keep = keep + jnp.where(rm, jnp.exp(al_h), 0.0)
        xd2 = jnp.concatenate(xds, axis=0)
        y_pair = (_bdot(jnp.concatenate(ms, axis=1), xd2)
                  + _bdot_nt(jnp.concatenate(cds, axis=1), jnp.concatenate(sms, axis=1)) + skip)
        ys.append(y_pair)
        snew.append(sp * keep + _bdot_tn(xd2, jnp.concatenate(bds, axis=0)))
    return ys, snew


def _ssd_specs(t):
    nc = t // L
    xs = lambda f: pl.BlockSpec((L, 128 * PAIRS), lambda c, g: (f(c), g))
    bspec = lambda f: pl.BlockSpec((L, NS), lambda c, g: (f(c), DI // NS + g))
    cspec = lambda f: pl.BlockSpec((L, NS), lambda c, g: (f(c), DI // NS + NG + g))
    dts = lambda f: pl.BlockSpec((L, 128), lambda c, g: (f(c), O_DT // 128))
    par = pl.BlockSpec((1, 128), lambda c, g: (0, 0))
    st = lambda f: pl.BlockSpec((1, 1, PAIRS, 128, NS), lambda c, g: (f(c), g, 0, 0, 0))
    return nc, xs, bspec, cspec, dts, par, st


def _ssd_fwd(xbc_act, proj, dtb, alog, dsk, *, name, jobs=()):
    t = proj.shape[0]
    nc, xs, bspec, cspec, dts, par, st = _ssd_specs(t)
    ident = lambda c: c

    def body(x_ref, b_ref, c_ref, dt_ref, dtb_ref, al_ref, dsk_ref, y_ref, sin_ref, s_ref):
        c, g = pl.program_id(0), pl.program_id(1)

        @pl.when(c == 0)
        def _():
            s_ref[g] = jnp.zeros((PAIRS, 128, NS), F32)

        sps = [s_ref[g, q] for q in range(PAIRS)]
        for q in range(PAIRS):
            sin_ref[0, 0, q] = sps[q]
        xps = [x_ref[:, 128 * q:128 * (q + 1)] for q in range(PAIRS)]
        ys, snew = _ssd_chunk(g, xps, dt_ref[...], b_ref[...], c_ref[...], sps, dtb_ref[...], al_ref[...], dsk_ref[...])
        for q in range(PAIRS):
            y_ref[:, 128 * q:128 * (q + 1)] = ys[q]
            s_ref[g, q] = snew[q]

    return _call(
        body, jobs=jobs, name=name,
        out_shape=(jax.ShapeDtypeStruct((t, DI), F32), jax.ShapeDtypeStruct((nc, NG, PAIRS, 128, NS), F32)),
        grid=(nc, NG),
        in_specs=[xs(ident), bspec(ident), cspec(ident), dts(ident), par, par, par],
        out_specs=(pl.BlockSpec((L, 128 * PAIRS), lambda c, g: (c, g)), st(ident)),
        scratch_shapes=[pltpu.VMEM((NG, PAIRS, 128, NS), F32)],
        compiler_params=_cp(("arbitrary", "arbitrary")),
    )(xbc_act, xbc_act, xbc_act, proj, dtb, alog, dsk)


def _ssd_bwd(xbc_act, proj, dtb, alog, dsk, states, dy, dproj, *, name, jobs=()):
    t = proj.shape[0]
    nc, xs, bspec, cspec, dts, par, st = _ssd_specs(t)
    rev = lambda c: nc - 1 - c

    def body(x_ref, b_ref, c_ref, dt_ref, dtb_ref, al_ref, dsk_ref, sin_ref, dy_ref, _,
             dx_ref, db_ref, dc_ref, ddtp_ref, ddtb_ref, dal_ref, ddsk_ref, ds_ref, ddt_ref):
        c, g = pl.program_id(0), pl.program_id(1)

        @pl.when(c == 0)
        def _():
            ds_ref[g] = jnp.zeros((PAIRS, 128, NS), F32)

        @pl.when(jnp.logical_and(c == 0, g == 0))
        def _():
            ddtb_ref[...] = jnp.zeros_like(ddtb_ref)
            dal_ref[...] = jnp.zeros_like(dal_ref)
            ddsk_ref[...] = jnp.zeros_like(ddsk_ref)

        @pl.when(g == 0)
        def _():
            ddt_ref[...] = jnp.zeros_like(ddt_ref)

        sps = [sin_ref[0, 0, q] for q in range(PAIRS)]
        xps = [x_ref[:, 128 * q:128 * (q + 1)] for q in range(PAIRS)]
        _, vjp = jax.vjp(functools.partial(_ssd_chunk, g), xps, dt_ref[...], b_ref[...], c_ref[...], sps,
                         dtb_ref[...], al_ref[...], dsk_ref[...])
        dys = [dy_ref[:, 128 * q:128 * (q + 1)] for q in range(PAIRS)]
        dss = [ds_ref[g, q] for q in range(PAIRS)]
        dxps, ddt, db, dc, dsps, ddtb, dal, ddsk = vjp((dys, dss))
        for q in range(PAIRS):
            dx_ref[:, 128 * q:128 * (q + 1)] = dxps[q]
            ds_ref[g, q] = dsps[q]
        db_ref[...] = db
        dc_ref[...] = dc
        ddt_ref[...] += ddt
        ddtb_ref[...] += jnp.broadcast_to(ddtb, ddtb_ref.shape)
        dal_ref[...] += jnp.broadcast_to(dal, dal_ref.shape)
        ddsk_ref[...] += jnp.broadcast_to(ddsk, ddsk_ref.shape)

        @pl.when(g == NG - 1)
        def _():
            ddtp_ref[:, :128] = ddt_ref[...].astype(BF16)
            ddtp_ref[:, 128:] = jnp.zeros((L, DT_PAD - 128), BF16)

    acc = pl.BlockSpec((8, 128), lambda c, g: (0, 0))
    o8 = jax.ShapeDtypeStruct((8, 128), F32)
    return _call(
        body, jobs=jobs, name=name,
        out_shape=(jax.ShapeDtypeStruct((t, DI), F32), jax.ShapeDtypeStruct((t, NG * NS), F32),
                   jax.ShapeDtypeStruct((t, NG * NS), F32), jax.ShapeDtypeStruct(dproj.shape, BF16), o8, o8, o8),
        grid=(nc, NG),
        in_specs=[xs(rev), bspec(rev), cspec(rev), dts(rev), par, par, par, st(rev),
                  pl.BlockSpec((L, 128 * PAIRS), lambda c, g: (rev(c), g)), ANY],
        out_specs=(pl.BlockSpec((L, 128 * PAIRS), lambda c, g: (rev(c), g)),
                   pl.BlockSpec((L, NS), lambda c, g: (rev(c), g)), pl.BlockSpec((L, NS), lambda c, g: (rev(c), g)),
                   pl.BlockSpec((L, DT_PAD), lambda c, g: (rev(c), O_DT // DT_PAD)), acc, acc, acc),
        scratch_shapes=[pltpu.VMEM((NG, PAIRS, 128, NS), F32), pltpu.VMEM((L, 128), F32)],
        compiler_params=_cp(("arbitrary", "arbitrary")), aliases={9: 3},
    )(xbc_act, xbc_act, xbc_act, proj, dtb, alog, dsk, states, dy, dproj)


def _pad_lanes(v, n=128):
    return jnp.pad(v, ((0, 0), (0, n - v.shape[1])))


def _local_step(x, p, positions, target, small, plan):
    t = x.shape[0]
    cos, sin = _rope_tables(positions, t)
    dtb, alog, dsk = _pad_lanes(small["dt_bias"]), _pad_lanes(small["a_log"]), _pad_lanes(small["d_skip"])
    w, jobs = plan.w, plan.jobs

    def mm(a, b, *, name, tm=t, tn=512, **kw):
        return _matmul(a, b, tm=tm, tn=tn, name=name, jobs=jobs(name), **kw)

    tkl = FFN // 4

    def dw(wname, a, dy, *, name, tm):
        plan.g(wname, _matmul(a, dy, ta=True, out_dtype=BF16, tm=tm, tn=512, tk=t, name=name, jobs=jobs(name)))

    u = _rmsnorm_fwd(x, small["g_mix"], name="norm_mix")
    proj = mm(u, w("w_in"), tn=1024, tk=D, name="mm_in")
    attn, lse = _attn_fwd(proj, cos, sin, small["sinks"], name="attn_fwd", jobs=jobs("attn_fwd"))
    out_a = mm(attn, w("w_attn_br"), tk=QD, name="mm_attn_br")
    xbc_act = _conv_fwd(proj, small["conv_w"], small["conv_b"], name="conv_fwd")
    y_pre, states = _ssd_fwd(xbc_act, proj, dtb, alog, dsk, name="ssd_fwd", jobs=jobs("ssd_fwd"))
    yn = _gated_norm_fwd(y_pre, proj, small["g_ssd"], name="gated_norm_fwd")
    out_s = mm(yn, w("w_ssd_br"), tk=DI, name="mm_ssd_br")
    merged = _merge_fwd(proj, out_a, out_s, name="merge_fwd")
    h1 = mm(merged, w("w_o"), add=x, tk=D, name="mm_o")
    f = _rmsnorm_fwd(h1, small["g_ffn"], name="norm_ffn")
    gate, up, act = _swiglu_fwd(f, w("w_gate"), w("w_up"), name="swiglu_fwd", jobs=jobs("swiglu_fwd"))
    h2 = mm(act, w("w_down"), add=h1, tm=t // 2, tk=FFN // 2, name="mm_down")
    e = _rmsnorm_fwd(h2, small["g_ple"], name="norm_ple")
    pgl = mm(e, w("w_ple_gate"), tk=D, name="mm_ple_gate")
    pb = p.astype(BF16)
    pp = mm(pb, w("w_ple_proj"), tk=PLE, name="mm_ple_proj")
    dh3, dpgl, dpp, loss, dg_final = _final(h2, pgl, pp, target, small["g_final"].reshape(1, D), name="final")

    dw("w_ple_proj", pb, dpp, tm=PLE, name="mm_d_ple_proj")
    dw("w_ple_gate", e, dpgl, tm=D, name="mm_d_ple_gate")
    de = mm(dpgl, w("w_ple_gate"), tb=True, tk=D, name="mm_de")
    dh2, dh2b, dg_ple = _rmsnorm_bwd(h2, small["g_ple"], de, dh3, name="norm_ple_bwd", jobs=jobs("norm_ple_bwd"))
    dw("w_down", act, dh2b, tm=FFN // 2, name="mm_d_down")
    dgate, dup = _swiglu_bwd(dh2b, w("w_down"), gate, up, name="swiglu_bwd", jobs=jobs("swiglu_bwd"))
    dw("w_gate", f, dgate, tm=D, name="mm_d_gate")
    dw("w_up", f, dup, tm=D, name="mm_d_up")
    df = mm(dgate, w("w_gate"), tb=True, tn=1024, tk=tkl, name="mm_df_gate")
    df = mm(dup, w("w_up"), tb=True, add=df, tm=t // 2, tk=FFN // 2, name="mm_df_up")
    dh1, dh1b, dg_ffn = _rmsnorm_bwd(h1, small["g_ffn"], df, dh2, name="norm_ffn_bwd", jobs=jobs("norm_ffn_bwd"))
    dw("w_o", merged, dh1b, tm=D, name="mm_d_o")
    dmerged = mm(dh1b, w("w_o"), tb=True, tk=D, name="mm_dmerged")
    dout_a, dout_s, dproj = _merge_bwd(proj, out_a, out_s, dmerged, name="merge_bwd")
    dw("w_attn_br", attn, dout_a, tm=QD, name="mm_d_attn_br")
    dw("w_ssd_br", yn, dout_s, tm=DI, name="mm_d_ssd_br")
    dattn = mm(dout_a, w("w_attn_br"), tb=True, tk=D, name="mm_dattn")
    dyn = mm(dout_s, w("w_ssd_br"), tb=True, tk=D, name="mm_dyn")
    dproj, dsinks = _attn_bwd(proj, cos, sin, small["sinks"], attn, lse, dattn, dproj, name="attn_bwd",
                              jobs=jobs("attn_bwd"))
    dy_pre, dproj, dg_ssd = _gated_norm_bwd(y_pre, proj, small["g_ssd"], dyn, dproj, name="gated_norm_bwd",
                                            jobs=jobs("gated_norm_bwd"))
    dxs, db, dc, dproj, ddtb, dalog, ddsk = _ssd_bwd(xbc_act, proj, dtb, alog, dsk, states, dy_pre, dproj, name="ssd_bwd",
                                                     jobs=jobs("ssd_bwd"))
    dproj, dconv_w, dconv_b = _conv_bwd(proj, small["conv_w"], small["conv_b"], dxs, db, dc, dproj, name="conv_bwd",
                                        jobs=jobs("conv_bwd"))
    for which, h in (("send", 1 - plan.core), ("keep", plan.core)):
        uh = lax.dynamic_slice_in_dim(u, h * (D // 2), D // 2, axis=1)
        name = "mm_d_in_" + which
        plan.g_half("w_in", which, _matmul(uh, dproj, ta=True, out_dtype=BF16, tm=D // 2, tn=1024, tk=t, name=name,
                                           jobs=jobs(name)))
    du = mm(dproj, w("w_in"), tb=True, tn=1024, tk=tkl, name="mm_du")
    grad_x, _, dg_mix = _rmsnorm_bwd(x, small["g_mix"], du, dh1, name="norm_mix_bwd", jobs=jobs("norm_mix_bwd"))

    gs = {
        "g_mix": dg_mix[:1], "conv_w": dconv_w[:CW], "conv_b": dconv_b[:1], "dt_bias": ddtb[:1, :NH],
        "a_log": dalog[:1, :NH], "d_skip": ddsk[:1, :NH], "g_ssd": dg_ssd[:1], "sinks": dsinks[:1, :NQH],
        "g_ffn": dg_ffn[:1], "g_ple": dg_ple[:1], "g_final": dg_final[0],
    }
    return loss, grad_x, gs


def _shard_pieces():
    segs = ((R_Q, QD, O_Q), (R_K, KVD, O_K), (R_V, KVD, O_V), (R_Z, DI, O_Z), (R_XBC, CONV, O_XBC), (R_DT, NH, O_DT),
            (R_GA, D, O_GA), (R_GS, D, O_GS))
    cs = IN_DIM // NCHIP
    out = []
    for j in range(NCHIP):
        for r0, n, k0 in segs:
            lo, hi = max(r0, j * cs), min(r0 + n, (j + 1) * cs)
            if lo < hi:
                out.append((j, lo - j * cs, hi - lo, k0 + lo - r0))
    return out


SLAB = IN_DIM // NCHIP
SLAB_PAD = -(-SLAB // 128) * 128
REMAP_ROWS = 256


def _lane_remap(src, dst_slabs, dst_cols, moves, *, name, add=None, jobs=()):
    s_n, rows, s_cols = src.shape
    assert s_cols % 128 == 0 and dst_cols % 128 == 0 and rows % REMAP_ROWS == 0
    half = REMAP_ROWS // 2

    def body(s_ref, *refs):
        d_ref = refs[-1]
        lane = lax.broadcasted_iota(jnp.int32, (half, 128), 1)
        tiles = {}

        def tile(j, m):
            if (j, m) not in tiles:
                tiles[j, m] = pltpu.bitcast(s_ref[j, :, 128 * m:128 * (m + 1)], jnp.uint32)
            return tiles[j, m]

        def window(j, base):
            m0, s = base // 128, base % 128
            left = tile(j, m0) if 0 <= m0 < s_cols // 128 else None
            if s == 0:
                return left
            right = tile(j, m0 + 1) if 0 <= m0 + 1 < s_cols // 128 else None
            left = None if left is None else pltpu.roll(left, 128 - s, 1)
            right = None if right is None else pltpu.roll(right, 128 - s, 1)
            if left is None or right is None:
                return right if left is None else left
            return jnp.where(lane < 128 - s, left, right)

        for ds in range(dst_slabs):
            for t in range(dst_cols // 128):
                o = 128 * t
                acc = jnp.zeros((half, 128), jnp.uint32)
                for sj, sc, n, dj, dc in moves:
                    lo, hi = max(o, dc) - o, min(o + 128, dc + n) - o
                    if dj != ds or lo >= hi:
                        continue
                    win = window(sj, o - dc + sc)
                    acc = win if (lo, hi) == (0, 128) else jnp.where(jnp.logical_and(lane >= lo, lane < hi), win, acc)
                out = pltpu.bitcast(acc, BF16)
                if add is not None:
                    out = (out.astype(F32) + refs[0][ds, :, o:o + 128].astype(F32)).astype(BF16)
                d_ref[ds, :, o:o + 128] = out

    dst_blk = pl.BlockSpec((dst_slabs, REMAP_ROWS, dst_cols), lambda i: (0, i, 0))
    return _call(
        body, jobs=jobs, name=name, out_shape=jax.ShapeDtypeStruct((dst_slabs, rows, dst_cols), BF16),
        grid=(rows // REMAP_ROWS,),
        in_specs=[pl.BlockSpec((s_n, REMAP_ROWS, s_cols), lambda i: (0, i, 0))] + ([dst_blk] if add is not None else []),
        out_specs=dst_blk, compiler_params=_cp(("parallel",)),
    )(*((src,) if add is None else (src, add)))


def _slabs_to_kernel_cols(slabs, *, name, jobs=()):
    moves = [(j, a, n, 0, k0) for j, a, n, k0 in _shard_pieces()]
    return _lane_remap(slabs, 1, NP, moves, name=name, jobs=jobs)[0]


def _kernel_cols_to_slabs(g, *, name, add=None, jobs=()):
    moves = [(0, k0, n, j, a) for j, a, n, k0 in _shard_pieces()]
    return _lane_remap(g[None], NCHIP, SLAB_PAD, moves, name=name, add=add, jobs=jobs)


MATS = {
    n: (n, kind, 1, r, c, tp, tf) for n, kind, r, c, tp, tf in (
        ("w_in", "stk", 2048, SLAB_PAD, 256, 256),
        ("w_attn_br", "col", 1024, 512, 256, 256),
        ("w_ssd_br", "row", 512, 2048, 512, 256),
        ("w_o", "row", 512, 2048, 512, 256),
        ("w_gate", "col", 2048, 1408, 256, 256),
        ("w_up", "col", 2048, 1408, 256, 256),
        ("w_down", "row", 1408, 2048, 704, 704),
        ("w_ple_gate", "row", 512, 2048, 512, 256),
        ("w_ple_proj", "col", 256, 512, 128, 128),
    )}


def _pos():
    return lax.axis_index("x"), lax.axis_index("y"), lax.axis_index("c")


def _flip(v, a):
    return 1 - v if a else v


def _remote(src, dst, send, recv, dev):
    return pltpu.make_async_remote_copy(src_ref=src, dst_ref=dst, send_sem=send, recv_sem=recv, device_id=dev,
                                        device_id_type=MESH)


def _whole_shape(kind, g, r, c):
    return {"row": (g, NCHIP * r, c), "col": (g, r, NCHIP * c), "stk": (NCHIP, r, c)}[kind]


def _cols(j, c):
    return pl.ds(pl.multiple_of(j * c, 128), c)


def _whole_shard(kind, ref, j, r, c):
    if kind == "row":
        return ref.at[:, pl.ds(j * r, r), :]
    if kind == "col":
        return ref.at[:, :, _cols(j, c)]
    return ref.at[pl.ds(j, 1)]


def _whole_rows(kind, ref, j, row, n, r, c):
    if kind == "row":
        return ref.at[:, pl.ds(j * r + row, n), :]
    if kind == "col":
        return ref.at[:, pl.ds(row, n), _cols(j, c)]
    return ref.at[pl.ds(j, 1), pl.ds(row, n), :]


class _GatherJob(_Job):
    has_mid = True
    NCP = 13

    def __init__(self, names, shards, sink):
        self.mats = [MATS[n] for n in names]
        self.srcs = [shards[n] for n in names]
        self.news = [jax.ShapeDtypeStruct(_whole_shape(kind, g, r, c), BF16) for _, kind, g, r, c, _, _ in self.mats]
        n = len(names)
        self.scratch = [pltpu.SemaphoreType.DMA((self.NCP * n,)), pltpu.SemaphoreType.DMA((self.NCP * n,))]
        self.names, self.sink = names, sink

    def _copies(self, srcs, news, sems):
        send, recv = sems
        x, y, c = _pos()
        me, jx, jy, jd = 2 * x + y, 2 * (1 - x) + y, 2 * x + (1 - y), 2 * (1 - x) + (1 - y)
        nbx, nby, sib = (1 - x, y, c), (x, 1 - y, c), (x, y, 1 - c)
        cps = []
        for w, (_, kind, g, r, cc, _, _) in enumerate(self.mats):
            hr, qr = r // 2, r // 4
            at = lambda j, h, q, n: _whole_rows(kind, news[w], j, h * hr + q * qr, n, r, cc)
            mine = lambda q: srcs[w].at[:, pl.ds(c * hr + q * qr, qr), :]
            cp = lambda k, s, d, dev: _remote(s, d, send.at[self.NCP * w + k], recv.at[self.NCP * w + k], dev)
            cps.append([
                cp(0, mine(0), at(me, c, 0, qr), nbx), cp(1, mine(1), at(me, c, 1, qr), nbx),
                cp(2, mine(1), at(me, c, 1, qr), nby), cp(3, mine(0), at(me, c, 0, qr), nby),
                cp(4, at(jx, c, 0, qr), at(jx, c, 0, qr), nby), cp(5, at(jy, c, 1, qr), at(jy, c, 1, qr), nbx),
                cp(6, at(jx, c, 0, qr), at(jx, c, 0, qr), sib), cp(7, at(jx, c, 1, qr), at(jx, c, 1, qr), sib),
                cp(8, at(jy, c, 1, qr), at(jy, c, 1, qr), sib), cp(9, at(jy, c, 0, qr), at(jy, c, 0, qr), sib),
                cp(10, at(jd, c, 0, qr), at(jd, c, 0, qr), sib), cp(11, at(jd, c, 1, qr), at(jd, c, 1, qr), sib),
                cp(12, srcs[w], _whole_shard(kind, news[w], me, r, cc), sib)])
        return cps

    def _pass_on(self, srcs, news, sems, pairs):
        cps = self._copies(srcs, news, sems)
        for w in range(len(self.mats)):
            for arrived, onward in pairs:
                cps[w][arrived].wait_recv()
                for k in onward:
                    cps[w][k].start()

    def start(self, srcs, dsts, news, sems):
        cps = self._copies(srcs, news, sems)
        for k in (0, 2, 1, 3, 12):
            for w in range(len(self.mats)):
                cps[w][k].start()

    def mid(self, srcs, dsts, news, sems):
        self._pass_on(srcs, news, sems, ((0, (4, 6)), (2, (5, 8))))

    def late(self, srcs, dsts, news, sems):
        self._pass_on(srcs, news, sems, ((1, (7,)), (3, (9,))))

    def finish(self, srcs, dsts, news, sems):
        self._pass_on(srcs, news, sems, ((4, (10,)), (5, (11,))))
        cps = self._copies(srcs, news, sems)
        for w in range(len(self.mats)):
            for k in (6, 7, 8, 9, 10, 11, 12):
                cps[w][k].wait_recv()
            for k in range(self.NCP):
                cps[w][k].wait_send()

    def done(self, dsts, news):
        for n, a in zip(self.names, news):
            self.sink[n] = a


class _SwapJob(_Job):
    def __init__(self, build, ncopies, *, srcs=(), dsts=(), news=(), done=None):
        self.build, self.srcs, self.dsts, self.news, self._done = build, list(srcs), list(dsts), list(news), done
        self.scratch = [pltpu.SemaphoreType.DMA((ncopies,)), pltpu.SemaphoreType.DMA((ncopies,))]

    def start(self, srcs, dsts, news, sems):
        for cp in self.build(srcs, dsts, news, *sems):
            cp.start()

    def finish(self, srcs, dsts, news, sems):
        for cp in self.build(srcs, dsts, news, *sems):
            cp.wait()

    def done(self, dsts, news):
        if self._done is not None:
            self._done(dsts, news)


def _half_of_whole(kind, ref, h, r, c):
    if kind == "row":
        return ref.at[:, :, pl.ds(pl.multiple_of(h * (c // 2), 128), c // 2)]
    return ref.at[:, pl.ds(h * (r // 2), r // 2), :]


def _half_shape(kind, g, r, c):
    return {"row": (g, NCHIP * r, c // 2), "col": (g, r // 2, NCHIP * c), "stk": (NCHIP, r // 2, c)}[kind]


def _sub_shape(kind, r, c):
    return {"row": (1, r // 2, c // 2), "col": (1, r // 4, c), "stk": (1, r // 4, c)}[kind]


def _sub_of_half(kind, ref, j, p, r, c):
    sr = _sub_shape(kind, r, c)[1]
    if kind == "row":
        return ref.at[:, pl.ds(j * r + p * sr, sr), :]
    if kind == "col":
        return ref.at[:, pl.ds(p * sr, sr), _cols(j, c)]
    return ref.at[pl.ds(j, 1), pl.ds(p * sr, sr), :]


def _sub_tile(sr):
    return 256 if sr % 256 == 0 else sr


def _half_of_shard(kind, ref, h, r, c):
    if kind == "row":
        return ref.at[:, :, pl.ds(pl.multiple_of(h * (c // 2), 128), c // 2)]
    return ref.at[:, pl.ds(h * (r // 2), r // 2), :]


def _pair_sum(pack, core, mine, got, whole=True):
    name, kind, g, r, c, tr, _ = pack
    hs = _half_shape(kind, g, r, c)
    nb = hs[1] // tr

    def body(core_ref, a_ref, b_ref, o_ref):
        o_ref[...] = (a_ref[...].astype(F32) + b_ref[...].astype(F32)).astype(BF16)

    blk = (1, tr, hs[2])
    same = lambda gi, i, core_ref: (gi, i, 0)
    if not whole:
        a_map = same
    elif kind == "row":
        a_map = lambda gi, i, core_ref: (gi, i, core_ref[0])
    else:
        a_map = lambda gi, i, core_ref: (gi, core_ref[0] * nb + i, 0)
    return pl.pallas_call(
        body, name="pair_sum_" + name, out_shape=jax.ShapeDtypeStruct(hs, BF16),
        grid_spec=pltpu.PrefetchScalarGridSpec(
            num_scalar_prefetch=1, grid=(hs[0], nb),
            in_specs=[pl.BlockSpec(blk, a_map), pl.BlockSpec(blk, same)], out_specs=pl.BlockSpec(blk, same)),
        compiler_params=_cp(("parallel", "parallel")),
    )(core, mine, got)


def _sub_sums(pack, idx, half, got, *, name):
    _, kind, g, r, c, _, _ = pack
    _, sr, sc = _sub_shape(kind, r, c)
    tr = _sub_tile(sr)
    nb = sr // tr

    def body(idx_ref, a_ref, ga_ref, b_ref, gb_ref, k_ref, p_ref):
        k_ref[0, 0] = a_ref[0].astype(F32) + ga_ref[0, 0].astype(F32)
        p_ref[0, 0] = (b_ref[0].astype(F32) + gb_ref[0, 0].astype(F32)).astype(BF16)

    def sub_map(o):
        if kind == "row":
            return lambda q, i, ix: (0, ix[4 * q + o] * (r // tr) + ix[4 * q + o + 1] * nb + i, 0)
        if kind == "col":
            return lambda q, i, ix: (0, ix[4 * q + o + 1] * nb + i, ix[4 * q + o])
        return lambda q, i, ix: (ix[4 * q + o], ix[4 * q + o + 1] * nb + i, 0)

    sub = lambda o: pl.BlockSpec((1, tr, sc), sub_map(o))
    got_blk = lambda o: pl.BlockSpec((1, 1, tr, sc), lambda q, i, ix: (2 * q + o, 0, i, 0))
    out_blk = pl.BlockSpec((1, 1, tr, sc), lambda q, i, ix: (q, 0, i, 0))
    return pl.pallas_call(
        body, name=name,
        out_shape=(jax.ShapeDtypeStruct((2, 1, sr, sc), F32), jax.ShapeDtypeStruct((2, 1, sr, sc), BF16)),
        grid_spec=pltpu.PrefetchScalarGridSpec(
            num_scalar_prefetch=1, grid=(2, nb), in_specs=[sub(0), got_blk(0), sub(2), got_blk(1)],
            out_specs=(out_blk, out_blk)),
        compiler_params=_cp(("parallel", "parallel")),
    )(idx, half, got, half, got)


def _shard_sum(pack, core, keep, got):
    name, kind, g, r, c, _, _ = pack
    _, sr, sc = _sub_shape(kind, r, c)
    tr = _sub_tile(sr)
    nb = sr // tr

    def body(core_ref, a_ref, b_ref, o_ref):
        o_ref[0] = a_ref[0, 0] + b_ref[0, 0].astype(F32)

    blk = pl.BlockSpec((1, 1, tr, sc), lambda p, i, cr: (p, 0, i, 0))
    if kind == "row":
        o_map = lambda p, i, cr: (0, p * nb + i, cr[0])
    else:
        o_map = lambda p, i, cr: (0, cr[0] * 2 * nb + p * nb + i, 0)
    return pl.pallas_call(
        body, name="shard_sum_" + name, out_shape=jax.ShapeDtypeStruct((g, r, c), F32),
        grid_spec=pltpu.PrefetchScalarGridSpec(
            num_scalar_prefetch=1, grid=(2, nb), in_specs=[blk, blk], out_specs=pl.BlockSpec((1, tr, sc), o_map)),
        compiler_params=_cp(("parallel", "parallel")),
    )(core, keep, got)


class _Plan:
    def __init__(self, shards, table):
        self.shards, self.table = shards, table
        self.whole, self.grad, self.got_a, self.half, self.gshard = {}, {}, {}, {}, {}
        self.got_b1, self.kept, self.pass_on, self.got_b2 = {}, {}, {}, {}
        x, y, c = _pos()
        me, jx, jy = 2 * x + y, 2 * (1 - x) + y, 2 * x + (1 - y)
        self.core = c
        self.core1 = c.reshape(1).astype(jnp.int32)
        zero = 0 * me
        self.idx_sums = jnp.stack([me, zero, jy, zero, me, zero + 1, jx, zero + 1]).astype(jnp.int32)
        self._w_in = None
        self.send, self.keep = {}, {}

    def w(self, n):
        if n != "w_in":
            return self.whole[n][0]
        if self._w_in is None:
            self._w_in = _slabs_to_kernel_cols(self.whole[n], name="relayout_w_in", jobs=self.jobs("relayout_w_in"))
        return self._w_in

    def g(self, n, a):
        self.grad[n] = a[None]

    def g_half(self, n, which, a):
        if which == "keep" and n in self.got_a:
            self.half[n] = _kernel_cols_to_slabs(a, name="relayout_d_in_keep", add=self.got_a[n])
        else:
            (self.send if which == "send" else self.keep)[n] = _kernel_cols_to_slabs(a, name="relayout_d_in_" + which)

    def jobs(self, tag):
        out = []
        for spec in self.table.get(tag, ()):
            out += getattr(self, "_" + spec[0])(*spec[1:])
        return out

    def run(self, name, jobs):
        if jobs:
            _call(lambda: None, jobs=jobs, name=name, out_shape=[], in_specs=[], out_specs=[])()

    def _gather(self, names):
        return [_GatherJob(names, self.shards, self.whole)]

    def _rs_a(self, names):
        mats = [MATS[n] for n in names]

        def build(srcs, dsts, news, send, recv):
            x, y, c = _pos()
            return [_remote(srcs[i] if names[i] in self.send else _half_of_whole(kind, srcs[i], 1 - c, r, cc), news[i],
                            send.at[i], recv.at[i], (x, y, 1 - c))
                    for i, (_, kind, g, r, cc, _, _) in enumerate(mats)]

        def done(dsts, news):
            self.got_a.update(zip(names, news))

        return [_SwapJob(build, len(names), srcs=[self.send.get(n, self.grad.get(n)) for n in names], done=done,
                         news=[jax.ShapeDtypeStruct(_half_shape(kind, g, r, c), BF16) for _, kind, g, r, c, _, _ in mats])]

    def _rs_b1(self, names):
        mats = [MATS[n] for n in names]
        for n in names:
            if n in self.half:
                continue
            if n in self.keep:
                self.half[n] = _pair_sum(MATS[n], self.core1, self.keep[n], self.got_a[n], whole=False)
            else:
                self.half[n] = _pair_sum(MATS[n], self.core1, self.grad[n], self.got_a[n])

        def build(srcs, dsts, news, send, recv):
            x, y, c = _pos()
            jx, jy, jd = 2 * (1 - x) + y, 2 * x + (1 - y), 2 * (1 - x) + (1 - y)
            nbx, nby = (1 - x, y, c), (x, 1 - y, c)
            cps = []
            for i, (_, kind, g, r, cc, _, _) in enumerate(mats):
                sub = lambda j, p: _sub_of_half(kind, srcs[i], j, p, r, cc)
                for k, (j, p, dev) in enumerate(((jx, 0, nbx), (jd, 0, nbx), (jy, 1, nby), (jd, 1, nby))):
                    cps.append(_remote(sub(j, p), news[i].at[k], send.at[4 * i + k], recv.at[4 * i + k], dev))
            return cps

        def done(dsts, news):
            self.got_b1.update(zip(names, news))

        return [_SwapJob(build, 4 * len(names), srcs=[self.half[n] for n in names], done=done,
                         news=[jax.ShapeDtypeStruct((4,) + _sub_shape(kind, r, c), BF16) for _, kind, g, r, c, _, _ in mats])]

    def _rs_b2(self, names):
        mats = [MATS[n] for n in names]
        for n in names:
            self.kept[n], self.pass_on[n] = _sub_sums(MATS[n], self.idx_sums, self.half[n], self.got_b1[n], name="sums_" + n)

        def build(srcs, dsts, news, send, recv):
            x, y, c = _pos()
            cps = []
            for i in range(len(mats)):
                cps.append(_remote(srcs[i].at[0], news[i].at[0], send.at[2 * i], recv.at[2 * i], (x, 1 - y, c)))
                cps.append(_remote(srcs[i].at[1], news[i].at[1], send.at[2 * i + 1], recv.at[2 * i + 1], (1 - x, y, c)))
            return cps

        def done(dsts, news):
            self.got_b2.update(zip(names, news))

        return [_SwapJob(build, 2 * len(names), srcs=[self.pass_on[n] for n in names], done=done,
                         news=[jax.ShapeDtypeStruct((2,) + _sub_shape(kind, r, c), BF16) for _, kind, g, r, c, _, _ in mats])]

    def _rs_c(self, names):
        mats = [MATS[n] for n in names]
        parts = [_shard_sum(MATS[n], self.core1, self.kept[n], self.got_b2[n]) for n in names]

        def build(srcs, dsts, news, send, recv):
            x, y, c = _pos()
            cps = []
            for i, (_, kind, g, r, cc, _, _) in enumerate(mats):
                mine = _half_of_shard(kind, dsts[i], c, r, cc)
                cps.append(_remote(mine, mine, send.at[i], recv.at[i], (x, y, 1 - c)))
            return cps

        def done(dsts, news):
            self.gshard.update(zip(names, dsts))

        return [_SwapJob(build, len(names), dsts=parts, done=done)]

    def finish(self, n):
        if n not in self.got_a:
            self.run("rs_a_" + n, self._rs_a((n,)))
        if n not in self.got_b1:
            self.run("rs_b1_" + n, self._rs_b1((n,)))
        if n not in self.got_b2:
            self.run("rs_b2_" + n, self._rs_b2((n,)))
        if n not in self.gshard:
            self.run("rs_c_" + n, self._rs_c((n,)))
        return self.gshard[n]


TABLE = {
    "gather_w_in": (("gather", ("w_in",)),),
    "relayout_w_in": (("gather", ("w_gate",)),),
    "mm_in": (("gather", ("w_up",)),),
    "attn_fwd": (("gather", ("w_attn_br", "w_ssd_br")),),
    "ssd_fwd": (("gather", ("w_o",)),),
    "swiglu_fwd": (("gather", ("w_down",)),),
    "mm_down": (("gather", ("w_ple_gate", "w_ple_proj")),),
    "mm_de": (("rs_a", ("w_ple_proj", "w_ple_gate")),),
    "mm_d_down": (("rs_b1", ("w_ple_proj", "w_ple_gate")),),
    "swiglu_bwd": (("rs_a", ("w_down",)), ("rs_b2", ("w_ple_proj", "w_ple_gate"))),
    "mm_d_gate": (("rs_b1", ("w_down",)),),
    "mm_d_up": (("rs_b2", ("w_down",)), ("rs_c", ("w_ple_proj", "w_ple_gate")), ("rs_a", ("w_gate",))),
    "mm_df_gate": (("rs_b1", ("w_gate",)), ("rs_a", ("w_up",)), ("rs_c", ("w_down",))),
    "mm_df_up": (("rs_b2", ("w_gate",)),),
    "mm_dmerged": (("rs_a", ("w_o",)), ("rs_c", ("w_gate",))),
    "mm_dyn": (("rs_a", ("w_attn_br", "w_ssd_br")),),
    "attn_bwd": (("rs_b1", ("w_up",)),),
    "gated_norm_bwd": (("rs_b2", ("w_up",)),),
    "ssd_bwd": (("rs_b1", ("w_o", "w_attn_br", "w_ssd_br")), ("rs_c", ("w_up",))),
    "conv_bwd": (("rs_b2", ("w_o", "w_attn_br", "w_ssd_br")),),
    "mm_d_in_keep": (("rs_a", ("w_in",)), ("rs_c", ("w_o", "w_attn_br", "w_ssd_br"))),
    "mm_du": (("rs_b1", ("w_in",)),),
    "norm_mix_bwd": (("rs_b2", ("w_in",)),),
}


NDEV = 8


def _allreduce_small(v, *, name):
    rows = v.shape[0]

    def body(v_ref, o_ref, slots, send, recv):
        x, y, c = _pos()
        me = 4 * x + 2 * y + c
        slots[me] = v_ref[...]
        cps = []
        for k in range(1, NDEV):
            peer = (_flip(x, k & 4), _flip(y, k & 2), _flip(c, k & 1))
            cp = _remote(v_ref, slots.at[me], send.at[k - 1], recv.at[k - 1], peer)
            cp.start()
            cps.append(cp)
        for cp in cps:
            cp.wait()
        acc = slots[0]
        for s in range(1, NDEV):
            acc = acc + slots[s]
        o_ref[...] = acc

    return pl.pallas_call(
        body, name=name, out_shape=jax.ShapeDtypeStruct((rows, 128), F32),
        in_specs=[pl.BlockSpec(memory_space=pltpu.VMEM)], out_specs=pl.BlockSpec(memory_space=pltpu.VMEM),
        scratch_shapes=[pltpu.VMEM((NDEV, rows, 128), F32), pltpu.SemaphoreType.DMA((NDEV - 1,)),
                        pltpu.SemaphoreType.DMA((NDEV - 1,))],
    )(v)


def _adamw(w, g, m, v, *, name, tr=None, tc=None, jobs=()):
    r, c = w.shape
    tr = r if tr is None else tr
    c1 = 1.0 / (1.0 - B1 ** STEP)
    c2 = 1.0 / (1.0 - B2 ** STEP)

    def body(w_ref, g_ref, m_ref, v_ref, d_ref, mo_ref, vo_ref):
        gv = g_ref[...]
        mn = B1 * m_ref[...] + (1.0 - B1) * gv
        vn = B2 * v_ref[...] + (1.0 - B2) * (gv * gv)
        mo_ref[...] = mn
        vo_ref[...] = vn
        d_ref[...] = -LR * ((mn * c1) / (jnp.sqrt(vn * c2) + AEPS) + WD * w_ref[...])

    if tc is None:
        blk, grid = pl.BlockSpec((tr, c), lambda i: (i, 0)), (r // tr,)
    else:
        blk, grid = pl.BlockSpec((r, tc), lambda i: (0, i)), (c // tc,)
    o = jax.ShapeDtypeStruct((r, c), F32)
    return _call(
        body, jobs=jobs, name=name, out_shape=(o, o, o), grid=grid, in_specs=[blk] * 4, out_specs=(blk, blk, blk),
        compiler_params=_cp(("parallel",)),
    )(w, g, m, v)


WEIGHTS = ("g_mix", "w_in", "conv_w", "conv_b", "dt_bias", "a_log", "d_skip", "g_ssd", "sinks", "w_attn_br", "w_ssd_br",
           "w_o", "g_ffn", "w_gate", "w_up", "w_down", "g_ple", "w_ple_gate", "w_ple_proj", "g_final")
BIG = {
    "w_gate": 256, "w_up": 256, "w_down": 128, "w_ssd_br": 128, "w_o": 128, "w_ple_gate": 128, "w_attn_br": 256,
    "w_ple_proj": 256, "w_in": None,
}
SMALL = tuple(n for n in WEIGHTS if n not in BIG)


def _pack_small(parts):
    rows = []
    for a in parts:
        a = a.reshape(-1)
        rows.append(jnp.pad(a, (0, -a.shape[0] % 128)).reshape(-1, 128))
    out = jnp.concatenate(rows, axis=0)
    return jnp.pad(out, ((0, -out.shape[0] % 8), (0, 0)))


def _unpack_small(packed, shapes):
    out, r = [], 0
    for s in shapes:
        n = int(np.prod(s))
        nr = -(-n // 128)
        out.append(packed[r:r + nr].reshape(-1)[:n].reshape(s))
        r += nr
    return out


def kernel(x, p, positions, g_mix, w_in, conv_w, conv_b, dt_bias, a_log, d_skip, g_ssd, sinks, w_attn_br, w_ssd_br, w_o, g_ffn, w_gate, w_up, w_down, g_ple, w_ple_gate, w_ple_proj, g_final, loss_target, m_g_mix, m_w_in, m_conv_w, m_conv_b, m_dt_bias, m_a_log, m_d_skip, m_g_ssd, m_sinks, m_w_attn_br, m_w_ssd_br, m_w_o, m_g_ffn, m_w_gate, m_w_up, m_w_down, m_g_ple, m_w_ple_gate, m_w_ple_proj, m_g_final, v_g_mix, v_w_in, v_conv_w, v_conv_b, v_dt_bias, v_a_log, v_d_skip, v_g_ssd, v_sinks, v_w_attn_br, v_w_ssd_br, v_w_o, v_g_ffn, v_w_gate, v_w_up, v_w_down, v_g_ple, v_w_ple_gate, v_w_ple_proj, v_g_final):
    w = dict(zip(WEIGHTS, (g_mix, w_in, conv_w, conv_b, dt_bias, a_log, d_skip, g_ssd, sinks, w_attn_br, w_ssd_br, w_o,
                           g_ffn, w_gate, w_up, w_down, g_ple, w_ple_gate, w_ple_proj, g_final)))
    m = dict(zip(WEIGHTS, (m_g_mix, m_w_in, m_conv_w, m_conv_b, m_dt_bias, m_a_log, m_d_skip, m_g_ssd, m_sinks, m_w_attn_br,
                           m_w_ssd_br, m_w_o, m_g_ffn, m_w_gate, m_w_up, m_w_down, m_g_ple, m_w_ple_gate, m_w_ple_proj,
                           m_g_final)))
    v = dict(zip(WEIGHTS, (v_g_mix, v_w_in, v_conv_w, v_conv_b, v_dt_bias, v_a_log, v_d_skip, v_g_ssd, v_sinks, v_w_attn_br,
                           v_w_ssd_br, v_w_o, v_g_ffn, v_w_gate, v_w_up, v_w_down, v_g_ple, v_w_ple_gate, v_w_ple_proj,
                           v_g_final)))
    xi, yi, ci = _pos()
    chip = 2 * xi + yi
    t = x.shape[1]
    cshard = CONV // NCHIP

    shards = {n: w[n].astype(BF16) for n in MATS}
    shards["w_in"] = jnp.pad(shards["w_in"], ((0, 0), (0, 0), (0, SLAB_PAD - SLAB)))
    plan = _Plan(shards, TABLE)
    plan.run("gather_w_in", plan.jobs("gather_w_in"))
    placed = lax.dynamic_update_slice(jnp.zeros((CW, CONV), F32), w["conv_w"][0], (0, chip * cshard))
    conv_whole = _allreduce_small(jnp.where(ci == 0, placed, 0.0).reshape(-1, 128), name="gather_conv_w").reshape(CW, CONV)

    small = {n: w[n] for n in ("g_mix", "conv_b", "dt_bias", "a_log", "d_skip", "g_ssd", "sinks", "g_ffn", "g_ple", "g_final")}
    small["conv_w"] = conv_whole
    loss8, grad_x, gs = _local_step(x[0], p[0, 0], positions, loss_target[0], small, plan)

    order = ("g_mix", "conv_b", "dt_bias", "a_log", "d_skip", "g_ssd", "sinks", "g_ffn", "g_ple", "g_final", "conv_w")
    summed = _allreduce_small(_pack_small([loss8[0, :1]] + [gs[n] for n in order]), name="sum_small")
    parts = _unpack_small(summed, [(1,)] + [w[n].shape for n in order[:-1]] + [(CW, CONV)])
    loss = parts[0][0]
    grad = dict(zip(order, parts[1:]))
    grad["conv_w"] = lax.dynamic_slice(grad["conv_w"], (0, chip * cshard), (CW, cshard))[None]

    delta, new_m, new_v = {}, {}, {}
    for n, tr in BIG.items():
        grad[n] = plan.finish(n)[:, :, :w[n].shape[2]]
        if n == "w_in":
            d_, m_, v_ = _adamw(w[n][0].T, grad[n][0].T, m[n][0].T, v[n][0].T, tc=128, name="adamw_" + n)
            d_, m_, v_ = d_.T, m_.T, v_.T
        else:
            d_, m_, v_ = _adamw(w[n][0], grad[n][0], m[n][0], v[n][0], tr=tr, name="adamw_" + n)
        delta[n], new_m[n], new_v[n] = d_[None], m_[None], v_[None]
    shapes = [w[n].shape for n in SMALL]
    d_, m_, v_ = _adamw(_pack_small([w[n] for n in SMALL]), _pack_small([grad[n] for n in SMALL]),
                        _pack_small([m[n] for n in SMALL]), _pack_small([v[n] for n in SMALL]), tr=None, name="adamw_small")
    for n, a, b, c_ in zip(SMALL, _unpack_small(d_, shapes), _unpack_small(m_, shapes), _unpack_small(v_, shapes)):
        delta[n], new_m[n], new_v[n] = a, b, c_

    return (loss, grad_x[None], *[grad[n] for n in WEIGHTS], *[delta[n] for n in WEIGHTS],
            *[new_m[n] for n in WEIGHTS], *[new_v[n] for n in WEIGHTS])
```

```python
import functools

import jax
import jax.numpy as jnp
import numpy as np
from jax import lax
from jax.experimental import pallas as pl
from jax.experimental.pallas import tpu as pltpu

F32 = jnp.float32
BF16 = jnp.bfloat16
MESH = pl.DeviceIdType.MESH

D = 2048
HD = 64
NQH = 16
NKV = 4
QD = NQH * HD
KVD = NKV * HD
DI = 2048
NH = 32
NG = 4
NS = 128
CW = 4
L = 128
CONV = DI + 2 * NG * NS
FFN = 5632
PLE = 256
IN_DIM = QD + 2 * KVD + DI + CONV + NH + 2 * D
EPS = 1e-6
SSM_EPS = 1e-5
ROPE_THETA = 10000.0
LR, B1, B2, AEPS, WD, STEP = 0.001, 0.9, 0.999, 1e-08, 0.01, 10

O_GA, O_GS, O_Z, O_XBC, O_Q, O_K, O_V, O_DT = 0, 2048, 4096, 6144, 9216, 10240, 10496, 10752
DT_PAD = 512
NP = O_DT + DT_PAD
R_Q, R_K, R_V, R_Z, R_XBC, R_DT, R_GA, R_GS = 0, 1024, 1280, 1536, 3584, 6656, 6688, 8736

NCHIP = 4
VMEM_LIMIT = 52 * 1024 * 1024
NEG = -1e30


def _cp(sem=None):
    return pltpu.CompilerParams(dimension_semantics=sem, vmem_limit_bytes=VMEM_LIMIT)


def _dot(a, b):
    return lax.dot_general(a, b, (((1,), (0,)), ((), ())), preferred_element_type=F32)


def _dot_nt(a, b):
    return lax.dot_general(a, b, (((1,), (1,)), ((), ())), preferred_element_type=F32)


def _dot_tn(a, b):
    return lax.dot_general(a, b, (((0,), (0,)), ((), ())), preferred_element_type=F32)


def _sigmoid(x):
    return 1.0 / (1.0 + jnp.exp(-x))


def _bf16_dot(dot, da, db):
    @jax.custom_vjp
    def f(a, b):
        return dot(a.astype(BF16), b.astype(BF16))

    def fwd(a, b):
        return f(a, b), (a.astype(BF16), b.astype(BF16))

    def bwd(res, g):
        a, b = res
        g = g.astype(BF16)
        return da(g, a, b), db(g, a, b)

    f.defvjp(fwd, bwd)
    return f


_bdot = _bf16_dot(_dot, lambda g, a, b: _dot_nt(g, b), lambda g, a, b: _dot_tn(a, g))
_bdot_nt = _bf16_dot(_dot_nt, lambda g, a, b: _dot(g, b), lambda g, a, b: _dot_tn(g, a))
_bdot_tn = _bf16_dot(_dot_tn, lambda g, a, b: _dot_nt(b, g), lambda g, a, b: _dot(a, g))


ANY = pl.BlockSpec(memory_space=pl.ANY)


class _Job:
    srcs, dsts, news, scratch = (), (), (), ()
    has_mid = False

    def start(self, srcs, dsts, news, sems):
        raise NotImplementedError

    def mid(self, srcs, dsts, news, sems):
        pass

    def late(self, srcs, dsts, news, sems):
        pass

    def finish(self, srcs, dsts, news, sems):
        raise NotImplementedError

    def done(self, dsts, news):
        pass


def _call(body, *, jobs=(), name, out_shape, in_specs, out_specs, grid=(), scratch_shapes=(), compiler_params=None,
          aliases=None):
    jobs = [j for j in jobs if j is not None]
    aliases = dict(aliases or {})
    if not jobs:
        return pl.pallas_call(body, name=name, out_shape=out_shape, in_specs=in_specs, out_specs=out_specs, grid=grid,
                              scratch_shapes=scratch_shapes, compiler_params=compiler_params,
                              input_output_aliases=aliases)
    single = not isinstance(out_shape, (tuple, list))
    outs = [out_shape] if single else list(out_shape)
    ospecs = [out_specs] if single else list(out_specs)
    n_in, n_out, n_scr = len(in_specs), len(outs), len(scratch_shapes)
    srcs = [a for j in jobs for a in j.srcs]
    dsts = [a for j in jobs for a in j.dsts]
    news = [a for j in jobs for a in j.news]
    sems = [a for j in jobs for a in j.scratch]

    def wrapped(*refs):
        pos = n_in + len(srcs) + len(dsts)
        ins, jsrc = refs[:n_in], refs[n_in:n_in + len(srcs)]
        o_refs = refs[pos:pos + n_out]
        pos += n_out
        jdst, jnew = refs[pos:pos + len(dsts)], refs[pos + len(dsts):pos + len(dsts) + len(news)]
        pos += len(dsts) + len(news)
        scr, jsem = refs[pos:pos + n_scr], refs[pos + n_scr:]

        def run(which):
            a = b = c = d = 0
            for j in jobs:
                getattr(j, which)(jsrc[a:a + len(j.srcs)], jdst[b:b + len(j.dsts)], jnew[c:c + len(j.news)],
                                  jsem[d:d + len(j.scratch)])
                a, b, c, d = a + len(j.srcs), b + len(j.dsts), c + len(j.news), d + len(j.scratch)

        if not grid:
            run("start")
            run("mid")
            run("late")
            body(*ins, *o_refs, *scr)
            run("finish")
            return
        step = functools.reduce(lambda acc, a: acc * grid[a] + pl.program_id(a), range(len(grid)), 0)
        steps = int(np.prod(grid))
        pl.when(step == 0)(lambda: run("start"))
        if any(j.has_mid for j in jobs):
            pl.when(step == steps // 3)(lambda: run("mid"))
            pl.when(step == (2 * steps) // 3)(lambda: run("late"))
        body(*ins, *o_refs, *scr)
        pl.when(step == steps - 1)(lambda: run("finish"))

    call = pl.pallas_call(
        wrapped, name=name,
        out_shape=outs + [jax.ShapeDtypeStruct(a.shape, a.dtype) for a in dsts] + news,
        in_specs=list(in_specs) + [ANY] * (len(srcs) + len(dsts)),
        out_specs=ospecs + [ANY] * (len(dsts) + len(news)),
        grid=grid, scratch_shapes=list(scratch_shapes) + sems,
        input_output_aliases={**aliases, **{n_in + len(srcs) + i: n_out + i for i in range(len(dsts))}},
        compiler_params=_cp(("arbitrary",) * len(grid) if grid else None))

    def run_call(*args):
        res = call(*args, *srcs, *dsts)
        b, c = n_out, n_out + len(dsts)
        for j in jobs:
            j.done(res[b:b + len(j.dsts)], res[c:c + len(j.news)])
            b, c = b + len(j.dsts), c + len(j.news)
        return res[0] if single else tuple(res[:n_out])

    return run_call


def _matmul(a, b, *, ta=False, tb=False, out_dtype=F32, add=None, tm, tn, tk, name, jobs=()):
    k, m = a.shape if ta else a.shape[::-1]
    n = b.shape[0] if tb else b.shape[1]
    assert (b.shape[1] if tb else b.shape[0]) == k and not (ta and tb)
    assert m % tm == 0 and n % tn == 0 and k % tk == 0, (name, a.shape, b.shape)
    nk = k // tk
    has_add = add is not None

    def body(*refs):
        a_ref, b_ref = refs[0], refs[1]
        add_ref = refs[2] if has_add else None
        o_ref = refs[3] if has_add else refs[2]
        av = a_ref[...].astype(BF16)
        bv = b_ref[...].astype(BF16)
        part = _dot_tn(av, bv) if ta else _dot_nt(av, bv) if tb else _dot(av, bv)

        def finish(r):
            if has_add:
                r = r + add_ref[...]
            o_ref[...] = r.astype(out_dtype)

        if nk == 1:
            finish(part)
        elif out_dtype == F32:
            kk = pl.program_id(2)
            pl.when(kk == 0)(lambda: finish(part))

            @pl.when(kk > 0)
            def _():
                o_ref[...] += part
        else:
            acc_ref = refs[-1]
            kk = pl.program_id(2)

            @pl.when(kk == 0)
            def _():
                acc_ref[...] = part

            @pl.when(kk > 0)
            def _():
                acc_ref[...] += part

            @pl.when(kk == nk - 1)
            def _():
                finish(acc_ref[...])

    in_specs = [pl.BlockSpec((tk, tm), lambda i, j, kk: (kk, i)) if ta else pl.BlockSpec((tm, tk), lambda i, j, kk: (i, kk)),
                pl.BlockSpec((tn, tk), lambda i, j, kk: (j, kk)) if tb
                else pl.BlockSpec((tk, tn), lambda i, j, kk: (kk, j))]
    args = [a, b]
    if has_add:
        in_specs.append(pl.BlockSpec((tm, tn), lambda i, j, kk: (i, j)))
        args.append(add)
    return _call(
        body, jobs=jobs, name=name,
        out_shape=jax.ShapeDtypeStruct((m, n), out_dtype),
        grid=(m // tm, n // tn, nk),
        in_specs=in_specs,
        out_specs=pl.BlockSpec((tm, tn), lambda i, j, kk: (i, j)),
        scratch_shapes=[pltpu.VMEM((tm, tn), F32)] if nk > 1 and out_dtype != F32 else [],
        compiler_params=_cp(("parallel", "parallel", "arbitrary")),
    )(*args)


ROWS = 256


def _rmsnorm_fwd(x, g, *, name):
    t, d = x.shape

    def body(x_ref, g_ref, o_ref):
        xv = x_ref[...]
        r = lax.rsqrt(jnp.mean(xv * xv, axis=-1, keepdims=True) + EPS)
        o_ref[...] = (xv * r * g_ref[...]).astype(BF16)

    return pl.pallas_call(
        body, name=name, out_shape=jax.ShapeDtypeStruct((t, d), BF16), grid=(t // ROWS,),
        in_specs=[pl.BlockSpec((ROWS, d), lambda i: (i, 0)), pl.BlockSpec((1, d), lambda i: (0, 0))],
        out_specs=pl.BlockSpec((ROWS, d), lambda i: (i, 0)), compiler_params=_cp(("parallel",)),
    )(x, g)


def _rmsnorm_bwd(x, g, dy, dres, *, name, jobs=()):
    t, d = x.shape

    def body(x_ref, g_ref, dy_ref, dres_ref, dx_ref, dxb_ref, dg_ref):
        xv = x_ref[...]
        r = lax.rsqrt(jnp.mean(xv * xv, axis=-1, keepdims=True) + EPS)
        xh = xv * r
        dyv = dy_ref[...]
        dxh = dyv * g_ref[...]
        dx = r * (dxh - xh * jnp.mean(dxh * xh, axis=-1, keepdims=True))
        tot = dres_ref[...] + dx
        dx_ref[...] = tot
        dxb_ref[...] = tot.astype(BF16)

        @pl.when(pl.program_id(0) == 0)
        def _():
            dg_ref[...] = jnp.zeros_like(dg_ref)

        dg_ref[...] += jnp.broadcast_to(jnp.sum(dyv * xh, axis=0, keepdims=True), dg_ref.shape)

    row = pl.BlockSpec((ROWS, d), lambda i: (i, 0))
    return _call(
        body, jobs=jobs, name=name,
        out_shape=(jax.ShapeDtypeStruct((t, d), F32), jax.ShapeDtypeStruct((t, d), BF16),
                   jax.ShapeDtypeStruct((8, d), F32)),
        grid=(t // ROWS,),
        in_specs=[row, pl.BlockSpec((1, d), lambda i: (0, 0)), row, row],
        out_specs=(row, row, pl.BlockSpec((8, d), lambda i: (0, 0))),
        compiler_params=_cp(("arbitrary",)),
    )(x, g, dy, dres)


def _final(h2, pgl, pp, target, g_final, *, name):
    t, d = h2.shape

    def body(h2_ref, pgl_ref, pp_ref, tg_ref, g_ref, dh3_ref, dpgl_ref, dpp_ref, loss_ref, dg_ref):
        s = _sigmoid(pgl_ref[...])
        ppv = pp_ref[...]
        h3 = h2_ref[...] + s * ppv
        r = lax.rsqrt(jnp.mean(h3 * h3, axis=-1, keepdims=True) + EPS)
        xh = h3 * r
        gv = g_ref[...]
        err = xh * gv - tg_ref[...]
        dyv = err * (1.0 / d)
        dxh = dyv * gv
        dh3 = r * (dxh - xh * jnp.mean(dxh * xh, axis=-1, keepdims=True))
        dh3_ref[...] = dh3
        dpp_ref[...] = (dh3 * s).astype(BF16)
        dpgl_ref[...] = (dh3 * ppv * s * (1.0 - s)).astype(BF16)

        @pl.when(pl.program_id(0) == 0)
        def _():
            loss_ref[...] = jnp.zeros_like(loss_ref)
            dg_ref[...] = jnp.zeros_like(dg_ref)

        part = 0.5 * jnp.sum(jnp.mean(err * err, axis=-1, keepdims=True), axis=0, keepdims=True)
        loss_ref[...] += jnp.broadcast_to(part, loss_ref.shape)
        dg_ref[...] += jnp.broadcast_to(jnp.sum(dyv * xh, axis=0, keepdims=True), dg_ref.shape)

    row = pl.BlockSpec((ROWS, d), lambda i: (i, 0))
    return pl.pallas_call(
        body, name=name,
        out_shape=(jax.ShapeDtypeStruct((t, d), F32), jax.ShapeDtypeStruct((t, d), BF16),
                   jax.ShapeDtypeStruct((t, d), BF16), jax.ShapeDtypeStruct((8, 128), F32),
                   jax.ShapeDtypeStruct((8, d), F32)),
        grid=(t // ROWS,),
        in_specs=[row, row, row, row, pl.BlockSpec((1, d), lambda i: (0, 0))],
        out_specs=(row, row, row, pl.BlockSpec((8, 128), lambda i: (0, 0)), pl.BlockSpec((8, d), lambda i: (0, 0))),
        compiler_params=_cp(("arbitrary",)),
    )(h2, pgl, pp, target, g_final)


def _merge_fwd(proj, out_a, out_s, *, name):
    t = proj.shape[0]

    def body(ga_ref, gs_ref, a_ref, s_ref, o_ref):
        o_ref[...] = (_sigmoid(ga_ref[...]) * a_ref[...] + _sigmoid(gs_ref[...]) * s_ref[...]).astype(BF16)

    row = pl.BlockSpec((ROWS, D), lambda i: (i, 0))
    return pl.pallas_call(
        body, name=name, out_shape=jax.ShapeDtypeStruct((t, D), BF16), grid=(t // ROWS,),
        in_specs=[pl.BlockSpec((ROWS, D), lambda i: (i, O_GA // D)), pl.BlockSpec((ROWS, D), lambda i: (i, O_GS // D)),
                  row, row],
        out_specs=row, compiler_params=_cp(("parallel",)),
    )(proj, proj, out_a, out_s)


def _merge_bwd(proj, out_a, out_s, dmerged, *, name):
    t = proj.shape[0]
    assert O_GA == 0 and O_GS == D

    def body(ga_ref, gs_ref, a_ref, s_ref, dm_ref, da_ref, ds_ref, dp_ref):
        sa = _sigmoid(ga_ref[...])
        ss = _sigmoid(gs_ref[...])
        dm = dm_ref[...]
        da_ref[...] = (dm * sa).astype(BF16)
        ds_ref[...] = (dm * ss).astype(BF16)
        dp_ref[:, :D] = (dm * a_ref[...] * sa * (1.0 - sa)).astype(BF16)
        dp_ref[:, D:] = (dm * s_ref[...] * ss * (1.0 - ss)).astype(BF16)

    row = pl.BlockSpec((ROWS, D), lambda i: (i, 0))
    o = jax.ShapeDtypeStruct((t, D), BF16)
    return pl.pallas_call(
        body, name=name, out_shape=(o, o, jax.ShapeDtypeStruct((t, NP), BF16)), grid=(t // ROWS,),
        in_specs=[pl.BlockSpec((ROWS, D), lambda i: (i, O_GA // D)), pl.BlockSpec((ROWS, D), lambda i: (i, O_GS // D)),
                  row, row, row],
        out_specs=(row, row, pl.BlockSpec((ROWS, 2 * D), lambda i: (i, 0))), compiler_params=_cp(("parallel",)),
    )(proj, proj, out_a, out_s, dmerged)


def _swiglu_fwd(f, w_gate, w_up, *, name, tn=512, jobs=()):
    t, d = f.shape
    n = w_gate.shape[1]

    def body(f_ref, wg_ref, wu_ref, g_ref, u_ref, a_ref):
        fv = f_ref[...]
        g = _dot(fv, wg_ref[...])
        u = _dot(fv, wu_ref[...])
        g_ref[...] = g.astype(BF16)
        u_ref[...] = u.astype(BF16)
        a_ref[...] = (g * _sigmoid(g) * u).astype(BF16)

    col = pl.BlockSpec((t, tn), lambda j: (0, j))
    wcol = pl.BlockSpec((d, tn), lambda j: (0, j))
    return _call(
        body, jobs=jobs, name=name,
        out_shape=(jax.ShapeDtypeStruct((t, n), BF16), jax.ShapeDtypeStruct((t, n), BF16),
                   jax.ShapeDtypeStruct((t, n), BF16)),
        grid=(n // tn,),
        in_specs=[pl.BlockSpec((t, d), lambda j: (0, 0)), wcol, wcol],
        out_specs=(col, col, col), compiler_params=_cp(("parallel",)),
    )(f, w_gate, w_up)


def _swiglu_bwd(dh, w_down, gate, up, *, name, tn=512, jobs=()):
    t, d = dh.shape
    n = w_down.shape[0]

    def body(dh_ref, w_ref, g_ref, u_ref, dg_ref, du_ref):
        da = _dot_nt(dh_ref[...], w_ref[...])
        g = g_ref[...].astype(F32)
        s = _sigmoid(g)
        du_ref[...] = (da * g * s).astype(BF16)
        dg_ref[...] = (da * u_ref[...].astype(F32) * s * (1.0 + g * (1.0 - s))).astype(BF16)

    col = pl.BlockSpec((t, tn), lambda j: (0, j))
    o = jax.ShapeDtypeStruct((t, n), BF16)
    return _call(
        body, jobs=jobs, name=name, out_shape=(o, o), grid=(n // tn,),
        in_specs=[pl.BlockSpec((t, d), lambda j: (0, 0)), pl.BlockSpec((tn, d), lambda j: (j, 0)), col, col],
        out_specs=(col, col), compiler_params=_cp(("parallel",)),
    )(dh, w_down, gate, up)


def _gated_norm_fwd(y_pre, proj, g_ssd, *, name):
    t = y_pre.shape[0]

    def body(y_ref, z_ref, g_ref, o_ref):
        z = z_ref[...]
        v = y_ref[...] * z * _sigmoid(z)
        r = lax.rsqrt(jnp.mean(v * v, axis=-1, keepdims=True) + SSM_EPS)
        o_ref[...] = (v * r * g_ref[...]).astype(BF16)

    row = pl.BlockSpec((ROWS, DI), lambda i: (i, 0))
    return pl.pallas_call(
        body, name=name, out_shape=jax.ShapeDtypeStruct((t, DI), BF16), grid=(t // ROWS,),
        in_specs=[row, pl.BlockSpec((ROWS, DI), lambda i: (i, O_Z // DI)), pl.BlockSpec((1, DI), lambda i: (0, 0))],
        out_specs=row, compiler_params=_cp(("parallel",)),
    )(y_pre, proj, g_ssd)


def _gated_norm_bwd(y_pre, proj, g_ssd, dyn, dproj, *, name, jobs=()):
    t = y_pre.shape[0]

    def body(y_ref, z_ref, g_ref, dyn_ref, _, dy_ref, dz_ref, dg_ref):
        z = z_ref[...]
        s = _sigmoid(z)
        sz = z * s
        yv = y_ref[...]
        v = yv * sz
        r = lax.rsqrt(jnp.mean(v * v, axis=-1, keepdims=True) + SSM_EPS)
        vh = v * r
        dn = dyn_ref[...]
        dvh = dn * g_ref[...]
        dv = r * (dvh - vh * jnp.mean(dvh * vh, axis=-1, keepdims=True))
        dy_ref[...] = dv * sz
        dz_ref[...] = (dv * yv * s * (1.0 + z * (1.0 - s))).astype(BF16)

        @pl.when(pl.program_id(0) == 0)
        def _():
            dg_ref[...] = jnp.zeros_like(dg_ref)

        dg_ref[...] += jnp.broadcast_to(jnp.sum(dn * vh, axis=0, keepdims=True), dg_ref.shape)

    row = pl.BlockSpec((ROWS, DI), lambda i: (i, 0))
    return _call(
        body, jobs=jobs, name=name,
        out_shape=(jax.ShapeDtypeStruct((t, DI), F32), jax.ShapeDtypeStruct(dproj.shape, BF16),
                   jax.ShapeDtypeStruct((8, DI), F32)),
        grid=(t // ROWS,),
        in_specs=[row, pl.BlockSpec((ROWS, DI), lambda i: (i, O_Z // DI)), pl.BlockSpec((1, DI), lambda i: (0, 0)), row, ANY],
        out_specs=(row, pl.BlockSpec((ROWS, DI), lambda i: (i, O_Z // DI)), pl.BlockSpec((8, DI), lambda i: (0, 0))),
        compiler_params=_cp(("arbitrary",)), aliases={4: 1},
    )(y_pre, proj, g_ssd, dyn, dproj)


CONV_TC = 512


def _shift_down(x, s, row):
    if s == 0:
        return x
    return jnp.where(row >= s, pltpu.roll(x, s, 0), 0.0)


def _shift_up(x, s, row, t):
    if s == 0:
        return x
    return jnp.where(row < t - s, pltpu.roll(x, t - s, 0), 0.0)


def _conv_fwd(proj, conv_w, conv_b, *, name):
    t = proj.shape[0]

    def body(x_ref, w_ref, b_ref, o_ref):
        x = x_ref[...]
        row = lax.broadcasted_iota(jnp.int32, x.shape, 0)
        pre = jnp.broadcast_to(b_ref[...], x.shape)
        for k in range(CW):
            pre = pre + w_ref[k:k + 1, :] * _shift_down(x, CW - 1 - k, row)
        o_ref[...] = pre * _sigmoid(pre)

    return pl.pallas_call(
        body, name=name, out_shape=jax.ShapeDtypeStruct((t, CONV), F32), grid=(CONV // CONV_TC,),
        in_specs=[pl.BlockSpec((t, CONV_TC), lambda j: (0, O_XBC // CONV_TC + j)),
                  pl.BlockSpec((CW, CONV_TC), lambda j: (0, j)), pl.BlockSpec((1, CONV_TC), lambda j: (0, j))],
        out_specs=pl.BlockSpec((t, CONV_TC), lambda j: (0, j)), compiler_params=_cp(("parallel",)),
    )(proj, conv_w, conv_b)


def _conv_bwd(proj, conv_w, conv_b, dxs, db, dc, dproj, *, name, jobs=()):
    t = proj.shape[0]
    nx = DI // CONV_TC
    assert NG * NS == CONV_TC

    def body(x_ref, w_ref, b_ref, dxs_ref, db_ref, dc_ref, _, dx_ref, dw_ref, dbias_ref):
        j = pl.program_id(0)
        x = x_ref[...]
        row = lax.broadcasted_iota(jnp.int32, x.shape, 0)
        xs = [_shift_down(x, CW - 1 - k, row) for k in range(CW)]
        pre = jnp.broadcast_to(b_ref[...], x.shape)
        for k in range(CW):
            pre = pre + w_ref[k:k + 1, :] * xs[k]
        s = _sigmoid(pre)
        da = jnp.where(j < nx, dxs_ref[...], jnp.where(j == nx, db_ref[...], dc_ref[...]))
        dpre = da * s * (1.0 + pre * (1.0 - s))
        dx = jnp.zeros_like(x)
        row8 = lax.broadcasted_iota(jnp.int32, dw_ref.shape, 0)
        dw = jnp.zeros(dw_ref.shape, F32)
        for k in range(CW):
            dx = dx + w_ref[k:k + 1, :] * _shift_up(dpre, CW - 1 - k, row, t)
            dw = dw + jnp.where(row8 == k, jnp.sum(dpre * xs[k], axis=0, keepdims=True), 0.0)
        dx_ref[...] = dx.astype(BF16)
        dw_ref[...] = dw
        dbias_ref[...] = jnp.broadcast_to(jnp.sum(dpre, axis=0, keepdims=True), dbias_ref.shape)

    col8 = pl.BlockSpec((8, CONV_TC), lambda j: (0, j))
    xbc = pl.BlockSpec((t, CONV_TC), lambda j: (0, O_XBC // CONV_TC + j))
    whole = pl.BlockSpec((t, CONV_TC), lambda j: (0, 0))
    return _call(
        body, jobs=jobs, name=name,
        out_shape=(jax.ShapeDtypeStruct(dproj.shape, BF16), jax.ShapeDtypeStruct((8, CONV), F32),
                   jax.ShapeDtypeStruct((8, CONV), F32)),
        grid=(CONV // CONV_TC,),
        in_specs=[xbc, pl.BlockSpec((CW, CONV_TC), lambda j: (0, j)), pl.BlockSpec((1, CONV_TC), lambda j: (0, j)),
                  pl.BlockSpec((t, CONV_TC), lambda j: (0, jnp.minimum(j, nx - 1))), whole, whole, ANY],
        out_specs=(xbc, col8, col8),
        compiler_params=_cp(("arbitrary",)), aliases={6: 0},
    )(proj, conv_w, conv_b, dxs, db, dc, dproj)


def _rope_tables(positions, t):
    half = HD // 2
    inv_freq = ROPE_THETA ** (-jnp.arange(half, dtype=F32) * 2.0 / HD)
    ang = positions.reshape(t).astype(F32)[:, None] * inv_freq
    cos, sin = jnp.cos(ang), jnp.sin(ang)
    return jnp.concatenate([cos] * 4, axis=1), jnp.concatenate([-sin, sin] * 2, axis=1)


def _lane_consts():
    lane = lax.broadcasted_iota(jnp.int32, (L, 128), 1)
    return lane, (lane % HD) < (HD // 2), lane < HD


def _rope(tv, cos, sin, lo):
    return tv * cos + jnp.where(lo, pltpu.roll(tv, 128 - HD // 2, 1), pltpu.roll(tv, HD // 2, 1)) * sin


def _rope_t(dv, cos, sin, lo):
    ds = dv * sin
    return dv * cos + jnp.where(lo, pltpu.roll(ds, 128 - HD // 2, 1), pltpu.roll(ds, HD // 2, 1))


def _placed(chunk, g, half0):
    own = jnp.where(half0 if g % 2 == 0 else jnp.logical_not(half0), chunk, 0.0)
    other = pltpu.roll(own, HD, 1)
    return (own, other) if g % 2 == 0 else (other, own)


def _unplace(acc, hf, g, half0):
    v = jnp.where(half0 if hf == 0 else jnp.logical_not(half0), acc, 0.0)
    return v if hf == g % 2 else pltpu.roll(v, HD, 1)


def _attn_fwd(proj, cos, sin, sinks, *, name, jobs=()):
    t = proj.shape[0]
    nb = t // L
    scale = HD ** -0.5

    def body(sink_ref, q_ref, kc_ref, kp_ref, vc_ref, vp_ref, cc_ref, sc_ref, cp_ref, sp_ref, o_ref, lse_ref):
        i = pl.program_id(0)
        lane, lo, half0 = _lane_consts()
        cos_c, sin_c, cos_p, sin_p = cc_ref[...], sc_ref[...], cp_ref[...], sp_ref[...]
        row = lax.broadcasted_iota(jnp.int32, (L, 2 * L), 0)
        col = lax.broadcasted_iota(jnp.int32, (L, 2 * L), 1)
        valid = jnp.logical_or(jnp.logical_and(jnp.logical_and(col < L, col > row), i > 0),
                               jnp.logical_and(col >= L, col - L <= row))
        kc = [_rope(kc_ref[:, 128 * m:128 * (m + 1)], cos_c, sin_c, lo) for m in range(2)]
        kp = [_rope(kp_ref[:, 128 * m:128 * (m + 1)], cos_p, sin_p, lo) for m in range(2)]
        lse_acc = jnp.zeros((L, 128), F32)
        outs = [jnp.zeros((L, 128), F32) for _ in range(QD // 128)]
        qs = [(_rope(q_ref[:, 128 * ch:128 * (ch + 1)], cos_c, sin_c, lo) * scale).astype(BF16) for ch in range(QD // 128)]
        both = lambda prev, cur, g: [jnp.concatenate([a, b], axis=0).astype(BF16)
                                     for a, b in zip(_placed(prev, g, half0), _placed(cur, g, half0))]
        for g in range(NKV):
            sl = slice(128 * (g // 2), 128 * (g // 2 + 1))
            kv = both(kp[g // 2], kc[g // 2], g)
            vv = both(vp_ref[:, sl], vc_ref[:, sl], g)
            for r in range(NQH // NKV):
                h = g * (NQH // NKV) + r
                ch, hf = h // 2, h % 2
                s = jnp.where(valid, _dot_nt(qs[ch], kv[hf]), NEG)
                sink = sink_ref[0, h]
                mx = jnp.maximum(jnp.max(s, axis=-1, keepdims=True), sink)
                e = jnp.exp(s - mx)
                den = jnp.sum(e, axis=-1, keepdims=True) + jnp.exp(sink - mx)
                outs[ch] = outs[ch] + _dot((e * (1.0 / den)).astype(BF16), vv[hf])
                lse_acc = jnp.where(lane == h, mx + jnp.log(den), lse_acc)
        for ch in range(QD // 128):
            o_ref[:, 128 * ch:128 * (ch + 1)] = outs[ch].astype(BF16)
        lse_ref[...] = lse_acc

    prev = lambda i: jnp.maximum(i - 1, 0)
    tab_c = pl.BlockSpec((L, 128), lambda i: (i, 0))
    tab_p = pl.BlockSpec((L, 128), lambda i: (prev(i), 0))
    return _call(
        body, jobs=jobs, name=name,
        out_shape=(jax.ShapeDtypeStruct((t, QD), BF16), jax.ShapeDtypeStruct((t, 128), F32)),
        grid=(nb,),
        in_specs=[pl.BlockSpec(memory_space=pltpu.SMEM),
                  pl.BlockSpec((L, QD), lambda i: (i, O_Q // QD)),
                  pl.BlockSpec((L, KVD), lambda i: (i, O_K // KVD)), pl.BlockSpec((L, KVD), lambda i: (prev(i), O_K // KVD)),
                  pl.BlockSpec((L, KVD), lambda i: (i, O_V // KVD)), pl.BlockSpec((L, KVD), lambda i: (prev(i), O_V // KVD)),
                  tab_c, tab_c, tab_p, tab_p],
        out_specs=(pl.BlockSpec((L, QD), lambda i: (i, 0)), pl.BlockSpec((L, 128), lambda i: (i, 0))),
        compiler_params=_cp(("parallel",)),
    )(sinks, proj, proj, proj, proj, proj, cos, sin, cos, sin)


def _attn_bwd(proj, cos, sin, sinks, attn, lse, dattn, dproj, *, name, jobs=()):
    t = proj.shape[0]
    nb = t // L
    scale = HD ** -0.5

    def body(sink_ref, qi_ref, qn_ref, kc_ref, kp_ref, vc_ref, vp_ref, doi_ref, don_ref, oi_ref, on_ref,
             lsei_ref, lsen_ref, cc_ref, sc_ref, cp_ref, sp_ref, cn_ref, sn_ref, _, dqkv_ref, dsk_ref):
        i = pl.program_id(0)
        lane, lo, half0 = _lane_consts()
        half1 = jnp.logical_not(half0)
        cos_c, sin_c = cc_ref[...], sc_ref[...]
        row = lax.broadcasted_iota(jnp.int32, (L, 2 * L), 0)
        col = lax.broadcasted_iota(jnp.int32, (L, 2 * L), 1)
        valid = jnp.logical_or(jnp.logical_and(jnp.logical_and(col < L, col > row), i > 0),
                               jnp.logical_and(col >= L, col - L <= row))
        m_next = jnp.logical_and(col[:, :L] > row[:, :L], i < nb - 1)
        kc = [_rope(kc_ref[:, 128 * m:128 * (m + 1)], cos_c, sin_c, lo) for m in range(2)]
        kp = [_rope(kp_ref[:, 128 * m:128 * (m + 1)], cp_ref[...], sp_ref[...], lo) for m in range(2)]
        lse_i, lse_n = lsei_ref[...], lsen_ref[...]
        dk_acc = [jnp.zeros((L, 128), F32) for _ in range(2)]
        dv_acc = [jnp.zeros((L, 128), F32) for _ in range(2)]
        dsk_acc = jnp.zeros((1, 128), F32)
        lane1 = lax.broadcasted_iota(jnp.int32, (1, 128), 1)
        both = lambda prev, cur, g: [jnp.concatenate([a, b], axis=0).astype(BF16)
                                     for a, b in zip(_placed(prev, g, half0), _placed(cur, g, half0))]
        kvs = [both(kp[g // 2], kc[g // 2], g) for g in range(NKV)]
        vvs = [both(vp_ref[:, 128 * (g // 2):128 * (g // 2 + 1)], vc_ref[:, 128 * (g // 2):128 * (g // 2 + 1)], g)
               for g in range(NKV)]
        for ch in range(QD // 128):
            sl = slice(128 * ch, 128 * (ch + 1))
            q_i = (_rope(qi_ref[:, sl], cos_c, sin_c, lo) * scale).astype(BF16)
            q_n = (_rope(qn_ref[:, sl], cn_ref[...], sn_ref[...], lo) * scale).astype(BF16)
            q_in = jnp.concatenate([q_i, q_n], axis=0)
            do_i, do_n = doi_ref[:, sl], don_ref[:, sl]
            do_ib, do_nb = do_i.astype(BF16), do_n.astype(BF16)
            do_in = jnp.concatenate([do_ib, do_nb], axis=0)
            od_i = do_i * oi_ref[:, sl].astype(F32)
            od_n = do_n * on_ref[:, sl].astype(F32)
            dq_ch = jnp.zeros((L, 128), F32)
            for hf in range(2):
                h = 2 * ch + hf
                g = h // (NQH // NKV)
                hm = half0 if hf == 0 else half1
                kv, vv = kvs[g][hf], vvs[g][hf]
                kcv, vcv = kv[L:], vv[L:]
                dl_i = jnp.sum(jnp.where(hm, od_i, 0.0), axis=-1, keepdims=True)
                dl_n = jnp.sum(jnp.where(hm, od_n, 0.0), axis=-1, keepdims=True)
                ls_i = jnp.sum(jnp.where(lane == h, lse_i, 0.0), axis=-1, keepdims=True)
                ls_n = jnp.sum(jnp.where(lane == h, lse_n, 0.0), axis=-1, keepdims=True)
                p = jnp.where(valid, jnp.exp(_dot_nt(q_i, kv) - ls_i), 0.0)
                ds = (p * (_dot_nt(do_ib, vv) - dl_i)).astype(BF16)
                dq_ch = dq_ch + jnp.where(hm, _dot(ds, kv) * scale, 0.0)
                sink = sink_ref[0, h]
                dsk = -jnp.sum(jnp.exp(sink - ls_i) * dl_i, axis=0, keepdims=True)
                dsk_acc = dsk_acc + jnp.where(lane1 == h, dsk, 0.0)
                p_n = jnp.where(m_next, jnp.exp(_dot_nt(q_n, kcv) - ls_n), 0.0)
                ds_n = (p_n * (_dot_nt(do_nb, vcv) - dl_n)).astype(BF16)
                dv_h = _dot_tn(jnp.concatenate([p[:, L:].astype(BF16), p_n.astype(BF16)], axis=0), do_in)
                dk_h = _dot_tn(jnp.concatenate([ds[:, L:], ds_n], axis=0), q_in)
                dv_acc[g // 2] = dv_acc[g // 2] + _unplace(dv_h, hf, g, half0)
                dk_acc[g // 2] = dk_acc[g // 2] + _unplace(dk_h, hf, g, half0)
            dqkv_ref[:, sl] = _rope_t(dq_ch, cos_c, sin_c, lo).astype(BF16)
        for m in range(2):
            dqkv_ref[:, QD + 128 * m:QD + 128 * (m + 1)] = _rope_t(dk_acc[m], cos_c, sin_c, lo).astype(BF16)
            dqkv_ref[:, QD + KVD + 128 * m:QD + KVD + 128 * (m + 1)] = dv_acc[m].astype(BF16)

        @pl.when(i == 0)
        def _():
            dsk_ref[...] = jnp.zeros_like(dsk_ref)

        dsk_ref[...] += jnp.broadcast_to(dsk_acc, dsk_ref.shape)

    prev = lambda i: jnp.maximum(i - 1, 0)
    nxt = lambda i: jnp.minimum(i + 1, nb - 1)
    cur_q = pl.BlockSpec((L, QD), lambda i: (i, 0))
    nxt_q = pl.BlockSpec((L, QD), lambda i: (nxt(i), 0))
    tab = lambda f: pl.BlockSpec((L, 128), lambda i: (f(i), 0))
    ident = lambda i: i
    qkv = QD + 2 * KVD
    assert O_K == O_Q + QD and O_V == O_K + KVD and O_Q % qkv == 0
    return _call(
        body, jobs=jobs, name=name,
        out_shape=(jax.ShapeDtypeStruct(dproj.shape, BF16), jax.ShapeDtypeStruct((8, 128), F32)),
        grid=(nb,),
        in_specs=[pl.BlockSpec(memory_space=pltpu.SMEM),
                  pl.BlockSpec((L, QD), lambda i: (i, O_Q // QD)), pl.BlockSpec((L, QD), lambda i: (nxt(i), O_Q // QD)),
                  pl.BlockSpec((L, KVD), lambda i: (i, O_K // KVD)), pl.BlockSpec((L, KVD), lambda i: (prev(i), O_K // KVD)),
                  pl.BlockSpec((L, KVD), lambda i: (i, O_V // KVD)), pl.BlockSpec((L, KVD), lambda i: (prev(i), O_V // KVD)),
                  cur_q, nxt_q, cur_q, nxt_q, tab(ident), tab(nxt),
                  tab(ident), tab(ident), tab(prev), tab(prev), tab(nxt), tab(nxt), ANY],
        out_specs=(pl.BlockSpec((L, qkv), lambda i: (i, O_Q // qkv)), pl.BlockSpec((8, 128), lambda i: (0, 0))),
        compiler_params=_cp(("arbitrary",)), aliases={19: 0},
    )(sinks, proj, proj, proj, proj, proj, proj, dattn, dattn, attn, attn, lse, lse, cos, sin, cos, sin, cos, sin, dproj)


PAIRS = NH // NG // 2


def _softplus(x):
    return jnp.maximum(x, 0.0) + jnp.log(1.0 + jnp.exp(-jnp.abs(x)))


def _ssd_chunk(g, xps, dtr, bm, cm, sps, dtb, alog, dsk):
    lane = lax.broadcasted_iota(jnp.int32, (L, 128), 1)
    lane1 = lax.broadcasted_iota(jnp.int32, (1, 128), 1)
    row = lax.broadcasted_iota(jnp.int32, (L, L), 0)
    col = lax.broadcasted_iota(jnp.int32, (L, L), 1)
    rowc = lax.broadcasted_iota(jnp.int32, (128, 1), 0)
    tril = col <= row
    dt = _softplus(dtr + dtb)
    a = dt * (-jnp.exp(alog))
    a_cs = lax.dot_general(tril.astype(F32), a, (((1,), (0,)), ((), ())), precision=lax.Precision.HIGHEST,
                           preferred_element_type=F32)
    a_cst = a_cs.T
    a_last = jnp.sum(jnp.where(row == L - 1, a_cs, 0.0), axis=0, keepdims=True)
    cb = _bdot_nt(cm, bm)
    ys, snew = [], []
    for q in range(PAIRS):
        xp, sp = xps[q], sps[q]
        skip = jnp.zeros((L, 128), F32)
        keep = jnp.zeros((128, 1), F32)
        ms, xds, cds, sms, bds = [], [], [], [], []
        for hh in range(2):
            h = g * 2 * PAIRS + 2 * q + hh
            hm = (lane < HD) if hh == 0 else (lane >= HD)
            rm = (rowc < HD) if hh == 0 else (rowc >= HD)
            dt_h = jnp.sum(jnp.where(lane == h, dt, 0.0), axis=1, keepdims=True)
            acs_h = jnp.sum(jnp.where(lane == h, a_cs, 0.0), axis=1, keepdims=True)
            acst_h = jnp.sum(jnp.where(row == h, a_cst, 0.0), axis=0, keepdims=True)
            al_h = jnp.sum(jnp.where(lane1 == h, a_last, 0.0), axis=1, keepdims=True)
            dsk_h = jnp.sum(jnp.where(lane1 == h, dsk, 0.0), axis=1, keepdims=True)
            decay = jnp.where(tril, jnp.exp(jnp.where(tril, acs_h - acst_h, 0.0)), 0.0)
            xh = jnp.where(hm, xp, 0.0)
            ms.append(cb * decay)
            xds.append(xh * dt_h)
            cds.append(cm * jnp.exp(acs_h))
            sms.append(jnp.where(rm, sp, 0.0))
            bds.append(bm * jnp.exp(al_h - acs_h))
            skip = skip + dsk_h * xh
            keep = keep + jnp.where(rm, jnp.exp(al_h), 0.0)
        xd2 = jnp.concatenate(xds, axis=0)
        y_pair = (_bdot(jnp.concatenate(ms, axis=1), xd2)
                  + _bdot_nt(jnp.concatenate(cds, axis=1), jnp.concatenate(sms, axis=1)) + skip)
        ys.append(y_pair)
        snew.append(sp * keep + _bdot_tn(xd2, jnp.concatenate(bds, axis=0)))
    return ys, snew


def _ssd_specs(t):
    nc = t // L
    xs = lambda f: pl.BlockSpec((L, 128 * PAIRS), lambda c, g: (f(c), g))
    bspec = lambda f: pl.BlockSpec((L, NS), lambda c, g: (f(c), DI // NS + g))
    cspec = lambda f: pl.BlockSpec((L, NS), lambda c, g: (f(c), DI // NS + NG + g))
    dts = lambda f: pl.BlockSpec((L, 128), lambda c, g: (f(c), O_DT // 128))
    par = pl.BlockSpec((1, 128), lambda c, g: (0, 0))
    st = lambda f: pl.BlockSpec((1, 1, PAIRS, 128, NS), lambda c, g: (f(c), g, 0, 0, 0))
    return nc, xs, bspec, cspec, dts, par, st


def _ssd_fwd(xbc_act, proj, dtb, alog, dsk, *, name, jobs=()):
    t = proj.shape[0]
    nc, xs, bspec, cspec, dts, par, st = _ssd_specs(t)
    ident = lambda c: c

    def body(x_ref, b_ref, c_ref, dt_ref, dtb_ref, al_ref, dsk_ref, y_ref, sin_ref, s_ref):
        c, g = pl.program_id(0), pl.program_id(1)

        @pl.when(c == 0)
        def _():
            s_ref[g] = jnp.zeros((PAIRS, 128, NS), F32)

        sps = [s_ref[g, q] for q in range(PAIRS)]
        for q in range(PAIRS):
            sin_ref[0, 0, q] = sps[q]
        xps = [x_ref[:, 128 * q:128 * (q + 1)] for q in range(PAIRS)]
        ys, snew = _ssd_chunk(g, xps, dt_ref[...], b_ref[...], c_ref[...], sps, dtb_ref[...], al_ref[...], dsk_ref[...])
        for q in range(PAIRS):
            y_ref[:, 128 * q:128 * (q + 1)] = ys[q]
            s_ref[g, q] = snew[q]

    return _call(
        body, jobs=jobs, name=name,
        out_shape=(jax.ShapeDtypeStruct((t, DI), F32), jax.ShapeDtypeStruct((nc, NG, PAIRS, 128, NS), F32)),
        grid=(nc, NG),
        in_specs=[xs(ident), bspec(ident), cspec(ident), dts(ident), par, par, par],
        out_specs=(pl.BlockSpec((L, 128 * PAIRS), lambda c, g: (c, g)), st(ident)),
        scratch_shapes=[pltpu.VMEM((NG, PAIRS, 128, NS), F32)],
        compiler_params=_cp(("arbitrary", "arbitrary")),
    )(xbc_act, xbc_act, xbc_act, proj, dtb, alog, dsk)


def _ssd_bwd(xbc_act, proj, dtb, alog, dsk, states, dy, dproj, *, name, jobs=()):
    t = proj.shape[0]
    nc, xs, bspec, cspec, dts, par, st = _ssd_specs(t)
    rev = lambda c: nc - 1 - c

    def body(x_ref, b_ref, c_ref, dt_ref, dtb_ref, al_ref, dsk_ref, sin_ref, dy_ref, _,
             dx_ref, db_ref, dc_ref, ddtp_ref, ddtb_ref, dal_ref, ddsk_ref, ds_ref, ddt_ref):
        c, g = pl.program_id(0), pl.program_id(1)

        @pl.when(c == 0)
        def _():
            ds_ref[g] = jnp.zeros((PAIRS, 128, NS), F32)

        @pl.when(jnp.logical_and(c == 0, g == 0))
        def _():
            ddtb_ref[...] = jnp.zeros_like(ddtb_ref)
            dal_ref[...] = jnp.zeros_like(dal_ref)
            ddsk_ref[...] = jnp.zeros_like(ddsk_ref)

        @pl.when(g == 0)
        def _():
            ddt_ref[...] = jnp.zeros_like(ddt_ref)

        sps = [sin_ref[0, 0, q] for q in range(PAIRS)]
        xps = [x_ref[:, 128 * q:128 * (q + 1)] for q in range(PAIRS)]
        _, vjp = jax.vjp(functools.partial(_ssd_chunk, g), xps, dt_ref[...], b_ref[...], c_ref[...], sps,
                         dtb_ref[...], al_ref[...], dsk_ref[...])
        dys = [dy_ref[:, 128 * q:128 * (q + 1)] for q in range(PAIRS)]
        dss = [ds_ref[g, q] for q in range(PAIRS)]
        dxps, ddt, db, dc, dsps, ddtb, dal, ddsk = vjp((dys, dss))
        for q in range(PAIRS):
            dx_ref[:, 128 * q:128 * (q + 1)] = dxps[q]
            ds_ref[g, q] = dsps[q]
        db_ref[...] = db
        dc_ref[...] = dc
        ddt_ref[...] += ddt
        ddtb_ref[...] += jnp.broadcast_to(ddtb, ddtb_ref.shape)
        dal_ref[...] += jnp.broadcast_to(dal, dal_ref.shape)
        ddsk_ref[...] += jnp.broadcast_to(ddsk, ddsk_ref.shape)

        @pl.when(g == NG - 1)
        def _():
            ddtp_ref[:, :128] = ddt_ref[...].astype(BF16)
            ddtp_ref[:, 128:] = jnp.zeros((L, DT_PAD - 128), BF16)

    acc = pl.BlockSpec((8, 128), lambda c, g: (0, 0))
    o8 = jax.ShapeDtypeStruct((8, 128), F32)
    return _call(
        body, jobs=jobs, name=name,
        out_shape=(jax.ShapeDtypeStruct((t, DI), F32), jax.ShapeDtypeStruct((t, NG * NS), F32),
                   jax.ShapeDtypeStruct((t, NG * NS), F32), jax.ShapeDtypeStruct(dproj.shape, BF16), o8, o8, o8),
        grid=(nc, NG),
        in_specs=[xs(rev), bspec(rev), cspec(rev), dts(rev), par, par, par, st(rev),
                  pl.BlockSpec((L, 128 * PAIRS), lambda c, g: (rev(c), g)), ANY],
        out_specs=(pl.BlockSpec((L, 128 * PAIRS), lambda c, g: (rev(c), g)),
                   pl.BlockSpec((L, NS), lambda c, g: (rev(c), g)), pl.BlockSpec((L, NS), lambda c, g: (rev(c), g)),
                   pl.BlockSpec((L, DT_PAD), lambda c, g: (rev(c), O_DT // DT_PAD)), acc, acc, acc),
        scratch_shapes=[pltpu.VMEM((NG, PAIRS, 128, NS), F32), pltpu.VMEM((L, 128), F32)],
        compiler_params=_cp(("arbitrary", "arbitrary")), aliases={9: 3},
    )(xbc_act, xbc_act, xbc_act, proj, dtb, alog, dsk, states, dy, dproj)


def _pad_lanes(v, n=128):
    return jnp.pad(v, ((0, 0), (0, n - v.shape[1])))


def _local_step(x, p, positions, target, small, plan):
    t = x.shape[0]
    cos, sin = _rope_tables(positions, t)
    dtb, alog, dsk = _pad_lanes(small["dt_bias"]), _pad_lanes(small["a_log"]), _pad_lanes(small["d_skip"])
    w, jobs = plan.w, plan.jobs

    def mm(a, b, *, name, tm=t, tn=512, **kw):
        return _matmul(a, b, tm=tm, tn=tn, name=name, jobs=jobs(name), **kw)

    tkl = FFN // 4

    def dw(wname, a, dy, *, name, tm):
        plan.g(wname, _matmul(a, dy, ta=True, out_dtype=BF16, tm=tm, tn=512, tk=t, name=name, jobs=jobs(name)))

    u = _rmsnorm_fwd(x, small["g_mix"], name="norm_mix")
    proj = mm(u, w("w_in"), tn=1024, tk=D, name="mm_in")
    attn, lse = _attn_fwd(proj, cos, sin, small["sinks"], name="attn_fwd", jobs=jobs("attn_fwd"))
    out_a = mm(attn, w("w_attn_br"), tk=QD, name="mm_attn_br")
    xbc_act = _conv_fwd(proj, small["conv_w"], small["conv_b"], name="conv_fwd")
    y_pre, states = _ssd_fwd(xbc_act, proj, dtb, alog, dsk, name="ssd_fwd", jobs=jobs("ssd_fwd"))
    yn = _gated_norm_fwd(y_pre, proj, small["g_ssd"], name="gated_norm_fwd")
    out_s = mm(yn, w("w_ssd_br"), tk=DI, name="mm_ssd_br")
    merged = _merge_fwd(proj, out_a, out_s, name="merge_fwd")
    h1 = mm(merged, w("w_o"), add=x, tk=D, name="mm_o")
    f = _rmsnorm_fwd(h1, small["g_ffn"], name="norm_ffn")
    gate, up, act = _swiglu_fwd(f, w("w_gate"), w("w_up"), name="swiglu_fwd", jobs=jobs("swiglu_fwd"))
    h2 = mm(act, w("w_down"), add=h1, tm=t // 2, tk=FFN // 2, name="mm_down")
    e = _rmsnorm_fwd(h2, small["g_ple"], name="norm_ple")
    pgl = mm(e, w("w_ple_gate"), tk=D, name="mm_ple_gate")
    pb = p.astype(BF16)
    pp = mm(pb, w("w_ple_proj"), tk=PLE, name="mm_ple_proj")
    dh3, dpgl, dpp, loss, dg_final = _final(h2, pgl, pp, target, small["g_final"].reshape(1, D), name="final")

    dw("w_ple_proj", pb, dpp, tm=PLE, name="mm_d_ple_proj")
    dw("w_ple_gate", e, dpgl, tm=D, name="mm_d_ple_gate")
    de = mm(dpgl, w("w_ple_gate"), tb=True, tk=D, name="mm_de")
    dh2, dh2b, dg_ple = _rmsnorm_bwd(h2, small["g_ple"], de, dh3, name="norm_ple_bwd", jobs=jobs("norm_ple_bwd"))
    dw("w_down", act, dh2b, tm=FFN // 2, name="mm_d_down")
    dgate, dup = _swiglu_bwd(dh2b, w("w_down"), gate, up, name="swiglu_bwd", jobs=jobs("swiglu_bwd"))
    dw("w_gate", f, dgate, tm=D, name="mm_d_gate")
    dw("w_up", f, dup, tm=D, name="mm_d_up")
    df = mm(dgate, w("w_gate"), tb=True, tn=1024, tk=tkl, name="mm_df_gate")
    df = mm(dup, w("w_up"), tb=True, add=df, tm=t // 2, tk=FFN // 2, name="mm_df_up")
    dh1, dh1b, dg_ffn = _rmsnorm_bwd(h1, small["g_ffn"], df, dh2, name="norm_ffn_bwd", jobs=jobs("norm_ffn_bwd"))
    dw("w_o", merged, dh1b, tm=D, name="mm_d_o")
    dmerged = mm(dh1b, w("w_o"), tb=True, tk=D, name="mm_dmerged")
    dout_a, dout_s, dproj = _merge_bwd(proj, out_a, out_s, dmerged, name="merge_bwd")
    dw("w_attn_br", attn, dout_a, tm=QD, name="mm_d_attn_br")
    dw("w_ssd_br", yn, dout_s, tm=DI, name="mm_d_ssd_br")
    dattn = mm(dout_a, w("w_attn_br"), tb=True, tk=D, name="mm_dattn")
    dyn = mm(dout_s, w("w_ssd_br"), tb=True, tk=D, name="mm_dyn")
    dproj, dsinks = _attn_bwd(proj, cos, sin, small["sinks"], attn, lse, dattn, dproj, name="attn_bwd",
                              jobs=jobs("attn_bwd"))
    dy_pre, dproj, dg_ssd = _gated_norm_bwd(y_pre, proj, small["g_ssd"], dyn, dproj, name="gated_norm_bwd",
                                            jobs=jobs("gated_norm_bwd"))
    dxs, db, dc, dproj, ddtb, dalog, ddsk = _ssd_bwd(xbc_act, proj, dtb, alog, dsk, states, dy_pre, dproj, name="ssd_bwd",
                                                     jobs=jobs("ssd_bwd"))
    dproj, dconv_w, dconv_b = _conv_bwd(proj, small["conv_w"], small["conv_b"], dxs, db, dc, dproj, name="conv_bwd",
                                        jobs=jobs("conv_bwd"))
    for which, h in (("send", 1 - plan.core), ("keep", plan.core)):
        uh = lax.dynamic_slice_in_dim(u, h * (D // 2), D // 2, axis=1)
        name = "mm_d_in_" + which
        plan.g_half("w_in", which, _matmul(uh, dproj, ta=True, out_dtype=BF16, tm=D // 2, tn=1024, tk=t, name=name,
                                           jobs=jobs(name)))
    du = mm(dproj, w("w_in"), tb=True, tn=1024, tk=tkl, name="mm_du")
    grad_x, _, dg_mix = _rmsnorm_bwd(x, small["g_mix"], du, dh1, name="norm_mix_bwd", jobs=jobs("norm_mix_bwd"))

    gs = {
        "g_mix": dg_mix[:1], "conv_w": dconv_w[:CW], "conv_b": dconv_b[:1], "dt_bias": ddtb[:1, :NH],
        "a_log": dalog[:1, :NH], "d_skip": ddsk[:1, :NH], "g_ssd": dg_ssd[:1], "sinks": dsinks[:1, :NQH],
        "g_ffn": dg_ffn[:1], "g_ple": dg_ple[:1], "g_final": dg_final[0],
    }
    return loss, grad_x, gs


def _shard_pieces():
    segs = ((R_Q, QD, O_Q), (R_K, KVD, O_K), (R_V, KVD, O_V), (R_Z, DI, O_Z), (R_XBC, CONV, O_XBC), (R_DT, NH, O_DT),
            (R_GA, D, O_GA), (R_GS, D, O_GS))
    cs = IN_DIM // NCHIP
    out = []
    for j in range(NCHIP):
        for r0, n, k0 in segs:
            lo, hi = max(r0, j * cs), min(r0 + n, (j + 1) * cs)
            if lo < hi:
                out.append((j, lo - j * cs, hi - lo, k0 + lo - r0))
    return out


SLAB = IN_DIM // NCHIP
SLAB_PAD = -(-SLAB // 128) * 128
REMAP_ROWS = 256


def _lane_remap(src, dst_slabs, dst_cols, moves, *, name, add=None, jobs=()):
    s_n, rows, s_cols = src.shape
    assert s_cols % 128 == 0 and dst_cols % 128 == 0 and rows % REMAP_ROWS == 0
    half = REMAP_ROWS // 2

    def body(s_ref, *refs):
        d_ref = refs[-1]
        lane = lax.broadcasted_iota(jnp.int32, (half, 128), 1)
        tiles = {}

        def tile(j, m):
            if (j, m) not in tiles:
                tiles[j, m] = pltpu.bitcast(s_ref[j, :, 128 * m:128 * (m + 1)], jnp.uint32)
            return tiles[j, m]

        def window(j, base):
            m0, s = base // 128, base % 128
            left = tile(j, m0) if 0 <= m0 < s_cols // 128 else None
            if s == 0:
                return left
            right = tile(j, m0 + 1) if 0 <= m0 + 1 < s_cols // 128 else None
            left = None if left is None else pltpu.roll(left, 128 - s, 1)
            right = None if right is None else pltpu.roll(right, 128 - s, 1)
            if left is None or right is None:
                return right if left is None else left
            return jnp.where(lane < 128 - s, left, right)

        for ds in range(dst_slabs):
            for t in range(dst_cols // 128):
                o = 128 * t
                acc = jnp.zeros((half, 128), jnp.uint32)
                for sj, sc, n, dj, dc in moves:
                    lo, hi = max(o, dc) - o, min(o + 128, dc + n) - o
                    if dj != ds or lo >= hi:
                        continue
                    win = window(sj, o - dc + sc)
                    acc = win if (lo, hi) == (0, 128) else jnp.where(jnp.logical_and(lane >= lo, lane < hi), win, acc)
                out = pltpu.bitcast(acc, BF16)
                if add is not None:
                    out = (out.astype(F32) + refs[0][ds, :, o:o + 128].astype(F32)).astype(BF16)
                d_ref[ds, :, o:o + 128] = out

    dst_blk = pl.BlockSpec((dst_slabs, REMAP_ROWS, dst_cols), lambda i: (0, i, 0))
    return _call(
        body, jobs=jobs, name=name, out_shape=jax.ShapeDtypeStruct((dst_slabs, rows, dst_cols), BF16),
        grid=(rows // REMAP_ROWS,),
        in_specs=[pl.BlockSpec((s_n, REMAP_ROWS, s_cols), lambda i: (0, i, 0))] + ([dst_blk] if add is not None else []),
        out_specs=dst_blk, compiler_params=_cp(("parallel",)),
    )(*((src,) if add is None else (src, add)))


def _slabs_to_kernel_cols(slabs, *, name, jobs=()):
    moves = [(j, a, n, 0, k0) for j, a, n, k0 in _shard_pieces()]
    return _lane_remap(slabs, 1, NP, moves, name=name, jobs=jobs)[0]


def _kernel_cols_to_slabs(g, *, name, add=None, jobs=()):
    moves = [(0, k0, n, j, a) for j, a, n, k0 in _shard_pieces()]
    return _lane_remap(g[None], NCHIP, SLAB_PAD, moves, name=name, add=add, jobs=jobs)


MATS = {
    n: (n, kind, 1, r, c, tp, tf) for n, kind, r, c, tp, tf in (
        ("w_in", "stk", 2048, SLAB_PAD, 256, 256),
        ("w_attn_br", "col", 1024, 512, 256, 256),
        ("w_ssd_br", "row", 512, 2048, 512, 256),
        ("w_o", "row", 512, 2048, 512, 256),
        ("w_gate", "col", 2048, 1408, 256, 256),
        ("w_up", "col", 2048, 1408, 256, 256),
        ("w_down", "row", 1408, 2048, 704, 704),
        ("w_ple_gate", "row", 512, 2048, 512, 256),
        ("w_ple_proj", "col", 256, 512, 128, 128),
    )}


def _pos():
    return lax.axis_index("x"), lax.axis_index("y"), lax.axis_index("c")


def _flip(v, a):
    return 1 - v if a else v


def _remote(src, dst, send, recv, dev):
    return pltpu.make_async_remote_copy(src_ref=src, dst_ref=dst, send_sem=send, recv_sem=recv, device_id=dev,
                                        device_id_type=MESH)


def _whole_shape(kind, g, r, c):
    return {"row": (g, NCHIP * r, c), "col": (g, r, NCHIP * c), "stk": (NCHIP, r, c)}[kind]


def _cols(j, c):
    return pl.ds(pl.multiple_of(j * c, 128), c)


def _whole_shard(kind, ref, j, r, c):
    if kind == "row":
        return ref.at[:, pl.ds(j * r, r), :]
    if kind == "col":
        return ref.at[:, :, _cols(j, c)]
    return ref.at[pl.ds(j, 1)]


def _whole_rows(kind, ref, j, row, n, r, c):
    if kind == "row":
        return ref.at[:, pl.ds(j * r + row, n), :]
    if kind == "col":
        return ref.at[:, pl.ds(row, n), _cols(j, c)]
    return ref.at[pl.ds(j, 1), pl.ds(row, n), :]


class _GatherJob(_Job):
    has_mid = True
    NCP = 13

    def __init__(self, names, shards, sink):
        self.mats = [MATS[n] for n in names]
        self.srcs = [shards[n] for n in names]
        self.news = [jax.ShapeDtypeStruct(_whole_shape(kind, g, r, c), BF16) for _, kind, g, r, c, _, _ in self.mats]
        n = len(names)
        self.scratch = [pltpu.SemaphoreType.DMA((self.NCP * n,)), pltpu.SemaphoreType.DMA((self.NCP * n,))]
        self.names, self.sink = names, sink

    def _copies(self, srcs, news, sems):
        send, recv = sems
        x, y, c = _pos()
        me, jx, jy, jd = 2 * x + y, 2 * (1 - x) + y, 2 * x + (1 - y), 2 * (1 - x) + (1 - y)
        nbx, nby, sib = (1 - x, y, c), (x, 1 - y, c), (x, y, 1 - c)
        cps = []
        for w, (_, kind, g, r, cc, _, _) in enumerate(self.mats):
            hr, qr = r // 2, r // 4
            at = lambda j, h, q, n: _whole_rows(kind, news[w], j, h * hr + q * qr, n, r, cc)
            mine = lambda q: srcs[w].at[:, pl.ds(c * hr + q * qr, qr), :]
            cp = lambda k, s, d, dev: _remote(s, d, send.at[self.NCP * w + k], recv.at[self.NCP * w + k], dev)
            cps.append([
                cp(0, mine(0), at(me, c, 0, qr), nbx), cp(1, mine(1), at(me, c, 1, qr), nbx),
                cp(2, mine(1), at(me, c, 1, qr), nby), cp(3, mine(0), at(me, c, 0, qr), nby),
                cp(4, at(jx, c, 0, qr), at(jx, c, 0, qr), nby), cp(5, at(jy, c, 1, qr), at(jy, c, 1, qr), nbx),
                cp(6, at(jx, c, 0, qr), at(jx, c, 0, qr), sib), cp(7, at(jx, c, 1, qr), at(jx, c, 1, qr), sib),
                cp(8, at(jy, c, 1, qr), at(jy, c, 1, qr), sib), cp(9, at(jy, c, 0, qr), at(jy, c, 0, qr), sib),
                cp(10, at(jd, c, 0, qr), at(jd, c, 0, qr), sib), cp(11, at(jd, c, 1, qr), at(jd, c, 1, qr), sib),
                cp(12, srcs[w], _whole_shard(kind, news[w], me, r, cc), sib)])
        return cps

    def _pass_on(self, srcs, news, sems, pairs):
        cps = self._copies(srcs, news, sems)
        for w in range(len(self.mats)):
            for arrived, onward in pairs:
                cps[w][arrived].wait_recv()
                for k in onward:
                    cps[w][k].start()

    def start(self, srcs, dsts, news, sems):
        cps = self._copies(srcs, news, sems)
        for k in (0, 2, 1, 3, 12):
            for w in range(len(self.mats)):
                cps[w][k].start()

    def mid(self, srcs, dsts, news, sems):
        self._pass_on(srcs, news, sems, ((0, (4, 6)), (2, (5, 8))))

    def late(self, srcs, dsts, news, sems):
        self._pass_on(srcs, news, sems, ((1, (7,)), (3, (9,))))

    def finish(self, srcs, dsts, news, sems):
        self._pass_on(srcs, news, sems, ((4, (10,)), (5, (11,))))
        cps = self._copies(srcs, news, sems)
        for w in range(len(self.mats)):
            for k in (6, 7, 8, 9, 10, 11, 12):
                cps[w][k].wait_recv()
            for k in range(self.NCP):
                cps[w][k].wait_send()

    def done(self, dsts, news):
        for n, a in zip(self.names, news):
            self.sink[n] = a


class _SwapJob(_Job):
    def __init__(self, build, ncopies, *, srcs=(), dsts=(), news=(), done=None):
        self.build, self.srcs, self.dsts, self.news, self._done = build, list(srcs), list(dsts), list(news), done
        self.scratch = [pltpu.SemaphoreType.DMA((ncopies,)), pltpu.SemaphoreType.DMA((ncopies,))]

    def start(self, srcs, dsts, news, sems):
        for cp in self.build(srcs, dsts, news, *sems):
            cp.start()

    def finish(self, srcs, dsts, news, sems):
        for cp in self.build(srcs, dsts, news, *sems):
            cp.wait()

    def done(self, dsts, news):
        if self._done is not None:
            self._done(dsts, news)


def _half_of_whole(kind, ref, h, r, c):
    if kind == "row":
        return ref.at[:, :, pl.ds(pl.multiple_of(h * (c // 2), 128), c // 2)]
    return ref.at[:, pl.ds(h * (r // 2), r // 2), :]


def _half_shape(kind, g, r, c):
    return {"row": (g, NCHIP * r, c // 2), "col": (g, r // 2, NCHIP * c), "stk": (NCHIP, r // 2, c)}[kind]


def _sub_shape(kind, r, c):
    return {"row": (1, r // 2, c // 2), "col": (1, r // 4, c), "stk": (1, r // 4, c)}[kind]


def _sub_of_half(kind, ref, j, p, r, c):
    sr = _sub_shape(kind, r, c)[1]
    if kind == "row":
        return ref.at[:, pl.ds(j * r + p * sr, sr), :]
    if kind == "col":
        return ref.at[:, pl.ds(p * sr, sr), _cols(j, c)]
    return ref.at[pl.ds(j, 1), pl.ds(p * sr, sr), :]


def _sub_tile(sr):
    return 256 if sr % 256 == 0 else sr


def _half_of_shard(kind, ref, h, r, c):
    if kind == "row":
        return ref.at[:, :, pl.ds(pl.multiple_of(h * (c // 2), 128), c // 2)]
    return ref.at[:, pl.ds(h * (r // 2), r // 2), :]


def _pair_sum(pack, core, mine, got, whole=True):
    name, kind, g, r, c, tr, _ = pack
    hs = _half_shape(kind, g, r, c)
    nb = hs[1] // tr

    def body(core_ref, a_ref, b_ref, o_ref):
        o_ref[...] = (a_ref[...].astype(F32) + b_ref[...].astype(F32)).astype(BF16)

    blk = (1, tr, hs[2])
    same = lambda gi, i, core_ref: (gi, i, 0)
    if not whole:
        a_map = same
    elif kind == "row":
        a_map = lambda gi, i, core_ref: (gi, i, core_ref[0])
    else:
        a_map = lambda gi, i, core_ref: (gi, core_ref[0] * nb + i, 0)
    return pl.pallas_call(
        body, name="pair_sum_" + name, out_shape=jax.ShapeDtypeStruct(hs, BF16),
        grid_spec=pltpu.PrefetchScalarGridSpec(
            num_scalar_prefetch=1, grid=(hs[0], nb),
            in_specs=[pl.BlockSpec(blk, a_map), pl.BlockSpec(blk, same)], out_specs=pl.BlockSpec(blk, same)),
        compiler_params=_cp(("parallel", "parallel")),
    )(core, mine, got)


def _sub_sums(pack, idx, half, got, *, name):
    _, kind, g, r, c, _, _ = pack
    _, sr, sc = _sub_shape(kind, r, c)
    tr = _sub_tile(sr)
    nb = sr // tr

    def body(idx_ref, a_ref, ga_ref, b_ref, gb_ref, k_ref, p_ref):
        k_ref[0, 0] = a_ref[0].astype(F32) + ga_ref[0, 0].astype(F32)
        p_ref[0, 0] = (b_ref[0].astype(F32) + gb_ref[0, 0].astype(F32)).astype(BF16)

    def sub_map(o):
        if kind == "row":
            return lambda q, i, ix: (0, ix[4 * q + o] * (r // tr) + ix[4 * q + o + 1] * nb + i, 0)
        if kind == "col":
            return lambda q, i, ix: (0, ix[4 * q + o + 1] * nb + i, ix[4 * q + o])
        return lambda q, i, ix: (ix[4 * q + o], ix[4 * q + o + 1] * nb + i, 0)

    sub = lambda o: pl.BlockSpec((1, tr, sc), sub_map(o))
    got_blk = lambda o: pl.BlockSpec((1, 1, tr, sc), lambda q, i, ix: (2 * q + o, 0, i, 0))
    out_blk = pl.BlockSpec((1, 1, tr, sc), lambda q, i, ix: (q, 0, i, 0))
    return pl.pallas_call(
        body, name=name,
        out_shape=(jax.ShapeDtypeStruct((2, 1, sr, sc), F32), jax.ShapeDtypeStruct((2, 1, sr, sc), BF16)),
        grid_spec=pltpu.PrefetchScalarGridSpec(
            num_scalar_prefetch=1, grid=(2, nb), in_specs=[sub(0), got_blk(0), sub(2), got_blk(1)],
            out_specs=(out_blk, out_blk)),
        compiler_params=_cp(("parallel", "parallel")),
    )(idx, half, got, half, got)


def _shard_sum(pack, core, keep, got):
    name, kind, g, r, c, _, _ = pack
    _, sr, sc = _sub_shape(kind, r, c)
    tr = _sub_tile(sr)
    nb = sr // tr

    def body(core_ref, a_ref, b_ref, o_ref):
        o_ref[0] = a_ref[0, 0] + b_ref[0, 0].astype(F32)

    blk = pl.BlockSpec((1, 1, tr, sc), lambda p, i, cr: (p, 0, i, 0))
    if kind == "row":
        o_map = lambda p, i, cr: (0, p * nb + i, cr[0])
    else:
        o_map = lambda p, i, cr: (0, cr[0] * 2 * nb + p * nb + i, 0)
    return pl.pallas_call(
        body, name="shard_sum_" + name, out_shape=jax.ShapeDtypeStruct((g, r, c), F32),
        grid_spec=pltpu.PrefetchScalarGridSpec(
            num_scalar_prefetch=1, grid=(2, nb), in_specs=[blk, blk], out_specs=pl.BlockSpec((1, tr, sc), o_map)),
        compiler_params=_cp(("parallel", "parallel")),
    )(core, keep, got)


class _Plan:
    def __init__(self, shards, table):
        self.shards, self.table = shards, table
        self.whole, self.grad, self.got_a, self.half, self.gshard = {}, {}, {}, {}, {}
        self.got_b1, self.kept, self.pass_on, self.got_b2 = {}, {}, {}, {}
        x, y, c = _pos()
        me, jx, jy = 2 * x + y, 2 * (1 - x) + y, 2 * x + (1 - y)
        self.core = c
        self.core1 = c.reshape(1).astype(jnp.int32)
        zero = 0 * me
        self.idx_sums = jnp.stack([me, zero, jy, zero, me, zero + 1, jx, zero + 1]).astype(jnp.int32)
        self._w_in = None
        self.send, self.keep = {}, {}

    def w(self, n):
        if n != "w_in":
            return self.whole[n][0]
        if self._w_in is None:
            self._w_in = _slabs_to_kernel_cols(self.whole[n], name="relayout_w_in", jobs=self.jobs("relayout_w_in"))
        return self._w_in

    def g(self, n, a):
        self.grad[n] = a[None]

    def g_half(self, n, which, a):
        if which == "keep" and n in self.got_a:
            self.half[n] = _kernel_cols_to_slabs(a, name="relayout_d_in_keep", add=self.got_a[n])
        else:
            (self.send if which == "send" else self.keep)[n] = _kernel_cols_to_slabs(a, name="relayout_d_in_" + which)

    def jobs(self, tag):
        out = []
        for spec in self.table.get(tag, ()):
            out += getattr(self, "_" + spec[0])(*spec[1:])
        return out

    def run(self, name, jobs):
        if jobs:
            _call(lambda: None, jobs=jobs, name=name, out_shape=[], in_specs=[], out_specs=[])()

    def _gather(self, names):
        return [_GatherJob(names, self.shards, self.whole)]

    def _rs_a(self, names):
        mats = [MATS[n] for n in names]

        def build(srcs, dsts, news, send, recv):
            x, y, c = _pos()
            return [_remote(srcs[i] if names[i] in self.send else _half_of_whole(kind, srcs[i], 1 - c, r, cc), news[i],
                            send.at[i], recv.at[i], (x, y, 1 - c))
                    for i, (_, kind, g, r, cc, _, _) in enumerate(mats)]

        def done(dsts, news):
            self.got_a.update(zip(names, news))

        return [_SwapJob(build, len(names), srcs=[self.send.get(n, self.grad.get(n)) for n in names], done=done,
                         news=[jax.ShapeDtypeStruct(_half_shape(kind, g, r, c), BF16) for _, kind, g, r, c, _, _ in mats])]

    def _rs_b1(self, names):
        mats = [MATS[n] for n in names]
        for n in names:
            if n in self.half:
                continue
            if n in self.keep:
                self.half[n] = _pair_sum(MATS[n], self.core1, self.keep[n], self.got_a[n], whole=False)
            else:
                self.half[n] = _pair_sum(MATS[n], self.core1, self.grad[n], self.got_a[n])

        def build(srcs, dsts, news, send, recv):
            x, y, c = _pos()
            jx, jy, jd = 2 * (1 - x) + y, 2 * x + (1 - y), 2 * (1 - x) + (1 - y)
            nbx, nby = (1 - x, y, c), (x, 1 - y, c)
            cps = []
            for i, (_, kind, g, r, cc, _, _) in enumerate(mats):
                sub = lambda j, p: _sub_of_half(kind, srcs[i], j, p, r, cc)
                for k, (j, p, dev) in enumerate(((jx, 0, nbx), (jd, 0, nbx), (jy, 1, nby), (jd, 1, nby))):
                    cps.append(_remote(sub(j, p), news[i].at[k], send.at[4 * i + k], recv.at[4 * i + k], dev))
            return cps

        def done(dsts, news):
            self.got_b1.update(zip(names, news))

        return [_SwapJob(build, 4 * len(names), srcs=[self.half[n] for n in names], done=done,
                         news=[jax.ShapeDtypeStruct((4,) + _sub_shape(kind, r, c), BF16) for _, kind, g, r, c, _, _ in mats])]

    def _rs_b2(self, names):
        mats = [MATS[n] for n in names]
        for n in names:
            self.kept[n], self.pass_on[n] = _sub_sums(MATS[n], self.idx_sums, self.half[n], self.got_b1[n], name="sums_" + n)

        def build(srcs, dsts, news, send, recv):
            x, y, c = _pos()
            cps = []
            for i in range(len(mats)):
                cps.append(_remote(srcs[i].at[0], news[i].at[0], send.at[2 * i], recv.at[2 * i], (x, 1 - y, c)))
                cps.append(_remote(srcs[i].at[1], news[i].at[1], send.at[2 * i + 1], recv.at[2 * i + 1], (1 - x, y, c)))
            return cps

        def done(dsts, news):
            self.got_b2.update(zip(names, news))

        return [_SwapJob(build, 2 * len(names), srcs=[self.pass_on[n] for n in names], done=done,
                         news=[jax.ShapeDtypeStruct((2,) + _sub_shape(kind, r, c), BF16) for _, kind, g, r, c, _, _ in mats])]

    def _rs_c(self, names):
        mats = [MATS[n] for n in names]
        parts = [_shard_sum(MATS[n], self.core1, self.kept[n], self.got_b2[n]) for n in names]

        def build(srcs, dsts, news, send, recv):
            x, y, c = _pos()
            cps = []
            for i, (_, kind, g, r, cc, _, _) in enumerate(mats):
                mine = _half_of_shard(kind, dsts[i], c, r, cc)
                cps.append(_remote(mine, mine, send.at[i], recv.at[i], (x, y, 1 - c)))
            return cps

        def done(dsts, news):
            self.gshard.update(zip(names, dsts))

        return [_SwapJob(build, len(names), dsts=parts, done=done)]

    def finish(self, n):
        if n not in self.got_a:
            self.run("rs_a_" + n, self._rs_a((n,)))
        if n not in self.got_b1:
            self.run("rs_b1_" + n, self._rs_b1((n,)))
        if n not in self.got_b2:
            self.run("rs_b2_" + n, self._rs_b2((n,)))
        if n not in self.gshard:
            self.run("rs_c_" + n, self._rs_c((n,)))
        return self.gshard[n]


TABLE = {
    "gather_w_in": (("gather", ("w_in",)),),
    "relayout_w_in": (("gather", ("w_gate",)),),
    "mm_in": (("gather", ("w_up",)),),
    "attn_fwd": (("gather", ("w_attn_br", "w_ssd_br")),),
    "ssd_fwd": (("gather", ("w_o",)),),
    "swiglu_fwd": (("gather", ("w_down",)),),
    "mm_down": (("gather", ("w_ple_gate", "w_ple_proj")),),
    "mm_d_down": (("rs_a", ("w_ple_proj", "w_ple_gate")),),
    "swiglu_bwd": (("rs_a", ("w_down",)), ("rs_b1", ("w_ple_proj", "w_ple_gate"))),
    "mm_d_gate": (("rs_b1", ("w_down",)), ("rs_b2", ("w_ple_proj", "w_ple_gate"))),
    "mm_d_up": (("rs_b2", ("w_down",)), ("rs_a", ("w_gate",))),
    "mm_df_gate": (("rs_b1", ("w_gate",)), ("rs_a", ("w_up",)), ("rs_c", ("w_down", "w_ple_proj", "w_ple_gate"))),
    "mm_df_up": (("rs_b2", ("w_gate",)),),
    "mm_dmerged": (("rs_a", ("w_o",)), ("rs_c", ("w_gate",))),
    "mm_dyn": (("rs_a", ("w_attn_br", "w_ssd_br")),),
    "attn_bwd": (("rs_b1", ("w_up",)),),
    "ssd_bwd": (("rs_b1", ("w_o", "w_attn_br", "w_ssd_br")), ("rs_b2", ("w_up",))),
    "conv_bwd": (("rs_b2", ("w_o", "w_attn_br", "w_ssd_br")), ("rs_c", ("w_up",))),
    "mm_d_in_keep": (("rs_a", ("w_in",)), ("rs_c", ("w_o", "w_attn_br", "w_ssd_br"))),
    "mm_du": (("rs_b1", ("w_in",)),),
    "norm_mix_bwd": (("rs_b2", ("w_in",)),),
}


NDEV = 8


def _allreduce_small(v, *, name):
    rows = v.shape[0]

    def body(v_ref, o_ref, slots, send, recv):
        x, y, c = _pos()
        me = 4 * x + 2 * y + c
        slots[me] = v_ref[...]
        cps = []
        for k in range(1, NDEV):
            peer = (_flip(x, k & 4), _flip(y, k & 2), _flip(c, k & 1))
            cp = _remote(v_ref, slots.at[me], send.at[k - 1], recv.at[k - 1], peer)
            cp.start()
            cps.append(cp)
        for cp in cps:
            cp.wait()
        acc = slots[0]
        for s in range(1, NDEV):
            acc = acc + slots[s]
        o_ref[...] = acc

    return pl.pallas_call(
        body, name=name, out_shape=jax.ShapeDtypeStruct((rows, 128), F32),
        in_specs=[pl.BlockSpec(memory_space=pltpu.VMEM)], out_specs=pl.BlockSpec(memory_space=pltpu.VMEM),
        scratch_shapes=[pltpu.VMEM((NDEV, rows, 128), F32), pltpu.SemaphoreType.DMA((NDEV - 1,)),
                        pltpu.SemaphoreType.DMA((NDEV - 1,))],
    )(v)


def _adamw(w, g, m, v, *, name, tr=None, tc=None, jobs=()):
    r, c = w.shape
    tr = r if tr is None else tr
    c1 = 1.0 / (1.0 - B1 ** STEP)
    c2 = 1.0 / (1.0 - B2 ** STEP)

    def body(w_ref, g_ref, m_ref, v_ref, d_ref, mo_ref, vo_ref):
        gv = g_ref[...]
        mn = B1 * m_ref[...] + (1.0 - B1) * gv
        vn = B2 * v_ref[...] + (1.0 - B2) * (gv * gv)
        mo_ref[...] = mn
        vo_ref[...] = vn
        d_ref[...] = -LR * ((mn * c1) / (jnp.sqrt(vn * c2) + AEPS) + WD * w_ref[...])

    if tc is None:
        blk, grid = pl.BlockSpec((tr, c), lambda i: (i, 0)), (r // tr,)
    else:
        blk, grid = pl.BlockSpec((r, tc), lambda i: (0, i)), (c // tc,)
    o = jax.ShapeDtypeStruct((r, c), F32)
    return _call(
        body, jobs=jobs, name=name, out_shape=(o, o, o), grid=grid, in_specs=[blk] * 4, out_specs=(blk, blk, blk),
        compiler_params=_cp(("parallel",)),
    )(w, g, m, v)


WEIGHTS = ("g_mix", "w_in", "conv_w", "conv_b", "dt_bias", "a_log", "d_skip", "g_ssd", "sinks", "w_attn_br", "w_ssd_br",
           "w_o", "g_ffn", "w_gate", "w_up", "w_down", "g_ple", "w_ple_gate", "w_ple_proj", "g_final")
BIG = {
    "w_gate": 256, "w_up": 256, "w_down": 128, "w_ssd_br": 128, "w_o": 128, "w_ple_gate": 128, "w_attn_br": 256,
    "w_ple_proj": 256, "w_in": None,
}
SMALL = tuple(n for n in WEIGHTS if n not in BIG)


def _pack_small(parts):
    rows = []
    for a in parts:
        a = a.reshape(-1)
        rows.append(jnp.pad(a, (0, -a.shape[0] % 128)).reshape(-1, 128))
    out = jnp.concatenate(rows, axis=0)
    return jnp.pad(out, ((0, -out.shape[0] % 8), (0, 0)))


def _unpack_small(packed, shapes):
    out, r = [], 0
    for s in shapes:
        n = int(np.prod(s))
        nr = -(-n // 128)
        out.append(packed[r:r + nr].reshape(-1)[:n].reshape(s))
        r += nr
    return out


def kernel(x, p, positions, g_mix, w_in, conv_w, conv_b, dt_bias, a_log, d_skip, g_ssd, sinks, w_attn_br, w_ssd_br, w_o, g_ffn, w_gate, w_up, w_down, g_ple, w_ple_gate, w_ple_proj, g_final, loss_target, m_g_mix, m_w_in, m_conv_w, m_conv_b, m_dt_bias, m_a_log, m_d_skip, m_g_ssd, m_sinks, m_w_attn_br, m_w_ssd_br, m_w_o, m_g_ffn, m_w_gate, m_w_up, m_w_down, m_g_ple, m_w_ple_gate, m_w_ple_proj, m_g_final, v_g_mix, v_w_in, v_conv_w, v_conv_b, v_dt_bias, v_a_log, v_d_skip, v_g_ssd, v_sinks, v_w_attn_br, v_w_ssd_br, v_w_o, v_g_ffn, v_w_gate, v_w_up, v_w_down, v_g_ple, v_w_ple_gate, v_w_ple_proj, v_g_final):
    w = dict(zip(WEIGHTS, (g_mix, w_in, conv_w, conv_b, dt_bias, a_log, d_skip, g_ssd, sinks, w_attn_br, w_ssd_br, w_o,
                           g_ffn, w_gate, w_up, w_down, g_ple, w_ple_gate, w_ple_proj, g_final)))
    m = dict(zip(WEIGHTS, (m_g_mix, m_w_in, m_conv_w, m_conv_b, m_dt_bias, m_a_log, m_d_skip, m_g_ssd, m_sinks, m_w_attn_br,
                           m_w_ssd_br, m_w_o, m_g_ffn, m_w_gate, m_w_up, m_w_down, m_g_ple, m_w_ple_gate, m_w_ple_proj,
                           m_g_final)))
    v = dict(zip(WEIGHTS, (v_g_mix, v_w_in, v_conv_w, v_conv_b, v_dt_bias, v_a_log, v_d_skip, v_g_ssd, v_sinks, v_w_attn_br,
                           v_w_ssd_br, v_w_o, v_g_ffn, v_w_gate, v_w_up, v_w_down, v_g_ple, v_w_ple_gate, v_w_ple_proj,
                           v_g_final)))
    xi, yi, ci = _pos()
    chip = 2 * xi + yi
    t = x.shape[1]
    cshard = CONV // NCHIP

    shards = {n: w[n].astype(BF16) for n in MATS}
    shards["w_in"] = jnp.pad(shards["w_in"], ((0, 0), (0, 0), (0, SLAB_PAD - SLAB)))
    plan = _Plan(shards, TABLE)
    plan.run("gather_w_in", plan.jobs("gather_w_in"))
    placed = lax.dynamic_update_slice(jnp.zeros((CW, CONV), F32), w["conv_w"][0], (0, chip * cshard))
    conv_whole = _allreduce_small(jnp.where(ci == 0, placed, 0.0).reshape(-1, 128), name="gather_conv_w").reshape(CW, CONV)

    small = {n: w[n] for n in ("g_mix", "conv_b", "dt_bias", "a_log", "d_skip", "g_ssd", "sinks", "g_ffn", "g_ple", "g_final")}
    small["conv_w"] = conv_whole
    loss8, grad_x, gs = _local_step(x[0], p[0, 0], positions, loss_target[0], small, plan)

    order = ("g_mix", "conv_b", "dt_bias", "a_log", "d_skip", "g_ssd", "sinks", "g_ffn", "g_ple", "g_final", "conv_w")
    summed = _allreduce_small(_pack_small([loss8[0, :1]] + [gs[n] for n in order]), name="sum_small")
    parts = _unpack_small(summed, [(1,)] + [w[n].shape for n in order[:-1]] + [(CW, CONV)])
    loss = parts[0][0]
    grad = dict(zip(order, parts[1:]))
    grad["conv_w"] = lax.dynamic_slice(grad["conv_w"], (0, chip * cshard), (CW, cshard))[None]

    delta, new_m, new_v = {}, {}, {}
    for n, tr in BIG.items():
        grad[n] = plan.finish(n)[:, :, :w[n].shape[2]]
        if n == "w_in":
            d_, m_, v_ = _adamw(w[n][0].T, grad[n][0].T, m[n][0].T, v[n][0].T, tc=128, name="adamw_" + n)
            d_, m_, v_ = d_.T, m_.T, v_.T
        else:
            d_, m_, v_ = _adamw(w[n][0], grad[n][0], m[n][0], v[n][0], tr=tr, name="adamw_" + n)
        delta[n], new_m[n], new_v[n] = d_[None], m_[None], v_[None]
    shapes = [w[n].shape for n in SMALL]
    d_, m_, v_ = _adamw(_pack_small([w[n] for n in SMALL]), _pack_small([grad[n] for n in SMALL]),
                        _pack_small([m[n] for n in SMALL]), _pack_small([v[n] for n in SMALL]), tr=None, name="adamw_small")
    for n, a, b, c_ in zip(SMALL, _unpack_small(d_, shapes), _unpack_small(m_, shapes), _unpack_small(v_, shapes)):
        delta[n], new_m[n], new_v[n] = a, b, c_

    return (loss, grad_x[None], *[grad[n] for n in WEIGHTS], *[delta[n] for n in WEIGHTS],
            *[new_m[n] for n in WEIGHTS], *[new_v[n] for n in WEIGHTS])
```

```python
import functools

import jax
import jax.numpy as jnp
import numpy as np
from jax import lax
from jax.experimental import pallas as pl
from jax.experimental.pallas import tpu as pltpu

F32 = jnp.float32
BF16 = jnp.bfloat16
MESH = pl.DeviceIdType.MESH

D = 2048
HD = 64
NQH = 16
NKV = 4
QD = NQH * HD
KVD = NKV * HD
DI = 2048
NH = 32
NG = 4
NS = 128
CW = 4
L = 128
CONV = DI + 2 * NG * NS
FFN = 5632
PLE = 256
IN_DIM = QD + 2 * KVD + DI + CONV + NH + 2 * D
EPS = 1e-6
SSM_EPS = 1e-5
ROPE_THETA = 10000.0
LR, B1, B2, AEPS, WD, STEP = 0.001, 0.9, 0.999, 1e-08, 0.01, 10

O_GA, O_GS, O_Z, O_XBC, O_Q, O_K, O_V, O_DT = 0, 2048, 4096, 6144, 9216, 10240, 10496, 10752
DT_PAD = 512
NP = O_DT + DT_PAD
R_Q, R_K, R_V, R_Z, R_XBC, R_DT, R_GA, R_GS = 0, 1024, 1280, 1536, 3584, 6656, 6688, 8736

NCHIP = 4
VMEM_LIMIT = 52 * 1024 * 1024
NEG = -1e30


def _cp(sem=None):
    return pltpu.CompilerParams(dimension_semantics=sem, vmem_limit_bytes=VMEM_LIMIT)


def _dot(a, b):
    return lax.dot_general(a, b, (((1,), (0,)), ((), ())), preferred_element_type=F32)


def _dot_nt(a, b):
    return lax.dot_general(a, b, (((1,), (1,)), ((), ())), preferred_element_type=F32)


def _dot_tn(a, b):
    return lax.dot_general(a, b, (((0,), (0,)), ((), ())), preferred_element_type=F32)


def _sigmoid(x):
    return 1.0 / (1.0 + jnp.exp(-x))


def _bf16_dot(dot, da, db):
    @jax.custom_vjp
    def f(a, b):
        return dot(a.astype(BF16), b.astype(BF16))

    def fwd(a, b):
        return f(a, b), (a.astype(BF16), b.astype(BF16))

    def bwd(res, g):
        a, b = res
        g = g.astype(BF16)
        return da(g, a, b), db(g, a, b)

    f.defvjp(fwd, bwd)
    return f


_bdot = _bf16_dot(_dot, lambda g, a, b: _dot_nt(g, b), lambda g, a, b: _dot_tn(a, g))
_bdot_nt = _bf16_dot(_dot_nt, lambda g, a, b: _dot(g, b), lambda g, a, b: _dot_tn(g, a))
_bdot_tn = _bf16_dot(_dot_tn, lambda g, a, b: _dot_nt(b, g), lambda g, a, b: _dot(a, g))


ANY = pl.BlockSpec(memory_space=pl.ANY)


class _Job:
    srcs, dsts, news, scratch = (), (), (), ()
    has_mid = False

    def start(self, srcs, dsts, news, sems):
        raise NotImplementedError

    def mid(self, srcs, dsts, news, sems):
        pass

    def late(self, srcs, dsts, news, sems):
        pass

    def finish(self, srcs, dsts, news, sems):
        raise NotImplementedError

    def done(self, dsts, news):
        pass


def _call(body, *, jobs=(), name, out_shape, in_specs, out_specs, grid=(), scratch_shapes=(), compiler_params=None,
          aliases=None):
    jobs = [j for j in jobs if j is not None]
    aliases = dict(aliases or {})
    if not jobs:
        return pl.pallas_call(body, name=name, out_shape=out_shape, in_specs=in_specs, out_specs=out_specs, grid=grid,
                              scratch_shapes=scratch_shapes, compiler_params=compiler_params,
                              input_output_aliases=aliases)
    single = not isinstance(out_shape, (tuple, list))
    outs = [out_shape] if single else list(out_shape)
    ospecs = [out_specs] if single else list(out_specs)
    n_in, n_out, n_scr = len(in_specs), len(outs), len(scratch_shapes)
    srcs = [a for j in jobs for a in j.srcs]
    dsts = [a for j in jobs for a in j.dsts]
    news = [a for j in jobs for a in j.news]
    sems = [a for j in jobs for a in j.scratch]

    def wrapped(*refs):
        pos = n_in + len(srcs) + len(dsts)
        ins, jsrc = refs[:n_in], refs[n_in:n_in + len(srcs)]
        o_refs = refs[pos:pos + n_out]
        pos += n_out
        jdst, jnew = refs[pos:pos + len(dsts)], refs[pos + len(dsts):pos + len(dsts) + len(news)]
        pos += len(dsts) + len(news)
        scr, jsem = refs[pos:pos + n_scr], refs[pos + n_scr:]

        def run(which):
            a = b = c = d = 0
            for j in jobs:
                getattr(j, which)(jsrc[a:a + len(j.srcs)], jdst[b:b + len(j.dsts)], jnew[c:c + len(j.news)],
                                  jsem[d:d + len(j.scratch)])
                a, b, c, d = a + len(j.srcs), b + len(j.dsts), c + len(j.news), d + len(j.scratch)

        if not grid:
            run("start")
            run("mid")
            run("late")
            body(*ins, *o_refs, *scr)
            run("finish")
            return
        step = functools.reduce(lambda acc, a: acc * grid[a] + pl.program_id(a), range(len(grid)), 0)
        steps = int(np.prod(grid))
        pl.when(step == 0)(lambda: run("start"))
        if any(j.has_mid for j in jobs):
            pl.when(step == steps // 3)(lambda: run("mid"))
            pl.when(step == (2 * steps) // 3)(lambda: run("late"))
        body(*ins, *o_refs, *scr)
        pl.when(step == steps - 1)(lambda: run("finish"))

    call = pl.pallas_call(
        wrapped, name=name,
        out_shape=outs + [jax.ShapeDtypeStruct(a.shape, a.dtype) for a in dsts] + news,
        in_specs=list(in_specs) + [ANY] * (len(srcs) + len(dsts)),
        out_specs=ospecs + [ANY] * (len(dsts) + len(news)),
        grid=grid, scratch_shapes=list(scratch_shapes) + sems,
        input_output_aliases={**aliases, **{n_in + len(srcs) + i: n_out + i for i in range(len(dsts))}},
        compiler_params=_cp(("arbitrary",) * len(grid) if grid else None))

    def run_call(*args):
        res = call(*args, *srcs, *dsts)
        b, c = n_out, n_out + len(dsts)
        for j in jobs:
            j.done(res[b:b + len(j.dsts)], res[c:c + len(j.news)])
            b, c = b + len(j.dsts), c + len(j.news)
        return res[0] if single else tuple(res[:n_out])

    return run_call


def _matmul(a, b, *, ta=False, tb=False, out_dtype=F32, add=None, tm, tn, tk, name, jobs=()):
    k, m = a.shape if ta else a.shape[::-1]
    n = b.shape[0] if tb else b.shape[1]
    assert (b.shape[1] if tb else b.shape[0]) == k and not (ta and tb)
    assert m % tm == 0 and n % tn == 0 and k % tk == 0, (name, a.shape, b.shape)
    nk = k // tk
    has_add = add is not None

    def body(*refs):
        a_ref, b_ref = refs[0], refs[1]
        add_ref = refs[2] if has_add else None
        o_ref = refs[3] if has_add else refs[2]
        av = a_ref[...].astype(BF16)
        bv = b_ref[...].astype(BF16)
        part = _dot_tn(av, bv) if ta else _dot_nt(av, bv) if tb else _dot(av, bv)

        def finish(r):
            if has_add:
                r = r + add_ref[...]
            o_ref[...] = r.astype(out_dtype)

        if nk == 1:
            finish(part)
        elif out_dtype == F32:
            kk = pl.program_id(2)
            pl.when(kk == 0)(lambda: finish(part))

            @pl.when(kk > 0)
            def _():
                o_ref[...] += part
        else:
            acc_ref = refs[-1]
            kk = pl.program_id(2)

            @pl.when(kk == 0)
            def _():
                acc_ref[...] = part

            @pl.when(kk > 0)
            def _():
                acc_ref[...] += part

            @pl.when(kk == nk - 1)
            def _():
                finish(acc_ref[...])

    in_specs = [pl.BlockSpec((tk, tm), lambda i, j, kk: (kk, i)) if ta else pl.BlockSpec((tm, tk), lambda i, j, kk: (i, kk)),
                pl.BlockSpec((tn, tk), lambda i, j, kk: (j, kk)) if tb
                else pl.BlockSpec((tk, tn), lambda i, j, kk: (kk, j))]
    args = [a, b]
    if has_add:
        in_specs.append(pl.BlockSpec((tm, tn), lambda i, j, kk: (i, j)))
        args.append(add)
    return _call(
        body, jobs=jobs, name=name,
        out_shape=jax.ShapeDtypeStruct((m, n), out_dtype),
        grid=(m // tm, n // tn, nk),
        in_specs=in_specs,
        out_specs=pl.BlockSpec((tm, tn), lambda i, j, kk: (i, j)),
        scratch_shapes=[pltpu.VMEM((tm, tn), F32)] if nk > 1 and out_dtype != F32 else [],
        compiler_params=_cp(("parallel", "parallel", "arbitrary")),
    )(*args)


ROWS = 256


def _rmsnorm_fwd(x, g, *, name):
    t, d = x.shape

    def body(x_ref, g_ref, o_ref):
        xv = x_ref[...]
        r = lax.rsqrt(jnp.mean(xv * xv, axis=-1, keepdims=True) + EPS)
        o_ref[...] = (xv * r * g_ref[...]).astype(BF16)

    return pl.pallas_call(
        body, name=name, out_shape=jax.ShapeDtypeStruct((t, d), BF16), grid=(t // ROWS,),
        in_specs=[pl.BlockSpec((ROWS, d), lambda i: (i, 0)), pl.BlockSpec((1, d), lambda i: (0, 0))],
        out_specs=pl.BlockSpec((ROWS, d), lambda i: (i, 0)), compiler_params=_cp(("parallel",)),
    )(x, g)


def _rmsnorm_bwd(x, g, dy, dres, *, name, jobs=()):
    t, d = x.shape

    def body(x_ref, g_ref, dy_ref, dres_ref, dx_ref, dxb_ref, dg_ref):
        xv = x_ref[...]
        r = lax.rsqrt(jnp.mean(xv * xv, axis=-1, keepdims=True) + EPS)
        xh = xv * r
        dyv = dy_ref[...]
        dxh = dyv * g_ref[...]
        dx = r * (dxh - xh * jnp.mean(dxh * xh, axis=-1, keepdims=True))
        tot = dres_ref[...] + dx
        dx_ref[...] = tot
        dxb_ref[...] = tot.astype(BF16)

        @pl.when(pl.program_id(0) == 0)
        def _():
            dg_ref[...] = jnp.zeros_like(dg_ref)

        dg_ref[...] += jnp.broadcast_to(jnp.sum(dyv * xh, axis=0, keepdims=True), dg_ref.shape)

    row = pl.BlockSpec((ROWS, d), lambda i: (i, 0))
    return _call(
        body, jobs=jobs, name=name,
        out_shape=(jax.ShapeDtypeStruct((t, d), F32), jax.ShapeDtypeStruct((t, d), BF16),
                   jax.ShapeDtypeStruct((8, d), F32)),
        grid=(t // ROWS,),
        in_specs=[row, pl.BlockSpec((1, d), lambda i: (0, 0)), row, row],
        out_specs=(row, row, pl.BlockSpec((8, d), lambda i: (0, 0))),
        compiler_params=_cp(("arbitrary",)),
    )(x, g, dy, dres)


def _final(h2, pgl, pp, target, g_final, *, name):
    t, d = h2.shape

    def body(h2_ref, pgl_ref, pp_ref, tg_ref, g_ref, dh3_ref, dpgl_ref, dpp_ref, loss_ref, dg_ref):
        s = _sigmoid(pgl_ref[...])
        ppv = pp_ref[...]
        h3 = h2_ref[...] + s * ppv
        r = lax.rsqrt(jnp.mean(h3 * h3, axis=-1, keepdims=True) + EPS)
        xh = h3 * r
        gv = g_ref[...]
        err = xh * gv - tg_ref[...]
        dyv = err * (1.0 / d)
        dxh = dyv * gv
        dh3 = r * (dxh - xh * jnp.mean(dxh * xh, axis=-1, keepdims=True))
        dh3_ref[...] = dh3
        dpp_ref[...] = (dh3 * s).astype(BF16)
        dpgl_ref[...] = (dh3 * ppv * s * (1.0 - s)).astype(BF16)

        @pl.when(pl.program_id(0) == 0)
        def _():
            loss_ref[...] = jnp.zeros_like(loss_ref)
            dg_ref[...] = jnp.zeros_like(dg_ref)

        part = 0.5 * jnp.sum(jnp.mean(err * err, axis=-1, keepdims=True), axis=0, keepdims=True)
        loss_ref[...] += jnp.broadcast_to(part, loss_ref.shape)
        dg_ref[...] += jnp.broadcast_to(jnp.sum(dyv * xh, axis=0, keepdims=True), dg_ref.shape)

    row = pl.BlockSpec((ROWS, d), lambda i: (i, 0))
    return pl.pallas_call(
        body, name=name,
        out_shape=(jax.ShapeDtypeStruct((t, d), F32), jax.ShapeDtypeStruct((t, d), BF16),
                   jax.ShapeDtypeStruct((t, d), BF16), jax.ShapeDtypeStruct((8, 128), F32),
                   jax.ShapeDtypeStruct((8, d), F32)),
        grid=(t // ROWS,),
        in_specs=[row, row, row, row, pl.BlockSpec((1, d), lambda i: (0, 0))],
        out_specs=(row, row, row, pl.BlockSpec((8, 128), lambda i: (0, 0)), pl.BlockSpec((8, d), lambda i: (0, 0))),
        compiler_params=_cp(("arbitrary",)),
    )(h2, pgl, pp, target, g_final)


def _merge_fwd(proj, out_a, out_s, *, name):
    t = proj.shape[0]

    def body(ga_ref, gs_ref, a_ref, s_ref, o_ref):
        o_ref[...] = (_sigmoid(ga_ref[...]) * a_ref[...] + _sigmoid(gs_ref[...]) * s_ref[...]).astype(BF16)

    row = pl.BlockSpec((ROWS, D), lambda i: (i, 0))
    return pl.pallas_call(
        body, name=name, out_shape=jax.ShapeDtypeStruct((t, D), BF16), grid=(t // ROWS,),
        in_specs=[pl.BlockSpec((ROWS, D), lambda i: (i, O_GA // D)), pl.BlockSpec((ROWS, D), lambda i: (i, O_GS // D)),
                  row, row],
        out_specs=row, compiler_params=_cp(("parallel",)),
    )(proj, proj, out_a, out_s)


def _merge_bwd(proj, out_a, out_s, dmerged, *, name):
    t = proj.shape[0]
    assert O_GA == 0 and O_GS == D

    def body(ga_ref, gs_ref, a_ref, s_ref, dm_ref, da_ref, ds_ref, dp_ref):
        sa = _sigmoid(ga_ref[...])
        ss = _sigmoid(gs_ref[...])
        dm = dm_ref[...]
        da_ref[...] = (dm * sa).astype(BF16)
        ds_ref[...] = (dm * ss).astype(BF16)
        dp_ref[:, :D] = (dm * a_ref[...] * sa * (1.0 - sa)).astype(BF16)
        dp_ref[:, D:] = (dm * s_ref[...] * ss * (1.0 - ss)).astype(BF16)

    row = pl.BlockSpec((ROWS, D), lambda i: (i, 0))
    o = jax.ShapeDtypeStruct((t, D), BF16)
    return pl.pallas_call(
        body, name=name, out_shape=(o, o, jax.ShapeDtypeStruct((t, NP), BF16)), grid=(t // ROWS,),
        in_specs=[pl.BlockSpec((ROWS, D), lambda i: (i, O_GA // D)), pl.BlockSpec((ROWS, D), lambda i: (i, O_GS // D)),
                  row, row, row],
        out_specs=(row, row, pl.BlockSpec((ROWS, 2 * D), lambda i: (i, 0))), compiler_params=_cp(("parallel",)),
    )(proj, proj, out_a, out_s, dmerged)


def _swiglu_fwd(f, w_gate, w_up, *, name, tn=512, jobs=()):
    t, d = f.shape
    n = w_gate.shape[1]

    def body(f_ref, wg_ref, wu_ref, g_ref, u_ref, a_ref):
        fv = f_ref[...]
        g = _dot(fv, wg_ref[...])
        u = _dot(fv, wu_ref[...])
        g_ref[...] = g.astype(BF16)
        u_ref[...] = u.astype(BF16)
        a_ref[...] = (g * _sigmoid(g) * u).astype(BF16)

    col = pl.BlockSpec((t, tn), lambda j: (0, j))
    wcol = pl.BlockSpec((d, tn), lambda j: (0, j))
    return _call(
        body, jobs=jobs, name=name,
        out_shape=(jax.ShapeDtypeStruct((t, n), BF16), jax.ShapeDtypeStruct((t, n), BF16),
                   jax.ShapeDtypeStruct((t, n), BF16)),
        grid=(n // tn,),
        in_specs=[pl.BlockSpec((t, d), lambda j: (0, 0)), wcol, wcol],
        out_specs=(col, col, col), compiler_params=_cp(("parallel",)),
    )(f, w_gate, w_up)


def _swiglu_bwd(dh, w_down, gate, up, *, name, tn=512, jobs=()):
    t, d = dh.shape
    n = w_down.shape[0]

    def body(dh_ref, w_ref, g_ref, u_ref, dg_ref, du_ref):
        da = _dot_nt(dh_ref[...], w_ref[...])
        g = g_ref[...].astype(F32)
        s = _sigmoid(g)
        du_ref[...] = (da * g * s).astype(BF16)
        dg_ref[...] = (da * u_ref[...].astype(F32) * s * (1.0 + g * (1.0 - s))).astype(BF16)

    col = pl.BlockSpec((t, tn), lambda j: (0, j))
    o = jax.ShapeDtypeStruct((t, n), BF16)
    return _call(
        body, jobs=jobs, name=name, out_shape=(o, o), grid=(n // tn,),
        in_specs=[pl.BlockSpec((t, d), lambda j: (0, 0)), pl.BlockSpec((tn, d), lambda j: (j, 0)), col, col],
        out_specs=(col, col), compiler_params=_cp(("parallel",)),
    )(dh, w_down, gate, up)


def _gated_norm_fwd(y_pre, proj, g_ssd, *, name):
    t = y_pre.shape[0]

    def body(y_ref, z_ref, g_ref, o_ref):
        z = z_ref[...]
        v = y_ref[...] * z * _sigmoid(z)
        r = lax.rsqrt(jnp.mean(v * v, axis=-1, keepdims=True) + SSM_EPS)
        o_ref[...] = (v * r * g_ref[...]).astype(BF16)

    row = pl.BlockSpec((ROWS, DI), lambda i: (i, 0))
    return pl.pallas_call(
        body, name=name, out_shape=jax.ShapeDtypeStruct((t, DI), BF16), grid=(t // ROWS,),
        in_specs=[row, pl.BlockSpec((ROWS, DI), lambda i: (i, O_Z // DI)), pl.BlockSpec((1, DI), lambda i: (0, 0))],
        out_specs=row, compiler_params=_cp(("parallel",)),
    )(y_pre, proj, g_ssd)


def _gated_norm_bwd(y_pre, proj, g_ssd, dyn, dproj, *, name, jobs=()):
    t = y_pre.shape[0]

    def body(y_ref, z_ref, g_ref, dyn_ref, _, dy_ref, dz_ref, dg_ref):
        z = z_ref[...]
        s = _sigmoid(z)
        sz = z * s
        yv = y_ref[...]
        v = yv * sz
        r = lax.rsqrt(jnp.mean(v * v, axis=-1, keepdims=True) + SSM_EPS)
        vh = v * r
        dn = dyn_ref[...]
        dvh = dn * g_ref[...]
        dv = r * (dvh - vh * jnp.mean(dvh * vh, axis=-1, keepdims=True))
        dy_ref[...] = dv * sz
        dz_ref[...] = (dv * yv * s * (1.0 + z * (1.0 - s))).astype(BF16)

        @pl.when(pl.program_id(0) == 0)
        def _():
            dg_ref[...] = jnp.zeros_like(dg_ref)

        dg_ref[...] += jnp.broadcast_to(jnp.sum(dn * vh, axis=0, keepdims=True), dg_ref.shape)

    row = pl.BlockSpec((ROWS, DI), lambda i: (i, 0))
    return _call(
        body, jobs=jobs, name=name,
        out_shape=(jax.ShapeDtypeStruct((t, DI), F32), jax.ShapeDtypeStruct(dproj.shape, BF16),
                   jax.ShapeDtypeStruct((8, DI), F32)),
        grid=(t // ROWS,),
        in_specs=[row, pl.BlockSpec((ROWS, DI), lambda i: (i, O_Z // DI)), pl.BlockSpec((1, DI), lambda i: (0, 0)), row, ANY],
        out_specs=(row, pl.BlockSpec((ROWS, DI), lambda i: (i, O_Z // DI)), pl.BlockSpec((8, DI), lambda i: (0, 0))),
        compiler_params=_cp(("arbitrary",)), aliases={4: 1},
    )(y_pre, proj, g_ssd, dyn, dproj)


CONV_TC = 512


def _shift_down(x, s, row):
    if s == 0:
        return x
    return jnp.where(row >= s, pltpu.roll(x, s, 0), 0.0)


def _shift_up(x, s, row, t):
    if s == 0:
        return x
    return jnp.where(row < t - s, pltpu.roll(x, t - s, 0), 0.0)


def _conv_fwd(proj, conv_w, conv_b, *, name):
    t = proj.shape[0]

    def body(x_ref, w_ref, b_ref, o_ref):
        x = x_ref[...]
        row = lax.broadcasted_iota(jnp.int32, x.shape, 0)
        pre = jnp.broadcast_to(b_ref[...], x.shape)
        for k in range(CW):
            pre = pre + w_ref[k:k + 1, :] * _shift_down(x, CW - 1 - k, row)
        o_ref[...] = pre * _sigmoid(pre)

    return pl.pallas_call(
        body, name=name, out_shape=jax.ShapeDtypeStruct((t, CONV), F32), grid=(CONV // CONV_TC,),
        in_specs=[pl.BlockSpec((t, CONV_TC), lambda j: (0, O_XBC // CONV_TC + j)),
                  pl.BlockSpec((CW, CONV_TC), lambda j: (0, j)), pl.BlockSpec((1, CONV_TC), lambda j: (0, j))],
        out_specs=pl.BlockSpec((t, CONV_TC), lambda j: (0, j)), compiler_params=_cp(("parallel",)),
    )(proj, conv_w, conv_b)


def _conv_bwd(proj, conv_w, conv_b, dxs, db, dc, dproj, *, name, jobs=()):
    t = proj.shape[0]
    nx = DI // CONV_TC
    assert NG * NS == CONV_TC

    def body(x_ref, w_ref, b_ref, dxs_ref, db_ref, dc_ref, _, dx_ref, dw_ref, dbias_ref):
        j = pl.program_id(0)
        x = x_ref[...]
        row = lax.broadcasted_iota(jnp.int32, x.shape, 0)
        xs = [_shift_down(x, CW - 1 - k, row) for k in range(CW)]
        pre = jnp.broadcast_to(b_ref[...], x.shape)
        for k in range(CW):
            pre = pre + w_ref[k:k + 1, :] * xs[k]
        s = _sigmoid(pre)
        da = jnp.where(j < nx, dxs_ref[...], jnp.where(j == nx, db_ref[...], dc_ref[...]))
        dpre = da * s * (1.0 + pre * (1.0 - s))
        dx = jnp.zeros_like(x)
        row8 = lax.broadcasted_iota(jnp.int32, dw_ref.shape, 0)
        dw = jnp.zeros(dw_ref.shape, F32)
        for k in range(CW):
            dx = dx + w_ref[k:k + 1, :] * _shift_up(dpre, CW - 1 - k, row, t)
            dw = dw + jnp.where(row8 == k, jnp.sum(dpre * xs[k], axis=0, keepdims=True), 0.0)
        dx_ref[...] = dx.astype(BF16)
        dw_ref[...] = dw
        dbias_ref[...] = jnp.broadcast_to(jnp.sum(dpre, axis=0, keepdims=True), dbias_ref.shape)

    col8 = pl.BlockSpec((8, CONV_TC), lambda j: (0, j))
    xbc = pl.BlockSpec((t, CONV_TC), lambda j: (0, O_XBC // CONV_TC + j))
    whole = pl.BlockSpec((t, CONV_TC), lambda j: (0, 0))
    return _call(
        body, jobs=jobs, name=name,
        out_shape=(jax.ShapeDtypeStruct(dproj.shape, BF16), jax.ShapeDtypeStruct((8, CONV), F32),
                   jax.ShapeDtypeStruct((8, CONV), F32)),
        grid=(CONV // CONV_TC,),
        in_specs=[xbc, pl.BlockSpec((CW, CONV_TC), lambda j: (0, j)), pl.BlockSpec((1, CONV_TC), lambda j: (0, j)),
                  pl.BlockSpec((t, CONV_TC), lambda j: (0, jnp.minimum(j, nx - 1))), whole, whole, ANY],
        out_specs=(xbc, col8, col8),
        compiler_params=_cp(("arbitrary",)), aliases={6: 0},
    )(proj, conv_w, conv_b, dxs, db, dc, dproj)


def _rope_tables(positions, t):
    half = HD // 2
    inv_freq = ROPE_THETA ** (-jnp.arange(half, dtype=F32) * 2.0 / HD)
    ang = positions.reshape(t).astype(F32)[:, None] * inv_freq
    cos, sin = jnp.cos(ang), jnp.sin(ang)
    return jnp.concatenate([cos] * 4, axis=1), jnp.concatenate([-sin, sin] * 2, axis=1)


def _lane_consts():
    lane = lax.broadcasted_iota(jnp.int32, (L, 128), 1)
    return lane, (lane % HD) < (HD // 2), lane < HD


def _rope(tv, cos, sin, lo):
    return tv * cos + jnp.where(lo, pltpu.roll(tv, 128 - HD // 2, 1), pltpu.roll(tv, HD // 2, 1)) * sin


def _rope_t(dv, cos, sin, lo):
    ds = dv * sin
    return dv * cos + jnp.where(lo, pltpu.roll(ds, 128 - HD // 2, 1), pltpu.roll(ds, HD // 2, 1))


def _placed(chunk, g, half0):
    own = jnp.where(half0 if g % 2 == 0 else jnp.logical_not(half0), chunk, 0.0)
    other = pltpu.roll(own, HD, 1)
    return (own, other) if g % 2 == 0 else (other, own)


def _unplace(acc, hf, g, half0):
    v = jnp.where(half0 if hf == 0 else jnp.logical_not(half0), acc, 0.0)
    return v if hf == g % 2 else pltpu.roll(v, HD, 1)


def _attn_fwd(proj, cos, sin, sinks, *, name, jobs=()):
    t = proj.shape[0]
    nb = t // L
    scale = HD ** -0.5

    def body(sink_ref, q_ref, kc_ref, kp_ref, vc_ref, vp_ref, cc_ref, sc_ref, cp_ref, sp_ref, o_ref, lse_ref):
        i = pl.program_id(0)
        lane, lo, half0 = _lane_consts()
        cos_c, sin_c, cos_p, sin_p = cc_ref[...], sc_ref[...], cp_ref[...], sp_ref[...]
        row = lax.broadcasted_iota(jnp.int32, (L, 2 * L), 0)
        col = lax.broadcasted_iota(jnp.int32, (L, 2 * L), 1)
        valid = jnp.logical_or(jnp.logical_and(jnp.logical_and(col < L, col > row), i > 0),
                               jnp.logical_and(col >= L, col - L <= row))
        kc = [_rope(kc_ref[:, 128 * m:128 * (m + 1)], cos_c, sin_c, lo) for m in range(2)]
        kp = [_rope(kp_ref[:, 128 * m:128 * (m + 1)], cos_p, sin_p, lo) for m in range(2)]
        lse_acc = jnp.zeros((L, 128), F32)
        outs = [jnp.zeros((L, 128), F32) for _ in range(QD // 128)]
        qs = [(_rope(q_ref[:, 128 * ch:128 * (ch + 1)], cos_c, sin_c, lo) * scale).astype(BF16) for ch in range(QD // 128)]
        both = lambda prev, cur, g: [jnp.concatenate([a, b], axis=0).astype(BF16)
                                     for a, b in zip(_placed(prev, g, half0), _placed(cur, g, half0))]
        for g in range(NKV):
            sl = slice(128 * (g // 2), 128 * (g // 2 + 1))
            kv = both(kp[g // 2], kc[g // 2], g)
            vv = both(vp_ref[:, sl], vc_ref[:, sl], g)
            for r in range(NQH // NKV):
                h = g * (NQH // NKV) + r
                ch, hf = h // 2, h % 2
                s = jnp.where(valid, _dot_nt(qs[ch], kv[hf]), NEG)
                sink = sink_ref[0, h]
                mx = jnp.maximum(jnp.max(s, axis=-1, keepdims=True), sink)
                e = jnp.exp(s - mx)
                den = jnp.sum(e, axis=-1, keepdims=True) + jnp.exp(sink - mx)
                outs[ch] = outs[ch] + _dot((e * (1.0 / den)).astype(BF16), vv[hf])
                lse_acc = jnp.where(lane == h, mx + jnp.log(den), lse_acc)
        for ch in range(QD // 128):
            o_ref[:, 128 * ch:128 * (ch + 1)] = outs[ch].astype(BF16)
        lse_ref[...] = lse_acc

    prev = lambda i: jnp.maximum(i - 1, 0)
    tab_c = pl.BlockSpec((L, 128), lambda i: (i, 0))
    tab_p = pl.BlockSpec((L, 128), lambda i: (prev(i), 0))
    return _call(
        body, jobs=jobs, name=name,
        out_shape=(jax.ShapeDtypeStruct((t, QD), BF16), jax.ShapeDtypeStruct((t, 128), F32)),
        grid=(nb,),
        in_specs=[pl.BlockSpec(memory_space=pltpu.SMEM),
                  pl.BlockSpec((L, QD), lambda i: (i, O_Q // QD)),
                  pl.BlockSpec((L, KVD), lambda i: (i, O_K // KVD)), pl.BlockSpec((L, KVD), lambda i: (prev(i), O_K // KVD)),
                  pl.BlockSpec((L, KVD), lambda i: (i, O_V // KVD)), pl.BlockSpec((L, KVD), lambda i: (prev(i), O_V // KVD)),
                  tab_c, tab_c, tab_p, tab_p],
        out_specs=(pl.BlockSpec((L, QD), lambda i: (i, 0)), pl.BlockSpec((L, 128), lambda i: (i, 0))),
        compiler_params=_cp(("parallel",)),
    )(sinks, proj, proj, proj, proj, proj, cos, sin, cos, sin)


def _attn_bwd(proj, cos, sin, sinks, attn, lse, dattn, dproj, *, name, jobs=()):
    t = proj.shape[0]
    nb = t // L
    scale = HD ** -0.5

    def body(sink_ref, qi_ref, qn_ref, kc_ref, kp_ref, vc_ref, vp_ref, doi_ref, don_ref, oi_ref, on_ref,
             lsei_ref, lsen_ref, cc_ref, sc_ref, cp_ref, sp_ref, cn_ref, sn_ref, _, dqkv_ref, dsk_ref):
        i = pl.program_id(0)
        lane, lo, half0 = _lane_consts()
        half1 = jnp.logical_not(half0)
        cos_c, sin_c = cc_ref[...], sc_ref[...]
        row = lax.broadcasted_iota(jnp.int32, (L, 2 * L), 0)
        col = lax.broadcasted_iota(jnp.int32, (L, 2 * L), 1)
        valid = jnp.logical_or(jnp.logical_and(jnp.logical_and(col < L, col > row), i > 0),
                               jnp.logical_and(col >= L, col - L <= row))
        m_next = jnp.logical_and(col[:, :L] > row[:, :L], i < nb - 1)
        kc = [_rope(kc_ref[:, 128 * m:128 * (m + 1)], cos_c, sin_c, lo) for m in range(2)]
        kp = [_rope(kp_ref[:, 128 * m:128 * (m + 1)], cp_ref[...], sp_ref[...], lo) for m in range(2)]
        lse_i, lse_n = lsei_ref[...], lsen_ref[...]
        dk_acc = [jnp.zeros((L, 128), F32) for _ in range(2)]
        dv_acc = [jnp.zeros((L, 128), F32) for _ in range(2)]
        dsk_acc = jnp.zeros((1, 128), F32)
        lane1 = lax.broadcasted_iota(jnp.int32, (1, 128), 1)
        both = lambda prev, cur, g: [jnp.concatenate([a, b], axis=0).astype(BF16)
                                     for a, b in zip(_placed(prev, g, half0), _placed(cur, g, half0))]
        kvs = [both(kp[g // 2], kc[g // 2], g) for g in range(NKV)]
        vvs = [both(vp_ref[:, 128 * (g // 2):128 * (g // 2 + 1)], vc_ref[:, 128 * (g // 2):128 * (g // 2 + 1)], g)
               for g in range(NKV)]
        for ch in range(QD // 128):
            sl = slice(128 * ch, 128 * (ch + 1))
            q_i = (_rope(qi_ref[:, sl], cos_c, sin_c, lo) * scale).astype(BF16)
            q_n = (_rope(qn_ref[:, sl], cn_ref[...], sn_ref[...], lo) * scale).astype(BF16)
            q_in = jnp.concatenate([q_i, q_n], axis=0)
            do_i, do_n = doi_ref[:, sl], don_ref[:, sl]
            do_ib, do_nb = do_i.astype(BF16), do_n.astype(BF16)
            do_in = jnp.concatenate([do_ib, do_nb], axis=0)
            od_i = do_i * oi_ref[:, sl].astype(F32)
            od_n = do_n * on_ref[:, sl].astype(F32)
            dq_ch = jnp.zeros((L, 128), F32)
            for hf in range(2):
                h = 2 * ch + hf
                g = h // (NQH // NKV)
                hm = half0 if hf == 0 else half1
                kv, vv = kvs[g][hf], vvs[g][hf]
                kcv, vcv = kv[L:], vv[L:]
                dl_i = jnp.sum(jnp.where(hm, od_i, 0.0), axis=-1, keepdims=True)
                dl_n = jnp.sum(jnp.where(hm, od_n, 0.0), axis=-1, keepdims=True)
                ls_i = jnp.sum(jnp.where(lane == h, lse_i, 0.0), axis=-1, keepdims=True)
                ls_n = jnp.sum(jnp.where(lane == h, lse_n, 0.0), axis=-1, keepdims=True)
                p = jnp.where(valid, jnp.exp(_dot_nt(q_i, kv) - ls_i), 0.0)
                ds = (p * (_dot_nt(do_ib, vv) - dl_i)).astype(BF16)
                dq_ch = dq_ch + jnp.where(hm, _dot(ds, kv) * scale, 0.0)
                sink = sink_ref[0, h]
                dsk = -jnp.sum(jnp.exp(sink - ls_i) * dl_i, axis=0, keepdims=True)
                dsk_acc = dsk_acc + jnp.where(lane1 == h, dsk, 0.0)
                p_n = jnp.where(m_next, jnp.exp(_dot_nt(q_n, kcv) - ls_n), 0.0)
                ds_n = (p_n * (_dot_nt(do_nb, vcv) - dl_n)).astype(BF16)
                dv_h = _dot_tn(jnp.concatenate([p[:, L:].astype(BF16), p_n.astype(BF16)], axis=0), do_in)
                dk_h = _dot_tn(jnp.concatenate([ds[:, L:], ds_n], axis=0), q_in)
                dv_acc[g // 2] = dv_acc[g // 2] + _unplace(dv_h, hf, g, half0)
                dk_acc[g // 2] = dk_acc[g // 2] + _unplace(dk_h, hf, g, half0)
            dqkv_ref[:, sl] = _rope_t(dq_ch, cos_c, sin_c, lo).astype(BF16)
        for m in range(2):
            dqkv_ref[:, QD + 128 * m:QD + 128 * (m + 1)] = _rope_t(dk_acc[m], cos_c, sin_c, lo).astype(BF16)
            dqkv_ref[:, QD + KVD + 128 * m:QD + KVD + 128 * (m + 1)] = dv_acc[m].astype(BF16)

        @pl.when(i == 0)
        def _():
            dsk_ref[...] = jnp.zeros_like(dsk_ref)

        dsk_ref[...] += jnp.broadcast_to(dsk_acc, dsk_ref.shape)

    prev = lambda i: jnp.maximum(i - 1, 0)
    nxt = lambda i: jnp.minimum(i + 1, nb - 1)
    cur_q = pl.BlockSpec((L, QD), lambda i: (i, 0))
    nxt_q = pl.BlockSpec((L, QD), lambda i: (nxt(i), 0))
    tab = lambda f: pl.BlockSpec((L, 128), lambda i: (f(i), 0))
    ident = lambda i: i
    qkv = QD + 2 * KVD
    assert O_K == O_Q + QD and O_V == O_K + KVD and O_Q % qkv == 0
    return _call(
        body, jobs=jobs, name=name,
        out_shape=(jax.ShapeDtypeStruct(dproj.shape, BF16), jax.ShapeDtypeStruct((8, 128), F32)),
        grid=(nb,),
        in_specs=[pl.BlockSpec(memory_space=pltpu.SMEM),
                  pl.BlockSpec((L, QD), lambda i: (i, O_Q // QD)), pl.BlockSpec((L, QD), lambda i: (nxt(i), O_Q // QD)),
                  pl.BlockSpec((L, KVD), lambda i: (i, O_K // KVD)), pl.BlockSpec((L, KVD), lambda i: (prev(i), O_K // KVD)),
                  pl.BlockSpec((L, KVD), lambda i: (i, O_V // KVD)), pl.BlockSpec((L, KVD), lambda i: (prev(i), O_V // KVD)),
                  cur_q, nxt_q, cur_q, nxt_q, tab(ident), tab(nxt),
                  tab(ident), tab(ident), tab(prev), tab(prev), tab(nxt), tab(nxt), ANY],
        out_specs=(pl.BlockSpec((L, qkv), lambda i: (i, O_Q // qkv)), pl.BlockSpec((8, 128), lambda i: (0, 0))),
        compiler_params=_cp(("arbitrary",)), aliases={19: 0},
    )(sinks, proj, proj, proj, proj, proj, proj, dattn, dattn, attn, attn, lse, lse, cos, sin, cos, sin, cos, sin, dproj)


PAIRS = NH // NG // 2


def _softplus(x):
    return jnp.maximum(x, 0.0) + jnp.log(1.0 + jnp.exp(-jnp.abs(x)))


def _ssd_chunk(g, xps, dtr, bm, cm, sps, dtb, alog, dsk):
    lane = lax.broadcasted_iota(jnp.int32, (L, 128), 1)
    lane1 = lax.broadcasted_iota(jnp.int32, (1, 128), 1)
    row = lax.broadcasted_iota(jnp.int32, (L, L), 0)
    col = lax.broadcasted_iota(jnp.int32, (L, L), 1)
    rowc = lax.broadcasted_iota(jnp.int32, (128, 1), 0)
    tril = col <= row
    dt = _softplus(dtr + dtb)
    a = dt * (-jnp.exp(alog))
    a_cs = lax.dot_general(tril.astype(F32), a, (((1,), (0,)), ((), ())), precision=lax.Precision.HIGHEST,
                           preferred_element_type=F32)
    a_cst = a_cs.T
    a_last = jnp.sum(jnp.where(row == L - 1, a_cs, 0.0), axis=0, keepdims=True)
    cb = _bdot_nt(cm, bm)
    ys, snew = [], []
    for q in range(PAIRS):
        xp, sp = xps[q], sps[q]
        skip = jnp.zeros((L, 128), F32)
        keep = jnp.zeros((128, 1), F32)
        ms, xds, cds, sms, bds = [], [], [], [], []
        for hh in range(2):
            h = g * 2 * PAIRS + 2 * q + hh
            hm = (lane < HD) if hh == 0 else (lane >= HD)
            rm = (rowc < HD) if hh == 0 else (rowc >= HD)
            dt_h = jnp.sum(jnp.where(lane == h, dt, 0.0), axis=1, keepdims=True)
            acs_h = jnp.sum(jnp.where(lane == h, a_cs, 0.0), axis=1, keepdims=True)
            acst_h = jnp.sum(jnp.where(row == h, a_cst, 0.0), axis=0, keepdims=True)
            al_h = jnp.sum(jnp.where(lane1 == h, a_last, 0.0), axis=1, keepdims=True)
            dsk_h = jnp.sum(jnp.where(lane1 == h, dsk, 0.0), axis=1, keepdims=True)
            decay = jnp.where(tril, jnp.exp(jnp.where(tril, acs_h - acst_h, 0.0)), 0.0)
            xh = jnp.where(hm, xp, 0.0)
            ms.append(cb * decay)
            xds.append(xh * dt_h)
            cds.append(cm * jnp.exp(acs_h))
            sms.append(jnp.where(rm, sp, 0.0))
            bds.append(bm * jnp.exp(al_h - acs_h))
            skip = skip + dsk_h * xh
            keep = keep + jnp.where(rm, jnp.exp(al_h), 0.0)
        xd2 = jnp.concatenate(xds, axis=0)
        y_pair = (_bdot(jnp.concatenate(ms, axis=1), xd2)
                  + _bdot_nt(jnp.concatenate(cds, axis=1), jnp.concatenate(sms, axis=1)) + skip)
        ys.append(y_pair)
        snew.append(sp * keep + _bdot_tn(xd2, jnp.concatenate(bds, axis=0)))
    return ys, snew


def _ssd_specs(t):
    nc = t // L
    xs = lambda f: pl.BlockSpec((L, 128 * PAIRS), lambda c, g: (f(c), g))
    bspec = lambda f: pl.BlockSpec((L, NS), lambda c, g: (f(c), DI // NS + g))
    cspec = lambda f: pl.BlockSpec((L, NS), lambda c, g: (f(c), DI // NS + NG + g))
    dts = lambda f: pl.BlockSpec((L, 128), lambda c, g: (f(c), O_DT // 128))
    par = pl.BlockSpec((1, 128), lambda c, g: (0, 0))
    st = lambda f: pl.BlockSpec((1, 1, PAIRS, 128, NS), lambda c, g: (f(c), g, 0, 0, 0))
    return nc, xs, bspec, cspec, dts, par, st


def _ssd_fwd(xbc_act, proj, dtb, alog, dsk, *, name, jobs=()):
    t = proj.shape[0]
    nc, xs, bspec, cspec, dts, par, st = _ssd_specs(t)
    ident = lambda c: c

    def body(x_ref, b_ref, c_ref, dt_ref, dtb_ref, al_ref, dsk_ref, y_ref, sin_ref, s_ref):
        c, g = pl.program_id(0), pl.program_id(1)

        @pl.when(c == 0)
        def _():
            s_ref[g] = jnp.zeros((PAIRS, 128, NS), F32)

        sps = [s_ref[g, q] for q in range(PAIRS)]
        for q in range(PAIRS):
            sin_ref[0, 0, q] = sps[q]
        xps = [x_ref[:, 128 * q:128 * (q + 1)] for q in range(PAIRS)]
        ys, snew = _ssd_chunk(g, xps, dt_ref[...], b_ref[...], c_ref[...], sps, dtb_ref[...], al_ref[...], dsk_ref[...])
        for q in range(PAIRS):
            y_ref[:, 128 * q:128 * (q + 1)] = ys[q]
            s_ref[g, q] = snew[q]

    return _call(
        body, jobs=jobs, name=name,
        out_shape=(jax.ShapeDtypeStruct((t, DI), F32), jax.ShapeDtypeStruct((nc, NG, PAIRS, 128, NS), F32)),
        grid=(nc, NG),
        in_specs=[xs(ident), bspec(ident), cspec(ident), dts(ident), par, par, par],
        out_specs=(pl.BlockSpec((L, 128 * PAIRS), lambda c, g: (c, g)), st(ident)),
        scratch_shapes=[pltpu.VMEM((NG, PAIRS, 128, NS), F32)],
        compiler_params=_cp(("arbitrary", "arbitrary")),
    )(xbc_act, xbc_act, xbc_act, proj, dtb, alog, dsk)


def _ssd_bwd(xbc_act, proj, dtb, alog, dsk, states, dy, dproj, *, name, jobs=()):
    t = proj.shape[0]
    nc, xs, bspec, cspec, dts, par, st = _ssd_specs(t)
    rev = lambda c: nc - 1 - c

    def body(x_ref, b_ref, c_ref, dt_ref, dtb_ref, al_ref, dsk_ref, sin_ref, dy_ref, _,
             dx_ref, db_ref, dc_ref, ddtp_ref, ddtb_ref, dal_ref, ddsk_ref, ds_ref, ddt_ref):
        c, g = pl.program_id(0), pl.program_id(1)

        @pl.when(c == 0)
        def _():
            ds_ref[g] = jnp.zeros((PAIRS, 128, NS), F32)

        @pl.when(jnp.logical_and(c == 0, g == 0))
        def _():
            ddtb_ref[...] = jnp.zeros_like(ddtb_ref)
            dal_ref[...] = jnp.zeros_like(dal_ref)
            ddsk_ref[...] = jnp.zeros_like(ddsk_ref)

        @pl.when(g == 0)
        def _():
            ddt_ref[...] = jnp.zeros_like(ddt_ref)

        sps = [sin_ref[0, 0, q] for q in range(PAIRS)]
        xps = [x_ref[:, 128 * q:128 * (q + 1)] for q in range(PAIRS)]
        _, vjp = jax.vjp(functools.partial(_ssd_chunk, g), xps, dt_ref[...], b_ref[...], c_ref[...], sps,
                         dtb_ref[...], al_ref[...], dsk_ref[...])
        dys = [dy_ref[:, 128 * q:128 * (q + 1)] for q in range(PAIRS)]
        dss = [ds_ref[g, q] for q in range(PAIRS)]
        dxps, ddt, db, dc, dsps, ddtb, dal, ddsk = vjp((dys, dss))
        for q in range(PAIRS):
            dx_ref[:, 128 * q:128 * (q + 1)] = dxps[q]
            ds_ref[g, q] = dsps[q]
        db_ref[...] = db
        dc_ref[...] = dc
        ddt_ref[...] += ddt
        ddtb_ref[...] += jnp.broadcast_to(ddtb, ddtb_ref.shape)
        dal_ref[...] += jnp.broadcast_to(dal, dal_ref.shape)
        ddsk_ref[...] += jnp.broadcast_to(ddsk, ddsk_ref.shape)

        @pl.when(g == NG - 1)
        def _():
            ddtp_ref[:, :128] = ddt_ref[...].astype(BF16)
            ddtp_ref[:, 128:] = jnp.zeros((L, DT_PAD - 128), BF16)

    acc = pl.BlockSpec((8, 128), lambda c, g: (0, 0))
    o8 = jax.ShapeDtypeStruct((8, 128), F32)
    return _call(
        body, jobs=jobs, name=name,
        out_shape=(jax.ShapeDtypeStruct((t, DI), F32), jax.ShapeDtypeStruct((t, NG * NS), F32),
                   jax.ShapeDtypeStruct((t, NG * NS), F32), jax.ShapeDtypeStruct(dproj.shape, BF16), o8, o8, o8),
        grid=(nc, NG),
        in_specs=[xs(rev), bspec(rev), cspec(rev), dts(rev), par, par, par, st(rev),
                  pl.BlockSpec((L, 128 * PAIRS), lambda c, g: (rev(c), g)), ANY],
        out_specs=(pl.BlockSpec((L, 128 * PAIRS), lambda c, g: (rev(c), g)),
                   pl.BlockSpec((L, NS), lambda c, g: (rev(c), g)), pl.BlockSpec((L, NS), lambda c, g: (rev(c), g)),
                   pl.BlockSpec((L, DT_PAD), lambda c, g: (rev(c), O_DT // DT_PAD)), acc, acc, acc),
        scratch_shapes=[pltpu.VMEM((NG, PAIRS, 128, NS), F32), pltpu.VMEM((L, 128), F32)],
        compiler_params=_cp(("arbitrary", "arbitrary")), aliases={9: 3},
    )(xbc_act, xbc_act, xbc_act, proj, dtb, alog, dsk, states, dy, dproj)


def _pad_lanes(v, n=128):
    return jnp.pad(v, ((0, 0), (0, n - v.shape[1])))


def _local_step(x, p, positions, target, small, plan):
    t = x.shape[0]
    cos, sin = _rope_tables(positions, t)
    dtb, alog, dsk = _pad_lanes(small["dt_bias"]), _pad_lanes(small["a_log"]), _pad_lanes(small["d_skip"])
    w, jobs = plan.w, plan.jobs

    def mm(a, b, *, name, tm=t, tn=512, **kw):
        return _matmul(a, b, tm=tm, tn=tn, name=name, jobs=jobs(name), **kw)

    tkl = FFN // 4

    def dw(wname, a, dy, *, name, tm):
        plan.g(wname, _matmul(a, dy, ta=True, out_dtype=BF16, tm=tm, tn=512, tk=t, name=name, jobs=jobs(name)))

    u = _rmsnorm_fwd(x, small["g_mix"], name="norm_mix")
    proj = mm(u, w("w_in"), tn=1024, tk=D, name="mm_in")
    attn, lse = _attn_fwd(proj, cos, sin, small["sinks"], name="attn_fwd", jobs=jobs("attn_fwd"))
    out_a = mm(attn, w("w_attn_br"), tk=QD, name="mm_attn_br")
    xbc_act = _conv_fwd(proj, small["conv_w"], small["conv_b"], name="conv_fwd")
    y_pre, states = _ssd_fwd(xbc_act, proj, dtb, alog, dsk, name="ssd_fwd", jobs=jobs("ssd_fwd"))
    yn = _gated_norm_fwd(y_pre, proj, small["g_ssd"], name="gated_norm_fwd")
    out_s = mm(yn, w("w_ssd_br"), tk=DI, name="mm_ssd_br")
    merged = _merge_fwd(proj, out_a, out_s, name="merge_fwd")
    h1 = mm(merged, w("w_o"), add=x, tk=D, name="mm_o")
    f = _rmsnorm_fwd(h1, small["g_ffn"], name="norm_ffn")
    gate, up, act = _swiglu_fwd(f, w("w_gate"), w("w_up"), name="swiglu_fwd", jobs=jobs("swiglu_fwd"))
    h2 = mm(act, w("w_down"), add=h1, tm=t // 2, tk=FFN // 2, name="mm_down")
    e = _rmsnorm_fwd(h2, small["g_ple"], name="norm_ple")
    pgl = mm(e, w("w_ple_gate"), tk=D, name="mm_ple_gate")
    pb = p.astype(BF16)
    pp = mm(pb, w("w_ple_proj"), tk=PLE, name="mm_ple_proj")
    dh3, dpgl, dpp, loss, dg_final = _final(h2, pgl, pp, target, small["g_final"].reshape(1, D), name="final")

    dw("w_ple_proj", pb, dpp, tm=PLE, name="mm_d_ple_proj")
    dw("w_ple_gate", e, dpgl, tm=D, name="mm_d_ple_gate")
    de = mm(dpgl, w("w_ple_gate"), tb=True, tk=D, name="mm_de")
    dh2, dh2b, dg_ple = _rmsnorm_bwd(h2, small["g_ple"], de, dh3, name="norm_ple_bwd", jobs=jobs("norm_ple_bwd"))
    dw("w_down", act, dh2b, tm=FFN // 2, name="mm_d_down")
    dgate, dup = _swiglu_bwd(dh2b, w("w_down"), gate, up, name="swiglu_bwd", jobs=jobs("swiglu_bwd"))
    dw("w_gate", f, dgate, tm=D, name="mm_d_gate")
    dw("w_up", f, dup, tm=D, name="mm_d_up")
    df = mm(dgate, w("w_gate"), tb=True, tn=1024, tk=tkl, name="mm_df_gate")
    df = mm(dup, w("w_up"), tb=True, add=df, tm=t // 2, tk=FFN // 2, name="mm_df_up")
    dh1, dh1b, dg_ffn = _rmsnorm_bwd(h1, small["g_ffn"], df, dh2, name="norm_ffn_bwd", jobs=jobs("norm_ffn_bwd"))
    dw("w_o", merged, dh1b, tm=D, name="mm_d_o")
    dmerged = mm(dh1b, w("w_o"), tb=True, tk=D, name="mm_dmerged")
    dout_a, dout_s, dproj = _merge_bwd(proj, out_a, out_s, dmerged, name="merge_bwd")
    dw("w_attn_br", attn, dout_a, tm=QD, name="mm_d_attn_br")
    dw("w_ssd_br", yn, dout_s, tm=DI, name="mm_d_ssd_br")
    dattn = mm(dout_a, w("w_attn_br"), tb=True, tk=D, name="mm_dattn")
    dyn = mm(dout_s, w("w_ssd_br"), tb=True, tk=D, name="mm_dyn")
    dproj, dsinks = _attn_bwd(proj, cos, sin, small["sinks"], attn, lse, dattn, dproj, name="attn_bwd",
                              jobs=jobs("attn_bwd"))
    dy_pre, dproj, dg_ssd = _gated_norm_bwd(y_pre, proj, small["g_ssd"], dyn, dproj, name="gated_norm_bwd",
                                            jobs=jobs("gated_norm_bwd"))
    dxs, db, dc, dproj, ddtb, dalog, ddsk = _ssd_bwd(xbc_act, proj, dtb, alog, dsk, states, dy_pre, dproj, name="ssd_bwd",
                                                     jobs=jobs("ssd_bwd"))
    dproj, dconv_w, dconv_b = _conv_bwd(proj, small["conv_w"], small["conv_b"], dxs, db, dc, dproj, name="conv_bwd",
                                        jobs=jobs("conv_bwd"))
    for which, h in (("send", 1 - plan.core), ("keep", plan.core)):
        uh = lax.dynamic_slice_in_dim(u, h * (D // 2), D // 2, axis=1)
        name = "mm_d_in_" + which
        plan.g_half("w_in", which, _matmul(uh, dproj, ta=True, out_dtype=BF16, tm=D // 2, tn=1024, tk=t, name=name,
                                           jobs=jobs(name)))
    du = mm(dproj, w("w_in"), tb=True, tn=1024, tk=tkl, name="mm_du")
    grad_x, _, dg_mix = _rmsnorm_bwd(x, small["g_mix"], du, dh1, name="norm_mix_bwd", jobs=jobs("norm_mix_bwd"))

    gs = {
        "g_mix": dg_mix[:1], "conv_w": dconv_w[:CW], "conv_b": dconv_b[:1], "dt_bias": ddtb[:1, :NH],
        "a_log": dalog[:1, :NH], "d_skip": ddsk[:1, :NH], "g_ssd": dg_ssd[:1], "sinks": dsinks[:1, :NQH],
        "g_ffn": dg_ffn[:1], "g_ple": dg_ple[:1], "g_final": dg_final[0],
    }
    return loss, grad_x, gs


def _shard_pieces():
    segs = ((R_Q, QD, O_Q), (R_K, KVD, O_K), (R_V, KVD, O_V), (R_Z, DI, O_Z), (R_XBC, CONV, O_XBC), (R_DT, NH, O_DT),
            (R_GA, D, O_GA), (R_GS, D, O_GS))
    cs = IN_DIM // NCHIP
    out = []
    for j in range(NCHIP):
        for r0, n, k0 in segs:
            lo, hi = max(r0, j * cs), min(r0 + n, (j + 1) * cs)
            if lo < hi:
                out.append((j, lo - j * cs, hi - lo, k0 + lo - r0))
    return out


SLAB = IN_DIM // NCHIP
SLAB_PAD = -(-SLAB // 128) * 128
REMAP_ROWS = 256


def _lane_remap(src, dst_slabs, dst_cols, moves, *, name, add=None, jobs=()):
    s_n, rows, s_cols = src.shape
    assert s_cols % 128 == 0 and dst_cols % 128 == 0 and rows % REMAP_ROWS == 0
    half = REMAP_ROWS // 2

    def body(s_ref, *refs):
        d_ref = refs[-1]
        lane = lax.broadcasted_iota(jnp.int32, (half, 128), 1)
        tiles = {}

        def tile(j, m):
            if (j, m) not in tiles:
                tiles[j, m] = pltpu.bitcast(s_ref[j, :, 128 * m:128 * (m + 1)], jnp.uint32)
            return tiles[j, m]

        def window(j, base):
            m0, s = base // 128, base % 128
            left = tile(j, m0) if 0 <= m0 < s_cols // 128 else None
            if s == 0:
                return left
            right = tile(j, m0 + 1) if 0 <= m0 + 1 < s_cols // 128 else None
            left = None if left is None else pltpu.roll(left, 128 - s, 1)
            right = None if right is None else pltpu.roll(right, 128 - s, 1)
            if left is None or right is None:
                return right if left is None else left
            return jnp.where(lane < 128 - s, left, right)

        for ds in range(dst_slabs):
            for t in range(dst_cols // 128):
                o = 128 * t
                acc = jnp.zeros((half, 128), jnp.uint32)
                for sj, sc, n, dj, dc in moves:
                    lo, hi = max(o, dc) - o, min(o + 128, dc + n) - o
                    if dj != ds or lo >= hi:
                        continue
                    win = window(sj, o - dc + sc)
                    acc = win if (lo, hi) == (0, 128) else jnp.where(jnp.logical_and(lane >= lo, lane < hi), win, acc)
                out = pltpu.bitcast(acc, BF16)
                if add is not None:
                    out = (out.astype(F32) + refs[0][ds, :, o:o + 128].astype(F32)).astype(BF16)
                d_ref[ds, :, o:o + 128] = out

    dst_blk = pl.BlockSpec((dst_slabs, REMAP_ROWS, dst_cols), lambda i: (0, i, 0))
    return _call(
        body, jobs=jobs, name=name, out_shape=jax.ShapeDtypeStruct((dst_slabs, rows, dst_cols), BF16),
        grid=(rows // REMAP_ROWS,),
        in_specs=[pl.BlockSpec((s_n, REMAP_ROWS, s_cols), lambda i: (0, i, 0))] + ([dst_blk] if add is not None else []),
        out_specs=dst_blk, compiler_params=_cp(("parallel",)),
    )(*((src,) if add is None else (src, add)))


def _slabs_to_kernel_cols(slabs, *, name, jobs=()):
    moves = [(j, a, n, 0, k0) for j, a, n, k0 in _shard_pieces()]
    return _lane_remap(slabs, 1, NP, moves, name=name, jobs=jobs)[0]


def _kernel_cols_to_slabs(g, *, name, add=None, jobs=()):
    moves = [(0, k0, n, j, a) for j, a, n, k0 in _shard_pieces()]
    return _lane_remap(g[None], NCHIP, SLAB_PAD, moves, name=name, add=add, jobs=jobs)


MATS = {
    n: (n, kind, 1, r, c, tp, tf) for n, kind, r, c, tp, tf in (
        ("w_in", "stk", 2048, SLAB_PAD, 256, 256),
        ("w_attn_br", "col", 1024, 512, 256, 256),
        ("w_ssd_br", "row", 512, 2048, 512, 256),
        ("w_o", "row", 512, 2048, 512, 256),
        ("w_gate", "col", 2048, 1408, 256, 256),
        ("w_up", "col", 2048, 1408, 256, 256),
        ("w_down", "row", 1408, 2048, 704, 704),
        ("w_ple_gate", "row", 512, 2048, 512, 256),
        ("w_ple_proj", "col", 256, 512, 128, 128),
    )}


def _pos():
    return lax.axis_index("x"), lax.axis_index("y"), lax.axis_index("c")


def _flip(v, a):
    return 1 - v if a else v


def _remote(src, dst, send, recv, dev):
    return pltpu.make_async_remote_copy(src_ref=src, dst_ref=dst, send_sem=send, recv_sem=recv, device_id=dev,
                                        device_id_type=MESH)


def _whole_shape(kind, g, r, c):
    return {"row": (g, NCHIP * r, c), "col": (g, r, NCHIP * c), "stk": (NCHIP, r, c)}[kind]


def _cols(j, c):
    return pl.ds(pl.multiple_of(j * c, 128), c)


def _whole_shard(kind, ref, j, r, c):
    if kind == "row":
        return ref.at[:, pl.ds(j * r, r), :]
    if kind == "col":
        return ref.at[:, :, _cols(j, c)]
    return ref.at[pl.ds(j, 1)]


def _whole_rows(kind, ref, j, row, n, r, c):
    if kind == "row":
        return ref.at[:, pl.ds(j * r + row, n), :]
    if kind == "col":
        return ref.at[:, pl.ds(row, n), _cols(j, c)]
    return ref.at[pl.ds(j, 1), pl.ds(row, n), :]


class _GatherJob(_Job):
    has_mid = True
    NCP = 13

    def __init__(self, names, shards, sink):
        self.mats = [MATS[n] for n in names]
        self.srcs = [shards[n] for n in names]
        self.news = [jax.ShapeDtypeStruct(_whole_shape(kind, g, r, c), BF16) for _, kind, g, r, c, _, _ in self.mats]
        n = len(names)
        self.scratch = [pltpu.SemaphoreType.DMA((self.NCP * n,)), pltpu.SemaphoreType.DMA((self.NCP * n,))]
        self.names, self.sink = names, sink

    def _copies(self, srcs, news, sems):
        send, recv = sems
        x, y, c = _pos()
        me, jx, jy, jd = 2 * x + y, 2 * (1 - x) + y, 2 * x + (1 - y), 2 * (1 - x) + (1 - y)
        nbx, nby, sib = (1 - x, y, c), (x, 1 - y, c), (x, y, 1 - c)
        cps = []
        for w, (_, kind, g, r, cc, _, _) in enumerate(self.mats):
            hr, qr = r // 2, r // 4
            at = lambda j, h, q, n: _whole_rows(kind, news[w], j, h * hr + q * qr, n, r, cc)
            mine = lambda q: srcs[w].at[:, pl.ds(c * hr + q * qr, qr), :]
            cp = lambda k, s, d, dev: _remote(s, d, send.at[self.NCP * w + k], recv.at[self.NCP * w + k], dev)
            cps.append([
                cp(0, mine(0), at(me, c, 0, qr), nbx), cp(1, mine(1), at(me, c, 1, qr), nbx),
                cp(2, mine(1), at(me, c, 1, qr), nby), cp(3, mine(0), at(me, c, 0, qr), nby),
                cp(4, at(jx, c, 0, qr), at(jx, c, 0, qr), nby), cp(5, at(jy, c, 1, qr), at(jy, c, 1, qr), nbx),
                cp(6, at(jx, c, 0, qr), at(jx, c, 0, qr), sib), cp(7, at(jx, c, 1, qr), at(jx, c, 1, qr), sib),
                cp(8, at(jy, c, 1, qr), at(jy, c, 1, qr), sib), cp(9, at(jy, c, 0, qr), at(jy, c, 0, qr), sib),
                cp(10, at(jd, c, 0, qr), at(jd, c, 0, qr), sib), cp(11, at(jd, c, 1, qr), at(jd, c, 1, qr), sib),
                cp(12, srcs[w], _whole_shard(kind, news[w], me, r, cc), sib)])
        return cps

    def _pass_on(self, srcs, news, sems, pairs):
        cps = self._copies(srcs, news, sems)
        for w in range(len(self.mats)):
            for arrived, onward in pairs:
                cps[w][arrived].wait_recv()
                for k in onward:
                    cps[w][k].start()

    def start(self, srcs, dsts, news, sems):
        cps = self._copies(srcs, news, sems)
        for k in (0, 2, 1, 3, 12):
            for w in range(len(self.mats)):
                cps[w][k].start()

    def mid(self, srcs, dsts, news, sems):
        self._pass_on(srcs, news, sems, ((0, (4, 6)), (2, (5, 8))))

    def late(self, srcs, dsts, news, sems):
        self._pass_on(srcs, news, sems, ((1, (7,)), (3, (9,))))

    def finish(self, srcs, dsts, news, sems):
        self._pass_on(srcs, news, sems, ((4, (10,)), (5, (11,))))
        cps = self._copies(srcs, news, sems)
        for w in range(len(self.mats)):
            for k in (6, 7, 8, 9, 10, 11, 12):
                cps[w][k].wait_recv()
            for k in range(self.NCP):
                cps[w][k].wait_send()

    def done(self, dsts, news):
        for n, a in zip(self.names, news):
            self.sink[n] = a


class _SwapJob(_Job):
    def __init__(self, build, ncopies, *, srcs=(), dsts=(), news=(), done=None):
        self.build, self.srcs, self.dsts, self.news, self._done = build, list(srcs), list(dsts), list(news), done
        self.scratch = [pltpu.SemaphoreType.DMA((ncopies,)), pltpu.SemaphoreType.DMA((ncopies,))]

    def start(self, srcs, dsts, news, sems):
        for cp in self.build(srcs, dsts, news, *sems):
            cp.start()

    def finish(self, srcs, dsts, news, sems):
        for cp in self.build(srcs, dsts, news, *sems):
            cp.wait()

    def done(self, dsts, news):
        if self._done is not None:
            self._done(dsts, news)


def _half_of_whole(kind, ref, h, r, c):
    if kind == "row":
        return ref.at[:, :, pl.ds(pl.multiple_of(h * (c // 2), 128), c // 2)]
    return ref.at[:, pl.ds(h * (r // 2), r // 2), :]


def _half_shape(kind, g, r, c):
    return {"row": (g, NCHIP * r, c // 2), "col": (g, r // 2, NCHIP * c), "stk": (NCHIP, r // 2, c)}[kind]


def _sub_shape(kind, r, c):
    return {"row": (1, r // 2, c // 2), "col": (1, r // 4, c), "stk": (1, r // 4, c)}[kind]


def _sub_of_half(kind, ref, j, p, r, c):
    sr = _sub_shape(kind, r, c)[1]
    if kind == "row":
        return ref.at[:, pl.ds(j * r + p * sr, sr), :]
    if kind == "col":
        return ref.at[:, pl.ds(p * sr, sr), _cols(j, c)]
    return ref.at[pl.ds(j, 1), pl.ds(p * sr, sr), :]


def _sub_tile(sr):
    return 256 if sr % 256 == 0 else sr


def _half_of_shard(kind, ref, h, r, c):
    if kind == "row":
        return ref.at[:, :, pl.ds(pl.multiple_of(h * (c // 2), 128), c // 2)]
    return ref.at[:, pl.ds(h * (r // 2), r // 2), :]


def _pair_sum(pack, core, mine, got, whole=True):
    name, kind, g, r, c, tr, _ = pack
    hs = _half_shape(kind, g, r, c)
    nb = hs[1] // tr

    def body(core_ref, a_ref, b_ref, o_ref):
        o_ref[...] = (a_ref[...].astype(F32) + b_ref[...].astype(F32)).astype(BF16)

    blk = (1, tr, hs[2])
    same = lambda gi, i, core_ref: (gi, i, 0)
    if not whole:
        a_map = same
    elif kind == "row":
        a_map = lambda gi, i, core_ref: (gi, i, core_ref[0])
    else:
        a_map = lambda gi, i, core_ref: (gi, core_ref[0] * nb + i, 0)
    return pl.pallas_call(
        body, name="pair_sum_" + name, out_shape=jax.ShapeDtypeStruct(hs, BF16),
        grid_spec=pltpu.PrefetchScalarGridSpec(
            num_scalar_prefetch=1, grid=(hs[0], nb),
            in_specs=[pl.BlockSpec(blk, a_map), pl.BlockSpec(blk, same)], out_specs=pl.BlockSpec(blk, same)),
        compiler_params=_cp(("parallel", "parallel")),
    )(core, mine, got)


def _sub_sums(pack, idx, half, got, *, name):
    _, kind, g, r, c, _, _ = pack
    _, sr, sc = _sub_shape(kind, r, c)
    tr = _sub_tile(sr)
    nb = sr // tr

    def body(idx_ref, a_ref, ga_ref, b_ref, gb_ref, k_ref, p_ref):
        k_ref[0, 0] = a_ref[0].astype(F32) + ga_ref[0, 0].astype(F32)
        p_ref[0, 0] = (b_ref[0].astype(F32) + gb_ref[0, 0].astype(F32)).astype(BF16)

    def sub_map(o):
        if kind == "row":
            return lambda q, i, ix: (0, ix[4 * q + o] * (r // tr) + ix[4 * q + o + 1] * nb + i, 0)
        if kind == "col":
            return lambda q, i, ix: (0, ix[4 * q + o + 1] * nb + i, ix[4 * q + o])
        return lambda q, i, ix: (ix[4 * q + o], ix[4 * q + o + 1] * nb + i, 0)

    sub = lambda o: pl.BlockSpec((1, tr, sc), sub_map(o))
    got_blk = lambda o: pl.BlockSpec((1, 1, tr, sc), lambda q, i, ix: (2 * q + o, 0, i, 0))
    out_blk = pl.BlockSpec((1, 1, tr, sc), lambda q, i, ix: (q, 0, i, 0))
    return pl.pallas_call(
        body, name=name,
        out_shape=(jax.ShapeDtypeStruct((2, 1, sr, sc), F32), jax.ShapeDtypeStruct((2, 1, sr, sc), BF16)),
        grid_spec=pltpu.PrefetchScalarGridSpec(
            num_scalar_prefetch=1, grid=(2, nb), in_specs=[sub(0), got_blk(0), sub(2), got_blk(1)],
            out_specs=(out_blk, out_blk)),
        compiler_params=_cp(("parallel", "parallel")),
    )(idx, half, got, half, got)


def _shard_sum(pack, core, keep, got):
    name, kind, g, r, c, _, _ = pack
    _, sr, sc = _sub_shape(kind, r, c)
    tr = _sub_tile(sr)
    nb = sr // tr

    def body(core_ref, a_ref, b_ref, o_ref):
        o_ref[0] = a_ref[0, 0] + b_ref[0, 0].astype(F32)

    blk = pl.BlockSpec((1, 1, tr, sc), lambda p, i, cr: (p, 0, i, 0))
    if kind == "row":
        o_map = lambda p, i, cr: (0, p * nb + i, cr[0])
    else:
        o_map = lambda p, i, cr: (0, cr[0] * 2 * nb + p * nb + i, 0)
    return pl.pallas_call(
        body, name="shard_sum_" + name, out_shape=jax.ShapeDtypeStruct((g, r, c), F32),
        grid_spec=pltpu.PrefetchScalarGridSpec(
            num_scalar_prefetch=1, grid=(2, nb), in_specs=[blk, blk], out_specs=pl.BlockSpec((1, tr, sc), o_map)),
        compiler_params=_cp(("parallel", "parallel")),
    )(core, keep, got)


class _Plan:
    def __init__(self, shards, table):
        self.shards, self.table = shards, table
        self.whole, self.grad, self.got_a, self.half, self.gshard = {}, {}, {}, {}, {}
        self.got_b1, self.kept, self.pass_on, self.got_b2 = {}, {}, {}, {}
        x, y, c = _pos()
        me, jx, jy = 2 * x + y, 2 * (1 - x) + y, 2 * x + (1 - y)
        self.core = c
        self.core1 = c.reshape(1).astype(jnp.int32)
        zero = 0 * me
        self.idx_sums = jnp.stack([me, zero, jy, zero, me, zero + 1, jx, zero + 1]).astype(jnp.int32)
        self._w_in = None
        self.send, self.keep = {}, {}

    def w(self, n):
        if n != "w_in":
            return self.whole[n][0]
        if self._w_in is None:
            self._w_in = _slabs_to_kernel_cols(self.whole[n], name="relayout_w_in", jobs=self.jobs("relayout_w_in"))
        return self._w_in

    def g(self, n, a):
        self.grad[n] = a[None]

    def g_half(self, n, which, a):
        if which == "keep" and n in self.got_a:
            self.half[n] = _kernel_cols_to_slabs(a, name="relayout_d_in_keep", add=self.got_a[n])
        else:
            (self.send if which == "send" else self.keep)[n] = _kernel_cols_to_slabs(a, name="relayout_d_in_" + which)

    def jobs(self, tag):
        out = []
        for spec in self.table.get(tag, ()):
            out += getattr(self, "_" + spec[0])(*spec[1:])
        return out

    def run(self, name, jobs):
        if jobs:
            _call(lambda: None, jobs=jobs, name=name, out_shape=[], in_specs=[], out_specs=[])()

    def _gather(self, names):
        return [_GatherJob(names, self.shards, self.whole)]

    def _rs_a(self, names):
        mats = [MATS[n] for n in names]

        def build(srcs, dsts, news, send, recv):
            x, y, c = _pos()
            return [_remote(srcs[i] if names[i] in self.send else _half_of_whole(kind, srcs[i], 1 - c, r, cc), news[i],
                            send.at[i], recv.at[i], (x, y, 1 - c))
                    for i, (_, kind, g, r, cc, _, _) in enumerate(mats)]

        def done(dsts, news):
            self.got_a.update(zip(names, news))

        return [_SwapJob(build, len(names), srcs=[self.send.get(n, self.grad.get(n)) for n in names], done=done,
                         news=[jax.ShapeDtypeStruct(_half_shape(kind, g, r, c), BF16) for _, kind, g, r, c, _, _ in mats])]

    def _rs_b1(self, names):
        mats = [MATS[n] for n in names]
        for n in names:
            if n in self.half:
                continue
            if n in self.keep:
                self.half[n] = _pair_sum(MATS[n], self.core1, self.keep[n], self.got_a[n], whole=False)
            else:
                self.half[n] = _pair_sum(MATS[n], self.core1, self.grad[n], self.got_a[n])

        def build(srcs, dsts, news, send, recv):
            x, y, c = _pos()
            jx, jy, jd = 2 * (1 - x) + y, 2 * x + (1 - y), 2 * (1 - x) + (1 - y)
            nbx, nby = (1 - x, y, c), (x, 1 - y, c)
            cps = []
            for i, (_, kind, g, r, cc, _, _) in enumerate(mats):
                sub = lambda j, p: _sub_of_half(kind, srcs[i], j, p, r, cc)
                for k, (j, p, dev) in enumerate(((jx, 0, nbx), (jd, 0, nbx), (jy, 1, nby), (jd, 1, nby))):
                    cps.append(_remote(sub(j, p), news[i].at[k], send.at[4 * i + k], recv.at[4 * i + k], dev))
            return cps

        def done(dsts, news):
            self.got_b1.update(zip(names, news))

        return [_SwapJob(build, 4 * len(names), srcs=[self.half[n] for n in names], done=done,
                         news=[jax.ShapeDtypeStruct((4,) + _sub_shape(kind, r, c), BF16) for _, kind, g, r, c, _, _ in mats])]

    def _rs_b2(self, names):
        mats = [MATS[n] for n in names]
        for n in names:
            self.kept[n], self.pass_on[n] = _sub_sums(MATS[n], self.idx_sums, self.half[n], self.got_b1[n], name="sums_" + n)

        def build(srcs, dsts, news, send, recv):
            x, y, c = _pos()
            cps = []
            for i in range(len(mats)):
                cps.append(_remote(srcs[i].at[0], news[i].at[0], send.at[2 * i], recv.at[2 * i], (x, 1 - y, c)))
                cps.append(_remote(srcs[i].at[1], news[i].at[1], send.at[2 * i + 1], recv.at[2 * i + 1], (1 - x, y, c)))
            return cps

        def done(dsts, news):
            self.got_b2.update(zip(names, news))

        return [_SwapJob(build, 2 * len(names), srcs=[self.pass_on[n] for n in names], done=done,
                         news=[jax.ShapeDtypeStruct((2,) + _sub_shape(kind, r, c), BF16) for _, kind, g, r, c, _, _ in mats])]

    def _rs_c(self, names):
        mats = [MATS[n] for n in names]
        parts = [_shard_sum(MATS[n], self.core1, self.kept[n], self.got_b2[n]) for n in names]

        def build(srcs, dsts, news, send, recv):
            x, y, c = _pos()
            cps = []
            for i, (_, kind, g, r, cc, _, _) in enumerate(mats):
                mine = _half_of_shard(kind, dsts[i], c, r, cc)
                cps.append(_remote(mine, mine, send.at[i], recv.at[i], (x, y, 1 - c)))
            return cps

        def done(dsts, news):
            self.gshard.update(zip(names, dsts))

        return [_SwapJob(build, len(names), dsts=parts, done=done)]

    def finish(self, n):
        if n not in self.got_a:
            self.run("rs_a_" + n, self._rs_a((n,)))
        if n not in self.got_b1:
            self.run("rs_b1_" + n, self._rs_b1((n,)))
        if n not in self.got_b2:
            self.run("rs_b2_" + n, self._rs_b2((n,)))
        if n not in self.gshard:
            self.run("rs_c_" + n, self._rs_c((n,)))
        return self.gshard[n]


TABLE = {
    "gather_w_in": (("gather", ("w_in",)),),
    "relayout_w_in": (("gather", ("w_gate",)),),
    "mm_in": (("gather", ("w_up",)),),
    "attn_fwd": (("gather", ("w_attn_br", "w_ssd_br")),),
    "ssd_fwd": (("gather", ("w_o",)),),
    "swiglu_fwd": (("gather", ("w_down",)),),
    "mm_down": (("gather", ("w_ple_gate", "w_ple_proj")),),
    "mm_d_down": (("rs_a", ("w_ple_proj", "w_ple_gate")),),
    "swiglu_bwd": (("rs_a", ("w_down",)), ("rs_b1", ("w_ple_proj", "w_ple_gate"))),
    "mm_d_gate": (("rs_b1", ("w_down",)), ("rs_b2", ("w_ple_proj", "w_ple_gate"))),
    "mm_d_up": (("rs_b2", ("w_down",)), ("rs_a", ("w_gate",))),
    "mm_df_gate": (("rs_b1", ("w_gate",)), ("rs_a", ("w_up",)), ("rs_c", ("w_down", "w_ple_proj", "w_ple_gate"))),
    "mm_df_up": (("rs_b2", ("w_gate",)),),
    "attn_bwd": (("rs_b1", ("w_up",)), ("rs_a", ("w_o", "w_attn_br", "w_ssd_br")), ("rs_c", ("w_gate",))),
    "ssd_bwd": (("rs_b1", ("w_o", "w_attn_br", "w_ssd_br")), ("rs_b2", ("w_up",))),
    "conv_bwd": (("rs_b2", ("w_o", "w_attn_br", "w_ssd_br")), ("rs_c", ("w_up",))),
    "mm_d_in_keep": (("rs_a", ("w_in",)), ("rs_c", ("w_o", "w_attn_br", "w_ssd_br"))),
    "mm_du": (("rs_b1", ("w_in",)),),
    "norm_mix_bwd": (("rs_b2", ("w_in",)),),
}


NDEV = 8


def _allreduce_small(v, *, name):
    rows = v.shape[0]

    def body(v_ref, o_ref, slots, send, recv):
        x, y, c = _pos()
        me = 4 * x + 2 * y + c
        slots[me] = v_ref[...]
        cps = []
        for k in range(1, NDEV):
            peer = (_flip(x, k & 4), _flip(y, k & 2), _flip(c, k & 1))
            cp = _remote(v_ref, slots.at[me], send.at[k - 1], recv.at[k - 1], peer)
            cp.start()
            cps.append(cp)
        for cp in cps:
            cp.wait()
        acc = slots[0]
        for s in range(1, NDEV):
            acc = acc + slots[s]
        o_ref[...] = acc

    return pl.pallas_call(
        body, name=name, out_shape=jax.ShapeDtypeStruct((rows, 128), F32),
        in_specs=[pl.BlockSpec(memory_space=pltpu.VMEM)], out_specs=pl.BlockSpec(memory_space=pltpu.VMEM),
        scratch_shapes=[pltpu.VMEM((NDEV, rows, 128), F32), pltpu.SemaphoreType.DMA((NDEV - 1,)),
                        pltpu.SemaphoreType.DMA((NDEV - 1,))],
    )(v)


def _adamw(w, g, m, v, *, name, tr=None, tc=None, jobs=()):
    r, c = w.shape
    tr = r if tr is None else tr
    c1 = 1.0 / (1.0 - B1 ** STEP)
    c2 = 1.0 / (1.0 - B2 ** STEP)

    def body(w_ref, g_ref, m_ref, v_ref, d_ref, mo_ref, vo_ref):
        gv = g_ref[...]
        mn = B1 * m_ref[...] + (1.0 - B1) * gv
        vn = B2 * v_ref[...] + (1.0 - B2) * (gv * gv)
        mo_ref[...] = mn
        vo_ref[...] = vn
        d_ref[...] = -LR * ((mn * c1) / (jnp.sqrt(vn * c2) + AEPS) + WD * w_ref[...])

    if tc is None:
        blk, grid = pl.BlockSpec((tr, c), lambda i: (i, 0)), (r // tr,)
    else:
        blk, grid = pl.BlockSpec((r, tc), lambda i: (0, i)), (c // tc,)
    o = jax.ShapeDtypeStruct((r, c), F32)
    return _call(
        body, jobs=jobs, name=name, out_shape=(o, o, o), grid=grid, in_specs=[blk] * 4, out_specs=(blk, blk, blk),
        compiler_params=_cp(("parallel",)),
    )(w, g, m, v)


WEIGHTS = ("g_mix", "w_in", "conv_w", "conv_b", "dt_bias", "a_log", "d_skip", "g_ssd", "sinks", "w_attn_br", "w_ssd_br",
           "w_o", "g_ffn", "w_gate", "w_up", "w_down", "g_ple", "w_ple_gate", "w_ple_proj", "g_final")
BIG = {
    "w_gate": 256, "w_up": 256, "w_down": 128, "w_ssd_br": 128, "w_o": 128, "w_ple_gate": 128, "w_attn_br": 256,
    "w_ple_proj": 256, "w_in": None,
}
SMALL = tuple(n for n in WEIGHTS if n not in BIG)


def _pack_small(parts):
    rows = []
    for a in parts:
        a = a.reshape(-1)
        rows.append(jnp.pad(a, (0, -a.shape[0] % 128)).reshape(-1, 128))
    out = jnp.concatenate(rows, axis=0)
    return jnp.pad(out, ((0, -out.shape[0] % 8), (0, 0)))


def _unpack_small(packed, shapes):
    out, r = [], 0
    for s in shapes:
        n = int(np.prod(s))
        nr = -(-n // 128)
        out.append(packed[r:r + nr].reshape(-1)[:n].reshape(s))
        r += nr
    return out


def kernel(x, p, positions, g_mix, w_in, conv_w, conv_b, dt_bias, a_log, d_skip, g_ssd, sinks, w_attn_br, w_ssd_br, w_o, g_ffn, w_gate, w_up, w_down, g_ple, w_ple_gate, w_ple_proj, g_final, loss_target, m_g_mix, m_w_in, m_conv_w, m_conv_b, m_dt_bias, m_a_log, m_d_skip, m_g_ssd, m_sinks, m_w_attn_br, m_w_ssd_br, m_w_o, m_g_ffn, m_w_gate, m_w_up, m_w_down, m_g_ple, m_w_ple_gate, m_w_ple_proj, m_g_final, v_g_mix, v_w_in, v_conv_w, v_conv_b, v_dt_bias, v_a_log, v_d_skip, v_g_ssd, v_sinks, v_w_attn_br, v_w_ssd_br, v_w_o, v_g_ffn, v_w_gate, v_w_up, v_w_down, v_g_ple, v_w_ple_gate, v_w_ple_proj, v_g_final):
    w = dict(zip(WEIGHTS, (g_mix, w_in, conv_w, conv_b, dt_bias, a_log, d_skip, g_ssd, sinks, w_attn_br, w_ssd_br, w_o,
                           g_ffn, w_gate, w_up, w_down, g_ple, w_ple_gate, w_ple_proj, g_final)))
    m = dict(zip(WEIGHTS, (m_g_mix, m_w_in, m_conv_w, m_conv_b, m_dt_bias, m_a_log, m_d_skip, m_g_ssd, m_sinks, m_w_attn_br,
                           m_w_ssd_br, m_w_o, m_g_ffn, m_w_gate, m_w_up, m_w_down, m_g_ple, m_w_ple_gate, m_w_ple_proj,
                           m_g_final)))
    v = dict(zip(WEIGHTS, (v_g_mix, v_w_in, v_conv_w, v_conv_b, v_dt_bias, v_a_log, v_d_skip, v_g_ssd, v_sinks, v_w_attn_br,
                           v_w_ssd_br, v_w_o, v_g_ffn, v_w_gate, v_w_up, v_w_down, v_g_ple, v_w_ple_gate, v_w_ple_proj,
                           v_g_final)))
    xi, yi, ci = _pos()
    chip = 2 * xi + yi
    t = x.shape[1]
    cshard = CONV // NCHIP

    shards = {n: w[n].astype(BF16) for n in MATS}
    shards["w_in"] = jnp.pad(shards["w_in"], ((0, 0), (0, 0), (0, SLAB_PAD - SLAB)))
    plan = _Plan(shards, TABLE)
    plan.run("gather_w_in", plan.jobs("gather_w_in"))
    placed = lax.dynamic_update_slice(jnp.zeros((CW, CONV), F32), w["conv_w"][0], (0, chip * cshard))
    conv_whole = _allreduce_small(jnp.where(ci == 0, placed, 0.0).reshape(-1, 128), name="gather_conv_w").reshape(CW, CONV)

    small = {n: w[n] for n in ("g_mix", "conv_b", "dt_bias", "a_log", "d_skip", "g_ssd", "sinks", "g_ffn", "g_ple", "g_final")}
    small["conv_w"] = conv_whole
    loss8, grad_x, gs = _local_step(x[0], p[0, 0], positions, loss_target[0], small, plan)

    order = ("g_mix", "conv_b", "dt_bias", "a_log", "d_skip", "g_ssd", "sinks", "g_ffn", "g_ple", "g_final", "conv_w")
    summed = _allreduce_small(_pack_small([loss8[0, :1]] + [gs[n] for n in order]), name="sum_small")
    parts = _unpack_small(summed, [(1,)] + [w[n].shape for n in order[:-1]] + [(CW, CONV)])
    loss = parts[0][0]
    grad = dict(zip(order, parts[1:]))
    grad["conv_w"] = lax.dynamic_slice(grad["conv_w"], (0, chip * cshard), (CW, cshard))[None]

    delta, new_m, new_v = {}, {}, {}
    for n, tr in BIG.items():
        grad[n] = plan.finish(n)[:, :, :w[n].shape[2]]
        if n == "w_in":
            d_, m_, v_ = _adamw(w[n][0].T, grad[n][0].T, m[n][0].T, v[n][0].T, tc=128, name="adamw_" + n)
            d_, m_, v_ = d_.T, m_.T, v_.T
        else:
            d_, m_, v_ = _adamw(w[n][0], grad[n][0], m[n][0], v[n][0], tr=tr, name="adamw_" + n)
        delta[n], new_m[n], new_v[n] = d_[None], m_[None], v_[None]
    shapes = [w[n].shape for n in SMALL]
    d_, m_, v_ = _adamw(_pack_small([w[n] for n in SMALL]), _pack_small([grad[n] for n in SMALL]),
                        _pack_small([m[n] for n in SMALL]), _pack_small([v[n] for n in SMALL]), tr=None, name="adamw_small")
    for n, a, b, c_ in zip(SMALL, _unpack_small(d_, shapes), _unpack_small(m_, shapes), _unpack_small(v_, shapes)):
        delta[n], new_m[n], new_v[n] = a, b, c_

    return (loss, grad_x[None], *[grad[n] for n in WEIGHTS], *[delta[n] for n in WEIGHTS],
            *[new_m[n] for n in WEIGHTS], *[new_v[n] for n in WEIGHTS])
```

```python
import functools

import jax
import jax.numpy as jnp
import numpy as np
from jax import lax
from jax.experimental import pallas as pl
from jax.experimental.pallas import tpu as pltpu

F32 = jnp.float32
BF16 = jnp.bfloat16
MESH = pl.DeviceIdType.MESH

D = 2048
HD = 64
NQH = 16
NKV = 4
QD = NQH * HD
KVD = NKV * HD
DI = 2048
NH = 32
NG = 4
NS = 128
CW = 4
L = 128
CONV = DI + 2 * NG * NS
FFN = 5632
PLE = 256
IN_DIM = QD + 2 * KVD + DI + CONV + NH + 2 * D
EPS = 1e-6
SSM_EPS = 1e-5
ROPE_THETA = 10000.0
LR, B1, B2, AEPS, WD, STEP = 0.001, 0.9, 0.999, 1e-08, 0.01, 10

O_GA, O_GS, O_Z, O_XBC, O_Q, O_K, O_V, O_DT = 0, 2048, 4096, 6144, 9216, 10240, 10496, 10752
DT_PAD = 512
NP = O_DT + DT_PAD
R_Q, R_K, R_V, R_Z, R_XBC, R_DT, R_GA, R_GS = 0, 1024, 1280, 1536, 3584, 6656, 6688, 8736

NCHIP = 4
VMEM_LIMIT = 52 * 1024 * 1024
NEG = -1e30


def _cp(sem=None):
    return pltpu.CompilerParams(dimension_semantics=sem, vmem_limit_bytes=VMEM_LIMIT)


def _dot(a, b):
    return lax.dot_general(a, b, (((1,), (0,)), ((), ())), preferred_element_type=F32)


def _dot_nt(a, b):
    return lax.dot_general(a, b, (((1,), (1,)), ((), ())), preferred_element_type=F32)


def _dot_tn(a, b):
    return lax.dot_general(a, b, (((0,), (0,)), ((), ())), preferred_element_type=F32)


def _sigmoid(x):
    return 1.0 / (1.0 + jnp.exp(-x))


def _bf16_dot(dot, da, db):
    @jax.custom_vjp
    def f(a, b):
        return dot(a.astype(BF16), b.astype(BF16))

    def fwd(a, b):
        return f(a, b), (a.astype(BF16), b.astype(BF16))

    def bwd(res, g):
        a, b = res
        g = g.astype(BF16)
        return da(g, a, b), db(g, a, b)

    f.defvjp(fwd, bwd)
    return f


_bdot = _bf16_dot(_dot, lambda g, a, b: _dot_nt(g, b), lambda g, a, b: _dot_tn(a, g))
_bdot_nt = _bf16_dot(_dot_nt, lambda g, a, b: _dot(g, b), lambda g, a, b: _dot_tn(g, a))
_bdot_tn = _bf16_dot(_dot_tn, lambda g, a, b: _dot_nt(b, g), lambda g, a, b: _dot(a, g))


ANY = pl.BlockSpec(memory_space=pl.ANY)


class _Job:
    srcs, dsts, news, scratch = (), (), (), ()
    has_mid = False

    def start(self, srcs, dsts, news, sems):
        raise NotImplementedError

    def mid(self, srcs, dsts, news, sems):
        pass

    def late(self, srcs, dsts, news, sems):
        pass

    def finish(self, srcs, dsts, news, sems):
        raise NotImplementedError

    def done(self, dsts, news):
        pass


def _call(body, *, jobs=(), name, out_shape, in_specs, out_specs, grid=(), scratch_shapes=(), compiler_params=None,
          aliases=None):
    jobs = [j for j in jobs if j is not None]
    aliases = dict(aliases or {})
    if not jobs:
        return pl.pallas_call(body, name=name, out_shape=out_shape, in_specs=in_specs, out_specs=out_specs, grid=grid,
                              scratch_shapes=scratch_shapes, compiler_params=compiler_params,
                              input_output_aliases=aliases)
    single = not isinstance(out_shape, (tuple, list))
    outs = [out_shape] if single else list(out_shape)
    ospecs = [out_specs] if single else list(out_specs)
    n_in, n_out, n_scr = len(in_specs), len(outs), len(scratch_shapes)
    srcs = [a for j in jobs for a in j.srcs]
    dsts = [a for j in jobs for a in j.dsts]
    news = [a for j in jobs for a in j.news]
    sems = [a for j in jobs for a in j.scratch]

    def wrapped(*refs):
        pos = n_in + len(srcs) + len(dsts)
        ins, jsrc = refs[:n_in], refs[n_in:n_in + len(srcs)]
        o_refs = refs[pos:pos + n_out]
        pos += n_out
        jdst, jnew = refs[pos:pos + len(dsts)], refs[pos + len(dsts):pos + len(dsts) + len(news)]
        pos += len(dsts) + len(news)
        scr, jsem = refs[pos:pos + n_scr], refs[pos + n_scr:]

        def run(which):
            a = b = c = d = 0
            for j in jobs:
                getattr(j, which)(jsrc[a:a + len(j.srcs)], jdst[b:b + len(j.dsts)], jnew[c:c + len(j.news)],
                                  jsem[d:d + len(j.scratch)])
                a, b, c, d = a + len(j.srcs), b + len(j.dsts), c + len(j.news), d + len(j.scratch)

        if not grid:
            run("start")
            run("mid")
            run("late")
            body(*ins, *o_refs, *scr)
            run("finish")
            return
        step = functools.reduce(lambda acc, a: acc * grid[a] + pl.program_id(a), range(len(grid)), 0)
        steps = int(np.prod(grid))
        pl.when(step == 0)(lambda: run("start"))
        if any(j.has_mid for j in jobs):
            pl.when(step == steps // 3)(lambda: run("mid"))
            pl.when(step == (2 * steps) // 3)(lambda: run("late"))
        body(*ins, *o_refs, *scr)
        pl.when(step == steps - 1)(lambda: run("finish"))

    call = pl.pallas_call(
        wrapped, name=name,
        out_shape=outs + [jax.ShapeDtypeStruct(a.shape, a.dtype) for a in dsts] + news,
        in_specs=list(in_specs) + [ANY] * (len(srcs) + len(dsts)),
        out_specs=ospecs + [ANY] * (len(dsts) + len(news)),
        grid=grid, scratch_shapes=list(scratch_shapes) + sems,
        input_output_aliases={**aliases, **{n_in + len(srcs) + i: n_out + i for i in range(len(dsts))}},
        compiler_params=_cp(("arbitrary",) * len(grid) if grid else None))

    def run_call(*args):
        res = call(*args, *srcs, *dsts)
        b, c = n_out, n_out + len(dsts)
        for j in jobs:
            j.done(res[b:b + len(j.dsts)], res[c:c + len(j.news)])
            b, c = b + len(j.dsts), c + len(j.news)
        return res[0] if single else tuple(res[:n_out])

    return run_call


def _matmul(a, b, *, ta=False, tb=False, out_dtype=F32, add=None, tm, tn, tk, name, jobs=()):
    k, m = a.shape if ta else a.shape[::-1]
    n = b.shape[0] if tb else b.shape[1]
    assert (b.shape[1] if tb else b.shape[0]) == k and not (ta and tb)
    assert m % tm == 0 and n % tn == 0 and k % tk == 0, (name, a.shape, b.shape)
    nk = k // tk
    has_add = add is not None

    def body(*refs):
        a_ref, b_ref = refs[0], refs[1]
        add_ref = refs[2] if has_add else None
        o_ref = refs[3] if has_add else refs[2]
        av = a_ref[...].astype(BF16)
        bv = b_ref[...].astype(BF16)
        part = _dot_tn(av, bv) if ta else _dot_nt(av, bv) if tb else _dot(av, bv)

        def finish(r):
            if has_add:
                r = r + add_ref[...]
            o_ref[...] = r.astype(out_dtype)

        if nk == 1:
            finish(part)
        elif out_dtype == F32:
            kk = pl.program_id(2)
            pl.when(kk == 0)(lambda: finish(part))

            @pl.when(kk > 0)
            def _():
                o_ref[...] += part
        else:
            acc_ref = refs[-1]
            kk = pl.program_id(2)

            @pl.when(kk == 0)
            def _():
                acc_ref[...] = part

            @pl.when(kk > 0)
            def _():
                acc_ref[...] += part

            @pl.when(kk == nk - 1)
            def _():
                finish(acc_ref[...])

    in_specs = [pl.BlockSpec((tk, tm), lambda i, j, kk: (kk, i)) if ta else pl.BlockSpec((tm, tk), lambda i, j, kk: (i, kk)),
                pl.BlockSpec((tn, tk), lambda i, j, kk: (j, kk)) if tb
                else pl.BlockSpec((tk, tn), lambda i, j, kk: (kk, j))]
    args = [a, b]
    if has_add:
        in_specs.append(pl.BlockSpec((tm, tn), lambda i, j, kk: (i, j)))
        args.append(add)
    return _call(
        body, jobs=jobs, name=name,
        out_shape=jax.ShapeDtypeStruct((m, n), out_dtype),
        grid=(m // tm, n // tn, nk),
        in_specs=in_specs,
        out_specs=pl.BlockSpec((tm, tn), lambda i, j, kk: (i, j)),
        scratch_shapes=[pltpu.VMEM((tm, tn), F32)] if nk > 1 and out_dtype != F32 else [],
        compiler_params=_cp(("parallel", "parallel", "arbitrary")),
    )(*args)


ROWS = 256


def _rmsnorm_fwd(x, g, *, name):
    t, d = x.shape

    def body(x_ref, g_ref, o_ref):
        xv = x_ref[...]
        r = lax.rsqrt(jnp.mean(xv * xv, axis=-1, keepdims=True) + EPS)
        o_ref[...] = (xv * r * g_ref[...]).astype(BF16)

    return pl.pallas_call(
        body, name=name, out_shape=jax.ShapeDtypeStruct((t, d), BF16), grid=(t // ROWS,),
        in_specs=[pl.BlockSpec((ROWS, d), lambda i: (i, 0)), pl.BlockSpec((1, d), lambda i: (0, 0))],
        out_specs=pl.BlockSpec((ROWS, d), lambda i: (i, 0)), compiler_params=_cp(("parallel",)),
    )(x, g)


def _rmsnorm_bwd(x, g, dy, dres, *, name, jobs=()):
    t, d = x.shape

    def body(x_ref, g_ref, dy_ref, dres_ref, dx_ref, dxb_ref, dg_ref):
        xv = x_ref[...]
        r = lax.rsqrt(jnp.mean(xv * xv, axis=-1, keepdims=True) + EPS)
        xh = xv * r
        dyv = dy_ref[...]
        dxh = dyv * g_ref[...]
        dx = r * (dxh - xh * jnp.mean(dxh * xh, axis=-1, keepdims=True))
        tot = dres_ref[...] + dx
        dx_ref[...] = tot
        dxb_ref[...] = tot.astype(BF16)

        @pl.when(pl.program_id(0) == 0)
        def _():
            dg_ref[...] = jnp.zeros_like(dg_ref)

        dg_ref[...] += jnp.broadcast_to(jnp.sum(dyv * xh, axis=0, keepdims=True), dg_ref.shape)

    row = pl.BlockSpec((ROWS, d), lambda i: (i, 0))
    return _call(
        body, jobs=jobs, name=name,
        out_shape=(jax.ShapeDtypeStruct((t, d), F32), jax.ShapeDtypeStruct((t, d), BF16),
                   jax.ShapeDtypeStruct((8, d), F32)),
        grid=(t // ROWS,),
        in_specs=[row, pl.BlockSpec((1, d), lambda i: (0, 0)), row, row],
        out_specs=(row, row, pl.BlockSpec((8, d), lambda i: (0, 0))),
        compiler_params=_cp(("arbitrary",)),
    )(x, g, dy, dres)


def _final(h2, pgl, pp, target, g_final, *, name):
    t, d = h2.shape

    def body(h2_ref, pgl_ref, pp_ref, tg_ref, g_ref, dh3_ref, dpgl_ref, dpp_ref, loss_ref, dg_ref):
        s = _sigmoid(pgl_ref[...])
        ppv = pp_ref[...]
        h3 = h2_ref[...] + s * ppv
        r = lax.rsqrt(jnp.mean(h3 * h3, axis=-1, keepdims=True) + EPS)
        xh = h3 * r
        gv = g_ref[...]
        err = xh * gv - tg_ref[...]
        dyv = err * (1.0 / d)
        dxh = dyv * gv
        dh3 = r * (dxh - xh * jnp.mean(dxh * xh, axis=-1, keepdims=True))
        dh3_ref[...] = dh3
        dpp_ref[...] = (dh3 * s).astype(BF16)
        dpgl_ref[...] = (dh3 * ppv * s * (1.0 - s)).astype(BF16)

        @pl.when(pl.program_id(0) == 0)
        def _():
            loss_ref[...] = jnp.zeros_like(loss_ref)
            dg_ref[...] = jnp.zeros_like(dg_ref)

        part = 0.5 * jnp.sum(jnp.mean(err * err, axis=-1, keepdims=True), axis=0, keepdims=True)
        loss_ref[...] += jnp.broadcast_to(part, loss_ref.shape)
        dg_ref[...] += jnp.broadcast_to(jnp.sum(dyv * xh, axis=0, keepdims=True), dg_ref.shape)

    row = pl.BlockSpec((ROWS, d), lambda i: (i, 0))
    return pl.pallas_call(
        body, name=name,
        out_shape=(jax.ShapeDtypeStruct((t, d), F32), jax.ShapeDtypeStruct((t, d), BF16),
                   jax.ShapeDtypeStruct((t, d), BF16), jax.ShapeDtypeStruct((8, 128), F32),
                   jax.ShapeDtypeStruct((8, d), F32)),
        grid=(t // ROWS,),
        in_specs=[row, row, row, row, pl.BlockSpec((1, d), lambda i: (0, 0))],
        out_specs=(row, row, row, pl.BlockSpec((8, 128), lambda i: (0, 0)), pl.BlockSpec((8, d), lambda i: (0, 0))),
        compiler_params=_cp(("arbitrary",)),
    )(h2, pgl, pp, target, g_final)


def _merge_fwd(proj, out_a, out_s, *, name):
    t = proj.shape[0]

    def body(ga_ref, gs_ref, a_ref, s_ref, o_ref):
        o_ref[...] = (_sigmoid(ga_ref[...]) * a_ref[...] + _sigmoid(gs_ref[...]) * s_ref[...]).astype(BF16)

    row = pl.BlockSpec((ROWS, D), lambda i: (i, 0))
    return pl.pallas_call(
        body, name=name, out_shape=jax.ShapeDtypeStruct((t, D), BF16), grid=(t // ROWS,),
        in_specs=[pl.BlockSpec((ROWS, D), lambda i: (i, O_GA // D)), pl.BlockSpec((ROWS, D), lambda i: (i, O_GS // D)),
                  row, row],
        out_specs=row, compiler_params=_cp(("parallel",)),
    )(proj, proj, out_a, out_s)


def _merge_bwd(proj, out_a, out_s, dmerged, *, name):
    t = proj.shape[0]
    assert O_GA == 0 and O_GS == D

    def body(ga_ref, gs_ref, a_ref, s_ref, dm_ref, da_ref, ds_ref, dp_ref):
        sa = _sigmoid(ga_ref[...])
        ss = _sigmoid(gs_ref[...])
        dm = dm_ref[...]
        da_ref[...] = (dm * sa).astype(BF16)
        ds_ref[...] = (dm * ss).astype(BF16)
        dp_ref[:, :D] = (dm * a_ref[...] * sa * (1.0 - sa)).astype(BF16)
        dp_ref[:, D:] = (dm * s_ref[...] * ss * (1.0 - ss)).astype(BF16)

    row = pl.BlockSpec((ROWS, D), lambda i: (i, 0))
    o = jax.ShapeDtypeStruct((t, D), BF16)
    return pl.pallas_call(
        body, name=name, out_shape=(o, o, jax.ShapeDtypeStruct((t, NP), BF16)), grid=(t // ROWS,),
        in_specs=[pl.BlockSpec((ROWS, D), lambda i: (i, O_GA // D)), pl.BlockSpec((ROWS, D), lambda i: (i, O_GS // D)),
                  row, row, row],
        out_specs=(row, row, pl.BlockSpec((ROWS, 2 * D), lambda i: (i, 0))), compiler_params=_cp(("parallel",)),
    )(proj, proj, out_a, out_s, dmerged)


def _swiglu_fwd(f, w_gate, w_up, *, name, tn=512, jobs=()):
    t, d = f.shape
    n = w_gate.shape[1]

    def body(f_ref, wg_ref, wu_ref, g_ref, u_ref, a_ref):
        fv = f_ref[...]
        g = _dot(fv, wg_ref[...])
        u = _dot(fv, wu_ref[...])
        g_ref[...] = g.astype(BF16)
        u_ref[...] = u.astype(BF16)
        a_ref[...] = (g * _sigmoid(g) * u).astype(BF16)

    col = pl.BlockSpec((t, tn), lambda j: (0, j))
    wcol = pl.BlockSpec((d, tn), lambda j: (0, j))
    return _call(
        body, jobs=jobs, name=name,
        out_shape=(jax.ShapeDtypeStruct((t, n), BF16), jax.ShapeDtypeStruct((t, n), BF16),
                   jax.ShapeDtypeStruct((t, n), BF16)),
        grid=(n // tn,),
        in_specs=[pl.BlockSpec((t, d), lambda j: (0, 0)), wcol, wcol],
        out_specs=(col, col, col), compiler_params=_cp(("parallel",)),
    )(f, w_gate, w_up)


def _swiglu_bwd(dh, w_down, gate, up, *, name, tn=512, jobs=()):
    t, d = dh.shape
    n = w_down.shape[0]

    def body(dh_ref, w_ref, g_ref, u_ref, dg_ref, du_ref):
        da = _dot_nt(dh_ref[...], w_ref[...])
        g = g_ref[...].astype(F32)
        s = _sigmoid(g)
        du_ref[...] = (da * g * s).astype(BF16)
        dg_ref[...] = (da * u_ref[...].astype(F32) * s * (1.0 + g * (1.0 - s))).astype(BF16)

    col = pl.BlockSpec((t, tn), lambda j: (0, j))
    o = jax.ShapeDtypeStruct((t, n), BF16)
    return _call(
        body, jobs=jobs, name=name, out_shape=(o, o), grid=(n // tn,),
        in_specs=[pl.BlockSpec((t, d), lambda j: (0, 0)), pl.BlockSpec((tn, d), lambda j: (j, 0)), col, col],
        out_specs=(col, col), compiler_params=_cp(("parallel",)),
    )(dh, w_down, gate, up)


def _gated_norm_fwd(y_pre, proj, g_ssd, *, name):
    t = y_pre.shape[0]

    def body(y_ref, z_ref, g_ref, o_ref):
        z = z_ref[...]
        v = y_ref[...] * z * _sigmoid(z)
        r = lax.rsqrt(jnp.mean(v * v, axis=-1, keepdims=True) + SSM_EPS)
        o_ref[...] = (v * r * g_ref[...]).astype(BF16)

    row = pl.BlockSpec((ROWS, DI), lambda i: (i, 0))
    return pl.pallas_call(
        body, name=name, out_shape=jax.ShapeDtypeStruct((t, DI), BF16), grid=(t // ROWS,),
        in_specs=[row, pl.BlockSpec((ROWS, DI), lambda i: (i, O_Z // DI)), pl.BlockSpec((1, DI), lambda i: (0, 0))],
        out_specs=row, compiler_params=_cp(("parallel",)),
    )(y_pre, proj, g_ssd)


def _gated_norm_bwd(y_pre, proj, g_ssd, dyn, dproj, *, name, jobs=()):
    t = y_pre.shape[0]

    def body(y_ref, z_ref, g_ref, dyn_ref, _, dy_ref, dz_ref, dg_ref):
        z = z_ref[...]
        s = _sigmoid(z)
        sz = z * s
        yv = y_ref[...]
        v = yv * sz
        r = lax.rsqrt(jnp.mean(v * v, axis=-1, keepdims=True) + SSM_EPS)
        vh = v * r
        dn = dyn_ref[...]
        dvh = dn * g_ref[...]
        dv = r * (dvh - vh * jnp.mean(dvh * vh, axis=-1, keepdims=True))
        dy_ref[...] = dv * sz
        dz_ref[...] = (dv * yv * s * (1.0 + z * (1.0 - s))).astype(BF16)

        @pl.when(pl.program_id(0) == 0)
        def _():
            dg_ref[...] = jnp.zeros_like(dg_ref)

        dg_ref[...] += jnp.broadcast_to(jnp.sum(dn * vh, axis=0, keepdims=True), dg_ref.shape)

    row = pl.BlockSpec((ROWS, DI), lambda i: (i, 0))
    return _call(
        body, jobs=jobs, name=name,
        out_shape=(jax.ShapeDtypeStruct((t, DI), F32), jax.ShapeDtypeStruct(dproj.shape, BF16),
                   jax.ShapeDtypeStruct((8, DI), F32)),
        grid=(t // ROWS,),
        in_specs=[row, pl.BlockSpec((ROWS, DI), lambda i: (i, O_Z // DI)), pl.BlockSpec((1, DI), lambda i: (0, 0)), row, ANY],
        out_specs=(row, pl.BlockSpec((ROWS, DI), lambda i: (i, O_Z // DI)), pl.BlockSpec((8, DI), lambda i: (0, 0))),
        compiler_params=_cp(("arbitrary",)), aliases={4: 1},
    )(y_pre, proj, g_ssd, dyn, dproj)


CONV_TC = 512


def _shift_down(x, s, row):
    if s == 0:
        return x
    return jnp.where(row >= s, pltpu.roll(x, s, 0), 0.0)


def _shift_up(x, s, row, t):
    if s == 0:
        return x
    return jnp.where(row < t - s, pltpu.roll(x, t - s, 0), 0.0)


def _conv_fwd(proj, conv_w, conv_b, *, name):
    t = proj.shape[0]

    def body(x_ref, w_ref, b_ref, o_ref):
        x = x_ref[...]
        row = lax.broadcasted_iota(jnp.int32, x.shape, 0)
        pre = jnp.broadcast_to(b_ref[...], x.shape)
        for k in range(CW):
            pre = pre + w_ref[k:k + 1, :] * _shift_down(x, CW - 1 - k, row)
        o_ref[...] = pre * _sigmoid(pre)

    return pl.pallas_call(
        body, name=name, out_shape=jax.ShapeDtypeStruct((t, CONV), F32), grid=(CONV // CONV_TC,),
        in_specs=[pl.BlockSpec((t, CONV_TC), lambda j: (0, O_XBC // CONV_TC + j)),
                  pl.BlockSpec((CW, CONV_TC), lambda j: (0, j)), pl.BlockSpec((1, CONV_TC), lambda j: (0, j))],
        out_specs=pl.BlockSpec((t, CONV_TC), lambda j: (0, j)), compiler_params=_cp(("parallel",)),
    )(proj, conv_w, conv_b)


def _conv_bwd(proj, conv_w, conv_b, dxs, db, dc, dproj, *, name, jobs=()):
    t = proj.shape[0]
    nx = DI // CONV_TC
    assert NG * NS == CONV_TC

    def body(x_ref, w_ref, b_ref, dxs_ref, db_ref, dc_ref, _, dx_ref, dw_ref, dbias_ref):
        j = pl.program_id(0)
        x = x_ref[...]
        row = lax.broadcasted_iota(jnp.int32, x.shape, 0)
        xs = [_shift_down(x, CW - 1 - k, row) for k in range(CW)]
        pre = jnp.broadcast_to(b_ref[...], x.shape)
        for k in range(CW):
            pre = pre + w_ref[k:k + 1, :] * xs[k]
        s = _sigmoid(pre)
        da = jnp.where(j < nx, dxs_ref[...], jnp.where(j == nx, db_ref[...], dc_ref[...]))
        dpre = da * s * (1.0 + pre * (1.0 - s))
        dx = jnp.zeros_like(x)
        row8 = lax.broadcasted_iota(jnp.int32, dw_ref.shape, 0)
        dw = jnp.zeros(dw_ref.shape, F32)
        for k in range(CW):
            dx = dx + w_ref[k:k + 1, :] * _shift_up(dpre, CW - 1 - k, row, t)
            dw = dw + jnp.where(row8 == k, jnp.sum(dpre * xs[k], axis=0, keepdims=True), 0.0)
        dx_ref[...] = dx.astype(BF16)
        dw_ref[...] = dw
        dbias_ref[...] = jnp.broadcast_to(jnp.sum(dpre, axis=0, keepdims=True), dbias_ref.shape)

    col8 = pl.BlockSpec((8, CONV_TC), lambda j: (0, j))
    xbc = pl.BlockSpec((t, CONV_TC), lambda j: (0, O_XBC // CONV_TC + j))
    whole = pl.BlockSpec((t, CONV_TC), lambda j: (0, 0))
    return _call(
        body, jobs=jobs, name=name,
        out_shape=(jax.ShapeDtypeStruct(dproj.shape, BF16), jax.ShapeDtypeStruct((8, CONV), F32),
                   jax.ShapeDtypeStruct((8, CONV), F32)),
        grid=(CONV // CONV_TC,),
        in_specs=[xbc, pl.BlockSpec((CW, CONV_TC), lambda j: (0, j)), pl.BlockSpec((1, CONV_TC), lambda j: (0, j)),
                  pl.BlockSpec((t, CONV_TC), lambda j: (0, jnp.minimum(j, nx - 1))), whole, whole, ANY],
        out_specs=(xbc, col8, col8),
        compiler_params=_cp(("arbitrary",)), aliases={6: 0},
    )(proj, conv_w, conv_b, dxs, db, dc, dproj)


def _rope_tables(positions, t):
    half = HD // 2
    inv_freq = ROPE_THETA ** (-jnp.arange(half, dtype=F32) * 2.0 / HD)
    ang = positions.reshape(t).astype(F32)[:, None] * inv_freq
    cos, sin = jnp.cos(ang), jnp.sin(ang)
    return jnp.concatenate([cos] * 4, axis=1), jnp.concatenate([-sin, sin] * 2, axis=1)


def _lane_consts():
    lane = lax.broadcasted_iota(jnp.int32, (L, 128), 1)
    return lane, (lane % HD) < (HD // 2), lane < HD


def _rope(tv, cos, sin, lo):
    return tv * cos + jnp.where(lo, pltpu.roll(tv, 128 - HD // 2, 1), pltpu.roll(tv, HD // 2, 1)) * sin


def _rope_t(dv, cos, sin, lo):
    ds = dv * sin
    return dv * cos + jnp.where(lo, pltpu.roll(ds, 128 - HD // 2, 1), pltpu.roll(ds, HD // 2, 1))


def _placed(chunk, g, half0):
    own = jnp.where(half0 if g % 2 == 0 else jnp.logical_not(half0), chunk, 0.0)
    other = pltpu.roll(own, HD, 1)
    return (own, other) if g % 2 == 0 else (other, own)


def _unplace(acc, hf, g, half0):
    v = jnp.where(half0 if hf == 0 else jnp.logical_not(half0), acc, 0.0)
    return v if hf == g % 2 else pltpu.roll(v, HD, 1)


def _attn_fwd(proj, cos, sin, sinks, *, name, jobs=()):
    t = proj.shape[0]
    nb = t // L
    scale = HD ** -0.5

    def body(sink_ref, q_ref, kc_ref, kp_ref, vc_ref, vp_ref, cc_ref, sc_ref, cp_ref, sp_ref, o_ref, lse_ref):
        i = pl.program_id(0)
        lane, lo, half0 = _lane_consts()
        cos_c, sin_c, cos_p, sin_p = cc_ref[...], sc_ref[...], cp_ref[...], sp_ref[...]
        row = lax.broadcasted_iota(jnp.int32, (L, 2 * L), 0)
        col = lax.broadcasted_iota(jnp.int32, (L, 2 * L), 1)
        valid = jnp.logical_or(jnp.logical_and(jnp.logical_and(col < L, col > row), i > 0),
                               jnp.logical_and(col >= L, col - L <= row))
        kc = [_rope(kc_ref[:, 128 * m:128 * (m + 1)], cos_c, sin_c, lo) for m in range(2)]
        kp = [_rope(kp_ref[:, 128 * m:128 * (m + 1)], cos_p, sin_p, lo) for m in range(2)]
        lse_acc = jnp.zeros((L, 128), F32)
        outs = [jnp.zeros((L, 128), F32) for _ in range(QD // 128)]
        qs = [(_rope(q_ref[:, 128 * ch:128 * (ch + 1)], cos_c, sin_c, lo) * scale).astype(BF16) for ch in range(QD // 128)]
        both = lambda prev, cur, g: [jnp.concatenate([a, b], axis=0).astype(BF16)
                                     for a, b in zip(_placed(prev, g, half0), _placed(cur, g, half0))]
        for g in range(NKV):
            sl = slice(128 * (g // 2), 128 * (g // 2 + 1))
            kv = both(kp[g // 2], kc[g // 2], g)
            vv = both(vp_ref[:, sl], vc_ref[:, sl], g)
            for r in range(NQH // NKV):
                h = g * (NQH // NKV) + r
                ch, hf = h // 2, h % 2
                s = jnp.where(valid, _dot_nt(qs[ch], kv[hf]), NEG)
                sink = sink_ref[0, h]
                mx = jnp.maximum(jnp.max(s, axis=-1, keepdims=True), sink)
                e = jnp.exp(s - mx)
                den = jnp.sum(e, axis=-1, keepdims=True) + jnp.exp(sink - mx)
                outs[ch] = outs[ch] + _dot((e * (1.0 / den)).astype(BF16), vv[hf])
                lse_acc = jnp.where(lane == h, mx + jnp.log(den), lse_acc)
        for ch in range(QD // 128):
            o_ref[:, 128 * ch:128 * (ch + 1)] = outs[ch].astype(BF16)
        lse_ref[...] = lse_acc

    prev = lambda i: jnp.maximum(i - 1, 0)
    tab_c = pl.BlockSpec((L, 128), lambda i: (i, 0))
    tab_p = pl.BlockSpec((L, 128), lambda i: (prev(i), 0))
    return _call(
        body, jobs=jobs, name=name,
        out_shape=(jax.ShapeDtypeStruct((t, QD), BF16), jax.ShapeDtypeStruct((t, 128), F32)),
        grid=(nb,),
        in_specs=[pl.BlockSpec(memory_space=pltpu.SMEM),
                  pl.BlockSpec((L, QD), lambda i: (i, O_Q // QD)),
                  pl.BlockSpec((L, KVD), lambda i: (i, O_K // KVD)), pl.BlockSpec((L, KVD), lambda i: (prev(i), O_K // KVD)),
                  pl.BlockSpec((L, KVD), lambda i: (i, O_V // KVD)), pl.BlockSpec((L, KVD), lambda i: (prev(i), O_V // KVD)),
                  tab_c, tab_c, tab_p, tab_p],
        out_specs=(pl.BlockSpec((L, QD), lambda i: (i, 0)), pl.BlockSpec((L, 128), lambda i: (i, 0))),
        compiler_params=_cp(("parallel",)),
    )(sinks, proj, proj, proj, proj, proj, cos, sin, cos, sin)


def _attn_bwd(proj, cos, sin, sinks, attn, lse, dattn, dproj, *, name, jobs=()):
    t = proj.shape[0]
    nb = t // L
    scale = HD ** -0.5

    def body(sink_ref, qi_ref, qn_ref, kc_ref, kp_ref, vc_ref, vp_ref, doi_ref, don_ref, oi_ref, on_ref,
             lsei_ref, lsen_ref, cc_ref, sc_ref, cp_ref, sp_ref, cn_ref, sn_ref, _, dqkv_ref, dsk_ref):
        i = pl.program_id(0)
        lane, lo, half0 = _lane_consts()
        half1 = jnp.logical_not(half0)
        cos_c, sin_c = cc_ref[...], sc_ref[...]
        row = lax.broadcasted_iota(jnp.int32, (L, 2 * L), 0)
        col = lax.broadcasted_iota(jnp.int32, (L, 2 * L), 1)
        valid = jnp.logical_or(jnp.logical_and(jnp.logical_and(col < L, col > row), i > 0),
                               jnp.logical_and(col >= L, col - L <= row))
        m_next = jnp.logical_and(col[:, :L] > row[:, :L], i < nb - 1)
        kc = [_rope(kc_ref[:, 128 * m:128 * (m + 1)], cos_c, sin_c, lo) for m in range(2)]
        kp = [_rope(kp_ref[:, 128 * m:128 * (m + 1)], cp_ref[...], sp_ref[...], lo) for m in range(2)]
        lse_i, lse_n = lsei_ref[...], lsen_ref[...]
        dk_acc = [jnp.zeros((L, 128), F32) for _ in range(2)]
        dv_acc = [jnp.zeros((L, 128), F32) for _ in range(2)]
        dsk_acc = jnp.zeros((1, 128), F32)
        lane1 = lax.broadcasted_iota(jnp.int32, (1, 128), 1)
        both = lambda prev, cur, g: [jnp.concatenate([a, b], axis=0).astype(BF16)
                                     for a, b in zip(_placed(prev, g, half0), _placed(cur, g, half0))]
        kvs = [both(kp[g // 2], kc[g // 2], g) for g in range(NKV)]
        vvs = [both(vp_ref[:, 128 * (g // 2):128 * (g // 2 + 1)], vc_ref[:, 128 * (g // 2):128 * (g // 2 + 1)], g)
               for g in range(NKV)]
        for ch in range(QD // 128):
            sl = slice(128 * ch, 128 * (ch + 1))
            q_i = (_rope(qi_ref[:, sl], cos_c, sin_c, lo) * scale).astype(BF16)
            q_n = (_rope(qn_ref[:, sl], cn_ref[...], sn_ref[...], lo) * scale).astype(BF16)
            q_in = jnp.concatenate([q_i, q_n], axis=0)
            do_i, do_n = doi_ref[:, sl], don_ref[:, sl]
            do_ib, do_nb = do_i.astype(BF16), do_n.astype(BF16)
            do_in = jnp.concatenate([do_ib, do_nb], axis=0)
            od_i = do_i * oi_ref[:, sl].astype(F32)
            od_n = do_n * on_ref[:, sl].astype(F32)
            dq_ch = jnp.zeros((L, 128), F32)
            for hf in range(2):
                h = 2 * ch + hf
                g = h // (NQH // NKV)
                hm = half0 if hf == 0 else half1
                kv, vv = kvs[g][hf], vvs[g][hf]
                kcv, vcv = kv[L:], vv[L:]
                dl_i = jnp.sum(jnp.where(hm, od_i, 0.0), axis=-1, keepdims=True)
                dl_n = jnp.sum(jnp.where(hm, od_n, 0.0), axis=-1, keepdims=True)
                ls_i = jnp.sum(jnp.where(lane == h, lse_i, 0.0), axis=-1, keepdims=True)
                ls_n = jnp.sum(jnp.where(lane == h, lse_n, 0.0), axis=-1, keepdims=True)
                p = jnp.where(valid, jnp.exp(_dot_nt(q_i, kv) - ls_i), 0.0)
                ds = (p * (_dot_nt(do_ib, vv) - dl_i)).astype(BF16)
                dq_ch = dq_ch + jnp.where(hm, _dot(ds, kv) * scale, 0.0)
                sink = sink_ref[0, h]
                dsk = -jnp.sum(jnp.exp(sink - ls_i) * dl_i, axis=0, keepdims=True)
                dsk_acc = dsk_acc + jnp.where(lane1 == h, dsk, 0.0)
                p_n = jnp.where(m_next, jnp.exp(_dot_nt(q_n, kcv) - ls_n), 0.0)
                ds_n = (p_n * (_dot_nt(do_nb, vcv) - dl_n)).astype(BF16)
                dv_h = _dot_tn(jnp.concatenate([p[:, L:].astype(BF16), p_n.astype(BF16)], axis=0), do_in)
                dk_h = _dot_tn(jnp.concatenate([ds[:, L:], ds_n], axis=0), q_in)
                dv_acc[g // 2] = dv_acc[g // 2] + _unplace(dv_h, hf, g, half0)
                dk_acc[g // 2] = dk_acc[g // 2] + _unplace(dk_h, hf, g, half0)
            dqkv_ref[:, sl] = _rope_t(dq_ch, cos_c, sin_c, lo).astype(BF16)
        for m in range(2):
            dqkv_ref[:, QD + 128 * m:QD + 128 * (m + 1)] = _rope_t(dk_acc[m], cos_c, sin_c, lo).astype(BF16)
            dqkv_ref[:, QD + KVD + 128 * m:QD + KVD + 128 * (m + 1)] = dv_acc[m].astype(BF16)

        @pl.when(i == 0)
        def _():
            dsk_ref[...] = jnp.zeros_like(dsk_ref)

        dsk_ref[...] += jnp.broadcast_to(dsk_acc, dsk_ref.shape)

    prev = lambda i: jnp.maximum(i - 1, 0)
    nxt = lambda i: jnp.minimum(i + 1, nb - 1)
    cur_q = pl.BlockSpec((L, QD), lambda i: (i, 0))
    nxt_q = pl.BlockSpec((L, QD), lambda i: (nxt(i), 0))
    tab = lambda f: pl.BlockSpec((L, 128), lambda i: (f(i), 0))
    ident = lambda i: i
    qkv = QD + 2 * KVD
    assert O_K == O_Q + QD and O_V == O_K + KVD and O_Q % qkv == 0
    return _call(
        body, jobs=jobs, name=name,
        out_shape=(jax.ShapeDtypeStruct(dproj.shape, BF16), jax.ShapeDtypeStruct((8, 128), F32)),
        grid=(nb,),
        in_specs=[pl.BlockSpec(memory_space=pltpu.SMEM),
                  pl.BlockSpec((L, QD), lambda i: (i, O_Q // QD)), pl.BlockSpec((L, QD), lambda i: (nxt(i), O_Q // QD)),
                  pl.BlockSpec((L, KVD), lambda i: (i, O_K // KVD)), pl.BlockSpec((L, KVD), lambda i: (prev(i), O_K // KVD)),
                  pl.BlockSpec((L, KVD), lambda i: (i, O_V // KVD)), pl.BlockSpec((L, KVD), lambda i: (prev(i), O_V // KVD)),
                  cur_q, nxt_q, cur_q, nxt_q, tab(ident), tab(nxt),
                  tab(ident), tab(ident), tab(prev), tab(prev), tab(nxt), tab(nxt), ANY],
        out_specs=(pl.BlockSpec((L, qkv), lambda i: (i, O_Q // qkv)), pl.BlockSpec((8, 128), lambda i: (0, 0))),
        compiler_params=_cp(("arbitrary",)), aliases={19: 0},
    )(sinks, proj, proj, proj, proj, proj, proj, dattn, dattn, attn, attn, lse, lse, cos, sin, cos, sin, cos, sin, dproj)


PAIRS = NH // NG // 2


def _softplus(x):
    return jnp.maximum(x, 0.0) + jnp.log(1.0 + jnp.exp(-jnp.abs(x)))


def _ssd_chunk(g, xps, dtr, bm, cm, sps, dtb, alog, dsk):
    lane = lax.broadcasted_iota(jnp.int32, (L, 128), 1)
    lane1 = lax.broadcasted_iota(jnp.int32, (1, 128), 1)
    row = lax.broadcasted_iota(jnp.int32, (L, L), 0)
    col = lax.broadcasted_iota(jnp.int32, (L, L), 1)
    rowc = lax.broadcasted_iota(jnp.int32, (128, 1), 0)
    tril = col <= row
    dt = _softplus(dtr + dtb)
    a = dt * (-jnp.exp(alog))
    a_cs = lax.dot_general(tril.astype(F32), a, (((1,), (0,)), ((), ())), precision=lax.Precision.HIGHEST,
                           preferred_element_type=F32)
    a_cst = a_cs.T
    a_last = jnp.sum(jnp.where(row == L - 1, a_cs, 0.0), axis=0, keepdims=True)
    cb = _bdot_nt(cm, bm)
    ys, snew = [], []
    for q in range(PAIRS):
        xp, sp = xps[q], sps[q]
        skip = jnp.zeros((L, 128), F32)
        keep = jnp.zeros((128, 1), F32)
        ms, xds, cds, sms, bds = [], [], [], [], []
        for hh in range(2):
            h = g * 2 * PAIRS + 2 * q + hh
            hm = (lane < HD) if hh == 0 else (lane >= HD)
            rm = (rowc < HD) if hh == 0 else (rowc >= HD)
            dt_h = jnp.sum(jnp.where(lane == h, dt, 0.0), axis=1, keepdims=True)
            acs_h = jnp.sum(jnp.where(lane == h, a_cs, 0.0), axis=1, keepdims=True)
            acst_h = jnp.sum(jnp.where(row == h, a_cst, 0.0), axis=0, keepdims=True)
            al_h = jnp.sum(jnp.where(lane1 == h, a_last, 0.0), axis=1, keepdims=True)
            dsk_h = jnp.sum(jnp.where(lane1 == h, dsk, 0.0), axis=1, keepdims=True)
            decay = jnp.where(tril, jnp.exp(jnp.where(tril, acs_h - acst_h, 0.0)), 0.0)
            xh = jnp.where(hm, xp, 0.0)
            ms.append(cb * decay)
            xds.append(xh * dt_h)
            cds.append(cm * jnp.exp(acs_h))
            sms.append(jnp.where(rm, sp, 0.0))
            bds.append(bm * jnp.exp(al_h - acs_h))
            skip = skip + dsk_h * xh
            keep = keep + jnp.where(rm, jnp.exp(al_h), 0.0)
        xd2 = jnp.concatenate(xds, axis=0)
        y_pair = (_bdot(jnp.concatenate(ms, axis=1), xd2)
                  + _bdot_nt(jnp.concatenate(cds, axis=1), jnp.concatenate(sms, axis=1)) + skip)
        ys.append(y_pair)
        snew.append(sp * keep + _bdot_tn(xd2, jnp.concatenate(bds, axis=0)))
    return ys, snew


def _ssd_specs(t):
    nc = t // L
    xs = lambda f: pl.BlockSpec((L, 128 * PAIRS), lambda c, g: (f(c), g))
    bspec = lambda f: pl.BlockSpec((L, NS), lambda c, g: (f(c), DI // NS + g))
    cspec = lambda f: pl.BlockSpec((L, NS), lambda c, g: (f(c), DI // NS + NG + g))
    dts = lambda f: pl.BlockSpec((L, 128), lambda c, g: (f(c), O_DT // 128))
    par = pl.BlockSpec((1, 128), lambda c, g: (0, 0))
    st = lambda f: pl.BlockSpec((1, 1, PAIRS, 128, NS), lambda c, g: (f(c), g, 0, 0, 0))
    return nc, xs, bspec, cspec, dts, par, st


def _ssd_fwd(xbc_act, proj, dtb, alog, dsk, *, name, jobs=()):
    t = proj.shape[0]
    nc, xs, bspec, cspec, dts, par, st = _ssd_specs(t)
    ident = lambda c: c

    def body(x_ref, b_ref, c_ref, dt_ref, dtb_ref, al_ref, dsk_ref, y_ref, sin_ref, s_ref):
        c, g = pl.program_id(0), pl.program_id(1)

        @pl.when(c == 0)
        def _():
            s_ref[g] = jnp.zeros((PAIRS, 128, NS), F32)

        sps = [s_ref[g, q] for q in range(PAIRS)]
        for q in range(PAIRS):
            sin_ref[0, 0, q] = sps[q]
        xps = [x_ref[:, 128 * q:128 * (q + 1)] for q in range(PAIRS)]
        ys, snew = _ssd_chunk(g, xps, dt_ref[...], b_ref[...], c_ref[...], sps, dtb_ref[...], al_ref[...], dsk_ref[...])
        for q in range(PAIRS):
            y_ref[:, 128 * q:128 * (q + 1)] = ys[q]
            s_ref[g, q] = snew[q]

    return _call(
        body, jobs=jobs, name=name,
        out_shape=(jax.ShapeDtypeStruct((t, DI), F32), jax.ShapeDtypeStruct((nc, NG, PAIRS, 128, NS), F32)),
        grid=(nc, NG),
        in_specs=[xs(ident), bspec(ident), cspec(ident), dts(ident), par, par, par],
        out_specs=(pl.BlockSpec((L, 128 * PAIRS), lambda c, g: (c, g)), st(ident)),
        scratch_shapes=[pltpu.VMEM((NG, PAIRS, 128, NS), F32)],
        compiler_params=_cp(("arbitrary", "arbitrary")),
    )(xbc_act, xbc_act, xbc_act, proj, dtb, alog, dsk)


def _ssd_bwd(xbc_act, proj, dtb, alog, dsk, states, dy, dproj, *, name, jobs=()):
    t = proj.shape[0]
    nc, xs, bspec, cspec, dts, par, st = _ssd_specs(t)
    rev = lambda c: nc - 1 - c

    def body(x_ref, b_ref, c_ref, dt_ref, dtb_ref, al_ref, dsk_ref, sin_ref, dy_ref, _,
             dx_ref, db_ref, dc_ref, ddtp_ref, ddtb_ref, dal_ref, ddsk_ref, ds_ref, ddt_ref):
        c, g = pl.program_id(0), pl.program_id(1)

        @pl.when(c == 0)
        def _():
            ds_ref[g] = jnp.zeros((PAIRS, 128, NS), F32)

        @pl.when(jnp.logical_and(c == 0, g == 0))
        def _():
            ddtb_ref[...] = jnp.zeros_like(ddtb_ref)
            dal_ref[...] = jnp.zeros_like(dal_ref)
            ddsk_ref[...] = jnp.zeros_like(ddsk_ref)

        @pl.when(g == 0)
        def _():
            ddt_ref[...] = jnp.zeros_like(ddt_ref)

        sps = [sin_ref[0, 0, q] for q in range(PAIRS)]
        xps = [x_ref[:, 128 * q:128 * (q + 1)] for q in range(PAIRS)]
        _, vjp = jax.vjp(functools.partial(_ssd_chunk, g), xps, dt_ref[...], b_ref[...], c_ref[...], sps,
                         dtb_ref[...], al_ref[...], dsk_ref[...])
        dys = [dy_ref[:, 128 * q:128 * (q + 1)] for q in range(PAIRS)]
        dss = [ds_ref[g, q] for q in range(PAIRS)]
        dxps, ddt, db, dc, dsps, ddtb, dal, ddsk = vjp((dys, dss))
        for q in range(PAIRS):
            dx_ref[:, 128 * q:128 * (q + 1)] = dxps[q]
            ds_ref[g, q] = dsps[q]
        db_ref[...] = db
        dc_ref[...] = dc
        ddt_ref[...] += ddt
        ddtb_ref[...] += jnp.broadcast_to(ddtb, ddtb_ref.shape)
        dal_ref[...] += jnp.broadcast_to(dal, dal_ref.shape)
        ddsk_ref[...] += jnp.broadcast_to(ddsk, ddsk_ref.shape)

        @pl.when(g == NG - 1)
        def _():
            ddtp_ref[:, :128] = ddt_ref[...].astype(BF16)
            ddtp_ref[:, 128:] = jnp.zeros((L, DT_PAD - 128), BF16)

    acc = pl.BlockSpec((8, 128), lambda c, g: (0, 0))
    o8 = jax.ShapeDtypeStruct((8, 128), F32)
    return _call(
        body, jobs=jobs, name=name,
        out_shape=(jax.ShapeDtypeStruct((t, DI), F32), jax.ShapeDtypeStruct((t, NG * NS), F32),
                   jax.ShapeDtypeStruct((t, NG * NS), F32), jax.ShapeDtypeStruct(dproj.shape, BF16), o8, o8, o8),
        grid=(nc, NG),
        in_specs=[xs(rev), bspec(rev), cspec(rev), dts(rev), par, par, par, st(rev),
                  pl.BlockSpec((L, 128 * PAIRS), lambda c, g: (rev(c), g)), ANY],
        out_specs=(pl.BlockSpec((L, 128 * PAIRS), lambda c, g: (rev(c), g)),
                   pl.BlockSpec((L, NS), lambda c, g: (rev(c), g)), pl.BlockSpec((L, NS), lambda c, g: (rev(c), g)),
                   pl.BlockSpec((L, DT_PAD), lambda c, g: (rev(c), O_DT // DT_PAD)), acc, acc, acc),
        scratch_shapes=[pltpu.VMEM((NG, PAIRS, 128, NS), F32), pltpu.VMEM((L, 128), F32)],
        compiler_params=_cp(("arbitrary", "arbitrary")), aliases={9: 3},
    )(xbc_act, xbc_act, xbc_act, proj, dtb, alog, dsk, states, dy, dproj)


def _pad_lanes(v, n=128):
    return jnp.pad(v, ((0, 0), (0, n - v.shape[1])))


def _local_step(x, p, positions, target, small, plan):
    t = x.shape[0]
    cos, sin = _rope_tables(positions, t)
    dtb, alog, dsk = _pad_lanes(small["dt_bias"]), _pad_lanes(small["a_log"]), _pad_lanes(small["d_skip"])
    w, jobs = plan.w, plan.jobs

    def mm(a, b, *, name, tm=t, tn=512, **kw):
        return _matmul(a, b, tm=tm, tn=tn, name=name, jobs=jobs(name), **kw)

    tkl = FFN // 4

    def dw(wname, a, dy, *, name, tm):
        plan.g(wname, _matmul(a, dy, ta=True, out_dtype=BF16, tm=tm, tn=512, tk=t, name=name, jobs=jobs(name)))

    u = _rmsnorm_fwd(x, small["g_mix"], name="norm_mix")
    proj = mm(u, w("w_in"), tn=1024, tk=D, name="mm_in")
    attn, lse = _attn_fwd(proj, cos, sin, small["sinks"], name="attn_fwd", jobs=jobs("attn_fwd"))
    out_a = mm(attn, w("w_attn_br"), tk=QD, name="mm_attn_br")
    xbc_act = _conv_fwd(proj, small["conv_w"], small["conv_b"], name="conv_fwd")
    y_pre, states = _ssd_fwd(xbc_act, proj, dtb, alog, dsk, name="ssd_fwd", jobs=jobs("ssd_fwd"))
    yn = _gated_norm_fwd(y_pre, proj, small["g_ssd"], name="gated_norm_fwd")
    out_s = mm(yn, w("w_ssd_br"), tk=DI, name="mm_ssd_br")
    merged = _merge_fwd(proj, out_a, out_s, name="merge_fwd")
    h1 = mm(merged, w("w_o"), add=x, tk=D, name="mm_o")
    f = _rmsnorm_fwd(h1, small["g_ffn"], name="norm_ffn")
    gate, up, act = _swiglu_fwd(f, w("w_gate"), w("w_up"), name="swiglu_fwd", jobs=jobs("swiglu_fwd"))
    h2 = mm(act, w("w_down"), add=h1, tm=t // 2, tk=FFN // 2, name="mm_down")
    e = _rmsnorm_fwd(h2, small["g_ple"], name="norm_ple")
    pgl = mm(e, w("w_ple_gate"), tk=D, name="mm_ple_gate")
    pb = p.astype(BF16)
    pp = mm(pb, w("w_ple_proj"), tk=PLE, name="mm_ple_proj")
    dh3, dpgl, dpp, loss, dg_final = _final(h2, pgl, pp, target, small["g_final"].reshape(1, D), name="final")

    dw("w_ple_proj", pb, dpp, tm=PLE, name="mm_d_ple_proj")
    dw("w_ple_gate", e, dpgl, tm=D, name="mm_d_ple_gate")
    de = mm(dpgl, w("w_ple_gate"), tb=True, tk=D, name="mm_de")
    dh2, dh2b, dg_ple = _rmsnorm_bwd(h2, small["g_ple"], de, dh3, name="norm_ple_bwd", jobs=jobs("norm_ple_bwd"))
    dw("w_down", act, dh2b, tm=FFN // 2, name="mm_d_down")
    dgate, dup = _swiglu_bwd(dh2b, w("w_down"), gate, up, name="swiglu_bwd", jobs=jobs("swiglu_bwd"))
    dw("w_gate", f, dgate, tm=D, name="mm_d_gate")
    dw("w_up", f, dup, tm=D, name="mm_d_up")
    df = mm(dgate, w("w_gate"), tb=True, tn=1024, tk=tkl, name="mm_df_gate")
    df = mm(dup, w("w_up"), tb=True, add=df, tm=t // 2, tk=FFN // 2, name="mm_df_up")
    dh1, dh1b, dg_ffn = _rmsnorm_bwd(h1, small["g_ffn"], df, dh2, name="norm_ffn_bwd", jobs=jobs("norm_ffn_bwd"))
    dw("w_o", merged, dh1b, tm=D, name="mm_d_o")
    dmerged = mm(dh1b, w("w_o"), tb=True, tk=D, name="mm_dmerged")
    dout_a, dout_s, dproj = _merge_bwd(proj, out_a, out_s, dmerged, name="merge_bwd")
    dw("w_attn_br", attn, dout_a, tm=QD, name="mm_d_attn_br")
    dw("w_ssd_br", yn, dout_s, tm=DI, name="mm_d_ssd_br")
    dattn = mm(dout_a, w("w_attn_br"), tb=True, tk=D, name="mm_dattn")
    dyn = mm(dout_s, w("w_ssd_br"), tb=True, tk=D, name="mm_dyn")
    dproj, dsinks = _attn_bwd(proj, cos, sin, small["sinks"], attn, lse, dattn, dproj, name="attn_bwd",
                              jobs=jobs("attn_bwd"))
    dy_pre, dproj, dg_ssd = _gated_norm_bwd(y_pre, proj, small["g_ssd"], dyn, dproj, name="gated_norm_bwd",
                                            jobs=jobs("gated_norm_bwd"))
    dxs, db, dc, dproj, ddtb, dalog, ddsk = _ssd_bwd(xbc_act, proj, dtb, alog, dsk, states, dy_pre, dproj, name="ssd_bwd",
                                                     jobs=jobs("ssd_bwd"))
    dproj, dconv_w, dconv_b = _conv_bwd(proj, small["conv_w"], small["conv_b"], dxs, db, dc, dproj, name="conv_bwd",
                                        jobs=jobs("conv_bwd"))
    for which, h in (("send", 1 - plan.core), ("keep", plan.core)):
        uh = lax.dynamic_slice_in_dim(u, h * (D // 2), D // 2, axis=1)
        name = "mm_d_in_" + which
        plan.g_half("w_in", which, _matmul(uh, dproj, ta=True, out_dtype=BF16, tm=D // 2, tn=1024, tk=t, name=name,
                                           jobs=jobs(name)))
    du = mm(dproj, w("w_in"), tb=True, tn=1024, tk=tkl, name="mm_du")
    grad_x, _, dg_mix = _rmsnorm_bwd(x, small["g_mix"], du, dh1, name="norm_mix_bwd", jobs=jobs("norm_mix_bwd"))

    gs = {
        "g_mix": dg_mix[:1], "conv_w": dconv_w[:CW], "conv_b": dconv_b[:1], "dt_bias": ddtb[:1, :NH],
        "a_log": dalog[:1, :NH], "d_skip": ddsk[:1, :NH], "g_ssd": dg_ssd[:1], "sinks": dsinks[:1, :NQH],
        "g_ffn": dg_ffn[:1], "g_ple": dg_ple[:1], "g_final": dg_final[0],
    }
    return loss, grad_x, gs


def _shard_pieces():
    segs = ((R_Q, QD, O_Q), (R_K, KVD, O_K), (R_V, KVD, O_V), (R_Z, DI, O_Z), (R_XBC, CONV, O_XBC), (R_DT, NH, O_DT),
            (R_GA, D, O_GA), (R_GS, D, O_GS))
    cs = IN_DIM // NCHIP
    out = []
    for j in range(NCHIP):
        for r0, n, k0 in segs:
            lo, hi = max(r0, j * cs), min(r0 + n, (j + 1) * cs)
            if lo < hi:
                out.append((j, lo - j * cs, hi - lo, k0 + lo - r0))
    return out


SLAB = IN_DIM // NCHIP
SLAB_PAD = -(-SLAB // 128) * 128
REMAP_ROWS = 256


def _lane_remap(src, dst_slabs, dst_cols, moves, *, name, add=None, jobs=()):
    s_n, rows, s_cols = src.shape
    assert s_cols % 128 == 0 and dst_cols % 128 == 0 and rows % REMAP_ROWS == 0
    half = REMAP_ROWS // 2

    def body(s_ref, *refs):
        d_ref = refs[-1]
        lane = lax.broadcasted_iota(jnp.int32, (half, 128), 1)
        tiles = {}

        def tile(j, m):
            if (j, m) not in tiles:
                tiles[j, m] = pltpu.bitcast(s_ref[j, :, 128 * m:128 * (m + 1)], jnp.uint32)
            return tiles[j, m]

        def window(j, base):
            m0, s = base // 128, base % 128
            left = tile(j, m0) if 0 <= m0 < s_cols // 128 else None
            if s == 0:
                return left
            right = tile(j, m0 + 1) if 0 <= m0 + 1 < s_cols // 128 else None
            left = None if left is None else pltpu.roll(left, 128 - s, 1)
            right = None if right is None else pltpu.roll(right, 128 - s, 1)
            if left is None or right is None:
                return right if left is None else left
            return jnp.where(lane < 128 - s, left, right)

        for ds in range(dst_slabs):
            for t in range(dst_cols // 128):
                o = 128 * t
                acc = jnp.zeros((half, 128), jnp.uint32)
                for sj, sc, n, dj, dc in moves:
                    lo, hi = max(o, dc) - o, min(o + 128, dc + n) - o
                    if dj != ds or lo >= hi:
                        continue
                    win = window(sj, o - dc + sc)
                    acc = win if (lo, hi) == (0, 128) else jnp.where(jnp.logical_and(lane >= lo, lane < hi), win, acc)
                out = pltpu.bitcast(acc, BF16)
                if add is not None:
                    out = (out.astype(F32) + refs[0][ds, :, o:o + 128].astype(F32)).astype(BF16)
                d_ref[ds, :, o:o + 128] = out

    dst_blk = pl.BlockSpec((dst_slabs, REMAP_ROWS, dst_cols), lambda i: (0, i, 0))
    return _call(
        body, jobs=jobs, name=name, out_shape=jax.ShapeDtypeStruct((dst_slabs, rows, dst_cols), BF16),
        grid=(rows // REMAP_ROWS,),
        in_specs=[pl.BlockSpec((s_n, REMAP_ROWS, s_cols), lambda i: (0, i, 0))] + ([dst_blk] if add is not None else []),
        out_specs=dst_blk, compiler_params=_cp(("parallel",)),
    )(*((src,) if add is None else (src, add)))


def _slabs_to_kernel_cols(slabs, *, name, jobs=()):
    moves = [(j, a, n, 0, k0) for j, a, n, k0 in _shard_pieces()]
    return _lane_remap(slabs, 1, NP, moves, name=name, jobs=jobs)[0]


def _kernel_cols_to_slabs(g, *, name, add=None, jobs=()):
    moves = [(0, k0, n, j, a) for j, a, n, k0 in _shard_pieces()]
    return _lane_remap(g[None], NCHIP, SLAB_PAD, moves, name=name, add=add, jobs=jobs)


MATS = {
    n: (n, kind, 1, r, c, tp, tf) for n, kind, r, c, tp, tf in (
        ("w_in", "stk", 2048, SLAB_PAD, 256, 256),
        ("w_attn_br", "col", 1024, 512, 256, 256),
        ("w_ssd_br", "row", 512, 2048, 512, 256),
        ("w_o", "row", 512, 2048, 512, 256),
        ("w_gate", "col", 2048, 1408, 256, 256),
        ("w_up", "col", 2048, 1408, 256, 256),
        ("w_down", "row", 1408, 2048, 704, 704),
        ("w_ple_gate", "row", 512, 2048, 512, 256),
        ("w_ple_proj", "col", 256, 512, 128, 128),
    )}


def _pos():
    return lax.axis_index("x"), lax.axis_index("y"), lax.axis_index("c")


def _flip(v, a):
    return 1 - v if a else v


def _remote(src, dst, send, recv, dev):
    return pltpu.make_async_remote_copy(src_ref=src, dst_ref=dst, send_sem=send, recv_sem=recv, device_id=dev,
                                        device_id_type=MESH)


def _whole_shape(kind, g, r, c):
    return {"row": (g, NCHIP * r, c), "col": (g, r, NCHIP * c), "stk": (NCHIP, r, c)}[kind]


def _cols(j, c):
    return pl.ds(pl.multiple_of(j * c, 128), c)


def _whole_shard(kind, ref, j, r, c):
    if kind == "row":
        return ref.at[:, pl.ds(j * r, r), :]
    if kind == "col":
        return ref.at[:, :, _cols(j, c)]
    return ref.at[pl.ds(j, 1)]


def _whole_rows(kind, ref, j, row, n, r, c):
    if kind == "row":
        return ref.at[:, pl.ds(j * r + row, n), :]
    if kind == "col":
        return ref.at[:, pl.ds(row, n), _cols(j, c)]
    return ref.at[pl.ds(j, 1), pl.ds(row, n), :]


class _GatherJob(_Job):
    has_mid = True
    NCP = 13

    def __init__(self, names, shards, sink):
        self.mats = [MATS[n] for n in names]
        self.srcs = [shards[n] for n in names]
        self.news = [jax.ShapeDtypeStruct(_whole_shape(kind, g, r, c), BF16) for _, kind, g, r, c, _, _ in self.mats]
        n = len(names)
        self.scratch = [pltpu.SemaphoreType.DMA((self.NCP * n,)), pltpu.SemaphoreType.DMA((self.NCP * n,))]
        self.names, self.sink = names, sink

    def _copies(self, srcs, news, sems):
        send, recv = sems
        x, y, c = _pos()
        me, jx, jy, jd = 2 * x + y, 2 * (1 - x) + y, 2 * x + (1 - y), 2 * (1 - x) + (1 - y)
        nbx, nby, sib = (1 - x, y, c), (x, 1 - y, c), (x, y, 1 - c)
        cps = []
        for w, (_, kind, g, r, cc, _, _) in enumerate(self.mats):
            hr, qr = r // 2, r // 4
            at = lambda j, h, q, n: _whole_rows(kind, news[w], j, h * hr + q * qr, n, r, cc)
            mine = lambda q: srcs[w].at[:, pl.ds(c * hr + q * qr, qr), :]
            cp = lambda k, s, d, dev: _remote(s, d, send.at[self.NCP * w + k], recv.at[self.NCP * w + k], dev)
            cps.append([
                cp(0, mine(0), at(me, c, 0, qr), nbx), cp(1, mine(1), at(me, c, 1, qr), nbx),
                cp(2, mine(1), at(me, c, 1, qr), nby), cp(3, mine(0), at(me, c, 0, qr), nby),
                cp(4, at(jx, c, 0, qr), at(jx, c, 0, qr), nby), cp(5, at(jy, c, 1, qr), at(jy, c, 1, qr), nbx),
                cp(6, at(jx, c, 0, qr), at(jx, c, 0, qr), sib), cp(7, at(jx, c, 1, qr), at(jx, c, 1, qr), sib),
                cp(8, at(jy, c, 1, qr), at(jy, c, 1, qr), sib), cp(9, at(jy, c, 0, qr), at(jy, c, 0, qr), sib),
                cp(10, at(jd, c, 0, qr), at(jd, c, 0, qr), sib), cp(11, at(jd, c, 1, qr), at(jd, c, 1, qr), sib),
                cp(12, srcs[w], _whole_shard(kind, news[w], me, r, cc), sib)])
        return cps

    def _pass_on(self, srcs, news, sems, pairs):
        cps = self._copies(srcs, news, sems)
        for w in range(len(self.mats)):
            for arrived, onward in pairs:
                cps[w][arrived].wait_recv()
                for k in onward:
                    cps[w][k].start()

    def start(self, srcs, dsts, news, sems):
        cps = self._copies(srcs, news, sems)
        for k in (0, 2, 1, 3, 12):
            for w in range(len(self.mats)):
                cps[w][k].start()

    def mid(self, srcs, dsts, news, sems):
        self._pass_on(srcs, news, sems, ((0, (4, 6)), (2, (5, 8))))

    def late(self, srcs, dsts, news, sems):
        self._pass_on(srcs, news, sems, ((1, (7,)), (3, (9,))))

    def finish(self, srcs, dsts, news, sems):
        self._pass_on(srcs, news, sems, ((4, (10,)), (5, (11,))))
        cps = self._copies(srcs, news, sems)
        for w in range(len(self.mats)):
            for k in (6, 7, 8, 9, 10, 11, 12):
                cps[w][k].wait_recv()
            for k in range(self.NCP):
                cps[w][k].wait_send()

    def done(self, dsts, news):
        for n, a in zip(self.names, news):
            self.sink[n] = a


class _SwapJob(_Job):
    def __init__(self, build, ncopies, *, srcs=(), dsts=(), news=(), done=None):
        self.build, self.srcs, self.dsts, self.news, self._done = build, list(srcs), list(dsts), list(news), done
        self.scratch = [pltpu.SemaphoreType.DMA((ncopies,)), pltpu.SemaphoreType.DMA((ncopies,))]

    def start(self, srcs, dsts, news, sems):
        for cp in self.build(srcs, dsts, news, *sems):
            cp.start()

    def finish(self, srcs, dsts, news, sems):
        for cp in self.build(srcs, dsts, news, *sems):
            cp.wait()

    def done(self, dsts, news):
        if self._done is not None:
            self._done(dsts, news)


def _half_of_whole(kind, ref, h, r, c):
    if kind == "row":
        return ref.at[:, :, pl.ds(pl.multiple_of(h * (c // 2), 128), c // 2)]
    return ref.at[:, pl.ds(h * (r // 2), r // 2), :]


def _half_shape(kind, g, r, c):
    return {"row": (g, NCHIP * r, c // 2), "col": (g, r // 2, NCHIP * c), "stk": (NCHIP, r // 2, c)}[kind]


def _sub_shape(kind, r, c):
    return {"row": (1, r // 2, c // 2), "col": (1, r // 4, c), "stk": (1, r // 4, c)}[kind]


def _sub_of_half(kind, ref, j, p, r, c):
    sr = _sub_shape(kind, r, c)[1]
    if kind == "row":
        return ref.at[:, pl.ds(j * r + p * sr, sr), :]
    if kind == "col":
        return ref.at[:, pl.ds(p * sr, sr), _cols(j, c)]
    return ref.at[pl.ds(j, 1), pl.ds(p * sr, sr), :]


def _sub_tile(sr):
    return 256 if sr % 256 == 0 else sr


def _half_of_shard(kind, ref, h, r, c):
    if kind == "row":
        return ref.at[:, :, pl.ds(pl.multiple_of(h * (c // 2), 128), c // 2)]
    return ref.at[:, pl.ds(h * (r // 2), r // 2), :]


def _pair_sum(pack, core, mine, got, whole=True):
    name, kind, g, r, c, tr, _ = pack
    hs = _half_shape(kind, g, r, c)
    nb = hs[1] // tr

    def body(core_ref, a_ref, b_ref, o_ref):
        o_ref[...] = (a_ref[...].astype(F32) + b_ref[...].astype(F32)).astype(BF16)

    blk = (1, tr, hs[2])
    same = lambda gi, i, core_ref: (gi, i, 0)
    if not whole:
        a_map = same
    elif kind == "row":
        a_map = lambda gi, i, core_ref: (gi, i, core_ref[0])
    else:
        a_map = lambda gi, i, core_ref: (gi, core_ref[0] * nb + i, 0)
    return pl.pallas_call(
        body, name="pair_sum_" + name, out_shape=jax.ShapeDtypeStruct(hs, BF16),
        grid_spec=pltpu.PrefetchScalarGridSpec(
            num_scalar_prefetch=1, grid=(hs[0], nb),
            in_specs=[pl.BlockSpec(blk, a_map), pl.BlockSpec(blk, same)], out_specs=pl.BlockSpec(blk, same)),
        compiler_params=_cp(("parallel", "parallel")),
    )(core, mine, got)


def _sub_sums(pack, idx, half, got, *, name):
    _, kind, g, r, c, _, _ = pack
    _, sr, sc = _sub_shape(kind, r, c)
    tr = _sub_tile(sr)
    nb = sr // tr

    def body(idx_ref, a_ref, ga_ref, b_ref, gb_ref, k_ref, p_ref):
        k_ref[0, 0] = a_ref[0].astype(F32) + ga_ref[0, 0].astype(F32)
        p_ref[0, 0] = (b_ref[0].astype(F32) + gb_ref[0, 0].astype(F32)).astype(BF16)

    def sub_map(o):
        if kind == "row":
            return lambda q, i, ix: (0, ix[4 * q + o] * (r // tr) + ix[4 * q + o + 1] * nb + i, 0)
        if kind == "col":
            return lambda q, i, ix: (0, ix[4 * q + o + 1] * nb + i, ix[4 * q + o])
        return lambda q, i, ix: (ix[4 * q + o], ix[4 * q + o + 1] * nb + i, 0)

    sub = lambda o: pl.BlockSpec((1, tr, sc), sub_map(o))
    got_blk = lambda o: pl.BlockSpec((1, 1, tr, sc), lambda q, i, ix: (2 * q + o, 0, i, 0))
    out_blk = pl.BlockSpec((1, 1, tr, sc), lambda q, i, ix: (q, 0, i, 0))
    return pl.pallas_call(
        body, name=name,
        out_shape=(jax.ShapeDtypeStruct((2, 1, sr, sc), F32), jax.ShapeDtypeStruct((2, 1, sr, sc), BF16)),
        grid_spec=pltpu.PrefetchScalarGridSpec(
            num_scalar_prefetch=1, grid=(2, nb), in_specs=[sub(0), got_blk(0), sub(2), got_blk(1)],
            out_specs=(out_blk, out_blk)),
        compiler_params=_cp(("parallel", "parallel")),
    )(idx, half, got, half, got)


def _shard_sum(pack, core, keep, got):
    name, kind, g, r, c, _, _ = pack
    _, sr, sc = _sub_shape(kind, r, c)
    tr = _sub_tile(sr)
    nb = sr // tr

    def body(core_ref, a_ref, b_ref, o_ref):
        o_ref[0] = a_ref[0, 0] + b_ref[0, 0].astype(F32)

    blk = pl.BlockSpec((1, 1, tr, sc), lambda p, i, cr: (p, 0, i, 0))
    if kind == "row":
        o_map = lambda p, i, cr: (0, p * nb + i, cr[0])
    else:
        o_map = lambda p, i, cr: (0, cr[0] * 2 * nb + p * nb + i, 0)
    return pl.pallas_call(
        body, name="shard_sum_" + name, out_shape=jax.ShapeDtypeStruct((g, r, c), F32),
        grid_spec=pltpu.PrefetchScalarGridSpec(
            num_scalar_prefetch=1, grid=(2, nb), in_specs=[blk, blk], out_specs=pl.BlockSpec((1, tr, sc), o_map)),
        compiler_params=_cp(("parallel", "parallel")),
    )(core, keep, got)


class _Plan:
    def __init__(self, shards, table):
        self.shards, self.table = shards, table
        self.whole, self.grad, self.got_a, self.half, self.gshard = {}, {}, {}, {}, {}
        self.got_b1, self.kept, self.pass_on, self.got_b2 = {}, {}, {}, {}
        x, y, c = _pos()
        me, jx, jy = 2 * x + y, 2 * (1 - x) + y, 2 * x + (1 - y)
        self.core = c
        self.core1 = c.reshape(1).astype(jnp.int32)
        zero = 0 * me
        self.idx_sums = jnp.stack([me, zero, jy, zero, me, zero + 1, jx, zero + 1]).astype(jnp.int32)
        self._w_in = None
        self.send, self.keep = {}, {}

    def w(self, n):
        if n != "w_in":
            return self.whole[n][0]
        if self._w_in is None:
            self._w_in = _slabs_to_kernel_cols(self.whole[n], name="relayout_w_in", jobs=self.jobs("relayout_w_in"))
        return self._w_in

    def g(self, n, a):
        self.grad[n] = a[None]

    def g_half(self, n, which, a):
        if which == "keep" and n in self.got_a:
            self.half[n] = _kernel_cols_to_slabs(a, name="relayout_d_in_keep", add=self.got_a[n])
        else:
            (self.send if which == "send" else self.keep)[n] = _kernel_cols_to_slabs(a, name="relayout_d_in_" + which)

    def jobs(self, tag):
        out = []
        for spec in self.table.get(tag, ()):
            out += getattr(self, "_" + spec[0])(*spec[1:])
        return out

    def run(self, name, jobs):
        if jobs:
            _call(lambda: None, jobs=jobs, name=name, out_shape=[], in_specs=[], out_specs=[])()

    def _gather(self, names):
        return [_GatherJob(names, self.shards, self.whole)]

    def _rs_a(self, names):
        mats = [MATS[n] for n in names]

        def build(srcs, dsts, news, send, recv):
            x, y, c = _pos()
            return [_remote(srcs[i] if names[i] in self.send else _half_of_whole(kind, srcs[i], 1 - c, r, cc), news[i],
                            send.at[i], recv.at[i], (x, y, 1 - c))
                    for i, (_, kind, g, r, cc, _, _) in enumerate(mats)]

        def done(dsts, news):
            self.got_a.update(zip(names, news))

        return [_SwapJob(build, len(names), srcs=[self.send.get(n, self.grad.get(n)) for n in names], done=done,
                         news=[jax.ShapeDtypeStruct(_half_shape(kind, g, r, c), BF16) for _, kind, g, r, c, _, _ in mats])]

    def _rs_b1(self, names):
        mats = [MATS[n] for n in names]
        for n in names:
            if n in self.half:
                continue
            if n in self.keep:
                self.half[n] = _pair_sum(MATS[n], self.core1, self.keep[n], self.got_a[n], whole=False)
            else:
                self.half[n] = _pair_sum(MATS[n], self.core1, self.grad[n], self.got_a[n])

        def build(srcs, dsts, news, send, recv):
            x, y, c = _pos()
            jx, jy, jd = 2 * (1 - x) + y, 2 * x + (1 - y), 2 * (1 - x) + (1 - y)
            nbx, nby = (1 - x, y, c), (x, 1 - y, c)
            cps = []
            for i, (_, kind, g, r, cc, _, _) in enumerate(mats):
                sub = lambda j, p: _sub_of_half(kind, srcs[i], j, p, r, cc)
                for k, (j, p, dev) in enumerate(((jx, 0, nbx), (jd, 0, nbx), (jy, 1, nby), (jd, 1, nby))):
                    cps.append(_remote(sub(j, p), news[i].at[k], send.at[4 * i + k], recv.at[4 * i + k], dev))
            return cps

        def done(dsts, news):
            self.got_b1.update(zip(names, news))

        return [_SwapJob(build, 4 * len(names), srcs=[self.half[n] for n in names], done=done,
                         news=[jax.ShapeDtypeStruct((4,) + _sub_shape(kind, r, c), BF16) for _, kind, g, r, c, _, _ in mats])]

    def _rs_b2(self, names):
        mats = [MATS[n] for n in names]
        for n in names:
            self.kept[n], self.pass_on[n] = _sub_sums(MATS[n], self.idx_sums, self.half[n], self.got_b1[n], name="sums_" + n)

        def build(srcs, dsts, news, send, recv):
            x, y, c = _pos()
            cps = []
            for i in range(len(mats)):
                cps.append(_remote(srcs[i].at[0], news[i].at[0], send.at[2 * i], recv.at[2 * i], (x, 1 - y, c)))
                cps.append(_remote(srcs[i].at[1], news[i].at[1], send.at[2 * i + 1], recv.at[2 * i + 1], (1 - x, y, c)))
            return cps

        def done(dsts, news):
            self.got_b2.update(zip(names, news))

        return [_SwapJob(build, 2 * len(names), srcs=[self.pass_on[n] for n in names], done=done,
                         news=[jax.ShapeDtypeStruct((2,) + _sub_shape(kind, r, c), BF16) for _, kind, g, r, c, _, _ in mats])]

    def _rs_c(self, names):
        mats = [MATS[n] for n in names]
        parts = [_shard_sum(MATS[n], self.core1, self.kept[n], self.got_b2[n]) for n in names]

        def build(srcs, dsts, news, send, recv):
            x, y, c = _pos()
            cps = []
            for i, (_, kind, g, r, cc, _, _) in enumerate(mats):
                mine = _half_of_shard(kind, dsts[i], c, r, cc)
                cps.append(_remote(mine, mine, send.at[i], recv.at[i], (x, y, 1 - c)))
            return cps

        def done(dsts, news):
            self.gshard.update(zip(names, dsts))

        return [_SwapJob(build, len(names), dsts=parts, done=done)]

    def finish(self, n):
        if n not in self.got_a:
            self.run("rs_a_" + n, self._rs_a((n,)))
        if n not in self.got_b1:
            self.run("rs_b1_" + n, self._rs_b1((n,)))
        if n not in self.got_b2:
            self.run("rs_b2_" + n, self._rs_b2((n,)))
        if n not in self.gshard:
            self.run("rs_c_" + n, self._rs_c((n,)))
        return self.gshard[n]


TABLE = {
    "gather_w_in": (("gather", ("w_in",)),),
    "relayout_w_in": (("gather", ("w_gate",)),),
    "mm_in": (("gather", ("w_up",)),),
    "attn_fwd": (("gather", ("w_attn_br", "w_ssd_br")),),
    "ssd_fwd": (("gather", ("w_o",)),),
    "swiglu_fwd": (("gather", ("w_down",)),),
    "mm_down": (("gather", ("w_ple_gate", "w_ple_proj")),),
    "mm_d_down": (("rs_a", ("w_ple_proj", "w_ple_gate")),),
    "swiglu_bwd": (("rs_a", ("w_down",)), ("rs_b1", ("w_ple_proj", "w_ple_gate"))),
    "mm_d_gate": (("rs_b1", ("w_down",)), ("rs_b2", ("w_ple_proj", "w_ple_gate"))),
    "mm_d_up": (("rs_b2", ("w_down",)), ("rs_a", ("w_gate",))),
    "mm_df_gate": (("rs_b1", ("w_gate",)), ("rs_a", ("w_up",)), ("rs_c", ("w_down", "w_ple_proj", "w_ple_gate"))),
    "mm_df_up": (("rs_b2", ("w_gate",)),),
    "attn_bwd": (("rs_b1", ("w_up",)), ("rs_a", ("w_o", "w_attn_br", "w_ssd_br")), ("rs_c", ("w_gate",))),
    "ssd_bwd": (("rs_b1", ("w_o", "w_attn_br", "w_ssd_br")), ("rs_b2", ("w_up",))),
    "conv_bwd": (("rs_b2", ("w_o", "w_attn_br", "w_ssd_br")), ("rs_c", ("w_up",))),
    "mm_d_in_keep": (("rs_a", ("w_in",)), ("rs_c", ("w_o", "w_attn_br", "w_ssd_br"))),
    "mm_du": (("rs_b1", ("w_in",)),),
    "norm_mix_bwd": (("rs_b2", ("w_in",)),),
    "sum_small": (("rs_c", ("w_in",)),),
}


NDEV = 8


def _allreduce_small(v, *, name, jobs=()):
    rows = v.shape[0]

    def body(v_ref, o_ref, slots, send, recv):
        x, y, c = _pos()
        me = 4 * x + 2 * y + c
        slots[me] = v_ref[...]
        cps = []
        for k in range(1, NDEV):
            peer = (_flip(x, k & 4), _flip(y, k & 2), _flip(c, k & 1))
            cp = _remote(v_ref, slots.at[me], send.at[k - 1], recv.at[k - 1], peer)
            cp.start()
            cps.append(cp)
        for cp in cps:
            cp.wait()
        acc = slots[0]
        for s in range(1, NDEV):
            acc = acc + slots[s]
        o_ref[...] = acc

    return _call(
        body, jobs=jobs, name=name, out_shape=jax.ShapeDtypeStruct((rows, 128), F32),
        in_specs=[pl.BlockSpec(memory_space=pltpu.VMEM)], out_specs=pl.BlockSpec(memory_space=pltpu.VMEM),
        scratch_shapes=[pltpu.VMEM((NDEV, rows, 128), F32), pltpu.SemaphoreType.DMA((NDEV - 1,)),
                        pltpu.SemaphoreType.DMA((NDEV - 1,))],
    )(v)


def _adamw(w, g, m, v, *, name, tr=None, tc=None, jobs=()):
    r, c = w.shape
    tr = r if tr is None else tr
    c1 = 1.0 / (1.0 - B1 ** STEP)
    c2 = 1.0 / (1.0 - B2 ** STEP)

    def body(w_ref, g_ref, m_ref, v_ref, d_ref, mo_ref, vo_ref):
        gv = g_ref[...]
        mn = B1 * m_ref[...] + (1.0 - B1) * gv
        vn = B2 * v_ref[...] + (1.0 - B2) * (gv * gv)
        mo_ref[...] = mn
        vo_ref[...] = vn
        d_ref[...] = -LR * ((mn * c1) / (jnp.sqrt(vn * c2) + AEPS) + WD * w_ref[...])

    if tc is None:
        blk, grid = pl.BlockSpec((tr, c), lambda i: (i, 0)), (r // tr,)
    else:
        blk, grid = pl.BlockSpec((r, tc), lambda i: (0, i)), (c // tc,)
    o = jax.ShapeDtypeStruct((r, c), F32)
    return _call(
        body, jobs=jobs, name=name, out_shape=(o, o, o), grid=grid, in_specs=[blk] * 4, out_specs=(blk, blk, blk),
        compiler_params=_cp(("parallel",)),
    )(w, g, m, v)


WEIGHTS = ("g_mix", "w_in", "conv_w", "conv_b", "dt_bias", "a_log", "d_skip", "g_ssd", "sinks", "w_attn_br", "w_ssd_br",
           "w_o", "g_ffn", "w_gate", "w_up", "w_down", "g_ple", "w_ple_gate", "w_ple_proj", "g_final")
BIG = {
    "w_gate": 256, "w_up": 256, "w_down": 128, "w_ssd_br": 128, "w_o": 128, "w_ple_gate": 128, "w_attn_br": 256,
    "w_ple_proj": 256, "w_in": None,
}
SMALL = tuple(n for n in WEIGHTS if n not in BIG)


def _pack_small(parts):
    rows = []
    for a in parts:
        a = a.reshape(-1)
        rows.append(jnp.pad(a, (0, -a.shape[0] % 128)).reshape(-1, 128))
    out = jnp.concatenate(rows, axis=0)
    return jnp.pad(out, ((0, -out.shape[0] % 8), (0, 0)))


def _unpack_small(packed, shapes):
    out, r = [], 0
    for s in shapes:
        n = int(np.prod(s))
        nr = -(-n // 128)
        out.append(packed[r:r + nr].reshape(-1)[:n].reshape(s))
        r += nr
    return out


def kernel(x, p, positions, g_mix, w_in, conv_w, conv_b, dt_bias, a_log, d_skip, g_ssd, sinks, w_attn_br, w_ssd_br, w_o, g_ffn, w_gate, w_up, w_down, g_ple, w_ple_gate, w_ple_proj, g_final, loss_target, m_g_mix, m_w_in, m_conv_w, m_conv_b, m_dt_bias, m_a_log, m_d_skip, m_g_ssd, m_sinks, m_w_attn_br, m_w_ssd_br, m_w_o, m_g_ffn, m_w_gate, m_w_up, m_w_down, m_g_ple, m_w_ple_gate, m_w_ple_proj, m_g_final, v_g_mix, v_w_in, v_conv_w, v_conv_b, v_dt_bias, v_a_log, v_d_skip, v_g_ssd, v_sinks, v_w_attn_br, v_w_ssd_br, v_w_o, v_g_ffn, v_w_gate, v_w_up, v_w_down, v_g_ple, v_w_ple_gate, v_w_ple_proj, v_g_final):
    w = dict(zip(WEIGHTS, (g_mix, w_in, conv_w, conv_b, dt_bias, a_log, d_skip, g_ssd, sinks, w_attn_br, w_ssd_br, w_o,
                           g_ffn, w_gate, w_up, w_down, g_ple, w_ple_gate, w_ple_proj, g_final)))
    m = dict(zip(WEIGHTS, (m_g_mix, m_w_in, m_conv_w, m_conv_b, m_dt_bias, m_a_log, m_d_skip, m_g_ssd, m_sinks, m_w_attn_br,
                           m_w_ssd_br, m_w_o, m_g_ffn, m_w_gate, m_w_up, m_w_down, m_g_ple, m_w_ple_gate, m_w_ple_proj,
                           m_g_final)))
    v = dict(zip(WEIGHTS, (v_g_mix, v_w_in, v_conv_w, v_conv_b, v_dt_bias, v_a_log, v_d_skip, v_g_ssd, v_sinks, v_w_attn_br,
                           v_w_ssd_br, v_w_o, v_g_ffn, v_w_gate, v_w_up, v_w_down, v_g_ple, v_w_ple_gate, v_w_ple_proj,
                           v_g_final)))
    xi, yi, ci = _pos()
    chip = 2 * xi + yi
    t = x.shape[1]
    cshard = CONV // NCHIP

    shards = {n: w[n].astype(BF16) for n in MATS}
    shards["w_in"] = jnp.pad(shards["w_in"], ((0, 0), (0, 0), (0, SLAB_PAD - SLAB)))
    plan = _Plan(shards, TABLE)
    plan.run("gather_w_in", plan.jobs("gather_w_in"))
    placed = lax.dynamic_update_slice(jnp.zeros((CW, CONV), F32), w["conv_w"][0], (0, chip * cshard))
    conv_whole = _allreduce_small(jnp.where(ci == 0, placed, 0.0).reshape(-1, 128), name="gather_conv_w").reshape(CW, CONV)

    small = {n: w[n] for n in ("g_mix", "conv_b", "dt_bias", "a_log", "d_skip", "g_ssd", "sinks", "g_ffn", "g_ple", "g_final")}
    small["conv_w"] = conv_whole
    loss8, grad_x, gs = _local_step(x[0], p[0, 0], positions, loss_target[0], small, plan)

    order = ("g_mix", "conv_b", "dt_bias", "a_log", "d_skip", "g_ssd", "sinks", "g_ffn", "g_ple", "g_final", "conv_w")
    summed = _allreduce_small(_pack_small([loss8[0, :1]] + [gs[n] for n in order]), name="sum_small",
                              jobs=plan.jobs("sum_small"))
    parts = _unpack_small(summed, [(1,)] + [w[n].shape for n in order[:-1]] + [(CW, CONV)])
    loss = parts[0][0]
    grad = dict(zip(order, parts[1:]))
    grad["conv_w"] = lax.dynamic_slice(grad["conv_w"], (0, chip * cshard), (CW, cshard))[None]

    delta, new_m, new_v = {}, {}, {}
    for n, tr in BIG.items():
        grad[n] = plan.finish(n)[:, :, :w[n].shape[2]]
        if n == "w_in":
            d_, m_, v_ = _adamw(w[n][0].T, grad[n][0].T, m[n][0].T, v[n][0].T, tc=128, name="adamw_" + n)
            d_, m_, v_ = d_.T, m_.T, v_.T
        else:
            d_, m_, v_ = _adamw(w[n][0], grad[n][0], m[n][0], v[n][0], tr=tr, name="adamw_" + n)
        delta[n], new_m[n], new_v[n] = d_[None], m_[None], v_[None]
    shapes = [w[n].shape for n in SMALL]
    d_, m_, v_ = _adamw(_pack_small([w[n] for n in SMALL]), _pack_small([grad[n] for n in SMALL]),
                        _pack_small([m[n] for n in SMALL]), _pack_small([v[n] for n in SMALL]), tr=None, name="adamw_small")
    for n, a, b, c_ in zip(SMALL, _unpack_small(d_, shapes), _unpack_small(m_, shapes), _unpack_small(v_, shapes)):
        delta[n], new_m[n], new_v[n] = a, b, c_

    return (loss, grad_x[None], *[grad[n] for n in WEIGHTS], *[delta[n] for n in WEIGHTS],
            *[new_m[n] for n in WEIGHTS], *[new_v[n] for n in WEIGHTS])
```
